```python
import jax, jax.numpy as jnp
from jax import lax
import numpy as np

D_MODEL = 1024
BATCH = 8
SEQ = 2048
DEPTH = 4

BRANCH_WIDTH = D_MODEL // 2
N_BRANCH = 3
HGRN_KEY = 128
HGRN_VAL = 128
HGRN_HEADS = BRANCH_WIDTH // HGRN_KEY
HGRN_CHUNK = 64
LB_FLOOR = 1e-30
CONV_K = 3
SG_CHUNK = 128
SG_GROUPS = 4
SG_GROUP_DIM = BRANCH_WIDTH // SG_GROUPS
D_FF = 4 * D_MODEL
NORM_EPS = 1e-6
LN_EPS = 1e-5
IN_COLS = 9 * BRANCH_WIDTH + N_BRANCH * D_MODEL

kernel_name = 'hybrid_hgrn2_shortconv_spatialgate_block'


def rms_norm(x, g, eps=NORM_EPS):
    xf = x.astype(jnp.float32)
    y = xf * lax.rsqrt(jnp.mean(xf * xf, axis=-1, keepdims=True) + eps)
    return (y * g.astype(jnp.float32)).astype(x.dtype)


def layer_norm(x, g, b, eps=LN_EPS):
    xf = x.astype(jnp.float32)
    mu = jnp.mean(xf, axis=-1, keepdims=True)
    xc = xf - mu
    y = xc * lax.rsqrt(jnp.mean(xc * xc, axis=-1, keepdims=True) + eps)
    return (y * g.astype(jnp.float32) + b.astype(jnp.float32)).astype(x.dtype)


def hgrn2_mix(q, fp, iv, go, lb, g_out):
    B, S, _ = q.shape
    H, K, V, L = HGRN_HEADS, HGRN_KEY, HGRN_VAL, HGRN_CHUNK
    N = S // L
    f32 = jnp.float32
    q = jax.nn.silu(q.astype(f32))
    fp = fp.astype(f32)
    lb = lb.astype(f32)
    logf = jnp.logaddexp(jnp.log(jnp.maximum(lb, LB_FLOOR)),
                         jnp.log1p(-lb) + jax.nn.log_sigmoid(fp))
    k = (1.0 - lb) * jax.nn.sigmoid(-fp)
    v = iv.astype(f32)

    def to_chunks(t, d):
        return t.reshape(B, N, L, H, d).transpose(1, 0, 3, 2, 4)

    qs, ks, vs, ls = to_chunks(q, K), to_chunks(k, K), to_chunks(v, V), to_chunks(logf, K)
    causal = jnp.tril(jnp.ones((L, L), dtype=bool))[:, :, None]

    def step(state, inp):
        qc, kc, vc, lc = inp
        b = jnp.cumsum(lc, axis=2)
        diff = b[:, :, :, None, :] - b[:, :, None, :, :]
        decay = jnp.where(causal, jnp.exp(jnp.where(causal, diff, 0.0)), 0.0)
        attn = jnp.einsum('bhtk,bhsk,bhtsk->bhts', qc, kc, decay)
        o = (jnp.einsum('bhts,bhsv->bhtv', attn, vc)
             + jnp.einsum('bhtk,bhkv->bhtv', qc * jnp.exp(b), state))
        b_last = b[:, :, -1:, :]
        new_state = (jnp.exp(b_last[:, :, 0, :])[..., None] * state
                     + jnp.einsum('bhsk,bhsv->bhkv', kc * jnp.exp(b_last - b), vc))
        return new_state, o

    state0 = jnp.zeros((B, H, K, V), f32)
    _, o = lax.scan(step, state0, (qs, ks, vs, ls))
    o = o.transpose(1, 0, 3, 2, 4).reshape(B, S, H, V)
    o = rms_norm(o, g_out.reshape(H, V)) * jax.nn.sigmoid(go.astype(f32)).reshape(B, S, H, V)
    return o.reshape(B, S, H * V).astype(iv.dtype)


def short_conv_mix(bg, cg, xc, w_conv):
    z = cg * xc
    ch = z.shape[-1]
    y = lax.conv_general_dilated(z, w_conv[:, None, :].astype(z.dtype), window_strides=(1,),
                                 padding=[(CONV_K - 1, 0)],
                                 dimension_numbers=('NWC', 'WIO', 'NWC'),
                                 feature_group_count=ch)
    return bg * y


def spatial_gating_mix(u, v, ln_g, ln_b, w_s, b_s):
    B, S, _ = u.shape
    N = S // SG_CHUNK
    u = jax.nn.gelu(u)
    v = layer_norm(jax.nn.gelu(v), ln_g, ln_b)
    vg = v.reshape(B, N, SG_CHUNK, SG_GROUPS, SG_GROUP_DIM)
    mask = jnp.tril(jnp.ones((SG_CHUNK, SG_CHUNK), dtype=w_s.dtype))
    sv = jnp.einsum('gts,bnsgd->bntgd', w_s * mask, vg) + b_s.T[:, :, None]
    return u * sv.reshape(B, S, BRANCH_WIDTH)


def _fwd_setup_inputs(seed: int = 0) -> dict:
    key = jax.random.key(seed)
    ks = jax.random.split(key, 17)
    W, D = BRANCH_WIDTH, D_MODEL
    nrm = jax.random.normal
    f32 = jnp.float32
    return {
        'x': nrm(ks[0], (BATCH, SEQ, D), f32),
        'w_in': nrm(ks[1], (DEPTH, D, IN_COLS), f32) * D ** -0.5,
        'g_mix': 1.0 + 0.01 * nrm(ks[2], (DEPTH, D), f32),
        'lower_bounds': 0.1 * nrm(ks[3], (DEPTH, W), f32),
        'g_hgrn_out': 1.0 + 0.01 * nrm(ks[4], (DEPTH, W), f32),
        'w_conv': nrm(ks[5], (DEPTH, CONV_K, W), f32) * CONV_K ** -0.5,
        'sg_ln_g': 1.0 + 0.01 * nrm(ks[6], (DEPTH, W), f32),
        'sg_ln_b': 0.01 * nrm(ks[7], (DEPTH, W), f32),
        'w_sg': nrm(ks[8], (DEPTH, SG_GROUPS, SG_CHUNK, SG_CHUNK), f32) * SG_CHUNK ** -0.5,
        'b_sg': 1.0 + 0.01 * nrm(ks[9], (DEPTH, SG_GROUPS, SG_CHUNK), f32),
        'w_branch': nrm(ks[10], (DEPTH, N_BRANCH, W, D), f32) * W ** -0.5,
        'w_o': nrm(ks[11], (DEPTH, D, D), f32) * D ** -0.5,
        'g_ffn': 1.0 + 0.01 * nrm(ks[12], (DEPTH, D), f32),
        'w_ff1': nrm(ks[13], (DEPTH, D, D_FF), f32) * D ** -0.5,
        'w_ff2': nrm(ks[14], (DEPTH, D_FF, D), f32) * D_FF ** -0.5,
        'g_final': 1.0 + 0.01 * nrm(ks[15], (D,), f32),
    }


def _fwd_reference(x, w_in, g_mix, lower_bounds, g_hgrn_out, w_conv, sg_ln_g, sg_ln_b,
              w_sg, b_sg, w_branch, w_o, g_ffn, w_ff1, w_ff2, g_final):
    B, S, D = x.shape
    W = BRANCH_WIDTH
    lbs = jax.nn.softmax(lower_bounds.astype(jnp.float32), axis=0)
    lbs = jnp.cumsum(lbs, axis=0) - lbs[0]
    offsets = [W * i for i in range(1, 10)]
    for l in range(DEPTH):
        h = rms_norm(x, g_mix[l])
        p = h @ w_in[l]
        q, fp, iv, go, bg, cg, xc, u, v, gates = jnp.split(p, offsets, axis=-1)
        o_a = hgrn2_mix(q, fp, iv, go, lbs[l], g_hgrn_out[l])
        o_b = short_conv_mix(bg, cg, xc, w_conv[l])
        o_c = spatial_gating_mix(u, v, sg_ln_g[l], sg_ln_b[l], w_sg[l], b_sg[l])
        z = jnp.stack([o_a, o_b, o_c], axis=2)
        y = jnp.einsum('bsnw,nwd->bsnd', z, w_branch[l])
        gate = jax.nn.sigmoid(gates).reshape(B, S, N_BRANCH, D)
        merged = jnp.sum(gate * y, axis=2)
        x = x + merged @ w_o[l]
        h2 = rms_norm(x, g_ffn[l])
        x = x + jnp.square(jax.nn.relu(h2 @ w_ff1[l])) @ w_ff2[l]
    return rms_norm(x, g_final)


import jax as _jax
import jax.numpy as _jnp

TWIN_FORMAT = 'train_step'
FWD_PARAMS = ['x', 'w_in', 'g_mix', 'lower_bounds', 'g_hgrn_out', 'w_conv', 'sg_ln_g', 'sg_ln_b', 'w_sg', 'b_sg', 'w_branch', 'w_o', 'g_ffn', 'w_ff1', 'w_ff2', 'g_final']
TWIN_WEIGHTS = ['w_in', 'g_mix', 'lower_bounds', 'g_hgrn_out', 'w_conv', 'sg_ln_g', 'sg_ln_b', 'w_sg', 'b_sg', 'w_branch', 'w_o', 'g_ffn', 'w_ff1', 'w_ff2', 'g_final']
TWIN_DIFF_INPUT = 'x'
TWIN_INPUTS = ['x', 'w_in', 'g_mix', 'lower_bounds', 'g_hgrn_out', 'w_conv', 'sg_ln_g', 'sg_ln_b', 'w_sg', 'b_sg', 'w_branch', 'w_o', 'g_ffn', 'w_ff1', 'w_ff2', 'g_final', 'loss_target', 'm_w_in', 'm_g_mix', 'm_lower_bounds', 'm_g_hgrn_out', 'm_w_conv', 'm_sg_ln_g', 'm_sg_ln_b', 'm_w_sg', 'm_b_sg', 'm_w_branch', 'm_w_o', 'm_g_ffn', 'm_w_ff1', 'm_w_ff2', 'm_g_final', 'v_w_in', 'v_g_mix', 'v_lower_bounds', 'v_g_hgrn_out', 'v_w_conv', 'v_sg_ln_g', 'v_sg_ln_b', 'v_w_sg', 'v_b_sg', 'v_w_branch', 'v_w_o', 'v_g_ffn', 'v_w_ff1', 'v_w_ff2', 'v_g_final']
TWIN_OUTPUTS = ['loss', 'grad_x', 'grad_w_in', 'grad_g_mix', 'grad_lower_bounds', 'grad_g_hgrn_out', 'grad_w_conv', 'grad_sg_ln_g', 'grad_sg_ln_b', 'grad_w_sg', 'grad_b_sg', 'grad_w_branch', 'grad_w_o', 'grad_g_ffn', 'grad_w_ff1', 'grad_w_ff2', 'grad_g_final', 'delta_w_in', 'delta_g_mix', 'delta_lower_bounds', 'delta_g_hgrn_out', 'delta_w_conv', 'delta_sg_ln_g', 'delta_sg_ln_b', 'delta_w_sg', 'delta_b_sg', 'delta_w_branch', 'delta_w_o', 'delta_g_ffn', 'delta_w_ff1', 'delta_w_ff2', 'delta_g_final', 'new_m_w_in', 'new_m_g_mix', 'new_m_lower_bounds', 'new_m_g_hgrn_out', 'new_m_w_conv', 'new_m_sg_ln_g', 'new_m_sg_ln_b', 'new_m_w_sg', 'new_m_b_sg', 'new_m_w_branch', 'new_m_w_o', 'new_m_g_ffn', 'new_m_w_ff1', 'new_m_w_ff2', 'new_m_g_final', 'new_v_w_in', 'new_v_g_mix', 'new_v_lower_bounds', 'new_v_g_hgrn_out', 'new_v_w_conv', 'new_v_sg_ln_g', 'new_v_sg_ln_b', 'new_v_w_sg', 'new_v_b_sg', 'new_v_w_branch', 'new_v_w_o', 'new_v_g_ffn', 'new_v_w_ff1', 'new_v_w_ff2', 'new_v_g_final']
TWIN_LEAF_KINDS = {'loss': 'loss', 'grad_x': 'grad_x', 'grad_w_in': 'grad_w', 'grad_g_mix': 'grad_w', 'grad_lower_bounds': 'grad_w', 'grad_g_hgrn_out': 'grad_w', 'grad_w_conv': 'grad_w', 'grad_sg_ln_g': 'grad_w', 'grad_sg_ln_b': 'grad_w', 'grad_w_sg': 'grad_w', 'grad_b_sg': 'grad_w', 'grad_w_branch': 'grad_w', 'grad_w_o': 'grad_w', 'grad_g_ffn': 'grad_w', 'grad_w_ff1': 'grad_w', 'grad_w_ff2': 'grad_w', 'grad_g_final': 'grad_w', 'delta_w_in': 'delta_w', 'delta_g_mix': 'delta_w', 'delta_lower_bounds': 'delta_w', 'delta_g_hgrn_out': 'delta_w', 'delta_w_conv': 'delta_w', 'delta_sg_ln_g': 'delta_w', 'delta_sg_ln_b': 'delta_w', 'delta_w_sg': 'delta_w', 'delta_b_sg': 'delta_w', 'delta_w_branch': 'delta_w', 'delta_w_o': 'delta_w', 'delta_g_ffn': 'delta_w', 'delta_w_ff1': 'delta_w', 'delta_w_ff2': 'delta_w', 'delta_g_final': 'delta_w', 'new_m_w_in': 'new_m', 'new_m_g_mix': 'new_m', 'new_m_lower_bounds': 'new_m', 'new_m_g_hgrn_out': 'new_m', 'new_m_w_conv': 'new_m', 'new_m_sg_ln_g': 'new_m', 'new_m_sg_ln_b': 'new_m', 'new_m_w_sg': 'new_m', 'new_m_b_sg': 'new_m', 'new_m_w_branch': 'new_m', 'new_m_w_o': 'new_m', 'new_m_g_ffn': 'new_m', 'new_m_w_ff1': 'new_m', 'new_m_w_ff2': 'new_m', 'new_m_g_final': 'new_m', 'new_v_w_in': 'new_v', 'new_v_g_mix': 'new_v', 'new_v_lower_bounds': 'new_v', 'new_v_g_hgrn_out': 'new_v', 'new_v_w_conv': 'new_v', 'new_v_sg_ln_g': 'new_v', 'new_v_sg_ln_b': 'new_v', 'new_v_w_sg': 'new_v', 'new_v_b_sg': 'new_v', 'new_v_w_branch': 'new_v', 'new_v_w_o': 'new_v', 'new_v_g_ffn': 'new_v', 'new_v_w_ff1': 'new_v', 'new_v_w_ff2': 'new_v', 'new_v_g_final': 'new_v'}


def _forward(args):
    return _fwd_reference(*[args[k] for k in FWD_PARAMS])


def _output_shape():
    out = _jax.eval_shape(lambda: _forward(_fwd_setup_inputs(0)))
    return out.shape, out.dtype

N_MICROBATCH = 1
ADAM_LR = 0.001
ADAM_B1 = 0.9
ADAM_B2 = 0.999
ADAM_EPS = 1e-08
ADAM_WD = 0.01
ADAM_STEP = 10
PER_EXAMPLE_BATCH_AXIS = {'x': 0, 'loss_target': 0}
SHARED_INPUTS = []
_WEIGHT_DTYPES = {'w_in': _jnp.float32, 'g_mix': _jnp.float32, 'lower_bounds': _jnp.float32, 'g_hgrn_out': _jnp.float32, 'w_conv': _jnp.float32, 'sg_ln_g': _jnp.float32, 'sg_ln_b': _jnp.float32, 'w_sg': _jnp.float32, 'b_sg': _jnp.float32, 'w_branch': _jnp.float32, 'w_o': _jnp.float32, 'g_ffn': _jnp.float32, 'w_ff1': _jnp.float32, 'w_ff2': _jnp.float32, 'g_final': _jnp.float32}
MOMENT_SCALE = {'w_in': 3.938029e-02, 'g_mix': 1.144622e-01, 'lower_bounds': 1.602272e-03, 'g_hgrn_out': 3.957779e-02, 'w_conv': 7.234644e-02, 'sg_ln_g': 3.230790e-02, 'sg_ln_b': 3.163141e-02, 'w_sg': 3.213050e-02, 'b_sg': 4.658265e-02, 'w_branch': 4.043258e-02, 'w_o': 6.987436e-02, 'g_ffn': 9.654873e-02, 'w_ff1': 4.758445e-02, 'w_ff2': 8.399676e-02, 'g_final': 1.633337e+01}


def _to_microbatches(a, axis):
    t = _jnp.moveaxis(a, axis, 0)
    t = t.reshape((N_MICROBATCH, t.shape[0] // N_MICROBATCH) + t.shape[1:])
    return _jnp.moveaxis(t, 1, axis + 1)


def setup_inputs(seed: int = 0) -> dict:
    inp = _fwd_setup_inputs(seed)
    key = _jax.random.fold_in(_jax.random.key(seed), 7919)
    shape, _ = _output_shape()
    out = dict(inp)
    out["loss_target"] = _jax.random.normal(_jax.random.fold_in(key, 0), shape, _jnp.float32)
    for i, name in enumerate(TWIN_WEIGHTS):
        w = inp[name].astype(_jnp.float32)
        if MOMENT_SCALE is None:
            s = _jnp.sqrt(_jnp.mean(_jnp.square(w)) + 1e-30)
        else:
            s = MOMENT_SCALE[name]
        km, kv = _jax.random.split(_jax.random.fold_in(key, i + 1))
        out[name] = w
        out["m_" + name] = s * _jax.random.normal(km, w.shape, _jnp.float32)
        out["v_" + name] = (s * s) * _jax.random.uniform(kv, w.shape, _jnp.float32, 0.5, 1.5)
    if N_MICROBATCH > 1:
        for name, axis in PER_EXAMPLE_BATCH_AXIS.items():
            out[name] = _to_microbatches(out[name], axis)
    return {'x': out['x'], 'w_in': out['w_in'], 'g_mix': out['g_mix'], 'lower_bounds': out['lower_bounds'], 'g_hgrn_out': out['g_hgrn_out'], 'w_conv': out['w_conv'], 'sg_ln_g': out['sg_ln_g'], 'sg_ln_b': out['sg_ln_b'], 'w_sg': out['w_sg'], 'b_sg': out['b_sg'], 'w_branch': out['w_branch'], 'w_o': out['w_o'], 'g_ffn': out['g_ffn'], 'w_ff1': out['w_ff1'], 'w_ff2': out['w_ff2'], 'g_final': out['g_final'], 'loss_target': out['loss_target'], 'm_w_in': out['m_w_in'], 'm_g_mix': out['m_g_mix'], 'm_lower_bounds': out['m_lower_bounds'], 'm_g_hgrn_out': out['m_g_hgrn_out'], 'm_w_conv': out['m_w_conv'], 'm_sg_ln_g': out['m_sg_ln_g'], 'm_sg_ln_b': out['m_sg_ln_b'], 'm_w_sg': out['m_w_sg'], 'm_b_sg': out['m_b_sg'], 'm_w_branch': out['m_w_branch'], 'm_w_o': out['m_w_o'], 'm_g_ffn': out['m_g_ffn'], 'm_w_ff1': out['m_w_ff1'], 'm_w_ff2': out['m_w_ff2'], 'm_g_final': out['m_g_final'], 'v_w_in': out['v_w_in'], 'v_g_mix': out['v_g_mix'], 'v_lower_bounds': out['v_lower_bounds'], 'v_g_hgrn_out': out['v_g_hgrn_out'], 'v_w_conv': out['v_w_conv'], 'v_sg_ln_g': out['v_sg_ln_g'], 'v_sg_ln_b': out['v_sg_ln_b'], 'v_w_sg': out['v_w_sg'], 'v_b_sg': out['v_b_sg'], 'v_w_branch': out['v_w_branch'], 'v_w_o': out['v_w_o'], 'v_g_ffn': out['v_g_ffn'], 'v_w_ff1': out['v_w_ff1'], 'v_w_ff2': out['v_w_ff2'], 'v_g_final': out['v_g_final']}


def _loss(weights, diff, rest, loss_target):
    with _jax.named_scope("forward"):
        args = {**rest, TWIN_DIFF_INPUT: diff, **{k: w.astype(_WEIGHT_DTYPES[k]) for k, w in weights.items()}}
        y = _forward(args)
    with _jax.named_scope("loss_head"):
        err = _jnp.square(y.astype(_jnp.float32) - loss_target)
        return 0.5 * _jnp.sum(_jnp.mean(err, axis=-1)) if err.ndim else 0.5 * err


def _adamw(w, g, m, v):
    m = ADAM_B1 * m + (1.0 - ADAM_B1) * g
    v = ADAM_B2 * v + (1.0 - ADAM_B2) * _jnp.square(g)
    m_hat = m / (1.0 - ADAM_B1 ** ADAM_STEP)
    v_hat = v / (1.0 - ADAM_B2 ** ADAM_STEP)
    delta = -ADAM_LR * (m_hat / (_jnp.sqrt(v_hat) + ADAM_EPS) + ADAM_WD * w)
    return delta, m, v


def reference(x, w_in, g_mix, lower_bounds, g_hgrn_out, w_conv, sg_ln_g, sg_ln_b, w_sg, b_sg, w_branch, w_o, g_ffn, w_ff1, w_ff2, g_final, loss_target, m_w_in, m_g_mix, m_lower_bounds, m_g_hgrn_out, m_w_conv, m_sg_ln_g, m_sg_ln_b, m_w_sg, m_b_sg, m_w_branch, m_w_o, m_g_ffn, m_w_ff1, m_w_ff2, m_g_final, v_w_in, v_g_mix, v_lower_bounds, v_g_hgrn_out, v_w_conv, v_sg_ln_g, v_sg_ln_b, v_w_sg, v_b_sg, v_w_branch, v_w_o, v_g_ffn, v_w_ff1, v_w_ff2, v_g_final):
    given = dict(x=x, w_in=w_in, g_mix=g_mix, lower_bounds=lower_bounds, g_hgrn_out=g_hgrn_out, w_conv=w_conv, sg_ln_g=sg_ln_g, sg_ln_b=sg_ln_b, w_sg=w_sg, b_sg=b_sg, w_branch=w_branch, w_o=w_o, g_ffn=g_ffn, w_ff1=w_ff1, w_ff2=w_ff2, g_final=g_final, loss_target=loss_target, m_w_in=m_w_in, m_g_mix=m_g_mix, m_lower_bounds=m_lower_bounds, m_g_hgrn_out=m_g_hgrn_out, m_w_conv=m_w_conv, m_sg_ln_g=m_sg_ln_g, m_sg_ln_b=m_sg_ln_b, m_w_sg=m_w_sg, m_b_sg=m_b_sg, m_w_branch=m_w_branch, m_w_o=m_w_o, m_g_ffn=m_g_ffn, m_w_ff1=m_w_ff1, m_w_ff2=m_w_ff2, m_g_final=m_g_final, v_w_in=v_w_in, v_g_mix=v_g_mix, v_lower_bounds=v_lower_bounds, v_g_hgrn_out=v_g_hgrn_out, v_w_conv=v_w_conv, v_sg_ln_g=v_sg_ln_g, v_sg_ln_b=v_sg_ln_b, v_w_sg=v_w_sg, v_b_sg=v_b_sg, v_w_branch=v_w_branch, v_w_o=v_w_o, v_g_ffn=v_g_ffn, v_w_ff1=v_w_ff1, v_w_ff2=v_w_ff2, v_g_final=v_g_final)
    weights = {n: given[n] for n in TWIN_WEIGHTS}
    shared = {n: given[n] for n in SHARED_INPUTS}
    per_example = {n: given[n] for n in ['x']}
    grad_fn = _jax.value_and_grad(_loss, argnums=(0, 1))

    def one_microbatch(ex, loss_target):
        ex = dict(ex)
        diff = ex.pop(TWIN_DIFF_INPUT)
        return grad_fn(weights, diff, {**shared, **ex}, loss_target)

    if N_MICROBATCH == 1:
        loss, (grad_w, grad_x) = one_microbatch(per_example, given["loss_target"])
    else:
        def body(carry, xs):
            loss_sum, grad_sum = carry
            l_k, (gw_k, gx_k) = one_microbatch(xs[0], xs[1])
            with _jax.named_scope("update"):
                return (loss_sum + l_k, _jax.tree.map(_jnp.add, grad_sum, gw_k)), gx_k

        init = (_jnp.zeros((), _jnp.float32), _jax.tree.map(_jnp.zeros_like, weights))
        (loss, grad_w), grad_x = _jax.lax.scan(body, init, (per_example, given["loss_target"]))
    with _jax.named_scope("update"):
        delta_w, new_m, new_v = {}, {}, {}
        for n in TWIN_WEIGHTS:
            delta_w[n], new_m[n], new_v[n] = _adamw(weights[n], grad_w[n], given["m_" + n], given["v_" + n])
    return (loss, grad_x, *[grad_w[n] for n in TWIN_WEIGHTS], *[delta_w[n] for n in TWIN_WEIGHTS],
            *[new_m[n] for n in TWIN_WEIGHTS], *[new_v[n] for n in TWIN_WEIGHTS])
```

```python
import functools

import jax
import jax.numpy as jnp
from jax import lax
from jax.experimental import pallas as pl
from jax.experimental.pallas import tpu as pltpu

f32 = jnp.float32
bf16 = jnp.bfloat16
SDS = jax.ShapeDtypeStruct
MESH = pl.DeviceIdType.MESH

D_MODEL = 1024
BRANCH = 512
N_COLS = 7680
D_FF = 4096
DEPTH = 4
HEADS = 4
HEAD_DIM = 128
HGRN_CHUNK = 64
SG_CHUNK = 128
SG_GROUPS = 4
NORM_EPS = 1e-6
LN_EPS = 1e-5
LB_FLOOR = 1e-30
N_DEV = 8
SHARD_IN = N_COLS // N_DEV
WIN = 1024
LANE = 128
GATE_COL0 = 9 * BRANCH

ADAM_LR = 0.001
ADAM_B1 = 0.9
ADAM_B2 = 0.999
ADAM_EPS = 1e-08
ADAM_WD = 0.01
ADAM_STEP = 10

MIX_TILE = 256
VMEM_LIMIT = 56 * 1024 * 1024


def _cp(*sem):
    return pltpu.CompilerParams(dimension_semantics=sem or None, vmem_limit_bytes=VMEM_LIMIT)


def _dot(a, b):
    return jnp.dot(a, b, preferred_element_type=f32)


def _dot_nt(a, b):
    return lax.dot_general(a, b, (((1,), (1,)), ((), ())), preferred_element_type=f32)


def _dot_tn(a, b):
    return lax.dot_general(a, b, (((0,), (0,)), ((), ())), preferred_element_type=f32)


def _dot_exact(a, b):
    return jnp.dot(a, b, precision=lax.Precision.HIGHEST, preferred_element_type=f32)


def _sigmoid(x):
    return jax.nn.sigmoid(x)


_GELU_C = 0.7978845608028654
_GELU_A = 0.044715


def _gelu(x):
    return 0.5 * x * (1.0 + jnp.tanh(_GELU_C * (x + _GELU_A * x * x * x)))


def _gelu_grad(x):
    x2 = x * x
    t = jnp.tanh(_GELU_C * (x + _GELU_A * x * x2))
    return 0.5 * (1.0 + t) + 0.5 * x * (1.0 - t * t) * _GELU_C * (1.0 + 3.0 * _GELU_A * x2)


def _rms_stats(x):
    r = lax.rsqrt(jnp.mean(x * x, axis=-1, keepdims=True) + NORM_EPS)
    return r, x * r


def _rms_bwd(dh, xh, r, g):
    dg = jnp.sum(dh * xh, axis=0, keepdims=True)
    dxn = dh * g
    dx = r * (dxn - xh * jnp.mean(dxn * xh, axis=-1, keepdims=True))
    return dx, dg


def _tri(n, upper=False):
    r = lax.broadcasted_iota(jnp.int32, (n, n), 0)
    c = lax.broadcasted_iota(jnp.int32, (n, n), 1)
    return (c >= r) if upper else (c <= r)


def _acc_rows(ref, first, val):
    @pl.when(first)
    def _():
        ref[...] = val

    @pl.when(jnp.logical_not(first))
    def _():
        ref[...] += val


def _rms_mm(x, g, w, tm=512, tn=1920):
    s, n = x.shape[0], w.shape[1]

    def body(x_ref, g_ref, w_ref, p_ref, h_ref, hs):
        @pl.when(pl.program_id(1) == 0)
        def _():
            _, xh = _rms_stats(x_ref[...])
            hv = (xh * g_ref[...]).astype(bf16)
            hs[...] = hv
            h_ref[...] = hv

        p_ref[...] = _dot(hs[...], w_ref[...])

    return pl.pallas_call(
        body, name="rms_mm", grid=(s // tm, n // tn),
        in_specs=[pl.BlockSpec((tm, D_MODEL), lambda i, j: (i, 0)), pl.BlockSpec((1, D_MODEL), lambda i, j: (0, 0)),
                  pl.BlockSpec((D_MODEL, tn), lambda i, j: (0, j))],
        out_specs=[pl.BlockSpec((tm, tn), lambda i, j: (i, j)), pl.BlockSpec((tm, D_MODEL), lambda i, j: (i, 0))],
        out_shape=[SDS((s, n), f32), SDS((s, D_MODEL), bf16)],
        scratch_shapes=[pltpu.VMEM((tm, D_MODEL), bf16)], compiler_params=_cp("parallel", "arbitrary"),
    )(x, g, w)


def _hgrn_gates(fp, lb):
    logf = jnp.logaddexp(jnp.log(jnp.maximum(lb, LB_FLOOR)), jnp.log1p(-lb) + jax.nn.log_sigmoid(fp))
    snf = _sigmoid(-fp)
    return logf, snf, (1.0 - lb) * snf


def _p_specs(tile, cols, row_map):
    return [pl.BlockSpec((tile, BRANCH), functools.partial(lambda c, i: (row_map(i), c), c)) for c in cols]


def _mixer_fwd(p, lb, gout, wconv, lng, lnb, wsg, bsg_t):
    s = p.shape[0]
    tt = MIX_TILE
    nch = tt // HGRN_CHUNK

    def body(q_ref, fp_ref, iv_ref, go_ref, bg_ref, cg_ref, xc_ref, u_ref, v_ref, lb_ref, gout_ref, wconv_ref, lng_ref,
             lnb_ref, wsg_ref, bsg_ref, z_ref, opre_ref, st_ref, st_scr, zbuf):
        @pl.when(pl.program_id(0) == 0)
        def _():
            st_scr[...] = jnp.zeros_like(st_scr)
            zbuf[0:8, :] = jnp.zeros((8, BRANCH), f32)

        lbv = lb_ref[...]
        q_raw = q_ref[...]
        qs = q_raw * _sigmoid(q_raw)
        logf, _, kk = _hgrn_gates(fp_ref[...], lbv)
        iv = iv_ref[...]
        causal = _tri(HGRN_CHUNK)
        tri = causal.astype(f32)
        last_row = lax.broadcasted_iota(jnp.int32, (HGRN_CHUNK, 1), 0) == HGRN_CHUNK - 1
        for c in range(nch):
            rows = slice(HGRN_CHUNK * c, HGRN_CHUNK * (c + 1))
            b = _dot_exact(tri, logf[rows])
            bl = jnp.sum(jnp.where(last_row, b, 0.0), axis=0, keepdims=True)
            qb = (qs[rows] * jnp.exp(b)).astype(bf16)
            kb = (kk[rows] * jnp.exp(-b)).astype(bf16)
            kd = (kk[rows] * jnp.exp(bl - b)).astype(bf16)
            ebl = jnp.exp(bl)
            vc = iv[rows].astype(bf16)
            for h in range(HEADS):
                sl = slice(HEAD_DIM * h, HEAD_DIM * (h + 1))
                st = st_scr[h]
                st_ref[c, h] = st
                a = jnp.where(causal, _dot_nt(qb[:, sl], kb[:, sl]), 0.0)
                opre_ref[rows, sl] = _dot(a.astype(bf16), vc[:, sl]) + _dot_nt(qb[:, sl], st.astype(bf16))
                st_scr[h] = st * ebl[:, sl] + _dot_tn(vc[:, sl], kd[:, sl])

        o = opre_ref[...]
        go = go_ref[...]
        gout_v = gout_ref[...]
        for h in range(HEADS):
            sl = slice(HEAD_DIM * h, HEAD_DIM * (h + 1))
            _, oh = _rms_stats(o[:, sl])
            z_ref[:, sl] = (oh * gout_v[:, sl] * _sigmoid(go[:, sl])).astype(bf16)

        zc = cg_ref[...] * xc_ref[...]
        zbuf[8:8 + tt, :] = zc
        y = wconv_ref[0:1, :] * zbuf[pl.ds(6, tt), :] + wconv_ref[1:2, :] * zbuf[pl.ds(7, tt), :] + wconv_ref[2:3, :] * zc
        z_ref[:, BRANCH:2 * BRANCH] = (bg_ref[...] * y).astype(bf16)
        zbuf[0:8, :] = zbuf[tt:tt + 8, :]

        ug = _gelu(u_ref[...])
        vg = _gelu(v_ref[...])
        vcen = vg - jnp.mean(vg, axis=-1, keepdims=True)
        rstd = lax.rsqrt(jnp.mean(vcen * vcen, axis=-1, keepdims=True) + LN_EPS)
        vn = (vcen * rstd * lng_ref[...] + lnb_ref[...]).astype(bf16)
        low = _tri(SG_CHUNK)
        for g in range(SG_GROUPS):
            sl = slice(LANE * g, LANE * (g + 1))
            wm = jnp.where(low, wsg_ref[g], 0.0).astype(bf16)
            bias = bsg_ref[:, g:g + 1]
            for cc in range(tt // SG_CHUNK):
                rows = slice(SG_CHUNK * cc, SG_CHUNK * (cc + 1))
                sv = _dot(wm, vn[rows, sl]) + bias
                z_ref[rows, 2 * BRANCH + LANE * g:2 * BRANCH + LANE * (g + 1)] = (ug[rows, sl] * sv).astype(bf16)

    full = lambda shape: pl.BlockSpec(shape, lambda i: (0,) * len(shape))
    return pl.pallas_call(
        body, name="mixer_fwd", grid=(s // tt,),
        in_specs=_p_specs(tt, range(9), lambda i: i) + [full((1, BRANCH)), full((1, BRANCH)), full((3, BRANCH)), full((1, BRANCH)),
                                                        full((1, BRANCH)), full((SG_GROUPS, SG_CHUNK, SG_CHUNK)), full((SG_CHUNK, SG_GROUPS))],
        out_specs=[pl.BlockSpec((tt, 3 * BRANCH), lambda i: (i, 0)), pl.BlockSpec((tt, BRANCH), lambda i: (i, 0)),
                   pl.BlockSpec((nch, HEADS, HEAD_DIM, HEAD_DIM), lambda i: (i, 0, 0, 0))],
        out_shape=[SDS((s, 3 * BRANCH), bf16), SDS((s, BRANCH), f32), SDS((s // HGRN_CHUNK, HEADS, HEAD_DIM, HEAD_DIM), f32)],
        scratch_shapes=[pltpu.VMEM((HEADS, HEAD_DIM, HEAD_DIM), f32), pltpu.VMEM((tt + 8, BRANCH), f32)],
        compiler_params=_cp("arbitrary"),
    )(*([p] * 9), lb, gout, wconv, lng, lnb, wsg, bsg_t)


def _branch_gate(z, wb, p, tm=256):
    s = z.shape[0]
    half = 3 * D_MODEL // 2

    def body(z_ref, wb_ref, ga_ref, gb_ref, y_ref, m_ref):
        ga, gb = ga_ref[...], gb_ref[...]
        gates = [ga[:, :D_MODEL], jnp.concatenate([ga[:, D_MODEL:], gb[:, :D_MODEL // 2]], axis=1), gb[:, D_MODEL // 2:]]
        acc = None
        for n in range(3):
            yn = _dot(z_ref[:, BRANCH * n:BRANCH * (n + 1)], wb_ref[n])
            y_ref[:, D_MODEL * n:D_MODEL * (n + 1)] = yn.astype(bf16)
            t = _sigmoid(gates[n]) * yn
            acc = t if acc is None else acc + t
        m_ref[...] = acc.astype(bf16)

    blk0 = GATE_COL0 // half
    return pl.pallas_call(
        body, name="branch_gate", grid=(s // tm,),
        in_specs=[pl.BlockSpec((tm, 3 * BRANCH), lambda i: (i, 0)), pl.BlockSpec((3, BRANCH, D_MODEL), lambda i: (0, 0, 0)),
                  pl.BlockSpec((tm, half), lambda i: (i, blk0)), pl.BlockSpec((tm, half), lambda i: (i, blk0 + 1))],
        out_specs=[pl.BlockSpec((tm, 3 * D_MODEL), lambda i: (i, 0)), pl.BlockSpec((tm, D_MODEL), lambda i: (i, 0))],
        out_shape=[SDS((s, 3 * D_MODEL), bf16), SDS((s, D_MODEL), bf16)], compiler_params=_cp("parallel"),
    )(z, wb, p, p)


def _mm_resid(x, m, wo, tm=512):
    s = x.shape[0]

    def body(x_ref, m_ref, w_ref, o_ref):
        o_ref[...] = x_ref[...] + _dot(m_ref[...], w_ref[...])

    return pl.pallas_call(
        body, name="mm_resid", grid=(s // tm,),
        in_specs=[pl.BlockSpec((tm, D_MODEL), lambda i: (i, 0)), pl.BlockSpec((tm, D_MODEL), lambda i: (i, 0)),
                  pl.BlockSpec((D_MODEL, D_MODEL), lambda i: (0, 0))],
        out_specs=pl.BlockSpec((tm, D_MODEL), lambda i: (i, 0)), out_shape=SDS((s, D_MODEL), f32), compiler_params=_cp("parallel"),
    )(x, m, wo)


def _ffn(x1, g, w1, w2, tm=512, tf=1024):
    s = x1.shape[0]
    nf = D_FF // tf

    def body(x_ref, g_ref, w1_ref, w2_ref, o_ref, h_ref, ra_ref, hs, acc):
        f = pl.program_id(1)

        @pl.when(f == 0)
        def _():
            _, xh = _rms_stats(x_ref[...])
            hv = (xh * g_ref[...]).astype(bf16)
            hs[...] = hv
            h_ref[...] = hv
            acc[...] = jnp.zeros_like(acc)

        ra = jnp.maximum(_dot(hs[...], w1_ref[...]), 0.0)
        ra_ref[...] = ra.astype(bf16)
        acc[...] += _dot((ra * ra).astype(bf16), w2_ref[...])

        @pl.when(f == nf - 1)
        def _():
            o_ref[...] = x_ref[...] + acc[...]

    return pl.pallas_call(
        body, name="ffn", grid=(s // tm, nf),
        in_specs=[pl.BlockSpec((tm, D_MODEL), lambda i, f: (i, 0)), pl.BlockSpec((1, D_MODEL), lambda i, f: (0, 0)),
                  pl.BlockSpec((D_MODEL, tf), lambda i, f: (0, f)), pl.BlockSpec((tf, D_MODEL), lambda i, f: (f, 0))],
        out_specs=[pl.BlockSpec((tm, D_MODEL), lambda i, f: (i, 0)), pl.BlockSpec((tm, D_MODEL), lambda i, f: (i, 0)),
                   pl.BlockSpec((tm, tf), lambda i, f: (i, f))],
        out_shape=[SDS((s, D_MODEL), f32), SDS((s, D_MODEL), bf16), SDS((s, D_FF), bf16)],
        scratch_shapes=[pltpu.VMEM((tm, D_MODEL), bf16), pltpu.VMEM((tm, D_MODEL), f32)], compiler_params=_cp("parallel", "arbitrary"),
    )(x1, g, w1, w2)


def _final(x, target, g, tm=512):
    s = x.shape[0]

    def body(x_ref, t_ref, g_ref, loss_ref, dx_ref, dxb_ref, dg_ref):
        first = pl.program_id(0) == 0
        gv = g_ref[...]
        r, xh = _rms_stats(x_ref[...])
        e = xh * gv - t_ref[...]
        tile_loss = 0.5 * jnp.sum(jnp.mean(e * e, axis=-1, keepdims=True), axis=0, keepdims=True)
        dx, dg = _rms_bwd(e * (1.0 / D_MODEL), xh, r, gv)
        dx_ref[...] = dx
        dxb_ref[...] = dx.astype(bf16)
        _acc_rows(dg_ref, first, dg)
        _acc_rows(loss_ref, first, jnp.broadcast_to(tile_loss, (1, LANE)))

    row = pl.BlockSpec((tm, D_MODEL), lambda i: (i, 0))
    return pl.pallas_call(
        body, name="final_loss", grid=(s // tm,), in_specs=[row, row, pl.BlockSpec((1, D_MODEL), lambda i: (0, 0))],
        out_specs=[pl.BlockSpec((1, LANE), lambda i: (0, 0)), row, row, pl.BlockSpec((1, D_MODEL), lambda i: (0, 0))],
        out_shape=[SDS((1, LANE), f32), SDS((s, D_MODEL), f32), SDS((s, D_MODEL), bf16), SDS((1, D_MODEL), f32)],
        compiler_params=_cp("arbitrary"),
    )(x, target, g)


def _ffn_bwd(dx2, dx2b, x1, g, ra, w1, w2, tm=512, tf=1024):
    s = x1.shape[0]
    nf = D_FF // tf

    def body(dx_ref, dxb_ref, x_ref, g_ref, ra_ref, w1_ref, w2_ref, da_ref, act_ref, dx1_ref, dx1b_ref, dg_ref, acc):
        i, f = pl.program_id(0), pl.program_id(1)

        @pl.when(f == 0)
        def _():
            acc[...] = jnp.zeros_like(acc)

        rav = ra_ref[...].astype(f32)
        da = (_dot_nt(dxb_ref[...], w2_ref[...]) * (2.0 * rav)).astype(bf16)
        da_ref[...] = da
        act_ref[...] = (rav * rav).astype(bf16)
        acc[...] += _dot_nt(da, w1_ref[...])

        @pl.when(f == nf - 1)
        def _():
            r, xh = _rms_stats(x_ref[...])
            dx, dg = _rms_bwd(acc[...], xh, r, g_ref[...])
            dx = dx + dx_ref[...]
            dx1_ref[...] = dx
            dx1b_ref[...] = dx.astype(bf16)
            _acc_rows(dg_ref, i == 0, dg)

    row = pl.BlockSpec((tm, D_MODEL), lambda i, f: (i, 0))
    col = pl.BlockSpec((tm, tf), lambda i, f: (i, f))
    return pl.pallas_call(
        body, name="ffn_bwd", grid=(s // tm, nf),
        in_specs=[row, row, row, pl.BlockSpec((1, D_MODEL), lambda i, f: (0, 0)), col,
                  pl.BlockSpec((D_MODEL, tf), lambda i, f: (0, f)), pl.BlockSpec((tf, D_MODEL), lambda i, f: (f, 0))],
        out_specs=[col, col, row, row, pl.BlockSpec((1, D_MODEL), lambda i, f: (0, 0))],
        out_shape=[SDS((s, D_FF), bf16), SDS((s, D_FF), bf16), SDS((s, D_MODEL), f32), SDS((s, D_MODEL), bf16), SDS((1, D_MODEL), f32)],
        scratch_shapes=[pltpu.VMEM((tm, D_MODEL), f32)], compiler_params=_cp("arbitrary", "arbitrary"),
    )(dx2, dx2b, x1, g, ra, w1, w2)


def _mm_tn(a, b, nb, m, n, tm, tn, ts=512, name="mm_tn"):
    s = a.shape[0]
    mi, nj, ns = m // tm, n // tn, s // ts

    def body(a_ref, b_ref, o_ref, acc):
        t = pl.program_id(3)

        @pl.when(t == 0)
        def _():
            acc[...] = jnp.zeros_like(acc)

        acc[...] += _dot_tn(a_ref[...], b_ref[...])

        @pl.when(t == ns - 1)
        def _():
            o_ref[...] = acc[...].astype(bf16)

    return pl.pallas_call(
        body, name=name, grid=(nb, mi, nj, ns),
        in_specs=[pl.BlockSpec((ts, tm), lambda k, i, j, t: (t, k * mi + i)), pl.BlockSpec((ts, tn), lambda k, i, j, t: (t, k * nj + j))],
        out_specs=pl.BlockSpec((None, tm, tn), lambda k, i, j, t: (k, i, j)), out_shape=SDS((nb, m, n), bf16),
        scratch_shapes=[pltpu.VMEM((tm, tn), f32)], compiler_params=_cp("parallel", "parallel", "parallel", "arbitrary"),
    )(a, b)


def _merge_bwd(dx1b, wo, y, p, wb, tm=256):
    s = dx1b.shape[0]
    hw = D_MODEL // 2
    gblk = GATE_COL0 // hw

    def body(dx_ref, wo_ref, y_ref, gt_ref, wb_ref, dy_ref, dp_ref, dz_ref, dm_scr, dz_acc):
        k = pl.program_id(1)
        hf = k % 2

        @pl.when(k == 0)
        def _():
            dm = _dot_nt(dx_ref[...], wo_ref[...])
            dm_scr[0] = dm[:, :hw]
            dm_scr[1] = dm[:, hw:]

        dmh = dm_scr[hf]
        gate = _sigmoid(gt_ref[...])
        dy = (dmh * gate).astype(bf16)
        dy_ref[...] = dy
        dp_ref[...] = (dmh * y_ref[...].astype(f32) * gate * (1.0 - gate)).astype(bf16)
        part = _dot_nt(dy, wb_ref[...])

        @pl.when(hf == 0)
        def _():
            dz_acc[...] = part

        @pl.when(hf == 1)
        def _():
            dz_ref[...] = dz_acc[...] + part

    return pl.pallas_call(
        body, name="merge_bwd", grid=(s // tm, 6),
        in_specs=[pl.BlockSpec((tm, D_MODEL), lambda i, k: (i, 0)), pl.BlockSpec((D_MODEL, D_MODEL), lambda i, k: (0, 0)),
                  pl.BlockSpec((tm, hw), lambda i, k: (i, k)), pl.BlockSpec((tm, hw), lambda i, k: (i, gblk + k)),
                  pl.BlockSpec((None, BRANCH, hw), lambda i, k: (k // 2, 0, k % 2))],
        out_specs=[pl.BlockSpec((tm, hw), lambda i, k: (i, k)), pl.BlockSpec((tm, hw), lambda i, k: (i, gblk + k)),
                   pl.BlockSpec((tm, BRANCH), lambda i, k: (i, k // 2))],
        out_shape=[SDS((s, 3 * D_MODEL), bf16), SDS((s, N_COLS), bf16), SDS((s, 3 * BRANCH), f32)],
        scratch_shapes=[pltpu.VMEM((2, tm, hw), f32), pltpu.VMEM((tm, BRANCH), f32)], compiler_params=_cp("parallel", "arbitrary"),
    )(dx1b, wo, y, p, wb)


def _mixer_bwd(p, dz, opre, states, dp, lb, gout, wconv, lng, lnb, wsg, bsg_t):
    s = p.shape[0]
    tt = MIX_TILE
    nt = s // tt
    nch = tt // HGRN_CHUNK
    rev = lambda i: nt - 1 - i

    def body(q_ref, fp_ref, iv_ref, go_ref, bg_ref, cg_ref, xc_ref, u_ref, v_ref, cgp_ref, xcp_ref, dz_ref, opre_ref, st_ref,
             dp_in, lb_ref, gout_ref, wconv_ref, lng_ref, lnb_ref, wsg_ref, bsg_ref,
             dp_ref, vec_ref, dwsg_ref, dbsg_ref, dst_scr, zbuf, dybuf, dvn_scr, dbsg_acc):
        del dp_in
        i = pl.program_id(0)

        @pl.when(i == 0)
        def _():
            dst_scr[...] = jnp.zeros_like(dst_scr)
            dybuf[tt:tt + 8, :] = jnp.zeros((8, BRANCH), f32)
            vec_ref[...] = jnp.zeros_like(vec_ref)
            dwsg_ref[...] = jnp.zeros_like(dwsg_ref)
            dbsg_acc[...] = jnp.zeros_like(dbsg_acc)

        lbv = lb_ref[...]
        q_raw, fp = q_ref[...], fp_ref[...]
        sq = _sigmoid(q_raw)
        qs = q_raw * sq
        sfp = _sigmoid(fp)
        logf, snf, kk = _hgrn_gates(fp, lbv)
        inv_f = jnp.exp(-logf)
        iv = iv_ref[...]
        doa = dz_ref[:, 0:BRANCH]
        o = opre_ref[...]
        sgo = _sigmoid(go_ref[...])
        gout_v = gout_ref[...]
        d_o, dgo, dgout = [], [], []
        for h in range(HEADS):
            sl = slice(HEAD_DIM * h, HEAD_DIM * (h + 1))
            r, oh = _rms_stats(o[:, sl])
            d_on = doa[:, sl] * sgo[:, sl]
            dgo.append(doa[:, sl] * oh * gout_v[:, sl] * sgo[:, sl] * (1.0 - sgo[:, sl]))
            dx, dg = _rms_bwd(d_on, oh, r, gout_v[:, sl])
            d_o.append(dx)
            dgout.append(dg)
        d_o = jnp.concatenate(d_o, axis=1)
        dp_ref[:, 3 * BRANCH:4 * BRANCH] = jnp.concatenate(dgo, axis=1).astype(bf16)
        vec_ref[1:2, :] += jnp.concatenate(dgout, axis=1)

        causal = _tri(HGRN_CHUNK)
        tri = causal.astype(f32)
        tri_up = _tri(HGRN_CHUNK, upper=True).astype(f32)
        last_row = lax.broadcasted_iota(jnp.int32, (HGRN_CHUNK, 1), 0) == HGRN_CHUNK - 1
        lb_live = (lbv > LB_FLOOR).astype(f32)
        dlb = jnp.zeros((1, BRANCH), f32)
        for c in reversed(range(nch)):
            rows = slice(HGRN_CHUNK * c, HGRN_CHUNK * (c + 1))
            b = _dot_exact(tri, logf[rows])
            bl = jnp.sum(jnp.where(last_row, b, 0.0), axis=0, keepdims=True)
            eb, enb, edl, ebl = jnp.exp(b), jnp.exp(-b), jnp.exp(bl - b), jnp.exp(bl)
            qbf, kbf, kdf = qs[rows] * eb, kk[rows] * enb, kk[rows] * edl
            qb, kb, kd = qbf.astype(bf16), kbf.astype(bf16), kdf.astype(bf16)
            vc = iv[rows].astype(bf16)
            dob = d_o[rows].astype(bf16)
            dv, dqb, dkb, dkd, debl = [], [], [], [], []
            for h in range(HEADS):
                sl = slice(HEAD_DIM * h, HEAD_DIM * (h + 1))
                st = st_ref[c, h]
                dst = dst_scr[h]
                stb, dstb = st.astype(bf16), dst.astype(bf16)
                a = jnp.where(causal, _dot_nt(qb[:, sl], kb[:, sl]), 0.0).astype(bf16)
                da = jnp.where(causal, _dot_nt(dob[:, sl], vc[:, sl]), 0.0).astype(bf16)
                dv.append(_dot_tn(a, dob[:, sl]) + _dot_nt(kd[:, sl], dstb))
                dqb.append(_dot(dob[:, sl], stb) + _dot(da, kb[:, sl]))
                dkb.append(_dot_tn(da, qb[:, sl]))
                dkd.append(_dot(vc[:, sl], dstb))
                debl.append(jnp.sum(st * dst, axis=0, keepdims=True))
                dst_scr[h] = _dot_tn(dob[:, sl], qb[:, sl]) + dst * ebl[:, sl]
            dv, dqb, dkb, dkd = (jnp.concatenate(t, axis=1) for t in (dv, dqb, dkb, dkd))
            debl = jnp.concatenate(debl, axis=1)
            t_kd = dkd * kdf
            dbl = ebl * debl + jnp.sum(t_kd, axis=0, keepdims=True)
            db = dqb * qbf - dkb * kbf - t_kd + jnp.where(last_row, dbl, 0.0)
            dkk = dkb * enb + dkd * edl
            dlc = _dot_exact(tri_up, db)
            sq_c, q_c, sfp_c, snf_c, invf_c = sq[rows], q_raw[rows], sfp[rows], snf[rows], inv_f[rows]
            slope = (1.0 - lbv) * sfp_c * snf_c
            dp_ref[rows, 0:BRANCH] = (dqb * eb * sq_c * (1.0 + q_c * (1.0 - sq_c))).astype(bf16)
            dp_ref[rows, BRANCH:2 * BRANCH] = (slope * (dlc * invf_c - dkk)).astype(bf16)
            dp_ref[rows, 2 * BRANCH:3 * BRANCH] = dv.astype(bf16)
            dlb = dlb + jnp.sum(dlc * (lb_live - sfp_c) * invf_c - dkk * snf_c, axis=0, keepdims=True)
        vec_ref[0:1, :] += dlb

        dob_ = dz_ref[:, BRANCH:2 * BRANCH]
        bg, cg, xc = bg_ref[...], cg_ref[...], xc_ref[...]
        zc = cg * xc
        zbuf[0:8, :] = jnp.where(i < nt - 1, cgp_ref[...] * xcp_ref[...], 0.0)
        zbuf[8:8 + tt, :] = zc
        w0, w1, w2 = wconv_ref[0:1, :], wconv_ref[1:2, :], wconv_ref[2:3, :]
        y = w0 * zbuf[pl.ds(6, tt), :] + w1 * zbuf[pl.ds(7, tt), :] + w2 * zc
        dy = dob_ * bg
        dybuf[0:tt, :] = dy
        dy1, dy2 = dybuf[pl.ds(1, tt), :], dybuf[pl.ds(2, tt), :]
        dzc = w2 * dy + w1 * dy1 + w0 * dy2
        dp_ref[:, 4 * BRANCH:5 * BRANCH] = (dob_ * y).astype(bf16)
        dp_ref[:, 5 * BRANCH:6 * BRANCH] = (dzc * xc).astype(bf16)
        dp_ref[:, 6 * BRANCH:7 * BRANCH] = (dzc * cg).astype(bf16)
        vec_ref[4:5, :] += jnp.sum(zc * dy2, axis=0, keepdims=True)
        vec_ref[5:6, :] += jnp.sum(zc * dy1, axis=0, keepdims=True)
        vec_ref[6:7, :] += jnp.sum(zc * dy, axis=0, keepdims=True)
        dybuf[tt:tt + 8, :] = dybuf[0:8, :]

        doc = dz_ref[:, 2 * BRANCH:3 * BRANCH]
        u_raw, v_raw = u_ref[...], v_ref[...]
        ug = _gelu(u_raw)
        dug_scale = _gelu_grad(u_raw)
        vg = _gelu(v_raw)
        vcen = vg - jnp.mean(vg, axis=-1, keepdims=True)
        rstd = lax.rsqrt(jnp.mean(vcen * vcen, axis=-1, keepdims=True) + LN_EPS)
        vhat = vcen * rstd
        lng_v = lng_ref[...]
        vn = (vhat * lng_v + lnb_ref[...]).astype(bf16)
        low = _tri(SG_CHUNK)
        for g in range(SG_GROUPS):
            sl = slice(LANE * g, LANE * (g + 1))
            wm = jnp.where(low, wsg_ref[g], 0.0).astype(bf16)
            bias = bsg_ref[:, g:g + 1]
            dw = jnp.zeros((SG_CHUNK, SG_CHUNK), f32)
            dbs = jnp.zeros((SG_CHUNK, LANE), f32)
            for cc in range(tt // SG_CHUNK):
                rows = slice(SG_CHUNK * cc, SG_CHUNK * (cc + 1))
                vn_c = vn[rows, sl]
                sv = _dot(wm, vn_c) + bias
                doc_c = doc[rows, sl]
                dp_ref[rows, 7 * BRANCH + LANE * g:7 * BRANCH + LANE * (g + 1)] = (doc_c * sv * dug_scale[rows, sl]).astype(bf16)
                dsv = doc_c * ug[rows, sl]
                dsvb = dsv.astype(bf16)
                dbs = dbs + dsv
                dw = dw + _dot_nt(dsvb, vn_c)
                dvn_scr[rows, sl] = _dot_tn(wm, dsvb)
            dwsg_ref[g] += jnp.where(low, dw, 0.0)
            dbsg_acc[:, sl] += dbs
        dvn = dvn_scr[...]
        vec_ref[2:3, :] += jnp.sum(dvn * vhat, axis=0, keepdims=True)
        vec_ref[3:4, :] += jnp.sum(dvn, axis=0, keepdims=True)
        dvh = dvn * lng_v
        dvg = rstd * (dvh - jnp.mean(dvh, axis=-1, keepdims=True) - vhat * jnp.mean(dvh * vhat, axis=-1, keepdims=True))
        dp_ref[:, 8 * BRANCH:9 * BRANCH] = (dvg * _gelu_grad(v_raw)).astype(bf16)

        @pl.when(i == nt - 1)
        def _():
            for g in range(SG_GROUPS):
                dbsg_ref[:, g:g + 1] = jnp.sum(dbsg_acc[:, LANE * g:LANE * (g + 1)], axis=1, keepdims=True)

    full = lambda shape: pl.BlockSpec(shape, lambda i: (0,) * len(shape))
    tail = lambda c: pl.BlockSpec((8, BRANCH), lambda i: (jnp.maximum(rev(i) * (tt // 8) - 1, 0), c))
    return pl.pallas_call(
        body, name="mixer_bwd", grid=(nt,),
        in_specs=_p_specs(tt, range(9), rev) + [tail(5), tail(6), pl.BlockSpec((tt, 3 * BRANCH), lambda i: (rev(i), 0)),
                                                pl.BlockSpec((tt, BRANCH), lambda i: (rev(i), 0)),
                                                pl.BlockSpec((nch, HEADS, HEAD_DIM, HEAD_DIM), lambda i: (rev(i), 0, 0, 0)),
                                                pl.BlockSpec(memory_space=pl.ANY),
                                                full((1, BRANCH)), full((1, BRANCH)), full((3, BRANCH)), full((1, BRANCH)), full((1, BRANCH)),
                                                full((SG_GROUPS, SG_CHUNK, SG_CHUNK)), full((SG_CHUNK, SG_GROUPS))],
        out_specs=[pl.BlockSpec((tt, 9 * BRANCH), lambda i: (rev(i), 0)), full((8, BRANCH)), full((SG_GROUPS, SG_CHUNK, SG_CHUNK)),
                   full((SG_CHUNK, SG_GROUPS))],
        out_shape=[SDS((s, N_COLS), bf16), SDS((8, BRANCH), f32), SDS((SG_GROUPS, SG_CHUNK, SG_CHUNK), f32), SDS((SG_CHUNK, SG_GROUPS), f32)],
        scratch_shapes=[pltpu.VMEM((HEADS, HEAD_DIM, HEAD_DIM), f32), pltpu.VMEM((tt + 8, BRANCH), f32), pltpu.VMEM((tt + 8, BRANCH), f32),
                        pltpu.VMEM((tt, BRANCH), f32), pltpu.VMEM((SG_CHUNK, BRANCH), f32)],
        input_output_aliases={14: 0}, compiler_params=_cp("arbitrary"),
    )(*([p] * 11), dz, opre, states, dp, lb, gout, wconv, lng, lnb, wsg, bsg_t)


def _dh_bwd(dp, w_in, x, dx1, g, tm=512, tk=1536):
    s = x.shape[0]
    nk = N_COLS // tk

    def body(dp_ref, w_ref, x_ref, dx1_ref, g_ref, dx_ref, dxb_ref, dg_ref, acc):
        i, k = pl.program_id(0), pl.program_id(1)

        @pl.when(k == 0)
        def _():
            acc[...] = jnp.zeros_like(acc)

        acc[...] += _dot_nt(dp_ref[...], w_ref[...])

        @pl.when(k == nk - 1)
        def _():
            r, xh = _rms_stats(x_ref[...])
            dx, dg = _rms_bwd(acc[...], xh, r, g_ref[...])
            dx = dx + dx1_ref[...]
            dx_ref[...] = dx
            dxb_ref[...] = dx.astype(bf16)
            _acc_rows(dg_ref, i == 0, dg)

    row = pl.BlockSpec((tm, D_MODEL), lambda i, k: (i, 0))
    vec = pl.BlockSpec((1, D_MODEL), lambda i, k: (0, 0))
    return pl.pallas_call(
        body, name="dh_bwd", grid=(s // tm, nk),
        in_specs=[pl.BlockSpec((tm, tk), lambda i, k: (i, k)), pl.BlockSpec((D_MODEL, tk), lambda i, k: (0, k)), row, row, vec],
        out_specs=[row, row, vec], out_shape=[SDS((s, D_MODEL), f32), SDS((s, D_MODEL), bf16), SDS((1, D_MODEL), f32)],
        scratch_shapes=[pltpu.VMEM((tm, D_MODEL), f32)], compiler_params=_cp("arbitrary", "arbitrary"),
    )(dp, w_in, x, dx1, g)


def _layer_fwd(x, w, sm):
    p, h = _rms_mm(x, sm["g_mix"], w["w_in"])
    z, opre, states = _mixer_fwd(p, sm["lb"], sm["g_out"], sm["w_conv"], sm["ln_g"], sm["ln_b"], sm["w_sg"], sm["b_sg_t"])
    y, merged = _branch_gate(z, w["w_branch"], p)
    x1 = _mm_resid(x, merged, w["w_o"])
    x2, h2, ra = _ffn(x1, sm["g_ffn"], w["w_ff1"], w["w_ff2"])
    saved = dict(x=x, p=p, h=h, z=z, opre=opre, states=states, y=y, merged=merged, x1=x1, h2=h2, ra=ra)
    return x2, saved


def _layer_bwd(dx2, dx2b, sv, w, sm):
    da, act, dx1, dx1b, dg_ffn = _ffn_bwd(dx2, dx2b, sv["x1"], sm["g_ffn"], sv["ra"], w["w_ff1"], w["w_ff2"])
    g_ff2 = _mm_tn(act, dx2b, 1, D_FF, D_MODEL, 512, 1024, name="dw_ff2")[0]
    g_ff1 = _mm_tn(sv["h2"], da, 1, D_MODEL, D_FF, 512, 1024, name="dw_ff1")[0]
    g_o = _mm_tn(sv["merged"], dx1b, 1, D_MODEL, D_MODEL, 512, 1024, name="dw_o")[0]
    dy, dp, dz = _merge_bwd(dx1b, w["w_o"], sv["y"], sv["p"], w["w_branch"])
    g_branch = _mm_tn(sv["z"], dy, 3, BRANCH, D_MODEL, 512, 1024, name="dw_branch")
    dp, vecs, dwsg, dbsg_t = _mixer_bwd(sv["p"], dz, sv["opre"], sv["states"], dp, sm["lb"], sm["g_out"], sm["w_conv"],
                                        sm["ln_g"], sm["ln_b"], sm["w_sg"], sm["b_sg_t"])
    g_in = _mm_tn(sv["h"], dp, 1, D_MODEL, N_COLS, 512, 1536, name="dw_in")[0]
    dx, dxb, dg_mix = _dh_bwd(dp, w["w_in"], sv["x"], dx1, sm["g_mix"])
    big = dict(w_in=g_in, w_branch=g_branch, w_o=g_o, w_ff1=g_ff1, w_ff2=g_ff2)
    small = dict(g_mix=dg_mix, g_ffn=dg_ffn, vecs=vecs, w_sg=dwsg, b_sg_t=dbsg_t)
    return dx, dxb, big, small


BIG = ("w_in", "w_branch", "w_o", "w_ff1", "w_ff2")
ANY = pl.BlockSpec(memory_space=pl.ANY)


def _place():
    return lax.axis_index("x"), lax.axis_index("y"), lax.axis_index("c")


def _al(v, m):
    return pl.multiple_of(v * m, m)


def _shard_of(refs, dev):
    w_in, w_b, w_o, w_1, w_2 = refs
    nb, no, n1, n2 = w_b.shape[-1] // N_DEV, w_o.shape[0] // N_DEV, w_1.shape[-1] // N_DEV, w_2.shape[0] // N_DEV
    return [w_in, w_b.at[:, :, pl.ds(_al(dev, nb), nb)], w_o.at[pl.ds(_al(dev, no), no), :],
            w_1.at[:, pl.ds(_al(dev, n1), n1)], w_2.at[pl.ds(_al(dev, n2), n2), :]]


def _all_gather_layer(layer, shards):
    s_in, s_b, s_o, s_1, s_2 = (shards[n] for n in BIG)
    out_shape = [SDS((N_DEV,) + s_in.shape[1:], bf16), SDS(s_b.shape[1:3] + (s_b.shape[3] * N_DEV,), bf16),
                 SDS((s_o.shape[1] * N_DEV, s_o.shape[2]), bf16), SDS((s_1.shape[1], s_1.shape[2] * N_DEV), bf16),
                 SDS((s_2.shape[1] * N_DEV, s_2.shape[2]), bf16)]
    nt = len(BIG)

    def body(i_in, i_b, i_o, i_1, i_2, o_in, o_b, o_o, o_1, o_2, send_sems, recv_sems, local_sems):
        x, y, c = _place()
        me, sibling = (x, y, c), (x, y, 1 - c)
        chips = [(1 - x, y), (x, 1 - y), (1 - x, 1 - y)]
        mine = [r.at[layer] for r in (i_in, i_b, i_o, i_1, i_2)]

        def block(px, py, pc):
            dev = 4 * px + 2 * py + pc
            return [o_in.at[dev]] + _shard_of((None, o_b, o_o, o_1, o_2), dev)[1:]

        def copies(k, blk, to, src=None):
            dst = block(*blk)
            src = dst if src is None else src
            return [pltpu.make_async_remote_copy(src_ref=src[t], dst_ref=dst[t], send_sem=send_sems.at[k, t], recv_sem=recv_sems.at[k, t],
                                                 device_id=to, device_id_type=MESH) for t in range(nt)]

        local = [pltpu.make_async_copy(mine[t], block(*me)[t], local_sems.at[t]) for t in range(nt)]
        for cp in local:
            cp.start()
        first = copies(0, me, sibling, src=mine)
        for j, chip in enumerate(chips):
            first += copies(1 + j, me, (*chip, c), src=mine)
        for cp in first:
            cp.start()
        passed = [copies(4 + j, (*chip, c), sibling) for j, chip in enumerate(chips)]
        for j, chip in enumerate(chips):
            for cp in copies(1 + j, (*chip, c), me):
                cp.wait_recv()
            for cp in passed[j]:
                cp.start()
        for cp in copies(0, sibling, me):
            cp.wait_recv()
        for j, chip in enumerate(chips):
            for cp in copies(4 + j, (*chip, 1 - c), me):
                cp.wait_recv()
        for cp in first + [cp for grp in passed for cp in grp]:
            cp.wait_send()
        for cp in local:
            cp.wait()

    return pl.pallas_call(
        body, name=f"all_gather_l{layer}", in_specs=[ANY] * nt, out_specs=[ANY] * nt, out_shape=out_shape,
        scratch_shapes=[pltpu.SemaphoreType.DMA((7, nt)), pltpu.SemaphoreType.DMA((7, nt)), pltpu.SemaphoreType.DMA((nt,))],
    )(s_in, s_b, s_o, s_1, s_2)


def _merge_windows(land, tr=512):
    r = land.shape[1]
    tr = min(tr, r)
    chip_cols = 2 * SHARD_IN
    cut = WIN - LANE

    def body(e_ref, o_ref, out_ref):
        out_ref[:, 0:cut] = e_ref[:, 0:cut]
        out_ref[:, cut:WIN] = e_ref[:, cut:WIN] + o_ref[:, 0:LANE]
        out_ref[:, WIN:chip_cols] = o_ref[:, LANE:WIN]

    return pl.pallas_call(
        body, name="merge_windows", grid=(N_DEV // 2, r // tr),
        in_specs=[pl.BlockSpec((None, tr, WIN), lambda k, i: (2 * k, i, 0)), pl.BlockSpec((None, tr, WIN), lambda k, i: (2 * k + 1, i, 0))],
        out_specs=pl.BlockSpec((tr, chip_cols), lambda k, i: (i, k)), out_shape=SDS((r, N_COLS), bf16),
        compiler_params=_cp("parallel", "parallel"),
    )(land, land)


def _grad_slabs(refs, chip, core):
    dev = 2 * chip + core
    parts = _shard_of(refs, dev)
    parts[0] = refs[0].at[:, pl.ds(_al(15 * chip + 7 * core, LANE), WIN)]
    return parts


def _slab_shapes(grads):
    g_in, g_b, g_o, g_1, g_2 = (grads[n] for n in BIG)
    return [(g_in.shape[0], WIN), g_b.shape[:2] + (g_b.shape[2] // N_DEV,), (g_o.shape[0] // N_DEV, g_o.shape[1]),
            (g_1.shape[0], g_1.shape[1] // N_DEV), (g_2.shape[0] // N_DEV, g_2.shape[1])]


def _exchange_on_chip(grads):
    nt, nchip = len(BIG), N_DEV // 2
    shapes = [SDS((nchip,) + sh, bf16) for sh in _slab_shapes(grads)]

    def body(*refs):
        g = refs[:nt]
        staged, landed = refs[nt:2 * nt], refs[2 * nt:3 * nt]
        send_sems, recv_sems, local_sems = refs[3 * nt:]
        x, y, c = _place()
        remote, local = [], []
        for j in range(nchip):
            away, keep = _grad_slabs(g, j, 1 - c), _grad_slabs(g, j, c)
            for t in range(nt):
                remote.append(pltpu.make_async_remote_copy(src_ref=away[t], dst_ref=landed[t].at[j], send_sem=send_sems.at[j, t],
                                                           recv_sem=recv_sems.at[j, t], device_id=(x, y, 1 - c), device_id_type=MESH))
                local.append(pltpu.make_async_copy(keep[t], staged[t].at[j], local_sems.at[j, t]))
        for cp in remote + local:
            cp.start()
        for cp in remote:
            cp.wait_recv()
        for cp in remote:
            cp.wait_send()
        for cp in local:
            cp.wait()

    out = pl.pallas_call(
        body, name="rs_on_chip", in_specs=[ANY] * nt, out_specs=[ANY] * (2 * nt), out_shape=shapes + shapes,
        scratch_shapes=[pltpu.SemaphoreType.DMA((nchip, nt)), pltpu.SemaphoreType.DMA((nchip, nt)), pltpu.SemaphoreType.DMA((nchip, nt))],
    )(*(grads[n] for n in BIG))
    return out[:nt], out[nt:]


def _add_bf16(a, b, tr=1024):
    shape = a.shape
    cols = shape[-1]
    a2, b2 = a.reshape(-1, cols), b.reshape(-1, cols)
    rows = a2.shape[0]
    tr = min(tr, rows)

    def body(a_ref, b_ref, o_ref):
        o_ref[...] = (a_ref[...].astype(f32) + b_ref[...].astype(f32)).astype(bf16)

    blk = pl.BlockSpec((tr, cols), lambda i: (i, 0))
    out = pl.pallas_call(body, name="chip_sum", grid=(rows // tr,), in_specs=[blk, blk], out_specs=blk, out_shape=SDS((rows, cols), bf16),
                         compiler_params=_cp("parallel"))(a2, b2)
    return out.reshape(shape)


def _exchange_between_chips(layer, sums, own, landed):
    nt = len(BIG)

    def body(*refs):
        sums_r = refs[:nt]
        own_r, land_r = refs[3 * nt:4 * nt], refs[4 * nt:5 * nt]
        send_sems, recv_sems, local_sems = refs[5 * nt:]
        x, y, c = _place()
        chips = [(1 - x, y), (x, 1 - y), (1 - x, 1 - y)]
        remote = []
        for k, (cx, cy) in enumerate(chips):
            for t in range(nt):
                remote.append(pltpu.make_async_remote_copy(src_ref=sums_r[t].at[2 * cx + cy], dst_ref=land_r[t].at[k, layer],
                                                           send_sem=send_sems.at[k, t], recv_sem=recv_sems.at[k, t],
                                                           device_id=(cx, cy, c), device_id_type=MESH))
        local = [pltpu.make_async_copy(sums_r[t].at[2 * x + y], own_r[t].at[layer], local_sems.at[t]) for t in range(nt)]
        for cp in remote + local:
            cp.start()
        for cp in remote:
            cp.wait_recv()
        for cp in remote:
            cp.wait_send()
        for cp in local:
            cp.wait()

    out = pl.pallas_call(
        body, name=f"rs_between_chips_l{layer}", in_specs=[ANY] * (3 * nt), out_specs=[ANY] * (2 * nt),
        out_shape=[SDS(a.shape, a.dtype) for a in own + landed],
        input_output_aliases={nt + i: i for i in range(2 * nt)},
        scratch_shapes=[pltpu.SemaphoreType.DMA((3, nt)), pltpu.SemaphoreType.DMA((3, nt)), pltpu.SemaphoreType.DMA((nt,))],
    )(*sums, *own, *landed)
    return out[:nt], out[nt:]


def _all_reduce_rows(pack):
    rows = pack.shape[0]
    blk = rows // N_DEV

    def body(in_ref, out_ref, land, send1, recv1, send2, recv2):
        x, y, c = _place()
        me = 4 * x + 2 * y + c
        others = [(px, py, pc) for px in range(2) for py in range(2) for pc in range(2)]

        def is_me(p):
            return jnp.logical_and(jnp.logical_and(p[0] == x, p[1] == y), p[2] == c)

        land[me] = in_ref[pl.ds(_al(me, blk), blk), :]
        for d, p in enumerate(others):
            @pl.when(jnp.logical_not(is_me(p)))
            def _():
                pltpu.make_async_remote_copy(src_ref=in_ref.at[pl.ds(d * blk, blk), :], dst_ref=land.at[me], send_sem=send1.at[d],
                                             recv_sem=recv1.at[me], device_id=p, device_id_type=MESH).start()
        for d, p in enumerate(others):
            @pl.when(jnp.logical_not(is_me(p)))
            def _():
                cp = pltpu.make_async_remote_copy(src_ref=in_ref.at[pl.ds(d * blk, blk), :], dst_ref=land.at[d], send_sem=send1.at[d],
                                                  recv_sem=recv1.at[d], device_id=p, device_id_type=MESH)
                cp.wait_recv()
                cp.wait_send()
        total = land[0]
        for d in range(1, N_DEV):
            total = total + land[d]
        out_ref[pl.ds(_al(me, blk), blk), :] = total
        for d, p in enumerate(others):
            @pl.when(jnp.logical_not(is_me(p)))
            def _():
                mine = out_ref.at[pl.ds(_al(me, blk), blk), :]
                pltpu.make_async_remote_copy(src_ref=mine, dst_ref=mine, send_sem=send2.at[d], recv_sem=recv2.at[me],
                                             device_id=p, device_id_type=MESH).start()
        for d, p in enumerate(others):
            @pl.when(jnp.logical_not(is_me(p)))
            def _():
                theirs = out_ref.at[pl.ds(d * blk, blk), :]
                cp = pltpu.make_async_remote_copy(src_ref=theirs, dst_ref=theirs, send_sem=send2.at[d], recv_sem=recv2.at[d],
                                                  device_id=p, device_id_type=MESH)
                cp.wait_recv()
                cp.wait_send()

    vm = pl.BlockSpec(memory_space=pltpu.VMEM)
    return pl.pallas_call(
        body, name="all_reduce_rows", in_specs=[vm], out_specs=vm, out_shape=SDS((rows, LANE), f32),
        scratch_shapes=[pltpu.VMEM((N_DEV, blk, LANE), f32)] + [pltpu.SemaphoreType.DMA((N_DEV,))] * 4,
        compiler_params=pltpu.CompilerParams(vmem_limit_bytes=VMEM_LIMIT),
    )(pack)


def _lower_bounds_fwd(lower):
    def body(l_ref, o_ref):
        sm = _layer_softmax(l_ref)
        run = jnp.zeros_like(sm[0])
        for l in range(DEPTH):
            o_ref[l:l + 1, :] = run
            if l + 1 < DEPTH:
                run = run + sm[l + 1]

    return pl.pallas_call(body, name="lower_bounds_fwd", out_shape=SDS(lower.shape, f32))(lower)


def _layer_softmax(l_ref):
    rows = [l_ref[l:l + 1, :] for l in range(DEPTH)]
    top = functools.reduce(jnp.maximum, rows)
    e = [jnp.exp(r - top) for r in rows]
    tot = functools.reduce(lambda a, b: a + b, e)
    return [v / tot for v in e]


def _lower_bounds_bwd(lower, dlbs):
    def body(l_ref, d_ref, o_ref):
        sm = _layer_softmax(l_ref)
        dsm = [None] * DEPTH
        run = jnp.zeros_like(sm[0])
        dsm[0] = run
        for l in reversed(range(1, DEPTH)):
            run = run + d_ref[l:l + 1, :]
            dsm[l] = run
        inner = functools.reduce(lambda a, b: a + b, [sm[l] * dsm[l] for l in range(DEPTH)])
        for l in range(DEPTH):
            o_ref[l:l + 1, :] = sm[l] * (dsm[l] - inner)

    return pl.pallas_call(body, name="lower_bounds_bwd", out_shape=SDS(lower.shape, f32))(lower, dlbs)


_ADAM_C1 = 1.0 - ADAM_B1 ** ADAM_STEP
_ADAM_C2 = 1.0 - ADAM_B2 ** ADAM_STEP


def _adamw(w, g, m, v):
    m = ADAM_B1 * m + (1.0 - ADAM_B1) * g
    v = ADAM_B2 * v + (1.0 - ADAM_B2) * (g * g)
    delta = -ADAM_LR * ((m / _ADAM_C1) / (jnp.sqrt(v / _ADAM_C2) + ADAM_EPS) + ADAM_WD * w)
    return delta, m, v


def _adam_big(w, m, v, own, landed, windowed=False, tr=512):
    shape = w.shape
    cols = shape[-1]
    gcols = own.shape[-1]
    w3, m3, v3 = (a.reshape(DEPTH, -1, cols) for a in (w, m, v))
    own3 = own.reshape(DEPTH, -1, gcols)
    land4 = landed.reshape(3, DEPTH, -1, gcols)
    rows = w3.shape[1]
    tr = min(tr, rows)

    def body(w_ref, m_ref, v_ref, own_ref, land_ref, g_ref, d_ref, nm_ref, nv_ref):
        g = own_ref[...].astype(f32)
        for k in range(3):
            g = g + land_ref[k].astype(f32)
        if windowed:
            g = pltpu.roll(g, SHARD_IN * lax.axis_index("c"), 1)[:, :cols]
        delta, nm, nv = _adamw(w_ref[...], g, m_ref[...], v_ref[...])
        g_ref[...] = g
        d_ref[...] = delta
        nm_ref[...] = nm
        nv_ref[...] = nv

    blk = pl.BlockSpec((None, tr, cols), lambda l, i: (l, i, 0))
    outs = pl.pallas_call(
        body, name="adam_big", grid=(DEPTH, rows // tr),
        in_specs=[blk, blk, blk, pl.BlockSpec((None, tr, gcols), lambda l, i: (l, i, 0)),
                  pl.BlockSpec((3, None, tr, gcols), lambda l, i: (0, l, i, 0))],
        out_specs=[blk] * 4, out_shape=[SDS(w3.shape, f32)] * 4, compiler_params=_cp("parallel", "parallel"),
    )(w3, m3, v3, own3, land4)
    return [o.reshape(shape) for o in outs]


def _adam_rows(w, g, m, v):
    def body(w_ref, g_ref, m_ref, v_ref, d_ref, nm_ref, nv_ref):
        delta, nm, nv = _adamw(w_ref[...], g_ref[...], m_ref[...], v_ref[...])
        d_ref[...] = delta
        nm_ref[...] = nm
        nv_ref[...] = nv

    return pl.pallas_call(body, name="adam_rows", out_shape=[SDS(w.shape, f32)] * 3)(w, g, m, v)


SMALL = ("g_mix", "lower_bounds", "g_hgrn_out", "w_conv", "sg_ln_g", "sg_ln_b", "w_sg", "b_sg", "g_ffn", "g_final")
WEIGHTS = ("w_in", "g_mix", "lower_bounds", "g_hgrn_out", "w_conv", "sg_ln_g", "sg_ln_b", "w_sg", "b_sg", "w_branch", "w_o", "g_ffn",
           "w_ff1", "w_ff2", "g_final")


def _pack_rows(arrays, multiple):
    flat = jnp.concatenate([a.reshape(-1) for a in arrays])
    rows = -(-flat.shape[0] // (LANE * multiple)) * multiple
    return jnp.pad(flat, (0, rows * LANE - flat.shape[0])).reshape(rows, LANE)


def _unpack_rows(pack, like):
    flat = pack.reshape(-1)
    out, at = [], 0
    for a in like:
        out.append(flat[at:at + a.size].reshape(a.shape))
        at += a.size
    return out


def kernel(x, w_in, g_mix, lower_bounds, g_hgrn_out, w_conv, sg_ln_g, sg_ln_b, w_sg, b_sg, w_branch, w_o, g_ffn, w_ff1, w_ff2, g_final, loss_target, m_w_in, m_g_mix, m_lower_bounds, m_g_hgrn_out, m_w_conv, m_sg_ln_g, m_sg_ln_b, m_w_sg, m_b_sg, m_w_branch, m_w_o, m_g_ffn, m_w_ff1, m_w_ff2, m_g_final, v_w_in, v_g_mix, v_lower_bounds, v_g_hgrn_out, v_w_conv, v_sg_ln_g, v_sg_ln_b, v_w_sg, v_b_sg, v_w_branch, v_w_o, v_g_ffn, v_w_ff1, v_w_ff2, v_g_final):
    weights = dict(w_in=w_in, g_mix=g_mix, lower_bounds=lower_bounds, g_hgrn_out=g_hgrn_out, w_conv=w_conv, sg_ln_g=sg_ln_g,
                   sg_ln_b=sg_ln_b, w_sg=w_sg, b_sg=b_sg, w_branch=w_branch, w_o=w_o, g_ffn=g_ffn, w_ff1=w_ff1, w_ff2=w_ff2, g_final=g_final)
    mom1 = dict(w_in=m_w_in, g_mix=m_g_mix, lower_bounds=m_lower_bounds, g_hgrn_out=m_g_hgrn_out, w_conv=m_w_conv, sg_ln_g=m_sg_ln_g,
                sg_ln_b=m_sg_ln_b, w_sg=m_w_sg, b_sg=m_b_sg, w_branch=m_w_branch, w_o=m_w_o, g_ffn=m_g_ffn, w_ff1=m_w_ff1, w_ff2=m_w_ff2,
                g_final=m_g_final)
    mom2 = dict(w_in=v_w_in, g_mix=v_g_mix, lower_bounds=v_lower_bounds, g_hgrn_out=v_g_hgrn_out, w_conv=v_w_conv, sg_ln_g=v_sg_ln_g,
                sg_ln_b=v_sg_ln_b, w_sg=v_w_sg, b_sg=v_b_sg, w_branch=v_w_branch, w_o=v_w_o, g_ffn=v_g_ffn, w_ff1=v_w_ff1, w_ff2=v_w_ff2,
                g_final=v_g_final)
    xi, yi, ci = _place()
    dev = 4 * xi + 2 * yi + ci
    conv_cols = w_conv.shape[-1]

    w_in_b = w_in.astype(bf16)
    pad = WIN - SHARD_IN
    w_in_win = jnp.where(ci == 0, jnp.pad(w_in_b, ((0, 0), (0, 0), (0, pad))), jnp.pad(w_in_b, ((0, 0), (0, 0), (pad, 0))))
    shards = dict(w_in=w_in_win, w_branch=w_branch.astype(bf16), w_o=w_o.astype(bf16), w_ff1=w_ff1.astype(bf16), w_ff2=w_ff2.astype(bf16))

    conv_place = lax.dynamic_update_slice(jnp.zeros((DEPTH, 3, BRANCH), f32), w_conv, (0, 0, dev * conv_cols))
    (w_conv_full,) = _unpack_rows(_all_reduce_rows(_pack_rows([conv_place], 8 * N_DEV)), [conv_place])
    lbs = _lower_bounds_fwd(lower_bounds)

    def small_of(l):
        return dict(g_mix=g_mix[l][None], lb=lbs[l][None], g_out=g_hgrn_out[l][None], w_conv=w_conv_full[l], ln_g=sg_ln_g[l][None],
                    ln_b=sg_ln_b[l][None], w_sg=w_sg[l], b_sg_t=b_sg[l].T, g_ffn=g_ffn[l][None])

    act = x[0]
    full, saved = [], []
    for l in range(DEPTH):
        land, f_b, f_o, f_1, f_2 = _all_gather_layer(l, shards)
        full.append(dict(w_in=_merge_windows(land), w_branch=f_b, w_o=f_o, w_ff1=f_1, w_ff2=f_2))
        act, sv = _layer_fwd(act, full[l], small_of(l))
        saved.append(sv)
    loss_row, dx, dxb, dg_final = _final(act, loss_target[0], g_final[None])
    loss = lax.psum(loss_row[0, 0], ("x", "y", "c"))

    slab_shapes = _slab_shapes({n: full[0][n] for n in BIG})
    own = [lax.empty((DEPTH,) + sh, bf16) for sh in slab_shapes]
    landed = [lax.empty((3, DEPTH) + sh, bf16) for sh in slab_shapes]
    small_grads = [None] * DEPTH
    for l in reversed(range(DEPTH)):
        dx, dxb, big, small_grads[l] = _layer_bwd(dx, dxb, saved[l], full[l], small_of(l))
        staged, received = _exchange_on_chip(big)
        sums = [_add_bf16(a, b) for a, b in zip(staged, received)]
        own, landed = _exchange_between_chips(l, sums, own, landed)

    stack = lambda f: jnp.stack([f(small_grads[l]) for l in range(DEPTH)])
    d_lower = _lower_bounds_bwd(lower_bounds, stack(lambda s: s["vecs"][0]))
    local_small = dict(g_mix=stack(lambda s: s["g_mix"][0]), lower_bounds=d_lower, g_hgrn_out=stack(lambda s: s["vecs"][1]),
                       w_conv=stack(lambda s: s["vecs"][4:7]), sg_ln_g=stack(lambda s: s["vecs"][2]), sg_ln_b=stack(lambda s: s["vecs"][3]),
                       w_sg=stack(lambda s: s["w_sg"]), b_sg=stack(lambda s: s["b_sg_t"].T), g_ffn=stack(lambda s: s["g_ffn"][0]),
                       g_final=dg_final[0])
    order = [local_small[n] for n in SMALL]
    grads = dict(zip(SMALL, _unpack_rows(_all_reduce_rows(_pack_rows(order, 8 * N_DEV)), order)))
    grads["w_conv"] = lax.dynamic_slice(grads["w_conv"], (0, 0, dev * conv_cols), (DEPTH, 3, conv_cols))

    deltas, new_m, new_v = {}, {}, {}
    for t, n in enumerate(BIG):
        grads[n], deltas[n], new_m[n], new_v[n] = _adam_big(weights[n], mom1[n], mom2[n], own[t], landed[t], windowed=(n == "w_in"))
    packs = [_pack_rows([d[n] for n in SMALL], 8) for d in (weights, grads, mom1, mom2)]
    like = [weights[n] for n in SMALL]
    for out, pack in zip((deltas, new_m, new_v), _adam_rows(*packs)):
        out.update(zip(SMALL, _unpack_rows(pack, like)))

    return (loss, dx[None], *[grads[n] for n in WEIGHTS], *[deltas[n] for n in WEIGHTS], *[new_m[n] for n in WEIGHTS],
            *[new_v[n] for n in WEIGHTS])
```

```python
import functools

import jax
import jax.numpy as jnp
from jax import lax
from jax.experimental import pallas as pl
from jax.experimental.pallas import tpu as pltpu

f32 = jnp.float32
bf16 = jnp.bfloat16
SDS = jax.ShapeDtypeStruct
MESH = pl.DeviceIdType.MESH

D_MODEL = 1024
BRANCH = 512
N_COLS = 7680
D_FF = 4096
DEPTH = 4
HEADS = 4
HEAD_DIM = 128
HGRN_CHUNK = 64
SG_CHUNK = 128
SG_GROUPS = 4
NORM_EPS = 1e-6
LN_EPS = 1e-5
LB_FLOOR = 1e-30
N_DEV = 8
SHARD_IN = N_COLS // N_DEV
WIN = 1024
LANE = 128
GATE_COL0 = 9 * BRANCH

ADAM_LR = 0.001
ADAM_B1 = 0.9
ADAM_B2 = 0.999
ADAM_EPS = 1e-08
ADAM_WD = 0.01
ADAM_STEP = 10

MIX_TILE = 256
VMEM_LIMIT = 56 * 1024 * 1024


def _cp(*sem):
    return pltpu.CompilerParams(dimension_semantics=sem or None, vmem_limit_bytes=VMEM_LIMIT)


def _dot(a, b):
    return jnp.dot(a, b, preferred_element_type=f32)


def _dot_nt(a, b):
    return lax.dot_general(a, b, (((1,), (1,)), ((), ())), preferred_element_type=f32)


def _dot_tn(a, b):
    return lax.dot_general(a, b, (((0,), (0,)), ((), ())), preferred_element_type=f32)


def _dot_exact(a, b):
    return jnp.dot(a, b, precision=lax.Precision.HIGHEST, preferred_element_type=f32)


def _sigmoid(x):
    return jax.nn.sigmoid(x)


_GELU_C = 0.7978845608028654
_GELU_A = 0.044715


def _gelu(x):
    return 0.5 * x * (1.0 + jnp.tanh(_GELU_C * (x + _GELU_A * x * x * x)))


def _gelu_grad(x):
    x2 = x * x
    t = jnp.tanh(_GELU_C * (x + _GELU_A * x * x2))
    return 0.5 * (1.0 + t) + 0.5 * x * (1.0 - t * t) * _GELU_C * (1.0 + 3.0 * _GELU_A * x2)


def _rms_stats(x):
    r = lax.rsqrt(jnp.mean(x * x, axis=-1, keepdims=True) + NORM_EPS)
    return r, x * r


def _rms_bwd(dh, xh, r, g):
    dg = jnp.sum(dh * xh, axis=0, keepdims=True)
    dxn = dh * g
    dx = r * (dxn - xh * jnp.mean(dxn * xh, axis=-1, keepdims=True))
    return dx, dg


def _tri(n, upper=False):
    r = lax.broadcasted_iota(jnp.int32, (n, n), 0)
    c = lax.broadcasted_iota(jnp.int32, (n, n), 1)
    return (c >= r) if upper else (c <= r)


def _acc_rows(ref, first, val):
    @pl.when(first)
    def _():
        ref[...] = val

    @pl.when(jnp.logical_not(first))
    def _():
        ref[...] += val


def _rms_mm(x, g, w, tm=512, tn=1920):
    s, n = x.shape[0], w.shape[1]

    def body(x_ref, g_ref, w_ref, p_ref, h_ref, hs):
        @pl.when(pl.program_id(1) == 0)
        def _():
            _, xh = _rms_stats(x_ref[...])
            hv = (xh * g_ref[...]).astype(bf16)
            hs[...] = hv
            h_ref[...] = hv

        p_ref[...] = _dot(hs[...], w_ref[...])

    return pl.pallas_call(
        body, name="rms_mm", grid=(s // tm, n // tn),
        in_specs=[pl.BlockSpec((tm, D_MODEL), lambda i, j: (i, 0)), pl.BlockSpec((1, D_MODEL), lambda i, j: (0, 0)),
                  pl.BlockSpec((D_MODEL, tn), lambda i, j: (0, j))],
        out_specs=[pl.BlockSpec((tm, tn), lambda i, j: (i, j)), pl.BlockSpec((tm, D_MODEL), lambda i, j: (i, 0))],
        out_shape=[SDS((s, n), f32), SDS((s, D_MODEL), bf16)],
        scratch_shapes=[pltpu.VMEM((tm, D_MODEL), bf16)], compiler_params=_cp("parallel", "arbitrary"),
    )(x, g, w)


def _hgrn_gates(fp, lb):
    logf = jnp.logaddexp(jnp.log(jnp.maximum(lb, LB_FLOOR)), jnp.log1p(-lb) + jax.nn.log_sigmoid(fp))
    snf = _sigmoid(-fp)
    return logf, snf, (1.0 - lb) * snf


def _p_specs(tile, cols, row_map):
    return [pl.BlockSpec((tile, BRANCH), functools.partial(lambda c, i: (row_map(i), c), c)) for c in cols]


def _mixer_fwd(p, lb, gout, wconv, lng, lnb, wsg, bsg_t):
    s = p.shape[0]
    tt = MIX_TILE
    nch = tt // HGRN_CHUNK

    def body(q_ref, fp_ref, iv_ref, go_ref, bg_ref, cg_ref, xc_ref, u_ref, v_ref, lb_ref, gout_ref, wconv_ref, lng_ref,
             lnb_ref, wsg_ref, bsg_ref, z_ref, opre_ref, st_ref, st_scr, zbuf):
        @pl.when(pl.program_id(0) == 0)
        def _():
            st_scr[...] = jnp.zeros_like(st_scr)
            zbuf[0:8, :] = jnp.zeros((8, BRANCH), f32)

        lbv = lb_ref[...]
        q_raw = q_ref[...]
        qs = q_raw * _sigmoid(q_raw)
        logf, _, kk = _hgrn_gates(fp_ref[...], lbv)
        iv = iv_ref[...]
        causal = _tri(HGRN_CHUNK)
        tri = causal.astype(f32)
        last_row = lax.broadcasted_iota(jnp.int32, (HGRN_CHUNK, 1), 0) == HGRN_CHUNK - 1
        for c in range(nch):
            rows = slice(HGRN_CHUNK * c, HGRN_CHUNK * (c + 1))
            b = _dot_exact(tri, logf[rows])
            bl = jnp.sum(jnp.where(last_row, b, 0.0), axis=0, keepdims=True)
            qb = (qs[rows] * jnp.exp(b)).astype(bf16)
            kb = (kk[rows] * jnp.exp(-b)).astype(bf16)
            kd = (kk[rows] * jnp.exp(bl - b)).astype(bf16)
            ebl = jnp.exp(bl)
            vc = iv[rows].astype(bf16)
            for h in range(HEADS):
                sl = slice(HEAD_DIM * h, HEAD_DIM * (h + 1))
                st = st_scr[h]
                st_ref[c, h] = st
                a = jnp.where(causal, _dot_nt(qb[:, sl], kb[:, sl]), 0.0)
                opre_ref[rows, sl] = _dot(a.astype(bf16), vc[:, sl]) + _dot_nt(qb[:, sl], st.astype(bf16))
                st_scr[h] = st * ebl[:, sl] + _dot_tn(vc[:, sl], kd[:, sl])

        o = opre_ref[...]
        go = go_ref[...]
        gout_v = gout_ref[...]
        for h in range(HEADS):
            sl = slice(HEAD_DIM * h, HEAD_DIM * (h + 1))
            _, oh = _rms_stats(o[:, sl])
            z_ref[:, sl] = (oh * gout_v[:, sl] * _sigmoid(go[:, sl])).astype(bf16)

        zc = cg_ref[...] * xc_ref[...]
        zbuf[8:8 + tt, :] = zc
        y = wconv_ref[0:1, :] * zbuf[pl.ds(6, tt), :] + wconv_ref[1:2, :] * zbuf[pl.ds(7, tt), :] + wconv_ref[2:3, :] * zc
        z_ref[:, BRANCH:2 * BRANCH] = (bg_ref[...] * y).astype(bf16)
        zbuf[0:8, :] = zbuf[tt:tt + 8, :]

        ug = _gelu(u_ref[...])
        vg = _gelu(v_ref[...])
        vcen = vg - jnp.mean(vg, axis=-1, keepdims=True)
        rstd = lax.rsqrt(jnp.mean(vcen * vcen, axis=-1, keepdims=True) + LN_EPS)
        vn = (vcen * rstd * lng_ref[...] + lnb_ref[...]).astype(bf16)
        low = _tri(SG_CHUNK)
        for g in range(SG_GROUPS):
            sl = slice(LANE * g, LANE * (g + 1))
            wm = jnp.where(low, wsg_ref[g], 0.0).astype(bf16)
            bias = bsg_ref[:, g:g + 1]
            for cc in range(tt // SG_CHUNK):
                rows = slice(SG_CHUNK * cc, SG_CHUNK * (cc + 1))
                sv = _dot(wm, vn[rows, sl]) + bias
                z_ref[rows, 2 * BRANCH + LANE * g:2 * BRANCH + LANE * (g + 1)] = (ug[rows, sl] * sv).astype(bf16)

    full = lambda shape: pl.BlockSpec(shape, lambda i: (0,) * len(shape))
    return pl.pallas_call(
        body, name="mixer_fwd", grid=(s // tt,),
        in_specs=_p_specs(tt, range(9), lambda i: i) + [full((1, BRANCH)), full((1, BRANCH)), full((3, BRANCH)), full((1, BRANCH)),
                                                        full((1, BRANCH)), full((SG_GROUPS, SG_CHUNK, SG_CHUNK)), full((SG_CHUNK, SG_GROUPS))],
        out_specs=[pl.BlockSpec((tt, 3 * BRANCH), lambda i: (i, 0)), pl.BlockSpec((tt, BRANCH), lambda i: (i, 0)),
                   pl.BlockSpec((nch, HEADS, HEAD_DIM, HEAD_DIM), lambda i: (i, 0, 0, 0))],
        out_shape=[SDS((s, 3 * BRANCH), bf16), SDS((s, BRANCH), f32), SDS((s // HGRN_CHUNK, HEADS, HEAD_DIM, HEAD_DIM), f32)],
        scratch_shapes=[pltpu.VMEM((HEADS, HEAD_DIM, HEAD_DIM), f32), pltpu.VMEM((tt + 8, BRANCH), f32)],
        compiler_params=_cp("arbitrary"),
    )(*([p] * 9), lb, gout, wconv, lng, lnb, wsg, bsg_t)


def _branch_gate(z, wb, p, tm=256):
    s = z.shape[0]
    half = 3 * D_MODEL // 2

    def body(z_ref, wb_ref, ga_ref, gb_ref, y_ref, m_ref):
        ga, gb = ga_ref[...], gb_ref[...]
        gates = [ga[:, :D_MODEL], jnp.concatenate([ga[:, D_MODEL:], gb[:, :D_MODEL // 2]], axis=1), gb[:, D_MODEL // 2:]]
        acc = None
        for n in range(3):
            yn = _dot(z_ref[:, BRANCH * n:BRANCH * (n + 1)], wb_ref[n])
            y_ref[:, D_MODEL * n:D_MODEL * (n + 1)] = yn.astype(bf16)
            t = _sigmoid(gates[n]) * yn
            acc = t if acc is None else acc + t
        m_ref[...] = acc.astype(bf16)

    blk0 = GATE_COL0 // half
    return pl.pallas_call(
        body, name="branch_gate", grid=(s // tm,),
        in_specs=[pl.BlockSpec((tm, 3 * BRANCH), lambda i: (i, 0)), pl.BlockSpec((3, BRANCH, D_MODEL), lambda i: (0, 0, 0)),
                  pl.BlockSpec((tm, half), lambda i: (i, blk0)), pl.BlockSpec((tm, half), lambda i: (i, blk0 + 1))],
        out_specs=[pl.BlockSpec((tm, 3 * D_MODEL), lambda i: (i, 0)), pl.BlockSpec((tm, D_MODEL), lambda i: (i, 0))],
        out_shape=[SDS((s, 3 * D_MODEL), bf16), SDS((s, D_MODEL), bf16)], compiler_params=_cp("parallel"),
    )(z, wb, p, p)


def _mm_resid(x, m, wo, tm=512):
    s = x.shape[0]

    def body(x_ref, m_ref, w_ref, o_ref):
        o_ref[...] = x_ref[...] + _dot(m_ref[...], w_ref[...])

    return pl.pallas_call(
        body, name="mm_resid", grid=(s // tm,),
        in_specs=[pl.BlockSpec((tm, D_MODEL), lambda i: (i, 0)), pl.BlockSpec((tm, D_MODEL), lambda i: (i, 0)),
                  pl.BlockSpec((D_MODEL, D_MODEL), lambda i: (0, 0))],
        out_specs=pl.BlockSpec((tm, D_MODEL), lambda i: (i, 0)), out_shape=SDS((s, D_MODEL), f32), compiler_params=_cp("parallel"),
    )(x, m, wo)


def _ffn(x1, g, w1, w2, tm=512, tf=1024):
    s = x1.shape[0]
    nf = D_FF // tf

    def body(x_ref, g_ref, w1_ref, w2_ref, o_ref, h_ref, ra_ref, hs, acc):
        f = pl.program_id(1)

        @pl.when(f == 0)
        def _():
            _, xh = _rms_stats(x_ref[...])
            hv = (xh * g_ref[...]).astype(bf16)
            hs[...] = hv
            h_ref[...] = hv
            acc[...] = jnp.zeros_like(acc)

        ra = jnp.maximum(_dot(hs[...], w1_ref[...]), 0.0)
        ra_ref[...] = ra.astype(bf16)
        acc[...] += _dot((ra * ra).astype(bf16), w2_ref[...])

        @pl.when(f == nf - 1)
        def _():
            o_ref[...] = x_ref[...] + acc[...]

    return pl.pallas_call(
        body, name="ffn", grid=(s // tm, nf),
        in_specs=[pl.BlockSpec((tm, D_MODEL), lambda i, f: (i, 0)), pl.BlockSpec((1, D_MODEL), lambda i, f: (0, 0)),
                  pl.BlockSpec((D_MODEL, tf), lambda i, f: (0, f)), pl.BlockSpec((tf, D_MODEL), lambda i, f: (f, 0))],
        out_specs=[pl.BlockSpec((tm, D_MODEL), lambda i, f: (i, 0)), pl.BlockSpec((tm, D_MODEL), lambda i, f: (i, 0)),
                   pl.BlockSpec((tm, tf), lambda i, f: (i, f))],
        out_shape=[SDS((s, D_MODEL), f32), SDS((s, D_MODEL), bf16), SDS((s, D_FF), bf16)],
        scratch_shapes=[pltpu.VMEM((tm, D_MODEL), bf16), pltpu.VMEM((tm, D_MODEL), f32)], compiler_params=_cp("parallel", "arbitrary"),
    )(x1, g, w1, w2)


def _final(x, target, g, tm=512):
    s = x.shape[0]

    def body(x_ref, t_ref, g_ref, loss_ref, dx_ref, dxb_ref, dg_ref):
        first = pl.program_id(0) == 0
        gv = g_ref[...]
        r, xh = _rms_stats(x_ref[...])
        e = xh * gv - t_ref[...]
        tile_loss = 0.5 * jnp.sum(jnp.mean(e * e, axis=-1, keepdims=True), axis=0, keepdims=True)
        dx, dg = _rms_bwd(e * (1.0 / D_MODEL), xh, r, gv)
        dx_ref[...] = dx
        dxb_ref[...] = dx.astype(bf16)
        _acc_rows(dg_ref, first, dg)
        _acc_rows(loss_ref, first, jnp.broadcast_to(tile_loss, (1, LANE)))

    row = pl.BlockSpec((tm, D_MODEL), lambda i: (i, 0))
    return pl.pallas_call(
        body, name="final_loss", grid=(s // tm,), in_specs=[row, row, pl.BlockSpec((1, D_MODEL), lambda i: (0, 0))],
        out_specs=[pl.BlockSpec((1, LANE), lambda i: (0, 0)), row, row, pl.BlockSpec((1, D_MODEL), lambda i: (0, 0))],
        out_shape=[SDS((1, LANE), f32), SDS((s, D_MODEL), f32), SDS((s, D_MODEL), bf16), SDS((1, D_MODEL), f32)],
        compiler_params=_cp("arbitrary"),
    )(x, target, g)


def _ffn_bwd(dx2, dx2b, x1, g, ra, w1, w2, tm=512, tf=1024):
    s = x1.shape[0]
    nf = D_FF // tf

    def body(dx_ref, dxb_ref, x_ref, g_ref, ra_ref, w1_ref, w2_ref, da_ref, act_ref, dx1_ref, dx1b_ref, dg_ref, acc):
        i, f = pl.program_id(0), pl.program_id(1)

        @pl.when(f == 0)
        def _():
            acc[...] = jnp.zeros_like(acc)

        rav = ra_ref[...].astype(f32)
        da = (_dot_nt(dxb_ref[...], w2_ref[...]) * (2.0 * rav)).astype(bf16)
        da_ref[...] = da
        act_ref[...] = (rav * rav).astype(bf16)
        acc[...] += _dot_nt(da, w1_ref[...])

        @pl.when(f == nf - 1)
        def _():
            r, xh = _rms_stats(x_ref[...])
            dx, dg = _rms_bwd(acc[...], xh, r, g_ref[...])
            dx = dx + dx_ref[...]
            dx1_ref[...] = dx
            dx1b_ref[...] = dx.astype(bf16)
            _acc_rows(dg_ref, i == 0, dg)

    row = pl.BlockSpec((tm, D_MODEL), lambda i, f: (i, 0))
    col = pl.BlockSpec((tm, tf), lambda i, f: (i, f))
    return pl.pallas_call(
        body, name="ffn_bwd", grid=(s // tm, nf),
        in_specs=[row, row, row, pl.BlockSpec((1, D_MODEL), lambda i, f: (0, 0)), col,
                  pl.BlockSpec((D_MODEL, tf), lambda i, f: (0, f)), pl.BlockSpec((tf, D_MODEL), lambda i, f: (f, 0))],
        out_specs=[col, col, row, row, pl.BlockSpec((1, D_MODEL), lambda i, f: (0, 0))],
        out_shape=[SDS((s, D_FF), bf16), SDS((s, D_FF), bf16), SDS((s, D_MODEL), f32), SDS((s, D_MODEL), bf16), SDS((1, D_MODEL), f32)],
        scratch_shapes=[pltpu.VMEM((tm, D_MODEL), f32)], compiler_params=_cp("arbitrary", "arbitrary"),
    )(dx2, dx2b, x1, g, ra, w1, w2)


def _mm_tn(a, b, nb, m, n, tm, tn, ts=512, name="mm_tn"):
    s = a.shape[0]
    mi, nj, ns = m // tm, n // tn, s // ts

    def body(a_ref, b_ref, o_ref, acc):
        t = pl.program_id(3)

        @pl.when(t == 0)
        def _():
            acc[...] = jnp.zeros_like(acc)

        acc[...] += _dot_tn(a_ref[...], b_ref[...])

        @pl.when(t == ns - 1)
        def _():
            o_ref[...] = acc[...].astype(bf16)

    return pl.pallas_call(
        body, name=name, grid=(nb, mi, nj, ns),
        in_specs=[pl.BlockSpec((ts, tm), lambda k, i, j, t: (t, k * mi + i)), pl.BlockSpec((ts, tn), lambda k, i, j, t: (t, k * nj + j))],
        out_specs=pl.BlockSpec((None, tm, tn), lambda k, i, j, t: (k, i, j)), out_shape=SDS((nb, m, n), bf16),
        scratch_shapes=[pltpu.VMEM((tm, tn), f32)], compiler_params=_cp("parallel", "parallel", "parallel", "arbitrary"),
    )(a, b)


def _mm_tn_slabs(a, b, nb, m, nblk, rel, width, tm=512, ts=512, name="mm_tn_slabs"):
    s = a.shape[0]
    n = b.shape[1] // nb
    ng, mi, ns, nw = n // nblk, m // tm, s // ts, len(rel)

    def body(a_ref, b_ref, o_ref, acc):
        t = pl.program_id(3)

        @pl.when(t == 0)
        def _():
            acc[...] = jnp.zeros_like(acc)

        acc[...] += _dot_tn(a_ref[...], b_ref[...])

        @pl.when(t == ns - 1)
        def _():
            for r, start in enumerate(rel):
                o_ref[r] = acc[:, start:start + width].astype(bf16)

    return pl.pallas_call(
        body, name=name, grid=(nb, ng, mi, ns),
        in_specs=[pl.BlockSpec((ts, tm), lambda k, g, i, t: (t, k * mi + i)), pl.BlockSpec((ts, nblk), lambda k, g, i, t: (t, k * ng + g))],
        out_specs=pl.BlockSpec((nw, None, tm, width), lambda k, g, i, t: (g, k, i, 0)), out_shape=SDS((ng * nw, nb, m, width), bf16),
        scratch_shapes=[pltpu.VMEM((tm, nblk), f32)], compiler_params=_cp("parallel", "parallel", "parallel", "arbitrary"),
    )(a, b)


def _merge_bwd(dx1b, wo, y, p, wb, tm=256):
    s = dx1b.shape[0]
    hw = D_MODEL // 2
    gblk = GATE_COL0 // hw

    def body(dx_ref, wo_ref, y_ref, gt_ref, wb_ref, dy_ref, dp_ref, dz_ref, dm_scr, dz_acc):
        k = pl.program_id(1)
        hf = k % 2

        @pl.when(k == 0)
        def _():
            dm = _dot_nt(dx_ref[...], wo_ref[...])
            dm_scr[0] = dm[:, :hw]
            dm_scr[1] = dm[:, hw:]

        dmh = dm_scr[hf]
        gate = _sigmoid(gt_ref[...])
        dy = (dmh * gate).astype(bf16)
        dy_ref[...] = dy
        dp_ref[...] = (dmh * y_ref[...].astype(f32) * gate * (1.0 - gate)).astype(bf16)
        part = _dot_nt(dy, wb_ref[...])

        @pl.when(hf == 0)
        def _():
            dz_acc[...] = part

        @pl.when(hf == 1)
        def _():
            dz_ref[...] = dz_acc[...] + part

    return pl.pallas_call(
        body, name="merge_bwd", grid=(s // tm, 6),
        in_specs=[pl.BlockSpec((tm, D_MODEL), lambda i, k: (i, 0)), pl.BlockSpec((D_MODEL, D_MODEL), lambda i, k: (0, 0)),
                  pl.BlockSpec((tm, hw), lambda i, k: (i, k)), pl.BlockSpec((tm, hw), lambda i, k: (i, gblk + k)),
                  pl.BlockSpec((None, BRANCH, hw), lambda i, k: (k // 2, 0, k % 2))],
        out_specs=[pl.BlockSpec((tm, hw), lambda i, k: (i, k)), pl.BlockSpec((tm, hw), lambda i, k: (i, gblk + k)),
                   pl.BlockSpec((tm, BRANCH), lambda i, k: (i, k // 2))],
        out_shape=[SDS((s, 3 * D_MODEL), bf16), SDS((s, N_COLS), bf16), SDS((s, 3 * BRANCH), f32)],
        scratch_shapes=[pltpu.VMEM((2, tm, hw), f32), pltpu.VMEM((tm, BRANCH), f32)], compiler_params=_cp("parallel", "arbitrary"),
    )(dx1b, wo, y, p, wb)


def _mixer_bwd(p, dz, opre, states, dp, lb, gout, wconv, lng, lnb, wsg, bsg_t):
    s = p.shape[0]
    tt = MIX_TILE
    nt = s // tt
    nch = tt // HGRN_CHUNK
    rev = lambda i: nt - 1 - i

    def body(q_ref, fp_ref, iv_ref, go_ref, bg_ref, cg_ref, xc_ref, u_ref, v_ref, cgp_ref, xcp_ref, dz_ref, opre_ref, st_ref,
             dp_in, lb_ref, gout_ref, wconv_ref, lng_ref, lnb_ref, wsg_ref, bsg_ref,
             dp_ref, vec_ref, dwsg_ref, dbsg_ref, dst_scr, zbuf, dybuf, dvn_scr, dbsg_acc):
        del dp_in
        i = pl.program_id(0)

        @pl.when(i == 0)
        def _():
            dst_scr[...] = jnp.zeros_like(dst_scr)
            dybuf[tt:tt + 8, :] = jnp.zeros((8, BRANCH), f32)
            vec_ref[...] = jnp.zeros_like(vec_ref)
            dwsg_ref[...] = jnp.zeros_like(dwsg_ref)
            dbsg_acc[...] = jnp.zeros_like(dbsg_acc)

        lbv = lb_ref[...]
        q_raw, fp = q_ref[...], fp_ref[...]
        sq = _sigmoid(q_raw)
        qs = q_raw * sq
        sfp = _sigmoid(fp)
        logf, snf, kk = _hgrn_gates(fp, lbv)
        inv_f = jnp.exp(-logf)
        iv = iv_ref[...]
        doa = dz_ref[:, 0:BRANCH]
        o = opre_ref[...]
        sgo = _sigmoid(go_ref[...])
        gout_v = gout_ref[...]
        d_o, dgo, dgout = [], [], []
        for h in range(HEADS):
            sl = slice(HEAD_DIM * h, HEAD_DIM * (h + 1))
            r, oh = _rms_stats(o[:, sl])
            d_on = doa[:, sl] * sgo[:, sl]
            dgo.append(doa[:, sl] * oh * gout_v[:, sl] * sgo[:, sl] * (1.0 - sgo[:, sl]))
            dx, dg = _rms_bwd(d_on, oh, r, gout_v[:, sl])
            d_o.append(dx)
            dgout.append(dg)
        d_o = jnp.concatenate(d_o, axis=1)
        dp_ref[:, 3 * BRANCH:4 * BRANCH] = jnp.concatenate(dgo, axis=1).astype(bf16)
        vec_ref[1:2, :] += jnp.concatenate(dgout, axis=1)

        causal = _tri(HGRN_CHUNK)
        tri = causal.astype(f32)
        tri_up = _tri(HGRN_CHUNK, upper=True).astype(f32)
        last_row = lax.broadcasted_iota(jnp.int32, (HGRN_CHUNK, 1), 0) == HGRN_CHUNK - 1
        lb_live = (lbv > LB_FLOOR).astype(f32)
        dlb = jnp.zeros((1, BRANCH), f32)
        for c in reversed(range(nch)):
            rows = slice(HGRN_CHUNK * c, HGRN_CHUNK * (c + 1))
            b = _dot_exact(tri, logf[rows])
            bl = jnp.sum(jnp.where(last_row, b, 0.0), axis=0, keepdims=True)
            eb, enb, edl, ebl = jnp.exp(b), jnp.exp(-b), jnp.exp(bl - b), jnp.exp(bl)
            qbf, kbf, kdf = qs[rows] * eb, kk[rows] * enb, kk[rows] * edl
            qb, kb, kd = qbf.astype(bf16), kbf.astype(bf16), kdf.astype(bf16)
            vc = iv[rows].astype(bf16)
            dob = d_o[rows].astype(bf16)
            dv, dqb, dkb, dkd, debl = [], [], [], [], []
            for h in range(HEADS):
                sl = slice(HEAD_DIM * h, HEAD_DIM * (h + 1))
                st = st_ref[c, h]
                dst = dst_scr[h]
                stb, dstb = st.astype(bf16), dst.astype(bf16)
                a = jnp.where(causal, _dot_nt(qb[:, sl], kb[:, sl]), 0.0).astype(bf16)
                da = jnp.where(causal, _dot_nt(dob[:, sl], vc[:, sl]), 0.0).astype(bf16)
                dv.append(_dot_tn(a, dob[:, sl]) + _dot_nt(kd[:, sl], dstb))
                dqb.append(_dot(dob[:, sl], stb) + _dot(da, kb[:, sl]))
                dkb.append(_dot_tn(da, qb[:, sl]))
                dkd.append(_dot(vc[:, sl], dstb))
                debl.append(jnp.sum(st * dst, axis=0, keepdims=True))
                dst_scr[h] = _dot_tn(dob[:, sl], qb[:, sl]) + dst * ebl[:, sl]
            dv, dqb, dkb, dkd = (jnp.concatenate(t, axis=1) for t in (dv, dqb, dkb, dkd))
            debl = jnp.concatenate(debl, axis=1)
            t_kd = dkd * kdf
            dbl = ebl * debl + jnp.sum(t_kd, axis=0, keepdims=True)
            db = dqb * qbf - dkb * kbf - t_kd + jnp.where(last_row, dbl, 0.0)
            dkk = dkb * enb + dkd * edl
            dlc = _dot_exact(tri_up, db)
            sq_c, q_c, sfp_c, snf_c, invf_c = sq[rows], q_raw[rows], sfp[rows], snf[rows], inv_f[rows]
            slope = (1.0 - lbv) * sfp_c * snf_c
            dp_ref[rows, 0:BRANCH] = (dqb * eb * sq_c * (1.0 + q_c * (1.0 - sq_c))).astype(bf16)
            dp_ref[rows, BRANCH:2 * BRANCH] = (slope * (dlc * invf_c - dkk)).astype(bf16)
            dp_ref[rows, 2 * BRANCH:3 * BRANCH] = dv.astype(bf16)
            dlb = dlb + jnp.sum(dlc * (lb_live - sfp_c) * invf_c - dkk * snf_c, axis=0, keepdims=True)
        vec_ref[0:1, :] += dlb

        dob_ = dz_ref[:, BRANCH:2 * BRANCH]
        bg, cg, xc = bg_ref[...], cg_ref[...], xc_ref[...]
        zc = cg * xc
        zbuf[0:8, :] = jnp.where(i < nt - 1, cgp_ref[...] * xcp_ref[...], 0.0)
        zbuf[8:8 + tt, :] = zc
        w0, w1, w2 = wconv_ref[0:1, :], wconv_ref[1:2, :], wconv_ref[2:3, :]
        y = w0 * zbuf[pl.ds(6, tt), :] + w1 * zbuf[pl.ds(7, tt), :] + w2 * zc
        dy = dob_ * bg
        dybuf[0:tt, :] = dy
        dy1, dy2 = dybuf[pl.ds(1, tt), :], dybuf[pl.ds(2, tt), :]
        dzc = w2 * dy + w1 * dy1 + w0 * dy2
        dp_ref[:, 4 * BRANCH:5 * BRANCH] = (dob_ * y).astype(bf16)
        dp_ref[:, 5 * BRANCH:6 * BRANCH] = (dzc * xc).astype(bf16)
        dp_ref[:, 6 * BRANCH:7 * BRANCH] = (dzc * cg).astype(bf16)
        vec_ref[4:5, :] += jnp.sum(zc * dy2, axis=0, keepdims=True)
        vec_ref[5:6, :] += jnp.sum(zc * dy1, axis=0, keepdims=True)
        vec_ref[6:7, :] += jnp.sum(zc * dy, axis=0, keepdims=True)
        dybuf[tt:tt + 8, :] = dybuf[0:8, :]

        doc = dz_ref[:, 2 * BRANCH:3 * BRANCH]
        u_raw, v_raw = u_ref[...], v_ref[...]
        ug = _gelu(u_raw)
        dug_scale = _gelu_grad(u_raw)
        vg = _gelu(v_raw)
        vcen = vg - jnp.mean(vg, axis=-1, keepdims=True)
        rstd = lax.rsqrt(jnp.mean(vcen * vcen, axis=-1, keepdims=True) + LN_EPS)
        vhat = vcen * rstd
        lng_v = lng_ref[...]
        vn = (vhat * lng_v + lnb_ref[...]).astype(bf16)
        low = _tri(SG_CHUNK)
        for g in range(SG_GROUPS):
            sl = slice(LANE * g, LANE * (g + 1))
            wm = jnp.where(low, wsg_ref[g], 0.0).astype(bf16)
            bias = bsg_ref[:, g:g + 1]
            dw = jnp.zeros((SG_CHUNK, SG_CHUNK), f32)
            dbs = jnp.zeros((SG_CHUNK, LANE), f32)
            for cc in range(tt // SG_CHUNK):
                rows = slice(SG_CHUNK * cc, SG_CHUNK * (cc + 1))
                vn_c = vn[rows, sl]
                sv = _dot(wm, vn_c) + bias
                doc_c = doc[rows, sl]
                dp_ref[rows, 7 * BRANCH + LANE * g:7 * BRANCH + LANE * (g + 1)] = (doc_c * sv * dug_scale[rows, sl]).astype(bf16)
                dsv = doc_c * ug[rows, sl]
                dsvb = dsv.astype(bf16)
                dbs = dbs + dsv
                dw = dw + _dot_nt(dsvb, vn_c)
                dvn_scr[rows, sl] = _dot_tn(wm, dsvb)
            dwsg_ref[g] += jnp.where(low, dw, 0.0)
            dbsg_acc[:, sl] += dbs
        dvn = dvn_scr[...]
        vec_ref[2:3, :] += jnp.sum(dvn * vhat, axis=0, keepdims=True)
        vec_ref[3:4, :] += jnp.sum(dvn, axis=0, keepdims=True)
        dvh = dvn * lng_v
        dvg = rstd * (dvh - jnp.mean(dvh, axis=-1, keepdims=True) - vhat * jnp.mean(dvh * vhat, axis=-1, keepdims=True))
        dp_ref[:, 8 * BRANCH:9 * BRANCH] = (dvg * _gelu_grad(v_raw)).astype(bf16)

        @pl.when(i == nt - 1)
        def _():
            for g in range(SG_GROUPS):
                dbsg_ref[:, g:g + 1] = jnp.sum(dbsg_acc[:, LANE * g:LANE * (g + 1)], axis=1, keepdims=True)

    full = lambda shape: pl.BlockSpec(shape, lambda i: (0,) * len(shape))
    tail = lambda c: pl.BlockSpec((8, BRANCH), lambda i: (jnp.maximum(rev(i) * (tt // 8) - 1, 0), c))
    return pl.pallas_call(
        body, name="mixer_bwd", grid=(nt,),
        in_specs=_p_specs(tt, range(9), rev) + [tail(5), tail(6), pl.BlockSpec((tt, 3 * BRANCH), lambda i: (rev(i), 0)),
                                                pl.BlockSpec((tt, BRANCH), lambda i: (rev(i), 0)),
                                                pl.BlockSpec((nch, HEADS, HEAD_DIM, HEAD_DIM), lambda i: (rev(i), 0, 0, 0)),
                                                pl.BlockSpec(memory_space=pl.ANY),
                                                full((1, BRANCH)), full((1, BRANCH)), full((3, BRANCH)), full((1, BRANCH)), full((1, BRANCH)),
                                                full((SG_GROUPS, SG_CHUNK, SG_CHUNK)), full((SG_CHUNK, SG_GROUPS))],
        out_specs=[pl.BlockSpec((tt, 9 * BRANCH), lambda i: (rev(i), 0)), full((8, BRANCH)), full((SG_GROUPS, SG_CHUNK, SG_CHUNK)),
                   full((SG_CHUNK, SG_GROUPS))],
        out_shape=[SDS((s, N_COLS), bf16), SDS((8, BRANCH), f32), SDS((SG_GROUPS, SG_CHUNK, SG_CHUNK), f32), SDS((SG_CHUNK, SG_GROUPS), f32)],
        scratch_shapes=[pltpu.VMEM((HEADS, HEAD_DIM, HEAD_DIM), f32), pltpu.VMEM((tt + 8, BRANCH), f32), pltpu.VMEM((tt + 8, BRANCH), f32),
                        pltpu.VMEM((tt, BRANCH), f32), pltpu.VMEM((SG_CHUNK, BRANCH), f32)],
        input_output_aliases={14: 0}, compiler_params=_cp("arbitrary"),
    )(*([p] * 11), dz, opre, states, dp, lb, gout, wconv, lng, lnb, wsg, bsg_t)


def _dh_bwd(dp, w_in, x, dx1, g, tm=512, tk=1536):
    s = x.shape[0]
    nk = N_COLS // tk

    def body(dp_ref, w_ref, x_ref, dx1_ref, g_ref, dx_ref, dxb_ref, dg_ref, acc):
        i, k = pl.program_id(0), pl.program_id(1)

        @pl.when(k == 0)
        def _():
            acc[...] = jnp.zeros_like(acc)

        acc[...] += _dot_nt(dp_ref[...], w_ref[...])

        @pl.when(k == nk - 1)
        def _():
            r, xh = _rms_stats(x_ref[...])
            dx, dg = _rms_bwd(acc[...], xh, r, g_ref[...])
            dx = dx + dx1_ref[...]
            dx_ref[...] = dx
            dxb_ref[...] = dx.astype(bf16)
            _acc_rows(dg_ref, i == 0, dg)

    row = pl.BlockSpec((tm, D_MODEL), lambda i, k: (i, 0))
    vec = pl.BlockSpec((1, D_MODEL), lambda i, k: (0, 0))
    return pl.pallas_call(
        body, name="dh_bwd", grid=(s // tm, nk),
        in_specs=[pl.BlockSpec((tm, tk), lambda i, k: (i, k)), pl.BlockSpec((D_MODEL, tk), lambda i, k: (0, k)), row, row, vec],
        out_specs=[row, row, vec], out_shape=[SDS((s, D_MODEL), f32), SDS((s, D_MODEL), bf16), SDS((1, D_MODEL), f32)],
        scratch_shapes=[pltpu.VMEM((tm, D_MODEL), f32)], compiler_params=_cp("arbitrary", "arbitrary"),
    )(dp, w_in, x, dx1, g)


def _layer_fwd(x, w, sm):
    p, h = _rms_mm(x, sm["g_mix"], w["w_in"])
    z, opre, states = _mixer_fwd(p, sm["lb"], sm["g_out"], sm["w_conv"], sm["ln_g"], sm["ln_b"], sm["w_sg"], sm["b_sg_t"])
    y, merged = _branch_gate(z, w["w_branch"], p)
    x1 = _mm_resid(x, merged, w["w_o"])
    x2, h2, ra = _ffn(x1, sm["g_ffn"], w["w_ff1"], w["w_ff2"])
    saved = dict(x=x, p=p, h=h, z=z, opre=opre, states=states, y=y, merged=merged, x1=x1, h2=h2, ra=ra)
    return x2, saved


def _layer_bwd(dx2, dx2b, sv, w, sm):
    nchip = N_DEV // 2
    by_chip = lambda g: g.reshape((nchip, 2) + g.shape[1:])
    da, act, dx1, dx1b, dg_ffn = _ffn_bwd(dx2, dx2b, sv["x1"], sm["g_ffn"], sv["ra"], w["w_ff1"], w["w_ff2"])
    g_ff2 = by_chip(_mm_tn(act, dx2b, 1, D_FF, D_MODEL, 512, 1024, name="dw_ff2")[0].reshape(N_DEV, D_FF // N_DEV, D_MODEL))
    g_ff1 = by_chip(_mm_tn_slabs(sv["h2"], da, 1, D_MODEL, D_FF // 2, [i * (D_FF // N_DEV) for i in range(nchip)], D_FF // N_DEV,
                                 name="dw_ff1")[:, 0])
    g_o = by_chip(_mm_tn(sv["merged"], dx1b, 1, D_MODEL, D_MODEL, 512, 1024, name="dw_o")[0].reshape(N_DEV, D_MODEL // N_DEV, D_MODEL))
    dy, dp, dz = _merge_bwd(dx1b, w["w_o"], sv["y"], sv["p"], w["w_branch"])
    g_branch = by_chip(_mm_tn_slabs(sv["z"], dy, 3, BRANCH, D_MODEL, [i * (D_MODEL // N_DEV) for i in range(N_DEV)], D_MODEL // N_DEV,
                                    name="dw_branch"))
    dp, vecs, dwsg, dbsg_t = _mixer_bwd(sv["p"], dz, sv["opre"], sv["states"], dp, sm["lb"], sm["g_out"], sm["w_conv"],
                                        sm["ln_g"], sm["ln_b"], sm["w_sg"], sm["b_sg_t"])
    starts = [SHARD_IN * i - (WIN - SHARD_IN) * (i % 2) for i in range(nchip)]
    g_in = by_chip(_mm_tn_slabs(sv["h"], dp, 1, D_MODEL, N_COLS // 2, starts, WIN, name="dw_in")[:, 0])
    dx, dxb, dg_mix = _dh_bwd(dp, w["w_in"], sv["x"], dx1, sm["g_mix"])
    big = dict(w_in=g_in, w_branch=g_branch, w_o=g_o, w_ff1=g_ff1, w_ff2=g_ff2)
    small = dict(g_mix=dg_mix, g_ffn=dg_ffn, vecs=vecs, w_sg=dwsg, b_sg_t=dbsg_t)
    return dx, dxb, big, small


BIG = ("w_in", "w_branch", "w_o", "w_ff1", "w_ff2")
ANY = pl.BlockSpec(memory_space=pl.ANY)


def _place():
    return lax.axis_index("x"), lax.axis_index("y"), lax.axis_index("c")


def _al(v, m):
    return pl.multiple_of(v * m, m)


def _shard_of(refs, dev):
    w_in, w_b, w_o, w_1, w_2 = refs
    nb, no, n1, n2 = w_b.shape[-1] // N_DEV, w_o.shape[0] // N_DEV, w_1.shape[-1] // N_DEV, w_2.shape[0] // N_DEV
    return [w_in, w_b.at[:, :, pl.ds(_al(dev, nb), nb)], w_o.at[pl.ds(_al(dev, no), no), :],
            w_1.at[:, pl.ds(_al(dev, n1), n1)], w_2.at[pl.ds(_al(dev, n2), n2), :]]


def _all_gather_layer(layer, shards):
    s_in, s_b, s_o, s_1, s_2 = (shards[n] for n in BIG)
    out_shape = [SDS((N_DEV,) + s_in.shape[1:], bf16), SDS(s_b.shape[1:3] + (s_b.shape[3] * N_DEV,), bf16),
                 SDS((s_o.shape[1] * N_DEV, s_o.shape[2]), bf16), SDS((s_1.shape[1], s_1.shape[2] * N_DEV), bf16),
                 SDS((s_2.shape[1] * N_DEV, s_2.shape[2]), bf16)]
    nt = len(BIG)

    def body(i_in, i_b, i_o, i_1, i_2, o_in, o_b, o_o, o_1, o_2, send_sems, recv_sems, local_sems):
        x, y, c = _place()
        me, sibling = (x, y, c), (x, y, 1 - c)
        chips = [(1 - x, y), (x, 1 - y), (1 - x, 1 - y)]
        mine = [r.at[layer] for r in (i_in, i_b, i_o, i_1, i_2)]

        def block(px, py, pc):
            dev = 4 * px + 2 * py + pc
            return [o_in.at[dev]] + _shard_of((None, o_b, o_o, o_1, o_2), dev)[1:]

        def copies(k, blk, to, src=None):
            dst = block(*blk)
            src = dst if src is None else src
            return [pltpu.make_async_remote_copy(src_ref=src[t], dst_ref=dst[t], send_sem=send_sems.at[k, t], recv_sem=recv_sems.at[k, t],
                                                 device_id=to, device_id_type=MESH) for t in range(nt)]

        local = [pltpu.make_async_copy(mine[t], block(*me)[t], local_sems.at[t]) for t in range(nt)]
        for cp in local:
            cp.start()
        first = copies(0, me, sibling, src=mine)
        for j, chip in enumerate(chips):
            first += copies(1 + j, me, (*chip, c), src=mine)
        for cp in first:
            cp.start()
        passed = [copies(4 + j, (*chip, c), sibling) for j, chip in enumerate(chips)]
        for j, chip in enumerate(chips):
            for cp in copies(1 + j, (*chip, c), me):
                cp.wait_recv()
            for cp in passed[j]:
                cp.start()
        for cp in copies(0, sibling, me):
            cp.wait_recv()
        for j, chip in enumerate(chips):
            for cp in copies(4 + j, (*chip, 1 - c), me):
                cp.wait_recv()
        for cp in first + [cp for grp in passed for cp in grp]:
            cp.wait_send()
        for cp in local:
            cp.wait()

    return pl.pallas_call(
        body, name=f"all_gather_l{layer}", in_specs=[ANY] * nt, out_specs=[ANY] * nt, out_shape=out_shape,
        scratch_shapes=[pltpu.SemaphoreType.DMA((7, nt)), pltpu.SemaphoreType.DMA((7, nt)), pltpu.SemaphoreType.DMA((nt,))],
    )(s_in, s_b, s_o, s_1, s_2)


def _merge_windows(land, tr=512):
    r = land.shape[1]
    tr = min(tr, r)
    chip_cols = 2 * SHARD_IN
    cut = WIN - LANE

    def body(e_ref, o_ref, out_ref):
        out_ref[:, 0:cut] = e_ref[:, 0:cut]
        out_ref[:, cut:WIN] = e_ref[:, cut:WIN] + o_ref[:, 0:LANE]
        out_ref[:, WIN:chip_cols] = o_ref[:, LANE:WIN]

    return pl.pallas_call(
        body, name="merge_windows", grid=(N_DEV // 2, r // tr),
        in_specs=[pl.BlockSpec((None, tr, WIN), lambda k, i: (2 * k, i, 0)), pl.BlockSpec((None, tr, WIN), lambda k, i: (2 * k + 1, i, 0))],
        out_specs=pl.BlockSpec((tr, chip_cols), lambda k, i: (i, k)), out_shape=SDS((r, N_COLS), bf16),
        compiler_params=_cp("parallel", "parallel"),
    )(land, land)


def _exchange_on_chip(grads):
    nt, nchip = len(BIG), N_DEV // 2

    def body(*refs):
        g, landed = refs[:nt], refs[nt:2 * nt]
        send_sems, recv_sems = refs[2 * nt:]
        x, y, c = _place()
        remote = [pltpu.make_async_remote_copy(src_ref=g[t].at[j, 1 - c], dst_ref=landed[t].at[j], send_sem=send_sems.at[j, t],
                                               recv_sem=recv_sems.at[j, t], device_id=(x, y, 1 - c), device_id_type=MESH)
                  for j in range(nchip) for t in range(nt)]
        for cp in remote:
            cp.start()
        for cp in remote:
            cp.wait_recv()
        for cp in remote:
            cp.wait_send()

    return pl.pallas_call(
        body, name="rs_on_chip", in_specs=[ANY] * nt, out_specs=[ANY] * nt,
        out_shape=[SDS((nchip,) + grads[n].shape[2:], bf16) for n in BIG],
        scratch_shapes=[pltpu.SemaphoreType.DMA((nchip, nt)), pltpu.SemaphoreType.DMA((nchip, nt))],
    )(*(grads[n] for n in BIG))


def _chip_sum(core, mine, other, tr=1024):
    nchip = mine.shape[0]
    cols = mine.shape[-1]
    m4 = mine.reshape(nchip, 2, -1, cols)
    o3 = other.reshape(nchip, -1, cols)
    rows = o3.shape[1]
    tr = min(tr, rows)
    while rows % tr:
        tr //= 2

    def body(c_ref, a_ref, b_ref, o_ref):
        del c_ref
        o_ref[...] = (a_ref[...].astype(f32) + b_ref[...].astype(f32)).astype(bf16)

    blk = pl.BlockSpec((None, tr, cols), lambda j, i, c_ref: (j, i, 0))
    out = pl.pallas_call(
        body, name="chip_sum", out_shape=SDS(o3.shape, bf16), compiler_params=_cp("parallel", "parallel"),
        grid_spec=pltpu.PrefetchScalarGridSpec(
            num_scalar_prefetch=1, grid=(nchip, rows // tr),
            in_specs=[pl.BlockSpec((None, None, tr, cols), lambda j, i, c_ref: (j, c_ref[0], i, 0)), blk], out_specs=blk),
    )(core, m4, o3)
    return out.reshape(other.shape)


def _exchange_between_chips(sums):
    nt = len(BIG)

    def body(*refs):
        sums_r, land_r = refs[:nt], refs[nt:2 * nt]
        send_sems, recv_sems = refs[2 * nt:]
        x, y, c = _place()
        chips = [(1 - x, y), (x, 1 - y), (1 - x, 1 - y)]
        remote = [pltpu.make_async_remote_copy(src_ref=sums_r[t].at[2 * cx + cy], dst_ref=land_r[t].at[k], send_sem=send_sems.at[k, t],
                                               recv_sem=recv_sems.at[k, t], device_id=(cx, cy, c), device_id_type=MESH)
                  for k, (cx, cy) in enumerate(chips) for t in range(nt)]
        for cp in remote:
            cp.start()
        for cp in remote:
            cp.wait_recv()
        for cp in remote:
            cp.wait_send()

    return pl.pallas_call(
        body, name="rs_between_chips", in_specs=[ANY] * nt, out_specs=[ANY] * nt,
        out_shape=[SDS((3,) + a.shape[1:], bf16) for a in sums],
        scratch_shapes=[pltpu.SemaphoreType.DMA((3, nt)), pltpu.SemaphoreType.DMA((3, nt))],
    )(*sums)


def _all_reduce_rows(pack):
    rows = pack.shape[0]
    blk = rows // N_DEV

    def body(in_ref, out_ref, land, send1, recv1, send2, recv2):
        x, y, c = _place()
        me = 4 * x + 2 * y + c
        others = [(px, py, pc) for px in range(2) for py in range(2) for pc in range(2)]

        def is_me(p):
            return jnp.logical_and(jnp.logical_and(p[0] == x, p[1] == y), p[2] == c)

        land[me] = in_ref[pl.ds(_al(me, blk), blk), :]
        for d, p in enumerate(others):
            @pl.when(jnp.logical_not(is_me(p)))
            def _():
                pltpu.make_async_remote_copy(src_ref=in_ref.at[pl.ds(d * blk, blk), :], dst_ref=land.at[me], send_sem=send1.at[d],
                                             recv_sem=recv1.at[me], device_id=p, device_id_type=MESH).start()
        for d, p in enumerate(others):
            @pl.when(jnp.logical_not(is_me(p)))
            def _():
                cp = pltpu.make_async_remote_copy(src_ref=in_ref.at[pl.ds(d * blk, blk), :], dst_ref=land.at[d], send_sem=send1.at[d],
                                                  recv_sem=recv1.at[d], device_id=p, device_id_type=MESH)
                cp.wait_recv()
                cp.wait_send()
        total = land[0]
        for d in range(1, N_DEV):
            total = total + land[d]
        out_ref[pl.ds(_al(me, blk), blk), :] = total
        for d, p in enumerate(others):
            @pl.when(jnp.logical_not(is_me(p)))
            def _():
                mine = out_ref.at[pl.ds(_al(me, blk), blk), :]
                pltpu.make_async_remote_copy(src_ref=mine, dst_ref=mine, send_sem=send2.at[d], recv_sem=recv2.at[me],
                                             device_id=p, device_id_type=MESH).start()
        for d, p in enumerate(others):
            @pl.when(jnp.logical_not(is_me(p)))
            def _():
                theirs = out_ref.at[pl.ds(d * blk, blk), :]
                cp = pltpu.make_async_remote_copy(src_ref=theirs, dst_ref=theirs, send_sem=send2.at[d], recv_sem=recv2.at[d],
                                                  device_id=p, device_id_type=MESH)
                cp.wait_recv()
                cp.wait_send()

    vm = pl.BlockSpec(memory_space=pltpu.VMEM)
    return pl.pallas_call(
        body, name="all_reduce_rows", in_specs=[vm], out_specs=vm, out_shape=SDS((rows, LANE), f32),
        scratch_shapes=[pltpu.VMEM((N_DEV, blk, LANE), f32)] + [pltpu.SemaphoreType.DMA((N_DEV,))] * 4,
        compiler_params=pltpu.CompilerParams(vmem_limit_bytes=VMEM_LIMIT),
    )(pack)


def _lower_bounds_fwd(lower):
    def body(l_ref, o_ref):
        sm = _layer_softmax(l_ref)
        run = jnp.zeros_like(sm[0])
        for l in range(DEPTH):
            o_ref[l:l + 1, :] = run
            if l + 1 < DEPTH:
                run = run + sm[l + 1]

    return pl.pallas_call(body, name="lower_bounds_fwd", out_shape=SDS(lower.shape, f32))(lower)


def _layer_softmax(l_ref):
    rows = [l_ref[l:l + 1, :] for l in range(DEPTH)]
    top = functools.reduce(jnp.maximum, rows)
    e = [jnp.exp(r - top) for r in rows]
    tot = functools.reduce(lambda a, b: a + b, e)
    return [v / tot for v in e]


def _lower_bounds_bwd(lower, dlbs):
    def body(l_ref, d_ref, o_ref):
        sm = _layer_softmax(l_ref)
        dsm = [None] * DEPTH
        run = jnp.zeros_like(sm[0])
        dsm[0] = run
        for l in reversed(range(1, DEPTH)):
            run = run + d_ref[l:l + 1, :]
            dsm[l] = run
        inner = functools.reduce(lambda a, b: a + b, [sm[l] * dsm[l] for l in range(DEPTH)])
        for l in range(DEPTH):
            o_ref[l:l + 1, :] = sm[l] * (dsm[l] - inner)

    return pl.pallas_call(body, name="lower_bounds_bwd", out_shape=SDS(lower.shape, f32))(lower, dlbs)


_ADAM_C1 = 1.0 - ADAM_B1 ** ADAM_STEP
_ADAM_C2 = 1.0 - ADAM_B2 ** ADAM_STEP


def _adamw(w, g, m, v):
    m = ADAM_B1 * m + (1.0 - ADAM_B1) * g
    v = ADAM_B2 * v + (1.0 - ADAM_B2) * (g * g)
    delta = -ADAM_LR * ((m / _ADAM_C1) / (jnp.sqrt(v / _ADAM_C2) + ADAM_EPS) + ADAM_WD * w)
    return delta, m, v


def _adam_big(where, w, m, v, sums, landed, outs, windowed=False, tr=512):
    shape = w.shape
    cols = shape[-1]
    gcols = sums.shape[-1]
    w3, m3, v3 = (a.reshape(DEPTH, -1, cols) for a in (w, m, v))
    outs3 = [a.reshape(DEPTH, -1, cols) for a in outs]
    sums3 = sums.reshape(sums.shape[0], -1, gcols)
    land3 = landed.reshape(3, -1, gcols)
    rows = w3.shape[1]
    tr = min(tr, rows)

    def body(where_ref, w_ref, m_ref, v_ref, sum_ref, land_ref, *rest):
        g_ref, d_ref, nm_ref, nv_ref = rest[4:]
        g = sum_ref[...].astype(f32)
        for k in range(3):
            g = g + land_ref[k].astype(f32)
        if windowed:
            g = pltpu.roll(g, SHARD_IN * where_ref[2], 1)[:, :cols]
        delta, nm, nv = _adamw(w_ref[...], g, m_ref[...], v_ref[...])
        g_ref[...] = g
        d_ref[...] = delta
        nm_ref[...] = nm
        nv_ref[...] = nv

    blk = pl.BlockSpec((None, tr, cols), lambda i, wh: (wh[0], i, 0))
    res = pl.pallas_call(
        body, name="adam_big", out_shape=[SDS(w3.shape, f32)] * 4, input_output_aliases={6 + i: i for i in range(4)},
        compiler_params=_cp("parallel"),
        grid_spec=pltpu.PrefetchScalarGridSpec(
            num_scalar_prefetch=1, grid=(rows // tr,),
            in_specs=[blk, blk, blk, pl.BlockSpec((None, tr, gcols), lambda i, wh: (wh[1], i, 0)),
                      pl.BlockSpec((3, tr, gcols), lambda i, wh: (0, i, 0))] + [ANY] * 4,
            out_specs=[blk] * 4),
    )(where, w3, m3, v3, sums3, land3, *outs3)
    return [o.reshape(shape) for o in res]


def _adam_rows(w, g, m, v):
    def body(w_ref, g_ref, m_ref, v_ref, d_ref, nm_ref, nv_ref):
        delta, nm, nv = _adamw(w_ref[...], g_ref[...], m_ref[...], v_ref[...])
        d_ref[...] = delta
        nm_ref[...] = nm
        nv_ref[...] = nv

    return pl.pallas_call(body, name="adam_rows", out_shape=[SDS(w.shape, f32)] * 3)(w, g, m, v)


SMALL = ("g_mix", "lower_bounds", "g_hgrn_out", "w_conv", "sg_ln_g", "sg_ln_b", "w_sg", "b_sg", "g_ffn", "g_final")
WEIGHTS = ("w_in", "g_mix", "lower_bounds", "g_hgrn_out", "w_conv", "sg_ln_g", "sg_ln_b", "w_sg", "b_sg", "w_branch", "w_o", "g_ffn",
           "w_ff1", "w_ff2", "g_final")


def _pack_rows(arrays, multiple):
    flat = jnp.concatenate([a.reshape(-1) for a in arrays])
    rows = -(-flat.shape[0] // (LANE * multiple)) * multiple
    return jnp.pad(flat, (0, rows * LANE - flat.shape[0])).reshape(rows, LANE)


def _unpack_rows(pack, like):
    flat = pack.reshape(-1)
    out, at = [], 0
    for a in like:
        out.append(flat[at:at + a.size].reshape(a.shape))
        at += a.size
    return out


def kernel(x, w_in, g_mix, lower_bounds, g_hgrn_out, w_conv, sg_ln_g, sg_ln_b, w_sg, b_sg, w_branch, w_o, g_ffn, w_ff1, w_ff2, g_final, loss_target, m_w_in, m_g_mix, m_lower_bounds, m_g_hgrn_out, m_w_conv, m_sg_ln_g, m_sg_ln_b, m_w_sg, m_b_sg, m_w_branch, m_w_o, m_g_ffn, m_w_ff1, m_w_ff2, m_g_final, v_w_in, v_g_mix, v_lower_bounds, v_g_hgrn_out, v_w_conv, v_sg_ln_g, v_sg_ln_b, v_w_sg, v_b_sg, v_w_branch, v_w_o, v_g_ffn, v_w_ff1, v_w_ff2, v_g_final):
    weights = dict(w_in=w_in, g_mix=g_mix, lower_bounds=lower_bounds, g_hgrn_out=g_hgrn_out, w_conv=w_conv, sg_ln_g=sg_ln_g,
                   sg_ln_b=sg_ln_b, w_sg=w_sg, b_sg=b_sg, w_branch=w_branch, w_o=w_o, g_ffn=g_ffn, w_ff1=w_ff1, w_ff2=w_ff2, g_final=g_final)
    mom1 = dict(w_in=m_w_in, g_mix=m_g_mix, lower_bounds=m_lower_bounds, g_hgrn_out=m_g_hgrn_out, w_conv=m_w_conv, sg_ln_g=m_sg_ln_g,
                sg_ln_b=m_sg_ln_b, w_sg=m_w_sg, b_sg=m_b_sg, w_branch=m_w_branch, w_o=m_w_o, g_ffn=m_g_ffn, w_ff1=m_w_ff1, w_ff2=m_w_ff2,
                g_final=m_g_final)
    mom2 = dict(w_in=v_w_in, g_mix=v_g_mix, lower_bounds=v_lower_bounds, g_hgrn_out=v_g_hgrn_out, w_conv=v_w_conv, sg_ln_g=v_sg_ln_g,
                sg_ln_b=v_sg_ln_b, w_sg=v_w_sg, b_sg=v_b_sg, w_branch=v_w_branch, w_o=v_w_o, g_ffn=v_g_ffn, w_ff1=v_w_ff1, w_ff2=v_w_ff2,
                g_final=v_g_final)
    xi, yi, ci = _place()
    dev = 4 * xi + 2 * yi + ci
    conv_cols = w_conv.shape[-1]

    w_in_b = w_in.astype(bf16)
    pad = WIN - SHARD_IN
    w_in_win = jnp.where(ci == 0, jnp.pad(w_in_b, ((0, 0), (0, 0), (0, pad))), jnp.pad(w_in_b, ((0, 0), (0, 0), (pad, 0))))
    shards = dict(w_in=w_in_win, w_branch=w_branch.astype(bf16), w_o=w_o.astype(bf16), w_ff1=w_ff1.astype(bf16), w_ff2=w_ff2.astype(bf16))

    conv_place = lax.dynamic_update_slice(jnp.zeros((DEPTH, 3, BRANCH), f32), w_conv, (0, 0, dev * conv_cols))
    (w_conv_full,) = _unpack_rows(_all_reduce_rows(_pack_rows([conv_place], 8 * N_DEV)), [conv_place])
    lbs = _lower_bounds_fwd(lower_bounds)

    def small_of(l):
        return dict(g_mix=g_mix[l][None], lb=lbs[l][None], g_out=g_hgrn_out[l][None], w_conv=w_conv_full[l], ln_g=sg_ln_g[l][None],
                    ln_b=sg_ln_b[l][None], w_sg=w_sg[l], b_sg_t=b_sg[l].T, g_ffn=g_ffn[l][None])

    act = x[0]
    full, saved = [], []
    for l in range(DEPTH):
        land, f_b, f_o, f_1, f_2 = _all_gather_layer(l, shards)
        full.append(dict(w_in=_merge_windows(land), w_branch=f_b, w_o=f_o, w_ff1=f_1, w_ff2=f_2))
        act, sv = _layer_fwd(act, full[l], small_of(l))
        saved.append(sv)
    loss_row, dx, dxb, dg_final = _final(act, loss_target[0], g_final[None])
    loss = lax.psum(loss_row[0, 0], ("x", "y", "c"))

    core = ci.astype(jnp.int32)[None]
    big_out = {n: [lax.empty(weights[n].shape, f32) for _ in range(4)] for n in BIG}
    small_grads = [None] * DEPTH
    for l in reversed(range(DEPTH)):
        dx, dxb, big, small_grads[l] = _layer_bwd(dx, dxb, saved[l], full[l], small_of(l))
        received = _exchange_on_chip(big)
        sums = [_chip_sum(core, big[n], r) for n, r in zip(BIG, received)]
        landed = _exchange_between_chips(sums)
        where = jnp.stack([jnp.int32(l), (2 * xi + yi).astype(jnp.int32), ci.astype(jnp.int32)])
        for t, n in enumerate(BIG):
            big_out[n] = _adam_big(where, weights[n], mom1[n], mom2[n], sums[t], landed[t], big_out[n], windowed=(n == "w_in"))

    stack = lambda f: jnp.stack([f(small_grads[l]) for l in range(DEPTH)])
    d_lower = _lower_bounds_bwd(lower_bounds, stack(lambda s: s["vecs"][0]))
    local_small = dict(g_mix=stack(lambda s: s["g_mix"][0]), lower_bounds=d_lower, g_hgrn_out=stack(lambda s: s["vecs"][1]),
                       w_conv=stack(lambda s: s["vecs"][4:7]), sg_ln_g=stack(lambda s: s["vecs"][2]), sg_ln_b=stack(lambda s: s["vecs"][3]),
                       w_sg=stack(lambda s: s["w_sg"]), b_sg=stack(lambda s: s["b_sg_t"].T), g_ffn=stack(lambda s: s["g_ffn"][0]),
                       g_final=dg_final[0])
    order = [local_small[n] for n in SMALL]
    grads = dict(zip(SMALL, _unpack_rows(_all_reduce_rows(_pack_rows(order, 8 * N_DEV)), order)))
    grads["w_conv"] = lax.dynamic_slice(grads["w_conv"], (0, 0, dev * conv_cols), (DEPTH, 3, conv_cols))

    deltas, new_m, new_v = {}, {}, {}
    for n in BIG:
        grads[n], deltas[n], new_m[n], new_v[n] = big_out[n]
    packs = [_pack_rows([d[n] for n in SMALL], 8) for d in (weights, grads, mom1, mom2)]
    like = [weights[n] for n in SMALL]
    for out, pack in zip((deltas, new_m, new_v), _adam_rows(*packs)):
        out.update(zip(SMALL, _unpack_rows(pack, like)))

    return (loss, dx[None], *[grads[n] for n in WEIGHTS], *[deltas[n] for n in WEIGHTS], *[new_m[n] for n in WEIGHTS],
            *[new_v[n] for n in WEIGHTS])
```

```python
import functools

import jax
import jax.numpy as jnp
from jax import lax
from jax.experimental import pallas as pl
from jax.experimental.pallas import tpu as pltpu
from jax.experimental.pallas import tpu_sc as plsc

f32 = jnp.float32
bf16 = jnp.bfloat16
SDS = jax.ShapeDtypeStruct
MESH = pl.DeviceIdType.MESH

D_MODEL = 1024
BRANCH = 512
N_COLS = 7680
D_FF = 4096
DEPTH = 4
HEADS = 4
HEAD_DIM = 128
HGRN_CHUNK = 64
SG_CHUNK = 128
SG_GROUPS = 4
NORM_EPS = 1e-6
LN_EPS = 1e-5
LB_FLOOR = 1e-30
N_DEV = 8
SHARD_IN = N_COLS // N_DEV
WIN = 1024
LANE = 128
GATE_COL0 = 9 * BRANCH

ADAM_LR = 0.001
ADAM_B1 = 0.9
ADAM_B2 = 0.999
ADAM_EPS = 1e-08
ADAM_WD = 0.01
ADAM_STEP = 10

MIX_TILE = 256
VMEM_LIMIT = 56 * 1024 * 1024


def _cp(*sem):
    return pltpu.CompilerParams(dimension_semantics=sem or None, vmem_limit_bytes=VMEM_LIMIT)


def _dot(a, b):
    return jnp.dot(a, b, preferred_element_type=f32)


def _dot_nt(a, b):
    return lax.dot_general(a, b, (((1,), (1,)), ((), ())), preferred_element_type=f32)


def _dot_tn(a, b):
    return lax.dot_general(a, b, (((0,), (0,)), ((), ())), preferred_element_type=f32)


def _dot_exact(a, b):
    return jnp.dot(a, b, precision=lax.Precision.HIGHEST, preferred_element_type=f32)


def _sigmoid(x):
    return jax.nn.sigmoid(x)


_GELU_C = 0.7978845608028654
_GELU_A = 0.044715


def _gelu(x):
    return 0.5 * x * (1.0 + jnp.tanh(_GELU_C * (x + _GELU_A * x * x * x)))


def _gelu_grad(x):
    x2 = x * x
    t = jnp.tanh(_GELU_C * (x + _GELU_A * x * x2))
    return 0.5 * (1.0 + t) + 0.5 * x * (1.0 - t * t) * _GELU_C * (1.0 + 3.0 * _GELU_A * x2)


def _rms_stats(x):
    r = lax.rsqrt(jnp.mean(x * x, axis=-1, keepdims=True) + NORM_EPS)
    return r, x * r


def _rms_bwd(dh, xh, r, g):
    dg = jnp.sum(dh * xh, axis=0, keepdims=True)
    dxn = dh * g
    dx = r * (dxn - xh * jnp.mean(dxn * xh, axis=-1, keepdims=True))
    return dx, dg


def _tri(n, upper=False):
    r = lax.broadcasted_iota(jnp.int32, (n, n), 0)
    c = lax.broadcasted_iota(jnp.int32, (n, n), 1)
    return (c >= r) if upper else (c <= r)


def _acc_rows(ref, first, val):
    @pl.when(first)
    def _():
        ref[...] = val

    @pl.when(jnp.logical_not(first))
    def _():
        ref[...] += val


def _rms_mm(x, g, w, tm=512, tn=1920):
    s, n = x.shape[0], w.shape[1]

    def body(x_ref, g_ref, w_ref, p_ref, h_ref, hs):
        @pl.when(pl.program_id(1) == 0)
        def _():
            _, xh = _rms_stats(x_ref[...])
            hv = (xh * g_ref[...]).astype(bf16)
            hs[...] = hv
            h_ref[...] = hv

        p_ref[...] = _dot(hs[...], w_ref[...])

    return pl.pallas_call(
        body, name="rms_mm", grid=(s // tm, n // tn),
        in_specs=[pl.BlockSpec((tm, D_MODEL), lambda i, j: (i, 0)), pl.BlockSpec((1, D_MODEL), lambda i, j: (0, 0)),
                  pl.BlockSpec((D_MODEL, tn), lambda i, j: (0, j))],
        out_specs=[pl.BlockSpec((tm, tn), lambda i, j: (i, j)), pl.BlockSpec((tm, D_MODEL), lambda i, j: (i, 0))],
        out_shape=[SDS((s, n), f32), SDS((s, D_MODEL), bf16)],
        scratch_shapes=[pltpu.VMEM((tm, D_MODEL), bf16)], compiler_params=_cp("parallel", "arbitrary"),
    )(x, g, w)


def _hgrn_gates(fp, lb):
    logf = jnp.logaddexp(jnp.log(jnp.maximum(lb, LB_FLOOR)), jnp.log1p(-lb) + jax.nn.log_sigmoid(fp))
    snf = _sigmoid(-fp)
    return logf, snf, (1.0 - lb) * snf


def _p_specs(tile, cols, row_map):
    return [pl.BlockSpec((tile, BRANCH), functools.partial(lambda c, i: (row_map(i), c), c)) for c in cols]


def _mixer_fwd(p, lb, gout, wconv, lng, lnb, wsg, bsg_t):
    s = p.shape[0]
    tt = MIX_TILE
    nch = tt // HGRN_CHUNK

    def body(q_ref, fp_ref, iv_ref, go_ref, bg_ref, cg_ref, xc_ref, u_ref, v_ref, lb_ref, gout_ref, wconv_ref, lng_ref,
             lnb_ref, wsg_ref, bsg_ref, z_ref, opre_ref, st_ref, st_scr, zbuf):
        @pl.when(pl.program_id(0) == 0)
        def _():
            st_scr[...] = jnp.zeros_like(st_scr)
            zbuf[0:8, :] = jnp.zeros((8, BRANCH), f32)

        lbv = lb_ref[...]
        q_raw = q_ref[...]
        qs = q_raw * _sigmoid(q_raw)
        logf, _, kk = _hgrn_gates(fp_ref[...], lbv)
        iv = iv_ref[...]
        causal = _tri(HGRN_CHUNK)
        tri = causal.astype(f32)
        last_row = lax.broadcasted_iota(jnp.int32, (HGRN_CHUNK, 1), 0) == HGRN_CHUNK - 1
        for c in range(nch):
            rows = slice(HGRN_CHUNK * c, HGRN_CHUNK * (c + 1))
            b = _dot_exact(tri, logf[rows])
            bl = jnp.sum(jnp.where(last_row, b, 0.0), axis=0, keepdims=True)
            qb = (qs[rows] * jnp.exp(b)).astype(bf16)
            kb = (kk[rows] * jnp.exp(-b)).astype(bf16)
            kd = (kk[rows] * jnp.exp(bl - b)).astype(bf16)
            ebl = jnp.exp(bl)
            vc = iv[rows].astype(bf16)
            for h in range(HEADS):
                sl = slice(HEAD_DIM * h, HEAD_DIM * (h + 1))
                st = st_scr[h]
                st_ref[c, h] = st
                a = jnp.where(causal, _dot_nt(qb[:, sl], kb[:, sl]), 0.0)
                opre_ref[rows, sl] = _dot(a.astype(bf16), vc[:, sl]) + _dot_nt(qb[:, sl], st.astype(bf16))
                st_scr[h] = st * ebl[:, sl] + _dot_tn(vc[:, sl], kd[:, sl])

        o = opre_ref[...]
        go = go_ref[...]
        gout_v = gout_ref[...]
        for h in range(HEADS):
            sl = slice(HEAD_DIM * h, HEAD_DIM * (h + 1))
            _, oh = _rms_stats(o[:, sl])
            z_ref[:, sl] = (oh * gout_v[:, sl] * _sigmoid(go[:, sl])).astype(bf16)

        zc = cg_ref[...] * xc_ref[...]
        zbuf[8:8 + tt, :] = zc
        y = wconv_ref[0:1, :] * zbuf[pl.ds(6, tt), :] + wconv_ref[1:2, :] * zbuf[pl.ds(7, tt), :] + wconv_ref[2:3, :] * zc
        z_ref[:, BRANCH:2 * BRANCH] = (bg_ref[...] * y).astype(bf16)
        zbuf[0:8, :] = zbuf[tt:tt + 8, :]

        ug = _gelu(u_ref[...])
        vg = _gelu(v_ref[...])
        vcen = vg - jnp.mean(vg, axis=-1, keepdims=True)
        rstd = lax.rsqrt(jnp.mean(vcen * vcen, axis=-1, keepdims=True) + LN_EPS)
        vn = (vcen * rstd * lng_ref[...] + lnb_ref[...]).astype(bf16)
        low = _tri(SG_CHUNK)
        for g in range(SG_GROUPS):
            sl = slice(LANE * g, LANE * (g + 1))
            wm = jnp.where(low, wsg_ref[g], 0.0).astype(bf16)
            bias = bsg_ref[:, g:g + 1]
            for cc in range(tt // SG_CHUNK):
                rows = slice(SG_CHUNK * cc, SG_CHUNK * (cc + 1))
                sv = _dot(wm, vn[rows, sl]) + bias
                z_ref[rows, 2 * BRANCH + LANE * g:2 * BRANCH + LANE * (g + 1)] = (ug[rows, sl] * sv).astype(bf16)

    full = lambda shape: pl.BlockSpec(shape, lambda i: (0,) * len(shape))
    return pl.pallas_call(
        body, name="mixer_fwd", grid=(s // tt,),
        in_specs=_p_specs(tt, range(9), lambda i: i) + [full((1, BRANCH)), full((1, BRANCH)), full((3, BRANCH)), full((1, BRANCH)),
                                                        full((1, BRANCH)), full((SG_GROUPS, SG_CHUNK, SG_CHUNK)), full((SG_CHUNK, SG_GROUPS))],
        out_specs=[pl.BlockSpec((tt, 3 * BRANCH), lambda i: (i, 0)), pl.BlockSpec((tt, BRANCH), lambda i: (i, 0)),
                   pl.BlockSpec((nch, HEADS, HEAD_DIM, HEAD_DIM), lambda i: (i, 0, 0, 0))],
        out_shape=[SDS((s, 3 * BRANCH), bf16), SDS((s, BRANCH), f32), SDS((s // HGRN_CHUNK, HEADS, HEAD_DIM, HEAD_DIM), f32)],
        scratch_shapes=[pltpu.VMEM((HEADS, HEAD_DIM, HEAD_DIM), f32), pltpu.VMEM((tt + 8, BRANCH), f32)],
        compiler_params=_cp("arbitrary"),
    )(*([p] * 9), lb, gout, wconv, lng, lnb, wsg, bsg_t)


def _branch_gate(z, wb, p, tm=256):
    s = z.shape[0]
    half = 3 * D_MODEL // 2

    def body(z_ref, wb_ref, ga_ref, gb_ref, y_ref, m_ref):
        ga, gb = ga_ref[...], gb_ref[...]
        gates = [ga[:, :D_MODEL], jnp.concatenate([ga[:, D_MODEL:], gb[:, :D_MODEL // 2]], axis=1), gb[:, D_MODEL // 2:]]
        acc = None
        for n in range(3):
            yn = _dot(z_ref[:, BRANCH * n:BRANCH * (n + 1)], wb_ref[n])
            y_ref[:, D_MODEL * n:D_MODEL * (n + 1)] = yn.astype(bf16)
            t = _sigmoid(gates[n]) * yn
            acc = t if acc is None else acc + t
        m_ref[...] = acc.astype(bf16)

    blk0 = GATE_COL0 // half
    return pl.pallas_call(
        body, name="branch_gate", grid=(s // tm,),
        in_specs=[pl.BlockSpec((tm, 3 * BRANCH), lambda i: (i, 0)), pl.BlockSpec((3, BRANCH, D_MODEL), lambda i: (0, 0, 0)),
                  pl.BlockSpec((tm, half), lambda i: (i, blk0)), pl.BlockSpec((tm, half), lambda i: (i, blk0 + 1))],
        out_specs=[pl.BlockSpec((tm, 3 * D_MODEL), lambda i: (i, 0)), pl.BlockSpec((tm, D_MODEL), lambda i: (i, 0))],
        out_shape=[SDS((s, 3 * D_MODEL), bf16), SDS((s, D_MODEL), bf16)], compiler_params=_cp("parallel"),
    )(z, wb, p, p)


def _mm_resid(x, m, wo, tm=512):
    s = x.shape[0]

    def body(x_ref, m_ref, w_ref, o_ref):
        o_ref[...] = x_ref[...] + _dot(m_ref[...], w_ref[...])

    return pl.pallas_call(
        body, name="mm_resid", grid=(s // tm,),
        in_specs=[pl.BlockSpec((tm, D_MODEL), lambda i: (i, 0)), pl.BlockSpec((tm, D_MODEL), lambda i: (i, 0)),
                  pl.BlockSpec((D_MODEL, D_MODEL), lambda i: (0, 0))],
        out_specs=pl.BlockSpec((tm, D_MODEL), lambda i: (i, 0)), out_shape=SDS((s, D_MODEL), f32), compiler_params=_cp("parallel"),
    )(x, m, wo)


def _ffn(x1, g, w1, w2, tm=512, tf=1024):
    s = x1.shape[0]
    nf = D_FF // tf

    def body(x_ref, g_ref, w1_ref, w2_ref, o_ref, h_ref, ra_ref, hs, acc):
        f = pl.program_id(1)

        @pl.when(f == 0)
        def _():
            _, xh = _rms_stats(x_ref[...])
            hv = (xh * g_ref[...]).astype(bf16)
            hs[...] = hv
            h_ref[...] = hv
            acc[...] = jnp.zeros_like(acc)

        ra = jnp.maximum(_dot(hs[...], w1_ref[...]), 0.0)
        ra_ref[...] = ra.astype(bf16)
        acc[...] += _dot((ra * ra).astype(bf16), w2_ref[...])

        @pl.when(f == nf - 1)
        def _():
            o_ref[...] = x_ref[...] + acc[...]

    return pl.pallas_call(
        body, name="ffn", grid=(s // tm, nf),
        in_specs=[pl.BlockSpec((tm, D_MODEL), lambda i, f: (i, 0)), pl.BlockSpec((1, D_MODEL), lambda i, f: (0, 0)),
                  pl.BlockSpec((D_MODEL, tf), lambda i, f: (0, f)), pl.BlockSpec((tf, D_MODEL), lambda i, f: (f, 0))],
        out_specs=[pl.BlockSpec((tm, D_MODEL), lambda i, f: (i, 0)), pl.BlockSpec((tm, D_MODEL), lambda i, f: (i, 0)),
                   pl.BlockSpec((tm, tf), lambda i, f: (i, f))],
        out_shape=[SDS((s, D_MODEL), f32), SDS((s, D_MODEL), bf16), SDS((s, D_FF), bf16)],
        scratch_shapes=[pltpu.VMEM((tm, D_MODEL), bf16), pltpu.VMEM((tm, D_MODEL), f32)], compiler_params=_cp("parallel", "arbitrary"),
    )(x1, g, w1, w2)


def _final(x, target, g, tm=512):
    s = x.shape[0]

    def body(x_ref, t_ref, g_ref, loss_ref, dx_ref, dxb_ref, dg_ref):
        first = pl.program_id(0) == 0
        gv = g_ref[...]
        r, xh = _rms_stats(x_ref[...])
        e = xh * gv - t_ref[...]
        tile_loss = 0.5 * jnp.sum(jnp.mean(e * e, axis=-1, keepdims=True), axis=0, keepdims=True)
        dx, dg = _rms_bwd(e * (1.0 / D_MODEL), xh, r, gv)
        dx_ref[...] = dx
        dxb_ref[...] = dx.astype(bf16)
        _acc_rows(dg_ref, first, dg)
        _acc_rows(loss_ref, first, jnp.broadcast_to(tile_loss, (1, LANE)))

    row = pl.BlockSpec((tm, D_MODEL), lambda i: (i, 0))
    return pl.pallas_call(
        body, name="final_loss", grid=(s // tm,), in_specs=[row, row, pl.BlockSpec((1, D_MODEL), lambda i: (0, 0))],
        out_specs=[pl.BlockSpec((1, LANE), lambda i: (0, 0)), row, row, pl.BlockSpec((1, D_MODEL), lambda i: (0, 0))],
        out_shape=[SDS((1, LANE), f32), SDS((s, D_MODEL), f32), SDS((s, D_MODEL), bf16), SDS((1, D_MODEL), f32)],
        compiler_params=_cp("arbitrary"),
    )(x, target, g)


def _ffn_bwd(dx2, dx2b, x1, g, ra, w1, w2, tm=512, tf=1024):
    s = x1.shape[0]
    nf = D_FF // tf

    def body(dx_ref, dxb_ref, x_ref, g_ref, ra_ref, w1_ref, w2_ref, da_ref, act_ref, dx1_ref, dx1b_ref, dg_ref, acc):
        i, f = pl.program_id(0), pl.program_id(1)

        @pl.when(f == 0)
        def _():
            acc[...] = jnp.zeros_like(acc)

        rav = ra_ref[...].astype(f32)
        da = (_dot_nt(dxb_ref[...], w2_ref[...]) * (2.0 * rav)).astype(bf16)
        da_ref[...] = da
        act_ref[...] = (rav * rav).astype(bf16)
        acc[...] += _dot_nt(da, w1_ref[...])

        @pl.when(f == nf - 1)
        def _():
            r, xh = _rms_stats(x_ref[...])
            dx, dg = _rms_bwd(acc[...], xh, r, g_ref[...])
            dx = dx + dx_ref[...]
            dx1_ref[...] = dx
            dx1b_ref[...] = dx.astype(bf16)
            _acc_rows(dg_ref, i == 0, dg)

    row = pl.BlockSpec((tm, D_MODEL), lambda i, f: (i, 0))
    col = pl.BlockSpec((tm, tf), lambda i, f: (i, f))
    return pl.pallas_call(
        body, name="ffn_bwd", grid=(s // tm, nf),
        in_specs=[row, row, row, pl.BlockSpec((1, D_MODEL), lambda i, f: (0, 0)), col,
                  pl.BlockSpec((D_MODEL, tf), lambda i, f: (0, f)), pl.BlockSpec((tf, D_MODEL), lambda i, f: (f, 0))],
        out_specs=[col, col, row, row, pl.BlockSpec((1, D_MODEL), lambda i, f: (0, 0))],
        out_shape=[SDS((s, D_FF), bf16), SDS((s, D_FF), bf16), SDS((s, D_MODEL), f32), SDS((s, D_MODEL), bf16), SDS((1, D_MODEL), f32)],
        scratch_shapes=[pltpu.VMEM((tm, D_MODEL), f32)], compiler_params=_cp("arbitrary", "arbitrary"),
    )(dx2, dx2b, x1, g, ra, w1, w2)


def _mm_tn(a, b, nb, m, n, tm, tn, ts=512, name="mm_tn"):
    s = a.shape[0]
    mi, nj, ns = m // tm, n // tn, s // ts

    def body(a_ref, b_ref, o_ref, acc):
        t = pl.program_id(3)

        @pl.when(t == 0)
        def _():
            acc[...] = jnp.zeros_like(acc)

        acc[...] += _dot_tn(a_ref[...], b_ref[...])

        @pl.when(t == ns - 1)
        def _():
            o_ref[...] = acc[...].astype(bf16)

    return pl.pallas_call(
        body, name=name, grid=(nb, mi, nj, ns),
        in_specs=[pl.BlockSpec((ts, tm), lambda k, i, j, t: (t, k * mi + i)), pl.BlockSpec((ts, tn), lambda k, i, j, t: (t, k * nj + j))],
        out_specs=pl.BlockSpec((None, tm, tn), lambda k, i, j, t: (k, i, j)), out_shape=SDS((nb, m, n), bf16),
        scratch_shapes=[pltpu.VMEM((tm, tn), f32)], compiler_params=_cp("parallel", "parallel", "parallel", "arbitrary"),
    )(a, b)


def _mm_tn_slabs(a, b, nb, m, nblk, rel, width, tm=512, ts=512, name="mm_tn_slabs"):
    s = a.shape[0]
    n = b.shape[1] // nb
    ng, mi, ns, nw = n // nblk, m // tm, s // ts, len(rel)

    def body(a_ref, b_ref, o_ref, acc):
        t = pl.program_id(3)

        @pl.when(t == 0)
        def _():
            acc[...] = jnp.zeros_like(acc)

        acc[...] += _dot_tn(a_ref[...], b_ref[...])

        @pl.when(t == ns - 1)
        def _():
            for r, start in enumerate(rel):
                o_ref[r] = acc[:, start:start + width].astype(bf16)

    return pl.pallas_call(
        body, name=name, grid=(nb, ng, mi, ns),
        in_specs=[pl.BlockSpec((ts, tm), lambda k, g, i, t: (t, k * mi + i)), pl.BlockSpec((ts, nblk), lambda k, g, i, t: (t, k * ng + g))],
        out_specs=pl.BlockSpec((nw, None, tm, width), lambda k, g, i, t: (g, k, i, 0)), out_shape=SDS((ng * nw, nb, m, width), bf16),
        scratch_shapes=[pltpu.VMEM((tm, nblk), f32)], compiler_params=_cp("parallel", "parallel", "parallel", "arbitrary"),
    )(a, b)


def _merge_bwd(dx1b, wo, y, p, wb, tm=256):
    s = dx1b.shape[0]
    hw = D_MODEL // 2
    gblk = GATE_COL0 // hw

    def body(dx_ref, wo_ref, y_ref, gt_ref, wb_ref, dy_ref, dp_ref, dz_ref, dm_scr, dz_acc):
        k = pl.program_id(1)
        hf = k % 2

        @pl.when(k == 0)
        def _():
            dm = _dot_nt(dx_ref[...], wo_ref[...])
            dm_scr[0] = dm[:, :hw]
            dm_scr[1] = dm[:, hw:]

        dmh = dm_scr[hf]
        gate = _sigmoid(gt_ref[...])
        dy = (dmh * gate).astype(bf16)
        dy_ref[...] = dy
        dp_ref[...] = (dmh * y_ref[...].astype(f32) * gate * (1.0 - gate)).astype(bf16)
        part = _dot_nt(dy, wb_ref[...])

        @pl.when(hf == 0)
        def _():
            dz_acc[...] = part

        @pl.when(hf == 1)
        def _():
            dz_ref[...] = dz_acc[...] + part

    return pl.pallas_call(
        body, name="merge_bwd", grid=(s // tm, 6),
        in_specs=[pl.BlockSpec((tm, D_MODEL), lambda i, k: (i, 0)), pl.BlockSpec((D_MODEL, D_MODEL), lambda i, k: (0, 0)),
                  pl.BlockSpec((tm, hw), lambda i, k: (i, k)), pl.BlockSpec((tm, hw), lambda i, k: (i, gblk + k)),
                  pl.BlockSpec((None, BRANCH, hw), lambda i, k: (k // 2, 0, k % 2))],
        out_specs=[pl.BlockSpec((tm, hw), lambda i, k: (i, k)), pl.BlockSpec((tm, hw), lambda i, k: (i, gblk + k)),
                   pl.BlockSpec((tm, BRANCH), lambda i, k: (i, k // 2))],
        out_shape=[SDS((s, 3 * D_MODEL), bf16), SDS((s, N_COLS), bf16), SDS((s, 3 * BRANCH), f32)],
        scratch_shapes=[pltpu.VMEM((2, tm, hw), f32), pltpu.VMEM((tm, BRANCH), f32)], compiler_params=_cp("parallel", "arbitrary"),
    )(dx1b, wo, y, p, wb)


def _mixer_bwd(p, dz, opre, states, dp, lb, gout, wconv, lng, lnb, wsg, bsg_t):
    s = p.shape[0]
    tt = MIX_TILE
    nt = s // tt
    nch = tt // HGRN_CHUNK
    rev = lambda i: nt - 1 - i

    def body(q_ref, fp_ref, iv_ref, go_ref, bg_ref, cg_ref, xc_ref, u_ref, v_ref, cgp_ref, xcp_ref, dz_ref, opre_ref, st_ref,
             dp_in, lb_ref, gout_ref, wconv_ref, lng_ref, lnb_ref, wsg_ref, bsg_ref,
             dp_ref, vec_ref, dwsg_ref, dbsg_ref, dst_scr, zbuf, dybuf, dvn_scr, dbsg_acc):
        del dp_in
        i = pl.program_id(0)

        @pl.when(i == 0)
        def _():
            dst_scr[...] = jnp.zeros_like(dst_scr)
            dybuf[tt:tt + 8, :] = jnp.zeros((8, BRANCH), f32)
            vec_ref[...] = jnp.zeros_like(vec_ref)
            dwsg_ref[...] = jnp.zeros_like(dwsg_ref)
            dbsg_acc[...] = jnp.zeros_like(dbsg_acc)

        lbv = lb_ref[...]
        q_raw, fp = q_ref[...], fp_ref[...]
        sq = _sigmoid(q_raw)
        qs = q_raw * sq
        sfp = _sigmoid(fp)
        logf, snf, kk = _hgrn_gates(fp, lbv)
        inv_f = jnp.exp(-logf)
        iv = iv_ref[...]
        doa = dz_ref[:, 0:BRANCH]
        o = opre_ref[...]
        sgo = _sigmoid(go_ref[...])
        gout_v = gout_ref[...]
        d_o, dgo, dgout = [], [], []
        for h in range(HEADS):
            sl = slice(HEAD_DIM * h, HEAD_DIM * (h + 1))
            r, oh = _rms_stats(o[:, sl])
            d_on = doa[:, sl] * sgo[:, sl]
            dgo.append(doa[:, sl] * oh * gout_v[:, sl] * sgo[:, sl] * (1.0 - sgo[:, sl]))
            dx, dg = _rms_bwd(d_on, oh, r, gout_v[:, sl])
            d_o.append(dx)
            dgout.append(dg)
        d_o = jnp.concatenate(d_o, axis=1)
        dp_ref[:, 3 * BRANCH:4 * BRANCH] = jnp.concatenate(dgo, axis=1).astype(bf16)
        vec_ref[1:2, :] += jnp.concatenate(dgout, axis=1)

        causal = _tri(HGRN_CHUNK)
        tri = causal.astype(f32)
        tri_up = _tri(HGRN_CHUNK, upper=True).astype(f32)
        last_row = lax.broadcasted_iota(jnp.int32, (HGRN_CHUNK, 1), 0) == HGRN_CHUNK - 1
        lb_live = (lbv > LB_FLOOR).astype(f32)
        dlb = jnp.zeros((1, BRANCH), f32)
        for c in reversed(range(nch)):
            rows = slice(HGRN_CHUNK * c, HGRN_CHUNK * (c + 1))
            b = _dot_exact(tri, logf[rows])
            bl = jnp.sum(jnp.where(last_row, b, 0.0), axis=0, keepdims=True)
            eb, enb, edl, ebl = jnp.exp(b), jnp.exp(-b), jnp.exp(bl - b), jnp.exp(bl)
            qbf, kbf, kdf = qs[rows] * eb, kk[rows] * enb, kk[rows] * edl
            qb, kb, kd = qbf.astype(bf16), kbf.astype(bf16), kdf.astype(bf16)
            vc = iv[rows].astype(bf16)
            dob = d_o[rows].astype(bf16)
            dv, dqb, dkb, dkd, debl = [], [], [], [], []
            for h in range(HEADS):
                sl = slice(HEAD_DIM * h, HEAD_DIM * (h + 1))
                st = st_ref[c, h]
                dst = dst_scr[h]
                stb, dstb = st.astype(bf16), dst.astype(bf16)
                a = jnp.where(causal, _dot_nt(qb[:, sl], kb[:, sl]), 0.0).astype(bf16)
                da = jnp.where(causal, _dot_nt(dob[:, sl], vc[:, sl]), 0.0).astype(bf16)
                dv.append(_dot_tn(a, dob[:, sl]) + _dot_nt(kd[:, sl], dstb))
                dqb.append(_dot(dob[:, sl], stb) + _dot(da, kb[:, sl]))
                dkb.append(_dot_tn(da, qb[:, sl]))
                dkd.append(_dot(vc[:, sl], dstb))
                debl.append(jnp.sum(st * dst, axis=0, keepdims=True))
                dst_scr[h] = _dot_tn(dob[:, sl], qb[:, sl]) + dst * ebl[:, sl]
            dv, dqb, dkb, dkd = (jnp.concatenate(t, axis=1) for t in (dv, dqb, dkb, dkd))
            debl = jnp.concatenate(debl, axis=1)
            t_kd = dkd * kdf
            dbl = ebl * debl + jnp.sum(t_kd, axis=0, keepdims=True)
            db = dqb * qbf - dkb * kbf - t_kd + jnp.where(last_row, dbl, 0.0)
            dkk = dkb * enb + dkd * edl
            dlc = _dot_exact(tri_up, db)
            sq_c, q_c, sfp_c, snf_c, invf_c = sq[rows], q_raw[rows], sfp[rows], snf[rows], inv_f[rows]
            slope = (1.0 - lbv) * sfp_c * snf_c
            dp_ref[rows, 0:BRANCH] = (dqb * eb * sq_c * (1.0 + q_c * (1.0 - sq_c))).astype(bf16)
            dp_ref[rows, BRANCH:2 * BRANCH] = (slope * (dlc * invf_c - dkk)).astype(bf16)
            dp_ref[rows, 2 * BRANCH:3 * BRANCH] = dv.astype(bf16)
            dlb = dlb + jnp.sum(dlc * (lb_live - sfp_c) * invf_c - dkk * snf_c, axis=0, keepdims=True)
        vec_ref[0:1, :] += dlb

        dob_ = dz_ref[:, BRANCH:2 * BRANCH]
        bg, cg, xc = bg_ref[...], cg_ref[...], xc_ref[...]
        zc = cg * xc
        zbuf[0:8, :] = jnp.where(i < nt - 1, cgp_ref[...] * xcp_ref[...], 0.0)
        zbuf[8:8 + tt, :] = zc
        w0, w1, w2 = wconv_ref[0:1, :], wconv_ref[1:2, :], wconv_ref[2:3, :]
        y = w0 * zbuf[pl.ds(6, tt), :] + w1 * zbuf[pl.ds(7, tt), :] + w2 * zc
        dy = dob_ * bg
        dybuf[0:tt, :] = dy
        dy1, dy2 = dybuf[pl.ds(1, tt), :], dybuf[pl.ds(2, tt), :]
        dzc = w2 * dy + w1 * dy1 + w0 * dy2
        dp_ref[:, 4 * BRANCH:5 * BRANCH] = (dob_ * y).astype(bf16)
        dp_ref[:, 5 * BRANCH:6 * BRANCH] = (dzc * xc).astype(bf16)
        dp_ref[:, 6 * BRANCH:7 * BRANCH] = (dzc * cg).astype(bf16)
        vec_ref[4:5, :] += jnp.sum(zc * dy2, axis=0, keepdims=True)
        vec_ref[5:6, :] += jnp.sum(zc * dy1, axis=0, keepdims=True)
        vec_ref[6:7, :] += jnp.sum(zc * dy, axis=0, keepdims=True)
        dybuf[tt:tt + 8, :] = dybuf[0:8, :]

        doc = dz_ref[:, 2 * BRANCH:3 * BRANCH]
        u_raw, v_raw = u_ref[...], v_ref[...]
        ug = _gelu(u_raw)
        dug_scale = _gelu_grad(u_raw)
        vg = _gelu(v_raw)
        vcen = vg - jnp.mean(vg, axis=-1, keepdims=True)
        rstd = lax.rsqrt(jnp.mean(vcen * vcen, axis=-1, keepdims=True) + LN_EPS)
        vhat = vcen * rstd
        lng_v = lng_ref[...]
        vn = (vhat * lng_v + lnb_ref[...]).astype(bf16)
        low = _tri(SG_CHUNK)
        for g in range(SG_GROUPS):
            sl = slice(LANE * g, LANE * (g + 1))
            wm = jnp.where(low, wsg_ref[g], 0.0).astype(bf16)
            bias = bsg_ref[:, g:g + 1]
            dw = jnp.zeros((SG_CHUNK, SG_CHUNK), f32)
            dbs = jnp.zeros((SG_CHUNK, LANE), f32)
            for cc in range(tt // SG_CHUNK):
                rows = slice(SG_CHUNK * cc, SG_CHUNK * (cc + 1))
                vn_c = vn[rows, sl]
                sv = _dot(wm, vn_c) + bias
                doc_c = doc[rows, sl]
                dp_ref[rows, 7 * BRANCH + LANE * g:7 * BRANCH + LANE * (g + 1)] = (doc_c * sv * dug_scale[rows, sl]).astype(bf16)
                dsv = doc_c * ug[rows, sl]
                dsvb = dsv.astype(bf16)
                dbs = dbs + dsv
                dw = dw + _dot_nt(dsvb, vn_c)
                dvn_scr[rows, sl] = _dot_tn(wm, dsvb)
            dwsg_ref[g] += jnp.where(low, dw, 0.0)
            dbsg_acc[:, sl] += dbs
        dvn = dvn_scr[...]
        vec_ref[2:3, :] += jnp.sum(dvn * vhat, axis=0, keepdims=True)
        vec_ref[3:4, :] += jnp.sum(dvn, axis=0, keepdims=True)
        dvh = dvn * lng_v
        dvg = rstd * (dvh - jnp.mean(dvh, axis=-1, keepdims=True) - vhat * jnp.mean(dvh * vhat, axis=-1, keepdims=True))
        dp_ref[:, 8 * BRANCH:9 * BRANCH] = (dvg * _gelu_grad(v_raw)).astype(bf16)

        @pl.when(i == nt - 1)
        def _():
            for g in range(SG_GROUPS):
                dbsg_ref[:, g:g + 1] = jnp.sum(dbsg_acc[:, LANE * g:LANE * (g + 1)], axis=1, keepdims=True)

    full = lambda shape: pl.BlockSpec(shape, lambda i: (0,) * len(shape))
    tail = lambda c: pl.BlockSpec((8, BRANCH), lambda i: (jnp.maximum(rev(i) * (tt // 8) - 1, 0), c))
    return pl.pallas_call(
        body, name="mixer_bwd", grid=(nt,),
        in_specs=_p_specs(tt, range(9), rev) + [tail(5), tail(6), pl.BlockSpec((tt, 3 * BRANCH), lambda i: (rev(i), 0)),
                                                pl.BlockSpec((tt, BRANCH), lambda i: (rev(i), 0)),
                                                pl.BlockSpec((nch, HEADS, HEAD_DIM, HEAD_DIM), lambda i: (rev(i), 0, 0, 0)),
                                                pl.BlockSpec(memory_space=pl.ANY),
                                                full((1, BRANCH)), full((1, BRANCH)), full((3, BRANCH)), full((1, BRANCH)), full((1, BRANCH)),
                                                full((SG_GROUPS, SG_CHUNK, SG_CHUNK)), full((SG_CHUNK, SG_GROUPS))],
        out_specs=[pl.BlockSpec((tt, 9 * BRANCH), lambda i: (rev(i), 0)), full((8, BRANCH)), full((SG_GROUPS, SG_CHUNK, SG_CHUNK)),
                   full((SG_CHUNK, SG_GROUPS))],
        out_shape=[SDS((s, N_COLS), bf16), SDS((8, BRANCH), f32), SDS((SG_GROUPS, SG_CHUNK, SG_CHUNK), f32), SDS((SG_CHUNK, SG_GROUPS), f32)],
        scratch_shapes=[pltpu.VMEM((HEADS, HEAD_DIM, HEAD_DIM), f32), pltpu.VMEM((tt + 8, BRANCH), f32), pltpu.VMEM((tt + 8, BRANCH), f32),
                        pltpu.VMEM((tt, BRANCH), f32), pltpu.VMEM((SG_CHUNK, BRANCH), f32)],
        input_output_aliases={14: 0}, compiler_params=_cp("arbitrary"),
    )(*([p] * 11), dz, opre, states, dp, lb, gout, wconv, lng, lnb, wsg, bsg_t)


def _dh_bwd(dp, w_in, x, dx1, g, tm=512, tk=1536):
    s = x.shape[0]
    nk = N_COLS // tk

    def body(dp_ref, w_ref, x_ref, dx1_ref, g_ref, dx_ref, dxb_ref, dg_ref, acc):
        i, k = pl.program_id(0), pl.program_id(1)

        @pl.when(k == 0)
        def _():
            acc[...] = jnp.zeros_like(acc)

        acc[...] += _dot_nt(dp_ref[...], w_ref[...])

        @pl.when(k == nk - 1)
        def _():
            r, xh = _rms_stats(x_ref[...])
            dx, dg = _rms_bwd(acc[...], xh, r, g_ref[...])
            dx = dx + dx1_ref[...]
            dx_ref[...] = dx
            dxb_ref[...] = dx.astype(bf16)
            _acc_rows(dg_ref, i == 0, dg)

    row = pl.BlockSpec((tm, D_MODEL), lambda i, k: (i, 0))
    vec = pl.BlockSpec((1, D_MODEL), lambda i, k: (0, 0))
    return pl.pallas_call(
        body, name="dh_bwd", grid=(s // tm, nk),
        in_specs=[pl.BlockSpec((tm, tk), lambda i, k: (i, k)), pl.BlockSpec((D_MODEL, tk), lambda i, k: (0, k)), row, row, vec],
        out_specs=[row, row, vec], out_shape=[SDS((s, D_MODEL), f32), SDS((s, D_MODEL), bf16), SDS((1, D_MODEL), f32)],
        scratch_shapes=[pltpu.VMEM((tm, D_MODEL), f32)], compiler_params=_cp("arbitrary", "arbitrary"),
    )(dp, w_in, x, dx1, g)


def _layer_fwd(x, w, sm):
    p, h = _rms_mm(x, sm["g_mix"], w["w_in"])
    z, opre, states = _mixer_fwd(p, sm["lb"], sm["g_out"], sm["w_conv"], sm["ln_g"], sm["ln_b"], sm["w_sg"], sm["b_sg_t"])
    y, merged = _branch_gate(z, w["w_branch"], p)
    x1 = _mm_resid(x, merged, w["w_o"])
    x2, h2, ra = _ffn(x1, sm["g_ffn"], w["w_ff1"], w["w_ff2"])
    saved = dict(x=x, p=p, h=h, z=z, opre=opre, states=states, y=y, merged=merged, x1=x1, h2=h2, ra=ra)
    return x2, saved


def _layer_bwd(dx2, dx2b, sv, w, sm):
    nchip = N_DEV // 2
    by_chip = lambda g: g.reshape((nchip, 2) + g.shape[1:])
    da, act, dx1, dx1b, dg_ffn = _ffn_bwd(dx2, dx2b, sv["x1"], sm["g_ffn"], sv["ra"], w["w_ff1"], w["w_ff2"])
    g_ff2 = by_chip(_mm_tn(act, dx2b, 1, D_FF, D_MODEL, 512, 1024, name="dw_ff2")[0].reshape(N_DEV, D_FF // N_DEV, D_MODEL))
    g_ff1 = by_chip(_mm_tn_slabs(sv["h2"], da, 1, D_MODEL, D_FF // 2, [i * (D_FF // N_DEV) for i in range(nchip)], D_FF // N_DEV,
                                 name="dw_ff1")[:, 0])
    g_o = by_chip(_mm_tn(sv["merged"], dx1b, 1, D_MODEL, D_MODEL, 512, 1024, name="dw_o")[0].reshape(N_DEV, D_MODEL // N_DEV, D_MODEL))
    dy, dp, dz = _merge_bwd(dx1b, w["w_o"], sv["y"], sv["p"], w["w_branch"])
    g_branch = by_chip(_mm_tn_slabs(sv["z"], dy, 3, BRANCH, D_MODEL, [i * (D_MODEL // N_DEV) for i in range(N_DEV)], D_MODEL // N_DEV,
                                    name="dw_branch"))
    dp, vecs, dwsg, dbsg_t = _mixer_bwd(sv["p"], dz, sv["opre"], sv["states"], dp, sm["lb"], sm["g_out"], sm["w_conv"],
                                        sm["ln_g"], sm["ln_b"], sm["w_sg"], sm["b_sg_t"])
    starts = [SHARD_IN * i - (WIN - SHARD_IN) * (i % 2) for i in range(nchip)]
    g_in = by_chip(_mm_tn_slabs(sv["h"], dp, 1, D_MODEL, N_COLS // 2, starts, WIN, name="dw_in")[:, 0])
    dx, dxb, dg_mix = _dh_bwd(dp, w["w_in"], sv["x"], dx1, sm["g_mix"])
    big = dict(w_in=g_in, w_branch=g_branch, w_o=g_o, w_ff1=g_ff1, w_ff2=g_ff2)
    small = dict(g_mix=dg_mix, g_ffn=dg_ffn, vecs=vecs, w_sg=dwsg, b_sg_t=dbsg_t)
    return dx, dxb, big, small


BIG = ("w_in", "w_branch", "w_o", "w_ff1", "w_ff2")
ANY = pl.BlockSpec(memory_space=pl.ANY)


def _place():
    return lax.axis_index("x"), lax.axis_index("y"), lax.axis_index("c")


def _al(v, m):
    return pl.multiple_of(v * m, m)


def _shard_of(refs, dev):
    w_in, w_b, w_o, w_1, w_2 = refs
    nb, no, n1, n2 = w_b.shape[-1] // N_DEV, w_o.shape[0] // N_DEV, w_1.shape[-1] // N_DEV, w_2.shape[0] // N_DEV
    return [w_in, w_b.at[:, :, pl.ds(_al(dev, nb), nb)], w_o.at[pl.ds(_al(dev, no), no), :],
            w_1.at[:, pl.ds(_al(dev, n1), n1)], w_2.at[pl.ds(_al(dev, n2), n2), :]]


def _all_gather_layer(layer, shards):
    s_in, s_b, s_o, s_1, s_2 = (shards[n] for n in BIG)
    out_shape = [SDS((N_DEV,) + s_in.shape[1:], bf16), SDS(s_b.shape[1:3] + (s_b.shape[3] * N_DEV,), bf16),
                 SDS((s_o.shape[1] * N_DEV, s_o.shape[2]), bf16), SDS((s_1.shape[1], s_1.shape[2] * N_DEV), bf16),
                 SDS((s_2.shape[1] * N_DEV, s_2.shape[2]), bf16)]
    nt = len(BIG)

    def body(i_in, i_b, i_o, i_1, i_2, o_in, o_b, o_o, o_1, o_2, send_sems, recv_sems, local_sems):
        x, y, c = _place()
        me, sibling = (x, y, c), (x, y, 1 - c)
        chips = [(1 - x, y), (x, 1 - y), (1 - x, 1 - y)]
        mine = [r.at[layer] for r in (i_in, i_b, i_o, i_1, i_2)]

        def block(px, py, pc):
            dev = 4 * px + 2 * py + pc
            return [o_in.at[dev]] + _shard_of((None, o_b, o_o, o_1, o_2), dev)[1:]

        def copies(k, blk, to, src=None):
            dst = block(*blk)
            src = dst if src is None else src
            return [pltpu.make_async_remote_copy(src_ref=src[t], dst_ref=dst[t], send_sem=send_sems.at[k, t], recv_sem=recv_sems.at[k, t],
                                                 device_id=to, device_id_type=MESH) for t in range(nt)]

        local = [pltpu.make_async_copy(mine[t], block(*me)[t], local_sems.at[t]) for t in range(nt)]
        for cp in local:
            cp.start()
        first = copies(0, me, sibling, src=mine)
        for j, chip in enumerate(chips):
            first += copies(1 + j, me, (*chip, c), src=mine)
        for cp in first:
            cp.start()
        passed = [copies(4 + j, (*chip, c), sibling) for j, chip in enumerate(chips)]
        for j, chip in enumerate(chips):
            for cp in copies(1 + j, (*chip, c), me):
                cp.wait_recv()
            for cp in passed[j]:
                cp.start()
        for cp in copies(0, sibling, me):
            cp.wait_recv()
        for j, chip in enumerate(chips):
            for cp in copies(4 + j, (*chip, 1 - c), me):
                cp.wait_recv()
        for cp in first + [cp for grp in passed for cp in grp]:
            cp.wait_send()
        for cp in local:
            cp.wait()

    return pl.pallas_call(
        body, name=f"all_gather_l{layer}", in_specs=[ANY] * nt, out_specs=[ANY] * nt, out_shape=out_shape,
        scratch_shapes=[pltpu.SemaphoreType.DMA((7, nt)), pltpu.SemaphoreType.DMA((7, nt)), pltpu.SemaphoreType.DMA((nt,))],
    )(s_in, s_b, s_o, s_1, s_2)


def _gather_out_shapes(shards):
    s_in, s_b, s_o, s_1, s_2 = (shards[n] for n in BIG)
    return [SDS((N_DEV,) + s_in.shape[1:], bf16), SDS(s_b.shape[1:3] + (s_b.shape[3] * N_DEV,), bf16),
            SDS((s_o.shape[1] * N_DEV, s_o.shape[2]), bf16), SDS((s_1.shape[1], s_1.shape[2] * N_DEV), bf16),
            SDS((s_2.shape[1] * N_DEV, s_2.shape[2]), bf16)]


def _seq_all_gather_layer(layer, shard_refs, out_shapes):
    nt = len(BIG)
    outs = [jax.empty_ref(sh, memory_space=pltpu.MemorySpace.HBM) for sh in out_shapes]

    @pl.kernel(mesh=plsc.ScalarSubcoreMesh(axis_name="seq", num_cores=1), name=f"seq_all_gather_l{layer}",
               scratch_types=(pltpu.SemaphoreType.DMA((7,)), pltpu.SemaphoreType.DMA((7,))),
               compiler_params=pltpu.CompilerParams(collective_id=1))
    def launch(send_sems, recv_sems):
        x, y, c = _place()
        me, sibling = (x, y, c), (x, y, 1 - c)
        chips = [(1 - x, y), (x, 1 - y), (1 - x, 1 - y)]
        peers = [sibling] + [(*chip, c) for chip in chips]
        barrier = pltpu.get_barrier_semaphore()
        for p in peers:
            pl.semaphore_signal(barrier, inc=1, device_id=p, device_id_type=MESH)
        pl.semaphore_wait(barrier, len(peers))
        mine = [r.at[layer] for r in shard_refs]
        o_in, o_b, o_o, o_1, o_2 = outs

        def block(px, py, pc):
            dev = 4 * px + 2 * py + pc
            return [o_in.at[dev]] + _shard_of((None, o_b, o_o, o_1, o_2), dev)[1:]

        def copies(k, blk, to, src=None):
            dst = block(*blk)
            src = dst if src is None else src
            return [pltpu.make_async_remote_copy(src_ref=src[t], dst_ref=dst[t], send_sem=send_sems.at[k], recv_sem=recv_sems.at[k],
                                                 device_id=to, device_id_type=MESH) for t in range(nt)]

        first = copies(0, me, sibling, src=mine)
        for j, chip in enumerate(chips):
            first += copies(1 + j, me, (*chip, c), src=mine)
        for cp in first:
            cp.start()
        passed = [copies(4 + j, (*chip, c), sibling) for j, chip in enumerate(chips)]
        for j, chip in enumerate(chips):
            for cp in copies(1 + j, (*chip, c), me):
                cp.wait_recv()
            for cp in passed[j]:
                cp.start()
        for cp in copies(0, sibling, me):
            cp.wait_recv()
        for j, chip in enumerate(chips):
            for cp in copies(4 + j, (*chip, 1 - c), me):
                cp.wait_recv()
        for cp in first + [cp for grp in passed for cp in grp]:
            cp.wait_send()

    launch()
    return [o[...] for o in outs]


def _place_own(where, shards, gathered, after):
    s_in, s_b, s_o, s_1, s_2 = (shards[n] for n in BIG)
    nt = len(BIG)

    def body(where_ref, *refs):
        del where_ref
        for src, dst in zip(refs[:nt], refs[2 * nt + 1:]):
            dst[...] = src[...]

    lay = lambda shape, fn: pl.BlockSpec(shape, fn)
    in_specs = [lay((None,) + s_in.shape[1:], lambda i, wh: (wh[0], 0, 0)), lay((None,) + s_b.shape[1:], lambda i, wh: (wh[0], 0, 0, 0)),
                lay((None,) + s_o.shape[1:], lambda i, wh: (wh[0], 0, 0)), lay((None,) + s_1.shape[1:], lambda i, wh: (wh[0], 0, 0)),
                lay((None,) + s_2.shape[1:], lambda i, wh: (wh[0], 0, 0))]
    out_specs = [lay((None,) + s_in.shape[1:], lambda i, wh: (wh[1], 0, 0)), lay(s_b.shape[1:], lambda i, wh: (0, 0, wh[1])),
                 lay(s_o.shape[1:], lambda i, wh: (wh[1], 0)), lay(s_1.shape[1:], lambda i, wh: (0, wh[1])),
                 lay(s_2.shape[1:], lambda i, wh: (wh[1], 0))]
    return pl.pallas_call(
        body, name="place_own", out_shape=[SDS(g.shape, g.dtype) for g in gathered],
        input_output_aliases={1 + nt + i: i for i in range(nt)}, compiler_params=_cp("arbitrary"),
        grid_spec=pltpu.PrefetchScalarGridSpec(num_scalar_prefetch=1, grid=(1,), in_specs=in_specs + [ANY] * (nt + 1), out_specs=out_specs),
    )(where, s_in, s_b, s_o, s_1, s_2, *gathered, after)


def _merge_windows(land, tr=512):
    r = land.shape[1]
    tr = min(tr, r)
    chip_cols = 2 * SHARD_IN
    cut = WIN - LANE

    def body(e_ref, o_ref, out_ref):
        out_ref[:, 0:cut] = e_ref[:, 0:cut]
        out_ref[:, cut:WIN] = e_ref[:, cut:WIN] + o_ref[:, 0:LANE]
        out_ref[:, WIN:chip_cols] = o_ref[:, LANE:WIN]

    return pl.pallas_call(
        body, name="merge_windows", grid=(N_DEV // 2, r // tr),
        in_specs=[pl.BlockSpec((None, tr, WIN), lambda k, i: (2 * k, i, 0)), pl.BlockSpec((None, tr, WIN), lambda k, i: (2 * k + 1, i, 0))],
        out_specs=pl.BlockSpec((tr, chip_cols), lambda k, i: (i, k)), out_shape=SDS((r, N_COLS), bf16),
        compiler_params=_cp("parallel", "parallel"),
    )(land, land)


def _exchange_on_chip(grads):
    nt, nchip = len(BIG), N_DEV // 2

    def body(*refs):
        g, landed = refs[:nt], refs[nt:2 * nt]
        send_sems, recv_sems = refs[2 * nt:]
        x, y, c = _place()
        remote = [pltpu.make_async_remote_copy(src_ref=g[t].at[j, 1 - c], dst_ref=landed[t].at[j], send_sem=send_sems.at[j, t],
                                               recv_sem=recv_sems.at[j, t], device_id=(x, y, 1 - c), device_id_type=MESH)
                  for j in range(nchip) for t in range(nt)]
        for cp in remote:
            cp.start()
        for cp in remote:
            cp.wait_recv()
        for cp in remote:
            cp.wait_send()

    return pl.pallas_call(
        body, name="rs_on_chip", in_specs=[ANY] * nt, out_specs=[ANY] * nt,
        out_shape=[SDS((nchip,) + grads[n].shape[2:], bf16) for n in BIG],
        scratch_shapes=[pltpu.SemaphoreType.DMA((nchip, nt)), pltpu.SemaphoreType.DMA((nchip, nt))],
    )(*(grads[n] for n in BIG))


def _chip_sum(core, mine, other, tr=1024):
    nchip = mine.shape[0]
    cols = mine.shape[-1]
    m4 = mine.reshape(nchip, 2, -1, cols)
    o3 = other.reshape(nchip, -1, cols)
    rows = o3.shape[1]
    tr = min(tr, rows)
    while rows % tr:
        tr //= 2

    def body(c_ref, a_ref, b_ref, o_ref):
        del c_ref
        o_ref[...] = (a_ref[...].astype(f32) + b_ref[...].astype(f32)).astype(bf16)

    blk = pl.BlockSpec((None, tr, cols), lambda j, i, c_ref: (j, i, 0))
    out = pl.pallas_call(
        body, name="chip_sum", out_shape=SDS(o3.shape, bf16), compiler_params=_cp("parallel", "parallel"),
        grid_spec=pltpu.PrefetchScalarGridSpec(
            num_scalar_prefetch=1, grid=(nchip, rows // tr),
            in_specs=[pl.BlockSpec((None, None, tr, cols), lambda j, i, c_ref: (j, c_ref[0], i, 0)), blk], out_specs=blk),
    )(core, m4, o3)
    return out.reshape(other.shape)


def _exchange_between_chips(sums):
    nt = len(BIG)

    def body(*refs):
        sums_r, land_r = refs[:nt], refs[nt:2 * nt]
        send_sems, recv_sems = refs[2 * nt:]
        x, y, c = _place()
        chips = [(1 - x, y), (x, 1 - y), (1 - x, 1 - y)]
        remote = [pltpu.make_async_remote_copy(src_ref=sums_r[t].at[2 * cx + cy], dst_ref=land_r[t].at[k], send_sem=send_sems.at[k, t],
                                               recv_sem=recv_sems.at[k, t], device_id=(cx, cy, c), device_id_type=MESH)
                  for k, (cx, cy) in enumerate(chips) for t in range(nt)]
        for cp in remote:
            cp.start()
        for cp in remote:
            cp.wait_recv()
        for cp in remote:
            cp.wait_send()

    return pl.pallas_call(
        body, name="rs_between_chips", in_specs=[ANY] * nt, out_specs=[ANY] * nt,
        out_shape=[SDS((3,) + a.shape[1:], bf16) for a in sums],
        scratch_shapes=[pltpu.SemaphoreType.DMA((3, nt)), pltpu.SemaphoreType.DMA((3, nt))],
    )(*sums)


def _all_reduce_rows(pack):
    rows = pack.shape[0]
    blk = rows // N_DEV

    def body(in_ref, out_ref, land, send1, recv1, send2, recv2):
        x, y, c = _place()
        me = 4 * x + 2 * y + c
        others = [(px, py, pc) for px in range(2) for py in range(2) for pc in range(2)]

        def is_me(p):
            return jnp.logical_and(jnp.logical_and(p[0] == x, p[1] == y), p[2] == c)

        land[me] = in_ref[pl.ds(_al(me, blk), blk), :]
        for d, p in enumerate(others):
            @pl.when(jnp.logical_not(is_me(p)))
            def _():
                pltpu.make_async_remote_copy(src_ref=in_ref.at[pl.ds(d * blk, blk), :], dst_ref=land.at[me], send_sem=send1.at[d],
                                             recv_sem=recv1.at[me], device_id=p, device_id_type=MESH).start()
        for d, p in enumerate(others):
            @pl.when(jnp.logical_not(is_me(p)))
            def _():
                cp = pltpu.make_async_remote_copy(src_ref=in_ref.at[pl.ds(d * blk, blk), :], dst_ref=land.at[d], send_sem=send1.at[d],
                                                  recv_sem=recv1.at[d], device_id=p, device_id_type=MESH)
                cp.wait_recv()
                cp.wait_send()
        total = land[0]
        for d in range(1, N_DEV):
            total = total + land[d]
        out_ref[pl.ds(_al(me, blk), blk), :] = total
        for d, p in enumerate(others):
            @pl.when(jnp.logical_not(is_me(p)))
            def _():
                mine = out_ref.at[pl.ds(_al(me, blk), blk), :]
                pltpu.make_async_remote_copy(src_ref=mine, dst_ref=mine, send_sem=send2.at[d], recv_sem=recv2.at[me],
                                             device_id=p, device_id_type=MESH).start()
        for d, p in enumerate(others):
            @pl.when(jnp.logical_not(is_me(p)))
            def _():
                theirs = out_ref.at[pl.ds(d * blk, blk), :]
                cp = pltpu.make_async_remote_copy(src_ref=theirs, dst_ref=theirs, send_sem=send2.at[d], recv_sem=recv2.at[d],
                                                  device_id=p, device_id_type=MESH)
                cp.wait_recv()
                cp.wait_send()

    vm = pl.BlockSpec(memory_space=pltpu.VMEM)
    return pl.pallas_call(
        body, name="all_reduce_rows", in_specs=[vm], out_specs=vm, out_shape=SDS((rows, LANE), f32),
        scratch_shapes=[pltpu.VMEM((N_DEV, blk, LANE), f32)] + [pltpu.SemaphoreType.DMA((N_DEV,))] * 4,
        compiler_params=pltpu.CompilerParams(vmem_limit_bytes=VMEM_LIMIT),
    )(pack)


def _lower_bounds_fwd(lower):
    def body(l_ref, o_ref):
        sm = _layer_softmax(l_ref)
        run = jnp.zeros_like(sm[0])
        for l in range(DEPTH):
            o_ref[l:l + 1, :] = run
            if l + 1 < DEPTH:
                run = run + sm[l + 1]

    return pl.pallas_call(body, name="lower_bounds_fwd", out_shape=SDS(lower.shape, f32))(lower)


def _layer_softmax(l_ref):
    rows = [l_ref[l:l + 1, :] for l in range(DEPTH)]
    top = functools.reduce(jnp.maximum, rows)
    e = [jnp.exp(r - top) for r in rows]
    tot = functools.reduce(lambda a, b: a + b, e)
    return [v / tot for v in e]


def _lower_bounds_bwd(lower, dlbs):
    def body(l_ref, d_ref, o_ref):
        sm = _layer_softmax(l_ref)
        dsm = [None] * DEPTH
        run = jnp.zeros_like(sm[0])
        dsm[0] = run
        for l in reversed(range(1, DEPTH)):
            run = run + d_ref[l:l + 1, :]
            dsm[l] = run
        inner = functools.reduce(lambda a, b: a + b, [sm[l] * dsm[l] for l in range(DEPTH)])
        for l in range(DEPTH):
            o_ref[l:l + 1, :] = sm[l] * (dsm[l] - inner)

    return pl.pallas_call(body, name="lower_bounds_bwd", out_shape=SDS(lower.shape, f32))(lower, dlbs)


_ADAM_C1 = 1.0 - ADAM_B1 ** ADAM_STEP
_ADAM_C2 = 1.0 - ADAM_B2 ** ADAM_STEP


def _adamw(w, g, m, v):
    m = ADAM_B1 * m + (1.0 - ADAM_B1) * g
    v = ADAM_B2 * v + (1.0 - ADAM_B2) * (g * g)
    delta = -ADAM_LR * ((m / _ADAM_C1) / (jnp.sqrt(v / _ADAM_C2) + ADAM_EPS) + ADAM_WD * w)
    return delta, m, v


def _adam_big(where, w, m, v, sums, landed, outs, windowed=False, tr=512):
    shape = w.shape
    cols = shape[-1]
    gcols = sums.shape[-1]
    w3, m3, v3 = (a.reshape(DEPTH, -1, cols) for a in (w, m, v))
    outs3 = [a.reshape(DEPTH, -1, cols) for a in outs]
    sums3 = sums.reshape(sums.shape[0], -1, gcols)
    land3 = landed.reshape(3, -1, gcols)
    rows = w3.shape[1]
    tr = min(tr, rows)

    def body(where_ref, w_ref, m_ref, v_ref, sum_ref, land_ref, *rest):
        g_ref, d_ref, nm_ref, nv_ref = rest[4:]
        g = sum_ref[...].astype(f32)
        for k in range(3):
            g = g + land_ref[k].astype(f32)
        if windowed:
            g = pltpu.roll(g, SHARD_IN * where_ref[2], 1)[:, :cols]
        delta, nm, nv = _adamw(w_ref[...], g, m_ref[...], v_ref[...])
        g_ref[...] = g
        d_ref[...] = delta
        nm_ref[...] = nm
        nv_ref[...] = nv

    blk = pl.BlockSpec((None, tr, cols), lambda i, wh: (wh[0], i, 0))
    res = pl.pallas_call(
        body, name="adam_big", out_shape=[SDS(w3.shape, f32)] * 4, input_output_aliases={6 + i: i for i in range(4)},
        compiler_params=_cp("parallel"),
        grid_spec=pltpu.PrefetchScalarGridSpec(
            num_scalar_prefetch=1, grid=(rows // tr,),
            in_specs=[blk, blk, blk, pl.BlockSpec((None, tr, gcols), lambda i, wh: (wh[1], i, 0)),
                      pl.BlockSpec((3, tr, gcols), lambda i, wh: (0, i, 0))] + [ANY] * 4,
            out_specs=[blk] * 4),
    )(where, w3, m3, v3, sums3, land3, *outs3)
    return [o.reshape(shape) for o in res]


def _adam_rows(w, g, m, v):
    def body(w_ref, g_ref, m_ref, v_ref, d_ref, nm_ref, nv_ref):
        delta, nm, nv = _adamw(w_ref[...], g_ref[...], m_ref[...], v_ref[...])
        d_ref[...] = delta
        nm_ref[...] = nm
        nv_ref[...] = nv

    return pl.pallas_call(body, name="adam_rows", out_shape=[SDS(w.shape, f32)] * 3)(w, g, m, v)


SMALL = ("g_mix", "lower_bounds", "g_hgrn_out", "w_conv", "sg_ln_g", "sg_ln_b", "w_sg", "b_sg", "g_ffn", "g_final")
WEIGHTS = ("w_in", "g_mix", "lower_bounds", "g_hgrn_out", "w_conv", "sg_ln_g", "sg_ln_b", "w_sg", "b_sg", "w_branch", "w_o", "g_ffn",
           "w_ff1", "w_ff2", "g_final")


def _pack_rows(arrays, multiple):
    flat = jnp.concatenate([a.reshape(-1) for a in arrays])
    rows = -(-flat.shape[0] // (LANE * multiple)) * multiple
    return jnp.pad(flat, (0, rows * LANE - flat.shape[0])).reshape(rows, LANE)


def _unpack_rows(pack, like):
    flat = pack.reshape(-1)
    out, at = [], 0
    for a in like:
        out.append(flat[at:at + a.size].reshape(a.shape))
        at += a.size
    return out


def kernel(x, w_in, g_mix, lower_bounds, g_hgrn_out, w_conv, sg_ln_g, sg_ln_b, w_sg, b_sg, w_branch, w_o, g_ffn, w_ff1, w_ff2, g_final, loss_target, m_w_in, m_g_mix, m_lower_bounds, m_g_hgrn_out, m_w_conv, m_sg_ln_g, m_sg_ln_b, m_w_sg, m_b_sg, m_w_branch, m_w_o, m_g_ffn, m_w_ff1, m_w_ff2, m_g_final, v_w_in, v_g_mix, v_lower_bounds, v_g_hgrn_out, v_w_conv, v_sg_ln_g, v_sg_ln_b, v_w_sg, v_b_sg, v_w_branch, v_w_o, v_g_ffn, v_w_ff1, v_w_ff2, v_g_final):
    weights = dict(w_in=w_in, g_mix=g_mix, lower_bounds=lower_bounds, g_hgrn_out=g_hgrn_out, w_conv=w_conv, sg_ln_g=sg_ln_g,
                   sg_ln_b=sg_ln_b, w_sg=w_sg, b_sg=b_sg, w_branch=w_branch, w_o=w_o, g_ffn=g_ffn, w_ff1=w_ff1, w_ff2=w_ff2, g_final=g_final)
    mom1 = dict(w_in=m_w_in, g_mix=m_g_mix, lower_bounds=m_lower_bounds, g_hgrn_out=m_g_hgrn_out, w_conv=m_w_conv, sg_ln_g=m_sg_ln_g,
                sg_ln_b=m_sg_ln_b, w_sg=m_w_sg, b_sg=m_b_sg, w_branch=m_w_branch, w_o=m_w_o, g_ffn=m_g_ffn, w_ff1=m_w_ff1, w_ff2=m_w_ff2,
                g_final=m_g_final)
    mom2 = dict(w_in=v_w_in, g_mix=v_g_mix, lower_bounds=v_lower_bounds, g_hgrn_out=v_g_hgrn_out, w_conv=v_w_conv, sg_ln_g=v_sg_ln_g,
                sg_ln_b=v_sg_ln_b, w_sg=v_w_sg, b_sg=v_b_sg, w_branch=v_w_branch, w_o=v_w_o, g_ffn=v_g_ffn, w_ff1=v_w_ff1, w_ff2=v_w_ff2,
                g_final=v_g_final)
    xi, yi, ci = _place()
    dev = 4 * xi + 2 * yi + ci
    conv_cols = w_conv.shape[-1]

    w_in_b = w_in.astype(bf16)
    pad = WIN - SHARD_IN
    w_in_win = jnp.where(ci == 0, jnp.pad(w_in_b, ((0, 0), (0, 0), (0, pad))), jnp.pad(w_in_b, ((0, 0), (0, 0), (pad, 0))))
    shards = dict(w_in=w_in_win, w_branch=w_branch.astype(bf16), w_o=w_o.astype(bf16), w_ff1=w_ff1.astype(bf16), w_ff2=w_ff2.astype(bf16))

    conv_place = lax.dynamic_update_slice(jnp.zeros((DEPTH, 3, BRANCH), f32), w_conv, (0, 0, dev * conv_cols))
    (w_conv_full,) = _unpack_rows(_all_reduce_rows(_pack_rows([conv_place], 8 * N_DEV)), [conv_place])
    lbs = _lower_bounds_fwd(lower_bounds)

    def small_of(l):
        return dict(g_mix=g_mix[l][None], lb=lbs[l][None], g_out=g_hgrn_out[l][None], w_conv=w_conv_full[l], ln_g=sg_ln_g[l][None],
                    ln_b=sg_ln_b[l][None], w_sg=w_sg[l], b_sg_t=b_sg[l].T, g_ffn=g_ffn[l][None])

    act = x[0]
    full, saved = [], []
    shard_refs = [jax.new_ref(shards[n], memory_space=pltpu.MemorySpace.HBM) for n in BIG]
    gathered = [_seq_all_gather_layer(l, shard_refs, _gather_out_shapes(shards)) for l in range(DEPTH)]
    for l in range(DEPTH):
        land, f_b, f_o, f_1, f_2 = _place_own(jnp.stack([jnp.int32(l), dev.astype(jnp.int32)]), shards, gathered[l], act)
        full.append(dict(w_in=_merge_windows(land), w_branch=f_b, w_o=f_o, w_ff1=f_1, w_ff2=f_2))
        act, sv = _layer_fwd(act, full[l], small_of(l))
        saved.append(sv)
    loss_row, dx, dxb, dg_final = _final(act, loss_target[0], g_final[None])
    loss = lax.psum(loss_row[0, 0], ("x", "y", "c"))

    core = ci.astype(jnp.int32)[None]
    big_out = {n: [lax.empty(weights[n].shape, f32) for _ in range(4)] for n in BIG}
    small_grads = [None] * DEPTH
    for l in reversed(range(DEPTH)):
        dx, dxb, big, small_grads[l] = _layer_bwd(dx, dxb, saved[l], full[l], small_of(l))
        received = _exchange_on_chip(big)
        sums = [_chip_sum(core, big[n], r) for n, r in zip(BIG, received)]
        landed = _exchange_between_chips(sums)
        where = jnp.stack([jnp.int32(l), (2 * xi + yi).astype(jnp.int32), ci.astype(jnp.int32)])
        for t, n in enumerate(BIG):
            big_out[n] = _adam_big(where, weights[n], mom1[n], mom2[n], sums[t], landed[t], big_out[n], windowed=(n == "w_in"))

    stack = lambda f: jnp.stack([f(small_grads[l]) for l in range(DEPTH)])
    d_lower = _lower_bounds_bwd(lower_bounds, stack(lambda s: s["vecs"][0]))
    local_small = dict(g_mix=stack(lambda s: s["g_mix"][0]), lower_bounds=d_lower, g_hgrn_out=stack(lambda s: s["vecs"][1]),
                       w_conv=stack(lambda s: s["vecs"][4:7]), sg_ln_g=stack(lambda s: s["vecs"][2]), sg_ln_b=stack(lambda s: s["vecs"][3]),
                       w_sg=stack(lambda s: s["w_sg"]), b_sg=stack(lambda s: s["b_sg_t"].T), g_ffn=stack(lambda s: s["g_ffn"][0]),
                       g_final=dg_final[0])
    order = [local_small[n] for n in SMALL]
    grads = dict(zip(SMALL, _unpack_rows(_all_reduce_rows(_pack_rows(order, 8 * N_DEV)), order)))
    grads["w_conv"] = lax.dynamic_slice(grads["w_conv"], (0, 0, dev * conv_cols), (DEPTH, 3, conv_cols))

    deltas, new_m, new_v = {}, {}, {}
    for n in BIG:
        grads[n], deltas[n], new_m[n], new_v[n] = big_out[n]
    packs = [_pack_rows([d[n] for n in SMALL], 8) for d in (weights, grads, mom1, mom2)]
    like = [weights[n] for n in SMALL]
    for out, pack in zip((deltas, new_m, new_v), _adam_rows(*packs)):
        out.update(zip(SMALL, _unpack_rows(pack, like)))

    return (loss, dx[None], *[grads[n] for n in WEIGHTS], *[deltas[n] for n in WEIGHTS], *[new_m[n] for n in WEIGHTS],
            *[new_v[n] for n in WEIGHTS])
```

```python
import functools

import jax
import jax.numpy as jnp
from jax import lax
from jax.experimental import pallas as pl
from jax.experimental.pallas import tpu as pltpu
from jax.experimental.pallas import tpu_sc as plsc

f32 = jnp.float32
bf16 = jnp.bfloat16
SDS = jax.ShapeDtypeStruct
MESH = pl.DeviceIdType.MESH

D_MODEL = 1024
BRANCH = 512
N_COLS = 7680
D_FF = 4096
DEPTH = 4
HEADS = 4
HEAD_DIM = 128
HGRN_CHUNK = 64
SG_CHUNK = 128
SG_GROUPS = 4
NORM_EPS = 1e-6
LN_EPS = 1e-5
LB_FLOOR = 1e-30
N_DEV = 8
SHARD_IN = N_COLS // N_DEV
WIN = 1024
LANE = 128
GATE_COL0 = 9 * BRANCH

ADAM_LR = 0.001
ADAM_B1 = 0.9
ADAM_B2 = 0.999
ADAM_EPS = 1e-08
ADAM_WD = 0.01
ADAM_STEP = 10

MIX_TILE = 256
VMEM_LIMIT = 56 * 1024 * 1024


def _cp(*sem):
    return pltpu.CompilerParams(dimension_semantics=sem or None, vmem_limit_bytes=VMEM_LIMIT)


def _dot(a, b):
    return jnp.dot(a, b, preferred_element_type=f32)


def _dot_nt(a, b):
    return lax.dot_general(a, b, (((1,), (1,)), ((), ())), preferred_element_type=f32)


def _dot_tn(a, b):
    return lax.dot_general(a, b, (((0,), (0,)), ((), ())), preferred_element_type=f32)


def _dot_exact(a, b):
    return jnp.dot(a, b, precision=lax.Precision.HIGHEST, preferred_element_type=f32)


def _sigmoid(x):
    return jax.nn.sigmoid(x)


_GELU_C = 0.7978845608028654
_GELU_A = 0.044715


def _gelu(x):
    return 0.5 * x * (1.0 + jnp.tanh(_GELU_C * (x + _GELU_A * x * x * x)))


def _gelu_grad(x):
    x2 = x * x
    t = jnp.tanh(_GELU_C * (x + _GELU_A * x * x2))
    return 0.5 * (1.0 + t) + 0.5 * x * (1.0 - t * t) * _GELU_C * (1.0 + 3.0 * _GELU_A * x2)


def _rms_stats(x):
    r = lax.rsqrt(jnp.mean(x * x, axis=-1, keepdims=True) + NORM_EPS)
    return r, x * r


def _rms_bwd(dh, xh, r, g):
    dg = jnp.sum(dh * xh, axis=0, keepdims=True)
    dxn = dh * g
    dx = r * (dxn - xh * jnp.mean(dxn * xh, axis=-1, keepdims=True))
    return dx, dg


def _tri(n, upper=False):
    r = lax.broadcasted_iota(jnp.int32, (n, n), 0)
    c = lax.broadcasted_iota(jnp.int32, (n, n), 1)
    return (c >= r) if upper else (c <= r)


def _acc_rows(ref, first, val):
    @pl.when(first)
    def _():
        ref[...] = val

    @pl.when(jnp.logical_not(first))
    def _():
        ref[...] += val


def _rms_mm(x, g, w, tm=512, tn=1920):
    s, n = x.shape[0], w.shape[1]

    def body(x_ref, g_ref, w_ref, p_ref, h_ref, hs):
        @pl.when(pl.program_id(1) == 0)
        def _():
            _, xh = _rms_stats(x_ref[...])
            hv = (xh * g_ref[...]).astype(bf16)
            hs[...] = hv
            h_ref[...] = hv

        p_ref[...] = _dot(hs[...], w_ref[...])

    return pl.pallas_call(
        body, name="rms_mm", grid=(s // tm, n // tn),
        in_specs=[pl.BlockSpec((tm, D_MODEL), lambda i, j: (i, 0)), pl.BlockSpec((1, D_MODEL), lambda i, j: (0, 0)),
                  pl.BlockSpec((D_MODEL, tn), lambda i, j: (0, j))],
        out_specs=[pl.BlockSpec((tm, tn), lambda i, j: (i, j)), pl.BlockSpec((tm, D_MODEL), lambda i, j: (i, 0))],
        out_shape=[SDS((s, n), f32), SDS((s, D_MODEL), bf16)],
        scratch_shapes=[pltpu.VMEM((tm, D_MODEL), bf16)], compiler_params=_cp("parallel", "arbitrary"),
    )(x, g, w)


def _hgrn_gates(fp, lb):
    logf = jnp.logaddexp(jnp.log(jnp.maximum(lb, LB_FLOOR)), jnp.log1p(-lb) + jax.nn.log_sigmoid(fp))
    snf = _sigmoid(-fp)
    return logf, snf, (1.0 - lb) * snf


def _p_specs(tile, cols, row_map):
    return [pl.BlockSpec((tile, BRANCH), functools.partial(lambda c, i: (row_map(i), c), c)) for c in cols]


def _mixer_fwd(p, lb, gout, wconv, lng, lnb, wsg, bsg_t):
    s = p.shape[0]
    tt = MIX_TILE
    nch = tt // HGRN_CHUNK

    def body(q_ref, fp_ref, iv_ref, go_ref, bg_ref, cg_ref, xc_ref, u_ref, v_ref, lb_ref, gout_ref, wconv_ref, lng_ref,
             lnb_ref, wsg_ref, bsg_ref, z_ref, opre_ref, st_ref, st_scr, zbuf):
        @pl.when(pl.program_id(0) == 0)
        def _():
            st_scr[...] = jnp.zeros_like(st_scr)
            zbuf[0:8, :] = jnp.zeros((8, BRANCH), f32)

        lbv = lb_ref[...]
        q_raw = q_ref[...]
        qs = q_raw * _sigmoid(q_raw)
        logf, _, kk = _hgrn_gates(fp_ref[...], lbv)
        iv = iv_ref[...]
        causal = _tri(HGRN_CHUNK)
        tri = causal.astype(f32)
        last_row = lax.broadcasted_iota(jnp.int32, (HGRN_CHUNK, 1), 0) == HGRN_CHUNK - 1
        for c in range(nch):
            rows = slice(HGRN_CHUNK * c, HGRN_CHUNK * (c + 1))
            b = _dot_exact(tri, logf[rows])
            bl = jnp.sum(jnp.where(last_row, b, 0.0), axis=0, keepdims=True)
            qb = (qs[rows] * jnp.exp(b)).astype(bf16)
            kb = (kk[rows] * jnp.exp(-b)).astype(bf16)
            kd = (kk[rows] * jnp.exp(bl - b)).astype(bf16)
            ebl = jnp.exp(bl)
            vc = iv[rows].astype(bf16)
            for h in range(HEADS):
                sl = slice(HEAD_DIM * h, HEAD_DIM * (h + 1))
                st = st_scr[h]
                st_ref[c, h] = st
                a = jnp.where(causal, _dot_nt(qb[:, sl], kb[:, sl]), 0.0)
                opre_ref[rows, sl] = _dot(a.astype(bf16), vc[:, sl]) + _dot_nt(qb[:, sl], st.astype(bf16))
                st_scr[h] = st * ebl[:, sl] + _dot_tn(vc[:, sl], kd[:, sl])

        o = opre_ref[...]
        go = go_ref[...]
        gout_v = gout_ref[...]
        for h in range(HEADS):
            sl = slice(HEAD_DIM * h, HEAD_DIM * (h + 1))
            _, oh = _rms_stats(o[:, sl])
            z_ref[:, sl] = (oh * gout_v[:, sl] * _sigmoid(go[:, sl])).astype(bf16)

        zc = cg_ref[...] * xc_ref[...]
        zbuf[8:8 + tt, :] = zc
        y = wconv_ref[0:1, :] * zbuf[pl.ds(6, tt), :] + wconv_ref[1:2, :] * zbuf[pl.ds(7, tt), :] + wconv_ref[2:3, :] * zc
        z_ref[:, BRANCH:2 * BRANCH] = (bg_ref[...] * y).astype(bf16)
        zbuf[0:8, :] = zbuf[tt:tt + 8, :]

        ug = _gelu(u_ref[...])
        vg = _gelu(v_ref[...])
        vcen = vg - jnp.mean(vg, axis=-1, keepdims=True)
        rstd = lax.rsqrt(jnp.mean(vcen * vcen, axis=-1, keepdims=True) + LN_EPS)
        vn = (vcen * rstd * lng_ref[...] + lnb_ref[...]).astype(bf16)
        low = _tri(SG_CHUNK)
        for g in range(SG_GROUPS):
            sl = slice(LANE * g, LANE * (g + 1))
            wm = jnp.where(low, wsg_ref[g], 0.0).astype(bf16)
            bias = bsg_ref[:, g:g + 1]
            for cc in range(tt // SG_CHUNK):
                rows = slice(SG_CHUNK * cc, SG_CHUNK * (cc + 1))
                sv = _dot(wm, vn[rows, sl]) + bias
                z_ref[rows, 2 * BRANCH + LANE * g:2 * BRANCH + LANE * (g + 1)] = (ug[rows, sl] * sv).astype(bf16)

    full = lambda shape: pl.BlockSpec(shape, lambda i: (0,) * len(shape))
    return pl.pallas_call(
        body, name="mixer_fwd", grid=(s // tt,),
        in_specs=_p_specs(tt, range(9), lambda i: i) + [full((1, BRANCH)), full((1, BRANCH)), full((3, BRANCH)), full((1, BRANCH)),
                                                        full((1, BRANCH)), full((SG_GROUPS, SG_CHUNK, SG_CHUNK)), full((SG_CHUNK, SG_GROUPS))],
        out_specs=[pl.BlockSpec((tt, 3 * BRANCH), lambda i: (i, 0)), pl.BlockSpec((tt, BRANCH), lambda i: (i, 0)),
                   pl.BlockSpec((nch, HEADS, HEAD_DIM, HEAD_DIM), lambda i: (i, 0, 0, 0))],
        out_shape=[SDS((s, 3 * BRANCH), bf16), SDS((s, BRANCH), f32), SDS((s // HGRN_CHUNK, HEADS, HEAD_DIM, HEAD_DIM), f32)],
        scratch_shapes=[pltpu.VMEM((HEADS, HEAD_DIM, HEAD_DIM), f32), pltpu.VMEM((tt + 8, BRANCH), f32)],
        compiler_params=_cp("arbitrary"),
    )(*([p] * 9), lb, gout, wconv, lng, lnb, wsg, bsg_t)


def _branch_gate(z, wb, p, tm=256):
    s = z.shape[0]
    half = 3 * D_MODEL // 2

    def body(z_ref, wb_ref, ga_ref, gb_ref, y_ref, m_ref):
        ga, gb = ga_ref[...], gb_ref[...]
        gates = [ga[:, :D_MODEL], jnp.concatenate([ga[:, D_MODEL:], gb[:, :D_MODEL // 2]], axis=1), gb[:, D_MODEL // 2:]]
        acc = None
        for n in range(3):
            yn = _dot(z_ref[:, BRANCH * n:BRANCH * (n + 1)], wb_ref[n])
            y_ref[:, D_MODEL * n:D_MODEL * (n + 1)] = yn.astype(bf16)
            t = _sigmoid(gates[n]) * yn
            acc = t if acc is None else acc + t
        m_ref[...] = acc.astype(bf16)

    blk0 = GATE_COL0 // half
    return pl.pallas_call(
        body, name="branch_gate", grid=(s // tm,),
        in_specs=[pl.BlockSpec((tm, 3 * BRANCH), lambda i: (i, 0)), pl.BlockSpec((3, BRANCH, D_MODEL), lambda i: (0, 0, 0)),
                  pl.BlockSpec((tm, half), lambda i: (i, blk0)), pl.BlockSpec((tm, half), lambda i: (i, blk0 + 1))],
        out_specs=[pl.BlockSpec((tm, 3 * D_MODEL), lambda i: (i, 0)), pl.BlockSpec((tm, D_MODEL), lambda i: (i, 0))],
        out_shape=[SDS((s, 3 * D_MODEL), bf16), SDS((s, D_MODEL), bf16)], compiler_params=_cp("parallel"),
    )(z, wb, p, p)


def _mm_resid(x, m, wo, tm=512):
    s = x.shape[0]

    def body(x_ref, m_ref, w_ref, o_ref):
        o_ref[...] = x_ref[...] + _dot(m_ref[...], w_ref[...])

    return pl.pallas_call(
        body, name="mm_resid", grid=(s // tm,),
        in_specs=[pl.BlockSpec((tm, D_MODEL), lambda i: (i, 0)), pl.BlockSpec((tm, D_MODEL), lambda i: (i, 0)),
                  pl.BlockSpec((D_MODEL, D_MODEL), lambda i: (0, 0))],
        out_specs=pl.BlockSpec((tm, D_MODEL), lambda i: (i, 0)), out_shape=SDS((s, D_MODEL), f32), compiler_params=_cp("parallel"),
    )(x, m, wo)


def _ffn(x1, g, w1, w2, tm=512, tf=1024):
    s = x1.shape[0]
    nf = D_FF // tf

    def body(x_ref, g_ref, w1_ref, w2_ref, o_ref, h_ref, ra_ref, hs, acc):
        f = pl.program_id(1)

        @pl.when(f == 0)
        def _():
            _, xh = _rms_stats(x_ref[...])
            hv = (xh * g_ref[...]).astype(bf16)
            hs[...] = hv
            h_ref[...] = hv
            acc[...] = jnp.zeros_like(acc)

        ra = jnp.maximum(_dot(hs[...], w1_ref[...]), 0.0)
        ra_ref[...] = ra.astype(bf16)
        acc[...] += _dot((ra * ra).astype(bf16), w2_ref[...])

        @pl.when(f == nf - 1)
        def _():
            o_ref[...] = x_ref[...] + acc[...]

    return pl.pallas_call(
        body, name="ffn", grid=(s // tm, nf),
        in_specs=[pl.BlockSpec((tm, D_MODEL), lambda i, f: (i, 0)), pl.BlockSpec((1, D_MODEL), lambda i, f: (0, 0)),
                  pl.BlockSpec((D_MODEL, tf), lambda i, f: (0, f)), pl.BlockSpec((tf, D_MODEL), lambda i, f: (f, 0))],
        out_specs=[pl.BlockSpec((tm, D_MODEL), lambda i, f: (i, 0)), pl.BlockSpec((tm, D_MODEL), lambda i, f: (i, 0)),
                   pl.BlockSpec((tm, tf), lambda i, f: (i, f))],
        out_shape=[SDS((s, D_MODEL), f32), SDS((s, D_MODEL), bf16), SDS((s, D_FF), bf16)],
        scratch_shapes=[pltpu.VMEM((tm, D_MODEL), bf16), pltpu.VMEM((tm, D_MODEL), f32)], compiler_params=_cp("parallel", "arbitrary"),
    )(x1, g, w1, w2)


def _final(x, target, g, tm=512):
    s = x.shape[0]

    def body(x_ref, t_ref, g_ref, loss_ref, dx_ref, dxb_ref, dg_ref):
        first = pl.program_id(0) == 0
        gv = g_ref[...]
        r, xh = _rms_stats(x_ref[...])
        e = xh * gv - t_ref[...]
        tile_loss = 0.5 * jnp.sum(jnp.mean(e * e, axis=-1, keepdims=True), axis=0, keepdims=True)
        dx, dg = _rms_bwd(e * (1.0 / D_MODEL), xh, r, gv)
        dx_ref[...] = dx
        dxb_ref[...] = dx.astype(bf16)
        _acc_rows(dg_ref, first, dg)
        _acc_rows(loss_ref, first, jnp.broadcast_to(tile_loss, (1, LANE)))

    row = pl.BlockSpec((tm, D_MODEL), lambda i: (i, 0))
    return pl.pallas_call(
        body, name="final_loss", grid=(s // tm,), in_specs=[row, row, pl.BlockSpec((1, D_MODEL), lambda i: (0, 0))],
        out_specs=[pl.BlockSpec((1, LANE), lambda i: (0, 0)), row, row, pl.BlockSpec((1, D_MODEL), lambda i: (0, 0))],
        out_shape=[SDS((1, LANE), f32), SDS((s, D_MODEL), f32), SDS((s, D_MODEL), bf16), SDS((1, D_MODEL), f32)],
        compiler_params=_cp("arbitrary"),
    )(x, target, g)


def _ffn_bwd(dx2, dx2b, x1, g, ra, w1, w2, tm=512, tf=1024):
    s = x1.shape[0]
    nf = D_FF // tf

    def body(dx_ref, dxb_ref, x_ref, g_ref, ra_ref, w1_ref, w2_ref, da_ref, act_ref, dx1_ref, dx1b_ref, dg_ref, acc):
        i, f = pl.program_id(0), pl.program_id(1)

        @pl.when(f == 0)
        def _():
            acc[...] = jnp.zeros_like(acc)

        rav = ra_ref[...].astype(f32)
        da = (_dot_nt(dxb_ref[...], w2_ref[...]) * (2.0 * rav)).astype(bf16)
        da_ref[...] = da
        act_ref[...] = (rav * rav).astype(bf16)
        acc[...] += _dot_nt(da, w1_ref[...])

        @pl.when(f == nf - 1)
        def _():
            r, xh = _rms_stats(x_ref[...])
            dx, dg = _rms_bwd(acc[...], xh, r, g_ref[...])
            dx = dx + dx_ref[...]
            dx1_ref[...] = dx
            dx1b_ref[...] = dx.astype(bf16)
            _acc_rows(dg_ref, i == 0, dg)

    row = pl.BlockSpec((tm, D_MODEL), lambda i, f: (i, 0))
    col = pl.BlockSpec((tm, tf), lambda i, f: (i, f))
    return pl.pallas_call(
        body, name="ffn_bwd", grid=(s // tm, nf),
        in_specs=[row, row, row, pl.BlockSpec((1, D_MODEL), lambda i, f: (0, 0)), col,
                  pl.BlockSpec((D_MODEL, tf), lambda i, f: (0, f)), pl.BlockSpec((tf, D_MODEL), lambda i, f: (f, 0))],
        out_specs=[col, col, row, row, pl.BlockSpec((1, D_MODEL), lambda i, f: (0, 0))],
        out_shape=[SDS((s, D_FF), bf16), SDS((s, D_FF), bf16), SDS((s, D_MODEL), f32), SDS((s, D_MODEL), bf16), SDS((1, D_MODEL), f32)],
        scratch_shapes=[pltpu.VMEM((tm, D_MODEL), f32)], compiler_params=_cp("arbitrary", "arbitrary"),
    )(dx2, dx2b, x1, g, ra, w1, w2)


def _mm_tn(a, b, nb, m, n, tm, tn, ts=512, name="mm_tn"):
    s = a.shape[0]
    mi, nj, ns = m // tm, n // tn, s // ts

    def body(a_ref, b_ref, o_ref, acc):
        t = pl.program_id(3)

        @pl.when(t == 0)
        def _():
            acc[...] = jnp.zeros_like(acc)

        acc[...] += _dot_tn(a_ref[...], b_ref[...])

        @pl.when(t == ns - 1)
        def _():
            o_ref[...] = acc[...].astype(bf16)

    return pl.pallas_call(
        body, name=name, grid=(nb, mi, nj, ns),
        in_specs=[pl.BlockSpec((ts, tm), lambda k, i, j, t: (t, k * mi + i)), pl.BlockSpec((ts, tn), lambda k, i, j, t: (t, k * nj + j))],
        out_specs=pl.BlockSpec((None, tm, tn), lambda k, i, j, t: (k, i, j)), out_shape=SDS((nb, m, n), bf16),
        scratch_shapes=[pltpu.VMEM((tm, tn), f32)], compiler_params=_cp("parallel", "parallel", "parallel", "arbitrary"),
    )(a, b)


def _mm_tn_slabs(a, b, nb, m, nblk, rel, width, tm=512, ts=512, name="mm_tn_slabs"):
    s = a.shape[0]
    n = b.shape[1] // nb
    ng, mi, ns, nw = n // nblk, m // tm, s // ts, len(rel)

    def body(a_ref, b_ref, o_ref, acc):
        t = pl.program_id(3)

        @pl.when(t == 0)
        def _():
            acc[...] = jnp.zeros_like(acc)

        acc[...] += _dot_tn(a_ref[...], b_ref[...])

        @pl.when(t == ns - 1)
        def _():
            for r, start in enumerate(rel):
                o_ref[r] = acc[:, start:start + width].astype(bf16)

    return pl.pallas_call(
        body, name=name, grid=(nb, ng, mi, ns),
        in_specs=[pl.BlockSpec((ts, tm), lambda k, g, i, t: (t, k * mi + i)), pl.BlockSpec((ts, nblk), lambda k, g, i, t: (t, k * ng + g))],
        out_specs=pl.BlockSpec((nw, None, tm, width), lambda k, g, i, t: (g, k, i, 0)), out_shape=SDS((ng * nw, nb, m, width), bf16),
        scratch_shapes=[pltpu.VMEM((tm, nblk), f32)], compiler_params=_cp("parallel", "parallel", "parallel", "arbitrary"),
    )(a, b)


def _merge_bwd(dx1b, wo, y, p, wb, after, tm=256):
    s = dx1b.shape[0]
    hw = D_MODEL // 2
    gblk = GATE_COL0 // hw

    def body(dx_ref, wo_ref, y_ref, gt_ref, wb_ref, after_ref, dy_ref, dp_ref, dz_ref, dm_scr, dz_acc):
        del after_ref
        k = pl.program_id(1)
        hf = k % 2

        @pl.when(k == 0)
        def _():
            dm = _dot_nt(dx_ref[...], wo_ref[...])
            dm_scr[0] = dm[:, :hw]
            dm_scr[1] = dm[:, hw:]

        dmh = dm_scr[hf]
        gate = _sigmoid(gt_ref[...])
        dy = (dmh * gate).astype(bf16)
        dy_ref[...] = dy
        dp_ref[...] = (dmh * y_ref[...].astype(f32) * gate * (1.0 - gate)).astype(bf16)
        part = _dot_nt(dy, wb_ref[...])

        @pl.when(hf == 0)
        def _():
            dz_acc[...] = part

        @pl.when(hf == 1)
        def _():
            dz_ref[...] = dz_acc[...] + part

    return pl.pallas_call(
        body, name="merge_bwd", grid=(s // tm, 6),
        in_specs=[pl.BlockSpec((tm, D_MODEL), lambda i, k: (i, 0)), pl.BlockSpec((D_MODEL, D_MODEL), lambda i, k: (0, 0)),
                  pl.BlockSpec((tm, hw), lambda i, k: (i, k)), pl.BlockSpec((tm, hw), lambda i, k: (i, gblk + k)),
                  pl.BlockSpec((None, BRANCH, hw), lambda i, k: (k // 2, 0, k % 2)), pl.BlockSpec(memory_space=pl.ANY)],
        out_specs=[pl.BlockSpec((tm, hw), lambda i, k: (i, k)), pl.BlockSpec((tm, hw), lambda i, k: (i, gblk + k)),
                   pl.BlockSpec((tm, BRANCH), lambda i, k: (i, k // 2))],
        out_shape=[SDS((s, 3 * D_MODEL), bf16), SDS((s, N_COLS), bf16), SDS((s, 3 * BRANCH), f32)],
        scratch_shapes=[pltpu.VMEM((2, tm, hw), f32), pltpu.VMEM((tm, BRANCH), f32)], compiler_params=_cp("parallel", "arbitrary"),
    )(dx1b, wo, y, p, wb, after)


def _mixer_bwd(p, dz, opre, states, dp, lb, gout, wconv, lng, lnb, wsg, bsg_t):
    s = p.shape[0]
    tt = MIX_TILE
    nt = s // tt
    nch = tt // HGRN_CHUNK
    rev = lambda i: nt - 1 - i

    def body(q_ref, fp_ref, iv_ref, go_ref, bg_ref, cg_ref, xc_ref, u_ref, v_ref, cgp_ref, xcp_ref, dz_ref, opre_ref, st_ref,
             dp_in, lb_ref, gout_ref, wconv_ref, lng_ref, lnb_ref, wsg_ref, bsg_ref,
             dp_ref, vec_ref, dwsg_ref, dbsg_ref, dst_scr, zbuf, dybuf, dvn_scr, dbsg_acc):
        del dp_in
        i = pl.program_id(0)

        @pl.when(i == 0)
        def _():
            dst_scr[...] = jnp.zeros_like(dst_scr)
            dybuf[tt:tt + 8, :] = jnp.zeros((8, BRANCH), f32)
            vec_ref[...] = jnp.zeros_like(vec_ref)
            dwsg_ref[...] = jnp.zeros_like(dwsg_ref)
            dbsg_acc[...] = jnp.zeros_like(dbsg_acc)

        lbv = lb_ref[...]
        q_raw, fp = q_ref[...], fp_ref[...]
        sq = _sigmoid(q_raw)
        qs = q_raw * sq
        sfp = _sigmoid(fp)
        logf, snf, kk = _hgrn_gates(fp, lbv)
        inv_f = jnp.exp(-logf)
        iv = iv_ref[...]
        doa = dz_ref[:, 0:BRANCH]
        o = opre_ref[...]
        sgo = _sigmoid(go_ref[...])
        gout_v = gout_ref[...]
        d_o, dgo, dgout = [], [], []
        for h in range(HEADS):
            sl = slice(HEAD_DIM * h, HEAD_DIM * (h + 1))
            r, oh = _rms_stats(o[:, sl])
            d_on = doa[:, sl] * sgo[:, sl]
            dgo.append(doa[:, sl] * oh * gout_v[:, sl] * sgo[:, sl] * (1.0 - sgo[:, sl]))
            dx, dg = _rms_bwd(d_on, oh, r, gout_v[:, sl])
            d_o.append(dx)
            dgout.append(dg)
        d_o = jnp.concatenate(d_o, axis=1)
        dp_ref[:, 3 * BRANCH:4 * BRANCH] = jnp.concatenate(dgo, axis=1).astype(bf16)
        vec_ref[1:2, :] += jnp.concatenate(dgout, axis=1)

        causal = _tri(HGRN_CHUNK)
        tri = causal.astype(f32)
        tri_up = _tri(HGRN_CHUNK, upper=True).astype(f32)
        last_row = lax.broadcasted_iota(jnp.int32, (HGRN_CHUNK, 1), 0) == HGRN_CHUNK - 1
        lb_live = (lbv > LB_FLOOR).astype(f32)
        dlb = jnp.zeros((1, BRANCH), f32)
        for c in reversed(range(nch)):
            rows = slice(HGRN_CHUNK * c, HGRN_CHUNK * (c + 1))
            b = _dot_exact(tri, logf[rows])
            bl = jnp.sum(jnp.where(last_row, b, 0.0), axis=0, keepdims=True)
            eb, enb, edl, ebl = jnp.exp(b), jnp.exp(-b), jnp.exp(bl - b), jnp.exp(bl)
            qbf, kbf, kdf = qs[rows] * eb, kk[rows] * enb, kk[rows] * edl
            qb, kb, kd = qbf.astype(bf16), kbf.astype(bf16), kdf.astype(bf16)
            vc = iv[rows].astype(bf16)
            dob = d_o[rows].astype(bf16)
            dv, dqb, dkb, dkd, debl = [], [], [], [], []
            for h in range(HEADS):
                sl = slice(HEAD_DIM * h, HEAD_DIM * (h + 1))
                st = st_ref[c, h]
                dst = dst_scr[h]
                stb, dstb = st.astype(bf16), dst.astype(bf16)
                a = jnp.where(causal, _dot_nt(qb[:, sl], kb[:, sl]), 0.0).astype(bf16)
                da = jnp.where(causal, _dot_nt(dob[:, sl], vc[:, sl]), 0.0).astype(bf16)
                dv.append(_dot_tn(a, dob[:, sl]) + _dot_nt(kd[:, sl], dstb))
                dqb.append(_dot(dob[:, sl], stb) + _dot(da, kb[:, sl]))
                dkb.append(_dot_tn(da, qb[:, sl]))
                dkd.append(_dot(vc[:, sl], dstb))
                debl.append(jnp.sum(st * dst, axis=0, keepdims=True))
                dst_scr[h] = _dot_tn(dob[:, sl], qb[:, sl]) + dst * ebl[:, sl]
            dv, dqb, dkb, dkd = (jnp.concatenate(t, axis=1) for t in (dv, dqb, dkb, dkd))
            debl = jnp.concatenate(debl, axis=1)
            t_kd = dkd * kdf
            dbl = ebl * debl + jnp.sum(t_kd, axis=0, keepdims=True)
            db = dqb * qbf - dkb * kbf - t_kd + jnp.where(last_row, dbl, 0.0)
            dkk = dkb * enb + dkd * edl
            dlc = _dot_exact(tri_up, db)
            sq_c, q_c, sfp_c, snf_c, invf_c = sq[rows], q_raw[rows], sfp[rows], snf[rows], inv_f[rows]
            slope = (1.0 - lbv) * sfp_c * snf_c
            dp_ref[rows, 0:BRANCH] = (dqb * eb * sq_c * (1.0 + q_c * (1.0 - sq_c))).astype(bf16)
            dp_ref[rows, BRANCH:2 * BRANCH] = (slope * (dlc * invf_c - dkk)).astype(bf16)
            dp_ref[rows, 2 * BRANCH:3 * BRANCH] = dv.astype(bf16)
            dlb = dlb + jnp.sum(dlc * (lb_live - sfp_c) * invf_c - dkk * snf_c, axis=0, keepdims=True)
        vec_ref[0:1, :] += dlb

        dob_ = dz_ref[:, BRANCH:2 * BRANCH]
        bg, cg, xc = bg_ref[...], cg_ref[...], xc_ref[...]
        zc = cg * xc
        zbuf[0:8, :] = jnp.where(i < nt - 1, cgp_ref[...] * xcp_ref[...], 0.0)
        zbuf[8:8 + tt, :] = zc
        w0, w1, w2 = wconv_ref[0:1, :], wconv_ref[1:2, :], wconv_ref[2:3, :]
        y = w0 * zbuf[pl.ds(6, tt), :] + w1 * zbuf[pl.ds(7, tt), :] + w2 * zc
        dy = dob_ * bg
        dybuf[0:tt, :] = dy
        dy1, dy2 = dybuf[pl.ds(1, tt), :], dybuf[pl.ds(2, tt), :]
        dzc = w2 * dy + w1 * dy1 + w0 * dy2
        dp_ref[:, 4 * BRANCH:5 * BRANCH] = (dob_ * y).astype(bf16)
        dp_ref[:, 5 * BRANCH:6 * BRANCH] = (dzc * xc).astype(bf16)
        dp_ref[:, 6 * BRANCH:7 * BRANCH] = (dzc * cg).astype(bf16)
        vec_ref[4:5, :] += jnp.sum(zc * dy2, axis=0, keepdims=True)
        vec_ref[5:6, :] += jnp.sum(zc * dy1, axis=0, keepdims=True)
        vec_ref[6:7, :] += jnp.sum(zc * dy, axis=0, keepdims=True)
        dybuf[tt:tt + 8, :] = dybuf[0:8, :]

        doc = dz_ref[:, 2 * BRANCH:3 * BRANCH]
        u_raw, v_raw = u_ref[...], v_ref[...]
        ug = _gelu(u_raw)
        dug_scale = _gelu_grad(u_raw)
        vg = _gelu(v_raw)
        vcen = vg - jnp.mean(vg, axis=-1, keepdims=True)
        rstd = lax.rsqrt(jnp.mean(vcen * vcen, axis=-1, keepdims=True) + LN_EPS)
        vhat = vcen * rstd
        lng_v = lng_ref[...]
        vn = (vhat * lng_v + lnb_ref[...]).astype(bf16)
        low = _tri(SG_CHUNK)
        for g in range(SG_GROUPS):
            sl = slice(LANE * g, LANE * (g + 1))
            wm = jnp.where(low, wsg_ref[g], 0.0).astype(bf16)
            bias = bsg_ref[:, g:g + 1]
            dw = jnp.zeros((SG_CHUNK, SG_CHUNK), f32)
            dbs = jnp.zeros((SG_CHUNK, LANE), f32)
            for cc in range(tt // SG_CHUNK):
                rows = slice(SG_CHUNK * cc, SG_CHUNK * (cc + 1))
                vn_c = vn[rows, sl]
                sv = _dot(wm, vn_c) + bias
                doc_c = doc[rows, sl]
                dp_ref[rows, 7 * BRANCH + LANE * g:7 * BRANCH + LANE * (g + 1)] = (doc_c * sv * dug_scale[rows, sl]).astype(bf16)
                dsv = doc_c * ug[rows, sl]
                dsvb = dsv.astype(bf16)
                dbs = dbs + dsv
                dw = dw + _dot_nt(dsvb, vn_c)
                dvn_scr[rows, sl] = _dot_tn(wm, dsvb)
            dwsg_ref[g] += jnp.where(low, dw, 0.0)
            dbsg_acc[:, sl] += dbs
        dvn = dvn_scr[...]
        vec_ref[2:3, :] += jnp.sum(dvn * vhat, axis=0, keepdims=True)
        vec_ref[3:4, :] += jnp.sum(dvn, axis=0, keepdims=True)
        dvh = dvn * lng_v
        dvg = rstd * (dvh - jnp.mean(dvh, axis=-1, keepdims=True) - vhat * jnp.mean(dvh * vhat, axis=-1, keepdims=True))
        dp_ref[:, 8 * BRANCH:9 * BRANCH] = (dvg * _gelu_grad(v_raw)).astype(bf16)

        @pl.when(i == nt - 1)
        def _():
            for g in range(SG_GROUPS):
                dbsg_ref[:, g:g + 1] = jnp.sum(dbsg_acc[:, LANE * g:LANE * (g + 1)], axis=1, keepdims=True)

    full = lambda shape: pl.BlockSpec(shape, lambda i: (0,) * len(shape))
    tail = lambda c: pl.BlockSpec((8, BRANCH), lambda i: (jnp.maximum(rev(i) * (tt // 8) - 1, 0), c))
    return pl.pallas_call(
        body, name="mixer_bwd", grid=(nt,),
        in_specs=_p_specs(tt, range(9), rev) + [tail(5), tail(6), pl.BlockSpec((tt, 3 * BRANCH), lambda i: (rev(i), 0)),
                                                pl.BlockSpec((tt, BRANCH), lambda i: (rev(i), 0)),
                                                pl.BlockSpec((nch, HEADS, HEAD_DIM, HEAD_DIM), lambda i: (rev(i), 0, 0, 0)),
                                                pl.BlockSpec(memory_space=pl.ANY),
                                                full((1, BRANCH)), full((1, BRANCH)), full((3, BRANCH)), full((1, BRANCH)), full((1, BRANCH)),
                                                full((SG_GROUPS, SG_CHUNK, SG_CHUNK)), full((SG_CHUNK, SG_GROUPS))],
        out_specs=[pl.BlockSpec((tt, 9 * BRANCH), lambda i: (rev(i), 0)), full((8, BRANCH)), full((SG_GROUPS, SG_CHUNK, SG_CHUNK)),
                   full((SG_CHUNK, SG_GROUPS))],
        out_shape=[SDS((s, N_COLS), bf16), SDS((8, BRANCH), f32), SDS((SG_GROUPS, SG_CHUNK, SG_CHUNK), f32), SDS((SG_CHUNK, SG_GROUPS), f32)],
        scratch_shapes=[pltpu.VMEM((HEADS, HEAD_DIM, HEAD_DIM), f32), pltpu.VMEM((tt + 8, BRANCH), f32), pltpu.VMEM((tt + 8, BRANCH), f32),
                        pltpu.VMEM((tt, BRANCH), f32), pltpu.VMEM((SG_CHUNK, BRANCH), f32)],
        input_output_aliases={14: 0}, compiler_params=_cp("arbitrary"),
    )(*([p] * 11), dz, opre, states, dp, lb, gout, wconv, lng, lnb, wsg, bsg_t)


def _dh_bwd(dp, w_in, x, dx1, g, after, tm=512, tk=1536):
    s = x.shape[0]
    nk = N_COLS // tk

    def body(dp_ref, w_ref, x_ref, dx1_ref, g_ref, after_ref, dx_ref, dxb_ref, dg_ref, acc):
        del after_ref
        i, k = pl.program_id(0), pl.program_id(1)

        @pl.when(k == 0)
        def _():
            acc[...] = jnp.zeros_like(acc)

        acc[...] += _dot_nt(dp_ref[...], w_ref[...])

        @pl.when(k == nk - 1)
        def _():
            r, xh = _rms_stats(x_ref[...])
            dx, dg = _rms_bwd(acc[...], xh, r, g_ref[...])
            dx = dx + dx1_ref[...]
            dx_ref[...] = dx
            dxb_ref[...] = dx.astype(bf16)
            _acc_rows(dg_ref, i == 0, dg)

    row = pl.BlockSpec((tm, D_MODEL), lambda i, k: (i, 0))
    vec = pl.BlockSpec((1, D_MODEL), lambda i, k: (0, 0))
    return pl.pallas_call(
        body, name="dh_bwd", grid=(s // tm, nk),
        in_specs=[pl.BlockSpec((tm, tk), lambda i, k: (i, k)), pl.BlockSpec((D_MODEL, tk), lambda i, k: (0, k)), row, row, vec,
                  pl.BlockSpec(memory_space=pl.ANY)],
        out_specs=[row, row, vec], out_shape=[SDS((s, D_MODEL), f32), SDS((s, D_MODEL), bf16), SDS((1, D_MODEL), f32)],
        scratch_shapes=[pltpu.VMEM((tm, D_MODEL), f32)], compiler_params=_cp("arbitrary", "arbitrary"),
    )(dp, w_in, x, dx1, g, after)


def _layer_fwd(x, w, sm):
    p, h = _rms_mm(x, sm["g_mix"], w["w_in"])
    z, opre, states = _mixer_fwd(p, sm["lb"], sm["g_out"], sm["w_conv"], sm["ln_g"], sm["ln_b"], sm["w_sg"], sm["b_sg_t"])
    y, merged = _branch_gate(z, w["w_branch"], p)
    x1 = _mm_resid(x, merged, w["w_o"])
    x2, h2, ra = _ffn(x1, sm["g_ffn"], w["w_ff1"], w["w_ff2"])
    saved = dict(x=x, p=p, h=h, z=z, opre=opre, states=states, y=y, merged=merged, x1=x1, h2=h2, ra=ra)
    return x2, saved


def _layer_bwd(dx2, dx2b, sv, w, sm, between, before_end):
    nchip = N_DEV // 2
    by_chip = lambda g: g.reshape((nchip, 2) + g.shape[1:])
    da, act, dx1, dx1b, dg_ffn = _ffn_bwd(dx2, dx2b, sv["x1"], sm["g_ffn"], sv["ra"], w["w_ff1"], w["w_ff2"])
    g_ff2 = by_chip(_mm_tn(act, dx2b, 1, D_FF, D_MODEL, 512, 1024, name="dw_ff2")[0].reshape(N_DEV, D_FF // N_DEV, D_MODEL))
    g_ff1 = by_chip(_mm_tn_slabs(sv["h2"], da, 1, D_MODEL, D_FF // 2, [i * (D_FF // N_DEV) for i in range(nchip)], D_FF // N_DEV,
                                 name="dw_ff1")[:, 0])
    g_o = by_chip(_mm_tn(sv["merged"], dx1b, 1, D_MODEL, D_MODEL, 512, 1024, name="dw_o")[0].reshape(N_DEV, D_MODEL // N_DEV, D_MODEL))
    dy, dp, dz = _merge_bwd(dx1b, w["w_o"], sv["y"], sv["p"], w["w_branch"], between(dx1))
    g_branch = by_chip(_mm_tn_slabs(sv["z"], dy, 3, BRANCH, D_MODEL, [i * (D_MODEL // N_DEV) for i in range(N_DEV)], D_MODEL // N_DEV,
                                    name="dw_branch"))
    dp, vecs, dwsg, dbsg_t = _mixer_bwd(sv["p"], dz, sv["opre"], sv["states"], dp, sm["lb"], sm["g_out"], sm["w_conv"],
                                        sm["ln_g"], sm["ln_b"], sm["w_sg"], sm["b_sg_t"])
    starts = [SHARD_IN * i - (WIN - SHARD_IN) * (i % 2) for i in range(nchip)]
    g_in = by_chip(_mm_tn_slabs(sv["h"], dp, 1, D_MODEL, N_COLS // 2, starts, WIN, name="dw_in")[:, 0])
    dx, dxb, dg_mix = _dh_bwd(dp, w["w_in"], sv["x"], dx1, sm["g_mix"], before_end(g_in))
    big = dict(w_in=g_in, w_branch=g_branch, w_o=g_o, w_ff1=g_ff1, w_ff2=g_ff2)
    small = dict(g_mix=dg_mix, g_ffn=dg_ffn, vecs=vecs, w_sg=dwsg, b_sg_t=dbsg_t, dx1=dx1)
    return dx, dxb, big, small


BIG = ("w_in", "w_branch", "w_o", "w_ff1", "w_ff2")
ANY = pl.BlockSpec(memory_space=pl.ANY)


def _place():
    return lax.axis_index("x"), lax.axis_index("y"), lax.axis_index("c")


def _al(v, m):
    return pl.multiple_of(v * m, m)


def _shard_of(refs, dev):
    w_in, w_b, w_o, w_1, w_2 = refs
    nb, no, n1, n2 = w_b.shape[-1] // N_DEV, w_o.shape[0] // N_DEV, w_1.shape[-1] // N_DEV, w_2.shape[0] // N_DEV
    return [w_in, w_b.at[:, :, pl.ds(_al(dev, nb), nb)], w_o.at[pl.ds(_al(dev, no), no), :],
            w_1.at[:, pl.ds(_al(dev, n1), n1)], w_2.at[pl.ds(_al(dev, n2), n2), :]]


def _all_gather_layer(layer, shards):
    s_in, s_b, s_o, s_1, s_2 = (shards[n] for n in BIG)
    out_shape = [SDS((N_DEV,) + s_in.shape[1:], bf16), SDS(s_b.shape[1:3] + (s_b.shape[3] * N_DEV,), bf16),
                 SDS((s_o.shape[1] * N_DEV, s_o.shape[2]), bf16), SDS((s_1.shape[1], s_1.shape[2] * N_DEV), bf16),
                 SDS((s_2.shape[1] * N_DEV, s_2.shape[2]), bf16)]
    nt = len(BIG)

    def body(i_in, i_b, i_o, i_1, i_2, o_in, o_b, o_o, o_1, o_2, send_sems, recv_sems, local_sems):
        x, y, c = _place()
        me, sibling = (x, y, c), (x, y, 1 - c)
        chips = [(1 - x, y), (x, 1 - y), (1 - x, 1 - y)]
        mine = [r.at[layer] for r in (i_in, i_b, i_o, i_1, i_2)]

        def block(px, py, pc):
            dev = 4 * px + 2 * py + pc
            return [o_in.at[dev]] + _shard_of((None, o_b, o_o, o_1, o_2), dev)[1:]

        def copies(k, blk, to, src=None):
            dst = block(*blk)
            src = dst if src is None else src
            return [pltpu.make_async_remote_copy(src_ref=src[t], dst_ref=dst[t], send_sem=send_sems.at[k, t], recv_sem=recv_sems.at[k, t],
                                                 device_id=to, device_id_type=MESH) for t in range(nt)]

        local = [pltpu.make_async_copy(mine[t], block(*me)[t], local_sems.at[t]) for t in range(nt)]
        for cp in local:
            cp.start()
        first = copies(0, me, sibling, src=mine)
        for j, chip in enumerate(chips):
            first += copies(1 + j, me, (*chip, c), src=mine)
        for cp in first:
            cp.start()
        passed = [copies(4 + j, (*chip, c), sibling) for j, chip in enumerate(chips)]
        for j, chip in enumerate(chips):
            for cp in copies(1 + j, (*chip, c), me):
                cp.wait_recv()
            for cp in passed[j]:
                cp.start()
        for cp in copies(0, sibling, me):
            cp.wait_recv()
        for j, chip in enumerate(chips):
            for cp in copies(4 + j, (*chip, 1 - c), me):
                cp.wait_recv()
        for cp in first + [cp for grp in passed for cp in grp]:
            cp.wait_send()
        for cp in local:
            cp.wait()

    return pl.pallas_call(
        body, name=f"all_gather_l{layer}", in_specs=[ANY] * nt, out_specs=[ANY] * nt, out_shape=out_shape,
        scratch_shapes=[pltpu.SemaphoreType.DMA((7, nt)), pltpu.SemaphoreType.DMA((7, nt)), pltpu.SemaphoreType.DMA((nt,))],
    )(s_in, s_b, s_o, s_1, s_2)


def _gather_out_shapes(shards):
    s_in, s_b, s_o, s_1, s_2 = (shards[n] for n in BIG)
    return [SDS((N_DEV,) + s_in.shape[1:], bf16), SDS(s_b.shape[1:3] + (s_b.shape[3] * N_DEV,), bf16),
            SDS((s_o.shape[1] * N_DEV, s_o.shape[2]), bf16), SDS((s_1.shape[1], s_1.shape[2] * N_DEV), bf16),
            SDS((s_2.shape[1] * N_DEV, s_2.shape[2]), bf16)]


def _seq_all_gather_layer(layer, shard_refs, out_shapes):
    nt = len(BIG)
    outs = [jax.empty_ref(sh, memory_space=pltpu.MemorySpace.HBM) for sh in out_shapes]

    @pl.kernel(mesh=plsc.ScalarSubcoreMesh(axis_name="seq", num_cores=1), name=f"seq_all_gather_l{layer}",
               scratch_types=(pltpu.SemaphoreType.DMA((7,)), pltpu.SemaphoreType.DMA((7,))),
               compiler_params=pltpu.CompilerParams(collective_id=1))
    def launch(send_sems, recv_sems):
        x, y, c = _place()
        me, sibling = (x, y, c), (x, y, 1 - c)
        chips = [(1 - x, y), (x, 1 - y), (1 - x, 1 - y)]
        peers = [sibling] + [(*chip, c) for chip in chips]
        barrier = pltpu.get_barrier_semaphore()
        for p in peers:
            pl.semaphore_signal(barrier, inc=1, device_id=p, device_id_type=MESH)
        pl.semaphore_wait(barrier, len(peers))
        mine = [r.at[layer] for r in shard_refs]
        o_in, o_b, o_o, o_1, o_2 = outs

        def block(px, py, pc):
            dev = 4 * px + 2 * py + pc
            return [o_in.at[dev]] + _shard_of((None, o_b, o_o, o_1, o_2), dev)[1:]

        def copies(k, blk, to, src=None):
            dst = block(*blk)
            src = dst if src is None else src
            return [pltpu.make_async_remote_copy(src_ref=src[t], dst_ref=dst[t], send_sem=send_sems.at[k], recv_sem=recv_sems.at[k],
                                                 device_id=to, device_id_type=MESH) for t in range(nt)]

        first = copies(0, me, sibling, src=mine)
        for j, chip in enumerate(chips):
            first += copies(1 + j, me, (*chip, c), src=mine)
        for cp in first:
            cp.start()
        passed = [copies(4 + j, (*chip, c), sibling) for j, chip in enumerate(chips)]
        for j, chip in enumerate(chips):
            for cp in copies(1 + j, (*chip, c), me):
                cp.wait_recv()
            for cp in passed[j]:
                cp.start()
        for cp in copies(0, sibling, me):
            cp.wait_recv()
        for j, chip in enumerate(chips):
            for cp in copies(4 + j, (*chip, 1 - c), me):
                cp.wait_recv()
        for cp in first + [cp for grp in passed for cp in grp]:
            cp.wait_send()

    launch()
    return [o[...] for o in outs]


def _place_own(where, shards, gathered, after):
    s_in, s_b, s_o, s_1, s_2 = (shards[n] for n in BIG)
    nt = len(BIG)

    def body(where_ref, *refs):
        del where_ref
        for src, dst in zip(refs[:nt], refs[2 * nt + 1:]):
            dst[...] = src[...]

    lay = lambda shape, fn: pl.BlockSpec(shape, fn)
    in_specs = [lay((None,) + s_in.shape[1:], lambda i, wh: (wh[0], 0, 0)), lay((None,) + s_b.shape[1:], lambda i, wh: (wh[0], 0, 0, 0)),
                lay((None,) + s_o.shape[1:], lambda i, wh: (wh[0], 0, 0)), lay((None,) + s_1.shape[1:], lambda i, wh: (wh[0], 0, 0)),
                lay((None,) + s_2.shape[1:], lambda i, wh: (wh[0], 0, 0))]
    out_specs = [lay((None,) + s_in.shape[1:], lambda i, wh: (wh[1], 0, 0)), lay(s_b.shape[1:], lambda i, wh: (0, 0, wh[1])),
                 lay(s_o.shape[1:], lambda i, wh: (wh[1], 0)), lay(s_1.shape[1:], lambda i, wh: (0, wh[1])),
                 lay(s_2.shape[1:], lambda i, wh: (wh[1], 0))]
    return pl.pallas_call(
        body, name="place_own", out_shape=[SDS(g.shape, g.dtype) for g in gathered],
        input_output_aliases={1 + nt + i: i for i in range(nt)}, compiler_params=_cp("arbitrary"),
        grid_spec=pltpu.PrefetchScalarGridSpec(num_scalar_prefetch=1, grid=(1,), in_specs=in_specs + [ANY] * (nt + 1), out_specs=out_specs),
    )(where, s_in, s_b, s_o, s_1, s_2, *gathered, after)


def _merge_windows(land, tr=512):
    r = land.shape[1]
    tr = min(tr, r)
    chip_cols = 2 * SHARD_IN
    cut = WIN - LANE

    def body(e_ref, o_ref, out_ref):
        out_ref[:, 0:cut] = e_ref[:, 0:cut]
        out_ref[:, cut:WIN] = e_ref[:, cut:WIN] + o_ref[:, 0:LANE]
        out_ref[:, WIN:chip_cols] = o_ref[:, LANE:WIN]

    return pl.pallas_call(
        body, name="merge_windows", grid=(N_DEV // 2, r // tr),
        in_specs=[pl.BlockSpec((None, tr, WIN), lambda k, i: (2 * k, i, 0)), pl.BlockSpec((None, tr, WIN), lambda k, i: (2 * k + 1, i, 0))],
        out_specs=pl.BlockSpec((tr, chip_cols), lambda k, i: (i, k)), out_shape=SDS((r, N_COLS), bf16),
        compiler_params=_cp("parallel", "parallel"),
    )(land, land)


def _exchange_on_chip(grads):
    nt, nchip = len(BIG), N_DEV // 2

    def body(*refs):
        g, landed = refs[:nt], refs[nt:2 * nt]
        send_sems, recv_sems = refs[2 * nt:]
        x, y, c = _place()
        remote = [pltpu.make_async_remote_copy(src_ref=g[t].at[j, 1 - c], dst_ref=landed[t].at[j], send_sem=send_sems.at[j, t],
                                               recv_sem=recv_sems.at[j, t], device_id=(x, y, 1 - c), device_id_type=MESH)
                  for j in range(nchip) for t in range(nt)]
        for cp in remote:
            cp.start()
        for cp in remote:
            cp.wait_recv()
        for cp in remote:
            cp.wait_send()

    return pl.pallas_call(
        body, name="rs_on_chip", in_specs=[ANY] * nt, out_specs=[ANY] * nt,
        out_shape=[SDS((nchip,) + grads[n].shape[2:], bf16) for n in BIG],
        scratch_shapes=[pltpu.SemaphoreType.DMA((nchip, nt)), pltpu.SemaphoreType.DMA((nchip, nt))],
    )(*(grads[n] for n in BIG))


def _handshake(peers, seq_sems=None):
    barrier = pltpu.get_barrier_semaphore()
    for p in peers:
        pl.semaphore_signal(barrier, inc=1, device_id=p, device_id_type=MESH)
    pl.semaphore_wait(barrier, len(peers))


def _seq_exchange_on_chip(grads):
    nt, nchip = len(BIG), N_DEV // 2
    g_refs = [jax.new_ref(g, memory_space=pltpu.MemorySpace.HBM) for g in grads]
    outs = [jax.empty_ref(SDS((nchip,) + g.shape[2:], bf16), memory_space=pltpu.MemorySpace.HBM) for g in grads]

    @pl.kernel(mesh=plsc.ScalarSubcoreMesh(axis_name="seq", num_cores=1), name="seq_rs_on_chip",
               scratch_types=(pltpu.SemaphoreType.DMA((nchip,)), pltpu.SemaphoreType.DMA((nchip,))),
               compiler_params=pltpu.CompilerParams(collective_id=2))
    def launch(send_sems, recv_sems):
        x, y, c = _place()
        sibling = (x, y, 1 - c)
        _handshake([sibling])
        remote = [pltpu.make_async_remote_copy(src_ref=g_refs[t].at[j, 1 - c], dst_ref=outs[t].at[j], send_sem=send_sems.at[j],
                                               recv_sem=recv_sems.at[j], device_id=sibling, device_id_type=MESH)
                  for j in range(nchip) for t in range(nt)]
        for cp in remote:
            cp.start()
        for cp in remote:
            cp.wait_recv()
        for cp in remote:
            cp.wait_send()

    launch()
    return [o[...] for o in outs], [g[...] for g in g_refs]


def _seq_exchange_between_chips(sums):
    nt = len(BIG)
    s_refs = [jax.new_ref(a, memory_space=pltpu.MemorySpace.HBM) for a in sums]
    outs = [jax.empty_ref(SDS((3,) + a.shape[1:], bf16), memory_space=pltpu.MemorySpace.HBM) for a in sums]

    @pl.kernel(mesh=plsc.ScalarSubcoreMesh(axis_name="seq", num_cores=1), name="seq_rs_between_chips",
               scratch_types=(pltpu.SemaphoreType.DMA((3,)), pltpu.SemaphoreType.DMA((3,))),
               compiler_params=pltpu.CompilerParams(collective_id=3))
    def launch(send_sems, recv_sems):
        x, y, c = _place()
        chips = [(1 - x, y), (x, 1 - y), (1 - x, 1 - y)]
        _handshake([(cx, cy, c) for cx, cy in chips])
        remote = [pltpu.make_async_remote_copy(src_ref=s_refs[t].at[2 * cx + cy], dst_ref=outs[t].at[k], send_sem=send_sems.at[k],
                                               recv_sem=recv_sems.at[k], device_id=(cx, cy, c), device_id_type=MESH)
                  for k, (cx, cy) in enumerate(chips) for t in range(nt)]
        for cp in remote:
            cp.start()
        for cp in remote:
            cp.wait_recv()
        for cp in remote:
            cp.wait_send()

    launch()
    return [o[...] for o in outs], [a[...] for a in s_refs]


def _chip_sums(core, mine, other, after, steps=2):
    nt, nchip = len(mine), mine[0].shape[0]
    m4 = [a.reshape(nchip, 2, -1, a.shape[-1]) for a in mine]
    o3 = [a.reshape(nchip, -1, a.shape[-1]) for a in other]

    def body(c_ref, *refs):
        del c_ref
        for a_ref, b_ref, o_ref in zip(refs[:nt], refs[nt:2 * nt], refs[2 * nt + 1:]):
            o_ref[...] = (a_ref[...].astype(f32) + b_ref[...].astype(f32)).astype(bf16)

    tiles = [(a.shape[1] // steps, a.shape[2]) for a in o3]
    blks = [pl.BlockSpec((None,) + t, lambda j, i, c_ref: (j, i, 0)) for t in tiles]
    outs = pl.pallas_call(
        body, name="chip_sums", out_shape=[SDS(a.shape, bf16) for a in o3], compiler_params=_cp("parallel", "parallel"),
        grid_spec=pltpu.PrefetchScalarGridSpec(
            num_scalar_prefetch=1, grid=(nchip, steps),
            in_specs=[pl.BlockSpec((None, None) + t, lambda j, i, c_ref: (j, c_ref[0], i, 0)) for t in tiles] + blks + [ANY],
            out_specs=blks),
    )(core, *m4, *o3, after)
    return [o.reshape(a.shape) for o, a in zip(outs, other)]


def _exchange_between_chips(sums):
    nt = len(BIG)

    def body(*refs):
        sums_r, land_r = refs[:nt], refs[nt:2 * nt]
        send_sems, recv_sems = refs[2 * nt:]
        x, y, c = _place()
        chips = [(1 - x, y), (x, 1 - y), (1 - x, 1 - y)]
        remote = [pltpu.make_async_remote_copy(src_ref=sums_r[t].at[2 * cx + cy], dst_ref=land_r[t].at[k], send_sem=send_sems.at[k, t],
                                               recv_sem=recv_sems.at[k, t], device_id=(cx, cy, c), device_id_type=MESH)
                  for k, (cx, cy) in enumerate(chips) for t in range(nt)]
        for cp in remote:
            cp.start()
        for cp in remote:
            cp.wait_recv()
        for cp in remote:
            cp.wait_send()

    return pl.pallas_call(
        body, name="rs_between_chips", in_specs=[ANY] * nt, out_specs=[ANY] * nt,
        out_shape=[SDS((3,) + a.shape[1:], bf16) for a in sums],
        scratch_shapes=[pltpu.SemaphoreType.DMA((3, nt)), pltpu.SemaphoreType.DMA((3, nt))],
    )(*sums)


def _all_reduce_rows(pack):
    rows = pack.shape[0]
    blk = rows // N_DEV

    def body(in_ref, out_ref, land, send1, recv1, send2, recv2):
        x, y, c = _place()
        me = 4 * x + 2 * y + c
        others = [(px, py, pc) for px in range(2) for py in range(2) for pc in range(2)]

        def is_me(p):
            return jnp.logical_and(jnp.logical_and(p[0] == x, p[1] == y), p[2] == c)

        land[me] = in_ref[pl.ds(_al(me, blk), blk), :]
        for d, p in enumerate(others):
            @pl.when(jnp.logical_not(is_me(p)))
            def _():
                pltpu.make_async_remote_copy(src_ref=in_ref.at[pl.ds(d * blk, blk), :], dst_ref=land.at[me], send_sem=send1.at[d],
                                             recv_sem=recv1.at[me], device_id=p, device_id_type=MESH).start()
        for d, p in enumerate(others):
            @pl.when(jnp.logical_not(is_me(p)))
            def _():
                cp = pltpu.make_async_remote_copy(src_ref=in_ref.at[pl.ds(d * blk, blk), :], dst_ref=land.at[d], send_sem=send1.at[d],
                                                  recv_sem=recv1.at[d], device_id=p, device_id_type=MESH)
                cp.wait_recv()
                cp.wait_send()
        total = land[0]
        for d in range(1, N_DEV):
            total = total + land[d]
        out_ref[pl.ds(_al(me, blk), blk), :] = total
        for d, p in enumerate(others):
            @pl.when(jnp.logical_not(is_me(p)))
            def _():
                mine = out_ref.at[pl.ds(_al(me, blk), blk), :]
                pltpu.make_async_remote_copy(src_ref=mine, dst_ref=mine, send_sem=send2.at[d], recv_sem=recv2.at[me],
                                             device_id=p, device_id_type=MESH).start()
        for d, p in enumerate(others):
            @pl.when(jnp.logical_not(is_me(p)))
            def _():
                theirs = out_ref.at[pl.ds(d * blk, blk), :]
                cp = pltpu.make_async_remote_copy(src_ref=theirs, dst_ref=theirs, send_sem=send2.at[d], recv_sem=recv2.at[d],
                                                  device_id=p, device_id_type=MESH)
                cp.wait_recv()
                cp.wait_send()

    vm = pl.BlockSpec(memory_space=pltpu.VMEM)
    return pl.pallas_call(
        body, name="all_reduce_rows", in_specs=[vm], out_specs=vm, out_shape=SDS((rows, LANE), f32),
        scratch_shapes=[pltpu.VMEM((N_DEV, blk, LANE), f32)] + [pltpu.SemaphoreType.DMA((N_DEV,))] * 4,
        compiler_params=pltpu.CompilerParams(vmem_limit_bytes=VMEM_LIMIT),
    )(pack)


def _lower_bounds_fwd(lower):
    def body(l_ref, o_ref):
        sm = _layer_softmax(l_ref)
        run = jnp.zeros_like(sm[0])
        for l in range(DEPTH):
            o_ref[l:l + 1, :] = run
            if l + 1 < DEPTH:
                run = run + sm[l + 1]

    return pl.pallas_call(body, name="lower_bounds_fwd", out_shape=SDS(lower.shape, f32))(lower)


def _layer_softmax(l_ref):
    rows = [l_ref[l:l + 1, :] for l in range(DEPTH)]
    top = functools.reduce(jnp.maximum, rows)
    e = [jnp.exp(r - top) for r in rows]
    tot = functools.reduce(lambda a, b: a + b, e)
    return [v / tot for v in e]


def _lower_bounds_bwd(lower, dlbs):
    def body(l_ref, d_ref, o_ref):
        sm = _layer_softmax(l_ref)
        dsm = [None] * DEPTH
        run = jnp.zeros_like(sm[0])
        dsm[0] = run
        for l in reversed(range(1, DEPTH)):
            run = run + d_ref[l:l + 1, :]
            dsm[l] = run
        inner = functools.reduce(lambda a, b: a + b, [sm[l] * dsm[l] for l in range(DEPTH)])
        for l in range(DEPTH):
            o_ref[l:l + 1, :] = sm[l] * (dsm[l] - inner)

    return pl.pallas_call(body, name="lower_bounds_bwd", out_shape=SDS(lower.shape, f32))(lower, dlbs)


_ADAM_C1 = 1.0 - ADAM_B1 ** ADAM_STEP
_ADAM_C2 = 1.0 - ADAM_B2 ** ADAM_STEP


def _adamw(w, g, m, v):
    m = ADAM_B1 * m + (1.0 - ADAM_B1) * g
    v = ADAM_B2 * v + (1.0 - ADAM_B2) * (g * g)
    delta = -ADAM_LR * ((m / _ADAM_C1) / (jnp.sqrt(v / _ADAM_C2) + ADAM_EPS) + ADAM_WD * w)
    return delta, m, v


def _adam_big(where, w, m, v, sums, landed, outs, after, windowed=False, tr=512):
    shape = w.shape
    cols = shape[-1]
    gcols = sums.shape[-1]
    w3, m3, v3 = (a.reshape(DEPTH, -1, cols) for a in (w, m, v))
    outs3 = [a.reshape(DEPTH, -1, cols) for a in outs]
    sums3 = sums.reshape(sums.shape[0], -1, gcols)
    land3 = landed.reshape(3, -1, gcols)
    rows = w3.shape[1]
    tr = min(tr, rows)

    def body(where_ref, w_ref, m_ref, v_ref, sum_ref, land_ref, *rest):
        g_ref, d_ref, nm_ref, nv_ref = rest[5:]
        g = sum_ref[...].astype(f32)
        for k in range(3):
            g = g + land_ref[k].astype(f32)
        if windowed:
            g = pltpu.roll(g, SHARD_IN * where_ref[2], 1)[:, :cols]
        delta, nm, nv = _adamw(w_ref[...], g, m_ref[...], v_ref[...])
        g_ref[...] = g
        d_ref[...] = delta
        nm_ref[...] = nm
        nv_ref[...] = nv

    blk = pl.BlockSpec((None, tr, cols), lambda i, wh: (wh[0], i, 0))
    res = pl.pallas_call(
        body, name="adam_big", out_shape=[SDS(w3.shape, f32)] * 4, input_output_aliases={6 + i: i for i in range(4)},
        compiler_params=_cp("parallel"),
        grid_spec=pltpu.PrefetchScalarGridSpec(
            num_scalar_prefetch=1, grid=(rows // tr,),
            in_specs=[blk, blk, blk, pl.BlockSpec((None, tr, gcols), lambda i, wh: (wh[1], i, 0)),
                      pl.BlockSpec((3, tr, gcols), lambda i, wh: (0, i, 0))] + [ANY] * 5,
            out_specs=[blk] * 4),
    )(where, w3, m3, v3, sums3, land3, *outs3, after)
    return [o.reshape(shape) for o in res]


def _adam_rows(w, g, m, v):
    def body(w_ref, g_ref, m_ref, v_ref, d_ref, nm_ref, nv_ref):
        delta, nm, nv = _adamw(w_ref[...], g_ref[...], m_ref[...], v_ref[...])
        d_ref[...] = delta
        nm_ref[...] = nm
        nv_ref[...] = nv

    return pl.pallas_call(body, name="adam_rows", out_shape=[SDS(w.shape, f32)] * 3)(w, g, m, v)


SMALL = ("g_mix", "lower_bounds", "g_hgrn_out", "w_conv", "sg_ln_g", "sg_ln_b", "w_sg", "b_sg", "g_ffn", "g_final")
WEIGHTS = ("w_in", "g_mix", "lower_bounds", "g_hgrn_out", "w_conv", "sg_ln_g", "sg_ln_b", "w_sg", "b_sg", "w_branch", "w_o", "g_ffn",
           "w_ff1", "w_ff2", "g_final")


def _pack_rows(arrays, multiple):
    flat = jnp.concatenate([a.reshape(-1) for a in arrays])
    rows = -(-flat.shape[0] // (LANE * multiple)) * multiple
    return jnp.pad(flat, (0, rows * LANE - flat.shape[0])).reshape(rows, LANE)


def _unpack_rows(pack, like):
    flat = pack.reshape(-1)
    out, at = [], 0
    for a in like:
        out.append(flat[at:at + a.size].reshape(a.shape))
        at += a.size
    return out


def kernel(x, w_in, g_mix, lower_bounds, g_hgrn_out, w_conv, sg_ln_g, sg_ln_b, w_sg, b_sg, w_branch, w_o, g_ffn, w_ff1, w_ff2, g_final, loss_target, m_w_in, m_g_mix, m_lower_bounds, m_g_hgrn_out, m_w_conv, m_sg_ln_g, m_sg_ln_b, m_w_sg, m_b_sg, m_w_branch, m_w_o, m_g_ffn, m_w_ff1, m_w_ff2, m_g_final, v_w_in, v_g_mix, v_lower_bounds, v_g_hgrn_out, v_w_conv, v_sg_ln_g, v_sg_ln_b, v_w_sg, v_b_sg, v_w_branch, v_w_o, v_g_ffn, v_w_ff1, v_w_ff2, v_g_final):
    weights = dict(w_in=w_in, g_mix=g_mix, lower_bounds=lower_bounds, g_hgrn_out=g_hgrn_out, w_conv=w_conv, sg_ln_g=sg_ln_g,
                   sg_ln_b=sg_ln_b, w_sg=w_sg, b_sg=b_sg, w_branch=w_branch, w_o=w_o, g_ffn=g_ffn, w_ff1=w_ff1, w_ff2=w_ff2, g_final=g_final)
    mom1 = dict(w_in=m_w_in, g_mix=m_g_mix, lower_bounds=m_lower_bounds, g_hgrn_out=m_g_hgrn_out, w_conv=m_w_conv, sg_ln_g=m_sg_ln_g,
                sg_ln_b=m_sg_ln_b, w_sg=m_w_sg, b_sg=m_b_sg, w_branch=m_w_branch, w_o=m_w_o, g_ffn=m_g_ffn, w_ff1=m_w_ff1, w_ff2=m_w_ff2,
                g_final=m_g_final)
    mom2 = dict(w_in=v_w_in, g_mix=v_g_mix, lower_bounds=v_lower_bounds, g_hgrn_out=v_g_hgrn_out, w_conv=v_w_conv, sg_ln_g=v_sg_ln_g,
                sg_ln_b=v_sg_ln_b, w_sg=v_w_sg, b_sg=v_b_sg, w_branch=v_w_branch, w_o=v_w_o, g_ffn=v_g_ffn, w_ff1=v_w_ff1, w_ff2=v_w_ff2,
                g_final=v_g_final)
    xi, yi, ci = _place()
    dev = 4 * xi + 2 * yi + ci
    conv_cols = w_conv.shape[-1]

    w_in_b = w_in.astype(bf16)
    pad = WIN - SHARD_IN
    w_in_win = jnp.where(ci == 0, jnp.pad(w_in_b, ((0, 0), (0, 0), (0, pad))), jnp.pad(w_in_b, ((0, 0), (0, 0), (pad, 0))))
    shards = dict(w_in=w_in_win, w_branch=w_branch.astype(bf16), w_o=w_o.astype(bf16), w_ff1=w_ff1.astype(bf16), w_ff2=w_ff2.astype(bf16))

    conv_place = lax.dynamic_update_slice(jnp.zeros((DEPTH, 3, BRANCH), f32), w_conv, (0, 0, dev * conv_cols))
    (w_conv_full,) = _unpack_rows(_all_reduce_rows(_pack_rows([conv_place], 8 * N_DEV)), [conv_place])
    lbs = _lower_bounds_fwd(lower_bounds)

    def small_of(l):
        return dict(g_mix=g_mix[l][None], lb=lbs[l][None], g_out=g_hgrn_out[l][None], w_conv=w_conv_full[l], ln_g=sg_ln_g[l][None],
                    ln_b=sg_ln_b[l][None], w_sg=w_sg[l], b_sg_t=b_sg[l].T, g_ffn=g_ffn[l][None])

    act = x[0]
    full, saved = [], []
    shard_refs = [jax.new_ref(shards[n], memory_space=pltpu.MemorySpace.HBM) for n in BIG]
    gathered = [_seq_all_gather_layer(l, shard_refs, _gather_out_shapes(shards)) for l in range(DEPTH)]
    for l in range(DEPTH):
        land, f_b, f_o, f_1, f_2 = _place_own(jnp.stack([jnp.int32(l), dev.astype(jnp.int32)]), shards, gathered[l], act)
        full.append(dict(w_in=_merge_windows(land), w_branch=f_b, w_o=f_o, w_ff1=f_1, w_ff2=f_2))
        act, sv = _layer_fwd(act, full[l], small_of(l))
        saved.append(sv)
    loss_row, dx, dxb, dg_final = _final(act, loss_target[0], g_final[None])
    loss = lax.psum(loss_row[0, 0], ("x", "y", "c"))

    core = ci.astype(jnp.int32)[None]
    big_out = {n: [lax.empty(weights[n].shape, f32) for _ in range(4)] for n in BIG}
    small_grads = [None] * DEPTH

    def chip_sums(stage, after):
        l, received, mine = stage
        sums = _chip_sums(core, mine, received, after)
        placed.append(sums[BIG.index("w_o")])
        landed, sums = _seq_exchange_between_chips(sums)
        return l, sums, landed

    def adam_layer(stage, after):
        l, sums, landed = stage
        where = jnp.stack([jnp.int32(l), (2 * xi + yi).astype(jnp.int32), ci.astype(jnp.int32)])
        for t, n in enumerate(BIG):
            big_out[n] = _adam_big(where, weights[n], mom1[n], mom2[n], sums[t], landed[t], big_out[n], after, windowed=(n == "w_in"))

    above = None
    placed = []
    for l in reversed(range(DEPTH)):
        summed = []

        def between(dx1):
            if above is None:
                return dx1
            summed.append(chip_sums(above, dx1))
            return placed[-1]

        def before_end(g_in):
            if not summed:
                return g_in
            adam_layer(summed[0], g_in)
            return big_out["w_o"][0]

        dx, dxb, big, small_grads[l] = _layer_bwd(dx, dxb, saved[l], full[l], small_of(l), between, before_end)
        above = (l, *_seq_exchange_on_chip([big[n] for n in BIG]))

    stack = lambda f: jnp.stack([f(small_grads[l]) for l in range(DEPTH)])
    d_lower = _lower_bounds_bwd(lower_bounds, stack(lambda s: s["vecs"][0]))
    local_small = dict(g_mix=stack(lambda s: s["g_mix"][0]), lower_bounds=d_lower, g_hgrn_out=stack(lambda s: s["vecs"][1]),
                       w_conv=stack(lambda s: s["vecs"][4:7]), sg_ln_g=stack(lambda s: s["vecs"][2]), sg_ln_b=stack(lambda s: s["vecs"][3]),
                       w_sg=stack(lambda s: s["w_sg"]), b_sg=stack(lambda s: s["b_sg_t"].T), g_ffn=stack(lambda s: s["g_ffn"][0]),
                       g_final=dg_final[0])
    order = [local_small[n] for n in SMALL]
    grads = dict(zip(SMALL, _unpack_rows(_all_reduce_rows(_pack_rows(order, 8 * N_DEV)), order)))
    grads["w_conv"] = lax.dynamic_slice(grads["w_conv"], (0, 0, dev * conv_cols), (DEPTH, 3, conv_cols))

    deltas, new_m, new_v = {}, {}, {}
    packs = [_pack_rows([d[n] for n in SMALL], 8) for d in (weights, grads, mom1, mom2)]
    like = [weights[n] for n in SMALL]
    small_out = _adam_rows(*packs)
    for out, pack in zip((deltas, new_m, new_v), small_out):
        out.update(zip(SMALL, _unpack_rows(pack, like)))
    adam_layer(chip_sums(above, dx), small_out[0])
    for n in BIG:
        grads[n], deltas[n], new_m[n], new_v[n] = big_out[n]

    return (loss, dx[None], *[grads[n] for n in WEIGHTS], *[deltas[n] for n in WEIGHTS], *[new_m[n] for n in WEIGHTS],
            *[new_v[n] for n in WEIGHTS])
```

```python
import functools

import jax
import jax.numpy as jnp
from jax import lax
from jax.experimental import pallas as pl
from jax.experimental.pallas import tpu as pltpu
from jax.experimental.pallas import tpu_sc as plsc

f32 = jnp.float32
bf16 = jnp.bfloat16
SDS = jax.ShapeDtypeStruct
MESH = pl.DeviceIdType.MESH

D_MODEL = 1024
BRANCH = 512
N_COLS = 7680
D_FF = 4096
DEPTH = 4
HEADS = 4
HEAD_DIM = 128
HGRN_CHUNK = 64
SG_CHUNK = 128
SG_GROUPS = 4
NORM_EPS = 1e-6
LN_EPS = 1e-5
LB_FLOOR = 1e-30
N_DEV = 8
SHARD_IN = N_COLS // N_DEV
WIN = 1024
LANE = 128
GATE_COL0 = 9 * BRANCH

ADAM_LR = 0.001
ADAM_B1 = 0.9
ADAM_B2 = 0.999
ADAM_EPS = 1e-08
ADAM_WD = 0.01
ADAM_STEP = 10

MIX_TILE = 256
VMEM_LIMIT = 56 * 1024 * 1024


def _cp(*sem):
    return pltpu.CompilerParams(dimension_semantics=sem or None, vmem_limit_bytes=VMEM_LIMIT)


def _dot(a, b):
    return jnp.dot(a, b, preferred_element_type=f32)


def _dot_nt(a, b):
    return lax.dot_general(a, b, (((1,), (1,)), ((), ())), preferred_element_type=f32)


def _dot_tn(a, b):
    return lax.dot_general(a, b, (((0,), (0,)), ((), ())), preferred_element_type=f32)


def _dot_exact(a, b):
    return jnp.dot(a, b, precision=lax.Precision.HIGHEST, preferred_element_type=f32)


def _sigmoid(x):
    return jax.nn.sigmoid(x)


_GELU_C = 0.7978845608028654
_GELU_A = 0.044715


def _gelu(x):
    return 0.5 * x * (1.0 + jnp.tanh(_GELU_C * (x + _GELU_A * x * x * x)))


def _gelu_grad(x):
    x2 = x * x
    t = jnp.tanh(_GELU_C * (x + _GELU_A * x * x2))
    return 0.5 * (1.0 + t) + 0.5 * x * (1.0 - t * t) * _GELU_C * (1.0 + 3.0 * _GELU_A * x2)


def _rms_stats(x):
    r = lax.rsqrt(jnp.mean(x * x, axis=-1, keepdims=True) + NORM_EPS)
    return r, x * r


def _rms_bwd(dh, xh, r, g):
    dg = jnp.sum(dh * xh, axis=0, keepdims=True)
    dxn = dh * g
    dx = r * (dxn - xh * jnp.mean(dxn * xh, axis=-1, keepdims=True))
    return dx, dg


def _tri(n, upper=False):
    r = lax.broadcasted_iota(jnp.int32, (n, n), 0)
    c = lax.broadcasted_iota(jnp.int32, (n, n), 1)
    return (c >= r) if upper else (c <= r)


def _acc_rows(ref, first, val):
    @pl.when(first)
    def _():
        ref[...] = val

    @pl.when(jnp.logical_not(first))
    def _():
        ref[...] += val


def _rms_mm(x, g, w_t, tm=1024, tn=1920):
    s, n = x.shape[0], w_t.shape[0]

    def body(x_ref, g_ref, w_ref, p_ref, h_ref, hs):
        @pl.when(pl.program_id(1) == 0)
        def _():
            _, xh = _rms_stats(x_ref[...])
            hv = (xh * g_ref[...]).astype(bf16)
            hs[...] = hv
            h_ref[...] = hv

        p_ref[...] = _dot_nt(hs[...], w_ref[...])

    return pl.pallas_call(
        body, name="rms_mm", grid=(s // tm, n // tn),
        in_specs=[pl.BlockSpec((tm, D_MODEL), lambda i, j: (i, 0)), pl.BlockSpec((1, D_MODEL), lambda i, j: (0, 0)),
                  pl.BlockSpec((tn, D_MODEL), lambda i, j: (j, 0))],
        out_specs=[pl.BlockSpec((tm, tn), lambda i, j: (i, j)), pl.BlockSpec((tm, D_MODEL), lambda i, j: (i, 0))],
        out_shape=[SDS((s, n), f32), SDS((s, D_MODEL), bf16)],
        scratch_shapes=[pltpu.VMEM((tm, D_MODEL), bf16)], compiler_params=_cp("parallel", "arbitrary"),
    )(x, g, w_t)


def _hgrn_gates(fp, lb):
    logf = jnp.logaddexp(jnp.log(jnp.maximum(lb, LB_FLOOR)), jnp.log1p(-lb) + jax.nn.log_sigmoid(fp))
    snf = _sigmoid(-fp)
    return logf, snf, (1.0 - lb) * snf


def _p_specs(tile, cols, row_map):
    return [pl.BlockSpec((tile, BRANCH), functools.partial(lambda c, i: (row_map(i), c), c)) for c in cols]


def _mixer_fwd(p, lb, gout, wconv, lng, lnb, wsg, bsg_t):
    s = p.shape[0]
    tt = MIX_TILE
    nch = tt // HGRN_CHUNK

    def body(q_ref, fp_ref, iv_ref, go_ref, bg_ref, cg_ref, xc_ref, u_ref, v_ref, lb_ref, gout_ref, wconv_ref, lng_ref,
             lnb_ref, wsg_ref, bsg_ref, z_ref, opre_ref, st_ref, st_scr, zbuf):
        @pl.when(pl.program_id(0) == 0)
        def _():
            st_scr[...] = jnp.zeros_like(st_scr)
            zbuf[0:8, :] = jnp.zeros((8, BRANCH), f32)

        lbv = lb_ref[...]
        q_raw = q_ref[...]
        qs = q_raw * _sigmoid(q_raw)
        logf, _, kk = _hgrn_gates(fp_ref[...], lbv)
        iv = iv_ref[...]
        causal = _tri(HGRN_CHUNK)
        tri = causal.astype(f32)
        last_row = lax.broadcasted_iota(jnp.int32, (HGRN_CHUNK, 1), 0) == HGRN_CHUNK - 1
        for c in range(nch):
            rows = slice(HGRN_CHUNK * c, HGRN_CHUNK * (c + 1))
            b = _dot_exact(tri, logf[rows])
            bl = jnp.sum(jnp.where(last_row, b, 0.0), axis=0, keepdims=True)
            qb = (qs[rows] * jnp.exp(b)).astype(bf16)
            kb = (kk[rows] * jnp.exp(-b)).astype(bf16)
            kd = (kk[rows] * jnp.exp(bl - b)).astype(bf16)
            ebl = jnp.exp(bl)
            vc = iv[rows].astype(bf16)
            for h in range(HEADS):
                sl = slice(HEAD_DIM * h, HEAD_DIM * (h + 1))
                st = st_scr[h]
                st_ref[c, h] = st
                a = jnp.where(causal, _dot_nt(qb[:, sl], kb[:, sl]), 0.0)
                opre_ref[rows, sl] = _dot(a.astype(bf16), vc[:, sl]) + _dot_nt(qb[:, sl], st.astype(bf16))
                st_scr[h] = st * ebl[:, sl] + _dot_tn(vc[:, sl], kd[:, sl])

        o = opre_ref[...]
        go = go_ref[...]
        gout_v = gout_ref[...]
        for h in range(HEADS):
            sl = slice(HEAD_DIM * h, HEAD_DIM * (h + 1))
            _, oh = _rms_stats(o[:, sl])
            z_ref[:, sl] = (oh * gout_v[:, sl] * _sigmoid(go[:, sl])).astype(bf16)

        zc = cg_ref[...] * xc_ref[...]
        zbuf[8:8 + tt, :] = zc
        y = wconv_ref[0:1, :] * zbuf[pl.ds(6, tt), :] + wconv_ref[1:2, :] * zbuf[pl.ds(7, tt), :] + wconv_ref[2:3, :] * zc
        z_ref[:, BRANCH:2 * BRANCH] = (bg_ref[...] * y).astype(bf16)
        zbuf[0:8, :] = zbuf[tt:tt + 8, :]

        ug = _gelu(u_ref[...])
        vg = _gelu(v_ref[...])
        vcen = vg - jnp.mean(vg, axis=-1, keepdims=True)
        rstd = lax.rsqrt(jnp.mean(vcen * vcen, axis=-1, keepdims=True) + LN_EPS)
        vn = (vcen * rstd * lng_ref[...] + lnb_ref[...]).astype(bf16)
        low = _tri(SG_CHUNK)
        for g in range(SG_GROUPS):
            sl = slice(LANE * g, LANE * (g + 1))
            wm = jnp.where(low, wsg_ref[g], 0.0).astype(bf16)
            bias = bsg_ref[:, g:g + 1]
            for cc in range(tt // SG_CHUNK):
                rows = slice(SG_CHUNK * cc, SG_CHUNK * (cc + 1))
                sv = _dot(wm, vn[rows, sl]) + bias
                z_ref[rows, 2 * BRANCH + LANE * g:2 * BRANCH + LANE * (g + 1)] = (ug[rows, sl] * sv).astype(bf16)

    full = lambda shape: pl.BlockSpec(shape, lambda i: (0,) * len(shape))
    return pl.pallas_call(
        body, name="mixer_fwd", grid=(s // tt,),
        in_specs=_p_specs(tt, range(9), lambda i: i) + [full((1, BRANCH)), full((1, BRANCH)), full((3, BRANCH)), full((1, BRANCH)),
                                                        full((1, BRANCH)), full((SG_GROUPS, SG_CHUNK, SG_CHUNK)), full((SG_CHUNK, SG_GROUPS))],
        out_specs=[pl.BlockSpec((tt, 3 * BRANCH), lambda i: (i, 0)), pl.BlockSpec((tt, BRANCH), lambda i: (i, 0)),
                   pl.BlockSpec((nch, HEADS, HEAD_DIM, HEAD_DIM), lambda i: (i, 0, 0, 0))],
        out_shape=[SDS((s, 3 * BRANCH), bf16), SDS((s, BRANCH), f32), SDS((s // HGRN_CHUNK, HEADS, HEAD_DIM, HEAD_DIM), f32)],
        scratch_shapes=[pltpu.VMEM((HEADS, HEAD_DIM, HEAD_DIM), f32), pltpu.VMEM((tt + 8, BRANCH), f32)],
        compiler_params=_cp("arbitrary"),
    )(*([p] * 9), lb, gout, wconv, lng, lnb, wsg, bsg_t)


def _branch_gate(z, wb, p, tm=256):
    s = z.shape[0]
    half = 3 * D_MODEL // 2

    def body(z_ref, wb_ref, ga_ref, gb_ref, y_ref, m_ref):
        ga, gb = ga_ref[...], gb_ref[...]
        gates = [ga[:, :D_MODEL], jnp.concatenate([ga[:, D_MODEL:], gb[:, :D_MODEL // 2]], axis=1), gb[:, D_MODEL // 2:]]
        acc = None
        for n in range(3):
            yn = _dot(z_ref[:, BRANCH * n:BRANCH * (n + 1)], wb_ref[n])
            y_ref[:, D_MODEL * n:D_MODEL * (n + 1)] = yn.astype(bf16)
            t = _sigmoid(gates[n]) * yn
            acc = t if acc is None else acc + t
        m_ref[...] = acc.astype(bf16)

    blk0 = GATE_COL0 // half
    return pl.pallas_call(
        body, name="branch_gate", grid=(s // tm,),
        in_specs=[pl.BlockSpec((tm, 3 * BRANCH), lambda i: (i, 0)), pl.BlockSpec((3, BRANCH, D_MODEL), lambda i: (0, 0, 0)),
                  pl.BlockSpec((tm, half), lambda i: (i, blk0)), pl.BlockSpec((tm, half), lambda i: (i, blk0 + 1))],
        out_specs=[pl.BlockSpec((tm, 3 * D_MODEL), lambda i: (i, 0)), pl.BlockSpec((tm, D_MODEL), lambda i: (i, 0))],
        out_shape=[SDS((s, 3 * D_MODEL), bf16), SDS((s, D_MODEL), bf16)], compiler_params=_cp("parallel"),
    )(z, wb, p, p)


def _mm_resid(x, m, wo, tm=512):
    s = x.shape[0]

    def body(x_ref, m_ref, w_ref, o_ref):
        o_ref[...] = x_ref[...] + _dot(m_ref[...], w_ref[...])

    return pl.pallas_call(
        body, name="mm_resid", grid=(s // tm,),
        in_specs=[pl.BlockSpec((tm, D_MODEL), lambda i: (i, 0)), pl.BlockSpec((tm, D_MODEL), lambda i: (i, 0)),
                  pl.BlockSpec((D_MODEL, D_MODEL), lambda i: (0, 0))],
        out_specs=pl.BlockSpec((tm, D_MODEL), lambda i: (i, 0)), out_shape=SDS((s, D_MODEL), f32), compiler_params=_cp("parallel"),
    )(x, m, wo)


def _ffn(x1, g, w1, w2, tm=512, tf=1024):
    s = x1.shape[0]
    nf = D_FF // tf

    def body(x_ref, g_ref, w1_ref, w2_ref, o_ref, h_ref, ra_ref, hs, acc):
        f = pl.program_id(1)

        @pl.when(f == 0)
        def _():
            _, xh = _rms_stats(x_ref[...])
            hv = (xh * g_ref[...]).astype(bf16)
            hs[...] = hv
            h_ref[...] = hv
            acc[...] = jnp.zeros_like(acc)

        ra = jnp.maximum(_dot(hs[...], w1_ref[...]), 0.0)
        ra_ref[...] = ra.astype(bf16)
        acc[...] += _dot((ra * ra).astype(bf16), w2_ref[...])

        @pl.when(f == nf - 1)
        def _():
            o_ref[...] = x_ref[...] + acc[...]

    return pl.pallas_call(
        body, name="ffn", grid=(s // tm, nf),
        in_specs=[pl.BlockSpec((tm, D_MODEL), lambda i, f: (i, 0)), pl.BlockSpec((1, D_MODEL), lambda i, f: (0, 0)),
                  pl.BlockSpec((D_MODEL, tf), lambda i, f: (0, f)), pl.BlockSpec((tf, D_MODEL), lambda i, f: (f, 0))],
        out_specs=[pl.BlockSpec((tm, D_MODEL), lambda i, f: (i, 0)), pl.BlockSpec((tm, D_MODEL), lambda i, f: (i, 0)),
                   pl.BlockSpec((tm, tf), lambda i, f: (i, f))],
        out_shape=[SDS((s, D_MODEL), f32), SDS((s, D_MODEL), bf16), SDS((s, D_FF), bf16)],
        scratch_shapes=[pltpu.VMEM((tm, D_MODEL), bf16), pltpu.VMEM((tm, D_MODEL), f32)], compiler_params=_cp("parallel", "arbitrary"),
    )(x1, g, w1, w2)


def _final(x, target, g, tm=512):
    s = x.shape[0]

    def body(x_ref, t_ref, g_ref, loss_ref, dx_ref, dxb_ref, dg_ref):
        first = pl.program_id(0) == 0
        gv = g_ref[...]
        r, xh = _rms_stats(x_ref[...])
        e = xh * gv - t_ref[...]
        tile_loss = 0.5 * jnp.sum(jnp.mean(e * e, axis=-1, keepdims=True), axis=0, keepdims=True)
        dx, dg = _rms_bwd(e * (1.0 / D_MODEL), xh, r, gv)
        dx_ref[...] = dx
        dxb_ref[...] = dx.astype(bf16)
        _acc_rows(dg_ref, first, dg)
        _acc_rows(loss_ref, first, jnp.broadcast_to(tile_loss, (1, LANE)))

    row = pl.BlockSpec((tm, D_MODEL), lambda i: (i, 0))
    return pl.pallas_call(
        body, name="final_loss", grid=(s // tm,), in_specs=[row, row, pl.BlockSpec((1, D_MODEL), lambda i: (0, 0))],
        out_specs=[pl.BlockSpec((1, LANE), lambda i: (0, 0)), row, row, pl.BlockSpec((1, D_MODEL), lambda i: (0, 0))],
        out_shape=[SDS((1, LANE), f32), SDS((s, D_MODEL), f32), SDS((s, D_MODEL), bf16), SDS((1, D_MODEL), f32)],
        compiler_params=_cp("arbitrary"),
    )(x, target, g)


def _ffn_bwd(dx2, dx2b, x1, g, ra, w1, w2, tm=512, tf=1024):
    s = x1.shape[0]
    nf = D_FF // tf

    def body(dx_ref, dxb_ref, x_ref, g_ref, ra_ref, w1_ref, w2_ref, da_ref, act_ref, dx1_ref, dx1b_ref, dg_ref, acc):
        i, f = pl.program_id(0), pl.program_id(1)

        @pl.when(f == 0)
        def _():
            acc[...] = jnp.zeros_like(acc)

        rav = ra_ref[...].astype(f32)
        da = (_dot_nt(dxb_ref[...], w2_ref[...]) * (2.0 * rav)).astype(bf16)
        da_ref[...] = da
        act_ref[...] = (rav * rav).astype(bf16)
        acc[...] += _dot_nt(da, w1_ref[...])

        @pl.when(f == nf - 1)
        def _():
            r, xh = _rms_stats(x_ref[...])
            dx, dg = _rms_bwd(acc[...], xh, r, g_ref[...])
            dx = dx + dx_ref[...]
            dx1_ref[...] = dx
            dx1b_ref[...] = dx.astype(bf16)
            _acc_rows(dg_ref, i == 0, dg)

    row = pl.BlockSpec((tm, D_MODEL), lambda i, f: (i, 0))
    col = pl.BlockSpec((tm, tf), lambda i, f: (i, f))
    return pl.pallas_call(
        body, name="ffn_bwd", grid=(s // tm, nf),
        in_specs=[row, row, row, pl.BlockSpec((1, D_MODEL), lambda i, f: (0, 0)), col,
                  pl.BlockSpec((D_MODEL, tf), lambda i, f: (0, f)), pl.BlockSpec((tf, D_MODEL), lambda i, f: (f, 0))],
        out_specs=[col, col, row, row, pl.BlockSpec((1, D_MODEL), lambda i, f: (0, 0))],
        out_shape=[SDS((s, D_FF), bf16), SDS((s, D_FF), bf16), SDS((s, D_MODEL), f32), SDS((s, D_MODEL), bf16), SDS((1, D_MODEL), f32)],
        scratch_shapes=[pltpu.VMEM((tm, D_MODEL), f32)], compiler_params=_cp("arbitrary", "arbitrary"),
    )(dx2, dx2b, x1, g, ra, w1, w2)


def _mm_tn(a, b, nb, m, n, tm, tn, ts=512, name="mm_tn"):
    s = a.shape[0]
    mi, nj, ns = m // tm, n // tn, s // ts

    def body(a_ref, b_ref, o_ref, acc):
        t = pl.program_id(3)

        @pl.when(t == 0)
        def _():
            acc[...] = jnp.zeros_like(acc)

        acc[...] += _dot_tn(a_ref[...], b_ref[...])

        @pl.when(t == ns - 1)
        def _():
            o_ref[...] = acc[...].astype(bf16)

    return pl.pallas_call(
        body, name=name, grid=(nb, mi, nj, ns),
        in_specs=[pl.BlockSpec((ts, tm), lambda k, i, j, t: (t, k * mi + i)), pl.BlockSpec((ts, tn), lambda k, i, j, t: (t, k * nj + j))],
        out_specs=pl.BlockSpec((None, tm, tn), lambda k, i, j, t: (k, i, j)), out_shape=SDS((nb, m, n), bf16),
        scratch_shapes=[pltpu.VMEM((tm, tn), f32)], compiler_params=_cp("parallel", "parallel", "parallel", "arbitrary"),
    )(a, b)


def _mm_tn_slabs(a, b, nb, m, nblk, rel, width, tm=512, ts=512, name="mm_tn_slabs"):
    s = a.shape[0]
    n = b.shape[1] // nb
    ng, mi, ns, nw = n // nblk, m // tm, s // ts, len(rel)

    def body(a_ref, b_ref, o_ref, acc):
        t = pl.program_id(3)

        @pl.when(t == 0)
        def _():
            acc[...] = jnp.zeros_like(acc)

        acc[...] += _dot_tn(a_ref[...], b_ref[...])

        @pl.when(t == ns - 1)
        def _():
            for r, start in enumerate(rel):
                o_ref[r] = acc[:, start:start + width].astype(bf16)

    return pl.pallas_call(
        body, name=name, grid=(nb, ng, mi, ns),
        in_specs=[pl.BlockSpec((ts, tm), lambda k, g, i, t: (t, k * mi + i)), pl.BlockSpec((ts, nblk), lambda k, g, i, t: (t, k * ng + g))],
        out_specs=pl.BlockSpec((nw, None, tm, width), lambda k, g, i, t: (g, k, i, 0)), out_shape=SDS((ng * nw, nb, m, width), bf16),
        scratch_shapes=[pltpu.VMEM((tm, nblk), f32)], compiler_params=_cp("parallel", "parallel", "parallel", "arbitrary"),
    )(a, b)


def _merge_bwd(dx1b, wo, y, p, wb, after, tm=256):
    s = dx1b.shape[0]
    hw = D_MODEL // 2
    gblk = GATE_COL0 // hw

    def body(dx_ref, wo_ref, y_ref, gt_ref, wb_ref, after_ref, dy_ref, dp_ref, dz_ref, dm_scr, dz_acc):
        del after_ref
        k = pl.program_id(1)
        hf = k % 2

        @pl.when(k == 0)
        def _():
            dm = _dot_nt(dx_ref[...], wo_ref[...])
            dm_scr[0] = dm[:, :hw]
            dm_scr[1] = dm[:, hw:]

        dmh = dm_scr[hf]
        gate = _sigmoid(gt_ref[...])
        dy = (dmh * gate).astype(bf16)
        dy_ref[...] = dy
        dp_ref[...] = (dmh * y_ref[...].astype(f32) * gate * (1.0 - gate)).astype(bf16)
        part = _dot_nt(dy, wb_ref[...])

        @pl.when(hf == 0)
        def _():
            dz_acc[...] = part

        @pl.when(hf == 1)
        def _():
            dz_ref[...] = dz_acc[...] + part

    return pl.pallas_call(
        body, name="merge_bwd", grid=(s // tm, 6),
        in_specs=[pl.BlockSpec((tm, D_MODEL), lambda i, k: (i, 0)), pl.BlockSpec((D_MODEL, D_MODEL), lambda i, k: (0, 0)),
                  pl.BlockSpec((tm, hw), lambda i, k: (i, k)), pl.BlockSpec((tm, hw), lambda i, k: (i, gblk + k)),
                  pl.BlockSpec((None, BRANCH, hw), lambda i, k: (k // 2, 0, k % 2)), pl.BlockSpec(memory_space=pl.ANY)],
        out_specs=[pl.BlockSpec((tm, hw), lambda i, k: (i, k)), pl.BlockSpec((tm, hw), lambda i, k: (i, gblk + k)),
                   pl.BlockSpec((tm, BRANCH), lambda i, k: (i, k // 2))],
        out_shape=[SDS((s, 3 * D_MODEL), bf16), SDS((s, N_COLS), bf16), SDS((s, 3 * BRANCH), f32)],
        scratch_shapes=[pltpu.VMEM((2, tm, hw), f32), pltpu.VMEM((tm, BRANCH), f32)], compiler_params=_cp("parallel", "arbitrary"),
    )(dx1b, wo, y, p, wb, after)


def _mixer_bwd(p, dz, opre, states, dp, lb, gout, wconv, lng, lnb, wsg, bsg_t):
    s = p.shape[0]
    tt = MIX_TILE
    nt = s // tt
    nch = tt // HGRN_CHUNK
    rev = lambda i: nt - 1 - i

    def body(q_ref, fp_ref, iv_ref, go_ref, bg_ref, cg_ref, xc_ref, u_ref, v_ref, cgp_ref, xcp_ref, dz_ref, opre_ref, st_ref,
             dp_in, lb_ref, gout_ref, wconv_ref, lng_ref, lnb_ref, wsg_ref, bsg_ref,
             dp_ref, vec_ref, dwsg_ref, dbsg_ref, dst_scr, zbuf, dybuf, dvn_scr, dbsg_acc):
        del dp_in
        i = pl.program_id(0)

        @pl.when(i == 0)
        def _():
            dst_scr[...] = jnp.zeros_like(dst_scr)
            dybuf[tt:tt + 8, :] = jnp.zeros((8, BRANCH), f32)
            vec_ref[...] = jnp.zeros_like(vec_ref)
            dwsg_ref[...] = jnp.zeros_like(dwsg_ref)
            dbsg_acc[...] = jnp.zeros_like(dbsg_acc)

        lbv = lb_ref[...]
        q_raw, fp = q_ref[...], fp_ref[...]
        sq = _sigmoid(q_raw)
        qs = q_raw * sq
        sfp = _sigmoid(fp)
        logf, snf, kk = _hgrn_gates(fp, lbv)
        inv_f = jnp.exp(-logf)
        iv = iv_ref[...]
        doa = dz_ref[:, 0:BRANCH]
        o = opre_ref[...]
        sgo = _sigmoid(go_ref[...])
        gout_v = gout_ref[...]
        d_o, dgo, dgout = [], [], []
        for h in range(HEADS):
            sl = slice(HEAD_DIM * h, HEAD_DIM * (h + 1))
            r, oh = _rms_stats(o[:, sl])
            d_on = doa[:, sl] * sgo[:, sl]
            dgo.append(doa[:, sl] * oh * gout_v[:, sl] * sgo[:, sl] * (1.0 - sgo[:, sl]))
            dx, dg = _rms_bwd(d_on, oh, r, gout_v[:, sl])
            d_o.append(dx)
            dgout.append(dg)
        d_o = jnp.concatenate(d_o, axis=1)
        dp_ref[:, 3 * BRANCH:4 * BRANCH] = jnp.concatenate(dgo, axis=1).astype(bf16)
        vec_ref[1:2, :] += jnp.concatenate(dgout, axis=1)

        causal = _tri(HGRN_CHUNK)
        tri = causal.astype(f32)
        tri_up = _tri(HGRN_CHUNK, upper=True).astype(f32)
        last_row = lax.broadcasted_iota(jnp.int32, (HGRN_CHUNK, 1), 0) == HGRN_CHUNK - 1
        lb_live = (lbv > LB_FLOOR).astype(f32)
        dlb = jnp.zeros((1, BRANCH), f32)
        for c in reversed(range(nch)):
            rows = slice(HGRN_CHUNK * c, HGRN_CHUNK * (c + 1))
            b = _dot_exact(tri, logf[rows])
            bl = jnp.sum(jnp.where(last_row, b, 0.0), axis=0, keepdims=True)
            eb, enb, edl, ebl = jnp.exp(b), jnp.exp(-b), jnp.exp(bl - b), jnp.exp(bl)
            qbf, kbf, kdf = qs[rows] * eb, kk[rows] * enb, kk[rows] * edl
            qb, kb, kd = qbf.astype(bf16), kbf.astype(bf16), kdf.astype(bf16)
            vc = iv[rows].astype(bf16)
            dob = d_o[rows].astype(bf16)
            dv, dqb, dkb, dkd, debl = [], [], [], [], []
            for h in range(HEADS):
                sl = slice(HEAD_DIM * h, HEAD_DIM * (h + 1))
                st = st_ref[c, h]
                dst = dst_scr[h]
                stb, dstb = st.astype(bf16), dst.astype(bf16)
                a = jnp.where(causal, _dot_nt(qb[:, sl], kb[:, sl]), 0.0).astype(bf16)
                da = jnp.where(causal, _dot_nt(dob[:, sl], vc[:, sl]), 0.0).astype(bf16)
                dv.append(_dot_tn(a, dob[:, sl]) + _dot_nt(kd[:, sl], dstb))
                dqb.append(_dot(dob[:, sl], stb) + _dot(da, kb[:, sl]))
                dkb.append(_dot_tn(da, qb[:, sl]))
                dkd.append(_dot(vc[:, sl], dstb))
                debl.append(jnp.sum(st * dst, axis=0, keepdims=True))
                dst_scr[h] = _dot_tn(dob[:, sl], qb[:, sl]) + dst * ebl[:, sl]
            dv, dqb, dkb, dkd = (jnp.concatenate(t, axis=1) for t in (dv, dqb, dkb, dkd))
            debl = jnp.concatenate(debl, axis=1)
            t_kd = dkd * kdf
            dbl = ebl * debl + jnp.sum(t_kd, axis=0, keepdims=True)
            db = dqb * qbf - dkb * kbf - t_kd + jnp.where(last_row, dbl, 0.0)
            dkk = dkb * enb + dkd * edl
            dlc = _dot_exact(tri_up, db)
            sq_c, q_c, sfp_c, snf_c, invf_c = sq[rows], q_raw[rows], sfp[rows], snf[rows], inv_f[rows]
            slope = (1.0 - lbv) * sfp_c * snf_c
            dp_ref[rows, 0:BRANCH] = (dqb * eb * sq_c * (1.0 + q_c * (1.0 - sq_c))).astype(bf16)
            dp_ref[rows, BRANCH:2 * BRANCH] = (slope * (dlc * invf_c - dkk)).astype(bf16)
            dp_ref[rows, 2 * BRANCH:3 * BRANCH] = dv.astype(bf16)
            dlb = dlb + jnp.sum(dlc * (lb_live - sfp_c) * invf_c - dkk * snf_c, axis=0, keepdims=True)
        vec_ref[0:1, :] += dlb

        dob_ = dz_ref[:, BRANCH:2 * BRANCH]
        bg, cg, xc = bg_ref[...], cg_ref[...], xc_ref[...]
        zc = cg * xc
        zbuf[0:8, :] = jnp.where(i < nt - 1, cgp_ref[...] * xcp_ref[...], 0.0)
        zbuf[8:8 + tt, :] = zc
        w0, w1, w2 = wconv_ref[0:1, :], wconv_ref[1:2, :], wconv_ref[2:3, :]
        y = w0 * zbuf[pl.ds(6, tt), :] + w1 * zbuf[pl.ds(7, tt), :] + w2 * zc
        dy = dob_ * bg
        dybuf[0:tt, :] = dy
        dy1, dy2 = dybuf[pl.ds(1, tt), :], dybuf[pl.ds(2, tt), :]
        dzc = w2 * dy + w1 * dy1 + w0 * dy2
        dp_ref[:, 4 * BRANCH:5 * BRANCH] = (dob_ * y).astype(bf16)
        dp_ref[:, 5 * BRANCH:6 * BRANCH] = (dzc * xc).astype(bf16)
        dp_ref[:, 6 * BRANCH:7 * BRANCH] = (dzc * cg).astype(bf16)
        vec_ref[4:5, :] += jnp.sum(zc * dy2, axis=0, keepdims=True)
        vec_ref[5:6, :] += jnp.sum(zc * dy1, axis=0, keepdims=True)
        vec_ref[6:7, :] += jnp.sum(zc * dy, axis=0, keepdims=True)
        dybuf[tt:tt + 8, :] = dybuf[0:8, :]

        doc = dz_ref[:, 2 * BRANCH:3 * BRANCH]
        u_raw, v_raw = u_ref[...], v_ref[...]
        ug = _gelu(u_raw)
        dug_scale = _gelu_grad(u_raw)
        vg = _gelu(v_raw)
        vcen = vg - jnp.mean(vg, axis=-1, keepdims=True)
        rstd = lax.rsqrt(jnp.mean(vcen * vcen, axis=-1, keepdims=True) + LN_EPS)
        vhat = vcen * rstd
        lng_v = lng_ref[...]
        vn = (vhat * lng_v + lnb_ref[...]).astype(bf16)
        low = _tri(SG_CHUNK)
        for g in range(SG_GROUPS):
            sl = slice(LANE * g, LANE * (g + 1))
            wm = jnp.where(low, wsg_ref[g], 0.0).astype(bf16)
            bias = bsg_ref[:, g:g + 1]
            dw = jnp.zeros((SG_CHUNK, SG_CHUNK), f32)
            dbs = jnp.zeros((SG_CHUNK, LANE), f32)
            for cc in range(tt // SG_CHUNK):
                rows = slice(SG_CHUNK * cc, SG_CHUNK * (cc + 1))
                vn_c = vn[rows, sl]
                sv = _dot(wm, vn_c) + bias
                doc_c = doc[rows, sl]
                dp_ref[rows, 7 * BRANCH + LANE * g:7 * BRANCH + LANE * (g + 1)] = (doc_c * sv * dug_scale[rows, sl]).astype(bf16)
                dsv = doc_c * ug[rows, sl]
                dsvb = dsv.astype(bf16)
                dbs = dbs + dsv
                dw = dw + _dot_nt(dsvb, vn_c)
                dvn_scr[rows, sl] = _dot_tn(wm, dsvb)
            dwsg_ref[g] += jnp.where(low, dw, 0.0)
            dbsg_acc[:, sl] += dbs
        dvn = dvn_scr[...]
        vec_ref[2:3, :] += jnp.sum(dvn * vhat, axis=0, keepdims=True)
        vec_ref[3:4, :] += jnp.sum(dvn, axis=0, keepdims=True)
        dvh = dvn * lng_v
        dvg = rstd * (dvh - jnp.mean(dvh, axis=-1, keepdims=True) - vhat * jnp.mean(dvh * vhat, axis=-1, keepdims=True))
        dp_ref[:, 8 * BRANCH:9 * BRANCH] = (dvg * _gelu_grad(v_raw)).astype(bf16)

        @pl.when(i == nt - 1)
        def _():
            for g in range(SG_GROUPS):
                dbsg_ref[:, g:g + 1] = jnp.sum(dbsg_acc[:, LANE * g:LANE * (g + 1)], axis=1, keepdims=True)

    full = lambda shape: pl.BlockSpec(shape, lambda i: (0,) * len(shape))
    tail = lambda c: pl.BlockSpec((8, BRANCH), lambda i: (jnp.maximum(rev(i) * (tt // 8) - 1, 0), c))
    return pl.pallas_call(
        body, name="mixer_bwd", grid=(nt,),
        in_specs=_p_specs(tt, range(9), rev) + [tail(5), tail(6), pl.BlockSpec((tt, 3 * BRANCH), lambda i: (rev(i), 0)),
                                                pl.BlockSpec((tt, BRANCH), lambda i: (rev(i), 0)),
                                                pl.BlockSpec((nch, HEADS, HEAD_DIM, HEAD_DIM), lambda i: (rev(i), 0, 0, 0)),
                                                pl.BlockSpec(memory_space=pl.ANY),
                                                full((1, BRANCH)), full((1, BRANCH)), full((3, BRANCH)), full((1, BRANCH)), full((1, BRANCH)),
                                                full((SG_GROUPS, SG_CHUNK, SG_CHUNK)), full((SG_CHUNK, SG_GROUPS))],
        out_specs=[pl.BlockSpec((tt, 9 * BRANCH), lambda i: (rev(i), 0)), full((8, BRANCH)), full((SG_GROUPS, SG_CHUNK, SG_CHUNK)),
                   full((SG_CHUNK, SG_GROUPS))],
        out_shape=[SDS((s, N_COLS), bf16), SDS((8, BRANCH), f32), SDS((SG_GROUPS, SG_CHUNK, SG_CHUNK), f32), SDS((SG_CHUNK, SG_GROUPS), f32)],
        scratch_shapes=[pltpu.VMEM((HEADS, HEAD_DIM, HEAD_DIM), f32), pltpu.VMEM((tt + 8, BRANCH), f32), pltpu.VMEM((tt + 8, BRANCH), f32),
                        pltpu.VMEM((tt, BRANCH), f32), pltpu.VMEM((SG_CHUNK, BRANCH), f32)],
        input_output_aliases={14: 0}, compiler_params=_cp("arbitrary"),
    )(*([p] * 11), dz, opre, states, dp, lb, gout, wconv, lng, lnb, wsg, bsg_t)


def _dh_bwd(dp, w_t, x, dx1, g, after, tm=1024, tk=1536):
    s = x.shape[0]
    nk = N_COLS // tk

    def body(dp_ref, w_ref, x_ref, dx1_ref, g_ref, after_ref, dx_ref, dxb_ref, dg_ref, acc):
        del after_ref
        i, k = pl.program_id(0), pl.program_id(1)

        @pl.when(k == 0)
        def _():
            acc[...] = jnp.zeros_like(acc)

        acc[...] += _dot(dp_ref[...], w_ref[...])

        @pl.when(k == nk - 1)
        def _():
            r, xh = _rms_stats(x_ref[...])
            dx, dg = _rms_bwd(acc[...], xh, r, g_ref[...])
            dx = dx + dx1_ref[...]
            dx_ref[...] = dx
            dxb_ref[...] = dx.astype(bf16)
            _acc_rows(dg_ref, i == 0, dg)

    row = pl.BlockSpec((tm, D_MODEL), lambda i, k: (i, 0))
    vec = pl.BlockSpec((1, D_MODEL), lambda i, k: (0, 0))
    return pl.pallas_call(
        body, name="dh_bwd", grid=(s // tm, nk),
        in_specs=[pl.BlockSpec((tm, tk), lambda i, k: (i, k)), pl.BlockSpec((tk, D_MODEL), lambda i, k: (k, 0)), row, row, vec,
                  pl.BlockSpec(memory_space=pl.ANY)],
        out_specs=[row, row, vec], out_shape=[SDS((s, D_MODEL), f32), SDS((s, D_MODEL), bf16), SDS((1, D_MODEL), f32)],
        scratch_shapes=[pltpu.VMEM((tm, D_MODEL), f32)], compiler_params=_cp("arbitrary", "arbitrary"),
    )(dp, w_t, x, dx1, g, after)


def _layer_fwd(x, w, sm):
    p, h = _rms_mm(x, sm["g_mix"], w["w_in"])
    z, opre, states = _mixer_fwd(p, sm["lb"], sm["g_out"], sm["w_conv"], sm["ln_g"], sm["ln_b"], sm["w_sg"], sm["b_sg_t"])
    y, merged = _branch_gate(z, w["w_branch"], p)
    x1 = _mm_resid(x, merged, w["w_o"])
    x2, h2, ra = _ffn(x1, sm["g_ffn"], w["w_ff1"], w["w_ff2"])
    saved = dict(x=x, p=p, h=h, z=z, opre=opre, states=states, y=y, merged=merged, x1=x1, h2=h2, ra=ra)
    return x2, saved


def _layer_bwd(dx2, dx2b, sv, w, sm, between, before_end):
    nchip = N_DEV // 2
    by_chip = lambda g: g.reshape((nchip, 2) + g.shape[1:])
    da, act, dx1, dx1b, dg_ffn = _ffn_bwd(dx2, dx2b, sv["x1"], sm["g_ffn"], sv["ra"], w["w_ff1"], w["w_ff2"])
    g_ff2 = by_chip(_mm_tn(act, dx2b, 1, D_FF, D_MODEL, 512, 1024, name="dw_ff2")[0].reshape(N_DEV, D_FF // N_DEV, D_MODEL))
    g_ff1 = by_chip(_mm_tn_slabs(sv["h2"], da, 1, D_MODEL, D_FF // 2, [i * (D_FF // N_DEV) for i in range(nchip)], D_FF // N_DEV,
                                 name="dw_ff1")[:, 0])
    g_o = by_chip(_mm_tn(sv["merged"], dx1b, 1, D_MODEL, D_MODEL, 512, 1024, name="dw_o")[0].reshape(N_DEV, D_MODEL // N_DEV, D_MODEL))
    dy, dp, dz = _merge_bwd(dx1b, w["w_o"], sv["y"], sv["p"], w["w_branch"], between(dx1))
    g_branch = by_chip(_mm_tn_slabs(sv["z"], dy, 3, BRANCH, D_MODEL, [i * (D_MODEL // N_DEV) for i in range(N_DEV)], D_MODEL // N_DEV,
                                    name="dw_branch"))
    dp, vecs, dwsg, dbsg_t = _mixer_bwd(sv["p"], dz, sv["opre"], sv["states"], dp, sm["lb"], sm["g_out"], sm["w_conv"],
                                        sm["ln_g"], sm["ln_b"], sm["w_sg"], sm["b_sg_t"])
    g_in = by_chip(_mm_tn(dp, sv["h"], 1, N_COLS, D_MODEL, 512, 1024, name="dw_in")[0].reshape(N_DEV, SHARD_IN, D_MODEL))
    dx, dxb, dg_mix = _dh_bwd(dp, w["w_in"], sv["x"], dx1, sm["g_mix"], before_end(g_in))
    big = dict(w_in=g_in, w_branch=g_branch, w_o=g_o, w_ff1=g_ff1, w_ff2=g_ff2)
    small = dict(g_mix=dg_mix, g_ffn=dg_ffn, vecs=vecs, w_sg=dwsg, b_sg_t=dbsg_t, dx1=dx1)
    return dx, dxb, big, small


BIG = ("w_in", "w_branch", "w_o", "w_ff1", "w_ff2")
ANY = pl.BlockSpec(memory_space=pl.ANY)


def _place():
    return lax.axis_index("x"), lax.axis_index("y"), lax.axis_index("c")


def _al(v, m):
    return pl.multiple_of(v * m, m)


def _shard_of(refs, dev):
    w_in, w_b, w_o, w_1, w_2 = refs
    ni, nb, no, n1, n2 = w_in.shape[0] // N_DEV, w_b.shape[-1] // N_DEV, w_o.shape[0] // N_DEV, w_1.shape[-1] // N_DEV, w_2.shape[0] // N_DEV
    return [w_in.at[pl.ds(_al(dev, ni), ni), :], w_b.at[:, :, pl.ds(_al(dev, nb), nb)], w_o.at[pl.ds(_al(dev, no), no), :],
            w_1.at[:, pl.ds(_al(dev, n1), n1)], w_2.at[pl.ds(_al(dev, n2), n2), :]]


def _gather_out_shapes(shards):
    s_in, s_b, s_o, s_1, s_2 = (shards[n] for n in BIG)
    return [SDS((s_in.shape[1] * N_DEV, s_in.shape[2]), bf16), SDS(s_b.shape[1:3] + (s_b.shape[3] * N_DEV,), bf16),
            SDS((s_o.shape[1] * N_DEV, s_o.shape[2]), bf16), SDS((s_1.shape[1], s_1.shape[2] * N_DEV), bf16),
            SDS((s_2.shape[1] * N_DEV, s_2.shape[2]), bf16)]


def _seq_all_gather_layer(layer, shard_refs, out_shapes):
    nt = len(BIG)
    outs = [jax.empty_ref(sh, memory_space=pltpu.MemorySpace.HBM) for sh in out_shapes]

    @pl.kernel(mesh=plsc.ScalarSubcoreMesh(axis_name="seq", num_cores=1), name=f"seq_all_gather_l{layer}",
               scratch_types=(pltpu.SemaphoreType.DMA((7,)), pltpu.SemaphoreType.DMA((7,))),
               compiler_params=pltpu.CompilerParams(collective_id=1))
    def launch(send_sems, recv_sems):
        x, y, c = _place()
        me, sibling = (x, y, c), (x, y, 1 - c)
        chips = [(1 - x, y), (x, 1 - y), (1 - x, 1 - y)]
        peers = [sibling] + [(*chip, c) for chip in chips]
        barrier = pltpu.get_barrier_semaphore()
        for p in peers:
            pl.semaphore_signal(barrier, inc=1, device_id=p, device_id_type=MESH)
        pl.semaphore_wait(barrier, len(peers))
        mine = [r.at[layer] for r in shard_refs]

        def block(px, py, pc):
            return _shard_of(outs, 4 * px + 2 * py + pc)

        def copies(k, blk, to, src=None):
            dst = block(*blk)
            src = dst if src is None else src
            return [pltpu.make_async_remote_copy(src_ref=src[t], dst_ref=dst[t], send_sem=send_sems.at[k], recv_sem=recv_sems.at[k],
                                                 device_id=to, device_id_type=MESH) for t in range(nt)]

        first = copies(0, me, sibling, src=mine)
        for j, chip in enumerate(chips):
            first += copies(1 + j, me, (*chip, c), src=mine)
        for cp in first:
            cp.start()
        passed = [copies(4 + j, (*chip, c), sibling) for j, chip in enumerate(chips)]
        for j, chip in enumerate(chips):
            for cp in copies(1 + j, (*chip, c), me):
                cp.wait_recv()
            for cp in passed[j]:
                cp.start()
        for cp in copies(0, sibling, me):
            cp.wait_recv()
        for j, chip in enumerate(chips):
            for cp in copies(4 + j, (*chip, 1 - c), me):
                cp.wait_recv()
        for cp in first + [cp for grp in passed for cp in grp]:
            cp.wait_send()

    launch()
    return [o[...] for o in outs]


def _place_own(where, shards, gathered, after):
    s_in, s_b, s_o, s_1, s_2 = (shards[n] for n in BIG)
    nt = len(BIG)

    def body(where_ref, *refs):
        del where_ref
        for src, dst in zip(refs[:nt], refs[2 * nt + 1:]):
            dst[...] = src[...]

    lay = lambda shape, fn: pl.BlockSpec(shape, fn)
    in_specs = [lay((None,) + s_in.shape[1:], lambda i, wh: (wh[0], 0, 0)), lay((None,) + s_b.shape[1:], lambda i, wh: (wh[0], 0, 0, 0)),
                lay((None,) + s_o.shape[1:], lambda i, wh: (wh[0], 0, 0)), lay((None,) + s_1.shape[1:], lambda i, wh: (wh[0], 0, 0)),
                lay((None,) + s_2.shape[1:], lambda i, wh: (wh[0], 0, 0))]
    out_specs = [lay(s_in.shape[1:], lambda i, wh: (wh[1], 0)), lay(s_b.shape[1:], lambda i, wh: (0, 0, wh[1])),
                 lay(s_o.shape[1:], lambda i, wh: (wh[1], 0)), lay(s_1.shape[1:], lambda i, wh: (0, wh[1])),
                 lay(s_2.shape[1:], lambda i, wh: (wh[1], 0))]
    return pl.pallas_call(
        body, name="place_own", out_shape=[SDS(g.shape, g.dtype) for g in gathered],
        input_output_aliases={1 + nt + i: i for i in range(nt)}, compiler_params=_cp("arbitrary"),
        grid_spec=pltpu.PrefetchScalarGridSpec(num_scalar_prefetch=1, grid=(1,), in_specs=in_specs + [ANY] * (nt + 1), out_specs=out_specs),
    )(where, s_in, s_b, s_o, s_1, s_2, *gathered, after)


def _handshake(peers):
    barrier = pltpu.get_barrier_semaphore()
    for p in peers:
        pl.semaphore_signal(barrier, inc=1, device_id=p, device_id_type=MESH)
    pl.semaphore_wait(barrier, len(peers))


def _seq_exchange_on_chip(grads):
    nt, nchip = len(BIG), N_DEV // 2
    g_refs = [jax.new_ref(g, memory_space=pltpu.MemorySpace.HBM) for g in grads]
    outs = [jax.empty_ref(SDS((nchip,) + g.shape[2:], bf16), memory_space=pltpu.MemorySpace.HBM) for g in grads]

    @pl.kernel(mesh=plsc.ScalarSubcoreMesh(axis_name="seq", num_cores=1), name="seq_rs_on_chip",
               scratch_types=(pltpu.SemaphoreType.DMA((nchip,)), pltpu.SemaphoreType.DMA((nchip,))),
               compiler_params=pltpu.CompilerParams(collective_id=2))
    def launch(send_sems, recv_sems):
        x, y, c = _place()
        sibling = (x, y, 1 - c)
        _handshake([sibling])
        remote = [pltpu.make_async_remote_copy(src_ref=g_refs[t].at[j, 1 - c], dst_ref=outs[t].at[j], send_sem=send_sems.at[j],
                                               recv_sem=recv_sems.at[j], device_id=sibling, device_id_type=MESH)
                  for j in range(nchip) for t in range(nt)]
        for cp in remote:
            cp.start()
        for cp in remote:
            cp.wait_recv()
        for cp in remote:
            cp.wait_send()

    launch()
    return [o[...] for o in outs], [g[...] for g in g_refs]


def _seq_exchange_between_chips(sums):
    nt = len(BIG)
    s_refs = [jax.new_ref(a, memory_space=pltpu.MemorySpace.HBM) for a in sums]
    outs = [jax.empty_ref(SDS((3,) + a.shape[1:], bf16), memory_space=pltpu.MemorySpace.HBM) for a in sums]

    @pl.kernel(mesh=plsc.ScalarSubcoreMesh(axis_name="seq", num_cores=1), name="seq_rs_between_chips",
               scratch_types=(pltpu.SemaphoreType.DMA((3,)), pltpu.SemaphoreType.DMA((3,))),
               compiler_params=pltpu.CompilerParams(collective_id=3))
    def launch(send_sems, recv_sems):
        x, y, c = _place()
        chips = [(1 - x, y), (x, 1 - y), (1 - x, 1 - y)]
        _handshake([(cx, cy, c) for cx, cy in chips])
        remote = [pltpu.make_async_remote_copy(src_ref=s_refs[t].at[2 * cx + cy], dst_ref=outs[t].at[k], send_sem=send_sems.at[k],
                                               recv_sem=recv_sems.at[k], device_id=(cx, cy, c), device_id_type=MESH)
                  for k, (cx, cy) in enumerate(chips) for t in range(nt)]
        for cp in remote:
            cp.start()
        for cp in remote:
            cp.wait_recv()
        for cp in remote:
            cp.wait_send()

    launch()
    return [o[...] for o in outs], [a[...] for a in s_refs]


def _chip_sums(core, mine, other, after, steps=2):
    nt, nchip = len(mine), mine[0].shape[0]
    m4 = [a.reshape(nchip, 2, -1, a.shape[-1]) for a in mine]
    o3 = [a.reshape(nchip, -1, a.shape[-1]) for a in other]

    def body(c_ref, *refs):
        del c_ref
        for a_ref, b_ref, o_ref in zip(refs[:nt], refs[nt:2 * nt], refs[2 * nt + 1:]):
            o_ref[...] = (a_ref[...].astype(f32) + b_ref[...].astype(f32)).astype(bf16)

    tiles = [(a.shape[1] // steps, a.shape[2]) for a in o3]
    blks = [pl.BlockSpec((None,) + t, lambda j, i, c_ref: (j, i, 0)) for t in tiles]
    outs = pl.pallas_call(
        body, name="chip_sums", out_shape=[SDS(a.shape, bf16) for a in o3], compiler_params=_cp("parallel", "parallel"),
        grid_spec=pltpu.PrefetchScalarGridSpec(
            num_scalar_prefetch=1, grid=(nchip, steps),
            in_specs=[pl.BlockSpec((None, None) + t, lambda j, i, c_ref: (j, c_ref[0], i, 0)) for t in tiles] + blks + [ANY],
            out_specs=blks),
    )(core, *m4, *o3, after)
    return [o.reshape(a.shape) for o, a in zip(outs, other)]


def _all_reduce_rows(pack):
    rows = pack.shape[0]
    blk = rows // N_DEV

    def body(in_ref, out_ref, land, send1, recv1, send2, recv2):
        x, y, c = _place()
        me = 4 * x + 2 * y + c
        others = [(px, py, pc) for px in range(2) for py in range(2) for pc in range(2)]

        def is_me(p):
            return jnp.logical_and(jnp.logical_and(p[0] == x, p[1] == y), p[2] == c)

        land[me] = in_ref[pl.ds(_al(me, blk), blk), :]
        for d, p in enumerate(others):
            @pl.when(jnp.logical_not(is_me(p)))
            def _():
                pltpu.make_async_remote_copy(src_ref=in_ref.at[pl.ds(d * blk, blk), :], dst_ref=land.at[me], send_sem=send1.at[d],
                                             recv_sem=recv1.at[me], device_id=p, device_id_type=MESH).start()
        for d, p in enumerate(others):
            @pl.when(jnp.logical_not(is_me(p)))
            def _():
                cp = pltpu.make_async_remote_copy(src_ref=in_ref.at[pl.ds(d * blk, blk), :], dst_ref=land.at[d], send_sem=send1.at[d],
                                                  recv_sem=recv1.at[d], device_id=p, device_id_type=MESH)
                cp.wait_recv()
                cp.wait_send()
        total = land[0]
        for d in range(1, N_DEV):
            total = total + land[d]
        out_ref[pl.ds(_al(me, blk), blk), :] = total
        for d, p in enumerate(others):
            @pl.when(jnp.logical_not(is_me(p)))
            def _():
                mine = out_ref.at[pl.ds(_al(me, blk), blk), :]
                pltpu.make_async_remote_copy(src_ref=mine, dst_ref=mine, send_sem=send2.at[d], recv_sem=recv2.at[me],
                                             device_id=p, device_id_type=MESH).start()
        for d, p in enumerate(others):
            @pl.when(jnp.logical_not(is_me(p)))
            def _():
                theirs = out_ref.at[pl.ds(d * blk, blk), :]
                cp = pltpu.make_async_remote_copy(src_ref=theirs, dst_ref=theirs, send_sem=send2.at[d], recv_sem=recv2.at[d],
                                                  device_id=p, device_id_type=MESH)
                cp.wait_recv()
                cp.wait_send()

    vm = pl.BlockSpec(memory_space=pltpu.VMEM)
    return pl.pallas_call(
        body, name="all_reduce_rows", in_specs=[vm], out_specs=vm, out_shape=SDS((rows, LANE), f32),
        scratch_shapes=[pltpu.VMEM((N_DEV, blk, LANE), f32)] + [pltpu.SemaphoreType.DMA((N_DEV,))] * 4,
        compiler_params=pltpu.CompilerParams(vmem_limit_bytes=VMEM_LIMIT),
    )(pack)


def _lower_bounds_fwd(lower):
    def body(l_ref, o_ref):
        sm = _layer_softmax(l_ref)
        run = jnp.zeros_like(sm[0])
        for l in range(DEPTH):
            o_ref[l:l + 1, :] = run
            if l + 1 < DEPTH:
                run = run + sm[l + 1]

    return pl.pallas_call(body, name="lower_bounds_fwd", out_shape=SDS(lower.shape, f32))(lower)


def _layer_softmax(l_ref):
    rows = [l_ref[l:l + 1, :] for l in range(DEPTH)]
    top = functools.reduce(jnp.maximum, rows)
    e = [jnp.exp(r - top) for r in rows]
    tot = functools.reduce(lambda a, b: a + b, e)
    return [v / tot for v in e]


def _lower_bounds_bwd(lower, dlbs):
    def body(l_ref, d_ref, o_ref):
        sm = _layer_softmax(l_ref)
        dsm = [None] * DEPTH
        run = jnp.zeros_like(sm[0])
        dsm[0] = run
        for l in reversed(range(1, DEPTH)):
            run = run + d_ref[l:l + 1, :]
            dsm[l] = run
        inner = functools.reduce(lambda a, b: a + b, [sm[l] * dsm[l] for l in range(DEPTH)])
        for l in range(DEPTH):
            o_ref[l:l + 1, :] = sm[l] * (dsm[l] - inner)

    return pl.pallas_call(body, name="lower_bounds_bwd", out_shape=SDS(lower.shape, f32))(lower, dlbs)


_ADAM_C1 = 1.0 - ADAM_B1 ** ADAM_STEP
_ADAM_C2 = 1.0 - ADAM_B2 ** ADAM_STEP


def _adamw(w, g, m, v):
    m = ADAM_B1 * m + (1.0 - ADAM_B1) * g
    v = ADAM_B2 * v + (1.0 - ADAM_B2) * (g * g)
    delta = -ADAM_LR * ((m / _ADAM_C1) / (jnp.sqrt(v / _ADAM_C2) + ADAM_EPS) + ADAM_WD * w)
    return delta, m, v


def _row_tile(rows, cap):
    return next(t for t in range(min(cap, rows) // 16 * 16, 0, -16) if rows % t == 0)


def _adam_big(where, w, m, v, sums, landed, outs, after, tr=512):
    shape = w.shape
    cols = shape[-1]
    w3, m3, v3 = (a.reshape(DEPTH, -1, cols) for a in (w, m, v))
    outs3 = [a.reshape(DEPTH, -1, cols) for a in outs]
    sums3 = sums.reshape(sums.shape[0], -1, cols)
    land3 = landed.reshape(3, -1, cols)
    rows = w3.shape[1]
    tr = _row_tile(rows, tr)
    gcols = cols

    def body(where_ref, w_ref, m_ref, v_ref, sum_ref, land_ref, *rest):
        del where_ref
        g_ref, d_ref, nm_ref, nv_ref = rest[5:]
        g = sum_ref[...].astype(f32)
        for k in range(3):
            g = g + land_ref[k].astype(f32)
        delta, nm, nv = _adamw(w_ref[...], g, m_ref[...], v_ref[...])
        g_ref[...] = g
        d_ref[...] = delta
        nm_ref[...] = nm
        nv_ref[...] = nv

    blk = pl.BlockSpec((None, tr, cols), lambda i, wh: (wh[0], i, 0))
    res = pl.pallas_call(
        body, name="adam_big", out_shape=[SDS(w3.shape, f32)] * 4, input_output_aliases={6 + i: i for i in range(4)},
        compiler_params=_cp("parallel"),
        grid_spec=pltpu.PrefetchScalarGridSpec(
            num_scalar_prefetch=1, grid=(rows // tr,),
            in_specs=[blk, blk, blk, pl.BlockSpec((None, tr, gcols), lambda i, wh: (wh[1], i, 0)),
                      pl.BlockSpec((3, tr, gcols), lambda i, wh: (0, i, 0))] + [ANY] * 5,
            out_specs=[blk] * 4),
    )(where, w3, m3, v3, sums3, land3, *outs3, after)
    return [o.reshape(shape) for o in res]


def _adam_rows(w, g, m, v):
    def body(w_ref, g_ref, m_ref, v_ref, d_ref, nm_ref, nv_ref):
        delta, nm, nv = _adamw(w_ref[...], g_ref[...], m_ref[...], v_ref[...])
        d_ref[...] = delta
        nm_ref[...] = nm
        nv_ref[...] = nv

    return pl.pallas_call(body, name="adam_rows", out_shape=[SDS(w.shape, f32)] * 3)(w, g, m, v)


SMALL = ("g_mix", "lower_bounds", "g_hgrn_out", "w_conv", "sg_ln_g", "sg_ln_b", "w_sg", "b_sg", "g_ffn", "g_final")
WEIGHTS = ("w_in", "g_mix", "lower_bounds", "g_hgrn_out", "w_conv", "sg_ln_g", "sg_ln_b", "w_sg", "b_sg", "w_branch", "w_o", "g_ffn",
           "w_ff1", "w_ff2", "g_final")


def _pack_rows(arrays, multiple):
    flat = jnp.concatenate([a.reshape(-1) for a in arrays])
    rows = -(-flat.shape[0] // (LANE * multiple)) * multiple
    return jnp.pad(flat, (0, rows * LANE - flat.shape[0])).reshape(rows, LANE)


def _unpack_rows(pack, like):
    flat = pack.reshape(-1)
    out, at = [], 0
    for a in like:
        out.append(flat[at:at + a.size].reshape(a.shape))
        at += a.size
    return out


def kernel(x, w_in, g_mix, lower_bounds, g_hgrn_out, w_conv, sg_ln_g, sg_ln_b, w_sg, b_sg, w_branch, w_o, g_ffn, w_ff1, w_ff2, g_final, loss_target, m_w_in, m_g_mix, m_lower_bounds, m_g_hgrn_out, m_w_conv, m_sg_ln_g, m_sg_ln_b, m_w_sg, m_b_sg, m_w_branch, m_w_o, m_g_ffn, m_w_ff1, m_w_ff2, m_g_final, v_w_in, v_g_mix, v_lower_bounds, v_g_hgrn_out, v_w_conv, v_sg_ln_g, v_sg_ln_b, v_w_sg, v_b_sg, v_w_branch, v_w_o, v_g_ffn, v_w_ff1, v_w_ff2, v_g_final):
    weights = dict(w_in=w_in, g_mix=g_mix, lower_bounds=lower_bounds, g_hgrn_out=g_hgrn_out, w_conv=w_conv, sg_ln_g=sg_ln_g,
                   sg_ln_b=sg_ln_b, w_sg=w_sg, b_sg=b_sg, w_branch=w_branch, w_o=w_o, g_ffn=g_ffn, w_ff1=w_ff1, w_ff2=w_ff2, g_final=g_final)
    mom1 = dict(w_in=m_w_in, g_mix=m_g_mix, lower_bounds=m_lower_bounds, g_hgrn_out=m_g_hgrn_out, w_conv=m_w_conv, sg_ln_g=m_sg_ln_g,
                sg_ln_b=m_sg_ln_b, w_sg=m_w_sg, b_sg=m_b_sg, w_branch=m_w_branch, w_o=m_w_o, g_ffn=m_g_ffn, w_ff1=m_w_ff1, w_ff2=m_w_ff2,
                g_final=m_g_final)
    mom2 = dict(w_in=v_w_in, g_mix=v_g_mix, lower_bounds=v_lower_bounds, g_hgrn_out=v_g_hgrn_out, w_conv=v_w_conv, sg_ln_g=v_sg_ln_g,
                sg_ln_b=v_sg_ln_b, w_sg=v_w_sg, b_sg=v_b_sg, w_branch=v_w_branch, w_o=v_w_o, g_ffn=v_g_ffn, w_ff1=v_w_ff1, w_ff2=v_w_ff2,
                g_final=v_g_final)
    xi, yi, ci = _place()
    dev = 4 * xi + 2 * yi + ci
    conv_cols = w_conv.shape[-1]

    for d in (weights, mom1, mom2):
        d["w_in"] = jnp.swapaxes(d["w_in"], 1, 2)
    shards = {n: weights[n].astype(bf16) for n in BIG}

    conv_place = lax.dynamic_update_slice(jnp.zeros((DEPTH, 3, BRANCH), f32), w_conv, (0, 0, dev * conv_cols))
    (w_conv_full,) = _unpack_rows(_all_reduce_rows(_pack_rows([conv_place], 8 * N_DEV)), [conv_place])
    lbs = _lower_bounds_fwd(lower_bounds)

    def small_of(l):
        return dict(g_mix=g_mix[l][None], lb=lbs[l][None], g_out=g_hgrn_out[l][None], w_conv=w_conv_full[l], ln_g=sg_ln_g[l][None],
                    ln_b=sg_ln_b[l][None], w_sg=w_sg[l], b_sg_t=b_sg[l].T, g_ffn=g_ffn[l][None])

    act = x[0]
    full, saved = [], []
    shard_refs = [jax.new_ref(shards[n], memory_space=pltpu.MemorySpace.HBM) for n in BIG]
    gathered = [_seq_all_gather_layer(l, shard_refs, _gather_out_shapes(shards)) for l in range(DEPTH)]
    for l in range(DEPTH):
        full.append(dict(zip(BIG, _place_own(jnp.stack([jnp.int32(l), dev.astype(jnp.int32)]), shards, gathered[l], act))))
        act, sv = _layer_fwd(act, full[l], small_of(l))
        saved.append(sv)
    loss_row, dx, dxb, dg_final = _final(act, loss_target[0], g_final[None])
    loss = lax.psum(loss_row[0, 0], ("x", "y", "c"))

    core = ci.astype(jnp.int32)[None]
    big_out = {n: [lax.empty(weights[n].shape, f32) for _ in range(4)] for n in BIG}
    small_grads = [None] * DEPTH

    def chip_sums(stage, after):
        l, received, mine = stage
        sums = _chip_sums(core, mine, received, after)
        placed.append(sums[BIG.index("w_o")])
        landed, sums = _seq_exchange_between_chips(sums)
        return l, sums, landed

    def adam_layer(stage, after):
        l, sums, landed = stage
        where = jnp.stack([jnp.int32(l), (2 * xi + yi).astype(jnp.int32)])
        for t, n in enumerate(BIG):
            big_out[n] = _adam_big(where, weights[n], mom1[n], mom2[n], sums[t], landed[t], big_out[n], after)

    above = None
    placed = []
    for l in reversed(range(DEPTH)):
        summed = []

        def between(dx1):
            if above is None:
                return dx1
            summed.append(chip_sums(above, dx1))
            return placed[-1]

        def before_end(g_in):
            if not summed:
                return g_in
            adam_layer(summed[0], g_in)
            return big_out["w_o"][0]

        dx, dxb, big, small_grads[l] = _layer_bwd(dx, dxb, saved[l], full[l], small_of(l), between, before_end)
        above = (l, *_seq_exchange_on_chip([big[n] for n in BIG]))

    stack = lambda f: jnp.stack([f(small_grads[l]) for l in range(DEPTH)])
    d_lower = _lower_bounds_bwd(lower_bounds, stack(lambda s: s["vecs"][0]))
    local_small = dict(g_mix=stack(lambda s: s["g_mix"][0]), lower_bounds=d_lower, g_hgrn_out=stack(lambda s: s["vecs"][1]),
                       w_conv=stack(lambda s: s["vecs"][4:7]), sg_ln_g=stack(lambda s: s["vecs"][2]), sg_ln_b=stack(lambda s: s["vecs"][3]),
                       w_sg=stack(lambda s: s["w_sg"]), b_sg=stack(lambda s: s["b_sg_t"].T), g_ffn=stack(lambda s: s["g_ffn"][0]),
                       g_final=dg_final[0])
    order = [local_small[n] for n in SMALL]
    grads = dict(zip(SMALL, _unpack_rows(_all_reduce_rows(_pack_rows(order, 8 * N_DEV)), order)))
    grads["w_conv"] = lax.dynamic_slice(grads["w_conv"], (0, 0, dev * conv_cols), (DEPTH, 3, conv_cols))

    deltas, new_m, new_v = {}, {}, {}
    packs = [_pack_rows([d[n] for n in SMALL], 8) for d in (weights, grads, mom1, mom2)]
    like = [weights[n] for n in SMALL]
    small_out = _adam_rows(*packs)
    for out, pack in zip((deltas, new_m, new_v), small_out):
        out.update(zip(SMALL, _unpack_rows(pack, like)))
    adam_layer(chip_sums(above, dx), small_out[0])
    for n in BIG:
        grads[n], deltas[n], new_m[n], new_v[n] = (jnp.swapaxes(a, 1, 2) if n == "w_in" else a for a in big_out[n])

    return (loss, dx[None], *[grads[n] for n in WEIGHTS], *[deltas[n] for n in WEIGHTS], *[new_m[n] for n in WEIGHTS],
            *[new_v[n] for n in WEIGHTS])
```

```python
import functools

import jax
import jax.numpy as jnp
from jax import lax
from jax.experimental import pallas as pl
from jax.experimental.pallas import tpu as pltpu
from jax.experimental.pallas import tpu_sc as plsc

f32 = jnp.float32
bf16 = jnp.bfloat16
SDS = jax.ShapeDtypeStruct
MESH = pl.DeviceIdType.MESH

D_MODEL = 1024
BRANCH = 512
N_COLS = 7680
D_FF = 4096
DEPTH = 4
HEADS = 4
HEAD_DIM = 128
HGRN_CHUNK = 64
SG_CHUNK = 128
SG_GROUPS = 4
NORM_EPS = 1e-6
LN_EPS = 1e-5
LB_FLOOR = 1e-30
N_DEV = 8
SHARD_IN = N_COLS // N_DEV
WIN = 1024
LANE = 128
GATE_COL0 = 9 * BRANCH

ADAM_LR = 0.001
ADAM_B1 = 0.9
ADAM_B2 = 0.999
ADAM_EPS = 1e-08
ADAM_WD = 0.01
ADAM_STEP = 10

MIX_TILE = 256
VMEM_LIMIT = 56 * 1024 * 1024


def _cp(*sem):
    return pltpu.CompilerParams(dimension_semantics=sem or None, vmem_limit_bytes=VMEM_LIMIT)


def _dot(a, b):
    return jnp.dot(a, b, preferred_element_type=f32)


def _dot_nt(a, b):
    return lax.dot_general(a, b, (((1,), (1,)), ((), ())), preferred_element_type=f32)


def _dot_tn(a, b):
    return lax.dot_general(a, b, (((0,), (0,)), ((), ())), preferred_element_type=f32)


def _dot_exact(a, b):
    return jnp.dot(a, b, precision=lax.Precision.HIGHEST, preferred_element_type=f32)


def _sigmoid(x):
    return jax.nn.sigmoid(x)


_GELU_C = 0.7978845608028654
_GELU_A = 0.044715


def _gelu(x):
    return 0.5 * x * (1.0 + jnp.tanh(_GELU_C * (x + _GELU_A * x * x * x)))


def _gelu_grad(x):
    x2 = x * x
    t = jnp.tanh(_GELU_C * (x + _GELU_A * x * x2))
    return 0.5 * (1.0 + t) + 0.5 * x * (1.0 - t * t) * _GELU_C * (1.0 + 3.0 * _GELU_A * x2)


def _rms_stats(x):
    r = lax.rsqrt(jnp.mean(x * x, axis=-1, keepdims=True) + NORM_EPS)
    return r, x * r


def _rms_bwd(dh, xh, r, g):
    dg = jnp.sum(dh * xh, axis=0, keepdims=True)
    dxn = dh * g
    dx = r * (dxn - xh * jnp.mean(dxn * xh, axis=-1, keepdims=True))
    return dx, dg


def _tri(n, upper=False):
    r = lax.broadcasted_iota(jnp.int32, (n, n), 0)
    c = lax.broadcasted_iota(jnp.int32, (n, n), 1)
    return (c >= r) if upper else (c <= r)


def _acc_rows(ref, first, val):
    @pl.when(first)
    def _():
        ref[...] = val

    @pl.when(jnp.logical_not(first))
    def _():
        ref[...] += val


def _rms_mm(x, g, w_t, tm=1024, tn=1920):
    s, n = x.shape[0], w_t.shape[0]

    def body(x_ref, g_ref, w_ref, p_ref, h_ref, hs):
        @pl.when(pl.program_id(1) == 0)
        def _():
            _, xh = _rms_stats(x_ref[...])
            hv = (xh * g_ref[...]).astype(bf16)
            hs[...] = hv
            h_ref[...] = hv

        p_ref[...] = _dot_nt(hs[...], w_ref[...])

    return pl.pallas_call(
        body, name="rms_mm", grid=(s // tm, n // tn),
        in_specs=[pl.BlockSpec((tm, D_MODEL), lambda i, j: (i, 0)), pl.BlockSpec((1, D_MODEL), lambda i, j: (0, 0)),
                  pl.BlockSpec((tn, D_MODEL), lambda i, j: (j, 0))],
        out_specs=[pl.BlockSpec((tm, tn), lambda i, j: (i, j)), pl.BlockSpec((tm, D_MODEL), lambda i, j: (i, 0))],
        out_shape=[SDS((s, n), f32), SDS((s, D_MODEL), bf16)],
        scratch_shapes=[pltpu.VMEM((tm, D_MODEL), bf16)], compiler_params=_cp("parallel", "arbitrary"),
    )(x, g, w_t)


def _hgrn_gates(fp, lb):
    logf = jnp.logaddexp(jnp.log(jnp.maximum(lb, LB_FLOOR)), jnp.log1p(-lb) + jax.nn.log_sigmoid(fp))
    snf = _sigmoid(-fp)
    return logf, snf, (1.0 - lb) * snf


def _p_specs(tile, cols, row_map):
    return [pl.BlockSpec((tile, BRANCH), functools.partial(lambda c, i: (row_map(i), c), c)) for c in cols]


def _mixer_fwd(p, lb, gout, wconv, lng, lnb, wsg, bsg_t):
    s = p.shape[0]
    tt = MIX_TILE
    nch = tt // HGRN_CHUNK

    def body(q_ref, fp_ref, iv_ref, go_ref, bg_ref, cg_ref, xc_ref, u_ref, v_ref, lb_ref, gout_ref, wconv_ref, lng_ref,
             lnb_ref, wsg_ref, bsg_ref, z_ref, opre_ref, st_ref, st_scr, zbuf):
        @pl.when(pl.program_id(0) == 0)
        def _():
            st_scr[...] = jnp.zeros_like(st_scr)
            zbuf[0:8, :] = jnp.zeros((8, BRANCH), f32)

        lbv = lb_ref[...]
        q_raw = q_ref[...]
        qs = q_raw * _sigmoid(q_raw)
        logf, _, kk = _hgrn_gates(fp_ref[...], lbv)
        iv = iv_ref[...]
        causal = _tri(HGRN_CHUNK)
        tri = causal.astype(f32)
        last_row = lax.broadcasted_iota(jnp.int32, (HGRN_CHUNK, 1), 0) == HGRN_CHUNK - 1
        for c in range(nch):
            rows = slice(HGRN_CHUNK * c, HGRN_CHUNK * (c + 1))
            b = _dot_exact(tri, logf[rows])
            bl = jnp.sum(jnp.where(last_row, b, 0.0), axis=0, keepdims=True)
            qb = (qs[rows] * jnp.exp(b)).astype(bf16)
            kb = (kk[rows] * jnp.exp(-b)).astype(bf16)
            kd = (kk[rows] * jnp.exp(bl - b)).astype(bf16)
            ebl = jnp.exp(bl)
            vc = iv[rows].astype(bf16)
            for h in range(HEADS):
                sl = slice(HEAD_DIM * h, HEAD_DIM * (h + 1))
                st = st_scr[h]
                st_ref[c, h] = st
                a = jnp.where(causal, _dot_nt(qb[:, sl], kb[:, sl]), 0.0)
                opre_ref[rows, sl] = _dot(a.astype(bf16), vc[:, sl]) + _dot_nt(qb[:, sl], st.astype(bf16))
                st_scr[h] = st * ebl[:, sl] + _dot_tn(vc[:, sl], kd[:, sl])

        o = opre_ref[...]
        go = go_ref[...]
        gout_v = gout_ref[...]
        for h in range(HEADS):
            sl = slice(HEAD_DIM * h, HEAD_DIM * (h + 1))
            _, oh = _rms_stats(o[:, sl])
            z_ref[:, sl] = (oh * gout_v[:, sl] * _sigmoid(go[:, sl])).astype(bf16)

        zc = cg_ref[...] * xc_ref[...]
        zbuf[8:8 + tt, :] = zc
        y = wconv_ref[0:1, :] * zbuf[pl.ds(6, tt), :] + wconv_ref[1:2, :] * zbuf[pl.ds(7, tt), :] + wconv_ref[2:3, :] * zc
        z_ref[:, BRANCH:2 * BRANCH] = (bg_ref[...] * y).astype(bf16)
        zbuf[0:8, :] = zbuf[tt:tt + 8, :]

        ug = _gelu(u_ref[...])
        vg = _gelu(v_ref[...])
        vcen = vg - jnp.mean(vg, axis=-1, keepdims=True)
        rstd = lax.rsqrt(jnp.mean(vcen * vcen, axis=-1, keepdims=True) + LN_EPS)
        vn = (vcen * rstd * lng_ref[...] + lnb_ref[...]).astype(bf16)
        low = _tri(SG_CHUNK)
        for g in range(SG_GROUPS):
            sl = slice(LANE * g, LANE * (g + 1))
            wm = jnp.where(low, wsg_ref[g], 0.0).astype(bf16)
            bias = bsg_ref[:, g:g + 1]
            for cc in range(tt // SG_CHUNK):
                rows = slice(SG_CHUNK * cc, SG_CHUNK * (cc + 1))
                sv = _dot(wm, vn[rows, sl]) + bias
                z_ref[rows, 2 * BRANCH + LANE * g:2 * BRANCH + LANE * (g + 1)] = (ug[rows, sl] * sv).astype(bf16)

    full = lambda shape: pl.BlockSpec(shape, lambda i: (0,) * len(shape))
    return pl.pallas_call(
        body, name="mixer_fwd", grid=(s // tt,),
        in_specs=_p_specs(tt, range(9), lambda i: i) + [full((1, BRANCH)), full((1, BRANCH)), full((3, BRANCH)), full((1, BRANCH)),
                                                        full((1, BRANCH)), full((SG_GROUPS, SG_CHUNK, SG_CHUNK)), full((SG_CHUNK, SG_GROUPS))],
        out_specs=[pl.BlockSpec((tt, 3 * BRANCH), lambda i: (i, 0)), pl.BlockSpec((tt, BRANCH), lambda i: (i, 0)),
                   pl.BlockSpec((nch, HEADS, HEAD_DIM, HEAD_DIM), lambda i: (i, 0, 0, 0))],
        out_shape=[SDS((s, 3 * BRANCH), bf16), SDS((s, BRANCH), f32), SDS((s // HGRN_CHUNK, HEADS, HEAD_DIM, HEAD_DIM), f32)],
        scratch_shapes=[pltpu.VMEM((HEADS, HEAD_DIM, HEAD_DIM), f32), pltpu.VMEM((tt + 8, BRANCH), f32)],
        compiler_params=_cp("arbitrary"),
    )(*([p] * 9), lb, gout, wconv, lng, lnb, wsg, bsg_t)


def _branch_gate(z, wb, p, tm=256):
    s = z.shape[0]
    half = 3 * D_MODEL // 2

    def body(z_ref, wb_ref, ga_ref, gb_ref, y_ref, m_ref):
        ga, gb = ga_ref[...], gb_ref[...]
        gates = [ga[:, :D_MODEL], jnp.concatenate([ga[:, D_MODEL:], gb[:, :D_MODEL // 2]], axis=1), gb[:, D_MODEL // 2:]]
        acc = None
        for n in range(3):
            yn = _dot(z_ref[:, BRANCH * n:BRANCH * (n + 1)], wb_ref[n])
            y_ref[:, D_MODEL * n:D_MODEL * (n + 1)] = yn.astype(bf16)
            t = _sigmoid(gates[n]) * yn
            acc = t if acc is None else acc + t
        m_ref[...] = acc.astype(bf16)

    blk0 = GATE_COL0 // half
    return pl.pallas_call(
        body, name="branch_gate", grid=(s // tm,),
        in_specs=[pl.BlockSpec((tm, 3 * BRANCH), lambda i: (i, 0)), pl.BlockSpec((3, BRANCH, D_MODEL), lambda i: (0, 0, 0)),
                  pl.BlockSpec((tm, half), lambda i: (i, blk0)), pl.BlockSpec((tm, half), lambda i: (i, blk0 + 1))],
        out_specs=[pl.BlockSpec((tm, 3 * D_MODEL), lambda i: (i, 0)), pl.BlockSpec((tm, D_MODEL), lambda i: (i, 0))],
        out_shape=[SDS((s, 3 * D_MODEL), bf16), SDS((s, D_MODEL), bf16)], compiler_params=_cp("parallel"),
    )(z, wb, p, p)


def _mm_resid(x, m, wo, tm=512):
    s = x.shape[0]

    def body(x_ref, m_ref, w_ref, o_ref):
        o_ref[...] = x_ref[...] + _dot(m_ref[...], w_ref[...])

    return pl.pallas_call(
        body, name="mm_resid", grid=(s // tm,),
        in_specs=[pl.BlockSpec((tm, D_MODEL), lambda i: (i, 0)), pl.BlockSpec((tm, D_MODEL), lambda i: (i, 0)),
                  pl.BlockSpec((D_MODEL, D_MODEL), lambda i: (0, 0))],
        out_specs=pl.BlockSpec((tm, D_MODEL), lambda i: (i, 0)), out_shape=SDS((s, D_MODEL), f32), compiler_params=_cp("parallel"),
    )(x, m, wo)


def _ffn(x1, g, w1, w2, tm=512, tf=1024):
    s = x1.shape[0]
    nf = D_FF // tf

    def body(x_ref, g_ref, w1_ref, w2_ref, o_ref, h_ref, ra_ref, hs, acc):
        f = pl.program_id(1)

        @pl.when(f == 0)
        def _():
            _, xh = _rms_stats(x_ref[...])
            hv = (xh * g_ref[...]).astype(bf16)
            hs[...] = hv
            h_ref[...] = hv
            acc[...] = jnp.zeros_like(acc)

        ra = jnp.maximum(_dot(hs[...], w1_ref[...]), 0.0)
        ra_ref[...] = ra.astype(bf16)
        acc[...] += _dot((ra * ra).astype(bf16), w2_ref[...])

        @pl.when(f == nf - 1)
        def _():
            o_ref[...] = x_ref[...] + acc[...]

    return pl.pallas_call(
        body, name="ffn", grid=(s // tm, nf),
        in_specs=[pl.BlockSpec((tm, D_MODEL), lambda i, f: (i, 0)), pl.BlockSpec((1, D_MODEL), lambda i, f: (0, 0)),
                  pl.BlockSpec((D_MODEL, tf), lambda i, f: (0, f)), pl.BlockSpec((tf, D_MODEL), lambda i, f: (f, 0))],
        out_specs=[pl.BlockSpec((tm, D_MODEL), lambda i, f: (i, 0)), pl.BlockSpec((tm, D_MODEL), lambda i, f: (i, 0)),
                   pl.BlockSpec((tm, tf), lambda i, f: (i, f))],
        out_shape=[SDS((s, D_MODEL), f32), SDS((s, D_MODEL), bf16), SDS((s, D_FF), bf16)],
        scratch_shapes=[pltpu.VMEM((tm, D_MODEL), bf16), pltpu.VMEM((tm, D_MODEL), f32)], compiler_params=_cp("parallel", "arbitrary"),
    )(x1, g, w1, w2)


def _final(x, target, g, tm=512):
    s = x.shape[0]

    def body(x_ref, t_ref, g_ref, loss_ref, dx_ref, dxb_ref, dg_ref):
        first = pl.program_id(0) == 0
        gv = g_ref[...]
        r, xh = _rms_stats(x_ref[...])
        e = xh * gv - t_ref[...]
        tile_loss = 0.5 * jnp.sum(jnp.mean(e * e, axis=-1, keepdims=True), axis=0, keepdims=True)
        dx, dg = _rms_bwd(e * (1.0 / D_MODEL), xh, r, gv)
        dx_ref[...] = dx
        dxb_ref[...] = dx.astype(bf16)
        _acc_rows(dg_ref, first, dg)
        _acc_rows(loss_ref, first, jnp.broadcast_to(tile_loss, (1, LANE)))

    row = pl.BlockSpec((tm, D_MODEL), lambda i: (i, 0))
    return pl.pallas_call(
        body, name="final_loss", grid=(s // tm,), in_specs=[row, row, pl.BlockSpec((1, D_MODEL), lambda i: (0, 0))],
        out_specs=[pl.BlockSpec((1, LANE), lambda i: (0, 0)), row, row, pl.BlockSpec((1, D_MODEL), lambda i: (0, 0))],
        out_shape=[SDS((1, LANE), f32), SDS((s, D_MODEL), f32), SDS((s, D_MODEL), bf16), SDS((1, D_MODEL), f32)],
        compiler_params=_cp("arbitrary"),
    )(x, target, g)


def _ffn_bwd(dx2, dx2b, x1, g, ra, w1, w2, tm=512, tf=1024):
    s = x1.shape[0]
    nf = D_FF // tf

    def body(dx_ref, dxb_ref, x_ref, g_ref, ra_ref, w1_ref, w2_ref, da_ref, act_ref, dx1_ref, dx1b_ref, dg_ref, acc):
        i, f = pl.program_id(0), pl.program_id(1)

        @pl.when(f == 0)
        def _():
            acc[...] = jnp.zeros_like(acc)

        rav = ra_ref[...].astype(f32)
        da = (_dot_nt(dxb_ref[...], w2_ref[...]) * (2.0 * rav)).astype(bf16)
        da_ref[...] = da
        act_ref[...] = (rav * rav).astype(bf16)
        acc[...] += _dot_nt(da, w1_ref[...])

        @pl.when(f == nf - 1)
        def _():
            r, xh = _rms_stats(x_ref[...])
            dx, dg = _rms_bwd(acc[...], xh, r, g_ref[...])
            dx = dx + dx_ref[...]
            dx1_ref[...] = dx
            dx1b_ref[...] = dx.astype(bf16)
            _acc_rows(dg_ref, i == 0, dg)

    row = pl.BlockSpec((tm, D_MODEL), lambda i, f: (i, 0))
    col = pl.BlockSpec((tm, tf), lambda i, f: (i, f))
    return pl.pallas_call(
        body, name="ffn_bwd", grid=(s // tm, nf),
        in_specs=[row, row, row, pl.BlockSpec((1, D_MODEL), lambda i, f: (0, 0)), col,
                  pl.BlockSpec((D_MODEL, tf), lambda i, f: (0, f)), pl.BlockSpec((tf, D_MODEL), lambda i, f: (f, 0))],
        out_specs=[col, col, row, row, pl.BlockSpec((1, D_MODEL), lambda i, f: (0, 0))],
        out_shape=[SDS((s, D_FF), bf16), SDS((s, D_FF), bf16), SDS((s, D_MODEL), f32), SDS((s, D_MODEL), bf16), SDS((1, D_MODEL), f32)],
        scratch_shapes=[pltpu.VMEM((tm, D_MODEL), f32)], compiler_params=_cp("arbitrary", "arbitrary"),
    )(dx2, dx2b, x1, g, ra, w1, w2)


def _mm_tn(a, b, nb, m, n, tm, tn, name="mm_tn", rows=None, row0=0, into=None):
    s = a.shape[0]
    mi, nj = m // tm, n // tn
    rows = m if rows is None else rows
    blk0 = row0 // tm

    def body(a_ref, b_ref, *rest):
        rest[-1][...] = _dot_tn(a_ref[...], b_ref[...]).astype(bf16)

    extra = {} if into is None else dict(input_output_aliases={2: 0})
    return pl.pallas_call(
        body, name=name, grid=(nb, mi, nj),
        in_specs=[pl.BlockSpec((s, tm), lambda k, i, j: (0, k * mi + i)), pl.BlockSpec((s, tn), lambda k, i, j: (0, k * nj + j))]
        + ([] if into is None else [pl.BlockSpec(memory_space=pl.ANY)]),
        out_specs=pl.BlockSpec((None, tm, tn), lambda k, i, j: (k, blk0 + i, j)), out_shape=SDS((nb, rows, n), bf16),
        compiler_params=_cp("parallel", "parallel", "parallel"), **extra,
    )(a, b, *([] if into is None else [into]))


def _mm_tn_slabs(a, b, nb, m, nblk, rel, width, tm=512, name="mm_tn_slabs"):
    s = a.shape[0]
    n = b.shape[1] // nb
    ng, mi, nw = n // nblk, m // tm, len(rel)

    def body(a_ref, b_ref, o_ref):
        full = _dot_tn(a_ref[...], b_ref[...])
        for r, start in enumerate(rel):
            o_ref[r] = full[:, start:start + width].astype(bf16)

    return pl.pallas_call(
        body, name=name, grid=(nb, ng, mi),
        in_specs=[pl.BlockSpec((s, tm), lambda k, g, i: (0, k * mi + i)), pl.BlockSpec((s, nblk), lambda k, g, i: (0, k * ng + g))],
        out_specs=pl.BlockSpec((nw, None, tm, width), lambda k, g, i: (g, k, i, 0)), out_shape=SDS((ng * nw, nb, m, width), bf16),
        compiler_params=_cp("parallel", "parallel", "parallel"),
    )(a, b)


def _merge_bwd(dx1b, wo, y, p, wb, after, tm=256):
    s = dx1b.shape[0]
    half = 3 * D_MODEL // 2
    blk0 = GATE_COL0 // half

    def body(dx_ref, wo_ref, y_ref, ga_ref, gb_ref, wb_ref, after_ref, dy_ref, dg_ref, dz_ref):
        del after_ref
        dm = _dot_nt(dx_ref[...], wo_ref[...])
        ga, gb = ga_ref[...], gb_ref[...]
        gates = [ga[:, :D_MODEL], jnp.concatenate([ga[:, D_MODEL:], gb[:, :D_MODEL // 2]], axis=1), gb[:, D_MODEL // 2:]]
        for n in range(3):
            cols = slice(D_MODEL * n, D_MODEL * (n + 1))
            gate = _sigmoid(gates[n])
            t = dm * gate
            dy = t.astype(bf16)
            dy_ref[:, cols] = dy
            dg_ref[:, cols] = (t * y_ref[:, cols].astype(f32) * (1.0 - gate)).astype(bf16)
            dz_ref[:, BRANCH * n:BRANCH * (n + 1)] = _dot_nt(dy, wb_ref[n])

    wide = pl.BlockSpec((tm, 3 * D_MODEL), lambda i: (i, 0))
    return pl.pallas_call(
        body, name="merge_bwd", grid=(s // tm,),
        in_specs=[pl.BlockSpec((tm, D_MODEL), lambda i: (i, 0)), pl.BlockSpec((D_MODEL, D_MODEL), lambda i: (0, 0)), wide,
                  pl.BlockSpec((tm, half), lambda i: (i, blk0)), pl.BlockSpec((tm, half), lambda i: (i, blk0 + 1)),
                  pl.BlockSpec((3, BRANCH, D_MODEL), lambda i: (0, 0, 0)), pl.BlockSpec(memory_space=pl.ANY)],
        out_specs=[wide, wide, pl.BlockSpec((tm, 3 * BRANCH), lambda i: (i, 0))],
        out_shape=[SDS((s, 3 * D_MODEL), bf16), SDS((s, 3 * D_MODEL), bf16), SDS((s, 3 * BRANCH), f32)],
        compiler_params=_cp("parallel"),
    )(dx1b, wo, y, p, p, wb, after)


def _mixer_bwd(p, dz, opre, states, lb, gout, wconv, lng, lnb, wsg, bsg_t):
    s = p.shape[0]
    tt = MIX_TILE
    nt = s // tt
    nch = tt // HGRN_CHUNK
    rev = lambda i: nt - 1 - i

    def body(q_ref, fp_ref, iv_ref, go_ref, bg_ref, cg_ref, xc_ref, u_ref, v_ref, cgp_ref, xcp_ref, dz_ref, opre_ref, st_ref,
             lb_ref, gout_ref, wconv_ref, lng_ref, lnb_ref, wsg_ref, bsg_ref,
             dp_ref, vec_ref, dwsg_ref, dbsg_ref, dst_scr, zbuf, dybuf, dvn_scr, dbsg_acc):
        i = pl.program_id(0)

        @pl.when(i == 0)
        def _():
            dst_scr[...] = jnp.zeros_like(dst_scr)
            dybuf[tt:tt + 8, :] = jnp.zeros((8, BRANCH), f32)
            vec_ref[...] = jnp.zeros_like(vec_ref)
            dwsg_ref[...] = jnp.zeros_like(dwsg_ref)
            dbsg_acc[...] = jnp.zeros_like(dbsg_acc)

        lbv = lb_ref[...]
        q_raw, fp = q_ref[...], fp_ref[...]
        sq = _sigmoid(q_raw)
        qs = q_raw * sq
        sfp = _sigmoid(fp)
        logf, snf, kk = _hgrn_gates(fp, lbv)
        inv_f = jnp.exp(-logf)
        iv = iv_ref[...]
        doa = dz_ref[:, 0:BRANCH]
        o = opre_ref[...]
        sgo = _sigmoid(go_ref[...])
        gout_v = gout_ref[...]
        d_o, dgo, dgout = [], [], []
        for h in range(HEADS):
            sl = slice(HEAD_DIM * h, HEAD_DIM * (h + 1))
            r, oh = _rms_stats(o[:, sl])
            d_on = doa[:, sl] * sgo[:, sl]
            dgo.append(doa[:, sl] * oh * gout_v[:, sl] * sgo[:, sl] * (1.0 - sgo[:, sl]))
            dx, dg = _rms_bwd(d_on, oh, r, gout_v[:, sl])
            d_o.append(dx)
            dgout.append(dg)
        d_o = jnp.concatenate(d_o, axis=1)
        dp_ref[:, 3 * BRANCH:4 * BRANCH] = jnp.concatenate(dgo, axis=1).astype(bf16)
        vec_ref[1:2, :] += jnp.concatenate(dgout, axis=1)

        causal = _tri(HGRN_CHUNK)
        tri = causal.astype(f32)
        tri_up = _tri(HGRN_CHUNK, upper=True).astype(f32)
        last_row = lax.broadcasted_iota(jnp.int32, (HGRN_CHUNK, 1), 0) == HGRN_CHUNK - 1
        lb_live = (lbv > LB_FLOOR).astype(f32)
        dlb = jnp.zeros((1, BRANCH), f32)
        for c in reversed(range(nch)):
            rows = slice(HGRN_CHUNK * c, HGRN_CHUNK * (c + 1))
            b = _dot_exact(tri, logf[rows])
            bl = jnp.sum(jnp.where(last_row, b, 0.0), axis=0, keepdims=True)
            eb, enb, edl, ebl = jnp.exp(b), jnp.exp(-b), jnp.exp(bl - b), jnp.exp(bl)
            qbf, kbf, kdf = qs[rows] * eb, kk[rows] * enb, kk[rows] * edl
            qb, kb, kd = qbf.astype(bf16), kbf.astype(bf16), kdf.astype(bf16)
            vc = iv[rows].astype(bf16)
            dob = d_o[rows].astype(bf16)
            dv, dqb, dkb, dkd, debl = [], [], [], [], []
            for h in range(HEADS):
                sl = slice(HEAD_DIM * h, HEAD_DIM * (h + 1))
                st = st_ref[c, h]
                dst = dst_scr[h]
                stb, dstb = st.astype(bf16), dst.astype(bf16)
                a = jnp.where(causal, _dot_nt(qb[:, sl], kb[:, sl]), 0.0).astype(bf16)
                da = jnp.where(causal, _dot_nt(dob[:, sl], vc[:, sl]), 0.0).astype(bf16)
                dv.append(_dot_tn(a, dob[:, sl]) + _dot_nt(kd[:, sl], dstb))
                dqb.append(_dot(dob[:, sl], stb) + _dot(da, kb[:, sl]))
                dkb.append(_dot_tn(da, qb[:, sl]))
                dkd.append(_dot(vc[:, sl], dstb))
                debl.append(jnp.sum(st * dst, axis=0, keepdims=True))
                dst_scr[h] = _dot_tn(dob[:, sl], qb[:, sl]) + dst * ebl[:, sl]
            dv, dqb, dkb, dkd = (jnp.concatenate(t, axis=1) for t in (dv, dqb, dkb, dkd))
            debl = jnp.concatenate(debl, axis=1)
            t_kd = dkd * kdf
            dbl = ebl * debl + jnp.sum(t_kd, axis=0, keepdims=True)
            db = dqb * qbf - dkb * kbf - t_kd + jnp.where(last_row, dbl, 0.0)
            dkk = dkb * enb + dkd * edl
            dlc = _dot_exact(tri_up, db)
            sq_c, q_c, sfp_c, snf_c, invf_c = sq[rows], q_raw[rows], sfp[rows], snf[rows], inv_f[rows]
            slope = (1.0 - lbv) * sfp_c * snf_c
            dp_ref[rows, 0:BRANCH] = (dqb * eb * sq_c * (1.0 + q_c * (1.0 - sq_c))).astype(bf16)
            dp_ref[rows, BRANCH:2 * BRANCH] = (slope * (dlc * invf_c - dkk)).astype(bf16)
            dp_ref[rows, 2 * BRANCH:3 * BRANCH] = dv.astype(bf16)
            dlb = dlb + jnp.sum(dlc * (lb_live - sfp_c) * invf_c - dkk * snf_c, axis=0, keepdims=True)
        vec_ref[0:1, :] += dlb

        dob_ = dz_ref[:, BRANCH:2 * BRANCH]
        bg, cg, xc = bg_ref[...], cg_ref[...], xc_ref[...]
        zc = cg * xc
        zbuf[0:8, :] = jnp.where(i < nt - 1, cgp_ref[...] * xcp_ref[...], 0.0)
        zbuf[8:8 + tt, :] = zc
        w0, w1, w2 = wconv_ref[0:1, :], wconv_ref[1:2, :], wconv_ref[2:3, :]
        y = w0 * zbuf[pl.ds(6, tt), :] + w1 * zbuf[pl.ds(7, tt), :] + w2 * zc
        dy = dob_ * bg
        dybuf[0:tt, :] = dy
        dy1, dy2 = dybuf[pl.ds(1, tt), :], dybuf[pl.ds(2, tt), :]
        dzc = w2 * dy + w1 * dy1 + w0 * dy2
        dp_ref[:, 4 * BRANCH:5 * BRANCH] = (dob_ * y).astype(bf16)
        dp_ref[:, 5 * BRANCH:6 * BRANCH] = (dzc * xc).astype(bf16)
        dp_ref[:, 6 * BRANCH:7 * BRANCH] = (dzc * cg).astype(bf16)
        vec_ref[4:5, :] += jnp.sum(zc * dy2, axis=0, keepdims=True)
        vec_ref[5:6, :] += jnp.sum(zc * dy1, axis=0, keepdims=True)
        vec_ref[6:7, :] += jnp.sum(zc * dy, axis=0, keepdims=True)
        dybuf[tt:tt + 8, :] = dybuf[0:8, :]

        doc = dz_ref[:, 2 * BRANCH:3 * BRANCH]
        u_raw, v_raw = u_ref[...], v_ref[...]
        ug = _gelu(u_raw)
        dug_scale = _gelu_grad(u_raw)
        vg = _gelu(v_raw)
        vcen = vg - jnp.mean(vg, axis=-1, keepdims=True)
        rstd = lax.rsqrt(jnp.mean(vcen * vcen, axis=-1, keepdims=True) + LN_EPS)
        vhat = vcen * rstd
        lng_v = lng_ref[...]
        vn = (vhat * lng_v + lnb_ref[...]).astype(bf16)
        low = _tri(SG_CHUNK)
        for g in range(SG_GROUPS):
            sl = slice(LANE * g, LANE * (g + 1))
            wm = jnp.where(low, wsg_ref[g], 0.0).astype(bf16)
            bias = bsg_ref[:, g:g + 1]
            dw = jnp.zeros((SG_CHUNK, SG_CHUNK), f32)
            dbs = jnp.zeros((SG_CHUNK, LANE), f32)
            for cc in range(tt // SG_CHUNK):
                rows = slice(SG_CHUNK * cc, SG_CHUNK * (cc + 1))
                vn_c = vn[rows, sl]
                sv = _dot(wm, vn_c) + bias
                doc_c = doc[rows, sl]
                dp_ref[rows, 7 * BRANCH + LANE * g:7 * BRANCH + LANE * (g + 1)] = (doc_c * sv * dug_scale[rows, sl]).astype(bf16)
                dsv = doc_c * ug[rows, sl]
                dsvb = dsv.astype(bf16)
                dbs = dbs + dsv
                dw = dw + _dot_nt(dsvb, vn_c)
                dvn_scr[rows, sl] = _dot_tn(wm, dsvb)
            dwsg_ref[g] += jnp.where(low, dw, 0.0)
            dbsg_acc[:, sl] += dbs
        dvn = dvn_scr[...]
        vec_ref[2:3, :] += jnp.sum(dvn * vhat, axis=0, keepdims=True)
        vec_ref[3:4, :] += jnp.sum(dvn, axis=0, keepdims=True)
        dvh = dvn * lng_v
        dvg = rstd * (dvh - jnp.mean(dvh, axis=-1, keepdims=True) - vhat * jnp.mean(dvh * vhat, axis=-1, keepdims=True))
        dp_ref[:, 8 * BRANCH:9 * BRANCH] = (dvg * _gelu_grad(v_raw)).astype(bf16)

        @pl.when(i == nt - 1)
        def _():
            for g in range(SG_GROUPS):
                dbsg_ref[:, g:g + 1] = jnp.sum(dbsg_acc[:, LANE * g:LANE * (g + 1)], axis=1, keepdims=True)

    full = lambda shape: pl.BlockSpec(shape, lambda i: (0,) * len(shape))
    tail = lambda c: pl.BlockSpec((8, BRANCH), lambda i: (jnp.maximum(rev(i) * (tt // 8) - 1, 0), c))
    return pl.pallas_call(
        body, name="mixer_bwd", grid=(nt,),
        in_specs=_p_specs(tt, range(9), rev) + [tail(5), tail(6), pl.BlockSpec((tt, 3 * BRANCH), lambda i: (rev(i), 0)),
                                                pl.BlockSpec((tt, BRANCH), lambda i: (rev(i), 0)),
                                                pl.BlockSpec((nch, HEADS, HEAD_DIM, HEAD_DIM), lambda i: (rev(i), 0, 0, 0)),
                                                full((1, BRANCH)), full((1, BRANCH)), full((3, BRANCH)), full((1, BRANCH)), full((1, BRANCH)),
                                                full((SG_GROUPS, SG_CHUNK, SG_CHUNK)), full((SG_CHUNK, SG_GROUPS))],
        out_specs=[pl.BlockSpec((tt, 9 * BRANCH), lambda i: (rev(i), 0)), full((8, BRANCH)), full((SG_GROUPS, SG_CHUNK, SG_CHUNK)),
                   full((SG_CHUNK, SG_GROUPS))],
        out_shape=[SDS((s, 9 * BRANCH), bf16), SDS((8, BRANCH), f32), SDS((SG_GROUPS, SG_CHUNK, SG_CHUNK), f32), SDS((SG_CHUNK, SG_GROUPS), f32)],
        scratch_shapes=[pltpu.VMEM((HEADS, HEAD_DIM, HEAD_DIM), f32), pltpu.VMEM((tt + 8, BRANCH), f32), pltpu.VMEM((tt + 8, BRANCH), f32),
                        pltpu.VMEM((tt, BRANCH), f32), pltpu.VMEM((SG_CHUNK, BRANCH), f32)],
        compiler_params=_cp("arbitrary"),
    )(*([p] * 11), dz, opre, states, lb, gout, wconv, lng, lnb, wsg, bsg_t)


def _dh_bwd(dpm, dpg, w_t, x, dx1, g, after, tm=1024, tk=1536):
    s = x.shape[0]
    km = dpm.shape[1] // tk
    nk = km + dpg.shape[1] // tk

    def body(dpm_ref, dpg_ref, w_ref, x_ref, dx1_ref, g_ref, after_ref, dx_ref, dxb_ref, dg_ref, acc):
        del after_ref
        i, k = pl.program_id(0), pl.program_id(1)

        @pl.when(k == 0)
        def _():
            acc[...] = jnp.zeros_like(acc)

        @pl.when(k < km)
        def _():
            acc[...] += _dot(dpm_ref[...], w_ref[...])

        @pl.when(k >= km)
        def _():
            acc[...] += _dot(dpg_ref[...], w_ref[...])

        @pl.when(k == nk - 1)
        def _():
            r, xh = _rms_stats(x_ref[...])
            dx, dg = _rms_bwd(acc[...], xh, r, g_ref[...])
            dx = dx + dx1_ref[...]
            dx_ref[...] = dx
            dxb_ref[...] = dx.astype(bf16)
            _acc_rows(dg_ref, i == 0, dg)

    row = pl.BlockSpec((tm, D_MODEL), lambda i, k: (i, 0))
    vec = pl.BlockSpec((1, D_MODEL), lambda i, k: (0, 0))
    return pl.pallas_call(
        body, name="dh_bwd", grid=(s // tm, nk),
        in_specs=[pl.BlockSpec((tm, tk), lambda i, k: (i, jnp.minimum(k, km - 1))),
                  pl.BlockSpec((tm, tk), lambda i, k: (i, jnp.maximum(k - km, 0))),
                  pl.BlockSpec((tk, D_MODEL), lambda i, k: (k, 0)), row, row, vec, pl.BlockSpec(memory_space=pl.ANY)],
        out_specs=[row, row, vec], out_shape=[SDS((s, D_MODEL), f32), SDS((s, D_MODEL), bf16), SDS((1, D_MODEL), f32)],
        scratch_shapes=[pltpu.VMEM((tm, D_MODEL), f32)], compiler_params=_cp("arbitrary", "arbitrary"),
    )(dpm, dpg, w_t, x, dx1, g, after)


def _layer_fwd(x, w, sm):
    p, h = _rms_mm(x, sm["g_mix"], w["w_in"])
    z, opre, states = _mixer_fwd(p, sm["lb"], sm["g_out"], sm["w_conv"], sm["ln_g"], sm["ln_b"], sm["w_sg"], sm["b_sg_t"])
    y, merged = _branch_gate(z, w["w_branch"], p)
    x1 = _mm_resid(x, merged, w["w_o"])
    x2, h2, ra = _ffn(x1, sm["g_ffn"], w["w_ff1"], w["w_ff2"])
    saved = dict(x=x, p=p, h=h, z=z, opre=opre, states=states, y=y, merged=merged, x1=x1, h2=h2, ra=ra)
    return x2, saved


def _layer_bwd(dx2, dx2b, sv, w, sm, between, before_end):
    nchip = N_DEV // 2
    by_chip = lambda g: g.reshape((nchip, 2) + g.shape[1:])
    da, act, dx1, dx1b, dg_ffn = _ffn_bwd(dx2, dx2b, sv["x1"], sm["g_ffn"], sv["ra"], w["w_ff1"], w["w_ff2"])
    g_ff2 = by_chip(_mm_tn(act, dx2b, 1, D_FF, D_MODEL, 512, 1024, name="dw_ff2")[0].reshape(N_DEV, D_FF // N_DEV, D_MODEL))
    g_ff1 = by_chip(_mm_tn_slabs(sv["h2"], da, 1, D_MODEL, D_FF // 2, [i * (D_FF // N_DEV) for i in range(nchip)], D_FF // N_DEV,
                                 name="dw_ff1")[:, 0])
    g_o = by_chip(_mm_tn(sv["merged"], dx1b, 1, D_MODEL, D_MODEL, 512, 1024, name="dw_o")[0].reshape(N_DEV, D_MODEL // N_DEV, D_MODEL))
    dy, dpg, dz = _merge_bwd(dx1b, w["w_o"], sv["y"], sv["p"], w["w_branch"], between(dx1))
    g_branch = by_chip(_mm_tn_slabs(sv["z"], dy, 3, BRANCH, D_MODEL, [i * (D_MODEL // N_DEV) for i in range(N_DEV)], D_MODEL // N_DEV,
                                    name="dw_branch"))
    g_in = _mm_tn(dpg, sv["h"], 1, 3 * D_MODEL, D_MODEL, 512, 1024, name="dw_in_gates", rows=N_COLS, row0=GATE_COL0)
    dpm, vecs, dwsg, dbsg_t = _mixer_bwd(sv["p"], dz, sv["opre"], sv["states"], sm["lb"], sm["g_out"], sm["w_conv"],
                                         sm["ln_g"], sm["ln_b"], sm["w_sg"], sm["b_sg_t"])
    g_in = _mm_tn(dpm, sv["h"], 1, GATE_COL0, D_MODEL, 512, 1024, name="dw_in_mixers", rows=N_COLS, into=g_in)
    g_in = by_chip(g_in[0].reshape(N_DEV, SHARD_IN, D_MODEL))
    dx, dxb, dg_mix = _dh_bwd(dpm, dpg, w["w_in"], sv["x"], dx1, sm["g_mix"], before_end(g_in))
    big = dict(w_in=g_in, w_branch=g_branch, w_o=g_o, w_ff1=g_ff1, w_ff2=g_ff2)
    small = dict(g_mix=dg_mix, g_ffn=dg_ffn, vecs=vecs, w_sg=dwsg, b_sg_t=dbsg_t, dx1=dx1)
    return dx, dxb, big, small


BIG = ("w_in", "w_branch", "w_o", "w_ff1", "w_ff2")
ANY = pl.BlockSpec(memory_space=pl.ANY)


def _place():
    return lax.axis_index("x"), lax.axis_index("y"), lax.axis_index("c")


def _al(v, m):
    return pl.multiple_of(v * m, m)


def _shard_of(refs, dev):
    w_in, w_b, w_o, w_1, w_2 = refs
    ni, nb, no, n1, n2 = w_in.shape[0] // N_DEV, w_b.shape[-1] // N_DEV, w_o.shape[0] // N_DEV, w_1.shape[-1] // N_DEV, w_2.shape[0] // N_DEV
    return [w_in.at[pl.ds(_al(dev, ni), ni), :], w_b.at[:, :, pl.ds(_al(dev, nb), nb)], w_o.at[pl.ds(_al(dev, no), no), :],
            w_1.at[:, pl.ds(_al(dev, n1), n1)], w_2.at[pl.ds(_al(dev, n2), n2), :]]


def _gather_out_shapes(shards):
    s_in, s_b, s_o, s_1, s_2 = (shards[n] for n in BIG)
    return [SDS((s_in.shape[1] * N_DEV, s_in.shape[2]), bf16), SDS(s_b.shape[1:3] + (s_b.shape[3] * N_DEV,), bf16),
            SDS((s_o.shape[1] * N_DEV, s_o.shape[2]), bf16), SDS((s_1.shape[1], s_1.shape[2] * N_DEV), bf16),
            SDS((s_2.shape[1] * N_DEV, s_2.shape[2]), bf16)]


def _seq_all_gather_layer(layer, shard_refs, out_shapes):
    nt = len(BIG)
    outs = [jax.empty_ref(sh, memory_space=pltpu.MemorySpace.HBM) for sh in out_shapes]

    @pl.kernel(mesh=plsc.ScalarSubcoreMesh(axis_name="seq", num_cores=1), name=f"seq_all_gather_l{layer}",
               scratch_types=(pltpu.SemaphoreType.DMA((7,)), pltpu.SemaphoreType.DMA((7,))),
               compiler_params=pltpu.CompilerParams(collective_id=1))
    def launch(send_sems, recv_sems):
        x, y, c = _place()
        me, sibling = (x, y, c), (x, y, 1 - c)
        chips = [(1 - x, y), (x, 1 - y), (1 - x, 1 - y)]
        peers = [sibling] + [(*chip, c) for chip in chips]
        barrier = pltpu.get_barrier_semaphore()
        for p in peers:
            pl.semaphore_signal(barrier, inc=1, device_id=p, device_id_type=MESH)
        pl.semaphore_wait(barrier, len(peers))
        mine = [r.at[layer] for r in shard_refs]

        def block(px, py, pc):
            return _shard_of(outs, 4 * px + 2 * py + pc)

        def copies(k, blk, to, src=None):
            dst = block(*blk)
            src = dst if src is None else src
            return [pltpu.make_async_remote_copy(src_ref=src[t], dst_ref=dst[t], send_sem=send_sems.at[k], recv_sem=recv_sems.at[k],
                                                 device_id=to, device_id_type=MESH) for t in range(nt)]

        first = copies(0, me, sibling, src=mine)
        for j, chip in enumerate(chips):
            first += copies(1 + j, me, (*chip, c), src=mine)
        for cp in first:
            cp.start()
        passed = [copies(4 + j, (*chip, c), sibling) for j, chip in enumerate(chips)]
        for j, chip in enumerate(chips):
            for cp in copies(1 + j, (*chip, c), me):
                cp.wait_recv()
            for cp in passed[j]:
                cp.start()
        for cp in copies(0, sibling, me):
            cp.wait_recv()
        for j, chip in enumerate(chips):
            for cp in copies(4 + j, (*chip, 1 - c), me):
                cp.wait_recv()
        for cp in first + [cp for grp in passed for cp in grp]:
            cp.wait_send()

    launch()
    return [o[...] for o in outs]


def _place_own(where, shards, gathered, after):
    s_in, s_b, s_o, s_1, s_2 = (shards[n] for n in BIG)
    nt = len(BIG)

    def body(where_ref, *refs):
        del where_ref
        for src, dst in zip(refs[:nt], refs[2 * nt + 1:]):
            dst[...] = src[...]

    lay = lambda shape, fn: pl.BlockSpec(shape, fn)
    in_specs = [lay((None,) + s_in.shape[1:], lambda i, wh: (wh[0], 0, 0)), lay((None,) + s_b.shape[1:], lambda i, wh: (wh[0], 0, 0, 0)),
                lay((None,) + s_o.shape[1:], lambda i, wh: (wh[0], 0, 0)), lay((None,) + s_1.shape[1:], lambda i, wh: (wh[0], 0, 0)),
                lay((None,) + s_2.shape[1:], lambda i, wh: (wh[0], 0, 0))]
    out_specs = [lay(s_in.shape[1:], lambda i, wh: (wh[1], 0)), lay(s_b.shape[1:], lambda i, wh: (0, 0, wh[1])),
                 lay(s_o.shape[1:], lambda i, wh: (wh[1], 0)), lay(s_1.shape[1:], lambda i, wh: (0, wh[1])),
                 lay(s_2.shape[1:], lambda i, wh: (wh[1], 0))]
    return pl.pallas_call(
        body, name="place_own", out_shape=[SDS(g.shape, g.dtype) for g in gathered],
        input_output_aliases={1 + nt + i: i for i in range(nt)}, compiler_params=_cp("arbitrary"),
        grid_spec=pltpu.PrefetchScalarGridSpec(num_scalar_prefetch=1, grid=(1,), in_specs=in_specs + [ANY] * (nt + 1), out_specs=out_specs),
    )(where, s_in, s_b, s_o, s_1, s_2, *gathered, after)


def _handshake(peers):
    barrier = pltpu.get_barrier_semaphore()
    for p in peers:
        pl.semaphore_signal(barrier, inc=1, device_id=p, device_id_type=MESH)
    pl.semaphore_wait(barrier, len(peers))


def _seq_exchange_on_chip(grads):
    nt, nchip = len(BIG), N_DEV // 2
    g_refs = [jax.new_ref(g, memory_space=pltpu.MemorySpace.HBM) for g in grads]
    outs = [jax.empty_ref(SDS((nchip,) + g.shape[2:], bf16), memory_space=pltpu.MemorySpace.HBM) for g in grads]

    @pl.kernel(mesh=plsc.ScalarSubcoreMesh(axis_name="seq", num_cores=1), name="seq_rs_on_chip",
               scratch_types=(pltpu.SemaphoreType.DMA((nchip,)), pltpu.SemaphoreType.DMA((nchip,))),
               compiler_params=pltpu.CompilerParams(collective_id=2))
    def launch(send_sems, recv_sems):
        x, y, c = _place()
        sibling = (x, y, 1 - c)
        _handshake([sibling])
        remote = [pltpu.make_async_remote_copy(src_ref=g_refs[t].at[j, 1 - c], dst_ref=outs[t].at[j], send_sem=send_sems.at[j],
                                               recv_sem=recv_sems.at[j], device_id=sibling, device_id_type=MESH)
                  for j in range(nchip) for t in range(nt)]
        for cp in remote:
            cp.start()
        for cp in remote:
            cp.wait_recv()
        for cp in remote:
            cp.wait_send()

    launch()
    return [o[...] for o in outs], [g[...] for g in g_refs]


def _seq_exchange_between_chips(sums):
    nt = len(BIG)
    s_refs = [jax.new_ref(a, memory_space=pltpu.MemorySpace.HBM) for a in sums]
    outs = [jax.empty_ref(SDS((3,) + a.shape[1:], bf16), memory_space=pltpu.MemorySpace.HBM) for a in sums]

    @pl.kernel(mesh=plsc.ScalarSubcoreMesh(axis_name="seq", num_cores=1), name="seq_rs_between_chips",
               scratch_types=(pltpu.SemaphoreType.DMA((3,)), pltpu.SemaphoreType.DMA((3,))),
               compiler_params=pltpu.CompilerParams(collective_id=3))
    def launch(send_sems, recv_sems):
        x, y, c = _place()
        chips = [(1 - x, y), (x, 1 - y), (1 - x, 1 - y)]
        _handshake([(cx, cy, c) for cx, cy in chips])
        remote = [pltpu.make_async_remote_copy(src_ref=s_refs[t].at[2 * cx + cy], dst_ref=outs[t].at[k], send_sem=send_sems.at[k],
                                               recv_sem=recv_sems.at[k], device_id=(cx, cy, c), device_id_type=MESH)
                  for k, (cx, cy) in enumerate(chips) for t in range(nt)]
        for cp in remote:
            cp.start()
        for cp in remote:
            cp.wait_recv()
        for cp in remote:
            cp.wait_send()

    launch()
    return [o[...] for o in outs], [a[...] for a in s_refs]


def _chip_sums(core, mine, other, after, steps=2):
    nt, nchip = len(mine), mine[0].shape[0]
    m4 = [a.reshape(nchip, 2, -1, a.shape[-1]) for a in mine]
    o3 = [a.reshape(nchip, -1, a.shape[-1]) for a in other]

    def body(c_ref, *refs):
        del c_ref
        for a_ref, b_ref, o_ref in zip(refs[:nt], refs[nt:2 * nt], refs[2 * nt + 1:]):
            o_ref[...] = (a_ref[...].astype(f32) + b_ref[...].astype(f32)).astype(bf16)

    tiles = [(a.shape[1] // steps, a.shape[2]) for a in o3]
    blks = [pl.BlockSpec((None,) + t, lambda j, i, c_ref: (j, i, 0)) for t in tiles]
    outs = pl.pallas_call(
        body, name="chip_sums", out_shape=[SDS(a.shape, bf16) for a in o3], compiler_params=_cp("parallel", "parallel"),
        grid_spec=pltpu.PrefetchScalarGridSpec(
            num_scalar_prefetch=1, grid=(nchip, steps),
            in_specs=[pl.BlockSpec((None, None) + t, lambda j, i, c_ref: (j, c_ref[0], i, 0)) for t in tiles] + blks + [ANY],
            out_specs=blks),
    )(core, *m4, *o3, after)
    return [o.reshape(a.shape) for o, a in zip(outs, other)]


def _all_reduce_rows(pack):
    rows = pack.shape[0]
    blk = rows // N_DEV

    def body(in_ref, out_ref, land, send1, recv1, send2, recv2):
        x, y, c = _place()
        me = 4 * x + 2 * y + c
        others = [(px, py, pc) for px in range(2) for py in range(2) for pc in range(2)]

        def is_me(p):
            return jnp.logical_and(jnp.logical_and(p[0] == x, p[1] == y), p[2] == c)

        land[me] = in_ref[pl.ds(_al(me, blk), blk), :]
        for d, p in enumerate(others):
            @pl.when(jnp.logical_not(is_me(p)))
            def _():
                pltpu.make_async_remote_copy(src_ref=in_ref.at[pl.ds(d * blk, blk), :], dst_ref=land.at[me], send_sem=send1.at[d],
                                             recv_sem=recv1.at[me], device_id=p, device_id_type=MESH).start()
        for d, p in enumerate(others):
            @pl.when(jnp.logical_not(is_me(p)))
            def _():
                cp = pltpu.make_async_remote_copy(src_ref=in_ref.at[pl.ds(d * blk, blk), :], dst_ref=land.at[d], send_sem=send1.at[d],
                                                  recv_sem=recv1.at[d], device_id=p, device_id_type=MESH)
                cp.wait_recv()
                cp.wait_send()
        total = land[0]
        for d in range(1, N_DEV):
            total = total + land[d]
        out_ref[pl.ds(_al(me, blk), blk), :] = total
        for d, p in enumerate(others):
            @pl.when(jnp.logical_not(is_me(p)))
            def _():
                mine = out_ref.at[pl.ds(_al(me, blk), blk), :]
                pltpu.make_async_remote_copy(src_ref=mine, dst_ref=mine, send_sem=send2.at[d], recv_sem=recv2.at[me],
                                             device_id=p, device_id_type=MESH).start()
        for d, p in enumerate(others):
            @pl.when(jnp.logical_not(is_me(p)))
            def _():
                theirs = out_ref.at[pl.ds(d * blk, blk), :]
                cp = pltpu.make_async_remote_copy(src_ref=theirs, dst_ref=theirs, send_sem=send2.at[d], recv_sem=recv2.at[d],
                                                  device_id=p, device_id_type=MESH)
                cp.wait_recv()
                cp.wait_send()

    vm = pl.BlockSpec(memory_space=pltpu.VMEM)
    return pl.pallas_call(
        body, name="all_reduce_rows", in_specs=[vm], out_specs=vm, out_shape=SDS((rows, LANE), f32),
        scratch_shapes=[pltpu.VMEM((N_DEV, blk, LANE), f32)] + [pltpu.SemaphoreType.DMA((N_DEV,))] * 4,
        compiler_params=pltpu.CompilerParams(vmem_limit_bytes=VMEM_LIMIT),
    )(pack)


def _lower_bounds_fwd(lower):
    def body(l_ref, o_ref):
        sm = _layer_softmax(l_ref)
        run = jnp.zeros_like(sm[0])
        for l in range(DEPTH):
            o_ref[l:l + 1, :] = run
            if l + 1 < DEPTH:
                run = run + sm[l + 1]

    return pl.pallas_call(body, name="lower_bounds_fwd", out_shape=SDS(lower.shape, f32))(lower)


def _layer_softmax(l_ref):
    rows = [l_ref[l:l + 1, :] for l in range(DEPTH)]
    top = functools.reduce(jnp.maximum, rows)
    e = [jnp.exp(r - top) for r in rows]
    tot = functools.reduce(lambda a, b: a + b, e)
    return [v / tot for v in e]


def _lower_bounds_bwd(lower, dlbs):
    def body(l_ref, d_ref, o_ref):
        sm = _layer_softmax(l_ref)
        dsm = [None] * DEPTH
        run = jnp.zeros_like(sm[0])
        dsm[0] = run
        for l in reversed(range(1, DEPTH)):
            run = run + d_ref[l:l + 1, :]
            dsm[l] = run
        inner = functools.reduce(lambda a, b: a + b, [sm[l] * dsm[l] for l in range(DEPTH)])
        for l in range(DEPTH):
            o_ref[l:l + 1, :] = sm[l] * (dsm[l] - inner)

    return pl.pallas_call(body, name="lower_bounds_bwd", out_shape=SDS(lower.shape, f32))(lower, dlbs)


_ADAM_C1 = 1.0 - ADAM_B1 ** ADAM_STEP
_ADAM_C2 = 1.0 - ADAM_B2 ** ADAM_STEP


def _adamw(w, g, m, v):
    m = ADAM_B1 * m + (1.0 - ADAM_B1) * g
    v = ADAM_B2 * v + (1.0 - ADAM_B2) * (g * g)
    delta = -ADAM_LR * ((m / _ADAM_C1) / (jnp.sqrt(v / _ADAM_C2) + ADAM_EPS) + ADAM_WD * w)
    return delta, m, v


def _row_tile(rows, cap):
    return next(t for t in range(min(cap, rows) // 16 * 16, 0, -16) if rows % t == 0)


def _adam_big(where, w, m, v, sums, landed, outs, after, tr=512):
    shape = w.shape
    cols = shape[-1]
    w3, m3, v3 = (a.reshape(DEPTH, -1, cols) for a in (w, m, v))
    outs3 = [a.reshape(DEPTH, -1, cols) for a in outs]
    sums3 = sums.reshape(sums.shape[0], -1, cols)
    land3 = landed.reshape(3, -1, cols)
    rows = w3.shape[1]
    tr = _row_tile(rows, tr)
    gcols = cols

    def body(where_ref, w_ref, m_ref, v_ref, sum_ref, land_ref, *rest):
        del where_ref
        g_ref, d_ref, nm_ref, nv_ref = rest[5:]
        g = sum_ref[...].astype(f32)
        for k in range(3):
            g = g + land_ref[k].astype(f32)
        delta, nm, nv = _adamw(w_ref[...], g, m_ref[...], v_ref[...])
        g_ref[...] = g
        d_ref[...] = delta
        nm_ref[...] = nm
        nv_ref[...] = nv

    blk = pl.BlockSpec((None, tr, cols), lambda i, wh: (wh[0], i, 0))
    res = pl.pallas_call(
        body, name="adam_big", out_shape=[SDS(w3.shape, f32)] * 4, input_output_aliases={6 + i: i for i in range(4)},
        compiler_params=_cp("parallel"),
        grid_spec=pltpu.PrefetchScalarGridSpec(
            num_scalar_prefetch=1, grid=(rows // tr,),
            in_specs=[blk, blk, blk, pl.BlockSpec((None, tr, gcols), lambda i, wh: (wh[1], i, 0)),
                      pl.BlockSpec((3, tr, gcols), lambda i, wh: (0, i, 0))] + [ANY] * 5,
            out_specs=[blk] * 4),
    )(where, w3, m3, v3, sums3, land3, *outs3, after)
    return [o.reshape(shape) for o in res]


def _adam_rows(w, g, m, v):
    def body(w_ref, g_ref, m_ref, v_ref, d_ref, nm_ref, nv_ref):
        delta, nm, nv = _adamw(w_ref[...], g_ref[...], m_ref[...], v_ref[...])
        d_ref[...] = delta
        nm_ref[...] = nm
        nv_ref[...] = nv

    return pl.pallas_call(body, name="adam_rows", out_shape=[SDS(w.shape, f32)] * 3)(w, g, m, v)


SMALL = ("g_mix", "lower_bounds", "g_hgrn_out", "w_conv", "sg_ln_g", "sg_ln_b", "w_sg", "b_sg", "g_ffn", "g_final")
WEIGHTS = ("w_in", "g_mix", "lower_bounds", "g_hgrn_out", "w_conv", "sg_ln_g", "sg_ln_b", "w_sg", "b_sg", "w_branch", "w_o", "g_ffn",
           "w_ff1", "w_ff2", "g_final")


def _pack_rows(arrays, multiple):
    flat = jnp.concatenate([a.reshape(-1) for a in arrays])
    rows = -(-flat.shape[0] // (LANE * multiple)) * multiple
    return jnp.pad(flat, (0, rows * LANE - flat.shape[0])).reshape(rows, LANE)


def _unpack_rows(pack, like):
    flat = pack.reshape(-1)
    out, at = [], 0
    for a in like:
        out.append(flat[at:at + a.size].reshape(a.shape))
        at += a.size
    return out


def kernel(x, w_in, g_mix, lower_bounds, g_hgrn_out, w_conv, sg_ln_g, sg_ln_b, w_sg, b_sg, w_branch, w_o, g_ffn, w_ff1, w_ff2, g_final, loss_target, m_w_in, m_g_mix, m_lower_bounds, m_g_hgrn_out, m_w_conv, m_sg_ln_g, m_sg_ln_b, m_w_sg, m_b_sg, m_w_branch, m_w_o, m_g_ffn, m_w_ff1, m_w_ff2, m_g_final, v_w_in, v_g_mix, v_lower_bounds, v_g_hgrn_out, v_w_conv, v_sg_ln_g, v_sg_ln_b, v_w_sg, v_b_sg, v_w_branch, v_w_o, v_g_ffn, v_w_ff1, v_w_ff2, v_g_final):
    weights = dict(w_in=w_in, g_mix=g_mix, lower_bounds=lower_bounds, g_hgrn_out=g_hgrn_out, w_conv=w_conv, sg_ln_g=sg_ln_g,
                   sg_ln_b=sg_ln_b, w_sg=w_sg, b_sg=b_sg, w_branch=w_branch, w_o=w_o, g_ffn=g_ffn, w_ff1=w_ff1, w_ff2=w_ff2, g_final=g_final)
    mom1 = dict(w_in=m_w_in, g_mix=m_g_mix, lower_bounds=m_lower_bounds, g_hgrn_out=m_g_hgrn_out, w_conv=m_w_conv, sg_ln_g=m_sg_ln_g,
                sg_ln_b=m_sg_ln_b, w_sg=m_w_sg, b_sg=m_b_sg, w_branch=m_w_branch, w_o=m_w_o, g_ffn=m_g_ffn, w_ff1=m_w_ff1, w_ff2=m_w_ff2,
                g_final=m_g_final)
    mom2 = dict(w_in=v_w_in, g_mix=v_g_mix, lower_bounds=v_lower_bounds, g_hgrn_out=v_g_hgrn_out, w_conv=v_w_conv, sg_ln_g=v_sg_ln_g,
                sg_ln_b=v_sg_ln_b, w_sg=v_w_sg, b_sg=v_b_sg, w_branch=v_w_branch, w_o=v_w_o, g_ffn=v_g_ffn, w_ff1=v_w_ff1, w_ff2=v_w_ff2,
                g_final=v_g_final)
    xi, yi, ci = _place()
    dev = 4 * xi + 2 * yi + ci
    conv_cols = w_conv.shape[-1]

    for d in (weights, mom1, mom2):
        d["w_in"] = jnp.swapaxes(d["w_in"], 1, 2)
    shards = {n: weights[n].astype(bf16) for n in BIG}

    conv_place = lax.dynamic_update_slice(jnp.zeros((DEPTH, 3, BRANCH), f32), w_conv, (0, 0, dev * conv_cols))
    (w_conv_full,) = _unpack_rows(_all_reduce_rows(_pack_rows([conv_place], 8 * N_DEV)), [conv_place])
    lbs = _lower_bounds_fwd(lower_bounds)

    def small_of(l):
        return dict(g_mix=g_mix[l][None], lb=lbs[l][None], g_out=g_hgrn_out[l][None], w_conv=w_conv_full[l], ln_g=sg_ln_g[l][None],
                    ln_b=sg_ln_b[l][None], w_sg=w_sg[l], b_sg_t=b_sg[l].T, g_ffn=g_ffn[l][None])

    act = x[0]
    full, saved = [], []
    shard_refs = [jax.new_ref(shards[n], memory_space=pltpu.MemorySpace.HBM) for n in BIG]
    gathered = [_seq_all_gather_layer(l, shard_refs, _gather_out_shapes(shards)) for l in range(DEPTH)]
    for l in range(DEPTH):
        full.append(dict(zip(BIG, _place_own(jnp.stack([jnp.int32(l), dev.astype(jnp.int32)]), shards, gathered[l], act))))
        act, sv = _layer_fwd(act, full[l], small_of(l))
        saved.append(sv)
    loss_row, dx, dxb, dg_final = _final(act, loss_target[0], g_final[None])
    loss = lax.psum(loss_row[0, 0], ("x", "y", "c"))

    core = ci.astype(jnp.int32)[None]
    big_out = {n: [lax.empty(weights[n].shape, f32) for _ in range(4)] for n in BIG}
    small_grads = [None] * DEPTH

    def chip_sums(stage, after):
        l, received, mine = stage
        sums = _chip_sums(core, mine, received, after)
        placed.append(sums[BIG.index("w_o")])
        landed, sums = _seq_exchange_between_chips(sums)
        return l, sums, landed

    def adam_layer(stage, after):
        l, sums, landed = stage
        where = jnp.stack([jnp.int32(l), (2 * xi + yi).astype(jnp.int32)])
        for t, n in enumerate(BIG):
            big_out[n] = _adam_big(where, weights[n], mom1[n], mom2[n], sums[t], landed[t], big_out[n], after)

    above = None
    placed = []
    for l in reversed(range(DEPTH)):
        summed = []

        def between(dx1):
            if above is None:
                return dx1
            summed.append(chip_sums(above, dx1))
            return placed[-1]

        def before_end(g_in):
            if not summed:
                return g_in
            adam_layer(summed[0], g_in)
            return big_out["w_o"][0]

        dx, dxb, big, small_grads[l] = _layer_bwd(dx, dxb, saved[l], full[l], small_of(l), between, before_end)
        above = (l, *_seq_exchange_on_chip([big[n] for n in BIG]))

    stack = lambda f: jnp.stack([f(small_grads[l]) for l in range(DEPTH)])
    d_lower = _lower_bounds_bwd(lower_bounds, stack(lambda s: s["vecs"][0]))
    local_small = dict(g_mix=stack(lambda s: s["g_mix"][0]), lower_bounds=d_lower, g_hgrn_out=stack(lambda s: s["vecs"][1]),
                       w_conv=stack(lambda s: s["vecs"][4:7]), sg_ln_g=stack(lambda s: s["vecs"][2]), sg_ln_b=stack(lambda s: s["vecs"][3]),
                       w_sg=stack(lambda s: s["w_sg"]), b_sg=stack(lambda s: s["b_sg_t"].T), g_ffn=stack(lambda s: s["g_ffn"][0]),
                       g_final=dg_final[0])
    order = [local_small[n] for n in SMALL]
    grads = dict(zip(SMALL, _unpack_rows(_all_reduce_rows(_pack_rows(order, 8 * N_DEV)), order)))
    grads["w_conv"] = lax.dynamic_slice(grads["w_conv"], (0, 0, dev * conv_cols), (DEPTH, 3, conv_cols))

    deltas, new_m, new_v = {}, {}, {}
    packs = [_pack_rows([d[n] for n in SMALL], 8) for d in (weights, grads, mom1, mom2)]
    like = [weights[n] for n in SMALL]
    small_out = _adam_rows(*packs)
    for out, pack in zip((deltas, new_m, new_v), small_out):
        out.update(zip(SMALL, _unpack_rows(pack, like)))
    adam_layer(chip_sums(above, dx), small_out[0])
    for n in BIG:
        grads[n], deltas[n], new_m[n], new_v[n] = (jnp.swapaxes(a, 1, 2) if n == "w_in" else a for a in big_out[n])

    return (loss, dx[None], *[grads[n] for n in WEIGHTS], *[deltas[n] for n in WEIGHTS], *[new_m[n] for n in WEIGHTS],
            *[new_v[n] for n in WEIGHTS])
```

```python
import functools

import jax
import jax.numpy as jnp
from jax import lax
from jax.experimental import pallas as pl
from jax.experimental.pallas import tpu as pltpu
from jax.experimental.pallas import tpu_sc as plsc

f32 = jnp.float32
bf16 = jnp.bfloat16
SDS = jax.ShapeDtypeStruct
MESH = pl.DeviceIdType.MESH

D_MODEL = 1024
BRANCH = 512
N_COLS = 7680
D_FF = 4096
DEPTH = 4
HEADS = 4
HEAD_DIM = 128
HGRN_CHUNK = 64
SG_CHUNK = 128
SG_GROUPS = 4
NORM_EPS = 1e-6
LN_EPS = 1e-5
LB_FLOOR = 1e-30
N_DEV = 8
SHARD_IN = N_COLS // N_DEV
WIN = 1024
LANE = 128
GATE_COL0 = 9 * BRANCH

ADAM_LR = 0.001
ADAM_B1 = 0.9
ADAM_B2 = 0.999
ADAM_EPS = 1e-08
ADAM_WD = 0.01
ADAM_STEP = 10

MIX_TILE = 256
VMEM_LIMIT = 56 * 1024 * 1024


def _cp(*sem):
    return pltpu.CompilerParams(dimension_semantics=sem or None, vmem_limit_bytes=VMEM_LIMIT)


def _dot(a, b):
    return jnp.dot(a, b, preferred_element_type=f32)


def _dot_nt(a, b):
    return lax.dot_general(a, b, (((1,), (1,)), ((), ())), preferred_element_type=f32)


def _dot_tn(a, b):
    return lax.dot_general(a, b, (((0,), (0,)), ((), ())), preferred_element_type=f32)


def _dot_exact(a, b):
    return jnp.dot(a, b, precision=lax.Precision.HIGHEST, preferred_element_type=f32)


def _sigmoid(x):
    return jax.nn.sigmoid(x)


_GELU_C = 0.7978845608028654
_GELU_A = 0.044715


def _gelu(x):
    return 0.5 * x * (1.0 + jnp.tanh(_GELU_C * (x + _GELU_A * x * x * x)))


def _gelu_grad(x):
    x2 = x * x
    t = jnp.tanh(_GELU_C * (x + _GELU_A * x * x2))
    return 0.5 * (1.0 + t) + 0.5 * x * (1.0 - t * t) * _GELU_C * (1.0 + 3.0 * _GELU_A * x2)


def _rms_stats(x):
    r = lax.rsqrt(jnp.mean(x * x, axis=-1, keepdims=True) + NORM_EPS)
    return r, x * r


def _rms_bwd(dh, xh, r, g):
    dg = jnp.sum(dh * xh, axis=0, keepdims=True)
    dxn = dh * g
    dx = r * (dxn - xh * jnp.mean(dxn * xh, axis=-1, keepdims=True))
    return dx, dg


def _tri(n, upper=False):
    r = lax.broadcasted_iota(jnp.int32, (n, n), 0)
    c = lax.broadcasted_iota(jnp.int32, (n, n), 1)
    return (c >= r) if upper else (c <= r)


def _acc_rows(ref, first, val):
    @pl.when(first)
    def _():
        ref[...] = val

    @pl.when(jnp.logical_not(first))
    def _():
        ref[...] += val


def _rms_mm(x, g, w_t, tm=1024, tn=1920):
    s, n = x.shape[0], w_t.shape[0]

    def body(x_ref, g_ref, w_ref, p_ref, h_ref, hs):
        @pl.when(pl.program_id(1) == 0)
        def _():
            _, xh = _rms_stats(x_ref[...])
            hv = (xh * g_ref[...]).astype(bf16)
            hs[...] = hv
            h_ref[...] = hv

        p_ref[...] = _dot_nt(hs[...], w_ref[...])

    return pl.pallas_call(
        body, name="rms_mm", grid=(s // tm, n // tn),
        in_specs=[pl.BlockSpec((tm, D_MODEL), lambda i, j: (i, 0)), pl.BlockSpec((1, D_MODEL), lambda i, j: (0, 0)),
                  pl.BlockSpec((tn, D_MODEL), lambda i, j: (j, 0))],
        out_specs=[pl.BlockSpec((tm, tn), lambda i, j: (i, j)), pl.BlockSpec((tm, D_MODEL), lambda i, j: (i, 0))],
        out_shape=[SDS((s, n), f32), SDS((s, D_MODEL), bf16)],
        scratch_shapes=[pltpu.VMEM((tm, D_MODEL), bf16)], compiler_params=_cp("parallel", "arbitrary"),
    )(x, g, w_t)


def _hgrn_gates(fp, lb):
    logf = jnp.logaddexp(jnp.log(jnp.maximum(lb, LB_FLOOR)), jnp.log1p(-lb) + jax.nn.log_sigmoid(fp))
    snf = _sigmoid(-fp)
    return logf, snf, (1.0 - lb) * snf


def _p_specs(tile, cols, row_map):
    return [pl.BlockSpec((tile, BRANCH), functools.partial(lambda c, i: (row_map(i), c), c)) for c in cols]


def _mixer_fwd(p, lb, gout, wconv, lng, lnb, wsg, bsg_t):
    s = p.shape[0]
    tt = MIX_TILE
    nch = tt // HGRN_CHUNK

    def body(q_ref, fp_ref, iv_ref, go_ref, bg_ref, cg_ref, xc_ref, u_ref, v_ref, lb_ref, gout_ref, wconv_ref, lng_ref,
             lnb_ref, wsg_ref, bsg_ref, z_ref, opre_ref, st_ref, st_scr, zbuf):
        @pl.when(pl.program_id(0) == 0)
        def _():
            st_scr[...] = jnp.zeros_like(st_scr)
            zbuf[0:8, :] = jnp.zeros((8, BRANCH), f32)

        lbv = lb_ref[...]
        q_raw = q_ref[...]
        qs = q_raw * _sigmoid(q_raw)
        logf, _, kk = _hgrn_gates(fp_ref[...], lbv)
        iv = iv_ref[...]
        causal = _tri(HGRN_CHUNK)
        tri = causal.astype(f32)
        last_row = lax.broadcasted_iota(jnp.int32, (HGRN_CHUNK, 1), 0) == HGRN_CHUNK - 1
        for c in range(nch):
            rows = slice(HGRN_CHUNK * c, HGRN_CHUNK * (c + 1))
            b = _dot_exact(tri, logf[rows])
            bl = jnp.sum(jnp.where(last_row, b, 0.0), axis=0, keepdims=True)
            qb = (qs[rows] * jnp.exp(b)).astype(bf16)
            kb = (kk[rows] * jnp.exp(-b)).astype(bf16)
            kd = (kk[rows] * jnp.exp(bl - b)).astype(bf16)
            ebl = jnp.exp(bl)
            vc = iv[rows].astype(bf16)
            for h in range(HEADS):
                sl = slice(HEAD_DIM * h, HEAD_DIM * (h + 1))
                st = st_scr[h]
                st_ref[c, h] = st
                a = jnp.where(causal, _dot_nt(qb[:, sl], kb[:, sl]), 0.0)
                opre_ref[rows, sl] = _dot(a.astype(bf16), vc[:, sl]) + _dot_nt(qb[:, sl], st.astype(bf16))
                st_scr[h] = st * ebl[:, sl] + _dot_tn(vc[:, sl], kd[:, sl])

        o = opre_ref[...]
        go = go_ref[...]
        gout_v = gout_ref[...]
        for h in range(HEADS):
            sl = slice(HEAD_DIM * h, HEAD_DIM * (h + 1))
            _, oh = _rms_stats(o[:, sl])
            z_ref[:, sl] = (oh * gout_v[:, sl] * _sigmoid(go[:, sl])).astype(bf16)

        zc = cg_ref[...] * xc_ref[...]
        zbuf[8:8 + tt, :] = zc
        y = wconv_ref[0:1, :] * zbuf[pl.ds(6, tt), :] + wconv_ref[1:2, :] * zbuf[pl.ds(7, tt), :] + wconv_ref[2:3, :] * zc
        z_ref[:, BRANCH:2 * BRANCH] = (bg_ref[...] * y).astype(bf16)
        zbuf[0:8, :] = zbuf[tt:tt + 8, :]

        ug = _gelu(u_ref[...])
        vg = _gelu(v_ref[...])
        vcen = vg - jnp.mean(vg, axis=-1, keepdims=True)
        rstd = lax.rsqrt(jnp.mean(vcen * vcen, axis=-1, keepdims=True) + LN_EPS)
        vn = (vcen * rstd * lng_ref[...] + lnb_ref[...]).astype(bf16)
        low = _tri(SG_CHUNK)
        for g in range(SG_GROUPS):
            sl = slice(LANE * g, LANE * (g + 1))
            wm = jnp.where(low, wsg_ref[g], 0.0).astype(bf16)
            bias = bsg_ref[:, g:g + 1]
            for cc in range(tt // SG_CHUNK):
                rows = slice(SG_CHUNK * cc, SG_CHUNK * (cc + 1))
                sv = _dot(wm, vn[rows, sl]) + bias
                z_ref[rows, 2 * BRANCH + LANE * g:2 * BRANCH + LANE * (g + 1)] = (ug[rows, sl] * sv).astype(bf16)

    full = lambda shape: pl.BlockSpec(shape, lambda i: (0,) * len(shape))
    return pl.pallas_call(
        body, name="mixer_fwd", grid=(s // tt,),
        in_specs=_p_specs(tt, range(9), lambda i: i) + [full((1, BRANCH)), full((1, BRANCH)), full((3, BRANCH)), full((1, BRANCH)),
                                                        full((1, BRANCH)), full((SG_GROUPS, SG_CHUNK, SG_CHUNK)), full((SG_CHUNK, SG_GROUPS))],
        out_specs=[pl.BlockSpec((tt, 3 * BRANCH), lambda i: (i, 0)), pl.BlockSpec((tt, BRANCH), lambda i: (i, 0)),
                   pl.BlockSpec((nch, HEADS, HEAD_DIM, HEAD_DIM), lambda i: (i, 0, 0, 0))],
        out_shape=[SDS((s, 3 * BRANCH), bf16), SDS((s, BRANCH), f32), SDS((s // HGRN_CHUNK, HEADS, HEAD_DIM, HEAD_DIM), f32)],
        scratch_shapes=[pltpu.VMEM((HEADS, HEAD_DIM, HEAD_DIM), f32), pltpu.VMEM((tt + 8, BRANCH), f32)],
        compiler_params=_cp("arbitrary"),
    )(*([p] * 9), lb, gout, wconv, lng, lnb, wsg, bsg_t)


def _branch_gate(z, wb, p, tm=256):
    s = z.shape[0]
    half = 3 * D_MODEL // 2

    def body(z_ref, wb_ref, ga_ref, gb_ref, y_ref, m_ref):
        ga, gb = ga_ref[...], gb_ref[...]
        gates = [ga[:, :D_MODEL], jnp.concatenate([ga[:, D_MODEL:], gb[:, :D_MODEL // 2]], axis=1), gb[:, D_MODEL // 2:]]
        acc = None
        for n in range(3):
            yn = _dot(z_ref[:, BRANCH * n:BRANCH * (n + 1)], wb_ref[n])
            y_ref[:, D_MODEL * n:D_MODEL * (n + 1)] = yn.astype(bf16)
            t = _sigmoid(gates[n]) * yn
            acc = t if acc is None else acc + t
        m_ref[...] = acc.astype(bf16)

    blk0 = GATE_COL0 // half
    return pl.pallas_call(
        body, name="branch_gate", grid=(s // tm,),
        in_specs=[pl.BlockSpec((tm, 3 * BRANCH), lambda i: (i, 0)), pl.BlockSpec((3, BRANCH, D_MODEL), lambda i: (0, 0, 0)),
                  pl.BlockSpec((tm, half), lambda i: (i, blk0)), pl.BlockSpec((tm, half), lambda i: (i, blk0 + 1))],
        out_specs=[pl.BlockSpec((tm, 3 * D_MODEL), lambda i: (i, 0)), pl.BlockSpec((tm, D_MODEL), lambda i: (i, 0))],
        out_shape=[SDS((s, 3 * D_MODEL), bf16), SDS((s, D_MODEL), bf16)], compiler_params=_cp("parallel"),
    )(z, wb, p, p)


def _mm_resid(x, m, wo, tm=512):
    s = x.shape[0]

    def body(x_ref, m_ref, w_ref, o_ref):
        o_ref[...] = x_ref[...] + _dot(m_ref[...], w_ref[...])

    return pl.pallas_call(
        body, name="mm_resid", grid=(s // tm,),
        in_specs=[pl.BlockSpec((tm, D_MODEL), lambda i: (i, 0)), pl.BlockSpec((tm, D_MODEL), lambda i: (i, 0)),
                  pl.BlockSpec((D_MODEL, D_MODEL), lambda i: (0, 0))],
        out_specs=pl.BlockSpec((tm, D_MODEL), lambda i: (i, 0)), out_shape=SDS((s, D_MODEL), f32), compiler_params=_cp("parallel"),
    )(x, m, wo)


def _ffn(x1, g, w1, w2, tm=512, tf=1024):
    s = x1.shape[0]
    nf = D_FF // tf

    def body(x_ref, g_ref, w1_ref, w2_ref, o_ref, h_ref, ra_ref, hs, acc):
        f = pl.program_id(1)

        @pl.when(f == 0)
        def _():
            _, xh = _rms_stats(x_ref[...])
            hv = (xh * g_ref[...]).astype(bf16)
            hs[...] = hv
            h_ref[...] = hv
            acc[...] = jnp.zeros_like(acc)

        ra = jnp.maximum(_dot(hs[...], w1_ref[...]), 0.0)
        ra_ref[...] = ra.astype(bf16)
        acc[...] += _dot((ra * ra).astype(bf16), w2_ref[...])

        @pl.when(f == nf - 1)
        def _():
            o_ref[...] = x_ref[...] + acc[...]

    return pl.pallas_call(
        body, name="ffn", grid=(s // tm, nf),
        in_specs=[pl.BlockSpec((tm, D_MODEL), lambda i, f: (i, 0)), pl.BlockSpec((1, D_MODEL), lambda i, f: (0, 0)),
                  pl.BlockSpec((D_MODEL, tf), lambda i, f: (0, f)), pl.BlockSpec((tf, D_MODEL), lambda i, f: (f, 0))],
        out_specs=[pl.BlockSpec((tm, D_MODEL), lambda i, f: (i, 0)), pl.BlockSpec((tm, D_MODEL), lambda i, f: (i, 0)),
                   pl.BlockSpec((tm, tf), lambda i, f: (i, f))],
        out_shape=[SDS((s, D_MODEL), f32), SDS((s, D_MODEL), bf16), SDS((s, D_FF), bf16)],
        scratch_shapes=[pltpu.VMEM((tm, D_MODEL), bf16), pltpu.VMEM((tm, D_MODEL), f32)], compiler_params=_cp("parallel", "arbitrary"),
    )(x1, g, w1, w2)


def _final(x, target, g, tm=512):
    s = x.shape[0]

    def body(x_ref, t_ref, g_ref, loss_ref, dx_ref, dxb_ref, dg_ref):
        first = pl.program_id(0) == 0
        gv = g_ref[...]
        r, xh = _rms_stats(x_ref[...])
        e = xh * gv - t_ref[...]
        tile_loss = 0.5 * jnp.sum(jnp.mean(e * e, axis=-1, keepdims=True), axis=0, keepdims=True)
        dx, dg = _rms_bwd(e * (1.0 / D_MODEL), xh, r, gv)
        dx_ref[...] = dx
        dxb_ref[...] = dx.astype(bf16)
        _acc_rows(dg_ref, first, dg)
        _acc_rows(loss_ref, first, jnp.broadcast_to(tile_loss, (1, LANE)))

    row = pl.BlockSpec((tm, D_MODEL), lambda i: (i, 0))
    return pl.pallas_call(
        body, name="final_loss", grid=(s // tm,), in_specs=[row, row, pl.BlockSpec((1, D_MODEL), lambda i: (0, 0))],
        out_specs=[pl.BlockSpec((1, LANE), lambda i: (0, 0)), row, row, pl.BlockSpec((1, D_MODEL), lambda i: (0, 0))],
        out_shape=[SDS((1, LANE), f32), SDS((s, D_MODEL), f32), SDS((s, D_MODEL), bf16), SDS((1, D_MODEL), f32)],
        compiler_params=_cp("arbitrary"),
    )(x, target, g)


def _ffn_bwd(dx2, dx2b, x1, g, ra, w1, w2, tm=512, tf=1024):
    s = x1.shape[0]
    nf = D_FF // tf

    def body(dx_ref, dxb_ref, x_ref, g_ref, ra_ref, w1_ref, w2_ref, da_ref, act_ref, dx1_ref, dx1b_ref, dg_ref, acc):
        i, f = pl.program_id(0), pl.program_id(1)

        @pl.when(f == 0)
        def _():
            acc[...] = jnp.zeros_like(acc)

        rav = ra_ref[...].astype(f32)
        da = (_dot_nt(dxb_ref[...], w2_ref[...]) * (2.0 * rav)).astype(bf16)
        da_ref[...] = da
        act_ref[...] = (rav * rav).astype(bf16)
        acc[...] += _dot_nt(da, w1_ref[...])

        @pl.when(f == nf - 1)
        def _():
            r, xh = _rms_stats(x_ref[...])
            dx, dg = _rms_bwd(acc[...], xh, r, g_ref[...])
            dx = dx + dx_ref[...]
            dx1_ref[...] = dx
            dx1b_ref[...] = dx.astype(bf16)
            _acc_rows(dg_ref, i == 0, dg)

    row = pl.BlockSpec((tm, D_MODEL), lambda i, f: (i, 0))
    col = pl.BlockSpec((tm, tf), lambda i, f: (i, f))
    return pl.pallas_call(
        body, name="ffn_bwd", grid=(s // tm, nf),
        in_specs=[row, row, row, pl.BlockSpec((1, D_MODEL), lambda i, f: (0, 0)), col,
                  pl.BlockSpec((D_MODEL, tf), lambda i, f: (0, f)), pl.BlockSpec((tf, D_MODEL), lambda i, f: (f, 0))],
        out_specs=[col, col, row, row, pl.BlockSpec((1, D_MODEL), lambda i, f: (0, 0))],
        out_shape=[SDS((s, D_FF), bf16), SDS((s, D_FF), bf16), SDS((s, D_MODEL), f32), SDS((s, D_MODEL), bf16), SDS((1, D_MODEL), f32)],
        scratch_shapes=[pltpu.VMEM((tm, D_MODEL), f32)], compiler_params=_cp("arbitrary", "arbitrary"),
    )(dx2, dx2b, x1, g, ra, w1, w2)


def _mm_tn(a, b, nb, m, n, tm, tn, name="mm_tn", rows=None, row0=0, into=None):
    s = a.shape[0]
    mi, nj = m // tm, n // tn
    rows = m if rows is None else rows
    blk0 = row0 // tm

    def body(a_ref, b_ref, *rest):
        rest[-1][...] = _dot_tn(a_ref[...], b_ref[...]).astype(bf16)

    extra = {} if into is None else dict(input_output_aliases={2: 0})
    return pl.pallas_call(
        body, name=name, grid=(nb, mi, nj),
        in_specs=[pl.BlockSpec((s, tm), lambda k, i, j: (0, k * mi + i)), pl.BlockSpec((s, tn), lambda k, i, j: (0, k * nj + j))]
        + ([] if into is None else [pl.BlockSpec(memory_space=pl.ANY)]),
        out_specs=pl.BlockSpec((None, tm, tn), lambda k, i, j: (k, blk0 + i, j)), out_shape=SDS((nb, rows, n), bf16),
        compiler_params=_cp("parallel", "parallel", "parallel"), **extra,
    )(a, b, *([] if into is None else [into]))


def _mm_tn_slabs(a, b, nb, m, nblk, rel, width, tm=512, name="mm_tn_slabs"):
    s = a.shape[0]
    n = b.shape[1] // nb
    ng, mi, nw = n // nblk, m // tm, len(rel)

    def body(a_ref, b_ref, o_ref):
        full = _dot_tn(a_ref[...], b_ref[...])
        for r, start in enumerate(rel):
            o_ref[r] = full[:, start:start + width].astype(bf16)

    return pl.pallas_call(
        body, name=name, grid=(nb, ng, mi),
        in_specs=[pl.BlockSpec((s, tm), lambda k, g, i: (0, k * mi + i)), pl.BlockSpec((s, nblk), lambda k, g, i: (0, k * ng + g))],
        out_specs=pl.BlockSpec((nw, None, tm, width), lambda k, g, i: (g, k, i, 0)), out_shape=SDS((ng * nw, nb, m, width), bf16),
        compiler_params=_cp("parallel", "parallel", "parallel"),
    )(a, b)


def _merge_bwd(dx1b, wo, y, p, wb, after, tm=256):
    s = dx1b.shape[0]
    half = 3 * D_MODEL // 2
    blk0 = GATE_COL0 // half

    def body(dx_ref, wo_ref, y_ref, ga_ref, gb_ref, wb_ref, after_ref, dy_ref, dg_ref, dz_ref):
        del after_ref
        dm = _dot_nt(dx_ref[...], wo_ref[...])
        ga, gb = ga_ref[...], gb_ref[...]
        gates = [ga[:, :D_MODEL], jnp.concatenate([ga[:, D_MODEL:], gb[:, :D_MODEL // 2]], axis=1), gb[:, D_MODEL // 2:]]
        for n in range(3):
            cols = slice(D_MODEL * n, D_MODEL * (n + 1))
            gate = _sigmoid(gates[n])
            t = dm * gate
            dy = t.astype(bf16)
            dy_ref[:, cols] = dy
            dg_ref[:, cols] = (t * y_ref[:, cols].astype(f32) * (1.0 - gate)).astype(bf16)
            dz_ref[:, BRANCH * n:BRANCH * (n + 1)] = _dot_nt(dy, wb_ref[n])

    wide = pl.BlockSpec((tm, 3 * D_MODEL), lambda i: (i, 0))
    return pl.pallas_call(
        body, name="merge_bwd", grid=(s // tm,),
        in_specs=[pl.BlockSpec((tm, D_MODEL), lambda i: (i, 0)), pl.BlockSpec((D_MODEL, D_MODEL), lambda i: (0, 0)), wide,
                  pl.BlockSpec((tm, half), lambda i: (i, blk0)), pl.BlockSpec((tm, half), lambda i: (i, blk0 + 1)),
                  pl.BlockSpec((3, BRANCH, D_MODEL), lambda i: (0, 0, 0)), pl.BlockSpec(memory_space=pl.ANY)],
        out_specs=[wide, wide, pl.BlockSpec((tm, 3 * BRANCH), lambda i: (i, 0))],
        out_shape=[SDS((s, 3 * D_MODEL), bf16), SDS((s, 3 * D_MODEL), bf16), SDS((s, 3 * BRANCH), f32)],
        compiler_params=_cp("parallel"),
    )(dx1b, wo, y, p, p, wb, after)


def _mixer_bwd(p, dz, opre, states, lb, gout, wconv, lng, lnb, wsg, bsg_t):
    s = p.shape[0]
    tt = MIX_TILE
    nt = s // tt
    nch = tt // HGRN_CHUNK
    rev = lambda i: nt - 1 - i

    def body(q_ref, fp_ref, iv_ref, go_ref, bg_ref, cg_ref, xc_ref, u_ref, v_ref, cgp_ref, xcp_ref, dz_ref, opre_ref, st_ref,
             lb_ref, gout_ref, wconv_ref, lng_ref, lnb_ref, wsg_ref, bsg_ref,
             dp_ref, vec_ref, dwsg_ref, dbsg_ref, dst_scr, zbuf, dybuf, dvn_scr, dbsg_acc):
        i = pl.program_id(0)

        @pl.when(i == 0)
        def _():
            dst_scr[...] = jnp.zeros_like(dst_scr)
            dybuf[tt:tt + 8, :] = jnp.zeros((8, BRANCH), f32)
            vec_ref[...] = jnp.zeros_like(vec_ref)
            dwsg_ref[...] = jnp.zeros_like(dwsg_ref)
            dbsg_acc[...] = jnp.zeros_like(dbsg_acc)

        lbv = lb_ref[...]
        q_raw, fp = q_ref[...], fp_ref[...]
        sq = _sigmoid(q_raw)
        qs = q_raw * sq
        sfp = _sigmoid(fp)
        logf, snf, kk = _hgrn_gates(fp, lbv)
        inv_f = jnp.exp(-logf)
        iv = iv_ref[...]
        doa = dz_ref[:, 0:BRANCH]
        o = opre_ref[...]
        sgo = _sigmoid(go_ref[...])
        gout_v = gout_ref[...]
        d_o, dgo, dgout = [], [], []
        for h in range(HEADS):
            sl = slice(HEAD_DIM * h, HEAD_DIM * (h + 1))
            r, oh = _rms_stats(o[:, sl])
            d_on = doa[:, sl] * sgo[:, sl]
            dgo.append(doa[:, sl] * oh * gout_v[:, sl] * sgo[:, sl] * (1.0 - sgo[:, sl]))
            dx, dg = _rms_bwd(d_on, oh, r, gout_v[:, sl])
            d_o.append(dx)
            dgout.append(dg)
        d_o = jnp.concatenate(d_o, axis=1)
        dp_ref[:, 3 * BRANCH:4 * BRANCH] = jnp.concatenate(dgo, axis=1).astype(bf16)
        vec_ref[1:2, :] += jnp.concatenate(dgout, axis=1)

        causal = _tri(HGRN_CHUNK)
        tri = causal.astype(f32)
        tri_up = _tri(HGRN_CHUNK, upper=True).astype(f32)
        last_row = lax.broadcasted_iota(jnp.int32, (HGRN_CHUNK, 1), 0) == HGRN_CHUNK - 1
        lb_live = (lbv > LB_FLOOR).astype(f32)
        dlb = jnp.zeros((1, BRANCH), f32)
        for c in reversed(range(nch)):
            rows = slice(HGRN_CHUNK * c, HGRN_CHUNK * (c + 1))
            b = _dot_exact(tri, logf[rows])
            bl = jnp.sum(jnp.where(last_row, b, 0.0), axis=0, keepdims=True)
            eb, enb, edl, ebl = jnp.exp(b), jnp.exp(-b), jnp.exp(bl - b), jnp.exp(bl)
            qbf, kbf, kdf = qs[rows] * eb, kk[rows] * enb, kk[rows] * edl
            qb, kb, kd = qbf.astype(bf16), kbf.astype(bf16), kdf.astype(bf16)
            vc = iv[rows].astype(bf16)
            dob = d_o[rows].astype(bf16)
            dv, dqb, dkb, dkd, debl = [], [], [], [], []
            for h in range(HEADS):
                sl = slice(HEAD_DIM * h, HEAD_DIM * (h + 1))
                st = st_ref[c, h]
                dst = dst_scr[h]
                stb, dstb = st.astype(bf16), dst.astype(bf16)
                a = jnp.where(causal, _dot_nt(qb[:, sl], kb[:, sl]), 0.0).astype(bf16)
                da = jnp.where(causal, _dot_nt(dob[:, sl], vc[:, sl]), 0.0).astype(bf16)
                dv.append(_dot_tn(a, dob[:, sl]) + _dot_nt(kd[:, sl], dstb))
                dqb.append(_dot(dob[:, sl], stb) + _dot(da, kb[:, sl]))
                dkb.append(_dot_tn(da, qb[:, sl]))
                dkd.append(_dot(vc[:, sl], dstb))
                debl.append(jnp.sum(st * dst, axis=0, keepdims=True))
                dst_scr[h] = _dot_tn(dob[:, sl], qb[:, sl]) + dst * ebl[:, sl]
            dv, dqb, dkb, dkd = (jnp.concatenate(t, axis=1) for t in (dv, dqb, dkb, dkd))
            debl = jnp.concatenate(debl, axis=1)
            t_kd = dkd * kdf
            dbl = ebl * debl + jnp.sum(t_kd, axis=0, keepdims=True)
            db = dqb * qbf - dkb * kbf - t_kd + jnp.where(last_row, dbl, 0.0)
            dkk = dkb * enb + dkd * edl
            dlc = _dot_exact(tri_up, db)
            sq_c, q_c, sfp_c, snf_c, invf_c = sq[rows], q_raw[rows], sfp[rows], snf[rows], inv_f[rows]
            slope = (1.0 - lbv) * sfp_c * snf_c
            dp_ref[rows, 0:BRANCH] = (dqb * eb * sq_c * (1.0 + q_c * (1.0 - sq_c))).astype(bf16)
            dp_ref[rows, BRANCH:2 * BRANCH] = (slope * (dlc * invf_c - dkk)).astype(bf16)
            dp_ref[rows, 2 * BRANCH:3 * BRANCH] = dv.astype(bf16)
            dlb = dlb + jnp.sum(dlc * (lb_live - sfp_c) * invf_c - dkk * snf_c, axis=0, keepdims=True)
        vec_ref[0:1, :] += dlb

        dob_ = dz_ref[:, BRANCH:2 * BRANCH]
        bg, cg, xc = bg_ref[...], cg_ref[...], xc_ref[...]
        zc = cg * xc
        zbuf[0:8, :] = jnp.where(i < nt - 1, cgp_ref[...] * xcp_ref[...], 0.0)
        zbuf[8:8 + tt, :] = zc
        w0, w1, w2 = wconv_ref[0:1, :], wconv_ref[1:2, :], wconv_ref[2:3, :]
        y = w0 * zbuf[pl.ds(6, tt), :] + w1 * zbuf[pl.ds(7, tt), :] + w2 * zc
        dy = dob_ * bg
        dybuf[0:tt, :] = dy
        dy1, dy2 = dybuf[pl.ds(1, tt), :], dybuf[pl.ds(2, tt), :]
        dzc = w2 * dy + w1 * dy1 + w0 * dy2
        dp_ref[:, 4 * BRANCH:5 * BRANCH] = (dob_ * y).astype(bf16)
        dp_ref[:, 5 * BRANCH:6 * BRANCH] = (dzc * xc).astype(bf16)
        dp_ref[:, 6 * BRANCH:7 * BRANCH] = (dzc * cg).astype(bf16)
        vec_ref[4:5, :] += jnp.sum(zc * dy2, axis=0, keepdims=True)
        vec_ref[5:6, :] += jnp.sum(zc * dy1, axis=0, keepdims=True)
        vec_ref[6:7, :] += jnp.sum(zc * dy, axis=0, keepdims=True)
        dybuf[tt:tt + 8, :] = dybuf[0:8, :]

        doc = dz_ref[:, 2 * BRANCH:3 * BRANCH]
        u_raw, v_raw = u_ref[...], v_ref[...]
        ug = _gelu(u_raw)
        dug_scale = _gelu_grad(u_raw)
        vg = _gelu(v_raw)
        vcen = vg - jnp.mean(vg, axis=-1, keepdims=True)
        rstd = lax.rsqrt(jnp.mean(vcen * vcen, axis=-1, keepdims=True) + LN_EPS)
        vhat = vcen * rstd
        lng_v = lng_ref[...]
        vn = (vhat * lng_v + lnb_ref[...]).astype(bf16)
        low = _tri(SG_CHUNK)
        for g in range(SG_GROUPS):
            sl = slice(LANE * g, LANE * (g + 1))
            wm = jnp.where(low, wsg_ref[g], 0.0).astype(bf16)
            bias = bsg_ref[:, g:g + 1]
            dw = jnp.zeros((SG_CHUNK, SG_CHUNK), f32)
            dbs = jnp.zeros((SG_CHUNK, LANE), f32)
            for cc in range(tt // SG_CHUNK):
                rows = slice(SG_CHUNK * cc, SG_CHUNK * (cc + 1))
                vn_c = vn[rows, sl]
                sv = _dot(wm, vn_c) + bias
                doc_c = doc[rows, sl]
                dp_ref[rows, 7 * BRANCH + LANE * g:7 * BRANCH + LANE * (g + 1)] = (doc_c * sv * dug_scale[rows, sl]).astype(bf16)
                dsv = doc_c * ug[rows, sl]
                dsvb = dsv.astype(bf16)
                dbs = dbs + dsv
                dw = dw + _dot_nt(dsvb, vn_c)
                dvn_scr[rows, sl] = _dot_tn(wm, dsvb)
            dwsg_ref[g] += jnp.where(low, dw, 0.0)
            dbsg_acc[:, sl] += dbs
        dvn = dvn_scr[...]
        vec_ref[2:3, :] += jnp.sum(dvn * vhat, axis=0, keepdims=True)
        vec_ref[3:4, :] += jnp.sum(dvn, axis=0, keepdims=True)
        dvh = dvn * lng_v
        dvg = rstd * (dvh - jnp.mean(dvh, axis=-1, keepdims=True) - vhat * jnp.mean(dvh * vhat, axis=-1, keepdims=True))
        dp_ref[:, 8 * BRANCH:9 * BRANCH] = (dvg * _gelu_grad(v_raw)).astype(bf16)

        @pl.when(i == nt - 1)
        def _():
            for g in range(SG_GROUPS):
                dbsg_ref[:, g:g + 1] = jnp.sum(dbsg_acc[:, LANE * g:LANE * (g + 1)], axis=1, keepdims=True)

    full = lambda shape: pl.BlockSpec(shape, lambda i: (0,) * len(shape))
    tail = lambda c: pl.BlockSpec((8, BRANCH), lambda i: (jnp.maximum(rev(i) * (tt // 8) - 1, 0), c))
    return pl.pallas_call(
        body, name="mixer_bwd", grid=(nt,),
        in_specs=_p_specs(tt, range(9), rev) + [tail(5), tail(6), pl.BlockSpec((tt, 3 * BRANCH), lambda i: (rev(i), 0)),
                                                pl.BlockSpec((tt, BRANCH), lambda i: (rev(i), 0)),
                                                pl.BlockSpec((nch, HEADS, HEAD_DIM, HEAD_DIM), lambda i: (rev(i), 0, 0, 0)),
                                                full((1, BRANCH)), full((1, BRANCH)), full((3, BRANCH)), full((1, BRANCH)), full((1, BRANCH)),
                                                full((SG_GROUPS, SG_CHUNK, SG_CHUNK)), full((SG_CHUNK, SG_GROUPS))],
        out_specs=[pl.BlockSpec((tt, 9 * BRANCH), lambda i: (rev(i), 0)), full((8, BRANCH)), full((SG_GROUPS, SG_CHUNK, SG_CHUNK)),
                   full((SG_CHUNK, SG_GROUPS))],
        out_shape=[SDS((s, 9 * BRANCH), bf16), SDS((8, BRANCH), f32), SDS((SG_GROUPS, SG_CHUNK, SG_CHUNK), f32), SDS((SG_CHUNK, SG_GROUPS), f32)],
        scratch_shapes=[pltpu.VMEM((HEADS, HEAD_DIM, HEAD_DIM), f32), pltpu.VMEM((tt + 8, BRANCH), f32), pltpu.VMEM((tt + 8, BRANCH), f32),
                        pltpu.VMEM((tt, BRANCH), f32), pltpu.VMEM((SG_CHUNK, BRANCH), f32)],
        compiler_params=_cp("arbitrary"),
    )(*([p] * 11), dz, opre, states, lb, gout, wconv, lng, lnb, wsg, bsg_t)


def _dh_bwd(dpm, dpg, w_t, x, dx1, g, after, tm=1024, tk=1536):
    s = x.shape[0]
    km = dpm.shape[1] // tk
    nk = km + dpg.shape[1] // tk

    def body(dpm_ref, dpg_ref, w_ref, x_ref, dx1_ref, g_ref, after_ref, dx_ref, dxb_ref, dg_ref, acc):
        del after_ref
        i, k = pl.program_id(0), pl.program_id(1)

        @pl.when(k == 0)
        def _():
            acc[...] = jnp.zeros_like(acc)

        @pl.when(k < km)
        def _():
            acc[...] += _dot(dpm_ref[...], w_ref[...])

        @pl.when(k >= km)
        def _():
            acc[...] += _dot(dpg_ref[...], w_ref[...])

        @pl.when(k == nk - 1)
        def _():
            r, xh = _rms_stats(x_ref[...])
            dx, dg = _rms_bwd(acc[...], xh, r, g_ref[...])
            dx = dx + dx1_ref[...]
            dx_ref[...] = dx
            dxb_ref[...] = dx.astype(bf16)
            _acc_rows(dg_ref, i == 0, dg)

    row = pl.BlockSpec((tm, D_MODEL), lambda i, k: (i, 0))
    vec = pl.BlockSpec((1, D_MODEL), lambda i, k: (0, 0))
    return pl.pallas_call(
        body, name="dh_bwd", grid=(s // tm, nk),
        in_specs=[pl.BlockSpec((tm, tk), lambda i, k: (i, jnp.minimum(k, km - 1))),
                  pl.BlockSpec((tm, tk), lambda i, k: (i, jnp.maximum(k - km, 0))),
                  pl.BlockSpec((tk, D_MODEL), lambda i, k: (k, 0)), row, row, vec, pl.BlockSpec(memory_space=pl.ANY)],
        out_specs=[row, row, vec], out_shape=[SDS((s, D_MODEL), f32), SDS((s, D_MODEL), bf16), SDS((1, D_MODEL), f32)],
        scratch_shapes=[pltpu.VMEM((tm, D_MODEL), f32)], compiler_params=_cp("arbitrary", "arbitrary"),
    )(dpm, dpg, w_t, x, dx1, g, after)


def _layer_fwd(x, w, sm):
    p, h = _rms_mm(x, sm["g_mix"], w["w_in"])
    z, opre, states = _mixer_fwd(p, sm["lb"], sm["g_out"], sm["w_conv"], sm["ln_g"], sm["ln_b"], sm["w_sg"], sm["b_sg_t"])
    y, merged = _branch_gate(z, w["w_branch"], p)
    x1 = _mm_resid(x, merged, w["w_o"])
    x2, h2, ra = _ffn(x1, sm["g_ffn"], w["w_ff1"], w["w_ff2"])
    saved = dict(x=x, p=p, h=h, z=z, opre=opre, states=states, y=y, merged=merged, x1=x1, h2=h2, ra=ra)
    return x2, saved


def _layer_bwd(dx2, dx2b, sv, w, sm, between, before_end):
    nchip = N_DEV // 2
    by_chip = lambda g: g.reshape((nchip, 2) + g.shape[1:])
    da, act, dx1, dx1b, dg_ffn = _ffn_bwd(dx2, dx2b, sv["x1"], sm["g_ffn"], sv["ra"], w["w_ff1"], w["w_ff2"])
    g_ff2 = by_chip(_mm_tn(act, dx2b, 1, D_FF, D_MODEL, 512, 1024, name="dw_ff2")[0].reshape(N_DEV, D_FF // N_DEV, D_MODEL))
    g_ff1 = by_chip(_mm_tn_slabs(sv["h2"], da, 1, D_MODEL, D_FF // 2, [i * (D_FF // N_DEV) for i in range(nchip)], D_FF // N_DEV,
                                 name="dw_ff1")[:, 0])
    g_o = by_chip(_mm_tn(sv["merged"], dx1b, 1, D_MODEL, D_MODEL, 512, 1024, name="dw_o")[0].reshape(N_DEV, D_MODEL // N_DEV, D_MODEL))
    dy, dpg, dz = _merge_bwd(dx1b, w["w_o"], sv["y"], sv["p"], w["w_branch"], between(dx1))
    g_branch = by_chip(_mm_tn_slabs(sv["z"], dy, 3, BRANCH, D_MODEL, [i * (D_MODEL // N_DEV) for i in range(N_DEV)], D_MODEL // N_DEV,
                                    name="dw_branch"))
    g_in = _mm_tn(dpg, sv["h"], 1, 3 * D_MODEL, D_MODEL, 512, 1024, name="dw_in_gates", rows=N_COLS, row0=GATE_COL0)
    dpm, vecs, dwsg, dbsg_t = _mixer_bwd(sv["p"], dz, sv["opre"], sv["states"], sm["lb"], sm["g_out"], sm["w_conv"],
                                         sm["ln_g"], sm["ln_b"], sm["w_sg"], sm["b_sg_t"])
    g_in = _mm_tn(dpm, sv["h"], 1, GATE_COL0, D_MODEL, 512, 1024, name="dw_in_mixers", rows=N_COLS, into=g_in)
    g_in = by_chip(g_in[0].reshape(N_DEV, SHARD_IN, D_MODEL))
    dx, dxb, dg_mix = _dh_bwd(dpm, dpg, w["w_in"], sv["x"], dx1, sm["g_mix"], before_end(g_in))
    big = dict(w_in=g_in, w_branch=g_branch, w_o=g_o, w_ff1=g_ff1, w_ff2=g_ff2)
    small = dict(g_mix=dg_mix, g_ffn=dg_ffn, vecs=vecs, w_sg=dwsg, b_sg_t=dbsg_t, dx1=dx1)
    return dx, dxb, big, small


BIG = ("w_in", "w_branch", "w_o", "w_ff1", "w_ff2")
ANY = pl.BlockSpec(memory_space=pl.ANY)


def _place():
    return lax.axis_index("x"), lax.axis_index("y"), lax.axis_index("c")


def _al(v, m):
    return pl.multiple_of(v * m, m)


def _shard_of(refs, dev):
    w_in, w_b, w_o, w_1, w_2 = refs
    ni, nb, no, n1, n2 = w_in.shape[0] // N_DEV, w_b.shape[-1] // N_DEV, w_o.shape[0] // N_DEV, w_1.shape[-1] // N_DEV, w_2.shape[0] // N_DEV
    return [w_in.at[pl.ds(_al(dev, ni), ni), :], w_b.at[:, :, pl.ds(_al(dev, nb), nb)], w_o.at[pl.ds(_al(dev, no), no), :],
            w_1.at[:, pl.ds(_al(dev, n1), n1)], w_2.at[pl.ds(_al(dev, n2), n2), :]]


def _gather_out_shapes(shards):
    s_in, s_b, s_o, s_1, s_2 = (shards[n] for n in BIG)
    return [SDS((s_in.shape[1] * N_DEV, s_in.shape[2]), bf16), SDS(s_b.shape[1:3] + (s_b.shape[3] * N_DEV,), bf16),
            SDS((s_o.shape[1] * N_DEV, s_o.shape[2]), bf16), SDS((s_1.shape[1], s_1.shape[2] * N_DEV), bf16),
            SDS((s_2.shape[1] * N_DEV, s_2.shape[2]), bf16)]


def _seq_all_gather_layer(layer, shard_refs, out_shapes):
    nt = len(BIG)
    outs = [jax.empty_ref(sh, memory_space=pltpu.MemorySpace.HBM) for sh in out_shapes]
    early, late = (0,), tuple(range(1, nt))

    @pl.kernel(mesh=plsc.ScalarSubcoreMesh(axis_name="seq", num_cores=1), name=f"seq_all_gather_l{layer}",
               scratch_types=(pltpu.SemaphoreType.DMA((9,)), pltpu.SemaphoreType.DMA((9,))),
               compiler_params=pltpu.CompilerParams(collective_id=1))
    def launch(send_sems, recv_sems):
        x, y, c = _place()
        me, sibling = (x, y, c), (x, y, 1 - c)
        first, second, diag = _ici_route(x, y, c)
        _handshake([sibling, first, second])
        mine = [r.at[layer] for r in shard_refs]

        def copies(k, blk, to, src=None, which=range(nt)):
            dst = _shard_of(outs, 4 * blk[0] + 2 * blk[1] + blk[2])
            src = dst if src is None else src
            return [pltpu.make_async_remote_copy(src_ref=src[t], dst_ref=dst[t], send_sem=send_sems.at[k], recv_sem=recv_sems.at[k],
                                                 device_id=to, device_id_type=MESH) for t in which]

        def start(cps):
            for cp in cps:
                cp.start()
            return cps

        def landed(cps):
            for cp in cps:
                cp.wait_recv()

        sent = start(copies(0, me, sibling, src=mine) + copies(1, me, first, src=mine, which=early)
                     + copies(2, me, first, src=mine, which=late) + copies(3, me, second, src=mine))
        landed(copies(1, first, me, which=early))
        sent += start(copies(4, first, second, which=early) + copies(6, first, sibling, which=early))
        landed(copies(2, first, me, which=late))
        sent += start(copies(5, first, second, which=late) + copies(6, first, sibling, which=late))
        landed(copies(3, second, me))
        sent += start(copies(7, second, sibling))
        landed(copies(4, diag, me, which=early) + copies(5, diag, me, which=late))
        sent += start(copies(8, diag, sibling))
        other = lambda p: (p[0], p[1], 1 - c)
        landed(copies(0, sibling, me) + copies(6, other(second), me) + copies(7, other(first), me) + copies(8, other(diag), me))
        for cp in sent:
            cp.wait_send()

    launch()
    return [o[...] for o in outs]


def _ici_route(x, y, c):
    return (x ^ (1 - c), y ^ c, c), (x ^ c, y ^ (1 - c), c), (1 - x, 1 - y, c)


def _place_own(where, shards, gathered, after):
    s_in, s_b, s_o, s_1, s_2 = (shards[n] for n in BIG)
    nt = len(BIG)

    def body(where_ref, *refs):
        del where_ref
        for src, dst in zip(refs[:nt], refs[2 * nt + 1:]):
            dst[...] = src[...]

    lay = lambda shape, fn: pl.BlockSpec(shape, fn)
    in_specs = [lay((None,) + s_in.shape[1:], lambda i, wh: (wh[0], 0, 0)), lay((None,) + s_b.shape[1:], lambda i, wh: (wh[0], 0, 0, 0)),
                lay((None,) + s_o.shape[1:], lambda i, wh: (wh[0], 0, 0)), lay((None,) + s_1.shape[1:], lambda i, wh: (wh[0], 0, 0)),
                lay((None,) + s_2.shape[1:], lambda i, wh: (wh[0], 0, 0))]
    out_specs = [lay(s_in.shape[1:], lambda i, wh: (wh[1], 0)), lay(s_b.shape[1:], lambda i, wh: (0, 0, wh[1])),
                 lay(s_o.shape[1:], lambda i, wh: (wh[1], 0)), lay(s_1.shape[1:], lambda i, wh: (0, wh[1])),
                 lay(s_2.shape[1:], lambda i, wh: (wh[1], 0))]
    return pl.pallas_call(
        body, name="place_own", out_shape=[SDS(g.shape, g.dtype) for g in gathered],
        input_output_aliases={1 + nt + i: i for i in range(nt)}, compiler_params=_cp("arbitrary"),
        grid_spec=pltpu.PrefetchScalarGridSpec(num_scalar_prefetch=1, grid=(1,), in_specs=in_specs + [ANY] * (nt + 1), out_specs=out_specs),
    )(where, s_in, s_b, s_o, s_1, s_2, *gathered, after)


def _handshake(peers):
    barrier = pltpu.get_barrier_semaphore()
    for p in peers:
        pl.semaphore_signal(barrier, inc=1, device_id=p, device_id_type=MESH)
    pl.semaphore_wait(barrier, len(peers))


def _seq_exchange_on_chip(grads):
    nt, nchip = len(BIG), N_DEV // 2
    g_refs = [jax.new_ref(g, memory_space=pltpu.MemorySpace.HBM) for g in grads]
    outs = [jax.empty_ref(SDS((nchip,) + g.shape[2:], bf16), memory_space=pltpu.MemorySpace.HBM) for g in grads]

    @pl.kernel(mesh=plsc.ScalarSubcoreMesh(axis_name="seq", num_cores=1), name="seq_rs_on_chip",
               scratch_types=(pltpu.SemaphoreType.DMA((nchip,)), pltpu.SemaphoreType.DMA((nchip,))),
               compiler_params=pltpu.CompilerParams(collective_id=2))
    def launch(send_sems, recv_sems):
        x, y, c = _place()
        sibling = (x, y, 1 - c)
        _handshake([sibling])
        remote = [pltpu.make_async_remote_copy(src_ref=g_refs[t].at[j, 1 - c], dst_ref=outs[t].at[j], send_sem=send_sems.at[j],
                                               recv_sem=recv_sems.at[j], device_id=sibling, device_id_type=MESH)
                  for j in range(nchip) for t in range(nt)]
        for cp in remote:
            cp.start()
        for cp in remote:
            cp.wait_recv()
        for cp in remote:
            cp.wait_send()

    launch()
    return [o[...] for o in outs], [g[...] for g in g_refs]


def _seq_exchange_between_chips(sums):
    nt = len(BIG)
    s_refs = [jax.new_ref(a, memory_space=pltpu.MemorySpace.HBM) for a in sums]
    outs = [jax.empty_ref(SDS((3,) + a.shape[1:], bf16), memory_space=pltpu.MemorySpace.HBM) for a in sums]
    transit = [jax.empty_ref(SDS(a.shape[1:], bf16), memory_space=pltpu.MemorySpace.HBM) for a in sums]

    @pl.kernel(mesh=plsc.ScalarSubcoreMesh(axis_name="seq", num_cores=1), name="seq_rs_between_chips",
               scratch_types=(pltpu.SemaphoreType.DMA((4,)), pltpu.SemaphoreType.DMA((4,))),
               compiler_params=pltpu.CompilerParams(collective_id=3))
    def launch(send_sems, recv_sems):
        x, y, c = _place()
        first, second, diag = _ici_route(x, y, c)
        _handshake([first, second])

        def copies(k, src, dst, to):
            return [pltpu.make_async_remote_copy(src_ref=src(t), dst_ref=dst(t), send_sem=send_sems.at[k], recv_sem=recv_sems.at[k],
                                                 device_id=to, device_id_type=MESH) for t in range(nt)]

        chip_of = lambda p: 2 * p[0] + p[1]
        direct = (copies(0, lambda t: s_refs[t].at[chip_of(first)], lambda t: outs[t].at[0], first)
                  + copies(1, lambda t: s_refs[t].at[chip_of(second)], lambda t: outs[t].at[1], second)
                  + copies(2, lambda t: s_refs[t].at[chip_of(diag)], lambda t: transit[t], first))
        for cp in direct:
            cp.start()
        for cp in direct[2 * nt:]:
            cp.wait_recv()
        passed = copies(3, lambda t: transit[t], lambda t: outs[t].at[2], second)
        for cp in passed:
            cp.start()
        for cp in direct[:2 * nt] + passed:
            cp.wait_recv()
        for cp in direct + passed:
            cp.wait_send()

    launch()
    return [o[...] for o in outs], [a[...] for a in s_refs]


def _chip_sums(core, mine, other, after, steps=2):
    nt, nchip = len(mine), mine[0].shape[0]
    m4 = [a.reshape(nchip, 2, -1, a.shape[-1]) for a in mine]
    o3 = [a.reshape(nchip, -1, a.shape[-1]) for a in other]

    def body(c_ref, *refs):
        del c_ref
        for a_ref, b_ref, o_ref in zip(refs[:nt], refs[nt:2 * nt], refs[2 * nt + 1:]):
            o_ref[...] = (a_ref[...].astype(f32) + b_ref[...].astype(f32)).astype(bf16)

    tiles = [(a.shape[1] // steps, a.shape[2]) for a in o3]
    blks = [pl.BlockSpec((None,) + t, lambda j, i, c_ref: (j, i, 0)) for t in tiles]
    outs = pl.pallas_call(
        body, name="chip_sums", out_shape=[SDS(a.shape, bf16) for a in o3], compiler_params=_cp("parallel", "parallel"),
        grid_spec=pltpu.PrefetchScalarGridSpec(
            num_scalar_prefetch=1, grid=(nchip, steps),
            in_specs=[pl.BlockSpec((None, None) + t, lambda j, i, c_ref: (j, c_ref[0], i, 0)) for t in tiles] + blks + [ANY],
            out_specs=blks),
    )(core, *m4, *o3, after)
    return [o.reshape(a.shape) for o, a in zip(outs, other)]


def _all_reduce_rows(pack):
    rows = pack.shape[0]
    blk = rows // N_DEV

    def body(in_ref, out_ref, land, send1, recv1, send2, recv2):
        x, y, c = _place()
        me = 4 * x + 2 * y + c
        others = [(px, py, pc) for px in range(2) for py in range(2) for pc in range(2)]

        def is_me(p):
            return jnp.logical_and(jnp.logical_and(p[0] == x, p[1] == y), p[2] == c)

        land[me] = in_ref[pl.ds(_al(me, blk), blk), :]
        for d, p in enumerate(others):
            @pl.when(jnp.logical_not(is_me(p)))
            def _():
                pltpu.make_async_remote_copy(src_ref=in_ref.at[pl.ds(d * blk, blk), :], dst_ref=land.at[me], send_sem=send1.at[d],
                                             recv_sem=recv1.at[me], device_id=p, device_id_type=MESH).start()
        for d, p in enumerate(others):
            @pl.when(jnp.logical_not(is_me(p)))
            def _():
                cp = pltpu.make_async_remote_copy(src_ref=in_ref.at[pl.ds(d * blk, blk), :], dst_ref=land.at[d], send_sem=send1.at[d],
                                                  recv_sem=recv1.at[d], device_id=p, device_id_type=MESH)
                cp.wait_recv()
                cp.wait_send()
        total = land[0]
        for d in range(1, N_DEV):
            total = total + land[d]
        out_ref[pl.ds(_al(me, blk), blk), :] = total
        for d, p in enumerate(others):
            @pl.when(jnp.logical_not(is_me(p)))
            def _():
                mine = out_ref.at[pl.ds(_al(me, blk), blk), :]
                pltpu.make_async_remote_copy(src_ref=mine, dst_ref=mine, send_sem=send2.at[d], recv_sem=recv2.at[me],
                                             device_id=p, device_id_type=MESH).start()
        for d, p in enumerate(others):
            @pl.when(jnp.logical_not(is_me(p)))
            def _():
                theirs = out_ref.at[pl.ds(d * blk, blk), :]
                cp = pltpu.make_async_remote_copy(src_ref=theirs, dst_ref=theirs, send_sem=send2.at[d], recv_sem=recv2.at[d],
                                                  device_id=p, device_id_type=MESH)
                cp.wait_recv()
                cp.wait_send()

    vm = pl.BlockSpec(memory_space=pltpu.VMEM)
    return pl.pallas_call(
        body, name="all_reduce_rows", in_specs=[vm], out_specs=vm, out_shape=SDS((rows, LANE), f32),
        scratch_shapes=[pltpu.VMEM((N_DEV, blk, LANE), f32)] + [pltpu.SemaphoreType.DMA((N_DEV,))] * 4,
        compiler_params=pltpu.CompilerParams(vmem_limit_bytes=VMEM_LIMIT),
    )(pack)


def _lower_bounds_fwd(lower):
    def body(l_ref, o_ref):
        sm = _layer_softmax(l_ref)
        run = jnp.zeros_like(sm[0])
        for l in range(DEPTH):
            o_ref[l:l + 1, :] = run
            if l + 1 < DEPTH:
                run = run + sm[l + 1]

    return pl.pallas_call(body, name="lower_bounds_fwd", out_shape=SDS(lower.shape, f32))(lower)


def _layer_softmax(l_ref):
    rows = [l_ref[l:l + 1, :] for l in range(DEPTH)]
    top = functools.reduce(jnp.maximum, rows)
    e = [jnp.exp(r - top) for r in rows]
    tot = functools.reduce(lambda a, b: a + b, e)
    return [v / tot for v in e]


def _lower_bounds_bwd(lower, dlbs):
    def body(l_ref, d_ref, o_ref):
        sm = _layer_softmax(l_ref)
        dsm = [None] * DEPTH
        run = jnp.zeros_like(sm[0])
        dsm[0] = run
        for l in reversed(range(1, DEPTH)):
            run = run + d_ref[l:l + 1, :]
            dsm[l] = run
        inner = functools.reduce(lambda a, b: a + b, [sm[l] * dsm[l] for l in range(DEPTH)])
        for l in range(DEPTH):
            o_ref[l:l + 1, :] = sm[l] * (dsm[l] - inner)

    return pl.pallas_call(body, name="lower_bounds_bwd", out_shape=SDS(lower.shape, f32))(lower, dlbs)


_ADAM_C1 = 1.0 - ADAM_B1 ** ADAM_STEP
_ADAM_C2 = 1.0 - ADAM_B2 ** ADAM_STEP


def _adamw(w, g, m, v):
    m = ADAM_B1 * m + (1.0 - ADAM_B1) * g
    v = ADAM_B2 * v + (1.0 - ADAM_B2) * (g * g)
    delta = -ADAM_LR * ((m / _ADAM_C1) / (jnp.sqrt(v / _ADAM_C2) + ADAM_EPS) + ADAM_WD * w)
    return delta, m, v


def _row_tile(rows, cap):
    return next(t for t in range(min(cap, rows) // 16 * 16, 0, -16) if rows % t == 0)


def _adam_big(where, w, m, v, sums, landed, outs, after, tr=512):
    shape = w.shape
    cols = shape[-1]
    w3, m3, v3 = (a.reshape(DEPTH, -1, cols) for a in (w, m, v))
    outs3 = [a.reshape(DEPTH, -1, cols) for a in outs]
    sums3 = sums.reshape(sums.shape[0], -1, cols)
    land3 = landed.reshape(3, -1, cols)
    rows = w3.shape[1]
    tr = _row_tile(rows, tr)
    gcols = cols

    def body(where_ref, w_ref, m_ref, v_ref, sum_ref, land_ref, *rest):
        del where_ref
        g_ref, d_ref, nm_ref, nv_ref = rest[5:]
        g = sum_ref[...].astype(f32)
        for k in range(3):
            g = g + land_ref[k].astype(f32)
        delta, nm, nv = _adamw(w_ref[...], g, m_ref[...], v_ref[...])
        g_ref[...] = g
        d_ref[...] = delta
        nm_ref[...] = nm
        nv_ref[...] = nv

    blk = pl.BlockSpec((None, tr, cols), lambda i, wh: (wh[0], i, 0))
    res = pl.pallas_call(
        body, name="adam_big", out_shape=[SDS(w3.shape, f32)] * 4, input_output_aliases={6 + i: i for i in range(4)},
        compiler_params=_cp("parallel"),
        grid_spec=pltpu.PrefetchScalarGridSpec(
            num_scalar_prefetch=1, grid=(rows // tr,),
            in_specs=[blk, blk, blk, pl.BlockSpec((None, tr, gcols), lambda i, wh: (wh[1], i, 0)),
                      pl.BlockSpec((3, tr, gcols), lambda i, wh: (0, i, 0))] + [ANY] * 5,
            out_specs=[blk] * 4),
    )(where, w3, m3, v3, sums3, land3, *outs3, after)
    return [o.reshape(shape) for o in res]


def _adam_rows(w, g, m, v):
    def body(w_ref, g_ref, m_ref, v_ref, d_ref, nm_ref, nv_ref):
        delta, nm, nv = _adamw(w_ref[...], g_ref[...], m_ref[...], v_ref[...])
        d_ref[...] = delta
        nm_ref[...] = nm
        nv_ref[...] = nv

    return pl.pallas_call(body, name="adam_rows", out_shape=[SDS(w.shape, f32)] * 3)(w, g, m, v)


SMALL = ("g_mix", "lower_bounds", "g_hgrn_out", "w_conv", "sg_ln_g", "sg_ln_b", "w_sg", "b_sg", "g_ffn", "g_final")
WEIGHTS = ("w_in", "g_mix", "lower_bounds", "g_hgrn_out", "w_conv", "sg_ln_g", "sg_ln_b", "w_sg", "b_sg", "w_branch", "w_o", "g_ffn",
           "w_ff1", "w_ff2", "g_final")


def _pack_rows(arrays, multiple):
    flat = jnp.concatenate([a.reshape(-1) for a in arrays])
    rows = -(-flat.shape[0] // (LANE * multiple)) * multiple
    return jnp.pad(flat, (0, rows * LANE - flat.shape[0])).reshape(rows, LANE)


def _unpack_rows(pack, like):
    flat = pack.reshape(-1)
    out, at = [], 0
    for a in like:
        out.append(flat[at:at + a.size].reshape(a.shape))
        at += a.size
    return out


def kernel(x, w_in, g_mix, lower_bounds, g_hgrn_out, w_conv, sg_ln_g, sg_ln_b, w_sg, b_sg, w_branch, w_o, g_ffn, w_ff1, w_ff2, g_final, loss_target, m_w_in, m_g_mix, m_lower_bounds, m_g_hgrn_out, m_w_conv, m_sg_ln_g, m_sg_ln_b, m_w_sg, m_b_sg, m_w_branch, m_w_o, m_g_ffn, m_w_ff1, m_w_ff2, m_g_final, v_w_in, v_g_mix, v_lower_bounds, v_g_hgrn_out, v_w_conv, v_sg_ln_g, v_sg_ln_b, v_w_sg, v_b_sg, v_w_branch, v_w_o, v_g_ffn, v_w_ff1, v_w_ff2, v_g_final):
    weights = dict(w_in=w_in, g_mix=g_mix, lower_bounds=lower_bounds, g_hgrn_out=g_hgrn_out, w_conv=w_conv, sg_ln_g=sg_ln_g,
                   sg_ln_b=sg_ln_b, w_sg=w_sg, b_sg=b_sg, w_branch=w_branch, w_o=w_o, g_ffn=g_ffn, w_ff1=w_ff1, w_ff2=w_ff2, g_final=g_final)
    mom1 = dict(w_in=m_w_in, g_mix=m_g_mix, lower_bounds=m_lower_bounds, g_hgrn_out=m_g_hgrn_out, w_conv=m_w_conv, sg_ln_g=m_sg_ln_g,
                sg_ln_b=m_sg_ln_b, w_sg=m_w_sg, b_sg=m_b_sg, w_branch=m_w_branch, w_o=m_w_o, g_ffn=m_g_ffn, w_ff1=m_w_ff1, w_ff2=m_w_ff2,
                g_final=m_g_final)
    mom2 = dict(w_in=v_w_in, g_mix=v_g_mix, lower_bounds=v_lower_bounds, g_hgrn_out=v_g_hgrn_out, w_conv=v_w_conv, sg_ln_g=v_sg_ln_g,
                sg_ln_b=v_sg_ln_b, w_sg=v_w_sg, b_sg=v_b_sg, w_branch=v_w_branch, w_o=v_w_o, g_ffn=v_g_ffn, w_ff1=v_w_ff1, w_ff2=v_w_ff2,
                g_final=v_g_final)
    xi, yi, ci = _place()
    dev = 4 * xi + 2 * yi + ci
    conv_cols = w_conv.shape[-1]

    for d in (weights, mom1, mom2):
        d["w_in"] = jnp.swapaxes(d["w_in"], 1, 2)
    shards = {n: weights[n].astype(bf16) for n in BIG}

    conv_place = lax.dynamic_update_slice(jnp.zeros((DEPTH, 3, BRANCH), f32), w_conv, (0, 0, dev * conv_cols))
    (w_conv_full,) = _unpack_rows(_all_reduce_rows(_pack_rows([conv_place], 8 * N_DEV)), [conv_place])
    lbs = _lower_bounds_fwd(lower_bounds)

    def small_of(l):
        return dict(g_mix=g_mix[l][None], lb=lbs[l][None], g_out=g_hgrn_out[l][None], w_conv=w_conv_full[l], ln_g=sg_ln_g[l][None],
                    ln_b=sg_ln_b[l][None], w_sg=w_sg[l], b_sg_t=b_sg[l].T, g_ffn=g_ffn[l][None])

    act = x[0]
    full, saved = [], []
    shard_refs = [jax.new_ref(shards[n], memory_space=pltpu.MemorySpace.HBM) for n in BIG]
    gathered = [_seq_all_gather_layer(l, shard_refs, _gather_out_shapes(shards)) for l in range(DEPTH)]
    for l in range(DEPTH):
        full.append(dict(zip(BIG, _place_own(jnp.stack([jnp.int32(l), dev.astype(jnp.int32)]), shards, gathered[l], act))))
        act, sv = _layer_fwd(act, full[l], small_of(l))
        saved.append(sv)
    loss_row, dx, dxb, dg_final = _final(act, loss_target[0], g_final[None])
    loss = lax.psum(loss_row[0, 0], ("x", "y", "c"))

    core = ci.astype(jnp.int32)[None]
    big_out = {n: [lax.empty(weights[n].shape, f32) for _ in range(4)] for n in BIG}
    small_grads = [None] * DEPTH

    def chip_sums(stage, after):
        l, received, mine = stage
        sums = _chip_sums(core, mine, received, after)
        placed.append(sums[BIG.index("w_o")])
        landed, sums = _seq_exchange_between_chips(sums)
        return l, sums, landed

    def adam_layer(stage, after):
        l, sums, landed = stage
        where = jnp.stack([jnp.int32(l), (2 * xi + yi).astype(jnp.int32)])
        for t, n in enumerate(BIG):
            big_out[n] = _adam_big(where, weights[n], mom1[n], mom2[n], sums[t], landed[t], big_out[n], after)

    above = None
    placed = []
    for l in reversed(range(DEPTH)):
        summed = []

        def between(dx1):
            if above is None:
                return dx1
            summed.append(chip_sums(above, dx1))
            return placed[-1]

        def before_end(g_in):
            if not summed:
                return g_in
            adam_layer(summed[0], g_in)
            return big_out["w_o"][0]

        dx, dxb, big, small_grads[l] = _layer_bwd(dx, dxb, saved[l], full[l], small_of(l), between, before_end)
        above = (l, *_seq_exchange_on_chip([big[n] for n in BIG]))

    stack = lambda f: jnp.stack([f(small_grads[l]) for l in range(DEPTH)])
    d_lower = _lower_bounds_bwd(lower_bounds, stack(lambda s: s["vecs"][0]))
    local_small = dict(g_mix=stack(lambda s: s["g_mix"][0]), lower_bounds=d_lower, g_hgrn_out=stack(lambda s: s["vecs"][1]),
                       w_conv=stack(lambda s: s["vecs"][4:7]), sg_ln_g=stack(lambda s: s["vecs"][2]), sg_ln_b=stack(lambda s: s["vecs"][3]),
                       w_sg=stack(lambda s: s["w_sg"]), b_sg=stack(lambda s: s["b_sg_t"].T), g_ffn=stack(lambda s: s["g_ffn"][0]),
                       g_final=dg_final[0])
    order = [local_small[n] for n in SMALL]
    grads = dict(zip(SMALL, _unpack_rows(_all_reduce_rows(_pack_rows(order, 8 * N_DEV)), order)))
    grads["w_conv"] = lax.dynamic_slice(grads["w_conv"], (0, 0, dev * conv_cols), (DEPTH, 3, conv_cols))

    deltas, new_m, new_v = {}, {}, {}
    packs = [_pack_rows([d[n] for n in SMALL], 8) for d in (weights, grads, mom1, mom2)]
    like = [weights[n] for n in SMALL]
    small_out = _adam_rows(*packs)
    for out, pack in zip((deltas, new_m, new_v), small_out):
        out.update(zip(SMALL, _unpack_rows(pack, like)))
    adam_layer(chip_sums(above, dx), small_out[0])
    for n in BIG:
        grads[n], deltas[n], new_m[n], new_v[n] = (jnp.swapaxes(a, 1, 2) if n == "w_in" else a for a in big_out[n])

    return (loss, dx[None], *[grads[n] for n in WEIGHTS], *[deltas[n] for n in WEIGHTS], *[new_m[n] for n in WEIGHTS],
            *[new_v[n] for n in WEIGHTS])
```

```python
import functools

import jax
import jax.numpy as jnp
from jax import lax
from jax.experimental import pallas as pl
from jax.experimental.pallas import tpu as pltpu
from jax.experimental.pallas import tpu_sc as plsc

f32 = jnp.float32
bf16 = jnp.bfloat16
SDS = jax.ShapeDtypeStruct
MESH = pl.DeviceIdType.MESH

D_MODEL = 1024
BRANCH = 512
N_COLS = 7680
D_FF = 4096
DEPTH = 4
HEADS = 4
HEAD_DIM = 128
HGRN_CHUNK = 64
SG_CHUNK = 128
SG_GROUPS = 4
NORM_EPS = 1e-6
LN_EPS = 1e-5
LB_FLOOR = 1e-30
N_DEV = 8
SHARD_IN = N_COLS // N_DEV
WIN = 1024
LANE = 128
GATE_COL0 = 9 * BRANCH

ADAM_LR = 0.001
ADAM_B1 = 0.9
ADAM_B2 = 0.999
ADAM_EPS = 1e-08
ADAM_WD = 0.01
ADAM_STEP = 10

MIX_TILE = 256
VMEM_LIMIT = 56 * 1024 * 1024


def _cp(*sem):
    return pltpu.CompilerParams(dimension_semantics=sem or None, vmem_limit_bytes=VMEM_LIMIT)


def _dot(a, b):
    return jnp.dot(a, b, preferred_element_type=f32)


def _dot_nt(a, b):
    return lax.dot_general(a, b, (((1,), (1,)), ((), ())), preferred_element_type=f32)


def _dot_tn(a, b):
    return lax.dot_general(a, b, (((0,), (0,)), ((), ())), preferred_element_type=f32)


def _dot_exact(a, b):
    return jnp.dot(a, b, precision=lax.Precision.HIGHEST, preferred_element_type=f32)


def _sigmoid(x):
    return jax.nn.sigmoid(x)


_GELU_C = 0.7978845608028654
_GELU_A = 0.044715


def _gelu(x):
    return 0.5 * x * (1.0 + jnp.tanh(_GELU_C * (x + _GELU_A * x * x * x)))


def _gelu_grad(x):
    x2 = x * x
    t = jnp.tanh(_GELU_C * (x + _GELU_A * x * x2))
    return 0.5 * (1.0 + t) + 0.5 * x * (1.0 - t * t) * _GELU_C * (1.0 + 3.0 * _GELU_A * x2)


def _rms_stats(x):
    r = lax.rsqrt(jnp.mean(x * x, axis=-1, keepdims=True) + NORM_EPS)
    return r, x * r


def _rms_bwd(dh, xh, r, g):
    dg = jnp.sum(dh * xh, axis=0, keepdims=True)
    dxn = dh * g
    dx = r * (dxn - xh * jnp.mean(dxn * xh, axis=-1, keepdims=True))
    return dx, dg


def _tri(n, upper=False):
    r = lax.broadcasted_iota(jnp.int32, (n, n), 0)
    c = lax.broadcasted_iota(jnp.int32, (n, n), 1)
    return (c >= r) if upper else (c <= r)


def _acc_rows(ref, first, val):
    @pl.when(first)
    def _():
        ref[...] = val

    @pl.when(jnp.logical_not(first))
    def _():
        ref[...] += val


def _rms_mm(x, g, w_t, tm=1024, tn=1920):
    s, n = x.shape[0], w_t.shape[0]

    def body(x_ref, g_ref, w_ref, p_ref, h_ref, hs):
        @pl.when(pl.program_id(1) == 0)
        def _():
            _, xh = _rms_stats(x_ref[...])
            hv = (xh * g_ref[...]).astype(bf16)
            hs[...] = hv
            h_ref[...] = hv

        p_ref[...] = _dot_nt(hs[...], w_ref[...])

    return pl.pallas_call(
        body, name="rms_mm", grid=(s // tm, n // tn),
        in_specs=[pl.BlockSpec((tm, D_MODEL), lambda i, j: (i, 0)), pl.BlockSpec((1, D_MODEL), lambda i, j: (0, 0)),
                  pl.BlockSpec((tn, D_MODEL), lambda i, j: (j, 0))],
        out_specs=[pl.BlockSpec((tm, tn), lambda i, j: (i, j)), pl.BlockSpec((tm, D_MODEL), lambda i, j: (i, 0))],
        out_shape=[SDS((s, n), f32), SDS((s, D_MODEL), bf16)],
        scratch_shapes=[pltpu.VMEM((tm, D_MODEL), bf16)], compiler_params=_cp("parallel", "arbitrary"),
    )(x, g, w_t)


def _hgrn_gates(fp, lb):
    logf = jnp.logaddexp(jnp.log(jnp.maximum(lb, LB_FLOOR)), jnp.log1p(-lb) + jax.nn.log_sigmoid(fp))
    snf = _sigmoid(-fp)
    return logf, snf, (1.0 - lb) * snf


def _p_specs(tile, cols, row_map):
    return [pl.BlockSpec((tile, BRANCH), functools.partial(lambda c, i: (row_map(i), c), c)) for c in cols]


def _mixer_fwd(p, lb, gout, wconv, lng, lnb, wsg, bsg_t):
    s = p.shape[0]
    tt = MIX_TILE
    nch = tt // HGRN_CHUNK

    def body(q_ref, fp_ref, iv_ref, go_ref, bg_ref, cg_ref, xc_ref, u_ref, v_ref, lb_ref, gout_ref, wconv_ref, lng_ref,
             lnb_ref, wsg_ref, bsg_ref, z_ref, opre_ref, st_ref, st_scr, zbuf):
        @pl.when(pl.program_id(0) == 0)
        def _():
            st_scr[...] = jnp.zeros_like(st_scr)
            zbuf[0:8, :] = jnp.zeros((8, BRANCH), f32)

        lbv = lb_ref[...]
        q_raw = q_ref[...]
        qs = q_raw * _sigmoid(q_raw)
        logf, _, kk = _hgrn_gates(fp_ref[...], lbv)
        iv = iv_ref[...]
        causal = _tri(HGRN_CHUNK)
        tri = causal.astype(f32)
        last_row = lax.broadcasted_iota(jnp.int32, (HGRN_CHUNK, 1), 0) == HGRN_CHUNK - 1
        for c in range(nch):
            rows = slice(HGRN_CHUNK * c, HGRN_CHUNK * (c + 1))
            b = _dot_exact(tri, logf[rows])
            bl = jnp.sum(jnp.where(last_row, b, 0.0), axis=0, keepdims=True)
            qb = (qs[rows] * jnp.exp(b)).astype(bf16)
            kb = (kk[rows] * jnp.exp(-b)).astype(bf16)
            kd = (kk[rows] * jnp.exp(bl - b)).astype(bf16)
            ebl = jnp.exp(bl)
            vc = iv[rows].astype(bf16)
            for h in range(HEADS):
                sl = slice(HEAD_DIM * h, HEAD_DIM * (h + 1))
                st = st_scr[h]
                st_ref[c, h] = st
                a = jnp.where(causal, _dot_nt(qb[:, sl], kb[:, sl]), 0.0)
                opre_ref[rows, sl] = _dot(a.astype(bf16), vc[:, sl]) + _dot_nt(qb[:, sl], st.astype(bf16))
                st_scr[h] = st * ebl[:, sl] + _dot_tn(vc[:, sl], kd[:, sl])

        o = opre_ref[...]
        go = go_ref[...]
        gout_v = gout_ref[...]
        for h in range(HEADS):
            sl = slice(HEAD_DIM * h, HEAD_DIM * (h + 1))
            _, oh = _rms_stats(o[:, sl])
            z_ref[:, sl] = (oh * gout_v[:, sl] * _sigmoid(go[:, sl])).astype(bf16)

        zc = cg_ref[...] * xc_ref[...]
        zbuf[8:8 + tt, :] = zc
        y = wconv_ref[0:1, :] * zbuf[pl.ds(6, tt), :] + wconv_ref[1:2, :] * zbuf[pl.ds(7, tt), :] + wconv_ref[2:3, :] * zc
        z_ref[:, BRANCH:2 * BRANCH] = (bg_ref[...] * y).astype(bf16)
        zbuf[0:8, :] = zbuf[tt:tt + 8, :]

        ug = _gelu(u_ref[...])
        vg = _gelu(v_ref[...])
        vcen = vg - jnp.mean(vg, axis=-1, keepdims=True)
        rstd = lax.rsqrt(jnp.mean(vcen * vcen, axis=-1, keepdims=True) + LN_EPS)
        vn = (vcen * rstd * lng_ref[...] + lnb_ref[...]).astype(bf16)
        low = _tri(SG_CHUNK)
        for g in range(SG_GROUPS):
            sl = slice(LANE * g, LANE * (g + 1))
            wm = jnp.where(low, wsg_ref[g], 0.0).astype(bf16)
            bias = bsg_ref[:, g:g + 1]
            for cc in range(tt // SG_CHUNK):
                rows = slice(SG_CHUNK * cc, SG_CHUNK * (cc + 1))
                sv = _dot(wm, vn[rows, sl]) + bias
                z_ref[rows, 2 * BRANCH + LANE * g:2 * BRANCH + LANE * (g + 1)] = (ug[rows, sl] * sv).astype(bf16)

    full = lambda shape: pl.BlockSpec(shape, lambda i: (0,) * len(shape))
    return pl.pallas_call(
        body, name="mixer_fwd", grid=(s // tt,),
        in_specs=_p_specs(tt, range(9), lambda i: i) + [full((1, BRANCH)), full((1, BRANCH)), full((3, BRANCH)), full((1, BRANCH)),
                                                        full((1, BRANCH)), full((SG_GROUPS, SG_CHUNK, SG_CHUNK)), full((SG_CHUNK, SG_GROUPS))],
        out_specs=[pl.BlockSpec((tt, 3 * BRANCH), lambda i: (i, 0)), pl.BlockSpec((tt, BRANCH), lambda i: (i, 0)),
                   pl.BlockSpec((nch, HEADS, HEAD_DIM, HEAD_DIM), lambda i: (i, 0, 0, 0))],
        out_shape=[SDS((s, 3 * BRANCH), bf16), SDS((s, BRANCH), f32), SDS((s // HGRN_CHUNK, HEADS, HEAD_DIM, HEAD_DIM), f32)],
        scratch_shapes=[pltpu.VMEM((HEADS, HEAD_DIM, HEAD_DIM), f32), pltpu.VMEM((tt + 8, BRANCH), f32)],
        compiler_params=_cp("arbitrary"),
    )(*([p] * 9), lb, gout, wconv, lng, lnb, wsg, bsg_t)


def _branch_gate(z, wb, p, x, wo, tm=256):
    s = z.shape[0]
    half = 3 * D_MODEL // 2

    def body(z_ref, wb_ref, ga_ref, gb_ref, x_ref, wo_ref, y_ref, m_ref, x1_ref):
        ga, gb = ga_ref[...], gb_ref[...]
        gates = [ga[:, :D_MODEL], jnp.concatenate([ga[:, D_MODEL:], gb[:, :D_MODEL // 2]], axis=1), gb[:, D_MODEL // 2:]]
        acc = None
        for n in range(3):
            yn = _dot(z_ref[:, BRANCH * n:BRANCH * (n + 1)], wb_ref[n])
            y_ref[:, D_MODEL * n:D_MODEL * (n + 1)] = yn.astype(bf16)
            t = _sigmoid(gates[n]) * yn
            acc = t if acc is None else acc + t
        merged = acc.astype(bf16)
        m_ref[...] = merged
        x1_ref[...] = x_ref[...] + _dot(merged, wo_ref[...])

    blk0 = GATE_COL0 // half
    row = pl.BlockSpec((tm, D_MODEL), lambda i: (i, 0))
    return pl.pallas_call(
        body, name="branch_gate", grid=(s // tm,),
        in_specs=[pl.BlockSpec((tm, 3 * BRANCH), lambda i: (i, 0)), pl.BlockSpec((3, BRANCH, D_MODEL), lambda i: (0, 0, 0)),
                  pl.BlockSpec((tm, half), lambda i: (i, blk0)), pl.BlockSpec((tm, half), lambda i: (i, blk0 + 1)), row,
                  pl.BlockSpec((D_MODEL, D_MODEL), lambda i: (0, 0))],
        out_specs=[pl.BlockSpec((tm, 3 * D_MODEL), lambda i: (i, 0)), row, row],
        out_shape=[SDS((s, 3 * D_MODEL), bf16), SDS((s, D_MODEL), bf16), SDS((s, D_MODEL), f32)], compiler_params=_cp("parallel"),
    )(z, wb, p, p, x, wo)


def _ffn(x1, g, w1, w2, tm=512, tf=1024):
    s = x1.shape[0]
    nf = D_FF // tf

    def body(x_ref, g_ref, w1_ref, w2_ref, o_ref, h_ref, ra_ref, hs, acc):
        f = pl.program_id(1)

        @pl.when(f == 0)
        def _():
            _, xh = _rms_stats(x_ref[...])
            hv = (xh * g_ref[...]).astype(bf16)
            hs[...] = hv
            h_ref[...] = hv
            acc[...] = jnp.zeros_like(acc)

        ra = jnp.maximum(_dot(hs[...], w1_ref[...]), 0.0)
        ra_ref[...] = ra.astype(bf16)
        acc[...] += _dot((ra * ra).astype(bf16), w2_ref[...])

        @pl.when(f == nf - 1)
        def _():
            o_ref[...] = x_ref[...] + acc[...]

    return pl.pallas_call(
        body, name="ffn", grid=(s // tm, nf),
        in_specs=[pl.BlockSpec((tm, D_MODEL), lambda i, f: (i, 0)), pl.BlockSpec((1, D_MODEL), lambda i, f: (0, 0)),
                  pl.BlockSpec((D_MODEL, tf), lambda i, f: (0, f)), pl.BlockSpec((tf, D_MODEL), lambda i, f: (f, 0))],
        out_specs=[pl.BlockSpec((tm, D_MODEL), lambda i, f: (i, 0)), pl.BlockSpec((tm, D_MODEL), lambda i, f: (i, 0)),
                   pl.BlockSpec((tm, tf), lambda i, f: (i, f))],
        out_shape=[SDS((s, D_MODEL), f32), SDS((s, D_MODEL), bf16), SDS((s, D_FF), bf16)],
        scratch_shapes=[pltpu.VMEM((tm, D_MODEL), bf16), pltpu.VMEM((tm, D_MODEL), f32)], compiler_params=_cp("parallel", "arbitrary"),
    )(x1, g, w1, w2)


def _final(x, target, g, tm=512):
    s = x.shape[0]

    def body(x_ref, t_ref, g_ref, loss_ref, dx_ref, dxb_ref, dg_ref):
        first = pl.program_id(0) == 0
        gv = g_ref[...]
        r, xh = _rms_stats(x_ref[...])
        e = xh * gv - t_ref[...]
        tile_loss = 0.5 * jnp.sum(jnp.mean(e * e, axis=-1, keepdims=True), axis=0, keepdims=True)
        dx, dg = _rms_bwd(e * (1.0 / D_MODEL), xh, r, gv)
        dx_ref[...] = dx
        dxb_ref[...] = dx.astype(bf16)
        _acc_rows(dg_ref, first, dg)
        _acc_rows(loss_ref, first, jnp.broadcast_to(tile_loss, (1, LANE)))

    row = pl.BlockSpec((tm, D_MODEL), lambda i: (i, 0))
    return pl.pallas_call(
        body, name="final_loss", grid=(s // tm,), in_specs=[row, row, pl.BlockSpec((1, D_MODEL), lambda i: (0, 0))],
        out_specs=[pl.BlockSpec((1, LANE), lambda i: (0, 0)), row, row, pl.BlockSpec((1, D_MODEL), lambda i: (0, 0))],
        out_shape=[SDS((1, LANE), f32), SDS((s, D_MODEL), f32), SDS((s, D_MODEL), bf16), SDS((1, D_MODEL), f32)],
        compiler_params=_cp("arbitrary"),
    )(x, target, g)


def _ffn_bwd(dx2, dx2b, x1, g, ra, w1, w2, tm=512, tf=1024):
    s = x1.shape[0]
    nf = D_FF // tf

    def body(dx_ref, dxb_ref, x_ref, g_ref, ra_ref, w1_ref, w2_ref, da_ref, dx1_ref, dx1b_ref, dg_ref, acc):
        i, f = pl.program_id(0), pl.program_id(1)

        @pl.when(f == 0)
        def _():
            acc[...] = jnp.zeros_like(acc)

        da = (_dot_nt(dxb_ref[...], w2_ref[...]) * (2.0 * ra_ref[...].astype(f32))).astype(bf16)
        da_ref[...] = da
        acc[...] += _dot_nt(da, w1_ref[...])

        @pl.when(f == nf - 1)
        def _():
            r, xh = _rms_stats(x_ref[...])
            dx, dg = _rms_bwd(acc[...], xh, r, g_ref[...])
            dx = dx + dx_ref[...]
            dx1_ref[...] = dx
            dx1b_ref[...] = dx.astype(bf16)
            _acc_rows(dg_ref, i == 0, dg)

    row = pl.BlockSpec((tm, D_MODEL), lambda i, f: (i, 0))
    col = pl.BlockSpec((tm, tf), lambda i, f: (i, f))
    return pl.pallas_call(
        body, name="ffn_bwd", grid=(s // tm, nf),
        in_specs=[row, row, row, pl.BlockSpec((1, D_MODEL), lambda i, f: (0, 0)), col,
                  pl.BlockSpec((D_MODEL, tf), lambda i, f: (0, f)), pl.BlockSpec((tf, D_MODEL), lambda i, f: (f, 0))],
        out_specs=[col, row, row, pl.BlockSpec((1, D_MODEL), lambda i, f: (0, 0))],
        out_shape=[SDS((s, D_FF), bf16), SDS((s, D_MODEL), f32), SDS((s, D_MODEL), bf16), SDS((1, D_MODEL), f32)],
        scratch_shapes=[pltpu.VMEM((tm, D_MODEL), f32)], compiler_params=_cp("arbitrary", "arbitrary"),
    )(dx2, dx2b, x1, g, ra, w1, w2)


def _mm_tn(a, b, nb, m, n, tm, tn, name="mm_tn", rows=None, row0=0, into=None, square_a=False):
    s = a.shape[0]
    mi, nj = m // tm, n // tn
    rows = m if rows is None else rows
    blk0 = row0 // tm

    def body(a_ref, b_ref, *rest):
        av = a_ref[...]
        if square_a:
            av = av.astype(f32)
            av = (av * av).astype(bf16)
        rest[-1][...] = _dot_tn(av, b_ref[...]).astype(bf16)

    extra = {} if into is None else dict(input_output_aliases={2: 0})
    return pl.pallas_call(
        body, name=name, grid=(nb, mi, nj),
        in_specs=[pl.BlockSpec((s, tm), lambda k, i, j: (0, k * mi + i)), pl.BlockSpec((s, tn), lambda k, i, j: (0, k * nj + j))]
        + ([] if into is None else [pl.BlockSpec(memory_space=pl.ANY)]),
        out_specs=pl.BlockSpec((None, tm, tn), lambda k, i, j: (k, blk0 + i, j)), out_shape=SDS((nb, rows, n), bf16),
        compiler_params=_cp("parallel", "parallel", "parallel"), **extra,
    )(a, b, *([] if into is None else [into]))


def _mm_tn_slabs(a, b, nb, m, nblk, rel, width, tm=512, name="mm_tn_slabs"):
    s = a.shape[0]
    n = b.shape[1] // nb
    ng, mi, nw = n // nblk, m // tm, len(rel)

    def body(a_ref, b_ref, o_ref):
        full = _dot_tn(a_ref[...], b_ref[...])
        for r, start in enumerate(rel):
            o_ref[r] = full[:, start:start + width].astype(bf16)

    return pl.pallas_call(
        body, name=name, grid=(nb, ng, mi),
        in_specs=[pl.BlockSpec((s, tm), lambda k, g, i: (0, k * mi + i)), pl.BlockSpec((s, nblk), lambda k, g, i: (0, k * ng + g))],
        out_specs=pl.BlockSpec((nw, None, tm, width), lambda k, g, i: (g, k, i, 0)), out_shape=SDS((ng * nw, nb, m, width), bf16),
        compiler_params=_cp("parallel", "parallel", "parallel"),
    )(a, b)


def _merge_bwd(dx1b, wo, y, p, wb, after, tm=256):
    s = dx1b.shape[0]
    half = 3 * D_MODEL // 2
    blk0 = GATE_COL0 // half

    def body(dx_ref, wo_ref, y_ref, ga_ref, gb_ref, wb_ref, after_ref, dy_ref, dg_ref, dz_ref):
        del after_ref
        dm = _dot_nt(dx_ref[...], wo_ref[...])
        ga, gb = ga_ref[...], gb_ref[...]
        gates = [ga[:, :D_MODEL], jnp.concatenate([ga[:, D_MODEL:], gb[:, :D_MODEL // 2]], axis=1), gb[:, D_MODEL // 2:]]
        for n in range(3):
            cols = slice(D_MODEL * n, D_MODEL * (n + 1))
            gate = _sigmoid(gates[n])
            t = dm * gate
            dy = t.astype(bf16)
            dy_ref[:, cols] = dy
            dg_ref[:, cols] = (t * y_ref[:, cols].astype(f32) * (1.0 - gate)).astype(bf16)
            dz_ref[:, BRANCH * n:BRANCH * (n + 1)] = _dot_nt(dy, wb_ref[n])

    wide = pl.BlockSpec((tm, 3 * D_MODEL), lambda i: (i, 0))
    return pl.pallas_call(
        body, name="merge_bwd", grid=(s // tm,),
        in_specs=[pl.BlockSpec((tm, D_MODEL), lambda i: (i, 0)), pl.BlockSpec((D_MODEL, D_MODEL), lambda i: (0, 0)), wide,
                  pl.BlockSpec((tm, half), lambda i: (i, blk0)), pl.BlockSpec((tm, half), lambda i: (i, blk0 + 1)),
                  pl.BlockSpec((3, BRANCH, D_MODEL), lambda i: (0, 0, 0)), pl.BlockSpec(memory_space=pl.ANY)],
        out_specs=[wide, wide, pl.BlockSpec((tm, 3 * BRANCH), lambda i: (i, 0))],
        out_shape=[SDS((s, 3 * D_MODEL), bf16), SDS((s, 3 * D_MODEL), bf16), SDS((s, 3 * BRANCH), f32)],
        compiler_params=_cp("parallel"),
    )(dx1b, wo, y, p, p, wb, after)


def _mixer_bwd(p, dz, opre, states, lb, gout, wconv, lng, lnb, wsg, bsg_t):
    s = p.shape[0]
    tt = MIX_TILE
    nt = s // tt
    nch = tt // HGRN_CHUNK
    rev = lambda i: nt - 1 - i

    def body(q_ref, fp_ref, iv_ref, go_ref, bg_ref, cg_ref, xc_ref, u_ref, v_ref, cgp_ref, xcp_ref, dz_ref, opre_ref, st_ref,
             lb_ref, gout_ref, wconv_ref, lng_ref, lnb_ref, wsg_ref, bsg_ref,
             dp_ref, vec_ref, dwsg_ref, dbsg_ref, dst_scr, zbuf, dybuf, dvn_scr, dbsg_acc):
        i = pl.program_id(0)

        @pl.when(i == 0)
        def _():
            dst_scr[...] = jnp.zeros_like(dst_scr)
            dybuf[tt:tt + 8, :] = jnp.zeros((8, BRANCH), f32)
            vec_ref[...] = jnp.zeros_like(vec_ref)
            dwsg_ref[...] = jnp.zeros_like(dwsg_ref)
            dbsg_acc[...] = jnp.zeros_like(dbsg_acc)

        lbv = lb_ref[...]
        q_raw, fp = q_ref[...], fp_ref[...]
        sq = _sigmoid(q_raw)
        qs = q_raw * sq
        sfp = _sigmoid(fp)
        logf, snf, kk = _hgrn_gates(fp, lbv)
        inv_f = jnp.exp(-logf)
        iv = iv_ref[...]
        doa = dz_ref[:, 0:BRANCH]
        o = opre_ref[...]
        sgo = _sigmoid(go_ref[...])
        gout_v = gout_ref[...]
        d_o, dgo, dgout = [], [], []
        for h in range(HEADS):
            sl = slice(HEAD_DIM * h, HEAD_DIM * (h + 1))
            r, oh = _rms_stats(o[:, sl])
            d_on = doa[:, sl] * sgo[:, sl]
            dgo.append(doa[:, sl] * oh * gout_v[:, sl] * sgo[:, sl] * (1.0 - sgo[:, sl]))
            dx, dg = _rms_bwd(d_on, oh, r, gout_v[:, sl])
            d_o.append(dx)
            dgout.append(dg)
        d_o = jnp.concatenate(d_o, axis=1)
        dp_ref[:, 3 * BRANCH:4 * BRANCH] = jnp.concatenate(dgo, axis=1).astype(bf16)
        vec_ref[1:2, :] += jnp.concatenate(dgout, axis=1)

        causal = _tri(HGRN_CHUNK)
        tri = causal.astype(f32)
        tri_up = _tri(HGRN_CHUNK, upper=True).astype(f32)
        last_row = lax.broadcasted_iota(jnp.int32, (HGRN_CHUNK, 1), 0) == HGRN_CHUNK - 1
        lb_live = (lbv > LB_FLOOR).astype(f32)
        dlb = jnp.zeros((1, BRANCH), f32)
        for c in reversed(range(nch)):
            rows = slice(HGRN_CHUNK * c, HGRN_CHUNK * (c + 1))
            b = _dot_exact(tri, logf[rows])
            bl = jnp.sum(jnp.where(last_row, b, 0.0), axis=0, keepdims=True)
            eb, enb, edl, ebl = jnp.exp(b), jnp.exp(-b), jnp.exp(bl - b), jnp.exp(bl)
            qbf, kbf, kdf = qs[rows] * eb, kk[rows] * enb, kk[rows] * edl
            qb, kb, kd = qbf.astype(bf16), kbf.astype(bf16), kdf.astype(bf16)
            vc = iv[rows].astype(bf16)
            dob = d_o[rows].astype(bf16)
            dv, dqb, dkb, dkd, debl = [], [], [], [], []
            for h in range(HEADS):
                sl = slice(HEAD_DIM * h, HEAD_DIM * (h + 1))
                st = st_ref[c, h]
                dst = dst_scr[h]
                stb, dstb = st.astype(bf16), dst.astype(bf16)
                a = jnp.where(causal, _dot_nt(qb[:, sl], kb[:, sl]), 0.0).astype(bf16)
                da = jnp.where(causal, _dot_nt(dob[:, sl], vc[:, sl]), 0.0).astype(bf16)
                dv.append(_dot_tn(a, dob[:, sl]) + _dot_nt(kd[:, sl], dstb))
                dqb.append(_dot(dob[:, sl], stb) + _dot(da, kb[:, sl]))
                dkb.append(_dot_tn(da, qb[:, sl]))
                dkd.append(_dot(vc[:, sl], dstb))
                debl.append(jnp.sum(st * dst, axis=0, keepdims=True))
                dst_scr[h] = _dot_tn(dob[:, sl], qb[:, sl]) + dst * ebl[:, sl]
            dv, dqb, dkb, dkd = (jnp.concatenate(t, axis=1) for t in (dv, dqb, dkb, dkd))
            debl = jnp.concatenate(debl, axis=1)
            t_kd = dkd * kdf
            dbl = ebl * debl + jnp.sum(t_kd, axis=0, keepdims=True)
            db = dqb * qbf - dkb * kbf - t_kd + jnp.where(last_row, dbl, 0.0)
            dkk = dkb * enb + dkd * edl
            dlc = _dot_exact(tri_up, db)
            sq_c, q_c, sfp_c, snf_c, invf_c = sq[rows], q_raw[rows], sfp[rows], snf[rows], inv_f[rows]
            slope = (1.0 - lbv) * sfp_c * snf_c
            dp_ref[rows, 0:BRANCH] = (dqb * eb * sq_c * (1.0 + q_c * (1.0 - sq_c))).astype(bf16)
            dp_ref[rows, BRANCH:2 * BRANCH] = (slope * (dlc * invf_c - dkk)).astype(bf16)
            dp_ref[rows, 2 * BRANCH:3 * BRANCH] = dv.astype(bf16)
            dlb = dlb + jnp.sum(dlc * (lb_live - sfp_c) * invf_c - dkk * snf_c, axis=0, keepdims=True)
        vec_ref[0:1, :] += dlb

        dob_ = dz_ref[:, BRANCH:2 * BRANCH]
        bg, cg, xc = bg_ref[...], cg_ref[...], xc_ref[...]
        zc = cg * xc
        zbuf[0:8, :] = jnp.where(i < nt - 1, cgp_ref[...] * xcp_ref[...], 0.0)
        zbuf[8:8 + tt, :] = zc
        w0, w1, w2 = wconv_ref[0:1, :], wconv_ref[1:2, :], wconv_ref[2:3, :]
        y = w0 * zbuf[pl.ds(6, tt), :] + w1 * zbuf[pl.ds(7, tt), :] + w2 * zc
        dy = dob_ * bg
        dybuf[0:tt, :] = dy
        dy1, dy2 = dybuf[pl.ds(1, tt), :], dybuf[pl.ds(2, tt), :]
        dzc = w2 * dy + w1 * dy1 + w0 * dy2
        dp_ref[:, 4 * BRANCH:5 * BRANCH] = (dob_ * y).astype(bf16)
        dp_ref[:, 5 * BRANCH:6 * BRANCH] = (dzc * xc).astype(bf16)
        dp_ref[:, 6 * BRANCH:7 * BRANCH] = (dzc * cg).astype(bf16)
        vec_ref[4:5, :] += jnp.sum(zc * dy2, axis=0, keepdims=True)
        vec_ref[5:6, :] += jnp.sum(zc * dy1, axis=0, keepdims=True)
        vec_ref[6:7, :] += jnp.sum(zc * dy, axis=0, keepdims=True)
        dybuf[tt:tt + 8, :] = dybuf[0:8, :]

        doc = dz_ref[:, 2 * BRANCH:3 * BRANCH]
        u_raw, v_raw = u_ref[...], v_ref[...]
        ug = _gelu(u_raw)
        dug_scale = _gelu_grad(u_raw)
        vg = _gelu(v_raw)
        vcen = vg - jnp.mean(vg, axis=-1, keepdims=True)
        rstd = lax.rsqrt(jnp.mean(vcen * vcen, axis=-1, keepdims=True) + LN_EPS)
        vhat = vcen * rstd
        lng_v = lng_ref[...]
        vn = (vhat * lng_v + lnb_ref[...]).astype(bf16)
        low = _tri(SG_CHUNK)
        for g in range(SG_GROUPS):
            sl = slice(LANE * g, LANE * (g + 1))
            wm = jnp.where(low, wsg_ref[g], 0.0).astype(bf16)
            bias = bsg_ref[:, g:g + 1]
            dw = jnp.zeros((SG_CHUNK, SG_CHUNK), f32)
            dbs = jnp.zeros((SG_CHUNK, LANE), f32)
            for cc in range(tt // SG_CHUNK):
                rows = slice(SG_CHUNK * cc, SG_CHUNK * (cc + 1))
                vn_c = vn[rows, sl]
                sv = _dot(wm, vn_c) + bias
                doc_c = doc[rows, sl]
                dp_ref[rows, 7 * BRANCH + LANE * g:7 * BRANCH + LANE * (g + 1)] = (doc_c * sv * dug_scale[rows, sl]).astype(bf16)
                dsv = doc_c * ug[rows, sl]
                dsvb = dsv.astype(bf16)
                dbs = dbs + dsv
                dw = dw + _dot_nt(dsvb, vn_c)
                dvn_scr[rows, sl] = _dot_tn(wm, dsvb)
            dwsg_ref[g] += jnp.where(low, dw, 0.0)
            dbsg_acc[:, sl] += dbs
        dvn = dvn_scr[...]
        vec_ref[2:3, :] += jnp.sum(dvn * vhat, axis=0, keepdims=True)
        vec_ref[3:4, :] += jnp.sum(dvn, axis=0, keepdims=True)
        dvh = dvn * lng_v
        dvg = rstd * (dvh - jnp.mean(dvh, axis=-1, keepdims=True) - vhat * jnp.mean(dvh * vhat, axis=-1, keepdims=True))
        dp_ref[:, 8 * BRANCH:9 * BRANCH] = (dvg * _gelu_grad(v_raw)).astype(bf16)

        @pl.when(i == nt - 1)
        def _():
            for g in range(SG_GROUPS):
                dbsg_ref[:, g:g + 1] = jnp.sum(dbsg_acc[:, LANE * g:LANE * (g + 1)], axis=1, keepdims=True)

    full = lambda shape: pl.BlockSpec(shape, lambda i: (0,) * len(shape))
    tail = lambda c: pl.BlockSpec((8, BRANCH), lambda i: (jnp.maximum(rev(i) * (tt // 8) - 1, 0), c))
    return pl.pallas_call(
        body, name="mixer_bwd", grid=(nt,),
        in_specs=_p_specs(tt, range(9), rev) + [tail(5), tail(6), pl.BlockSpec((tt, 3 * BRANCH), lambda i: (rev(i), 0)),
                                                pl.BlockSpec((tt, BRANCH), lambda i: (rev(i), 0)),
                                                pl.BlockSpec((nch, HEADS, HEAD_DIM, HEAD_DIM), lambda i: (rev(i), 0, 0, 0)),
                                                full((1, BRANCH)), full((1, BRANCH)), full((3, BRANCH)), full((1, BRANCH)), full((1, BRANCH)),
                                                full((SG_GROUPS, SG_CHUNK, SG_CHUNK)), full((SG_CHUNK, SG_GROUPS))],
        out_specs=[pl.BlockSpec((tt, 9 * BRANCH), lambda i: (rev(i), 0)), full((8, BRANCH)), full((SG_GROUPS, SG_CHUNK, SG_CHUNK)),
                   full((SG_CHUNK, SG_GROUPS))],
        out_shape=[SDS((s, 9 * BRANCH), bf16), SDS((8, BRANCH), f32), SDS((SG_GROUPS, SG_CHUNK, SG_CHUNK), f32), SDS((SG_CHUNK, SG_GROUPS), f32)],
        scratch_shapes=[pltpu.VMEM((HEADS, HEAD_DIM, HEAD_DIM), f32), pltpu.VMEM((tt + 8, BRANCH), f32), pltpu.VMEM((tt + 8, BRANCH), f32),
                        pltpu.VMEM((tt, BRANCH), f32), pltpu.VMEM((SG_CHUNK, BRANCH), f32)],
        compiler_params=_cp("arbitrary"),
    )(*([p] * 11), dz, opre, states, lb, gout, wconv, lng, lnb, wsg, bsg_t)


def _dh_bwd(dpm, dpg, w_t, x, dx1, g, after, tm=1024, tk=1536):
    s = x.shape[0]
    km = dpm.shape[1] // tk
    nk = km + dpg.shape[1] // tk

    def body(dpm_ref, dpg_ref, w_ref, x_ref, dx1_ref, g_ref, after_ref, dx_ref, dxb_ref, dg_ref, acc):
        del after_ref
        i, k = pl.program_id(0), pl.program_id(1)

        @pl.when(k == 0)
        def _():
            acc[...] = jnp.zeros_like(acc)

        @pl.when(k < km)
        def _():
            acc[...] += _dot(dpm_ref[...], w_ref[...])

        @pl.when(k >= km)
        def _():
            acc[...] += _dot(dpg_ref[...], w_ref[...])

        @pl.when(k == nk - 1)
        def _():
            r, xh = _rms_stats(x_ref[...])
            dx, dg = _rms_bwd(acc[...], xh, r, g_ref[...])
            dx = dx + dx1_ref[...]
            dx_ref[...] = dx
            dxb_ref[...] = dx.astype(bf16)
            _acc_rows(dg_ref, i == 0, dg)

    row = pl.BlockSpec((tm, D_MODEL), lambda i, k: (i, 0))
    vec = pl.BlockSpec((1, D_MODEL), lambda i, k: (0, 0))
    return pl.pallas_call(
        body, name="dh_bwd", grid=(s // tm, nk),
        in_specs=[pl.BlockSpec((tm, tk), lambda i, k: (i, jnp.minimum(k, km - 1))),
                  pl.BlockSpec((tm, tk), lambda i, k: (i, jnp.maximum(k - km, 0))),
                  pl.BlockSpec((tk, D_MODEL), lambda i, k: (k, 0)), row, row, vec, pl.BlockSpec(memory_space=pl.ANY)],
        out_specs=[row, row, vec], out_shape=[SDS((s, D_MODEL), f32), SDS((s, D_MODEL), bf16), SDS((1, D_MODEL), f32)],
        scratch_shapes=[pltpu.VMEM((tm, D_MODEL), f32)], compiler_params=_cp("arbitrary", "arbitrary"),
    )(dpm, dpg, w_t, x, dx1, g, after)


def _layer_fwd(x, w, sm, p_h=None):
    p, h = _rms_mm(x, sm["g_mix"], w["w_in"]) if p_h is None else p_h
    z, opre, states = _mixer_fwd(p, sm["lb"], sm["g_out"], sm["w_conv"], sm["ln_g"], sm["ln_b"], sm["w_sg"], sm["b_sg_t"])
    y, merged, x1 = _branch_gate(z, w["w_branch"], p, x, w["w_o"])
    x2, h2, ra = _ffn(x1, sm["g_ffn"], w["w_ff1"], w["w_ff2"])
    saved = dict(x=x, p=p, h=h, z=z, opre=opre, states=states, y=y, merged=merged, x1=x1, h2=h2, ra=ra)
    return x2, saved


def _layer_bwd(dx2, dx2b, sv, w, sm, between, before_end):
    nchip = N_DEV // 2
    by_chip = lambda g: g.reshape((nchip, 2) + g.shape[1:])
    da, dx1, dx1b, dg_ffn = _ffn_bwd(dx2, dx2b, sv["x1"], sm["g_ffn"], sv["ra"], w["w_ff1"], w["w_ff2"])
    g_ff2 = by_chip(_mm_tn(sv["ra"], dx2b, 1, D_FF, D_MODEL, 512, 1024, name="dw_ff2", square_a=True)[0]
                    .reshape(N_DEV, D_FF // N_DEV, D_MODEL))
    g_ff1 = by_chip(_mm_tn_slabs(sv["h2"], da, 1, D_MODEL, D_FF // 2, [i * (D_FF // N_DEV) for i in range(nchip)], D_FF // N_DEV,
                                 name="dw_ff1")[:, 0])
    g_o = by_chip(_mm_tn(sv["merged"], dx1b, 1, D_MODEL, D_MODEL, 512, 1024, name="dw_o")[0].reshape(N_DEV, D_MODEL // N_DEV, D_MODEL))
    dy, dpg, dz = _merge_bwd(dx1b, w["w_o"], sv["y"], sv["p"], w["w_branch"], between(dx1))
    g_branch = by_chip(_mm_tn_slabs(sv["z"], dy, 3, BRANCH, D_MODEL, [i * (D_MODEL // N_DEV) for i in range(N_DEV)], D_MODEL // N_DEV,
                                    name="dw_branch"))
    g_in = _mm_tn(dpg, sv["h"], 1, 3 * D_MODEL, D_MODEL, 512, 1024, name="dw_in_gates", rows=N_COLS, row0=GATE_COL0)
    dpm, vecs, dwsg, dbsg_t = _mixer_bwd(sv["p"], dz, sv["opre"], sv["states"], sm["lb"], sm["g_out"], sm["w_conv"],
                                         sm["ln_g"], sm["ln_b"], sm["w_sg"], sm["b_sg_t"])
    g_in = _mm_tn(dpm, sv["h"], 1, GATE_COL0, D_MODEL, 512, 1024, name="dw_in_mixers", rows=N_COLS, into=g_in)
    g_in = by_chip(g_in[0].reshape(N_DEV, SHARD_IN, D_MODEL))
    dx, dxb, dg_mix = _dh_bwd(dpm, dpg, w["w_in"], sv["x"], dx1, sm["g_mix"], before_end(g_in))
    big = dict(w_in=g_in, w_branch=g_branch, w_o=g_o, w_ff1=g_ff1, w_ff2=g_ff2)
    small = dict(g_mix=dg_mix, g_ffn=dg_ffn, vecs=vecs, w_sg=dwsg, b_sg_t=dbsg_t, dx1=dx1)
    return dx, dxb, big, small


BIG = ("w_in", "w_branch", "w_o", "w_ff1", "w_ff2")
ANY = pl.BlockSpec(memory_space=pl.ANY)


def _place():
    return lax.axis_index("x"), lax.axis_index("y"), lax.axis_index("c")


def _al(v, m):
    return pl.multiple_of(v * m, m)


def _shard_of(refs, dev, which=range(len(BIG))):
    out = []
    for ref, t in zip(refs, which):
        by_cols = BIG[t] in ("w_branch", "w_ff1")
        n = ref.shape[-1 if by_cols else 0] // N_DEV
        part = pl.ds(_al(dev, n), n)
        out.append(ref.at[(slice(None),) * (len(ref.shape) - 1) + (part,)] if by_cols else ref.at[part])
    return out


def _gather_out_shapes(shards):
    s_in, s_b, s_o, s_1, s_2 = (shards[n] for n in BIG)
    return [SDS((s_in.shape[1] * N_DEV, s_in.shape[2]), bf16), SDS(s_b.shape[1:3] + (s_b.shape[3] * N_DEV,), bf16),
            SDS((s_o.shape[1] * N_DEV, s_o.shape[2]), bf16), SDS((s_1.shape[1], s_1.shape[2] * N_DEV), bf16),
            SDS((s_2.shape[1] * N_DEV, s_2.shape[2]), bf16)]


def _seq_all_gather_layer(layer, which, n_early, shard_refs, out_shapes, tag=""):
    nt = len(which)
    outs = [jax.empty_ref(sh, memory_space=pltpu.MemorySpace.HBM) for sh in out_shapes]
    early, late = tuple(range(n_early)), tuple(range(n_early, nt))

    @pl.kernel(mesh=plsc.ScalarSubcoreMesh(axis_name="seq", num_cores=1), name=f"seq_all_gather_l{layer}{tag}",
               scratch_types=(pltpu.SemaphoreType.DMA((9,)), pltpu.SemaphoreType.DMA((9,))),
               compiler_params=pltpu.CompilerParams(collective_id=1))
    def launch(send_sems, recv_sems):
        x, y, c = _place()
        me, sibling = (x, y, c), (x, y, 1 - c)
        first, second, diag = _ici_route(x, y, c)
        _handshake([sibling, first, second])
        mine = [r.at[layer] for r in shard_refs]

        def copies(k, blk, to, src=None, part=range(nt)):
            dst = _shard_of(outs, 4 * blk[0] + 2 * blk[1] + blk[2], which)
            src = dst if src is None else src
            return [pltpu.make_async_remote_copy(src_ref=src[t], dst_ref=dst[t], send_sem=send_sems.at[k], recv_sem=recv_sems.at[k],
                                                 device_id=to, device_id_type=MESH) for t in part]

        def start(cps):
            for cp in cps:
                cp.start()
            return cps

        def landed(cps):
            for cp in cps:
                cp.wait_recv()

        sent = start(copies(0, me, sibling, src=mine) + copies(1, me, first, src=mine, part=early)
                     + copies(2, me, first, src=mine, part=late) + copies(3, me, second, src=mine))
        landed(copies(1, first, me, part=early))
        sent += start(copies(4, first, second, part=early) + copies(6, first, sibling, part=early))
        landed(copies(2, first, me, part=late))
        sent += start(copies(5, first, second, part=late) + copies(6, first, sibling, part=late))
        landed(copies(3, second, me))
        sent += start(copies(7, second, sibling))
        landed(copies(4, diag, me, part=early) + copies(5, diag, me, part=late))
        sent += start(copies(8, diag, sibling))
        other = lambda p: (p[0], p[1], 1 - c)
        landed(copies(0, sibling, me) + copies(6, other(second), me) + copies(7, other(first), me) + copies(8, other(diag), me))
        for cp in sent:
            cp.wait_send()

    launch()
    return [o[...] for o in outs]


def _ici_route(x, y, c):
    return (x ^ (1 - c), y ^ c, c), (x ^ c, y ^ (1 - c), c), (1 - x, 1 - y, c)


def _place_own(where, which, shards, gathered, after):
    nt = len(which)

    def body(where_ref, *refs):
        del where_ref
        for src, dst in zip(refs[:nt], refs[2 * nt + 1:]):
            dst[...] = src[...]

    in_specs, out_specs = [], []
    for t, sh in zip(which, shards):
        blk = sh.shape[1:]
        in_specs.append(pl.BlockSpec((None,) + blk, functools.partial(lambda nd, i, wh: (wh[0],) + (0,) * nd, len(blk))))
        by_cols = BIG[t] in ("w_branch", "w_ff1")
        out_specs.append(pl.BlockSpec(blk, functools.partial(
            lambda nd, cols, i, wh: (0,) * (nd - 1) + (wh[1],) if cols else (wh[1],) + (0,) * (nd - 1), len(blk), by_cols)))
    return pl.pallas_call(
        body, name="place_own", out_shape=[SDS(g.shape, g.dtype) for g in gathered],
        input_output_aliases={1 + nt + i: i for i in range(nt)}, compiler_params=_cp("arbitrary"),
        grid_spec=pltpu.PrefetchScalarGridSpec(num_scalar_prefetch=1, grid=(1,), in_specs=in_specs + [ANY] * (nt + 1), out_specs=out_specs),
    )(where, *shards, *gathered, after)


def _handshake(peers):
    barrier = pltpu.get_barrier_semaphore()
    for p in peers:
        pl.semaphore_signal(barrier, inc=1, device_id=p, device_id_type=MESH)
    pl.semaphore_wait(barrier, len(peers))


def _seq_exchange_on_chip(grads):
    nt, nchip = len(BIG), N_DEV // 2
    g_refs = [jax.new_ref(g, memory_space=pltpu.MemorySpace.HBM) for g in grads]
    outs = [jax.empty_ref(SDS((nchip,) + g.shape[2:], bf16), memory_space=pltpu.MemorySpace.HBM) for g in grads]

    @pl.kernel(mesh=plsc.ScalarSubcoreMesh(axis_name="seq", num_cores=1), name="seq_rs_on_chip",
               scratch_types=(pltpu.SemaphoreType.DMA((nchip,)), pltpu.SemaphoreType.DMA((nchip,))),
               compiler_params=pltpu.CompilerParams(collective_id=2))
    def launch(send_sems, recv_sems):
        x, y, c = _place()
        sibling = (x, y, 1 - c)
        _handshake([sibling])
        remote = [pltpu.make_async_remote_copy(src_ref=g_refs[t].at[j, 1 - c], dst_ref=outs[t].at[j], send_sem=send_sems.at[j],
                                               recv_sem=recv_sems.at[j], device_id=sibling, device_id_type=MESH)
                  for j in range(nchip) for t in range(nt)]
        for cp in remote:
            cp.start()
        for cp in remote:
            cp.wait_recv()
        for cp in remote:
            cp.wait_send()

    launch()
    return [o[...] for o in outs], [g[...] for g in g_refs]


def _seq_exchange_between_chips(sums):
    nt = len(BIG)
    s_refs = [jax.new_ref(a, memory_space=pltpu.MemorySpace.HBM) for a in sums]
    outs = [jax.empty_ref(SDS((3,) + a.shape[1:], bf16), memory_space=pltpu.MemorySpace.HBM) for a in sums]
    transit = [jax.empty_ref(SDS(a.shape[1:], bf16), memory_space=pltpu.MemorySpace.HBM) for a in sums]

    early, late = (0,), tuple(range(1, nt))

    @pl.kernel(mesh=plsc.ScalarSubcoreMesh(axis_name="seq", num_cores=1), name="seq_rs_between_chips",
               scratch_types=(pltpu.SemaphoreType.DMA((6,)), pltpu.SemaphoreType.DMA((6,))),
               compiler_params=pltpu.CompilerParams(collective_id=3))
    def launch(send_sems, recv_sems):
        x, y, c = _place()
        first, second, diag = _ici_route(x, y, c)
        _handshake([first, second])

        def copies(k, src, dst, to, part=range(nt)):
            return [pltpu.make_async_remote_copy(src_ref=src(t), dst_ref=dst(t), send_sem=send_sems.at[k], recv_sem=recv_sems.at[k],
                                                 device_id=to, device_id_type=MESH) for t in part]

        chip_of = lambda p: 2 * p[0] + p[1]
        for_diag = lambda t: s_refs[t].at[chip_of(diag)]
        through = lambda t: transit[t]
        last = lambda t: outs[t].at[2]
        direct = (copies(0, lambda t: s_refs[t].at[chip_of(first)], lambda t: outs[t].at[0], first)
                  + copies(1, lambda t: s_refs[t].at[chip_of(second)], lambda t: outs[t].at[1], second))
        via = [copies(2, for_diag, through, first, early), copies(3, for_diag, through, first, late)]
        passed = [copies(4, through, last, second, early), copies(5, through, last, second, late)]
        for cp in via[0] + direct + via[1]:
            cp.start()
        for arrived, onward in zip(via, passed):
            for cp in arrived:
                cp.wait_recv()
            for cp in onward:
                cp.start()
        sent = direct + via[0] + via[1] + passed[0] + passed[1]
        for cp in direct + passed[0] + passed[1]:
            cp.wait_recv()
        for cp in sent:
            cp.wait_send()

    launch()
    return [o[...] for o in outs], [a[...] for a in s_refs]


def _chip_sums(core, mine, other, after, steps=2):
    nt, nchip = len(mine), mine[0].shape[0]
    m4 = [a.reshape(nchip, 2, -1, a.shape[-1]) for a in mine]
    o3 = [a.reshape(nchip, -1, a.shape[-1]) for a in other]

    def body(c_ref, *refs):
        del c_ref
        for a_ref, b_ref, o_ref in zip(refs[:nt], refs[nt:2 * nt], refs[2 * nt + 1:]):
            o_ref[...] = (a_ref[...].astype(f32) + b_ref[...].astype(f32)).astype(bf16)

    tiles = [(a.shape[1] // steps, a.shape[2]) for a in o3]
    blks = [pl.BlockSpec((None,) + t, lambda j, i, c_ref: (j, i, 0)) for t in tiles]
    outs = pl.pallas_call(
        body, name="chip_sums", out_shape=[SDS(a.shape, bf16) for a in o3], compiler_params=_cp("parallel", "parallel"),
        grid_spec=pltpu.PrefetchScalarGridSpec(
            num_scalar_prefetch=1, grid=(nchip, steps),
            in_specs=[pl.BlockSpec((None, None) + t, lambda j, i, c_ref: (j, c_ref[0], i, 0)) for t in tiles] + blks + [ANY],
            out_specs=blks),
    )(core, *m4, *o3, after)
    return [o.reshape(a.shape) for o, a in zip(outs, other)]


def _all_reduce_rows(pack):
    rows = pack.shape[0]
    blk = rows // N_DEV

    def body(in_ref, out_ref, land, send1, recv1, send2, recv2):
        x, y, c = _place()
        me = 4 * x + 2 * y + c
        others = [(px, py, pc) for px in range(2) for py in range(2) for pc in range(2)]

        def is_me(p):
            return jnp.logical_and(jnp.logical_and(p[0] == x, p[1] == y), p[2] == c)

        land[me] = in_ref[pl.ds(_al(me, blk), blk), :]
        for d, p in enumerate(others):
            @pl.when(jnp.logical_not(is_me(p)))
            def _():
                pltpu.make_async_remote_copy(src_ref=in_ref.at[pl.ds(d * blk, blk), :], dst_ref=land.at[me], send_sem=send1.at[d],
                                             recv_sem=recv1.at[me], device_id=p, device_id_type=MESH).start()
        for d, p in enumerate(others):
            @pl.when(jnp.logical_not(is_me(p)))
            def _():
                cp = pltpu.make_async_remote_copy(src_ref=in_ref.at[pl.ds(d * blk, blk), :], dst_ref=land.at[d], send_sem=send1.at[d],
                                                  recv_sem=recv1.at[d], device_id=p, device_id_type=MESH)
                cp.wait_recv()
                cp.wait_send()
        total = land[0]
        for d in range(1, N_DEV):
            total = total + land[d]
        out_ref[pl.ds(_al(me, blk), blk), :] = total
        for d, p in enumerate(others):
            @pl.when(jnp.logical_not(is_me(p)))
            def _():
                mine = out_ref.at[pl.ds(_al(me, blk), blk), :]
                pltpu.make_async_remote_copy(src_ref=mine, dst_ref=mine, send_sem=send2.at[d], recv_sem=recv2.at[me],
                                             device_id=p, device_id_type=MESH).start()
        for d, p in enumerate(others):
            @pl.when(jnp.logical_not(is_me(p)))
            def _():
                theirs = out_ref.at[pl.ds(d * blk, blk), :]
                cp = pltpu.make_async_remote_copy(src_ref=theirs, dst_ref=theirs, send_sem=send2.at[d], recv_sem=recv2.at[d],
                                                  device_id=p, device_id_type=MESH)
                cp.wait_recv()
                cp.wait_send()

    vm = pl.BlockSpec(memory_space=pltpu.VMEM)
    return pl.pallas_call(
        body, name="all_reduce_rows", in_specs=[vm], out_specs=vm, out_shape=SDS((rows, LANE), f32),
        scratch_shapes=[pltpu.VMEM((N_DEV, blk, LANE), f32)] + [pltpu.SemaphoreType.DMA((N_DEV,))] * 4,
        compiler_params=pltpu.CompilerParams(vmem_limit_bytes=VMEM_LIMIT),
    )(pack)


def _lower_bounds_fwd(lower):
    def body(l_ref, o_ref):
        sm = _layer_softmax(l_ref)
        run = jnp.zeros_like(sm[0])
        for l in range(DEPTH):
            o_ref[l:l + 1, :] = run
            if l + 1 < DEPTH:
                run = run + sm[l + 1]

    return pl.pallas_call(body, name="lower_bounds_fwd", out_shape=SDS(lower.shape, f32))(lower)


def _layer_softmax(l_ref):
    rows = [l_ref[l:l + 1, :] for l in range(DEPTH)]
    top = functools.reduce(jnp.maximum, rows)
    e = [jnp.exp(r - top) for r in rows]
    tot = functools.reduce(lambda a, b: a + b, e)
    return [v / tot for v in e]


def _lower_bounds_bwd(lower, dlbs):
    def body(l_ref, d_ref, o_ref):
        sm = _layer_softmax(l_ref)
        dsm = [None] * DEPTH
        run = jnp.zeros_like(sm[0])
        dsm[0] = run
        for l in reversed(range(1, DEPTH)):
            run = run + d_ref[l:l + 1, :]
            dsm[l] = run
        inner = functools.reduce(lambda a, b: a + b, [sm[l] * dsm[l] for l in range(DEPTH)])
        for l in range(DEPTH):
            o_ref[l:l + 1, :] = sm[l] * (dsm[l] - inner)

    return pl.pallas_call(body, name="lower_bounds_bwd", out_shape=SDS(lower.shape, f32))(lower, dlbs)


_ADAM_C1 = 1.0 - ADAM_B1 ** ADAM_STEP
_ADAM_C2 = 1.0 - ADAM_B2 ** ADAM_STEP


def _adamw(w, g, m, v):
    m = ADAM_B1 * m + (1.0 - ADAM_B1) * g
    v = ADAM_B2 * v + (1.0 - ADAM_B2) * (g * g)
    delta = -ADAM_LR * ((m / _ADAM_C1) / (jnp.sqrt(v / _ADAM_C2) + ADAM_EPS) + ADAM_WD * w)
    return delta, m, v


def _row_tile(rows, cap):
    return next(t for t in range(min(cap, rows) // 16 * 16, 0, -16) if rows % t == 0)


def _adam_big(where, names, w, m, v, sums, landed, outs, after, steps=4):
    nt = len(names)
    three = lambda a: a.reshape(a.shape[0], -1, a.shape[-1])
    w3, m3, v3 = ([three(d[n]) for n in names] for d in (w, m, v))
    outs3 = [three(a) for n in names for a in outs[n]]
    sums3 = [three(a) for a in sums]
    land3 = [three(a) for a in landed]

    def body(where_ref, *refs):
        del where_ref
        o_refs = refs[5 * nt + 4 * nt + 1:]
        for t in range(nt):
            w_ref, m_ref, v_ref, sum_ref, land_ref = (refs[q * nt + t] for q in range(5))
            g = sum_ref[...].astype(f32)
            for k in range(3):
                g = g + land_ref[k].astype(f32)
            delta, nm, nv = _adamw(w_ref[...], g, m_ref[...], v_ref[...])
            for o_ref, val in zip(o_refs[4 * t:4 * t + 4], (g, delta, nm, nv)):
                o_ref[...] = val

    tiles = [(a.shape[1] // steps, a.shape[2]) for a in w3]
    own = [pl.BlockSpec((None,) + t, lambda i, wh: (wh[0], i, 0)) for t in tiles]
    res = pl.pallas_call(
        body, name="adam_big", out_shape=[SDS(a.shape, f32) for a in outs3],
        input_output_aliases={1 + 5 * nt + i: i for i in range(4 * nt)}, compiler_params=_cp("parallel"),
        grid_spec=pltpu.PrefetchScalarGridSpec(
            num_scalar_prefetch=1, grid=(steps,),
            in_specs=own * 3 + [pl.BlockSpec((None,) + t, lambda i, wh: (wh[1], i, 0)) for t in tiles]
            + [pl.BlockSpec((3,) + t, lambda i, wh: (0, i, 0)) for t in tiles] + [ANY] * (4 * nt + 1),
            out_specs=[s for s in own for _ in range(4)]),
    )(where, *w3, *m3, *v3, *sums3, *land3, *outs3, after)
    return {n: [o.reshape(w[n].shape) for o in res[4 * t:4 * t + 4]] for t, n in enumerate(names)}


def _adam_rows(w, g, m, v):
    def body(w_ref, g_ref, m_ref, v_ref, d_ref, nm_ref, nv_ref):
        delta, nm, nv = _adamw(w_ref[...], g_ref[...], m_ref[...], v_ref[...])
        d_ref[...] = delta
        nm_ref[...] = nm
        nv_ref[...] = nv

    return pl.pallas_call(body, name="adam_rows", out_shape=[SDS(w.shape, f32)] * 3)(w, g, m, v)


SMALL = ("g_mix", "lower_bounds", "g_hgrn_out", "w_conv", "sg_ln_g", "sg_ln_b", "w_sg", "b_sg", "g_ffn", "g_final")
WEIGHTS = ("w_in", "g_mix", "lower_bounds", "g_hgrn_out", "w_conv", "sg_ln_g", "sg_ln_b", "w_sg", "b_sg", "w_branch", "w_o", "g_ffn",
           "w_ff1", "w_ff2", "g_final")


def _pack_rows(arrays, multiple):
    flat = jnp.concatenate([a.reshape(-1) for a in arrays])
    rows = -(-flat.shape[0] // (LANE * multiple)) * multiple
    return jnp.pad(flat, (0, rows * LANE - flat.shape[0])).reshape(rows, LANE)


def _unpack_rows(pack, like):
    flat = pack.reshape(-1)
    out, at = [], 0
    for a in like:
        out.append(flat[at:at + a.size].reshape(a.shape))
        at += a.size
    return out


def kernel(x, w_in, g_mix, lower_bounds, g_hgrn_out, w_conv, sg_ln_g, sg_ln_b, w_sg, b_sg, w_branch, w_o, g_ffn, w_ff1, w_ff2, g_final, loss_target, m_w_in, m_g_mix, m_lower_bounds, m_g_hgrn_out, m_w_conv, m_sg_ln_g, m_sg_ln_b, m_w_sg, m_b_sg, m_w_branch, m_w_o, m_g_ffn, m_w_ff1, m_w_ff2, m_g_final, v_w_in, v_g_mix, v_lower_bounds, v_g_hgrn_out, v_w_conv, v_sg_ln_g, v_sg_ln_b, v_w_sg, v_b_sg, v_w_branch, v_w_o, v_g_ffn, v_w_ff1, v_w_ff2, v_g_final):
    weights = dict(w_in=w_in, g_mix=g_mix, lower_bounds=lower_bounds, g_hgrn_out=g_hgrn_out, w_conv=w_conv, sg_ln_g=sg_ln_g,
                   sg_ln_b=sg_ln_b, w_sg=w_sg, b_sg=b_sg, w_branch=w_branch, w_o=w_o, g_ffn=g_ffn, w_ff1=w_ff1, w_ff2=w_ff2, g_final=g_final)
    mom1 = dict(w_in=m_w_in, g_mix=m_g_mix, lower_bounds=m_lower_bounds, g_hgrn_out=m_g_hgrn_out, w_conv=m_w_conv, sg_ln_g=m_sg_ln_g,
                sg_ln_b=m_sg_ln_b, w_sg=m_w_sg, b_sg=m_b_sg, w_branch=m_w_branch, w_o=m_w_o, g_ffn=m_g_ffn, w_ff1=m_w_ff1, w_ff2=m_w_ff2,
                g_final=m_g_final)
    mom2 = dict(w_in=v_w_in, g_mix=v_g_mix, lower_bounds=v_lower_bounds, g_hgrn_out=v_g_hgrn_out, w_conv=v_w_conv, sg_ln_g=v_sg_ln_g,
                sg_ln_b=v_sg_ln_b, w_sg=v_w_sg, b_sg=v_b_sg, w_branch=v_w_branch, w_o=v_w_o, g_ffn=v_g_ffn, w_ff1=v_w_ff1, w_ff2=v_w_ff2,
                g_final=v_g_final)
    xi, yi, ci = _place()
    dev = 4 * xi + 2 * yi + ci
    conv_cols = w_conv.shape[-1]

    for d in (weights, mom1, mom2):
        d["w_in"] = jnp.swapaxes(d["w_in"], 1, 2)
    shards = {n: weights[n].astype(bf16) for n in BIG}

    conv_place = lax.dynamic_update_slice(jnp.zeros((DEPTH, 3, BRANCH), f32), w_conv, (0, 0, dev * conv_cols))
    (w_conv_full,) = _unpack_rows(_all_reduce_rows(_pack_rows([conv_place], 8 * N_DEV)), [conv_place])
    lbs = _lower_bounds_fwd(lower_bounds)

    def small_of(l):
        return dict(g_mix=g_mix[l][None], lb=lbs[l][None], g_out=g_hgrn_out[l][None], w_conv=w_conv_full[l], ln_g=sg_ln_g[l][None],
                    ln_b=sg_ln_b[l][None], w_sg=w_sg[l], b_sg_t=b_sg[l].T, g_ffn=g_ffn[l][None])

    act = x[0]
    full, saved = [], []
    shard_refs = [jax.new_ref(shards[n], memory_space=pltpu.MemorySpace.HBM) for n in BIG]
    shapes = _gather_out_shapes(shards)
    every = tuple(range(len(BIG)))
    rest = (3, 4, 1, 2)
    groups = [((0,), 1, "a"), (rest, 1, "b")] + [(every, 1, "")] * (DEPTH - 1)
    gathered = [_seq_all_gather_layer(max(i - 1, 0), which, n_early, [shard_refs[t] for t in which], [shapes[t] for t in which], tag)
                for i, (which, n_early, tag) in enumerate(groups)]

    def placed_weights(l, which, arrived, after):
        where = jnp.stack([jnp.int32(l), dev.astype(jnp.int32)])
        return dict(zip([BIG[t] for t in which], _place_own(where, which, [shards[BIG[t]] for t in which], arrived, after)))

    for l in range(DEPTH):
        if l == 0:
            w_l = placed_weights(0, (0,), gathered[0], act)
            p_h = _rms_mm(act, small_of(0)["g_mix"], w_l["w_in"])
            w_l.update(placed_weights(0, rest, gathered[1], p_h[0]))
        else:
            w_l, p_h = placed_weights(l, every, gathered[l + 1], act), None
        full.append(w_l)
        act, sv = _layer_fwd(act, w_l, small_of(l), p_h)
        saved.append(sv)
    loss_row, dx, dxb, dg_final = _final(act, loss_target[0], g_final[None])
    loss = lax.psum(loss_row[0, 0], ("x", "y", "c"))

    core = ci.astype(jnp.int32)[None]
    big_out = {n: [lax.empty(weights[n].shape, f32) for _ in range(4)] for n in BIG}
    small_grads = [None] * DEPTH

    def chip_sums(stage, after):
        l, received, mine = stage
        sums = _chip_sums(core, mine, received, after)
        placed.append(sums[BIG.index("w_o")])
        landed, sums = _seq_exchange_between_chips(sums)
        return l, sums, landed

    def adam_layer(stage, after):
        l, sums, landed = stage
        where = jnp.stack([jnp.int32(l), (2 * xi + yi).astype(jnp.int32)])
        big_out.update(_adam_big(where, BIG, weights, mom1, mom2, sums, landed, big_out, after))

    above = None
    placed = []
    for l in reversed(range(DEPTH)):
        summed = []

        def between(dx1):
            if above is None:
                return dx1
            summed.append(chip_sums(above, dx1))
            return placed[-1]

        def before_end(g_in):
            if not summed:
                return g_in
            adam_layer(summed[0], g_in)
            return big_out["w_o"][0]

        dx, dxb, big, small_grads[l] = _layer_bwd(dx, dxb, saved[l], full[l], small_of(l), between, before_end)
        above = (l, *_seq_exchange_on_chip([big[n] for n in BIG]))

    stack = lambda f: jnp.stack([f(small_grads[l]) for l in range(DEPTH)])
    d_lower = _lower_bounds_bwd(lower_bounds, stack(lambda s: s["vecs"][0]))
    local_small = dict(g_mix=stack(lambda s: s["g_mix"][0]), lower_bounds=d_lower, g_hgrn_out=stack(lambda s: s["vecs"][1]),
                       w_conv=stack(lambda s: s["vecs"][4:7]), sg_ln_g=stack(lambda s: s["vecs"][2]), sg_ln_b=stack(lambda s: s["vecs"][3]),
                       w_sg=stack(lambda s: s["w_sg"]), b_sg=stack(lambda s: s["b_sg_t"].T), g_ffn=stack(lambda s: s["g_ffn"][0]),
                       g_final=dg_final[0])
    order = [local_small[n] for n in SMALL]
    grads = dict(zip(SMALL, _unpack_rows(_all_reduce_rows(_pack_rows(order, 8 * N_DEV)), order)))
    grads["w_conv"] = lax.dynamic_slice(grads["w_conv"], (0, 0, dev * conv_cols), (DEPTH, 3, conv_cols))

    deltas, new_m, new_v = {}, {}, {}
    packs = [_pack_rows([d[n] for n in SMALL], 8) for d in (weights, grads, mom1, mom2)]
    like = [weights[n] for n in SMALL]
    small_out = _adam_rows(*packs)
    for out, pack in zip((deltas, new_m, new_v), small_out):
        out.update(zip(SMALL, _unpack_rows(pack, like)))
    adam_layer(chip_sums(above, dx), small_out[0])
    for n in BIG:
        grads[n], deltas[n], new_m[n], new_v[n] = (jnp.swapaxes(a, 1, 2) if n == "w_in" else a for a in big_out[n])

    return (loss, dx[None], *[grads[n] for n in WEIGHTS], *[deltas[n] for n in WEIGHTS], *[new_m[n] for n in WEIGHTS],
            *[new_v[n] for n in WEIGHTS])
```

```python
import functools

import jax
import jax.numpy as jnp
from jax import lax
from jax.experimental import pallas as pl
from jax.experimental.pallas import tpu as pltpu
from jax.experimental.pallas import tpu_sc as plsc

f32 = jnp.float32
bf16 = jnp.bfloat16
SDS = jax.ShapeDtypeStruct
MESH = pl.DeviceIdType.MESH

D_MODEL = 1024
BRANCH = 512
N_COLS = 7680
D_FF = 4096
DEPTH = 4
HEADS = 4
HEAD_DIM = 128
HGRN_CHUNK = 64
SG_CHUNK = 128
SG_GROUPS = 4
NORM_EPS = 1e-6
LN_EPS = 1e-5
LB_FLOOR = 1e-30
N_DEV = 8
SHARD_IN = N_COLS // N_DEV
WIN = 1024
LANE = 128
GATE_COL0 = 9 * BRANCH

ADAM_LR = 0.001
ADAM_B1 = 0.9
ADAM_B2 = 0.999
ADAM_EPS = 1e-08
ADAM_WD = 0.01
ADAM_STEP = 10

MIX_TILE = 256
VMEM_LIMIT = 56 * 1024 * 1024


def _cp(*sem):
    return pltpu.CompilerParams(dimension_semantics=sem or None, vmem_limit_bytes=VMEM_LIMIT)


def _dot(a, b):
    return jnp.dot(a, b, preferred_element_type=f32)


def _dot_nt(a, b):
    return lax.dot_general(a, b, (((1,), (1,)), ((), ())), preferred_element_type=f32)


def _dot_tn(a, b):
    return lax.dot_general(a, b, (((0,), (0,)), ((), ())), preferred_element_type=f32)


def _dot_exact(a, b):
    return jnp.dot(a, b, precision=lax.Precision.HIGHEST, preferred_element_type=f32)


def _sigmoid(x):
    return jax.nn.sigmoid(x)


_GELU_C = 0.7978845608028654
_GELU_A = 0.044715


def _gelu(x):
    return 0.5 * x * (1.0 + jnp.tanh(_GELU_C * (x + _GELU_A * x * x * x)))


def _gelu_grad(x):
    x2 = x * x
    t = jnp.tanh(_GELU_C * (x + _GELU_A * x * x2))
    return 0.5 * (1.0 + t) + 0.5 * x * (1.0 - t * t) * _GELU_C * (1.0 + 3.0 * _GELU_A * x2)


def _rms_stats(x):
    r = lax.rsqrt(jnp.mean(x * x, axis=-1, keepdims=True) + NORM_EPS)
    return r, x * r


def _rms_bwd(dh, xh, r, g):
    dg = jnp.sum(dh * xh, axis=0, keepdims=True)
    dxn = dh * g
    dx = r * (dxn - xh * jnp.mean(dxn * xh, axis=-1, keepdims=True))
    return dx, dg


def _tri(n, upper=False):
    r = lax.broadcasted_iota(jnp.int32, (n, n), 0)
    c = lax.broadcasted_iota(jnp.int32, (n, n), 1)
    return (c >= r) if upper else (c <= r)


def _acc_rows(ref, first, val):
    @pl.when(first)
    def _():
        ref[...] = val

    @pl.when(jnp.logical_not(first))
    def _():
        ref[...] += val


def _rms_mm(x, g, w_t, tm=1024, tn=1920):
    s, n = x.shape[0], w_t.shape[0]

    def body(x_ref, g_ref, w_ref, p_ref, h_ref, hs):
        @pl.when(pl.program_id(1) == 0)
        def _():
            _, xh = _rms_stats(x_ref[...])
            hv = (xh * g_ref[...]).astype(bf16)
            hs[...] = hv
            h_ref[...] = hv

        p_ref[...] = _dot_nt(hs[...], w_ref[...])

    return pl.pallas_call(
        body, name="rms_mm", grid=(s // tm, n // tn),
        in_specs=[pl.BlockSpec((tm, D_MODEL), lambda i, j: (i, 0)), pl.BlockSpec((1, D_MODEL), lambda i, j: (0, 0)),
                  pl.BlockSpec((tn, D_MODEL), lambda i, j: (j, 0))],
        out_specs=[pl.BlockSpec((tm, tn), lambda i, j: (i, j)), pl.BlockSpec((tm, D_MODEL), lambda i, j: (i, 0))],
        out_shape=[SDS((s, n), f32), SDS((s, D_MODEL), bf16)],
        scratch_shapes=[pltpu.VMEM((tm, D_MODEL), bf16)], compiler_params=_cp("parallel", "arbitrary"),
    )(x, g, w_t)


def _hgrn_gates(fp, lb):
    logf = jnp.logaddexp(jnp.log(jnp.maximum(lb, LB_FLOOR)), jnp.log1p(-lb) + jax.nn.log_sigmoid(fp))
    snf = _sigmoid(-fp)
    return logf, snf, (1.0 - lb) * snf


def _p_specs(tile, cols, row_map):
    return [pl.BlockSpec((tile, BRANCH), functools.partial(lambda c, i: (row_map(i), c), c)) for c in cols]


def _mixer_fwd(p, lb, gout, wconv, lng, lnb, wsg, bsg_t):
    s = p.shape[0]
    tt = MIX_TILE
    nch = tt // HGRN_CHUNK

    def body(q_ref, fp_ref, iv_ref, go_ref, bg_ref, cg_ref, xc_ref, u_ref, v_ref, lb_ref, gout_ref, wconv_ref, lng_ref,
             lnb_ref, wsg_ref, bsg_ref, z_ref, opre_ref, st_ref, st_scr, zbuf):
        @pl.when(pl.program_id(0) == 0)
        def _():
            st_scr[...] = jnp.zeros_like(st_scr)
            zbuf[0:8, :] = jnp.zeros((8, BRANCH), f32)

        lbv = lb_ref[...]
        q_raw = q_ref[...]
        qs = q_raw * _sigmoid(q_raw)
        logf, _, kk = _hgrn_gates(fp_ref[...], lbv)
        iv = iv_ref[...]
        causal = _tri(HGRN_CHUNK)
        tri = causal.astype(f32)
        last_row = lax.broadcasted_iota(jnp.int32, (HGRN_CHUNK, 1), 0) == HGRN_CHUNK - 1
        for c in range(nch):
            rows = slice(HGRN_CHUNK * c, HGRN_CHUNK * (c + 1))
            b = _dot_exact(tri, logf[rows])
            bl = jnp.sum(jnp.where(last_row, b, 0.0), axis=0, keepdims=True)
            qb = (qs[rows] * jnp.exp(b)).astype(bf16)
            kb = (kk[rows] * jnp.exp(-b)).astype(bf16)
            kd = (kk[rows] * jnp.exp(bl - b)).astype(bf16)
            ebl = jnp.exp(bl)
            vc = iv[rows].astype(bf16)
            for h in range(HEADS):
                sl = slice(HEAD_DIM * h, HEAD_DIM * (h + 1))
                st = st_scr[h]
                st_ref[c, h] = st
                a = jnp.where(causal, _dot_nt(qb[:, sl], kb[:, sl]), 0.0)
                opre_ref[rows, sl] = _dot(a.astype(bf16), vc[:, sl]) + _dot_nt(qb[:, sl], st.astype(bf16))
                st_scr[h] = st * ebl[:, sl] + _dot_tn(vc[:, sl], kd[:, sl])

        o = opre_ref[...]
        go = go_ref[...]
        gout_v = gout_ref[...]
        for h in range(HEADS):
            sl = slice(HEAD_DIM * h, HEAD_DIM * (h + 1))
            _, oh = _rms_stats(o[:, sl])
            z_ref[:, sl] = (oh * gout_v[:, sl] * _sigmoid(go[:, sl])).astype(bf16)

        zc = cg_ref[...] * xc_ref[...]
        zbuf[8:8 + tt, :] = zc
        y = wconv_ref[0:1, :] * zbuf[pl.ds(6, tt), :] + wconv_ref[1:2, :] * zbuf[pl.ds(7, tt), :] + wconv_ref[2:3, :] * zc
        z_ref[:, BRANCH:2 * BRANCH] = (bg_ref[...] * y).astype(bf16)
        zbuf[0:8, :] = zbuf[tt:tt + 8, :]

        ug = _gelu(u_ref[...])
        vg = _gelu(v_ref[...])
        vcen = vg - jnp.mean(vg, axis=-1, keepdims=True)
        rstd = lax.rsqrt(jnp.mean(vcen * vcen, axis=-1, keepdims=True) + LN_EPS)
        vn = (vcen * rstd * lng_ref[...] + lnb_ref[...]).astype(bf16)
        low = _tri(SG_CHUNK)
        for g in range(SG_GROUPS):
            sl = slice(LANE * g, LANE * (g + 1))
            wm = jnp.where(low, wsg_ref[g], 0.0).astype(bf16)
            bias = bsg_ref[:, g:g + 1]
            for cc in range(tt // SG_CHUNK):
                rows = slice(SG_CHUNK * cc, SG_CHUNK * (cc + 1))
                sv = _dot(wm, vn[rows, sl]) + bias
                z_ref[rows, 2 * BRANCH + LANE * g:2 * BRANCH + LANE * (g + 1)] = (ug[rows, sl] * sv).astype(bf16)

    full = lambda shape: pl.BlockSpec(shape, lambda i: (0,) * len(shape))
    return pl.pallas_call(
        body, name="mixer_fwd", grid=(s // tt,),
        in_specs=_p_specs(tt, range(9), lambda i: i) + [full((1, BRANCH)), full((1, BRANCH)), full((3, BRANCH)), full((1, BRANCH)),
                                                        full((1, BRANCH)), full((SG_GROUPS, SG_CHUNK, SG_CHUNK)), full((SG_CHUNK, SG_GROUPS))],
        out_specs=[pl.BlockSpec((tt, 3 * BRANCH), lambda i: (i, 0)), pl.BlockSpec((tt, BRANCH), lambda i: (i, 0)),
                   pl.BlockSpec((nch, HEADS, HEAD_DIM, HEAD_DIM), lambda i: (i, 0, 0, 0))],
        out_shape=[SDS((s, 3 * BRANCH), bf16), SDS((s, BRANCH), f32), SDS((s // HGRN_CHUNK, HEADS, HEAD_DIM, HEAD_DIM), f32)],
        scratch_shapes=[pltpu.VMEM((HEADS, HEAD_DIM, HEAD_DIM), f32), pltpu.VMEM((tt + 8, BRANCH), f32)],
        compiler_params=_cp("arbitrary"),
    )(*([p] * 9), lb, gout, wconv, lng, lnb, wsg, bsg_t)


def _branch_gate(z, wb, p, x, wo, tm=256):
    s = z.shape[0]
    half = 3 * D_MODEL // 2

    def body(z_ref, wb_ref, ga_ref, gb_ref, x_ref, wo_ref, y_ref, m_ref, x1_ref):
        ga, gb = ga_ref[...], gb_ref[...]
        gates = [ga[:, :D_MODEL], jnp.concatenate([ga[:, D_MODEL:], gb[:, :D_MODEL // 2]], axis=1), gb[:, D_MODEL // 2:]]
        acc = None
        for n in range(3):
            yn = _dot(z_ref[:, BRANCH * n:BRANCH * (n + 1)], wb_ref[n])
            y_ref[:, D_MODEL * n:D_MODEL * (n + 1)] = yn.astype(bf16)
            t = _sigmoid(gates[n]) * yn
            acc = t if acc is None else acc + t
        merged = acc.astype(bf16)
        m_ref[...] = merged
        x1_ref[...] = x_ref[...] + _dot(merged, wo_ref[...])

    blk0 = GATE_COL0 // half
    row = pl.BlockSpec((tm, D_MODEL), lambda i: (i, 0))
    return pl.pallas_call(
        body, name="branch_gate", grid=(s // tm,),
        in_specs=[pl.BlockSpec((tm, 3 * BRANCH), lambda i: (i, 0)), pl.BlockSpec((3, BRANCH, D_MODEL), lambda i: (0, 0, 0)),
                  pl.BlockSpec((tm, half), lambda i: (i, blk0)), pl.BlockSpec((tm, half), lambda i: (i, blk0 + 1)), row,
                  pl.BlockSpec((D_MODEL, D_MODEL), lambda i: (0, 0))],
        out_specs=[pl.BlockSpec((tm, 3 * D_MODEL), lambda i: (i, 0)), row, row],
        out_shape=[SDS((s, 3 * D_MODEL), bf16), SDS((s, D_MODEL), bf16), SDS((s, D_MODEL), f32)], compiler_params=_cp("parallel"),
    )(z, wb, p, p, x, wo)


def _ffn(x1, g, w1, w2, tm=512, tf=1024):
    s = x1.shape[0]
    nf = D_FF // tf

    def body(x_ref, g_ref, w1_ref, w2_ref, o_ref, h_ref, ra_ref, hs, acc):
        f = pl.program_id(1)

        @pl.when(f == 0)
        def _():
            _, xh = _rms_stats(x_ref[...])
            hv = (xh * g_ref[...]).astype(bf16)
            hs[...] = hv
            h_ref[...] = hv
            acc[...] = jnp.zeros_like(acc)

        ra = jnp.maximum(_dot(hs[...], w1_ref[...]), 0.0)
        ra_ref[...] = ra.astype(bf16)
        acc[...] += _dot((ra * ra).astype(bf16), w2_ref[...])

        @pl.when(f == nf - 1)
        def _():
            o_ref[...] = x_ref[...] + acc[...]

    return pl.pallas_call(
        body, name="ffn", grid=(s // tm, nf),
        in_specs=[pl.BlockSpec((tm, D_MODEL), lambda i, f: (i, 0)), pl.BlockSpec((1, D_MODEL), lambda i, f: (0, 0)),
                  pl.BlockSpec((D_MODEL, tf), lambda i, f: (0, f)), pl.BlockSpec((tf, D_MODEL), lambda i, f: (f, 0))],
        out_specs=[pl.BlockSpec((tm, D_MODEL), lambda i, f: (i, 0)), pl.BlockSpec((tm, D_MODEL), lambda i, f: (i, 0)),
                   pl.BlockSpec((tm, tf), lambda i, f: (i, f))],
        out_shape=[SDS((s, D_MODEL), f32), SDS((s, D_MODEL), bf16), SDS((s, D_FF), bf16)],
        scratch_shapes=[pltpu.VMEM((tm, D_MODEL), bf16), pltpu.VMEM((tm, D_MODEL), f32)], compiler_params=_cp("parallel", "arbitrary"),
    )(x1, g, w1, w2)


def _final(x, target, g, tm=512):
    s = x.shape[0]

    def body(x_ref, t_ref, g_ref, loss_ref, dx_ref, dxb_ref, dg_ref):
        first = pl.program_id(0) == 0
        gv = g_ref[...]
        r, xh = _rms_stats(x_ref[...])
        e = xh * gv - t_ref[...]
        tile_loss = 0.5 * jnp.sum(jnp.mean(e * e, axis=-1, keepdims=True), axis=0, keepdims=True)
        dx, dg = _rms_bwd(e * (1.0 / D_MODEL), xh, r, gv)
        dx_ref[...] = dx
        dxb_ref[...] = dx.astype(bf16)
        _acc_rows(dg_ref, first, dg)
        _acc_rows(loss_ref, first, jnp.broadcast_to(tile_loss, (1, LANE)))

    row = pl.BlockSpec((tm, D_MODEL), lambda i: (i, 0))
    return pl.pallas_call(
        body, name="final_loss", grid=(s // tm,), in_specs=[row, row, pl.BlockSpec((1, D_MODEL), lambda i: (0, 0))],
        out_specs=[pl.BlockSpec((1, LANE), lambda i: (0, 0)), row, row, pl.BlockSpec((1, D_MODEL), lambda i: (0, 0))],
        out_shape=[SDS((1, LANE), f32), SDS((s, D_MODEL), f32), SDS((s, D_MODEL), bf16), SDS((1, D_MODEL), f32)],
        compiler_params=_cp("arbitrary"),
    )(x, target, g)


def _ffn_bwd(dx2, dx2b, x1, g, ra, w1, w2, tm=512, tf=1024):
    s = x1.shape[0]
    nf = D_FF // tf

    def body(dx_ref, dxb_ref, x_ref, g_ref, ra_ref, w1_ref, w2_ref, da_ref, dx1_ref, dx1b_ref, dg_ref, acc):
        i, f = pl.program_id(0), pl.program_id(1)

        @pl.when(f == 0)
        def _():
            acc[...] = jnp.zeros_like(acc)

        da = (_dot_nt(dxb_ref[...], w2_ref[...]) * (2.0 * ra_ref[...].astype(f32))).astype(bf16)
        da_ref[...] = da
        acc[...] += _dot_nt(da, w1_ref[...])

        @pl.when(f == nf - 1)
        def _():
            r, xh = _rms_stats(x_ref[...])
            dx, dg = _rms_bwd(acc[...], xh, r, g_ref[...])
            dx = dx + dx_ref[...]
            dx1_ref[...] = dx
            dx1b_ref[...] = dx.astype(bf16)
            _acc_rows(dg_ref, i == 0, dg)

    row = pl.BlockSpec((tm, D_MODEL), lambda i, f: (i, 0))
    col = pl.BlockSpec((tm, tf), lambda i, f: (i, f))
    return pl.pallas_call(
        body, name="ffn_bwd", grid=(s // tm, nf),
        in_specs=[row, row, row, pl.BlockSpec((1, D_MODEL), lambda i, f: (0, 0)), col,
                  pl.BlockSpec((D_MODEL, tf), lambda i, f: (0, f)), pl.BlockSpec((tf, D_MODEL), lambda i, f: (f, 0))],
        out_specs=[col, row, row, pl.BlockSpec((1, D_MODEL), lambda i, f: (0, 0))],
        out_shape=[SDS((s, D_FF), bf16), SDS((s, D_MODEL), f32), SDS((s, D_MODEL), bf16), SDS((1, D_MODEL), f32)],
        scratch_shapes=[pltpu.VMEM((tm, D_MODEL), f32)], compiler_params=_cp("arbitrary", "arbitrary"),
    )(dx2, dx2b, x1, g, ra, w1, w2)


def _mm_tn(a, b, nb, m, n, tm, tn, name="mm_tn", rows=None, row0=0, into=None, square_a=False):
    s = a.shape[0]
    mi, nj = m // tm, n // tn
    rows = m if rows is None else rows
    blk0 = row0 // tm

    def body(a_ref, b_ref, *rest):
        av = a_ref[...]
        if square_a:
            av = av.astype(f32)
            av = (av * av).astype(bf16)
        rest[-1][...] = _dot_tn(av, b_ref[...]).astype(bf16)

    extra = {} if into is None else dict(input_output_aliases={2: 0})
    return pl.pallas_call(
        body, name=name, grid=(nb, mi, nj),
        in_specs=[pl.BlockSpec((s, tm), lambda k, i, j: (0, k * mi + i)), pl.BlockSpec((s, tn), lambda k, i, j: (0, k * nj + j))]
        + ([] if into is None else [pl.BlockSpec(memory_space=pl.ANY)]),
        out_specs=pl.BlockSpec((None, tm, tn), lambda k, i, j: (k, blk0 + i, j)), out_shape=SDS((nb, rows, n), bf16),
        compiler_params=_cp("parallel", "parallel", "parallel"), **extra,
    )(a, b, *([] if into is None else [into]))


def _mm_tn_slabs(a, b, nb, m, nblk, rel, width, tm=512, name="mm_tn_slabs"):
    s = a.shape[0]
    n = b.shape[1] // nb
    ng, mi, nw = n // nblk, m // tm, len(rel)

    def body(a_ref, b_ref, o_ref):
        full = _dot_tn(a_ref[...], b_ref[...])
        for r, start in enumerate(rel):
            o_ref[r] = full[:, start:start + width].astype(bf16)

    return pl.pallas_call(
        body, name=name, grid=(nb, ng, mi),
        in_specs=[pl.BlockSpec((s, tm), lambda k, g, i: (0, k * mi + i)), pl.BlockSpec((s, nblk), lambda k, g, i: (0, k * ng + g))],
        out_specs=pl.BlockSpec((nw, None, tm, width), lambda k, g, i: (g, k, i, 0)), out_shape=SDS((ng * nw, nb, m, width), bf16),
        compiler_params=_cp("parallel", "parallel", "parallel"),
    )(a, b)


def _merge_bwd(dx1b, wo, y, p, wb, after, tm=256):
    s = dx1b.shape[0]
    half = 3 * D_MODEL // 2
    blk0 = GATE_COL0 // half

    def body(dx_ref, wo_ref, y_ref, ga_ref, gb_ref, wb_ref, after_ref, dy_ref, dg_ref, dz_ref):
        del after_ref
        dm = _dot_nt(dx_ref[...], wo_ref[...])
        ga, gb = ga_ref[...], gb_ref[...]
        gates = [ga[:, :D_MODEL], jnp.concatenate([ga[:, D_MODEL:], gb[:, :D_MODEL // 2]], axis=1), gb[:, D_MODEL // 2:]]
        for n in range(3):
            cols = slice(D_MODEL * n, D_MODEL * (n + 1))
            gate = _sigmoid(gates[n])
            t = dm * gate
            dy = t.astype(bf16)
            dy_ref[:, cols] = dy
            dg_ref[:, cols] = (t * y_ref[:, cols].astype(f32) * (1.0 - gate)).astype(bf16)
            dz_ref[:, BRANCH * n:BRANCH * (n + 1)] = _dot_nt(dy, wb_ref[n])

    wide = pl.BlockSpec((tm, 3 * D_MODEL), lambda i: (i, 0))
    return pl.pallas_call(
        body, name="merge_bwd", grid=(s // tm,),
        in_specs=[pl.BlockSpec((tm, D_MODEL), lambda i: (i, 0)), pl.BlockSpec((D_MODEL, D_MODEL), lambda i: (0, 0)), wide,
                  pl.BlockSpec((tm, half), lambda i: (i, blk0)), pl.BlockSpec((tm, half), lambda i: (i, blk0 + 1)),
                  pl.BlockSpec((3, BRANCH, D_MODEL), lambda i: (0, 0, 0)), pl.BlockSpec(memory_space=pl.ANY)],
        out_specs=[wide, wide, pl.BlockSpec((tm, 3 * BRANCH), lambda i: (i, 0))],
        out_shape=[SDS((s, 3 * D_MODEL), bf16), SDS((s, 3 * D_MODEL), bf16), SDS((s, 3 * BRANCH), f32)],
        compiler_params=_cp("parallel"),
    )(dx1b, wo, y, p, p, wb, after)


def _mixer_bwd(p, dz, opre, states, lb, gout, wconv, lng, lnb, wsg, bsg_t):
    s = p.shape[0]
    tt = MIX_TILE
    nt = s // tt
    nch = tt // HGRN_CHUNK
    rev = lambda i: nt - 1 - i

    def body(q_ref, fp_ref, iv_ref, go_ref, bg_ref, cg_ref, xc_ref, u_ref, v_ref, cgp_ref, xcp_ref, dz_ref, opre_ref, st_ref,
             lb_ref, gout_ref, wconv_ref, lng_ref, lnb_ref, wsg_ref, bsg_ref,
             dp_ref, vec_ref, dwsg_ref, dbsg_ref, dst_scr, zbuf, dybuf, dvn_scr, dbsg_acc):
        i = pl.program_id(0)

        @pl.when(i == 0)
        def _():
            dst_scr[...] = jnp.zeros_like(dst_scr)
            dybuf[tt:tt + 8, :] = jnp.zeros((8, BRANCH), f32)
            vec_ref[...] = jnp.zeros_like(vec_ref)
            dwsg_ref[...] = jnp.zeros_like(dwsg_ref)
            dbsg_acc[...] = jnp.zeros_like(dbsg_acc)

        lbv = lb_ref[...]
        q_raw, fp = q_ref[...], fp_ref[...]
        sq = _sigmoid(q_raw)
        qs = q_raw * sq
        sfp = _sigmoid(fp)
        logf, snf, kk = _hgrn_gates(fp, lbv)
        inv_f = jnp.exp(-logf)
        iv = iv_ref[...]
        doa = dz_ref[:, 0:BRANCH]
        o = opre_ref[...]
        sgo = _sigmoid(go_ref[...])
        gout_v = gout_ref[...]
        d_o, dgo, dgout = [], [], []
        for h in range(HEADS):
            sl = slice(HEAD_DIM * h, HEAD_DIM * (h + 1))
            r, oh = _rms_stats(o[:, sl])
            d_on = doa[:, sl] * sgo[:, sl]
            dgo.append(doa[:, sl] * oh * gout_v[:, sl] * sgo[:, sl] * (1.0 - sgo[:, sl]))
            dx, dg = _rms_bwd(d_on, oh, r, gout_v[:, sl])
            d_o.append(dx)
            dgout.append(dg)
        d_o = jnp.concatenate(d_o, axis=1)
        dp_ref[:, 3 * BRANCH:4 * BRANCH] = jnp.concatenate(dgo, axis=1).astype(bf16)
        vec_ref[1:2, :] += jnp.concatenate(dgout, axis=1)

        causal = _tri(HGRN_CHUNK)
        tri = causal.astype(f32)
        tri_up = _tri(HGRN_CHUNK, upper=True).astype(f32)
        last_row = lax.broadcasted_iota(jnp.int32, (HGRN_CHUNK, 1), 0) == HGRN_CHUNK - 1
        lb_live = (lbv > LB_FLOOR).astype(f32)
        dlb = jnp.zeros((1, BRANCH), f32)
        for c in reversed(range(nch)):
            rows = slice(HGRN_CHUNK * c, HGRN_CHUNK * (c + 1))
            b = _dot_exact(tri, logf[rows])
            bl = jnp.sum(jnp.where(last_row, b, 0.0), axis=0, keepdims=True)
            eb, enb, edl, ebl = jnp.exp(b), jnp.exp(-b), jnp.exp(bl - b), jnp.exp(bl)
            qbf, kbf, kdf = qs[rows] * eb, kk[rows] * enb, kk[rows] * edl
            qb, kb, kd = qbf.astype(bf16), kbf.astype(bf16), kdf.astype(bf16)
            vc = iv[rows].astype(bf16)
            dob = d_o[rows].astype(bf16)
            dv, dqb, dkb, dkd, debl = [], [], [], [], []
            for h in range(HEADS):
                sl = slice(HEAD_DIM * h, HEAD_DIM * (h + 1))
                st = st_ref[c, h]
                dst = dst_scr[h]
                stb, dstb = st.astype(bf16), dst.astype(bf16)
                a = jnp.where(causal, _dot_nt(qb[:, sl], kb[:, sl]), 0.0).astype(bf16)
                da = jnp.where(causal, _dot_nt(dob[:, sl], vc[:, sl]), 0.0).astype(bf16)
                dv.append(_dot_tn(a, dob[:, sl]) + _dot_nt(kd[:, sl], dstb))
                dqb.append(_dot(dob[:, sl], stb) + _dot(da, kb[:, sl]))
                dkb.append(_dot_tn(da, qb[:, sl]))
                dkd.append(_dot(vc[:, sl], dstb))
                debl.append(jnp.sum(st * dst, axis=0, keepdims=True))
                dst_scr[h] = _dot_tn(dob[:, sl], qb[:, sl]) + dst * ebl[:, sl]
            dv, dqb, dkb, dkd = (jnp.concatenate(t, axis=1) for t in (dv, dqb, dkb, dkd))
            debl = jnp.concatenate(debl, axis=1)
            t_kd = dkd * kdf
            dbl = ebl * debl + jnp.sum(t_kd, axis=0, keepdims=True)
            db = dqb * qbf - dkb * kbf - t_kd + jnp.where(last_row, dbl, 0.0)
            dkk = dkb * enb + dkd * edl
            dlc = _dot_exact(tri_up, db)
            sq_c, q_c, sfp_c, snf_c, invf_c = sq[rows], q_raw[rows], sfp[rows], snf[rows], inv_f[rows]
            slope = (1.0 - lbv) * sfp_c * snf_c
            dp_ref[rows, 0:BRANCH] = (dqb * eb * sq_c * (1.0 + q_c * (1.0 - sq_c))).astype(bf16)
            dp_ref[rows, BRANCH:2 * BRANCH] = (slope * (dlc * invf_c - dkk)).astype(bf16)
            dp_ref[rows, 2 * BRANCH:3 * BRANCH] = dv.astype(bf16)
            dlb = dlb + jnp.sum(dlc * (lb_live - sfp_c) * invf_c - dkk * snf_c, axis=0, keepdims=True)
        vec_ref[0:1, :] += dlb

        dob_ = dz_ref[:, BRANCH:2 * BRANCH]
        bg, cg, xc = bg_ref[...], cg_ref[...], xc_ref[...]
        zc = cg * xc
        zbuf[0:8, :] = jnp.where(i < nt - 1, cgp_ref[...] * xcp_ref[...], 0.0)
        zbuf[8:8 + tt, :] = zc
        w0, w1, w2 = wconv_ref[0:1, :], wconv_ref[1:2, :], wconv_ref[2:3, :]
        y = w0 * zbuf[pl.ds(6, tt), :] + w1 * zbuf[pl.ds(7, tt), :] + w2 * zc
        dy = dob_ * bg
        dybuf[0:tt, :] = dy
        dy1, dy2 = dybuf[pl.ds(1, tt), :], dybuf[pl.ds(2, tt), :]
        dzc = w2 * dy + w1 * dy1 + w0 * dy2
        dp_ref[:, 4 * BRANCH:5 * BRANCH] = (dob_ * y).astype(bf16)
        dp_ref[:, 5 * BRANCH:6 * BRANCH] = (dzc * xc).astype(bf16)
        dp_ref[:, 6 * BRANCH:7 * BRANCH] = (dzc * cg).astype(bf16)
        vec_ref[4:5, :] += jnp.sum(zc * dy2, axis=0, keepdims=True)
        vec_ref[5:6, :] += jnp.sum(zc * dy1, axis=0, keepdims=True)
        vec_ref[6:7, :] += jnp.sum(zc * dy, axis=0, keepdims=True)
        dybuf[tt:tt + 8, :] = dybuf[0:8, :]

        doc = dz_ref[:, 2 * BRANCH:3 * BRANCH]
        u_raw, v_raw = u_ref[...], v_ref[...]
        ug = _gelu(u_raw)
        dug_scale = _gelu_grad(u_raw)
        vg = _gelu(v_raw)
        vcen = vg - jnp.mean(vg, axis=-1, keepdims=True)
        rstd = lax.rsqrt(jnp.mean(vcen * vcen, axis=-1, keepdims=True) + LN_EPS)
        vhat = vcen * rstd
        lng_v = lng_ref[...]
        vn = (vhat * lng_v + lnb_ref[...]).astype(bf16)
        low = _tri(SG_CHUNK)
        for g in range(SG_GROUPS):
            sl = slice(LANE * g, LANE * (g + 1))
            wm = jnp.where(low, wsg_ref[g], 0.0).astype(bf16)
            bias = bsg_ref[:, g:g + 1]
            dw = jnp.zeros((SG_CHUNK, SG_CHUNK), f32)
            dbs = jnp.zeros((SG_CHUNK, LANE), f32)
            for cc in range(tt // SG_CHUNK):
                rows = slice(SG_CHUNK * cc, SG_CHUNK * (cc + 1))
                vn_c = vn[rows, sl]
                sv = _dot(wm, vn_c) + bias
                doc_c = doc[rows, sl]
                dp_ref[rows, 7 * BRANCH + LANE * g:7 * BRANCH + LANE * (g + 1)] = (doc_c * sv * dug_scale[rows, sl]).astype(bf16)
                dsv = doc_c * ug[rows, sl]
                dsvb = dsv.astype(bf16)
                dbs = dbs + dsv
                dw = dw + _dot_nt(dsvb, vn_c)
                dvn_scr[rows, sl] = _dot_tn(wm, dsvb)
            dwsg_ref[g] += jnp.where(low, dw, 0.0)
            dbsg_acc[:, sl] += dbs
        dvn = dvn_scr[...]
        vec_ref[2:3, :] += jnp.sum(dvn * vhat, axis=0, keepdims=True)
        vec_ref[3:4, :] += jnp.sum(dvn, axis=0, keepdims=True)
        dvh = dvn * lng_v
        dvg = rstd * (dvh - jnp.mean(dvh, axis=-1, keepdims=True) - vhat * jnp.mean(dvh * vhat, axis=-1, keepdims=True))
        dp_ref[:, 8 * BRANCH:9 * BRANCH] = (dvg * _gelu_grad(v_raw)).astype(bf16)

        @pl.when(i == nt - 1)
        def _():
            for g in range(SG_GROUPS):
                dbsg_ref[:, g:g + 1] = jnp.sum(dbsg_acc[:, LANE * g:LANE * (g + 1)], axis=1, keepdims=True)

    full = lambda shape: pl.BlockSpec(shape, lambda i: (0,) * len(shape))
    tail = lambda c: pl.BlockSpec((8, BRANCH), lambda i: (jnp.maximum(rev(i) * (tt // 8) - 1, 0), c))
    return pl.pallas_call(
        body, name="mixer_bwd", grid=(nt,),
        in_specs=_p_specs(tt, range(9), rev) + [tail(5), tail(6), pl.BlockSpec((tt, 3 * BRANCH), lambda i: (rev(i), 0)),
                                                pl.BlockSpec((tt, BRANCH), lambda i: (rev(i), 0)),
                                                pl.BlockSpec((nch, HEADS, HEAD_DIM, HEAD_DIM), lambda i: (rev(i), 0, 0, 0)),
                                                full((1, BRANCH)), full((1, BRANCH)), full((3, BRANCH)), full((1, BRANCH)), full((1, BRANCH)),
                                                full((SG_GROUPS, SG_CHUNK, SG_CHUNK)), full((SG_CHUNK, SG_GROUPS))],
        out_specs=[pl.BlockSpec((tt, 9 * BRANCH), lambda i: (rev(i), 0)), full((8, BRANCH)), full((SG_GROUPS, SG_CHUNK, SG_CHUNK)),
                   full((SG_CHUNK, SG_GROUPS))],
        out_shape=[SDS((s, 9 * BRANCH), bf16), SDS((8, BRANCH), f32), SDS((SG_GROUPS, SG_CHUNK, SG_CHUNK), f32), SDS((SG_CHUNK, SG_GROUPS), f32)],
        scratch_shapes=[pltpu.VMEM((HEADS, HEAD_DIM, HEAD_DIM), f32), pltpu.VMEM((tt + 8, BRANCH), f32), pltpu.VMEM((tt + 8, BRANCH), f32),
                        pltpu.VMEM((tt, BRANCH), f32), pltpu.VMEM((SG_CHUNK, BRANCH), f32)],
        compiler_params=_cp("arbitrary"),
    )(*([p] * 11), dz, opre, states, lb, gout, wconv, lng, lnb, wsg, bsg_t)


def _dh_bwd(dpm, dpg, w_t, x, dx1, g, after, tm=1024, tk=1536):
    s = x.shape[0]
    km = dpm.shape[1] // tk
    nk = km + dpg.shape[1] // tk

    def body(dpm_ref, dpg_ref, w_ref, x_ref, dx1_ref, g_ref, after_ref, dx_ref, dxb_ref, dg_ref, acc):
        del after_ref
        i, k = pl.program_id(0), pl.program_id(1)

        @pl.when(k == 0)
        def _():
            acc[...] = jnp.zeros_like(acc)

        @pl.when(k < km)
        def _():
            acc[...] += _dot(dpm_ref[...], w_ref[...])

        @pl.when(k >= km)
        def _():
            acc[...] += _dot(dpg_ref[...], w_ref[...])

        @pl.when(k == nk - 1)
        def _():
            r, xh = _rms_stats(x_ref[...])
            dx, dg = _rms_bwd(acc[...], xh, r, g_ref[...])
            dx = dx + dx1_ref[...]
            dx_ref[...] = dx
            dxb_ref[...] = dx.astype(bf16)
            _acc_rows(dg_ref, i == 0, dg)

    row = pl.BlockSpec((tm, D_MODEL), lambda i, k: (i, 0))
    vec = pl.BlockSpec((1, D_MODEL), lambda i, k: (0, 0))
    return pl.pallas_call(
        body, name="dh_bwd", grid=(s // tm, nk),
        in_specs=[pl.BlockSpec((tm, tk), lambda i, k: (i, jnp.minimum(k, km - 1))),
                  pl.BlockSpec((tm, tk), lambda i, k: (i, jnp.maximum(k - km, 0))),
                  pl.BlockSpec((tk, D_MODEL), lambda i, k: (k, 0)), row, row, vec, pl.BlockSpec(memory_space=pl.ANY)],
        out_specs=[row, row, vec], out_shape=[SDS((s, D_MODEL), f32), SDS((s, D_MODEL), bf16), SDS((1, D_MODEL), f32)],
        scratch_shapes=[pltpu.VMEM((tm, D_MODEL), f32)], compiler_params=_cp("arbitrary", "arbitrary"),
    )(dpm, dpg, w_t, x, dx1, g, after)


def _layer_fwd(x, w, sm, p_h=None):
    p, h = _rms_mm(x, sm["g_mix"], w["w_in"]) if p_h is None else p_h
    z, opre, states = _mixer_fwd(p, sm["lb"], sm["g_out"], sm["w_conv"], sm["ln_g"], sm["ln_b"], sm["w_sg"], sm["b_sg_t"])
    y, merged, x1 = _branch_gate(z, w["w_branch"], p, x, w["w_o"])
    x2, h2, ra = _ffn(x1, sm["g_ffn"], w["w_ff1"], w["w_ff2"])
    saved = dict(x=x, p=p, h=h, z=z, opre=opre, states=states, y=y, merged=merged, x1=x1, h2=h2, ra=ra)
    return x2, saved


def _layer_bwd(dx2, dx2b, sv, w, sm, between, before_end):
    nchip = N_DEV // 2
    by_chip = lambda g: g.reshape((nchip, 2) + g.shape[1:])
    da, dx1, dx1b, dg_ffn = _ffn_bwd(dx2, dx2b, sv["x1"], sm["g_ffn"], sv["ra"], w["w_ff1"], w["w_ff2"])
    g_ff2 = by_chip(_mm_tn(sv["ra"], dx2b, 1, D_FF, D_MODEL, 512, 1024, name="dw_ff2", square_a=True)[0]
                    .reshape(N_DEV, D_FF // N_DEV, D_MODEL))
    g_ff1 = by_chip(_mm_tn_slabs(sv["h2"], da, 1, D_MODEL, D_FF // 2, [i * (D_FF // N_DEV) for i in range(nchip)], D_FF // N_DEV,
                                 name="dw_ff1")[:, 0])
    g_o = by_chip(_mm_tn(sv["merged"], dx1b, 1, D_MODEL, D_MODEL, 512, 1024, name="dw_o")[0].reshape(N_DEV, D_MODEL // N_DEV, D_MODEL))
    dy, dpg, dz = _merge_bwd(dx1b, w["w_o"], sv["y"], sv["p"], w["w_branch"], between(dx1))
    g_branch = by_chip(_mm_tn_slabs(sv["z"], dy, 3, BRANCH, D_MODEL, [i * (D_MODEL // N_DEV) for i in range(N_DEV)], D_MODEL // N_DEV,
                                    name="dw_branch"))
    g_in = _mm_tn(dpg, sv["h"], 1, 3 * D_MODEL, D_MODEL, 512, 1024, name="dw_in_gates", rows=N_COLS, row0=GATE_COL0)
    dpm, vecs, dwsg, dbsg_t = _mixer_bwd(sv["p"], dz, sv["opre"], sv["states"], sm["lb"], sm["g_out"], sm["w_conv"],
                                         sm["ln_g"], sm["ln_b"], sm["w_sg"], sm["b_sg_t"])
    g_in = _mm_tn(dpm, sv["h"], 1, GATE_COL0, D_MODEL, 512, 1024, name="dw_in_mixers", rows=N_COLS, into=g_in)
    g_in = by_chip(g_in[0].reshape(N_DEV, SHARD_IN, D_MODEL))
    big = dict(w_in=g_in, w_branch=g_branch, w_o=g_o, w_ff1=g_ff1, w_ff2=g_ff2)
    dx, dxb, dg_mix = _dh_bwd(dpm, dpg, w["w_in"], sv["x"], dx1, sm["g_mix"], before_end(big))
    small = dict(g_mix=dg_mix, g_ffn=dg_ffn, vecs=vecs, w_sg=dwsg, b_sg_t=dbsg_t, dx1=dx1)
    return dx, dxb, big, small


BIG = ("w_in", "w_branch", "w_o", "w_ff1", "w_ff2")
ANY = pl.BlockSpec(memory_space=pl.ANY)


def _place():
    return lax.axis_index("x"), lax.axis_index("y"), lax.axis_index("c")


def _al(v, m):
    return pl.multiple_of(v * m, m)


def _shard_of(refs, dev, which=range(len(BIG))):
    out = []
    for ref, t in zip(refs, which):
        by_cols = BIG[t] in ("w_branch", "w_ff1")
        n = ref.shape[-1 if by_cols else 0] // N_DEV
        part = pl.ds(_al(dev, n), n)
        out.append(ref.at[(slice(None),) * (len(ref.shape) - 1) + (part,)] if by_cols else ref.at[part])
    return out


def _gather_out_shapes(shards):
    s_in, s_b, s_o, s_1, s_2 = (shards[n] for n in BIG)
    return [SDS((s_in.shape[1] * N_DEV, s_in.shape[2]), bf16), SDS(s_b.shape[1:3] + (s_b.shape[3] * N_DEV,), bf16),
            SDS((s_o.shape[1] * N_DEV, s_o.shape[2]), bf16), SDS((s_1.shape[1], s_1.shape[2] * N_DEV), bf16),
            SDS((s_2.shape[1] * N_DEV, s_2.shape[2]), bf16)]


def _seq_all_gather_layer(layer, which, n_early, shard_refs, out_shapes, tag=""):
    nt = len(which)
    outs = [jax.empty_ref(sh, memory_space=pltpu.MemorySpace.HBM) for sh in out_shapes]
    early, late = tuple(range(n_early)), tuple(range(n_early, nt))

    @pl.kernel(mesh=plsc.ScalarSubcoreMesh(axis_name="seq", num_cores=1), name=f"seq_all_gather_l{layer}{tag}",
               scratch_types=(pltpu.SemaphoreType.DMA((9,)), pltpu.SemaphoreType.DMA((9,))),
               compiler_params=pltpu.CompilerParams(collective_id=1))
    def launch(send_sems, recv_sems):
        x, y, c = _place()
        me, sibling = (x, y, c), (x, y, 1 - c)
        first, second, diag = _ici_route(x, y, c)
        _handshake([sibling, first, second])
        mine = [r.at[layer] for r in shard_refs]

        def copies(k, blk, to, src=None, part=range(nt)):
            dst = _shard_of(outs, 4 * blk[0] + 2 * blk[1] + blk[2], which)
            src = dst if src is None else src
            return [pltpu.make_async_remote_copy(src_ref=src[t], dst_ref=dst[t], send_sem=send_sems.at[k], recv_sem=recv_sems.at[k],
                                                 device_id=to, device_id_type=MESH) for t in part]

        def start(cps):
            for cp in cps:
                cp.start()
            return cps

        def landed(cps):
            for cp in cps:
                cp.wait_recv()

        sent = start(copies(0, me, sibling, src=mine) + copies(1, me, first, src=mine, part=early)
                     + copies(2, me, first, src=mine, part=late) + copies(3, me, second, src=mine))
        landed(copies(1, first, me, part=early))
        sent += start(copies(4, first, second, part=early) + copies(6, first, sibling, part=early))
        landed(copies(2, first, me, part=late))
        sent += start(copies(5, first, second, part=late) + copies(6, first, sibling, part=late))
        landed(copies(3, second, me))
        sent += start(copies(7, second, sibling))
        landed(copies(4, diag, me, part=early) + copies(5, diag, me, part=late))
        sent += start(copies(8, diag, sibling))
        other = lambda p: (p[0], p[1], 1 - c)
        landed(copies(0, sibling, me) + copies(6, other(second), me) + copies(7, other(first), me) + copies(8, other(diag), me))
        for cp in sent:
            cp.wait_send()

    launch()
    return [o[...] for o in outs]


def _ici_route(x, y, c):
    return (x ^ (1 - c), y ^ c, c), (x ^ c, y ^ (1 - c), c), (1 - x, 1 - y, c)


def _place_own(where, which, shards, gathered, after):
    nt = len(which)

    def body(where_ref, *refs):
        del where_ref
        for src, dst in zip(refs[:nt], refs[2 * nt + 1:]):
            dst[...] = src[...]

    in_specs, out_specs = [], []
    for t, sh in zip(which, shards):
        blk = sh.shape[1:]
        in_specs.append(pl.BlockSpec((None,) + blk, functools.partial(lambda nd, i, wh: (wh[0],) + (0,) * nd, len(blk))))
        by_cols = BIG[t] in ("w_branch", "w_ff1")
        out_specs.append(pl.BlockSpec(blk, functools.partial(
            lambda nd, cols, i, wh: (0,) * (nd - 1) + (wh[1],) if cols else (wh[1],) + (0,) * (nd - 1), len(blk), by_cols)))
    return pl.pallas_call(
        body, name="place_own", out_shape=[SDS(g.shape, g.dtype) for g in gathered],
        input_output_aliases={1 + nt + i: i for i in range(nt)}, compiler_params=_cp("arbitrary"),
        grid_spec=pltpu.PrefetchScalarGridSpec(num_scalar_prefetch=1, grid=(1,), in_specs=in_specs + [ANY] * (nt + 1), out_specs=out_specs),
    )(where, *shards, *gathered, after)


def _handshake(peers):
    barrier = pltpu.get_barrier_semaphore()
    for p in peers:
        pl.semaphore_signal(barrier, inc=1, device_id=p, device_id_type=MESH)
    pl.semaphore_wait(barrier, len(peers))


def _seq_exchange_on_chip(grads):
    nt, nchip = len(BIG), N_DEV // 2
    g_refs = [jax.new_ref(g, memory_space=pltpu.MemorySpace.HBM) for g in grads]
    outs = [jax.empty_ref(SDS((nchip,) + g.shape[2:], bf16), memory_space=pltpu.MemorySpace.HBM) for g in grads]

    @pl.kernel(mesh=plsc.ScalarSubcoreMesh(axis_name="seq", num_cores=1), name="seq_rs_on_chip",
               scratch_types=(pltpu.SemaphoreType.DMA((nchip,)), pltpu.SemaphoreType.DMA((nchip,))),
               compiler_params=pltpu.CompilerParams(collective_id=2))
    def launch(send_sems, recv_sems):
        x, y, c = _place()
        sibling = (x, y, 1 - c)
        _handshake([sibling])
        remote = [pltpu.make_async_remote_copy(src_ref=g_refs[t].at[j, 1 - c], dst_ref=outs[t].at[j], send_sem=send_sems.at[j],
                                               recv_sem=recv_sems.at[j], device_id=sibling, device_id_type=MESH)
                  for j in range(nchip) for t in range(nt)]
        for cp in remote:
            cp.start()
        for cp in remote:
            cp.wait_recv()
        for cp in remote:
            cp.wait_send()

    launch()
    return [o[...] for o in outs], [g[...] for g in g_refs]


def _seq_exchange_between_chips(sums):
    nt = len(BIG)
    s_refs = [jax.new_ref(a, memory_space=pltpu.MemorySpace.HBM) for a in sums]
    outs = [jax.empty_ref(SDS((3,) + a.shape[1:], bf16), memory_space=pltpu.MemorySpace.HBM) for a in sums]
    transit = [jax.empty_ref(SDS(a.shape[1:], bf16), memory_space=pltpu.MemorySpace.HBM) for a in sums]

    early, late = (0,), tuple(range(1, nt))

    @pl.kernel(mesh=plsc.ScalarSubcoreMesh(axis_name="seq", num_cores=1), name="seq_rs_between_chips",
               scratch_types=(pltpu.SemaphoreType.DMA((6,)), pltpu.SemaphoreType.DMA((6,))),
               compiler_params=pltpu.CompilerParams(collective_id=3))
    def launch(send_sems, recv_sems):
        x, y, c = _place()
        first, second, diag = _ici_route(x, y, c)
        _handshake([first, second])

        def copies(k, src, dst, to, part=range(nt)):
            return [pltpu.make_async_remote_copy(src_ref=src(t), dst_ref=dst(t), send_sem=send_sems.at[k], recv_sem=recv_sems.at[k],
                                                 device_id=to, device_id_type=MESH) for t in part]

        chip_of = lambda p: 2 * p[0] + p[1]
        for_diag = lambda t: s_refs[t].at[chip_of(diag)]
        through = lambda t: transit[t]
        last = lambda t: outs[t].at[2]
        direct = (copies(0, lambda t: s_refs[t].at[chip_of(first)], lambda t: outs[t].at[0], first)
                  + copies(1, lambda t: s_refs[t].at[chip_of(second)], lambda t: outs[t].at[1], second))
        via = [copies(2, for_diag, through, first, early), copies(3, for_diag, through, first, late)]
        passed = [copies(4, through, last, second, early), copies(5, through, last, second, late)]
        for cp in via[0] + direct + via[1]:
            cp.start()
        for arrived, onward in zip(via, passed):
            for cp in arrived:
                cp.wait_recv()
            for cp in onward:
                cp.start()
        sent = direct + via[0] + via[1] + passed[0] + passed[1]
        for cp in direct + passed[0] + passed[1]:
            cp.wait_recv()
        for cp in sent:
            cp.wait_send()

    launch()
    return [o[...] for o in outs], [a[...] for a in s_refs]


def _chip_sums(core, mine, other, after, steps=2):
    nt, nchip = len(mine), mine[0].shape[0]
    m4 = [a.reshape(nchip, 2, -1, a.shape[-1]) for a in mine]
    o3 = [a.reshape(nchip, -1, a.shape[-1]) for a in other]

    def body(c_ref, *refs):
        del c_ref
        for a_ref, b_ref, o_ref in zip(refs[:nt], refs[nt:2 * nt], refs[2 * nt + 1:]):
            o_ref[...] = (a_ref[...].astype(f32) + b_ref[...].astype(f32)).astype(bf16)

    tiles = [(a.shape[1] // steps, a.shape[2]) for a in o3]
    blks = [pl.BlockSpec((None,) + t, lambda j, i, c_ref: (j, i, 0)) for t in tiles]
    outs = pl.pallas_call(
        body, name="chip_sums", out_shape=[SDS(a.shape, bf16) for a in o3], compiler_params=_cp("parallel", "parallel"),
        grid_spec=pltpu.PrefetchScalarGridSpec(
            num_scalar_prefetch=1, grid=(nchip, steps),
            in_specs=[pl.BlockSpec((None, None) + t, lambda j, i, c_ref: (j, c_ref[0], i, 0)) for t in tiles] + blks + [ANY],
            out_specs=blks),
    )(core, *m4, *o3, after)
    return [o.reshape(a.shape) for o, a in zip(outs, other)]


def _all_reduce_rows(pack):
    rows = pack.shape[0]
    blk = rows // N_DEV

    def body(in_ref, out_ref, land, send1, recv1, send2, recv2):
        x, y, c = _place()
        me = 4 * x + 2 * y + c
        others = [(px, py, pc) for px in range(2) for py in range(2) for pc in range(2)]

        def is_me(p):
            return jnp.logical_and(jnp.logical_and(p[0] == x, p[1] == y), p[2] == c)

        land[me] = in_ref[pl.ds(_al(me, blk), blk), :]
        for d, p in enumerate(others):
            @pl.when(jnp.logical_not(is_me(p)))
            def _():
                pltpu.make_async_remote_copy(src_ref=in_ref.at[pl.ds(d * blk, blk), :], dst_ref=land.at[me], send_sem=send1.at[d],
                                             recv_sem=recv1.at[me], device_id=p, device_id_type=MESH).start()
        for d, p in enumerate(others):
            @pl.when(jnp.logical_not(is_me(p)))
            def _():
                cp = pltpu.make_async_remote_copy(src_ref=in_ref.at[pl.ds(d * blk, blk), :], dst_ref=land.at[d], send_sem=send1.at[d],
                                                  recv_sem=recv1.at[d], device_id=p, device_id_type=MESH)
                cp.wait_recv()
                cp.wait_send()
        total = land[0]
        for d in range(1, N_DEV):
            total = total + land[d]
        out_ref[pl.ds(_al(me, blk), blk), :] = total
        for d, p in enumerate(others):
            @pl.when(jnp.logical_not(is_me(p)))
            def _():
                mine = out_ref.at[pl.ds(_al(me, blk), blk), :]
                pltpu.make_async_remote_copy(src_ref=mine, dst_ref=mine, send_sem=send2.at[d], recv_sem=recv2.at[me],
                                             device_id=p, device_id_type=MESH).start()
        for d, p in enumerate(others):
            @pl.when(jnp.logical_not(is_me(p)))
            def _():
                theirs = out_ref.at[pl.ds(d * blk, blk), :]
                cp = pltpu.make_async_remote_copy(src_ref=theirs, dst_ref=theirs, send_sem=send2.at[d], recv_sem=recv2.at[d],
                                                  device_id=p, device_id_type=MESH)
                cp.wait_recv()
                cp.wait_send()

    vm = pl.BlockSpec(memory_space=pltpu.VMEM)
    return pl.pallas_call(
        body, name="all_reduce_rows", in_specs=[vm], out_specs=vm, out_shape=SDS((rows, LANE), f32),
        scratch_shapes=[pltpu.VMEM((N_DEV, blk, LANE), f32)] + [pltpu.SemaphoreType.DMA((N_DEV,))] * 4,
        compiler_params=pltpu.CompilerParams(vmem_limit_bytes=VMEM_LIMIT),
    )(pack)


def _lower_bounds_fwd(lower):
    def body(l_ref, o_ref):
        sm = _layer_softmax(l_ref)
        run = jnp.zeros_like(sm[0])
        for l in range(DEPTH):
            o_ref[l:l + 1, :] = run
            if l + 1 < DEPTH:
                run = run + sm[l + 1]

    return pl.pallas_call(body, name="lower_bounds_fwd", out_shape=SDS(lower.shape, f32))(lower)


def _layer_softmax(l_ref):
    rows = [l_ref[l:l + 1, :] for l in range(DEPTH)]
    top = functools.reduce(jnp.maximum, rows)
    e = [jnp.exp(r - top) for r in rows]
    tot = functools.reduce(lambda a, b: a + b, e)
    return [v / tot for v in e]


def _lower_bounds_bwd(lower, dlbs):
    def body(l_ref, d_ref, o_ref):
        sm = _layer_softmax(l_ref)
        dsm = [None] * DEPTH
        run = jnp.zeros_like(sm[0])
        dsm[0] = run
        for l in reversed(range(1, DEPTH)):
            run = run + d_ref[l:l + 1, :]
            dsm[l] = run
        inner = functools.reduce(lambda a, b: a + b, [sm[l] * dsm[l] for l in range(DEPTH)])
        for l in range(DEPTH):
            o_ref[l:l + 1, :] = sm[l] * (dsm[l] - inner)

    return pl.pallas_call(body, name="lower_bounds_bwd", out_shape=SDS(lower.shape, f32))(lower, dlbs)


_ADAM_C1 = 1.0 - ADAM_B1 ** ADAM_STEP
_ADAM_C2 = 1.0 - ADAM_B2 ** ADAM_STEP


def _adamw(w, g, m, v):
    m = ADAM_B1 * m + (1.0 - ADAM_B1) * g
    v = ADAM_B2 * v + (1.0 - ADAM_B2) * (g * g)
    delta = -ADAM_LR * ((m / _ADAM_C1) / (jnp.sqrt(v / _ADAM_C2) + ADAM_EPS) + ADAM_WD * w)
    return delta, m, v


def _row_tile(rows, cap):
    return next(t for t in range(min(cap, rows) // 16 * 16, 0, -16) if rows % t == 0)


def _adam_big(where, names, w, m, v, sums, landed, outs, after, steps=4):
    nt = len(names)
    three = lambda a: a.reshape(a.shape[0], -1, a.shape[-1])
    w3, m3, v3 = ([three(d[n]) for n in names] for d in (w, m, v))
    outs3 = [three(a) for n in names for a in outs[n]]
    sums3 = [three(a) for a in sums]
    land3 = [three(a) for a in landed]

    def body(where_ref, *refs):
        del where_ref
        o_refs = refs[5 * nt + 4 * nt + 1:]
        for t in range(nt):
            w_ref, m_ref, v_ref, sum_ref, land_ref = (refs[q * nt + t] for q in range(5))
            g = sum_ref[...].astype(f32)
            for k in range(3):
                g = g + land_ref[k].astype(f32)
            delta, nm, nv = _adamw(w_ref[...], g, m_ref[...], v_ref[...])
            for o_ref, val in zip(o_refs[4 * t:4 * t + 4], (g, delta, nm, nv)):
                o_ref[...] = val

    tiles = [(a.shape[1] // steps, a.shape[2]) for a in w3]
    own = [pl.BlockSpec((None,) + t, lambda i, wh: (wh[0], i, 0)) for t in tiles]
    res = pl.pallas_call(
        body, name="adam_big", out_shape=[SDS(a.shape, f32) for a in outs3],
        input_output_aliases={1 + 5 * nt + i: i for i in range(4 * nt)}, compiler_params=_cp("parallel"),
        grid_spec=pltpu.PrefetchScalarGridSpec(
            num_scalar_prefetch=1, grid=(steps,),
            in_specs=own * 3 + [pl.BlockSpec((None,) + t, lambda i, wh: (wh[1], i, 0)) for t in tiles]
            + [pl.BlockSpec((3,) + t, lambda i, wh: (0, i, 0)) for t in tiles] + [ANY] * (4 * nt + 1),
            out_specs=[s for s in own for _ in range(4)]),
    )(where, *w3, *m3, *v3, *sums3, *land3, *outs3, after)
    return {n: [o.reshape(w[n].shape) for o in res[4 * t:4 * t + 4]] for t, n in enumerate(names)}


def _adam_rows(w, g, m, v):
    def body(w_ref, g_ref, m_ref, v_ref, d_ref, nm_ref, nv_ref):
        delta, nm, nv = _adamw(w_ref[...], g_ref[...], m_ref[...], v_ref[...])
        d_ref[...] = delta
        nm_ref[...] = nm
        nv_ref[...] = nv

    return pl.pallas_call(body, name="adam_rows", out_shape=[SDS(w.shape, f32)] * 3)(w, g, m, v)


SMALL = ("g_mix", "lower_bounds", "g_hgrn_out", "w_conv", "sg_ln_g", "sg_ln_b", "w_sg", "b_sg", "g_ffn", "g_final")
WEIGHTS = ("w_in", "g_mix", "lower_bounds", "g_hgrn_out", "w_conv", "sg_ln_g", "sg_ln_b", "w_sg", "b_sg", "w_branch", "w_o", "g_ffn",
           "w_ff1", "w_ff2", "g_final")


def _pack_rows(arrays, multiple):
    flat = jnp.concatenate([a.reshape(-1) for a in arrays])
    rows = -(-flat.shape[0] // (LANE * multiple)) * multiple
    return jnp.pad(flat, (0, rows * LANE - flat.shape[0])).reshape(rows, LANE)


def _unpack_rows(pack, like):
    flat = pack.reshape(-1)
    out, at = [], 0
    for a in like:
        out.append(flat[at:at + a.size].reshape(a.shape))
        at += a.size
    return out


def kernel(x, w_in, g_mix, lower_bounds, g_hgrn_out, w_conv, sg_ln_g, sg_ln_b, w_sg, b_sg, w_branch, w_o, g_ffn, w_ff1, w_ff2, g_final, loss_target, m_w_in, m_g_mix, m_lower_bounds, m_g_hgrn_out, m_w_conv, m_sg_ln_g, m_sg_ln_b, m_w_sg, m_b_sg, m_w_branch, m_w_o, m_g_ffn, m_w_ff1, m_w_ff2, m_g_final, v_w_in, v_g_mix, v_lower_bounds, v_g_hgrn_out, v_w_conv, v_sg_ln_g, v_sg_ln_b, v_w_sg, v_b_sg, v_w_branch, v_w_o, v_g_ffn, v_w_ff1, v_w_ff2, v_g_final):
    weights = dict(w_in=w_in, g_mix=g_mix, lower_bounds=lower_bounds, g_hgrn_out=g_hgrn_out, w_conv=w_conv, sg_ln_g=sg_ln_g,
                   sg_ln_b=sg_ln_b, w_sg=w_sg, b_sg=b_sg, w_branch=w_branch, w_o=w_o, g_ffn=g_ffn, w_ff1=w_ff1, w_ff2=w_ff2, g_final=g_final)
    mom1 = dict(w_in=m_w_in, g_mix=m_g_mix, lower_bounds=m_lower_bounds, g_hgrn_out=m_g_hgrn_out, w_conv=m_w_conv, sg_ln_g=m_sg_ln_g,
                sg_ln_b=m_sg_ln_b, w_sg=m_w_sg, b_sg=m_b_sg, w_branch=m_w_branch, w_o=m_w_o, g_ffn=m_g_ffn, w_ff1=m_w_ff1, w_ff2=m_w_ff2,
                g_final=m_g_final)
    mom2 = dict(w_in=v_w_in, g_mix=v_g_mix, lower_bounds=v_lower_bounds, g_hgrn_out=v_g_hgrn_out, w_conv=v_w_conv, sg_ln_g=v_sg_ln_g,
                sg_ln_b=v_sg_ln_b, w_sg=v_w_sg, b_sg=v_b_sg, w_branch=v_w_branch, w_o=v_w_o, g_ffn=v_g_ffn, w_ff1=v_w_ff1, w_ff2=v_w_ff2,
                g_final=v_g_final)
    xi, yi, ci = _place()
    dev = 4 * xi + 2 * yi + ci
    conv_cols = w_conv.shape[-1]

    for d in (weights, mom1, mom2):
        d["w_in"] = jnp.swapaxes(d["w_in"], 1, 2)
    shards = {n: weights[n].astype(bf16) for n in BIG}

    conv_place = lax.dynamic_update_slice(jnp.zeros((DEPTH, 3, BRANCH), f32), w_conv, (0, 0, dev * conv_cols))
    (w_conv_full,) = _unpack_rows(_all_reduce_rows(_pack_rows([conv_place], 8 * N_DEV)), [conv_place])
    lbs = _lower_bounds_fwd(lower_bounds)

    def small_of(l):
        return dict(g_mix=g_mix[l][None], lb=lbs[l][None], g_out=g_hgrn_out[l][None], w_conv=w_conv_full[l], ln_g=sg_ln_g[l][None],
                    ln_b=sg_ln_b[l][None], w_sg=w_sg[l], b_sg_t=b_sg[l].T, g_ffn=g_ffn[l][None])

    act = x[0]
    full, saved = [], []
    shard_refs = [jax.new_ref(shards[n], memory_space=pltpu.MemorySpace.HBM) for n in BIG]
    shapes = _gather_out_shapes(shards)
    every = tuple(range(len(BIG)))
    rest = (3, 4, 1, 2)
    groups = [((0,), 1, "a"), (rest, 1, "b")] + [(every, 1, "")] * (DEPTH - 1)
    gathered = [_seq_all_gather_layer(max(i - 1, 0), which, n_early, [shard_refs[t] for t in which], [shapes[t] for t in which], tag)
                for i, (which, n_early, tag) in enumerate(groups)]

    def placed_weights(l, which, arrived, after):
        where = jnp.stack([jnp.int32(l), dev.astype(jnp.int32)])
        return dict(zip([BIG[t] for t in which], _place_own(where, which, [shards[BIG[t]] for t in which], arrived, after)))

    for l in range(DEPTH):
        if l == 0:
            w_l = placed_weights(0, (0,), gathered[0], act)
            p_h = _rms_mm(act, small_of(0)["g_mix"], w_l["w_in"])
            w_l.update(placed_weights(0, rest, gathered[1], p_h[0]))
        else:
            w_l, p_h = placed_weights(l, every, gathered[l + 1], act), None
        full.append(w_l)
        act, sv = _layer_fwd(act, w_l, small_of(l), p_h)
        saved.append(sv)
    loss_row, dx, dxb, dg_final = _final(act, loss_target[0], g_final[None])
    loss = lax.psum(loss_row[0, 0], ("x", "y", "c"))

    core = ci.astype(jnp.int32)[None]
    big_out = {n: [lax.empty(weights[n].shape, f32) for _ in range(4)] for n in BIG}
    small_grads = [None] * DEPTH

    def chip_sums(stage, after):
        l, received, mine = stage
        sums = _chip_sums(core, mine, received, after)
        placed.append(sums[BIG.index("w_o")])
        landed, sums = _seq_exchange_between_chips(sums)
        return l, sums, landed

    def adam_layer(stage, after):
        l, sums, landed = stage
        where = jnp.stack([jnp.int32(l), (2 * xi + yi).astype(jnp.int32)])
        big_out.update(_adam_big(where, BIG, weights, mom1, mom2, sums, landed, big_out, after))

    above = None
    placed = []
    for l in reversed(range(DEPTH)):
        summed = []

        def between(dx1):
            if above is None:
                return dx1
            summed.append(chip_sums(above, dx1))
            return placed[-1]

        dx, dxb, big, small_grads[l] = _layer_bwd(dx, dxb, saved[l], full[l], small_of(l), between, lambda big: big["w_in"])
        if summed:
            adam_layer(summed[0], dx)
        above = (l, *_seq_exchange_on_chip([big[n] for n in BIG]))

    stack = lambda f: jnp.stack([f(small_grads[l]) for l in range(DEPTH)])
    d_lower = _lower_bounds_bwd(lower_bounds, stack(lambda s: s["vecs"][0]))
    local_small = dict(g_mix=stack(lambda s: s["g_mix"][0]), lower_bounds=d_lower, g_hgrn_out=stack(lambda s: s["vecs"][1]),
                       w_conv=stack(lambda s: s["vecs"][4:7]), sg_ln_g=stack(lambda s: s["vecs"][2]), sg_ln_b=stack(lambda s: s["vecs"][3]),
                       w_sg=stack(lambda s: s["w_sg"]), b_sg=stack(lambda s: s["b_sg_t"].T), g_ffn=stack(lambda s: s["g_ffn"][0]),
                       g_final=dg_final[0])
    order = [local_small[n] for n in SMALL]
    grads = dict(zip(SMALL, _unpack_rows(_all_reduce_rows(_pack_rows(order, 8 * N_DEV)), order)))
    grads["w_conv"] = lax.dynamic_slice(grads["w_conv"], (0, 0, dev * conv_cols), (DEPTH, 3, conv_cols))

    deltas, new_m, new_v = {}, {}, {}
    packs = [_pack_rows([d[n] for n in SMALL], 8) for d in (weights, grads, mom1, mom2)]
    like = [weights[n] for n in SMALL]
    small_out = _adam_rows(*packs)
    for out, pack in zip((deltas, new_m, new_v), small_out):
        out.update(zip(SMALL, _unpack_rows(pack, like)))
    adam_layer(chip_sums(above, dx), small_out[0])
    for n in BIG:
        grads[n], deltas[n], new_m[n], new_v[n] = (jnp.swapaxes(a, 1, 2) if n == "w_in" else a for a in big_out[n])

    return (loss, dx[None], *[grads[n] for n in WEIGHTS], *[deltas[n] for n in WEIGHTS], *[new_m[n] for n in WEIGHTS],
            *[new_v[n] for n in WEIGHTS])
```

```python
import functools

import jax
import jax.numpy as jnp
from jax import lax
from jax.experimental import pallas as pl
from jax.experimental.pallas import tpu as pltpu
from jax.experimental.pallas import tpu_sc as plsc

f32 = jnp.float32
bf16 = jnp.bfloat16
SDS = jax.ShapeDtypeStruct
MESH = pl.DeviceIdType.MESH

D_MODEL = 1024
BRANCH = 512
N_COLS = 7680
D_FF = 4096
DEPTH = 4
HEADS = 4
HEAD_DIM = 128
HGRN_CHUNK = 64
SG_CHUNK = 128
SG_GROUPS = 4
NORM_EPS = 1e-6
LN_EPS = 1e-5
LB_FLOOR = 1e-30
N_DEV = 8
SHARD_IN = N_COLS // N_DEV
WIN = 1024
LANE = 128
GATE_COL0 = 9 * BRANCH

ADAM_LR = 0.001
ADAM_B1 = 0.9
ADAM_B2 = 0.999
ADAM_EPS = 1e-08
ADAM_WD = 0.01
ADAM_STEP = 10

MIX_TILE = 256
VMEM_LIMIT = 56 * 1024 * 1024


def _cp(*sem):
    return pltpu.CompilerParams(dimension_semantics=sem or None, vmem_limit_bytes=VMEM_LIMIT)


def _dot(a, b):
    return jnp.dot(a, b, preferred_element_type=f32)


def _dot_nt(a, b):
    return lax.dot_general(a, b, (((1,), (1,)), ((), ())), preferred_element_type=f32)


def _dot_tn(a, b):
    return lax.dot_general(a, b, (((0,), (0,)), ((), ())), preferred_element_type=f32)


def _dot_exact(a, b):
    return jnp.dot(a, b, precision=lax.Precision.HIGHEST, preferred_element_type=f32)


def _sigmoid(x):
    return jax.nn.sigmoid(x)


_GELU_C = 0.7978845608028654
_GELU_A = 0.044715


def _gelu(x):
    return 0.5 * x * (1.0 + jnp.tanh(_GELU_C * (x + _GELU_A * x * x * x)))


def _gelu_grad(x):
    x2 = x * x
    t = jnp.tanh(_GELU_C * (x + _GELU_A * x * x2))
    return 0.5 * (1.0 + t) + 0.5 * x * (1.0 - t * t) * _GELU_C * (1.0 + 3.0 * _GELU_A * x2)


def _rms_stats(x):
    r = lax.rsqrt(jnp.mean(x * x, axis=-1, keepdims=True) + NORM_EPS)
    return r, x * r


def _rms_bwd(dh, xh, r, g):
    dg = jnp.sum(dh * xh, axis=0, keepdims=True)
    dxn = dh * g
    dx = r * (dxn - xh * jnp.mean(dxn * xh, axis=-1, keepdims=True))
    return dx, dg


def _tri(n, upper=False):
    r = lax.broadcasted_iota(jnp.int32, (n, n), 0)
    c = lax.broadcasted_iota(jnp.int32, (n, n), 1)
    return (c >= r) if upper else (c <= r)


def _acc_rows(ref, first, val):
    @pl.when(first)
    def _():
        ref[...] = val

    @pl.when(jnp.logical_not(first))
    def _():
        ref[...] += val


def _rms_mm(x, g, w_t, tm=1024, tn=1920):
    s, n = x.shape[0], w_t.shape[0]

    def body(x_ref, g_ref, w_ref, p_ref, h_ref, hs):
        @pl.when(pl.program_id(1) == 0)
        def _():
            _, xh = _rms_stats(x_ref[...])
            hv = (xh * g_ref[...]).astype(bf16)
            hs[...] = hv
            h_ref[...] = hv

        p_ref[...] = _dot_nt(hs[...], w_ref[...])

    return pl.pallas_call(
        body, name="rms_mm", grid=(s // tm, n // tn),
        in_specs=[pl.BlockSpec((tm, D_MODEL), lambda i, j: (i, 0)), pl.BlockSpec((1, D_MODEL), lambda i, j: (0, 0)),
                  pl.BlockSpec((tn, D_MODEL), lambda i, j: (j, 0))],
        out_specs=[pl.BlockSpec((tm, tn), lambda i, j: (i, j)), pl.BlockSpec((tm, D_MODEL), lambda i, j: (i, 0))],
        out_shape=[SDS((s, n), f32), SDS((s, D_MODEL), bf16)],
        scratch_shapes=[pltpu.VMEM((tm, D_MODEL), bf16)], compiler_params=_cp("parallel", "arbitrary"),
    )(x, g, w_t)


def _hgrn_gates(fp, lb):
    logf = jnp.logaddexp(jnp.log(jnp.maximum(lb, LB_FLOOR)), jnp.log1p(-lb) + jax.nn.log_sigmoid(fp))
    snf = _sigmoid(-fp)
    return logf, snf, (1.0 - lb) * snf


def _p_specs(tile, cols, row_map):
    return [pl.BlockSpec((tile, BRANCH), functools.partial(lambda c, i: (row_map(i), c), c)) for c in cols]


def _mixer_fwd(p, lb, gout, wconv, lng, lnb, wsg, bsg_t):
    s = p.shape[0]
    tt = MIX_TILE
    nch = tt // HGRN_CHUNK

    def body(q_ref, fp_ref, iv_ref, go_ref, bg_ref, cg_ref, xc_ref, u_ref, v_ref, lb_ref, gout_ref, wconv_ref, lng_ref,
             lnb_ref, wsg_ref, bsg_ref, z_ref, opre_ref, st_ref, st_scr, zbuf):
        @pl.when(pl.program_id(0) == 0)
        def _():
            st_scr[...] = jnp.zeros_like(st_scr)
            zbuf[0:8, :] = jnp.zeros((8, BRANCH), f32)

        lbv = lb_ref[...]
        q_raw = q_ref[...]
        qs = q_raw * _sigmoid(q_raw)
        logf, _, kk = _hgrn_gates(fp_ref[...], lbv)
        iv = iv_ref[...]
        causal = _tri(HGRN_CHUNK)
        tri = causal.astype(f32)
        last_row = lax.broadcasted_iota(jnp.int32, (HGRN_CHUNK, 1), 0) == HGRN_CHUNK - 1
        for c in range(nch):
            rows = slice(HGRN_CHUNK * c, HGRN_CHUNK * (c + 1))
            b = _dot_exact(tri, logf[rows])
            bl = jnp.sum(jnp.where(last_row, b, 0.0), axis=0, keepdims=True)
            qb = (qs[rows] * jnp.exp(b)).astype(bf16)
            kb = (kk[rows] * jnp.exp(-b)).astype(bf16)
            kd = (kk[rows] * jnp.exp(bl - b)).astype(bf16)
            ebl = jnp.exp(bl)
            vc = iv[rows].astype(bf16)
            for h in range(HEADS):
                sl = slice(HEAD_DIM * h, HEAD_DIM * (h + 1))
                st = st_scr[h]
                st_ref[c, h] = st
                a = jnp.where(causal, _dot_nt(qb[:, sl], kb[:, sl]), 0.0)
                opre_ref[rows, sl] = _dot(a.astype(bf16), vc[:, sl]) + _dot_nt(qb[:, sl], st.astype(bf16))
                st_scr[h] = st * ebl[:, sl] + _dot_tn(vc[:, sl], kd[:, sl])

        o = opre_ref[...]
        go = go_ref[...]
        gout_v = gout_ref[...]
        for h in range(HEADS):
            sl = slice(HEAD_DIM * h, HEAD_DIM * (h + 1))
            _, oh = _rms_stats(o[:, sl])
            z_ref[:, sl] = (oh * gout_v[:, sl] * _sigmoid(go[:, sl])).astype(bf16)

        zc = cg_ref[...] * xc_ref[...]
        zbuf[8:8 + tt, :] = zc
        y = wconv_ref[0:1, :] * zbuf[pl.ds(6, tt), :] + wconv_ref[1:2, :] * zbuf[pl.ds(7, tt), :] + wconv_ref[2:3, :] * zc
        z_ref[:, BRANCH:2 * BRANCH] = (bg_ref[...] * y).astype(bf16)
        zbuf[0:8, :] = zbuf[tt:tt + 8, :]

        ug = _gelu(u_ref[...])
        vg = _gelu(v_ref[...])
        vcen = vg - jnp.mean(vg, axis=-1, keepdims=True)
        rstd = lax.rsqrt(jnp.mean(vcen * vcen, axis=-1, keepdims=True) + LN_EPS)
        vn = (vcen * rstd * lng_ref[...] + lnb_ref[...]).astype(bf16)
        low = _tri(SG_CHUNK)
        for g in range(SG_GROUPS):
            sl = slice(LANE * g, LANE * (g + 1))
            wm = jnp.where(low, wsg_ref[g], 0.0).astype(bf16)
            bias = bsg_ref[:, g:g + 1]
            for cc in range(tt // SG_CHUNK):
                rows = slice(SG_CHUNK * cc, SG_CHUNK * (cc + 1))
                sv = _dot(wm, vn[rows, sl]) + bias
                z_ref[rows, 2 * BRANCH + LANE * g:2 * BRANCH + LANE * (g + 1)] = (ug[rows, sl] * sv).astype(bf16)

    full = lambda shape: pl.BlockSpec(shape, lambda i: (0,) * len(shape))
    return pl.pallas_call(
        body, name="mixer_fwd", grid=(s // tt,),
        in_specs=_p_specs(tt, range(9), lambda i: i) + [full((1, BRANCH)), full((1, BRANCH)), full((3, BRANCH)), full((1, BRANCH)),
                                                        full((1, BRANCH)), full((SG_GROUPS, SG_CHUNK, SG_CHUNK)), full((SG_CHUNK, SG_GROUPS))],
        out_specs=[pl.BlockSpec((tt, 3 * BRANCH), lambda i: (i, 0)), pl.BlockSpec((tt, BRANCH), lambda i: (i, 0)),
                   pl.BlockSpec((nch, HEADS, HEAD_DIM, HEAD_DIM), lambda i: (i, 0, 0, 0))],
        out_shape=[SDS((s, 3 * BRANCH), bf16), SDS((s, BRANCH), f32), SDS((s // HGRN_CHUNK, HEADS, HEAD_DIM, HEAD_DIM), f32)],
        scratch_shapes=[pltpu.VMEM((HEADS, HEAD_DIM, HEAD_DIM), f32), pltpu.VMEM((tt + 8, BRANCH), f32)],
        compiler_params=_cp("arbitrary"),
    )(*([p] * 9), lb, gout, wconv, lng, lnb, wsg, bsg_t)


def _branch_gate(z, wb, p, x, wo, tm=256):
    s = z.shape[0]
    half = 3 * D_MODEL // 2

    def body(z_ref, wb_ref, ga_ref, gb_ref, x_ref, wo_ref, y_ref, m_ref, x1_ref):
        ga, gb = ga_ref[...], gb_ref[...]
        gates = [ga[:, :D_MODEL], jnp.concatenate([ga[:, D_MODEL:], gb[:, :D_MODEL // 2]], axis=1), gb[:, D_MODEL // 2:]]
        acc = None
        for n in range(3):
            yn = _dot(z_ref[:, BRANCH * n:BRANCH * (n + 1)], wb_ref[n])
            y_ref[:, D_MODEL * n:D_MODEL * (n + 1)] = yn.astype(bf16)
            t = _sigmoid(gates[n]) * yn
            acc = t if acc is None else acc + t
        merged = acc.astype(bf16)
        m_ref[...] = merged
        x1_ref[...] = x_ref[...] + _dot(merged, wo_ref[...])

    blk0 = GATE_COL0 // half
    row = pl.BlockSpec((tm, D_MODEL), lambda i: (i, 0))
    return pl.pallas_call(
        body, name="branch_gate", grid=(s // tm,),
        in_specs=[pl.BlockSpec((tm, 3 * BRANCH), lambda i: (i, 0)), pl.BlockSpec((3, BRANCH, D_MODEL), lambda i: (0, 0, 0)),
                  pl.BlockSpec((tm, half), lambda i: (i, blk0)), pl.BlockSpec((tm, half), lambda i: (i, blk0 + 1)), row,
                  pl.BlockSpec((D_MODEL, D_MODEL), lambda i: (0, 0))],
        out_specs=[pl.BlockSpec((tm, 3 * D_MODEL), lambda i: (i, 0)), row, row],
        out_shape=[SDS((s, 3 * D_MODEL), bf16), SDS((s, D_MODEL), bf16), SDS((s, D_MODEL), f32)], compiler_params=_cp("parallel"),
    )(z, wb, p, p, x, wo)


def _ffn(x1, g, w1, w2, tm=512, tf=1024):
    s = x1.shape[0]
    nf = D_FF // tf

    def body(x_ref, g_ref, w1_ref, w2_ref, o_ref, h_ref, ra_ref, hs, acc):
        f = pl.program_id(1)

        @pl.when(f == 0)
        def _():
            _, xh = _rms_stats(x_ref[...])
            hv = (xh * g_ref[...]).astype(bf16)
            hs[...] = hv
            h_ref[...] = hv
            acc[...] = jnp.zeros_like(acc)

        ra = jnp.maximum(_dot(hs[...], w1_ref[...]), 0.0)
        ra_ref[...] = ra.astype(bf16)
        acc[...] += _dot((ra * ra).astype(bf16), w2_ref[...])

        @pl.when(f == nf - 1)
        def _():
            o_ref[...] = x_ref[...] + acc[...]

    return pl.pallas_call(
        body, name="ffn", grid=(s // tm, nf),
        in_specs=[pl.BlockSpec((tm, D_MODEL), lambda i, f: (i, 0)), pl.BlockSpec((1, D_MODEL), lambda i, f: (0, 0)),
                  pl.BlockSpec((D_MODEL, tf), lambda i, f: (0, f)), pl.BlockSpec((tf, D_MODEL), lambda i, f: (f, 0))],
        out_specs=[pl.BlockSpec((tm, D_MODEL), lambda i, f: (i, 0)), pl.BlockSpec((tm, D_MODEL), lambda i, f: (i, 0)),
                   pl.BlockSpec((tm, tf), lambda i, f: (i, f))],
        out_shape=[SDS((s, D_MODEL), f32), SDS((s, D_MODEL), bf16), SDS((s, D_FF), bf16)],
        scratch_shapes=[pltpu.VMEM((tm, D_MODEL), bf16), pltpu.VMEM((tm, D_MODEL), f32)], compiler_params=_cp("parallel", "arbitrary"),
    )(x1, g, w1, w2)


def _final(x, target, g, tm=512):
    s = x.shape[0]

    def body(x_ref, t_ref, g_ref, loss_ref, dx_ref, dxb_ref, dg_ref):
        first = pl.program_id(0) == 0
        gv = g_ref[...]
        r, xh = _rms_stats(x_ref[...])
        e = xh * gv - t_ref[...]
        tile_loss = 0.5 * jnp.sum(jnp.mean(e * e, axis=-1, keepdims=True), axis=0, keepdims=True)
        dx, dg = _rms_bwd(e * (1.0 / D_MODEL), xh, r, gv)
        dx_ref[...] = dx
        dxb_ref[...] = dx.astype(bf16)
        _acc_rows(dg_ref, first, dg)
        _acc_rows(loss_ref, first, jnp.broadcast_to(tile_loss, (1, LANE)))

    row = pl.BlockSpec((tm, D_MODEL), lambda i: (i, 0))
    return pl.pallas_call(
        body, name="final_loss", grid=(s // tm,), in_specs=[row, row, pl.BlockSpec((1, D_MODEL), lambda i: (0, 0))],
        out_specs=[pl.BlockSpec((1, LANE), lambda i: (0, 0)), row, row, pl.BlockSpec((1, D_MODEL), lambda i: (0, 0))],
        out_shape=[SDS((1, LANE), f32), SDS((s, D_MODEL), f32), SDS((s, D_MODEL), bf16), SDS((1, D_MODEL), f32)],
        compiler_params=_cp("arbitrary"),
    )(x, target, g)


def _ffn_bwd(dx2, dx2b, x1, g, ra, w1, w2, tm=512, tf=1024):
    s = x1.shape[0]
    nf = D_FF // tf

    def body(dx_ref, dxb_ref, x_ref, g_ref, ra_ref, w1_ref, w2_ref, da_ref, dx1_ref, dx1b_ref, dg_ref, acc):
        i, f = pl.program_id(0), pl.program_id(1)

        @pl.when(f == 0)
        def _():
            acc[...] = jnp.zeros_like(acc)

        da = (_dot_nt(dxb_ref[...], w2_ref[...]) * (2.0 * ra_ref[...].astype(f32))).astype(bf16)
        da_ref[...] = da
        acc[...] += _dot_nt(da, w1_ref[...])

        @pl.when(f == nf - 1)
        def _():
            r, xh = _rms_stats(x_ref[...])
            dx, dg = _rms_bwd(acc[...], xh, r, g_ref[...])
            dx = dx + dx_ref[...]
            dx1_ref[...] = dx
            dx1b_ref[...] = dx.astype(bf16)
            _acc_rows(dg_ref, i == 0, dg)

    row = pl.BlockSpec((tm, D_MODEL), lambda i, f: (i, 0))
    col = pl.BlockSpec((tm, tf), lambda i, f: (i, f))
    return pl.pallas_call(
        body, name="ffn_bwd", grid=(s // tm, nf),
        in_specs=[row, row, row, pl.BlockSpec((1, D_MODEL), lambda i, f: (0, 0)), col,
                  pl.BlockSpec((D_MODEL, tf), lambda i, f: (0, f)), pl.BlockSpec((tf, D_MODEL), lambda i, f: (f, 0))],
        out_specs=[col, row, row, pl.BlockSpec((1, D_MODEL), lambda i, f: (0, 0))],
        out_shape=[SDS((s, D_FF), bf16), SDS((s, D_MODEL), f32), SDS((s, D_MODEL), bf16), SDS((1, D_MODEL), f32)],
        scratch_shapes=[pltpu.VMEM((tm, D_MODEL), f32)], compiler_params=_cp("arbitrary", "arbitrary"),
    )(dx2, dx2b, x1, g, ra, w1, w2)


def _mm_tn(a, b, nb, m, n, tm, tn, name="mm_tn", rows=None, row0=0, into=None, square_a=False):
    s = a.shape[0]
    mi, nj = m // tm, n // tn
    rows = m if rows is None else rows
    blk0 = row0 // tm

    def body(a_ref, b_ref, *rest):
        av = a_ref[...]
        if square_a:
            av = av.astype(f32)
            av = (av * av).astype(bf16)
        rest[-1][...] = _dot_tn(av, b_ref[...]).astype(bf16)

    extra = {} if into is None else dict(input_output_aliases={2: 0})
    return pl.pallas_call(
        body, name=name, grid=(nb, mi, nj),
        in_specs=[pl.BlockSpec((s, tm), lambda k, i, j: (0, k * mi + i)), pl.BlockSpec((s, tn), lambda k, i, j: (0, k * nj + j))]
        + ([] if into is None else [pl.BlockSpec(memory_space=pl.ANY)]),
        out_specs=pl.BlockSpec((None, tm, tn), lambda k, i, j: (k, blk0 + i, j)), out_shape=SDS((nb, rows, n), bf16),
        compiler_params=_cp("parallel", "parallel", "parallel"), **extra,
    )(a, b, *([] if into is None else [into]))


def _mm_tn_slabs(a, b, nb, m, nblk, rel, width, tm=512, name="mm_tn_slabs"):
    s = a.shape[0]
    n = b.shape[1] // nb
    ng, mi, nw = n // nblk, m // tm, len(rel)

    def body(a_ref, b_ref, o_ref):
        full = _dot_tn(a_ref[...], b_ref[...])
        for r, start in enumerate(rel):
            o_ref[r] = full[:, start:start + width].astype(bf16)

    return pl.pallas_call(
        body, name=name, grid=(nb, ng, mi),
        in_specs=[pl.BlockSpec((s, tm), lambda k, g, i: (0, k * mi + i)), pl.BlockSpec((s, nblk), lambda k, g, i: (0, k * ng + g))],
        out_specs=pl.BlockSpec((nw, None, tm, width), lambda k, g, i: (g, k, i, 0)), out_shape=SDS((ng * nw, nb, m, width), bf16),
        compiler_params=_cp("parallel", "parallel", "parallel"),
    )(a, b)


def _merge_bwd(dx1b, wo, y, p, wb, after, tm=256):
    s = dx1b.shape[0]
    half = 3 * D_MODEL // 2
    blk0 = GATE_COL0 // half

    def body(dx_ref, wo_ref, y_ref, ga_ref, gb_ref, wb_ref, after_ref, dy_ref, dg_ref, dz_ref):
        del after_ref
        dm = _dot_nt(dx_ref[...], wo_ref[...])
        ga, gb = ga_ref[...], gb_ref[...]
        gates = [ga[:, :D_MODEL], jnp.concatenate([ga[:, D_MODEL:], gb[:, :D_MODEL // 2]], axis=1), gb[:, D_MODEL // 2:]]
        for n in range(3):
            cols = slice(D_MODEL * n, D_MODEL * (n + 1))
            gate = _sigmoid(gates[n])
            t = dm * gate
            dy = t.astype(bf16)
            dy_ref[:, cols] = dy
            dg_ref[:, cols] = (t * y_ref[:, cols].astype(f32) * (1.0 - gate)).astype(bf16)
            dz_ref[:, BRANCH * n:BRANCH * (n + 1)] = _dot_nt(dy, wb_ref[n])

    wide = pl.BlockSpec((tm, 3 * D_MODEL), lambda i: (i, 0))
    return pl.pallas_call(
        body, name="merge_bwd", grid=(s // tm,),
        in_specs=[pl.BlockSpec((tm, D_MODEL), lambda i: (i, 0)), pl.BlockSpec((D_MODEL, D_MODEL), lambda i: (0, 0)), wide,
                  pl.BlockSpec((tm, half), lambda i: (i, blk0)), pl.BlockSpec((tm, half), lambda i: (i, blk0 + 1)),
                  pl.BlockSpec((3, BRANCH, D_MODEL), lambda i: (0, 0, 0)), pl.BlockSpec(memory_space=pl.ANY)],
        out_specs=[wide, wide, pl.BlockSpec((tm, 3 * BRANCH), lambda i: (i, 0))],
        out_shape=[SDS((s, 3 * D_MODEL), bf16), SDS((s, 3 * D_MODEL), bf16), SDS((s, 3 * BRANCH), f32)],
        compiler_params=_cp("parallel"),
    )(dx1b, wo, y, p, p, wb, after)


def _mixer_bwd(p, dz, opre, states, lb, gout, wconv, lng, lnb, wsg, bsg_t):
    s = p.shape[0]
    tt = MIX_TILE
    nt = s // tt
    nch = tt // HGRN_CHUNK
    rev = lambda i: nt - 1 - i

    def body(q_ref, fp_ref, iv_ref, go_ref, bg_ref, cg_ref, xc_ref, u_ref, v_ref, cgp_ref, xcp_ref, dz_ref, opre_ref, st_ref,
             lb_ref, gout_ref, wconv_ref, lng_ref, lnb_ref, wsg_ref, bsg_ref,
             dp_ref, vec_ref, dwsg_ref, dbsg_ref, dst_scr, zbuf, dybuf, dvn_scr, dbsg_acc):
        i = pl.program_id(0)

        @pl.when(i == 0)
        def _():
            dst_scr[...] = jnp.zeros_like(dst_scr)
            dybuf[tt:tt + 8, :] = jnp.zeros((8, BRANCH), f32)
            vec_ref[...] = jnp.zeros_like(vec_ref)
            dwsg_ref[...] = jnp.zeros_like(dwsg_ref)
            dbsg_acc[...] = jnp.zeros_like(dbsg_acc)

        lbv = lb_ref[...]
        q_raw, fp = q_ref[...], fp_ref[...]
        sq = _sigmoid(q_raw)
        qs = q_raw * sq
        sfp = _sigmoid(fp)
        logf, snf, kk = _hgrn_gates(fp, lbv)
        inv_f = jnp.exp(-logf)
        iv = iv_ref[...]
        doa = dz_ref[:, 0:BRANCH]
        o = opre_ref[...]
        sgo = _sigmoid(go_ref[...])
        gout_v = gout_ref[...]
        d_o, dgo, dgout = [], [], []
        for h in range(HEADS):
            sl = slice(HEAD_DIM * h, HEAD_DIM * (h + 1))
            r, oh = _rms_stats(o[:, sl])
            d_on = doa[:, sl] * sgo[:, sl]
            dgo.append(doa[:, sl] * oh * gout_v[:, sl] * sgo[:, sl] * (1.0 - sgo[:, sl]))
            dx, dg = _rms_bwd(d_on, oh, r, gout_v[:, sl])
            d_o.append(dx)
            dgout.append(dg)
        d_o = jnp.concatenate(d_o, axis=1)
        dp_ref[:, 3 * BRANCH:4 * BRANCH] = jnp.concatenate(dgo, axis=1).astype(bf16)
        vec_ref[1:2, :] += jnp.concatenate(dgout, axis=1)

        causal = _tri(HGRN_CHUNK)
        tri = causal.astype(f32)
        tri_up = _tri(HGRN_CHUNK, upper=True).astype(f32)
        last_row = lax.broadcasted_iota(jnp.int32, (HGRN_CHUNK, 1), 0) == HGRN_CHUNK - 1
        lb_live = (lbv > LB_FLOOR).astype(f32)
        dlb = jnp.zeros((1, BRANCH), f32)
        for c in reversed(range(nch)):
            rows = slice(HGRN_CHUNK * c, HGRN_CHUNK * (c + 1))
            b = _dot_exact(tri, logf[rows])
            bl = jnp.sum(jnp.where(last_row, b, 0.0), axis=0, keepdims=True)
            eb, enb, edl, ebl = jnp.exp(b), jnp.exp(-b), jnp.exp(bl - b), jnp.exp(bl)
            qbf, kbf, kdf = qs[rows] * eb, kk[rows] * enb, kk[rows] * edl
            qb, kb, kd = qbf.astype(bf16), kbf.astype(bf16), kdf.astype(bf16)
            vc = iv[rows].astype(bf16)
            dob = d_o[rows].astype(bf16)
            dv, dqb, dkb, dkd, debl = [], [], [], [], []
            for h in range(HEADS):
                sl = slice(HEAD_DIM * h, HEAD_DIM * (h + 1))
                st = st_ref[c, h]
                dst = dst_scr[h]
                stb, dstb = st.astype(bf16), dst.astype(bf16)
                a = jnp.where(causal, _dot_nt(qb[:, sl], kb[:, sl]), 0.0).astype(bf16)
                da = jnp.where(causal, _dot_nt(dob[:, sl], vc[:, sl]), 0.0).astype(bf16)
                dv.append(_dot_tn(a, dob[:, sl]) + _dot_nt(kd[:, sl], dstb))
                dqb.append(_dot(dob[:, sl], stb) + _dot(da, kb[:, sl]))
                dkb.append(_dot_tn(da, qb[:, sl]))
                dkd.append(_dot(vc[:, sl], dstb))
                debl.append(jnp.sum(st * dst, axis=0, keepdims=True))
                dst_scr[h] = _dot_tn(dob[:, sl], qb[:, sl]) + dst * ebl[:, sl]
            dv, dqb, dkb, dkd = (jnp.concatenate(t, axis=1) for t in (dv, dqb, dkb, dkd))
            debl = jnp.concatenate(debl, axis=1)
            t_kd = dkd * kdf
            dbl = ebl * debl + jnp.sum(t_kd, axis=0, keepdims=True)
            db = dqb * qbf - dkb * kbf - t_kd + jnp.where(last_row, dbl, 0.0)
            dkk = dkb * enb + dkd * edl
            dlc = _dot_exact(tri_up, db)
            sq_c, q_c, sfp_c, snf_c, invf_c = sq[rows], q_raw[rows], sfp[rows], snf[rows], inv_f[rows]
            slope = (1.0 - lbv) * sfp_c * snf_c
            dp_ref[rows, 0:BRANCH] = (dqb * eb * sq_c * (1.0 + q_c * (1.0 - sq_c))).astype(bf16)
            dp_ref[rows, BRANCH:2 * BRANCH] = (slope * (dlc * invf_c - dkk)).astype(bf16)
            dp_ref[rows, 2 * BRANCH:3 * BRANCH] = dv.astype(bf16)
            dlb = dlb + jnp.sum(dlc * (lb_live - sfp_c) * invf_c - dkk * snf_c, axis=0, keepdims=True)
        vec_ref[0:1, :] += dlb

        dob_ = dz_ref[:, BRANCH:2 * BRANCH]
        bg, cg, xc = bg_ref[...], cg_ref[...], xc_ref[...]
        zc = cg * xc
        zbuf[0:8, :] = jnp.where(i < nt - 1, cgp_ref[...] * xcp_ref[...], 0.0)
        zbuf[8:8 + tt, :] = zc
        w0, w1, w2 = wconv_ref[0:1, :], wconv_ref[1:2, :], wconv_ref[2:3, :]
        y = w0 * zbuf[pl.ds(6, tt), :] + w1 * zbuf[pl.ds(7, tt), :] + w2 * zc
        dy = dob_ * bg
        dybuf[0:tt, :] = dy
        dy1, dy2 = dybuf[pl.ds(1, tt), :], dybuf[pl.ds(2, tt), :]
        dzc = w2 * dy + w1 * dy1 + w0 * dy2
        dp_ref[:, 4 * BRANCH:5 * BRANCH] = (dob_ * y).astype(bf16)
        dp_ref[:, 5 * BRANCH:6 * BRANCH] = (dzc * xc).astype(bf16)
        dp_ref[:, 6 * BRANCH:7 * BRANCH] = (dzc * cg).astype(bf16)
        vec_ref[4:5, :] += jnp.sum(zc * dy2, axis=0, keepdims=True)
        vec_ref[5:6, :] += jnp.sum(zc * dy1, axis=0, keepdims=True)
        vec_ref[6:7, :] += jnp.sum(zc * dy, axis=0, keepdims=True)
        dybuf[tt:tt + 8, :] = dybuf[0:8, :]

        doc = dz_ref[:, 2 * BRANCH:3 * BRANCH]
        u_raw, v_raw = u_ref[...], v_ref[...]
        ug = _gelu(u_raw)
        dug_scale = _gelu_grad(u_raw)
        vg = _gelu(v_raw)
        vcen = vg - jnp.mean(vg, axis=-1, keepdims=True)
        rstd = lax.rsqrt(jnp.mean(vcen * vcen, axis=-1, keepdims=True) + LN_EPS)
        vhat = vcen * rstd
        lng_v = lng_ref[...]
        vn = (vhat * lng_v + lnb_ref[...]).astype(bf16)
        low = _tri(SG_CHUNK)
        for g in range(SG_GROUPS):
            sl = slice(LANE * g, LANE * (g + 1))
            wm = jnp.where(low, wsg_ref[g], 0.0).astype(bf16)
            bias = bsg_ref[:, g:g + 1]
            dw = jnp.zeros((SG_CHUNK, SG_CHUNK), f32)
            dbs = jnp.zeros((SG_CHUNK, LANE), f32)
            for cc in range(tt // SG_CHUNK):
                rows = slice(SG_CHUNK * cc, SG_CHUNK * (cc + 1))
                vn_c = vn[rows, sl]
                sv = _dot(wm, vn_c) + bias
                doc_c = doc[rows, sl]
                dp_ref[rows, 7 * BRANCH + LANE * g:7 * BRANCH + LANE * (g + 1)] = (doc_c * sv * dug_scale[rows, sl]).astype(bf16)
                dsv = doc_c * ug[rows, sl]
                dsvb = dsv.astype(bf16)
                dbs = dbs + dsv
                dw = dw + _dot_nt(dsvb, vn_c)
                dvn_scr[rows, sl] = _dot_tn(wm, dsvb)
            dwsg_ref[g] += jnp.where(low, dw, 0.0)
            dbsg_acc[:, sl] += dbs
        dvn = dvn_scr[...]
        vec_ref[2:3, :] += jnp.sum(dvn * vhat, axis=0, keepdims=True)
        vec_ref[3:4, :] += jnp.sum(dvn, axis=0, keepdims=True)
        dvh = dvn * lng_v
        dvg = rstd * (dvh - jnp.mean(dvh, axis=-1, keepdims=True) - vhat * jnp.mean(dvh * vhat, axis=-1, keepdims=True))
        dp_ref[:, 8 * BRANCH:9 * BRANCH] = (dvg * _gelu_grad(v_raw)).astype(bf16)

        @pl.when(i == nt - 1)
        def _():
            for g in range(SG_GROUPS):
                dbsg_ref[:, g:g + 1] = jnp.sum(dbsg_acc[:, LANE * g:LANE * (g + 1)], axis=1, keepdims=True)

    full = lambda shape: pl.BlockSpec(shape, lambda i: (0,) * len(shape))
    tail = lambda c: pl.BlockSpec((8, BRANCH), lambda i: (jnp.maximum(rev(i) * (tt // 8) - 1, 0), c))
    return pl.pallas_call(
        body, name="mixer_bwd", grid=(nt,),
        in_specs=_p_specs(tt, range(9), rev) + [tail(5), tail(6), pl.BlockSpec((tt, 3 * BRANCH), lambda i: (rev(i), 0)),
                                                pl.BlockSpec((tt, BRANCH), lambda i: (rev(i), 0)),
                                                pl.BlockSpec((nch, HEADS, HEAD_DIM, HEAD_DIM), lambda i: (rev(i), 0, 0, 0)),
                                                full((1, BRANCH)), full((1, BRANCH)), full((3, BRANCH)), full((1, BRANCH)), full((1, BRANCH)),
                                                full((SG_GROUPS, SG_CHUNK, SG_CHUNK)), full((SG_CHUNK, SG_GROUPS))],
        out_specs=[pl.BlockSpec((tt, 9 * BRANCH), lambda i: (rev(i), 0)), full((8, BRANCH)), full((SG_GROUPS, SG_CHUNK, SG_CHUNK)),
                   full((SG_CHUNK, SG_GROUPS))],
        out_shape=[SDS((s, 9 * BRANCH), bf16), SDS((8, BRANCH), f32), SDS((SG_GROUPS, SG_CHUNK, SG_CHUNK), f32), SDS((SG_CHUNK, SG_GROUPS), f32)],
        scratch_shapes=[pltpu.VMEM((HEADS, HEAD_DIM, HEAD_DIM), f32), pltpu.VMEM((tt + 8, BRANCH), f32), pltpu.VMEM((tt + 8, BRANCH), f32),
                        pltpu.VMEM((tt, BRANCH), f32), pltpu.VMEM((SG_CHUNK, BRANCH), f32)],
        compiler_params=_cp("arbitrary"),
    )(*([p] * 11), dz, opre, states, lb, gout, wconv, lng, lnb, wsg, bsg_t)


def _dh_bwd(dpm, dpg, w_t, x, dx1, g, after, tm=1024, tk=1536):
    s = x.shape[0]
    km = dpm.shape[1] // tk
    nk = km + dpg.shape[1] // tk

    def body(dpm_ref, dpg_ref, w_ref, x_ref, dx1_ref, g_ref, after_ref, dx_ref, dxb_ref, dg_ref, acc):
        del after_ref
        i, k = pl.program_id(0), pl.program_id(1)

        @pl.when(k == 0)
        def _():
            acc[...] = jnp.zeros_like(acc)

        @pl.when(k < km)
        def _():
            acc[...] += _dot(dpm_ref[...], w_ref[...])

        @pl.when(k >= km)
        def _():
            acc[...] += _dot(dpg_ref[...], w_ref[...])

        @pl.when(k == nk - 1)
        def _():
            r, xh = _rms_stats(x_ref[...])
            dx, dg = _rms_bwd(acc[...], xh, r, g_ref[...])
            dx = dx + dx1_ref[...]
            dx_ref[...] = dx
            dxb_ref[...] = dx.astype(bf16)
            _acc_rows(dg_ref, i == 0, dg)

    row = pl.BlockSpec((tm, D_MODEL), lambda i, k: (i, 0))
    vec = pl.BlockSpec((1, D_MODEL), lambda i, k: (0, 0))
    return pl.pallas_call(
        body, name="dh_bwd", grid=(s // tm, nk),
        in_specs=[pl.BlockSpec((tm, tk), lambda i, k: (i, jnp.minimum(k, km - 1))),
                  pl.BlockSpec((tm, tk), lambda i, k: (i, jnp.maximum(k - km, 0))),
                  pl.BlockSpec((tk, D_MODEL), lambda i, k: (k, 0)), row, row, vec, pl.BlockSpec(memory_space=pl.ANY)],
        out_specs=[row, row, vec], out_shape=[SDS((s, D_MODEL), f32), SDS((s, D_MODEL), bf16), SDS((1, D_MODEL), f32)],
        scratch_shapes=[pltpu.VMEM((tm, D_MODEL), f32)], compiler_params=_cp("arbitrary", "arbitrary"),
    )(dpm, dpg, w_t, x, dx1, g, after)


def _layer_fwd(x, w, sm, p_h=None):
    p, h = _rms_mm(x, sm["g_mix"], w["w_in"]) if p_h is None else p_h
    z, opre, states = _mixer_fwd(p, sm["lb"], sm["g_out"], sm["w_conv"], sm["ln_g"], sm["ln_b"], sm["w_sg"], sm["b_sg_t"])
    y, merged, x1 = _branch_gate(z, w["w_branch"], p, x, w["w_o"])
    x2, h2, ra = _ffn(x1, sm["g_ffn"], w["w_ff1"], w["w_ff2"])
    saved = dict(x=x, p=p, h=h, z=z, opre=opre, states=states, y=y, merged=merged, x1=x1, h2=h2, ra=ra)
    return x2, saved


def _layer_bwd(dx2, dx2b, sv, w, sm, between, before_end):
    nchip = N_DEV // 2
    by_chip = lambda g: g.reshape((nchip, 2) + g.shape[1:])
    da, dx1, dx1b, dg_ffn = _ffn_bwd(dx2, dx2b, sv["x1"], sm["g_ffn"], sv["ra"], w["w_ff1"], w["w_ff2"])
    g_ff2 = by_chip(_mm_tn(sv["ra"], dx2b, 1, D_FF, D_MODEL, 512, 1024, name="dw_ff2", square_a=True)[0]
                    .reshape(N_DEV, D_FF // N_DEV, D_MODEL))
    g_ff1 = by_chip(_mm_tn_slabs(sv["h2"], da, 1, D_MODEL, D_FF // 2, [i * (D_FF // N_DEV) for i in range(nchip)], D_FF // N_DEV,
                                 name="dw_ff1")[:, 0])
    g_o = by_chip(_mm_tn(sv["merged"], dx1b, 1, D_MODEL, D_MODEL, 512, 1024, name="dw_o")[0].reshape(N_DEV, D_MODEL // N_DEV, D_MODEL))
    dy, dpg, dz = _merge_bwd(dx1b, w["w_o"], sv["y"], sv["p"], w["w_branch"], between(dx1))
    g_branch = by_chip(_mm_tn_slabs(sv["z"], dy, 3, BRANCH, D_MODEL, [i * (D_MODEL // N_DEV) for i in range(N_DEV)], D_MODEL // N_DEV,
                                    name="dw_branch"))
    g_in = _mm_tn(dpg, sv["h"], 1, 3 * D_MODEL, D_MODEL, 512, 1024, name="dw_in_gates", rows=N_COLS, row0=GATE_COL0)
    dpm, vecs, dwsg, dbsg_t = _mixer_bwd(sv["p"], dz, sv["opre"], sv["states"], sm["lb"], sm["g_out"], sm["w_conv"],
                                         sm["ln_g"], sm["ln_b"], sm["w_sg"], sm["b_sg_t"])
    g_in = _mm_tn(dpm, sv["h"], 1, GATE_COL0, D_MODEL, 512, 1024, name="dw_in_mixers", rows=N_COLS, into=g_in)
    g_in = by_chip(g_in[0].reshape(N_DEV, SHARD_IN, D_MODEL))
    big = dict(w_in=g_in, w_branch=g_branch, w_o=g_o, w_ff1=g_ff1, w_ff2=g_ff2)
    dx, dxb, dg_mix = _dh_bwd(dpm, dpg, w["w_in"], sv["x"], dx1, sm["g_mix"], before_end(big))
    small = dict(g_mix=dg_mix, g_ffn=dg_ffn, vecs=vecs, w_sg=dwsg, b_sg_t=dbsg_t, dx1=dx1)
    return dx, dxb, big, small


BIG = ("w_in", "w_branch", "w_o", "w_ff1", "w_ff2")
ANY = pl.BlockSpec(memory_space=pl.ANY)


def _place():
    return lax.axis_index("x"), lax.axis_index("y"), lax.axis_index("c")


def _al(v, m):
    return pl.multiple_of(v * m, m)


def _shard_of(refs, dev, which=range(len(BIG))):
    out = []
    for ref, t in zip(refs, which):
        by_cols = BIG[t] in ("w_branch", "w_ff1")
        n = ref.shape[-1 if by_cols else 0] // N_DEV
        part = pl.ds(_al(dev, n), n)
        out.append(ref.at[(slice(None),) * (len(ref.shape) - 1) + (part,)] if by_cols else ref.at[part])
    return out


def _gather_out_shapes(shards):
    s_in, s_b, s_o, s_1, s_2 = (shards[n] for n in BIG)
    return [SDS((s_in.shape[1] * N_DEV, s_in.shape[2]), bf16), SDS(s_b.shape[1:3] + (s_b.shape[3] * N_DEV,), bf16),
            SDS((s_o.shape[1] * N_DEV, s_o.shape[2]), bf16), SDS((s_1.shape[1], s_1.shape[2] * N_DEV), bf16),
            SDS((s_2.shape[1] * N_DEV, s_2.shape[2]), bf16)]


def _seq_all_gather_layer(layer, which, n_early, shard_refs, out_shapes, tag=""):
    nt = len(which)
    outs = [jax.empty_ref(sh, memory_space=pltpu.MemorySpace.HBM) for sh in out_shapes]
    early, late = tuple(range(n_early)), tuple(range(n_early, nt))

    @pl.kernel(mesh=plsc.ScalarSubcoreMesh(axis_name="seq", num_cores=1), name=f"seq_all_gather_l{layer}{tag}",
               scratch_types=(pltpu.SemaphoreType.DMA((9,)), pltpu.SemaphoreType.DMA((9,))),
               compiler_params=pltpu.CompilerParams(collective_id=1))
    def launch(send_sems, recv_sems):
        x, y, c = _place()
        me, sibling = (x, y, c), (x, y, 1 - c)
        first, second, diag = _ici_route(x, y, c)
        _handshake([sibling, first, second])
        mine = [r.at[layer] for r in shard_refs]

        def copies(k, blk, to, src=None, part=range(nt)):
            dst = _shard_of(outs, 4 * blk[0] + 2 * blk[1] + blk[2], which)
            src = dst if src is None else src
            return [pltpu.make_async_remote_copy(src_ref=src[t], dst_ref=dst[t], send_sem=send_sems.at[k], recv_sem=recv_sems.at[k],
                                                 device_id=to, device_id_type=MESH) for t in part]

        def start(cps):
            for cp in cps:
                cp.start()
            return cps

        def landed(cps):
            for cp in cps:
                cp.wait_recv()

        sent = start(copies(0, me, sibling, src=mine) + copies(1, me, first, src=mine, part=early)
                     + copies(2, me, first, src=mine, part=late) + copies(3, me, second, src=mine))
        landed(copies(1, first, me, part=early))
        sent += start(copies(4, first, second, part=early) + copies(6, first, sibling, part=early))
        landed(copies(2, first, me, part=late))
        sent += start(copies(5, first, second, part=late) + copies(6, first, sibling, part=late))
        landed(copies(3, second, me))
        sent += start(copies(7, second, sibling))
        landed(copies(4, diag, me, part=early) + copies(5, diag, me, part=late))
        sent += start(copies(8, diag, sibling))
        other = lambda p: (p[0], p[1], 1 - c)
        landed(copies(0, sibling, me) + copies(6, other(second), me) + copies(7, other(first), me) + copies(8, other(diag), me))
        for cp in sent:
            cp.wait_send()

    launch()
    return [o[...] for o in outs]


def _ici_route(x, y, c):
    return (x ^ (1 - c), y ^ c, c), (x ^ c, y ^ (1 - c), c), (1 - x, 1 - y, c)


def _place_own(where, which, shards, gathered, after):
    nt = len(which)

    def body(where_ref, *refs):
        del where_ref
        for src, dst in zip(refs[:nt], refs[2 * nt + 1:]):
            dst[...] = src[...]

    in_specs, out_specs = [], []
    for t, sh in zip(which, shards):
        blk = sh.shape[1:]
        in_specs.append(pl.BlockSpec((None,) + blk, functools.partial(lambda nd, i, wh: (wh[0],) + (0,) * nd, len(blk))))
        by_cols = BIG[t] in ("w_branch", "w_ff1")
        out_specs.append(pl.BlockSpec(blk, functools.partial(
            lambda nd, cols, i, wh: (0,) * (nd - 1) + (wh[1],) if cols else (wh[1],) + (0,) * (nd - 1), len(blk), by_cols)))
    return pl.pallas_call(
        body, name="place_own", out_shape=[SDS(g.shape, g.dtype) for g in gathered],
        input_output_aliases={1 + nt + i: i for i in range(nt)}, compiler_params=_cp("arbitrary"),
        grid_spec=pltpu.PrefetchScalarGridSpec(num_scalar_prefetch=1, grid=(1,), in_specs=in_specs + [ANY] * (nt + 1), out_specs=out_specs),
    )(where, *shards, *gathered, after)


def _handshake(peers):
    barrier = pltpu.get_barrier_semaphore()
    for p in peers:
        pl.semaphore_signal(barrier, inc=1, device_id=p, device_id_type=MESH)
    pl.semaphore_wait(barrier, len(peers))


def _seq_exchange_on_chip(grads):
    nt, nchip = len(BIG), N_DEV // 2
    g_refs = [jax.new_ref(g, memory_space=pltpu.MemorySpace.HBM) for g in grads]
    outs = [jax.empty_ref(SDS((nchip,) + g.shape[2:], bf16), memory_space=pltpu.MemorySpace.HBM) for g in grads]

    @pl.kernel(mesh=plsc.ScalarSubcoreMesh(axis_name="seq", num_cores=1), name="seq_rs_on_chip",
               scratch_types=(pltpu.SemaphoreType.DMA((nchip,)), pltpu.SemaphoreType.DMA((nchip,))),
               compiler_params=pltpu.CompilerParams(collective_id=2))
    def launch(send_sems, recv_sems):
        x, y, c = _place()
        sibling = (x, y, 1 - c)
        _handshake([sibling])
        remote = [pltpu.make_async_remote_copy(src_ref=g_refs[t].at[j, 1 - c], dst_ref=outs[t].at[j], send_sem=send_sems.at[j],
                                               recv_sem=recv_sems.at[j], device_id=sibling, device_id_type=MESH)
                  for j in range(nchip) for t in range(nt)]
        for cp in remote:
            cp.start()
        for cp in remote:
            cp.wait_recv()
        for cp in remote:
            cp.wait_send()

    launch()
    return [o[...] for o in outs], [g[...] for g in g_refs]


def _seq_exchange_between_chips(sums):
    nt = len(BIG)
    s_refs = [jax.new_ref(a, memory_space=pltpu.MemorySpace.HBM) for a in sums]
    outs = [jax.empty_ref(SDS((3,) + a.shape[1:], bf16), memory_space=pltpu.MemorySpace.HBM) for a in sums]
    transit = [jax.empty_ref(SDS(a.shape[1:], bf16), memory_space=pltpu.MemorySpace.HBM) for a in sums]

    early, late = (0,), tuple(range(1, nt))

    @pl.kernel(mesh=plsc.ScalarSubcoreMesh(axis_name="seq", num_cores=1), name="seq_rs_between_chips",
               scratch_types=(pltpu.SemaphoreType.DMA((6,)), pltpu.SemaphoreType.DMA((6,))),
               compiler_params=pltpu.CompilerParams(collective_id=3))
    def launch(send_sems, recv_sems):
        x, y, c = _place()
        first, second, diag = _ici_route(x, y, c)
        _handshake([first, second])

        def copies(k, src, dst, to, part=range(nt)):
            return [pltpu.make_async_remote_copy(src_ref=src(t), dst_ref=dst(t), send_sem=send_sems.at[k], recv_sem=recv_sems.at[k],
                                                 device_id=to, device_id_type=MESH) for t in part]

        chip_of = lambda p: 2 * p[0] + p[1]
        for_diag = lambda t: s_refs[t].at[chip_of(diag)]
        through = lambda t: transit[t]
        last = lambda t: outs[t].at[2]
        direct = (copies(0, lambda t: s_refs[t].at[chip_of(first)], lambda t: outs[t].at[0], first)
                  + copies(1, lambda t: s_refs[t].at[chip_of(second)], lambda t: outs[t].at[1], second))
        via = [copies(2, for_diag, through, first, early), copies(3, for_diag, through, first, late)]
        passed = [copies(4, through, last, second, early), copies(5, through, last, second, late)]
        for cp in via[0] + direct + via[1]:
            cp.start()
        for arrived, onward in zip(via, passed):
            for cp in arrived:
                cp.wait_recv()
            for cp in onward:
                cp.start()
        sent = direct + via[0] + via[1] + passed[0] + passed[1]
        for cp in direct + passed[0] + passed[1]:
            cp.wait_recv()
        for cp in sent:
            cp.wait_send()

    launch()
    return [o[...] for o in outs], [a[...] for a in s_refs]


def _chip_sums(core, mine, other, after, steps=2):
    nt, nchip = len(mine), mine[0].shape[0]
    m4 = [a.reshape(nchip, 2, -1, a.shape[-1]) for a in mine]
    o3 = [a.reshape(nchip, -1, a.shape[-1]) for a in other]

    def body(c_ref, *refs):
        del c_ref
        for a_ref, b_ref, o_ref in zip(refs[:nt], refs[nt:2 * nt], refs[2 * nt + 1:]):
            o_ref[...] = (a_ref[...].astype(f32) + b_ref[...].astype(f32)).astype(bf16)

    tiles = [(a.shape[1] // steps, a.shape[2]) for a in o3]
    blks = [pl.BlockSpec((None,) + t, lambda j, i, c_ref: (j, i, 0)) for t in tiles]
    outs = pl.pallas_call(
        body, name="chip_sums", out_shape=[SDS(a.shape, bf16) for a in o3], compiler_params=_cp("parallel", "parallel"),
        grid_spec=pltpu.PrefetchScalarGridSpec(
            num_scalar_prefetch=1, grid=(nchip, steps),
            in_specs=[pl.BlockSpec((None, None) + t, lambda j, i, c_ref: (j, c_ref[0], i, 0)) for t in tiles] + blks + [ANY],
            out_specs=blks),
    )(core, *m4, *o3, after)
    return [o.reshape(a.shape) for o, a in zip(outs, other)]


def _all_reduce_rows(pack):
    rows = pack.shape[0]
    blk = rows // N_DEV

    def body(in_ref, out_ref, land, send1, recv1, send2, recv2):
        x, y, c = _place()
        me = 4 * x + 2 * y + c
        others = [(px, py, pc) for px in range(2) for py in range(2) for pc in range(2)]

        def is_me(p):
            return jnp.logical_and(jnp.logical_and(p[0] == x, p[1] == y), p[2] == c)

        land[me] = in_ref[pl.ds(_al(me, blk), blk), :]
        for d, p in enumerate(others):
            @pl.when(jnp.logical_not(is_me(p)))
            def _():
                pltpu.make_async_remote_copy(src_ref=in_ref.at[pl.ds(d * blk, blk), :], dst_ref=land.at[me], send_sem=send1.at[d],
                                             recv_sem=recv1.at[me], device_id=p, device_id_type=MESH).start()
        for d, p in enumerate(others):
            @pl.when(jnp.logical_not(is_me(p)))
            def _():
                cp = pltpu.make_async_remote_copy(src_ref=in_ref.at[pl.ds(d * blk, blk), :], dst_ref=land.at[d], send_sem=send1.at[d],
                                                  recv_sem=recv1.at[d], device_id=p, device_id_type=MESH)
                cp.wait_recv()
                cp.wait_send()
        total = land[0]
        for d in range(1, N_DEV):
            total = total + land[d]
        out_ref[pl.ds(_al(me, blk), blk), :] = total
        for d, p in enumerate(others):
            @pl.when(jnp.logical_not(is_me(p)))
            def _():
                mine = out_ref.at[pl.ds(_al(me, blk), blk), :]
                pltpu.make_async_remote_copy(src_ref=mine, dst_ref=mine, send_sem=send2.at[d], recv_sem=recv2.at[me],
                                             device_id=p, device_id_type=MESH).start()
        for d, p in enumerate(others):
            @pl.when(jnp.logical_not(is_me(p)))
            def _():
                theirs = out_ref.at[pl.ds(d * blk, blk), :]
                cp = pltpu.make_async_remote_copy(src_ref=theirs, dst_ref=theirs, send_sem=send2.at[d], recv_sem=recv2.at[d],
                                                  device_id=p, device_id_type=MESH)
                cp.wait_recv()
                cp.wait_send()

    vm = pl.BlockSpec(memory_space=pltpu.VMEM)
    return pl.pallas_call(
        body, name="all_reduce_rows", in_specs=[vm], out_specs=vm, out_shape=SDS((rows, LANE), f32),
        scratch_shapes=[pltpu.VMEM((N_DEV, blk, LANE), f32)] + [pltpu.SemaphoreType.DMA((N_DEV,))] * 4,
        compiler_params=pltpu.CompilerParams(vmem_limit_bytes=VMEM_LIMIT),
    )(pack)


def _lower_bounds_fwd(lower):
    def body(l_ref, o_ref):
        sm = _layer_softmax(l_ref)
        run = jnp.zeros_like(sm[0])
        for l in range(DEPTH):
            o_ref[l:l + 1, :] = run
            if l + 1 < DEPTH:
                run = run + sm[l + 1]

    return pl.pallas_call(body, name="lower_bounds_fwd", out_shape=SDS(lower.shape, f32))(lower)


def _layer_softmax(l_ref):
    rows = [l_ref[l:l + 1, :] for l in range(DEPTH)]
    top = functools.reduce(jnp.maximum, rows)
    e = [jnp.exp(r - top) for r in rows]
    tot = functools.reduce(lambda a, b: a + b, e)
    return [v / tot for v in e]


def _lower_bounds_bwd(lower, dlbs):
    def body(l_ref, d_ref, o_ref):
        sm = _layer_softmax(l_ref)
        dsm = [None] * DEPTH
        run = jnp.zeros_like(sm[0])
        dsm[0] = run
        for l in reversed(range(1, DEPTH)):
            run = run + d_ref[l:l + 1, :]
            dsm[l] = run
        inner = functools.reduce(lambda a, b: a + b, [sm[l] * dsm[l] for l in range(DEPTH)])
        for l in range(DEPTH):
            o_ref[l:l + 1, :] = sm[l] * (dsm[l] - inner)

    return pl.pallas_call(body, name="lower_bounds_bwd", out_shape=SDS(lower.shape, f32))(lower, dlbs)


_ADAM_C1 = 1.0 - ADAM_B1 ** ADAM_STEP
_ADAM_C2 = 1.0 - ADAM_B2 ** ADAM_STEP


def _adamw(w, g, m, v):
    m = ADAM_B1 * m + (1.0 - ADAM_B1) * g
    v = ADAM_B2 * v + (1.0 - ADAM_B2) * (g * g)
    delta = -ADAM_LR * ((m / _ADAM_C1) / (jnp.sqrt(v / _ADAM_C2) + ADAM_EPS) + ADAM_WD * w)
    return delta, m, v


def _row_tile(rows, cap):
    return next(t for t in range(min(cap, rows) // 16 * 16, 0, -16) if rows % t == 0)


def _adam_big(where, names, w, m, v, sums, landed, outs, after, steps=4):
    nt = len(names)
    three = lambda a: a.reshape(a.shape[0], -1, a.shape[-1])
    w3, m3, v3 = ([three(d[n]) for n in names] for d in (w, m, v))
    outs3 = [three(a) for n in names for a in outs[n]]
    sums3 = [three(a) for a in sums]
    land3 = [three(a) for a in landed]

    def body(where_ref, *refs):
        del where_ref
        o_refs = refs[5 * nt + 4 * nt + 1:]
        for t in range(nt):
            w_ref, m_ref, v_ref, sum_ref, land_ref = (refs[q * nt + t] for q in range(5))
            g = sum_ref[...].astype(f32)
            for k in range(3):
                g = g + land_ref[k].astype(f32)
            delta, nm, nv = _adamw(w_ref[...], g, m_ref[...], v_ref[...])
            for o_ref, val in zip(o_refs[4 * t:4 * t + 4], (g, delta, nm, nv)):
                o_ref[...] = val

    tiles = [(a.shape[1] // steps, a.shape[2]) for a in w3]
    own = [pl.BlockSpec((None,) + t, lambda i, wh: (wh[0], i, 0)) for t in tiles]
    res = pl.pallas_call(
        body, name="adam_big", out_shape=[SDS(a.shape, f32) for a in outs3],
        input_output_aliases={1 + 5 * nt + i: i for i in range(4 * nt)}, compiler_params=_cp("parallel"),
        grid_spec=pltpu.PrefetchScalarGridSpec(
            num_scalar_prefetch=1, grid=(steps,),
            in_specs=own * 3 + [pl.BlockSpec((None,) + t, lambda i, wh: (wh[1], i, 0)) for t in tiles]
            + [pl.BlockSpec((3,) + t, lambda i, wh: (0, i, 0)) for t in tiles] + [ANY] * (4 * nt + 1),
            out_specs=[s for s in own for _ in range(4)]),
    )(where, *w3, *m3, *v3, *sums3, *land3, *outs3, after)
    return {n: [o.reshape(w[n].shape) for o in res[4 * t:4 * t + 4]] for t, n in enumerate(names)}


def _touch(a, after):
    a2 = a.reshape(-1, a.shape[-1])

    def body(a_ref, after_ref, o_ref):
        del after_ref
        o_ref[...] = a_ref[0:8, :].astype(f32)

    return pl.pallas_call(
        body, name="touch", grid=(1,), in_specs=[pl.BlockSpec((16, LANE), lambda i: (0, 0)), ANY],
        out_specs=pl.BlockSpec((8, LANE), lambda i: (0, 0)), out_shape=SDS((8, LANE), f32),
    )(a2, after)


def _adam_rows(w, g, m, v):
    def body(w_ref, g_ref, m_ref, v_ref, d_ref, nm_ref, nv_ref):
        delta, nm, nv = _adamw(w_ref[...], g_ref[...], m_ref[...], v_ref[...])
        d_ref[...] = delta
        nm_ref[...] = nm
        nv_ref[...] = nv

    return pl.pallas_call(body, name="adam_rows", out_shape=[SDS(w.shape, f32)] * 3)(w, g, m, v)


SMALL = ("g_mix", "lower_bounds", "g_hgrn_out", "w_conv", "sg_ln_g", "sg_ln_b", "w_sg", "b_sg", "g_ffn", "g_final")
WEIGHTS = ("w_in", "g_mix", "lower_bounds", "g_hgrn_out", "w_conv", "sg_ln_g", "sg_ln_b", "w_sg", "b_sg", "w_branch", "w_o", "g_ffn",
           "w_ff1", "w_ff2", "g_final")


def _pack_rows(arrays, multiple):
    flat = jnp.concatenate([a.reshape(-1) for a in arrays])
    rows = -(-flat.shape[0] // (LANE * multiple)) * multiple
    return jnp.pad(flat, (0, rows * LANE - flat.shape[0])).reshape(rows, LANE)


def _unpack_rows(pack, like):
    flat = pack.reshape(-1)
    out, at = [], 0
    for a in like:
        out.append(flat[at:at + a.size].reshape(a.shape))
        at += a.size
    return out


def kernel(x, w_in, g_mix, lower_bounds, g_hgrn_out, w_conv, sg_ln_g, sg_ln_b, w_sg, b_sg, w_branch, w_o, g_ffn, w_ff1, w_ff2, g_final, loss_target, m_w_in, m_g_mix, m_lower_bounds, m_g_hgrn_out, m_w_conv, m_sg_ln_g, m_sg_ln_b, m_w_sg, m_b_sg, m_w_branch, m_w_o, m_g_ffn, m_w_ff1, m_w_ff2, m_g_final, v_w_in, v_g_mix, v_lower_bounds, v_g_hgrn_out, v_w_conv, v_sg_ln_g, v_sg_ln_b, v_w_sg, v_b_sg, v_w_branch, v_w_o, v_g_ffn, v_w_ff1, v_w_ff2, v_g_final):
    weights = dict(w_in=w_in, g_mix=g_mix, lower_bounds=lower_bounds, g_hgrn_out=g_hgrn_out, w_conv=w_conv, sg_ln_g=sg_ln_g,
                   sg_ln_b=sg_ln_b, w_sg=w_sg, b_sg=b_sg, w_branch=w_branch, w_o=w_o, g_ffn=g_ffn, w_ff1=w_ff1, w_ff2=w_ff2, g_final=g_final)
    mom1 = dict(w_in=m_w_in, g_mix=m_g_mix, lower_bounds=m_lower_bounds, g_hgrn_out=m_g_hgrn_out, w_conv=m_w_conv, sg_ln_g=m_sg_ln_g,
                sg_ln_b=m_sg_ln_b, w_sg=m_w_sg, b_sg=m_b_sg, w_branch=m_w_branch, w_o=m_w_o, g_ffn=m_g_ffn, w_ff1=m_w_ff1, w_ff2=m_w_ff2,
                g_final=m_g_final)
    mom2 = dict(w_in=v_w_in, g_mix=v_g_mix, lower_bounds=v_lower_bounds, g_hgrn_out=v_g_hgrn_out, w_conv=v_w_conv, sg_ln_g=v_sg_ln_g,
                sg_ln_b=v_sg_ln_b, w_sg=v_w_sg, b_sg=v_b_sg, w_branch=v_w_branch, w_o=v_w_o, g_ffn=v_g_ffn, w_ff1=v_w_ff1, w_ff2=v_w_ff2,
                g_final=v_g_final)
    xi, yi, ci = _place()
    dev = 4 * xi + 2 * yi + ci
    conv_cols = w_conv.shape[-1]

    for d in (weights, mom1, mom2):
        d["w_in"] = jnp.swapaxes(d["w_in"], 1, 2)
    shards = {n: weights[n].astype(bf16) for n in BIG}

    conv_place = lax.dynamic_update_slice(jnp.zeros((DEPTH, 3, BRANCH), f32), w_conv, (0, 0, dev * conv_cols))
    (w_conv_full,) = _unpack_rows(_all_reduce_rows(_pack_rows([conv_place], 8 * N_DEV)), [conv_place])
    lbs = _lower_bounds_fwd(lower_bounds)

    def small_of(l):
        return dict(g_mix=g_mix[l][None], lb=lbs[l][None], g_out=g_hgrn_out[l][None], w_conv=w_conv_full[l], ln_g=sg_ln_g[l][None],
                    ln_b=sg_ln_b[l][None], w_sg=w_sg[l], b_sg_t=b_sg[l].T, g_ffn=g_ffn[l][None])

    act = x[0]
    full, saved = [], []
    shard_refs = [jax.new_ref(shards[n], memory_space=pltpu.MemorySpace.HBM) for n in BIG]
    shapes = _gather_out_shapes(shards)
    every = tuple(range(len(BIG)))
    rest = (3, 4, 1, 2)
    groups = [((0,), 1, "a"), (rest, 1, "b")] + [(every, 1, "")] * (DEPTH - 1)
    gathered = [_seq_all_gather_layer(max(i - 1, 0), which, n_early, [shard_refs[t] for t in which], [shapes[t] for t in which], tag)
                for i, (which, n_early, tag) in enumerate(groups)]

    def placed_weights(l, which, arrived, after):
        where = jnp.stack([jnp.int32(l), dev.astype(jnp.int32)])
        return dict(zip([BIG[t] for t in which], _place_own(where, which, [shards[BIG[t]] for t in which], arrived, after)))

    for l in range(DEPTH):
        if l == 0:
            w_l = placed_weights(0, (0,), gathered[0], act)
            p_h = _rms_mm(act, small_of(0)["g_mix"], w_l["w_in"])
            w_l.update(placed_weights(0, rest, gathered[1], p_h[0]))
        else:
            w_l, p_h = placed_weights(l, every, gathered[l + 1], act), None
        full.append(w_l)
        act, sv = _layer_fwd(act, w_l, small_of(l), p_h)
        saved.append(sv)
    loss_row, dx, dxb, dg_final = _final(act, loss_target[0], g_final[None])
    loss = lax.psum(loss_row[0, 0], ("x", "y", "c"))

    core = ci.astype(jnp.int32)[None]
    big_out = {n: [lax.empty(weights[n].shape, f32) for _ in range(4)] for n in BIG}
    small_grads = [None] * DEPTH

    def chip_sums(stage, after):
        l, received, mine = stage
        sums = _chip_sums(core, mine, received, after)
        placed.append(sums[BIG.index("w_o")])
        landed, sums = _seq_exchange_between_chips(sums)
        return l, sums, landed

    def adam_layer(stage, after):
        l, sums, landed = stage
        where = jnp.stack([jnp.int32(l), (2 * xi + yi).astype(jnp.int32)])
        big_out.update(_adam_big(where, BIG, weights, mom1, mom2, sums, landed, big_out, after))

    above = None
    placed = []
    for l in reversed(range(DEPTH)):
        summed = []

        def between(dx1):
            if above is None:
                return dx1
            summed.append(chip_sums(above, dx1))
            return placed[-1]

        def before_end(big):
            return _touch(summed[0][2][BIG.index("w_o")], big["w_in"]) if summed else big["w_in"]

        dx, dxb, big, small_grads[l] = _layer_bwd(dx, dxb, saved[l], full[l], small_of(l), between, before_end)
        if summed:
            adam_layer(summed[0], dx)
        above = (l, *_seq_exchange_on_chip([big[n] for n in BIG]))

    stack = lambda f: jnp.stack([f(small_grads[l]) for l in range(DEPTH)])
    d_lower = _lower_bounds_bwd(lower_bounds, stack(lambda s: s["vecs"][0]))
    local_small = dict(g_mix=stack(lambda s: s["g_mix"][0]), lower_bounds=d_lower, g_hgrn_out=stack(lambda s: s["vecs"][1]),
                       w_conv=stack(lambda s: s["vecs"][4:7]), sg_ln_g=stack(lambda s: s["vecs"][2]), sg_ln_b=stack(lambda s: s["vecs"][3]),
                       w_sg=stack(lambda s: s["w_sg"]), b_sg=stack(lambda s: s["b_sg_t"].T), g_ffn=stack(lambda s: s["g_ffn"][0]),
                       g_final=dg_final[0])
    order = [local_small[n] for n in SMALL]
    grads = dict(zip(SMALL, _unpack_rows(_all_reduce_rows(_pack_rows(order, 8 * N_DEV)), order)))
    grads["w_conv"] = lax.dynamic_slice(grads["w_conv"], (0, 0, dev * conv_cols), (DEPTH, 3, conv_cols))

    deltas, new_m, new_v = {}, {}, {}
    packs = [_pack_rows([d[n] for n in SMALL], 8) for d in (weights, grads, mom1, mom2)]
    like = [weights[n] for n in SMALL]
    small_out = _adam_rows(*packs)
    for out, pack in zip((deltas, new_m, new_v), small_out):
        out.update(zip(SMALL, _unpack_rows(pack, like)))
    adam_layer(chip_sums(above, dx), small_out[0])
    for n in BIG:
        grads[n], deltas[n], new_m[n], new_v[n] = (jnp.swapaxes(a, 1, 2) if n == "w_in" else a for a in big_out[n])

    return (loss, dx[None], *[grads[n] for n in WEIGHTS], *[deltas[n] for n in WEIGHTS], *[new_m[n] for n in WEIGHTS],
            *[new_v[n] for n in WEIGHTS])
```

```python
import functools

import jax
import jax.numpy as jnp
from jax import lax
from jax.experimental import pallas as pl
from jax.experimental.pallas import tpu as pltpu
from jax.experimental.pallas import tpu_sc as plsc

f32 = jnp.float32
bf16 = jnp.bfloat16
SDS = jax.ShapeDtypeStruct
MESH = pl.DeviceIdType.MESH

D_MODEL = 1024
BRANCH = 512
N_COLS = 7680
D_FF = 4096
DEPTH = 4
HEADS = 4
HEAD_DIM = 128
HGRN_CHUNK = 64
SG_CHUNK = 128
SG_GROUPS = 4
NORM_EPS = 1e-6
LN_EPS = 1e-5
LB_FLOOR = 1e-30
N_DEV = 8
SHARD_IN = N_COLS // N_DEV
WIN = 1024
LANE = 128
GATE_COL0 = 9 * BRANCH

ADAM_LR = 0.001
ADAM_B1 = 0.9
ADAM_B2 = 0.999
ADAM_EPS = 1e-08
ADAM_WD = 0.01
ADAM_STEP = 10

MIX_TILE = 256
VMEM_LIMIT = 56 * 1024 * 1024


def _cp(*sem):
    return pltpu.CompilerParams(dimension_semantics=sem or None, vmem_limit_bytes=VMEM_LIMIT)


def _dot(a, b):
    return jnp.dot(a, b, preferred_element_type=f32)


def _dot_nt(a, b):
    return lax.dot_general(a, b, (((1,), (1,)), ((), ())), preferred_element_type=f32)


def _dot_tn(a, b):
    return lax.dot_general(a, b, (((0,), (0,)), ((), ())), preferred_element_type=f32)


def _dot_exact(ones, b):
    hi = b.astype(bf16)
    rest = b - hi.astype(f32)
    mid = rest.astype(bf16)
    low = (rest - mid.astype(f32)).astype(bf16)
    ones = ones.astype(bf16)
    return _dot(ones, hi) + _dot(ones, mid) + _dot(ones, low)


def _sigmoid(x):
    return jax.nn.sigmoid(x)


_GELU_C = 0.7978845608028654
_GELU_A = 0.044715


def _gelu(x):
    return 0.5 * x * (1.0 + jnp.tanh(_GELU_C * (x + _GELU_A * x * x * x)))


def _gelu_grad(x):
    x2 = x * x
    t = jnp.tanh(_GELU_C * (x + _GELU_A * x * x2))
    return 0.5 * (1.0 + t) + 0.5 * x * (1.0 - t * t) * _GELU_C * (1.0 + 3.0 * _GELU_A * x2)


def _rms_stats(x):
    r = lax.rsqrt(jnp.mean(x * x, axis=-1, keepdims=True) + NORM_EPS)
    return r, x * r


def _rms_bwd(dh, xh, r, g):
    dg = jnp.sum(dh * xh, axis=0, keepdims=True)
    dxn = dh * g
    dx = r * (dxn - xh * jnp.mean(dxn * xh, axis=-1, keepdims=True))
    return dx, dg


def _tri(n, upper=False):
    r = lax.broadcasted_iota(jnp.int32, (n, n), 0)
    c = lax.broadcasted_iota(jnp.int32, (n, n), 1)
    return (c >= r) if upper else (c <= r)


def _acc_rows(ref, first, val):
    @pl.when(first)
    def _():
        ref[...] = val

    @pl.when(jnp.logical_not(first))
    def _():
        ref[...] += val


def _rms_mm(x, g, w_t, tm=1024, tn=1920):
    s, n = x.shape[0], w_t.shape[0]

    def body(x_ref, g_ref, w_ref, p_ref, h_ref, hs):
        @pl.when(pl.program_id(1) == 0)
        def _():
            _, xh = _rms_stats(x_ref[...])
            hv = (xh * g_ref[...]).astype(bf16)
            hs[...] = hv
            h_ref[...] = hv

        p_ref[...] = _dot_nt(hs[...], w_ref[...])

    return pl.pallas_call(
        body, name="rms_mm", grid=(s // tm, n // tn),
        in_specs=[pl.BlockSpec((tm, D_MODEL), lambda i, j: (i, 0)), pl.BlockSpec((1, D_MODEL), lambda i, j: (0, 0)),
                  pl.BlockSpec((tn, D_MODEL), lambda i, j: (j, 0))],
        out_specs=[pl.BlockSpec((tm, tn), lambda i, j: (i, j)), pl.BlockSpec((tm, D_MODEL), lambda i, j: (i, 0))],
        out_shape=[SDS((s, n), f32), SDS((s, D_MODEL), bf16)],
        scratch_shapes=[pltpu.VMEM((tm, D_MODEL), bf16)], compiler_params=_cp("parallel", "arbitrary"),
    )(x, g, w_t)


def _hgrn_gates(fp, lb):
    logf = jnp.logaddexp(jnp.log(jnp.maximum(lb, LB_FLOOR)), jnp.log1p(-lb) + jax.nn.log_sigmoid(fp))
    snf = _sigmoid(-fp)
    return logf, snf, (1.0 - lb) * snf


def _p_specs(tile, cols, row_map):
    return [pl.BlockSpec((tile, BRANCH), functools.partial(lambda c, i: (row_map(i), c), c)) for c in cols]


def _mixer_fwd(p, lb, gout, wconv, lng, lnb, wsg, bsg_t):
    s = p.shape[0]
    tt = MIX_TILE
    nch = tt // HGRN_CHUNK

    def body(q_ref, fp_ref, iv_ref, go_ref, bg_ref, cg_ref, xc_ref, u_ref, v_ref, lb_ref, gout_ref, wconv_ref, lng_ref,
             lnb_ref, wsg_ref, bsg_ref, z_ref, opre_ref, st_ref, st_scr, zbuf):
        @pl.when(pl.program_id(0) == 0)
        def _():
            st_scr[...] = jnp.zeros_like(st_scr)
            zbuf[0:8, :] = jnp.zeros((8, BRANCH), f32)

        lbv = lb_ref[...]
        q_raw = q_ref[...]
        qs = q_raw * _sigmoid(q_raw)
        logf, _, kk = _hgrn_gates(fp_ref[...], lbv)
        iv = iv_ref[...]
        causal = _tri(HGRN_CHUNK)
        tri = causal.astype(f32)
        last_row = lax.broadcasted_iota(jnp.int32, (HGRN_CHUNK, 1), 0) == HGRN_CHUNK - 1
        for c in range(nch):
            rows = slice(HGRN_CHUNK * c, HGRN_CHUNK * (c + 1))
            b = _dot_exact(tri, logf[rows])
            bl = jnp.sum(jnp.where(last_row, b, 0.0), axis=0, keepdims=True)
            qb = (qs[rows] * jnp.exp(b)).astype(bf16)
            kb = (kk[rows] * jnp.exp(-b)).astype(bf16)
            kd = (kk[rows] * jnp.exp(bl - b)).astype(bf16)
            ebl = jnp.exp(bl)
            vc = iv[rows].astype(bf16)
            for h in range(HEADS):
                sl = slice(HEAD_DIM * h, HEAD_DIM * (h + 1))
                st = st_scr[h]
                st_ref[c, h] = st
                a = jnp.where(causal, _dot_nt(qb[:, sl], kb[:, sl]), 0.0)
                opre_ref[rows, sl] = _dot(a.astype(bf16), vc[:, sl]) + _dot_nt(qb[:, sl], st.astype(bf16))
                st_scr[h] = st * ebl[:, sl] + _dot_tn(vc[:, sl], kd[:, sl])

        o = opre_ref[...]
        go = go_ref[...]
        gout_v = gout_ref[...]
        for h in range(HEADS):
            sl = slice(HEAD_DIM * h, HEAD_DIM * (h + 1))
            _, oh = _rms_stats(o[:, sl])
            z_ref[:, sl] = (oh * gout_v[:, sl] * _sigmoid(go[:, sl])).astype(bf16)

        zc = cg_ref[...] * xc_ref[...]
        zbuf[8:8 + tt, :] = zc
        y = wconv_ref[0:1, :] * zbuf[pl.ds(6, tt), :] + wconv_ref[1:2, :] * zbuf[pl.ds(7, tt), :] + wconv_ref[2:3, :] * zc
        z_ref[:, BRANCH:2 * BRANCH] = (bg_ref[...] * y).astype(bf16)
        zbuf[0:8, :] = zbuf[tt:tt + 8, :]

        ug = _gelu(u_ref[...])
        vg = _gelu(v_ref[...])
        vcen = vg - jnp.mean(vg, axis=-1, keepdims=True)
        rstd = lax.rsqrt(jnp.mean(vcen * vcen, axis=-1, keepdims=True) + LN_EPS)
        vn = (vcen * rstd * lng_ref[...] + lnb_ref[...]).astype(bf16)
        low = _tri(SG_CHUNK)
        for g in range(SG_GROUPS):
            sl = slice(LANE * g, LANE * (g + 1))
            wm = jnp.where(low, wsg_ref[g], 0.0).astype(bf16)
            bias = bsg_ref[:, g:g + 1]
            for cc in range(tt // SG_CHUNK):
                rows = slice(SG_CHUNK * cc, SG_CHUNK * (cc + 1))
                sv = _dot(wm, vn[rows, sl]) + bias
                z_ref[rows, 2 * BRANCH + LANE * g:2 * BRANCH + LANE * (g + 1)] = (ug[rows, sl] * sv).astype(bf16)

    full = lambda shape: pl.BlockSpec(shape, lambda i: (0,) * len(shape))
    return pl.pallas_call(
        body, name="mixer_fwd", grid=(s // tt,),
        in_specs=_p_specs(tt, range(9), lambda i: i) + [full((1, BRANCH)), full((1, BRANCH)), full((3, BRANCH)), full((1, BRANCH)),
                                                        full((1, BRANCH)), full((SG_GROUPS, SG_CHUNK, SG_CHUNK)), full((SG_CHUNK, SG_GROUPS))],
        out_specs=[pl.BlockSpec((tt, 3 * BRANCH), lambda i: (i, 0)), pl.BlockSpec((tt, BRANCH), lambda i: (i, 0)),
                   pl.BlockSpec((nch, HEADS, HEAD_DIM, HEAD_DIM), lambda i: (i, 0, 0, 0))],
        out_shape=[SDS((s, 3 * BRANCH), bf16), SDS((s, BRANCH), f32), SDS((s // HGRN_CHUNK, HEADS, HEAD_DIM, HEAD_DIM), f32)],
        scratch_shapes=[pltpu.VMEM((HEADS, HEAD_DIM, HEAD_DIM), f32), pltpu.VMEM((tt + 8, BRANCH), f32)],
        compiler_params=_cp("arbitrary"),
    )(*([p] * 9), lb, gout, wconv, lng, lnb, wsg, bsg_t)


def _branch_gate(z, wb, p, x, wo, tm=256):
    s = z.shape[0]
    half = 3 * D_MODEL // 2

    def body(z_ref, wb_ref, ga_ref, gb_ref, x_ref, wo_ref, y_ref, m_ref, x1_ref):
        ga, gb = ga_ref[...], gb_ref[...]
        gates = [ga[:, :D_MODEL], jnp.concatenate([ga[:, D_MODEL:], gb[:, :D_MODEL // 2]], axis=1), gb[:, D_MODEL // 2:]]
        acc = None
        for n in range(3):
            yn = _dot(z_ref[:, BRANCH * n:BRANCH * (n + 1)], wb_ref[n])
            y_ref[:, D_MODEL * n:D_MODEL * (n + 1)] = yn.astype(bf16)
            t = _sigmoid(gates[n]) * yn
            acc = t if acc is None else acc + t
        merged = acc.astype(bf16)
        m_ref[...] = merged
        x1_ref[...] = x_ref[...] + _dot(merged, wo_ref[...])

    blk0 = GATE_COL0 // half
    row = pl.BlockSpec((tm, D_MODEL), lambda i: (i, 0))
    return pl.pallas_call(
        body, name="branch_gate", grid=(s // tm,),
        in_specs=[pl.BlockSpec((tm, 3 * BRANCH), lambda i: (i, 0)), pl.BlockSpec((3, BRANCH, D_MODEL), lambda i: (0, 0, 0)),
                  pl.BlockSpec((tm, half), lambda i: (i, blk0)), pl.BlockSpec((tm, half), lambda i: (i, blk0 + 1)), row,
                  pl.BlockSpec((D_MODEL, D_MODEL), lambda i: (0, 0))],
        out_specs=[pl.BlockSpec((tm, 3 * D_MODEL), lambda i: (i, 0)), row, row],
        out_shape=[SDS((s, 3 * D_MODEL), bf16), SDS((s, D_MODEL), bf16), SDS((s, D_MODEL), f32)], compiler_params=_cp("parallel"),
    )(z, wb, p, p, x, wo)


def _ffn(x1, g, w1, w2, tm=512, tf=1024):
    s = x1.shape[0]
    nf = D_FF // tf

    def body(x_ref, g_ref, w1_ref, w2_ref, o_ref, h_ref, ra_ref, hs, acc):
        f = pl.program_id(1)

        @pl.when(f == 0)
        def _():
            _, xh = _rms_stats(x_ref[...])
            hv = (xh * g_ref[...]).astype(bf16)
            hs[...] = hv
            h_ref[...] = hv
            acc[...] = jnp.zeros_like(acc)

        ra = jnp.maximum(_dot(hs[...], w1_ref[...]), 0.0)
        ra_ref[...] = ra.astype(bf16)
        acc[...] += _dot((ra * ra).astype(bf16), w2_ref[...])

        @pl.when(f == nf - 1)
        def _():
            o_ref[...] = x_ref[...] + acc[...]

    return pl.pallas_call(
        body, name="ffn", grid=(s // tm, nf),
        in_specs=[pl.BlockSpec((tm, D_MODEL), lambda i, f: (i, 0)), pl.BlockSpec((1, D_MODEL), lambda i, f: (0, 0)),
                  pl.BlockSpec((D_MODEL, tf), lambda i, f: (0, f)), pl.BlockSpec((tf, D_MODEL), lambda i, f: (f, 0))],
        out_specs=[pl.BlockSpec((tm, D_MODEL), lambda i, f: (i, 0)), pl.BlockSpec((tm, D_MODEL), lambda i, f: (i, 0)),
                   pl.BlockSpec((tm, tf), lambda i, f: (i, f))],
        out_shape=[SDS((s, D_MODEL), f32), SDS((s, D_MODEL), bf16), SDS((s, D_FF), bf16)],
        scratch_shapes=[pltpu.VMEM((tm, D_MODEL), bf16), pltpu.VMEM((tm, D_MODEL), f32)], compiler_params=_cp("parallel", "arbitrary"),
    )(x1, g, w1, w2)


def _final(x, target, g, tm=512):
    s = x.shape[0]

    def body(x_ref, t_ref, g_ref, loss_ref, dx_ref, dxb_ref, dg_ref):
        first = pl.program_id(0) == 0
        gv = g_ref[...]
        r, xh = _rms_stats(x_ref[...])
        e = xh * gv - t_ref[...]
        tile_loss = 0.5 * jnp.sum(jnp.mean(e * e, axis=-1, keepdims=True), axis=0, keepdims=True)
        dx, dg = _rms_bwd(e * (1.0 / D_MODEL), xh, r, gv)
        dx_ref[...] = dx
        dxb_ref[...] = dx.astype(bf16)
        _acc_rows(dg_ref, first, dg)
        _acc_rows(loss_ref, first, jnp.broadcast_to(tile_loss, (1, LANE)))

    row = pl.BlockSpec((tm, D_MODEL), lambda i: (i, 0))
    return pl.pallas_call(
        body, name="final_loss", grid=(s // tm,), in_specs=[row, row, pl.BlockSpec((1, D_MODEL), lambda i: (0, 0))],
        out_specs=[pl.BlockSpec((1, LANE), lambda i: (0, 0)), row, row, pl.BlockSpec((1, D_MODEL), lambda i: (0, 0))],
        out_shape=[SDS((1, LANE), f32), SDS((s, D_MODEL), f32), SDS((s, D_MODEL), bf16), SDS((1, D_MODEL), f32)],
        compiler_params=_cp("arbitrary"),
    )(x, target, g)


def _ffn_bwd(dx2, dx2b, x1, g, ra, w1, w2, tm=512, tf=1024):
    s = x1.shape[0]
    nf = D_FF // tf

    def body(dx_ref, dxb_ref, x_ref, g_ref, ra_ref, w1_ref, w2_ref, da_ref, dx1_ref, dx1b_ref, dg_ref, acc):
        i, f = pl.program_id(0), pl.program_id(1)

        @pl.when(f == 0)
        def _():
            acc[...] = jnp.zeros_like(acc)

        da = (_dot_nt(dxb_ref[...], w2_ref[...]) * (2.0 * ra_ref[...].astype(f32))).astype(bf16)
        da_ref[...] = da
        acc[...] += _dot_nt(da, w1_ref[...])

        @pl.when(f == nf - 1)
        def _():
            r, xh = _rms_stats(x_ref[...])
            dx, dg = _rms_bwd(acc[...], xh, r, g_ref[...])
            dx = dx + dx_ref[...]
            dx1_ref[...] = dx
            dx1b_ref[...] = dx.astype(bf16)
            _acc_rows(dg_ref, i == 0, dg)

    row = pl.BlockSpec((tm, D_MODEL), lambda i, f: (i, 0))
    col = pl.BlockSpec((tm, tf), lambda i, f: (i, f))
    return pl.pallas_call(
        body, name="ffn_bwd", grid=(s // tm, nf),
        in_specs=[row, row, row, pl.BlockSpec((1, D_MODEL), lambda i, f: (0, 0)), col,
                  pl.BlockSpec((D_MODEL, tf), lambda i, f: (0, f)), pl.BlockSpec((tf, D_MODEL), lambda i, f: (f, 0))],
        out_specs=[col, row, row, pl.BlockSpec((1, D_MODEL), lambda i, f: (0, 0))],
        out_shape=[SDS((s, D_FF), bf16), SDS((s, D_MODEL), f32), SDS((s, D_MODEL), bf16), SDS((1, D_MODEL), f32)],
        scratch_shapes=[pltpu.VMEM((tm, D_MODEL), f32)], compiler_params=_cp("arbitrary", "arbitrary"),
    )(dx2, dx2b, x1, g, ra, w1, w2)


def _mm_tn(a, b, nb, m, n, tm, tn, name="mm_tn", rows=None, row0=0, into=None, square_a=False):
    s = a.shape[0]
    mi, nj = m // tm, n // tn
    rows = m if rows is None else rows
    blk0 = row0 // tm

    def body(a_ref, b_ref, *rest):
        av = a_ref[...]
        if square_a:
            av = av.astype(f32)
            av = (av * av).astype(bf16)
        rest[-1][...] = _dot_tn(av, b_ref[...]).astype(bf16)

    extra = {} if into is None else dict(input_output_aliases={2: 0})
    return pl.pallas_call(
        body, name=name, grid=(nb, mi, nj),
        in_specs=[pl.BlockSpec((s, tm), lambda k, i, j: (0, k * mi + i)), pl.BlockSpec((s, tn), lambda k, i, j: (0, k * nj + j))]
        + ([] if into is None else [pl.BlockSpec(memory_space=pl.ANY)]),
        out_specs=pl.BlockSpec((None, tm, tn), lambda k, i, j: (k, blk0 + i, j)), out_shape=SDS((nb, rows, n), bf16),
        compiler_params=_cp("parallel", "parallel", "parallel"), **extra,
    )(a, b, *([] if into is None else [into]))


def _mm_tn_slabs(a, b, nb, m, nblk, rel, width, tm=512, name="mm_tn_slabs"):
    s = a.shape[0]
    n = b.shape[1] // nb
    ng, mi, nw = n // nblk, m // tm, len(rel)

    def body(a_ref, b_ref, o_ref):
        full = _dot_tn(a_ref[...], b_ref[...])
        for r, start in enumerate(rel):
            o_ref[r] = full[:, start:start + width].astype(bf16)

    return pl.pallas_call(
        body, name=name, grid=(nb, ng, mi),
        in_specs=[pl.BlockSpec((s, tm), lambda k, g, i: (0, k * mi + i)), pl.BlockSpec((s, nblk), lambda k, g, i: (0, k * ng + g))],
        out_specs=pl.BlockSpec((nw, None, tm, width), lambda k, g, i: (g, k, i, 0)), out_shape=SDS((ng * nw, nb, m, width), bf16),
        compiler_params=_cp("parallel", "parallel", "parallel"),
    )(a, b)


def _merge_bwd(dx1b, wo, y, p, wb, after, tm=256):
    s = dx1b.shape[0]
    half = 3 * D_MODEL // 2
    blk0 = GATE_COL0 // half

    def body(dx_ref, wo_ref, y_ref, ga_ref, gb_ref, wb_ref, after_ref, dy_ref, dg_ref, dz_ref):
        del after_ref
        dm = _dot_nt(dx_ref[...], wo_ref[...])
        ga, gb = ga_ref[...], gb_ref[...]
        gates = [ga[:, :D_MODEL], jnp.concatenate([ga[:, D_MODEL:], gb[:, :D_MODEL // 2]], axis=1), gb[:, D_MODEL // 2:]]
        for n in range(3):
            cols = slice(D_MODEL * n, D_MODEL * (n + 1))
            gate = _sigmoid(gates[n])
            t = dm * gate
            dy = t.astype(bf16)
            dy_ref[:, cols] = dy
            dg_ref[:, cols] = (t * y_ref[:, cols].astype(f32) * (1.0 - gate)).astype(bf16)
            dz_ref[:, BRANCH * n:BRANCH * (n + 1)] = _dot_nt(dy, wb_ref[n]).astype(bf16)

    wide = pl.BlockSpec((tm, 3 * D_MODEL), lambda i: (i, 0))
    return pl.pallas_call(
        body, name="merge_bwd", grid=(s // tm,),
        in_specs=[pl.BlockSpec((tm, D_MODEL), lambda i: (i, 0)), pl.BlockSpec((D_MODEL, D_MODEL), lambda i: (0, 0)), wide,
                  pl.BlockSpec((tm, half), lambda i: (i, blk0)), pl.BlockSpec((tm, half), lambda i: (i, blk0 + 1)),
                  pl.BlockSpec((3, BRANCH, D_MODEL), lambda i: (0, 0, 0)), pl.BlockSpec(memory_space=pl.ANY)],
        out_specs=[wide, wide, pl.BlockSpec((tm, 3 * BRANCH), lambda i: (i, 0))],
        out_shape=[SDS((s, 3 * D_MODEL), bf16), SDS((s, 3 * D_MODEL), bf16), SDS((s, 3 * BRANCH), bf16)],
        compiler_params=_cp("parallel"),
    )(dx1b, wo, y, p, p, wb, after)


def _mixer_bwd(p, dz, opre, states, lb, gout, wconv, lng, lnb, wsg, bsg_t):
    s = p.shape[0]
    tt = MIX_TILE
    nt = s // tt
    nch = tt // HGRN_CHUNK
    rev = lambda i: nt - 1 - i

    def body(q_ref, fp_ref, iv_ref, go_ref, bg_ref, cg_ref, xc_ref, u_ref, v_ref, cgp_ref, xcp_ref, dz_ref, opre_ref, st_ref,
             lb_ref, gout_ref, wconv_ref, lng_ref, lnb_ref, wsg_ref, bsg_ref,
             dp_ref, vec_ref, dwsg_ref, dbsg_ref, dst_scr, zbuf, dybuf, dvn_scr, dbsg_acc):
        i = pl.program_id(0)

        @pl.when(i == 0)
        def _():
            dst_scr[...] = jnp.zeros_like(dst_scr)
            dybuf[tt:tt + 8, :] = jnp.zeros((8, BRANCH), f32)
            vec_ref[...] = jnp.zeros_like(vec_ref)
            dwsg_ref[...] = jnp.zeros_like(dwsg_ref)
            dbsg_acc[...] = jnp.zeros_like(dbsg_acc)

        lbv = lb_ref[...]
        q_raw, fp = q_ref[...], fp_ref[...]
        sq = _sigmoid(q_raw)
        qs = q_raw * sq
        sfp = _sigmoid(fp)
        logf, snf, kk = _hgrn_gates(fp, lbv)
        inv_f = jnp.exp(-logf)
        iv = iv_ref[...]
        doa = dz_ref[:, 0:BRANCH].astype(f32)
        o = opre_ref[...]
        sgo = _sigmoid(go_ref[...])
        gout_v = gout_ref[...]
        d_o, dgo, dgout = [], [], []
        for h in range(HEADS):
            sl = slice(HEAD_DIM * h, HEAD_DIM * (h + 1))
            r, oh = _rms_stats(o[:, sl])
            d_on = doa[:, sl] * sgo[:, sl]
            dgo.append(doa[:, sl] * oh * gout_v[:, sl] * sgo[:, sl] * (1.0 - sgo[:, sl]))
            dx, dg = _rms_bwd(d_on, oh, r, gout_v[:, sl])
            d_o.append(dx)
            dgout.append(dg)
        d_o = jnp.concatenate(d_o, axis=1)
        dp_ref[:, 3 * BRANCH:4 * BRANCH] = jnp.concatenate(dgo, axis=1).astype(bf16)
        vec_ref[1:2, :] += jnp.concatenate(dgout, axis=1)

        causal = _tri(HGRN_CHUNK)
        tri = causal.astype(f32)
        tri_up = _tri(HGRN_CHUNK, upper=True).astype(f32)
        last_row = lax.broadcasted_iota(jnp.int32, (HGRN_CHUNK, 1), 0) == HGRN_CHUNK - 1
        lb_live = (lbv > LB_FLOOR).astype(f32)
        dlb = jnp.zeros((1, BRANCH), f32)
        for c in reversed(range(nch)):
            rows = slice(HGRN_CHUNK * c, HGRN_CHUNK * (c + 1))
            b = _dot_exact(tri, logf[rows])
            bl = jnp.sum(jnp.where(last_row, b, 0.0), axis=0, keepdims=True)
            eb, enb, edl, ebl = jnp.exp(b), jnp.exp(-b), jnp.exp(bl - b), jnp.exp(bl)
            qbf, kbf, kdf = qs[rows] * eb, kk[rows] * enb, kk[rows] * edl
            qb, kb, kd = qbf.astype(bf16), kbf.astype(bf16), kdf.astype(bf16)
            vc = iv[rows].astype(bf16)
            dob = d_o[rows].astype(bf16)
            dv, dqb, dkb, dkd, debl = [], [], [], [], []
            for h in range(HEADS):
                sl = slice(HEAD_DIM * h, HEAD_DIM * (h + 1))
                st = st_ref[c, h]
                dst = dst_scr[h]
                stb, dstb = st.astype(bf16), dst.astype(bf16)
                a = jnp.where(causal, _dot_nt(qb[:, sl], kb[:, sl]), 0.0).astype(bf16)
                da = jnp.where(causal, _dot_nt(dob[:, sl], vc[:, sl]), 0.0).astype(bf16)
                dv.append(_dot_tn(a, dob[:, sl]) + _dot_nt(kd[:, sl], dstb))
                dqb.append(_dot(dob[:, sl], stb) + _dot(da, kb[:, sl]))
                dkb.append(_dot_tn(da, qb[:, sl]))
                dkd.append(_dot(vc[:, sl], dstb))
                debl.append(jnp.sum(st * dst, axis=0, keepdims=True))
                dst_scr[h] = _dot_tn(dob[:, sl], qb[:, sl]) + dst * ebl[:, sl]
            dv, dqb, dkb, dkd = (jnp.concatenate(t, axis=1) for t in (dv, dqb, dkb, dkd))
            debl = jnp.concatenate(debl, axis=1)
            t_kd = dkd * kdf
            dbl = ebl * debl + jnp.sum(t_kd, axis=0, keepdims=True)
            db = dqb * qbf - dkb * kbf - t_kd + jnp.where(last_row, dbl, 0.0)
            dkk = dkb * enb + dkd * edl
            dlc = _dot_exact(tri_up, db)
            sq_c, q_c, sfp_c, snf_c, invf_c = sq[rows], q_raw[rows], sfp[rows], snf[rows], inv_f[rows]
            slope = (1.0 - lbv) * sfp_c * snf_c
            dp_ref[rows, 0:BRANCH] = (dqb * eb * sq_c * (1.0 + q_c * (1.0 - sq_c))).astype(bf16)
            dp_ref[rows, BRANCH:2 * BRANCH] = (slope * (dlc * invf_c - dkk)).astype(bf16)
            dp_ref[rows, 2 * BRANCH:3 * BRANCH] = dv.astype(bf16)
            dlb = dlb + jnp.sum(dlc * (lb_live - sfp_c) * invf_c - dkk * snf_c, axis=0, keepdims=True)
        vec_ref[0:1, :] += dlb

        dob_ = dz_ref[:, BRANCH:2 * BRANCH].astype(f32)
        bg, cg, xc = bg_ref[...], cg_ref[...], xc_ref[...]
        zc = cg * xc
        zbuf[0:8, :] = jnp.where(i < nt - 1, cgp_ref[...] * xcp_ref[...], 0.0)
        zbuf[8:8 + tt, :] = zc
        w0, w1, w2 = wconv_ref[0:1, :], wconv_ref[1:2, :], wconv_ref[2:3, :]
        y = w0 * zbuf[pl.ds(6, tt), :] + w1 * zbuf[pl.ds(7, tt), :] + w2 * zc
        dy = dob_ * bg
        dybuf[0:tt, :] = dy
        dy1, dy2 = dybuf[pl.ds(1, tt), :], dybuf[pl.ds(2, tt), :]
        dzc = w2 * dy + w1 * dy1 + w0 * dy2
        dp_ref[:, 4 * BRANCH:5 * BRANCH] = (dob_ * y).astype(bf16)
        dp_ref[:, 5 * BRANCH:6 * BRANCH] = (dzc * xc).astype(bf16)
        dp_ref[:, 6 * BRANCH:7 * BRANCH] = (dzc * cg).astype(bf16)
        vec_ref[4:5, :] += jnp.sum(zc * dy2, axis=0, keepdims=True)
        vec_ref[5:6, :] += jnp.sum(zc * dy1, axis=0, keepdims=True)
        vec_ref[6:7, :] += jnp.sum(zc * dy, axis=0, keepdims=True)
        dybuf[tt:tt + 8, :] = dybuf[0:8, :]

        doc = dz_ref[:, 2 * BRANCH:3 * BRANCH].astype(f32)
        u_raw, v_raw = u_ref[...], v_ref[...]
        ug = _gelu(u_raw)
        dug_scale = _gelu_grad(u_raw)
        vg = _gelu(v_raw)
        vcen = vg - jnp.mean(vg, axis=-1, keepdims=True)
        rstd = lax.rsqrt(jnp.mean(vcen * vcen, axis=-1, keepdims=True) + LN_EPS)
        vhat = vcen * rstd
        lng_v = lng_ref[...]
        vn = (vhat * lng_v + lnb_ref[...]).astype(bf16)
        low = _tri(SG_CHUNK)
        for g in range(SG_GROUPS):
            sl = slice(LANE * g, LANE * (g + 1))
            wm = jnp.where(low, wsg_ref[g], 0.0).astype(bf16)
            bias = bsg_ref[:, g:g + 1]
            dw = jnp.zeros((SG_CHUNK, SG_CHUNK), f32)
            dbs = jnp.zeros((SG_CHUNK, LANE), f32)
            for cc in range(tt // SG_CHUNK):
                rows = slice(SG_CHUNK * cc, SG_CHUNK * (cc + 1))
                vn_c = vn[rows, sl]
                sv = _dot(wm, vn_c) + bias
                doc_c = doc[rows, sl]
                dp_ref[rows, 7 * BRANCH + LANE * g:7 * BRANCH + LANE * (g + 1)] = (doc_c * sv * dug_scale[rows, sl]).astype(bf16)
                dsv = doc_c * ug[rows, sl]
                dsvb = dsv.astype(bf16)
                dbs = dbs + dsv
                dw = dw + _dot_nt(dsvb, vn_c)
                dvn_scr[rows, sl] = _dot_tn(wm, dsvb)
            dwsg_ref[g] += jnp.where(low, dw, 0.0)
            dbsg_acc[:, sl] += dbs
        dvn = dvn_scr[...]
        vec_ref[2:3, :] += jnp.sum(dvn * vhat, axis=0, keepdims=True)
        vec_ref[3:4, :] += jnp.sum(dvn, axis=0, keepdims=True)
        dvh = dvn * lng_v
        dvg = rstd * (dvh - jnp.mean(dvh, axis=-1, keepdims=True) - vhat * jnp.mean(dvh * vhat, axis=-1, keepdims=True))
        dp_ref[:, 8 * BRANCH:9 * BRANCH] = (dvg * _gelu_grad(v_raw)).astype(bf16)

        @pl.when(i == nt - 1)
        def _():
            for g in range(SG_GROUPS):
                dbsg_ref[:, g:g + 1] = jnp.sum(dbsg_acc[:, LANE * g:LANE * (g + 1)], axis=1, keepdims=True)

    full = lambda shape: pl.BlockSpec(shape, lambda i: (0,) * len(shape))
    tail = lambda c: pl.BlockSpec((8, BRANCH), lambda i: (jnp.maximum(rev(i) * (tt // 8) - 1, 0), c))
    return pl.pallas_call(
        body, name="mixer_bwd", grid=(nt,),
        in_specs=_p_specs(tt, range(9), rev) + [tail(5), tail(6), pl.BlockSpec((tt, 3 * BRANCH), lambda i: (rev(i), 0)),
                                                pl.BlockSpec((tt, BRANCH), lambda i: (rev(i), 0)),
                                                pl.BlockSpec((nch, HEADS, HEAD_DIM, HEAD_DIM), lambda i: (rev(i), 0, 0, 0)),
                                                full((1, BRANCH)), full((1, BRANCH)), full((3, BRANCH)), full((1, BRANCH)), full((1, BRANCH)),
                                                full((SG_GROUPS, SG_CHUNK, SG_CHUNK)), full((SG_CHUNK, SG_GROUPS))],
        out_specs=[pl.BlockSpec((tt, 9 * BRANCH), lambda i: (rev(i), 0)), full((8, BRANCH)), full((SG_GROUPS, SG_CHUNK, SG_CHUNK)),
                   full((SG_CHUNK, SG_GROUPS))],
        out_shape=[SDS((s, 9 * BRANCH), bf16), SDS((8, BRANCH), f32), SDS((SG_GROUPS, SG_CHUNK, SG_CHUNK), f32), SDS((SG_CHUNK, SG_GROUPS), f32)],
        scratch_shapes=[pltpu.VMEM((HEADS, HEAD_DIM, HEAD_DIM), f32), pltpu.VMEM((tt + 8, BRANCH), f32), pltpu.VMEM((tt + 8, BRANCH), f32),
                        pltpu.VMEM((tt, BRANCH), f32), pltpu.VMEM((SG_CHUNK, BRANCH), f32)],
        compiler_params=_cp("arbitrary"),
    )(*([p] * 11), dz, opre, states, lb, gout, wconv, lng, lnb, wsg, bsg_t)


def _dh_bwd(dpm, dpg, w_t, x, dx1, g, after, tm=1024, tk=1536):
    s = x.shape[0]
    km = dpm.shape[1] // tk
    nk = km + dpg.shape[1] // tk

    def body(dpm_ref, dpg_ref, w_ref, x_ref, dx1_ref, g_ref, after_ref, dx_ref, dxb_ref, dg_ref, acc):
        del after_ref
        i, k = pl.program_id(0), pl.program_id(1)

        @pl.when(k == 0)
        def _():
            acc[...] = jnp.zeros_like(acc)

        @pl.when(k < km)
        def _():
            acc[...] += _dot(dpm_ref[...], w_ref[...])

        @pl.when(k >= km)
        def _():
            acc[...] += _dot(dpg_ref[...], w_ref[...])

        @pl.when(k == nk - 1)
        def _():
            r, xh = _rms_stats(x_ref[...])
            dx, dg = _rms_bwd(acc[...], xh, r, g_ref[...])
            dx = dx + dx1_ref[...]
            dx_ref[...] = dx
            dxb_ref[...] = dx.astype(bf16)
            _acc_rows(dg_ref, i == 0, dg)

    row = pl.BlockSpec((tm, D_MODEL), lambda i, k: (i, 0))
    vec = pl.BlockSpec((1, D_MODEL), lambda i, k: (0, 0))
    return pl.pallas_call(
        body, name="dh_bwd", grid=(s // tm, nk),
        in_specs=[pl.BlockSpec((tm, tk), lambda i, k: (i, jnp.minimum(k, km - 1))),
                  pl.BlockSpec((tm, tk), lambda i, k: (i, jnp.maximum(k - km, 0))),
                  pl.BlockSpec((tk, D_MODEL), lambda i, k: (k, 0)), row, row, vec, pl.BlockSpec(memory_space=pl.ANY)],
        out_specs=[row, row, vec], out_shape=[SDS((s, D_MODEL), f32), SDS((s, D_MODEL), bf16), SDS((1, D_MODEL), f32)],
        scratch_shapes=[pltpu.VMEM((tm, D_MODEL), f32)], compiler_params=_cp("arbitrary", "arbitrary"),
    )(dpm, dpg, w_t, x, dx1, g, after)


def _layer_fwd(x, weight, sm):
    p, h = _rms_mm(x, sm["g_mix"], weight("w_in", x))
    z, opre, states = _mixer_fwd(p, sm["lb"], sm["g_out"], sm["w_conv"], sm["ln_g"], sm["ln_b"], sm["w_sg"], sm["b_sg_t"])
    y, merged, x1 = _branch_gate(z, weight("w_branch", z), p, x, weight("w_o", z))
    x2, h2, ra = _ffn(x1, sm["g_ffn"], weight("w_ff1", x1), weight("w_ff2", x1))
    saved = dict(x=x, p=p, h=h, z=z, opre=opre, states=states, y=y, merged=merged, x1=x1, h2=h2, ra=ra)
    return x2, saved


def _layer_bwd(dx2, dx2b, sv, w, sm, between, before_end):
    nchip = N_DEV // 2
    by_chip = lambda g: g.reshape((nchip, 2) + g.shape[1:])
    da, dx1, dx1b, dg_ffn = _ffn_bwd(dx2, dx2b, sv["x1"], sm["g_ffn"], sv["ra"], w["w_ff1"], w["w_ff2"])
    g_ff2 = by_chip(_mm_tn(sv["ra"], dx2b, 1, D_FF, D_MODEL, 512, 1024, name="dw_ff2", square_a=True)[0]
                    .reshape(N_DEV, D_FF // N_DEV, D_MODEL))
    g_ff1 = by_chip(_mm_tn_slabs(sv["h2"], da, 1, D_MODEL, D_FF // 2, [i * (D_FF // N_DEV) for i in range(nchip)], D_FF // N_DEV,
                                 name="dw_ff1")[:, 0])
    g_o = by_chip(_mm_tn(sv["merged"], dx1b, 1, D_MODEL, D_MODEL, 512, 1024, name="dw_o")[0].reshape(N_DEV, D_MODEL // N_DEV, D_MODEL))
    dy, dpg, dz = _merge_bwd(dx1b, w["w_o"], sv["y"], sv["p"], w["w_branch"], between(dx1))
    g_branch = by_chip(_mm_tn_slabs(sv["z"], dy, 3, BRANCH, D_MODEL, [i * (D_MODEL // N_DEV) for i in range(N_DEV)], D_MODEL // N_DEV,
                                    name="dw_branch"))
    g_in = _mm_tn(dpg, sv["h"], 1, 3 * D_MODEL, D_MODEL, 512, 1024, name="dw_in_gates", rows=N_COLS, row0=GATE_COL0)
    dpm, vecs, dwsg, dbsg_t = _mixer_bwd(sv["p"], dz, sv["opre"], sv["states"], sm["lb"], sm["g_out"], sm["w_conv"],
                                         sm["ln_g"], sm["ln_b"], sm["w_sg"], sm["b_sg_t"])
    g_in = _mm_tn(dpm, sv["h"], 1, GATE_COL0, D_MODEL, 512, 1024, name="dw_in_mixers", rows=N_COLS, into=g_in)
    g_in = by_chip(g_in[0].reshape(N_DEV, SHARD_IN, D_MODEL))
    big = dict(w_in=g_in, w_branch=g_branch, w_o=g_o, w_ff1=g_ff1, w_ff2=g_ff2)
    dx, dxb, dg_mix = _dh_bwd(dpm, dpg, w["w_in"], sv["x"], dx1, sm["g_mix"], before_end(big))
    small = dict(g_mix=dg_mix, g_ffn=dg_ffn, vecs=vecs, w_sg=dwsg, b_sg_t=dbsg_t, dx1=dx1)
    return dx, dxb, big, small


BIG = ("w_in", "w_branch", "w_o", "w_ff1", "w_ff2")
ANY = pl.BlockSpec(memory_space=pl.ANY)


def _place():
    return lax.axis_index("x"), lax.axis_index("y"), lax.axis_index("c")


def _al(v, m):
    return pl.multiple_of(v * m, m)


def _shard_of(refs, dev, which=range(len(BIG))):
    out = []
    for ref, t in zip(refs, which):
        by_cols = BIG[t] in ("w_branch", "w_ff1")
        n = ref.shape[-1 if by_cols else 0] // N_DEV
        part = pl.ds(_al(dev, n), n)
        out.append(ref.at[(slice(None),) * (len(ref.shape) - 1) + (part,)] if by_cols else ref.at[part])
    return out


def _gather_out_shapes(shards):
    s_in, s_b, s_o, s_1, s_2 = (shards[n] for n in BIG)
    return [SDS((s_in.shape[1] * N_DEV, s_in.shape[2]), bf16), SDS(s_b.shape[1:3] + (s_b.shape[3] * N_DEV,), bf16),
            SDS((s_o.shape[1] * N_DEV, s_o.shape[2]), bf16), SDS((s_1.shape[1], s_1.shape[2] * N_DEV), bf16),
            SDS((s_2.shape[1] * N_DEV, s_2.shape[2]), bf16)]


def _seq_all_gather_layer(layer, which, n_early, shard_refs, out_shapes, tag=""):
    nt = len(which)
    outs = [jax.empty_ref(sh, memory_space=pltpu.MemorySpace.HBM) for sh in out_shapes]
    early, late = tuple(range(n_early)), tuple(range(n_early, nt))

    @pl.kernel(mesh=plsc.ScalarSubcoreMesh(axis_name="seq", num_cores=1), name=f"seq_all_gather_l{layer}{tag}",
               scratch_types=(pltpu.SemaphoreType.DMA((9,)), pltpu.SemaphoreType.DMA((9,))),
               compiler_params=pltpu.CompilerParams(collective_id=1))
    def launch(send_sems, recv_sems):
        x, y, c = _place()
        me, sibling = (x, y, c), (x, y, 1 - c)
        first, second, diag = _ici_route(x, y, c)
        _handshake([sibling, first, second])
        mine = [r.at[layer] for r in shard_refs]

        def copies(k, blk, to, src=None, part=range(nt)):
            dst = _shard_of(outs, 4 * blk[0] + 2 * blk[1] + blk[2], which)
            src = dst if src is None else src
            return [pltpu.make_async_remote_copy(src_ref=src[t], dst_ref=dst[t], send_sem=send_sems.at[k], recv_sem=recv_sems.at[k],
                                                 device_id=to, device_id_type=MESH) for t in part]

        def start(cps):
            for cp in cps:
                cp.start()
            return cps

        def landed(cps):
            for cp in cps:
                cp.wait_recv()

        sent = start(copies(0, me, sibling, src=mine) + copies(1, me, first, src=mine, part=early)
                     + copies(2, me, first, src=mine, part=late) + copies(3, me, second, src=mine))
        landed(copies(1, first, me, part=early))
        sent += start(copies(4, first, second, part=early) + copies(6, first, sibling, part=early))
        landed(copies(2, first, me, part=late))
        sent += start(copies(5, first, second, part=late) + copies(6, first, sibling, part=late))
        landed(copies(3, second, me))
        sent += start(copies(7, second, sibling))
        landed(copies(4, diag, me, part=early) + copies(5, diag, me, part=late))
        sent += start(copies(8, diag, sibling))
        other = lambda p: (p[0], p[1], 1 - c)
        landed(copies(0, sibling, me) + copies(6, other(second), me) + copies(7, other(first), me) + copies(8, other(diag), me))
        for cp in sent:
            cp.wait_send()

    launch()
    return [o[...] for o in outs]


def _ici_route(x, y, c):
    return (x ^ (1 - c), y ^ c, c), (x ^ c, y ^ (1 - c), c), (1 - x, 1 - y, c)


def _place_own(where, which, shards, gathered, after):
    nt = len(which)

    def body(where_ref, *refs):
        del where_ref
        for src, dst in zip(refs[:nt], refs[2 * nt + 1:]):
            dst[...] = src[...]

    in_specs, out_specs = [], []
    for t, sh in zip(which, shards):
        blk = sh.shape[1:]
        in_specs.append(pl.BlockSpec((None,) + blk, functools.partial(lambda nd, i, wh: (wh[0],) + (0,) * nd, len(blk))))
        by_cols = BIG[t] in ("w_branch", "w_ff1")
        out_specs.append(pl.BlockSpec(blk, functools.partial(
            lambda nd, cols, i, wh: (0,) * (nd - 1) + (wh[1],) if cols else (wh[1],) + (0,) * (nd - 1), len(blk), by_cols)))
    return pl.pallas_call(
        body, name="place_own", out_shape=[SDS(g.shape, g.dtype) for g in gathered],
        input_output_aliases={1 + nt + i: i for i in range(nt)}, compiler_params=_cp("arbitrary"),
        grid_spec=pltpu.PrefetchScalarGridSpec(num_scalar_prefetch=1, grid=(1,), in_specs=in_specs + [ANY] * (nt + 1), out_specs=out_specs),
    )(where, *shards, *gathered, after)


def _handshake(peers):
    barrier = pltpu.get_barrier_semaphore()
    for p in peers:
        pl.semaphore_signal(barrier, inc=1, device_id=p, device_id_type=MESH)
    pl.semaphore_wait(barrier, len(peers))


def _seq_exchange_on_chip(grads):
    nt, nchip = len(BIG), N_DEV // 2
    g_refs = [jax.new_ref(g, memory_space=pltpu.MemorySpace.HBM) for g in grads]
    outs = [jax.empty_ref(SDS((nchip,) + g.shape[2:], bf16), memory_space=pltpu.MemorySpace.HBM) for g in grads]

    @pl.kernel(mesh=plsc.ScalarSubcoreMesh(axis_name="seq", num_cores=1), name="seq_rs_on_chip",
               scratch_types=(pltpu.SemaphoreType.DMA((nchip,)), pltpu.SemaphoreType.DMA((nchip,))),
               compiler_params=pltpu.CompilerParams(collective_id=2))
    def launch(send_sems, recv_sems):
        x, y, c = _place()
        sibling = (x, y, 1 - c)
        _handshake([sibling])
        remote = [pltpu.make_async_remote_copy(src_ref=g_refs[t].at[j, 1 - c], dst_ref=outs[t].at[j], send_sem=send_sems.at[j],
                                               recv_sem=recv_sems.at[j], device_id=sibling, device_id_type=MESH)
                  for j in range(nchip) for t in range(nt)]
        for cp in remote:
            cp.start()
        for cp in remote:
            cp.wait_recv()
        for cp in remote:
            cp.wait_send()

    launch()
    return [o[...] for o in outs], [g[...] for g in g_refs]


def _seq_exchange_between_chips(sums):
    nt = len(BIG)
    s_refs = [jax.new_ref(a, memory_space=pltpu.MemorySpace.HBM) for a in sums]
    outs = [jax.empty_ref(SDS((3,) + a.shape[1:], bf16), memory_space=pltpu.MemorySpace.HBM) for a in sums]
    transit = [jax.empty_ref(SDS(a.shape[1:], bf16), memory_space=pltpu.MemorySpace.HBM) for a in sums]

    early, late = (0,), tuple(range(1, nt))

    @pl.kernel(mesh=plsc.ScalarSubcoreMesh(axis_name="seq", num_cores=1), name="seq_rs_between_chips",
               scratch_types=(pltpu.SemaphoreType.DMA((6,)), pltpu.SemaphoreType.DMA((6,))),
               compiler_params=pltpu.CompilerParams(collective_id=3))
    def launch(send_sems, recv_sems):
        x, y, c = _place()
        first, second, diag = _ici_route(x, y, c)
        _handshake([first, second])

        def copies(k, src, dst, to, part=range(nt)):
            return [pltpu.make_async_remote_copy(src_ref=src(t), dst_ref=dst(t), send_sem=send_sems.at[k], recv_sem=recv_sems.at[k],
                                                 device_id=to, device_id_type=MESH) for t in part]

        chip_of = lambda p: 2 * p[0] + p[1]
        for_diag = lambda t: s_refs[t].at[chip_of(diag)]
        through = lambda t: transit[t]
        last = lambda t: outs[t].at[2]
        direct = (copies(0, lambda t: s_refs[t].at[chip_of(first)], lambda t: outs[t].at[0], first)
                  + copies(1, lambda t: s_refs[t].at[chip_of(second)], lambda t: outs[t].at[1], second))
        via = [copies(2, for_diag, through, first, early), copies(3, for_diag, through, first, late)]
        passed = [copies(4, through, last, second, early), copies(5, through, last, second, late)]
        for cp in via[0] + direct + via[1]:
            cp.start()
        for arrived, onward in zip(via, passed):
            for cp in arrived:
                cp.wait_recv()
            for cp in onward:
                cp.start()
        sent = direct + via[0] + via[1] + passed[0] + passed[1]
        for cp in direct + passed[0] + passed[1]:
            cp.wait_recv()
        for cp in sent:
            cp.wait_send()

    launch()
    return [o[...] for o in outs], [a[...] for a in s_refs]


def _chip_sums(core, mine, other, after, steps=2):
    nt, nchip = len(mine), mine[0].shape[0]
    m4 = [a.reshape(nchip, 2, -1, a.shape[-1]) for a in mine]
    o3 = [a.reshape(nchip, -1, a.shape[-1]) for a in other]

    def body(c_ref, *refs):
        del c_ref
        for a_ref, b_ref, o_ref in zip(refs[:nt], refs[nt:2 * nt], refs[2 * nt + 1:]):
            o_ref[...] = (a_ref[...].astype(f32) + b_ref[...].astype(f32)).astype(bf16)

    tiles = [(a.shape[1] // steps, a.shape[2]) for a in o3]
    blks = [pl.BlockSpec((None,) + t, lambda j, i, c_ref: (j, i, 0)) for t in tiles]
    outs = pl.pallas_call(
        body, name="chip_sums", out_shape=[SDS(a.shape, bf16) for a in o3], compiler_params=_cp("parallel", "parallel"),
        grid_spec=pltpu.PrefetchScalarGridSpec(
            num_scalar_prefetch=1, grid=(nchip, steps),
            in_specs=[pl.BlockSpec((None, None) + t, lambda j, i, c_ref: (j, c_ref[0], i, 0)) for t in tiles] + blks + [ANY],
            out_specs=blks),
    )(core, *m4, *o3, after)
    return [o.reshape(a.shape) for o, a in zip(outs, other)]


def _all_reduce_rows(pack):
    rows = pack.shape[0]
    blk = rows // N_DEV

    def body(in_ref, out_ref, land, send1, recv1, send2, recv2):
        x, y, c = _place()
        me = 4 * x + 2 * y + c
        others = [(px, py, pc) for px in range(2) for py in range(2) for pc in range(2)]

        def is_me(p):
            return jnp.logical_and(jnp.logical_and(p[0] == x, p[1] == y), p[2] == c)

        land[me] = in_ref[pl.ds(_al(me, blk), blk), :]
        for d, p in enumerate(others):
            @pl.when(jnp.logical_not(is_me(p)))
            def _():
                pltpu.make_async_remote_copy(src_ref=in_ref.at[pl.ds(d * blk, blk), :], dst_ref=land.at[me], send_sem=send1.at[d],
                                             recv_sem=recv1.at[me], device_id=p, device_id_type=MESH).start()
        for d, p in enumerate(others):
            @pl.when(jnp.logical_not(is_me(p)))
            def _():
                cp = pltpu.make_async_remote_copy(src_ref=in_ref.at[pl.ds(d * blk, blk), :], dst_ref=land.at[d], send_sem=send1.at[d],
                                                  recv_sem=recv1.at[d], device_id=p, device_id_type=MESH)
                cp.wait_recv()
                cp.wait_send()
        total = land[0]
        for d in range(1, N_DEV):
            total = total + land[d]
        out_ref[pl.ds(_al(me, blk), blk), :] = total
        for d, p in enumerate(others):
            @pl.when(jnp.logical_not(is_me(p)))
            def _():
                mine = out_ref.at[pl.ds(_al(me, blk), blk), :]
                pltpu.make_async_remote_copy(src_ref=mine, dst_ref=mine, send_sem=send2.at[d], recv_sem=recv2.at[me],
                                             device_id=p, device_id_type=MESH).start()
        for d, p in enumerate(others):
            @pl.when(jnp.logical_not(is_me(p)))
            def _():
                theirs = out_ref.at[pl.ds(d * blk, blk), :]
                cp = pltpu.make_async_remote_copy(src_ref=theirs, dst_ref=theirs, send_sem=send2.at[d], recv_sem=recv2.at[d],
                                                  device_id=p, device_id_type=MESH)
                cp.wait_recv()
                cp.wait_send()

    vm = pl.BlockSpec(memory_space=pltpu.VMEM)
    return pl.pallas_call(
        body, name="all_reduce_rows", in_specs=[vm], out_specs=vm, out_shape=SDS((rows, LANE), f32),
        scratch_shapes=[pltpu.VMEM((N_DEV, blk, LANE), f32)] + [pltpu.SemaphoreType.DMA((N_DEV,))] * 4,
        compiler_params=pltpu.CompilerParams(vmem_limit_bytes=VMEM_LIMIT),
    )(pack)


def _lower_bounds_fwd(lower):
    def body(l_ref, o_ref):
        sm = _layer_softmax(l_ref)
        run = jnp.zeros_like(sm[0])
        for l in range(DEPTH):
            o_ref[l:l + 1, :] = run
            if l + 1 < DEPTH:
                run = run + sm[l + 1]

    return pl.pallas_call(body, name="lower_bounds_fwd", out_shape=SDS(lower.shape, f32))(lower)


def _layer_softmax(l_ref):
    rows = [l_ref[l:l + 1, :] for l in range(DEPTH)]
    top = functools.reduce(jnp.maximum, rows)
    e = [jnp.exp(r - top) for r in rows]
    tot = functools.reduce(lambda a, b: a + b, e)
    return [v / tot for v in e]


def _lower_bounds_bwd(lower, dlbs):
    def body(l_ref, d_ref, o_ref):
        sm = _layer_softmax(l_ref)
        dsm = [None] * DEPTH
        run = jnp.zeros_like(sm[0])
        dsm[0] = run
        for l in reversed(range(1, DEPTH)):
            run = run + d_ref[l:l + 1, :]
            dsm[l] = run
        inner = functools.reduce(lambda a, b: a + b, [sm[l] * dsm[l] for l in range(DEPTH)])
        for l in range(DEPTH):
            o_ref[l:l + 1, :] = sm[l] * (dsm[l] - inner)

    return pl.pallas_call(body, name="lower_bounds_bwd", out_shape=SDS(lower.shape, f32))(lower, dlbs)


_ADAM_C1 = 1.0 - ADAM_B1 ** ADAM_STEP
_ADAM_C2 = 1.0 - ADAM_B2 ** ADAM_STEP


def _adamw(w, g, m, v):
    m = ADAM_B1 * m + (1.0 - ADAM_B1) * g
    v = ADAM_B2 * v + (1.0 - ADAM_B2) * (g * g)
    delta = -ADAM_LR * ((m / _ADAM_C1) / (jnp.sqrt(v / _ADAM_C2) + ADAM_EPS) + ADAM_WD * w)
    return delta, m, v


def _row_tile(rows, cap):
    return next(t for t in range(min(cap, rows) // 16 * 16, 0, -16) if rows % t == 0)


def _adam_big(where, names, w, m, v, sums, landed, outs, after, steps=4):
    nt = len(names)
    three = lambda a: a.reshape(a.shape[0], -1, a.shape[-1])
    w3, m3, v3 = ([three(d[n]) for n in names] for d in (w, m, v))
    outs3 = [three(a) for n in names for a in outs[n]]
    sums3 = [three(a) for a in sums]
    land3 = [three(a) for a in landed]

    def body(where_ref, *refs):
        del where_ref
        o_refs = refs[5 * nt + 4 * nt + 1:]
        for t in range(nt):
            w_ref, m_ref, v_ref, sum_ref, land_ref = (refs[q * nt + t] for q in range(5))
            g = sum_ref[...].astype(f32)
            for k in range(3):
                g = g + land_ref[k].astype(f32)
            delta, nm, nv = _adamw(w_ref[...], g, m_ref[...], v_ref[...])
            for o_ref, val in zip(o_refs[4 * t:4 * t + 4], (g, delta, nm, nv)):
                o_ref[...] = val

    tiles = [(a.shape[1] // steps, a.shape[2]) for a in w3]
    own = [pl.BlockSpec((None,) + t, lambda i, wh: (wh[0], i, 0)) for t in tiles]
    res = pl.pallas_call(
        body, name="adam_big", out_shape=[SDS(a.shape, f32) for a in outs3],
        input_output_aliases={1 + 5 * nt + i: i for i in range(4 * nt)}, compiler_params=_cp("parallel"),
        grid_spec=pltpu.PrefetchScalarGridSpec(
            num_scalar_prefetch=1, grid=(steps,),
            in_specs=own * 3 + [pl.BlockSpec((None,) + t, lambda i, wh: (wh[1], i, 0)) for t in tiles]
            + [pl.BlockSpec((3,) + t, lambda i, wh: (0, i, 0)) for t in tiles] + [ANY] * (4 * nt + 1),
            out_specs=[s for s in own for _ in range(4)]),
    )(where, *w3, *m3, *v3, *sums3, *land3, *outs3, after)
    return {n: [o.reshape(w[n].shape) for o in res[4 * t:4 * t + 4]] for t, n in enumerate(names)}


def _touch(a, after):
    a2 = a.reshape(-1, a.shape[-1])

    def body(a_ref, after_ref, o_ref):
        del after_ref
        o_ref[...] = a_ref[0:8, :].astype(f32)

    return pl.pallas_call(
        body, name="touch", grid=(1,), in_specs=[pl.BlockSpec((16, LANE), lambda i: (0, 0)), ANY],
        out_specs=pl.BlockSpec((8, LANE), lambda i: (0, 0)), out_shape=SDS((8, LANE), f32),
    )(a2, after)


def _adam_rows(w, g, m, v):
    def body(w_ref, g_ref, m_ref, v_ref, d_ref, nm_ref, nv_ref):
        delta, nm, nv = _adamw(w_ref[...], g_ref[...], m_ref[...], v_ref[...])
        d_ref[...] = delta
        nm_ref[...] = nm
        nv_ref[...] = nv

    return pl.pallas_call(body, name="adam_rows", out_shape=[SDS(w.shape, f32)] * 3)(w, g, m, v)


SMALL = ("g_mix", "lower_bounds", "g_hgrn_out", "w_conv", "sg_ln_g", "sg_ln_b", "w_sg", "b_sg", "g_ffn", "g_final")
WEIGHTS = ("w_in", "g_mix", "lower_bounds", "g_hgrn_out", "w_conv", "sg_ln_g", "sg_ln_b", "w_sg", "b_sg", "w_branch", "w_o", "g_ffn",
           "w_ff1", "w_ff2", "g_final")


def _pack_rows(arrays, multiple):
    flat = jnp.concatenate([a.reshape(-1) for a in arrays])
    rows = -(-flat.shape[0] // (LANE * multiple)) * multiple
    return jnp.pad(flat, (0, rows * LANE - flat.shape[0])).reshape(rows, LANE)


def _unpack_rows(pack, like):
    flat = pack.reshape(-1)
    out, at = [], 0
    for a in like:
        out.append(flat[at:at + a.size].reshape(a.shape))
        at += a.size
    return out


def kernel(x, w_in, g_mix, lower_bounds, g_hgrn_out, w_conv, sg_ln_g, sg_ln_b, w_sg, b_sg, w_branch, w_o, g_ffn, w_ff1, w_ff2, g_final, loss_target, m_w_in, m_g_mix, m_lower_bounds, m_g_hgrn_out, m_w_conv, m_sg_ln_g, m_sg_ln_b, m_w_sg, m_b_sg, m_w_branch, m_w_o, m_g_ffn, m_w_ff1, m_w_ff2, m_g_final, v_w_in, v_g_mix, v_lower_bounds, v_g_hgrn_out, v_w_conv, v_sg_ln_g, v_sg_ln_b, v_w_sg, v_b_sg, v_w_branch, v_w_o, v_g_ffn, v_w_ff1, v_w_ff2, v_g_final):
    weights = dict(w_in=w_in, g_mix=g_mix, lower_bounds=lower_bounds, g_hgrn_out=g_hgrn_out, w_conv=w_conv, sg_ln_g=sg_ln_g,
                   sg_ln_b=sg_ln_b, w_sg=w_sg, b_sg=b_sg, w_branch=w_branch, w_o=w_o, g_ffn=g_ffn, w_ff1=w_ff1, w_ff2=w_ff2, g_final=g_final)
    mom1 = dict(w_in=m_w_in, g_mix=m_g_mix, lower_bounds=m_lower_bounds, g_hgrn_out=m_g_hgrn_out, w_conv=m_w_conv, sg_ln_g=m_sg_ln_g,
                sg_ln_b=m_sg_ln_b, w_sg=m_w_sg, b_sg=m_b_sg, w_branch=m_w_branch, w_o=m_w_o, g_ffn=m_g_ffn, w_ff1=m_w_ff1, w_ff2=m_w_ff2,
                g_final=m_g_final)
    mom2 = dict(w_in=v_w_in, g_mix=v_g_mix, lower_bounds=v_lower_bounds, g_hgrn_out=v_g_hgrn_out, w_conv=v_w_conv, sg_ln_g=v_sg_ln_g,
                sg_ln_b=v_sg_ln_b, w_sg=v_w_sg, b_sg=v_b_sg, w_branch=v_w_branch, w_o=v_w_o, g_ffn=v_g_ffn, w_ff1=v_w_ff1, w_ff2=v_w_ff2,
                g_final=v_g_final)
    xi, yi, ci = _place()
    dev = 4 * xi + 2 * yi + ci
    conv_cols = w_conv.shape[-1]

    for d in (weights, mom1, mom2):
        d["w_in"] = jnp.swapaxes(d["w_in"], 1, 2)
    shards = {n: weights[n].astype(bf16) for n in BIG}

    conv_place = lax.dynamic_update_slice(jnp.zeros((DEPTH, 3, BRANCH), f32), w_conv, (0, 0, dev * conv_cols))
    (w_conv_full,) = _unpack_rows(_all_reduce_rows(_pack_rows([conv_place], 8 * N_DEV)), [conv_place])
    lbs = _lower_bounds_fwd(lower_bounds)

    def small_of(l):
        return dict(g_mix=g_mix[l][None], lb=lbs[l][None], g_out=g_hgrn_out[l][None], w_conv=w_conv_full[l], ln_g=sg_ln_g[l][None],
                    ln_b=sg_ln_b[l][None], w_sg=w_sg[l], b_sg_t=b_sg[l].T, g_ffn=g_ffn[l][None])

    act = x[0]
    full, saved = [], []
    shard_refs = [jax.new_ref(shards[n], memory_space=pltpu.MemorySpace.HBM) for n in BIG]
    shapes = _gather_out_shapes(shards)
    every = tuple(range(len(BIG)))
    groups = [(0, (0,), 1, "a"), (0, (1, 2), 2, "b"), (0, (3, 4), 1, "c"), (1, (0,), 1, "a"), (1, (3, 4, 1, 2), 1, "b")]
    groups += [(l, every, 1, "") for l in range(2, DEPTH)]
    arrived = {}
    for l, which, n_early, tag in groups:
        got = _seq_all_gather_layer(l, which, n_early, [shard_refs[t] for t in which], [shapes[t] for t in which], tag)
        arrived.update({(l, BIG[t]): (which, got) for t in which})

    for l in range(DEPTH):
        full.append({})

        def weight(name, after, l=l):
            if name not in full[l]:
                which, got = arrived[(l, name)]
                where = jnp.stack([jnp.int32(l), dev.astype(jnp.int32)])
                full[l].update(zip([BIG[t] for t in which], _place_own(where, which, [shards[BIG[t]] for t in which], got, after)))
            return full[l][name]

        act, sv = _layer_fwd(act, weight, small_of(l))
        saved.append(sv)
    loss_row, dx, dxb, dg_final = _final(act, loss_target[0], g_final[None])

    core = ci.astype(jnp.int32)[None]
    big_out = {n: [lax.empty(weights[n].shape, f32) for _ in range(4)] for n in BIG}
    small_grads = [None] * DEPTH

    def chip_sums(stage, after):
        l, received, mine = stage
        sums = _chip_sums(core, mine, received, after)
        placed.append(sums[BIG.index("w_o")])
        landed, sums = _seq_exchange_between_chips(sums)
        return l, sums, landed

    def adam_layer(stage, after):
        l, sums, landed = stage
        where = jnp.stack([jnp.int32(l), (2 * xi + yi).astype(jnp.int32)])
        big_out.update(_adam_big(where, BIG, weights, mom1, mom2, sums, landed, big_out, after))

    above = None
    placed = []
    for l in reversed(range(DEPTH)):
        summed = []

        def between(dx1):
            if above is None:
                return dx1
            summed.append(chip_sums(above, dx1))
            return placed[-1]

        def before_end(big):
            return _touch(summed[0][2][BIG.index("w_o")], big["w_in"]) if summed else big["w_in"]

        dx, dxb, big, small_grads[l] = _layer_bwd(dx, dxb, saved[l], full[l], small_of(l), between, before_end)
        if summed:
            adam_layer(summed[0], dx)
        above = (l, *_seq_exchange_on_chip([big[n] for n in BIG]))

    stack = lambda f: jnp.stack([f(small_grads[l]) for l in range(DEPTH)])
    d_lower = _lower_bounds_bwd(lower_bounds, stack(lambda s: s["vecs"][0]))
    local_small = dict(g_mix=stack(lambda s: s["g_mix"][0]), lower_bounds=d_lower, g_hgrn_out=stack(lambda s: s["vecs"][1]),
                       w_conv=stack(lambda s: s["vecs"][4:7]), sg_ln_g=stack(lambda s: s["vecs"][2]), sg_ln_b=stack(lambda s: s["vecs"][3]),
                       w_sg=stack(lambda s: s["w_sg"]), b_sg=stack(lambda s: s["b_sg_t"].T), g_ffn=stack(lambda s: s["g_ffn"][0]),
                       g_final=dg_final[0])
    order = [local_small[n] for n in SMALL] + [loss_row]
    *reduced, loss_sum = _unpack_rows(_all_reduce_rows(_pack_rows(order, 8 * N_DEV)), order)
    loss = loss_sum[0, 0]
    grads = dict(zip(SMALL, reduced))
    grads["w_conv"] = lax.dynamic_slice(grads["w_conv"], (0, 0, dev * conv_cols), (DEPTH, 3, conv_cols))

    deltas, new_m, new_v = {}, {}, {}
    packs = [_pack_rows([d[n] for n in SMALL], 8) for d in (weights, grads, mom1, mom2)]
    like = [weights[n] for n in SMALL]
    small_out = _adam_rows(*packs)
    for out, pack in zip((deltas, new_m, new_v), small_out):
        out.update(zip(SMALL, _unpack_rows(pack, like)))
    adam_layer(chip_sums(above, dx), small_out[0])
    for n in BIG:
        grads[n], deltas[n], new_m[n], new_v[n] = (jnp.swapaxes(a, 1, 2) if n == "w_in" else a for a in big_out[n])

    return (loss, dx[None], *[grads[n] for n in WEIGHTS], *[deltas[n] for n in WEIGHTS], *[new_m[n] for n in WEIGHTS],
            *[new_v[n] for n in WEIGHTS])
```

```python
import functools

import jax
import jax.numpy as jnp
from jax import lax
from jax.experimental import pallas as pl
from jax.experimental.pallas import tpu as pltpu
from jax.experimental.pallas import tpu_sc as plsc

f32 = jnp.float32
bf16 = jnp.bfloat16
SDS = jax.ShapeDtypeStruct
MESH = pl.DeviceIdType.MESH

D_MODEL = 1024
BRANCH = 512
N_COLS = 7680
D_FF = 4096
DEPTH = 4
HEADS = 4
HEAD_DIM = 128
HGRN_CHUNK = 64
SG_CHUNK = 128
SG_GROUPS = 4
NORM_EPS = 1e-6
LN_EPS = 1e-5
LB_FLOOR = 1e-30
N_DEV = 8
SHARD_IN = N_COLS // N_DEV
WIN = 1024
LANE = 128
GATE_COL0 = 9 * BRANCH

ADAM_LR = 0.001
ADAM_B1 = 0.9
ADAM_B2 = 0.999
ADAM_EPS = 1e-08
ADAM_WD = 0.01
ADAM_STEP = 10

MIX_TILE = 256
VMEM_LIMIT = 56 * 1024 * 1024


def _cp(*sem):
    return pltpu.CompilerParams(dimension_semantics=sem or None, vmem_limit_bytes=VMEM_LIMIT)


def _dot(a, b):
    return jnp.dot(a, b, preferred_element_type=f32)


def _dot_nt(a, b):
    return lax.dot_general(a, b, (((1,), (1,)), ((), ())), preferred_element_type=f32)


def _dot_tn(a, b):
    return lax.dot_general(a, b, (((0,), (0,)), ((), ())), preferred_element_type=f32)


def _dot_exact(ones, b):
    hi = b.astype(bf16)
    rest = b - hi.astype(f32)
    mid = rest.astype(bf16)
    low = (rest - mid.astype(f32)).astype(bf16)
    ones = ones.astype(bf16)
    return _dot(ones, hi) + _dot(ones, mid) + _dot(ones, low)


def _sigmoid(x):
    return jax.nn.sigmoid(x)


_GELU_C = 0.7978845608028654
_GELU_A = 0.044715


def _gelu(x):
    return 0.5 * x * (1.0 + jnp.tanh(_GELU_C * (x + _GELU_A * x * x * x)))


def _gelu_grad(x):
    x2 = x * x
    t = jnp.tanh(_GELU_C * (x + _GELU_A * x * x2))
    return 0.5 * (1.0 + t) + 0.5 * x * (1.0 - t * t) * _GELU_C * (1.0 + 3.0 * _GELU_A * x2)


def _rms_stats(x):
    r = lax.rsqrt(jnp.mean(x * x, axis=-1, keepdims=True) + NORM_EPS)
    return r, x * r


def _rms_bwd(dh, xh, r, g):
    dg = jnp.sum(dh * xh, axis=0, keepdims=True)
    dxn = dh * g
    dx = r * (dxn - xh * jnp.mean(dxn * xh, axis=-1, keepdims=True))
    return dx, dg


def _tri(n, upper=False):
    r = lax.broadcasted_iota(jnp.int32, (n, n), 0)
    c = lax.broadcasted_iota(jnp.int32, (n, n), 1)
    return (c >= r) if upper else (c <= r)


def _acc_rows(ref, first, val):
    @pl.when(first)
    def _():
        ref[...] = val

    @pl.when(jnp.logical_not(first))
    def _():
        ref[...] += val


def _rms_mm(x, g, w_t, tm=1024, tn=1536):
    s, n = x.shape[0], w_t.shape[0]
    jm = GATE_COL0 // tn

    def body(x_ref, g_ref, w_ref, pm_ref, pg_ref, h_ref, hs):
        j = pl.program_id(1)

        @pl.when(j == 0)
        def _():
            _, xh = _rms_stats(x_ref[...])
            hv = (xh * g_ref[...]).astype(bf16)
            hs[...] = hv
            h_ref[...] = hv

        res = _dot_nt(hs[...], w_ref[...])

        @pl.when(j < jm)
        def _():
            pm_ref[...] = res

        @pl.when(j >= jm)
        def _():
            pg_ref[...] = res.astype(bf16)

    return pl.pallas_call(
        body, name="rms_mm", grid=(s // tm, n // tn),
        in_specs=[pl.BlockSpec((tm, D_MODEL), lambda i, j: (i, 0)), pl.BlockSpec((1, D_MODEL), lambda i, j: (0, 0)),
                  pl.BlockSpec((tn, D_MODEL), lambda i, j: (j, 0))],
        out_specs=[pl.BlockSpec((tm, tn), lambda i, j: (i, jnp.minimum(j, jm - 1))),
                   pl.BlockSpec((tm, tn), lambda i, j: (i, jnp.maximum(j - jm, 0))), pl.BlockSpec((tm, D_MODEL), lambda i, j: (i, 0))],
        out_shape=[SDS((s, GATE_COL0), f32), SDS((s, n - GATE_COL0), bf16), SDS((s, D_MODEL), bf16)],
        scratch_shapes=[pltpu.VMEM((tm, D_MODEL), bf16)], compiler_params=_cp("parallel", "arbitrary"),
    )(x, g, w_t)


def _hgrn_gates(fp, lb):
    logf = jnp.logaddexp(jnp.log(jnp.maximum(lb, LB_FLOOR)), jnp.log1p(-lb) + jax.nn.log_sigmoid(fp))
    snf = _sigmoid(-fp)
    return logf, snf, (1.0 - lb) * snf


def _p_specs(tile, cols, row_map):
    return [pl.BlockSpec((tile, BRANCH), functools.partial(lambda c, i: (row_map(i), c), c)) for c in cols]


def _mixer_fwd(p, lb, gout, wconv, lng, lnb, wsg, bsg_t):
    s = p.shape[0]
    tt = MIX_TILE
    nch = tt // HGRN_CHUNK

    def body(q_ref, fp_ref, iv_ref, go_ref, bg_ref, cg_ref, xc_ref, u_ref, v_ref, lb_ref, gout_ref, wconv_ref, lng_ref,
             lnb_ref, wsg_ref, bsg_ref, z_ref, opre_ref, st_ref, st_scr, zbuf):
        @pl.when(pl.program_id(0) == 0)
        def _():
            st_scr[...] = jnp.zeros_like(st_scr)
            zbuf[0:8, :] = jnp.zeros((8, BRANCH), f32)

        lbv = lb_ref[...]
        q_raw = q_ref[...]
        qs = q_raw * _sigmoid(q_raw)
        logf, _, kk = _hgrn_gates(fp_ref[...], lbv)
        iv = iv_ref[...]
        causal = _tri(HGRN_CHUNK)
        tri = causal.astype(f32)
        last_row = lax.broadcasted_iota(jnp.int32, (HGRN_CHUNK, 1), 0) == HGRN_CHUNK - 1
        for c in range(nch):
            rows = slice(HGRN_CHUNK * c, HGRN_CHUNK * (c + 1))
            b = _dot_exact(tri, logf[rows])
            bl = jnp.sum(jnp.where(last_row, b, 0.0), axis=0, keepdims=True)
            qb = (qs[rows] * jnp.exp(b)).astype(bf16)
            kb = (kk[rows] * jnp.exp(-b)).astype(bf16)
            kd = (kk[rows] * jnp.exp(bl - b)).astype(bf16)
            ebl = jnp.exp(bl)
            vc = iv[rows].astype(bf16)
            for h in range(HEADS):
                sl = slice(HEAD_DIM * h, HEAD_DIM * (h + 1))
                st = st_scr[h]
                st_ref[c, h] = st
                a = jnp.where(causal, _dot_nt(qb[:, sl], kb[:, sl]), 0.0)
                opre_ref[rows, sl] = _dot(a.astype(bf16), vc[:, sl]) + _dot_nt(qb[:, sl], st.astype(bf16))
                st_scr[h] = st * ebl[:, sl] + _dot_tn(vc[:, sl], kd[:, sl])

        o = opre_ref[...]
        go = go_ref[...]
        gout_v = gout_ref[...]
        for h in range(HEADS):
            sl = slice(HEAD_DIM * h, HEAD_DIM * (h + 1))
            _, oh = _rms_stats(o[:, sl])
            z_ref[:, sl] = (oh * gout_v[:, sl] * _sigmoid(go[:, sl])).astype(bf16)

        zc = cg_ref[...] * xc_ref[...]
        zbuf[8:8 + tt, :] = zc
        y = wconv_ref[0:1, :] * zbuf[pl.ds(6, tt), :] + wconv_ref[1:2, :] * zbuf[pl.ds(7, tt), :] + wconv_ref[2:3, :] * zc
        z_ref[:, BRANCH:2 * BRANCH] = (bg_ref[...] * y).astype(bf16)
        zbuf[0:8, :] = zbuf[tt:tt + 8, :]

        ug = _gelu(u_ref[...])
        vg = _gelu(v_ref[...])
        vcen = vg - jnp.mean(vg, axis=-1, keepdims=True)
        rstd = lax.rsqrt(jnp.mean(vcen * vcen, axis=-1, keepdims=True) + LN_EPS)
        vn = (vcen * rstd * lng_ref[...] + lnb_ref[...]).astype(bf16)
        low = _tri(SG_CHUNK)
        for g in range(SG_GROUPS):
            sl = slice(LANE * g, LANE * (g + 1))
            wm = jnp.where(low, wsg_ref[g], 0.0).astype(bf16)
            bias = bsg_ref[:, g:g + 1]
            for cc in range(tt // SG_CHUNK):
                rows = slice(SG_CHUNK * cc, SG_CHUNK * (cc + 1))
                sv = _dot(wm, vn[rows, sl]) + bias
                z_ref[rows, 2 * BRANCH + LANE * g:2 * BRANCH + LANE * (g + 1)] = (ug[rows, sl] * sv).astype(bf16)

    full = lambda shape: pl.BlockSpec(shape, lambda i: (0,) * len(shape))
    return pl.pallas_call(
        body, name="mixer_fwd", grid=(s // tt,),
        in_specs=_p_specs(tt, range(9), lambda i: i) + [full((1, BRANCH)), full((1, BRANCH)), full((3, BRANCH)), full((1, BRANCH)),
                                                        full((1, BRANCH)), full((SG_GROUPS, SG_CHUNK, SG_CHUNK)), full((SG_CHUNK, SG_GROUPS))],
        out_specs=[pl.BlockSpec((tt, 3 * BRANCH), lambda i: (i, 0)), pl.BlockSpec((tt, BRANCH), lambda i: (i, 0)),
                   pl.BlockSpec((nch, HEADS, HEAD_DIM, HEAD_DIM), lambda i: (i, 0, 0, 0))],
        out_shape=[SDS((s, 3 * BRANCH), bf16), SDS((s, BRANCH), f32), SDS((s // HGRN_CHUNK, HEADS, HEAD_DIM, HEAD_DIM), f32)],
        scratch_shapes=[pltpu.VMEM((HEADS, HEAD_DIM, HEAD_DIM), f32), pltpu.VMEM((tt + 8, BRANCH), f32)],
        compiler_params=_cp("arbitrary"),
    )(*([p] * 9), lb, gout, wconv, lng, lnb, wsg, bsg_t)


def _branch_gate(z, wb, pg, x, wo, tm=256):
    s = z.shape[0]

    def body(z_ref, wb_ref, g_ref, x_ref, wo_ref, y_ref, m_ref, x1_ref):
        acc = None
        for n in range(3):
            cols = slice(D_MODEL * n, D_MODEL * (n + 1))
            yn = _dot(z_ref[:, BRANCH * n:BRANCH * (n + 1)], wb_ref[n])
            y_ref[:, cols] = yn.astype(bf16)
            t = _sigmoid(g_ref[:, cols].astype(f32)) * yn
            acc = t if acc is None else acc + t
        merged = acc.astype(bf16)
        m_ref[...] = merged
        x1_ref[...] = x_ref[...] + _dot(merged, wo_ref[...])

    row = pl.BlockSpec((tm, D_MODEL), lambda i: (i, 0))
    wide = pl.BlockSpec((tm, 3 * D_MODEL), lambda i: (i, 0))
    return pl.pallas_call(
        body, name="branch_gate", grid=(s // tm,),
        in_specs=[pl.BlockSpec((tm, 3 * BRANCH), lambda i: (i, 0)), pl.BlockSpec((3, BRANCH, D_MODEL), lambda i: (0, 0, 0)), wide, row,
                  pl.BlockSpec((D_MODEL, D_MODEL), lambda i: (0, 0))],
        out_specs=[wide, row, row],
        out_shape=[SDS((s, 3 * D_MODEL), bf16), SDS((s, D_MODEL), bf16), SDS((s, D_MODEL), f32)], compiler_params=_cp("parallel"),
    )(z, wb, pg, x, wo)


def _ffn(x1, g, w1, w2, tm=1024, tf=1024):
    s = x1.shape[0]
    nf = D_FF // tf

    def body(x_ref, g_ref, w1_ref, w2_ref, o_ref, h_ref, ra_ref, hs, acc):
        f = pl.program_id(1)

        @pl.when(f == 0)
        def _():
            _, xh = _rms_stats(x_ref[...])
            hv = (xh * g_ref[...]).astype(bf16)
            hs[...] = hv
            h_ref[...] = hv
            acc[...] = jnp.zeros_like(acc)

        ra = jnp.maximum(_dot(hs[...], w1_ref[...]), 0.0)
        ra_ref[...] = ra.astype(bf16)
        acc[...] += _dot((ra * ra).astype(bf16), w2_ref[...])

        @pl.when(f == nf - 1)
        def _():
            o_ref[...] = x_ref[...] + acc[...]

    return pl.pallas_call(
        body, name="ffn", grid=(s // tm, nf),
        in_specs=[pl.BlockSpec((tm, D_MODEL), lambda i, f: (i, 0)), pl.BlockSpec((1, D_MODEL), lambda i, f: (0, 0)),
                  pl.BlockSpec((D_MODEL, tf), lambda i, f: (0, f)), pl.BlockSpec((tf, D_MODEL), lambda i, f: (f, 0))],
        out_specs=[pl.BlockSpec((tm, D_MODEL), lambda i, f: (i, 0)), pl.BlockSpec((tm, D_MODEL), lambda i, f: (i, 0)),
                   pl.BlockSpec((tm, tf), lambda i, f: (i, f))],
        out_shape=[SDS((s, D_MODEL), f32), SDS((s, D_MODEL), bf16), SDS((s, D_FF), bf16)],
        scratch_shapes=[pltpu.VMEM((tm, D_MODEL), bf16), pltpu.VMEM((tm, D_MODEL), f32)], compiler_params=_cp("parallel", "arbitrary"),
    )(x1, g, w1, w2)


def _final(x, target, g, tm=512):
    s = x.shape[0]

    def body(x_ref, t_ref, g_ref, loss_ref, dx_ref, dxb_ref, dg_ref):
        first = pl.program_id(0) == 0
        gv = g_ref[...]
        r, xh = _rms_stats(x_ref[...])
        e = xh * gv - t_ref[...]
        tile_loss = 0.5 * jnp.sum(jnp.mean(e * e, axis=-1, keepdims=True), axis=0, keepdims=True)
        dx, dg = _rms_bwd(e * (1.0 / D_MODEL), xh, r, gv)
        dx_ref[...] = dx
        dxb_ref[...] = dx.astype(bf16)
        _acc_rows(dg_ref, first, dg)
        _acc_rows(loss_ref, first, jnp.broadcast_to(tile_loss, (1, LANE)))

    row = pl.BlockSpec((tm, D_MODEL), lambda i: (i, 0))
    return pl.pallas_call(
        body, name="final_loss", grid=(s // tm,), in_specs=[row, row, pl.BlockSpec((1, D_MODEL), lambda i: (0, 0))],
        out_specs=[pl.BlockSpec((1, LANE), lambda i: (0, 0)), row, row, pl.BlockSpec((1, D_MODEL), lambda i: (0, 0))],
        out_shape=[SDS((1, LANE), f32), SDS((s, D_MODEL), f32), SDS((s, D_MODEL), bf16), SDS((1, D_MODEL), f32)],
        compiler_params=_cp("arbitrary"),
    )(x, target, g)


def _ffn_bwd(dx2, dx2b, x1, g, ra, w1, w2, tm=512, tf=1024):
    s = x1.shape[0]
    nf = D_FF // tf

    def body(dx_ref, dxb_ref, x_ref, g_ref, ra_ref, w1_ref, w2_ref, da_ref, dx1_ref, dx1b_ref, dg_ref, acc):
        i, f = pl.program_id(0), pl.program_id(1)

        @pl.when(f == 0)
        def _():
            acc[...] = jnp.zeros_like(acc)

        da = (_dot_nt(dxb_ref[...], w2_ref[...]) * (2.0 * ra_ref[...].astype(f32))).astype(bf16)
        da_ref[...] = da
        acc[...] += _dot_nt(da, w1_ref[...])

        @pl.when(f == nf - 1)
        def _():
            r, xh = _rms_stats(x_ref[...])
            dx, dg = _rms_bwd(acc[...], xh, r, g_ref[...])
            dx = dx + dx_ref[...]
            dx1_ref[...] = dx
            dx1b_ref[...] = dx.astype(bf16)
            _acc_rows(dg_ref, i == 0, dg)

    row = pl.BlockSpec((tm, D_MODEL), lambda i, f: (i, 0))
    col = pl.BlockSpec((tm, tf), lambda i, f: (i, f))
    return pl.pallas_call(
        body, name="ffn_bwd", grid=(s // tm, nf),
        in_specs=[row, row, row, pl.BlockSpec((1, D_MODEL), lambda i, f: (0, 0)), col,
                  pl.BlockSpec((D_MODEL, tf), lambda i, f: (0, f)), pl.BlockSpec((tf, D_MODEL), lambda i, f: (f, 0))],
        out_specs=[col, row, row, pl.BlockSpec((1, D_MODEL), lambda i, f: (0, 0))],
        out_shape=[SDS((s, D_FF), bf16), SDS((s, D_MODEL), f32), SDS((s, D_MODEL), bf16), SDS((1, D_MODEL), f32)],
        scratch_shapes=[pltpu.VMEM((tm, D_MODEL), f32)], compiler_params=_cp("arbitrary", "arbitrary"),
    )(dx2, dx2b, x1, g, ra, w1, w2)


def _mm_tn(a, b, nb, m, n, tm, tn, name="mm_tn", rows=None, row0=0, into=None, square_a=False):
    s = a.shape[0]
    mi, nj = m // tm, n // tn
    rows = m if rows is None else rows
    blk0 = row0 // tm

    def body(a_ref, b_ref, *rest):
        av = a_ref[...]
        if square_a:
            av = av.astype(f32)
            av = (av * av).astype(bf16)
        rest[-1][...] = _dot_tn(av, b_ref[...]).astype(bf16)

    extra = {} if into is None else dict(input_output_aliases={2: 0})
    return pl.pallas_call(
        body, name=name, grid=(nb, mi, nj),
        in_specs=[pl.BlockSpec((s, tm), lambda k, i, j: (0, k * mi + i)), pl.BlockSpec((s, tn), lambda k, i, j: (0, k * nj + j))]
        + ([] if into is None else [pl.BlockSpec(memory_space=pl.ANY)]),
        out_specs=pl.BlockSpec((None, tm, tn), lambda k, i, j: (k, blk0 + i, j)), out_shape=SDS((nb, rows, n), bf16),
        compiler_params=_cp("parallel", "parallel", "parallel"), **extra,
    )(a, b, *([] if into is None else [into]))


def _mm_tn_slabs(a, b, nb, m, nblk, rel, width, tm=512, name="mm_tn_slabs"):
    s = a.shape[0]
    n = b.shape[1] // nb
    ng, mi, nw = n // nblk, m // tm, len(rel)

    def body(a_ref, b_ref, o_ref):
        full = _dot_tn(a_ref[...], b_ref[...])
        for r, start in enumerate(rel):
            o_ref[r] = full[:, start:start + width].astype(bf16)

    return pl.pallas_call(
        body, name=name, grid=(nb, ng, mi),
        in_specs=[pl.BlockSpec((s, tm), lambda k, g, i: (0, k * mi + i)), pl.BlockSpec((s, nblk), lambda k, g, i: (0, k * ng + g))],
        out_specs=pl.BlockSpec((nw, None, tm, width), lambda k, g, i: (g, k, i, 0)), out_shape=SDS((ng * nw, nb, m, width), bf16),
        compiler_params=_cp("parallel", "parallel", "parallel"),
    )(a, b)


def _merge_bwd(dx1b, wo, y, pg, wb, after, tm=256):
    s = dx1b.shape[0]

    def body(dx_ref, wo_ref, y_ref, g_ref, wb_ref, after_ref, dy_ref, dg_ref, dz_ref):
        del after_ref
        dm = _dot_nt(dx_ref[...], wo_ref[...])
        for n in range(3):
            cols = slice(D_MODEL * n, D_MODEL * (n + 1))
            gate = _sigmoid(g_ref[:, cols].astype(f32))
            t = dm * gate
            dy = t.astype(bf16)
            dy_ref[:, cols] = dy
            dg_ref[:, cols] = (t * y_ref[:, cols].astype(f32) * (1.0 - gate)).astype(bf16)
            dz_ref[:, BRANCH * n:BRANCH * (n + 1)] = _dot_nt(dy, wb_ref[n]).astype(bf16)

    wide = pl.BlockSpec((tm, 3 * D_MODEL), lambda i: (i, 0))
    return pl.pallas_call(
        body, name="merge_bwd", grid=(s // tm,),
        in_specs=[pl.BlockSpec((tm, D_MODEL), lambda i: (i, 0)), pl.BlockSpec((D_MODEL, D_MODEL), lambda i: (0, 0)), wide, wide,
                  pl.BlockSpec((3, BRANCH, D_MODEL), lambda i: (0, 0, 0)), pl.BlockSpec(memory_space=pl.ANY)],
        out_specs=[wide, wide, pl.BlockSpec((tm, 3 * BRANCH), lambda i: (i, 0))],
        out_shape=[SDS((s, 3 * D_MODEL), bf16), SDS((s, 3 * D_MODEL), bf16), SDS((s, 3 * BRANCH), bf16)],
        compiler_params=_cp("parallel"),
    )(dx1b, wo, y, pg, wb, after)


def _mixer_bwd(p, dz, opre, states, lb, gout, wconv, lng, lnb, wsg, bsg_t):
    s = p.shape[0]
    tt = MIX_TILE
    nt = s // tt
    nch = tt // HGRN_CHUNK
    rev = lambda i: nt - 1 - i

    def body(q_ref, fp_ref, iv_ref, go_ref, bg_ref, cg_ref, xc_ref, u_ref, v_ref, cgp_ref, xcp_ref, dz_ref, opre_ref, st_ref,
             lb_ref, gout_ref, wconv_ref, lng_ref, lnb_ref, wsg_ref, bsg_ref,
             dp_ref, vec_ref, dwsg_ref, dbsg_ref, dst_scr, zbuf, dybuf, dvn_scr, dbsg_acc):
        i = pl.program_id(0)

        @pl.when(i == 0)
        def _():
            dst_scr[...] = jnp.zeros_like(dst_scr)
            dybuf[tt:tt + 8, :] = jnp.zeros((8, BRANCH), f32)
            vec_ref[...] = jnp.zeros_like(vec_ref)
            dwsg_ref[...] = jnp.zeros_like(dwsg_ref)
            dbsg_acc[...] = jnp.zeros_like(dbsg_acc)

        lbv = lb_ref[...]
        q_raw, fp = q_ref[...], fp_ref[...]
        sq = _sigmoid(q_raw)
        qs = q_raw * sq
        sfp = _sigmoid(fp)
        logf, snf, kk = _hgrn_gates(fp, lbv)
        inv_f = jnp.exp(-logf)
        iv = iv_ref[...]
        doa = dz_ref[:, 0:BRANCH].astype(f32)
        o = opre_ref[...]
        sgo = _sigmoid(go_ref[...])
        gout_v = gout_ref[...]
        d_o, dgo, dgout = [], [], []
        for h in range(HEADS):
            sl = slice(HEAD_DIM * h, HEAD_DIM * (h + 1))
            r, oh = _rms_stats(o[:, sl])
            d_on = doa[:, sl] * sgo[:, sl]
            dgo.append(doa[:, sl] * oh * gout_v[:, sl] * sgo[:, sl] * (1.0 - sgo[:, sl]))
            dx, dg = _rms_bwd(d_on, oh, r, gout_v[:, sl])
            d_o.append(dx)
            dgout.append(dg)
        d_o = jnp.concatenate(d_o, axis=1)
        dp_ref[:, 3 * BRANCH:4 * BRANCH] = jnp.concatenate(dgo, axis=1).astype(bf16)
        vec_ref[1:2, :] += jnp.concatenate(dgout, axis=1)

        causal = _tri(HGRN_CHUNK)
        tri = causal.astype(f32)
        tri_up = _tri(HGRN_CHUNK, upper=True).astype(f32)
        last_row = lax.broadcasted_iota(jnp.int32, (HGRN_CHUNK, 1), 0) == HGRN_CHUNK - 1
        lb_live = (lbv > LB_FLOOR).astype(f32)
        dlb = jnp.zeros((1, BRANCH), f32)
        for c in reversed(range(nch)):
            rows = slice(HGRN_CHUNK * c, HGRN_CHUNK * (c + 1))
            b = _dot_exact(tri, logf[rows])
            bl = jnp.sum(jnp.where(last_row, b, 0.0), axis=0, keepdims=True)
            eb, enb, edl, ebl = jnp.exp(b), jnp.exp(-b), jnp.exp(bl - b), jnp.exp(bl)
            qbf, kbf, kdf = qs[rows] * eb, kk[rows] * enb, kk[rows] * edl
            qb, kb, kd = qbf.astype(bf16), kbf.astype(bf16), kdf.astype(bf16)
            vc = iv[rows].astype(bf16)
            dob = d_o[rows].astype(bf16)
            dv, dqb, dkb, dkd, debl = [], [], [], [], []
            for h in range(HEADS):
                sl = slice(HEAD_DIM * h, HEAD_DIM * (h + 1))
                st = st_ref[c, h]
                dst = dst_scr[h]
                stb, dstb = st.astype(bf16), dst.astype(bf16)
                a = jnp.where(causal, _dot_nt(qb[:, sl], kb[:, sl]), 0.0).astype(bf16)
                da = jnp.where(causal, _dot_nt(dob[:, sl], vc[:, sl]), 0.0).astype(bf16)
                dv.append(_dot_tn(a, dob[:, sl]) + _dot_nt(kd[:, sl], dstb))
                dqb.append(_dot(dob[:, sl], stb) + _dot(da, kb[:, sl]))
                dkb.append(_dot_tn(da, qb[:, sl]))
                dkd.append(_dot(vc[:, sl], dstb))
                debl.append(jnp.sum(st * dst, axis=0, keepdims=True))
                dst_scr[h] = _dot_tn(dob[:, sl], qb[:, sl]) + dst * ebl[:, sl]
            dv, dqb, dkb, dkd = (jnp.concatenate(t, axis=1) for t in (dv, dqb, dkb, dkd))
            debl = jnp.concatenate(debl, axis=1)
            t_kd = dkd * kdf
            dbl = ebl * debl + jnp.sum(t_kd, axis=0, keepdims=True)
            db = dqb * qbf - dkb * kbf - t_kd + jnp.where(last_row, dbl, 0.0)
            dkk = dkb * enb + dkd * edl
            dlc = _dot_exact(tri_up, db)
            sq_c, q_c, sfp_c, snf_c, invf_c = sq[rows], q_raw[rows], sfp[rows], snf[rows], inv_f[rows]
            slope = (1.0 - lbv) * sfp_c * snf_c
            dp_ref[rows, 0:BRANCH] = (dqb * eb * sq_c * (1.0 + q_c * (1.0 - sq_c))).astype(bf16)
            dp_ref[rows, BRANCH:2 * BRANCH] = (slope * (dlc * invf_c - dkk)).astype(bf16)
            dp_ref[rows, 2 * BRANCH:3 * BRANCH] = dv.astype(bf16)
            dlb = dlb + jnp.sum(dlc * (lb_live - sfp_c) * invf_c - dkk * snf_c, axis=0, keepdims=True)
        vec_ref[0:1, :] += dlb

        dob_ = dz_ref[:, BRANCH:2 * BRANCH].astype(f32)
        bg, cg, xc = bg_ref[...], cg_ref[...], xc_ref[...]
        zc = cg * xc
        zbuf[0:8, :] = jnp.where(i < nt - 1, cgp_ref[...] * xcp_ref[...], 0.0)
        zbuf[8:8 + tt, :] = zc
        w0, w1, w2 = wconv_ref[0:1, :], wconv_ref[1:2, :], wconv_ref[2:3, :]
        y = w0 * zbuf[pl.ds(6, tt), :] + w1 * zbuf[pl.ds(7, tt), :] + w2 * zc
        dy = dob_ * bg
        dybuf[0:tt, :] = dy
        dy1, dy2 = dybuf[pl.ds(1, tt), :], dybuf[pl.ds(2, tt), :]
        dzc = w2 * dy + w1 * dy1 + w0 * dy2
        dp_ref[:, 4 * BRANCH:5 * BRANCH] = (dob_ * y).astype(bf16)
        dp_ref[:, 5 * BRANCH:6 * BRANCH] = (dzc * xc).astype(bf16)
        dp_ref[:, 6 * BRANCH:7 * BRANCH] = (dzc * cg).astype(bf16)
        vec_ref[4:5, :] += jnp.sum(zc * dy2, axis=0, keepdims=True)
        vec_ref[5:6, :] += jnp.sum(zc * dy1, axis=0, keepdims=True)
        vec_ref[6:7, :] += jnp.sum(zc * dy, axis=0, keepdims=True)
        dybuf[tt:tt + 8, :] = dybuf[0:8, :]

        doc = dz_ref[:, 2 * BRANCH:3 * BRANCH].astype(f32)
        u_raw, v_raw = u_ref[...], v_ref[...]
        ug = _gelu(u_raw)
        dug_scale = _gelu_grad(u_raw)
        vg = _gelu(v_raw)
        vcen = vg - jnp.mean(vg, axis=-1, keepdims=True)
        rstd = lax.rsqrt(jnp.mean(vcen * vcen, axis=-1, keepdims=True) + LN_EPS)
        vhat = vcen * rstd
        lng_v = lng_ref[...]
        vn = (vhat * lng_v + lnb_ref[...]).astype(bf16)
        low = _tri(SG_CHUNK)
        for g in range(SG_GROUPS):
            sl = slice(LANE * g, LANE * (g + 1))
            wm = jnp.where(low, wsg_ref[g], 0.0).astype(bf16)
            bias = bsg_ref[:, g:g + 1]
            dw = jnp.zeros((SG_CHUNK, SG_CHUNK), f32)
            dbs = jnp.zeros((SG_CHUNK, LANE), f32)
            for cc in range(tt // SG_CHUNK):
                rows = slice(SG_CHUNK * cc, SG_CHUNK * (cc + 1))
                vn_c = vn[rows, sl]
                sv = _dot(wm, vn_c) + bias
                doc_c = doc[rows, sl]
                dp_ref[rows, 7 * BRANCH + LANE * g:7 * BRANCH + LANE * (g + 1)] = (doc_c * sv * dug_scale[rows, sl]).astype(bf16)
                dsv = doc_c * ug[rows, sl]
                dsvb = dsv.astype(bf16)
                dbs = dbs + dsv
                dw = dw + _dot_nt(dsvb, vn_c)
                dvn_scr[rows, sl] = _dot_tn(wm, dsvb)
            dwsg_ref[g] += jnp.where(low, dw, 0.0)
            dbsg_acc[:, sl] += dbs
        dvn = dvn_scr[...]
        vec_ref[2:3, :] += jnp.sum(dvn * vhat, axis=0, keepdims=True)
        vec_ref[3:4, :] += jnp.sum(dvn, axis=0, keepdims=True)
        dvh = dvn * lng_v
        dvg = rstd * (dvh - jnp.mean(dvh, axis=-1, keepdims=True) - vhat * jnp.mean(dvh * vhat, axis=-1, keepdims=True))
        dp_ref[:, 8 * BRANCH:9 * BRANCH] = (dvg * _gelu_grad(v_raw)).astype(bf16)

        @pl.when(i == nt - 1)
        def _():
            for g in range(SG_GROUPS):
                dbsg_ref[:, g:g + 1] = jnp.sum(dbsg_acc[:, LANE * g:LANE * (g + 1)], axis=1, keepdims=True)

    full = lambda shape: pl.BlockSpec(shape, lambda i: (0,) * len(shape))
    tail = lambda c: pl.BlockSpec((8, BRANCH), lambda i: (jnp.maximum(rev(i) * (tt // 8) - 1, 0), c))
    return pl.pallas_call(
        body, name="mixer_bwd", grid=(nt,),
        in_specs=_p_specs(tt, range(9), rev) + [tail(5), tail(6), pl.BlockSpec((tt, 3 * BRANCH), lambda i: (rev(i), 0)),
                                                pl.BlockSpec((tt, BRANCH), lambda i: (rev(i), 0)),
                                                pl.BlockSpec((nch, HEADS, HEAD_DIM, HEAD_DIM), lambda i: (rev(i), 0, 0, 0)),
                                                full((1, BRANCH)), full((1, BRANCH)), full((3, BRANCH)), full((1, BRANCH)), full((1, BRANCH)),
                                                full((SG_GROUPS, SG_CHUNK, SG_CHUNK)), full((SG_CHUNK, SG_GROUPS))],
        out_specs=[pl.BlockSpec((tt, 9 * BRANCH), lambda i: (rev(i), 0)), full((8, BRANCH)), full((SG_GROUPS, SG_CHUNK, SG_CHUNK)),
                   full((SG_CHUNK, SG_GROUPS))],
        out_shape=[SDS((s, 9 * BRANCH), bf16), SDS((8, BRANCH), f32), SDS((SG_GROUPS, SG_CHUNK, SG_CHUNK), f32), SDS((SG_CHUNK, SG_GROUPS), f32)],
        scratch_shapes=[pltpu.VMEM((HEADS, HEAD_DIM, HEAD_DIM), f32), pltpu.VMEM((tt + 8, BRANCH), f32), pltpu.VMEM((tt + 8, BRANCH), f32),
                        pltpu.VMEM((tt, BRANCH), f32), pltpu.VMEM((SG_CHUNK, BRANCH), f32)],
        compiler_params=_cp("arbitrary"),
    )(*([p] * 11), dz, opre, states, lb, gout, wconv, lng, lnb, wsg, bsg_t)


def _dh_bwd(dpm, dpg, w_t, x, dx1, g, after, tm=1024, tk=1536):
    s = x.shape[0]
    km = dpm.shape[1] // tk
    nk = km + dpg.shape[1] // tk

    def body(dpm_ref, dpg_ref, w_ref, x_ref, dx1_ref, g_ref, after_ref, dx_ref, dxb_ref, dg_ref, acc):
        del after_ref
        i, k = pl.program_id(0), pl.program_id(1)

        @pl.when(k == 0)
        def _():
            acc[...] = jnp.zeros_like(acc)

        @pl.when(k < km)
        def _():
            acc[...] += _dot(dpm_ref[...], w_ref[...])

        @pl.when(k >= km)
        def _():
            acc[...] += _dot(dpg_ref[...], w_ref[...])

        @pl.when(k == nk - 1)
        def _():
            r, xh = _rms_stats(x_ref[...])
            dx, dg = _rms_bwd(acc[...], xh, r, g_ref[...])
            dx = dx + dx1_ref[...]
            dx_ref[...] = dx
            dxb_ref[...] = dx.astype(bf16)
            _acc_rows(dg_ref, i == 0, dg)

    row = pl.BlockSpec((tm, D_MODEL), lambda i, k: (i, 0))
    vec = pl.BlockSpec((1, D_MODEL), lambda i, k: (0, 0))
    return pl.pallas_call(
        body, name="dh_bwd", grid=(s // tm, nk),
        in_specs=[pl.BlockSpec((tm, tk), lambda i, k: (i, jnp.minimum(k, km - 1))),
                  pl.BlockSpec((tm, tk), lambda i, k: (i, jnp.maximum(k - km, 0))),
                  pl.BlockSpec((tk, D_MODEL), lambda i, k: (k, 0)), row, row, vec, pl.BlockSpec(memory_space=pl.ANY)],
        out_specs=[row, row, vec], out_shape=[SDS((s, D_MODEL), f32), SDS((s, D_MODEL), bf16), SDS((1, D_MODEL), f32)],
        scratch_shapes=[pltpu.VMEM((tm, D_MODEL), f32)], compiler_params=_cp("arbitrary", "arbitrary"),
    )(dpm, dpg, w_t, x, dx1, g, after)


def _layer_fwd(x, weight, sm):
    p, pg, h = _rms_mm(x, sm["g_mix"], weight("w_in", x))
    z, opre, states = _mixer_fwd(p, sm["lb"], sm["g_out"], sm["w_conv"], sm["ln_g"], sm["ln_b"], sm["w_sg"], sm["b_sg_t"])
    y, merged, x1 = _branch_gate(z, weight("w_branch", z), pg, x, weight("w_o", z))
    x2, h2, ra = _ffn(x1, sm["g_ffn"], weight("w_ff1", x1), weight("w_ff2", x1))
    saved = dict(x=x, p=p, pg=pg, h=h, z=z, opre=opre, states=states, y=y, merged=merged, x1=x1, h2=h2, ra=ra)
    return x2, saved


def _layer_bwd(dx2, dx2b, sv, w, sm, between, before_end):
    nchip = N_DEV // 2
    by_chip = lambda g: g.reshape((nchip, 2) + g.shape[1:])
    da, dx1, dx1b, dg_ffn = _ffn_bwd(dx2, dx2b, sv["x1"], sm["g_ffn"], sv["ra"], w["w_ff1"], w["w_ff2"])
    g_ff2 = by_chip(_mm_tn(sv["ra"], dx2b, 1, D_FF, D_MODEL, 512, 1024, name="dw_ff2", square_a=True)[0]
                    .reshape(N_DEV, D_FF // N_DEV, D_MODEL))
    g_ff1 = by_chip(_mm_tn_slabs(sv["h2"], da, 1, D_MODEL, D_FF // 2, [i * (D_FF // N_DEV) for i in range(nchip)], D_FF // N_DEV,
                                 name="dw_ff1")[:, 0])
    g_o = by_chip(_mm_tn(sv["merged"], dx1b, 1, D_MODEL, D_MODEL, 512, 1024, name="dw_o")[0].reshape(N_DEV, D_MODEL // N_DEV, D_MODEL))
    dy, dpg, dz = _merge_bwd(dx1b, w["w_o"], sv["y"], sv["pg"], w["w_branch"], between(dx1))
    g_branch = by_chip(_mm_tn_slabs(sv["z"], dy, 3, BRANCH, D_MODEL, [i * (D_MODEL // N_DEV) for i in range(N_DEV)], D_MODEL // N_DEV,
                                    name="dw_branch"))
    g_in = _mm_tn(dpg, sv["h"], 1, 3 * D_MODEL, D_MODEL, 512, 1024, name="dw_in_gates", rows=N_COLS, row0=GATE_COL0)
    dpm, vecs, dwsg, dbsg_t = _mixer_bwd(sv["p"], dz, sv["opre"], sv["states"], sm["lb"], sm["g_out"], sm["w_conv"],
                                         sm["ln_g"], sm["ln_b"], sm["w_sg"], sm["b_sg_t"])
    g_in = _mm_tn(dpm, sv["h"], 1, GATE_COL0, D_MODEL, 512, 1024, name="dw_in_mixers", rows=N_COLS, into=g_in)
    g_in = by_chip(g_in[0].reshape(N_DEV, SHARD_IN, D_MODEL))
    big = dict(w_in=g_in, w_branch=g_branch, w_o=g_o, w_ff1=g_ff1, w_ff2=g_ff2)
    dx, dxb, dg_mix = _dh_bwd(dpm, dpg, w["w_in"], sv["x"], dx1, sm["g_mix"], before_end(big))
    small = dict(g_mix=dg_mix, g_ffn=dg_ffn, vecs=vecs, w_sg=dwsg, b_sg_t=dbsg_t, dx1=dx1)
    return dx, dxb, big, small


BIG = ("w_in", "w_branch", "w_o", "w_ff1", "w_ff2")
ANY = pl.BlockSpec(memory_space=pl.ANY)


def _place():
    return lax.axis_index("x"), lax.axis_index("y"), lax.axis_index("c")


def _al(v, m):
    return pl.multiple_of(v * m, m)


def _shard_of(refs, dev, which=range(len(BIG))):
    out = []
    for ref, t in zip(refs, which):
        by_cols = BIG[t] in ("w_branch", "w_ff1")
        n = ref.shape[-1 if by_cols else 0] // N_DEV
        part = pl.ds(_al(dev, n), n)
        out.append(ref.at[(slice(None),) * (len(ref.shape) - 1) + (part,)] if by_cols else ref.at[part])
    return out


def _gather_out_shapes(shards):
    s_in, s_b, s_o, s_1, s_2 = (shards[n] for n in BIG)
    return [SDS((s_in.shape[1] * N_DEV, s_in.shape[2]), bf16), SDS(s_b.shape[1:3] + (s_b.shape[3] * N_DEV,), bf16),
            SDS((s_o.shape[1] * N_DEV, s_o.shape[2]), bf16), SDS((s_1.shape[1], s_1.shape[2] * N_DEV), bf16),
            SDS((s_2.shape[1] * N_DEV, s_2.shape[2]), bf16)]


def _seq_all_gather_layer(layer, which, n_early, shard_refs, out_shapes, tag=""):
    nt = len(which)
    outs = [jax.empty_ref(sh, memory_space=pltpu.MemorySpace.HBM) for sh in out_shapes]
    early, late = tuple(range(n_early)), tuple(range(n_early, nt))

    @pl.kernel(mesh=plsc.ScalarSubcoreMesh(axis_name="seq", num_cores=1), name=f"seq_all_gather_l{layer}{tag}",
               scratch_types=(pltpu.SemaphoreType.DMA((9,)), pltpu.SemaphoreType.DMA((9,))),
               compiler_params=pltpu.CompilerParams(collective_id=1))
    def launch(send_sems, recv_sems):
        x, y, c = _place()
        me, sibling = (x, y, c), (x, y, 1 - c)
        first, second, diag = _ici_route(x, y, c)
        _handshake([sibling, first, second])
        mine = [r.at[layer] for r in shard_refs]

        def copies(k, blk, to, src=None, part=range(nt)):
            dst = _shard_of(outs, 4 * blk[0] + 2 * blk[1] + blk[2], which)
            src = dst if src is None else src
            return [pltpu.make_async_remote_copy(src_ref=src[t], dst_ref=dst[t], send_sem=send_sems.at[k], recv_sem=recv_sems.at[k],
                                                 device_id=to, device_id_type=MESH) for t in part]

        def start(cps):
            for cp in cps:
                cp.start()
            return cps

        def landed(cps):
            for cp in cps:
                cp.wait_recv()

        sent = start(copies(0, me, sibling, src=mine) + copies(1, me, first, src=mine, part=early)
                     + copies(2, me, first, src=mine, part=late) + copies(3, me, second, src=mine))
        landed(copies(1, first, me, part=early))
        sent += start(copies(4, first, second, part=early) + copies(6, first, sibling, part=early))
        landed(copies(2, first, me, part=late))
        sent += start(copies(5, first, second, part=late) + copies(6, first, sibling, part=late))
        landed(copies(3, second, me))
        sent += start(copies(7, second, sibling))
        landed(copies(4, diag, me, part=early) + copies(5, diag, me, part=late))
        sent += start(copies(8, diag, sibling))
        other = lambda p: (p[0], p[1], 1 - c)
        landed(copies(0, sibling, me) + copies(6, other(second), me) + copies(7, other(first), me) + copies(8, other(diag), me))
        for cp in sent:
            cp.wait_send()

    launch()
    return [o[...] for o in outs]


def _ici_route(x, y, c):
    return (x ^ (1 - c), y ^ c, c), (x ^ c, y ^ (1 - c), c), (1 - x, 1 - y, c)


def _place_own(where, which, shards, gathered, after):
    nt = len(which)

    def body(where_ref, *refs):
        del where_ref
        for src, dst in zip(refs[:nt], refs[2 * nt + 1:]):
            dst[...] = src[...]

    in_specs, out_specs = [], []
    for t, sh in zip(which, shards):
        blk = sh.shape[1:]
        in_specs.append(pl.BlockSpec((None,) + blk, functools.partial(lambda nd, i, wh: (wh[0],) + (0,) * nd, len(blk))))
        by_cols = BIG[t] in ("w_branch", "w_ff1")
        out_specs.append(pl.BlockSpec(blk, functools.partial(
            lambda nd, cols, i, wh: (0,) * (nd - 1) + (wh[1],) if cols else (wh[1],) + (0,) * (nd - 1), len(blk), by_cols)))
    return pl.pallas_call(
        body, name="place_own", out_shape=[SDS(g.shape, g.dtype) for g in gathered],
        input_output_aliases={1 + nt + i: i for i in range(nt)}, compiler_params=_cp("arbitrary"),
        grid_spec=pltpu.PrefetchScalarGridSpec(num_scalar_prefetch=1, grid=(1,), in_specs=in_specs + [ANY] * (nt + 1), out_specs=out_specs),
    )(where, *shards, *gathered, after)


def _handshake(peers):
    barrier = pltpu.get_barrier_semaphore()
    for p in peers:
        pl.semaphore_signal(barrier, inc=1, device_id=p, device_id_type=MESH)
    pl.semaphore_wait(barrier, len(peers))


def _seq_exchange_on_chip(grads):
    nt, nchip = len(BIG), N_DEV // 2
    g_refs = [jax.new_ref(g, memory_space=pltpu.MemorySpace.HBM) for g in grads]
    outs = [jax.empty_ref(SDS((nchip,) + g.shape[2:], bf16), memory_space=pltpu.MemorySpace.HBM) for g in grads]

    @pl.kernel(mesh=plsc.ScalarSubcoreMesh(axis_name="seq", num_cores=1), name="seq_rs_on_chip",
               scratch_types=(pltpu.SemaphoreType.DMA((nchip,)), pltpu.SemaphoreType.DMA((nchip,))),
               compiler_params=pltpu.CompilerParams(collective_id=2))
    def launch(send_sems, recv_sems):
        x, y, c = _place()
        sibling = (x, y, 1 - c)
        _handshake([sibling])
        remote = [pltpu.make_async_remote_copy(src_ref=g_refs[t].at[j, 1 - c], dst_ref=outs[t].at[j], send_sem=send_sems.at[j],
                                               recv_sem=recv_sems.at[j], device_id=sibling, device_id_type=MESH)
                  for j in range(nchip) for t in range(nt)]
        for cp in remote:
            cp.start()
        for cp in remote:
            cp.wait_recv()
        for cp in remote:
            cp.wait_send()

    launch()
    return [o[...] for o in outs], [g[...] for g in g_refs]


def _seq_exchange_between_chips(sums):
    nt = len(BIG)
    s_refs = [jax.new_ref(a, memory_space=pltpu.MemorySpace.HBM) for a in sums]
    outs = [jax.empty_ref(SDS((3,) + a.shape[1:], bf16), memory_space=pltpu.MemorySpace.HBM) for a in sums]
    transit = [jax.empty_ref(SDS(a.shape[1:], bf16), memory_space=pltpu.MemorySpace.HBM) for a in sums]

    early, late = (0,), tuple(range(1, nt))

    @pl.kernel(mesh=plsc.ScalarSubcoreMesh(axis_name="seq", num_cores=1), name="seq_rs_between_chips",
               scratch_types=(pltpu.SemaphoreType.DMA((6,)), pltpu.SemaphoreType.DMA((6,))),
               compiler_params=pltpu.CompilerParams(collective_id=3))
    def launch(send_sems, recv_sems):
        x, y, c = _place()
        first, second, diag = _ici_route(x, y, c)
        _handshake([first, second])

        def copies(k, src, dst, to, part=range(nt)):
            return [pltpu.make_async_remote_copy(src_ref=src(t), dst_ref=dst(t), send_sem=send_sems.at[k], recv_sem=recv_sems.at[k],
                                                 device_id=to, device_id_type=MESH) for t in part]

        chip_of = lambda p: 2 * p[0] + p[1]
        for_diag = lambda t: s_refs[t].at[chip_of(diag)]
        through = lambda t: transit[t]
        last = lambda t: outs[t].at[2]
        direct = (copies(0, lambda t: s_refs[t].at[chip_of(first)], lambda t: outs[t].at[0], first)
                  + copies(1, lambda t: s_refs[t].at[chip_of(second)], lambda t: outs[t].at[1], second))
        via = [copies(2, for_diag, through, first, early), copies(3, for_diag, through, first, late)]
        passed = [copies(4, through, last, second, early), copies(5, through, last, second, late)]
        for cp in via[0] + direct + via[1]:
            cp.start()
        for arrived, onward in zip(via, passed):
            for cp in arrived:
                cp.wait_recv()
            for cp in onward:
                cp.start()
        sent = direct + via[0] + via[1] + passed[0] + passed[1]
        for cp in direct + passed[0] + passed[1]:
            cp.wait_recv()
        for cp in sent:
            cp.wait_send()

    launch()
    return [o[...] for o in outs], [a[...] for a in s_refs]


def _chip_sums(core, mine, other, after, steps=2):
    nt, nchip = len(mine), mine[0].shape[0]
    m4 = [a.reshape(nchip, 2, -1, a.shape[-1]) for a in mine]
    o3 = [a.reshape(nchip, -1, a.shape[-1]) for a in other]

    def body(c_ref, *refs):
        del c_ref
        for a_ref, b_ref, o_ref in zip(refs[:nt], refs[nt:2 * nt], refs[2 * nt + 1:]):
            o_ref[...] = (a_ref[...].astype(f32) + b_ref[...].astype(f32)).astype(bf16)

    tiles = [(a.shape[1] // steps, a.shape[2]) for a in o3]
    blks = [pl.BlockSpec((None,) + t, lambda j, i, c_ref: (j, i, 0)) for t in tiles]
    outs = pl.pallas_call(
        body, name="chip_sums", out_shape=[SDS(a.shape, bf16) for a in o3], compiler_params=_cp("parallel", "parallel"),
        grid_spec=pltpu.PrefetchScalarGridSpec(
            num_scalar_prefetch=1, grid=(nchip, steps),
            in_specs=[pl.BlockSpec((None, None) + t, lambda j, i, c_ref: (j, c_ref[0], i, 0)) for t in tiles] + blks + [ANY],
            out_specs=blks),
    )(core, *m4, *o3, after)
    return [o.reshape(a.shape) for o, a in zip(outs, other)]


def _all_reduce_rows(pack):
    rows = pack.shape[0]
    blk = rows // N_DEV

    def body(in_ref, out_ref, land, send1, recv1, send2, recv2):
        x, y, c = _place()
        me = 4 * x + 2 * y + c
        others = [(px, py, pc) for px in range(2) for py in range(2) for pc in range(2)]

        def is_me(p):
            return jnp.logical_and(jnp.logical_and(p[0] == x, p[1] == y), p[2] == c)

        land[me] = in_ref[pl.ds(_al(me, blk), blk), :]
        for d, p in enumerate(others):
            @pl.when(jnp.logical_not(is_me(p)))
            def _():
                pltpu.make_async_remote_copy(src_ref=in_ref.at[pl.ds(d * blk, blk), :], dst_ref=land.at[me], send_sem=send1.at[d],
                                             recv_sem=recv1.at[me], device_id=p, device_id_type=MESH).start()
        for d, p in enumerate(others):
            @pl.when(jnp.logical_not(is_me(p)))
            def _():
                cp = pltpu.make_async_remote_copy(src_ref=in_ref.at[pl.ds(d * blk, blk), :], dst_ref=land.at[d], send_sem=send1.at[d],
                                                  recv_sem=recv1.at[d], device_id=p, device_id_type=MESH)
                cp.wait_recv()
                cp.wait_send()
        total = land[0]
        for d in range(1, N_DEV):
            total = total + land[d]
        out_ref[pl.ds(_al(me, blk), blk), :] = total
        for d, p in enumerate(others):
            @pl.when(jnp.logical_not(is_me(p)))
            def _():
                mine = out_ref.at[pl.ds(_al(me, blk), blk), :]
                pltpu.make_async_remote_copy(src_ref=mine, dst_ref=mine, send_sem=send2.at[d], recv_sem=recv2.at[me],
                                             device_id=p, device_id_type=MESH).start()
        for d, p in enumerate(others):
            @pl.when(jnp.logical_not(is_me(p)))
            def _():
                theirs = out_ref.at[pl.ds(d * blk, blk), :]
                cp = pltpu.make_async_remote_copy(src_ref=theirs, dst_ref=theirs, send_sem=send2.at[d], recv_sem=recv2.at[d],
                                                  device_id=p, device_id_type=MESH)
                cp.wait_recv()
                cp.wait_send()

    vm = pl.BlockSpec(memory_space=pltpu.VMEM)
    return pl.pallas_call(
        body, name="all_reduce_rows", in_specs=[vm], out_specs=vm, out_shape=SDS((rows, LANE), f32),
        scratch_shapes=[pltpu.VMEM((N_DEV, blk, LANE), f32)] + [pltpu.SemaphoreType.DMA((N_DEV,))] * 4,
        compiler_params=pltpu.CompilerParams(vmem_limit_bytes=VMEM_LIMIT),
    )(pack)


def _lower_bounds_fwd(lower):
    def body(l_ref, o_ref):
        sm = _layer_softmax(l_ref)
        run = jnp.zeros_like(sm[0])
        for l in range(DEPTH):
            o_ref[l:l + 1, :] = run
            if l + 1 < DEPTH:
                run = run + sm[l + 1]

    return pl.pallas_call(body, name="lower_bounds_fwd", out_shape=SDS(lower.shape, f32))(lower)


def _layer_softmax(l_ref):
    rows = [l_ref[l:l + 1, :] for l in range(DEPTH)]
    top = functools.reduce(jnp.maximum, rows)
    e = [jnp.exp(r - top) for r in rows]
    tot = functools.reduce(lambda a, b: a + b, e)
    return [v / tot for v in e]


def _lower_bounds_bwd(lower, dlbs):
    def body(l_ref, d_ref, o_ref):
        sm = _layer_softmax(l_ref)
        dsm = [None] * DEPTH
        run = jnp.zeros_like(sm[0])
        dsm[0] = run
        for l in reversed(range(1, DEPTH)):
            run = run + d_ref[l:l + 1, :]
            dsm[l] = run
        inner = functools.reduce(lambda a, b: a + b, [sm[l] * dsm[l] for l in range(DEPTH)])
        for l in range(DEPTH):
            o_ref[l:l + 1, :] = sm[l] * (dsm[l] - inner)

    return pl.pallas_call(body, name="lower_bounds_bwd", out_shape=SDS(lower.shape, f32))(lower, dlbs)


_ADAM_C1 = 1.0 - ADAM_B1 ** ADAM_STEP
_ADAM_C2 = 1.0 - ADAM_B2 ** ADAM_STEP


def _adamw(w, g, m, v):
    m = ADAM_B1 * m + (1.0 - ADAM_B1) * g
    v = ADAM_B2 * v + (1.0 - ADAM_B2) * (g * g)
    delta = -ADAM_LR * ((m / _ADAM_C1) / (jnp.sqrt(v / _ADAM_C2) + ADAM_EPS) + ADAM_WD * w)
    return delta, m, v


def _row_tile(rows, cap):
    return next(t for t in range(min(cap, rows) // 16 * 16, 0, -16) if rows % t == 0)


def _adam_big(where, names, w, m, v, sums, landed, outs, after, steps=4):
    nt = len(names)
    three = lambda a: a.reshape(a.shape[0], -1, a.shape[-1])
    w3, m3, v3 = ([three(d[n]) for n in names] for d in (w, m, v))
    outs3 = [three(a) for n in names for a in outs[n]]
    sums3 = [three(a) for a in sums]
    land3 = [three(a) for a in landed]

    def body(where_ref, *refs):
        del where_ref
        o_refs = refs[5 * nt + 4 * nt + 1:]
        for t in range(nt):
            w_ref, m_ref, v_ref, sum_ref, land_ref = (refs[q * nt + t] for q in range(5))
            g = sum_ref[...].astype(f32)
            for k in range(3):
                g = g + land_ref[k].astype(f32)
            delta, nm, nv = _adamw(w_ref[...], g, m_ref[...], v_ref[...])
            for o_ref, val in zip(o_refs[4 * t:4 * t + 4], (g, delta, nm, nv)):
                o_ref[...] = val

    tiles = [(a.shape[1] // steps, a.shape[2]) for a in w3]
    own = [pl.BlockSpec((None,) + t, lambda i, wh: (wh[0], i, 0)) for t in tiles]
    res = pl.pallas_call(
        body, name="adam_big", out_shape=[SDS(a.shape, f32) for a in outs3],
        input_output_aliases={1 + 5 * nt + i: i for i in range(4 * nt)}, compiler_params=_cp("parallel"),
        grid_spec=pltpu.PrefetchScalarGridSpec(
            num_scalar_prefetch=1, grid=(steps,),
            in_specs=own * 3 + [pl.BlockSpec((None,) + t, lambda i, wh: (wh[1], i, 0)) for t in tiles]
            + [pl.BlockSpec((3,) + t, lambda i, wh: (0, i, 0)) for t in tiles] + [ANY] * (4 * nt + 1),
            out_specs=[s for s in own for _ in range(4)]),
    )(where, *w3, *m3, *v3, *sums3, *land3, *outs3, after)
    return {n: [o.reshape(w[n].shape) for o in res[4 * t:4 * t + 4]] for t, n in enumerate(names)}


def _touch(a, after):
    a2 = a.reshape(-1, a.shape[-1])

    def body(a_ref, after_ref, o_ref):
        del after_ref
        o_ref[...] = a_ref[0:8, :].astype(f32)

    return pl.pallas_call(
        body, name="touch", grid=(1,), in_specs=[pl.BlockSpec((16, LANE), lambda i: (0, 0)), ANY],
        out_specs=pl.BlockSpec((8, LANE), lambda i: (0, 0)), out_shape=SDS((8, LANE), f32),
    )(a2, after)


def _adam_rows(w, g, m, v):
    def body(w_ref, g_ref, m_ref, v_ref, d_ref, nm_ref, nv_ref):
        delta, nm, nv = _adamw(w_ref[...], g_ref[...], m_ref[...], v_ref[...])
        d_ref[...] = delta
        nm_ref[...] = nm
        nv_ref[...] = nv

    return pl.pallas_call(body, name="adam_rows", out_shape=[SDS(w.shape, f32)] * 3)(w, g, m, v)


SMALL = ("g_mix", "lower_bounds", "g_hgrn_out", "w_conv", "sg_ln_g", "sg_ln_b", "w_sg", "b_sg", "g_ffn", "g_final")
WEIGHTS = ("w_in", "g_mix", "lower_bounds", "g_hgrn_out", "w_conv", "sg_ln_g", "sg_ln_b", "w_sg", "b_sg", "w_branch", "w_o", "g_ffn",
           "w_ff1", "w_ff2", "g_final")


def _pack_rows(arrays, multiple):
    flat = jnp.concatenate([a.reshape(-1) for a in arrays])
    rows = -(-flat.shape[0] // (LANE * multiple)) * multiple
    return jnp.pad(flat, (0, rows * LANE - flat.shape[0])).reshape(rows, LANE)


def _unpack_rows(pack, like):
    flat = pack.reshape(-1)
    out, at = [], 0
    for a in like:
        out.append(flat[at:at + a.size].reshape(a.shape))
        at += a.size
    return out


def kernel(x, w_in, g_mix, lower_bounds, g_hgrn_out, w_conv, sg_ln_g, sg_ln_b, w_sg, b_sg, w_branch, w_o, g_ffn, w_ff1, w_ff2, g_final, loss_target, m_w_in, m_g_mix, m_lower_bounds, m_g_hgrn_out, m_w_conv, m_sg_ln_g, m_sg_ln_b, m_w_sg, m_b_sg, m_w_branch, m_w_o, m_g_ffn, m_w_ff1, m_w_ff2, m_g_final, v_w_in, v_g_mix, v_lower_bounds, v_g_hgrn_out, v_w_conv, v_sg_ln_g, v_sg_ln_b, v_w_sg, v_b_sg, v_w_branch, v_w_o, v_g_ffn, v_w_ff1, v_w_ff2, v_g_final):
    weights = dict(w_in=w_in, g_mix=g_mix, lower_bounds=lower_bounds, g_hgrn_out=g_hgrn_out, w_conv=w_conv, sg_ln_g=sg_ln_g,
                   sg_ln_b=sg_ln_b, w_sg=w_sg, b_sg=b_sg, w_branch=w_branch, w_o=w_o, g_ffn=g_ffn, w_ff1=w_ff1, w_ff2=w_ff2, g_final=g_final)
    mom1 = dict(w_in=m_w_in, g_mix=m_g_mix, lower_bounds=m_lower_bounds, g_hgrn_out=m_g_hgrn_out, w_conv=m_w_conv, sg_ln_g=m_sg_ln_g,
                sg_ln_b=m_sg_ln_b, w_sg=m_w_sg, b_sg=m_b_sg, w_branch=m_w_branch, w_o=m_w_o, g_ffn=m_g_ffn, w_ff1=m_w_ff1, w_ff2=m_w_ff2,
                g_final=m_g_final)
    mom2 = dict(w_in=v_w_in, g_mix=v_g_mix, lower_bounds=v_lower_bounds, g_hgrn_out=v_g_hgrn_out, w_conv=v_w_conv, sg_ln_g=v_sg_ln_g,
                sg_ln_b=v_sg_ln_b, w_sg=v_w_sg, b_sg=v_b_sg, w_branch=v_w_branch, w_o=v_w_o, g_ffn=v_g_ffn, w_ff1=v_w_ff1, w_ff2=v_w_ff2,
                g_final=v_g_final)
    xi, yi, ci = _place()
    dev = 4 * xi + 2 * yi + ci
    conv_cols = w_conv.shape[-1]

    for d in (weights, mom1, mom2):
        d["w_in"] = jnp.swapaxes(d["w_in"], 1, 2)
    shards = {n: weights[n].astype(bf16) for n in BIG}

    conv_place = lax.dynamic_update_slice(jnp.zeros((DEPTH, 3, BRANCH), f32), w_conv, (0, 0, dev * conv_cols))
    (w_conv_full,) = _unpack_rows(_all_reduce_rows(_pack_rows([conv_place], 8 * N_DEV)), [conv_place])
    lbs = _lower_bounds_fwd(lower_bounds)

    def small_of(l):
        return dict(g_mix=g_mix[l][None], lb=lbs[l][None], g_out=g_hgrn_out[l][None], w_conv=w_conv_full[l], ln_g=sg_ln_g[l][None],
                    ln_b=sg_ln_b[l][None], w_sg=w_sg[l], b_sg_t=b_sg[l].T, g_ffn=g_ffn[l][None])

    act = x[0]
    full, saved = [], []
    shard_refs = [jax.new_ref(shards[n], memory_space=pltpu.MemorySpace.HBM) for n in BIG]
    shapes = _gather_out_shapes(shards)
    every = tuple(range(len(BIG)))
    groups = [(0, (0,), 1, "a"), (0, (1, 2), 2, "b"), (0, (3, 4), 1, "c"), (1, (0,), 1, "a"), (1, (3, 4, 1, 2), 1, "b")]
    groups += [(l, every, 1, "") for l in range(2, DEPTH)]
    arrived = {}
    for l, which, n_early, tag in groups:
        got = _seq_all_gather_layer(l, which, n_early, [shard_refs[t] for t in which], [shapes[t] for t in which], tag)
        arrived.update({(l, BIG[t]): (which, got) for t in which})

    for l in range(DEPTH):
        full.append({})

        def weight(name, after, l=l):
            if name not in full[l]:
                which, got = arrived[(l, name)]
                where = jnp.stack([jnp.int32(l), dev.astype(jnp.int32)])
                full[l].update(zip([BIG[t] for t in which], _place_own(where, which, [shards[BIG[t]] for t in which], got, after)))
            return full[l][name]

        act, sv = _layer_fwd(act, weight, small_of(l))
        saved.append(sv)
    loss_row, dx, dxb, dg_final = _final(act, loss_target[0], g_final[None])

    core = ci.astype(jnp.int32)[None]
    big_out = {n: [lax.empty(weights[n].shape, f32) for _ in range(4)] for n in BIG}
    small_grads = [None] * DEPTH

    def chip_sums(stage, after):
        l, received, mine = stage
        sums = _chip_sums(core, mine, received, after)
        placed.append(sums[BIG.index("w_o")])
        landed, sums = _seq_exchange_between_chips(sums)
        return l, sums, landed

    def adam_layer(stage, after):
        l, sums, landed = stage
        where = jnp.stack([jnp.int32(l), (2 * xi + yi).astype(jnp.int32)])
        big_out.update(_adam_big(where, BIG, weights, mom1, mom2, sums, landed, big_out, after))

    above = None
    placed = []
    for l in reversed(range(DEPTH)):
        summed = []

        def between(dx1):
            if above is None:
                return dx1
            summed.append(chip_sums(above, dx1))
            return placed[-1]

        def before_end(big):
            return _touch(summed[0][2][BIG.index("w_o")], big["w_in"]) if summed else big["w_in"]

        dx, dxb, big, small_grads[l] = _layer_bwd(dx, dxb, saved[l], full[l], small_of(l), between, before_end)
        if summed:
            adam_layer(summed[0], dx)
        above = (l, *_seq_exchange_on_chip([big[n] for n in BIG]))

    stack = lambda f: jnp.stack([f(small_grads[l]) for l in range(DEPTH)])
    d_lower = _lower_bounds_bwd(lower_bounds, stack(lambda s: s["vecs"][0]))
    local_small = dict(g_mix=stack(lambda s: s["g_mix"][0]), lower_bounds=d_lower, g_hgrn_out=stack(lambda s: s["vecs"][1]),
                       w_conv=stack(lambda s: s["vecs"][4:7]), sg_ln_g=stack(lambda s: s["vecs"][2]), sg_ln_b=stack(lambda s: s["vecs"][3]),
                       w_sg=stack(lambda s: s["w_sg"]), b_sg=stack(lambda s: s["b_sg_t"].T), g_ffn=stack(lambda s: s["g_ffn"][0]),
                       g_final=dg_final[0])
    order = [local_small[n] for n in SMALL] + [loss_row]
    *reduced, loss_sum = _unpack_rows(_all_reduce_rows(_pack_rows(order, 8 * N_DEV)), order)
    loss = loss_sum[0, 0]
    grads = dict(zip(SMALL, reduced))
    grads["w_conv"] = lax.dynamic_slice(grads["w_conv"], (0, 0, dev * conv_cols), (DEPTH, 3, conv_cols))

    deltas, new_m, new_v = {}, {}, {}
    packs = [_pack_rows([d[n] for n in SMALL], 8) for d in (weights, grads, mom1, mom2)]
    like = [weights[n] for n in SMALL]
    small_out = _adam_rows(*packs)
    for out, pack in zip((deltas, new_m, new_v), small_out):
        out.update(zip(SMALL, _unpack_rows(pack, like)))
    adam_layer(chip_sums(above, dx), small_out[0])
    for n in BIG:
        grads[n], deltas[n], new_m[n], new_v[n] = (jnp.swapaxes(a, 1, 2) if n == "w_in" else a for a in big_out[n])

    return (loss, dx[None], *[grads[n] for n in WEIGHTS], *[deltas[n] for n in WEIGHTS], *[new_m[n] for n in WEIGHTS],
            *[new_v[n] for n in WEIGHTS])
```

```python
import functools

import jax
import jax.numpy as jnp
from jax import lax
from jax.experimental import pallas as pl
from jax.experimental.pallas import tpu as pltpu
from jax.experimental.pallas import tpu_sc as plsc

f32 = jnp.float32
bf16 = jnp.bfloat16
SDS = jax.ShapeDtypeStruct
MESH = pl.DeviceIdType.MESH

D_MODEL = 1024
BRANCH = 512
N_COLS = 7680
D_FF = 4096
DEPTH = 4
HEADS = 4
HEAD_DIM = 128
HGRN_CHUNK = 64
SG_CHUNK = 128
SG_GROUPS = 4
NORM_EPS = 1e-6
LN_EPS = 1e-5
LB_FLOOR = 1e-30
N_DEV = 8
SHARD_IN = N_COLS // N_DEV
LANE = 128
GATE_COL0 = 9 * BRANCH

ADAM_LR = 0.001
ADAM_B1 = 0.9
ADAM_B2 = 0.999
ADAM_EPS = 1e-08
ADAM_WD = 0.01
ADAM_STEP = 10

MIX_TILE = 512
VMEM_LIMIT = 56 * 1024 * 1024


def _cp(*sem):
    return pltpu.CompilerParams(dimension_semantics=sem or None, vmem_limit_bytes=VMEM_LIMIT)


def _dot(a, b):
    return jnp.dot(a, b, preferred_element_type=f32)


def _dot_nt(a, b):
    return lax.dot_general(a, b, (((1,), (1,)), ((), ())), preferred_element_type=f32)


def _dot_tn(a, b):
    return lax.dot_general(a, b, (((0,), (0,)), ((), ())), preferred_element_type=f32)


def _dot_exact(ones, b):
    hi = b.astype(bf16)
    rest = b - hi.astype(f32)
    mid = rest.astype(bf16)
    low = (rest - mid.astype(f32)).astype(bf16)
    ones = ones.astype(bf16)
    return _dot(ones, hi) + _dot(ones, mid) + _dot(ones, low)


def _sigmoid(x):
    return jax.nn.sigmoid(x)


_GELU_C = 0.7978845608028654
_GELU_A = 0.044715


def _gelu(x):
    return 0.5 * x * (1.0 + jnp.tanh(_GELU_C * (x + _GELU_A * x * x * x)))


def _gelu_grad(x):
    x2 = x * x
    t = jnp.tanh(_GELU_C * (x + _GELU_A * x * x2))
    return 0.5 * (1.0 + t) + 0.5 * x * (1.0 - t * t) * _GELU_C * (1.0 + 3.0 * _GELU_A * x2)


def _rms_stats(x):
    r = lax.rsqrt(jnp.mean(x * x, axis=-1, keepdims=True) + NORM_EPS)
    return r, x * r


def _rms_bwd(dh, xh, r, g):
    dg = jnp.sum(dh * xh, axis=0, keepdims=True)
    dxn = dh * g
    dx = r * (dxn - xh * jnp.mean(dxn * xh, axis=-1, keepdims=True))
    return dx, dg


def _tri(n, upper=False):
    r = lax.broadcasted_iota(jnp.int32, (n, n), 0)
    c = lax.broadcasted_iota(jnp.int32, (n, n), 1)
    return (c >= r) if upper else (c <= r)


def _acc_rows(ref, first, val):
    @pl.when(first)
    def _():
        ref[...] = val

    @pl.when(jnp.logical_not(first))
    def _():
        ref[...] += val


def _rms_mm(x, g, w_t, tm=1024, tn=1536):
    s, n = x.shape[0], w_t.shape[0]
    jm = GATE_COL0 // tn

    def body(x_ref, g_ref, w_ref, pm_ref, pg_ref, h_ref, hs):
        j = pl.program_id(1)

        @pl.when(j == 0)
        def _():
            _, xh = _rms_stats(x_ref[...])
            hv = (xh * g_ref[...]).astype(bf16)
            hs[...] = hv
            h_ref[...] = hv

        res = _dot_nt(hs[...], w_ref[...])

        @pl.when(j < jm)
        def _():
            pm_ref[...] = res

        @pl.when(j >= jm)
        def _():
            pg_ref[...] = res.astype(bf16)

    return pl.pallas_call(
        body, name="rms_mm", grid=(s // tm, n // tn),
        in_specs=[pl.BlockSpec((tm, D_MODEL), lambda i, j: (i, 0)), pl.BlockSpec((1, D_MODEL), lambda i, j: (0, 0)),
                  pl.BlockSpec((tn, D_MODEL), lambda i, j: (j, 0))],
        out_specs=[pl.BlockSpec((tm, tn), lambda i, j: (i, jnp.minimum(j, jm - 1))),
                   pl.BlockSpec((tm, tn), lambda i, j: (i, jnp.maximum(j - jm, 0))), pl.BlockSpec((tm, D_MODEL), lambda i, j: (i, 0))],
        out_shape=[SDS((s, GATE_COL0), f32), SDS((s, n - GATE_COL0), bf16), SDS((s, D_MODEL), bf16)],
        scratch_shapes=[pltpu.VMEM((tm, D_MODEL), bf16)], compiler_params=_cp("parallel", "arbitrary"),
    )(x, g, w_t)


def _hgrn_gates(fp, lb):
    logf = jnp.logaddexp(jnp.log(jnp.maximum(lb, LB_FLOOR)), jnp.log1p(-lb) + jax.nn.log_sigmoid(fp))
    snf = _sigmoid(-fp)
    return logf, snf, (1.0 - lb) * snf


def _p_specs(tile, cols, row_map):
    return [pl.BlockSpec((tile, BRANCH), functools.partial(lambda c, i: (row_map(i), c), c)) for c in cols]


def _mixer_fwd(p, lb, gout, wconv, lng, lnb, wsg, bsg_t):
    s = p.shape[0]
    tt = MIX_TILE
    nch = tt // HGRN_CHUNK

    def body(q_ref, fp_ref, iv_ref, go_ref, bg_ref, cg_ref, xc_ref, u_ref, v_ref, lb_ref, gout_ref, wconv_ref, lng_ref,
             lnb_ref, wsg_ref, bsg_ref, z_ref, opre_ref, st_ref, st_scr, zbuf):
        @pl.when(pl.program_id(0) == 0)
        def _():
            st_scr[...] = jnp.zeros_like(st_scr)
            zbuf[0:8, :] = jnp.zeros((8, BRANCH), f32)

        lbv = lb_ref[...]
        q_raw = q_ref[...]
        qs = q_raw * _sigmoid(q_raw)
        logf, _, kk = _hgrn_gates(fp_ref[...], lbv)
        iv = iv_ref[...]
        causal = _tri(HGRN_CHUNK)
        tri = causal.astype(f32)
        last_row = lax.broadcasted_iota(jnp.int32, (HGRN_CHUNK, 1), 0) == HGRN_CHUNK - 1
        for c in range(nch):
            rows = slice(HGRN_CHUNK * c, HGRN_CHUNK * (c + 1))
            b = _dot_exact(tri, logf[rows])
            bl = jnp.sum(jnp.where(last_row, b, 0.0), axis=0, keepdims=True)
            qb = (qs[rows] * jnp.exp(b)).astype(bf16)
            kb = (kk[rows] * jnp.exp(-b)).astype(bf16)
            kd = (kk[rows] * jnp.exp(bl - b)).astype(bf16)
            ebl = jnp.exp(bl)
            vc = iv[rows].astype(bf16)
            for h in range(HEADS):
                sl = slice(HEAD_DIM * h, HEAD_DIM * (h + 1))
                st = st_scr[h]
                st_ref[c, h] = st
                a = jnp.where(causal, _dot_nt(qb[:, sl], kb[:, sl]), 0.0)
                opre_ref[rows, sl] = _dot(a.astype(bf16), vc[:, sl]) + _dot_nt(qb[:, sl], st.astype(bf16))
                st_scr[h] = st * ebl[:, sl] + _dot_tn(vc[:, sl], kd[:, sl])

        o = opre_ref[...]
        go = go_ref[...]
        gout_v = gout_ref[...]
        for h in range(HEADS):
            sl = slice(HEAD_DIM * h, HEAD_DIM * (h + 1))
            _, oh = _rms_stats(o[:, sl])
            z_ref[:, sl] = (oh * gout_v[:, sl] * _sigmoid(go[:, sl])).astype(bf16)

        zc = cg_ref[...] * xc_ref[...]
        zbuf[8:8 + tt, :] = zc
        y = wconv_ref[0:1, :] * zbuf[pl.ds(6, tt), :] + wconv_ref[1:2, :] * zbuf[pl.ds(7, tt), :] + wconv_ref[2:3, :] * zc
        z_ref[:, BRANCH:2 * BRANCH] = (bg_ref[...] * y).astype(bf16)
        zbuf[0:8, :] = zbuf[tt:tt + 8, :]

        ug = _gelu(u_ref[...])
        vg = _gelu(v_ref[...])
        vcen = vg - jnp.mean(vg, axis=-1, keepdims=True)
        rstd = lax.rsqrt(jnp.mean(vcen * vcen, axis=-1, keepdims=True) + LN_EPS)
        vn = (vcen * rstd * lng_ref[...] + lnb_ref[...]).astype(bf16)
        low = _tri(SG_CHUNK)
        for g in range(SG_GROUPS):
            sl = slice(LANE * g, LANE * (g + 1))
            wm = jnp.where(low, wsg_ref[g], 0.0).astype(bf16)
            bias = bsg_ref[:, g:g + 1]
            for cc in range(tt // SG_CHUNK):
                rows = slice(SG_CHUNK * cc, SG_CHUNK * (cc + 1))
                sv = _dot(wm, vn[rows, sl]) + bias
                z_ref[rows, 2 * BRANCH + LANE * g:2 * BRANCH + LANE * (g + 1)] = (ug[rows, sl] * sv).astype(bf16)

    full = lambda shape: pl.BlockSpec(shape, lambda i: (0,) * len(shape))
    return pl.pallas_call(
        body, name="mixer_fwd", grid=(s // tt,),
        in_specs=_p_specs(tt, range(9), lambda i: i) + [full((1, BRANCH)), full((1, BRANCH)), full((3, BRANCH)), full((1, BRANCH)),
                                                        full((1, BRANCH)), full((SG_GROUPS, SG_CHUNK, SG_CHUNK)), full((SG_CHUNK, SG_GROUPS))],
        out_specs=[pl.BlockSpec((tt, 3 * BRANCH), lambda i: (i, 0)), pl.BlockSpec((tt, BRANCH), lambda i: (i, 0)),
                   pl.BlockSpec((nch, HEADS, HEAD_DIM, HEAD_DIM), lambda i: (i, 0, 0, 0))],
        out_shape=[SDS((s, 3 * BRANCH), bf16), SDS((s, BRANCH), f32), SDS((s // HGRN_CHUNK, HEADS, HEAD_DIM, HEAD_DIM), f32)],
        scratch_shapes=[pltpu.VMEM((HEADS, HEAD_DIM, HEAD_DIM), f32), pltpu.VMEM((tt + 8, BRANCH), f32)],
        compiler_params=_cp("arbitrary"),
    )(*([p] * 9), lb, gout, wconv, lng, lnb, wsg, bsg_t)


def _branch_gate(z, wb, pg, x, wo, tm=256):
    s = z.shape[0]

    def body(z_ref, wb_ref, g_ref, x_ref, wo_ref, y_ref, m_ref, x1_ref):
        acc = None
        for n in range(3):
            cols = slice(D_MODEL * n, D_MODEL * (n + 1))
            yn = _dot(z_ref[:, BRANCH * n:BRANCH * (n + 1)], wb_ref[n])
            y_ref[:, cols] = yn.astype(bf16)
            t = _sigmoid(g_ref[:, cols].astype(f32)) * yn
            acc = t if acc is None else acc + t
        merged = acc.astype(bf16)
        m_ref[...] = merged
        x1_ref[...] = x_ref[...] + _dot(merged, wo_ref[...])

    row = pl.BlockSpec((tm, D_MODEL), lambda i: (i, 0))
    wide = pl.BlockSpec((tm, 3 * D_MODEL), lambda i: (i, 0))
    return pl.pallas_call(
        body, name="branch_gate", grid=(s // tm,),
        in_specs=[pl.BlockSpec((tm, 3 * BRANCH), lambda i: (i, 0)), pl.BlockSpec((3, BRANCH, D_MODEL), lambda i: (0, 0, 0)), wide, row,
                  pl.BlockSpec((D_MODEL, D_MODEL), lambda i: (0, 0))],
        out_specs=[wide, row, row],
        out_shape=[SDS((s, 3 * D_MODEL), bf16), SDS((s, D_MODEL), bf16), SDS((s, D_MODEL), f32)], compiler_params=_cp("parallel"),
    )(z, wb, pg, x, wo)


def _ffn(x1, g, w1, w2, tm=1024, tf=1024):
    s = x1.shape[0]
    nf = D_FF // tf

    def body(x_ref, g_ref, w1_ref, w2_ref, o_ref, h_ref, ra_ref, hs, acc):
        f = pl.program_id(1)

        @pl.when(f == 0)
        def _():
            _, xh = _rms_stats(x_ref[...])
            hv = (xh * g_ref[...]).astype(bf16)
            hs[...] = hv
            h_ref[...] = hv
            acc[...] = jnp.zeros_like(acc)

        ra = jnp.maximum(_dot(hs[...], w1_ref[...]), 0.0)
        ra_ref[...] = ra.astype(bf16)
        acc[...] += _dot((ra * ra).astype(bf16), w2_ref[...])

        @pl.when(f == nf - 1)
        def _():
            o_ref[...] = x_ref[...] + acc[...]

    return pl.pallas_call(
        body, name="ffn", grid=(s // tm, nf),
        in_specs=[pl.BlockSpec((tm, D_MODEL), lambda i, f: (i, 0)), pl.BlockSpec((1, D_MODEL), lambda i, f: (0, 0)),
                  pl.BlockSpec((D_MODEL, tf), lambda i, f: (0, f)), pl.BlockSpec((tf, D_MODEL), lambda i, f: (f, 0))],
        out_specs=[pl.BlockSpec((tm, D_MODEL), lambda i, f: (i, 0)), pl.BlockSpec((tm, D_MODEL), lambda i, f: (i, 0)),
                   pl.BlockSpec((tm, tf), lambda i, f: (i, f))],
        out_shape=[SDS((s, D_MODEL), f32), SDS((s, D_MODEL), bf16), SDS((s, D_FF), bf16)],
        scratch_shapes=[pltpu.VMEM((tm, D_MODEL), bf16), pltpu.VMEM((tm, D_MODEL), f32)], compiler_params=_cp("parallel", "arbitrary"),
    )(x1, g, w1, w2)


def _final(x, target, g, tm=512):
    s = x.shape[0]

    def body(x_ref, t_ref, g_ref, loss_ref, dx_ref, dxb_ref, dg_ref):
        first = pl.program_id(0) == 0
        gv = g_ref[...]
        r, xh = _rms_stats(x_ref[...])
        e = xh * gv - t_ref[...]
        tile_loss = 0.5 * jnp.sum(jnp.mean(e * e, axis=-1, keepdims=True), axis=0, keepdims=True)
        dx, dg = _rms_bwd(e * (1.0 / D_MODEL), xh, r, gv)
        dx_ref[...] = dx
        dxb_ref[...] = dx.astype(bf16)
        _acc_rows(dg_ref, first, dg)
        _acc_rows(loss_ref, first, jnp.broadcast_to(tile_loss, (1, LANE)))

    row = pl.BlockSpec((tm, D_MODEL), lambda i: (i, 0))
    return pl.pallas_call(
        body, name="final_loss", grid=(s // tm,), in_specs=[row, row, pl.BlockSpec((1, D_MODEL), lambda i: (0, 0))],
        out_specs=[pl.BlockSpec((1, LANE), lambda i: (0, 0)), row, row, pl.BlockSpec((1, D_MODEL), lambda i: (0, 0))],
        out_shape=[SDS((1, LANE), f32), SDS((s, D_MODEL), f32), SDS((s, D_MODEL), bf16), SDS((1, D_MODEL), f32)],
        compiler_params=_cp("arbitrary"),
    )(x, target, g)


def _ffn_bwd(dx2, dx2b, x1, g, ra, w1, w2, tm=512, tf=1024):
    s = x1.shape[0]
    nf = D_FF // tf

    def body(dx_ref, dxb_ref, x_ref, g_ref, ra_ref, w1_ref, w2_ref, da_ref, dx1_ref, dx1b_ref, dg_ref, acc):
        i, f = pl.program_id(0), pl.program_id(1)

        @pl.when(f == 0)
        def _():
            acc[...] = jnp.zeros_like(acc)

        da = (_dot_nt(dxb_ref[...], w2_ref[...]) * (2.0 * ra_ref[...].astype(f32))).astype(bf16)
        da_ref[...] = da
        acc[...] += _dot_nt(da, w1_ref[...])

        @pl.when(f == nf - 1)
        def _():
            r, xh = _rms_stats(x_ref[...])
            dx, dg = _rms_bwd(acc[...], xh, r, g_ref[...])
            dx = dx + dx_ref[...]
            dx1_ref[...] = dx
            dx1b_ref[...] = dx.astype(bf16)
            _acc_rows(dg_ref, i == 0, dg)

    row = pl.BlockSpec((tm, D_MODEL), lambda i, f: (i, 0))
    col = pl.BlockSpec((tm, tf), lambda i, f: (i, f))
    return pl.pallas_call(
        body, name="ffn_bwd", grid=(s // tm, nf),
        in_specs=[row, row, row, pl.BlockSpec((1, D_MODEL), lambda i, f: (0, 0)), col,
                  pl.BlockSpec((D_MODEL, tf), lambda i, f: (0, f)), pl.BlockSpec((tf, D_MODEL), lambda i, f: (f, 0))],
        out_specs=[col, row, row, pl.BlockSpec((1, D_MODEL), lambda i, f: (0, 0))],
        out_shape=[SDS((s, D_FF), bf16), SDS((s, D_MODEL), f32), SDS((s, D_MODEL), bf16), SDS((1, D_MODEL), f32)],
        scratch_shapes=[pltpu.VMEM((tm, D_MODEL), f32)], compiler_params=_cp("arbitrary", "arbitrary"),
    )(dx2, dx2b, x1, g, ra, w1, w2)


def _mm_tn(a, b, nb, m, n, tm, tn, name="mm_tn", rows=None, row0=0, into=None, square_a=False):
    s = a.shape[0]
    mi, nj = m // tm, n // tn
    rows = m if rows is None else rows
    blk0 = row0 // tm

    def body(a_ref, b_ref, *rest):
        av = a_ref[...]
        if square_a:
            av = av.astype(f32)
            av = (av * av).astype(bf16)
        rest[-1][...] = _dot_tn(av, b_ref[...]).astype(bf16)

    extra = {} if into is None else dict(input_output_aliases={2: 0})
    return pl.pallas_call(
        body, name=name, grid=(nb, mi, nj),
        in_specs=[pl.BlockSpec((s, tm), lambda k, i, j: (0, k * mi + i)), pl.BlockSpec((s, tn), lambda k, i, j: (0, k * nj + j))]
        + ([] if into is None else [pl.BlockSpec(memory_space=pl.ANY)]),
        out_specs=pl.BlockSpec((None, tm, tn), lambda k, i, j: (k, blk0 + i, j)), out_shape=SDS((nb, rows, n), bf16),
        compiler_params=_cp("parallel", "parallel", "parallel"), **extra,
    )(a, b, *([] if into is None else [into]))


def _mm_tn_slabs(a, b, nb, m, nblk, rel, width, tm=512, name="mm_tn_slabs"):
    s = a.shape[0]
    n = b.shape[1] // nb
    ng, mi, nw = n // nblk, m // tm, len(rel)

    def body(a_ref, b_ref, o_ref):
        full = _dot_tn(a_ref[...], b_ref[...])
        for r, start in enumerate(rel):
            o_ref[r] = full[:, start:start + width].astype(bf16)

    return pl.pallas_call(
        body, name=name, grid=(nb, ng, mi),
        in_specs=[pl.BlockSpec((s, tm), lambda k, g, i: (0, k * mi + i)), pl.BlockSpec((s, nblk), lambda k, g, i: (0, k * ng + g))],
        out_specs=pl.BlockSpec((nw, None, tm, width), lambda k, g, i: (g, k, i, 0)), out_shape=SDS((ng * nw, nb, m, width), bf16),
        compiler_params=_cp("parallel", "parallel", "parallel"),
    )(a, b)


def _merge_bwd(dx1b, wo, y, pg, wb, after, tm=256):
    s = dx1b.shape[0]

    def body(dx_ref, wo_ref, y_ref, g_ref, wb_ref, after_ref, dy_ref, dg_ref, dz_ref):
        del after_ref
        dm = _dot_nt(dx_ref[...], wo_ref[...])
        for n in range(3):
            cols = slice(D_MODEL * n, D_MODEL * (n + 1))
            gate = _sigmoid(g_ref[:, cols].astype(f32))
            t = dm * gate
            dy = t.astype(bf16)
            dy_ref[:, cols] = dy
            dg_ref[:, cols] = (t * y_ref[:, cols].astype(f32) * (1.0 - gate)).astype(bf16)
            dz_ref[:, BRANCH * n:BRANCH * (n + 1)] = _dot_nt(dy, wb_ref[n]).astype(bf16)

    wide = pl.BlockSpec((tm, 3 * D_MODEL), lambda i: (i, 0))
    return pl.pallas_call(
        body, name="merge_bwd", grid=(s // tm,),
        in_specs=[pl.BlockSpec((tm, D_MODEL), lambda i: (i, 0)), pl.BlockSpec((D_MODEL, D_MODEL), lambda i: (0, 0)), wide, wide,
                  pl.BlockSpec((3, BRANCH, D_MODEL), lambda i: (0, 0, 0)), pl.BlockSpec(memory_space=pl.ANY)],
        out_specs=[wide, wide, pl.BlockSpec((tm, 3 * BRANCH), lambda i: (i, 0))],
        out_shape=[SDS((s, 3 * D_MODEL), bf16), SDS((s, 3 * D_MODEL), bf16), SDS((s, 3 * BRANCH), bf16)],
        compiler_params=_cp("parallel"),
    )(dx1b, wo, y, pg, wb, after)


def _mixer_bwd(p, dz, opre, states, lb, gout, wconv, lng, lnb, wsg, bsg_t):
    s = p.shape[0]
    tt = MIX_TILE
    nt = s // tt
    nch = tt // HGRN_CHUNK
    rev = lambda i: nt - 1 - i

    def body(q_ref, fp_ref, iv_ref, go_ref, bg_ref, cg_ref, xc_ref, u_ref, v_ref, cgp_ref, xcp_ref, dz_ref, opre_ref, st_ref,
             lb_ref, gout_ref, wconv_ref, lng_ref, lnb_ref, wsg_ref, bsg_ref,
             dp_ref, vec_ref, dwsg_ref, dbsg_ref, dst_scr, zbuf, dybuf, dvn_scr, dbsg_acc):
        i = pl.program_id(0)

        @pl.when(i == 0)
        def _():
            dst_scr[...] = jnp.zeros_like(dst_scr)
            dybuf[tt:tt + 8, :] = jnp.zeros((8, BRANCH), f32)
            vec_ref[...] = jnp.zeros_like(vec_ref)
            dwsg_ref[...] = jnp.zeros_like(dwsg_ref)
            dbsg_acc[...] = jnp.zeros_like(dbsg_acc)

        lbv = lb_ref[...]
        q_raw, fp = q_ref[...], fp_ref[...]
        sq = _sigmoid(q_raw)
        qs = q_raw * sq
        sfp = _sigmoid(fp)
        logf, snf, kk = _hgrn_gates(fp, lbv)
        inv_f = jnp.exp(-logf)
        iv = iv_ref[...]
        doa = dz_ref[:, 0:BRANCH].astype(f32)
        o = opre_ref[...]
        sgo = _sigmoid(go_ref[...])
        gout_v = gout_ref[...]
        d_o, dgo, dgout = [], [], []
        for h in range(HEADS):
            sl = slice(HEAD_DIM * h, HEAD_DIM * (h + 1))
            r, oh = _rms_stats(o[:, sl])
            d_on = doa[:, sl] * sgo[:, sl]
            dgo.append(doa[:, sl] * oh * gout_v[:, sl] * sgo[:, sl] * (1.0 - sgo[:, sl]))
            dx, dg = _rms_bwd(d_on, oh, r, gout_v[:, sl])
            d_o.append(dx)
            dgout.append(dg)
        d_o = jnp.concatenate(d_o, axis=1)
        dp_ref[:, 3 * BRANCH:4 * BRANCH] = jnp.concatenate(dgo, axis=1).astype(bf16)
        vec_ref[1:2, :] += jnp.concatenate(dgout, axis=1)

        causal = _tri(HGRN_CHUNK)
        tri = causal.astype(f32)
        tri_up = _tri(HGRN_CHUNK, upper=True).astype(f32)
        last_row = lax.broadcasted_iota(jnp.int32, (HGRN_CHUNK, 1), 0) == HGRN_CHUNK - 1
        lb_live = (lbv > LB_FLOOR).astype(f32)
        dlb = jnp.zeros((1, BRANCH), f32)
        for c in reversed(range(nch)):
            rows = slice(HGRN_CHUNK * c, HGRN_CHUNK * (c + 1))
            b = _dot_exact(tri, logf[rows])
            bl = jnp.sum(jnp.where(last_row, b, 0.0), axis=0, keepdims=True)
            eb, enb, edl, ebl = jnp.exp(b), jnp.exp(-b), jnp.exp(bl - b), jnp.exp(bl)
            qbf, kbf, kdf = qs[rows] * eb, kk[rows] * enb, kk[rows] * edl
            qb, kb, kd = qbf.astype(bf16), kbf.astype(bf16), kdf.astype(bf16)
            vc = iv[rows].astype(bf16)
            dob = d_o[rows].astype(bf16)
            dv, dqb, dkb, dkd, debl = [], [], [], [], []
            for h in range(HEADS):
                sl = slice(HEAD_DIM * h, HEAD_DIM * (h + 1))
                st = st_ref[c, h]
                dst = dst_scr[h]
                stb, dstb = st.astype(bf16), dst.astype(bf16)
                a = jnp.where(causal, _dot_nt(qb[:, sl], kb[:, sl]), 0.0).astype(bf16)
                da = jnp.where(causal, _dot_nt(dob[:, sl], vc[:, sl]), 0.0).astype(bf16)
                dv.append(_dot_tn(a, dob[:, sl]) + _dot_nt(kd[:, sl], dstb))
                dqb.append(_dot(dob[:, sl], stb) + _dot(da, kb[:, sl]))
                dkb.append(_dot_tn(da, qb[:, sl]))
                dkd.append(_dot(vc[:, sl], dstb))
                debl.append(jnp.sum(st * dst, axis=0, keepdims=True))
                dst_scr[h] = _dot_tn(dob[:, sl], qb[:, sl]) + dst * ebl[:, sl]
            dv, dqb, dkb, dkd = (jnp.concatenate(t, axis=1) for t in (dv, dqb, dkb, dkd))
            debl = jnp.concatenate(debl, axis=1)
            t_kd = dkd * kdf
            dbl = ebl * debl + jnp.sum(t_kd, axis=0, keepdims=True)
            db = dqb * qbf - dkb * kbf - t_kd + jnp.where(last_row, dbl, 0.0)
            dkk = dkb * enb + dkd * edl
            dlc = _dot_exact(tri_up, db)
            sq_c, q_c, sfp_c, snf_c, invf_c = sq[rows], q_raw[rows], sfp[rows], snf[rows], inv_f[rows]
            slope = (1.0 - lbv) * sfp_c * snf_c
            dp_ref[rows, 0:BRANCH] = (dqb * eb * sq_c * (1.0 + q_c * (1.0 - sq_c))).astype(bf16)
            dp_ref[rows, BRANCH:2 * BRANCH] = (slope * (dlc * invf_c - dkk)).astype(bf16)
            dp_ref[rows, 2 * BRANCH:3 * BRANCH] = dv.astype(bf16)
            dlb = dlb + jnp.sum(dlc * (lb_live - sfp_c) * invf_c - dkk * snf_c, axis=0, keepdims=True)
        vec_ref[0:1, :] += dlb

        dob_ = dz_ref[:, BRANCH:2 * BRANCH].astype(f32)
        bg, cg, xc = bg_ref[...], cg_ref[...], xc_ref[...]
        zc = cg * xc
        zbuf[0:8, :] = jnp.where(i < nt - 1, cgp_ref[...] * xcp_ref[...], 0.0)
        zbuf[8:8 + tt, :] = zc
        w0, w1, w2 = wconv_ref[0:1, :], wconv_ref[1:2, :], wconv_ref[2:3, :]
        y = w0 * zbuf[pl.ds(6, tt), :] + w1 * zbuf[pl.ds(7, tt), :] + w2 * zc
        dy = dob_ * bg
        dybuf[0:tt, :] = dy
        dy1, dy2 = dybuf[pl.ds(1, tt), :], dybuf[pl.ds(2, tt), :]
        dzc = w2 * dy + w1 * dy1 + w0 * dy2
        dp_ref[:, 4 * BRANCH:5 * BRANCH] = (dob_ * y).astype(bf16)
        dp_ref[:, 5 * BRANCH:6 * BRANCH] = (dzc * xc).astype(bf16)
        dp_ref[:, 6 * BRANCH:7 * BRANCH] = (dzc * cg).astype(bf16)
        vec_ref[4:5, :] += jnp.sum(zc * dy2, axis=0, keepdims=True)
        vec_ref[5:6, :] += jnp.sum(zc * dy1, axis=0, keepdims=True)
        vec_ref[6:7, :] += jnp.sum(zc * dy, axis=0, keepdims=True)
        dybuf[tt:tt + 8, :] = dybuf[0:8, :]

        doc = dz_ref[:, 2 * BRANCH:3 * BRANCH].astype(f32)
        u_raw, v_raw = u_ref[...], v_ref[...]
        ug = _gelu(u_raw)
        dug_scale = _gelu_grad(u_raw)
        vg = _gelu(v_raw)
        vcen = vg - jnp.mean(vg, axis=-1, keepdims=True)
        rstd = lax.rsqrt(jnp.mean(vcen * vcen, axis=-1, keepdims=True) + LN_EPS)
        vhat = vcen * rstd
        lng_v = lng_ref[...]
        vn = (vhat * lng_v + lnb_ref[...]).astype(bf16)
        low = _tri(SG_CHUNK)
        for g in range(SG_GROUPS):
            sl = slice(LANE * g, LANE * (g + 1))
            wm = jnp.where(low, wsg_ref[g], 0.0).astype(bf16)
            bias = bsg_ref[:, g:g + 1]
            dw = jnp.zeros((SG_CHUNK, SG_CHUNK), f32)
            dbs = jnp.zeros((SG_CHUNK, LANE), f32)
            for cc in range(tt // SG_CHUNK):
                rows = slice(SG_CHUNK * cc, SG_CHUNK * (cc + 1))
                vn_c = vn[rows, sl]
                sv = _dot(wm, vn_c) + bias
                doc_c = doc[rows, sl]
                dp_ref[rows, 7 * BRANCH + LANE * g:7 * BRANCH + LANE * (g + 1)] = (doc_c * sv * dug_scale[rows, sl]).astype(bf16)
                dsv = doc_c * ug[rows, sl]
                dsvb = dsv.astype(bf16)
                dbs = dbs + dsv
                dw = dw + _dot_nt(dsvb, vn_c)
                dvn_scr[rows, sl] = _dot_tn(wm, dsvb)
            dwsg_ref[g] += jnp.where(low, dw, 0.0)
            dbsg_acc[:, sl] += dbs
        dvn = dvn_scr[...]
        vec_ref[2:3, :] += jnp.sum(dvn * vhat, axis=0, keepdims=True)
        vec_ref[3:4, :] += jnp.sum(dvn, axis=0, keepdims=True)
        dvh = dvn * lng_v
        dvg = rstd * (dvh - jnp.mean(dvh, axis=-1, keepdims=True) - vhat * jnp.mean(dvh * vhat, axis=-1, keepdims=True))
        dp_ref[:, 8 * BRANCH:9 * BRANCH] = (dvg * _gelu_grad(v_raw)).astype(bf16)

        @pl.when(i == nt - 1)
        def _():
            for g in range(SG_GROUPS):
                dbsg_ref[:, g:g + 1] = jnp.sum(dbsg_acc[:, LANE * g:LANE * (g + 1)], axis=1, keepdims=True)

    full = lambda shape: pl.BlockSpec(shape, lambda i: (0,) * len(shape))
    tail = lambda c: pl.BlockSpec((8, BRANCH), lambda i: (jnp.maximum(rev(i) * (tt // 8) - 1, 0), c))
    return pl.pallas_call(
        body, name="mixer_bwd", grid=(nt,),
        in_specs=_p_specs(tt, range(9), rev) + [tail(5), tail(6), pl.BlockSpec((tt, 3 * BRANCH), lambda i: (rev(i), 0)),
                                                pl.BlockSpec((tt, BRANCH), lambda i: (rev(i), 0)),
                                                pl.BlockSpec((nch, HEADS, HEAD_DIM, HEAD_DIM), lambda i: (rev(i), 0, 0, 0)),
                                                full((1, BRANCH)), full((1, BRANCH)), full((3, BRANCH)), full((1, BRANCH)), full((1, BRANCH)),
                                                full((SG_GROUPS, SG_CHUNK, SG_CHUNK)), full((SG_CHUNK, SG_GROUPS))],
        out_specs=[pl.BlockSpec((tt, 9 * BRANCH), lambda i: (rev(i), 0)), full((8, BRANCH)), full((SG_GROUPS, SG_CHUNK, SG_CHUNK)),
                   full((SG_CHUNK, SG_GROUPS))],
        out_shape=[SDS((s, 9 * BRANCH), bf16), SDS((8, BRANCH), f32), SDS((SG_GROUPS, SG_CHUNK, SG_CHUNK), f32), SDS((SG_CHUNK, SG_GROUPS), f32)],
        scratch_shapes=[pltpu.VMEM((HEADS, HEAD_DIM, HEAD_DIM), f32), pltpu.VMEM((tt + 8, BRANCH), f32), pltpu.VMEM((tt + 8, BRANCH), f32),
                        pltpu.VMEM((tt, BRANCH), f32), pltpu.VMEM((SG_CHUNK, BRANCH), f32)],
        compiler_params=_cp("arbitrary"),
    )(*([p] * 11), dz, opre, states, lb, gout, wconv, lng, lnb, wsg, bsg_t)


def _dh_bwd(dpm, dpg, w_t, x, dx1, g, after, tm=1024, tk=1536):
    s = x.shape[0]
    km = dpm.shape[1] // tk
    nk = km + dpg.shape[1] // tk

    def body(dpm_ref, dpg_ref, w_ref, x_ref, dx1_ref, g_ref, after_ref, dx_ref, dxb_ref, dg_ref, acc):
        del after_ref
        i, k = pl.program_id(0), pl.program_id(1)

        @pl.when(k == 0)
        def _():
            acc[...] = jnp.zeros_like(acc)

        @pl.when(k < km)
        def _():
            acc[...] += _dot(dpm_ref[...], w_ref[...])

        @pl.when(k >= km)
        def _():
            acc[...] += _dot(dpg_ref[...], w_ref[...])

        @pl.when(k == nk - 1)
        def _():
            r, xh = _rms_stats(x_ref[...])
            dx, dg = _rms_bwd(acc[...], xh, r, g_ref[...])
            dx = dx + dx1_ref[...]
            dx_ref[...] = dx
            dxb_ref[...] = dx.astype(bf16)
            _acc_rows(dg_ref, i == 0, dg)

    row = pl.BlockSpec((tm, D_MODEL), lambda i, k: (i, 0))
    vec = pl.BlockSpec((1, D_MODEL), lambda i, k: (0, 0))
    return pl.pallas_call(
        body, name="dh_bwd", grid=(s // tm, nk),
        in_specs=[pl.BlockSpec((tm, tk), lambda i, k: (i, jnp.minimum(k, km - 1))),
                  pl.BlockSpec((tm, tk), lambda i, k: (i, jnp.maximum(k - km, 0))),
                  pl.BlockSpec((tk, D_MODEL), lambda i, k: (k, 0)), row, row, vec, pl.BlockSpec(memory_space=pl.ANY)],
        out_specs=[row, row, vec], out_shape=[SDS((s, D_MODEL), f32), SDS((s, D_MODEL), bf16), SDS((1, D_MODEL), f32)],
        scratch_shapes=[pltpu.VMEM((tm, D_MODEL), f32)], compiler_params=_cp("arbitrary", "arbitrary"),
    )(dpm, dpg, w_t, x, dx1, g, after)


def _layer_fwd(x, weight, sm):
    p, pg, h = _rms_mm(x, sm["g_mix"], weight("w_in", x))
    z, opre, states = _mixer_fwd(p, sm["lb"], sm["g_out"], sm["w_conv"], sm["ln_g"], sm["ln_b"], sm["w_sg"], sm["b_sg_t"])
    y, merged, x1 = _branch_gate(z, weight("w_branch", z), pg, x, weight("w_o", z))
    x2, h2, ra = _ffn(x1, sm["g_ffn"], weight("w_ff1", x1), weight("w_ff2", x1))
    saved = dict(x=x, p=p, pg=pg, h=h, z=z, opre=opre, states=states, y=y, merged=merged, x1=x1, h2=h2, ra=ra)
    return x2, saved


def _layer_bwd(dx2, dx2b, sv, w, sm, between, before_end):
    nchip = N_DEV // 2
    by_chip = lambda g: g.reshape((nchip, 2) + g.shape[1:])
    da, dx1, dx1b, dg_ffn = _ffn_bwd(dx2, dx2b, sv["x1"], sm["g_ffn"], sv["ra"], w["w_ff1"], w["w_ff2"])
    g_ff2 = by_chip(_mm_tn(sv["ra"], dx2b, 1, D_FF, D_MODEL, 512, 1024, name="dw_ff2", square_a=True)[0]
                    .reshape(N_DEV, D_FF // N_DEV, D_MODEL))
    g_ff1 = by_chip(_mm_tn_slabs(sv["h2"], da, 1, D_MODEL, D_FF // 2, [i * (D_FF // N_DEV) for i in range(nchip)], D_FF // N_DEV,
                                 name="dw_ff1")[:, 0])
    g_o = by_chip(_mm_tn(sv["merged"], dx1b, 1, D_MODEL, D_MODEL, 512, 1024, name="dw_o")[0].reshape(N_DEV, D_MODEL // N_DEV, D_MODEL))
    dy, dpg, dz = _merge_bwd(dx1b, w["w_o"], sv["y"], sv["pg"], w["w_branch"], between(dx1))
    g_branch = by_chip(_mm_tn_slabs(sv["z"], dy, 3, BRANCH, D_MODEL, [i * (D_MODEL // N_DEV) for i in range(N_DEV)], D_MODEL // N_DEV,
                                    name="dw_branch"))
    g_in = _mm_tn(dpg, sv["h"], 1, 3 * D_MODEL, D_MODEL, 768, 1024, name="dw_in_gates", rows=N_COLS, row0=GATE_COL0)
    dpm, vecs, dwsg, dbsg_t = _mixer_bwd(sv["p"], dz, sv["opre"], sv["states"], sm["lb"], sm["g_out"], sm["w_conv"],
                                         sm["ln_g"], sm["ln_b"], sm["w_sg"], sm["b_sg_t"])
    g_in = _mm_tn(dpm, sv["h"], 1, GATE_COL0, D_MODEL, 768, 1024, name="dw_in_mixers", rows=N_COLS, into=g_in)
    g_in = by_chip(g_in[0].reshape(N_DEV, SHARD_IN, D_MODEL))
    big = dict(w_in=g_in, w_branch=g_branch, w_o=g_o, w_ff1=g_ff1, w_ff2=g_ff2)
    dx, dxb, dg_mix = _dh_bwd(dpm, dpg, w["w_in"], sv["x"], dx1, sm["g_mix"], before_end(big))
    small = dict(g_mix=dg_mix, g_ffn=dg_ffn, vecs=vecs, w_sg=dwsg, b_sg_t=dbsg_t, dx1=dx1)
    return dx, dxb, big, small


BIG = ("w_in", "w_branch", "w_o", "w_ff1", "w_ff2")
ANY = pl.BlockSpec(memory_space=pl.ANY)


def _place():
    return lax.axis_index("x"), lax.axis_index("y"), lax.axis_index("c")


def _al(v, m):
    return pl.multiple_of(v * m, m)


def _shard_of(refs, dev, which=range(len(BIG))):
    out = []
    for ref, t in zip(refs, which):
        by_cols = BIG[t] in ("w_branch", "w_ff1")
        n = ref.shape[-1 if by_cols else 0] // N_DEV
        part = pl.ds(_al(dev, n), n)
        out.append(ref.at[(slice(None),) * (len(ref.shape) - 1) + (part,)] if by_cols else ref.at[part])
    return out


def _gather_out_shapes(shards):
    s_in, s_b, s_o, s_1, s_2 = (shards[n] for n in BIG)
    return [SDS((s_in.shape[1] * N_DEV, s_in.shape[2]), bf16), SDS(s_b.shape[1:3] + (s_b.shape[3] * N_DEV,), bf16),
            SDS((s_o.shape[1] * N_DEV, s_o.shape[2]), bf16), SDS((s_1.shape[1], s_1.shape[2] * N_DEV), bf16),
            SDS((s_2.shape[1] * N_DEV, s_2.shape[2]), bf16)]


def _seq_all_gather_layer(layer, which, n_early, shard_refs, out_shapes, tag=""):
    nt = len(which)
    outs = [jax.empty_ref(sh, memory_space=pltpu.MemorySpace.HBM) for sh in out_shapes]
    early, late = tuple(range(n_early)), tuple(range(n_early, nt))

    @pl.kernel(mesh=plsc.ScalarSubcoreMesh(axis_name="seq", num_cores=1), name=f"seq_all_gather_l{layer}{tag}",
               scratch_types=(pltpu.SemaphoreType.DMA((9,)), pltpu.SemaphoreType.DMA((9,))),
               compiler_params=pltpu.CompilerParams(collective_id=1))
    def launch(send_sems, recv_sems):
        x, y, c = _place()
        me, sibling = (x, y, c), (x, y, 1 - c)
        first, second, diag = _ici_route(x, y, c)
        _handshake([sibling, first, second])
        mine = [r.at[layer] for r in shard_refs]

        def copies(k, blk, to, src=None, part=range(nt)):
            dst = _shard_of(outs, 4 * blk[0] + 2 * blk[1] + blk[2], which)
            src = dst if src is None else src
            return [pltpu.make_async_remote_copy(src_ref=src[t], dst_ref=dst[t], send_sem=send_sems.at[k], recv_sem=recv_sems.at[k],
                                                 device_id=to, device_id_type=MESH) for t in part]

        def start(cps):
            for cp in cps:
                cp.start()
            return cps

        def landed(cps):
            for cp in cps:
                cp.wait_recv()

        sent = start(copies(0, me, sibling, src=mine) + copies(1, me, first, src=mine, part=early)
                     + copies(2, me, first, src=mine, part=late) + copies(3, me, second, src=mine))
        landed(copies(1, first, me, part=early))
        sent += start(copies(4, first, second, part=early) + copies(6, first, sibling, part=early))
        landed(copies(2, first, me, part=late))
        sent += start(copies(5, first, second, part=late) + copies(6, first, sibling, part=late))
        landed(copies(3, second, me))
        sent += start(copies(7, second, sibling))
        landed(copies(4, diag, me, part=early) + copies(5, diag, me, part=late))
        sent += start(copies(8, diag, sibling))
        other = lambda p: (p[0], p[1], 1 - c)
        landed(copies(0, sibling, me) + copies(6, other(second), me) + copies(7, other(first), me) + copies(8, other(diag), me))
        for cp in sent:
            cp.wait_send()

    launch()
    return [o[...] for o in outs]


def _ici_route(x, y, c):
    return (x ^ (1 - c), y ^ c, c), (x ^ c, y ^ (1 - c), c), (1 - x, 1 - y, c)


def _place_own(where, which, shards, gathered, after):
    nt = len(which)

    def body(where_ref, *refs):
        del where_ref
        for src, dst in zip(refs[:nt], refs[2 * nt + 1:]):
            dst[...] = src[...]

    in_specs, out_specs = [], []
    for t, sh in zip(which, shards):
        blk = sh.shape[1:]
        in_specs.append(pl.BlockSpec((None,) + blk, functools.partial(lambda nd, i, wh: (wh[0],) + (0,) * nd, len(blk))))
        by_cols = BIG[t] in ("w_branch", "w_ff1")
        out_specs.append(pl.BlockSpec(blk, functools.partial(
            lambda nd, cols, i, wh: (0,) * (nd - 1) + (wh[1],) if cols else (wh[1],) + (0,) * (nd - 1), len(blk), by_cols)))
    return pl.pallas_call(
        body, name="place_own", out_shape=[SDS(g.shape, g.dtype) for g in gathered],
        input_output_aliases={1 + nt + i: i for i in range(nt)}, compiler_params=_cp("arbitrary"),
        grid_spec=pltpu.PrefetchScalarGridSpec(num_scalar_prefetch=1, grid=(1,), in_specs=in_specs + [ANY] * (nt + 1), out_specs=out_specs),
    )(where, *shards, *gathered, after)


def _handshake(peers):
    barrier = pltpu.get_barrier_semaphore()
    for p in peers:
        pl.semaphore_signal(barrier, inc=1, device_id=p, device_id_type=MESH)
    pl.semaphore_wait(barrier, len(peers))


def _seq_exchange_on_chip(grads):
    nt, nchip = len(BIG), N_DEV // 2
    g_refs = [jax.new_ref(g, memory_space=pltpu.MemorySpace.HBM) for g in grads]
    outs = [jax.empty_ref(SDS((nchip,) + g.shape[2:], bf16), memory_space=pltpu.MemorySpace.HBM) for g in grads]

    @pl.kernel(mesh=plsc.ScalarSubcoreMesh(axis_name="seq", num_cores=1), name="seq_rs_on_chip",
               scratch_types=(pltpu.SemaphoreType.DMA((nchip,)), pltpu.SemaphoreType.DMA((nchip,))),
               compiler_params=pltpu.CompilerParams(collective_id=2))
    def launch(send_sems, recv_sems):
        x, y, c = _place()
        sibling = (x, y, 1 - c)
        _handshake([sibling])
        remote = [pltpu.make_async_remote_copy(src_ref=g_refs[t].at[j, 1 - c], dst_ref=outs[t].at[j], send_sem=send_sems.at[j],
                                               recv_sem=recv_sems.at[j], device_id=sibling, device_id_type=MESH)
                  for j in range(nchip) for t in range(nt)]
        for cp in remote:
            cp.start()
        for cp in remote:
            cp.wait_recv()
        for cp in remote:
            cp.wait_send()

    launch()
    return [o[...] for o in outs], [g[...] for g in g_refs]


def _seq_exchange_between_chips(sums):
    nt = len(BIG)
    s_refs = [jax.new_ref(a, memory_space=pltpu.MemorySpace.HBM) for a in sums]
    outs = [jax.empty_ref(SDS((3,) + a.shape[1:], bf16), memory_space=pltpu.MemorySpace.HBM) for a in sums]
    transit = [jax.empty_ref(SDS(a.shape[1:], bf16), memory_space=pltpu.MemorySpace.HBM) for a in sums]

    early, late = (0,), tuple(range(1, nt))

    @pl.kernel(mesh=plsc.ScalarSubcoreMesh(axis_name="seq", num_cores=1), name="seq_rs_between_chips",
               scratch_types=(pltpu.SemaphoreType.DMA((6,)), pltpu.SemaphoreType.DMA((6,))),
               compiler_params=pltpu.CompilerParams(collective_id=3))
    def launch(send_sems, recv_sems):
        x, y, c = _place()
        first, second, diag = _ici_route(x, y, c)
        _handshake([first, second])

        def copies(k, src, dst, to, part=range(nt)):
            return [pltpu.make_async_remote_copy(src_ref=src(t), dst_ref=dst(t), send_sem=send_sems.at[k], recv_sem=recv_sems.at[k],
                                                 device_id=to, device_id_type=MESH) for t in part]

        chip_of = lambda p: 2 * p[0] + p[1]
        for_diag = lambda t: s_refs[t].at[chip_of(diag)]
        through = lambda t: transit[t]
        last = lambda t: outs[t].at[2]
        direct = (copies(0, lambda t: s_refs[t].at[chip_of(first)], lambda t: outs[t].at[0], first)
                  + copies(1, lambda t: s_refs[t].at[chip_of(second)], lambda t: outs[t].at[1], second))
        via = [copies(2, for_diag, through, first, early), copies(3, for_diag, through, first, late)]
        passed = [copies(4, through, last, second, early), copies(5, through, last, second, late)]
        for cp in via[0] + direct + via[1]:
            cp.start()
        for arrived, onward in zip(via, passed):
            for cp in arrived:
                cp.wait_recv()
            for cp in onward:
                cp.start()
        sent = direct + via[0] + via[1] + passed[0] + passed[1]
        for cp in direct + passed[0] + passed[1]:
            cp.wait_recv()
        for cp in sent:
            cp.wait_send()

    launch()
    return [o[...] for o in outs], [a[...] for a in s_refs]


def _chip_sums(core, mine, other, after, steps=2):
    nt, nchip = len(mine), mine[0].shape[0]
    m4 = [a.reshape(nchip, 2, -1, a.shape[-1]) for a in mine]
    o3 = [a.reshape(nchip, -1, a.shape[-1]) for a in other]

    def body(c_ref, *refs):
        del c_ref
        for a_ref, b_ref, o_ref in zip(refs[:nt], refs[nt:2 * nt], refs[2 * nt + 1:]):
            o_ref[...] = (a_ref[...].astype(f32) + b_ref[...].astype(f32)).astype(bf16)

    tiles = [(a.shape[1] // steps, a.shape[2]) for a in o3]
    blks = [pl.BlockSpec((None,) + t, lambda j, i, c_ref: (j, i, 0)) for t in tiles]
    outs = pl.pallas_call(
        body, name="chip_sums", out_shape=[SDS(a.shape, bf16) for a in o3], compiler_params=_cp("parallel", "parallel"),
        grid_spec=pltpu.PrefetchScalarGridSpec(
            num_scalar_prefetch=1, grid=(nchip, steps),
            in_specs=[pl.BlockSpec((None, None) + t, lambda j, i, c_ref: (j, c_ref[0], i, 0)) for t in tiles] + blks + [ANY],
            out_specs=blks),
    )(core, *m4, *o3, after)
    return [o.reshape(a.shape) for o, a in zip(outs, other)]


def _all_reduce_rows(pack):
    rows = pack.shape[0]
    blk = rows // N_DEV

    def body(in_ref, out_ref, land, send1, recv1, send2, recv2):
        x, y, c = _place()
        me = 4 * x + 2 * y + c
        others = [(px, py, pc) for px in range(2) for py in range(2) for pc in range(2)]

        def is_me(p):
            return jnp.logical_and(jnp.logical_and(p[0] == x, p[1] == y), p[2] == c)

        land[me] = in_ref[pl.ds(_al(me, blk), blk), :]
        for d, p in enumerate(others):
            @pl.when(jnp.logical_not(is_me(p)))
            def _():
                pltpu.make_async_remote_copy(src_ref=in_ref.at[pl.ds(d * blk, blk), :], dst_ref=land.at[me], send_sem=send1.at[d],
                                             recv_sem=recv1.at[me], device_id=p, device_id_type=MESH).start()
        for d, p in enumerate(others):
            @pl.when(jnp.logical_not(is_me(p)))
            def _():
                cp = pltpu.make_async_remote_copy(src_ref=in_ref.at[pl.ds(d * blk, blk), :], dst_ref=land.at[d], send_sem=send1.at[d],
                                                  recv_sem=recv1.at[d], device_id=p, device_id_type=MESH)
                cp.wait_recv()
                cp.wait_send()
        total = land[0]
        for d in range(1, N_DEV):
            total = total + land[d]
        out_ref[pl.ds(_al(me, blk), blk), :] = total
        for d, p in enumerate(others):
            @pl.when(jnp.logical_not(is_me(p)))
            def _():
                mine = out_ref.at[pl.ds(_al(me, blk), blk), :]
                pltpu.make_async_remote_copy(src_ref=mine, dst_ref=mine, send_sem=send2.at[d], recv_sem=recv2.at[me],
                                             device_id=p, device_id_type=MESH).start()
        for d, p in enumerate(others):
            @pl.when(jnp.logical_not(is_me(p)))
            def _():
                theirs = out_ref.at[pl.ds(d * blk, blk), :]
                cp = pltpu.make_async_remote_copy(src_ref=theirs, dst_ref=theirs, send_sem=send2.at[d], recv_sem=recv2.at[d],
                                                  device_id=p, device_id_type=MESH)
                cp.wait_recv()
                cp.wait_send()

    vm = pl.BlockSpec(memory_space=pltpu.VMEM)
    return pl.pallas_call(
        body, name="all_reduce_rows", in_specs=[vm], out_specs=vm, out_shape=SDS((rows, LANE), f32),
        scratch_shapes=[pltpu.VMEM((N_DEV, blk, LANE), f32)] + [pltpu.SemaphoreType.DMA((N_DEV,))] * 4,
        compiler_params=pltpu.CompilerParams(vmem_limit_bytes=VMEM_LIMIT),
    )(pack)


def _lower_bounds_fwd(lower):
    def body(l_ref, o_ref):
        sm = _layer_softmax(l_ref)
        run = jnp.zeros_like(sm[0])
        for l in range(DEPTH):
            o_ref[l:l + 1, :] = run
            if l + 1 < DEPTH:
                run = run + sm[l + 1]

    return pl.pallas_call(body, name="lower_bounds_fwd", out_shape=SDS(lower.shape, f32))(lower)


def _layer_softmax(l_ref):
    rows = [l_ref[l:l + 1, :] for l in range(DEPTH)]
    top = functools.reduce(jnp.maximum, rows)
    e = [jnp.exp(r - top) for r in rows]
    tot = functools.reduce(lambda a, b: a + b, e)
    return [v / tot for v in e]


def _lower_bounds_bwd(lower, dlbs):
    def body(l_ref, d_ref, o_ref):
        sm = _layer_softmax(l_ref)
        dsm = [None] * DEPTH
        run = jnp.zeros_like(sm[0])
        dsm[0] = run
        for l in reversed(range(1, DEPTH)):
            run = run + d_ref[l:l + 1, :]
            dsm[l] = run
        inner = functools.reduce(lambda a, b: a + b, [sm[l] * dsm[l] for l in range(DEPTH)])
        for l in range(DEPTH):
            o_ref[l:l + 1, :] = sm[l] * (dsm[l] - inner)

    return pl.pallas_call(body, name="lower_bounds_bwd", out_shape=SDS(lower.shape, f32))(lower, dlbs)


_ADAM_C1 = 1.0 - ADAM_B1 ** ADAM_STEP
_ADAM_C2 = 1.0 - ADAM_B2 ** ADAM_STEP


def _adamw(w, g, m, v):
    m = ADAM_B1 * m + (1.0 - ADAM_B1) * g
    v = ADAM_B2 * v + (1.0 - ADAM_B2) * (g * g)
    delta = -ADAM_LR * ((m / _ADAM_C1) / (jnp.sqrt(v / _ADAM_C2) + ADAM_EPS) + ADAM_WD * w)
    return delta, m, v


def _adam_big(where, names, w, m, v, sums, landed, outs, after, steps=4):
    nt = len(names)
    three = lambda a: a.reshape(a.shape[0], -1, a.shape[-1])
    w3, m3, v3 = ([three(d[n]) for n in names] for d in (w, m, v))
    outs3 = [three(a) for n in names for a in outs[n]]
    sums3 = [three(a) for a in sums]
    land3 = [three(a) for a in landed]

    def body(where_ref, *refs):
        del where_ref
        o_refs = refs[5 * nt + 4 * nt + 1:]
        for t in range(nt):
            w_ref, m_ref, v_ref, sum_ref, land_ref = (refs[q * nt + t] for q in range(5))
            g = sum_ref[...].astype(f32)
            for k in range(3):
                g = g + land_ref[k].astype(f32)
            delta, nm, nv = _adamw(w_ref[...], g, m_ref[...], v_ref[...])
            for o_ref, val in zip(o_refs[4 * t:4 * t + 4], (g, delta, nm, nv)):
                o_ref[...] = val

    tiles = [(a.shape[1] // steps, a.shape[2]) for a in w3]
    own = [pl.BlockSpec((None,) + t, lambda i, wh: (wh[0], i, 0)) for t in tiles]
    res = pl.pallas_call(
        body, name="adam_big", out_shape=[SDS(a.shape, f32) for a in outs3],
        input_output_aliases={1 + 5 * nt + i: i for i in range(4 * nt)}, compiler_params=_cp("parallel"),
        grid_spec=pltpu.PrefetchScalarGridSpec(
            num_scalar_prefetch=1, grid=(steps,),
            in_specs=own * 3 + [pl.BlockSpec((None,) + t, lambda i, wh: (wh[1], i, 0)) for t in tiles]
            + [pl.BlockSpec((3,) + t, lambda i, wh: (0, i, 0)) for t in tiles] + [ANY] * (4 * nt + 1),
            out_specs=[s for s in own for _ in range(4)]),
    )(where, *w3, *m3, *v3, *sums3, *land3, *outs3, after)
    return {n: [o.reshape(w[n].shape) for o in res[4 * t:4 * t + 4]] for t, n in enumerate(names)}


def _touch(a, after):
    a2 = a.reshape(-1, a.shape[-1])

    def body(a_ref, after_ref, o_ref):
        del after_ref
        o_ref[...] = a_ref[0:8, :].astype(f32)

    return pl.pallas_call(
        body, name="touch", grid=(1,), in_specs=[pl.BlockSpec((16, LANE), lambda i: (0, 0)), ANY],
        out_specs=pl.BlockSpec((8, LANE), lambda i: (0, 0)), out_shape=SDS((8, LANE), f32),
    )(a2, after)


def _adam_rows(w, g, m, v):
    def body(w_ref, g_ref, m_ref, v_ref, d_ref, nm_ref, nv_ref):
        delta, nm, nv = _adamw(w_ref[...], g_ref[...], m_ref[...], v_ref[...])
        d_ref[...] = delta
        nm_ref[...] = nm
        nv_ref[...] = nv

    return pl.pallas_call(body, name="adam_rows", out_shape=[SDS(w.shape, f32)] * 3)(w, g, m, v)


SMALL = ("g_mix", "lower_bounds", "g_hgrn_out", "w_conv", "sg_ln_g", "sg_ln_b", "w_sg", "b_sg", "g_ffn", "g_final")
WEIGHTS = ("w_in", "g_mix", "lower_bounds", "g_hgrn_out", "w_conv", "sg_ln_g", "sg_ln_b", "w_sg", "b_sg", "w_branch", "w_o", "g_ffn",
           "w_ff1", "w_ff2", "g_final")


def _pack_rows(arrays, multiple):
    flat = jnp.concatenate([a.reshape(-1) for a in arrays])
    rows = -(-flat.shape[0] // (LANE * multiple)) * multiple
    return jnp.pad(flat, (0, rows * LANE - flat.shape[0])).reshape(rows, LANE)


def _unpack_rows(pack, like):
    flat = pack.reshape(-1)
    out, at = [], 0
    for a in like:
        out.append(flat[at:at + a.size].reshape(a.shape))
        at += a.size
    return out


def kernel(x, w_in, g_mix, lower_bounds, g_hgrn_out, w_conv, sg_ln_g, sg_ln_b, w_sg, b_sg, w_branch, w_o, g_ffn, w_ff1, w_ff2, g_final, loss_target, m_w_in, m_g_mix, m_lower_bounds, m_g_hgrn_out, m_w_conv, m_sg_ln_g, m_sg_ln_b, m_w_sg, m_b_sg, m_w_branch, m_w_o, m_g_ffn, m_w_ff1, m_w_ff2, m_g_final, v_w_in, v_g_mix, v_lower_bounds, v_g_hgrn_out, v_w_conv, v_sg_ln_g, v_sg_ln_b, v_w_sg, v_b_sg, v_w_branch, v_w_o, v_g_ffn, v_w_ff1, v_w_ff2, v_g_final):
    weights = dict(w_in=w_in, g_mix=g_mix, lower_bounds=lower_bounds, g_hgrn_out=g_hgrn_out, w_conv=w_conv, sg_ln_g=sg_ln_g,
                   sg_ln_b=sg_ln_b, w_sg=w_sg, b_sg=b_sg, w_branch=w_branch, w_o=w_o, g_ffn=g_ffn, w_ff1=w_ff1, w_ff2=w_ff2, g_final=g_final)
    mom1 = dict(w_in=m_w_in, g_mix=m_g_mix, lower_bounds=m_lower_bounds, g_hgrn_out=m_g_hgrn_out, w_conv=m_w_conv, sg_ln_g=m_sg_ln_g,
                sg_ln_b=m_sg_ln_b, w_sg=m_w_sg, b_sg=m_b_sg, w_branch=m_w_branch, w_o=m_w_o, g_ffn=m_g_ffn, w_ff1=m_w_ff1, w_ff2=m_w_ff2,
                g_final=m_g_final)
    mom2 = dict(w_in=v_w_in, g_mix=v_g_mix, lower_bounds=v_lower_bounds, g_hgrn_out=v_g_hgrn_out, w_conv=v_w_conv, sg_ln_g=v_sg_ln_g,
                sg_ln_b=v_sg_ln_b, w_sg=v_w_sg, b_sg=v_b_sg, w_branch=v_w_branch, w_o=v_w_o, g_ffn=v_g_ffn, w_ff1=v_w_ff1, w_ff2=v_w_ff2,
                g_final=v_g_final)
    xi, yi, ci = _place()
    dev = 4 * xi + 2 * yi + ci
    conv_cols = w_conv.shape[-1]

    for d in (weights, mom1, mom2):
        d["w_in"] = jnp.swapaxes(d["w_in"], 1, 2)
    shards = {n: weights[n].astype(bf16) for n in BIG}

    conv_place = lax.dynamic_update_slice(jnp.zeros((DEPTH, 3, BRANCH), f32), w_conv, (0, 0, dev * conv_cols))
    (w_conv_full,) = _unpack_rows(_all_reduce_rows(_pack_rows([conv_place], 8 * N_DEV)), [conv_place])
    lbs = _lower_bounds_fwd(lower_bounds)

    def small_of(l):
        return dict(g_mix=g_mix[l][None], lb=lbs[l][None], g_out=g_hgrn_out[l][None], w_conv=w_conv_full[l], ln_g=sg_ln_g[l][None],
                    ln_b=sg_ln_b[l][None], w_sg=w_sg[l], b_sg_t=b_sg[l].T, g_ffn=g_ffn[l][None])

    act = x[0]
    full, saved = [], []
    shard_refs = [jax.new_ref(shards[n], memory_space=pltpu.MemorySpace.HBM) for n in BIG]
    shapes = _gather_out_shapes(shards)
    every = tuple(range(len(BIG)))
    groups = [(0, (0,), 1, "a"), (0, (1, 2, 3, 4), 2, "b"), (1, (0,), 1, "a"), (1, (3, 4, 1, 2), 1, "b")]
    groups += [(l, every, 1, "") for l in range(2, DEPTH)]
    arrived = {}
    for l, which, n_early, tag in groups:
        got = _seq_all_gather_layer(l, which, n_early, [shard_refs[t] for t in which], [shapes[t] for t in which], tag)
        arrived.update({(l, BIG[t]): (which, got) for t in which})

    for l in range(DEPTH):
        full.append({})

        def weight(name, after, l=l):
            if name not in full[l]:
                which, got = arrived[(l, name)]
                where = jnp.stack([jnp.int32(l), dev.astype(jnp.int32)])
                full[l].update(zip([BIG[t] for t in which], _place_own(where, which, [shards[BIG[t]] for t in which], got, after)))
            return full[l][name]

        act, sv = _layer_fwd(act, weight, small_of(l))
        saved.append(sv)
    loss_row, dx, dxb, dg_final = _final(act, loss_target[0], g_final[None])

    core = ci.astype(jnp.int32)[None]
    big_out = {n: [lax.empty(weights[n].shape, f32) for _ in range(4)] for n in BIG}
    small_grads = [None] * DEPTH

    def chip_sums(stage, after):
        l, received, mine = stage
        sums = _chip_sums(core, mine, received, after)
        placed.append(sums[BIG.index("w_o")])
        landed, sums = _seq_exchange_between_chips(sums)
        return l, sums, landed

    def adam_layer(stage, after):
        l, sums, landed = stage
        where = jnp.stack([jnp.int32(l), (2 * xi + yi).astype(jnp.int32)])
        big_out.update(_adam_big(where, BIG, weights, mom1, mom2, sums, landed, big_out, after))

    above = None
    placed = []
    for l in reversed(range(DEPTH)):
        summed = []

        def between(dx1):
            if above is None:
                return dx1
            summed.append(chip_sums(above, dx1))
            return placed[-1]

        def before_end(big):
            return _touch(summed[0][2][BIG.index("w_o")], big["w_in"]) if summed else big["w_in"]

        dx, dxb, big, small_grads[l] = _layer_bwd(dx, dxb, saved[l], full[l], small_of(l), between, before_end)
        if summed:
            adam_layer(summed[0], dx)
        above = (l, *_seq_exchange_on_chip([big[n] for n in BIG]))

    stack = lambda f: jnp.stack([f(small_grads[l]) for l in range(DEPTH)])
    d_lower = _lower_bounds_bwd(lower_bounds, stack(lambda s: s["vecs"][0]))
    local_small = dict(g_mix=stack(lambda s: s["g_mix"][0]), lower_bounds=d_lower, g_hgrn_out=stack(lambda s: s["vecs"][1]),
                       w_conv=stack(lambda s: s["vecs"][4:7]), sg_ln_g=stack(lambda s: s["vecs"][2]), sg_ln_b=stack(lambda s: s["vecs"][3]),
                       w_sg=stack(lambda s: s["w_sg"]), b_sg=stack(lambda s: s["b_sg_t"].T), g_ffn=stack(lambda s: s["g_ffn"][0]),
                       g_final=dg_final[0])
    order = [local_small[n] for n in SMALL] + [loss_row]
    *reduced, loss_sum = _unpack_rows(_all_reduce_rows(_pack_rows(order, 8 * N_DEV)), order)
    loss = loss_sum[0, 0]
    grads = dict(zip(SMALL, reduced))
    grads["w_conv"] = lax.dynamic_slice(grads["w_conv"], (0, 0, dev * conv_cols), (DEPTH, 3, conv_cols))

    deltas, new_m, new_v = {}, {}, {}
    packs = [_pack_rows([d[n] for n in SMALL], 8) for d in (weights, grads, mom1, mom2)]
    like = [weights[n] for n in SMALL]
    small_out = _adam_rows(*packs)
    for out, pack in zip((deltas, new_m, new_v), small_out):
        out.update(zip(SMALL, _unpack_rows(pack, like)))
    adam_layer(chip_sums(above, dx), small_out[0])
    for n in BIG:
        grads[n], deltas[n], new_m[n], new_v[n] = (jnp.swapaxes(a, 1, 2) if n == "w_in" else a for a in big_out[n])

    return (loss, dx[None], *[grads[n] for n in WEIGHTS], *[deltas[n] for n in WEIGHTS], *[new_m[n] for n in WEIGHTS],
            *[new_v[n] for n in WEIGHTS])
```

```python
import functools

import jax
import jax.numpy as jnp
from jax import lax
from jax.experimental import pallas as pl
from jax.experimental.pallas import tpu as pltpu
from jax.experimental.pallas import tpu_sc as plsc

f32 = jnp.float32
bf16 = jnp.bfloat16
SDS = jax.ShapeDtypeStruct
MESH = pl.DeviceIdType.MESH

D_MODEL = 1024
BRANCH = 512
N_COLS = 7680
D_FF = 4096
DEPTH = 4
HEADS = 4
HEAD_DIM = 128
HGRN_CHUNK = 64
SG_CHUNK = 128
SG_GROUPS = 4
NORM_EPS = 1e-6
LN_EPS = 1e-5
LB_FLOOR = 1e-30
N_DEV = 8
SHARD_IN = N_COLS // N_DEV
LANE = 128
GATE_COL0 = 9 * BRANCH

ADAM_LR = 0.001
ADAM_B1 = 0.9
ADAM_B2 = 0.999
ADAM_EPS = 1e-08
ADAM_WD = 0.01
ADAM_STEP = 10

MIX_TILE = 256
VMEM_LIMIT = 56 * 1024 * 1024


def _cp(*sem):
    return pltpu.CompilerParams(dimension_semantics=sem or None, vmem_limit_bytes=VMEM_LIMIT)


def _dot(a, b):
    return jnp.dot(a, b, preferred_element_type=f32)


def _dot_nt(a, b):
    return lax.dot_general(a, b, (((1,), (1,)), ((), ())), preferred_element_type=f32)


def _dot_tn(a, b):
    return lax.dot_general(a, b, (((0,), (0,)), ((), ())), preferred_element_type=f32)


def _dot_exact(ones, b):
    hi = b.astype(bf16)
    rest = b - hi.astype(f32)
    mid = rest.astype(bf16)
    low = (rest - mid.astype(f32)).astype(bf16)
    ones = ones.astype(bf16)
    return _dot(ones, hi) + _dot(ones, mid) + _dot(ones, low)


def _sigmoid(x):
    return jax.nn.sigmoid(x)


_GELU_C = 0.7978845608028654
_GELU_A = 0.044715


def _gelu(x):
    return 0.5 * x * (1.0 + jnp.tanh(_GELU_C * (x + _GELU_A * x * x * x)))


def _gelu_grad(x):
    x2 = x * x
    t = jnp.tanh(_GELU_C * (x + _GELU_A * x * x2))
    return 0.5 * (1.0 + t) + 0.5 * x * (1.0 - t * t) * _GELU_C * (1.0 + 3.0 * _GELU_A * x2)


def _rms_stats(x):
    r = lax.rsqrt(jnp.mean(x * x, axis=-1, keepdims=True) + NORM_EPS)
    return r, x * r


def _rms_bwd(dh, xh, r, g):
    dg = jnp.sum(dh * xh, axis=0, keepdims=True)
    dxn = dh * g
    dx = r * (dxn - xh * jnp.mean(dxn * xh, axis=-1, keepdims=True))
    return dx, dg


def _tri(n, upper=False):
    r = lax.broadcasted_iota(jnp.int32, (n, n), 0)
    c = lax.broadcasted_iota(jnp.int32, (n, n), 1)
    return (c >= r) if upper else (c <= r)


def _acc_rows(ref, first, val):
    @pl.when(first)
    def _():
        ref[...] = val

    @pl.when(jnp.logical_not(first))
    def _():
        ref[...] += val


def _rms_mm(x, g, w_t, tm=1024, tn=1536):
    s, n = x.shape[0], w_t.shape[0]
    jm = GATE_COL0 // tn

    def body(x_ref, g_ref, w_ref, pm_ref, pg_ref, h_ref, hs):
        j = pl.program_id(1)

        @pl.when(j == 0)
        def _():
            _, xh = _rms_stats(x_ref[...])
            hv = (xh * g_ref[...]).astype(bf16)
            hs[...] = hv
            h_ref[...] = hv

        res = _dot_nt(hs[...], w_ref[...])

        @pl.when(j < jm)
        def _():
            pm_ref[...] = res

        @pl.when(j >= jm)
        def _():
            pg_ref[...] = res.astype(bf16)

    return pl.pallas_call(
        body, name="rms_mm", grid=(s // tm, n // tn),
        in_specs=[pl.BlockSpec((tm, D_MODEL), lambda i, j: (i, 0)), pl.BlockSpec((1, D_MODEL), lambda i, j: (0, 0)),
                  pl.BlockSpec((tn, D_MODEL), lambda i, j: (j, 0))],
        out_specs=[pl.BlockSpec((tm, tn), lambda i, j: (i, jnp.minimum(j, jm - 1))),
                   pl.BlockSpec((tm, tn), lambda i, j: (i, jnp.maximum(j - jm, 0))), pl.BlockSpec((tm, D_MODEL), lambda i, j: (i, 0))],
        out_shape=[SDS((s, GATE_COL0), f32), SDS((s, n - GATE_COL0), bf16), SDS((s, D_MODEL), bf16)],
        scratch_shapes=[pltpu.VMEM((tm, D_MODEL), bf16)], compiler_params=_cp("parallel", "arbitrary"),
    )(x, g, w_t)


def _hgrn_gates(fp, lb):
    logf = jnp.logaddexp(jnp.log(jnp.maximum(lb, LB_FLOOR)), jnp.log1p(-lb) + jax.nn.log_sigmoid(fp))
    snf = _sigmoid(-fp)
    return logf, snf, (1.0 - lb) * snf


def _p_specs(tile, cols, row_map):
    return [pl.BlockSpec((tile, BRANCH), functools.partial(lambda c, i: (row_map(i), c), c)) for c in cols]


def _mixer_fwd(p, lb, gout, wconv, lng, lnb, wsg, bsg_t):
    s = p.shape[0]
    tt = MIX_TILE
    nch = tt // HGRN_CHUNK

    def body(q_ref, fp_ref, iv_ref, go_ref, bg_ref, cg_ref, xc_ref, u_ref, v_ref, lb_ref, gout_ref, wconv_ref, lng_ref,
             lnb_ref, wsg_ref, bsg_ref, z_ref, opre_ref, st_ref, st_scr, zbuf):
        @pl.when(pl.program_id(0) == 0)
        def _():
            st_scr[...] = jnp.zeros_like(st_scr)
            zbuf[0:8, :] = jnp.zeros((8, BRANCH), f32)

        lbv = lb_ref[...]
        q_raw = q_ref[...]
        qs = q_raw * _sigmoid(q_raw)
        logf, _, kk = _hgrn_gates(fp_ref[...], lbv)
        iv = iv_ref[...]
        causal = _tri(HGRN_CHUNK)
        tri = causal.astype(f32)
        last_row = lax.broadcasted_iota(jnp.int32, (HGRN_CHUNK, 1), 0) == HGRN_CHUNK - 1
        for c in range(nch):
            rows = slice(HGRN_CHUNK * c, HGRN_CHUNK * (c + 1))
            b = _dot_exact(tri, logf[rows])
            bl = jnp.sum(jnp.where(last_row, b, 0.0), axis=0, keepdims=True)
            qb = (qs[rows] * jnp.exp(b)).astype(bf16)
            kb = (kk[rows] * jnp.exp(-b)).astype(bf16)
            kd = (kk[rows] * jnp.exp(bl - b)).astype(bf16)
            ebl = jnp.exp(bl)
            vc = iv[rows].astype(bf16)
            for h in range(HEADS):
                sl = slice(HEAD_DIM * h, HEAD_DIM * (h + 1))
                st = st_scr[h]
                st_ref[c, h] = st
                a = jnp.where(causal, _dot_nt(qb[:, sl], kb[:, sl]), 0.0)
                opre_ref[rows, sl] = _dot(a.astype(bf16), vc[:, sl]) + _dot_nt(qb[:, sl], st.astype(bf16))
                st_scr[h] = st * ebl[:, sl] + _dot_tn(vc[:, sl], kd[:, sl])

        o = opre_ref[...]
        go = go_ref[...]
        gout_v = gout_ref[...]
        for h in range(HEADS):
            sl = slice(HEAD_DIM * h, HEAD_DIM * (h + 1))
            _, oh = _rms_stats(o[:, sl])
            z_ref[:, sl] = (oh * gout_v[:, sl] * _sigmoid(go[:, sl])).astype(bf16)

        zc = cg_ref[...] * xc_ref[...]
        zbuf[8:8 + tt, :] = zc
        y = wconv_ref[0:1, :] * zbuf[pl.ds(6, tt), :] + wconv_ref[1:2, :] * zbuf[pl.ds(7, tt), :] + wconv_ref[2:3, :] * zc
        z_ref[:, BRANCH:2 * BRANCH] = (bg_ref[...] * y).astype(bf16)
        zbuf[0:8, :] = zbuf[tt:tt + 8, :]

        ug = _gelu(u_ref[...])
        vg = _gelu(v_ref[...])
        vcen = vg - jnp.mean(vg, axis=-1, keepdims=True)
        rstd = lax.rsqrt(jnp.mean(vcen * vcen, axis=-1, keepdims=True) + LN_EPS)
        vn = (vcen * rstd * lng_ref[...] + lnb_ref[...]).astype(bf16)
        low = _tri(SG_CHUNK)
        for g in range(SG_GROUPS):
            sl = slice(LANE * g, LANE * (g + 1))
            wm = jnp.where(low, wsg_ref[g], 0.0).astype(bf16)
            bias = bsg_ref[:, g:g + 1]
            for cc in range(tt // SG_CHUNK):
                rows = slice(SG_CHUNK * cc, SG_CHUNK * (cc + 1))
                sv = _dot(wm, vn[rows, sl]) + bias
                z_ref[rows, 2 * BRANCH + LANE * g:2 * BRANCH + LANE * (g + 1)] = (ug[rows, sl] * sv).astype(bf16)

    full = lambda shape: pl.BlockSpec(shape, lambda i: (0,) * len(shape))
    return pl.pallas_call(
        body, name="mixer_fwd", grid=(s // tt,),
        in_specs=_p_specs(tt, range(9), lambda i: i) + [full((1, BRANCH)), full((1, BRANCH)), full((3, BRANCH)), full((1, BRANCH)),
                                                        full((1, BRANCH)), full((SG_GROUPS, SG_CHUNK, SG_CHUNK)), full((SG_CHUNK, SG_GROUPS))],
        out_specs=[pl.BlockSpec((tt, 3 * BRANCH), lambda i: (i, 0)), pl.BlockSpec((tt, BRANCH), lambda i: (i, 0)),
                   pl.BlockSpec((nch, HEADS, HEAD_DIM, HEAD_DIM), lambda i: (i, 0, 0, 0))],
        out_shape=[SDS((s, 3 * BRANCH), bf16), SDS((s, BRANCH), f32), SDS((s // HGRN_CHUNK, HEADS, HEAD_DIM, HEAD_DIM), f32)],
        scratch_shapes=[pltpu.VMEM((HEADS, HEAD_DIM, HEAD_DIM), f32), pltpu.VMEM((tt + 8, BRANCH), f32)],
        compiler_params=_cp("arbitrary"),
    )(*([p] * 9), lb, gout, wconv, lng, lnb, wsg, bsg_t)


def _branch_gate(z, wb, pg, x, wo, tm=512):
    s = z.shape[0]

    def body(z_ref, wb_ref, g_ref, x_ref, wo_ref, y_ref, m_ref, x1_ref):
        acc = None
        for n in range(3):
            cols = slice(D_MODEL * n, D_MODEL * (n + 1))
            yn = _dot(z_ref[:, BRANCH * n:BRANCH * (n + 1)], wb_ref[n])
            y_ref[:, cols] = yn.astype(bf16)
            t = _sigmoid(g_ref[:, cols].astype(f32)) * yn
            acc = t if acc is None else acc + t
        merged = acc.astype(bf16)
        m_ref[...] = merged
        x1_ref[...] = x_ref[...] + _dot(merged, wo_ref[...])

    row = pl.BlockSpec((tm, D_MODEL), lambda i: (i, 0))
    wide = pl.BlockSpec((tm, 3 * D_MODEL), lambda i: (i, 0))
    return pl.pallas_call(
        body, name="branch_gate", grid=(s // tm,),
        in_specs=[pl.BlockSpec((tm, 3 * BRANCH), lambda i: (i, 0)), pl.BlockSpec((3, BRANCH, D_MODEL), lambda i: (0, 0, 0)), wide, row,
                  pl.BlockSpec((D_MODEL, D_MODEL), lambda i: (0, 0))],
        out_specs=[wide, row, row],
        out_shape=[SDS((s, 3 * D_MODEL), bf16), SDS((s, D_MODEL), bf16), SDS((s, D_MODEL), f32)], compiler_params=_cp("parallel"),
    )(z, wb, pg, x, wo)


def _ffn(x1, g, w1, w2, tm=1024, tf=2048):
    s = x1.shape[0]
    nf = D_FF // tf

    def body(x_ref, g_ref, w1_ref, w2_ref, o_ref, h_ref, ra_ref, hs, acc):
        f = pl.program_id(1)

        @pl.when(f == 0)
        def _():
            _, xh = _rms_stats(x_ref[...])
            hv = (xh * g_ref[...]).astype(bf16)
            hs[...] = hv
            h_ref[...] = hv
            acc[...] = jnp.zeros_like(acc)

        ra = jnp.maximum(_dot(hs[...], w1_ref[...]), 0.0)
        ra_ref[...] = ra.astype(bf16)
        acc[...] += _dot((ra * ra).astype(bf16), w2_ref[...])

        @pl.when(f == nf - 1)
        def _():
            o_ref[...] = x_ref[...] + acc[...]

    return pl.pallas_call(
        body, name="ffn", grid=(s // tm, nf),
        in_specs=[pl.BlockSpec((tm, D_MODEL), lambda i, f: (i, 0)), pl.BlockSpec((1, D_MODEL), lambda i, f: (0, 0)),
                  pl.BlockSpec((D_MODEL, tf), lambda i, f: (0, f)), pl.BlockSpec((tf, D_MODEL), lambda i, f: (f, 0))],
        out_specs=[pl.BlockSpec((tm, D_MODEL), lambda i, f: (i, 0)), pl.BlockSpec((tm, D_MODEL), lambda i, f: (i, 0)),
                   pl.BlockSpec((tm, tf), lambda i, f: (i, f))],
        out_shape=[SDS((s, D_MODEL), f32), SDS((s, D_MODEL), bf16), SDS((s, D_FF), bf16)],
        scratch_shapes=[pltpu.VMEM((tm, D_MODEL), bf16), pltpu.VMEM((tm, D_MODEL), f32)], compiler_params=_cp("parallel", "arbitrary"),
    )(x1, g, w1, w2)


def _final(x, target, g, tm=512):
    s = x.shape[0]

    def body(x_ref, t_ref, g_ref, loss_ref, dx_ref, dxb_ref, dg_ref):
        first = pl.program_id(0) == 0
        gv = g_ref[...]
        r, xh = _rms_stats(x_ref[...])
        e = xh * gv - t_ref[...]
        tile_loss = 0.5 * jnp.sum(jnp.mean(e * e, axis=-1, keepdims=True), axis=0, keepdims=True)
        dx, dg = _rms_bwd(e * (1.0 / D_MODEL), xh, r, gv)
        dx_ref[...] = dx
        dxb_ref[...] = dx.astype(bf16)
        _acc_rows(dg_ref, first, dg)
        _acc_rows(loss_ref, first, jnp.broadcast_to(tile_loss, (1, LANE)))

    row = pl.BlockSpec((tm, D_MODEL), lambda i: (i, 0))
    return pl.pallas_call(
        body, name="final_loss", grid=(s // tm,), in_specs=[row, row, pl.BlockSpec((1, D_MODEL), lambda i: (0, 0))],
        out_specs=[pl.BlockSpec((1, LANE), lambda i: (0, 0)), row, row, pl.BlockSpec((1, D_MODEL), lambda i: (0, 0))],
        out_shape=[SDS((1, LANE), f32), SDS((s, D_MODEL), f32), SDS((s, D_MODEL), bf16), SDS((1, D_MODEL), f32)],
        compiler_params=_cp("arbitrary"),
    )(x, target, g)


def _ffn_bwd(dx2, dx2b, x1, g, ra, w1, w2, tm=512, tf=2048):
    s = x1.shape[0]
    nf = D_FF // tf

    def body(dx_ref, dxb_ref, x_ref, g_ref, ra_ref, w1_ref, w2_ref, da_ref, dx1_ref, dx1b_ref, dg_ref, acc):
        i, f = pl.program_id(0), pl.program_id(1)

        @pl.when(f == 0)
        def _():
            acc[...] = jnp.zeros_like(acc)

        da = (_dot_nt(dxb_ref[...], w2_ref[...]) * (2.0 * ra_ref[...].astype(f32))).astype(bf16)
        da_ref[...] = da
        acc[...] += _dot_nt(da, w1_ref[...])

        @pl.when(f == nf - 1)
        def _():
            r, xh = _rms_stats(x_ref[...])
            dx, dg = _rms_bwd(acc[...], xh, r, g_ref[...])
            dx = dx + dx_ref[...]
            dx1_ref[...] = dx
            dx1b_ref[...] = dx.astype(bf16)
            _acc_rows(dg_ref, i == 0, dg)

    row = pl.BlockSpec((tm, D_MODEL), lambda i, f: (i, 0))
    col = pl.BlockSpec((tm, tf), lambda i, f: (i, f))
    return pl.pallas_call(
        body, name="ffn_bwd", grid=(s // tm, nf),
        in_specs=[row, row, row, pl.BlockSpec((1, D_MODEL), lambda i, f: (0, 0)), col,
                  pl.BlockSpec((D_MODEL, tf), lambda i, f: (0, f)), pl.BlockSpec((tf, D_MODEL), lambda i, f: (f, 0))],
        out_specs=[col, row, row, pl.BlockSpec((1, D_MODEL), lambda i, f: (0, 0))],
        out_shape=[SDS((s, D_FF), bf16), SDS((s, D_MODEL), f32), SDS((s, D_MODEL), bf16), SDS((1, D_MODEL), f32)],
        scratch_shapes=[pltpu.VMEM((tm, D_MODEL), f32)], compiler_params=_cp("arbitrary", "arbitrary"),
    )(dx2, dx2b, x1, g, ra, w1, w2)


def _mm_tn(a, b, nb, m, n, tm, tn, name="mm_tn", rows=None, row0=0, into=None, square_a=False):
    s = a.shape[0]
    mi, nj = m // tm, n // tn
    rows = m if rows is None else rows
    blk0 = row0 // tm

    def body(a_ref, b_ref, *rest):
        av = a_ref[...]
        if square_a:
            av = av.astype(f32)
            av = (av * av).astype(bf16)
        rest[-1][...] = _dot_tn(av, b_ref[...]).astype(bf16)

    extra = {} if into is None else dict(input_output_aliases={2: 0})
    return pl.pallas_call(
        body, name=name, grid=(nb, mi, nj),
        in_specs=[pl.BlockSpec((s, tm), lambda k, i, j: (0, k * mi + i)), pl.BlockSpec((s, tn), lambda k, i, j: (0, k * nj + j))]
        + ([] if into is None else [pl.BlockSpec(memory_space=pl.ANY)]),
        out_specs=pl.BlockSpec((None, tm, tn), lambda k, i, j: (k, blk0 + i, j)), out_shape=SDS((nb, rows, n), bf16),
        compiler_params=_cp("parallel", "parallel", "parallel"), **extra,
    )(a, b, *([] if into is None else [into]))


def _mm_tn_slabs(a, b, nb, m, nblk, rel, width, tm=512, name="mm_tn_slabs"):
    s = a.shape[0]
    n = b.shape[1] // nb
    ng, mi, nw = n // nblk, m // tm, len(rel)

    def body(a_ref, b_ref, o_ref):
        full = _dot_tn(a_ref[...], b_ref[...])
        for r, start in enumerate(rel):
            o_ref[r] = full[:, start:start + width].astype(bf16)

    return pl.pallas_call(
        body, name=name, grid=(nb, ng, mi),
        in_specs=[pl.BlockSpec((s, tm), lambda k, g, i: (0, k * mi + i)), pl.BlockSpec((s, nblk), lambda k, g, i: (0, k * ng + g))],
        out_specs=pl.BlockSpec((nw, None, tm, width), lambda k, g, i: (g, k, i, 0)), out_shape=SDS((ng * nw, nb, m, width), bf16),
        compiler_params=_cp("parallel", "parallel", "parallel"),
    )(a, b)


def _merge_bwd(dx1b, wo, y, pg, wb, after, tm=512):
    s = dx1b.shape[0]

    def body(dx_ref, wo_ref, y_ref, g_ref, wb_ref, after_ref, dy_ref, dg_ref, dz_ref):
        del after_ref
        dm = _dot_nt(dx_ref[...], wo_ref[...])
        for n in range(3):
            cols = slice(D_MODEL * n, D_MODEL * (n + 1))
            gate = _sigmoid(g_ref[:, cols].astype(f32))
            t = dm * gate
            dy = t.astype(bf16)
            dy_ref[:, cols] = dy
            dg_ref[:, cols] = (t * y_ref[:, cols].astype(f32) * (1.0 - gate)).astype(bf16)
            dz_ref[:, BRANCH * n:BRANCH * (n + 1)] = _dot_nt(dy, wb_ref[n]).astype(bf16)

    wide = pl.BlockSpec((tm, 3 * D_MODEL), lambda i: (i, 0))
    return pl.pallas_call(
        body, name="merge_bwd", grid=(s // tm,),
        in_specs=[pl.BlockSpec((tm, D_MODEL), lambda i: (i, 0)), pl.BlockSpec((D_MODEL, D_MODEL), lambda i: (0, 0)), wide, wide,
                  pl.BlockSpec((3, BRANCH, D_MODEL), lambda i: (0, 0, 0)), pl.BlockSpec(memory_space=pl.ANY)],
        out_specs=[wide, wide, pl.BlockSpec((tm, 3 * BRANCH), lambda i: (i, 0))],
        out_shape=[SDS((s, 3 * D_MODEL), bf16), SDS((s, 3 * D_MODEL), bf16), SDS((s, 3 * BRANCH), bf16)],
        compiler_params=_cp("parallel"),
    )(dx1b, wo, y, pg, wb, after)


def _mixer_bwd(p, dz, opre, states, lb, gout, wconv, lng, lnb, wsg, bsg_t):
    s = p.shape[0]
    tt = MIX_TILE
    nt = s // tt
    nch = tt // HGRN_CHUNK
    rev = lambda i: nt - 1 - i

    def body(q_ref, fp_ref, iv_ref, go_ref, bg_ref, cg_ref, xc_ref, u_ref, v_ref, cgp_ref, xcp_ref, dz_ref, opre_ref, st_ref,
             lb_ref, gout_ref, wconv_ref, lng_ref, lnb_ref, wsg_ref, bsg_ref,
             dp_ref, vec_ref, dwsg_ref, dbsg_ref, dst_scr, zbuf, dybuf, dvn_scr, dbsg_acc):
        i = pl.program_id(0)

        @pl.when(i == 0)
        def _():
            dst_scr[...] = jnp.zeros_like(dst_scr)
            dybuf[tt:tt + 8, :] = jnp.zeros((8, BRANCH), f32)
            vec_ref[...] = jnp.zeros_like(vec_ref)
            dwsg_ref[...] = jnp.zeros_like(dwsg_ref)
            dbsg_acc[...] = jnp.zeros_like(dbsg_acc)

        lbv = lb_ref[...]
        q_raw, fp = q_ref[...], fp_ref[...]
        sq = _sigmoid(q_raw)
        qs = q_raw * sq
        sfp = _sigmoid(fp)
        logf, snf, kk = _hgrn_gates(fp, lbv)
        inv_f = jnp.exp(-logf)
        iv = iv_ref[...]
        doa = dz_ref[:, 0:BRANCH].astype(f32)
        o = opre_ref[...]
        sgo = _sigmoid(go_ref[...])
        gout_v = gout_ref[...]
        d_o, dgo, dgout = [], [], []
        for h in range(HEADS):
            sl = slice(HEAD_DIM * h, HEAD_DIM * (h + 1))
            r, oh = _rms_stats(o[:, sl])
            d_on = doa[:, sl] * sgo[:, sl]
            dgo.append(doa[:, sl] * oh * gout_v[:, sl] * sgo[:, sl] * (1.0 - sgo[:, sl]))
            dx, dg = _rms_bwd(d_on, oh, r, gout_v[:, sl])
            d_o.append(dx)
            dgout.append(dg)
        d_o = jnp.concatenate(d_o, axis=1)
        dp_ref[:, 3 * BRANCH:4 * BRANCH] = jnp.concatenate(dgo, axis=1).astype(bf16)
        vec_ref[1:2, :] += jnp.concatenate(dgout, axis=1)

        causal = _tri(HGRN_CHUNK)
        tri = causal.astype(f32)
        tri_up = _tri(HGRN_CHUNK, upper=True).astype(f32)
        last_row = lax.broadcasted_iota(jnp.int32, (HGRN_CHUNK, 1), 0) == HGRN_CHUNK - 1
        lb_live = (lbv > LB_FLOOR).astype(f32)
        dlb = jnp.zeros((1, BRANCH), f32)
        for c in reversed(range(nch)):
            rows = slice(HGRN_CHUNK * c, HGRN_CHUNK * (c + 1))
            b = _dot_exact(tri, logf[rows])
            bl = jnp.sum(jnp.where(last_row, b, 0.0), axis=0, keepdims=True)
            eb, enb, edl, ebl = jnp.exp(b), jnp.exp(-b), jnp.exp(bl - b), jnp.exp(bl)
            qbf, kbf, kdf = qs[rows] * eb, kk[rows] * enb, kk[rows] * edl
            qb, kb, kd = qbf.astype(bf16), kbf.astype(bf16), kdf.astype(bf16)
            vc = iv[rows].astype(bf16)
            dob = d_o[rows].astype(bf16)
            dv, dqb, dkb, dkd, debl = [], [], [], [], []
            for h in range(HEADS):
                sl = slice(HEAD_DIM * h, HEAD_DIM * (h + 1))
                st = st_ref[c, h]
                dst = dst_scr[h]
                stb, dstb = st.astype(bf16), dst.astype(bf16)
                a = jnp.where(causal, _dot_nt(qb[:, sl], kb[:, sl]), 0.0).astype(bf16)
                da = jnp.where(causal, _dot_nt(dob[:, sl], vc[:, sl]), 0.0).astype(bf16)
                dv.append(_dot_tn(a, dob[:, sl]) + _dot_nt(kd[:, sl], dstb))
                dqb.append(_dot(dob[:, sl], stb) + _dot(da, kb[:, sl]))
                dkb.append(_dot_tn(da, qb[:, sl]))
                dkd.append(_dot(vc[:, sl], dstb))
                debl.append(jnp.sum(st * dst, axis=0, keepdims=True))
                dst_scr[h] = _dot_tn(dob[:, sl], qb[:, sl]) + dst * ebl[:, sl]
            dv, dqb, dkb, dkd = (jnp.concatenate(t, axis=1) for t in (dv, dqb, dkb, dkd))
            debl = jnp.concatenate(debl, axis=1)
            t_kd = dkd * kdf
            dbl = ebl * debl + jnp.sum(t_kd, axis=0, keepdims=True)
            db = dqb * qbf - dkb * kbf - t_kd + jnp.where(last_row, dbl, 0.0)
            dkk = dkb * enb + dkd * edl
            dlc = _dot_exact(tri_up, db)
            sq_c, q_c, sfp_c, snf_c, invf_c = sq[rows], q_raw[rows], sfp[rows], snf[rows], inv_f[rows]
            slope = (1.0 - lbv) * sfp_c * snf_c
            dp_ref[rows, 0:BRANCH] = (dqb * eb * sq_c * (1.0 + q_c * (1.0 - sq_c))).astype(bf16)
            dp_ref[rows, BRANCH:2 * BRANCH] = (slope * (dlc * invf_c - dkk)).astype(bf16)
            dp_ref[rows, 2 * BRANCH:3 * BRANCH] = dv.astype(bf16)
            dlb = dlb + jnp.sum(dlc * (lb_live - sfp_c) * invf_c - dkk * snf_c, axis=0, keepdims=True)
        vec_ref[0:1, :] += dlb

        dob_ = dz_ref[:, BRANCH:2 * BRANCH].astype(f32)
        bg, cg, xc = bg_ref[...], cg_ref[...], xc_ref[...]
        zc = cg * xc
        zbuf[0:8, :] = jnp.where(i < nt - 1, cgp_ref[...] * xcp_ref[...], 0.0)
        zbuf[8:8 + tt, :] = zc
        w0, w1, w2 = wconv_ref[0:1, :], wconv_ref[1:2, :], wconv_ref[2:3, :]
        y = w0 * zbuf[pl.ds(6, tt), :] + w1 * zbuf[pl.ds(7, tt), :] + w2 * zc
        dy = dob_ * bg
        dybuf[0:tt, :] = dy
        dy1, dy2 = dybuf[pl.ds(1, tt), :], dybuf[pl.ds(2, tt), :]
        dzc = w2 * dy + w1 * dy1 + w0 * dy2
        dp_ref[:, 4 * BRANCH:5 * BRANCH] = (dob_ * y).astype(bf16)
        dp_ref[:, 5 * BRANCH:6 * BRANCH] = (dzc * xc).astype(bf16)
        dp_ref[:, 6 * BRANCH:7 * BRANCH] = (dzc * cg).astype(bf16)
        vec_ref[4:5, :] += jnp.sum(zc * dy2, axis=0, keepdims=True)
        vec_ref[5:6, :] += jnp.sum(zc * dy1, axis=0, keepdims=True)
        vec_ref[6:7, :] += jnp.sum(zc * dy, axis=0, keepdims=True)
        dybuf[tt:tt + 8, :] = dybuf[0:8, :]

        doc = dz_ref[:, 2 * BRANCH:3 * BRANCH].astype(f32)
        u_raw, v_raw = u_ref[...], v_ref[...]
        ug = _gelu(u_raw)
        dug_scale = _gelu_grad(u_raw)
        vg = _gelu(v_raw)
        vcen = vg - jnp.mean(vg, axis=-1, keepdims=True)
        rstd = lax.rsqrt(jnp.mean(vcen * vcen, axis=-1, keepdims=True) + LN_EPS)
        vhat = vcen * rstd
        lng_v = lng_ref[...]
        vn = (vhat * lng_v + lnb_ref[...]).astype(bf16)
        low = _tri(SG_CHUNK)
        for g in range(SG_GROUPS):
            sl = slice(LANE * g, LANE * (g + 1))
            wm = jnp.where(low, wsg_ref[g], 0.0).astype(bf16)
            bias = bsg_ref[:, g:g + 1]
            dw = jnp.zeros((SG_CHUNK, SG_CHUNK), f32)
            dbs = jnp.zeros((SG_CHUNK, LANE), f32)
            for cc in range(tt // SG_CHUNK):
                rows = slice(SG_CHUNK * cc, SG_CHUNK * (cc + 1))
                vn_c = vn[rows, sl]
                sv = _dot(wm, vn_c) + bias
                doc_c = doc[rows, sl]
                dp_ref[rows, 7 * BRANCH + LANE * g:7 * BRANCH + LANE * (g + 1)] = (doc_c * sv * dug_scale[rows, sl]).astype(bf16)
                dsv = doc_c * ug[rows, sl]
                dsvb = dsv.astype(bf16)
                dbs = dbs + dsv
                dw = dw + _dot_nt(dsvb, vn_c)
                dvn_scr[rows, sl] = _dot_tn(wm, dsvb)
            dwsg_ref[g] += jnp.where(low, dw, 0.0)
            dbsg_acc[:, sl] += dbs
        dvn = dvn_scr[...]
        vec_ref[2:3, :] += jnp.sum(dvn * vhat, axis=0, keepdims=True)
        vec_ref[3:4, :] += jnp.sum(dvn, axis=0, keepdims=True)
        dvh = dvn * lng_v
        dvg = rstd * (dvh - jnp.mean(dvh, axis=-1, keepdims=True) - vhat * jnp.mean(dvh * vhat, axis=-1, keepdims=True))
        dp_ref[:, 8 * BRANCH:9 * BRANCH] = (dvg * _gelu_grad(v_raw)).astype(bf16)

        @pl.when(i == nt - 1)
        def _():
            for g in range(SG_GROUPS):
                dbsg_ref[:, g:g + 1] = jnp.sum(dbsg_acc[:, LANE * g:LANE * (g + 1)], axis=1, keepdims=True)

    full = lambda shape: pl.BlockSpec(shape, lambda i: (0,) * len(shape))
    tail = lambda c: pl.BlockSpec((8, BRANCH), lambda i: (jnp.maximum(rev(i) * (tt // 8) - 1, 0), c))
    return pl.pallas_call(
        body, name="mixer_bwd", grid=(nt,),
        in_specs=_p_specs(tt, range(9), rev) + [tail(5), tail(6), pl.BlockSpec((tt, 3 * BRANCH), lambda i: (rev(i), 0)),
                                                pl.BlockSpec((tt, BRANCH), lambda i: (rev(i), 0)),
                                                pl.BlockSpec((nch, HEADS, HEAD_DIM, HEAD_DIM), lambda i: (rev(i), 0, 0, 0)),
                                                full((1, BRANCH)), full((1, BRANCH)), full((3, BRANCH)), full((1, BRANCH)), full((1, BRANCH)),
                                                full((SG_GROUPS, SG_CHUNK, SG_CHUNK)), full((SG_CHUNK, SG_GROUPS))],
        out_specs=[pl.BlockSpec((tt, 9 * BRANCH), lambda i: (rev(i), 0)), full((8, BRANCH)), full((SG_GROUPS, SG_CHUNK, SG_CHUNK)),
                   full((SG_CHUNK, SG_GROUPS))],
        out_shape=[SDS((s, 9 * BRANCH), bf16), SDS((8, BRANCH), f32), SDS((SG_GROUPS, SG_CHUNK, SG_CHUNK), f32), SDS((SG_CHUNK, SG_GROUPS), f32)],
        scratch_shapes=[pltpu.VMEM((HEADS, HEAD_DIM, HEAD_DIM), f32), pltpu.VMEM((tt + 8, BRANCH), f32), pltpu.VMEM((tt + 8, BRANCH), f32),
                        pltpu.VMEM((tt, BRANCH), f32), pltpu.VMEM((SG_CHUNK, BRANCH), f32)],
        compiler_params=_cp("arbitrary"),
    )(*([p] * 11), dz, opre, states, lb, gout, wconv, lng, lnb, wsg, bsg_t)


def _dh_bwd(dpm, dpg, w_t, x, dx1, g, after, tm=1024, tk=1536):
    s = x.shape[0]
    km = dpm.shape[1] // tk
    nk = km + dpg.shape[1] // tk

    def body(dpm_ref, dpg_ref, w_ref, x_ref, dx1_ref, g_ref, after_ref, dx_ref, dxb_ref, dg_ref, acc):
        del after_ref
        i, k = pl.program_id(0), pl.program_id(1)

        @pl.when(k == 0)
        def _():
            acc[...] = jnp.zeros_like(acc)

        @pl.when(k < km)
        def _():
            acc[...] += _dot(dpm_ref[...], w_ref[...])

        @pl.when(k >= km)
        def _():
            acc[...] += _dot(dpg_ref[...], w_ref[...])

        @pl.when(k == nk - 1)
        def _():
            r, xh = _rms_stats(x_ref[...])
            dx, dg = _rms_bwd(acc[...], xh, r, g_ref[...])
            dx = dx + dx1_ref[...]
            dx_ref[...] = dx
            dxb_ref[...] = dx.astype(bf16)
            _acc_rows(dg_ref, i == 0, dg)

    row = pl.BlockSpec((tm, D_MODEL), lambda i, k: (i, 0))
    vec = pl.BlockSpec((1, D_MODEL), lambda i, k: (0, 0))
    return pl.pallas_call(
        body, name="dh_bwd", grid=(s // tm, nk),
        in_specs=[pl.BlockSpec((tm, tk), lambda i, k: (i, jnp.minimum(k, km - 1))),
                  pl.BlockSpec((tm, tk), lambda i, k: (i, jnp.maximum(k - km, 0))),
                  pl.BlockSpec((tk, D_MODEL), lambda i, k: (k, 0)), row, row, vec, pl.BlockSpec(memory_space=pl.ANY)],
        out_specs=[row, row, vec], out_shape=[SDS((s, D_MODEL), f32), SDS((s, D_MODEL), bf16), SDS((1, D_MODEL), f32)],
        scratch_shapes=[pltpu.VMEM((tm, D_MODEL), f32)], compiler_params=_cp("arbitrary", "arbitrary"),
    )(dpm, dpg, w_t, x, dx1, g, after)


def _layer_fwd(x, weight, sm):
    p, pg, h = _rms_mm(x, sm["g_mix"], weight("w_in", x))
    z, opre, states = _mixer_fwd(p, sm["lb"], sm["g_out"], sm["w_conv"], sm["ln_g"], sm["ln_b"], sm["w_sg"], sm["b_sg_t"])
    y, merged, x1 = _branch_gate(z, weight("w_branch", z), pg, x, weight("w_o", z))
    x2, h2, ra = _ffn(x1, sm["g_ffn"], weight("w_ff1", x1), weight("w_ff2", x1))
    saved = dict(x=x, p=p, pg=pg, h=h, z=z, opre=opre, states=states, y=y, merged=merged, x1=x1, h2=h2, ra=ra)
    return x2, saved


def _layer_bwd(dx2, dx2b, sv, w, sm, between, before_end):
    nchip = N_DEV // 2
    by_chip = lambda g: g.reshape((nchip, 2) + g.shape[1:])
    da, dx1, dx1b, dg_ffn = _ffn_bwd(dx2, dx2b, sv["x1"], sm["g_ffn"], sv["ra"], w["w_ff1"], w["w_ff2"])
    g_ff2 = by_chip(_mm_tn(sv["ra"], dx2b, 1, D_FF, D_MODEL, 512, 1024, name="dw_ff2", square_a=True)[0]
                    .reshape(N_DEV, D_FF // N_DEV, D_MODEL))
    g_ff1 = by_chip(_mm_tn_slabs(sv["h2"], da, 1, D_MODEL, D_FF // 2, [i * (D_FF // N_DEV) for i in range(nchip)], D_FF // N_DEV,
                                 name="dw_ff1")[:, 0])
    g_o = by_chip(_mm_tn(sv["merged"], dx1b, 1, D_MODEL, D_MODEL, 512, 1024, name="dw_o")[0].reshape(N_DEV, D_MODEL // N_DEV, D_MODEL))
    dy, dpg, dz = _merge_bwd(dx1b, w["w_o"], sv["y"], sv["pg"], w["w_branch"], between(dx1))
    g_branch = by_chip(_mm_tn_slabs(sv["z"], dy, 3, BRANCH, D_MODEL, [i * (D_MODEL // N_DEV) for i in range(N_DEV)], D_MODEL // N_DEV,
                                    name="dw_branch"))
    g_in = _mm_tn(dpg, sv["h"], 1, 3 * D_MODEL, D_MODEL, 768, 1024, name="dw_in_gates", rows=N_COLS, row0=GATE_COL0)
    dpm, vecs, dwsg, dbsg_t = _mixer_bwd(sv["p"], dz, sv["opre"], sv["states"], sm["lb"], sm["g_out"], sm["w_conv"],
                                         sm["ln_g"], sm["ln_b"], sm["w_sg"], sm["b_sg_t"])
    g_in = _mm_tn(dpm, sv["h"], 1, GATE_COL0, D_MODEL, 768, 1024, name="dw_in_mixers", rows=N_COLS, into=g_in)
    g_in = by_chip(g_in[0].reshape(N_DEV, SHARD_IN, D_MODEL))
    big = dict(w_in=g_in, w_branch=g_branch, w_o=g_o, w_ff1=g_ff1, w_ff2=g_ff2)
    dx, dxb, dg_mix = _dh_bwd(dpm, dpg, w["w_in"], sv["x"], dx1, sm["g_mix"], before_end(big))
    small = dict(g_mix=dg_mix, g_ffn=dg_ffn, vecs=vecs, w_sg=dwsg, b_sg_t=dbsg_t, dx1=dx1)
    return dx, dxb, big, small


BIG = ("w_in", "w_branch", "w_o", "w_ff1", "w_ff2")
ANY = pl.BlockSpec(memory_space=pl.ANY)


def _place():
    return lax.axis_index("x"), lax.axis_index("y"), lax.axis_index("c")


def _al(v, m):
    return pl.multiple_of(v * m, m)


def _shard_of(refs, dev, which=range(len(BIG))):
    out = []
    for ref, t in zip(refs, which):
        by_cols = BIG[t] in ("w_branch", "w_ff1")
        n = ref.shape[-1 if by_cols else 0] // N_DEV
        part = pl.ds(_al(dev, n), n)
        out.append(ref.at[(slice(None),) * (len(ref.shape) - 1) + (part,)] if by_cols else ref.at[part])
    return out


def _gather_out_shapes(shards):
    s_in, s_b, s_o, s_1, s_2 = (shards[n] for n in BIG)
    return [SDS((s_in.shape[1] * N_DEV, s_in.shape[2]), bf16), SDS(s_b.shape[1:3] + (s_b.shape[3] * N_DEV,), bf16),
            SDS((s_o.shape[1] * N_DEV, s_o.shape[2]), bf16), SDS((s_1.shape[1], s_1.shape[2] * N_DEV), bf16),
            SDS((s_2.shape[1] * N_DEV, s_2.shape[2]), bf16)]


def _seq_all_gather_layer(layer, which, n_early, shard_refs, out_shapes, tag=""):
    nt = len(which)
    outs = [jax.empty_ref(sh, memory_space=pltpu.MemorySpace.HBM) for sh in out_shapes]
    early, late = tuple(range(n_early)), tuple(range(n_early, nt))

    @pl.kernel(mesh=plsc.ScalarSubcoreMesh(axis_name="seq", num_cores=1), name=f"seq_all_gather_l{layer}{tag}",
               scratch_types=(pltpu.SemaphoreType.DMA((9,)), pltpu.SemaphoreType.DMA((9,))),
               compiler_params=pltpu.CompilerParams(collective_id=1))
    def launch(send_sems, recv_sems):
        x, y, c = _place()
        me, sibling = (x, y, c), (x, y, 1 - c)
        first, second, diag = _ici_route(x, y, c)
        _handshake([sibling, first, second])
        mine = [r.at[layer] for r in shard_refs]

        def copies(k, blk, to, src=None, part=range(nt)):
            dst = _shard_of(outs, 4 * blk[0] + 2 * blk[1] + blk[2], which)
            src = dst if src is None else src
            return [pltpu.make_async_remote_copy(src_ref=src[t], dst_ref=dst[t], send_sem=send_sems.at[k], recv_sem=recv_sems.at[k],
                                                 device_id=to, device_id_type=MESH) for t in part]

        def start(cps):
            for cp in cps:
                cp.start()
            return cps

        def landed(cps):
            for cp in cps:
                cp.wait_recv()

        sent = start(copies(0, me, sibling, src=mine) + copies(1, me, first, src=mine, part=early)
                     + copies(2, me, first, src=mine, part=late) + copies(3, me, second, src=mine))
        landed(copies(1, first, me, part=early))
        sent += start(copies(4, first, second, part=early) + copies(6, first, sibling, part=early))
        landed(copies(2, first, me, part=late))
        sent += start(copies(5, first, second, part=late) + copies(6, first, sibling, part=late))
        landed(copies(3, second, me))
        sent += start(copies(7, second, sibling))
        landed(copies(4, diag, me, part=early) + copies(5, diag, me, part=late))
        sent += start(copies(8, diag, sibling))
        other = lambda p: (p[0], p[1], 1 - c)
        landed(copies(0, sibling, me) + copies(6, other(second), me) + copies(7, other(first), me) + copies(8, other(diag), me))
        for cp in sent:
            cp.wait_send()

    launch()
    return [o[...] for o in outs]


def _ici_route(x, y, c):
    return (x ^ (1 - c), y ^ c, c), (x ^ c, y ^ (1 - c), c), (1 - x, 1 - y, c)


def _place_own(where, which, shards, gathered, after):
    nt = len(which)

    def body(where_ref, *refs):
        del where_ref
        for src, dst in zip(refs[:nt], refs[2 * nt + 1:]):
            dst[...] = src[...]

    in_specs, out_specs = [], []
    for t, sh in zip(which, shards):
        blk = sh.shape[1:]
        in_specs.append(pl.BlockSpec((None,) + blk, functools.partial(lambda nd, i, wh: (wh[0],) + (0,) * nd, len(blk))))
        by_cols = BIG[t] in ("w_branch", "w_ff1")
        out_specs.append(pl.BlockSpec(blk, functools.partial(
            lambda nd, cols, i, wh: (0,) * (nd - 1) + (wh[1],) if cols else (wh[1],) + (0,) * (nd - 1), len(blk), by_cols)))
    return pl.pallas_call(
        body, name="place_own", out_shape=[SDS(g.shape, g.dtype) for g in gathered],
        input_output_aliases={1 + nt + i: i for i in range(nt)}, compiler_params=_cp("arbitrary"),
        grid_spec=pltpu.PrefetchScalarGridSpec(num_scalar_prefetch=1, grid=(1,), in_specs=in_specs + [ANY] * (nt + 1), out_specs=out_specs),
    )(where, *shards, *gathered, after)


def _handshake(peers):
    barrier = pltpu.get_barrier_semaphore()
    for p in peers:
        pl.semaphore_signal(barrier, inc=1, device_id=p, device_id_type=MESH)
    pl.semaphore_wait(barrier, len(peers))


def _seq_exchange_on_chip(grads):
    nt, nchip = len(BIG), N_DEV // 2
    g_refs = [jax.new_ref(g, memory_space=pltpu.MemorySpace.HBM) for g in grads]
    outs = [jax.empty_ref(SDS((nchip,) + g.shape[2:], bf16), memory_space=pltpu.MemorySpace.HBM) for g in grads]

    @pl.kernel(mesh=plsc.ScalarSubcoreMesh(axis_name="seq", num_cores=1), name="seq_rs_on_chip",
               scratch_types=(pltpu.SemaphoreType.DMA((nchip,)), pltpu.SemaphoreType.DMA((nchip,))),
               compiler_params=pltpu.CompilerParams(collective_id=2))
    def launch(send_sems, recv_sems):
        x, y, c = _place()
        sibling = (x, y, 1 - c)
        _handshake([sibling])
        remote = [pltpu.make_async_remote_copy(src_ref=g_refs[t].at[j, 1 - c], dst_ref=outs[t].at[j], send_sem=send_sems.at[j],
                                               recv_sem=recv_sems.at[j], device_id=sibling, device_id_type=MESH)
                  for j in range(nchip) for t in range(nt)]
        for cp in remote:
            cp.start()
        for cp in remote:
            cp.wait_recv()
        for cp in remote:
            cp.wait_send()

    launch()
    return [o[...] for o in outs], [g[...] for g in g_refs]


def _seq_exchange_between_chips(sums):
    nt = len(BIG)
    s_refs = [jax.new_ref(a, memory_space=pltpu.MemorySpace.HBM) for a in sums]
    outs = [jax.empty_ref(SDS((3,) + a.shape[1:], bf16), memory_space=pltpu.MemorySpace.HBM) for a in sums]
    transit = [jax.empty_ref(SDS(a.shape[1:], bf16), memory_space=pltpu.MemorySpace.HBM) for a in sums]

    early, late = (0,), tuple(range(1, nt))

    @pl.kernel(mesh=plsc.ScalarSubcoreMesh(axis_name="seq", num_cores=1), name="seq_rs_between_chips",
               scratch_types=(pltpu.SemaphoreType.DMA((6,)), pltpu.SemaphoreType.DMA((6,))),
               compiler_params=pltpu.CompilerParams(collective_id=3))
    def launch(send_sems, recv_sems):
        x, y, c = _place()
        first, second, diag = _ici_route(x, y, c)
        _handshake([first, second])

        def copies(k, src, dst, to, part=range(nt)):
            return [pltpu.make_async_remote_copy(src_ref=src(t), dst_ref=dst(t), send_sem=send_sems.at[k], recv_sem=recv_sems.at[k],
                                                 device_id=to, device_id_type=MESH) for t in part]

        chip_of = lambda p: 2 * p[0] + p[1]
        for_diag = lambda t: s_refs[t].at[chip_of(diag)]
        through = lambda t: transit[t]
        last = lambda t: outs[t].at[2]
        direct = (copies(0, lambda t: s_refs[t].at[chip_of(first)], lambda t: outs[t].at[0], first)
                  + copies(1, lambda t: s_refs[t].at[chip_of(second)], lambda t: outs[t].at[1], second))
        via = [copies(2, for_diag, through, first, early), copies(3, for_diag, through, first, late)]
        passed = [copies(4, through, last, second, early), copies(5, through, last, second, late)]
        for cp in via[0] + direct + via[1]:
            cp.start()
        for arrived, onward in zip(via, passed):
            for cp in arrived:
                cp.wait_recv()
            for cp in onward:
                cp.start()
        sent = direct + via[0] + via[1] + passed[0] + passed[1]
        for cp in direct + passed[0] + passed[1]:
            cp.wait_recv()
        for cp in sent:
            cp.wait_send()

    launch()
    return [o[...] for o in outs], [a[...] for a in s_refs]


def _chip_sums(core, mine, other, after, steps=2):
    nt, nchip = len(mine), mine[0].shape[0]
    m4 = [a.reshape(nchip, 2, -1, a.shape[-1]) for a in mine]
    o3 = [a.reshape(nchip, -1, a.shape[-1]) for a in other]

    def body(c_ref, *refs):
        del c_ref
        for a_ref, b_ref, o_ref in zip(refs[:nt], refs[nt:2 * nt], refs[2 * nt + 1:]):
            o_ref[...] = (a_ref[...].astype(f32) + b_ref[...].astype(f32)).astype(bf16)

    tiles = [(a.shape[1] // steps, a.shape[2]) for a in o3]
    blks = [pl.BlockSpec((None,) + t, lambda j, i, c_ref: (j, i, 0)) for t in tiles]
    outs = pl.pallas_call(
        body, name="chip_sums", out_shape=[SDS(a.shape, bf16) for a in o3], compiler_params=_cp("parallel", "parallel"),
        grid_spec=pltpu.PrefetchScalarGridSpec(
            num_scalar_prefetch=1, grid=(nchip, steps),
            in_specs=[pl.BlockSpec((None, None) + t, lambda j, i, c_ref: (j, c_ref[0], i, 0)) for t in tiles] + blks + [ANY],
            out_specs=blks),
    )(core, *m4, *o3, after)
    return [o.reshape(a.shape) for o, a in zip(outs, other)]


def _all_reduce_rows(pack):
    rows = pack.shape[0]
    blk = rows // N_DEV

    def body(in_ref, out_ref, land, send1, recv1, send2, recv2):
        x, y, c = _place()
        me = 4 * x + 2 * y + c
        others = [(px, py, pc) for px in range(2) for py in range(2) for pc in range(2)]

        def is_me(p):
            return jnp.logical_and(jnp.logical_and(p[0] == x, p[1] == y), p[2] == c)

        land[me] = in_ref[pl.ds(_al(me, blk), blk), :]
        for d, p in enumerate(others):
            @pl.when(jnp.logical_not(is_me(p)))
            def _():
                pltpu.make_async_remote_copy(src_ref=in_ref.at[pl.ds(d * blk, blk), :], dst_ref=land.at[me], send_sem=send1.at[d],
                                             recv_sem=recv1.at[me], device_id=p, device_id_type=MESH).start()
        for d, p in enumerate(others):
            @pl.when(jnp.logical_not(is_me(p)))
            def _():
                cp = pltpu.make_async_remote_copy(src_ref=in_ref.at[pl.ds(d * blk, blk), :], dst_ref=land.at[d], send_sem=send1.at[d],
                                                  recv_sem=recv1.at[d], device_id=p, device_id_type=MESH)
                cp.wait_recv()
                cp.wait_send()
        total = land[0]
        for d in range(1, N_DEV):
            total = total + land[d]
        out_ref[pl.ds(_al(me, blk), blk), :] = total
        for d, p in enumerate(others):
            @pl.when(jnp.logical_not(is_me(p)))
            def _():
                mine = out_ref.at[pl.ds(_al(me, blk), blk), :]
                pltpu.make_async_remote_copy(src_ref=mine, dst_ref=mine, send_sem=send2.at[d], recv_sem=recv2.at[me],
                                             device_id=p, device_id_type=MESH).start()
        for d, p in enumerate(others):
            @pl.when(jnp.logical_not(is_me(p)))
            def _():
                theirs = out_ref.at[pl.ds(d * blk, blk), :]
                cp = pltpu.make_async_remote_copy(src_ref=theirs, dst_ref=theirs, send_sem=send2.at[d], recv_sem=recv2.at[d],
                                                  device_id=p, device_id_type=MESH)
                cp.wait_recv()
                cp.wait_send()

    vm = pl.BlockSpec(memory_space=pltpu.VMEM)
    return pl.pallas_call(
        body, name="all_reduce_rows", in_specs=[vm], out_specs=vm, out_shape=SDS((rows, LANE), f32),
        scratch_shapes=[pltpu.VMEM((N_DEV, blk, LANE), f32)] + [pltpu.SemaphoreType.DMA((N_DEV,))] * 4,
        compiler_params=pltpu.CompilerParams(vmem_limit_bytes=VMEM_LIMIT),
    )(pack)


def _lower_bounds_fwd(lower):
    def body(l_ref, o_ref):
        sm = _layer_softmax(l_ref)
        run = jnp.zeros_like(sm[0])
        for l in range(DEPTH):
            o_ref[l:l + 1, :] = run
            if l + 1 < DEPTH:
                run = run + sm[l + 1]

    return pl.pallas_call(body, name="lower_bounds_fwd", out_shape=SDS(lower.shape, f32))(lower)


def _layer_softmax(l_ref):
    rows = [l_ref[l:l + 1, :] for l in range(DEPTH)]
    top = functools.reduce(jnp.maximum, rows)
    e = [jnp.exp(r - top) for r in rows]
    tot = functools.reduce(lambda a, b: a + b, e)
    return [v / tot for v in e]


def _lower_bounds_bwd(lower, dlbs):
    def body(l_ref, d_ref, o_ref):
        sm = _layer_softmax(l_ref)
        dsm = [None] * DEPTH
        run = jnp.zeros_like(sm[0])
        dsm[0] = run
        for l in reversed(range(1, DEPTH)):
            run = run + d_ref[l:l + 1, :]
            dsm[l] = run
        inner = functools.reduce(lambda a, b: a + b, [sm[l] * dsm[l] for l in range(DEPTH)])
        for l in range(DEPTH):
            o_ref[l:l + 1, :] = sm[l] * (dsm[l] - inner)

    return pl.pallas_call(body, name="lower_bounds_bwd", out_shape=SDS(lower.shape, f32))(lower, dlbs)


_ADAM_C1 = 1.0 - ADAM_B1 ** ADAM_STEP
_ADAM_C2 = 1.0 - ADAM_B2 ** ADAM_STEP


def _adamw(w, g, m, v):
    m = ADAM_B1 * m + (1.0 - ADAM_B1) * g
    v = ADAM_B2 * v + (1.0 - ADAM_B2) * (g * g)
    delta = -ADAM_LR * ((m / _ADAM_C1) / (jnp.sqrt(v / _ADAM_C2) + ADAM_EPS) + ADAM_WD * w)
    return delta, m, v


def _adam_big(where, names, w, m, v, sums, landed, outs, after, steps=4):
    nt = len(names)
    three = lambda a: a.reshape(a.shape[0], -1, a.shape[-1])
    w3, m3, v3 = ([three(d[n]) for n in names] for d in (w, m, v))
    outs3 = [three(a) for n in names for a in outs[n]]
    sums3 = [three(a) for a in sums]
    land3 = [three(a) for a in landed]

    def body(where_ref, *refs):
        del where_ref
        o_refs = refs[5 * nt + 4 * nt + 1:]
        for t in range(nt):
            w_ref, m_ref, v_ref, sum_ref, land_ref = (refs[q * nt + t] for q in range(5))
            g = sum_ref[...].astype(f32)
            for k in range(3):
                g = g + land_ref[k].astype(f32)
            delta, nm, nv = _adamw(w_ref[...], g, m_ref[...], v_ref[...])
            for o_ref, val in zip(o_refs[4 * t:4 * t + 4], (g, delta, nm, nv)):
                o_ref[...] = val

    tiles = [(a.shape[1] // steps, a.shape[2]) for a in w3]
    own = [pl.BlockSpec((None,) + t, lambda i, wh: (wh[0], i, 0)) for t in tiles]
    res = pl.pallas_call(
        body, name="adam_big", out_shape=[SDS(a.shape, f32) for a in outs3],
        input_output_aliases={1 + 5 * nt + i: i for i in range(4 * nt)}, compiler_params=_cp("parallel"),
        grid_spec=pltpu.PrefetchScalarGridSpec(
            num_scalar_prefetch=1, grid=(steps,),
            in_specs=own * 3 + [pl.BlockSpec((None,) + t, lambda i, wh: (wh[1], i, 0)) for t in tiles]
            + [pl.BlockSpec((3,) + t, lambda i, wh: (0, i, 0)) for t in tiles] + [ANY] * (4 * nt + 1),
            out_specs=[s for s in own for _ in range(4)]),
    )(where, *w3, *m3, *v3, *sums3, *land3, *outs3, after)
    return {n: [o.reshape(w[n].shape) for o in res[4 * t:4 * t + 4]] for t, n in enumerate(names)}


def _touch(a, after):
    a2 = a.reshape(-1, a.shape[-1])

    def body(a_ref, after_ref, o_ref):
        del after_ref
        o_ref[...] = a_ref[0:8, :].astype(f32)

    return pl.pallas_call(
        body, name="touch", grid=(1,), in_specs=[pl.BlockSpec((16, LANE), lambda i: (0, 0)), ANY],
        out_specs=pl.BlockSpec((8, LANE), lambda i: (0, 0)), out_shape=SDS((8, LANE), f32),
    )(a2, after)


def _adam_rows(w, g, m, v):
    def body(w_ref, g_ref, m_ref, v_ref, d_ref, nm_ref, nv_ref):
        delta, nm, nv = _adamw(w_ref[...], g_ref[...], m_ref[...], v_ref[...])
        d_ref[...] = delta
        nm_ref[...] = nm
        nv_ref[...] = nv

    return pl.pallas_call(body, name="adam_rows", out_shape=[SDS(w.shape, f32)] * 3)(w, g, m, v)


SMALL = ("g_mix", "lower_bounds", "g_hgrn_out", "w_conv", "sg_ln_g", "sg_ln_b", "w_sg", "b_sg", "g_ffn", "g_final")
WEIGHTS = ("w_in", "g_mix", "lower_bounds", "g_hgrn_out", "w_conv", "sg_ln_g", "sg_ln_b", "w_sg", "b_sg", "w_branch", "w_o", "g_ffn",
           "w_ff1", "w_ff2", "g_final")


def _pack_rows(arrays, multiple):
    flat = jnp.concatenate([a.reshape(-1) for a in arrays])
    rows = -(-flat.shape[0] // (LANE * multiple)) * multiple
    return jnp.pad(flat, (0, rows * LANE - flat.shape[0])).reshape(rows, LANE)


def _unpack_rows(pack, like):
    flat = pack.reshape(-1)
    out, at = [], 0
    for a in like:
        out.append(flat[at:at + a.size].reshape(a.shape))
        at += a.size
    return out


def kernel(x, w_in, g_mix, lower_bounds, g_hgrn_out, w_conv, sg_ln_g, sg_ln_b, w_sg, b_sg, w_branch, w_o, g_ffn, w_ff1, w_ff2, g_final, loss_target, m_w_in, m_g_mix, m_lower_bounds, m_g_hgrn_out, m_w_conv, m_sg_ln_g, m_sg_ln_b, m_w_sg, m_b_sg, m_w_branch, m_w_o, m_g_ffn, m_w_ff1, m_w_ff2, m_g_final, v_w_in, v_g_mix, v_lower_bounds, v_g_hgrn_out, v_w_conv, v_sg_ln_g, v_sg_ln_b, v_w_sg, v_b_sg, v_w_branch, v_w_o, v_g_ffn, v_w_ff1, v_w_ff2, v_g_final):
    weights = dict(w_in=w_in, g_mix=g_mix, lower_bounds=lower_bounds, g_hgrn_out=g_hgrn_out, w_conv=w_conv, sg_ln_g=sg_ln_g,
                   sg_ln_b=sg_ln_b, w_sg=w_sg, b_sg=b_sg, w_branch=w_branch, w_o=w_o, g_ffn=g_ffn, w_ff1=w_ff1, w_ff2=w_ff2, g_final=g_final)
    mom1 = dict(w_in=m_w_in, g_mix=m_g_mix, lower_bounds=m_lower_bounds, g_hgrn_out=m_g_hgrn_out, w_conv=m_w_conv, sg_ln_g=m_sg_ln_g,
                sg_ln_b=m_sg_ln_b, w_sg=m_w_sg, b_sg=m_b_sg, w_branch=m_w_branch, w_o=m_w_o, g_ffn=m_g_ffn, w_ff1=m_w_ff1, w_ff2=m_w_ff2,
                g_final=m_g_final)
    mom2 = dict(w_in=v_w_in, g_mix=v_g_mix, lower_bounds=v_lower_bounds, g_hgrn_out=v_g_hgrn_out, w_conv=v_w_conv, sg_ln_g=v_sg_ln_g,
                sg_ln_b=v_sg_ln_b, w_sg=v_w_sg, b_sg=v_b_sg, w_branch=v_w_branch, w_o=v_w_o, g_ffn=v_g_ffn, w_ff1=v_w_ff1, w_ff2=v_w_ff2,
                g_final=v_g_final)
    xi, yi, ci = _place()
    dev = 4 * xi + 2 * yi + ci
    conv_cols = w_conv.shape[-1]

    for d in (weights, mom1, mom2):
        d["w_in"] = jnp.swapaxes(d["w_in"], 1, 2)
    shards = {n: weights[n].astype(bf16) for n in BIG}

    conv_place = lax.dynamic_update_slice(jnp.zeros((DEPTH, 3, BRANCH), f32), w_conv, (0, 0, dev * conv_cols))
    (w_conv_full,) = _unpack_rows(_all_reduce_rows(_pack_rows([conv_place], 8 * N_DEV)), [conv_place])
    lbs = _lower_bounds_fwd(lower_bounds)

    def small_of(l):
        return dict(g_mix=g_mix[l][None], lb=lbs[l][None], g_out=g_hgrn_out[l][None], w_conv=w_conv_full[l], ln_g=sg_ln_g[l][None],
                    ln_b=sg_ln_b[l][None], w_sg=w_sg[l], b_sg_t=b_sg[l].T, g_ffn=g_ffn[l][None])

    act = x[0]
    full, saved = [], []
    shard_refs = [jax.new_ref(shards[n], memory_space=pltpu.MemorySpace.HBM) for n in BIG]
    shapes = _gather_out_shapes(shards)
    every = tuple(range(len(BIG)))
    groups = [(0, (0,), 1, "a"), (0, (1, 2, 3, 4), 2, "b"), (1, (0,), 1, "a"), (1, (3, 4, 1, 2), 1, "b")]
    groups += [(l, every, 1, "") for l in range(2, DEPTH)]
    arrived = {}
    for l, which, n_early, tag in groups:
        got = _seq_all_gather_layer(l, which, n_early, [shard_refs[t] for t in which], [shapes[t] for t in which], tag)
        arrived.update({(l, BIG[t]): (which, got) for t in which})

    for l in range(DEPTH):
        full.append({})

        def weight(name, after, l=l):
            if name not in full[l]:
                which, got = arrived[(l, name)]
                where = jnp.stack([jnp.int32(l), dev.astype(jnp.int32)])
                full[l].update(zip([BIG[t] for t in which], _place_own(where, which, [shards[BIG[t]] for t in which], got, after)))
            return full[l][name]

        act, sv = _layer_fwd(act, weight, small_of(l))
        saved.append(sv)
    loss_row, dx, dxb, dg_final = _final(act, loss_target[0], g_final[None])

    core = ci.astype(jnp.int32)[None]
    big_out = {n: [lax.empty(weights[n].shape, f32) for _ in range(4)] for n in BIG}
    small_grads = [None] * DEPTH

    def chip_sums(stage, after):
        l, received, mine = stage
        sums = _chip_sums(core, mine, received, after)
        placed.append(sums[BIG.index("w_o")])
        landed, sums = _seq_exchange_between_chips(sums)
        return l, sums, landed

    def adam_layer(stage, after):
        l, sums, landed = stage
        where = jnp.stack([jnp.int32(l), (2 * xi + yi).astype(jnp.int32)])
        big_out.update(_adam_big(where, BIG, weights, mom1, mom2, sums, landed, big_out, after))

    above = None
    placed = []
    for l in reversed(range(DEPTH)):
        summed = []

        def between(dx1):
            if above is None:
                return dx1
            summed.append(chip_sums(above, dx1))
            return placed[-1]

        def before_end(big):
            return _touch(summed[0][2][BIG.index("w_o")], big["w_in"]) if summed else big["w_in"]

        dx, dxb, big, small_grads[l] = _layer_bwd(dx, dxb, saved[l], full[l], small_of(l), between, before_end)
        if summed:
            adam_layer(summed[0], dx)
        above = (l, *_seq_exchange_on_chip([big[n] for n in BIG]))

    stack = lambda f: jnp.stack([f(small_grads[l]) for l in range(DEPTH)])
    d_lower = _lower_bounds_bwd(lower_bounds, stack(lambda s: s["vecs"][0]))
    local_small = dict(g_mix=stack(lambda s: s["g_mix"][0]), lower_bounds=d_lower, g_hgrn_out=stack(lambda s: s["vecs"][1]),
                       w_conv=stack(lambda s: s["vecs"][4:7]), sg_ln_g=stack(lambda s: s["vecs"][2]), sg_ln_b=stack(lambda s: s["vecs"][3]),
                       w_sg=stack(lambda s: s["w_sg"]), b_sg=stack(lambda s: s["b_sg_t"].T), g_ffn=stack(lambda s: s["g_ffn"][0]),
                       g_final=dg_final[0])
    order = [local_small[n] for n in SMALL] + [loss_row]
    *reduced, loss_sum = _unpack_rows(_all_reduce_rows(_pack_rows(order, 8 * N_DEV)), order)
    loss = loss_sum[0, 0]
    grads = dict(zip(SMALL, reduced))
    grads["w_conv"] = lax.dynamic_slice(grads["w_conv"], (0, 0, dev * conv_cols), (DEPTH, 3, conv_cols))

    deltas, new_m, new_v = {}, {}, {}
    packs = [_pack_rows([d[n] for n in SMALL], 8) for d in (weights, grads, mom1, mom2)]
    like = [weights[n] for n in SMALL]
    small_out = _adam_rows(*packs)
    for out, pack in zip((deltas, new_m, new_v), small_out):
        out.update(zip(SMALL, _unpack_rows(pack, like)))
    adam_layer(chip_sums(above, dx), small_out[0])
    for n in BIG:
        grads[n], deltas[n], new_m[n], new_v[n] = (jnp.swapaxes(a, 1, 2) if n == "w_in" else a for a in big_out[n])

    return (loss, dx[None], *[grads[n] for n in WEIGHTS], *[deltas[n] for n in WEIGHTS], *[new_m[n] for n in WEIGHTS],
            *[new_v[n] for n in WEIGHTS])
```

```python
import functools

import jax
import jax.numpy as jnp
from jax import lax
from jax.experimental import pallas as pl
from jax.experimental.pallas import tpu as pltpu
from jax.experimental.pallas import tpu_sc as plsc

f32 = jnp.float32
bf16 = jnp.bfloat16
SDS = jax.ShapeDtypeStruct
MESH = pl.DeviceIdType.MESH

D_MODEL = 1024
BRANCH = 512
N_COLS = 7680
D_FF = 4096
DEPTH = 4
HEADS = 4
HEAD_DIM = 128
HGRN_CHUNK = 64
SG_CHUNK = 128
SG_GROUPS = 4
NORM_EPS = 1e-6
LN_EPS = 1e-5
LB_FLOOR = 1e-30
N_DEV = 8
SHARD_IN = N_COLS // N_DEV
LANE = 128
GATE_COL0 = 9 * BRANCH

ADAM_LR = 0.001
ADAM_B1 = 0.9
ADAM_B2 = 0.999
ADAM_EPS = 1e-08
ADAM_WD = 0.01
ADAM_STEP = 10

MIX_TILE = 256
VMEM_LIMIT = 56 * 1024 * 1024


def _cp(*sem):
    return pltpu.CompilerParams(dimension_semantics=sem or None, vmem_limit_bytes=VMEM_LIMIT)


def _dot(a, b):
    return jnp.dot(a, b, preferred_element_type=f32)


def _dot_nt(a, b):
    return lax.dot_general(a, b, (((1,), (1,)), ((), ())), preferred_element_type=f32)


def _dot_tn(a, b):
    return lax.dot_general(a, b, (((0,), (0,)), ((), ())), preferred_element_type=f32)


def _dot_exact(ones, b):
    hi = b.astype(bf16)
    rest = b - hi.astype(f32)
    mid = rest.astype(bf16)
    low = (rest - mid.astype(f32)).astype(bf16)
    ones = ones.astype(bf16)
    return _dot(ones, hi) + _dot(ones, mid) + _dot(ones, low)


def _sigmoid(x):
    return jax.nn.sigmoid(x)


_GELU_C = 0.7978845608028654
_GELU_A = 0.044715


def _gelu(x):
    return 0.5 * x * (1.0 + jnp.tanh(_GELU_C * (x + _GELU_A * x * x * x)))


def _gelu_grad(x):
    x2 = x * x
    t = jnp.tanh(_GELU_C * (x + _GELU_A * x * x2))
    return 0.5 * (1.0 + t) + 0.5 * x * (1.0 - t * t) * _GELU_C * (1.0 + 3.0 * _GELU_A * x2)


def _rms_stats(x):
    r = lax.rsqrt(jnp.mean(x * x, axis=-1, keepdims=True) + NORM_EPS)
    return r, x * r


def _rms_bwd(dh, xh, r, g):
    dg = jnp.sum(dh * xh, axis=0, keepdims=True)
    dxn = dh * g
    dx = r * (dxn - xh * jnp.mean(dxn * xh, axis=-1, keepdims=True))
    return dx, dg


def _tri(n, upper=False):
    r = lax.broadcasted_iota(jnp.int32, (n, n), 0)
    c = lax.broadcasted_iota(jnp.int32, (n, n), 1)
    return (c >= r) if upper else (c <= r)


def _acc_rows(ref, first, val):
    @pl.when(first)
    def _():
        ref[...] = val

    @pl.when(jnp.logical_not(first))
    def _():
        ref[...] += val


def _rms_mm(x, g, w_t, tm=1024, tn=1536):
    s, n = x.shape[0], w_t.shape[0]
    jm = GATE_COL0 // tn

    def body(x_ref, g_ref, w_ref, pm_ref, pg_ref, h_ref, hs):
        j = pl.program_id(1)

        @pl.when(j == 0)
        def _():
            _, xh = _rms_stats(x_ref[...])
            hv = (xh * g_ref[...]).astype(bf16)
            hs[...] = hv
            h_ref[...] = hv

        res = _dot_nt(hs[...], w_ref[...])

        @pl.when(j < jm)
        def _():
            pm_ref[...] = res

        @pl.when(j >= jm)
        def _():
            pg_ref[...] = res.astype(bf16)

    return pl.pallas_call(
        body, name="rms_mm", grid=(s // tm, n // tn),
        in_specs=[pl.BlockSpec((tm, D_MODEL), lambda i, j: (i, 0)), pl.BlockSpec((1, D_MODEL), lambda i, j: (0, 0)),
                  pl.BlockSpec((tn, D_MODEL), lambda i, j: (j, 0))],
        out_specs=[pl.BlockSpec((tm, tn), lambda i, j: (i, jnp.minimum(j, jm - 1))),
                   pl.BlockSpec((tm, tn), lambda i, j: (i, jnp.maximum(j - jm, 0))), pl.BlockSpec((tm, D_MODEL), lambda i, j: (i, 0))],
        out_shape=[SDS((s, GATE_COL0), f32), SDS((s, n - GATE_COL0), bf16), SDS((s, D_MODEL), bf16)],
        scratch_shapes=[pltpu.VMEM((tm, D_MODEL), bf16)], compiler_params=_cp("parallel", "arbitrary"),
    )(x, g, w_t)


def _hgrn_gates(fp, lb):
    logf = jnp.logaddexp(jnp.log(jnp.maximum(lb, LB_FLOOR)), jnp.log1p(-lb) + jax.nn.log_sigmoid(fp))
    snf = _sigmoid(-fp)
    return logf, snf, (1.0 - lb) * snf


def _p_specs(tile, cols, row_map):
    return [pl.BlockSpec((tile, BRANCH), functools.partial(lambda c, i: (row_map(i), c), c)) for c in cols]


def _mixer_fwd(p, lb, gout, wconv, lng, lnb, wsg, bsg_t):
    s = p.shape[0]
    tt = MIX_TILE
    nch = tt // HGRN_CHUNK

    def body(q_ref, fp_ref, iv_ref, go_ref, bg_ref, cg_ref, xc_ref, u_ref, v_ref, lb_ref, gout_ref, wconv_ref, lng_ref,
             lnb_ref, wsg_ref, bsg_ref, z_ref, opre_ref, st_ref, st_scr, zbuf):
        @pl.when(pl.program_id(0) == 0)
        def _():
            st_scr[...] = jnp.zeros_like(st_scr)
            zbuf[0:8, :] = jnp.zeros((8, BRANCH), f32)

        lbv = lb_ref[...]
        q_raw = q_ref[...]
        qs = q_raw * _sigmoid(q_raw)
        logf, _, kk = _hgrn_gates(fp_ref[...], lbv)
        iv = iv_ref[...]
        causal = _tri(HGRN_CHUNK)
        tri = causal.astype(f32)
        last_row = lax.broadcasted_iota(jnp.int32, (HGRN_CHUNK, 1), 0) == HGRN_CHUNK - 1
        for c in range(nch):
            rows = slice(HGRN_CHUNK * c, HGRN_CHUNK * (c + 1))
            b = _dot_exact(tri, logf[rows])
            bl = jnp.sum(jnp.where(last_row, b, 0.0), axis=0, keepdims=True)
            qb = (qs[rows] * jnp.exp(b)).astype(bf16)
            kb = (kk[rows] * jnp.exp(-b)).astype(bf16)
            kd = (kk[rows] * jnp.exp(bl - b)).astype(bf16)
            ebl = jnp.exp(bl)
            vc = iv[rows].astype(bf16)
            for h in range(HEADS):
                sl = slice(HEAD_DIM * h, HEAD_DIM * (h + 1))
                st = st_scr[h]
                st_ref[c, h] = st
                a = jnp.where(causal, _dot_nt(qb[:, sl], kb[:, sl]), 0.0)
                opre_ref[rows, sl] = _dot(a.astype(bf16), vc[:, sl]) + _dot_nt(qb[:, sl], st.astype(bf16))
                st_scr[h] = st * ebl[:, sl] + _dot_tn(vc[:, sl], kd[:, sl])

        o = opre_ref[...]
        go = go_ref[...]
        gout_v = gout_ref[...]
        for h in range(HEADS):
            sl = slice(HEAD_DIM * h, HEAD_DIM * (h + 1))
            _, oh = _rms_stats(o[:, sl])
            z_ref[:, sl] = (oh * gout_v[:, sl] * _sigmoid(go[:, sl])).astype(bf16)

        zc = cg_ref[...] * xc_ref[...]
        zbuf[8:8 + tt, :] = zc
        y = wconv_ref[0:1, :] * zbuf[pl.ds(6, tt), :] + wconv_ref[1:2, :] * zbuf[pl.ds(7, tt), :] + wconv_ref[2:3, :] * zc
        z_ref[:, BRANCH:2 * BRANCH] = (bg_ref[...] * y).astype(bf16)
        zbuf[0:8, :] = zbuf[tt:tt + 8, :]

        ug = _gelu(u_ref[...])
        vg = _gelu(v_ref[...])
        vcen = vg - jnp.mean(vg, axis=-1, keepdims=True)
        rstd = lax.rsqrt(jnp.mean(vcen * vcen, axis=-1, keepdims=True) + LN_EPS)
        vn = (vcen * rstd * lng_ref[...] + lnb_ref[...]).astype(bf16)
        low = _tri(SG_CHUNK)
        for g in range(SG_GROUPS):
            sl = slice(LANE * g, LANE * (g + 1))
            wm = jnp.where(low, wsg_ref[g], 0.0).astype(bf16)
            bias = bsg_ref[:, g:g + 1]
            for cc in range(tt // SG_CHUNK):
                rows = slice(SG_CHUNK * cc, SG_CHUNK * (cc + 1))
                sv = _dot(wm, vn[rows, sl]) + bias
                z_ref[rows, 2 * BRANCH + LANE * g:2 * BRANCH + LANE * (g + 1)] = (ug[rows, sl] * sv).astype(bf16)

    full = lambda shape: pl.BlockSpec(shape, lambda i: (0,) * len(shape))
    return pl.pallas_call(
        body, name="mixer_fwd", grid=(s // tt,),
        in_specs=_p_specs(tt, range(9), lambda i: i) + [full((1, BRANCH)), full((1, BRANCH)), full((3, BRANCH)), full((1, BRANCH)),
                                                        full((1, BRANCH)), full((SG_GROUPS, SG_CHUNK, SG_CHUNK)), full((SG_CHUNK, SG_GROUPS))],
        out_specs=[pl.BlockSpec((tt, 3 * BRANCH), lambda i: (i, 0)), pl.BlockSpec((tt, BRANCH), lambda i: (i, 0)),
                   pl.BlockSpec((nch, HEADS, HEAD_DIM, HEAD_DIM), lambda i: (i, 0, 0, 0))],
        out_shape=[SDS((s, 3 * BRANCH), bf16), SDS((s, BRANCH), f32), SDS((s // HGRN_CHUNK, HEADS, HEAD_DIM, HEAD_DIM), f32)],
        scratch_shapes=[pltpu.VMEM((HEADS, HEAD_DIM, HEAD_DIM), f32), pltpu.VMEM((tt + 8, BRANCH), f32)],
        compiler_params=_cp("arbitrary"),
    )(*([p] * 9), lb, gout, wconv, lng, lnb, wsg, bsg_t)


def _branch_gate(z, wb, pg, x, wo, tm=512):
    s = z.shape[0]

    def body(z_ref, wb_ref, g_ref, x_ref, wo_ref, y_ref, m_ref, x1_ref):
        acc = None
        for n in range(3):
            cols = slice(D_MODEL * n, D_MODEL * (n + 1))
            yn = _dot(z_ref[:, BRANCH * n:BRANCH * (n + 1)], wb_ref[n])
            y_ref[:, cols] = yn.astype(bf16)
            t = _sigmoid(g_ref[:, cols].astype(f32)) * yn
            acc = t if acc is None else acc + t
        merged = acc.astype(bf16)
        m_ref[...] = merged
        x1_ref[...] = x_ref[...] + _dot(merged, wo_ref[...])

    row = pl.BlockSpec((tm, D_MODEL), lambda i: (i, 0))
    wide = pl.BlockSpec((tm, 3 * D_MODEL), lambda i: (i, 0))
    return pl.pallas_call(
        body, name="branch_gate", grid=(s // tm,),
        in_specs=[pl.BlockSpec((tm, 3 * BRANCH), lambda i: (i, 0)), pl.BlockSpec((3, BRANCH, D_MODEL), lambda i: (0, 0, 0)), wide, row,
                  pl.BlockSpec((D_MODEL, D_MODEL), lambda i: (0, 0))],
        out_specs=[wide, row, row],
        out_shape=[SDS((s, 3 * D_MODEL), bf16), SDS((s, D_MODEL), bf16), SDS((s, D_MODEL), f32)], compiler_params=_cp("parallel"),
    )(z, wb, pg, x, wo)


def _ffn(x1, g, w1, w2, tm=1024, tf=1024):
    s = x1.shape[0]
    nf = D_FF // tf

    def body(x_ref, g_ref, w1_ref, w2_ref, o_ref, h_ref, ra_ref, hs, acc):
        f = pl.program_id(1)

        @pl.when(f == 0)
        def _():
            _, xh = _rms_stats(x_ref[...])
            hv = (xh * g_ref[...]).astype(bf16)
            hs[...] = hv
            h_ref[...] = hv
            acc[...] = jnp.zeros_like(acc)

        ra = jnp.maximum(_dot(hs[...], w1_ref[...]), 0.0)
        ra_ref[...] = ra.astype(bf16)
        acc[...] += _dot((ra * ra).astype(bf16), w2_ref[...])

        @pl.when(f == nf - 1)
        def _():
            o_ref[...] = x_ref[...] + acc[...]

    return pl.pallas_call(
        body, name="ffn", grid=(s // tm, nf),
        in_specs=[pl.BlockSpec((tm, D_MODEL), lambda i, f: (i, 0)), pl.BlockSpec((1, D_MODEL), lambda i, f: (0, 0)),
                  pl.BlockSpec((D_MODEL, tf), lambda i, f: (0, f)), pl.BlockSpec((tf, D_MODEL), lambda i, f: (f, 0))],
        out_specs=[pl.BlockSpec((tm, D_MODEL), lambda i, f: (i, 0)), pl.BlockSpec((tm, D_MODEL), lambda i, f: (i, 0)),
                   pl.BlockSpec((tm, tf), lambda i, f: (i, f))],
        out_shape=[SDS((s, D_MODEL), f32), SDS((s, D_MODEL), bf16), SDS((s, D_FF), bf16)],
        scratch_shapes=[pltpu.VMEM((tm, D_MODEL), bf16), pltpu.VMEM((tm, D_MODEL), f32)], compiler_params=_cp("parallel", "arbitrary"),
    )(x1, g, w1, w2)


def _final(x, target, g, tm=512):
    s = x.shape[0]

    def body(x_ref, t_ref, g_ref, loss_ref, dx_ref, dxb_ref, dg_ref):
        first = pl.program_id(0) == 0
        gv = g_ref[...]
        r, xh = _rms_stats(x_ref[...])
        e = xh * gv - t_ref[...]
        tile_loss = 0.5 * jnp.sum(jnp.mean(e * e, axis=-1, keepdims=True), axis=0, keepdims=True)
        dx, dg = _rms_bwd(e * (1.0 / D_MODEL), xh, r, gv)
        dx_ref[...] = dx
        dxb_ref[...] = dx.astype(bf16)
        _acc_rows(dg_ref, first, dg)
        _acc_rows(loss_ref, first, jnp.broadcast_to(tile_loss, (1, LANE)))

    row = pl.BlockSpec((tm, D_MODEL), lambda i: (i, 0))
    return pl.pallas_call(
        body, name="final_loss", grid=(s // tm,), in_specs=[row, row, pl.BlockSpec((1, D_MODEL), lambda i: (0, 0))],
        out_specs=[pl.BlockSpec((1, LANE), lambda i: (0, 0)), row, row, pl.BlockSpec((1, D_MODEL), lambda i: (0, 0))],
        out_shape=[SDS((1, LANE), f32), SDS((s, D_MODEL), f32), SDS((s, D_MODEL), bf16), SDS((1, D_MODEL), f32)],
        compiler_params=_cp("arbitrary"),
    )(x, target, g)


def _ffn_bwd(dx2, dx2b, x1, g, ra, w1, w2, tm=512, tf=2048):
    s = x1.shape[0]
    nf = D_FF // tf

    def body(dx_ref, dxb_ref, x_ref, g_ref, ra_ref, w1_ref, w2_ref, da_ref, dx1_ref, dx1b_ref, dg_ref, acc):
        i, f = pl.program_id(0), pl.program_id(1)

        @pl.when(f == 0)
        def _():
            acc[...] = jnp.zeros_like(acc)

        da = (_dot_nt(dxb_ref[...], w2_ref[...]) * (2.0 * ra_ref[...].astype(f32))).astype(bf16)
        da_ref[...] = da
        acc[...] += _dot_nt(da, w1_ref[...])

        @pl.when(f == nf - 1)
        def _():
            r, xh = _rms_stats(x_ref[...])
            dx, dg = _rms_bwd(acc[...], xh, r, g_ref[...])
            dx = dx + dx_ref[...]
            dx1_ref[...] = dx
            dx1b_ref[...] = dx.astype(bf16)
            _acc_rows(dg_ref, i == 0, dg)

    row = pl.BlockSpec((tm, D_MODEL), lambda i, f: (i, 0))
    col = pl.BlockSpec((tm, tf), lambda i, f: (i, f))
    return pl.pallas_call(
        body, name="ffn_bwd", grid=(s // tm, nf),
        in_specs=[row, row, row, pl.BlockSpec((1, D_MODEL), lambda i, f: (0, 0)), col,
                  pl.BlockSpec((D_MODEL, tf), lambda i, f: (0, f)), pl.BlockSpec((tf, D_MODEL), lambda i, f: (f, 0))],
        out_specs=[col, row, row, pl.BlockSpec((1, D_MODEL), lambda i, f: (0, 0))],
        out_shape=[SDS((s, D_FF), bf16), SDS((s, D_MODEL), f32), SDS((s, D_MODEL), bf16), SDS((1, D_MODEL), f32)],
        scratch_shapes=[pltpu.VMEM((tm, D_MODEL), f32)], compiler_params=_cp("arbitrary", "arbitrary"),
    )(dx2, dx2b, x1, g, ra, w1, w2)


def _mm_tn(a, b, nb, m, n, tm, tn, name="mm_tn", rows=None, row0=0, into=None, square_a=False):
    s = a.shape[0]
    mi, nj = m // tm, n // tn
    rows = m if rows is None else rows
    blk0 = row0 // tm

    def body(a_ref, b_ref, *rest):
        av = a_ref[...]
        if square_a:
            av = av.astype(f32)
            av = (av * av).astype(bf16)
        rest[-1][...] = _dot_tn(av, b_ref[...]).astype(bf16)

    extra = {} if into is None else dict(input_output_aliases={2: 0})
    return pl.pallas_call(
        body, name=name, grid=(nb, mi, nj),
        in_specs=[pl.BlockSpec((s, tm), lambda k, i, j: (0, k * mi + i)), pl.BlockSpec((s, tn), lambda k, i, j: (0, k * nj + j))]
        + ([] if into is None else [pl.BlockSpec(memory_space=pl.ANY)]),
        out_specs=pl.BlockSpec((None, tm, tn), lambda k, i, j: (k, blk0 + i, j)), out_shape=SDS((nb, rows, n), bf16),
        compiler_params=_cp("parallel", "parallel", "parallel"), **extra,
    )(a, b, *([] if into is None else [into]))


def _mm_tn_slabs(a, b, nb, m, nblk, rel, width, tm=512, name="mm_tn_slabs"):
    s = a.shape[0]
    n = b.shape[1] // nb
    ng, mi, nw = n // nblk, m // tm, len(rel)

    def body(a_ref, b_ref, o_ref):
        full = _dot_tn(a_ref[...], b_ref[...])
        for r, start in enumerate(rel):
            o_ref[r] = full[:, start:start + width].astype(bf16)

    return pl.pallas_call(
        body, name=name, grid=(nb, ng, mi),
        in_specs=[pl.BlockSpec((s, tm), lambda k, g, i: (0, k * mi + i)), pl.BlockSpec((s, nblk), lambda k, g, i: (0, k * ng + g))],
        out_specs=pl.BlockSpec((nw, None, tm, width), lambda k, g, i: (g, k, i, 0)), out_shape=SDS((ng * nw, nb, m, width), bf16),
        compiler_params=_cp("parallel", "parallel", "parallel"),
    )(a, b)


def _merge_bwd(dx1b, wo, y, pg, wb, after, tm=512):
    s = dx1b.shape[0]

    def body(dx_ref, wo_ref, y_ref, g_ref, wb_ref, after_ref, dy_ref, dg_ref, dz_ref):
        del after_ref
        dm = _dot_nt(dx_ref[...], wo_ref[...])
        for n in range(3):
            cols = slice(D_MODEL * n, D_MODEL * (n + 1))
            gate = _sigmoid(g_ref[:, cols].astype(f32))
            t = dm * gate
            dy = t.astype(bf16)
            dy_ref[:, cols] = dy
            dg_ref[:, cols] = (t * y_ref[:, cols].astype(f32) * (1.0 - gate)).astype(bf16)
            dz_ref[:, BRANCH * n:BRANCH * (n + 1)] = _dot_nt(dy, wb_ref[n]).astype(bf16)

    wide = pl.BlockSpec((tm, 3 * D_MODEL), lambda i: (i, 0))
    return pl.pallas_call(
        body, name="merge_bwd", grid=(s // tm,),
        in_specs=[pl.BlockSpec((tm, D_MODEL), lambda i: (i, 0)), pl.BlockSpec((D_MODEL, D_MODEL), lambda i: (0, 0)), wide, wide,
                  pl.BlockSpec((3, BRANCH, D_MODEL), lambda i: (0, 0, 0)), pl.BlockSpec(memory_space=pl.ANY)],
        out_specs=[wide, wide, pl.BlockSpec((tm, 3 * BRANCH), lambda i: (i, 0))],
        out_shape=[SDS((s, 3 * D_MODEL), bf16), SDS((s, 3 * D_MODEL), bf16), SDS((s, 3 * BRANCH), bf16)],
        compiler_params=_cp("parallel"),
    )(dx1b, wo, y, pg, wb, after)


def _mixer_bwd(p, dz, opre, states, lb, gout, wconv, lng, lnb, wsg, bsg_t):
    s = p.shape[0]
    tt = MIX_TILE
    nt = s // tt
    nch = tt // HGRN_CHUNK
    rev = lambda i: nt - 1 - i

    def body(q_ref, fp_ref, iv_ref, go_ref, bg_ref, cg_ref, xc_ref, u_ref, v_ref, cgp_ref, xcp_ref, dz_ref, opre_ref, st_ref,
             lb_ref, gout_ref, wconv_ref, lng_ref, lnb_ref, wsg_ref, bsg_ref,
             dp_ref, vec_ref, dwsg_ref, dbsg_ref, dst_scr, zbuf, dybuf, dvn_scr, dbsg_acc):
        i = pl.program_id(0)

        @pl.when(i == 0)
        def _():
            dst_scr[...] = jnp.zeros_like(dst_scr)
            dybuf[tt:tt + 8, :] = jnp.zeros((8, BRANCH), f32)
            vec_ref[...] = jnp.zeros_like(vec_ref)
            dwsg_ref[...] = jnp.zeros_like(dwsg_ref)
            dbsg_acc[...] = jnp.zeros_like(dbsg_acc)

        lbv = lb_ref[...]
        q_raw, fp = q_ref[...], fp_ref[...]
        sq = _sigmoid(q_raw)
        qs = q_raw * sq
        sfp = _sigmoid(fp)
        logf, snf, kk = _hgrn_gates(fp, lbv)
        inv_f = jnp.exp(-logf)
        iv = iv_ref[...]
        doa = dz_ref[:, 0:BRANCH].astype(f32)
        o = opre_ref[...]
        sgo = _sigmoid(go_ref[...])
        gout_v = gout_ref[...]
        d_o, dgo, dgout = [], [], []
        for h in range(HEADS):
            sl = slice(HEAD_DIM * h, HEAD_DIM * (h + 1))
            r, oh = _rms_stats(o[:, sl])
            d_on = doa[:, sl] * sgo[:, sl]
            dgo.append(doa[:, sl] * oh * gout_v[:, sl] * sgo[:, sl] * (1.0 - sgo[:, sl]))
            dx, dg = _rms_bwd(d_on, oh, r, gout_v[:, sl])
            d_o.append(dx)
            dgout.append(dg)
        d_o = jnp.concatenate(d_o, axis=1)
        dp_ref[:, 3 * BRANCH:4 * BRANCH] = jnp.concatenate(dgo, axis=1).astype(bf16)
        vec_ref[1:2, :] += jnp.concatenate(dgout, axis=1)

        causal = _tri(HGRN_CHUNK)
        tri = causal.astype(f32)
        tri_up = _tri(HGRN_CHUNK, upper=True).astype(f32)
        last_row = lax.broadcasted_iota(jnp.int32, (HGRN_CHUNK, 1), 0) == HGRN_CHUNK - 1
        lb_live = (lbv > LB_FLOOR).astype(f32)
        dlb = jnp.zeros((1, BRANCH), f32)
        for c in reversed(range(nch)):
            rows = slice(HGRN_CHUNK * c, HGRN_CHUNK * (c + 1))
            b = _dot_exact(tri, logf[rows])
            bl = jnp.sum(jnp.where(last_row, b, 0.0), axis=0, keepdims=True)
            eb, enb, edl, ebl = jnp.exp(b), jnp.exp(-b), jnp.exp(bl - b), jnp.exp(bl)
            qbf, kbf, kdf = qs[rows] * eb, kk[rows] * enb, kk[rows] * edl
            qb, kb, kd = qbf.astype(bf16), kbf.astype(bf16), kdf.astype(bf16)
            vc = iv[rows].astype(bf16)
            dob = d_o[rows].astype(bf16)
            dv, dqb, dkb, dkd, debl = [], [], [], [], []
            for h in range(HEADS):
                sl = slice(HEAD_DIM * h, HEAD_DIM * (h + 1))
                st = st_ref[c, h]
                dst = dst_scr[h]
                stb, dstb = st.astype(bf16), dst.astype(bf16)
                a = jnp.where(causal, _dot_nt(qb[:, sl], kb[:, sl]), 0.0).astype(bf16)
                da = jnp.where(causal, _dot_nt(dob[:, sl], vc[:, sl]), 0.0).astype(bf16)
                dv.append(_dot_tn(a, dob[:, sl]) + _dot_nt(kd[:, sl], dstb))
                dqb.append(_dot(dob[:, sl], stb) + _dot(da, kb[:, sl]))
                dkb.append(_dot_tn(da, qb[:, sl]))
                dkd.append(_dot(vc[:, sl], dstb))
                debl.append(jnp.sum(st * dst, axis=0, keepdims=True))
                dst_scr[h] = _dot_tn(dob[:, sl], qb[:, sl]) + dst * ebl[:, sl]
            dv, dqb, dkb, dkd = (jnp.concatenate(t, axis=1) for t in (dv, dqb, dkb, dkd))
            debl = jnp.concatenate(debl, axis=1)
            t_kd = dkd * kdf
            dbl = ebl * debl + jnp.sum(t_kd, axis=0, keepdims=True)
            db = dqb * qbf - dkb * kbf - t_kd + jnp.where(last_row, dbl, 0.0)
            dkk = dkb * enb + dkd * edl
            dlc = _dot_exact(tri_up, db)
            sq_c, q_c, sfp_c, snf_c, invf_c = sq[rows], q_raw[rows], sfp[rows], snf[rows], inv_f[rows]
            slope = (1.0 - lbv) * sfp_c * snf_c
            dp_ref[rows, 0:BRANCH] = (dqb * eb * sq_c * (1.0 + q_c * (1.0 - sq_c))).astype(bf16)
            dp_ref[rows, BRANCH:2 * BRANCH] = (slope * (dlc * invf_c - dkk)).astype(bf16)
            dp_ref[rows, 2 * BRANCH:3 * BRANCH] = dv.astype(bf16)
            dlb = dlb + jnp.sum(dlc * (lb_live - sfp_c) * invf_c - dkk * snf_c, axis=0, keepdims=True)
        vec_ref[0:1, :] += dlb

        dob_ = dz_ref[:, BRANCH:2 * BRANCH].astype(f32)
        bg, cg, xc = bg_ref[...], cg_ref[...], xc_ref[...]
        zc = cg * xc
        zbuf[0:8, :] = jnp.where(i < nt - 1, cgp_ref[...] * xcp_ref[...], 0.0)
        zbuf[8:8 + tt, :] = zc
        w0, w1, w2 = wconv_ref[0:1, :], wconv_ref[1:2, :], wconv_ref[2:3, :]
        y = w0 * zbuf[pl.ds(6, tt), :] + w1 * zbuf[pl.ds(7, tt), :] + w2 * zc
        dy = dob_ * bg
        dybuf[0:tt, :] = dy
        dy1, dy2 = dybuf[pl.ds(1, tt), :], dybuf[pl.ds(2, tt), :]
        dzc = w2 * dy + w1 * dy1 + w0 * dy2
        dp_ref[:, 4 * BRANCH:5 * BRANCH] = (dob_ * y).astype(bf16)
        dp_ref[:, 5 * BRANCH:6 * BRANCH] = (dzc * xc).astype(bf16)
        dp_ref[:, 6 * BRANCH:7 * BRANCH] = (dzc * cg).astype(bf16)
        vec_ref[4:5, :] += jnp.sum(zc * dy2, axis=0, keepdims=True)
        vec_ref[5:6, :] += jnp.sum(zc * dy1, axis=0, keepdims=True)
        vec_ref[6:7, :] += jnp.sum(zc * dy, axis=0, keepdims=True)
        dybuf[tt:tt + 8, :] = dybuf[0:8, :]

        doc = dz_ref[:, 2 * BRANCH:3 * BRANCH].astype(f32)
        u_raw, v_raw = u_ref[...], v_ref[...]
        ug = _gelu(u_raw)
        dug_scale = _gelu_grad(u_raw)
        vg = _gelu(v_raw)
        vcen = vg - jnp.mean(vg, axis=-1, keepdims=True)
        rstd = lax.rsqrt(jnp.mean(vcen * vcen, axis=-1, keepdims=True) + LN_EPS)
        vhat = vcen * rstd
        lng_v = lng_ref[...]
        vn = (vhat * lng_v + lnb_ref[...]).astype(bf16)
        low = _tri(SG_CHUNK)
        for g in range(SG_GROUPS):
            sl = slice(LANE * g, LANE * (g + 1))
            wm = jnp.where(low, wsg_ref[g], 0.0).astype(bf16)
            bias = bsg_ref[:, g:g + 1]
            dw = jnp.zeros((SG_CHUNK, SG_CHUNK), f32)
            dbs = jnp.zeros((SG_CHUNK, LANE), f32)
            for cc in range(tt // SG_CHUNK):
                rows = slice(SG_CHUNK * cc, SG_CHUNK * (cc + 1))
                vn_c = vn[rows, sl]
                sv = _dot(wm, vn_c) + bias
                doc_c = doc[rows, sl]
                dp_ref[rows, 7 * BRANCH + LANE * g:7 * BRANCH + LANE * (g + 1)] = (doc_c * sv * dug_scale[rows, sl]).astype(bf16)
                dsv = doc_c * ug[rows, sl]
                dsvb = dsv.astype(bf16)
                dbs = dbs + dsv
                dw = dw + _dot_nt(dsvb, vn_c)
                dvn_scr[rows, sl] = _dot_tn(wm, dsvb)
            dwsg_ref[g] += jnp.where(low, dw, 0.0)
            dbsg_acc[:, sl] += dbs
        dvn = dvn_scr[...]
        vec_ref[2:3, :] += jnp.sum(dvn * vhat, axis=0, keepdims=True)
        vec_ref[3:4, :] += jnp.sum(dvn, axis=0, keepdims=True)
        dvh = dvn * lng_v
        dvg = rstd * (dvh - jnp.mean(dvh, axis=-1, keepdims=True) - vhat * jnp.mean(dvh * vhat, axis=-1, keepdims=True))
        dp_ref[:, 8 * BRANCH:9 * BRANCH] = (dvg * _gelu_grad(v_raw)).astype(bf16)

        @pl.when(i == nt - 1)
        def _():
            for g in range(SG_GROUPS):
                dbsg_ref[:, g:g + 1] = jnp.sum(dbsg_acc[:, LANE * g:LANE * (g + 1)], axis=1, keepdims=True)

    full = lambda shape: pl.BlockSpec(shape, lambda i: (0,) * len(shape))
    tail = lambda c: pl.BlockSpec((8, BRANCH), lambda i: (jnp.maximum(rev(i) * (tt // 8) - 1, 0), c))
    return pl.pallas_call(
        body, name="mixer_bwd", grid=(nt,),
        in_specs=_p_specs(tt, range(9), rev) + [tail(5), tail(6), pl.BlockSpec((tt, 3 * BRANCH), lambda i: (rev(i), 0)),
                                                pl.BlockSpec((tt, BRANCH), lambda i: (rev(i), 0)),
                                                pl.BlockSpec((nch, HEADS, HEAD_DIM, HEAD_DIM), lambda i: (rev(i), 0, 0, 0)),
                                                full((1, BRANCH)), full((1, BRANCH)), full((3, BRANCH)), full((1, BRANCH)), full((1, BRANCH)),
                                                full((SG_GROUPS, SG_CHUNK, SG_CHUNK)), full((SG_CHUNK, SG_GROUPS))],
        out_specs=[pl.BlockSpec((tt, 9 * BRANCH), lambda i: (rev(i), 0)), full((8, BRANCH)), full((SG_GROUPS, SG_CHUNK, SG_CHUNK)),
                   full((SG_CHUNK, SG_GROUPS))],
        out_shape=[SDS((s, 9 * BRANCH), bf16), SDS((8, BRANCH), f32), SDS((SG_GROUPS, SG_CHUNK, SG_CHUNK), f32), SDS((SG_CHUNK, SG_GROUPS), f32)],
        scratch_shapes=[pltpu.VMEM((HEADS, HEAD_DIM, HEAD_DIM), f32), pltpu.VMEM((tt + 8, BRANCH), f32), pltpu.VMEM((tt + 8, BRANCH), f32),
                        pltpu.VMEM((tt, BRANCH), f32), pltpu.VMEM((SG_CHUNK, BRANCH), f32)],
        compiler_params=_cp("arbitrary"),
    )(*([p] * 11), dz, opre, states, lb, gout, wconv, lng, lnb, wsg, bsg_t)


def _dh_bwd(dpm, dpg, w_t, x, dx1, g, after, tm=1024, tk=1536):
    s = x.shape[0]
    km = dpm.shape[1] // tk
    nk = km + dpg.shape[1] // tk

    def body(dpm_ref, dpg_ref, w_ref, x_ref, dx1_ref, g_ref, after_ref, dx_ref, dxb_ref, dg_ref, acc):
        del after_ref
        i, k = pl.program_id(0), pl.program_id(1)

        @pl.when(k == 0)
        def _():
            acc[...] = jnp.zeros_like(acc)

        @pl.when(k < km)
        def _():
            acc[...] += _dot(dpm_ref[...], w_ref[...])

        @pl.when(k >= km)
        def _():
            acc[...] += _dot(dpg_ref[...], w_ref[...])

        @pl.when(k == nk - 1)
        def _():
            r, xh = _rms_stats(x_ref[...])
            dx, dg = _rms_bwd(acc[...], xh, r, g_ref[...])
            dx = dx + dx1_ref[...]
            dx_ref[...] = dx
            dxb_ref[...] = dx.astype(bf16)
            _acc_rows(dg_ref, i == 0, dg)

    row = pl.BlockSpec((tm, D_MODEL), lambda i, k: (i, 0))
    vec = pl.BlockSpec((1, D_MODEL), lambda i, k: (0, 0))
    return pl.pallas_call(
        body, name="dh_bwd", grid=(s // tm, nk),
        in_specs=[pl.BlockSpec((tm, tk), lambda i, k: (i, jnp.minimum(k, km - 1))),
                  pl.BlockSpec((tm, tk), lambda i, k: (i, jnp.maximum(k - km, 0))),
                  pl.BlockSpec((tk, D_MODEL), lambda i, k: (k, 0)), row, row, vec, pl.BlockSpec(memory_space=pl.ANY)],
        out_specs=[row, row, vec], out_shape=[SDS((s, D_MODEL), f32), SDS((s, D_MODEL), bf16), SDS((1, D_MODEL), f32)],
        scratch_shapes=[pltpu.VMEM((tm, D_MODEL), f32)], compiler_params=_cp("arbitrary", "arbitrary"),
    )(dpm, dpg, w_t, x, dx1, g, after)


def _layer_fwd(x, weight, sm):
    p, pg, h = _rms_mm(x, sm["g_mix"], weight("w_in", x))
    z, opre, states = _mixer_fwd(p, sm["lb"], sm["g_out"], sm["w_conv"], sm["ln_g"], sm["ln_b"], sm["w_sg"], sm["b_sg_t"])
    y, merged, x1 = _branch_gate(z, weight("w_branch", z), pg, x, weight("w_o", z))
    x2, h2, ra = _ffn(x1, sm["g_ffn"], weight("w_ff1", x1), weight("w_ff2", x1))
    saved = dict(x=x, p=p, pg=pg, h=h, z=z, opre=opre, states=states, y=y, merged=merged, x1=x1, h2=h2, ra=ra)
    return x2, saved


def _layer_bwd(dx2, dx2b, sv, w, sm, between, before_end):
    nchip = N_DEV // 2
    by_chip = lambda g: g.reshape((nchip, 2) + g.shape[1:])
    da, dx1, dx1b, dg_ffn = _ffn_bwd(dx2, dx2b, sv["x1"], sm["g_ffn"], sv["ra"], w["w_ff1"], w["w_ff2"])
    g_ff2 = by_chip(_mm_tn(sv["ra"], dx2b, 1, D_FF, D_MODEL, 512, 1024, name="dw_ff2", square_a=True)[0]
                    .reshape(N_DEV, D_FF // N_DEV, D_MODEL))
    g_ff1 = by_chip(_mm_tn_slabs(sv["h2"], da, 1, D_MODEL, D_FF // 2, [i * (D_FF // N_DEV) for i in range(nchip)], D_FF // N_DEV,
                                 name="dw_ff1")[:, 0])
    g_o = by_chip(_mm_tn(sv["merged"], dx1b, 1, D_MODEL, D_MODEL, 512, 1024, name="dw_o")[0].reshape(N_DEV, D_MODEL // N_DEV, D_MODEL))
    dy, dpg, dz = _merge_bwd(dx1b, w["w_o"], sv["y"], sv["pg"], w["w_branch"], between(dx1))
    g_branch = by_chip(_mm_tn_slabs(sv["z"], dy, 3, BRANCH, D_MODEL, [i * (D_MODEL // N_DEV) for i in range(N_DEV)], D_MODEL // N_DEV,
                                    name="dw_branch"))
    g_in = _mm_tn(dpg, sv["h"], 1, 3 * D_MODEL, D_MODEL, 768, 1024, name="dw_in_gates", rows=N_COLS, row0=GATE_COL0)
    dpm, vecs, dwsg, dbsg_t = _mixer_bwd(sv["p"], dz, sv["opre"], sv["states"], sm["lb"], sm["g_out"], sm["w_conv"],
                                         sm["ln_g"], sm["ln_b"], sm["w_sg"], sm["b_sg_t"])
    g_in = _mm_tn(dpm, sv["h"], 1, GATE_COL0, D_MODEL, 768, 1024, name="dw_in_mixers", rows=N_COLS, into=g_in)
    g_in = by_chip(g_in[0].reshape(N_DEV, SHARD_IN, D_MODEL))
    big = dict(w_in=g_in, w_branch=g_branch, w_o=g_o, w_ff1=g_ff1, w_ff2=g_ff2)
    dx, dxb, dg_mix = _dh_bwd(dpm, dpg, w["w_in"], sv["x"], dx1, sm["g_mix"], before_end(big))
    small = dict(g_mix=dg_mix, g_ffn=dg_ffn, vecs=vecs, w_sg=dwsg, b_sg_t=dbsg_t, dx1=dx1)
    return dx, dxb, big, small


BIG = ("w_in", "w_branch", "w_o", "w_ff1", "w_ff2")
ANY = pl.BlockSpec(memory_space=pl.ANY)


def _place():
    return lax.axis_index("x"), lax.axis_index("y"), lax.axis_index("c")


def _al(v, m):
    return pl.multiple_of(v * m, m)


def _shard_of(refs, dev, which=range(len(BIG))):
    out = []
    for ref, t in zip(refs, which):
        by_cols = BIG[t] in ("w_branch", "w_ff1")
        n = ref.shape[-1 if by_cols else 0] // N_DEV
        part = pl.ds(_al(dev, n), n)
        out.append(ref.at[(slice(None),) * (len(ref.shape) - 1) + (part,)] if by_cols else ref.at[part])
    return out


def _gather_out_shapes(shards):
    s_in, s_b, s_o, s_1, s_2 = (shards[n] for n in BIG)
    return [SDS((s_in.shape[1] * N_DEV, s_in.shape[2]), bf16), SDS(s_b.shape[1:3] + (s_b.shape[3] * N_DEV,), bf16),
            SDS((s_o.shape[1] * N_DEV, s_o.shape[2]), bf16), SDS((s_1.shape[1], s_1.shape[2] * N_DEV), bf16),
            SDS((s_2.shape[1] * N_DEV, s_2.shape[2]), bf16)]


def _seq_all_gather_layer(layer, which, n_early, shard_refs, out_shapes, tag=""):
    nt = len(which)
    outs = [jax.empty_ref(sh, memory_space=pltpu.MemorySpace.HBM) for sh in out_shapes]
    early, late = tuple(range(n_early)), tuple(range(n_early, nt))

    @pl.kernel(mesh=plsc.ScalarSubcoreMesh(axis_name="seq", num_cores=1), name=f"seq_all_gather_l{layer}{tag}",
               scratch_types=(pltpu.SemaphoreType.DMA((9,)), pltpu.SemaphoreType.DMA((9,))),
               compiler_params=pltpu.CompilerParams(collective_id=1))
    def launch(send_sems, recv_sems):
        x, y, c = _place()
        me, sibling = (x, y, c), (x, y, 1 - c)
        first, second, diag = _ici_route(x, y, c)
        _handshake([sibling, first, second])
        mine = [r.at[layer] for r in shard_refs]

        def copies(k, blk, to, src=None, part=range(nt)):
            dst = _shard_of(outs, 4 * blk[0] + 2 * blk[1] + blk[2], which)
            src = dst if src is None else src
            return [pltpu.make_async_remote_copy(src_ref=src[t], dst_ref=dst[t], send_sem=send_sems.at[k], recv_sem=recv_sems.at[k],
                                                 device_id=to, device_id_type=MESH) for t in part]

        def start(cps):
            for cp in cps:
                cp.start()
            return cps

        def landed(cps):
            for cp in cps:
                cp.wait_recv()

        sent = start(copies(0, me, sibling, src=mine) + copies(1, me, first, src=mine, part=early)
                     + copies(2, me, first, src=mine, part=late) + copies(3, me, second, src=mine))
        landed(copies(1, first, me, part=early))
        sent += start(copies(4, first, second, part=early) + copies(6, first, sibling, part=early))
        landed(copies(2, first, me, part=late))
        sent += start(copies(5, first, second, part=late) + copies(6, first, sibling, part=late))
        landed(copies(3, second, me))
        sent += start(copies(7, second, sibling))
        landed(copies(4, diag, me, part=early) + copies(5, diag, me, part=late))
        sent += start(copies(8, diag, sibling))
        other = lambda p: (p[0], p[1], 1 - c)
        landed(copies(0, sibling, me) + copies(6, other(second), me) + copies(7, other(first), me) + copies(8, other(diag), me))
        for cp in sent:
            cp.wait_send()

    launch()
    return [o[...] for o in outs]


def _ici_route(x, y, c):
    return (x ^ (1 - c), y ^ c, c), (x ^ c, y ^ (1 - c), c), (1 - x, 1 - y, c)


def _place_own(where, which, shards, gathered, after):
    nt = len(which)

    def body(where_ref, *refs):
        del where_ref
        for src, dst in zip(refs[:nt], refs[2 * nt + 1:]):
            dst[...] = src[...]

    in_specs, out_specs = [], []
    for t, sh in zip(which, shards):
        blk = sh.shape[1:]
        in_specs.append(pl.BlockSpec((None,) + blk, functools.partial(lambda nd, i, wh: (wh[0],) + (0,) * nd, len(blk))))
        by_cols = BIG[t] in ("w_branch", "w_ff1")
        out_specs.append(pl.BlockSpec(blk, functools.partial(
            lambda nd, cols, i, wh: (0,) * (nd - 1) + (wh[1],) if cols else (wh[1],) + (0,) * (nd - 1), len(blk), by_cols)))
    return pl.pallas_call(
        body, name="place_own", out_shape=[SDS(g.shape, g.dtype) for g in gathered],
        input_output_aliases={1 + nt + i: i for i in range(nt)}, compiler_params=_cp("arbitrary"),
        grid_spec=pltpu.PrefetchScalarGridSpec(num_scalar_prefetch=1, grid=(1,), in_specs=in_specs + [ANY] * (nt + 1), out_specs=out_specs),
    )(where, *shards, *gathered, after)


def _handshake(peers):
    barrier = pltpu.get_barrier_semaphore()
    for p in peers:
        pl.semaphore_signal(barrier, inc=1, device_id=p, device_id_type=MESH)
    pl.semaphore_wait(barrier, len(peers))


def _seq_exchange_on_chip(grads):
    nt, nchip = len(BIG), N_DEV // 2
    g_refs = [jax.new_ref(g, memory_space=pltpu.MemorySpace.HBM) for g in grads]
    outs = [jax.empty_ref(SDS((nchip,) + g.shape[2:], bf16), memory_space=pltpu.MemorySpace.HBM) for g in grads]

    @pl.kernel(mesh=plsc.ScalarSubcoreMesh(axis_name="seq", num_cores=1), name="seq_rs_on_chip",
               scratch_types=(pltpu.SemaphoreType.DMA((nchip,)), pltpu.SemaphoreType.DMA((nchip,))),
               compiler_params=pltpu.CompilerParams(collective_id=2))
    def launch(send_sems, recv_sems):
        x, y, c = _place()
        sibling = (x, y, 1 - c)
        _handshake([sibling])
        remote = [pltpu.make_async_remote_copy(src_ref=g_refs[t].at[j, 1 - c], dst_ref=outs[t].at[j], send_sem=send_sems.at[j],
                                               recv_sem=recv_sems.at[j], device_id=sibling, device_id_type=MESH)
                  for j in range(nchip) for t in range(nt)]
        for cp in remote:
            cp.start()
        for cp in remote:
            cp.wait_recv()
        for cp in remote:
            cp.wait_send()

    launch()
    return [o[...] for o in outs], [g[...] for g in g_refs]


def _seq_exchange_between_chips(sums):
    nt = len(BIG)
    s_refs = [jax.new_ref(a, memory_space=pltpu.MemorySpace.HBM) for a in sums]
    outs = [jax.empty_ref(SDS((3,) + a.shape[1:], bf16), memory_space=pltpu.MemorySpace.HBM) for a in sums]
    transit = [jax.empty_ref(SDS(a.shape[1:], bf16), memory_space=pltpu.MemorySpace.HBM) for a in sums]

    early, late = (0,), tuple(range(1, nt))

    @pl.kernel(mesh=plsc.ScalarSubcoreMesh(axis_name="seq", num_cores=1), name="seq_rs_between_chips",
               scratch_types=(pltpu.SemaphoreType.DMA((6,)), pltpu.SemaphoreType.DMA((6,))),
               compiler_params=pltpu.CompilerParams(collective_id=3))
    def launch(send_sems, recv_sems):
        x, y, c = _place()
        first, second, diag = _ici_route(x, y, c)
        _handshake([first, second])

        def copies(k, src, dst, to, part=range(nt)):
            return [pltpu.make_async_remote_copy(src_ref=src(t), dst_ref=dst(t), send_sem=send_sems.at[k], recv_sem=recv_sems.at[k],
                                                 device_id=to, device_id_type=MESH) for t in part]

        chip_of = lambda p: 2 * p[0] + p[1]
        for_diag = lambda t: s_refs[t].at[chip_of(diag)]
        through = lambda t: transit[t]
        last = lambda t: outs[t].at[2]
        direct = (copies(0, lambda t: s_refs[t].at[chip_of(first)], lambda t: outs[t].at[0], first)
                  + copies(1, lambda t: s_refs[t].at[chip_of(second)], lambda t: outs[t].at[1], second))
        via = [copies(2, for_diag, through, first, early), copies(3, for_diag, through, first, late)]
        passed = [copies(4, through, last, second, early), copies(5, through, last, second, late)]
        for cp in via[0] + direct + via[1]:
            cp.start()
        for arrived, onward in zip(via, passed):
            for cp in arrived:
                cp.wait_recv()
            for cp in onward:
                cp.start()
        sent = direct + via[0] + via[1] + passed[0] + passed[1]
        for cp in direct + passed[0] + passed[1]:
            cp.wait_recv()
        for cp in sent:
            cp.wait_send()

    launch()
    return [o[...] for o in outs], [a[...] for a in s_refs]


def _chip_sums(core, mine, other, after, steps=2):
    nt, nchip = len(mine), mine[0].shape[0]
    m4 = [a.reshape(nchip, 2, -1, a.shape[-1]) for a in mine]
    o3 = [a.reshape(nchip, -1, a.shape[-1]) for a in other]

    def body(c_ref, *refs):
        del c_ref
        for a_ref, b_ref, o_ref in zip(refs[:nt], refs[nt:2 * nt], refs[2 * nt + 1:]):
            o_ref[...] = (a_ref[...].astype(f32) + b_ref[...].astype(f32)).astype(bf16)

    tiles = [(a.shape[1] // steps, a.shape[2]) for a in o3]
    blks = [pl.BlockSpec((None,) + t, lambda j, i, c_ref: (j, i, 0)) for t in tiles]
    outs = pl.pallas_call(
        body, name="chip_sums", out_shape=[SDS(a.shape, bf16) for a in o3], compiler_params=_cp("parallel", "parallel"),
        grid_spec=pltpu.PrefetchScalarGridSpec(
            num_scalar_prefetch=1, grid=(nchip, steps),
            in_specs=[pl.BlockSpec((None, None) + t, lambda j, i, c_ref: (j, c_ref[0], i, 0)) for t in tiles] + blks + [ANY],
            out_specs=blks),
    )(core, *m4, *o3, after)
    return [o.reshape(a.shape) for o, a in zip(outs, other)]


def _all_reduce_rows(pack):
    rows = pack.shape[0]
    blk = rows // N_DEV

    def body(in_ref, out_ref, land, send1, recv1, send2, recv2):
        x, y, c = _place()
        me = 4 * x + 2 * y + c
        others = [(px, py, pc) for px in range(2) for py in range(2) for pc in range(2)]

        def is_me(p):
            return jnp.logical_and(jnp.logical_and(p[0] == x, p[1] == y), p[2] == c)

        land[me] = in_ref[pl.ds(_al(me, blk), blk), :]
        for d, p in enumerate(others):
            @pl.when(jnp.logical_not(is_me(p)))
            def _():
                pltpu.make_async_remote_copy(src_ref=in_ref.at[pl.ds(d * blk, blk), :], dst_ref=land.at[me], send_sem=send1.at[d],
                                             recv_sem=recv1.at[me], device_id=p, device_id_type=MESH).start()
        for d, p in enumerate(others):
            @pl.when(jnp.logical_not(is_me(p)))
            def _():
                cp = pltpu.make_async_remote_copy(src_ref=in_ref.at[pl.ds(d * blk, blk), :], dst_ref=land.at[d], send_sem=send1.at[d],
                                                  recv_sem=recv1.at[d], device_id=p, device_id_type=MESH)
                cp.wait_recv()
                cp.wait_send()
        total = land[0]
        for d in range(1, N_DEV):
            total = total + land[d]
        out_ref[pl.ds(_al(me, blk), blk), :] = total
        for d, p in enumerate(others):
            @pl.when(jnp.logical_not(is_me(p)))
            def _():
                mine = out_ref.at[pl.ds(_al(me, blk), blk), :]
                pltpu.make_async_remote_copy(src_ref=mine, dst_ref=mine, send_sem=send2.at[d], recv_sem=recv2.at[me],
                                             device_id=p, device_id_type=MESH).start()
        for d, p in enumerate(others):
            @pl.when(jnp.logical_not(is_me(p)))
            def _():
                theirs = out_ref.at[pl.ds(d * blk, blk), :]
                cp = pltpu.make_async_remote_copy(src_ref=theirs, dst_ref=theirs, send_sem=send2.at[d], recv_sem=recv2.at[d],
                                                  device_id=p, device_id_type=MESH)
                cp.wait_recv()
                cp.wait_send()

    vm = pl.BlockSpec(memory_space=pltpu.VMEM)
    return pl.pallas_call(
        body, name="all_reduce_rows", in_specs=[vm], out_specs=vm, out_shape=SDS((rows, LANE), f32),
        scratch_shapes=[pltpu.VMEM((N_DEV, blk, LANE), f32)] + [pltpu.SemaphoreType.DMA((N_DEV,))] * 4,
        compiler_params=pltpu.CompilerParams(vmem_limit_bytes=VMEM_LIMIT),
    )(pack)


def _lower_bounds_fwd(lower):
    def body(l_ref, o_ref):
        sm = _layer_softmax(l_ref)
        run = jnp.zeros_like(sm[0])
        for l in range(DEPTH):
            o_ref[l:l + 1, :] = run
            if l + 1 < DEPTH:
                run = run + sm[l + 1]

    return pl.pallas_call(body, name="lower_bounds_fwd", out_shape=SDS(lower.shape, f32))(lower)


def _layer_softmax(l_ref):
    rows = [l_ref[l:l + 1, :] for l in range(DEPTH)]
    top = functools.reduce(jnp.maximum, rows)
    e = [jnp.exp(r - top) for r in rows]
    tot = functools.reduce(lambda a, b: a + b, e)
    return [v / tot for v in e]


def _lower_bounds_bwd(lower, dlbs):
    def body(l_ref, d_ref, o_ref):
        sm = _layer_softmax(l_ref)
        dsm = [None] * DEPTH
        run = jnp.zeros_like(sm[0])
        dsm[0] = run
        for l in reversed(range(1, DEPTH)):
            run = run + d_ref[l:l + 1, :]
            dsm[l] = run
        inner = functools.reduce(lambda a, b: a + b, [sm[l] * dsm[l] for l in range(DEPTH)])
        for l in range(DEPTH):
            o_ref[l:l + 1, :] = sm[l] * (dsm[l] - inner)

    return pl.pallas_call(body, name="lower_bounds_bwd", out_shape=SDS(lower.shape, f32))(lower, dlbs)


_ADAM_C1 = 1.0 - ADAM_B1 ** ADAM_STEP
_ADAM_C2 = 1.0 - ADAM_B2 ** ADAM_STEP


def _adamw(w, g, m, v):
    m = ADAM_B1 * m + (1.0 - ADAM_B1) * g
    v = ADAM_B2 * v + (1.0 - ADAM_B2) * (g * g)
    delta = -ADAM_LR * ((m / _ADAM_C1) / (jnp.sqrt(v / _ADAM_C2) + ADAM_EPS) + ADAM_WD * w)
    return delta, m, v


def _adam_big(where, names, w, m, v, sums, landed, outs, after, steps=4):
    nt = len(names)
    three = lambda a: a.reshape(a.shape[0], -1, a.shape[-1])
    w3, m3, v3 = ([three(d[n]) for n in names] for d in (w, m, v))
    outs3 = [three(a) for n in names for a in outs[n]]
    sums3 = [three(a) for a in sums]
    land3 = [three(a) for a in landed]

    def body(where_ref, *refs):
        del where_ref
        o_refs = refs[5 * nt + 4 * nt + 1:]
        for t in range(nt):
            w_ref, m_ref, v_ref, sum_ref, land_ref = (refs[q * nt + t] for q in range(5))
            g = sum_ref[...].astype(f32)
            for k in range(3):
                g = g + land_ref[k].astype(f32)
            delta, nm, nv = _adamw(w_ref[...], g, m_ref[...], v_ref[...])
            for o_ref, val in zip(o_refs[4 * t:4 * t + 4], (g, delta, nm, nv)):
                o_ref[...] = val

    tiles = [(a.shape[1] // steps, a.shape[2]) for a in w3]
    own = [pl.BlockSpec((None,) + t, lambda i, wh: (wh[0], i, 0)) for t in tiles]
    res = pl.pallas_call(
        body, name="adam_big", out_shape=[SDS(a.shape, f32) for a in outs3],
        input_output_aliases={1 + 5 * nt + i: i for i in range(4 * nt)}, compiler_params=_cp("parallel"),
        grid_spec=pltpu.PrefetchScalarGridSpec(
            num_scalar_prefetch=1, grid=(steps,),
            in_specs=own * 3 + [pl.BlockSpec((None,) + t, lambda i, wh: (wh[1], i, 0)) for t in tiles]
            + [pl.BlockSpec((3,) + t, lambda i, wh: (0, i, 0)) for t in tiles] + [ANY] * (4 * nt + 1),
            out_specs=[s for s in own for _ in range(4)]),
    )(where, *w3, *m3, *v3, *sums3, *land3, *outs3, after)
    return {n: [o.reshape(w[n].shape) for o in res[4 * t:4 * t + 4]] for t, n in enumerate(names)}


def _touch(a, after):
    a2 = a.reshape(-1, a.shape[-1])

    def body(a_ref, after_ref, o_ref):
        del after_ref
        o_ref[...] = a_ref[0:8, :].astype(f32)

    return pl.pallas_call(
        body, name="touch", grid=(1,), in_specs=[pl.BlockSpec((16, LANE), lambda i: (0, 0)), ANY],
        out_specs=pl.BlockSpec((8, LANE), lambda i: (0, 0)), out_shape=SDS((8, LANE), f32),
    )(a2, after)


def _adam_rows(w, g, m, v):
    def body(w_ref, g_ref, m_ref, v_ref, d_ref, nm_ref, nv_ref):
        delta, nm, nv = _adamw(w_ref[...], g_ref[...], m_ref[...], v_ref[...])
        d_ref[...] = delta
        nm_ref[...] = nm
        nv_ref[...] = nv

    return pl.pallas_call(body, name="adam_rows", out_shape=[SDS(w.shape, f32)] * 3)(w, g, m, v)


SMALL = ("g_mix", "lower_bounds", "g_hgrn_out", "w_conv", "sg_ln_g", "sg_ln_b", "w_sg", "b_sg", "g_ffn", "g_final")
WEIGHTS = ("w_in", "g_mix", "lower_bounds", "g_hgrn_out", "w_conv", "sg_ln_g", "sg_ln_b", "w_sg", "b_sg", "w_branch", "w_o", "g_ffn",
           "w_ff1", "w_ff2", "g_final")


def _pack_rows(arrays, multiple):
    flat = jnp.concatenate([a.reshape(-1) for a in arrays])
    rows = -(-flat.shape[0] // (LANE * multiple)) * multiple
    return jnp.pad(flat, (0, rows * LANE - flat.shape[0])).reshape(rows, LANE)


def _unpack_rows(pack, like):
    flat = pack.reshape(-1)
    out, at = [], 0
    for a in like:
        out.append(flat[at:at + a.size].reshape(a.shape))
        at += a.size
    return out


def kernel(x, w_in, g_mix, lower_bounds, g_hgrn_out, w_conv, sg_ln_g, sg_ln_b, w_sg, b_sg, w_branch, w_o, g_ffn, w_ff1, w_ff2, g_final, loss_target, m_w_in, m_g_mix, m_lower_bounds, m_g_hgrn_out, m_w_conv, m_sg_ln_g, m_sg_ln_b, m_w_sg, m_b_sg, m_w_branch, m_w_o, m_g_ffn, m_w_ff1, m_w_ff2, m_g_final, v_w_in, v_g_mix, v_lower_bounds, v_g_hgrn_out, v_w_conv, v_sg_ln_g, v_sg_ln_b, v_w_sg, v_b_sg, v_w_branch, v_w_o, v_g_ffn, v_w_ff1, v_w_ff2, v_g_final):
    weights = dict(w_in=w_in, g_mix=g_mix, lower_bounds=lower_bounds, g_hgrn_out=g_hgrn_out, w_conv=w_conv, sg_ln_g=sg_ln_g,
                   sg_ln_b=sg_ln_b, w_sg=w_sg, b_sg=b_sg, w_branch=w_branch, w_o=w_o, g_ffn=g_ffn, w_ff1=w_ff1, w_ff2=w_ff2, g_final=g_final)
    mom1 = dict(w_in=m_w_in, g_mix=m_g_mix, lower_bounds=m_lower_bounds, g_hgrn_out=m_g_hgrn_out, w_conv=m_w_conv, sg_ln_g=m_sg_ln_g,
                sg_ln_b=m_sg_ln_b, w_sg=m_w_sg, b_sg=m_b_sg, w_branch=m_w_branch, w_o=m_w_o, g_ffn=m_g_ffn, w_ff1=m_w_ff1, w_ff2=m_w_ff2,
                g_final=m_g_final)
    mom2 = dict(w_in=v_w_in, g_mix=v_g_mix, lower_bounds=v_lower_bounds, g_hgrn_out=v_g_hgrn_out, w_conv=v_w_conv, sg_ln_g=v_sg_ln_g,
                sg_ln_b=v_sg_ln_b, w_sg=v_w_sg, b_sg=v_b_sg, w_branch=v_w_branch, w_o=v_w_o, g_ffn=v_g_ffn, w_ff1=v_w_ff1, w_ff2=v_w_ff2,
                g_final=v_g_final)
    xi, yi, ci = _place()
    dev = 4 * xi + 2 * yi + ci
    conv_cols = w_conv.shape[-1]

    for d in (weights, mom1, mom2):
        d["w_in"] = jnp.swapaxes(d["w_in"], 1, 2)
    shards = {n: weights[n].astype(bf16) for n in BIG}

    conv_place = lax.dynamic_update_slice(jnp.zeros((DEPTH, 3, BRANCH), f32), w_conv, (0, 0, dev * conv_cols))
    (w_conv_full,) = _unpack_rows(_all_reduce_rows(_pack_rows([conv_place], 8 * N_DEV)), [conv_place])
    lbs = _lower_bounds_fwd(lower_bounds)

    def small_of(l):
        return dict(g_mix=g_mix[l][None], lb=lbs[l][None], g_out=g_hgrn_out[l][None], w_conv=w_conv_full[l], ln_g=sg_ln_g[l][None],
                    ln_b=sg_ln_b[l][None], w_sg=w_sg[l], b_sg_t=b_sg[l].T, g_ffn=g_ffn[l][None])

    act = x[0]
    full, saved = [], []
    shard_refs = [jax.new_ref(shards[n], memory_space=pltpu.MemorySpace.HBM) for n in BIG]
    shapes = _gather_out_shapes(shards)
    every = tuple(range(len(BIG)))
    groups = [(0, (0,), 1, "a"), (0, (1, 2, 3, 4), 2, "b"), (1, (0,), 1, "a"), (1, (3, 4, 1, 2), 1, "b")]
    groups += [(l, every, 1, "") for l in range(2, DEPTH)]
    arrived = {}
    for l, which, n_early, tag in groups:
        got = _seq_all_gather_layer(l, which, n_early, [shard_refs[t] for t in which], [shapes[t] for t in which], tag)
        arrived.update({(l, BIG[t]): (which, got) for t in which})

    for l in range(DEPTH):
        full.append({})

        def weight(name, after, l=l):
            if name not in full[l]:
                which, got = arrived[(l, name)]
                where = jnp.stack([jnp.int32(l), dev.astype(jnp.int32)])
                full[l].update(zip([BIG[t] for t in which], _place_own(where, which, [shards[BIG[t]] for t in which], got, after)))
            return full[l][name]

        act, sv = _layer_fwd(act, weight, small_of(l))
        saved.append(sv)
    loss_row, dx, dxb, dg_final = _final(act, loss_target[0], g_final[None])

    core = ci.astype(jnp.int32)[None]
    big_out = {n: [lax.empty(weights[n].shape, f32) for _ in range(4)] for n in BIG}
    small_grads = [None] * DEPTH

    def chip_sums(stage, after):
        l, received, mine = stage
        sums = _chip_sums(core, mine, received, after)
        placed.append(sums[BIG.index("w_o")])
        landed, sums = _seq_exchange_between_chips(sums)
        return l, sums, landed

    def adam_layer(stage, after):
        l, sums, landed = stage
        where = jnp.stack([jnp.int32(l), (2 * xi + yi).astype(jnp.int32)])
        big_out.update(_adam_big(where, BIG, weights, mom1, mom2, sums, landed, big_out, after))

    above = None
    placed = []
    for l in reversed(range(DEPTH)):
        summed = []

        def between(dx1):
            if above is None:
                return dx1
            summed.append(chip_sums(above, dx1))
            return placed[-1]

        def before_end(big):
            return _touch(summed[0][2][BIG.index("w_o")], big["w_in"]) if summed else big["w_in"]

        dx, dxb, big, small_grads[l] = _layer_bwd(dx, dxb, saved[l], full[l], small_of(l), between, before_end)
        if summed:
            adam_layer(summed[0], dx)
        above = (l, *_seq_exchange_on_chip([big[n] for n in BIG]))

    stack = lambda f: jnp.stack([f(small_grads[l]) for l in range(DEPTH)])
    d_lower = _lower_bounds_bwd(lower_bounds, stack(lambda s: s["vecs"][0]))
    local_small = dict(g_mix=stack(lambda s: s["g_mix"][0]), lower_bounds=d_lower, g_hgrn_out=stack(lambda s: s["vecs"][1]),
                       w_conv=stack(lambda s: s["vecs"][4:7]), sg_ln_g=stack(lambda s: s["vecs"][2]), sg_ln_b=stack(lambda s: s["vecs"][3]),
                       w_sg=stack(lambda s: s["w_sg"]), b_sg=stack(lambda s: s["b_sg_t"].T), g_ffn=stack(lambda s: s["g_ffn"][0]),
                       g_final=dg_final[0])
    order = [local_small[n] for n in SMALL] + [loss_row]
    *reduced, loss_sum = _unpack_rows(_all_reduce_rows(_pack_rows(order, 8 * N_DEV)), order)
    loss = loss_sum[0, 0]
    grads = dict(zip(SMALL, reduced))
    grads["w_conv"] = lax.dynamic_slice(grads["w_conv"], (0, 0, dev * conv_cols), (DEPTH, 3, conv_cols))

    deltas, new_m, new_v = {}, {}, {}
    packs = [_pack_rows([d[n] for n in SMALL], 8) for d in (weights, grads, mom1, mom2)]
    like = [weights[n] for n in SMALL]
    small_out = _adam_rows(*packs)
    for out, pack in zip((deltas, new_m, new_v), small_out):
        out.update(zip(SMALL, _unpack_rows(pack, like)))
    adam_layer(chip_sums(above, dx), small_out[0])
    for n in BIG:
        grads[n], deltas[n], new_m[n], new_v[n] = (jnp.swapaxes(a, 1, 2) if n == "w_in" else a for a in big_out[n])

    return (loss, dx[None], *[grads[n] for n in WEIGHTS], *[deltas[n] for n in WEIGHTS], *[new_m[n] for n in WEIGHTS],
            *[new_v[n] for n in WEIGHTS])
```

```python
import functools

import jax
import jax.numpy as jnp
from jax import lax
from jax.experimental import pallas as pl
from jax.experimental.pallas import tpu as pltpu
from jax.experimental.pallas import tpu_sc as plsc

f32 = jnp.float32
bf16 = jnp.bfloat16
SDS = jax.ShapeDtypeStruct
MESH = pl.DeviceIdType.MESH

D_MODEL = 1024
BRANCH = 512
N_COLS = 7680
D_FF = 4096
DEPTH = 4
HEADS = 4
HEAD_DIM = 128
HGRN_CHUNK = 64
SG_CHUNK = 128
SG_GROUPS = 4
NORM_EPS = 1e-6
LN_EPS = 1e-5
LB_FLOOR = 1e-30
N_DEV = 8
SHARD_IN = N_COLS // N_DEV
LANE = 128
GATE_COL0 = 9 * BRANCH

ADAM_LR = 0.001
ADAM_B1 = 0.9
ADAM_B2 = 0.999
ADAM_EPS = 1e-08
ADAM_WD = 0.01
ADAM_STEP = 10

MIX_TILE = 256
VMEM_LIMIT = 56 * 1024 * 1024


def _cp(*sem):
    return pltpu.CompilerParams(dimension_semantics=sem or None, vmem_limit_bytes=VMEM_LIMIT)


def _dot(a, b):
    return jnp.dot(a, b, preferred_element_type=f32)


def _dot_nt(a, b):
    return lax.dot_general(a, b, (((1,), (1,)), ((), ())), preferred_element_type=f32)


def _dot_tn(a, b):
    return lax.dot_general(a, b, (((0,), (0,)), ((), ())), preferred_element_type=f32)


def _dot_exact(ones, b):
    hi = b.astype(bf16)
    rest = b - hi.astype(f32)
    mid = rest.astype(bf16)
    low = (rest - mid.astype(f32)).astype(bf16)
    ones = ones.astype(bf16)
    return _dot(ones, hi) + _dot(ones, mid) + _dot(ones, low)


def _sigmoid(x):
    return jax.nn.sigmoid(x)


_GELU_C = 0.7978845608028654
_GELU_A = 0.044715


def _gelu(x):
    return 0.5 * x * (1.0 + jnp.tanh(_GELU_C * (x + _GELU_A * x * x * x)))


def _gelu_grad(x):
    x2 = x * x
    t = jnp.tanh(_GELU_C * (x + _GELU_A * x * x2))
    return 0.5 * (1.0 + t) + 0.5 * x * (1.0 - t * t) * _GELU_C * (1.0 + 3.0 * _GELU_A * x2)


def _rms_stats(x):
    r = lax.rsqrt(jnp.mean(x * x, axis=-1, keepdims=True) + NORM_EPS)
    return r, x * r


def _rms_bwd(dh, xh, r, g):
    dg = jnp.sum(dh * xh, axis=0, keepdims=True)
    dxn = dh * g
    dx = r * (dxn - xh * jnp.mean(dxn * xh, axis=-1, keepdims=True))
    return dx, dg


def _tri(n, upper=False):
    r = lax.broadcasted_iota(jnp.int32, (n, n), 0)
    c = lax.broadcasted_iota(jnp.int32, (n, n), 1)
    return (c >= r) if upper else (c <= r)


def _acc_rows(ref, first, val):
    @pl.when(first)
    def _():
        ref[...] = val

    @pl.when(jnp.logical_not(first))
    def _():
        ref[...] += val


def _rms_mm(x, g, w_t, tm=1024, tn=1536):
    s, n = x.shape[0], w_t.shape[0]
    jm = GATE_COL0 // tn

    def body(x_ref, g_ref, w_ref, pm_ref, pg_ref, h_ref, hs):
        j = pl.program_id(1)

        @pl.when(j == 0)
        def _():
            _, xh = _rms_stats(x_ref[...])
            hv = (xh * g_ref[...]).astype(bf16)
            hs[...] = hv
            h_ref[...] = hv

        res = _dot_nt(hs[...], w_ref[...])

        @pl.when(j < jm)
        def _():
            pm_ref[...] = res

        @pl.when(j >= jm)
        def _():
            pg_ref[...] = res.astype(bf16)

    return pl.pallas_call(
        body, name="rms_mm", grid=(s // tm, n // tn),
        in_specs=[pl.BlockSpec((tm, D_MODEL), lambda i, j: (i, 0)), pl.BlockSpec((1, D_MODEL), lambda i, j: (0, 0)),
                  pl.BlockSpec((tn, D_MODEL), lambda i, j: (j, 0))],
        out_specs=[pl.BlockSpec((tm, tn), lambda i, j: (i, jnp.minimum(j, jm - 1))),
                   pl.BlockSpec((tm, tn), lambda i, j: (i, jnp.maximum(j - jm, 0))), pl.BlockSpec((tm, D_MODEL), lambda i, j: (i, 0))],
        out_shape=[SDS((s, GATE_COL0), f32), SDS((s, n - GATE_COL0), bf16), SDS((s, D_MODEL), bf16)],
        scratch_shapes=[pltpu.VMEM((tm, D_MODEL), bf16)], compiler_params=_cp("parallel", "arbitrary"),
    )(x, g, w_t)


def _hgrn_gates(fp, lb):
    logf = jnp.logaddexp(jnp.log(jnp.maximum(lb, LB_FLOOR)), jnp.log1p(-lb) + jax.nn.log_sigmoid(fp))
    snf = _sigmoid(-fp)
    return logf, snf, (1.0 - lb) * snf


def _p_specs(tile, cols, row_map):
    return [pl.BlockSpec((tile, BRANCH), functools.partial(lambda c, i: (row_map(i), c), c)) for c in cols]


def _mixer_fwd(p, lb, gout, wconv, lng, lnb, wsg, bsg_t):
    s = p.shape[0]
    tt = MIX_TILE
    nch = tt // HGRN_CHUNK

    def body(q_ref, fp_ref, iv_ref, go_ref, bg_ref, cg_ref, xc_ref, u_ref, v_ref, lb_ref, gout_ref, wconv_ref, lng_ref,
             lnb_ref, wsg_ref, bsg_ref, z_ref, opre_ref, st_ref, st_scr, zbuf):
        @pl.when(pl.program_id(0) == 0)
        def _():
            st_scr[...] = jnp.zeros_like(st_scr)
            zbuf[0:8, :] = jnp.zeros((8, BRANCH), f32)

        lbv = lb_ref[...]
        gout_v = gout_ref[...]
        causal = _tri(HGRN_CHUNK)
        tri = causal.astype(f32)
        last_row = lax.broadcasted_iota(jnp.int32, (HGRN_CHUNK, 1), 0) == HGRN_CHUNK - 1
        for c in range(nch):
            rows = slice(HGRN_CHUNK * c, HGRN_CHUNK * (c + 1))
            q_raw = q_ref[rows, :]
            qs = q_raw * _sigmoid(q_raw)
            logf, _, kk = _hgrn_gates(fp_ref[rows, :], lbv)
            b = _dot_exact(tri, logf)
            bl = jnp.sum(jnp.where(last_row, b, 0.0), axis=0, keepdims=True)
            qb = (qs * jnp.exp(b)).astype(bf16)
            kb = (kk * jnp.exp(-b)).astype(bf16)
            kd = (kk * jnp.exp(bl - b)).astype(bf16)
            ebl = jnp.exp(bl)
            vc = iv_ref[rows, :].astype(bf16)
            gate = _sigmoid(go_ref[rows, :])
            for h in range(HEADS):
                sl = slice(HEAD_DIM * h, HEAD_DIM * (h + 1))
                st = st_scr[h]
                st_ref[c, h] = st
                a = jnp.where(causal, _dot_nt(qb[:, sl], kb[:, sl]), 0.0)
                o = _dot(a.astype(bf16), vc[:, sl]) + _dot_nt(qb[:, sl], st.astype(bf16))
                opre_ref[rows, sl] = o
                st_scr[h] = st * ebl[:, sl] + _dot_tn(vc[:, sl], kd[:, sl])
                _, oh = _rms_stats(o)
                z_ref[rows, sl] = (oh * gout_v[:, sl] * gate[:, sl]).astype(bf16)

        zc = cg_ref[...] * xc_ref[...]
        zbuf[8:8 + tt, :] = zc
        y = wconv_ref[0:1, :] * zbuf[pl.ds(6, tt), :] + wconv_ref[1:2, :] * zbuf[pl.ds(7, tt), :] + wconv_ref[2:3, :] * zc
        z_ref[:, BRANCH:2 * BRANCH] = (bg_ref[...] * y).astype(bf16)
        zbuf[0:8, :] = zbuf[tt:tt + 8, :]

        lng_v, lnb_v = lng_ref[...], lnb_ref[...]
        low = _tri(SG_CHUNK)
        wms = [jnp.where(low, wsg_ref[g], 0.0).astype(bf16) for g in range(SG_GROUPS)]
        for cc in range(tt // SG_CHUNK):
            rows = slice(SG_CHUNK * cc, SG_CHUNK * (cc + 1))
            ug = _gelu(u_ref[rows, :])
            vg = _gelu(v_ref[rows, :])
            vcen = vg - jnp.mean(vg, axis=-1, keepdims=True)
            rstd = lax.rsqrt(jnp.mean(vcen * vcen, axis=-1, keepdims=True) + LN_EPS)
            vn = (vcen * rstd * lng_v + lnb_v).astype(bf16)
            for g in range(SG_GROUPS):
                sl = slice(LANE * g, LANE * (g + 1))
                sv = _dot(wms[g], vn[:, sl]) + bsg_ref[:, g:g + 1]
                z_ref[rows, 2 * BRANCH + LANE * g:2 * BRANCH + LANE * (g + 1)] = (ug[:, sl] * sv).astype(bf16)

    full = lambda shape: pl.BlockSpec(shape, lambda i: (0,) * len(shape))
    return pl.pallas_call(
        body, name="mixer_fwd", grid=(s // tt,),
        in_specs=_p_specs(tt, range(9), lambda i: i) + [full((1, BRANCH)), full((1, BRANCH)), full((3, BRANCH)), full((1, BRANCH)),
                                                        full((1, BRANCH)), full((SG_GROUPS, SG_CHUNK, SG_CHUNK)), full((SG_CHUNK, SG_GROUPS))],
        out_specs=[pl.BlockSpec((tt, 3 * BRANCH), lambda i: (i, 0)), pl.BlockSpec((tt, BRANCH), lambda i: (i, 0)),
                   pl.BlockSpec((nch, HEADS, HEAD_DIM, HEAD_DIM), lambda i: (i, 0, 0, 0))],
        out_shape=[SDS((s, 3 * BRANCH), bf16), SDS((s, BRANCH), f32), SDS((s // HGRN_CHUNK, HEADS, HEAD_DIM, HEAD_DIM), f32)],
        scratch_shapes=[pltpu.VMEM((HEADS, HEAD_DIM, HEAD_DIM), f32), pltpu.VMEM((tt + 8, BRANCH), f32)],
        compiler_params=_cp("arbitrary"),
    )(*([p] * 9), lb, gout, wconv, lng, lnb, wsg, bsg_t)


def _branch_gate(z, wb, pg, x, wo, tm=512):
    s = z.shape[0]

    def body(z_ref, wb_ref, g_ref, x_ref, wo_ref, y_ref, m_ref, x1_ref):
        acc = None
        for n in range(3):
            cols = slice(D_MODEL * n, D_MODEL * (n + 1))
            yn = _dot(z_ref[:, BRANCH * n:BRANCH * (n + 1)], wb_ref[n])
            y_ref[:, cols] = yn.astype(bf16)
            t = _sigmoid(g_ref[:, cols].astype(f32)) * yn
            acc = t if acc is None else acc + t
        merged = acc.astype(bf16)
        m_ref[...] = merged
        x1_ref[...] = x_ref[...] + _dot(merged, wo_ref[...])

    row = pl.BlockSpec((tm, D_MODEL), lambda i: (i, 0))
    wide = pl.BlockSpec((tm, 3 * D_MODEL), lambda i: (i, 0))
    return pl.pallas_call(
        body, name="branch_gate", grid=(s // tm,),
        in_specs=[pl.BlockSpec((tm, 3 * BRANCH), lambda i: (i, 0)), pl.BlockSpec((3, BRANCH, D_MODEL), lambda i: (0, 0, 0)), wide, row,
                  pl.BlockSpec((D_MODEL, D_MODEL), lambda i: (0, 0))],
        out_specs=[wide, row, row],
        out_shape=[SDS((s, 3 * D_MODEL), bf16), SDS((s, D_MODEL), bf16), SDS((s, D_MODEL), f32)], compiler_params=_cp("parallel"),
    )(z, wb, pg, x, wo)


def _ffn(x1, g, w1, w2, tm=1024, tf=1024):
    s = x1.shape[0]
    nf = D_FF // tf

    def body(x_ref, g_ref, w1_ref, w2_ref, o_ref, h_ref, ra_ref, hs, acc):
        f = pl.program_id(1)

        @pl.when(f == 0)
        def _():
            _, xh = _rms_stats(x_ref[...])
            hv = (xh * g_ref[...]).astype(bf16)
            hs[...] = hv
            h_ref[...] = hv
            acc[...] = jnp.zeros_like(acc)

        ra = jnp.maximum(_dot(hs[...], w1_ref[...]), 0.0)
        ra_ref[...] = ra.astype(bf16)
        acc[...] += _dot((ra * ra).astype(bf16), w2_ref[...])

        @pl.when(f == nf - 1)
        def _():
            o_ref[...] = x_ref[...] + acc[...]

    return pl.pallas_call(
        body, name="ffn", grid=(s // tm, nf),
        in_specs=[pl.BlockSpec((tm, D_MODEL), lambda i, f: (i, 0)), pl.BlockSpec((1, D_MODEL), lambda i, f: (0, 0)),
                  pl.BlockSpec((D_MODEL, tf), lambda i, f: (0, f)), pl.BlockSpec((tf, D_MODEL), lambda i, f: (f, 0))],
        out_specs=[pl.BlockSpec((tm, D_MODEL), lambda i, f: (i, 0)), pl.BlockSpec((tm, D_MODEL), lambda i, f: (i, 0)),
                   pl.BlockSpec((tm, tf), lambda i, f: (i, f))],
        out_shape=[SDS((s, D_MODEL), f32), SDS((s, D_MODEL), bf16), SDS((s, D_FF), bf16)],
        scratch_shapes=[pltpu.VMEM((tm, D_MODEL), bf16), pltpu.VMEM((tm, D_MODEL), f32)], compiler_params=_cp("parallel", "arbitrary"),
    )(x1, g, w1, w2)


def _final(x, target, g, tm=512):
    s = x.shape[0]

    def body(x_ref, t_ref, g_ref, loss_ref, dx_ref, dxb_ref, dg_ref):
        first = pl.program_id(0) == 0
        gv = g_ref[...]
        r, xh = _rms_stats(x_ref[...])
        e = xh * gv - t_ref[...]
        tile_loss = 0.5 * jnp.sum(jnp.mean(e * e, axis=-1, keepdims=True), axis=0, keepdims=True)
        dx, dg = _rms_bwd(e * (1.0 / D_MODEL), xh, r, gv)
        dx_ref[...] = dx
        dxb_ref[...] = dx.astype(bf16)
        _acc_rows(dg_ref, first, dg)
        _acc_rows(loss_ref, first, jnp.broadcast_to(tile_loss, (1, LANE)))

    row = pl.BlockSpec((tm, D_MODEL), lambda i: (i, 0))
    return pl.pallas_call(
        body, name="final_loss", grid=(s // tm,), in_specs=[row, row, pl.BlockSpec((1, D_MODEL), lambda i: (0, 0))],
        out_specs=[pl.BlockSpec((1, LANE), lambda i: (0, 0)), row, row, pl.BlockSpec((1, D_MODEL), lambda i: (0, 0))],
        out_shape=[SDS((1, LANE), f32), SDS((s, D_MODEL), f32), SDS((s, D_MODEL), bf16), SDS((1, D_MODEL), f32)],
        compiler_params=_cp("arbitrary"),
    )(x, target, g)


def _ffn_bwd(dx2, dx2b, x1, g, ra, w1, w2, tm=512, tf=2048):
    s = x1.shape[0]
    nf = D_FF // tf

    def body(dx_ref, dxb_ref, x_ref, g_ref, ra_ref, w1_ref, w2_ref, da_ref, dx1_ref, dx1b_ref, dg_ref, acc):
        i, f = pl.program_id(0), pl.program_id(1)

        @pl.when(f == 0)
        def _():
            acc[...] = jnp.zeros_like(acc)

        da = (_dot_nt(dxb_ref[...], w2_ref[...]) * (2.0 * ra_ref[...].astype(f32))).astype(bf16)
        da_ref[...] = da
        acc[...] += _dot_nt(da, w1_ref[...])

        @pl.when(f == nf - 1)
        def _():
            r, xh = _rms_stats(x_ref[...])
            dx, dg = _rms_bwd(acc[...], xh, r, g_ref[...])
            dx = dx + dx_ref[...]
            dx1_ref[...] = dx
            dx1b_ref[...] = dx.astype(bf16)
            _acc_rows(dg_ref, i == 0, dg)

    row = pl.BlockSpec((tm, D_MODEL), lambda i, f: (i, 0))
    col = pl.BlockSpec((tm, tf), lambda i, f: (i, f))
    return pl.pallas_call(
        body, name="ffn_bwd", grid=(s // tm, nf),
        in_specs=[row, row, row, pl.BlockSpec((1, D_MODEL), lambda i, f: (0, 0)), col,
                  pl.BlockSpec((D_MODEL, tf), lambda i, f: (0, f)), pl.BlockSpec((tf, D_MODEL), lambda i, f: (f, 0))],
        out_specs=[col, row, row, pl.BlockSpec((1, D_MODEL), lambda i, f: (0, 0))],
        out_shape=[SDS((s, D_FF), bf16), SDS((s, D_MODEL), f32), SDS((s, D_MODEL), bf16), SDS((1, D_MODEL), f32)],
        scratch_shapes=[pltpu.VMEM((tm, D_MODEL), f32)], compiler_params=_cp("arbitrary", "arbitrary"),
    )(dx2, dx2b, x1, g, ra, w1, w2)


def _mm_tn(a, b, nb, m, n, tm, tn, name="mm_tn", rows=None, row0=0, into=None, square_a=False):
    s = a.shape[0]
    mi, nj = m // tm, n // tn
    rows = m if rows is None else rows
    blk0 = row0 // tm

    def body(a_ref, b_ref, *rest):
        av = a_ref[...]
        if square_a:
            av = av.astype(f32)
            av = (av * av).astype(bf16)
        rest[-1][...] = _dot_tn(av, b_ref[...]).astype(bf16)

    extra = {} if into is None else dict(input_output_aliases={2: 0})
    return pl.pallas_call(
        body, name=name, grid=(nb, mi, nj),
        in_specs=[pl.BlockSpec((s, tm), lambda k, i, j: (0, k * mi + i)), pl.BlockSpec((s, tn), lambda k, i, j: (0, k * nj + j))]
        + ([] if into is None else [pl.BlockSpec(memory_space=pl.ANY)]),
        out_specs=pl.BlockSpec((None, tm, tn), lambda k, i, j: (k, blk0 + i, j)), out_shape=SDS((nb, rows, n), bf16),
        compiler_params=_cp("parallel", "parallel", "parallel"), **extra,
    )(a, b, *([] if into is None else [into]))


def _mm_tn_slabs(a, b, nb, m, nblk, rel, width, tm=512, name="mm_tn_slabs"):
    s = a.shape[0]
    n = b.shape[1] // nb
    ng, mi, nw = n // nblk, m // tm, len(rel)

    def body(a_ref, b_ref, o_ref):
        full = _dot_tn(a_ref[...], b_ref[...])
        for r, start in enumerate(rel):
            o_ref[r] = full[:, start:start + width].astype(bf16)

    return pl.pallas_call(
        body, name=name, grid=(nb, ng, mi),
        in_specs=[pl.BlockSpec((s, tm), lambda k, g, i: (0, k * mi + i)), pl.BlockSpec((s, nblk), lambda k, g, i: (0, k * ng + g))],
        out_specs=pl.BlockSpec((nw, None, tm, width), lambda k, g, i: (g, k, i, 0)), out_shape=SDS((ng * nw, nb, m, width), bf16),
        compiler_params=_cp("parallel", "parallel", "parallel"),
    )(a, b)


def _merge_bwd(dx1b, wo, y, pg, wb, after, tm=512):
    s = dx1b.shape[0]

    def body(dx_ref, wo_ref, y_ref, g_ref, wb_ref, after_ref, dy_ref, dg_ref, dz_ref):
        del after_ref
        dm = _dot_nt(dx_ref[...], wo_ref[...])
        for n in range(3):
            cols = slice(D_MODEL * n, D_MODEL * (n + 1))
            gate = _sigmoid(g_ref[:, cols].astype(f32))
            t = dm * gate
            dy = t.astype(bf16)
            dy_ref[:, cols] = dy
            dg_ref[:, cols] = (t * y_ref[:, cols].astype(f32) * (1.0 - gate)).astype(bf16)
            dz_ref[:, BRANCH * n:BRANCH * (n + 1)] = _dot_nt(dy, wb_ref[n]).astype(bf16)

    wide = pl.BlockSpec((tm, 3 * D_MODEL), lambda i: (i, 0))
    return pl.pallas_call(
        body, name="merge_bwd", grid=(s // tm,),
        in_specs=[pl.BlockSpec((tm, D_MODEL), lambda i: (i, 0)), pl.BlockSpec((D_MODEL, D_MODEL), lambda i: (0, 0)), wide, wide,
                  pl.BlockSpec((3, BRANCH, D_MODEL), lambda i: (0, 0, 0)), pl.BlockSpec(memory_space=pl.ANY)],
        out_specs=[wide, wide, pl.BlockSpec((tm, 3 * BRANCH), lambda i: (i, 0))],
        out_shape=[SDS((s, 3 * D_MODEL), bf16), SDS((s, 3 * D_MODEL), bf16), SDS((s, 3 * BRANCH), bf16)],
        compiler_params=_cp("parallel"),
    )(dx1b, wo, y, pg, wb, after)


def _mixer_bwd(p, dz, opre, states, lb, gout, wconv, lng, lnb, wsg, bsg_t):
    s = p.shape[0]
    tt = MIX_TILE
    nt = s // tt
    nch = tt // HGRN_CHUNK
    rev = lambda i: nt - 1 - i

    def body(q_ref, fp_ref, iv_ref, go_ref, bg_ref, cg_ref, xc_ref, u_ref, v_ref, cgp_ref, xcp_ref, dz_ref, opre_ref, st_ref,
             lb_ref, gout_ref, wconv_ref, lng_ref, lnb_ref, wsg_ref, bsg_ref,
             dp_ref, vec_ref, dwsg_ref, dbsg_ref, dst_scr, zbuf, dybuf, dbsg_acc):
        i = pl.program_id(0)

        @pl.when(i == 0)
        def _():
            dst_scr[...] = jnp.zeros_like(dst_scr)
            dybuf[tt:tt + 8, :] = jnp.zeros((8, BRANCH), f32)
            vec_ref[...] = jnp.zeros_like(vec_ref)
            dwsg_ref[...] = jnp.zeros_like(dwsg_ref)
            dbsg_acc[...] = jnp.zeros_like(dbsg_acc)

        lbv = lb_ref[...]
        gout_v = gout_ref[...]
        causal = _tri(HGRN_CHUNK)
        tri = causal.astype(f32)
        tri_up = _tri(HGRN_CHUNK, upper=True).astype(f32)
        last_row = lax.broadcasted_iota(jnp.int32, (HGRN_CHUNK, 1), 0) == HGRN_CHUNK - 1
        lb_live = (lbv > LB_FLOOR).astype(f32)
        dlb = jnp.zeros((1, BRANCH), f32)
        dgout = jnp.zeros((1, BRANCH), f32)
        for c in reversed(range(nch)):
            rows = slice(HGRN_CHUNK * c, HGRN_CHUNK * (c + 1))
            q_c, fp = q_ref[rows, :], fp_ref[rows, :]
            sq_c = _sigmoid(q_c)
            sfp_c = _sigmoid(fp)
            logf, snf_c, kk = _hgrn_gates(fp, lbv)
            invf_c = jnp.exp(-logf)
            doa = dz_ref[rows, 0:BRANCH].astype(f32)
            o = opre_ref[rows, :]
            sgo = _sigmoid(go_ref[rows, :])
            d_o, dgo, dg_c = [], [], []
            for h in range(HEADS):
                sl = slice(HEAD_DIM * h, HEAD_DIM * (h + 1))
                r, oh = _rms_stats(o[:, sl])
                dgo.append(doa[:, sl] * oh * gout_v[:, sl] * sgo[:, sl] * (1.0 - sgo[:, sl]))
                dx, dg = _rms_bwd(doa[:, sl] * sgo[:, sl], oh, r, gout_v[:, sl])
                d_o.append(dx)
                dg_c.append(dg)
            dp_ref[rows, 3 * BRANCH:4 * BRANCH] = jnp.concatenate(dgo, axis=1).astype(bf16)
            dgout = dgout + jnp.concatenate(dg_c, axis=1)
            dob = jnp.concatenate(d_o, axis=1).astype(bf16)
            b = _dot_exact(tri, logf)
            bl = jnp.sum(jnp.where(last_row, b, 0.0), axis=0, keepdims=True)
            eb, enb, edl, ebl = jnp.exp(b), jnp.exp(-b), jnp.exp(bl - b), jnp.exp(bl)
            qbf, kbf, kdf = q_c * sq_c * eb, kk * enb, kk * edl
            qb, kb, kd = qbf.astype(bf16), kbf.astype(bf16), kdf.astype(bf16)
            vc = iv_ref[rows, :].astype(bf16)
            dv, dqb, dkb, dkd, debl = [], [], [], [], []
            for h in range(HEADS):
                sl = slice(HEAD_DIM * h, HEAD_DIM * (h + 1))
                st = st_ref[c, h]
                dst = dst_scr[h]
                stb, dstb = st.astype(bf16), dst.astype(bf16)
                a = jnp.where(causal, _dot_nt(qb[:, sl], kb[:, sl]), 0.0).astype(bf16)
                da = jnp.where(causal, _dot_nt(dob[:, sl], vc[:, sl]), 0.0).astype(bf16)
                dv.append(_dot_tn(a, dob[:, sl]) + _dot_nt(kd[:, sl], dstb))
                dqb.append(_dot(dob[:, sl], stb) + _dot(da, kb[:, sl]))
                dkb.append(_dot_tn(da, qb[:, sl]))
                dkd.append(_dot(vc[:, sl], dstb))
                debl.append(jnp.sum(st * dst, axis=0, keepdims=True))
                dst_scr[h] = _dot_tn(dob[:, sl], qb[:, sl]) + dst * ebl[:, sl]
            dv, dqb, dkb, dkd = (jnp.concatenate(t, axis=1) for t in (dv, dqb, dkb, dkd))
            debl = jnp.concatenate(debl, axis=1)
            t_kd = dkd * kdf
            dbl = ebl * debl + jnp.sum(t_kd, axis=0, keepdims=True)
            db = dqb * qbf - dkb * kbf - t_kd + jnp.where(last_row, dbl, 0.0)
            dkk = dkb * enb + dkd * edl
            dlc = _dot_exact(tri_up, db)
            slope = (1.0 - lbv) * sfp_c * snf_c
            dp_ref[rows, 0:BRANCH] = (dqb * eb * sq_c * (1.0 + q_c * (1.0 - sq_c))).astype(bf16)
            dp_ref[rows, BRANCH:2 * BRANCH] = (slope * (dlc * invf_c - dkk)).astype(bf16)
            dp_ref[rows, 2 * BRANCH:3 * BRANCH] = dv.astype(bf16)
            dlb = dlb + jnp.sum(dlc * (lb_live - sfp_c) * invf_c - dkk * snf_c, axis=0, keepdims=True)
        vec_ref[0:1, :] += dlb
        vec_ref[1:2, :] += dgout

        dob_ = dz_ref[:, BRANCH:2 * BRANCH].astype(f32)
        bg, cg, xc = bg_ref[...], cg_ref[...], xc_ref[...]
        zc = cg * xc
        zbuf[0:8, :] = jnp.where(i < nt - 1, cgp_ref[...] * xcp_ref[...], 0.0)
        zbuf[8:8 + tt, :] = zc
        w0, w1, w2 = wconv_ref[0:1, :], wconv_ref[1:2, :], wconv_ref[2:3, :]
        y = w0 * zbuf[pl.ds(6, tt), :] + w1 * zbuf[pl.ds(7, tt), :] + w2 * zc
        dy = dob_ * bg
        dybuf[0:tt, :] = dy
        dy1, dy2 = dybuf[pl.ds(1, tt), :], dybuf[pl.ds(2, tt), :]
        dzc = w2 * dy + w1 * dy1 + w0 * dy2
        dp_ref[:, 4 * BRANCH:5 * BRANCH] = (dob_ * y).astype(bf16)
        dp_ref[:, 5 * BRANCH:6 * BRANCH] = (dzc * xc).astype(bf16)
        dp_ref[:, 6 * BRANCH:7 * BRANCH] = (dzc * cg).astype(bf16)
        vec_ref[4:5, :] += jnp.sum(zc * dy2, axis=0, keepdims=True)
        vec_ref[5:6, :] += jnp.sum(zc * dy1, axis=0, keepdims=True)
        vec_ref[6:7, :] += jnp.sum(zc * dy, axis=0, keepdims=True)
        dybuf[tt:tt + 8, :] = dybuf[0:8, :]

        lng_v, lnb_v = lng_ref[...], lnb_ref[...]
        low = _tri(SG_CHUNK)
        wms = [jnp.where(low, wsg_ref[g], 0.0).astype(bf16) for g in range(SG_GROUPS)]
        dlng = jnp.zeros((1, BRANCH), f32)
        dlnb = jnp.zeros((1, BRANCH), f32)
        for cc in range(tt // SG_CHUNK):
            rows = slice(SG_CHUNK * cc, SG_CHUNK * (cc + 1))
            doc = dz_ref[rows, 2 * BRANCH:3 * BRANCH].astype(f32)
            u_raw, v_raw = u_ref[rows, :], v_ref[rows, :]
            ug = _gelu(u_raw)
            vg = _gelu(v_raw)
            vcen = vg - jnp.mean(vg, axis=-1, keepdims=True)
            rstd = lax.rsqrt(jnp.mean(vcen * vcen, axis=-1, keepdims=True) + LN_EPS)
            vhat = vcen * rstd
            vn = (vhat * lng_v + lnb_v).astype(bf16)
            dvn = []
            for g in range(SG_GROUPS):
                sl = slice(LANE * g, LANE * (g + 1))
                sv = _dot(wms[g], vn[:, sl]) + bsg_ref[:, g:g + 1]
                dp_ref[rows, 7 * BRANCH + LANE * g:7 * BRANCH + LANE * (g + 1)] = (doc[:, sl] * sv * _gelu_grad(u_raw[:, sl])).astype(bf16)
                dsv = doc[:, sl] * ug[:, sl]
                dsvb = dsv.astype(bf16)
                dbsg_acc[:, sl] += dsv
                dwsg_ref[g] += jnp.where(low, _dot_nt(dsvb, vn[:, sl]), 0.0)
                dvn.append(_dot_tn(wms[g], dsvb))
            dvn = jnp.concatenate(dvn, axis=1)
            dlng = dlng + jnp.sum(dvn * vhat, axis=0, keepdims=True)
            dlnb = dlnb + jnp.sum(dvn, axis=0, keepdims=True)
            dvh = dvn * lng_v
            dvg = rstd * (dvh - jnp.mean(dvh, axis=-1, keepdims=True) - vhat * jnp.mean(dvh * vhat, axis=-1, keepdims=True))
            dp_ref[rows, 8 * BRANCH:9 * BRANCH] = (dvg * _gelu_grad(v_raw)).astype(bf16)
        vec_ref[2:3, :] += dlng
        vec_ref[3:4, :] += dlnb

        @pl.when(i == nt - 1)
        def _():
            for g in range(SG_GROUPS):
                dbsg_ref[:, g:g + 1] = jnp.sum(dbsg_acc[:, LANE * g:LANE * (g + 1)], axis=1, keepdims=True)

    full = lambda shape: pl.BlockSpec(shape, lambda i: (0,) * len(shape))
    tail = lambda c: pl.BlockSpec((8, BRANCH), lambda i: (jnp.maximum(rev(i) * (tt // 8) - 1, 0), c))
    return pl.pallas_call(
        body, name="mixer_bwd", grid=(nt,),
        in_specs=_p_specs(tt, range(9), rev) + [tail(5), tail(6), pl.BlockSpec((tt, 3 * BRANCH), lambda i: (rev(i), 0)),
                                                pl.BlockSpec((tt, BRANCH), lambda i: (rev(i), 0)),
                                                pl.BlockSpec((nch, HEADS, HEAD_DIM, HEAD_DIM), lambda i: (rev(i), 0, 0, 0)),
                                                full((1, BRANCH)), full((1, BRANCH)), full((3, BRANCH)), full((1, BRANCH)), full((1, BRANCH)),
                                                full((SG_GROUPS, SG_CHUNK, SG_CHUNK)), full((SG_CHUNK, SG_GROUPS))],
        out_specs=[pl.BlockSpec((tt, 9 * BRANCH), lambda i: (rev(i), 0)), full((8, BRANCH)), full((SG_GROUPS, SG_CHUNK, SG_CHUNK)),
                   full((SG_CHUNK, SG_GROUPS))],
        out_shape=[SDS((s, 9 * BRANCH), bf16), SDS((8, BRANCH), f32), SDS((SG_GROUPS, SG_CHUNK, SG_CHUNK), f32), SDS((SG_CHUNK, SG_GROUPS), f32)],
        scratch_shapes=[pltpu.VMEM((HEADS, HEAD_DIM, HEAD_DIM), f32), pltpu.VMEM((tt + 8, BRANCH), f32), pltpu.VMEM((tt + 8, BRANCH), f32),
                        pltpu.VMEM((SG_CHUNK, BRANCH), f32)],
        compiler_params=_cp("arbitrary"),
    )(*([p] * 11), dz, opre, states, lb, gout, wconv, lng, lnb, wsg, bsg_t)


def _dh_bwd(dpm, dpg, w_t, x, dx1, g, after, tm=1024, tk=1536):
    s = x.shape[0]
    km = dpm.shape[1] // tk
    nk = km + dpg.shape[1] // tk

    def body(dpm_ref, dpg_ref, w_ref, x_ref, dx1_ref, g_ref, after_ref, dx_ref, dxb_ref, dg_ref, acc):
        del after_ref
        i, k = pl.program_id(0), pl.program_id(1)

        @pl.when(k == 0)
        def _():
            acc[...] = jnp.zeros_like(acc)

        @pl.when(k < km)
        def _():
            acc[...] += _dot(dpm_ref[...], w_ref[...])

        @pl.when(k >= km)
        def _():
            acc[...] += _dot(dpg_ref[...], w_ref[...])

        @pl.when(k == nk - 1)
        def _():
            r, xh = _rms_stats(x_ref[...])
            dx, dg = _rms_bwd(acc[...], xh, r, g_ref[...])
            dx = dx + dx1_ref[...]
            dx_ref[...] = dx
            dxb_ref[...] = dx.astype(bf16)
            _acc_rows(dg_ref, i == 0, dg)

    row = pl.BlockSpec((tm, D_MODEL), lambda i, k: (i, 0))
    vec = pl.BlockSpec((1, D_MODEL), lambda i, k: (0, 0))
    return pl.pallas_call(
        body, name="dh_bwd", grid=(s // tm, nk),
        in_specs=[pl.BlockSpec((tm, tk), lambda i, k: (i, jnp.minimum(k, km - 1))),
                  pl.BlockSpec((tm, tk), lambda i, k: (i, jnp.maximum(k - km, 0))),
                  pl.BlockSpec((tk, D_MODEL), lambda i, k: (k, 0)), row, row, vec, pl.BlockSpec(memory_space=pl.ANY)],
        out_specs=[row, row, vec], out_shape=[SDS((s, D_MODEL), f32), SDS((s, D_MODEL), bf16), SDS((1, D_MODEL), f32)],
        scratch_shapes=[pltpu.VMEM((tm, D_MODEL), f32)], compiler_params=_cp("arbitrary", "arbitrary"),
    )(dpm, dpg, w_t, x, dx1, g, after)


def _layer_fwd(x, weight, sm):
    p, pg, h = _rms_mm(x, sm["g_mix"], weight("w_in", x))
    z, opre, states = _mixer_fwd(p, sm["lb"], sm["g_out"], sm["w_conv"], sm["ln_g"], sm["ln_b"], sm["w_sg"], sm["b_sg_t"])
    y, merged, x1 = _branch_gate(z, weight("w_branch", z), pg, x, weight("w_o", z))
    x2, h2, ra = _ffn(x1, sm["g_ffn"], weight("w_ff1", x1), weight("w_ff2", x1))
    saved = dict(x=x, p=p, pg=pg, h=h, z=z, opre=opre, states=states, y=y, merged=merged, x1=x1, h2=h2, ra=ra)
    return x2, saved


def _layer_bwd(dx2, dx2b, sv, w, sm, between, before_end):
    nchip = N_DEV // 2
    by_chip = lambda g: g.reshape((nchip, 2) + g.shape[1:])
    da, dx1, dx1b, dg_ffn = _ffn_bwd(dx2, dx2b, sv["x1"], sm["g_ffn"], sv["ra"], w["w_ff1"], w["w_ff2"])
    g_ff2 = by_chip(_mm_tn(sv["ra"], dx2b, 1, D_FF, D_MODEL, 512, 1024, name="dw_ff2", square_a=True)[0]
                    .reshape(N_DEV, D_FF // N_DEV, D_MODEL))
    g_ff1 = by_chip(_mm_tn_slabs(sv["h2"], da, 1, D_MODEL, D_FF // 2, [i * (D_FF // N_DEV) for i in range(nchip)], D_FF // N_DEV,
                                 name="dw_ff1")[:, 0])
    g_o = by_chip(_mm_tn(sv["merged"], dx1b, 1, D_MODEL, D_MODEL, 512, 1024, name="dw_o")[0].reshape(N_DEV, D_MODEL // N_DEV, D_MODEL))
    dy, dpg, dz = _merge_bwd(dx1b, w["w_o"], sv["y"], sv["pg"], w["w_branch"], between(dx1))
    g_branch = by_chip(_mm_tn_slabs(sv["z"], dy, 3, BRANCH, D_MODEL, [i * (D_MODEL // N_DEV) for i in range(N_DEV)], D_MODEL // N_DEV,
                                    name="dw_branch"))
    g_in = _mm_tn(dpg, sv["h"], 1, 3 * D_MODEL, D_MODEL, 768, 1024, name="dw_in_gates", rows=N_COLS, row0=GATE_COL0)
    dpm, vecs, dwsg, dbsg_t = _mixer_bwd(sv["p"], dz, sv["opre"], sv["states"], sm["lb"], sm["g_out"], sm["w_conv"],
                                         sm["ln_g"], sm["ln_b"], sm["w_sg"], sm["b_sg_t"])
    g_in = _mm_tn(dpm, sv["h"], 1, GATE_COL0, D_MODEL, 768, 1024, name="dw_in_mixers", rows=N_COLS, into=g_in)
    g_in = by_chip(g_in[0].reshape(N_DEV, SHARD_IN, D_MODEL))
    big = dict(w_in=g_in, w_branch=g_branch, w_o=g_o, w_ff1=g_ff1, w_ff2=g_ff2)
    dx, dxb, dg_mix = _dh_bwd(dpm, dpg, w["w_in"], sv["x"], dx1, sm["g_mix"], before_end(big))
    small = dict(g_mix=dg_mix, g_ffn=dg_ffn, vecs=vecs, w_sg=dwsg, b_sg_t=dbsg_t, dx1=dx1)
    return dx, dxb, big, small


BIG = ("w_in", "w_branch", "w_o", "w_ff1", "w_ff2")
ANY = pl.BlockSpec(memory_space=pl.ANY)


def _place():
    return lax.axis_index("x"), lax.axis_index("y"), lax.axis_index("c")


def _al(v, m):
    return pl.multiple_of(v * m, m)


def _shard_of(refs, dev, which=range(len(BIG))):
    out = []
    for ref, t in zip(refs, which):
        by_cols = BIG[t] in ("w_branch", "w_ff1")
        n = ref.shape[-1 if by_cols else 0] // N_DEV
        part = pl.ds(_al(dev, n), n)
        out.append(ref.at[(slice(None),) * (len(ref.shape) - 1) + (part,)] if by_cols else ref.at[part])
    return out


def _gather_out_shapes(shards):
    s_in, s_b, s_o, s_1, s_2 = (shards[n] for n in BIG)
    return [SDS((s_in.shape[1] * N_DEV, s_in.shape[2]), bf16), SDS(s_b.shape[1:3] + (s_b.shape[3] * N_DEV,), bf16),
            SDS((s_o.shape[1] * N_DEV, s_o.shape[2]), bf16), SDS((s_1.shape[1], s_1.shape[2] * N_DEV), bf16),
            SDS((s_2.shape[1] * N_DEV, s_2.shape[2]), bf16)]


def _seq_all_gather_layer(layer, which, n_early, shard_refs, out_shapes, tag=""):
    nt = len(which)
    outs = [jax.empty_ref(sh, memory_space=pltpu.MemorySpace.HBM) for sh in out_shapes]
    early, late = tuple(range(n_early)), tuple(range(n_early, nt))

    @pl.kernel(mesh=plsc.ScalarSubcoreMesh(axis_name="seq", num_cores=1), name=f"seq_all_gather_l{layer}{tag}",
               scratch_types=(pltpu.SemaphoreType.DMA((9,)), pltpu.SemaphoreType.DMA((9,))),
               compiler_params=pltpu.CompilerParams(collective_id=1))
    def launch(send_sems, recv_sems):
        x, y, c = _place()
        me, sibling = (x, y, c), (x, y, 1 - c)
        first, second, diag = _ici_route(x, y, c)
        _handshake([sibling, first, second])
        mine = [r.at[layer] for r in shard_refs]

        def copies(k, blk, to, src=None, part=range(nt)):
            dst = _shard_of(outs, 4 * blk[0] + 2 * blk[1] + blk[2], which)
            src = dst if src is None else src
            return [pltpu.make_async_remote_copy(src_ref=src[t], dst_ref=dst[t], send_sem=send_sems.at[k], recv_sem=recv_sems.at[k],
                                                 device_id=to, device_id_type=MESH) for t in part]

        def start(cps):
            for cp in cps:
                cp.start()
            return cps

        def landed(cps):
            for cp in cps:
                cp.wait_recv()

        sent = start(copies(0, me, sibling, src=mine) + copies(1, me, first, src=mine, part=early)
                     + copies(2, me, first, src=mine, part=late) + copies(3, me, second, src=mine))
        landed(copies(1, first, me, part=early))
        sent += start(copies(4, first, second, part=early) + copies(6, first, sibling, part=early))
        landed(copies(2, first, me, part=late))
        sent += start(copies(5, first, second, part=late) + copies(6, first, sibling, part=late))
        landed(copies(3, second, me))
        sent += start(copies(7, second, sibling))
        landed(copies(4, diag, me, part=early) + copies(5, diag, me, part=late))
        sent += start(copies(8, diag, sibling))
        other = lambda p: (p[0], p[1], 1 - c)
        landed(copies(0, sibling, me) + copies(6, other(second), me) + copies(7, other(first), me) + copies(8, other(diag), me))
        for cp in sent:
            cp.wait_send()

    launch()
    return [o[...] for o in outs]


def _ici_route(x, y, c):
    return (x ^ (1 - c), y ^ c, c), (x ^ c, y ^ (1 - c), c), (1 - x, 1 - y, c)


def _place_own(where, which, shards, gathered, after):
    nt = len(which)

    def body(where_ref, *refs):
        del where_ref
        for src, dst in zip(refs[:nt], refs[2 * nt + 1:]):
            dst[...] = src[...]

    in_specs, out_specs = [], []
    for t, sh in zip(which, shards):
        blk = sh.shape[1:]
        in_specs.append(pl.BlockSpec((None,) + blk, functools.partial(lambda nd, i, wh: (wh[0],) + (0,) * nd, len(blk))))
        by_cols = BIG[t] in ("w_branch", "w_ff1")
        out_specs.append(pl.BlockSpec(blk, functools.partial(
            lambda nd, cols, i, wh: (0,) * (nd - 1) + (wh[1],) if cols else (wh[1],) + (0,) * (nd - 1), len(blk), by_cols)))
    return pl.pallas_call(
        body, name="place_own", out_shape=[SDS(g.shape, g.dtype) for g in gathered],
        input_output_aliases={1 + nt + i: i for i in range(nt)}, compiler_params=_cp("arbitrary"),
        grid_spec=pltpu.PrefetchScalarGridSpec(num_scalar_prefetch=1, grid=(1,), in_specs=in_specs + [ANY] * (nt + 1), out_specs=out_specs),
    )(where, *shards, *gathered, after)


def _handshake(peers):
    barrier = pltpu.get_barrier_semaphore()
    for p in peers:
        pl.semaphore_signal(barrier, inc=1, device_id=p, device_id_type=MESH)
    pl.semaphore_wait(barrier, len(peers))


def _seq_exchange_on_chip(grads):
    nt, nchip = len(BIG), N_DEV // 2
    g_refs = [jax.new_ref(g, memory_space=pltpu.MemorySpace.HBM) for g in grads]
    outs = [jax.empty_ref(SDS((nchip,) + g.shape[2:], bf16), memory_space=pltpu.MemorySpace.HBM) for g in grads]

    @pl.kernel(mesh=plsc.ScalarSubcoreMesh(axis_name="seq", num_cores=1), name="seq_rs_on_chip",
               scratch_types=(pltpu.SemaphoreType.DMA((nchip,)), pltpu.SemaphoreType.DMA((nchip,))),
               compiler_params=pltpu.CompilerParams(collective_id=2))
    def launch(send_sems, recv_sems):
        x, y, c = _place()
        sibling = (x, y, 1 - c)
        _handshake([sibling])
        remote = [pltpu.make_async_remote_copy(src_ref=g_refs[t].at[j, 1 - c], dst_ref=outs[t].at[j], send_sem=send_sems.at[j],
                                               recv_sem=recv_sems.at[j], device_id=sibling, device_id_type=MESH)
                  for j in range(nchip) for t in range(nt)]
        for cp in remote:
            cp.start()
        for cp in remote:
            cp.wait_recv()
        for cp in remote:
            cp.wait_send()

    launch()
    return [o[...] for o in outs], [g[...] for g in g_refs]


def _seq_exchange_between_chips(sums):
    nt = len(BIG)
    s_refs = [jax.new_ref(a, memory_space=pltpu.MemorySpace.HBM) for a in sums]
    outs = [jax.empty_ref(SDS((3,) + a.shape[1:], bf16), memory_space=pltpu.MemorySpace.HBM) for a in sums]
    transit = [jax.empty_ref(SDS(a.shape[1:], bf16), memory_space=pltpu.MemorySpace.HBM) for a in sums]

    early, late = (0,), tuple(range(1, nt))

    @pl.kernel(mesh=plsc.ScalarSubcoreMesh(axis_name="seq", num_cores=1), name="seq_rs_between_chips",
               scratch_types=(pltpu.SemaphoreType.DMA((6,)), pltpu.SemaphoreType.DMA((6,))),
               compiler_params=pltpu.CompilerParams(collective_id=3))
    def launch(send_sems, recv_sems):
        x, y, c = _place()
        first, second, diag = _ici_route(x, y, c)
        _handshake([first, second])

        def copies(k, src, dst, to, part=range(nt)):
            return [pltpu.make_async_remote_copy(src_ref=src(t), dst_ref=dst(t), send_sem=send_sems.at[k], recv_sem=recv_sems.at[k],
                                                 device_id=to, device_id_type=MESH) for t in part]

        chip_of = lambda p: 2 * p[0] + p[1]
        for_diag = lambda t: s_refs[t].at[chip_of(diag)]
        through = lambda t: transit[t]
        last = lambda t: outs[t].at[2]
        direct = (copies(0, lambda t: s_refs[t].at[chip_of(first)], lambda t: outs[t].at[0], first)
                  + copies(1, lambda t: s_refs[t].at[chip_of(second)], lambda t: outs[t].at[1], second))
        via = [copies(2, for_diag, through, first, early), copies(3, for_diag, through, first, late)]
        passed = [copies(4, through, last, second, early), copies(5, through, last, second, late)]
        for cp in via[0] + direct + via[1]:
            cp.start()
        for arrived, onward in zip(via, passed):
            for cp in arrived:
                cp.wait_recv()
            for cp in onward:
                cp.start()
        sent = direct + via[0] + via[1] + passed[0] + passed[1]
        for cp in direct + passed[0] + passed[1]:
            cp.wait_recv()
        for cp in sent:
            cp.wait_send()

    launch()
    return [o[...] for o in outs], [a[...] for a in s_refs]


def _chip_sums(core, mine, other, after, steps=2):
    nt, nchip = len(mine), mine[0].shape[0]
    m4 = [a.reshape(nchip, 2, -1, a.shape[-1]) for a in mine]
    o3 = [a.reshape(nchip, -1, a.shape[-1]) for a in other]

    def body(c_ref, *refs):
        del c_ref
        for a_ref, b_ref, o_ref in zip(refs[:nt], refs[nt:2 * nt], refs[2 * nt + 1:]):
            o_ref[...] = (a_ref[...].astype(f32) + b_ref[...].astype(f32)).astype(bf16)

    tiles = [(a.shape[1] // steps, a.shape[2]) for a in o3]
    blks = [pl.BlockSpec((None,) + t, lambda j, i, c_ref: (j, i, 0)) for t in tiles]
    outs = pl.pallas_call(
        body, name="chip_sums", out_shape=[SDS(a.shape, bf16) for a in o3], compiler_params=_cp("parallel", "parallel"),
        grid_spec=pltpu.PrefetchScalarGridSpec(
            num_scalar_prefetch=1, grid=(nchip, steps),
            in_specs=[pl.BlockSpec((None, None) + t, lambda j, i, c_ref: (j, c_ref[0], i, 0)) for t in tiles] + blks + [ANY],
            out_specs=blks),
    )(core, *m4, *o3, after)
    return [o.reshape(a.shape) for o, a in zip(outs, other)]


def _all_reduce_rows(pack):
    rows = pack.shape[0]
    blk = rows // N_DEV

    def body(in_ref, out_ref, land, send1, recv1, send2, recv2):
        x, y, c = _place()
        me = 4 * x + 2 * y + c
        others = [(px, py, pc) for px in range(2) for py in range(2) for pc in range(2)]

        def is_me(p):
            return jnp.logical_and(jnp.logical_and(p[0] == x, p[1] == y), p[2] == c)

        land[me] = in_ref[pl.ds(_al(me, blk), blk), :]
        for d, p in enumerate(others):
            @pl.when(jnp.logical_not(is_me(p)))
            def _():
                pltpu.make_async_remote_copy(src_ref=in_ref.at[pl.ds(d * blk, blk), :], dst_ref=land.at[me], send_sem=send1.at[d],
                                             recv_sem=recv1.at[me], device_id=p, device_id_type=MESH).start()
        for d, p in enumerate(others):
            @pl.when(jnp.logical_not(is_me(p)))
            def _():
                cp = pltpu.make_async_remote_copy(src_ref=in_ref.at[pl.ds(d * blk, blk), :], dst_ref=land.at[d], send_sem=send1.at[d],
                                                  recv_sem=recv1.at[d], device_id=p, device_id_type=MESH)
                cp.wait_recv()
                cp.wait_send()
        total = land[0]
        for d in range(1, N_DEV):
            total = total + land[d]
        out_ref[pl.ds(_al(me, blk), blk), :] = total
        for d, p in enumerate(others):
            @pl.when(jnp.logical_not(is_me(p)))
            def _():
                mine = out_ref.at[pl.ds(_al(me, blk), blk), :]
                pltpu.make_async_remote_copy(src_ref=mine, dst_ref=mine, send_sem=send2.at[d], recv_sem=recv2.at[me],
                                             device_id=p, device_id_type=MESH).start()
        for d, p in enumerate(others):
            @pl.when(jnp.logical_not(is_me(p)))
            def _():
                theirs = out_ref.at[pl.ds(d * blk, blk), :]
                cp = pltpu.make_async_remote_copy(src_ref=theirs, dst_ref=theirs, send_sem=send2.at[d], recv_sem=recv2.at[d],
                                                  device_id=p, device_id_type=MESH)
                cp.wait_recv()
                cp.wait_send()

    vm = pl.BlockSpec(memory_space=pltpu.VMEM)
    return pl.pallas_call(
        body, name="all_reduce_rows", in_specs=[vm], out_specs=vm, out_shape=SDS((rows, LANE), f32),
        scratch_shapes=[pltpu.VMEM((N_DEV, blk, LANE), f32)] + [pltpu.SemaphoreType.DMA((N_DEV,))] * 4,
        compiler_params=pltpu.CompilerParams(vmem_limit_bytes=VMEM_LIMIT),
    )(pack)


def _lower_bounds_fwd(lower):
    def body(l_ref, o_ref):
        sm = _layer_softmax(l_ref)
        run = jnp.zeros_like(sm[0])
        for l in range(DEPTH):
            o_ref[l:l + 1, :] = run
            if l + 1 < DEPTH:
                run = run + sm[l + 1]

    return pl.pallas_call(body, name="lower_bounds_fwd", out_shape=SDS(lower.shape, f32))(lower)


def _layer_softmax(l_ref):
    rows = [l_ref[l:l + 1, :] for l in range(DEPTH)]
    top = functools.reduce(jnp.maximum, rows)
    e = [jnp.exp(r - top) for r in rows]
    tot = functools.reduce(lambda a, b: a + b, e)
    return [v / tot for v in e]


def _lower_bounds_bwd(lower, dlbs):
    def body(l_ref, d_ref, o_ref):
        sm = _layer_softmax(l_ref)
        dsm = [None] * DEPTH
        run = jnp.zeros_like(sm[0])
        dsm[0] = run
        for l in reversed(range(1, DEPTH)):
            run = run + d_ref[l:l + 1, :]
            dsm[l] = run
        inner = functools.reduce(lambda a, b: a + b, [sm[l] * dsm[l] for l in range(DEPTH)])
        for l in range(DEPTH):
            o_ref[l:l + 1, :] = sm[l] * (dsm[l] - inner)

    return pl.pallas_call(body, name="lower_bounds_bwd", out_shape=SDS(lower.shape, f32))(lower, dlbs)


_ADAM_C1 = 1.0 - ADAM_B1 ** ADAM_STEP
_ADAM_C2 = 1.0 - ADAM_B2 ** ADAM_STEP


def _adamw(w, g, m, v):
    m = ADAM_B1 * m + (1.0 - ADAM_B1) * g
    v = ADAM_B2 * v + (1.0 - ADAM_B2) * (g * g)
    delta = -ADAM_LR * ((m / _ADAM_C1) / (jnp.sqrt(v / _ADAM_C2) + ADAM_EPS) + ADAM_WD * w)
    return delta, m, v


def _adam_big(where, names, w, m, v, sums, landed, outs, after, steps=4):
    nt = len(names)
    three = lambda a: a.reshape(a.shape[0], -1, a.shape[-1])
    w3, m3, v3 = ([three(d[n]) for n in names] for d in (w, m, v))
    outs3 = [three(a) for n in names for a in outs[n]]
    sums3 = [three(a) for a in sums]
    land3 = [three(a) for a in landed]

    def body(where_ref, *refs):
        del where_ref
        o_refs = refs[5 * nt + 4 * nt + 1:]
        for t in range(nt):
            w_ref, m_ref, v_ref, sum_ref, land_ref = (refs[q * nt + t] for q in range(5))
            g = sum_ref[...].astype(f32)
            for k in range(3):
                g = g + land_ref[k].astype(f32)
            delta, nm, nv = _adamw(w_ref[...], g, m_ref[...], v_ref[...])
            for o_ref, val in zip(o_refs[4 * t:4 * t + 4], (g, delta, nm, nv)):
                o_ref[...] = val

    tiles = [(a.shape[1] // steps, a.shape[2]) for a in w3]
    own = [pl.BlockSpec((None,) + t, lambda i, wh: (wh[0], i, 0)) for t in tiles]
    res = pl.pallas_call(
        body, name="adam_big", out_shape=[SDS(a.shape, f32) for a in outs3],
        input_output_aliases={1 + 5 * nt + i: i for i in range(4 * nt)}, compiler_params=_cp("parallel"),
        grid_spec=pltpu.PrefetchScalarGridSpec(
            num_scalar_prefetch=1, grid=(steps,),
            in_specs=own * 3 + [pl.BlockSpec((None,) + t, lambda i, wh: (wh[1], i, 0)) for t in tiles]
            + [pl.BlockSpec((3,) + t, lambda i, wh: (0, i, 0)) for t in tiles] + [ANY] * (4 * nt + 1),
            out_specs=[s for s in own for _ in range(4)]),
    )(where, *w3, *m3, *v3, *sums3, *land3, *outs3, after)
    return {n: [o.reshape(w[n].shape) for o in res[4 * t:4 * t + 4]] for t, n in enumerate(names)}


def _touch(a, after):
    a2 = a.reshape(-1, a.shape[-1])

    def body(a_ref, after_ref, o_ref):
        del after_ref
        o_ref[...] = a_ref[0:8, :].astype(f32)

    return pl.pallas_call(
        body, name="touch", grid=(1,), in_specs=[pl.BlockSpec((16, LANE), lambda i: (0, 0)), ANY],
        out_specs=pl.BlockSpec((8, LANE), lambda i: (0, 0)), out_shape=SDS((8, LANE), f32),
    )(a2, after)


def _adam_rows(w, g, m, v):
    def body(w_ref, g_ref, m_ref, v_ref, d_ref, nm_ref, nv_ref):
        delta, nm, nv = _adamw(w_ref[...], g_ref[...], m_ref[...], v_ref[...])
        d_ref[...] = delta
        nm_ref[...] = nm
        nv_ref[...] = nv

    return pl.pallas_call(body, name="adam_rows", out_shape=[SDS(w.shape, f32)] * 3)(w, g, m, v)


SMALL = ("g_mix", "lower_bounds", "g_hgrn_out", "w_conv", "sg_ln_g", "sg_ln_b", "w_sg", "b_sg", "g_ffn", "g_final")
WEIGHTS = ("w_in", "g_mix", "lower_bounds", "g_hgrn_out", "w_conv", "sg_ln_g", "sg_ln_b", "w_sg", "b_sg", "w_branch", "w_o", "g_ffn",
           "w_ff1", "w_ff2", "g_final")


def _pack_rows(arrays, multiple):
    flat = jnp.concatenate([a.reshape(-1) for a in arrays])
    rows = -(-flat.shape[0] // (LANE * multiple)) * multiple
    return jnp.pad(flat, (0, rows * LANE - flat.shape[0])).reshape(rows, LANE)


def _unpack_rows(pack, like):
    flat = pack.reshape(-1)
    out, at = [], 0
    for a in like:
        out.append(flat[at:at + a.size].reshape(a.shape))
        at += a.size
    return out


def kernel(x, w_in, g_mix, lower_bounds, g_hgrn_out, w_conv, sg_ln_g, sg_ln_b, w_sg, b_sg, w_branch, w_o, g_ffn, w_ff1, w_ff2, g_final, loss_target, m_w_in, m_g_mix, m_lower_bounds, m_g_hgrn_out, m_w_conv, m_sg_ln_g, m_sg_ln_b, m_w_sg, m_b_sg, m_w_branch, m_w_o, m_g_ffn, m_w_ff1, m_w_ff2, m_g_final, v_w_in, v_g_mix, v_lower_bounds, v_g_hgrn_out, v_w_conv, v_sg_ln_g, v_sg_ln_b, v_w_sg, v_b_sg, v_w_branch, v_w_o, v_g_ffn, v_w_ff1, v_w_ff2, v_g_final):
    weights = dict(w_in=w_in, g_mix=g_mix, lower_bounds=lower_bounds, g_hgrn_out=g_hgrn_out, w_conv=w_conv, sg_ln_g=sg_ln_g,
                   sg_ln_b=sg_ln_b, w_sg=w_sg, b_sg=b_sg, w_branch=w_branch, w_o=w_o, g_ffn=g_ffn, w_ff1=w_ff1, w_ff2=w_ff2, g_final=g_final)
    mom1 = dict(w_in=m_w_in, g_mix=m_g_mix, lower_bounds=m_lower_bounds, g_hgrn_out=m_g_hgrn_out, w_conv=m_w_conv, sg_ln_g=m_sg_ln_g,
                sg_ln_b=m_sg_ln_b, w_sg=m_w_sg, b_sg=m_b_sg, w_branch=m_w_branch, w_o=m_w_o, g_ffn=m_g_ffn, w_ff1=m_w_ff1, w_ff2=m_w_ff2,
                g_final=m_g_final)
    mom2 = dict(w_in=v_w_in, g_mix=v_g_mix, lower_bounds=v_lower_bounds, g_hgrn_out=v_g_hgrn_out, w_conv=v_w_conv, sg_ln_g=v_sg_ln_g,
                sg_ln_b=v_sg_ln_b, w_sg=v_w_sg, b_sg=v_b_sg, w_branch=v_w_branch, w_o=v_w_o, g_ffn=v_g_ffn, w_ff1=v_w_ff1, w_ff2=v_w_ff2,
                g_final=v_g_final)
    xi, yi, ci = _place()
    dev = 4 * xi + 2 * yi + ci
    conv_cols = w_conv.shape[-1]

    for d in (weights, mom1, mom2):
        d["w_in"] = jnp.swapaxes(d["w_in"], 1, 2)
    shards = {n: weights[n].astype(bf16) for n in BIG}

    conv_place = lax.dynamic_update_slice(jnp.zeros((DEPTH, 3, BRANCH), f32), w_conv, (0, 0, dev * conv_cols))
    (w_conv_full,) = _unpack_rows(_all_reduce_rows(_pack_rows([conv_place], 8 * N_DEV)), [conv_place])
    lbs = _lower_bounds_fwd(lower_bounds)

    def small_of(l):
        return dict(g_mix=g_mix[l][None], lb=lbs[l][None], g_out=g_hgrn_out[l][None], w_conv=w_conv_full[l], ln_g=sg_ln_g[l][None],
                    ln_b=sg_ln_b[l][None], w_sg=w_sg[l], b_sg_t=b_sg[l].T, g_ffn=g_ffn[l][None])

    act = x[0]
    full, saved = [], []
    shard_refs = [jax.new_ref(shards[n], memory_space=pltpu.MemorySpace.HBM) for n in BIG]
    shapes = _gather_out_shapes(shards)
    every = tuple(range(len(BIG)))
    groups = [(0, (0,), 1, "a"), (0, (1, 2, 3, 4), 2, "b"), (1, (0,), 1, "a"), (1, (3, 4, 1, 2), 1, "b")]
    groups += [(l, every, 1, "") for l in range(2, DEPTH)]
    arrived = {}
    for l, which, n_early, tag in groups:
        got = _seq_all_gather_layer(l, which, n_early, [shard_refs[t] for t in which], [shapes[t] for t in which], tag)
        arrived.update({(l, BIG[t]): (which, got) for t in which})

    for l in range(DEPTH):
        full.append({})

        def weight(name, after, l=l):
            if name not in full[l]:
                which, got = arrived[(l, name)]
                where = jnp.stack([jnp.int32(l), dev.astype(jnp.int32)])
                full[l].update(zip([BIG[t] for t in which], _place_own(where, which, [shards[BIG[t]] for t in which], got, after)))
            return full[l][name]

        act, sv = _layer_fwd(act, weight, small_of(l))
        saved.append(sv)
    loss_row, dx, dxb, dg_final = _final(act, loss_target[0], g_final[None])

    core = ci.astype(jnp.int32)[None]
    big_out = {n: [lax.empty(weights[n].shape, f32) for _ in range(4)] for n in BIG}
    small_grads = [None] * DEPTH

    def chip_sums(stage, after):
        l, received, mine = stage
        sums = _chip_sums(core, mine, received, after)
        placed.append(sums[BIG.index("w_o")])
        landed, sums = _seq_exchange_between_chips(sums)
        return l, sums, landed

    def adam_layer(stage, after):
        l, sums, landed = stage
        where = jnp.stack([jnp.int32(l), (2 * xi + yi).astype(jnp.int32)])
        big_out.update(_adam_big(where, BIG, weights, mom1, mom2, sums, landed, big_out, after))

    above = None
    placed = []
    for l in reversed(range(DEPTH)):
        summed = []

        def between(dx1):
            if above is None:
                return dx1
            summed.append(chip_sums(above, dx1))
            return placed[-1]

        def before_end(big):
            return _touch(summed[0][2][BIG.index("w_o")], big["w_in"]) if summed else big["w_in"]

        dx, dxb, big, small_grads[l] = _layer_bwd(dx, dxb, saved[l], full[l], small_of(l), between, before_end)
        if summed:
            adam_layer(summed[0], dx)
        above = (l, *_seq_exchange_on_chip([big[n] for n in BIG]))

    stack = lambda f: jnp.stack([f(small_grads[l]) for l in range(DEPTH)])
    d_lower = _lower_bounds_bwd(lower_bounds, stack(lambda s: s["vecs"][0]))
    local_small = dict(g_mix=stack(lambda s: s["g_mix"][0]), lower_bounds=d_lower, g_hgrn_out=stack(lambda s: s["vecs"][1]),
                       w_conv=stack(lambda s: s["vecs"][4:7]), sg_ln_g=stack(lambda s: s["vecs"][2]), sg_ln_b=stack(lambda s: s["vecs"][3]),
                       w_sg=stack(lambda s: s["w_sg"]), b_sg=stack(lambda s: s["b_sg_t"].T), g_ffn=stack(lambda s: s["g_ffn"][0]),
                       g_final=dg_final[0])
    order = [local_small[n] for n in SMALL] + [loss_row]
    *reduced, loss_sum = _unpack_rows(_all_reduce_rows(_pack_rows(order, 8 * N_DEV)), order)
    loss = loss_sum[0, 0]
    grads = dict(zip(SMALL, reduced))
    grads["w_conv"] = lax.dynamic_slice(grads["w_conv"], (0, 0, dev * conv_cols), (DEPTH, 3, conv_cols))

    deltas, new_m, new_v = {}, {}, {}
    packs = [_pack_rows([d[n] for n in SMALL], 8) for d in (weights, grads, mom1, mom2)]
    like = [weights[n] for n in SMALL]
    small_out = _adam_rows(*packs)
    for out, pack in zip((deltas, new_m, new_v), small_out):
        out.update(zip(SMALL, _unpack_rows(pack, like)))
    adam_layer(chip_sums(above, dx), small_out[0])
    for n in BIG:
        grads[n], deltas[n], new_m[n], new_v[n] = (jnp.swapaxes(a, 1, 2) if n == "w_in" else a for a in big_out[n])

    return (loss, dx[None], *[grads[n] for n in WEIGHTS], *[deltas[n] for n in WEIGHTS], *[new_m[n] for n in WEIGHTS],
            *[new_v[n] for n in WEIGHTS])
```

```python
import functools

import jax
import jax.numpy as jnp
from jax import lax
from jax.experimental import pallas as pl
from jax.experimental.pallas import tpu as pltpu
from jax.experimental.pallas import tpu_sc as plsc

f32 = jnp.float32
bf16 = jnp.bfloat16
SDS = jax.ShapeDtypeStruct
MESH = pl.DeviceIdType.MESH

D_MODEL = 1024
BRANCH = 512
N_COLS = 7680
D_FF = 4096
DEPTH = 4
HEADS = 4
HEAD_DIM = 128
HGRN_CHUNK = 64
SG_CHUNK = 128
SG_GROUPS = 4
NORM_EPS = 1e-6
LN_EPS = 1e-5
LB_FLOOR = 1e-30
N_DEV = 8
SHARD_IN = N_COLS // N_DEV
LANE = 128
GATE_COL0 = 9 * BRANCH

ADAM_LR = 0.001
ADAM_B1 = 0.9
ADAM_B2 = 0.999
ADAM_EPS = 1e-08
ADAM_WD = 0.01
ADAM_STEP = 10

MIX_TILE = 256
VMEM_LIMIT = 56 * 1024 * 1024


def _cp(*sem):
    return pltpu.CompilerParams(dimension_semantics=sem or None, vmem_limit_bytes=VMEM_LIMIT)


def _dot(a, b):
    return jnp.dot(a, b, preferred_element_type=f32)


def _dot_nt(a, b):
    return lax.dot_general(a, b, (((1,), (1,)), ((), ())), preferred_element_type=f32)


def _dot_tn(a, b):
    return lax.dot_general(a, b, (((0,), (0,)), ((), ())), preferred_element_type=f32)


def _dot_exact(ones, b):
    hi = b.astype(bf16)
    rest = b - hi.astype(f32)
    mid = rest.astype(bf16)
    low = (rest - mid.astype(f32)).astype(bf16)
    ones = ones.astype(bf16)
    return _dot(ones, hi) + _dot(ones, mid) + _dot(ones, low)


def _sigmoid(x):
    return jax.nn.sigmoid(x)


_GELU_C = 0.7978845608028654
_GELU_A = 0.044715


def _gelu(x):
    return 0.5 * x * (1.0 + jnp.tanh(_GELU_C * (x + _GELU_A * x * x * x)))


def _gelu_grad(x):
    x2 = x * x
    t = jnp.tanh(_GELU_C * (x + _GELU_A * x * x2))
    return 0.5 * (1.0 + t) + 0.5 * x * (1.0 - t * t) * _GELU_C * (1.0 + 3.0 * _GELU_A * x2)


def _rms_stats(x):
    r = lax.rsqrt(jnp.mean(x * x, axis=-1, keepdims=True) + NORM_EPS)
    return r, x * r


def _rms_bwd(dh, xh, r, g):
    dg = jnp.sum(dh * xh, axis=0, keepdims=True)
    dxn = dh * g
    dx = r * (dxn - xh * jnp.mean(dxn * xh, axis=-1, keepdims=True))
    return dx, dg


def _tri(n, upper=False):
    r = lax.broadcasted_iota(jnp.int32, (n, n), 0)
    c = lax.broadcasted_iota(jnp.int32, (n, n), 1)
    return (c >= r) if upper else (c <= r)


def _acc_rows(ref, first, val):
    @pl.when(first)
    def _():
        ref[...] = val

    @pl.when(jnp.logical_not(first))
    def _():
        ref[...] += val


def _rms_mm(x, g, w_t, tm=1024, tn=1536):
    s, n = x.shape[0], w_t.shape[0]
    jm = GATE_COL0 // tn

    def body(x_ref, g_ref, w_ref, pm_ref, pg_ref, h_ref, hs):
        j = pl.program_id(1)

        @pl.when(j == 0)
        def _():
            _, xh = _rms_stats(x_ref[...])
            hv = (xh * g_ref[...]).astype(bf16)
            hs[...] = hv
            h_ref[...] = hv

        res = _dot_nt(hs[...], w_ref[...])

        @pl.when(j < jm)
        def _():
            pm_ref[...] = res

        @pl.when(j >= jm)
        def _():
            pg_ref[...] = res.astype(bf16)

    return pl.pallas_call(
        body, name="rms_mm", grid=(s // tm, n // tn),
        in_specs=[pl.BlockSpec((tm, D_MODEL), lambda i, j: (i, 0)), pl.BlockSpec((1, D_MODEL), lambda i, j: (0, 0)),
                  pl.BlockSpec((tn, D_MODEL), lambda i, j: (j, 0))],
        out_specs=[pl.BlockSpec((tm, tn), lambda i, j: (i, jnp.minimum(j, jm - 1))),
                   pl.BlockSpec((tm, tn), lambda i, j: (i, jnp.maximum(j - jm, 0))), pl.BlockSpec((tm, D_MODEL), lambda i, j: (i, 0))],
        out_shape=[SDS((s, GATE_COL0), f32), SDS((s, n - GATE_COL0), bf16), SDS((s, D_MODEL), bf16)],
        scratch_shapes=[pltpu.VMEM((tm, D_MODEL), bf16)], compiler_params=_cp("parallel", "arbitrary"),
    )(x, g, w_t)


def _hgrn_gates(fp, lb):
    logf = jnp.logaddexp(jnp.log(jnp.maximum(lb, LB_FLOOR)), jnp.log1p(-lb) + jax.nn.log_sigmoid(fp))
    snf = _sigmoid(-fp)
    return logf, snf, (1.0 - lb) * snf


def _p_specs(tile, cols, row_map):
    return [pl.BlockSpec((tile, BRANCH), functools.partial(lambda c, i: (row_map(i), c), c)) for c in cols]


def _mixer_fwd(p, lb, gout, wconv, lng, lnb, wsg, bsg_t):
    s = p.shape[0]
    tt = MIX_TILE
    nch = tt // HGRN_CHUNK

    def body(q_ref, fp_ref, iv_ref, go_ref, bg_ref, cg_ref, xc_ref, u_ref, v_ref, lb_ref, gout_ref, wconv_ref, lng_ref,
             lnb_ref, wsg_ref, bsg_ref, z_ref, opre_ref, st_ref, st_scr, zbuf):
        @pl.when(pl.program_id(0) == 0)
        def _():
            st_scr[...] = jnp.zeros_like(st_scr)
            zbuf[0:8, :] = jnp.zeros((8, BRANCH), f32)

        lbv = lb_ref[...]
        gout_v = gout_ref[...]
        causal = _tri(HGRN_CHUNK)
        tri = causal.astype(f32)
        last_row = lax.broadcasted_iota(jnp.int32, (HGRN_CHUNK, 1), 0) == HGRN_CHUNK - 1
        for c in range(nch):
            rows = slice(HGRN_CHUNK * c, HGRN_CHUNK * (c + 1))
            q_raw = q_ref[rows, :]
            qs = q_raw * _sigmoid(q_raw)
            logf, _, kk = _hgrn_gates(fp_ref[rows, :], lbv)
            b = _dot_exact(tri, logf)
            bl = jnp.sum(jnp.where(last_row, b, 0.0), axis=0, keepdims=True)
            qb = (qs * jnp.exp(b)).astype(bf16)
            kb = (kk * jnp.exp(-b)).astype(bf16)
            kd = (kk * jnp.exp(bl - b)).astype(bf16)
            ebl = jnp.exp(bl)
            vc = iv_ref[rows, :].astype(bf16)
            gate = _sigmoid(go_ref[rows, :])
            for h in range(HEADS):
                sl = slice(HEAD_DIM * h, HEAD_DIM * (h + 1))
                st = st_scr[h]
                st_ref[c, h] = st
                a = jnp.where(causal, _dot_nt(qb[:, sl], kb[:, sl]), 0.0)
                o = _dot(a.astype(bf16), vc[:, sl]) + _dot_nt(qb[:, sl], st.astype(bf16))
                opre_ref[rows, sl] = o
                st_scr[h] = st * ebl[:, sl] + _dot_tn(vc[:, sl], kd[:, sl])
                _, oh = _rms_stats(o)
                z_ref[rows, sl] = (oh * gout_v[:, sl] * gate[:, sl]).astype(bf16)

        zc = cg_ref[...] * xc_ref[...]
        zbuf[8:8 + tt, :] = zc
        y = wconv_ref[0:1, :] * zbuf[pl.ds(6, tt), :] + wconv_ref[1:2, :] * zbuf[pl.ds(7, tt), :] + wconv_ref[2:3, :] * zc
        z_ref[:, BRANCH:2 * BRANCH] = (bg_ref[...] * y).astype(bf16)
        zbuf[0:8, :] = zbuf[tt:tt + 8, :]

        lng_v, lnb_v = lng_ref[...], lnb_ref[...]
        low = _tri(SG_CHUNK)
        wms = [jnp.where(low, wsg_ref[g], 0.0).astype(bf16) for g in range(SG_GROUPS)]
        for cc in range(tt // SG_CHUNK):
            rows = slice(SG_CHUNK * cc, SG_CHUNK * (cc + 1))
            ug = _gelu(u_ref[rows, :])
            vg = _gelu(v_ref[rows, :])
            vcen = vg - jnp.mean(vg, axis=-1, keepdims=True)
            rstd = lax.rsqrt(jnp.mean(vcen * vcen, axis=-1, keepdims=True) + LN_EPS)
            vn = (vcen * rstd * lng_v + lnb_v).astype(bf16)
            for g in range(SG_GROUPS):
                sl = slice(LANE * g, LANE * (g + 1))
                sv = _dot(wms[g], vn[:, sl]) + bsg_ref[:, g:g + 1]
                z_ref[rows, 2 * BRANCH + LANE * g:2 * BRANCH + LANE * (g + 1)] = (ug[:, sl] * sv).astype(bf16)

    full = lambda shape: pl.BlockSpec(shape, lambda i: (0,) * len(shape))
    return pl.pallas_call(
        body, name="mixer_fwd", grid=(s // tt,),
        in_specs=_p_specs(tt, range(9), lambda i: i) + [full((1, BRANCH)), full((1, BRANCH)), full((3, BRANCH)), full((1, BRANCH)),
                                                        full((1, BRANCH)), full((SG_GROUPS, SG_CHUNK, SG_CHUNK)), full((SG_CHUNK, SG_GROUPS))],
        out_specs=[pl.BlockSpec((tt, 3 * BRANCH), lambda i: (i, 0)), pl.BlockSpec((tt, BRANCH), lambda i: (i, 0)),
                   pl.BlockSpec((nch, HEADS, HEAD_DIM, HEAD_DIM), lambda i: (i, 0, 0, 0))],
        out_shape=[SDS((s, 3 * BRANCH), bf16), SDS((s, BRANCH), f32), SDS((s // HGRN_CHUNK, HEADS, HEAD_DIM, HEAD_DIM), f32)],
        scratch_shapes=[pltpu.VMEM((HEADS, HEAD_DIM, HEAD_DIM), f32), pltpu.VMEM((tt + 8, BRANCH), f32)],
        compiler_params=_cp("arbitrary"),
    )(*([p] * 9), lb, gout, wconv, lng, lnb, wsg, bsg_t)


def _branch_gate(z, wb, pg, x, wo, tm=512):
    s = z.shape[0]

    def body(z_ref, wb_ref, g_ref, x_ref, wo_ref, y_ref, m_ref, x1_ref):
        acc = None
        for n in range(3):
            cols = slice(D_MODEL * n, D_MODEL * (n + 1))
            yn = _dot(z_ref[:, BRANCH * n:BRANCH * (n + 1)], wb_ref[n])
            y_ref[:, cols] = yn.astype(bf16)
            t = _sigmoid(g_ref[:, cols].astype(f32)) * yn
            acc = t if acc is None else acc + t
        merged = acc.astype(bf16)
        m_ref[...] = merged
        x1_ref[...] = x_ref[...] + _dot(merged, wo_ref[...])

    row = pl.BlockSpec((tm, D_MODEL), lambda i: (i, 0))
    wide = pl.BlockSpec((tm, 3 * D_MODEL), lambda i: (i, 0))
    return pl.pallas_call(
        body, name="branch_gate", grid=(s // tm,),
        in_specs=[pl.BlockSpec((tm, 3 * BRANCH), lambda i: (i, 0)), pl.BlockSpec((3, BRANCH, D_MODEL), lambda i: (0, 0, 0)), wide, row,
                  pl.BlockSpec((D_MODEL, D_MODEL), lambda i: (0, 0))],
        out_specs=[wide, row, row],
        out_shape=[SDS((s, 3 * D_MODEL), bf16), SDS((s, D_MODEL), bf16), SDS((s, D_MODEL), f32)], compiler_params=_cp("parallel"),
    )(z, wb, pg, x, wo)


def _ffn(x1, g, w1, w2, tm=1024, tf=1024):
    s = x1.shape[0]
    nf = D_FF // tf

    def body(x_ref, g_ref, w1_ref, w2_ref, o_ref, h_ref, ra_ref, hs, acc):
        f = pl.program_id(1)

        @pl.when(f == 0)
        def _():
            _, xh = _rms_stats(x_ref[...])
            hv = (xh * g_ref[...]).astype(bf16)
            hs[...] = hv
            h_ref[...] = hv
            acc[...] = jnp.zeros_like(acc)

        ra = jnp.maximum(_dot(hs[...], w1_ref[...]), 0.0)
        ra_ref[...] = ra.astype(bf16)
        acc[...] += _dot((ra * ra).astype(bf16), w2_ref[...])

        @pl.when(f == nf - 1)
        def _():
            o_ref[...] = x_ref[...] + acc[...]

    return pl.pallas_call(
        body, name="ffn", grid=(s // tm, nf),
        in_specs=[pl.BlockSpec((tm, D_MODEL), lambda i, f: (i, 0)), pl.BlockSpec((1, D_MODEL), lambda i, f: (0, 0)),
                  pl.BlockSpec((D_MODEL, tf), lambda i, f: (0, f)), pl.BlockSpec((tf, D_MODEL), lambda i, f: (f, 0))],
        out_specs=[pl.BlockSpec((tm, D_MODEL), lambda i, f: (i, 0)), pl.BlockSpec((tm, D_MODEL), lambda i, f: (i, 0)),
                   pl.BlockSpec((tm, tf), lambda i, f: (i, f))],
        out_shape=[SDS((s, D_MODEL), f32), SDS((s, D_MODEL), bf16), SDS((s, D_FF), bf16)],
        scratch_shapes=[pltpu.VMEM((tm, D_MODEL), bf16), pltpu.VMEM((tm, D_MODEL), f32)], compiler_params=_cp("parallel", "arbitrary"),
    )(x1, g, w1, w2)


def _final(x, target, g, tm=512):
    s = x.shape[0]

    def body(x_ref, t_ref, g_ref, loss_ref, dx_ref, dxb_ref, dg_ref):
        first = pl.program_id(0) == 0
        gv = g_ref[...]
        r, xh = _rms_stats(x_ref[...])
        e = xh * gv - t_ref[...]
        tile_loss = 0.5 * jnp.sum(jnp.mean(e * e, axis=-1, keepdims=True), axis=0, keepdims=True)
        dx, dg = _rms_bwd(e * (1.0 / D_MODEL), xh, r, gv)
        dx_ref[...] = dx
        dxb_ref[...] = dx.astype(bf16)
        _acc_rows(dg_ref, first, dg)
        _acc_rows(loss_ref, first, jnp.broadcast_to(tile_loss, (1, LANE)))

    row = pl.BlockSpec((tm, D_MODEL), lambda i: (i, 0))
    return pl.pallas_call(
        body, name="final_loss", grid=(s // tm,), in_specs=[row, row, pl.BlockSpec((1, D_MODEL), lambda i: (0, 0))],
        out_specs=[pl.BlockSpec((1, LANE), lambda i: (0, 0)), row, row, pl.BlockSpec((1, D_MODEL), lambda i: (0, 0))],
        out_shape=[SDS((1, LANE), f32), SDS((s, D_MODEL), f32), SDS((s, D_MODEL), bf16), SDS((1, D_MODEL), f32)],
        compiler_params=_cp("arbitrary"),
    )(x, target, g)


def _ffn_bwd(dx2, dx2b, x1, g, ra, w1, w2, tm=512, tf=2048):
    s = x1.shape[0]
    nf = D_FF // tf

    def body(dx_ref, dxb_ref, x_ref, g_ref, ra_ref, w1_ref, w2_ref, da_ref, dx1_ref, dx1b_ref, dg_ref, acc):
        i, f = pl.program_id(0), pl.program_id(1)

        @pl.when(f == 0)
        def _():
            acc[...] = jnp.zeros_like(acc)

        da = (_dot_nt(dxb_ref[...], w2_ref[...]) * (2.0 * ra_ref[...].astype(f32))).astype(bf16)
        da_ref[...] = da
        acc[...] += _dot_nt(da, w1_ref[...])

        @pl.when(f == nf - 1)
        def _():
            r, xh = _rms_stats(x_ref[...])
            dx, dg = _rms_bwd(acc[...], xh, r, g_ref[...])
            dx = dx + dx_ref[...]
            dx1_ref[...] = dx
            dx1b_ref[...] = dx.astype(bf16)
            _acc_rows(dg_ref, i == 0, dg)

    row = pl.BlockSpec((tm, D_MODEL), lambda i, f: (i, 0))
    col = pl.BlockSpec((tm, tf), lambda i, f: (i, f))
    return pl.pallas_call(
        body, name="ffn_bwd", grid=(s // tm, nf),
        in_specs=[row, row, row, pl.BlockSpec((1, D_MODEL), lambda i, f: (0, 0)), col,
                  pl.BlockSpec((D_MODEL, tf), lambda i, f: (0, f)), pl.BlockSpec((tf, D_MODEL), lambda i, f: (f, 0))],
        out_specs=[col, row, row, pl.BlockSpec((1, D_MODEL), lambda i, f: (0, 0))],
        out_shape=[SDS((s, D_FF), bf16), SDS((s, D_MODEL), f32), SDS((s, D_MODEL), bf16), SDS((1, D_MODEL), f32)],
        scratch_shapes=[pltpu.VMEM((tm, D_MODEL), f32)], compiler_params=_cp("arbitrary", "arbitrary"),
    )(dx2, dx2b, x1, g, ra, w1, w2)


def _mm_tn(a, b, nb, m, n, tm, tn, name="mm_tn", rows=None, row0=0, into=None, square_a=False):
    s = a.shape[0]
    mi, nj = m // tm, n // tn
    rows = m if rows is None else rows
    blk0 = row0 // tm

    def body(a_ref, b_ref, *rest):
        av = a_ref[...]
        if square_a:
            av = av.astype(f32)
            av = (av * av).astype(bf16)
        rest[-1][...] = _dot_tn(av, b_ref[...]).astype(bf16)

    extra = {} if into is None else dict(input_output_aliases={2: 0})
    return pl.pallas_call(
        body, name=name, grid=(nb, mi, nj),
        in_specs=[pl.BlockSpec((s, tm), lambda k, i, j: (0, k * mi + i)), pl.BlockSpec((s, tn), lambda k, i, j: (0, k * nj + j))]
        + ([] if into is None else [pl.BlockSpec(memory_space=pl.ANY)]),
        out_specs=pl.BlockSpec((None, tm, tn), lambda k, i, j: (k, blk0 + i, j)), out_shape=SDS((nb, rows, n), bf16),
        compiler_params=_cp("parallel", "parallel", "parallel"), **extra,
    )(a, b, *([] if into is None else [into]))


def _mm_tn_slabs(a, b, nb, m, nblk, rel, width, tm=512, name="mm_tn_slabs"):
    s = a.shape[0]
    n = b.shape[1] // nb
    ng, mi, nw = n // nblk, m // tm, len(rel)

    def body(a_ref, b_ref, o_ref):
        full = _dot_tn(a_ref[...], b_ref[...])
        for r, start in enumerate(rel):
            o_ref[r] = full[:, start:start + width].astype(bf16)

    return pl.pallas_call(
        body, name=name, grid=(nb, ng, mi),
        in_specs=[pl.BlockSpec((s, tm), lambda k, g, i: (0, k * mi + i)), pl.BlockSpec((s, nblk), lambda k, g, i: (0, k * ng + g))],
        out_specs=pl.BlockSpec((nw, None, tm, width), lambda k, g, i: (g, k, i, 0)), out_shape=SDS((ng * nw, nb, m, width), bf16),
        compiler_params=_cp("parallel", "parallel", "parallel"),
    )(a, b)


def _merge_bwd(dx1b, wo, y, pg, wb, after, tm=512):
    s = dx1b.shape[0]

    def body(dx_ref, wo_ref, y_ref, g_ref, wb_ref, after_ref, dy_ref, dg_ref, dz_ref):
        del after_ref
        dm = _dot_nt(dx_ref[...], wo_ref[...])
        for n in range(3):
            cols = slice(D_MODEL * n, D_MODEL * (n + 1))
            gate = _sigmoid(g_ref[:, cols].astype(f32))
            t = dm * gate
            dy = t.astype(bf16)
            dy_ref[:, cols] = dy
            dg_ref[:, cols] = (t * y_ref[:, cols].astype(f32) * (1.0 - gate)).astype(bf16)
            dz_ref[:, BRANCH * n:BRANCH * (n + 1)] = _dot_nt(dy, wb_ref[n]).astype(bf16)

    wide = pl.BlockSpec((tm, 3 * D_MODEL), lambda i: (i, 0))
    return pl.pallas_call(
        body, name="merge_bwd", grid=(s // tm,),
        in_specs=[pl.BlockSpec((tm, D_MODEL), lambda i: (i, 0)), pl.BlockSpec((D_MODEL, D_MODEL), lambda i: (0, 0)), wide, wide,
                  pl.BlockSpec((3, BRANCH, D_MODEL), lambda i: (0, 0, 0)), pl.BlockSpec(memory_space=pl.ANY)],
        out_specs=[wide, wide, pl.BlockSpec((tm, 3 * BRANCH), lambda i: (i, 0))],
        out_shape=[SDS((s, 3 * D_MODEL), bf16), SDS((s, 3 * D_MODEL), bf16), SDS((s, 3 * BRANCH), bf16)],
        compiler_params=_cp("parallel"),
    )(dx1b, wo, y, pg, wb, after)


def _mixer_bwd(p, dz, opre, states, lb, gout, wconv, lng, lnb, wsg, bsg_t):
    s = p.shape[0]
    tt = MIX_TILE
    nt = s // tt
    nch = tt // HGRN_CHUNK
    rev = lambda i: nt - 1 - i

    def body(q_ref, fp_ref, iv_ref, go_ref, bg_ref, cg_ref, xc_ref, u_ref, v_ref, cgp_ref, xcp_ref, dz_ref, opre_ref, st_ref,
             lb_ref, gout_ref, wconv_ref, lng_ref, lnb_ref, wsg_ref, bsg_ref,
             dp_ref, vec_ref, dwsg_ref, dbsg_ref, dst_scr, zbuf, dybuf, dbsg_acc):
        i = pl.program_id(0)

        @pl.when(i == 0)
        def _():
            dst_scr[...] = jnp.zeros_like(dst_scr)
            dybuf[tt:tt + 8, :] = jnp.zeros((8, BRANCH), f32)
            vec_ref[...] = jnp.zeros_like(vec_ref)
            dwsg_ref[...] = jnp.zeros_like(dwsg_ref)
            dbsg_acc[...] = jnp.zeros_like(dbsg_acc)

        lbv = lb_ref[...]
        gout_v = gout_ref[...]
        causal = _tri(HGRN_CHUNK)
        tri = causal.astype(f32)
        tri_up = _tri(HGRN_CHUNK, upper=True).astype(f32)
        last_row = lax.broadcasted_iota(jnp.int32, (HGRN_CHUNK, 1), 0) == HGRN_CHUNK - 1
        lb_live = (lbv > LB_FLOOR).astype(f32)
        dlb = jnp.zeros((1, BRANCH), f32)
        dgout = jnp.zeros((1, BRANCH), f32)
        for c in reversed(range(nch)):
            rows = slice(HGRN_CHUNK * c, HGRN_CHUNK * (c + 1))
            q_c, fp = q_ref[rows, :], fp_ref[rows, :]
            sq_c = _sigmoid(q_c)
            sfp_c = _sigmoid(fp)
            logf, snf_c, kk = _hgrn_gates(fp, lbv)
            invf_c = jnp.exp(-logf)
            doa = dz_ref[rows, 0:BRANCH].astype(f32)
            o = opre_ref[rows, :]
            sgo = _sigmoid(go_ref[rows, :])
            d_o, dgo, dg_c = [], [], []
            for h in range(HEADS):
                sl = slice(HEAD_DIM * h, HEAD_DIM * (h + 1))
                r, oh = _rms_stats(o[:, sl])
                dgo.append(doa[:, sl] * oh * gout_v[:, sl] * sgo[:, sl] * (1.0 - sgo[:, sl]))
                dx, dg = _rms_bwd(doa[:, sl] * sgo[:, sl], oh, r, gout_v[:, sl])
                d_o.append(dx)
                dg_c.append(dg)
            dp_ref[rows, 3 * BRANCH:4 * BRANCH] = jnp.concatenate(dgo, axis=1).astype(bf16)
            dgout = dgout + jnp.concatenate(dg_c, axis=1)
            dob = jnp.concatenate(d_o, axis=1).astype(bf16)
            b = _dot_exact(tri, logf)
            bl = jnp.sum(jnp.where(last_row, b, 0.0), axis=0, keepdims=True)
            eb, enb, edl, ebl = jnp.exp(b), jnp.exp(-b), jnp.exp(bl - b), jnp.exp(bl)
            qbf, kbf, kdf = q_c * sq_c * eb, kk * enb, kk * edl
            qb, kb, kd = qbf.astype(bf16), kbf.astype(bf16), kdf.astype(bf16)
            vc = iv_ref[rows, :].astype(bf16)
            dv, dqb, dkb, dkd, debl = [], [], [], [], []
            for h in range(HEADS):
                sl = slice(HEAD_DIM * h, HEAD_DIM * (h + 1))
                st = st_ref[c, h]
                dst = dst_scr[h]
                stb, dstb = st.astype(bf16), dst.astype(bf16)
                a = jnp.where(causal, _dot_nt(qb[:, sl], kb[:, sl]), 0.0).astype(bf16)
                da = jnp.where(causal, _dot_nt(dob[:, sl], vc[:, sl]), 0.0).astype(bf16)
                dv.append(_dot_tn(a, dob[:, sl]) + _dot_nt(kd[:, sl], dstb))
                dqb.append(_dot(dob[:, sl], stb) + _dot(da, kb[:, sl]))
                dkb.append(_dot_tn(da, qb[:, sl]))
                dkd.append(_dot(vc[:, sl], dstb))
                debl.append(jnp.sum(st * dst, axis=0, keepdims=True))
                dst_scr[h] = _dot_tn(dob[:, sl], qb[:, sl]) + dst * ebl[:, sl]
            dv, dqb, dkb, dkd = (jnp.concatenate(t, axis=1) for t in (dv, dqb, dkb, dkd))
            debl = jnp.concatenate(debl, axis=1)
            t_kd = dkd * kdf
            dbl = ebl * debl + jnp.sum(t_kd, axis=0, keepdims=True)
            db = dqb * qbf - dkb * kbf - t_kd + jnp.where(last_row, dbl, 0.0)
            dkk = dkb * enb + dkd * edl
            dlc = _dot_exact(tri_up, db)
            slope = (1.0 - lbv) * sfp_c * snf_c
            dp_ref[rows, 0:BRANCH] = (dqb * eb * sq_c * (1.0 + q_c * (1.0 - sq_c))).astype(bf16)
            dp_ref[rows, BRANCH:2 * BRANCH] = (slope * (dlc * invf_c - dkk)).astype(bf16)
            dp_ref[rows, 2 * BRANCH:3 * BRANCH] = dv.astype(bf16)
            dlb = dlb + jnp.sum(dlc * (lb_live - sfp_c) * invf_c - dkk * snf_c, axis=0, keepdims=True)
        vec_ref[0:1, :] += dlb
        vec_ref[1:2, :] += dgout

        dob_ = dz_ref[:, BRANCH:2 * BRANCH].astype(f32)
        bg, cg, xc = bg_ref[...], cg_ref[...], xc_ref[...]
        zc = cg * xc
        zbuf[0:8, :] = jnp.where(i < nt - 1, cgp_ref[...] * xcp_ref[...], 0.0)
        zbuf[8:8 + tt, :] = zc
        w0, w1, w2 = wconv_ref[0:1, :], wconv_ref[1:2, :], wconv_ref[2:3, :]
        y = w0 * zbuf[pl.ds(6, tt), :] + w1 * zbuf[pl.ds(7, tt), :] + w2 * zc
        dy = dob_ * bg
        dybuf[0:tt, :] = dy
        dy1, dy2 = dybuf[pl.ds(1, tt), :], dybuf[pl.ds(2, tt), :]
        dzc = w2 * dy + w1 * dy1 + w0 * dy2
        dp_ref[:, 4 * BRANCH:5 * BRANCH] = (dob_ * y).astype(bf16)
        dp_ref[:, 5 * BRANCH:6 * BRANCH] = (dzc * xc).astype(bf16)
        dp_ref[:, 6 * BRANCH:7 * BRANCH] = (dzc * cg).astype(bf16)
        vec_ref[4:5, :] += jnp.sum(zc * dy2, axis=0, keepdims=True)
        vec_ref[5:6, :] += jnp.sum(zc * dy1, axis=0, keepdims=True)
        vec_ref[6:7, :] += jnp.sum(zc * dy, axis=0, keepdims=True)
        dybuf[tt:tt + 8, :] = dybuf[0:8, :]

        lng_v, lnb_v = lng_ref[...], lnb_ref[...]
        low = _tri(SG_CHUNK)
        wms = [jnp.where(low, wsg_ref[g], 0.0).astype(bf16) for g in range(SG_GROUPS)]
        dlng = jnp.zeros((1, BRANCH), f32)
        dlnb = jnp.zeros((1, BRANCH), f32)
        for cc in range(tt // SG_CHUNK):
            rows = slice(SG_CHUNK * cc, SG_CHUNK * (cc + 1))
            doc = dz_ref[rows, 2 * BRANCH:3 * BRANCH].astype(f32)
            u_raw, v_raw = u_ref[rows, :], v_ref[rows, :]
            ug = _gelu(u_raw)
            vg = _gelu(v_raw)
            vcen = vg - jnp.mean(vg, axis=-1, keepdims=True)
            rstd = lax.rsqrt(jnp.mean(vcen * vcen, axis=-1, keepdims=True) + LN_EPS)
            vhat = vcen * rstd
            vn = (vhat * lng_v + lnb_v).astype(bf16)
            dvn = []
            for g in range(SG_GROUPS):
                sl = slice(LANE * g, LANE * (g + 1))
                sv = _dot(wms[g], vn[:, sl]) + bsg_ref[:, g:g + 1]
                dp_ref[rows, 7 * BRANCH + LANE * g:7 * BRANCH + LANE * (g + 1)] = (doc[:, sl] * sv * _gelu_grad(u_raw[:, sl])).astype(bf16)
                dsv = doc[:, sl] * ug[:, sl]
                dsvb = dsv.astype(bf16)
                dbsg_acc[:, sl] += dsv
                dwsg_ref[g] += jnp.where(low, _dot_nt(dsvb, vn[:, sl]), 0.0)
                dvn.append(_dot_tn(wms[g], dsvb))
            dvn = jnp.concatenate(dvn, axis=1)
            dlng = dlng + jnp.sum(dvn * vhat, axis=0, keepdims=True)
            dlnb = dlnb + jnp.sum(dvn, axis=0, keepdims=True)
            dvh = dvn * lng_v
            dvg = rstd * (dvh - jnp.mean(dvh, axis=-1, keepdims=True) - vhat * jnp.mean(dvh * vhat, axis=-1, keepdims=True))
            dp_ref[rows, 8 * BRANCH:9 * BRANCH] = (dvg * _gelu_grad(v_raw)).astype(bf16)
        vec_ref[2:3, :] += dlng
        vec_ref[3:4, :] += dlnb

        @pl.when(i == nt - 1)
        def _():
            for g in range(SG_GROUPS):
                dbsg_ref[:, g:g + 1] = jnp.sum(dbsg_acc[:, LANE * g:LANE * (g + 1)], axis=1, keepdims=True)

    full = lambda shape: pl.BlockSpec(shape, lambda i: (0,) * len(shape))
    tail = lambda c: pl.BlockSpec((8, BRANCH), lambda i: (jnp.maximum(rev(i) * (tt // 8) - 1, 0), c))
    return pl.pallas_call(
        body, name="mixer_bwd", grid=(nt,),
        in_specs=_p_specs(tt, range(9), rev) + [tail(5), tail(6), pl.BlockSpec((tt, 3 * BRANCH), lambda i: (rev(i), 0)),
                                                pl.BlockSpec((tt, BRANCH), lambda i: (rev(i), 0)),
                                                pl.BlockSpec((nch, HEADS, HEAD_DIM, HEAD_DIM), lambda i: (rev(i), 0, 0, 0)),
                                                full((1, BRANCH)), full((1, BRANCH)), full((3, BRANCH)), full((1, BRANCH)), full((1, BRANCH)),
                                                full((SG_GROUPS, SG_CHUNK, SG_CHUNK)), full((SG_CHUNK, SG_GROUPS))],
        out_specs=[pl.BlockSpec((tt, 9 * BRANCH), lambda i: (rev(i), 0)), full((8, BRANCH)), full((SG_GROUPS, SG_CHUNK, SG_CHUNK)),
                   full((SG_CHUNK, SG_GROUPS))],
        out_shape=[SDS((s, 9 * BRANCH), bf16), SDS((8, BRANCH), f32), SDS((SG_GROUPS, SG_CHUNK, SG_CHUNK), f32), SDS((SG_CHUNK, SG_GROUPS), f32)],
        scratch_shapes=[pltpu.VMEM((HEADS, HEAD_DIM, HEAD_DIM), f32), pltpu.VMEM((tt + 8, BRANCH), f32), pltpu.VMEM((tt + 8, BRANCH), f32),
                        pltpu.VMEM((SG_CHUNK, BRANCH), f32)],
        compiler_params=_cp("arbitrary"),
    )(*([p] * 11), dz, opre, states, lb, gout, wconv, lng, lnb, wsg, bsg_t)


def _dh_bwd(dpm, dpg, w_t, x, dx1, g, after, tm=1024, tk=1536):
    s = x.shape[0]
    km = dpm.shape[1] // tk
    nk = km + dpg.shape[1] // tk

    def body(dpm_ref, dpg_ref, w_ref, x_ref, dx1_ref, g_ref, after_ref, dx_ref, dxb_ref, dg_ref, acc):
        del after_ref
        i, k = pl.program_id(0), pl.program_id(1)

        @pl.when(k == 0)
        def _():
            acc[...] = jnp.zeros_like(acc)

        @pl.when(k < km)
        def _():
            acc[...] += _dot(dpm_ref[...], w_ref[...])

        @pl.when(k >= km)
        def _():
            acc[...] += _dot(dpg_ref[...], w_ref[...])

        @pl.when(k == nk - 1)
        def _():
            r, xh = _rms_stats(x_ref[...])
            dx, dg = _rms_bwd(acc[...], xh, r, g_ref[...])
            dx = dx + dx1_ref[...]
            dx_ref[...] = dx
            dxb_ref[...] = dx.astype(bf16)
            _acc_rows(dg_ref, i == 0, dg)

    row = pl.BlockSpec((tm, D_MODEL), lambda i, k: (i, 0))
    vec = pl.BlockSpec((1, D_MODEL), lambda i, k: (0, 0))
    return pl.pallas_call(
        body, name="dh_bwd", grid=(s // tm, nk),
        in_specs=[pl.BlockSpec((tm, tk), lambda i, k: (i, jnp.minimum(k, km - 1))),
                  pl.BlockSpec((tm, tk), lambda i, k: (i, jnp.maximum(k - km, 0))),
                  pl.BlockSpec((tk, D_MODEL), lambda i, k: (k, 0)), row, row, vec, pl.BlockSpec(memory_space=pl.ANY)],
        out_specs=[row, row, vec], out_shape=[SDS((s, D_MODEL), f32), SDS((s, D_MODEL), bf16), SDS((1, D_MODEL), f32)],
        scratch_shapes=[pltpu.VMEM((tm, D_MODEL), f32)], compiler_params=_cp("arbitrary", "arbitrary"),
    )(dpm, dpg, w_t, x, dx1, g, after)


def _layer_fwd(x, weight, sm):
    p, pg, h = _rms_mm(x, sm["g_mix"], weight("w_in", x))
    z, opre, states = _mixer_fwd(p, sm["lb"], sm["g_out"], sm["w_conv"], sm["ln_g"], sm["ln_b"], sm["w_sg"], sm["b_sg_t"])
    y, merged, x1 = _branch_gate(z, weight("w_branch", z), pg, x, weight("w_o", z))
    x2, h2, ra = _ffn(x1, sm["g_ffn"], weight("w_ff1", x1), weight("w_ff2", x1))
    saved = dict(x=x, p=p, pg=pg, h=h, z=z, opre=opre, states=states, y=y, merged=merged, x1=x1, h2=h2, ra=ra)
    return x2, saved


def _layer_bwd(dx2, dx2b, sv, w, sm, between, before_end):
    nchip = N_DEV // 2
    by_chip = lambda g: g.reshape((nchip, 2) + g.shape[1:])
    da, dx1, dx1b, dg_ffn = _ffn_bwd(dx2, dx2b, sv["x1"], sm["g_ffn"], sv["ra"], w["w_ff1"], w["w_ff2"])
    g_ff2 = by_chip(_mm_tn(sv["ra"], dx2b, 1, D_FF, D_MODEL, 512, 1024, name="dw_ff2", square_a=True)[0]
                    .reshape(N_DEV, D_FF // N_DEV, D_MODEL))
    g_ff1 = by_chip(_mm_tn_slabs(sv["h2"], da, 1, D_MODEL, D_FF // 2, [i * (D_FF // N_DEV) for i in range(nchip)], D_FF // N_DEV,
                                 name="dw_ff1")[:, 0])
    g_o = by_chip(_mm_tn(sv["merged"], dx1b, 1, D_MODEL, D_MODEL, 512, 1024, name="dw_o")[0].reshape(N_DEV, D_MODEL // N_DEV, D_MODEL))
    dy, dpg, dz = _merge_bwd(dx1b, w["w_o"], sv["y"], sv["pg"], w["w_branch"], between(dx1))
    g_branch = by_chip(_mm_tn_slabs(sv["z"], dy, 3, BRANCH, D_MODEL, [i * (D_MODEL // N_DEV) for i in range(N_DEV)], D_MODEL // N_DEV,
                                    name="dw_branch"))
    g_in = _mm_tn(dpg, sv["h"], 1, 3 * D_MODEL, D_MODEL, 768, 1024, name="dw_in_gates", rows=N_COLS, row0=GATE_COL0)
    dpm, vecs, dwsg, dbsg_t = _mixer_bwd(sv["p"], dz, sv["opre"], sv["states"], sm["lb"], sm["g_out"], sm["w_conv"],
                                         sm["ln_g"], sm["ln_b"], sm["w_sg"], sm["b_sg_t"])
    g_in = _mm_tn(dpm, sv["h"], 1, GATE_COL0, D_MODEL, 768, 1024, name="dw_in_mixers", rows=N_COLS, into=g_in)
    g_in = by_chip(g_in[0].reshape(N_DEV, SHARD_IN, D_MODEL))
    big = dict(w_in=g_in, w_branch=g_branch, w_o=g_o, w_ff1=g_ff1, w_ff2=g_ff2)
    dx, dxb, dg_mix = _dh_bwd(dpm, dpg, w["w_in"], sv["x"], dx1, sm["g_mix"], before_end(big))
    small = dict(g_mix=dg_mix, g_ffn=dg_ffn, vecs=vecs, w_sg=dwsg, b_sg_t=dbsg_t, dx1=dx1)
    return dx, dxb, big, small


BIG = ("w_in", "w_branch", "w_o", "w_ff1", "w_ff2")
ANY = pl.BlockSpec(memory_space=pl.ANY)


def _place():
    return lax.axis_index("x"), lax.axis_index("y"), lax.axis_index("c")


def _al(v, m):
    return pl.multiple_of(v * m, m)


def _shard_of(refs, dev, which=range(len(BIG))):
    out = []
    for ref, t in zip(refs, which):
        by_cols = BIG[t] in ("w_branch", "w_ff1")
        n = ref.shape[-1 if by_cols else 0] // N_DEV
        part = pl.ds(_al(dev, n), n)
        out.append(ref.at[(slice(None),) * (len(ref.shape) - 1) + (part,)] if by_cols else ref.at[part])
    return out


def _gather_out_shapes(shards):
    s_in, s_b, s_o, s_1, s_2 = (shards[n] for n in BIG)
    return [SDS((s_in.shape[1] * N_DEV, s_in.shape[2]), bf16), SDS(s_b.shape[1:3] + (s_b.shape[3] * N_DEV,), bf16),
            SDS((s_o.shape[1] * N_DEV, s_o.shape[2]), bf16), SDS((s_1.shape[1], s_1.shape[2] * N_DEV), bf16),
            SDS((s_2.shape[1] * N_DEV, s_2.shape[2]), bf16)]


def _seq_all_gather_layer(layer, which, n_early, shard_refs, out_shapes, tag=""):
    nt = len(which)
    outs = [jax.empty_ref(sh, memory_space=pltpu.MemorySpace.HBM) for sh in out_shapes]
    early, late = tuple(range(n_early)), tuple(range(n_early, nt))

    @pl.kernel(mesh=plsc.ScalarSubcoreMesh(axis_name="seq", num_cores=1), name=f"seq_all_gather_l{layer}{tag}",
               scratch_types=(pltpu.SemaphoreType.DMA((9,)), pltpu.SemaphoreType.DMA((9,))),
               compiler_params=pltpu.CompilerParams(collective_id=1))
    def launch(send_sems, recv_sems):
        x, y, c = _place()
        me, sibling = (x, y, c), (x, y, 1 - c)
        first, second, diag = _ici_route(x, y, c)
        _handshake([sibling, first, second])
        mine = [r.at[layer] for r in shard_refs]

        def copies(k, blk, to, src=None, part=range(nt)):
            dst = _shard_of(outs, 4 * blk[0] + 2 * blk[1] + blk[2], which)
            src = dst if src is None else src
            return [pltpu.make_async_remote_copy(src_ref=src[t], dst_ref=dst[t], send_sem=send_sems.at[k], recv_sem=recv_sems.at[k],
                                                 device_id=to, device_id_type=MESH) for t in part]

        def start(cps):
            for cp in cps:
                cp.start()
            return cps

        def landed(cps):
            for cp in cps:
                cp.wait_recv()

        sent = start(copies(0, me, sibling, src=mine) + copies(1, me, first, src=mine, part=early)
                     + copies(2, me, first, src=mine, part=late) + copies(3, me, second, src=mine))
        landed(copies(1, first, me, part=early))
        sent += start(copies(4, first, second, part=early) + copies(6, first, sibling, part=early))
        landed(copies(2, first, me, part=late))
        sent += start(copies(5, first, second, part=late) + copies(6, first, sibling, part=late))
        landed(copies(3, second, me))
        sent += start(copies(7, second, sibling))
        landed(copies(4, diag, me, part=early) + copies(5, diag, me, part=late))
        sent += start(copies(8, diag, sibling))
        other = lambda p: (p[0], p[1], 1 - c)
        landed(copies(0, sibling, me) + copies(6, other(second), me) + copies(7, other(first), me) + copies(8, other(diag), me))
        for cp in sent:
            cp.wait_send()

    launch()
    return [o[...] for o in outs]


def _ici_route(x, y, c):
    return (x ^ (1 - c), y ^ c, c), (x ^ c, y ^ (1 - c), c), (1 - x, 1 - y, c)


def _place_own(where, which, shards, gathered, after):
    nt = len(which)

    def body(where_ref, *refs):
        del where_ref
        for src, dst in zip(refs[:nt], refs[2 * nt + 1:]):
            dst[...] = src[...]

    in_specs, out_specs = [], []
    for t, sh in zip(which, shards):
        blk = sh.shape[1:]
        in_specs.append(pl.BlockSpec((None,) + blk, functools.partial(lambda nd, i, wh: (wh[0],) + (0,) * nd, len(blk))))
        by_cols = BIG[t] in ("w_branch", "w_ff1")
        out_specs.append(pl.BlockSpec(blk, functools.partial(
            lambda nd, cols, i, wh: (0,) * (nd - 1) + (wh[1],) if cols else (wh[1],) + (0,) * (nd - 1), len(blk), by_cols)))
    return pl.pallas_call(
        body, name="place_own", out_shape=[SDS(g.shape, g.dtype) for g in gathered],
        input_output_aliases={1 + nt + i: i for i in range(nt)}, compiler_params=_cp("arbitrary"),
        grid_spec=pltpu.PrefetchScalarGridSpec(num_scalar_prefetch=1, grid=(1,), in_specs=in_specs + [ANY] * (nt + 1), out_specs=out_specs),
    )(where, *shards, *gathered, after)


def _handshake(peers):
    barrier = pltpu.get_barrier_semaphore()
    for p in peers:
        pl.semaphore_signal(barrier, inc=1, device_id=p, device_id_type=MESH)
    pl.semaphore_wait(barrier, len(peers))


def _seq_exchange_on_chip(grads):
    nt, nchip = len(BIG), N_DEV // 2
    g_refs = [jax.new_ref(g, memory_space=pltpu.MemorySpace.HBM) for g in grads]
    outs = [jax.empty_ref(SDS((nchip,) + g.shape[2:], bf16), memory_space=pltpu.MemorySpace.HBM) for g in grads]

    @pl.kernel(mesh=plsc.ScalarSubcoreMesh(axis_name="seq", num_cores=1), name="seq_rs_on_chip",
               scratch_types=(pltpu.SemaphoreType.DMA((nchip,)), pltpu.SemaphoreType.DMA((nchip,))),
               compiler_params=pltpu.CompilerParams(collective_id=2))
    def launch(send_sems, recv_sems):
        x, y, c = _place()
        sibling = (x, y, 1 - c)
        _handshake([sibling])
        remote = [pltpu.make_async_remote_copy(src_ref=g_refs[t].at[j, 1 - c], dst_ref=outs[t].at[j], send_sem=send_sems.at[j],
                                               recv_sem=recv_sems.at[j], device_id=sibling, device_id_type=MESH)
                  for j in range(nchip) for t in range(nt)]
        for cp in remote:
            cp.start()
        for cp in remote:
            cp.wait_recv()
        for cp in remote:
            cp.wait_send()

    launch()
    return [o[...] for o in outs], [g[...] for g in g_refs]


def _seq_exchange_between_chips(sums):
    nt = len(BIG)
    s_refs = [jax.new_ref(a, memory_space=pltpu.MemorySpace.HBM) for a in sums]
    outs = [jax.empty_ref(SDS((3,) + a.shape[1:], bf16), memory_space=pltpu.MemorySpace.HBM) for a in sums]
    transit = [jax.empty_ref(SDS(a.shape[1:], bf16), memory_space=pltpu.MemorySpace.HBM) for a in sums]

    early, late = (0,), tuple(range(1, nt))

    @pl.kernel(mesh=plsc.ScalarSubcoreMesh(axis_name="seq", num_cores=1), name="seq_rs_between_chips",
               scratch_types=(pltpu.SemaphoreType.DMA((6,)), pltpu.SemaphoreType.DMA((6,))),
               compiler_params=pltpu.CompilerParams(collective_id=3))
    def launch(send_sems, recv_sems):
        x, y, c = _place()
        first, second, diag = _ici_route(x, y, c)
        _handshake([first, second])

        def copies(k, src, dst, to, part=range(nt)):
            return [pltpu.make_async_remote_copy(src_ref=src(t), dst_ref=dst(t), send_sem=send_sems.at[k], recv_sem=recv_sems.at[k],
                                                 device_id=to, device_id_type=MESH) for t in part]

        chip_of = lambda p: 2 * p[0] + p[1]
        for_diag = lambda t: s_refs[t].at[chip_of(diag)]
        through = lambda t: transit[t]
        last = lambda t: outs[t].at[2]
        direct = (copies(0, lambda t: s_refs[t].at[chip_of(first)], lambda t: outs[t].at[0], first)
                  + copies(1, lambda t: s_refs[t].at[chip_of(second)], lambda t: outs[t].at[1], second))
        via = [copies(2, for_diag, through, first, early), copies(3, for_diag, through, first, late)]
        passed = [copies(4, through, last, second, early), copies(5, through, last, second, late)]
        for cp in via[0] + direct + via[1]:
            cp.start()
        for arrived, onward in zip(via, passed):
            for cp in arrived:
                cp.wait_recv()
            for cp in onward:
                cp.start()
        sent = direct + via[0] + via[1] + passed[0] + passed[1]
        for cp in direct + passed[0] + passed[1]:
            cp.wait_recv()
        for cp in sent:
            cp.wait_send()

    launch()
    return [o[...] for o in outs], [a[...] for a in s_refs]


def _chip_sums(core, mine, other, after, steps=2):
    nt, nchip = len(mine), mine[0].shape[0]
    m4 = [a.reshape(nchip, 2, -1, a.shape[-1]) for a in mine]
    o3 = [a.reshape(nchip, -1, a.shape[-1]) for a in other]

    def body(c_ref, *refs):
        del c_ref
        for a_ref, b_ref, o_ref in zip(refs[:nt], refs[nt:2 * nt], refs[2 * nt + 1:]):
            o_ref[...] = (a_ref[...].astype(f32) + b_ref[...].astype(f32)).astype(bf16)

    tiles = [(a.shape[1] // steps, a.shape[2]) for a in o3]
    blks = [pl.BlockSpec((None,) + t, lambda j, i, c_ref: (j, i, 0)) for t in tiles]
    outs = pl.pallas_call(
        body, name="chip_sums", out_shape=[SDS(a.shape, bf16) for a in o3], compiler_params=_cp("parallel", "parallel"),
        grid_spec=pltpu.PrefetchScalarGridSpec(
            num_scalar_prefetch=1, grid=(nchip, steps),
            in_specs=[pl.BlockSpec((None, None) + t, lambda j, i, c_ref: (j, c_ref[0], i, 0)) for t in tiles] + blks + [ANY],
            out_specs=blks),
    )(core, *m4, *o3, after)
    return [o.reshape(a.shape) for o, a in zip(outs, other)]


def _all_reduce_rows(pack):
    rows = pack.shape[0]
    blk = rows // N_DEV

    def body(in_ref, out_ref, land, send1, recv1, send2, recv2):
        x, y, c = _place()
        me = 4 * x + 2 * y + c
        others = [(px, py, pc) for px in range(2) for py in range(2) for pc in range(2)]

        def is_me(p):
            return jnp.logical_and(jnp.logical_and(p[0] == x, p[1] == y), p[2] == c)

        land[me] = in_ref[pl.ds(_al(me, blk), blk), :]
        for d, p in enumerate(others):
            @pl.when(jnp.logical_not(is_me(p)))
            def _():
                pltpu.make_async_remote_copy(src_ref=in_ref.at[pl.ds(d * blk, blk), :], dst_ref=land.at[me], send_sem=send1.at[d],
                                             recv_sem=recv1.at[me], device_id=p, device_id_type=MESH).start()
        for d, p in enumerate(others):
            @pl.when(jnp.logical_not(is_me(p)))
            def _():
                cp = pltpu.make_async_remote_copy(src_ref=in_ref.at[pl.ds(d * blk, blk), :], dst_ref=land.at[d], send_sem=send1.at[d],
                                                  recv_sem=recv1.at[d], device_id=p, device_id_type=MESH)
                cp.wait_recv()
                cp.wait_send()
        total = land[0]
        for d in range(1, N_DEV):
            total = total + land[d]
        out_ref[pl.ds(_al(me, blk), blk), :] = total
        for d, p in enumerate(others):
            @pl.when(jnp.logical_not(is_me(p)))
            def _():
                mine = out_ref.at[pl.ds(_al(me, blk), blk), :]
                pltpu.make_async_remote_copy(src_ref=mine, dst_ref=mine, send_sem=send2.at[d], recv_sem=recv2.at[me],
                                             device_id=p, device_id_type=MESH).start()
        for d, p in enumerate(others):
            @pl.when(jnp.logical_not(is_me(p)))
            def _():
                theirs = out_ref.at[pl.ds(d * blk, blk), :]
                cp = pltpu.make_async_remote_copy(src_ref=theirs, dst_ref=theirs, send_sem=send2.at[d], recv_sem=recv2.at[d],
                                                  device_id=p, device_id_type=MESH)
                cp.wait_recv()
                cp.wait_send()

    vm = pl.BlockSpec(memory_space=pltpu.VMEM)
    return pl.pallas_call(
        body, name="all_reduce_rows", in_specs=[vm], out_specs=vm, out_shape=SDS((rows, LANE), f32),
        scratch_shapes=[pltpu.VMEM((N_DEV, blk, LANE), f32)] + [pltpu.SemaphoreType.DMA((N_DEV,))] * 4,
        compiler_params=pltpu.CompilerParams(vmem_limit_bytes=VMEM_LIMIT),
    )(pack)


def _lower_bounds_fwd(lower):
    def body(l_ref, o_ref):
        sm = _layer_softmax(l_ref)
        run = jnp.zeros_like(sm[0])
        for l in range(DEPTH):
            o_ref[l:l + 1, :] = run
            if l + 1 < DEPTH:
                run = run + sm[l + 1]

    return pl.pallas_call(body, name="lower_bounds_fwd", out_shape=SDS(lower.shape, f32))(lower)


def _layer_softmax(l_ref):
    rows = [l_ref[l:l + 1, :] for l in range(DEPTH)]
    top = functools.reduce(jnp.maximum, rows)
    e = [jnp.exp(r - top) for r in rows]
    tot = functools.reduce(lambda a, b: a + b, e)
    return [v / tot for v in e]


def _lower_bounds_bwd(lower, dlbs):
    def body(l_ref, d_ref, o_ref):
        sm = _layer_softmax(l_ref)
        dsm = [None] * DEPTH
        run = jnp.zeros_like(sm[0])
        dsm[0] = run
        for l in reversed(range(1, DEPTH)):
            run = run + d_ref[l:l + 1, :]
            dsm[l] = run
        inner = functools.reduce(lambda a, b: a + b, [sm[l] * dsm[l] for l in range(DEPTH)])
        for l in range(DEPTH):
            o_ref[l:l + 1, :] = sm[l] * (dsm[l] - inner)

    return pl.pallas_call(body, name="lower_bounds_bwd", out_shape=SDS(lower.shape, f32))(lower, dlbs)


_ADAM_C1 = 1.0 - ADAM_B1 ** ADAM_STEP
_ADAM_C2 = 1.0 - ADAM_B2 ** ADAM_STEP


def _adamw(w, g, m, v):
    m = ADAM_B1 * m + (1.0 - ADAM_B1) * g
    v = ADAM_B2 * v + (1.0 - ADAM_B2) * (g * g)
    delta = -ADAM_LR * ((m / _ADAM_C1) / (jnp.sqrt(v / _ADAM_C2) + ADAM_EPS) + ADAM_WD * w)
    return delta, m, v


def _adam_big(where, names, w, m, v, sums, landed, outs, after, steps=4):
    nt = len(names)
    three = lambda a: a.reshape(a.shape[0], -1, a.shape[-1])
    w3, m3, v3 = ([three(d[n]) for n in names] for d in (w, m, v))
    outs3 = [three(a) for n in names for a in outs[n]]
    sums3 = [three(a) for a in sums]
    land3 = [three(a) for a in landed]

    def body(where_ref, *refs):
        del where_ref
        o_refs = refs[5 * nt + 4 * nt + 1:]
        for t in range(nt):
            w_ref, m_ref, v_ref, sum_ref, land_ref = (refs[q * nt + t] for q in range(5))
            g = sum_ref[...].astype(f32)
            for k in range(3):
                g = g + land_ref[k].astype(f32)
            delta, nm, nv = _adamw(w_ref[...], g, m_ref[...], v_ref[...])
            for o_ref, val in zip(o_refs[4 * t:4 * t + 4], (g, delta, nm, nv)):
                o_ref[...] = val

    tiles = [(a.shape[1] // steps, a.shape[2]) for a in w3]
    own = [pl.BlockSpec((None,) + t, lambda i, wh: (wh[0], i, 0)) for t in tiles]
    res = pl.pallas_call(
        body, name="adam_big", out_shape=[SDS(a.shape, f32) for a in outs3],
        input_output_aliases={1 + 5 * nt + i: i for i in range(4 * nt)}, compiler_params=_cp("parallel"),
        grid_spec=pltpu.PrefetchScalarGridSpec(
            num_scalar_prefetch=1, grid=(steps,),
            in_specs=own * 3 + [pl.BlockSpec((None,) + t, lambda i, wh: (wh[1], i, 0)) for t in tiles]
            + [pl.BlockSpec((3,) + t, lambda i, wh: (0, i, 0)) for t in tiles] + [ANY] * (4 * nt + 1),
            out_specs=[s for s in own for _ in range(4)]),
    )(where, *w3, *m3, *v3, *sums3, *land3, *outs3, after)
    return {n: [o.reshape(w[n].shape) for o in res[4 * t:4 * t + 4]] for t, n in enumerate(names)}


def _touch(a, after):
    a2 = a.reshape(-1, a.shape[-1])

    def body(a_ref, after_ref, o_ref):
        del after_ref
        o_ref[...] = a_ref[0:8, :].astype(f32)

    return pl.pallas_call(
        body, name="touch", grid=(1,), in_specs=[pl.BlockSpec((16, LANE), lambda i: (0, 0)), ANY],
        out_specs=pl.BlockSpec((8, LANE), lambda i: (0, 0)), out_shape=SDS((8, LANE), f32),
    )(a2, after)


def _adam_rows(w, g, m, v):
    def body(w_ref, g_ref, m_ref, v_ref, d_ref, nm_ref, nv_ref):
        delta, nm, nv = _adamw(w_ref[...], g_ref[...], m_ref[...], v_ref[...])
        d_ref[...] = delta
        nm_ref[...] = nm
        nv_ref[...] = nv

    return pl.pallas_call(body, name="adam_rows", out_shape=[SDS(w.shape, f32)] * 3)(w, g, m, v)


SMALL = ("g_mix", "lower_bounds", "g_hgrn_out", "w_conv", "sg_ln_g", "sg_ln_b", "w_sg", "b_sg", "g_ffn", "g_final")
WEIGHTS = ("w_in", "g_mix", "lower_bounds", "g_hgrn_out", "w_conv", "sg_ln_g", "sg_ln_b", "w_sg", "b_sg", "w_branch", "w_o", "g_ffn",
           "w_ff1", "w_ff2", "g_final")


def _pack_rows(arrays, multiple):
    flat = jnp.concatenate([a.reshape(-1) for a in arrays])
    rows = -(-flat.shape[0] // (LANE * multiple)) * multiple
    return jnp.pad(flat, (0, rows * LANE - flat.shape[0])).reshape(rows, LANE)


def _unpack_rows(pack, like):
    flat = pack.reshape(-1)
    out, at = [], 0
    for a in like:
        out.append(flat[at:at + a.size].reshape(a.shape))
        at += a.size
    return out


def kernel(x, w_in, g_mix, lower_bounds, g_hgrn_out, w_conv, sg_ln_g, sg_ln_b, w_sg, b_sg, w_branch, w_o, g_ffn, w_ff1, w_ff2, g_final, loss_target, m_w_in, m_g_mix, m_lower_bounds, m_g_hgrn_out, m_w_conv, m_sg_ln_g, m_sg_ln_b, m_w_sg, m_b_sg, m_w_branch, m_w_o, m_g_ffn, m_w_ff1, m_w_ff2, m_g_final, v_w_in, v_g_mix, v_lower_bounds, v_g_hgrn_out, v_w_conv, v_sg_ln_g, v_sg_ln_b, v_w_sg, v_b_sg, v_w_branch, v_w_o, v_g_ffn, v_w_ff1, v_w_ff2, v_g_final):
    weights = dict(w_in=w_in, g_mix=g_mix, lower_bounds=lower_bounds, g_hgrn_out=g_hgrn_out, w_conv=w_conv, sg_ln_g=sg_ln_g,
                   sg_ln_b=sg_ln_b, w_sg=w_sg, b_sg=b_sg, w_branch=w_branch, w_o=w_o, g_ffn=g_ffn, w_ff1=w_ff1, w_ff2=w_ff2, g_final=g_final)
    mom1 = dict(w_in=m_w_in, g_mix=m_g_mix, lower_bounds=m_lower_bounds, g_hgrn_out=m_g_hgrn_out, w_conv=m_w_conv, sg_ln_g=m_sg_ln_g,
                sg_ln_b=m_sg_ln_b, w_sg=m_w_sg, b_sg=m_b_sg, w_branch=m_w_branch, w_o=m_w_o, g_ffn=m_g_ffn, w_ff1=m_w_ff1, w_ff2=m_w_ff2,
                g_final=m_g_final)
    mom2 = dict(w_in=v_w_in, g_mix=v_g_mix, lower_bounds=v_lower_bounds, g_hgrn_out=v_g_hgrn_out, w_conv=v_w_conv, sg_ln_g=v_sg_ln_g,
                sg_ln_b=v_sg_ln_b, w_sg=v_w_sg, b_sg=v_b_sg, w_branch=v_w_branch, w_o=v_w_o, g_ffn=v_g_ffn, w_ff1=v_w_ff1, w_ff2=v_w_ff2,
                g_final=v_g_final)
    xi, yi, ci = _place()
    dev = 4 * xi + 2 * yi + ci
    conv_cols = w_conv.shape[-1]

    for d in (weights, mom1, mom2):
        d["w_in"] = jnp.swapaxes(d["w_in"], 1, 2)
    shards = {n: weights[n].astype(bf16) for n in BIG}

    conv_place = lax.dynamic_update_slice(jnp.zeros((DEPTH, 3, BRANCH), f32), w_conv, (0, 0, dev * conv_cols))
    (w_conv_full,) = _unpack_rows(_all_reduce_rows(_pack_rows([conv_place], 8 * N_DEV)), [conv_place])
    lbs = _lower_bounds_fwd(lower_bounds)

    def small_of(l):
        return dict(g_mix=g_mix[l][None], lb=lbs[l][None], g_out=g_hgrn_out[l][None], w_conv=w_conv_full[l], ln_g=sg_ln_g[l][None],
                    ln_b=sg_ln_b[l][None], w_sg=w_sg[l], b_sg_t=b_sg[l].T, g_ffn=g_ffn[l][None])

    act = x[0]
    full, saved = [], []
    shard_refs = [jax.new_ref(shards[n], memory_space=pltpu.MemorySpace.HBM) for n in BIG]
    shapes = _gather_out_shapes(shards)
    groups = [g for l in range(DEPTH) for g in ((l, (0,), 1, "a"), (l, (1, 2, 3, 4), 2, "b"))]
    arrived = {}
    for l, which, n_early, tag in groups:
        got = _seq_all_gather_layer(l, which, n_early, [shard_refs[t] for t in which], [shapes[t] for t in which], tag)
        arrived.update({(l, BIG[t]): (which, got) for t in which})

    for l in range(DEPTH):
        full.append({})

        def weight(name, after, l=l):
            if name not in full[l]:
                which, got = arrived[(l, name)]
                where = jnp.stack([jnp.int32(l), dev.astype(jnp.int32)])
                full[l].update(zip([BIG[t] for t in which], _place_own(where, which, [shards[BIG[t]] for t in which], got, after)))
            return full[l][name]

        act, sv = _layer_fwd(act, weight, small_of(l))
        saved.append(sv)
    loss_row, dx, dxb, dg_final = _final(act, loss_target[0], g_final[None])

    core = ci.astype(jnp.int32)[None]
    big_out = {n: [lax.empty(weights[n].shape, f32) for _ in range(4)] for n in BIG}
    small_grads = [None] * DEPTH

    def chip_sums(stage, after):
        l, received, mine = stage
        sums = _chip_sums(core, mine, received, after)
        placed.append(sums[BIG.index("w_o")])
        landed, sums = _seq_exchange_between_chips(sums)
        return l, sums, landed

    def adam_layer(stage, after):
        l, sums, landed = stage
        where = jnp.stack([jnp.int32(l), (2 * xi + yi).astype(jnp.int32)])
        big_out.update(_adam_big(where, BIG, weights, mom1, mom2, sums, landed, big_out, after))

    above = None
    placed = []
    for l in reversed(range(DEPTH)):
        summed = []

        def between(dx1):
            if above is None:
                return dx1
            summed.append(chip_sums(above, dx1))
            return placed[-1]

        def before_end(big):
            return _touch(summed[0][2][BIG.index("w_o")], big["w_in"]) if summed else big["w_in"]

        dx, dxb, big, small_grads[l] = _layer_bwd(dx, dxb, saved[l], full[l], small_of(l), between, before_end)
        if summed:
            adam_layer(summed[0], dx)
        above = (l, *_seq_exchange_on_chip([big[n] for n in BIG]))

    stack = lambda f: jnp.stack([f(small_grads[l]) for l in range(DEPTH)])
    d_lower = _lower_bounds_bwd(lower_bounds, stack(lambda s: s["vecs"][0]))
    local_small = dict(g_mix=stack(lambda s: s["g_mix"][0]), lower_bounds=d_lower, g_hgrn_out=stack(lambda s: s["vecs"][1]),
                       w_conv=stack(lambda s: s["vecs"][4:7]), sg_ln_g=stack(lambda s: s["vecs"][2]), sg_ln_b=stack(lambda s: s["vecs"][3]),
                       w_sg=stack(lambda s: s["w_sg"]), b_sg=stack(lambda s: s["b_sg_t"].T), g_ffn=stack(lambda s: s["g_ffn"][0]),
                       g_final=dg_final[0])
    order = [local_small[n] for n in SMALL] + [loss_row]
    *reduced, loss_sum = _unpack_rows(_all_reduce_rows(_pack_rows(order, 8 * N_DEV)), order)
    loss = loss_sum[0, 0]
    grads = dict(zip(SMALL, reduced))
    grads["w_conv"] = lax.dynamic_slice(grads["w_conv"], (0, 0, dev * conv_cols), (DEPTH, 3, conv_cols))

    deltas, new_m, new_v = {}, {}, {}
    packs = [_pack_rows([d[n] for n in SMALL], 8) for d in (weights, grads, mom1, mom2)]
    like = [weights[n] for n in SMALL]
    small_out = _adam_rows(*packs)
    for out, pack in zip((deltas, new_m, new_v), small_out):
        out.update(zip(SMALL, _unpack_rows(pack, like)))
    adam_layer(chip_sums(above, dx), small_out[0])
    for n in BIG:
        grads[n], deltas[n], new_m[n], new_v[n] = (jnp.swapaxes(a, 1, 2) if n == "w_in" else a for a in big_out[n])

    return (loss, dx[None], *[grads[n] for n in WEIGHTS], *[deltas[n] for n in WEIGHTS], *[new_m[n] for n in WEIGHTS],
            *[new_v[n] for n in WEIGHTS])
```

```python
import functools

import jax
import jax.numpy as jnp
from jax import lax
from jax.experimental import pallas as pl
from jax.experimental.pallas import tpu as pltpu
from jax.experimental.pallas import tpu_sc as plsc

f32 = jnp.float32
bf16 = jnp.bfloat16
SDS = jax.ShapeDtypeStruct
MESH = pl.DeviceIdType.MESH

D_MODEL = 1024
BRANCH = 512
N_COLS = 7680
D_FF = 4096
DEPTH = 4
HEADS = 4
HEAD_DIM = 128
HGRN_CHUNK = 64
SG_CHUNK = 128
SG_GROUPS = 4
NORM_EPS = 1e-6
LN_EPS = 1e-5
LB_FLOOR = 1e-30
N_DEV = 8
SHARD_IN = N_COLS // N_DEV
LANE = 128
GATE_COL0 = 9 * BRANCH

ADAM_LR = 0.001
ADAM_B1 = 0.9
ADAM_B2 = 0.999
ADAM_EPS = 1e-08
ADAM_WD = 0.01
ADAM_STEP = 10

MIX_TILE = 256
VMEM_LIMIT = 56 * 1024 * 1024


def _cp(*sem):
    return pltpu.CompilerParams(dimension_semantics=sem or None, vmem_limit_bytes=VMEM_LIMIT)


def _dot(a, b):
    return jnp.dot(a, b, preferred_element_type=f32)


def _dot_nt(a, b):
    return lax.dot_general(a, b, (((1,), (1,)), ((), ())), preferred_element_type=f32)


def _dot_tn(a, b):
    return lax.dot_general(a, b, (((0,), (0,)), ((), ())), preferred_element_type=f32)


def _dot_exact(ones, b):
    hi = b.astype(bf16)
    rest = b - hi.astype(f32)
    mid = rest.astype(bf16)
    low = (rest - mid.astype(f32)).astype(bf16)
    ones = ones.astype(bf16)
    return _dot(ones, hi) + _dot(ones, mid) + _dot(ones, low)


def _sigmoid(x):
    return jax.nn.sigmoid(x)


_GELU_C = 0.7978845608028654
_GELU_A = 0.044715


def _gelu(x):
    return 0.5 * x * (1.0 + jnp.tanh(_GELU_C * (x + _GELU_A * x * x * x)))


def _gelu_grad(x):
    x2 = x * x
    t = jnp.tanh(_GELU_C * (x + _GELU_A * x * x2))
    return 0.5 * (1.0 + t) + 0.5 * x * (1.0 - t * t) * _GELU_C * (1.0 + 3.0 * _GELU_A * x2)


def _rms_stats(x):
    r = lax.rsqrt(jnp.mean(x * x, axis=-1, keepdims=True) + NORM_EPS)
    return r, x * r


def _rms_bwd(dh, xh, r, g):
    dg = jnp.sum(dh * xh, axis=0, keepdims=True)
    dxn = dh * g
    dx = r * (dxn - xh * jnp.mean(dxn * xh, axis=-1, keepdims=True))
    return dx, dg


def _tri(n, upper=False):
    r = lax.broadcasted_iota(jnp.int32, (n, n), 0)
    c = lax.broadcasted_iota(jnp.int32, (n, n), 1)
    return (c >= r) if upper else (c <= r)


def _acc_rows(ref, first, val):
    @pl.when(first)
    def _():
        ref[...] = val

    @pl.when(jnp.logical_not(first))
    def _():
        ref[...] += val


def _rms_mm(x, g, w_t, tm=1024, tn=1536):
    s, n = x.shape[0], w_t.shape[0]
    jm = GATE_COL0 // tn

    def body(x_ref, g_ref, w_ref, pm_ref, pg_ref, h_ref, hs):
        j = pl.program_id(1)

        @pl.when(j == 0)
        def _():
            _, xh = _rms_stats(x_ref[...])
            hv = (xh * g_ref[...]).astype(bf16)
            hs[...] = hv
            h_ref[...] = hv

        res = _dot_nt(hs[...], w_ref[...])

        @pl.when(j < jm)
        def _():
            pm_ref[...] = res

        @pl.when(j >= jm)
        def _():
            pg_ref[...] = res.astype(bf16)

    return pl.pallas_call(
        body, name="rms_mm", grid=(s // tm, n // tn),
        in_specs=[pl.BlockSpec((tm, D_MODEL), lambda i, j: (i, 0)), pl.BlockSpec((1, D_MODEL), lambda i, j: (0, 0)),
                  pl.BlockSpec((tn, D_MODEL), lambda i, j: (j, 0))],
        out_specs=[pl.BlockSpec((tm, tn), lambda i, j: (i, jnp.minimum(j, jm - 1))),
                   pl.BlockSpec((tm, tn), lambda i, j: (i, jnp.maximum(j - jm, 0))), pl.BlockSpec((tm, D_MODEL), lambda i, j: (i, 0))],
        out_shape=[SDS((s, GATE_COL0), f32), SDS((s, n - GATE_COL0), bf16), SDS((s, D_MODEL), bf16)],
        scratch_shapes=[pltpu.VMEM((tm, D_MODEL), bf16)], compiler_params=_cp("parallel", "arbitrary"),
    )(x, g, w_t)


def _hgrn_gates(fp, lb):
    logf = jnp.logaddexp(jnp.log(jnp.maximum(lb, LB_FLOOR)), jnp.log1p(-lb) + jax.nn.log_sigmoid(fp))
    snf = _sigmoid(-fp)
    return logf, snf, (1.0 - lb) * snf


def _p_specs(tile, cols, row_map):
    return [pl.BlockSpec((tile, BRANCH), functools.partial(lambda c, i: (row_map(i), c), c)) for c in cols]


def _mixer_fwd(p, lb, gout, wconv, lng, lnb, wsg, bsg_t):
    s = p.shape[0]
    tt = MIX_TILE
    nch = tt // HGRN_CHUNK

    def body(q_ref, fp_ref, iv_ref, go_ref, bg_ref, cg_ref, xc_ref, u_ref, v_ref, lb_ref, gout_ref, wconv_ref, lng_ref,
             lnb_ref, wsg_ref, bsg_ref, z_ref, opre_ref, st_ref, st_scr, zbuf):
        @pl.when(pl.program_id(0) == 0)
        def _():
            st_scr[...] = jnp.zeros_like(st_scr)
            zbuf[0:8, :] = jnp.zeros((8, BRANCH), f32)

        lbv = lb_ref[...]
        gout_v = gout_ref[...]
        causal = _tri(HGRN_CHUNK)
        tri = causal.astype(f32)
        last_row = lax.broadcasted_iota(jnp.int32, (HGRN_CHUNK, 1), 0) == HGRN_CHUNK - 1
        for c in range(nch):
            rows = slice(HGRN_CHUNK * c, HGRN_CHUNK * (c + 1))
            q_raw = q_ref[rows, :]
            qs = q_raw * _sigmoid(q_raw)
            logf, _, kk = _hgrn_gates(fp_ref[rows, :], lbv)
            b = _dot_exact(tri, logf)
            bl = jnp.sum(jnp.where(last_row, b, 0.0), axis=0, keepdims=True)
            qb = (qs * jnp.exp(b)).astype(bf16)
            kb = (kk * jnp.exp(-b)).astype(bf16)
            kd = (kk * jnp.exp(bl - b)).astype(bf16)
            ebl = jnp.exp(bl)
            vc = iv_ref[rows, :].astype(bf16)
            gate = _sigmoid(go_ref[rows, :])
            for h in range(HEADS):
                sl = slice(HEAD_DIM * h, HEAD_DIM * (h + 1))
                st = st_scr[h]
                st_ref[c, h] = st
                a = jnp.where(causal, _dot_nt(qb[:, sl], kb[:, sl]), 0.0)
                o = _dot(a.astype(bf16), vc[:, sl]) + _dot_nt(qb[:, sl], st.astype(bf16))
                opre_ref[rows, sl] = o
                st_scr[h] = st * ebl[:, sl] + _dot_tn(vc[:, sl], kd[:, sl])
                _, oh = _rms_stats(o)
                z_ref[rows, sl] = (oh * gout_v[:, sl] * gate[:, sl]).astype(bf16)

        zc = cg_ref[...] * xc_ref[...]
        zbuf[8:8 + tt, :] = zc
        y = wconv_ref[0:1, :] * zbuf[pl.ds(6, tt), :] + wconv_ref[1:2, :] * zbuf[pl.ds(7, tt), :] + wconv_ref[2:3, :] * zc
        z_ref[:, BRANCH:2 * BRANCH] = (bg_ref[...] * y).astype(bf16)
        zbuf[0:8, :] = zbuf[tt:tt + 8, :]

        lng_v, lnb_v = lng_ref[...], lnb_ref[...]
        low = _tri(SG_CHUNK)
        wms = [jnp.where(low, wsg_ref[g], 0.0).astype(bf16) for g in range(SG_GROUPS)]
        for cc in range(tt // SG_CHUNK):
            rows = slice(SG_CHUNK * cc, SG_CHUNK * (cc + 1))
            ug = _gelu(u_ref[rows, :])
            vg = _gelu(v_ref[rows, :])
            vcen = vg - jnp.mean(vg, axis=-1, keepdims=True)
            rstd = lax.rsqrt(jnp.mean(vcen * vcen, axis=-1, keepdims=True) + LN_EPS)
            vn = (vcen * rstd * lng_v + lnb_v).astype(bf16)
            for g in range(SG_GROUPS):
                sl = slice(LANE * g, LANE * (g + 1))
                sv = _dot(wms[g], vn[:, sl]) + bsg_ref[:, g:g + 1]
                z_ref[rows, 2 * BRANCH + LANE * g:2 * BRANCH + LANE * (g + 1)] = (ug[:, sl] * sv).astype(bf16)

    full = lambda shape: pl.BlockSpec(shape, lambda i: (0,) * len(shape))
    return pl.pallas_call(
        body, name="mixer_fwd", grid=(s // tt,),
        in_specs=_p_specs(tt, range(9), lambda i: i) + [full((1, BRANCH)), full((1, BRANCH)), full((3, BRANCH)), full((1, BRANCH)),
                                                        full((1, BRANCH)), full((SG_GROUPS, SG_CHUNK, SG_CHUNK)), full((SG_CHUNK, SG_GROUPS))],
        out_specs=[pl.BlockSpec((tt, 3 * BRANCH), lambda i: (i, 0)), pl.BlockSpec((tt, BRANCH), lambda i: (i, 0)),
                   pl.BlockSpec((nch, HEADS, HEAD_DIM, HEAD_DIM), lambda i: (i, 0, 0, 0))],
        out_shape=[SDS((s, 3 * BRANCH), bf16), SDS((s, BRANCH), f32), SDS((s // HGRN_CHUNK, HEADS, HEAD_DIM, HEAD_DIM), f32)],
        scratch_shapes=[pltpu.VMEM((HEADS, HEAD_DIM, HEAD_DIM), f32), pltpu.VMEM((tt + 8, BRANCH), f32)],
        compiler_params=_cp("arbitrary"),
    )(*([p] * 9), lb, gout, wconv, lng, lnb, wsg, bsg_t)


def _branch_gate(z, wb, pg, x, wo, tm=512):
    s = z.shape[0]

    def body(z_ref, wb_ref, g_ref, x_ref, wo_ref, y_ref, m_ref, x1_ref):
        acc = None
        for n in range(3):
            cols = slice(D_MODEL * n, D_MODEL * (n + 1))
            yn = _dot(z_ref[:, BRANCH * n:BRANCH * (n + 1)], wb_ref[n])
            y_ref[:, cols] = yn.astype(bf16)
            t = _sigmoid(g_ref[:, cols].astype(f32)) * yn
            acc = t if acc is None else acc + t
        merged = acc.astype(bf16)
        m_ref[...] = merged
        x1_ref[...] = x_ref[...] + _dot(merged, wo_ref[...])

    row = pl.BlockSpec((tm, D_MODEL), lambda i: (i, 0))
    wide = pl.BlockSpec((tm, 3 * D_MODEL), lambda i: (i, 0))
    return pl.pallas_call(
        body, name="branch_gate", grid=(s // tm,),
        in_specs=[pl.BlockSpec((tm, 3 * BRANCH), lambda i: (i, 0)), pl.BlockSpec((3, BRANCH, D_MODEL), lambda i: (0, 0, 0)), wide, row,
                  pl.BlockSpec((D_MODEL, D_MODEL), lambda i: (0, 0))],
        out_specs=[wide, row, row],
        out_shape=[SDS((s, 3 * D_MODEL), bf16), SDS((s, D_MODEL), bf16), SDS((s, D_MODEL), f32)], compiler_params=_cp("parallel"),
    )(z, wb, pg, x, wo)


def _ffn(x1, g, w1, w2, tm=1024, tf=1024):
    s = x1.shape[0]
    nf = D_FF // tf

    def body(x_ref, g_ref, w1_ref, w2_ref, o_ref, h_ref, ra_ref, hs, acc):
        f = pl.program_id(1)

        @pl.when(f == 0)
        def _():
            _, xh = _rms_stats(x_ref[...])
            hv = (xh * g_ref[...]).astype(bf16)
            hs[...] = hv
            h_ref[...] = hv
            acc[...] = jnp.zeros_like(acc)

        ra = jnp.maximum(_dot(hs[...], w1_ref[...]), 0.0)
        ra_ref[...] = ra.astype(bf16)
        acc[...] += _dot((ra * ra).astype(bf16), w2_ref[...])

        @pl.when(f == nf - 1)
        def _():
            o_ref[...] = x_ref[...] + acc[...]

    return pl.pallas_call(
        body, name="ffn", grid=(s // tm, nf),
        in_specs=[pl.BlockSpec((tm, D_MODEL), lambda i, f: (i, 0)), pl.BlockSpec((1, D_MODEL), lambda i, f: (0, 0)),
                  pl.BlockSpec((D_MODEL, tf), lambda i, f: (0, f)), pl.BlockSpec((tf, D_MODEL), lambda i, f: (f, 0))],
        out_specs=[pl.BlockSpec((tm, D_MODEL), lambda i, f: (i, 0)), pl.BlockSpec((tm, D_MODEL), lambda i, f: (i, 0)),
                   pl.BlockSpec((tm, tf), lambda i, f: (i, f))],
        out_shape=[SDS((s, D_MODEL), f32), SDS((s, D_MODEL), bf16), SDS((s, D_FF), bf16)],
        scratch_shapes=[pltpu.VMEM((tm, D_MODEL), bf16), pltpu.VMEM((tm, D_MODEL), f32)], compiler_params=_cp("parallel", "arbitrary"),
    )(x1, g, w1, w2)


def _final(x, target, g, tm=512):
    s = x.shape[0]

    def body(x_ref, t_ref, g_ref, loss_ref, dx_ref, dxb_ref, dg_ref):
        first = pl.program_id(0) == 0
        gv = g_ref[...]
        r, xh = _rms_stats(x_ref[...])
        e = xh * gv - t_ref[...]
        tile_loss = 0.5 * jnp.sum(jnp.mean(e * e, axis=-1, keepdims=True), axis=0, keepdims=True)
        dx, dg = _rms_bwd(e * (1.0 / D_MODEL), xh, r, gv)
        dx_ref[...] = dx
        dxb_ref[...] = dx.astype(bf16)
        _acc_rows(dg_ref, first, dg)
        _acc_rows(loss_ref, first, jnp.broadcast_to(tile_loss, (1, LANE)))

    row = pl.BlockSpec((tm, D_MODEL), lambda i: (i, 0))
    return pl.pallas_call(
        body, name="final_loss", grid=(s // tm,), in_specs=[row, row, pl.BlockSpec((1, D_MODEL), lambda i: (0, 0))],
        out_specs=[pl.BlockSpec((1, LANE), lambda i: (0, 0)), row, row, pl.BlockSpec((1, D_MODEL), lambda i: (0, 0))],
        out_shape=[SDS((1, LANE), f32), SDS((s, D_MODEL), f32), SDS((s, D_MODEL), bf16), SDS((1, D_MODEL), f32)],
        compiler_params=_cp("arbitrary"),
    )(x, target, g)


def _ffn_bwd(dx2, dx2b, x1, g, ra, w1, w2, tm=512, tf=2048):
    s = x1.shape[0]
    nf = D_FF // tf

    def body(dx_ref, dxb_ref, x_ref, g_ref, ra_ref, w1_ref, w2_ref, da_ref, dx1_ref, dx1b_ref, dg_ref, acc):
        i, f = pl.program_id(0), pl.program_id(1)

        @pl.when(f == 0)
        def _():
            acc[...] = jnp.zeros_like(acc)

        da = (_dot_nt(dxb_ref[...], w2_ref[...]) * (2.0 * ra_ref[...].astype(f32))).astype(bf16)
        da_ref[...] = da
        acc[...] += _dot_nt(da, w1_ref[...])

        @pl.when(f == nf - 1)
        def _():
            r, xh = _rms_stats(x_ref[...])
            dx, dg = _rms_bwd(acc[...], xh, r, g_ref[...])
            dx = dx + dx_ref[...]
            dx1_ref[...] = dx
            dx1b_ref[...] = dx.astype(bf16)
            _acc_rows(dg_ref, i == 0, dg)

    row = pl.BlockSpec((tm, D_MODEL), lambda i, f: (i, 0))
    col = pl.BlockSpec((tm, tf), lambda i, f: (i, f))
    return pl.pallas_call(
        body, name="ffn_bwd", grid=(s // tm, nf),
        in_specs=[row, row, row, pl.BlockSpec((1, D_MODEL), lambda i, f: (0, 0)), col,
                  pl.BlockSpec((D_MODEL, tf), lambda i, f: (0, f)), pl.BlockSpec((tf, D_MODEL), lambda i, f: (f, 0))],
        out_specs=[col, row, row, pl.BlockSpec((1, D_MODEL), lambda i, f: (0, 0))],
        out_shape=[SDS((s, D_FF), bf16), SDS((s, D_MODEL), f32), SDS((s, D_MODEL), bf16), SDS((1, D_MODEL), f32)],
        scratch_shapes=[pltpu.VMEM((tm, D_MODEL), f32)], compiler_params=_cp("arbitrary", "arbitrary"),
    )(dx2, dx2b, x1, g, ra, w1, w2)


def _mm_tn(a, b, nb, m, n, tm, tn, name="mm_tn", rows=None, row0=0, into=None, square_a=False):
    s = a.shape[0]
    mi, nj = m // tm, n // tn
    rows = m if rows is None else rows
    blk0 = row0 // tm

    def body(a_ref, b_ref, *rest):
        av = a_ref[...]
        if square_a:
            av = av.astype(f32)
            av = (av * av).astype(bf16)
        rest[-1][...] = _dot_tn(av, b_ref[...]).astype(bf16)

    extra = {} if into is None else dict(input_output_aliases={2: 0})
    return pl.pallas_call(
        body, name=name, grid=(nb, mi, nj),
        in_specs=[pl.BlockSpec((s, tm), lambda k, i, j: (0, k * mi + i)), pl.BlockSpec((s, tn), lambda k, i, j: (0, k * nj + j))]
        + ([] if into is None else [pl.BlockSpec(memory_space=pl.ANY)]),
        out_specs=pl.BlockSpec((None, tm, tn), lambda k, i, j: (k, blk0 + i, j)), out_shape=SDS((nb, rows, n), bf16),
        compiler_params=_cp("parallel", "parallel", "parallel"), **extra,
    )(a, b, *([] if into is None else [into]))


def _mm_tn_slabs(a, b, nb, m, nblk, rel, width, tm=512, name="mm_tn_slabs"):
    s = a.shape[0]
    n = b.shape[1] // nb
    ng, mi, nw = n // nblk, m // tm, len(rel)

    def body(a_ref, b_ref, o_ref):
        full = _dot_tn(a_ref[...], b_ref[...])
        for r, start in enumerate(rel):
            o_ref[r] = full[:, start:start + width].astype(bf16)

    return pl.pallas_call(
        body, name=name, grid=(nb, ng, mi),
        in_specs=[pl.BlockSpec((s, tm), lambda k, g, i: (0, k * mi + i)), pl.BlockSpec((s, nblk), lambda k, g, i: (0, k * ng + g))],
        out_specs=pl.BlockSpec((nw, None, tm, width), lambda k, g, i: (g, k, i, 0)), out_shape=SDS((ng * nw, nb, m, width), bf16),
        compiler_params=_cp("parallel", "parallel", "parallel"),
    )(a, b)


def _merge_bwd(dx1b, wo, y, pg, wb, after, tm=512):
    s = dx1b.shape[0]

    def body(dx_ref, wo_ref, y_ref, g_ref, wb_ref, after_ref, dy_ref, dg_ref, dz_ref):
        del after_ref
        dm = _dot_nt(dx_ref[...], wo_ref[...])
        for n in range(3):
            cols = slice(D_MODEL * n, D_MODEL * (n + 1))
            gate = _sigmoid(g_ref[:, cols].astype(f32))
            t = dm * gate
            dy = t.astype(bf16)
            dy_ref[:, cols] = dy
            dg_ref[:, cols] = (t * y_ref[:, cols].astype(f32) * (1.0 - gate)).astype(bf16)
            dz_ref[:, BRANCH * n:BRANCH * (n + 1)] = _dot_nt(dy, wb_ref[n]).astype(bf16)

    wide = pl.BlockSpec((tm, 3 * D_MODEL), lambda i: (i, 0))
    return pl.pallas_call(
        body, name="merge_bwd", grid=(s // tm,),
        in_specs=[pl.BlockSpec((tm, D_MODEL), lambda i: (i, 0)), pl.BlockSpec((D_MODEL, D_MODEL), lambda i: (0, 0)), wide, wide,
                  pl.BlockSpec((3, BRANCH, D_MODEL), lambda i: (0, 0, 0)), pl.BlockSpec(memory_space=pl.ANY)],
        out_specs=[wide, wide, pl.BlockSpec((tm, 3 * BRANCH), lambda i: (i, 0))],
        out_shape=[SDS((s, 3 * D_MODEL), bf16), SDS((s, 3 * D_MODEL), bf16), SDS((s, 3 * BRANCH), bf16)],
        compiler_params=_cp("parallel"),
    )(dx1b, wo, y, pg, wb, after)


def _mixer_bwd(p, dz, opre, states, lb, gout, wconv, lng, lnb, wsg, bsg_t):
    s = p.shape[0]
    tt = MIX_TILE
    nt = s // tt
    nch = tt // HGRN_CHUNK
    rev = lambda i: nt - 1 - i

    def body(q_ref, fp_ref, iv_ref, go_ref, bg_ref, cg_ref, xc_ref, u_ref, v_ref, cgp_ref, xcp_ref, dz_ref, opre_ref, st_ref,
             lb_ref, gout_ref, wconv_ref, lng_ref, lnb_ref, wsg_ref, bsg_ref,
             dp_ref, vec_ref, dwsg_ref, dbsg_ref, dst_scr, zbuf, dybuf, dbsg_acc):
        i = pl.program_id(0)

        @pl.when(i == 0)
        def _():
            dst_scr[...] = jnp.zeros_like(dst_scr)
            dybuf[tt:tt + 8, :] = jnp.zeros((8, BRANCH), f32)
            vec_ref[...] = jnp.zeros_like(vec_ref)
            dwsg_ref[...] = jnp.zeros_like(dwsg_ref)
            dbsg_acc[...] = jnp.zeros_like(dbsg_acc)

        lbv = lb_ref[...]
        gout_v = gout_ref[...]
        causal = _tri(HGRN_CHUNK)
        tri = causal.astype(f32)
        tri_up = _tri(HGRN_CHUNK, upper=True).astype(f32)
        last_row = lax.broadcasted_iota(jnp.int32, (HGRN_CHUNK, 1), 0) == HGRN_CHUNK - 1
        lb_live = (lbv > LB_FLOOR).astype(f32)
        dlb = jnp.zeros((1, BRANCH), f32)
        dgout = jnp.zeros((1, BRANCH), f32)
        for c in reversed(range(nch)):
            rows = slice(HGRN_CHUNK * c, HGRN_CHUNK * (c + 1))
            q_c, fp = q_ref[rows, :], fp_ref[rows, :]
            sq_c = _sigmoid(q_c)
            sfp_c = _sigmoid(fp)
            logf, snf_c, kk = _hgrn_gates(fp, lbv)
            invf_c = jnp.exp(-logf)
            doa = dz_ref[rows, 0:BRANCH].astype(f32)
            o = opre_ref[rows, :]
            sgo = _sigmoid(go_ref[rows, :])
            d_o, dgo, dg_c = [], [], []
            for h in range(HEADS):
                sl = slice(HEAD_DIM * h, HEAD_DIM * (h + 1))
                r, oh = _rms_stats(o[:, sl])
                dgo.append(doa[:, sl] * oh * gout_v[:, sl] * sgo[:, sl] * (1.0 - sgo[:, sl]))
                dx, dg = _rms_bwd(doa[:, sl] * sgo[:, sl], oh, r, gout_v[:, sl])
                d_o.append(dx)
                dg_c.append(dg)
            dp_ref[rows, 3 * BRANCH:4 * BRANCH] = jnp.concatenate(dgo, axis=1).astype(bf16)
            dgout = dgout + jnp.concatenate(dg_c, axis=1)
            dob = jnp.concatenate(d_o, axis=1).astype(bf16)
            b = _dot_exact(tri, logf)
            bl = jnp.sum(jnp.where(last_row, b, 0.0), axis=0, keepdims=True)
            eb, enb, edl, ebl = jnp.exp(b), jnp.exp(-b), jnp.exp(bl - b), jnp.exp(bl)
            qbf, kbf, kdf = q_c * sq_c * eb, kk * enb, kk * edl
            qb, kb, kd = qbf.astype(bf16), kbf.astype(bf16), kdf.astype(bf16)
            vc = iv_ref[rows, :].astype(bf16)
            dv, dqb, dkb, dkd, debl = [], [], [], [], []
            for h in range(HEADS):
                sl = slice(HEAD_DIM * h, HEAD_DIM * (h + 1))
                st = st_ref[c, h]
                dst = dst_scr[h]
                stb, dstb = st.astype(bf16), dst.astype(bf16)
                a = jnp.where(causal, _dot_nt(qb[:, sl], kb[:, sl]), 0.0).astype(bf16)
                da = jnp.where(causal, _dot_nt(dob[:, sl], vc[:, sl]), 0.0).astype(bf16)
                dv.append(_dot_tn(a, dob[:, sl]) + _dot_nt(kd[:, sl], dstb))
                dqb.append(_dot(dob[:, sl], stb) + _dot(da, kb[:, sl]))
                dkb.append(_dot_tn(da, qb[:, sl]))
                dkd.append(_dot(vc[:, sl], dstb))
                debl.append(jnp.sum(st * dst, axis=0, keepdims=True))
                dst_scr[h] = _dot_tn(dob[:, sl], qb[:, sl]) + dst * ebl[:, sl]
            dv, dqb, dkb, dkd = (jnp.concatenate(t, axis=1) for t in (dv, dqb, dkb, dkd))
            debl = jnp.concatenate(debl, axis=1)
            t_kd = dkd * kdf
            dbl = ebl * debl + jnp.sum(t_kd, axis=0, keepdims=True)
            db = dqb * qbf - dkb * kbf - t_kd + jnp.where(last_row, dbl, 0.0)
            dkk = dkb * enb + dkd * edl
            dlc = _dot_exact(tri_up, db)
            slope = (1.0 - lbv) * sfp_c * snf_c
            dp_ref[rows, 0:BRANCH] = (dqb * eb * sq_c * (1.0 + q_c * (1.0 - sq_c))).astype(bf16)
            dp_ref[rows, BRANCH:2 * BRANCH] = (slope * (dlc * invf_c - dkk)).astype(bf16)
            dp_ref[rows, 2 * BRANCH:3 * BRANCH] = dv.astype(bf16)
            dlb = dlb + jnp.sum(dlc * (lb_live - sfp_c) * invf_c - dkk * snf_c, axis=0, keepdims=True)
        vec_ref[0:1, :] += dlb
        vec_ref[1:2, :] += dgout

        dob_ = dz_ref[:, BRANCH:2 * BRANCH].astype(f32)
        bg, cg, xc = bg_ref[...], cg_ref[...], xc_ref[...]
        zc = cg * xc
        zbuf[0:8, :] = jnp.where(i < nt - 1, cgp_ref[...] * xcp_ref[...], 0.0)
        zbuf[8:8 + tt, :] = zc
        w0, w1, w2 = wconv_ref[0:1, :], wconv_ref[1:2, :], wconv_ref[2:3, :]
        y = w0 * zbuf[pl.ds(6, tt), :] + w1 * zbuf[pl.ds(7, tt), :] + w2 * zc
        dy = dob_ * bg
        dybuf[0:tt, :] = dy
        dy1, dy2 = dybuf[pl.ds(1, tt), :], dybuf[pl.ds(2, tt), :]
        dzc = w2 * dy + w1 * dy1 + w0 * dy2
        dp_ref[:, 4 * BRANCH:5 * BRANCH] = (dob_ * y).astype(bf16)
        dp_ref[:, 5 * BRANCH:6 * BRANCH] = (dzc * xc).astype(bf16)
        dp_ref[:, 6 * BRANCH:7 * BRANCH] = (dzc * cg).astype(bf16)
        vec_ref[4:5, :] += jnp.sum(zc * dy2, axis=0, keepdims=True)
        vec_ref[5:6, :] += jnp.sum(zc * dy1, axis=0, keepdims=True)
        vec_ref[6:7, :] += jnp.sum(zc * dy, axis=0, keepdims=True)
        dybuf[tt:tt + 8, :] = dybuf[0:8, :]

        lng_v, lnb_v = lng_ref[...], lnb_ref[...]
        low = _tri(SG_CHUNK)
        wms = [jnp.where(low, wsg_ref[g], 0.0).astype(bf16) for g in range(SG_GROUPS)]
        dlng = jnp.zeros((1, BRANCH), f32)
        dlnb = jnp.zeros((1, BRANCH), f32)
        for cc in range(tt // SG_CHUNK):
            rows = slice(SG_CHUNK * cc, SG_CHUNK * (cc + 1))
            doc = dz_ref[rows, 2 * BRANCH:3 * BRANCH].astype(f32)
            u_raw, v_raw = u_ref[rows, :], v_ref[rows, :]
            ug = _gelu(u_raw)
            vg = _gelu(v_raw)
            vcen = vg - jnp.mean(vg, axis=-1, keepdims=True)
            rstd = lax.rsqrt(jnp.mean(vcen * vcen, axis=-1, keepdims=True) + LN_EPS)
            vhat = vcen * rstd
            vn = (vhat * lng_v + lnb_v).astype(bf16)
            dvn = []
            for g in range(SG_GROUPS):
                sl = slice(LANE * g, LANE * (g + 1))
                sv = _dot(wms[g], vn[:, sl]) + bsg_ref[:, g:g + 1]
                dp_ref[rows, 7 * BRANCH + LANE * g:7 * BRANCH + LANE * (g + 1)] = (doc[:, sl] * sv * _gelu_grad(u_raw[:, sl])).astype(bf16)
                dsv = doc[:, sl] * ug[:, sl]
                dsvb = dsv.astype(bf16)
                dbsg_acc[:, sl] += dsv
                dwsg_ref[g] += jnp.where(low, _dot_nt(dsvb, vn[:, sl]), 0.0)
                dvn.append(_dot_tn(wms[g], dsvb))
            dvn = jnp.concatenate(dvn, axis=1)
            dlng = dlng + jnp.sum(dvn * vhat, axis=0, keepdims=True)
            dlnb = dlnb + jnp.sum(dvn, axis=0, keepdims=True)
            dvh = dvn * lng_v
            dvg = rstd * (dvh - jnp.mean(dvh, axis=-1, keepdims=True) - vhat * jnp.mean(dvh * vhat, axis=-1, keepdims=True))
            dp_ref[rows, 8 * BRANCH:9 * BRANCH] = (dvg * _gelu_grad(v_raw)).astype(bf16)
        vec_ref[2:3, :] += dlng
        vec_ref[3:4, :] += dlnb

        @pl.when(i == nt - 1)
        def _():
            for g in range(SG_GROUPS):
                dbsg_ref[:, g:g + 1] = jnp.sum(dbsg_acc[:, LANE * g:LANE * (g + 1)], axis=1, keepdims=True)

    full = lambda shape: pl.BlockSpec(shape, lambda i: (0,) * len(shape))
    tail = lambda c: pl.BlockSpec((8, BRANCH), lambda i: (jnp.maximum(rev(i) * (tt // 8) - 1, 0), c))
    return pl.pallas_call(
        body, name="mixer_bwd", grid=(nt,),
        in_specs=_p_specs(tt, range(9), rev) + [tail(5), tail(6), pl.BlockSpec((tt, 3 * BRANCH), lambda i: (rev(i), 0)),
                                                pl.BlockSpec((tt, BRANCH), lambda i: (rev(i), 0)),
                                                pl.BlockSpec((nch, HEADS, HEAD_DIM, HEAD_DIM), lambda i: (rev(i), 0, 0, 0)),
                                                full((1, BRANCH)), full((1, BRANCH)), full((3, BRANCH)), full((1, BRANCH)), full((1, BRANCH)),
                                                full((SG_GROUPS, SG_CHUNK, SG_CHUNK)), full((SG_CHUNK, SG_GROUPS))],
        out_specs=[pl.BlockSpec((tt, 9 * BRANCH), lambda i: (rev(i), 0)), full((8, BRANCH)), full((SG_GROUPS, SG_CHUNK, SG_CHUNK)),
                   full((SG_CHUNK, SG_GROUPS))],
        out_shape=[SDS((s, 9 * BRANCH), bf16), SDS((8, BRANCH), f32), SDS((SG_GROUPS, SG_CHUNK, SG_CHUNK), f32), SDS((SG_CHUNK, SG_GROUPS), f32)],
        scratch_shapes=[pltpu.VMEM((HEADS, HEAD_DIM, HEAD_DIM), f32), pltpu.VMEM((tt + 8, BRANCH), f32), pltpu.VMEM((tt + 8, BRANCH), f32),
                        pltpu.VMEM((SG_CHUNK, BRANCH), f32)],
        compiler_params=_cp("arbitrary"),
    )(*([p] * 11), dz, opre, states, lb, gout, wconv, lng, lnb, wsg, bsg_t)


def _dh_bwd(dpm, dpg, w_t, x, dx1, g, after, tm=1024, tk=1536):
    s = x.shape[0]
    km = dpm.shape[1] // tk
    nk = km + dpg.shape[1] // tk

    def body(dpm_ref, dpg_ref, w_ref, x_ref, dx1_ref, g_ref, after_ref, dx_ref, dxb_ref, dg_ref, acc):
        del after_ref
        i, k = pl.program_id(0), pl.program_id(1)

        @pl.when(k == 0)
        def _():
            acc[...] = jnp.zeros_like(acc)

        @pl.when(k < km)
        def _():
            acc[...] += _dot(dpm_ref[...], w_ref[...])

        @pl.when(k >= km)
        def _():
            acc[...] += _dot(dpg_ref[...], w_ref[...])

        @pl.when(k == nk - 1)
        def _():
            r, xh = _rms_stats(x_ref[...])
            dx, dg = _rms_bwd(acc[...], xh, r, g_ref[...])
            dx = dx + dx1_ref[...]
            dx_ref[...] = dx
            dxb_ref[...] = dx.astype(bf16)
            _acc_rows(dg_ref, i == 0, dg)

    row = pl.BlockSpec((tm, D_MODEL), lambda i, k: (i, 0))
    vec = pl.BlockSpec((1, D_MODEL), lambda i, k: (0, 0))
    return pl.pallas_call(
        body, name="dh_bwd", grid=(s // tm, nk),
        in_specs=[pl.BlockSpec((tm, tk), lambda i, k: (i, jnp.minimum(k, km - 1))),
                  pl.BlockSpec((tm, tk), lambda i, k: (i, jnp.maximum(k - km, 0))),
                  pl.BlockSpec((tk, D_MODEL), lambda i, k: (k, 0)), row, row, vec, pl.BlockSpec(memory_space=pl.ANY)],
        out_specs=[row, row, vec], out_shape=[SDS((s, D_MODEL), f32), SDS((s, D_MODEL), bf16), SDS((1, D_MODEL), f32)],
        scratch_shapes=[pltpu.VMEM((tm, D_MODEL), f32)], compiler_params=_cp("arbitrary", "arbitrary"),
    )(dpm, dpg, w_t, x, dx1, g, after)


def _layer_fwd(x, weight, sm):
    p, pg, h = _rms_mm(x, sm["g_mix"], weight("w_in", x))
    z, opre, states = _mixer_fwd(p, sm["lb"], sm["g_out"], sm["w_conv"], sm["ln_g"], sm["ln_b"], sm["w_sg"], sm["b_sg_t"])
    y, merged, x1 = _branch_gate(z, weight("w_branch", z), pg, x, weight("w_o", z))
    x2, h2, ra = _ffn(x1, sm["g_ffn"], weight("w_ff1", x1), weight("w_ff2", x1))
    saved = dict(x=x, p=p, pg=pg, h=h, z=z, opre=opre, states=states, y=y, merged=merged, x1=x1, h2=h2, ra=ra)
    return x2, saved


def _layer_bwd(dx2, dx2b, sv, w, sm, between, before_end):
    nchip = N_DEV // 2
    by_chip = lambda g: g.reshape((nchip, 2) + g.shape[1:])
    da, dx1, dx1b, dg_ffn = _ffn_bwd(dx2, dx2b, sv["x1"], sm["g_ffn"], sv["ra"], w["w_ff1"], w["w_ff2"])
    g_ff2 = by_chip(_mm_tn(sv["ra"], dx2b, 1, D_FF, D_MODEL, 512, 1024, name="dw_ff2", square_a=True)[0]
                    .reshape(N_DEV, D_FF // N_DEV, D_MODEL))
    g_ff1 = by_chip(_mm_tn_slabs(sv["h2"], da, 1, D_MODEL, D_FF // 2, [i * (D_FF // N_DEV) for i in range(nchip)], D_FF // N_DEV,
                                 name="dw_ff1")[:, 0])
    g_o = by_chip(_mm_tn(sv["merged"], dx1b, 1, D_MODEL, D_MODEL, 512, 1024, name="dw_o")[0].reshape(N_DEV, D_MODEL // N_DEV, D_MODEL))
    dy, dpg, dz = _merge_bwd(dx1b, w["w_o"], sv["y"], sv["pg"], w["w_branch"], between(dx1))
    g_branch = by_chip(_mm_tn_slabs(sv["z"], dy, 3, BRANCH, D_MODEL, [i * (D_MODEL // N_DEV) for i in range(N_DEV)], D_MODEL // N_DEV,
                                    name="dw_branch"))
    g_in = _mm_tn(dpg, sv["h"], 1, 3 * D_MODEL, D_MODEL, 768, 1024, name="dw_in_gates", rows=N_COLS, row0=GATE_COL0)
    dpm, vecs, dwsg, dbsg_t = _mixer_bwd(sv["p"], dz, sv["opre"], sv["states"], sm["lb"], sm["g_out"], sm["w_conv"],
                                         sm["ln_g"], sm["ln_b"], sm["w_sg"], sm["b_sg_t"])
    g_in = _mm_tn(dpm, sv["h"], 1, GATE_COL0, D_MODEL, 768, 1024, name="dw_in_mixers", rows=N_COLS, into=g_in)
    g_in = by_chip(g_in[0].reshape(N_DEV, SHARD_IN, D_MODEL))
    big = dict(w_in=g_in, w_branch=g_branch, w_o=g_o, w_ff1=g_ff1, w_ff2=g_ff2)
    dx, dxb, dg_mix = _dh_bwd(dpm, dpg, w["w_in"], sv["x"], dx1, sm["g_mix"], before_end(big))
    small = dict(g_mix=dg_mix, g_ffn=dg_ffn, vecs=vecs, w_sg=dwsg, b_sg_t=dbsg_t, dx1=dx1)
    return dx, dxb, big, small


BIG = ("w_in", "w_branch", "w_o", "w_ff1", "w_ff2")
ANY = pl.BlockSpec(memory_space=pl.ANY)


def _place():
    return lax.axis_index("x"), lax.axis_index("y"), lax.axis_index("c")


def _al(v, m):
    return pl.multiple_of(v * m, m)


def _shard_of(refs, dev, which=range(len(BIG))):
    out = []
    for ref, t in zip(refs, which):
        by_cols = BIG[t] in ("w_branch", "w_ff1")
        n = ref.shape[-1 if by_cols else 0] // N_DEV
        part = pl.ds(_al(dev, n), n)
        out.append(ref.at[(slice(None),) * (len(ref.shape) - 1) + (part,)] if by_cols else ref.at[part])
    return out


def _gather_out_shapes(shards):
    s_in, s_b, s_o, s_1, s_2 = (shards[n] for n in BIG)
    return [SDS((s_in.shape[1] * N_DEV, s_in.shape[2]), bf16), SDS(s_b.shape[1:3] + (s_b.shape[3] * N_DEV,), bf16),
            SDS((s_o.shape[1] * N_DEV, s_o.shape[2]), bf16), SDS((s_1.shape[1], s_1.shape[2] * N_DEV), bf16),
            SDS((s_2.shape[1] * N_DEV, s_2.shape[2]), bf16)]


def _seq_all_gather_layer(layer, which, n_early, shard_refs, out_shapes, tag=""):
    nt = len(which)
    outs = [jax.empty_ref(sh, memory_space=pltpu.MemorySpace.HBM) for sh in out_shapes]
    early, late = tuple(range(n_early)), tuple(range(n_early, nt))

    @pl.kernel(mesh=plsc.ScalarSubcoreMesh(axis_name="seq", num_cores=1), name=f"seq_all_gather_l{layer}{tag}",
               scratch_types=(pltpu.SemaphoreType.DMA((9,)), pltpu.SemaphoreType.DMA((9,))),
               compiler_params=pltpu.CompilerParams(collective_id=1))
    def launch(send_sems, recv_sems):
        x, y, c = _place()
        me, sibling = (x, y, c), (x, y, 1 - c)
        first, second, diag = _ici_route(x, y, c)
        _handshake([sibling, first, second])
        mine = [r.at[layer] for r in shard_refs]

        def copies(k, blk, to, src=None, part=range(nt)):
            dst = _shard_of(outs, 4 * blk[0] + 2 * blk[1] + blk[2], which)
            src = dst if src is None else src
            return [pltpu.make_async_remote_copy(src_ref=src[t], dst_ref=dst[t], send_sem=send_sems.at[k], recv_sem=recv_sems.at[k],
                                                 device_id=to, device_id_type=MESH) for t in part]

        def start(cps):
            for cp in cps:
                cp.start()
            return cps

        def landed(cps):
            for cp in cps:
                cp.wait_recv()

        sent = start(copies(0, me, sibling, src=mine) + copies(1, me, first, src=mine, part=early)
                     + copies(2, me, first, src=mine, part=late) + copies(3, me, second, src=mine))
        landed(copies(1, first, me, part=early))
        sent += start(copies(4, first, second, part=early) + copies(6, first, sibling, part=early))
        landed(copies(2, first, me, part=late))
        sent += start(copies(5, first, second, part=late) + copies(6, first, sibling, part=late))
        landed(copies(3, second, me))
        sent += start(copies(7, second, sibling))
        landed(copies(4, diag, me, part=early) + copies(5, diag, me, part=late))
        sent += start(copies(8, diag, sibling))
        other = lambda p: (p[0], p[1], 1 - c)
        landed(copies(0, sibling, me) + copies(6, other(second), me) + copies(7, other(first), me) + copies(8, other(diag), me))
        for cp in sent:
            cp.wait_send()

    launch()
    return [o[...] for o in outs]


def _ici_route(x, y, c):
    return (x ^ (1 - c), y ^ c, c), (x ^ c, y ^ (1 - c), c), (1 - x, 1 - y, c)


def _place_own(where, which, shards, gathered, after):
    nt = len(which)

    def body(where_ref, *refs):
        del where_ref
        for src, dst in zip(refs[:nt], refs[2 * nt + 1:]):
            dst[...] = src[...]

    in_specs, out_specs = [], []
    for t, sh in zip(which, shards):
        blk = sh.shape[1:]
        in_specs.append(pl.BlockSpec((None,) + blk, functools.partial(lambda nd, i, wh: (wh[0],) + (0,) * nd, len(blk))))
        by_cols = BIG[t] in ("w_branch", "w_ff1")
        out_specs.append(pl.BlockSpec(blk, functools.partial(
            lambda nd, cols, i, wh: (0,) * (nd - 1) + (wh[1],) if cols else (wh[1],) + (0,) * (nd - 1), len(blk), by_cols)))
    return pl.pallas_call(
        body, name="place_own", out_shape=[SDS(g.shape, g.dtype) for g in gathered],
        input_output_aliases={1 + nt + i: i for i in range(nt)}, compiler_params=_cp("arbitrary"),
        grid_spec=pltpu.PrefetchScalarGridSpec(num_scalar_prefetch=1, grid=(1,), in_specs=in_specs + [ANY] * (nt + 1), out_specs=out_specs),
    )(where, *shards, *gathered, after)


def _handshake(peers):
    barrier = pltpu.get_barrier_semaphore()
    for p in peers:
        pl.semaphore_signal(barrier, inc=1, device_id=p, device_id_type=MESH)
    pl.semaphore_wait(barrier, len(peers))


def _seq_exchange_on_chip(grads):
    nt, nchip = len(BIG), N_DEV // 2
    g_refs = [jax.new_ref(g, memory_space=pltpu.MemorySpace.HBM) for g in grads]
    outs = [jax.empty_ref(SDS((nchip,) + g.shape[2:], bf16), memory_space=pltpu.MemorySpace.HBM) for g in grads]

    @pl.kernel(mesh=plsc.ScalarSubcoreMesh(axis_name="seq", num_cores=1), name="seq_rs_on_chip",
               scratch_types=(pltpu.SemaphoreType.DMA((nchip,)), pltpu.SemaphoreType.DMA((nchip,))),
               compiler_params=pltpu.CompilerParams(collective_id=2))
    def launch(send_sems, recv_sems):
        x, y, c = _place()
        sibling = (x, y, 1 - c)
        _handshake([sibling])
        remote = [pltpu.make_async_remote_copy(src_ref=g_refs[t].at[j, 1 - c], dst_ref=outs[t].at[j], send_sem=send_sems.at[j],
                                               recv_sem=recv_sems.at[j], device_id=sibling, device_id_type=MESH)
                  for j in range(nchip) for t in range(nt)]
        for cp in remote:
            cp.start()
        for cp in remote:
            cp.wait_recv()
        for cp in remote:
            cp.wait_send()

    launch()
    return [o[...] for o in outs], [g[...] for g in g_refs]


def _seq_exchange_between_chips(sums):
    nt = len(BIG)
    s_refs = [jax.new_ref(a, memory_space=pltpu.MemorySpace.HBM) for a in sums]
    outs = [jax.empty_ref(SDS((3,) + a.shape[1:], bf16), memory_space=pltpu.MemorySpace.HBM) for a in sums]
    transit = [jax.empty_ref(SDS(a.shape[1:], bf16), memory_space=pltpu.MemorySpace.HBM) for a in sums]

    early, late = (0,), tuple(range(1, nt))

    @pl.kernel(mesh=plsc.ScalarSubcoreMesh(axis_name="seq", num_cores=1), name="seq_rs_between_chips",
               scratch_types=(pltpu.SemaphoreType.DMA((6,)), pltpu.SemaphoreType.DMA((6,))),
               compiler_params=pltpu.CompilerParams(collective_id=3))
    def launch(send_sems, recv_sems):
        x, y, c = _place()
        first, second, diag = _ici_route(x, y, c)
        _handshake([first, second])

        def copies(k, src, dst, to, part=range(nt)):
            return [pltpu.make_async_remote_copy(src_ref=src(t), dst_ref=dst(t), send_sem=send_sems.at[k], recv_sem=recv_sems.at[k],
                                                 device_id=to, device_id_type=MESH) for t in part]

        chip_of = lambda p: 2 * p[0] + p[1]
        for_diag = lambda t: s_refs[t].at[chip_of(diag)]
        through = lambda t: transit[t]
        last = lambda t: outs[t].at[2]
        direct = (copies(0, lambda t: s_refs[t].at[chip_of(first)], lambda t: outs[t].at[0], first)
                  + copies(1, lambda t: s_refs[t].at[chip_of(second)], lambda t: outs[t].at[1], second))
        via = [copies(2, for_diag, through, first, early), copies(3, for_diag, through, first, late)]
        passed = [copies(4, through, last, second, early), copies(5, through, last, second, late)]
        for cp in via[0] + direct + via[1]:
            cp.start()
        for arrived, onward in zip(via, passed):
            for cp in arrived:
                cp.wait_recv()
            for cp in onward:
                cp.start()
        sent = direct + via[0] + via[1] + passed[0] + passed[1]
        for cp in direct + passed[0] + passed[1]:
            cp.wait_recv()
        for cp in sent:
            cp.wait_send()

    launch()
    return [o[...] for o in outs], [a[...] for a in s_refs]


def _chip_sums(core, mine, other, after, steps=2):
    nt, nchip = len(mine), mine[0].shape[0]
    m4 = [a.reshape(nchip, 2, -1, a.shape[-1]) for a in mine]
    o3 = [a.reshape(nchip, -1, a.shape[-1]) for a in other]

    def body(c_ref, *refs):
        del c_ref
        for a_ref, b_ref, o_ref in zip(refs[:nt], refs[nt:2 * nt], refs[2 * nt + 1:]):
            o_ref[...] = (a_ref[...].astype(f32) + b_ref[...].astype(f32)).astype(bf16)

    tiles = [(a.shape[1] // steps, a.shape[2]) for a in o3]
    blks = [pl.BlockSpec((None,) + t, lambda j, i, c_ref: (j, i, 0)) for t in tiles]
    outs = pl.pallas_call(
        body, name="chip_sums", out_shape=[SDS(a.shape, bf16) for a in o3], compiler_params=_cp("parallel", "parallel"),
        grid_spec=pltpu.PrefetchScalarGridSpec(
            num_scalar_prefetch=1, grid=(nchip, steps),
            in_specs=[pl.BlockSpec((None, None) + t, lambda j, i, c_ref: (j, c_ref[0], i, 0)) for t in tiles] + blks + [ANY],
            out_specs=blks),
    )(core, *m4, *o3, after)
    return [o.reshape(a.shape) for o, a in zip(outs, other)]


def _all_reduce_rows(pack):
    rows = pack.shape[0]
    blk = rows // N_DEV

    def body(in_ref, out_ref, land, send1, recv1, send2, recv2):
        x, y, c = _place()
        me = 4 * x + 2 * y + c
        others = [(px, py, pc) for px in range(2) for py in range(2) for pc in range(2)]

        def is_me(p):
            return jnp.logical_and(jnp.logical_and(p[0] == x, p[1] == y), p[2] == c)

        land[me] = in_ref[pl.ds(_al(me, blk), blk), :]
        for d, p in enumerate(others):
            @pl.when(jnp.logical_not(is_me(p)))
            def _():
                pltpu.make_async_remote_copy(src_ref=in_ref.at[pl.ds(d * blk, blk), :], dst_ref=land.at[me], send_sem=send1.at[d],
                                             recv_sem=recv1.at[me], device_id=p, device_id_type=MESH).start()
        for d, p in enumerate(others):
            @pl.when(jnp.logical_not(is_me(p)))
            def _():
                cp = pltpu.make_async_remote_copy(src_ref=in_ref.at[pl.ds(d * blk, blk), :], dst_ref=land.at[d], send_sem=send1.at[d],
                                                  recv_sem=recv1.at[d], device_id=p, device_id_type=MESH)
                cp.wait_recv()
                cp.wait_send()
        total = land[0]
        for d in range(1, N_DEV):
            total = total + land[d]
        out_ref[pl.ds(_al(me, blk), blk), :] = total
        for d, p in enumerate(others):
            @pl.when(jnp.logical_not(is_me(p)))
            def _():
                mine = out_ref.at[pl.ds(_al(me, blk), blk), :]
                pltpu.make_async_remote_copy(src_ref=mine, dst_ref=mine, send_sem=send2.at[d], recv_sem=recv2.at[me],
                                             device_id=p, device_id_type=MESH).start()
        for d, p in enumerate(others):
            @pl.when(jnp.logical_not(is_me(p)))
            def _():
                theirs = out_ref.at[pl.ds(d * blk, blk), :]
                cp = pltpu.make_async_remote_copy(src_ref=theirs, dst_ref=theirs, send_sem=send2.at[d], recv_sem=recv2.at[d],
                                                  device_id=p, device_id_type=MESH)
                cp.wait_recv()
                cp.wait_send()

    vm = pl.BlockSpec(memory_space=pltpu.VMEM)
    return pl.pallas_call(
        body, name="all_reduce_rows", in_specs=[vm], out_specs=vm, out_shape=SDS((rows, LANE), f32),
        scratch_shapes=[pltpu.VMEM((N_DEV, blk, LANE), f32)] + [pltpu.SemaphoreType.DMA((N_DEV,))] * 4,
        compiler_params=pltpu.CompilerParams(vmem_limit_bytes=VMEM_LIMIT),
    )(pack)


def _lower_bounds_fwd(lower, after):
    def body(l_ref, *rest):
        o_ref = rest[-1]
        sm = _layer_softmax(l_ref)
        run = jnp.zeros_like(sm[0])
        for l in range(DEPTH):
            o_ref[l:l + 1, :] = run
            if l + 1 < DEPTH:
                run = run + sm[l + 1]

    vm = pl.BlockSpec(memory_space=pltpu.VMEM)
    return pl.pallas_call(body, name="lower_bounds_fwd", in_specs=[vm] + [ANY] * len(after), out_specs=vm,
                          out_shape=SDS(lower.shape, f32))(lower, *after)


def _layer_softmax(l_ref):
    rows = [l_ref[l:l + 1, :] for l in range(DEPTH)]
    top = functools.reduce(jnp.maximum, rows)
    e = [jnp.exp(r - top) for r in rows]
    tot = functools.reduce(lambda a, b: a + b, e)
    return [v / tot for v in e]


def _lower_bounds_bwd(lower, dlbs):
    def body(l_ref, d_ref, o_ref):
        sm = _layer_softmax(l_ref)
        dsm = [None] * DEPTH
        run = jnp.zeros_like(sm[0])
        dsm[0] = run
        for l in reversed(range(1, DEPTH)):
            run = run + d_ref[l:l + 1, :]
            dsm[l] = run
        inner = functools.reduce(lambda a, b: a + b, [sm[l] * dsm[l] for l in range(DEPTH)])
        for l in range(DEPTH):
            o_ref[l:l + 1, :] = sm[l] * (dsm[l] - inner)

    return pl.pallas_call(body, name="lower_bounds_bwd", out_shape=SDS(lower.shape, f32))(lower, dlbs)


_ADAM_C1 = 1.0 - ADAM_B1 ** ADAM_STEP
_ADAM_C2 = 1.0 - ADAM_B2 ** ADAM_STEP


def _adamw(w, g, m, v):
    m = ADAM_B1 * m + (1.0 - ADAM_B1) * g
    v = ADAM_B2 * v + (1.0 - ADAM_B2) * (g * g)
    delta = -ADAM_LR * ((m / _ADAM_C1) / (jnp.sqrt(v / _ADAM_C2) + ADAM_EPS) + ADAM_WD * w)
    return delta, m, v


def _adam_big(where, names, w, m, v, sums, landed, outs, after, steps=4):
    nt = len(names)
    three = lambda a: a.reshape(a.shape[0], -1, a.shape[-1])
    w3, m3, v3 = ([three(d[n]) for n in names] for d in (w, m, v))
    outs3 = [three(a) for n in names for a in outs[n]]
    sums3 = [three(a) for a in sums]
    land3 = [three(a) for a in landed]

    def body(where_ref, *refs):
        del where_ref
        o_refs = refs[5 * nt + 4 * nt + 1:]
        for t in range(nt):
            w_ref, m_ref, v_ref, sum_ref, land_ref = (refs[q * nt + t] for q in range(5))
            g = sum_ref[...].astype(f32)
            for k in range(3):
                g = g + land_ref[k].astype(f32)
            delta, nm, nv = _adamw(w_ref[...], g, m_ref[...], v_ref[...])
            for o_ref, val in zip(o_refs[4 * t:4 * t + 4], (g, delta, nm, nv)):
                o_ref[...] = val

    tiles = [(a.shape[1] // steps, a.shape[2]) for a in w3]
    own = [pl.BlockSpec((None,) + t, lambda i, wh: (wh[0], i, 0)) for t in tiles]
    res = pl.pallas_call(
        body, name="adam_big", out_shape=[SDS(a.shape, f32) for a in outs3],
        input_output_aliases={1 + 5 * nt + i: i for i in range(4 * nt)}, compiler_params=_cp("parallel"),
        grid_spec=pltpu.PrefetchScalarGridSpec(
            num_scalar_prefetch=1, grid=(steps,),
            in_specs=own * 3 + [pl.BlockSpec((None,) + t, lambda i, wh: (wh[1], i, 0)) for t in tiles]
            + [pl.BlockSpec((3,) + t, lambda i, wh: (0, i, 0)) for t in tiles] + [ANY] * (4 * nt + 1),
            out_specs=[s for s in own for _ in range(4)]),
    )(where, *w3, *m3, *v3, *sums3, *land3, *outs3, after)
    return {n: [o.reshape(w[n].shape) for o in res[4 * t:4 * t + 4]] for t, n in enumerate(names)}


def _touch(a, after):
    a2 = a.reshape(-1, a.shape[-1])

    def body(a_ref, after_ref, o_ref):
        del after_ref
        o_ref[...] = a_ref[0:8, :].astype(f32)

    return pl.pallas_call(
        body, name="touch", grid=(1,), in_specs=[pl.BlockSpec((16, LANE), lambda i: (0, 0)), ANY],
        out_specs=pl.BlockSpec((8, LANE), lambda i: (0, 0)), out_shape=SDS((8, LANE), f32),
    )(a2, after)


def _adam_rows(w, g, m, v):
    def body(w_ref, g_ref, m_ref, v_ref, d_ref, nm_ref, nv_ref):
        delta, nm, nv = _adamw(w_ref[...], g_ref[...], m_ref[...], v_ref[...])
        d_ref[...] = delta
        nm_ref[...] = nm
        nv_ref[...] = nv

    return pl.pallas_call(body, name="adam_rows", out_shape=[SDS(w.shape, f32)] * 3)(w, g, m, v)


SMALL = ("g_mix", "lower_bounds", "g_hgrn_out", "w_conv", "sg_ln_g", "sg_ln_b", "w_sg", "b_sg", "g_ffn", "g_final")
WEIGHTS = ("w_in", "g_mix", "lower_bounds", "g_hgrn_out", "w_conv", "sg_ln_g", "sg_ln_b", "w_sg", "b_sg", "w_branch", "w_o", "g_ffn",
           "w_ff1", "w_ff2", "g_final")


def _pack_rows(arrays, multiple):
    flat = jnp.concatenate([a.reshape(-1) for a in arrays])
    rows = -(-flat.shape[0] // (LANE * multiple)) * multiple
    return jnp.pad(flat, (0, rows * LANE - flat.shape[0])).reshape(rows, LANE)


def _unpack_rows(pack, like):
    flat = pack.reshape(-1)
    out, at = [], 0
    for a in like:
        out.append(flat[at:at + a.size].reshape(a.shape))
        at += a.size
    return out


def kernel(x, w_in, g_mix, lower_bounds, g_hgrn_out, w_conv, sg_ln_g, sg_ln_b, w_sg, b_sg, w_branch, w_o, g_ffn, w_ff1, w_ff2, g_final, loss_target, m_w_in, m_g_mix, m_lower_bounds, m_g_hgrn_out, m_w_conv, m_sg_ln_g, m_sg_ln_b, m_w_sg, m_b_sg, m_w_branch, m_w_o, m_g_ffn, m_w_ff1, m_w_ff2, m_g_final, v_w_in, v_g_mix, v_lower_bounds, v_g_hgrn_out, v_w_conv, v_sg_ln_g, v_sg_ln_b, v_w_sg, v_b_sg, v_w_branch, v_w_o, v_g_ffn, v_w_ff1, v_w_ff2, v_g_final):
    weights = dict(w_in=w_in, g_mix=g_mix, lower_bounds=lower_bounds, g_hgrn_out=g_hgrn_out, w_conv=w_conv, sg_ln_g=sg_ln_g,
                   sg_ln_b=sg_ln_b, w_sg=w_sg, b_sg=b_sg, w_branch=w_branch, w_o=w_o, g_ffn=g_ffn, w_ff1=w_ff1, w_ff2=w_ff2, g_final=g_final)
    mom1 = dict(w_in=m_w_in, g_mix=m_g_mix, lower_bounds=m_lower_bounds, g_hgrn_out=m_g_hgrn_out, w_conv=m_w_conv, sg_ln_g=m_sg_ln_g,
                sg_ln_b=m_sg_ln_b, w_sg=m_w_sg, b_sg=m_b_sg, w_branch=m_w_branch, w_o=m_w_o, g_ffn=m_g_ffn, w_ff1=m_w_ff1, w_ff2=m_w_ff2,
                g_final=m_g_final)
    mom2 = dict(w_in=v_w_in, g_mix=v_g_mix, lower_bounds=v_lower_bounds, g_hgrn_out=v_g_hgrn_out, w_conv=v_w_conv, sg_ln_g=v_sg_ln_g,
                sg_ln_b=v_sg_ln_b, w_sg=v_w_sg, b_sg=v_b_sg, w_branch=v_w_branch, w_o=v_w_o, g_ffn=v_g_ffn, w_ff1=v_w_ff1, w_ff2=v_w_ff2,
                g_final=v_g_final)
    xi, yi, ci = _place()
    dev = 4 * xi + 2 * yi + ci
    conv_cols = w_conv.shape[-1]

    for d in (weights, mom1, mom2):
        d["w_in"] = jnp.swapaxes(d["w_in"], 1, 2)
    shards = {n: weights[n].astype(bf16) for n in BIG}

    conv_place = lax.dynamic_update_slice(jnp.zeros((DEPTH, 3, BRANCH), f32), w_conv, (0, 0, dev * conv_cols))
    (w_conv_full,) = _unpack_rows(_all_reduce_rows(_pack_rows([conv_place], 8 * N_DEV)), [conv_place])
    small_packs = [_pack_rows([d[n] for n in SMALL], 8) for d in (weights, mom1, mom2)]
    lbs = _lower_bounds_fwd(lower_bounds, small_packs)

    def small_of(l):
        return dict(g_mix=g_mix[l][None], lb=lbs[l][None], g_out=g_hgrn_out[l][None], w_conv=w_conv_full[l], ln_g=sg_ln_g[l][None],
                    ln_b=sg_ln_b[l][None], w_sg=w_sg[l], b_sg_t=b_sg[l].T, g_ffn=g_ffn[l][None])

    act = x[0]
    full, saved = [], []
    shard_refs = [jax.new_ref(shards[n], memory_space=pltpu.MemorySpace.HBM) for n in BIG]
    shapes = _gather_out_shapes(shards)
    groups = [g for l in range(DEPTH) for g in ((l, (0,), 1, "a"), (l, (1, 2, 3, 4), 2, "b"))]
    arrived = {}
    for l, which, n_early, tag in groups:
        got = _seq_all_gather_layer(l, which, n_early, [shard_refs[t] for t in which], [shapes[t] for t in which], tag)
        arrived.update({(l, BIG[t]): (which, got) for t in which})

    for l in range(DEPTH):
        full.append({})

        def weight(name, after, l=l):
            if name not in full[l]:
                which, got = arrived[(l, name)]
                where = jnp.stack([jnp.int32(l), dev.astype(jnp.int32)])
                full[l].update(zip([BIG[t] for t in which], _place_own(where, which, [shards[BIG[t]] for t in which], got, after)))
            return full[l][name]

        act, sv = _layer_fwd(act, weight, small_of(l))
        saved.append(sv)
    loss_row, dx, dxb, dg_final = _final(act, loss_target[0], g_final[None])

    core = ci.astype(jnp.int32)[None]
    big_out = {n: [lax.empty(weights[n].shape, f32) for _ in range(4)] for n in BIG}
    small_grads = [None] * DEPTH

    def chip_sums(stage, after):
        l, received, mine = stage
        sums = _chip_sums(core, mine, received, after)
        placed.append(sums[BIG.index("w_o")])
        landed, sums = _seq_exchange_between_chips(sums)
        return l, sums, landed

    def adam_layer(stage, after):
        l, sums, landed = stage
        where = jnp.stack([jnp.int32(l), (2 * xi + yi).astype(jnp.int32)])
        big_out.update(_adam_big(where, BIG, weights, mom1, mom2, sums, landed, big_out, after))

    above = None
    placed = []
    for l in reversed(range(DEPTH)):
        summed = []

        def between(dx1):
            if above is None:
                return dx1
            summed.append(chip_sums(above, dx1))
            return placed[-1]

        def before_end(big):
            return _touch(summed[0][2][BIG.index("w_o")], big["w_in"]) if summed else big["w_in"]

        dx, dxb, big, small_grads[l] = _layer_bwd(dx, dxb, saved[l], full[l], small_of(l), between, before_end)
        if summed:
            adam_layer(summed[0], dx)
        above = (l, *_seq_exchange_on_chip([big[n] for n in BIG]))

    stack = lambda f: jnp.stack([f(small_grads[l]) for l in range(DEPTH)])
    d_lower = _lower_bounds_bwd(lower_bounds, stack(lambda s: s["vecs"][0]))
    local_small = dict(g_mix=stack(lambda s: s["g_mix"][0]), lower_bounds=d_lower, g_hgrn_out=stack(lambda s: s["vecs"][1]),
                       w_conv=stack(lambda s: s["vecs"][4:7]), sg_ln_g=stack(lambda s: s["vecs"][2]), sg_ln_b=stack(lambda s: s["vecs"][3]),
                       w_sg=stack(lambda s: s["w_sg"]), b_sg=stack(lambda s: s["b_sg_t"].T), g_ffn=stack(lambda s: s["g_ffn"][0]),
                       g_final=dg_final[0])
    order = [local_small[n] for n in SMALL] + [loss_row]
    *reduced, loss_sum = _unpack_rows(_all_reduce_rows(_pack_rows(order, 8 * N_DEV)), order)
    loss = loss_sum[0, 0]
    grads = dict(zip(SMALL, reduced))
    grads["w_conv"] = lax.dynamic_slice(grads["w_conv"], (0, 0, dev * conv_cols), (DEPTH, 3, conv_cols))

    deltas, new_m, new_v = {}, {}, {}
    like = [weights[n] for n in SMALL]
    small_out = _adam_rows(small_packs[0], _pack_rows([grads[n] for n in SMALL], 8), small_packs[1], small_packs[2])
    for out, pack in zip((deltas, new_m, new_v), small_out):
        out.update(zip(SMALL, _unpack_rows(pack, like)))
    adam_layer(chip_sums(above, dx), small_out[0])
    for n in BIG:
        grads[n], deltas[n], new_m[n], new_v[n] = (jnp.swapaxes(a, 1, 2) if n == "w_in" else a for a in big_out[n])

    return (loss, dx[None], *[grads[n] for n in WEIGHTS], *[deltas[n] for n in WEIGHTS], *[new_m[n] for n in WEIGHTS],
            *[new_v[n] for n in WEIGHTS])
```

```python
import functools

import jax
import jax.numpy as jnp
from jax import lax
from jax.experimental import pallas as pl
from jax.experimental.pallas import tpu as pltpu
from jax.experimental.pallas import tpu_sc as plsc

f32 = jnp.float32
bf16 = jnp.bfloat16
SDS = jax.ShapeDtypeStruct
MESH = pl.DeviceIdType.MESH

D_MODEL = 1024
BRANCH = 512
N_COLS = 7680
D_FF = 4096
DEPTH = 4
HEADS = 4
HEAD_DIM = 128
HGRN_CHUNK = 64
SG_CHUNK = 128
SG_GROUPS = 4
NORM_EPS = 1e-6
LN_EPS = 1e-5
LB_FLOOR = 1e-30
N_DEV = 8
SHARD_IN = N_COLS // N_DEV
LANE = 128
GATE_COL0 = 9 * BRANCH

ADAM_LR = 0.001
ADAM_B1 = 0.9
ADAM_B2 = 0.999
ADAM_EPS = 1e-08
ADAM_WD = 0.01
ADAM_STEP = 10

MIX_TILE = 256
VMEM_LIMIT = 56 * 1024 * 1024


def _cp(*sem):
    return pltpu.CompilerParams(dimension_semantics=sem or None, vmem_limit_bytes=VMEM_LIMIT)


def _dot(a, b):
    return jnp.dot(a, b, preferred_element_type=f32)


def _dot_nt(a, b):
    return lax.dot_general(a, b, (((1,), (1,)), ((), ())), preferred_element_type=f32)


def _dot_tn(a, b):
    return lax.dot_general(a, b, (((0,), (0,)), ((), ())), preferred_element_type=f32)


def _dot_exact(ones, b):
    hi = b.astype(bf16)
    rest = b - hi.astype(f32)
    mid = rest.astype(bf16)
    low = (rest - mid.astype(f32)).astype(bf16)
    ones = ones.astype(bf16)
    return _dot(ones, hi) + _dot(ones, mid) + _dot(ones, low)


def _sigmoid(x):
    return jax.nn.sigmoid(x)


_GELU_C = 0.7978845608028654
_GELU_A = 0.044715


def _gelu(x):
    return 0.5 * x * (1.0 + jnp.tanh(_GELU_C * (x + _GELU_A * x * x * x)))


def _gelu_grad(x):
    x2 = x * x
    t = jnp.tanh(_GELU_C * (x + _GELU_A * x * x2))
    return 0.5 * (1.0 + t) + 0.5 * x * (1.0 - t * t) * _GELU_C * (1.0 + 3.0 * _GELU_A * x2)


def _rms_stats(x):
    r = lax.rsqrt(jnp.mean(x * x, axis=-1, keepdims=True) + NORM_EPS)
    return r, x * r


def _rms_bwd(dh, xh, r, g):
    dg = jnp.sum(dh * xh, axis=0, keepdims=True)
    dxn = dh * g
    dx = r * (dxn - xh * jnp.mean(dxn * xh, axis=-1, keepdims=True))
    return dx, dg


def _tri(n, upper=False):
    r = lax.broadcasted_iota(jnp.int32, (n, n), 0)
    c = lax.broadcasted_iota(jnp.int32, (n, n), 1)
    return (c >= r) if upper else (c <= r)


def _acc_rows(ref, first, val):
    @pl.when(first)
    def _():
        ref[...] = val

    @pl.when(jnp.logical_not(first))
    def _():
        ref[...] += val


def _rms_mm(x, g, w_t, tm=1024, tn=1536):
    s, n = x.shape[0], w_t.shape[0]
    jm = GATE_COL0 // tn

    def body(x_ref, g_ref, w_ref, pm_ref, pg_ref, h_ref, hs):
        j = pl.program_id(1)

        @pl.when(j == 0)
        def _():
            _, xh = _rms_stats(x_ref[...])
            hv = (xh * g_ref[...]).astype(bf16)
            hs[...] = hv
            h_ref[...] = hv

        res = _dot_nt(hs[...], w_ref[...])

        @pl.when(j < jm)
        def _():
            pm_ref[...] = res

        @pl.when(j >= jm)
        def _():
            pg_ref[...] = res.astype(bf16)

    return pl.pallas_call(
        body, name="rms_mm", grid=(s // tm, n // tn),
        in_specs=[pl.BlockSpec((tm, D_MODEL), lambda i, j: (i, 0)), pl.BlockSpec((1, D_MODEL), lambda i, j: (0, 0)),
                  pl.BlockSpec((tn, D_MODEL), lambda i, j: (j, 0))],
        out_specs=[pl.BlockSpec((tm, tn), lambda i, j: (i, jnp.minimum(j, jm - 1))),
                   pl.BlockSpec((tm, tn), lambda i, j: (i, jnp.maximum(j - jm, 0))), pl.BlockSpec((tm, D_MODEL), lambda i, j: (i, 0))],
        out_shape=[SDS((s, GATE_COL0), f32), SDS((s, n - GATE_COL0), bf16), SDS((s, D_MODEL), bf16)],
        scratch_shapes=[pltpu.VMEM((tm, D_MODEL), bf16)], compiler_params=_cp("parallel", "arbitrary"),
    )(x, g, w_t)


def _hgrn_gates(fp, lb):
    logf = jnp.logaddexp(jnp.log(jnp.maximum(lb, LB_FLOOR)), jnp.log1p(-lb) + jax.nn.log_sigmoid(fp))
    snf = _sigmoid(-fp)
    return logf, snf, (1.0 - lb) * snf


def _p_specs(tile, cols, row_map):
    return [pl.BlockSpec((tile, BRANCH), functools.partial(lambda c, i: (row_map(i), c), c)) for c in cols]


def _mixer_fwd(p, lb, gout, wconv, lng, lnb, wsg, bsg_t):
    s = p.shape[0]
    tt = MIX_TILE
    nch = tt // HGRN_CHUNK

    def body(q_ref, fp_ref, iv_ref, go_ref, bg_ref, cg_ref, xc_ref, u_ref, v_ref, lb_ref, gout_ref, wconv_ref, lng_ref,
             lnb_ref, wsg_ref, bsg_ref, z_ref, opre_ref, st_ref, st_scr, zbuf):
        @pl.when(pl.program_id(0) == 0)
        def _():
            st_scr[...] = jnp.zeros_like(st_scr)
            zbuf[0:8, :] = jnp.zeros((8, BRANCH), f32)

        lbv = lb_ref[...]
        gout_v = gout_ref[...]
        causal = _tri(HGRN_CHUNK)
        tri = causal.astype(f32)
        last_row = lax.broadcasted_iota(jnp.int32, (HGRN_CHUNK, 1), 0) == HGRN_CHUNK - 1
        for c in range(nch):
            rows = slice(HGRN_CHUNK * c, HGRN_CHUNK * (c + 1))
            q_raw = q_ref[rows, :]
            qs = q_raw * _sigmoid(q_raw)
            logf, _, kk = _hgrn_gates(fp_ref[rows, :], lbv)
            b = _dot_exact(tri, logf)
            bl = jnp.sum(jnp.where(last_row, b, 0.0), axis=0, keepdims=True)
            qb = (qs * jnp.exp(b)).astype(bf16)
            kb = (kk * jnp.exp(-b)).astype(bf16)
            kd = (kk * jnp.exp(bl - b)).astype(bf16)
            ebl = jnp.exp(bl)
            vc = iv_ref[rows, :].astype(bf16)
            gate = _sigmoid(go_ref[rows, :])
            for h in range(HEADS):
                sl = slice(HEAD_DIM * h, HEAD_DIM * (h + 1))
                st = st_scr[h]
                st_ref[c, h] = st
                a = jnp.where(causal, _dot_nt(qb[:, sl], kb[:, sl]), 0.0)
                o = _dot(a.astype(bf16), vc[:, sl]) + _dot_nt(qb[:, sl], st.astype(bf16))
                opre_ref[rows, sl] = o
                st_scr[h] = st * ebl[:, sl] + _dot_tn(vc[:, sl], kd[:, sl])
                _, oh = _rms_stats(o)
                z_ref[rows, sl] = (oh * gout_v[:, sl] * gate[:, sl]).astype(bf16)

        zc = cg_ref[...] * xc_ref[...]
        zbuf[8:8 + tt, :] = zc
        y = wconv_ref[0:1, :] * zbuf[pl.ds(6, tt), :] + wconv_ref[1:2, :] * zbuf[pl.ds(7, tt), :] + wconv_ref[2:3, :] * zc
        z_ref[:, BRANCH:2 * BRANCH] = (bg_ref[...] * y).astype(bf16)
        zbuf[0:8, :] = zbuf[tt:tt + 8, :]

        lng_v, lnb_v = lng_ref[...], lnb_ref[...]
        low = _tri(SG_CHUNK)
        wms = [jnp.where(low, wsg_ref[g], 0.0).astype(bf16) for g in range(SG_GROUPS)]
        for cc in range(tt // SG_CHUNK):
            rows = slice(SG_CHUNK * cc, SG_CHUNK * (cc + 1))
            ug = _gelu(u_ref[rows, :])
            vg = _gelu(v_ref[rows, :])
            vcen = vg - jnp.mean(vg, axis=-1, keepdims=True)
            rstd = lax.rsqrt(jnp.mean(vcen * vcen, axis=-1, keepdims=True) + LN_EPS)
            vn = (vcen * rstd * lng_v + lnb_v).astype(bf16)
            for g in range(SG_GROUPS):
                sl = slice(LANE * g, LANE * (g + 1))
                sv = _dot(wms[g], vn[:, sl]) + bsg_ref[:, g:g + 1]
                z_ref[rows, 2 * BRANCH + LANE * g:2 * BRANCH + LANE * (g + 1)] = (ug[:, sl] * sv).astype(bf16)

    full = lambda shape: pl.BlockSpec(shape, lambda i: (0,) * len(shape))
    return pl.pallas_call(
        body, name="mixer_fwd", grid=(s // tt,),
        in_specs=_p_specs(tt, range(9), lambda i: i) + [full((1, BRANCH)), full((1, BRANCH)), full((3, BRANCH)), full((1, BRANCH)),
                                                        full((1, BRANCH)), full((SG_GROUPS, SG_CHUNK, SG_CHUNK)), full((SG_CHUNK, SG_GROUPS))],
        out_specs=[pl.BlockSpec((tt, 3 * BRANCH), lambda i: (i, 0)), pl.BlockSpec((tt, BRANCH), lambda i: (i, 0)),
                   pl.BlockSpec((nch, HEADS, HEAD_DIM, HEAD_DIM), lambda i: (i, 0, 0, 0))],
        out_shape=[SDS((s, 3 * BRANCH), bf16), SDS((s, BRANCH), f32), SDS((s // HGRN_CHUNK, HEADS, HEAD_DIM, HEAD_DIM), f32)],
        scratch_shapes=[pltpu.VMEM((HEADS, HEAD_DIM, HEAD_DIM), f32), pltpu.VMEM((tt + 8, BRANCH), f32)],
        compiler_params=_cp("arbitrary"),
    )(*([p] * 9), lb, gout, wconv, lng, lnb, wsg, bsg_t)


def _branch_gate(z, wb, pg, x, wo, tm=512):
    s = z.shape[0]

    def body(z_ref, wb_ref, g_ref, x_ref, wo_ref, y_ref, m_ref, x1_ref):
        acc = None
        for n in range(3):
            cols = slice(D_MODEL * n, D_MODEL * (n + 1))
            yn = _dot(z_ref[:, BRANCH * n:BRANCH * (n + 1)], wb_ref[n])
            y_ref[:, cols] = yn.astype(bf16)
            t = _sigmoid(g_ref[:, cols].astype(f32)) * yn
            acc = t if acc is None else acc + t
        merged = acc.astype(bf16)
        m_ref[...] = merged
        x1_ref[...] = x_ref[...] + _dot(merged, wo_ref[...])

    row = pl.BlockSpec((tm, D_MODEL), lambda i: (i, 0))
    wide = pl.BlockSpec((tm, 3 * D_MODEL), lambda i: (i, 0))
    return pl.pallas_call(
        body, name="branch_gate", grid=(s // tm,),
        in_specs=[pl.BlockSpec((tm, 3 * BRANCH), lambda i: (i, 0)), pl.BlockSpec((3, BRANCH, D_MODEL), lambda i: (0, 0, 0)), wide, row,
                  pl.BlockSpec((D_MODEL, D_MODEL), lambda i: (0, 0))],
        out_specs=[wide, row, row],
        out_shape=[SDS((s, 3 * D_MODEL), bf16), SDS((s, D_MODEL), bf16), SDS((s, D_MODEL), f32)], compiler_params=_cp("parallel"),
    )(z, wb, pg, x, wo)


def _ffn(x1, g, w1, w2, tm=1024, tf=1024):
    s = x1.shape[0]
    nf = D_FF // tf

    def body(x_ref, g_ref, w1_ref, w2_ref, o_ref, h_ref, ra_ref, hs, acc):
        f = pl.program_id(1)

        @pl.when(f == 0)
        def _():
            _, xh = _rms_stats(x_ref[...])
            hv = (xh * g_ref[...]).astype(bf16)
            hs[...] = hv
            h_ref[...] = hv
            acc[...] = jnp.zeros_like(acc)

        ra = jnp.maximum(_dot(hs[...], w1_ref[...]), 0.0)
        ra_ref[...] = ra.astype(bf16)
        acc[...] += _dot((ra * ra).astype(bf16), w2_ref[...])

        @pl.when(f == nf - 1)
        def _():
            o_ref[...] = x_ref[...] + acc[...]

    return pl.pallas_call(
        body, name="ffn", grid=(s // tm, nf),
        in_specs=[pl.BlockSpec((tm, D_MODEL), lambda i, f: (i, 0)), pl.BlockSpec((1, D_MODEL), lambda i, f: (0, 0)),
                  pl.BlockSpec((D_MODEL, tf), lambda i, f: (0, f)), pl.BlockSpec((tf, D_MODEL), lambda i, f: (f, 0))],
        out_specs=[pl.BlockSpec((tm, D_MODEL), lambda i, f: (i, 0)), pl.BlockSpec((tm, D_MODEL), lambda i, f: (i, 0)),
                   pl.BlockSpec((tm, tf), lambda i, f: (i, f))],
        out_shape=[SDS((s, D_MODEL), f32), SDS((s, D_MODEL), bf16), SDS((s, D_FF), bf16)],
        scratch_shapes=[pltpu.VMEM((tm, D_MODEL), bf16), pltpu.VMEM((tm, D_MODEL), f32)], compiler_params=_cp("parallel", "arbitrary"),
    )(x1, g, w1, w2)


def _final(x, target, g, tm=512):
    s = x.shape[0]

    def body(x_ref, t_ref, g_ref, loss_ref, dx_ref, dxb_ref, dg_ref):
        first = pl.program_id(0) == 0
        gv = g_ref[...]
        r, xh = _rms_stats(x_ref[...])
        e = xh * gv - t_ref[...]
        tile_loss = 0.5 * jnp.sum(jnp.mean(e * e, axis=-1, keepdims=True), axis=0, keepdims=True)
        dx, dg = _rms_bwd(e * (1.0 / D_MODEL), xh, r, gv)
        dx_ref[...] = dx
        dxb_ref[...] = dx.astype(bf16)
        _acc_rows(dg_ref, first, dg)
        _acc_rows(loss_ref, first, jnp.broadcast_to(tile_loss, (1, LANE)))

    row = pl.BlockSpec((tm, D_MODEL), lambda i: (i, 0))
    return pl.pallas_call(
        body, name="final_loss", grid=(s // tm,), in_specs=[row, row, pl.BlockSpec((1, D_MODEL), lambda i: (0, 0))],
        out_specs=[pl.BlockSpec((1, LANE), lambda i: (0, 0)), row, row, pl.BlockSpec((1, D_MODEL), lambda i: (0, 0))],
        out_shape=[SDS((1, LANE), f32), SDS((s, D_MODEL), f32), SDS((s, D_MODEL), bf16), SDS((1, D_MODEL), f32)],
        compiler_params=_cp("arbitrary"),
    )(x, target, g)


def _ffn_bwd(dx2, dx2b, x1, g, ra, w1, w2, tm=512, tf=2048):
    s = x1.shape[0]
    nf = D_FF // tf

    def body(dx_ref, dxb_ref, x_ref, g_ref, ra_ref, w1_ref, w2_ref, da_ref, dx1_ref, dx1b_ref, dg_ref, acc):
        i, f = pl.program_id(0), pl.program_id(1)

        @pl.when(f == 0)
        def _():
            acc[...] = jnp.zeros_like(acc)

        da = (_dot_nt(dxb_ref[...], w2_ref[...]) * (2.0 * ra_ref[...].astype(f32))).astype(bf16)
        da_ref[...] = da
        acc[...] += _dot_nt(da, w1_ref[...])

        @pl.when(f == nf - 1)
        def _():
            r, xh = _rms_stats(x_ref[...])
            dx, dg = _rms_bwd(acc[...], xh, r, g_ref[...])
            dx = dx + dx_ref[...]
            dx1_ref[...] = dx
            dx1b_ref[...] = dx.astype(bf16)
            _acc_rows(dg_ref, i == 0, dg)

    row = pl.BlockSpec((tm, D_MODEL), lambda i, f: (i, 0))
    col = pl.BlockSpec((tm, tf), lambda i, f: (i, f))
    return pl.pallas_call(
        body, name="ffn_bwd", grid=(s // tm, nf),
        in_specs=[row, row, row, pl.BlockSpec((1, D_MODEL), lambda i, f: (0, 0)), col,
                  pl.BlockSpec((D_MODEL, tf), lambda i, f: (0, f)), pl.BlockSpec((tf, D_MODEL), lambda i, f: (f, 0))],
        out_specs=[col, row, row, pl.BlockSpec((1, D_MODEL), lambda i, f: (0, 0))],
        out_shape=[SDS((s, D_FF), bf16), SDS((s, D_MODEL), f32), SDS((s, D_MODEL), bf16), SDS((1, D_MODEL), f32)],
        scratch_shapes=[pltpu.VMEM((tm, D_MODEL), f32)], compiler_params=_cp("arbitrary", "arbitrary"),
    )(dx2, dx2b, x1, g, ra, w1, w2)


def _mm_tn(a, b, nb, m, n, tm, tn, name="mm_tn", rows=None, row0=0, into=None, square_a=False):
    s = a.shape[0]
    mi, nj = m // tm, n // tn
    rows = m if rows is None else rows
    blk0 = row0 // tm

    def body(a_ref, b_ref, *rest):
        av = a_ref[...]
        if square_a:
            av = av.astype(f32)
            av = (av * av).astype(bf16)
        rest[-1][...] = _dot_tn(av, b_ref[...]).astype(bf16)

    extra = {} if into is None else dict(input_output_aliases={2: 0})
    return pl.pallas_call(
        body, name=name, grid=(nb, mi, nj),
        in_specs=[pl.BlockSpec((s, tm), lambda k, i, j: (0, k * mi + i)), pl.BlockSpec((s, tn), lambda k, i, j: (0, k * nj + j))]
        + ([] if into is None else [pl.BlockSpec(memory_space=pl.ANY)]),
        out_specs=pl.BlockSpec((None, tm, tn), lambda k, i, j: (k, blk0 + i, j)), out_shape=SDS((nb, rows, n), bf16),
        compiler_params=_cp("parallel", "parallel", "parallel"), **extra,
    )(a, b, *([] if into is None else [into]))


def _mm_tn_slabs(a, b, nb, m, nblk, rel, width, tm=512, name="mm_tn_slabs"):
    s = a.shape[0]
    n = b.shape[1] // nb
    ng, mi, nw = n // nblk, m // tm, len(rel)

    def body(a_ref, b_ref, o_ref):
        full = _dot_tn(a_ref[...], b_ref[...])
        for r, start in enumerate(rel):
            o_ref[r] = full[:, start:start + width].astype(bf16)

    return pl.pallas_call(
        body, name=name, grid=(nb, ng, mi),
        in_specs=[pl.BlockSpec((s, tm), lambda k, g, i: (0, k * mi + i)), pl.BlockSpec((s, nblk), lambda k, g, i: (0, k * ng + g))],
        out_specs=pl.BlockSpec((nw, None, tm, width), lambda k, g, i: (g, k, i, 0)), out_shape=SDS((ng * nw, nb, m, width), bf16),
        compiler_params=_cp("parallel", "parallel", "parallel"),
    )(a, b)


def _merge_bwd(dx1b, wo, y, pg, wb, after, tm=512):
    s = dx1b.shape[0]

    def body(dx_ref, wo_ref, y_ref, g_ref, wb_ref, after_ref, dy_ref, dg_ref, dz_ref):
        del after_ref
        dm = _dot_nt(dx_ref[...], wo_ref[...])
        for n in range(3):
            cols = slice(D_MODEL * n, D_MODEL * (n + 1))
            gate = _sigmoid(g_ref[:, cols].astype(f32))
            t = dm * gate
            dy = t.astype(bf16)
            dy_ref[:, cols] = dy
            dg_ref[:, cols] = (t * y_ref[:, cols].astype(f32) * (1.0 - gate)).astype(bf16)
            dz_ref[:, BRANCH * n:BRANCH * (n + 1)] = _dot_nt(dy, wb_ref[n]).astype(bf16)

    wide = pl.BlockSpec((tm, 3 * D_MODEL), lambda i: (i, 0))
    return pl.pallas_call(
        body, name="merge_bwd", grid=(s // tm,),
        in_specs=[pl.BlockSpec((tm, D_MODEL), lambda i: (i, 0)), pl.BlockSpec((D_MODEL, D_MODEL), lambda i: (0, 0)), wide, wide,
                  pl.BlockSpec((3, BRANCH, D_MODEL), lambda i: (0, 0, 0)), pl.BlockSpec(memory_space=pl.ANY)],
        out_specs=[wide, wide, pl.BlockSpec((tm, 3 * BRANCH), lambda i: (i, 0))],
        out_shape=[SDS((s, 3 * D_MODEL), bf16), SDS((s, 3 * D_MODEL), bf16), SDS((s, 3 * BRANCH), bf16)],
        compiler_params=_cp("parallel"),
    )(dx1b, wo, y, pg, wb, after)


def _mixer_bwd(p, dz, opre, states, lb, gout, wconv, lng, lnb, wsg, bsg_t):
    s = p.shape[0]
    tt = MIX_TILE
    nt = s // tt
    nch = tt // HGRN_CHUNK
    rev = lambda i: nt - 1 - i

    def body(q_ref, fp_ref, iv_ref, go_ref, bg_ref, cg_ref, xc_ref, u_ref, v_ref, cgp_ref, xcp_ref, dz_ref, opre_ref, st_ref,
             lb_ref, gout_ref, wconv_ref, lng_ref, lnb_ref, wsg_ref, bsg_ref,
             dp_ref, vec_ref, dwsg_ref, dbsg_ref, dst_scr, zbuf, dybuf, dbsg_acc):
        i = pl.program_id(0)

        @pl.when(i == 0)
        def _():
            dst_scr[...] = jnp.zeros_like(dst_scr)
            dybuf[tt:tt + 8, :] = jnp.zeros((8, BRANCH), f32)
            vec_ref[...] = jnp.zeros_like(vec_ref)
            dwsg_ref[...] = jnp.zeros_like(dwsg_ref)
            dbsg_acc[...] = jnp.zeros_like(dbsg_acc)

        lbv = lb_ref[...]
        gout_v = gout_ref[...]
        causal = _tri(HGRN_CHUNK)
        tri = causal.astype(f32)
        tri_up = _tri(HGRN_CHUNK, upper=True).astype(f32)
        last_row = lax.broadcasted_iota(jnp.int32, (HGRN_CHUNK, 1), 0) == HGRN_CHUNK - 1
        lb_live = (lbv > LB_FLOOR).astype(f32)
        dlb = jnp.zeros((1, BRANCH), f32)
        dgout = jnp.zeros((1, BRANCH), f32)
        for c in reversed(range(nch)):
            rows = slice(HGRN_CHUNK * c, HGRN_CHUNK * (c + 1))
            q_c, fp = q_ref[rows, :], fp_ref[rows, :]
            sq_c = _sigmoid(q_c)
            sfp_c = _sigmoid(fp)
            logf, snf_c, kk = _hgrn_gates(fp, lbv)
            invf_c = jnp.exp(-logf)
            doa = dz_ref[rows, 0:BRANCH].astype(f32)
            o = opre_ref[rows, :]
            sgo = _sigmoid(go_ref[rows, :])
            d_o, dgo, dg_c = [], [], []
            for h in range(HEADS):
                sl = slice(HEAD_DIM * h, HEAD_DIM * (h + 1))
                r, oh = _rms_stats(o[:, sl])
                dgo.append(doa[:, sl] * oh * gout_v[:, sl] * sgo[:, sl] * (1.0 - sgo[:, sl]))
                dx, dg = _rms_bwd(doa[:, sl] * sgo[:, sl], oh, r, gout_v[:, sl])
                d_o.append(dx)
                dg_c.append(dg)
            dp_ref[rows, 3 * BRANCH:4 * BRANCH] = jnp.concatenate(dgo, axis=1).astype(bf16)
            dgout = dgout + jnp.concatenate(dg_c, axis=1)
            dob = jnp.concatenate(d_o, axis=1).astype(bf16)
            b = _dot_exact(tri, logf)
            bl = jnp.sum(jnp.where(last_row, b, 0.0), axis=0, keepdims=True)
            eb, enb, edl, ebl = jnp.exp(b), jnp.exp(-b), jnp.exp(bl - b), jnp.exp(bl)
            qbf, kbf, kdf = q_c * sq_c * eb, kk * enb, kk * edl
            qb, kb, kd = qbf.astype(bf16), kbf.astype(bf16), kdf.astype(bf16)
            vc = iv_ref[rows, :].astype(bf16)
            dv, dqb, dkb, dkd, debl = [], [], [], [], []
            for h in range(HEADS):
                sl = slice(HEAD_DIM * h, HEAD_DIM * (h + 1))
                st = st_ref[c, h]
                dst = dst_scr[h]
                stb, dstb = st.astype(bf16), dst.astype(bf16)
                a = jnp.where(causal, _dot_nt(qb[:, sl], kb[:, sl]), 0.0).astype(bf16)
                da = jnp.where(causal, _dot_nt(dob[:, sl], vc[:, sl]), 0.0).astype(bf16)
                dv.append(_dot_tn(a, dob[:, sl]) + _dot_nt(kd[:, sl], dstb))
                dqb.append(_dot(dob[:, sl], stb) + _dot(da, kb[:, sl]))
                dkb.append(_dot_tn(da, qb[:, sl]))
                dkd.append(_dot(vc[:, sl], dstb))
                debl.append(jnp.sum(st * dst, axis=0, keepdims=True))
                dst_scr[h] = _dot_tn(dob[:, sl], qb[:, sl]) + dst * ebl[:, sl]
            dv, dqb, dkb, dkd = (jnp.concatenate(t, axis=1) for t in (dv, dqb, dkb, dkd))
            debl = jnp.concatenate(debl, axis=1)
            t_kd = dkd * kdf
            dbl = ebl * debl + jnp.sum(t_kd, axis=0, keepdims=True)
            db = dqb * qbf - dkb * kbf - t_kd + jnp.where(last_row, dbl, 0.0)
            dkk = dkb * enb + dkd * edl
            dlc = _dot_exact(tri_up, db)
            slope = (1.0 - lbv) * sfp_c * snf_c
            dp_ref[rows, 0:BRANCH] = (dqb * eb * sq_c * (1.0 + q_c * (1.0 - sq_c))).astype(bf16)
            dp_ref[rows, BRANCH:2 * BRANCH] = (slope * (dlc * invf_c - dkk)).astype(bf16)
            dp_ref[rows, 2 * BRANCH:3 * BRANCH] = dv.astype(bf16)
            dlb = dlb + jnp.sum(dlc * (lb_live - sfp_c) * invf_c - dkk * snf_c, axis=0, keepdims=True)
        vec_ref[0:1, :] += dlb
        vec_ref[1:2, :] += dgout

        dob_ = dz_ref[:, BRANCH:2 * BRANCH].astype(f32)
        bg, cg, xc = bg_ref[...], cg_ref[...], xc_ref[...]
        zc = cg * xc
        zbuf[0:8, :] = jnp.where(i < nt - 1, cgp_ref[...] * xcp_ref[...], 0.0)
        zbuf[8:8 + tt, :] = zc
        w0, w1, w2 = wconv_ref[0:1, :], wconv_ref[1:2, :], wconv_ref[2:3, :]
        y = w0 * zbuf[pl.ds(6, tt), :] + w1 * zbuf[pl.ds(7, tt), :] + w2 * zc
        dy = dob_ * bg
        dybuf[0:tt, :] = dy
        dy1, dy2 = dybuf[pl.ds(1, tt), :], dybuf[pl.ds(2, tt), :]
        dzc = w2 * dy + w1 * dy1 + w0 * dy2
        dp_ref[:, 4 * BRANCH:5 * BRANCH] = (dob_ * y).astype(bf16)
        dp_ref[:, 5 * BRANCH:6 * BRANCH] = (dzc * xc).astype(bf16)
        dp_ref[:, 6 * BRANCH:7 * BRANCH] = (dzc * cg).astype(bf16)
        vec_ref[4:5, :] += jnp.sum(zc * dy2, axis=0, keepdims=True)
        vec_ref[5:6, :] += jnp.sum(zc * dy1, axis=0, keepdims=True)
        vec_ref[6:7, :] += jnp.sum(zc * dy, axis=0, keepdims=True)
        dybuf[tt:tt + 8, :] = dybuf[0:8, :]

        lng_v, lnb_v = lng_ref[...], lnb_ref[...]
        low = _tri(SG_CHUNK)
        wms = [jnp.where(low, wsg_ref[g], 0.0).astype(bf16) for g in range(SG_GROUPS)]
        dlng = jnp.zeros((1, BRANCH), f32)
        dlnb = jnp.zeros((1, BRANCH), f32)
        for cc in range(tt // SG_CHUNK):
            rows = slice(SG_CHUNK * cc, SG_CHUNK * (cc + 1))
            doc = dz_ref[rows, 2 * BRANCH:3 * BRANCH].astype(f32)
            u_raw, v_raw = u_ref[rows, :], v_ref[rows, :]
            ug = _gelu(u_raw)
            vg = _gelu(v_raw)
            vcen = vg - jnp.mean(vg, axis=-1, keepdims=True)
            rstd = lax.rsqrt(jnp.mean(vcen * vcen, axis=-1, keepdims=True) + LN_EPS)
            vhat = vcen * rstd
            vn = (vhat * lng_v + lnb_v).astype(bf16)
            dvn = []
            for g in range(SG_GROUPS):
                sl = slice(LANE * g, LANE * (g + 1))
                sv = _dot(wms[g], vn[:, sl]) + bsg_ref[:, g:g + 1]
                dp_ref[rows, 7 * BRANCH + LANE * g:7 * BRANCH + LANE * (g + 1)] = (doc[:, sl] * sv * _gelu_grad(u_raw[:, sl])).astype(bf16)
                dsv = doc[:, sl] * ug[:, sl]
                dsvb = dsv.astype(bf16)
                dbsg_acc[:, sl] += dsv
                dwsg_ref[g] += jnp.where(low, _dot_nt(dsvb, vn[:, sl]), 0.0)
                dvn.append(_dot_tn(wms[g], dsvb))
            dvn = jnp.concatenate(dvn, axis=1)
            dlng = dlng + jnp.sum(dvn * vhat, axis=0, keepdims=True)
            dlnb = dlnb + jnp.sum(dvn, axis=0, keepdims=True)
            dvh = dvn * lng_v
            dvg = rstd * (dvh - jnp.mean(dvh, axis=-1, keepdims=True) - vhat * jnp.mean(dvh * vhat, axis=-1, keepdims=True))
            dp_ref[rows, 8 * BRANCH:9 * BRANCH] = (dvg * _gelu_grad(v_raw)).astype(bf16)
        vec_ref[2:3, :] += dlng
        vec_ref[3:4, :] += dlnb

        @pl.when(i == nt - 1)
        def _():
            for g in range(SG_GROUPS):
                dbsg_ref[:, g:g + 1] = jnp.sum(dbsg_acc[:, LANE * g:LANE * (g + 1)], axis=1, keepdims=True)

    full = lambda shape: pl.BlockSpec(shape, lambda i: (0,) * len(shape))
    tail = lambda c: pl.BlockSpec((8, BRANCH), lambda i: (jnp.maximum(rev(i) * (tt // 8) - 1, 0), c))
    return pl.pallas_call(
        body, name="mixer_bwd", grid=(nt,),
        in_specs=_p_specs(tt, range(9), rev) + [tail(5), tail(6), pl.BlockSpec((tt, 3 * BRANCH), lambda i: (rev(i), 0)),
                                                pl.BlockSpec((tt, BRANCH), lambda i: (rev(i), 0)),
                                                pl.BlockSpec((nch, HEADS, HEAD_DIM, HEAD_DIM), lambda i: (rev(i), 0, 0, 0)),
                                                full((1, BRANCH)), full((1, BRANCH)), full((3, BRANCH)), full((1, BRANCH)), full((1, BRANCH)),
                                                full((SG_GROUPS, SG_CHUNK, SG_CHUNK)), full((SG_CHUNK, SG_GROUPS))],
        out_specs=[pl.BlockSpec((tt, 9 * BRANCH), lambda i: (rev(i), 0)), full((8, BRANCH)), full((SG_GROUPS, SG_CHUNK, SG_CHUNK)),
                   full((SG_CHUNK, SG_GROUPS))],
        out_shape=[SDS((s, 9 * BRANCH), bf16), SDS((8, BRANCH), f32), SDS((SG_GROUPS, SG_CHUNK, SG_CHUNK), f32), SDS((SG_CHUNK, SG_GROUPS), f32)],
        scratch_shapes=[pltpu.VMEM((HEADS, HEAD_DIM, HEAD_DIM), f32), pltpu.VMEM((tt + 8, BRANCH), f32), pltpu.VMEM((tt + 8, BRANCH), f32),
                        pltpu.VMEM((SG_CHUNK, BRANCH), f32)],
        compiler_params=_cp("arbitrary"),
    )(*([p] * 11), dz, opre, states, lb, gout, wconv, lng, lnb, wsg, bsg_t)


def _dh_bwd(dpm, dpg, w_t, x, dx1, g, after, tm=1024, tk=1536):
    s = x.shape[0]
    km = dpm.shape[1] // tk
    nk = km + dpg.shape[1] // tk

    def body(dpm_ref, dpg_ref, w_ref, x_ref, dx1_ref, g_ref, after_ref, dx_ref, dxb_ref, dg_ref, acc):
        del after_ref
        i, k = pl.program_id(0), pl.program_id(1)

        @pl.when(k == 0)
        def _():
            acc[...] = jnp.zeros_like(acc)

        @pl.when(k < km)
        def _():
            acc[...] += _dot(dpm_ref[...], w_ref[...])

        @pl.when(k >= km)
        def _():
            acc[...] += _dot(dpg_ref[...], w_ref[...])

        @pl.when(k == nk - 1)
        def _():
            r, xh = _rms_stats(x_ref[...])
            dx, dg = _rms_bwd(acc[...], xh, r, g_ref[...])
            dx = dx + dx1_ref[...]
            dx_ref[...] = dx
            dxb_ref[...] = dx.astype(bf16)
            _acc_rows(dg_ref, i == 0, dg)

    row = pl.BlockSpec((tm, D_MODEL), lambda i, k: (i, 0))
    vec = pl.BlockSpec((1, D_MODEL), lambda i, k: (0, 0))
    return pl.pallas_call(
        body, name="dh_bwd", grid=(s // tm, nk),
        in_specs=[pl.BlockSpec((tm, tk), lambda i, k: (i, jnp.minimum(k, km - 1))),
                  pl.BlockSpec((tm, tk), lambda i, k: (i, jnp.maximum(k - km, 0))),
                  pl.BlockSpec((tk, D_MODEL), lambda i, k: (k, 0)), row, row, vec, pl.BlockSpec(memory_space=pl.ANY)],
        out_specs=[row, row, vec], out_shape=[SDS((s, D_MODEL), f32), SDS((s, D_MODEL), bf16), SDS((1, D_MODEL), f32)],
        scratch_shapes=[pltpu.VMEM((tm, D_MODEL), f32)], compiler_params=_cp("arbitrary", "arbitrary"),
    )(dpm, dpg, w_t, x, dx1, g, after)


def _layer_fwd(x, weight, sm):
    p, pg, h = _rms_mm(x, sm["g_mix"], weight("w_in", x))
    z, opre, states = _mixer_fwd(p, sm["lb"], sm["g_out"], sm["w_conv"], sm["ln_g"], sm["ln_b"], sm["w_sg"], sm["b_sg_t"])
    y, merged, x1 = _branch_gate(z, weight("w_branch", z), pg, x, weight("w_o", z))
    x2, h2, ra = _ffn(x1, sm["g_ffn"], weight("w_ff1", x1), weight("w_ff2", x1))
    saved = dict(x=x, p=p, pg=pg, h=h, z=z, opre=opre, states=states, y=y, merged=merged, x1=x1, h2=h2, ra=ra)
    return x2, saved


def _layer_bwd(dx2, dx2b, sv, w, sm, between, before_end):
    nchip = N_DEV // 2
    by_chip = lambda g: g.reshape((nchip, 2) + g.shape[1:])
    da, dx1, dx1b, dg_ffn = _ffn_bwd(dx2, dx2b, sv["x1"], sm["g_ffn"], sv["ra"], w["w_ff1"], w["w_ff2"])
    g_ff2 = by_chip(_mm_tn(sv["ra"], dx2b, 1, D_FF, D_MODEL, 512, 1024, name="dw_ff2", square_a=True)[0]
                    .reshape(N_DEV, D_FF // N_DEV, D_MODEL))
    g_ff1 = by_chip(_mm_tn_slabs(sv["h2"], da, 1, D_MODEL, D_FF // 2, [i * (D_FF // N_DEV) for i in range(nchip)], D_FF // N_DEV,
                                 name="dw_ff1")[:, 0])
    g_o = by_chip(_mm_tn(sv["merged"], dx1b, 1, D_MODEL, D_MODEL, 512, 1024, name="dw_o")[0].reshape(N_DEV, D_MODEL // N_DEV, D_MODEL))
    dy, dpg, dz = _merge_bwd(dx1b, w["w_o"], sv["y"], sv["pg"], w["w_branch"], between(dx1))
    g_branch = by_chip(_mm_tn_slabs(sv["z"], dy, 3, BRANCH, D_MODEL, [i * (D_MODEL // N_DEV) for i in range(N_DEV)], D_MODEL // N_DEV,
                                    name="dw_branch"))
    g_in = _mm_tn(dpg, sv["h"], 1, 3 * D_MODEL, D_MODEL, 768, 1024, name="dw_in_gates", rows=N_COLS, row0=GATE_COL0)
    dpm, vecs, dwsg, dbsg_t = _mixer_bwd(sv["p"], dz, sv["opre"], sv["states"], sm["lb"], sm["g_out"], sm["w_conv"],
                                         sm["ln_g"], sm["ln_b"], sm["w_sg"], sm["b_sg_t"])
    g_in = _mm_tn(dpm, sv["h"], 1, GATE_COL0, D_MODEL, 768, 1024, name="dw_in_mixers", rows=N_COLS, into=g_in)
    g_in = by_chip(g_in[0].reshape(N_DEV, SHARD_IN, D_MODEL))
    big = dict(w_in=g_in, w_branch=g_branch, w_o=g_o, w_ff1=g_ff1, w_ff2=g_ff2)
    dx, dxb, dg_mix = _dh_bwd(dpm, dpg, w["w_in"], sv["x"], dx1, sm["g_mix"], before_end(big))
    small = dict(g_mix=dg_mix, g_ffn=dg_ffn, vecs=vecs, w_sg=dwsg, b_sg_t=dbsg_t, dx1=dx1)
    return dx, dxb, big, small


BIG = ("w_in", "w_branch", "w_o", "w_ff1", "w_ff2")
ANY = pl.BlockSpec(memory_space=pl.ANY)


def _place():
    return lax.axis_index("x"), lax.axis_index("y"), lax.axis_index("c")


def _al(v, m):
    return pl.multiple_of(v * m, m)


def _shard_of(refs, dev, which=range(len(BIG))):
    out = []
    for ref, t in zip(refs, which):
        by_cols = BIG[t] in ("w_branch", "w_ff1")
        n = ref.shape[-1 if by_cols else 0] // N_DEV
        part = pl.ds(_al(dev, n), n)
        out.append(ref.at[(slice(None),) * (len(ref.shape) - 1) + (part,)] if by_cols else ref.at[part])
    return out


def _gather_out_shapes(shards):
    s_in, s_b, s_o, s_1, s_2 = (shards[n] for n in BIG)
    return [SDS((s_in.shape[1] * N_DEV, s_in.shape[2]), bf16), SDS(s_b.shape[1:3] + (s_b.shape[3] * N_DEV,), bf16),
            SDS((s_o.shape[1] * N_DEV, s_o.shape[2]), bf16), SDS((s_1.shape[1], s_1.shape[2] * N_DEV), bf16),
            SDS((s_2.shape[1] * N_DEV, s_2.shape[2]), bf16)]


def _seq_all_gather_layer(layer, which, n_early, shard_refs, out_shapes, tag=""):
    nt = len(which)
    outs = [jax.empty_ref(sh, memory_space=pltpu.MemorySpace.HBM) for sh in out_shapes]
    early, late = tuple(range(n_early)), tuple(range(n_early, nt))

    @pl.kernel(mesh=plsc.ScalarSubcoreMesh(axis_name="seq", num_cores=1), name=f"seq_all_gather_l{layer}{tag}",
               scratch_types=(pltpu.SemaphoreType.DMA((9,)), pltpu.SemaphoreType.DMA((9,))),
               compiler_params=pltpu.CompilerParams(collective_id=1))
    def launch(send_sems, recv_sems):
        x, y, c = _place()
        me, sibling = (x, y, c), (x, y, 1 - c)
        first, second, diag = _ici_route(x, y, c)
        _handshake([sibling, first, second])
        mine = [r.at[layer] for r in shard_refs]

        def copies(k, blk, to, src=None, part=range(nt)):
            dst = _shard_of(outs, 4 * blk[0] + 2 * blk[1] + blk[2], which)
            src = dst if src is None else src
            return [pltpu.make_async_remote_copy(src_ref=src[t], dst_ref=dst[t], send_sem=send_sems.at[k], recv_sem=recv_sems.at[k],
                                                 device_id=to, device_id_type=MESH) for t in part]

        def start(cps):
            for cp in cps:
                cp.start()
            return cps

        def landed(cps):
            for cp in cps:
                cp.wait_recv()

        sent = start(copies(0, me, sibling, src=mine) + copies(1, me, first, src=mine, part=early)
                     + copies(2, me, first, src=mine, part=late) + copies(3, me, second, src=mine))
        landed(copies(1, first, me, part=early))
        sent += start(copies(4, first, second, part=early) + copies(6, first, sibling, part=early))
        landed(copies(2, first, me, part=late))
        sent += start(copies(5, first, second, part=late) + copies(6, first, sibling, part=late))
        landed(copies(3, second, me))
        sent += start(copies(7, second, sibling))
        landed(copies(4, diag, me, part=early) + copies(5, diag, me, part=late))
        sent += start(copies(8, diag, sibling))
        other = lambda p: (p[0], p[1], 1 - c)
        landed(copies(0, sibling, me) + copies(6, other(second), me) + copies(7, other(first), me) + copies(8, other(diag), me))
        for cp in sent:
            cp.wait_send()

    launch()
    return [o[...] for o in outs]


def _ici_route(x, y, c):
    return (x ^ (1 - c), y ^ c, c), (x ^ c, y ^ (1 - c), c), (1 - x, 1 - y, c)


def _place_own(where, which, shards, gathered, after):
    nt = len(which)

    def body(where_ref, *refs):
        del where_ref
        for src, dst in zip(refs[:nt], refs[2 * nt + 1:]):
            dst[...] = src[...]

    in_specs, out_specs = [], []
    for t, sh in zip(which, shards):
        blk = sh.shape[1:]
        in_specs.append(pl.BlockSpec((None,) + blk, functools.partial(lambda nd, i, wh: (wh[0],) + (0,) * nd, len(blk))))
        by_cols = BIG[t] in ("w_branch", "w_ff1")
        out_specs.append(pl.BlockSpec(blk, functools.partial(
            lambda nd, cols, i, wh: (0,) * (nd - 1) + (wh[1],) if cols else (wh[1],) + (0,) * (nd - 1), len(blk), by_cols)))
    return pl.pallas_call(
        body, name="place_own", out_shape=[SDS(g.shape, g.dtype) for g in gathered],
        input_output_aliases={1 + nt + i: i for i in range(nt)}, compiler_params=_cp("arbitrary"),
        grid_spec=pltpu.PrefetchScalarGridSpec(num_scalar_prefetch=1, grid=(1,), in_specs=in_specs + [ANY] * (nt + 1), out_specs=out_specs),
    )(where, *shards, *gathered, after)


def _handshake(peers):
    barrier = pltpu.get_barrier_semaphore()
    for p in peers:
        pl.semaphore_signal(barrier, inc=1, device_id=p, device_id_type=MESH)
    pl.semaphore_wait(barrier, len(peers))


def _seq_exchange_on_chip(grads):
    nt, nchip = len(BIG), N_DEV // 2
    g_refs = [jax.new_ref(g, memory_space=pltpu.MemorySpace.HBM) for g in grads]
    outs = [jax.empty_ref(SDS((nchip,) + g.shape[2:], bf16), memory_space=pltpu.MemorySpace.HBM) for g in grads]

    @pl.kernel(mesh=plsc.ScalarSubcoreMesh(axis_name="seq", num_cores=1), name="seq_rs_on_chip",
               scratch_types=(pltpu.SemaphoreType.DMA((nchip,)), pltpu.SemaphoreType.DMA((nchip,))),
               compiler_params=pltpu.CompilerParams(collective_id=2))
    def launch(send_sems, recv_sems):
        x, y, c = _place()
        sibling = (x, y, 1 - c)
        _handshake([sibling])
        remote = [pltpu.make_async_remote_copy(src_ref=g_refs[t].at[j, 1 - c], dst_ref=outs[t].at[j], send_sem=send_sems.at[j],
                                               recv_sem=recv_sems.at[j], device_id=sibling, device_id_type=MESH)
                  for j in range(nchip) for t in range(nt)]
        for cp in remote:
            cp.start()
        for cp in remote:
            cp.wait_recv()
        for cp in remote:
            cp.wait_send()

    launch()
    return [o[...] for o in outs], [g[...] for g in g_refs]


def _seq_exchange_between_chips(sums):
    nt = len(BIG)
    s_refs = [jax.new_ref(a, memory_space=pltpu.MemorySpace.HBM) for a in sums]
    outs = [jax.empty_ref(SDS((3,) + a.shape[1:], bf16), memory_space=pltpu.MemorySpace.HBM) for a in sums]
    transit = [jax.empty_ref(SDS(a.shape[1:], bf16), memory_space=pltpu.MemorySpace.HBM) for a in sums]

    early, late = (0,), tuple(range(1, nt))

    @pl.kernel(mesh=plsc.ScalarSubcoreMesh(axis_name="seq", num_cores=1), name="seq_rs_between_chips",
               scratch_types=(pltpu.SemaphoreType.DMA((6,)), pltpu.SemaphoreType.DMA((6,))),
               compiler_params=pltpu.CompilerParams(collective_id=3))
    def launch(send_sems, recv_sems):
        x, y, c = _place()
        first, second, diag = _ici_route(x, y, c)
        _handshake([first, second])

        def copies(k, src, dst, to, part=range(nt)):
            return [pltpu.make_async_remote_copy(src_ref=src(t), dst_ref=dst(t), send_sem=send_sems.at[k], recv_sem=recv_sems.at[k],
                                                 device_id=to, device_id_type=MESH) for t in part]

        chip_of = lambda p: 2 * p[0] + p[1]
        for_diag = lambda t: s_refs[t].at[chip_of(diag)]
        through = lambda t: transit[t]
        last = lambda t: outs[t].at[2]
        direct = (copies(0, lambda t: s_refs[t].at[chip_of(first)], lambda t: outs[t].at[0], first)
                  + copies(1, lambda t: s_refs[t].at[chip_of(second)], lambda t: outs[t].at[1], second))
        via = [copies(2, for_diag, through, first, early), copies(3, for_diag, through, first, late)]
        passed = [copies(4, through, last, second, early), copies(5, through, last, second, late)]
        for cp in via[0] + direct + via[1]:
            cp.start()
        for arrived, onward in zip(via, passed):
            for cp in arrived:
                cp.wait_recv()
            for cp in onward:
                cp.start()
        sent = direct + via[0] + via[1] + passed[0] + passed[1]
        for cp in direct + passed[0] + passed[1]:
            cp.wait_recv()
        for cp in sent:
            cp.wait_send()

    launch()
    return [o[...] for o in outs], [a[...] for a in s_refs]


def _chip_sums(place, mine, other, after, steps=2):
    nt, nchip = len(mine), mine[0].shape[0]
    m4 = [a.reshape(nchip, 2, -1, a.shape[-1]) for a in mine]
    o3 = [a.reshape(nchip, -1, a.shape[-1]) for a in other]

    def body(p_ref, *refs):
        del p_ref
        for a_ref, b_ref, o_ref in zip(refs[:nt], refs[nt:2 * nt], refs[2 * nt + 1:]):
            o_ref[...] = (a_ref[...].astype(f32) + b_ref[...].astype(f32)).astype(bf16)

    tiles = [(a.shape[1] // steps, a.shape[2]) for a in o3]
    blks = [pl.BlockSpec((None,) + t, lambda j, i, p_ref: (p_ref[1 + j], i, 0)) for t in tiles]
    outs = pl.pallas_call(
        body, name="chip_sums", out_shape=[SDS(a.shape, bf16) for a in o3], compiler_params=_cp("parallel", "parallel"),
        grid_spec=pltpu.PrefetchScalarGridSpec(
            num_scalar_prefetch=1, grid=(nchip - 1, steps),
            in_specs=[pl.BlockSpec((None, None) + t, lambda j, i, p_ref: (p_ref[1 + j], p_ref[0], i, 0)) for t in tiles] + blks + [ANY],
            out_specs=blks),
    )(place, *m4, *o3, after)
    return [o.reshape(a.shape) for o, a in zip(outs, other)]


def _all_reduce_rows(pack):
    rows = pack.shape[0]
    blk = rows // N_DEV

    def body(in_ref, out_ref, land, send1, recv1, send2, recv2):
        x, y, c = _place()
        me = 4 * x + 2 * y + c
        others = [(px, py, pc) for px in range(2) for py in range(2) for pc in range(2)]

        def is_me(p):
            return jnp.logical_and(jnp.logical_and(p[0] == x, p[1] == y), p[2] == c)

        land[me] = in_ref[pl.ds(_al(me, blk), blk), :]
        for d, p in enumerate(others):
            @pl.when(jnp.logical_not(is_me(p)))
            def _():
                pltpu.make_async_remote_copy(src_ref=in_ref.at[pl.ds(d * blk, blk), :], dst_ref=land.at[me], send_sem=send1.at[d],
                                             recv_sem=recv1.at[me], device_id=p, device_id_type=MESH).start()
        for d, p in enumerate(others):
            @pl.when(jnp.logical_not(is_me(p)))
            def _():
                cp = pltpu.make_async_remote_copy(src_ref=in_ref.at[pl.ds(d * blk, blk), :], dst_ref=land.at[d], send_sem=send1.at[d],
                                                  recv_sem=recv1.at[d], device_id=p, device_id_type=MESH)
                cp.wait_recv()
                cp.wait_send()
        total = land[0]
        for d in range(1, N_DEV):
            total = total + land[d]
        out_ref[pl.ds(_al(me, blk), blk), :] = total
        for d, p in enumerate(others):
            @pl.when(jnp.logical_not(is_me(p)))
            def _():
                mine = out_ref.at[pl.ds(_al(me, blk), blk), :]
                pltpu.make_async_remote_copy(src_ref=mine, dst_ref=mine, send_sem=send2.at[d], recv_sem=recv2.at[me],
                                             device_id=p, device_id_type=MESH).start()
        for d, p in enumerate(others):
            @pl.when(jnp.logical_not(is_me(p)))
            def _():
                theirs = out_ref.at[pl.ds(d * blk, blk), :]
                cp = pltpu.make_async_remote_copy(src_ref=theirs, dst_ref=theirs, send_sem=send2.at[d], recv_sem=recv2.at[d],
                                                  device_id=p, device_id_type=MESH)
                cp.wait_recv()
                cp.wait_send()

    vm = pl.BlockSpec(memory_space=pltpu.VMEM)
    return pl.pallas_call(
        body, name="all_reduce_rows", in_specs=[vm], out_specs=vm, out_shape=SDS((rows, LANE), f32),
        scratch_shapes=[pltpu.VMEM((N_DEV, blk, LANE), f32)] + [pltpu.SemaphoreType.DMA((N_DEV,))] * 4,
        compiler_params=pltpu.CompilerParams(vmem_limit_bytes=VMEM_LIMIT),
    )(pack)


def _lower_bounds_fwd(lower, after):
    def body(l_ref, *rest):
        o_ref = rest[-1]
        sm = _layer_softmax(l_ref)
        run = jnp.zeros_like(sm[0])
        for l in range(DEPTH):
            o_ref[l:l + 1, :] = run
            if l + 1 < DEPTH:
                run = run + sm[l + 1]

    vm = pl.BlockSpec(memory_space=pltpu.VMEM)
    return pl.pallas_call(body, name="lower_bounds_fwd", in_specs=[vm] + [ANY] * len(after), out_specs=vm,
                          out_shape=SDS(lower.shape, f32))(lower, *after)


def _layer_softmax(l_ref):
    rows = [l_ref[l:l + 1, :] for l in range(DEPTH)]
    top = functools.reduce(jnp.maximum, rows)
    e = [jnp.exp(r - top) for r in rows]
    tot = functools.reduce(lambda a, b: a + b, e)
    return [v / tot for v in e]


def _lower_bounds_bwd(lower, dlbs):
    def body(l_ref, d_ref, o_ref):
        sm = _layer_softmax(l_ref)
        dsm = [None] * DEPTH
        run = jnp.zeros_like(sm[0])
        dsm[0] = run
        for l in reversed(range(1, DEPTH)):
            run = run + d_ref[l:l + 1, :]
            dsm[l] = run
        inner = functools.reduce(lambda a, b: a + b, [sm[l] * dsm[l] for l in range(DEPTH)])
        for l in range(DEPTH):
            o_ref[l:l + 1, :] = sm[l] * (dsm[l] - inner)

    return pl.pallas_call(body, name="lower_bounds_bwd", out_shape=SDS(lower.shape, f32))(lower, dlbs)


_ADAM_C1 = 1.0 - ADAM_B1 ** ADAM_STEP
_ADAM_C2 = 1.0 - ADAM_B2 ** ADAM_STEP


def _adamw(w, g, m, v):
    m = ADAM_B1 * m + (1.0 - ADAM_B1) * g
    v = ADAM_B2 * v + (1.0 - ADAM_B2) * (g * g)
    delta = -ADAM_LR * ((m / _ADAM_C1) / (jnp.sqrt(v / _ADAM_C2) + ADAM_EPS) + ADAM_WD * w)
    return delta, m, v


def _adam_big(where, names, w, m, v, mine, other, landed, outs, after, steps=4):
    nt = len(names)
    three = lambda a: a.reshape(a.shape[0], -1, a.shape[-1])
    w3, m3, v3 = ([three(d[n]) for n in names] for d in (w, m, v))
    outs3 = [three(a) for n in names for a in outs[n]]
    mine4 = [a.reshape(a.shape[0], 2, -1, a.shape[-1]) for a in mine]
    other3 = [three(a) for a in other]
    land3 = [three(a) for a in landed]

    def body(where_ref, *refs):
        del where_ref
        o_refs = refs[6 * nt + 4 * nt + 1:]
        for t in range(nt):
            w_ref, m_ref, v_ref, mine_ref, other_ref, land_ref = (refs[q * nt + t] for q in range(6))
            g = mine_ref[...].astype(f32) + other_ref[...].astype(f32)
            for k in range(3):
                g = g + land_ref[k].astype(f32)
            delta, nm, nv = _adamw(w_ref[...], g, m_ref[...], v_ref[...])
            for o_ref, val in zip(o_refs[4 * t:4 * t + 4], (g, delta, nm, nv)):
                o_ref[...] = val

    tiles = [(a.shape[1] // steps, a.shape[2]) for a in w3]
    own = [pl.BlockSpec((None,) + t, lambda i, wh: (wh[0], i, 0)) for t in tiles]
    res = pl.pallas_call(
        body, name="adam_big", out_shape=[SDS(a.shape, f32) for a in outs3],
        input_output_aliases={1 + 6 * nt + i: i for i in range(4 * nt)}, compiler_params=_cp("parallel"),
        grid_spec=pltpu.PrefetchScalarGridSpec(
            num_scalar_prefetch=1, grid=(steps,),
            in_specs=own * 3 + [pl.BlockSpec((None, None) + t, lambda i, wh: (wh[1], wh[2], i, 0)) for t in tiles]
            + [pl.BlockSpec((None,) + t, lambda i, wh: (wh[1], i, 0)) for t in tiles]
            + [pl.BlockSpec((3,) + t, lambda i, wh: (0, i, 0)) for t in tiles] + [ANY] * (4 * nt + 1),
            out_specs=[s for s in own for _ in range(4)]),
    )(where, *w3, *m3, *v3, *mine4, *other3, *land3, *outs3, after)
    return {n: [o.reshape(w[n].shape) for o in res[4 * t:4 * t + 4]] for t, n in enumerate(names)}


def _touch(a, after):
    a2 = a.reshape(-1, a.shape[-1])

    def body(a_ref, after_ref, o_ref):
        del after_ref
        o_ref[...] = a_ref[0:8, :].astype(f32)

    return pl.pallas_call(
        body, name="touch", grid=(1,), in_specs=[pl.BlockSpec((16, LANE), lambda i: (0, 0)), ANY],
        out_specs=pl.BlockSpec((8, LANE), lambda i: (0, 0)), out_shape=SDS((8, LANE), f32),
    )(a2, after)


def _adam_rows(w, g, m, v):
    def body(w_ref, g_ref, m_ref, v_ref, d_ref, nm_ref, nv_ref):
        delta, nm, nv = _adamw(w_ref[...], g_ref[...], m_ref[...], v_ref[...])
        d_ref[...] = delta
        nm_ref[...] = nm
        nv_ref[...] = nv

    return pl.pallas_call(body, name="adam_rows", out_shape=[SDS(w.shape, f32)] * 3)(w, g, m, v)


SMALL = ("g_mix", "lower_bounds", "g_hgrn_out", "w_conv", "sg_ln_g", "sg_ln_b", "w_sg", "b_sg", "g_ffn", "g_final")
WEIGHTS = ("w_in", "g_mix", "lower_bounds", "g_hgrn_out", "w_conv", "sg_ln_g", "sg_ln_b", "w_sg", "b_sg", "w_branch", "w_o", "g_ffn",
           "w_ff1", "w_ff2", "g_final")


def _pack_rows(arrays, multiple):
    flat = jnp.concatenate([a.reshape(-1) for a in arrays])
    rows = -(-flat.shape[0] // (LANE * multiple)) * multiple
    return jnp.pad(flat, (0, rows * LANE - flat.shape[0])).reshape(rows, LANE)


def _unpack_rows(pack, like):
    flat = pack.reshape(-1)
    out, at = [], 0
    for a in like:
        out.append(flat[at:at + a.size].reshape(a.shape))
        at += a.size
    return out


def kernel(x, w_in, g_mix, lower_bounds, g_hgrn_out, w_conv, sg_ln_g, sg_ln_b, w_sg, b_sg, w_branch, w_o, g_ffn, w_ff1, w_ff2, g_final, loss_target, m_w_in, m_g_mix, m_lower_bounds, m_g_hgrn_out, m_w_conv, m_sg_ln_g, m_sg_ln_b, m_w_sg, m_b_sg, m_w_branch, m_w_o, m_g_ffn, m_w_ff1, m_w_ff2, m_g_final, v_w_in, v_g_mix, v_lower_bounds, v_g_hgrn_out, v_w_conv, v_sg_ln_g, v_sg_ln_b, v_w_sg, v_b_sg, v_w_branch, v_w_o, v_g_ffn, v_w_ff1, v_w_ff2, v_g_final):
    weights = dict(w_in=w_in, g_mix=g_mix, lower_bounds=lower_bounds, g_hgrn_out=g_hgrn_out, w_conv=w_conv, sg_ln_g=sg_ln_g,
                   sg_ln_b=sg_ln_b, w_sg=w_sg, b_sg=b_sg, w_branch=w_branch, w_o=w_o, g_ffn=g_ffn, w_ff1=w_ff1, w_ff2=w_ff2, g_final=g_final)
    mom1 = dict(w_in=m_w_in, g_mix=m_g_mix, lower_bounds=m_lower_bounds, g_hgrn_out=m_g_hgrn_out, w_conv=m_w_conv, sg_ln_g=m_sg_ln_g,
                sg_ln_b=m_sg_ln_b, w_sg=m_w_sg, b_sg=m_b_sg, w_branch=m_w_branch, w_o=m_w_o, g_ffn=m_g_ffn, w_ff1=m_w_ff1, w_ff2=m_w_ff2,
                g_final=m_g_final)
    mom2 = dict(w_in=v_w_in, g_mix=v_g_mix, lower_bounds=v_lower_bounds, g_hgrn_out=v_g_hgrn_out, w_conv=v_w_conv, sg_ln_g=v_sg_ln_g,
                sg_ln_b=v_sg_ln_b, w_sg=v_w_sg, b_sg=v_b_sg, w_branch=v_w_branch, w_o=v_w_o, g_ffn=v_g_ffn, w_ff1=v_w_ff1, w_ff2=v_w_ff2,
                g_final=v_g_final)
    xi, yi, ci = _place()
    dev = 4 * xi + 2 * yi + ci
    conv_cols = w_conv.shape[-1]

    for d in (weights, mom1, mom2):
        d["w_in"] = jnp.swapaxes(d["w_in"], 1, 2)
    shards = {n: weights[n].astype(bf16) for n in BIG}

    conv_place = lax.dynamic_update_slice(jnp.zeros((DEPTH, 3, BRANCH), f32), w_conv, (0, 0, dev * conv_cols))
    (w_conv_full,) = _unpack_rows(_all_reduce_rows(_pack_rows([conv_place], 8 * N_DEV)), [conv_place])
    small_packs = [_pack_rows([d[n] for n in SMALL], 8) for d in (weights, mom1, mom2)]
    lbs = _lower_bounds_fwd(lower_bounds, small_packs)

    def small_of(l):
        return dict(g_mix=g_mix[l][None], lb=lbs[l][None], g_out=g_hgrn_out[l][None], w_conv=w_conv_full[l], ln_g=sg_ln_g[l][None],
                    ln_b=sg_ln_b[l][None], w_sg=w_sg[l], b_sg_t=b_sg[l].T, g_ffn=g_ffn[l][None])

    act = x[0]
    full, saved = [], []
    shard_refs = [jax.new_ref(shards[n], memory_space=pltpu.MemorySpace.HBM) for n in BIG]
    shapes = _gather_out_shapes(shards)
    groups = [g for l in range(DEPTH) for g in ((l, (0,), 1, "a"), (l, (1, 2, 3, 4), 2, "b"))]
    arrived = {}
    for l, which, n_early, tag in groups:
        got = _seq_all_gather_layer(l, which, n_early, [shard_refs[t] for t in which], [shapes[t] for t in which], tag)
        arrived.update({(l, BIG[t]): (which, got) for t in which})

    for l in range(DEPTH):
        full.append({})

        def weight(name, after, l=l):
            if name not in full[l]:
                which, got = arrived[(l, name)]
                where = jnp.stack([jnp.int32(l), dev.astype(jnp.int32)])
                full[l].update(zip([BIG[t] for t in which], _place_own(where, which, [shards[BIG[t]] for t in which], got, after)))
            return full[l][name]

        act, sv = _layer_fwd(act, weight, small_of(l))
        saved.append(sv)
    loss_row, dx, dxb, dg_final = _final(act, loss_target[0], g_final[None])

    chip, core = (2 * xi + yi).astype(jnp.int32), ci.astype(jnp.int32)
    place = jnp.stack([core] + [(chip + k) % (N_DEV // 2) for k in (1, 2, 3)])
    big_out = {n: [lax.empty(weights[n].shape, f32) for _ in range(4)] for n in BIG}
    small_grads = [None] * DEPTH

    def chip_sums(stage, after):
        l, received, mine = stage
        sums = _chip_sums(place, mine, received, after)
        placed.append(sums[BIG.index("w_o")])
        landed, _ = _seq_exchange_between_chips(sums)
        return l, mine, received, landed

    def adam_layer(stage, after):
        l, mine, received, landed = stage
        where = jnp.stack([jnp.int32(l), chip, core])
        big_out.update(_adam_big(where, BIG, weights, mom1, mom2, mine, received, landed, big_out, after))

    above = None
    placed = []
    for l in reversed(range(DEPTH)):
        summed = []

        def between(dx1):
            if above is None:
                return dx1
            summed.append(chip_sums(above, dx1))
            return placed[-1]

        def before_end(big):
            return _touch(summed[0][3][BIG.index("w_o")], big["w_in"]) if summed else big["w_in"]

        dx, dxb, big, small_grads[l] = _layer_bwd(dx, dxb, saved[l], full[l], small_of(l), between, before_end)
        if summed:
            adam_layer(summed[0], dx)
        above = (l, *_seq_exchange_on_chip([big[n] for n in BIG]))

    stack = lambda f: jnp.stack([f(small_grads[l]) for l in range(DEPTH)])
    d_lower = _lower_bounds_bwd(lower_bounds, stack(lambda s: s["vecs"][0]))
    local_small = dict(g_mix=stack(lambda s: s["g_mix"][0]), lower_bounds=d_lower, g_hgrn_out=stack(lambda s: s["vecs"][1]),
                       w_conv=stack(lambda s: s["vecs"][4:7]), sg_ln_g=stack(lambda s: s["vecs"][2]), sg_ln_b=stack(lambda s: s["vecs"][3]),
                       w_sg=stack(lambda s: s["w_sg"]), b_sg=stack(lambda s: s["b_sg_t"].T), g_ffn=stack(lambda s: s["g_ffn"][0]),
                       g_final=dg_final[0])
    order = [local_small[n] for n in SMALL] + [loss_row]
    *reduced, loss_sum = _unpack_rows(_all_reduce_rows(_pack_rows(order, 8 * N_DEV)), order)
    loss = loss_sum[0, 0]
    grads = dict(zip(SMALL, reduced))
    grads["w_conv"] = lax.dynamic_slice(grads["w_conv"], (0, 0, dev * conv_cols), (DEPTH, 3, conv_cols))

    deltas, new_m, new_v = {}, {}, {}
    like = [weights[n] for n in SMALL]
    small_out = _adam_rows(small_packs[0], _pack_rows([grads[n] for n in SMALL], 8), small_packs[1], small_packs[2])
    for out, pack in zip((deltas, new_m, new_v), small_out):
        out.update(zip(SMALL, _unpack_rows(pack, like)))
    adam_layer(chip_sums(above, dx), small_out[0])
    for n in BIG:
        grads[n], deltas[n], new_m[n], new_v[n] = (jnp.swapaxes(a, 1, 2) if n == "w_in" else a for a in big_out[n])

    return (loss, dx[None], *[grads[n] for n in WEIGHTS], *[deltas[n] for n in WEIGHTS], *[new_m[n] for n in WEIGHTS],
            *[new_v[n] for n in WEIGHTS])
```

```python
import functools

import jax
import jax.numpy as jnp
from jax import lax
from jax.experimental import pallas as pl
from jax.experimental.pallas import tpu as pltpu
from jax.experimental.pallas import tpu_sc as plsc

f32 = jnp.float32
bf16 = jnp.bfloat16
SDS = jax.ShapeDtypeStruct
MESH = pl.DeviceIdType.MESH

D_MODEL = 1024
BRANCH = 512
N_COLS = 7680
D_FF = 4096
DEPTH = 4
HEADS = 4
HEAD_DIM = 128
HGRN_CHUNK = 64
SG_CHUNK = 128
SG_GROUPS = 4
NORM_EPS = 1e-6
LN_EPS = 1e-5
LB_FLOOR = 1e-30
N_DEV = 8
SHARD_IN = N_COLS // N_DEV
LANE = 128
GATE_COL0 = 9 * BRANCH

ADAM_LR = 0.001
ADAM_B1 = 0.9
ADAM_B2 = 0.999
ADAM_EPS = 1e-08
ADAM_WD = 0.01
ADAM_STEP = 10

MIX_TILE = 256
VMEM_LIMIT = 56 * 1024 * 1024


def _cp(*sem):
    return pltpu.CompilerParams(dimension_semantics=sem or None, vmem_limit_bytes=VMEM_LIMIT)


def _dot(a, b):
    return jnp.dot(a, b, preferred_element_type=f32)


def _dot_nt(a, b):
    return lax.dot_general(a, b, (((1,), (1,)), ((), ())), preferred_element_type=f32)


def _dot_tn(a, b):
    return lax.dot_general(a, b, (((0,), (0,)), ((), ())), preferred_element_type=f32)


def _dot_exact(ones, b):
    hi = b.astype(bf16)
    rest = b - hi.astype(f32)
    mid = rest.astype(bf16)
    low = (rest - mid.astype(f32)).astype(bf16)
    ones = ones.astype(bf16)
    return _dot(ones, hi) + _dot(ones, mid) + _dot(ones, low)


def _sigmoid(x):
    return jax.nn.sigmoid(x)


_GELU_C = 0.7978845608028654
_GELU_A = 0.044715


def _gelu(x):
    return 0.5 * x * (1.0 + jnp.tanh(_GELU_C * (x + _GELU_A * x * x * x)))


def _gelu_grad(x):
    x2 = x * x
    t = jnp.tanh(_GELU_C * (x + _GELU_A * x * x2))
    return 0.5 * (1.0 + t) + 0.5 * x * (1.0 - t * t) * _GELU_C * (1.0 + 3.0 * _GELU_A * x2)


def _rms_stats(x):
    r = lax.rsqrt(jnp.mean(x * x, axis=-1, keepdims=True) + NORM_EPS)
    return r, x * r


def _rms_bwd(dh, xh, r, g):
    dg = jnp.sum(dh * xh, axis=0, keepdims=True)
    dxn = dh * g
    dx = r * (dxn - xh * jnp.mean(dxn * xh, axis=-1, keepdims=True))
    return dx, dg


def _tri(n, upper=False):
    r = lax.broadcasted_iota(jnp.int32, (n, n), 0)
    c = lax.broadcasted_iota(jnp.int32, (n, n), 1)
    return (c >= r) if upper else (c <= r)


def _acc_rows(ref, first, val):
    @pl.when(first)
    def _():
        ref[...] = val

    @pl.when(jnp.logical_not(first))
    def _():
        ref[...] += val


def _rms_mm(x, g, w_t, tm=1024, tn=1536):
    s, n = x.shape[0], w_t.shape[0]
    jm = GATE_COL0 // tn

    def body(x_ref, g_ref, w_ref, pm_ref, pg_ref, h_ref, hs):
        j = pl.program_id(1)

        @pl.when(j == 0)
        def _():
            _, xh = _rms_stats(x_ref[...])
            hv = (xh * g_ref[...]).astype(bf16)
            hs[...] = hv
            h_ref[...] = hv

        res = _dot_nt(hs[...], w_ref[...])

        @pl.when(j < jm)
        def _():
            pm_ref[...] = res

        @pl.when(j >= jm)
        def _():
            pg_ref[...] = res.astype(bf16)

    return pl.pallas_call(
        body, name="rms_mm", grid=(s // tm, n // tn),
        in_specs=[pl.BlockSpec((tm, D_MODEL), lambda i, j: (i, 0)), pl.BlockSpec((1, D_MODEL), lambda i, j: (0, 0)),
                  pl.BlockSpec((tn, D_MODEL), lambda i, j: (j, 0))],
        out_specs=[pl.BlockSpec((tm, tn), lambda i, j: (i, jnp.minimum(j, jm - 1))),
                   pl.BlockSpec((tm, tn), lambda i, j: (i, jnp.maximum(j - jm, 0))), pl.BlockSpec((tm, D_MODEL), lambda i, j: (i, 0))],
        out_shape=[SDS((s, GATE_COL0), f32), SDS((s, n - GATE_COL0), bf16), SDS((s, D_MODEL), bf16)],
        scratch_shapes=[pltpu.VMEM((tm, D_MODEL), bf16)], compiler_params=_cp("parallel", "arbitrary"),
    )(x, g, w_t)


def _hgrn_gates(fp, lb):
    logf = jnp.logaddexp(jnp.log(jnp.maximum(lb, LB_FLOOR)), jnp.log1p(-lb) + jax.nn.log_sigmoid(fp))
    snf = _sigmoid(-fp)
    return logf, snf, (1.0 - lb) * snf


def _p_specs(tile, cols, row_map):
    return [pl.BlockSpec((tile, BRANCH), functools.partial(lambda c, i: (row_map(i), c), c)) for c in cols]


def _mixer_fwd(p, lb, gout, wconv, lng, lnb, wsg, bsg_t):
    s = p.shape[0]
    tt = MIX_TILE
    nch = tt // HGRN_CHUNK

    def body(q_ref, fp_ref, iv_ref, go_ref, bg_ref, cg_ref, xc_ref, u_ref, v_ref, lb_ref, gout_ref, wconv_ref, lng_ref,
             lnb_ref, wsg_ref, bsg_ref, z_ref, opre_ref, st_ref, st_scr, zbuf):
        @pl.when(pl.program_id(0) == 0)
        def _():
            st_scr[...] = jnp.zeros_like(st_scr)
            zbuf[0:8, :] = jnp.zeros((8, BRANCH), f32)

        lbv = lb_ref[...]
        gout_v = gout_ref[...]
        causal = _tri(HGRN_CHUNK)
        tri = causal.astype(f32)
        last_row = lax.broadcasted_iota(jnp.int32, (HGRN_CHUNK, 1), 0) == HGRN_CHUNK - 1
        for c in range(nch):
            rows = slice(HGRN_CHUNK * c, HGRN_CHUNK * (c + 1))
            q_raw = q_ref[rows, :]
            qs = q_raw * _sigmoid(q_raw)
            logf, _, kk = _hgrn_gates(fp_ref[rows, :], lbv)
            b = _dot_exact(tri, logf)
            bl = jnp.sum(jnp.where(last_row, b, 0.0), axis=0, keepdims=True)
            qb = (qs * jnp.exp(b)).astype(bf16)
            kb = (kk * jnp.exp(-b)).astype(bf16)
            kd = (kk * jnp.exp(bl - b)).astype(bf16)
            ebl = jnp.exp(bl)
            vc = iv_ref[rows, :].astype(bf16)
            gate = _sigmoid(go_ref[rows, :])
            for h in range(HEADS):
                sl = slice(HEAD_DIM * h, HEAD_DIM * (h + 1))
                st = st_scr[h]
                st_ref[c, h] = st
                a = jnp.where(causal, _dot_nt(qb[:, sl], kb[:, sl]), 0.0)
                o = _dot(a.astype(bf16), vc[:, sl]) + _dot_nt(qb[:, sl], st.astype(bf16))
                opre_ref[rows, sl] = o
                st_scr[h] = st * ebl[:, sl] + _dot_tn(vc[:, sl], kd[:, sl])
                _, oh = _rms_stats(o)
                z_ref[rows, sl] = (oh * gout_v[:, sl] * gate[:, sl]).astype(bf16)

        zc = cg_ref[...] * xc_ref[...]
        zbuf[8:8 + tt, :] = zc
        y = wconv_ref[0:1, :] * zbuf[pl.ds(6, tt), :] + wconv_ref[1:2, :] * zbuf[pl.ds(7, tt), :] + wconv_ref[2:3, :] * zc
        z_ref[:, BRANCH:2 * BRANCH] = (bg_ref[...] * y).astype(bf16)
        zbuf[0:8, :] = zbuf[tt:tt + 8, :]

        lng_v, lnb_v = lng_ref[...], lnb_ref[...]
        low = _tri(SG_CHUNK)
        wms = [jnp.where(low, wsg_ref[g], 0.0).astype(bf16) for g in range(SG_GROUPS)]
        for cc in range(tt // SG_CHUNK):
            rows = slice(SG_CHUNK * cc, SG_CHUNK * (cc + 1))
            ug = _gelu(u_ref[rows, :])
            vg = _gelu(v_ref[rows, :])
            vcen = vg - jnp.mean(vg, axis=-1, keepdims=True)
            rstd = lax.rsqrt(jnp.mean(vcen * vcen, axis=-1, keepdims=True) + LN_EPS)
            vn = (vcen * rstd * lng_v + lnb_v).astype(bf16)
            for g in range(SG_GROUPS):
                sl = slice(LANE * g, LANE * (g + 1))
                sv = _dot(wms[g], vn[:, sl]) + bsg_ref[:, g:g + 1]
                z_ref[rows, 2 * BRANCH + LANE * g:2 * BRANCH + LANE * (g + 1)] = (ug[:, sl] * sv).astype(bf16)

    full = lambda shape: pl.BlockSpec(shape, lambda i: (0,) * len(shape))
    return pl.pallas_call(
        body, name="mixer_fwd", grid=(s // tt,),
        in_specs=_p_specs(tt, range(9), lambda i: i) + [full((1, BRANCH)), full((1, BRANCH)), full((3, BRANCH)), full((1, BRANCH)),
                                                        full((1, BRANCH)), full((SG_GROUPS, SG_CHUNK, SG_CHUNK)), full((SG_CHUNK, SG_GROUPS))],
        out_specs=[pl.BlockSpec((tt, 3 * BRANCH), lambda i: (i, 0)), pl.BlockSpec((tt, BRANCH), lambda i: (i, 0)),
                   pl.BlockSpec((nch, HEADS, HEAD_DIM, HEAD_DIM), lambda i: (i, 0, 0, 0))],
        out_shape=[SDS((s, 3 * BRANCH), bf16), SDS((s, BRANCH), f32), SDS((s // HGRN_CHUNK, HEADS, HEAD_DIM, HEAD_DIM), f32)],
        scratch_shapes=[pltpu.VMEM((HEADS, HEAD_DIM, HEAD_DIM), f32), pltpu.VMEM((tt + 8, BRANCH), f32)],
        compiler_params=_cp("arbitrary"),
    )(*([p] * 9), lb, gout, wconv, lng, lnb, wsg, bsg_t)


def _branch_gate(z, wb, pg, x, wo, tm=512):
    s = z.shape[0]

    def body(z_ref, wb_ref, g_ref, x_ref, wo_ref, y_ref, m_ref, x1_ref):
        acc = None
        for n in range(3):
            cols = slice(D_MODEL * n, D_MODEL * (n + 1))
            yn = _dot(z_ref[:, BRANCH * n:BRANCH * (n + 1)], wb_ref[n])
            y_ref[:, cols] = yn.astype(bf16)
            t = _sigmoid(g_ref[:, cols].astype(f32)) * yn
            acc = t if acc is None else acc + t
        merged = acc.astype(bf16)
        m_ref[...] = merged
        x1_ref[...] = x_ref[...] + _dot(merged, wo_ref[...])

    row = pl.BlockSpec((tm, D_MODEL), lambda i: (i, 0))
    wide = pl.BlockSpec((tm, 3 * D_MODEL), lambda i: (i, 0))
    return pl.pallas_call(
        body, name="branch_gate", grid=(s // tm,),
        in_specs=[pl.BlockSpec((tm, 3 * BRANCH), lambda i: (i, 0)), pl.BlockSpec((3, BRANCH, D_MODEL), lambda i: (0, 0, 0)), wide, row,
                  pl.BlockSpec((D_MODEL, D_MODEL), lambda i: (0, 0))],
        out_specs=[wide, row, row],
        out_shape=[SDS((s, 3 * D_MODEL), bf16), SDS((s, D_MODEL), bf16), SDS((s, D_MODEL), f32)], compiler_params=_cp("parallel"),
    )(z, wb, pg, x, wo)


def _ffn(x1, g, w1, w2, tm=1024, tf=1024):
    s = x1.shape[0]
    nf = D_FF // tf

    def body(x_ref, g_ref, w1_ref, w2_ref, o_ref, h_ref, ra_ref, hs, acc):
        f = pl.program_id(1)

        @pl.when(f == 0)
        def _():
            _, xh = _rms_stats(x_ref[...])
            hv = (xh * g_ref[...]).astype(bf16)
            hs[...] = hv
            h_ref[...] = hv
            acc[...] = jnp.zeros_like(acc)

        ra = jnp.maximum(_dot(hs[...], w1_ref[...]), 0.0)
        ra_ref[...] = ra.astype(bf16)
        acc[...] += _dot((ra * ra).astype(bf16), w2_ref[...])

        @pl.when(f == nf - 1)
        def _():
            o_ref[...] = x_ref[...] + acc[...]

    return pl.pallas_call(
        body, name="ffn", grid=(s // tm, nf),
        in_specs=[pl.BlockSpec((tm, D_MODEL), lambda i, f: (i, 0)), pl.BlockSpec((1, D_MODEL), lambda i, f: (0, 0)),
                  pl.BlockSpec((D_MODEL, tf), lambda i, f: (0, f)), pl.BlockSpec((tf, D_MODEL), lambda i, f: (f, 0))],
        out_specs=[pl.BlockSpec((tm, D_MODEL), lambda i, f: (i, 0)), pl.BlockSpec((tm, D_MODEL), lambda i, f: (i, 0)),
                   pl.BlockSpec((tm, tf), lambda i, f: (i, f))],
        out_shape=[SDS((s, D_MODEL), f32), SDS((s, D_MODEL), bf16), SDS((s, D_FF), bf16)],
        scratch_shapes=[pltpu.VMEM((tm, D_MODEL), bf16), pltpu.VMEM((tm, D_MODEL), f32)], compiler_params=_cp("parallel", "arbitrary"),
    )(x1, g, w1, w2)


def _final(x, target, g, tm=512):
    s = x.shape[0]

    def body(x_ref, t_ref, g_ref, loss_ref, dx_ref, dxb_ref, dg_ref):
        first = pl.program_id(0) == 0
        gv = g_ref[...]
        r, xh = _rms_stats(x_ref[...])
        e = xh * gv - t_ref[...]
        tile_loss = 0.5 * jnp.sum(jnp.mean(e * e, axis=-1, keepdims=True), axis=0, keepdims=True)
        dx, dg = _rms_bwd(e * (1.0 / D_MODEL), xh, r, gv)
        dx_ref[...] = dx
        dxb_ref[...] = dx.astype(bf16)
        _acc_rows(dg_ref, first, dg)
        _acc_rows(loss_ref, first, jnp.broadcast_to(tile_loss, (1, LANE)))

    row = pl.BlockSpec((tm, D_MODEL), lambda i: (i, 0))
    return pl.pallas_call(
        body, name="final_loss", grid=(s // tm,), in_specs=[row, row, pl.BlockSpec((1, D_MODEL), lambda i: (0, 0))],
        out_specs=[pl.BlockSpec((1, LANE), lambda i: (0, 0)), row, row, pl.BlockSpec((1, D_MODEL), lambda i: (0, 0))],
        out_shape=[SDS((1, LANE), f32), SDS((s, D_MODEL), f32), SDS((s, D_MODEL), bf16), SDS((1, D_MODEL), f32)],
        compiler_params=_cp("arbitrary"),
    )(x, target, g)


def _ffn_bwd(dx2, dx2b, x1, g, ra, w1, w2, tm=512, tf=2048):
    s = x1.shape[0]
    nf = D_FF // tf

    def body(dx_ref, dxb_ref, x_ref, g_ref, ra_ref, w1_ref, w2_ref, da_ref, dx1_ref, dx1b_ref, dg_ref, acc):
        i, f = pl.program_id(0), pl.program_id(1)

        @pl.when(f == 0)
        def _():
            acc[...] = jnp.zeros_like(acc)

        da = (_dot_nt(dxb_ref[...], w2_ref[...]) * (2.0 * ra_ref[...].astype(f32))).astype(bf16)
        da_ref[...] = da
        acc[...] += _dot_nt(da, w1_ref[...])

        @pl.when(f == nf - 1)
        def _():
            r, xh = _rms_stats(x_ref[...])
            dx, dg = _rms_bwd(acc[...], xh, r, g_ref[...])
            dx = dx + dx_ref[...]
            dx1_ref[...] = dx
            dx1b_ref[...] = dx.astype(bf16)
            _acc_rows(dg_ref, i == 0, dg)

    row = pl.BlockSpec((tm, D_MODEL), lambda i, f: (i, 0))
    col = pl.BlockSpec((tm, tf), lambda i, f: (i, f))
    return pl.pallas_call(
        body, name="ffn_bwd", grid=(s // tm, nf),
        in_specs=[row, row, row, pl.BlockSpec((1, D_MODEL), lambda i, f: (0, 0)), col,
                  pl.BlockSpec((D_MODEL, tf), lambda i, f: (0, f)), pl.BlockSpec((tf, D_MODEL), lambda i, f: (f, 0))],
        out_specs=[col, row, row, pl.BlockSpec((1, D_MODEL), lambda i, f: (0, 0))],
        out_shape=[SDS((s, D_FF), bf16), SDS((s, D_MODEL), f32), SDS((s, D_MODEL), bf16), SDS((1, D_MODEL), f32)],
        scratch_shapes=[pltpu.VMEM((tm, D_MODEL), f32)], compiler_params=_cp("arbitrary", "arbitrary"),
    )(dx2, dx2b, x1, g, ra, w1, w2)


def _mm_tn(a, b, nb, m, n, tm, tn, name="mm_tn", square_a=False):
    s = a.shape[0]
    mi, nj = m // tm, n // tn

    def body(a_ref, b_ref, o_ref):
        av = a_ref[...]
        if square_a:
            av = av.astype(f32)
            av = (av * av).astype(bf16)
        o_ref[...] = _dot_tn(av, b_ref[...]).astype(bf16)

    return pl.pallas_call(
        body, name=name, grid=(nb, mi, nj),
        in_specs=[pl.BlockSpec((s, tm), lambda k, i, j: (0, k * mi + i)), pl.BlockSpec((s, tn), lambda k, i, j: (0, k * nj + j))],
        out_specs=pl.BlockSpec((None, tm, tn), lambda k, i, j: (k, i, j)), out_shape=SDS((nb, m, n), bf16),
        compiler_params=_cp("parallel", "parallel", "parallel"),
    )(a, b)


def _dw_in(dpm, dpg, h, tm=768):
    s = h.shape[0]
    km, kg = dpm.shape[1] // tm, dpg.shape[1] // tm

    def body(am_ref, ag_ref, h_ref, o_ref):
        i = pl.program_id(0)

        @pl.when(i < km)
        def _():
            o_ref[...] = _dot_tn(am_ref[...], h_ref[...]).astype(bf16)

        @pl.when(i >= km)
        def _():
            o_ref[...] = _dot_tn(ag_ref[...], h_ref[...]).astype(bf16)

    return pl.pallas_call(
        body, name="dw_in", grid=(km + kg,),
        in_specs=[pl.BlockSpec((s, tm), lambda i: (0, jnp.minimum(i, km - 1))), pl.BlockSpec((s, tm), lambda i: (0, jnp.maximum(i - km, 0))),
                  pl.BlockSpec((s, D_MODEL), lambda i: (0, 0))],
        out_specs=pl.BlockSpec((tm, D_MODEL), lambda i: (i, 0)), out_shape=SDS((km * tm + kg * tm, D_MODEL), bf16),
        compiler_params=_cp("arbitrary"),
    )(dpm, dpg, h)


def _mm_tn_slabs(a, b, nb, m, nblk, rel, width, tm=512, name="mm_tn_slabs"):
    s = a.shape[0]
    n = b.shape[1] // nb
    ng, mi, nw = n // nblk, m // tm, len(rel)

    def body(a_ref, b_ref, o_ref):
        full = _dot_tn(a_ref[...], b_ref[...])
        for r, start in enumerate(rel):
            o_ref[r] = full[:, start:start + width].astype(bf16)

    return pl.pallas_call(
        body, name=name, grid=(nb, ng, mi),
        in_specs=[pl.BlockSpec((s, tm), lambda k, g, i: (0, k * mi + i)), pl.BlockSpec((s, nblk), lambda k, g, i: (0, k * ng + g))],
        out_specs=pl.BlockSpec((nw, None, tm, width), lambda k, g, i: (g, k, i, 0)), out_shape=SDS((ng * nw, nb, m, width), bf16),
        compiler_params=_cp("parallel", "parallel", "parallel"),
    )(a, b)


def _merge_bwd(dx1b, wo, y, pg, wb, after, tm=512):
    s = dx1b.shape[0]

    def body(dx_ref, wo_ref, y_ref, g_ref, wb_ref, after_ref, dy_ref, dg_ref, dz_ref):
        del after_ref
        dm = _dot_nt(dx_ref[...], wo_ref[...])
        for n in range(3):
            cols = slice(D_MODEL * n, D_MODEL * (n + 1))
            gate = _sigmoid(g_ref[:, cols].astype(f32))
            t = dm * gate
            dy = t.astype(bf16)
            dy_ref[:, cols] = dy
            dg_ref[:, cols] = (t * y_ref[:, cols].astype(f32) * (1.0 - gate)).astype(bf16)
            dz_ref[:, BRANCH * n:BRANCH * (n + 1)] = _dot_nt(dy, wb_ref[n]).astype(bf16)

    wide = pl.BlockSpec((tm, 3 * D_MODEL), lambda i: (i, 0))
    return pl.pallas_call(
        body, name="merge_bwd", grid=(s // tm,),
        in_specs=[pl.BlockSpec((tm, D_MODEL), lambda i: (i, 0)), pl.BlockSpec((D_MODEL, D_MODEL), lambda i: (0, 0)), wide, wide,
                  pl.BlockSpec((3, BRANCH, D_MODEL), lambda i: (0, 0, 0)), pl.BlockSpec(memory_space=pl.ANY)],
        out_specs=[wide, wide, pl.BlockSpec((tm, 3 * BRANCH), lambda i: (i, 0))],
        out_shape=[SDS((s, 3 * D_MODEL), bf16), SDS((s, 3 * D_MODEL), bf16), SDS((s, 3 * BRANCH), bf16)],
        compiler_params=_cp("parallel"),
    )(dx1b, wo, y, pg, wb, after)


def _mixer_bwd(p, dz, opre, states, lb, gout, wconv, lng, lnb, wsg, bsg_t):
    s = p.shape[0]
    tt = MIX_TILE
    nt = s // tt
    nch = tt // HGRN_CHUNK
    rev = lambda i: nt - 1 - i

    def body(q_ref, fp_ref, iv_ref, go_ref, bg_ref, cg_ref, xc_ref, u_ref, v_ref, cgp_ref, xcp_ref, dz_ref, opre_ref, st_ref,
             lb_ref, gout_ref, wconv_ref, lng_ref, lnb_ref, wsg_ref, bsg_ref,
             dp_ref, vec_ref, dwsg_ref, dbsg_ref, dst_scr, zbuf, dybuf, dbsg_acc):
        i = pl.program_id(0)

        @pl.when(i == 0)
        def _():
            dst_scr[...] = jnp.zeros_like(dst_scr)
            dybuf[tt:tt + 8, :] = jnp.zeros((8, BRANCH), f32)
            vec_ref[...] = jnp.zeros_like(vec_ref)
            dwsg_ref[...] = jnp.zeros_like(dwsg_ref)
            dbsg_acc[...] = jnp.zeros_like(dbsg_acc)

        lbv = lb_ref[...]
        gout_v = gout_ref[...]
        causal = _tri(HGRN_CHUNK)
        tri = causal.astype(f32)
        tri_up = _tri(HGRN_CHUNK, upper=True).astype(f32)
        last_row = lax.broadcasted_iota(jnp.int32, (HGRN_CHUNK, 1), 0) == HGRN_CHUNK - 1
        lb_live = (lbv > LB_FLOOR).astype(f32)
        dlb = jnp.zeros((1, BRANCH), f32)
        dgout = jnp.zeros((1, BRANCH), f32)
        for c in reversed(range(nch)):
            rows = slice(HGRN_CHUNK * c, HGRN_CHUNK * (c + 1))
            q_c, fp = q_ref[rows, :], fp_ref[rows, :]
            sq_c = _sigmoid(q_c)
            sfp_c = _sigmoid(fp)
            logf, snf_c, kk = _hgrn_gates(fp, lbv)
            invf_c = jnp.exp(-logf)
            doa = dz_ref[rows, 0:BRANCH].astype(f32)
            o = opre_ref[rows, :]
            sgo = _sigmoid(go_ref[rows, :])
            d_o, dgo, dg_c = [], [], []
            for h in range(HEADS):
                sl = slice(HEAD_DIM * h, HEAD_DIM * (h + 1))
                r, oh = _rms_stats(o[:, sl])
                dgo.append(doa[:, sl] * oh * gout_v[:, sl] * sgo[:, sl] * (1.0 - sgo[:, sl]))
                dx, dg = _rms_bwd(doa[:, sl] * sgo[:, sl], oh, r, gout_v[:, sl])
                d_o.append(dx)
                dg_c.append(dg)
            dp_ref[rows, 3 * BRANCH:4 * BRANCH] = jnp.concatenate(dgo, axis=1).astype(bf16)
            dgout = dgout + jnp.concatenate(dg_c, axis=1)
            dob = jnp.concatenate(d_o, axis=1).astype(bf16)
            b = _dot_exact(tri, logf)
            bl = jnp.sum(jnp.where(last_row, b, 0.0), axis=0, keepdims=True)
            eb, enb, edl, ebl = jnp.exp(b), jnp.exp(-b), jnp.exp(bl - b), jnp.exp(bl)
            qbf, kbf, kdf = q_c * sq_c * eb, kk * enb, kk * edl
            qb, kb, kd = qbf.astype(bf16), kbf.astype(bf16), kdf.astype(bf16)
            vc = iv_ref[rows, :].astype(bf16)
            dv, dqb, dkb, dkd, debl = [], [], [], [], []
            for h in range(HEADS):
                sl = slice(HEAD_DIM * h, HEAD_DIM * (h + 1))
                st = st_ref[c, h]
                dst = dst_scr[h]
                stb, dstb = st.astype(bf16), dst.astype(bf16)
                a = jnp.where(causal, _dot_nt(qb[:, sl], kb[:, sl]), 0.0).astype(bf16)
                da = jnp.where(causal, _dot_nt(dob[:, sl], vc[:, sl]), 0.0).astype(bf16)
                dv.append(_dot_tn(a, dob[:, sl]) + _dot_nt(kd[:, sl], dstb))
                dqb.append(_dot(dob[:, sl], stb) + _dot(da, kb[:, sl]))
                dkb.append(_dot_tn(da, qb[:, sl]))
                dkd.append(_dot(vc[:, sl], dstb))
                debl.append(jnp.sum(st * dst, axis=0, keepdims=True))
                dst_scr[h] = _dot_tn(dob[:, sl], qb[:, sl]) + dst * ebl[:, sl]
            dv, dqb, dkb, dkd = (jnp.concatenate(t, axis=1) for t in (dv, dqb, dkb, dkd))
            debl = jnp.concatenate(debl, axis=1)
            t_kd = dkd * kdf
            dbl = ebl * debl + jnp.sum(t_kd, axis=0, keepdims=True)
            db = dqb * qbf - dkb * kbf - t_kd + jnp.where(last_row, dbl, 0.0)
            dkk = dkb * enb + dkd * edl
            dlc = _dot_exact(tri_up, db)
            slope = (1.0 - lbv) * sfp_c * snf_c
            dp_ref[rows, 0:BRANCH] = (dqb * eb * sq_c * (1.0 + q_c * (1.0 - sq_c))).astype(bf16)
            dp_ref[rows, BRANCH:2 * BRANCH] = (slope * (dlc * invf_c - dkk)).astype(bf16)
            dp_ref[rows, 2 * BRANCH:3 * BRANCH] = dv.astype(bf16)
            dlb = dlb + jnp.sum(dlc * (lb_live - sfp_c) * invf_c - dkk * snf_c, axis=0, keepdims=True)
        vec_ref[0:1, :] += dlb
        vec_ref[1:2, :] += dgout

        dob_ = dz_ref[:, BRANCH:2 * BRANCH].astype(f32)
        bg, cg, xc = bg_ref[...], cg_ref[...], xc_ref[...]
        zc = cg * xc
        zbuf[0:8, :] = jnp.where(i < nt - 1, cgp_ref[...] * xcp_ref[...], 0.0)
        zbuf[8:8 + tt, :] = zc
        w0, w1, w2 = wconv_ref[0:1, :], wconv_ref[1:2, :], wconv_ref[2:3, :]
        y = w0 * zbuf[pl.ds(6, tt), :] + w1 * zbuf[pl.ds(7, tt), :] + w2 * zc
        dy = dob_ * bg
        dybuf[0:tt, :] = dy
        dy1, dy2 = dybuf[pl.ds(1, tt), :], dybuf[pl.ds(2, tt), :]
        dzc = w2 * dy + w1 * dy1 + w0 * dy2
        dp_ref[:, 4 * BRANCH:5 * BRANCH] = (dob_ * y).astype(bf16)
        dp_ref[:, 5 * BRANCH:6 * BRANCH] = (dzc * xc).astype(bf16)
        dp_ref[:, 6 * BRANCH:7 * BRANCH] = (dzc * cg).astype(bf16)
        vec_ref[4:5, :] += jnp.sum(zc * dy2, axis=0, keepdims=True)
        vec_ref[5:6, :] += jnp.sum(zc * dy1, axis=0, keepdims=True)
        vec_ref[6:7, :] += jnp.sum(zc * dy, axis=0, keepdims=True)
        dybuf[tt:tt + 8, :] = dybuf[0:8, :]

        lng_v, lnb_v = lng_ref[...], lnb_ref[...]
        low = _tri(SG_CHUNK)
        wms = [jnp.where(low, wsg_ref[g], 0.0).astype(bf16) for g in range(SG_GROUPS)]
        dlng = jnp.zeros((1, BRANCH), f32)
        dlnb = jnp.zeros((1, BRANCH), f32)
        for cc in range(tt // SG_CHUNK):
            rows = slice(SG_CHUNK * cc, SG_CHUNK * (cc + 1))
            doc = dz_ref[rows, 2 * BRANCH:3 * BRANCH].astype(f32)
            u_raw, v_raw = u_ref[rows, :], v_ref[rows, :]
            ug = _gelu(u_raw)
            vg = _gelu(v_raw)
            vcen = vg - jnp.mean(vg, axis=-1, keepdims=True)
            rstd = lax.rsqrt(jnp.mean(vcen * vcen, axis=-1, keepdims=True) + LN_EPS)
            vhat = vcen * rstd
            vn = (vhat * lng_v + lnb_v).astype(bf16)
            dvn = []
            for g in range(SG_GROUPS):
                sl = slice(LANE * g, LANE * (g + 1))
                sv = _dot(wms[g], vn[:, sl]) + bsg_ref[:, g:g + 1]
                dp_ref[rows, 7 * BRANCH + LANE * g:7 * BRANCH + LANE * (g + 1)] = (doc[:, sl] * sv * _gelu_grad(u_raw[:, sl])).astype(bf16)
                dsv = doc[:, sl] * ug[:, sl]
                dsvb = dsv.astype(bf16)
                dbsg_acc[:, sl] += dsv
                dwsg_ref[g] += jnp.where(low, _dot_nt(dsvb, vn[:, sl]), 0.0)
                dvn.append(_dot_tn(wms[g], dsvb))
            dvn = jnp.concatenate(dvn, axis=1)
            dlng = dlng + jnp.sum(dvn * vhat, axis=0, keepdims=True)
            dlnb = dlnb + jnp.sum(dvn, axis=0, keepdims=True)
            dvh = dvn * lng_v
            dvg = rstd * (dvh - jnp.mean(dvh, axis=-1, keepdims=True) - vhat * jnp.mean(dvh * vhat, axis=-1, keepdims=True))
            dp_ref[rows, 8 * BRANCH:9 * BRANCH] = (dvg * _gelu_grad(v_raw)).astype(bf16)
        vec_ref[2:3, :] += dlng
        vec_ref[3:4, :] += dlnb

        @pl.when(i == nt - 1)
        def _():
            for g in range(SG_GROUPS):
                dbsg_ref[:, g:g + 1] = jnp.sum(dbsg_acc[:, LANE * g:LANE * (g + 1)], axis=1, keepdims=True)

    full = lambda shape: pl.BlockSpec(shape, lambda i: (0,) * len(shape))
    tail = lambda c: pl.BlockSpec((8, BRANCH), lambda i: (jnp.maximum(rev(i) * (tt // 8) - 1, 0), c))
    return pl.pallas_call(
        body, name="mixer_bwd", grid=(nt,),
        in_specs=_p_specs(tt, range(9), rev) + [tail(5), tail(6), pl.BlockSpec((tt, 3 * BRANCH), lambda i: (rev(i), 0)),
                                                pl.BlockSpec((tt, BRANCH), lambda i: (rev(i), 0)),
                                                pl.BlockSpec((nch, HEADS, HEAD_DIM, HEAD_DIM), lambda i: (rev(i), 0, 0, 0)),
                                                full((1, BRANCH)), full((1, BRANCH)), full((3, BRANCH)), full((1, BRANCH)), full((1, BRANCH)),
                                                full((SG_GROUPS, SG_CHUNK, SG_CHUNK)), full((SG_CHUNK, SG_GROUPS))],
        out_specs=[pl.BlockSpec((tt, 9 * BRANCH), lambda i: (rev(i), 0)), full((8, BRANCH)), full((SG_GROUPS, SG_CHUNK, SG_CHUNK)),
                   full((SG_CHUNK, SG_GROUPS))],
        out_shape=[SDS((s, 9 * BRANCH), bf16), SDS((8, BRANCH), f32), SDS((SG_GROUPS, SG_CHUNK, SG_CHUNK), f32), SDS((SG_CHUNK, SG_GROUPS), f32)],
        scratch_shapes=[pltpu.VMEM((HEADS, HEAD_DIM, HEAD_DIM), f32), pltpu.VMEM((tt + 8, BRANCH), f32), pltpu.VMEM((tt + 8, BRANCH), f32),
                        pltpu.VMEM((SG_CHUNK, BRANCH), f32)],
        compiler_params=_cp("arbitrary"),
    )(*([p] * 11), dz, opre, states, lb, gout, wconv, lng, lnb, wsg, bsg_t)


def _dh_bwd(dpm, dpg, w_t, x, dx1, g, after, tm=1024, tk=1536):
    s = x.shape[0]
    km = dpm.shape[1] // tk
    nk = km + dpg.shape[1] // tk

    def body(dpm_ref, dpg_ref, w_ref, x_ref, dx1_ref, g_ref, after_ref, dx_ref, dxb_ref, dg_ref, acc):
        del after_ref
        i, k = pl.program_id(0), pl.program_id(1)

        @pl.when(k == 0)
        def _():
            acc[...] = jnp.zeros_like(acc)

        @pl.when(k < km)
        def _():
            acc[...] += _dot(dpm_ref[...], w_ref[...])

        @pl.when(k >= km)
        def _():
            acc[...] += _dot(dpg_ref[...], w_ref[...])

        @pl.when(k == nk - 1)
        def _():
            r, xh = _rms_stats(x_ref[...])
            dx, dg = _rms_bwd(acc[...], xh, r, g_ref[...])
            dx = dx + dx1_ref[...]
            dx_ref[...] = dx
            dxb_ref[...] = dx.astype(bf16)
            _acc_rows(dg_ref, i == 0, dg)

    row = pl.BlockSpec((tm, D_MODEL), lambda i, k: (i, 0))
    vec = pl.BlockSpec((1, D_MODEL), lambda i, k: (0, 0))
    return pl.pallas_call(
        body, name="dh_bwd", grid=(s // tm, nk),
        in_specs=[pl.BlockSpec((tm, tk), lambda i, k: (i, jnp.minimum(k, km - 1))),
                  pl.BlockSpec((tm, tk), lambda i, k: (i, jnp.maximum(k - km, 0))),
                  pl.BlockSpec((tk, D_MODEL), lambda i, k: (k, 0)), row, row, vec, pl.BlockSpec(memory_space=pl.ANY)],
        out_specs=[row, row, vec], out_shape=[SDS((s, D_MODEL), f32), SDS((s, D_MODEL), bf16), SDS((1, D_MODEL), f32)],
        scratch_shapes=[pltpu.VMEM((tm, D_MODEL), f32)], compiler_params=_cp("arbitrary", "arbitrary"),
    )(dpm, dpg, w_t, x, dx1, g, after)


def _layer_fwd(x, weight, sm):
    p, pg, h = _rms_mm(x, sm["g_mix"], weight("w_in", x))
    z, opre, states = _mixer_fwd(p, sm["lb"], sm["g_out"], sm["w_conv"], sm["ln_g"], sm["ln_b"], sm["w_sg"], sm["b_sg_t"])
    y, merged, x1 = _branch_gate(z, weight("w_branch", z), pg, x, weight("w_o", z))
    x2, h2, ra = _ffn(x1, sm["g_ffn"], weight("w_ff1", x1), weight("w_ff2", x1))
    saved = dict(x=x, p=p, pg=pg, h=h, z=z, opre=opre, states=states, y=y, merged=merged, x1=x1, h2=h2, ra=ra)
    return x2, saved


def _layer_bwd(dx2, dx2b, sv, w, sm, between, before_end):
    nchip = N_DEV // 2
    by_chip = lambda g: g.reshape((nchip, 2) + g.shape[1:])
    da, dx1, dx1b, dg_ffn = _ffn_bwd(dx2, dx2b, sv["x1"], sm["g_ffn"], sv["ra"], w["w_ff1"], w["w_ff2"])
    g_ff2 = by_chip(_mm_tn(sv["ra"], dx2b, 1, D_FF, D_MODEL, 512, 1024, name="dw_ff2", square_a=True)[0]
                    .reshape(N_DEV, D_FF // N_DEV, D_MODEL))
    g_ff1 = by_chip(_mm_tn_slabs(sv["h2"], da, 1, D_MODEL, D_FF // 2, [i * (D_FF // N_DEV) for i in range(nchip)], D_FF // N_DEV,
                                 name="dw_ff1")[:, 0])
    g_o = by_chip(_mm_tn(sv["merged"], dx1b, 1, D_MODEL, D_MODEL, 512, 1024, name="dw_o")[0].reshape(N_DEV, D_MODEL // N_DEV, D_MODEL))
    dy, dpg, dz = _merge_bwd(dx1b, w["w_o"], sv["y"], sv["pg"], w["w_branch"], between(dx1))
    g_branch = by_chip(_mm_tn_slabs(sv["z"], dy, 3, BRANCH, D_MODEL, [i * (D_MODEL // N_DEV) for i in range(N_DEV)], D_MODEL // N_DEV,
                                    name="dw_branch"))
    dpm, vecs, dwsg, dbsg_t = _mixer_bwd(sv["p"], dz, sv["opre"], sv["states"], sm["lb"], sm["g_out"], sm["w_conv"],
                                         sm["ln_g"], sm["ln_b"], sm["w_sg"], sm["b_sg_t"])
    g_in = by_chip(_dw_in(dpm, dpg, sv["h"]).reshape(N_DEV, SHARD_IN, D_MODEL))
    big = dict(w_in=g_in, w_branch=g_branch, w_o=g_o, w_ff1=g_ff1, w_ff2=g_ff2)
    dx, dxb, dg_mix = _dh_bwd(dpm, dpg, w["w_in"], sv["x"], dx1, sm["g_mix"], before_end(big))
    small = dict(g_mix=dg_mix, g_ffn=dg_ffn, vecs=vecs, w_sg=dwsg, b_sg_t=dbsg_t, dx1=dx1)
    return dx, dxb, big, small


BIG = ("w_in", "w_branch", "w_o", "w_ff1", "w_ff2")
ANY = pl.BlockSpec(memory_space=pl.ANY)


def _place():
    return lax.axis_index("x"), lax.axis_index("y"), lax.axis_index("c")


def _al(v, m):
    return pl.multiple_of(v * m, m)


def _shard_of(refs, dev, which=range(len(BIG))):
    out = []
    for ref, t in zip(refs, which):
        by_cols = BIG[t] in ("w_branch", "w_ff1")
        n = ref.shape[-1 if by_cols else 0] // N_DEV
        part = pl.ds(_al(dev, n), n)
        out.append(ref.at[(slice(None),) * (len(ref.shape) - 1) + (part,)] if by_cols else ref.at[part])
    return out


def _gather_out_shapes(shards):
    s_in, s_b, s_o, s_1, s_2 = (shards[n] for n in BIG)
    return [SDS((s_in.shape[1] * N_DEV, s_in.shape[2]), bf16), SDS(s_b.shape[1:3] + (s_b.shape[3] * N_DEV,), bf16),
            SDS((s_o.shape[1] * N_DEV, s_o.shape[2]), bf16), SDS((s_1.shape[1], s_1.shape[2] * N_DEV), bf16),
            SDS((s_2.shape[1] * N_DEV, s_2.shape[2]), bf16)]


def _seq_all_gather_layer(layer, which, n_early, shard_refs, out_shapes, tag=""):
    nt = len(which)
    outs = [jax.empty_ref(sh, memory_space=pltpu.MemorySpace.HBM) for sh in out_shapes]
    early, late = tuple(range(n_early)), tuple(range(n_early, nt))

    @pl.kernel(mesh=plsc.ScalarSubcoreMesh(axis_name="seq", num_cores=1), name=f"seq_all_gather_l{layer}{tag}",
               scratch_types=(pltpu.SemaphoreType.DMA((9,)), pltpu.SemaphoreType.DMA((9,))),
               compiler_params=pltpu.CompilerParams(collective_id=1))
    def launch(send_sems, recv_sems):
        x, y, c = _place()
        me, sibling = (x, y, c), (x, y, 1 - c)
        first, second, diag = _ici_route(x, y, c)
        _handshake([sibling, first, second])
        mine = [r.at[layer] for r in shard_refs]

        def copies(k, blk, to, src=None, part=range(nt)):
            dst = _shard_of(outs, 4 * blk[0] + 2 * blk[1] + blk[2], which)
            src = dst if src is None else src
            return [pltpu.make_async_remote_copy(src_ref=src[t], dst_ref=dst[t], send_sem=send_sems.at[k], recv_sem=recv_sems.at[k],
                                                 device_id=to, device_id_type=MESH) for t in part]

        def start(cps):
            for cp in cps:
                cp.start()
            return cps

        def landed(cps):
            for cp in cps:
                cp.wait_recv()

        sent = start(copies(0, me, sibling, src=mine) + copies(1, me, first, src=mine, part=early)
                     + copies(2, me, first, src=mine, part=late) + copies(3, me, second, src=mine))
        landed(copies(1, first, me, part=early))
        sent += start(copies(4, first, second, part=early) + copies(6, first, sibling, part=early))
        landed(copies(2, first, me, part=late))
        sent += start(copies(5, first, second, part=late) + copies(6, first, sibling, part=late))
        landed(copies(3, second, me))
        sent += start(copies(7, second, sibling))
        landed(copies(4, diag, me, part=early) + copies(5, diag, me, part=late))
        sent += start(copies(8, diag, sibling))
        other = lambda p: (p[0], p[1], 1 - c)
        landed(copies(0, sibling, me) + copies(6, other(second), me) + copies(7, other(first), me) + copies(8, other(diag), me))
        for cp in sent:
            cp.wait_send()

    launch()
    return [o[...] for o in outs]


def _ici_route(x, y, c):
    return (x ^ (1 - c), y ^ c, c), (x ^ c, y ^ (1 - c), c), (1 - x, 1 - y, c)


def _place_own(where, which, shards, gathered, after):
    nt = len(which)

    def body(where_ref, *refs):
        del where_ref
        for src, dst in zip(refs[:nt], refs[2 * nt + 1:]):
            dst[...] = src[...]

    in_specs, out_specs = [], []
    for t, sh in zip(which, shards):
        blk = sh.shape[1:]
        in_specs.append(pl.BlockSpec((None,) + blk, functools.partial(lambda nd, i, wh: (wh[0],) + (0,) * nd, len(blk))))
        by_cols = BIG[t] in ("w_branch", "w_ff1")
        out_specs.append(pl.BlockSpec(blk, functools.partial(
            lambda nd, cols, i, wh: (0,) * (nd - 1) + (wh[1],) if cols else (wh[1],) + (0,) * (nd - 1), len(blk), by_cols)))
    return pl.pallas_call(
        body, name="place_own", out_shape=[SDS(g.shape, g.dtype) for g in gathered],
        input_output_aliases={1 + nt + i: i for i in range(nt)}, compiler_params=_cp("arbitrary"),
        grid_spec=pltpu.PrefetchScalarGridSpec(num_scalar_prefetch=1, grid=(1,), in_specs=in_specs + [ANY] * (nt + 1), out_specs=out_specs),
    )(where, *shards, *gathered, after)


def _handshake(peers):
    barrier = pltpu.get_barrier_semaphore()
    for p in peers:
        pl.semaphore_signal(barrier, inc=1, device_id=p, device_id_type=MESH)
    pl.semaphore_wait(barrier, len(peers))


def _seq_exchange_on_chip(grads):
    nt, nchip = len(BIG), N_DEV // 2
    g_refs = [jax.new_ref(g, memory_space=pltpu.MemorySpace.HBM) for g in grads]
    outs = [jax.empty_ref(SDS((nchip,) + g.shape[2:], bf16), memory_space=pltpu.MemorySpace.HBM) for g in grads]

    @pl.kernel(mesh=plsc.ScalarSubcoreMesh(axis_name="seq", num_cores=1), name="seq_rs_on_chip",
               scratch_types=(pltpu.SemaphoreType.DMA((nchip,)), pltpu.SemaphoreType.DMA((nchip,))),
               compiler_params=pltpu.CompilerParams(collective_id=2))
    def launch(send_sems, recv_sems):
        x, y, c = _place()
        sibling = (x, y, 1 - c)
        _handshake([sibling])
        remote = [pltpu.make_async_remote_copy(src_ref=g_refs[t].at[j, 1 - c], dst_ref=outs[t].at[j], send_sem=send_sems.at[j],
                                               recv_sem=recv_sems.at[j], device_id=sibling, device_id_type=MESH)
                  for j in range(nchip) for t in range(nt)]
        for cp in remote:
            cp.start()
        for cp in remote:
            cp.wait_recv()
        for cp in remote:
            cp.wait_send()

    launch()
    return [o[...] for o in outs], [g[...] for g in g_refs]


def _seq_exchange_between_chips(sums):
    nt = len(BIG)
    s_refs = [jax.new_ref(a, memory_space=pltpu.MemorySpace.HBM) for a in sums]
    outs = [jax.empty_ref(SDS((3,) + a.shape[1:], bf16), memory_space=pltpu.MemorySpace.HBM) for a in sums]
    transit = [jax.empty_ref(SDS(a.shape[1:], bf16), memory_space=pltpu.MemorySpace.HBM) for a in sums]

    early, late = (0,), tuple(range(1, nt))

    @pl.kernel(mesh=plsc.ScalarSubcoreMesh(axis_name="seq", num_cores=1), name="seq_rs_between_chips",
               scratch_types=(pltpu.SemaphoreType.DMA((6,)), pltpu.SemaphoreType.DMA((6,))),
               compiler_params=pltpu.CompilerParams(collective_id=3))
    def launch(send_sems, recv_sems):
        x, y, c = _place()
        first, second, diag = _ici_route(x, y, c)
        _handshake([first, second])

        def copies(k, src, dst, to, part=range(nt)):
            return [pltpu.make_async_remote_copy(src_ref=src(t), dst_ref=dst(t), send_sem=send_sems.at[k], recv_sem=recv_sems.at[k],
                                                 device_id=to, device_id_type=MESH) for t in part]

        chip_of = lambda p: 2 * p[0] + p[1]
        for_diag = lambda t: s_refs[t].at[chip_of(diag)]
        through = lambda t: transit[t]
        last = lambda t: outs[t].at[2]
        direct = (copies(0, lambda t: s_refs[t].at[chip_of(first)], lambda t: outs[t].at[0], first)
                  + copies(1, lambda t: s_refs[t].at[chip_of(second)], lambda t: outs[t].at[1], second))
        via = [copies(2, for_diag, through, first, early), copies(3, for_diag, through, first, late)]
        passed = [copies(4, through, last, second, early), copies(5, through, last, second, late)]
        for cp in via[0] + direct + via[1]:
            cp.start()
        for arrived, onward in zip(via, passed):
            for cp in arrived:
                cp.wait_recv()
            for cp in onward:
                cp.start()
        sent = direct + via[0] + via[1] + passed[0] + passed[1]
        for cp in direct + passed[0] + passed[1]:
            cp.wait_recv()
        for cp in sent:
            cp.wait_send()

    launch()
    return [o[...] for o in outs], [a[...] for a in s_refs]


def _chip_sums(place, mine, other, after, steps=2):
    nt, nchip = len(mine), mine[0].shape[0]
    m4 = [a.reshape(nchip, 2, -1, a.shape[-1]) for a in mine]
    o3 = [a.reshape(nchip, -1, a.shape[-1]) for a in other]

    def body(p_ref, *refs):
        del p_ref
        for a_ref, b_ref, o_ref in zip(refs[:nt], refs[nt:2 * nt], refs[2 * nt + 1:]):
            o_ref[...] = (a_ref[...].astype(f32) + b_ref[...].astype(f32)).astype(bf16)

    tiles = [(a.shape[1] // steps, a.shape[2]) for a in o3]
    blks = [pl.BlockSpec((None,) + t, lambda j, i, p_ref: (p_ref[1 + j], i, 0)) for t in tiles]
    outs = pl.pallas_call(
        body, name="chip_sums", out_shape=[SDS(a.shape, bf16) for a in o3], compiler_params=_cp("parallel", "parallel"),
        grid_spec=pltpu.PrefetchScalarGridSpec(
            num_scalar_prefetch=1, grid=(nchip - 1, steps),
            in_specs=[pl.BlockSpec((None, None) + t, lambda j, i, p_ref: (p_ref[1 + j], p_ref[0], i, 0)) for t in tiles] + blks + [ANY],
            out_specs=blks),
    )(place, *m4, *o3, after)
    return [o.reshape(a.shape) for o, a in zip(outs, other)]


def _all_reduce_rows(pack):
    rows = pack.shape[0]
    blk = rows // N_DEV

    def body(in_ref, out_ref, land, send1, recv1, send2, recv2):
        x, y, c = _place()
        me = 4 * x + 2 * y + c
        others = [(px, py, pc) for px in range(2) for py in range(2) for pc in range(2)]

        def is_me(p):
            return jnp.logical_and(jnp.logical_and(p[0] == x, p[1] == y), p[2] == c)

        land[me] = in_ref[pl.ds(_al(me, blk), blk), :]
        for d, p in enumerate(others):
            @pl.when(jnp.logical_not(is_me(p)))
            def _():
                pltpu.make_async_remote_copy(src_ref=in_ref.at[pl.ds(d * blk, blk), :], dst_ref=land.at[me], send_sem=send1.at[d],
                                             recv_sem=recv1.at[me], device_id=p, device_id_type=MESH).start()
        for d, p in enumerate(others):
            @pl.when(jnp.logical_not(is_me(p)))
            def _():
                cp = pltpu.make_async_remote_copy(src_ref=in_ref.at[pl.ds(d * blk, blk), :], dst_ref=land.at[d], send_sem=send1.at[d],
                                                  recv_sem=recv1.at[d], device_id=p, device_id_type=MESH)
                cp.wait_recv()
                cp.wait_send()
        total = land[0]
        for d in range(1, N_DEV):
            total = total + land[d]
        out_ref[pl.ds(_al(me, blk), blk), :] = total
        for d, p in enumerate(others):
            @pl.when(jnp.logical_not(is_me(p)))
            def _():
                mine = out_ref.at[pl.ds(_al(me, blk), blk), :]
                pltpu.make_async_remote_copy(src_ref=mine, dst_ref=mine, send_sem=send2.at[d], recv_sem=recv2.at[me],
                                             device_id=p, device_id_type=MESH).start()
        for d, p in enumerate(others):
            @pl.when(jnp.logical_not(is_me(p)))
            def _():
                theirs = out_ref.at[pl.ds(d * blk, blk), :]
                cp = pltpu.make_async_remote_copy(src_ref=theirs, dst_ref=theirs, send_sem=send2.at[d], recv_sem=recv2.at[d],
                                                  device_id=p, device_id_type=MESH)
                cp.wait_recv()
                cp.wait_send()

    vm = pl.BlockSpec(memory_space=pltpu.VMEM)
    return pl.pallas_call(
        body, name="all_reduce_rows", in_specs=[vm], out_specs=vm, out_shape=SDS((rows, LANE), f32),
        scratch_shapes=[pltpu.VMEM((N_DEV, blk, LANE), f32)] + [pltpu.SemaphoreType.DMA((N_DEV,))] * 4,
        compiler_params=pltpu.CompilerParams(vmem_limit_bytes=VMEM_LIMIT),
    )(pack)


def _lower_bounds_fwd(lower, after):
    def body(l_ref, *rest):
        o_ref = rest[-1]
        sm = _layer_softmax(l_ref)
        run = jnp.zeros_like(sm[0])
        for l in range(DEPTH):
            o_ref[l:l + 1, :] = run
            if l + 1 < DEPTH:
                run = run + sm[l + 1]

    vm = pl.BlockSpec(memory_space=pltpu.VMEM)
    return pl.pallas_call(body, name="lower_bounds_fwd", in_specs=[vm] + [ANY] * len(after), out_specs=vm,
                          out_shape=SDS(lower.shape, f32))(lower, *after)


def _layer_softmax(l_ref):
    rows = [l_ref[l:l + 1, :] for l in range(DEPTH)]
    top = functools.reduce(jnp.maximum, rows)
    e = [jnp.exp(r - top) for r in rows]
    tot = functools.reduce(lambda a, b: a + b, e)
    return [v / tot for v in e]


def _lower_bounds_bwd(lower, dlbs):
    def body(l_ref, d_ref, o_ref):
        sm = _layer_softmax(l_ref)
        dsm = [None] * DEPTH
        run = jnp.zeros_like(sm[0])
        dsm[0] = run
        for l in reversed(range(1, DEPTH)):
            run = run + d_ref[l:l + 1, :]
            dsm[l] = run
        inner = functools.reduce(lambda a, b: a + b, [sm[l] * dsm[l] for l in range(DEPTH)])
        for l in range(DEPTH):
            o_ref[l:l + 1, :] = sm[l] * (dsm[l] - inner)

    return pl.pallas_call(body, name="lower_bounds_bwd", out_shape=SDS(lower.shape, f32))(lower, dlbs)


_ADAM_C1 = 1.0 - ADAM_B1 ** ADAM_STEP
_ADAM_C2 = 1.0 - ADAM_B2 ** ADAM_STEP


def _adamw(w, g, m, v):
    m = ADAM_B1 * m + (1.0 - ADAM_B1) * g
    v = ADAM_B2 * v + (1.0 - ADAM_B2) * (g * g)
    delta = -ADAM_LR * ((m / _ADAM_C1) / (jnp.sqrt(v / _ADAM_C2) + ADAM_EPS) + ADAM_WD * w)
    return delta, m, v


def _adam_big(where, names, w, m, v, mine, other, landed, outs, after, steps=4):
    nt = len(names)
    three = lambda a: a.reshape(a.shape[0], -1, a.shape[-1])
    w3, m3, v3 = ([three(d[n]) for n in names] for d in (w, m, v))
    outs3 = [three(a) for n in names for a in outs[n]]
    mine4 = [a.reshape(a.shape[0], 2, -1, a.shape[-1]) for a in mine]
    other3 = [three(a) for a in other]
    land3 = [three(a) for a in landed]

    def body(where_ref, *refs):
        del where_ref
        o_refs = refs[6 * nt + 4 * nt + 1:]
        for t in range(nt):
            w_ref, m_ref, v_ref, mine_ref, other_ref, land_ref = (refs[q * nt + t] for q in range(6))
            g = mine_ref[...].astype(f32) + other_ref[...].astype(f32)
            for k in range(3):
                g = g + land_ref[k].astype(f32)
            delta, nm, nv = _adamw(w_ref[...], g, m_ref[...], v_ref[...])
            for o_ref, val in zip(o_refs[4 * t:4 * t + 4], (g, delta, nm, nv)):
                o_ref[...] = val

    tiles = [(a.shape[1] // steps, a.shape[2]) for a in w3]
    own = [pl.BlockSpec((None,) + t, lambda i, wh: (wh[0], i, 0)) for t in tiles]
    res = pl.pallas_call(
        body, name="adam_big", out_shape=[SDS(a.shape, f32) for a in outs3],
        input_output_aliases={1 + 6 * nt + i: i for i in range(4 * nt)}, compiler_params=_cp("parallel"),
        grid_spec=pltpu.PrefetchScalarGridSpec(
            num_scalar_prefetch=1, grid=(steps,),
            in_specs=own * 3 + [pl.BlockSpec((None, None) + t, lambda i, wh: (wh[1], wh[2], i, 0)) for t in tiles]
            + [pl.BlockSpec((None,) + t, lambda i, wh: (wh[1], i, 0)) for t in tiles]
            + [pl.BlockSpec((3,) + t, lambda i, wh: (0, i, 0)) for t in tiles] + [ANY] * (4 * nt + 1),
            out_specs=[s for s in own for _ in range(4)]),
    )(where, *w3, *m3, *v3, *mine4, *other3, *land3, *outs3, after)
    return {n: [o.reshape(w[n].shape) for o in res[4 * t:4 * t + 4]] for t, n in enumerate(names)}


def _touch(a, after):
    a2 = a.reshape(-1, a.shape[-1])

    def body(a_ref, after_ref, o_ref):
        del after_ref
        o_ref[...] = a_ref[0:8, :].astype(f32)

    return pl.pallas_call(
        body, name="touch", grid=(1,), in_specs=[pl.BlockSpec((16, LANE), lambda i: (0, 0)), ANY],
        out_specs=pl.BlockSpec((8, LANE), lambda i: (0, 0)), out_shape=SDS((8, LANE), f32),
    )(a2, after)


def _adam_rows(w, g, m, v):
    def body(w_ref, g_ref, m_ref, v_ref, d_ref, nm_ref, nv_ref):
        delta, nm, nv = _adamw(w_ref[...], g_ref[...], m_ref[...], v_ref[...])
        d_ref[...] = delta
        nm_ref[...] = nm
        nv_ref[...] = nv

    return pl.pallas_call(body, name="adam_rows", out_shape=[SDS(w.shape, f32)] * 3)(w, g, m, v)


SMALL = ("g_mix", "lower_bounds", "g_hgrn_out", "w_conv", "sg_ln_g", "sg_ln_b", "w_sg", "b_sg", "g_ffn", "g_final")
WEIGHTS = ("w_in", "g_mix", "lower_bounds", "g_hgrn_out", "w_conv", "sg_ln_g", "sg_ln_b", "w_sg", "b_sg", "w_branch", "w_o", "g_ffn",
           "w_ff1", "w_ff2", "g_final")


def _pack_rows(arrays, multiple):
    flat = jnp.concatenate([a.reshape(-1) for a in arrays])
    rows = -(-flat.shape[0] // (LANE * multiple)) * multiple
    return jnp.pad(flat, (0, rows * LANE - flat.shape[0])).reshape(rows, LANE)


def _unpack_rows(pack, like):
    flat = pack.reshape(-1)
    out, at = [], 0
    for a in like:
        out.append(flat[at:at + a.size].reshape(a.shape))
        at += a.size
    return out


def kernel(x, w_in, g_mix, lower_bounds, g_hgrn_out, w_conv, sg_ln_g, sg_ln_b, w_sg, b_sg, w_branch, w_o, g_ffn, w_ff1, w_ff2, g_final, loss_target, m_w_in, m_g_mix, m_lower_bounds, m_g_hgrn_out, m_w_conv, m_sg_ln_g, m_sg_ln_b, m_w_sg, m_b_sg, m_w_branch, m_w_o, m_g_ffn, m_w_ff1, m_w_ff2, m_g_final, v_w_in, v_g_mix, v_lower_bounds, v_g_hgrn_out, v_w_conv, v_sg_ln_g, v_sg_ln_b, v_w_sg, v_b_sg, v_w_branch, v_w_o, v_g_ffn, v_w_ff1, v_w_ff2, v_g_final):
    weights = dict(w_in=w_in, g_mix=g_mix, lower_bounds=lower_bounds, g_hgrn_out=g_hgrn_out, w_conv=w_conv, sg_ln_g=sg_ln_g,
                   sg_ln_b=sg_ln_b, w_sg=w_sg, b_sg=b_sg, w_branch=w_branch, w_o=w_o, g_ffn=g_ffn, w_ff1=w_ff1, w_ff2=w_ff2, g_final=g_final)
    mom1 = dict(w_in=m_w_in, g_mix=m_g_mix, lower_bounds=m_lower_bounds, g_hgrn_out=m_g_hgrn_out, w_conv=m_w_conv, sg_ln_g=m_sg_ln_g,
                sg_ln_b=m_sg_ln_b, w_sg=m_w_sg, b_sg=m_b_sg, w_branch=m_w_branch, w_o=m_w_o, g_ffn=m_g_ffn, w_ff1=m_w_ff1, w_ff2=m_w_ff2,
                g_final=m_g_final)
    mom2 = dict(w_in=v_w_in, g_mix=v_g_mix, lower_bounds=v_lower_bounds, g_hgrn_out=v_g_hgrn_out, w_conv=v_w_conv, sg_ln_g=v_sg_ln_g,
                sg_ln_b=v_sg_ln_b, w_sg=v_w_sg, b_sg=v_b_sg, w_branch=v_w_branch, w_o=v_w_o, g_ffn=v_g_ffn, w_ff1=v_w_ff1, w_ff2=v_w_ff2,
                g_final=v_g_final)
    xi, yi, ci = _place()
    dev = 4 * xi + 2 * yi + ci
    conv_cols = w_conv.shape[-1]

    for d in (weights, mom1, mom2):
        d["w_in"] = jnp.swapaxes(d["w_in"], 1, 2)
    shards = {n: weights[n].astype(bf16) for n in BIG}

    conv_place = lax.dynamic_update_slice(jnp.zeros((DEPTH, 3, BRANCH), f32), w_conv, (0, 0, dev * conv_cols))
    (w_conv_full,) = _unpack_rows(_all_reduce_rows(_pack_rows([conv_place], 8 * N_DEV)), [conv_place])
    small_packs = [_pack_rows([d[n] for n in SMALL], 8) for d in (weights, mom1, mom2)]
    lbs = _lower_bounds_fwd(lower_bounds, small_packs)

    def small_of(l):
        return dict(g_mix=g_mix[l][None], lb=lbs[l][None], g_out=g_hgrn_out[l][None], w_conv=w_conv_full[l], ln_g=sg_ln_g[l][None],
                    ln_b=sg_ln_b[l][None], w_sg=w_sg[l], b_sg_t=b_sg[l].T, g_ffn=g_ffn[l][None])

    act = x[0]
    full, saved = [], []
    shard_refs = [jax.new_ref(shards[n], memory_space=pltpu.MemorySpace.HBM) for n in BIG]
    shapes = _gather_out_shapes(shards)
    groups = [g for l in range(DEPTH) for g in ((l, (0,), 1, "a"), (l, (1, 2, 3, 4), 2, "b"))]
    arrived = {}
    for l, which, n_early, tag in groups:
        got = _seq_all_gather_layer(l, which, n_early, [shard_refs[t] for t in which], [shapes[t] for t in which], tag)
        arrived.update({(l, BIG[t]): (which, got) for t in which})

    for l in range(DEPTH):
        full.append({})

        def weight(name, after, l=l):
            if name not in full[l]:
                which, got = arrived[(l, name)]
                where = jnp.stack([jnp.int32(l), dev.astype(jnp.int32)])
                full[l].update(zip([BIG[t] for t in which], _place_own(where, which, [shards[BIG[t]] for t in which], got, after)))
            return full[l][name]

        act, sv = _layer_fwd(act, weight, small_of(l))
        saved.append(sv)
    loss_row, dx, dxb, dg_final = _final(act, loss_target[0], g_final[None])

    chip, core = (2 * xi + yi).astype(jnp.int32), ci.astype(jnp.int32)
    place = jnp.stack([core] + [(chip + k) % (N_DEV // 2) for k in (1, 2, 3)])
    big_out = {n: [lax.empty(weights[n].shape, f32) for _ in range(4)] for n in BIG}
    small_grads = [None] * DEPTH

    def chip_sums(stage, after):
        l, received, mine = stage
        sums = _chip_sums(place, mine, received, after)
        placed.append(sums[BIG.index("w_o")])
        landed, _ = _seq_exchange_between_chips(sums)
        return l, mine, received, landed

    def adam_layer(stage, after):
        l, mine, received, landed = stage
        where = jnp.stack([jnp.int32(l), chip, core])
        big_out.update(_adam_big(where, BIG, weights, mom1, mom2, mine, received, landed, big_out, after))

    above = None
    placed = []
    for l in reversed(range(DEPTH)):
        summed = []

        def between(dx1):
            if above is None:
                return dx1
            summed.append(chip_sums(above, dx1))
            return placed[-1]

        def before_end(big):
            return _touch(summed[0][3][BIG.index("w_o")], big["w_in"]) if summed else big["w_in"]

        dx, dxb, big, small_grads[l] = _layer_bwd(dx, dxb, saved[l], full[l], small_of(l), between, before_end)
        if summed:
            adam_layer(summed[0], dx)
        above = (l, *_seq_exchange_on_chip([big[n] for n in BIG]))

    stack = lambda f: jnp.stack([f(small_grads[l]) for l in range(DEPTH)])
    d_lower = _lower_bounds_bwd(lower_bounds, stack(lambda s: s["vecs"][0]))
    local_small = dict(g_mix=stack(lambda s: s["g_mix"][0]), lower_bounds=d_lower, g_hgrn_out=stack(lambda s: s["vecs"][1]),
                       w_conv=stack(lambda s: s["vecs"][4:7]), sg_ln_g=stack(lambda s: s["vecs"][2]), sg_ln_b=stack(lambda s: s["vecs"][3]),
                       w_sg=stack(lambda s: s["w_sg"]), b_sg=stack(lambda s: s["b_sg_t"].T), g_ffn=stack(lambda s: s["g_ffn"][0]),
                       g_final=dg_final[0])
    order = [local_small[n] for n in SMALL] + [loss_row]
    *reduced, loss_sum = _unpack_rows(_all_reduce_rows(_pack_rows(order, 8 * N_DEV)), order)
    loss = loss_sum[0, 0]
    grads = dict(zip(SMALL, reduced))
    grads["w_conv"] = lax.dynamic_slice(grads["w_conv"], (0, 0, dev * conv_cols), (DEPTH, 3, conv_cols))

    deltas, new_m, new_v = {}, {}, {}
    like = [weights[n] for n in SMALL]
    small_out = _adam_rows(small_packs[0], _pack_rows([grads[n] for n in SMALL], 8), small_packs[1], small_packs[2])
    for out, pack in zip((deltas, new_m, new_v), small_out):
        out.update(zip(SMALL, _unpack_rows(pack, like)))
    adam_layer(chip_sums(above, dx), small_out[0])
    for n in BIG:
        grads[n], deltas[n], new_m[n], new_v[n] = (jnp.swapaxes(a, 1, 2) if n == "w_in" else a for a in big_out[n])

    return (loss, dx[None], *[grads[n] for n in WEIGHTS], *[deltas[n] for n in WEIGHTS], *[new_m[n] for n in WEIGHTS],
            *[new_v[n] for n in WEIGHTS])
```

```python
import functools

import jax
import jax.numpy as jnp
from jax import lax
from jax.experimental import pallas as pl
from jax.experimental.pallas import tpu as pltpu
from jax.experimental.pallas import tpu_sc as plsc

f32 = jnp.float32
bf16 = jnp.bfloat16
SDS = jax.ShapeDtypeStruct
MESH = pl.DeviceIdType.MESH

D_MODEL = 1024
BRANCH = 512
N_COLS = 7680
D_FF = 4096
DEPTH = 4
HEADS = 4
HEAD_DIM = 128
HGRN_CHUNK = 64
SG_CHUNK = 128
SG_GROUPS = 4
NORM_EPS = 1e-6
LN_EPS = 1e-5
LB_FLOOR = 1e-30
N_DEV = 8
SHARD_IN = N_COLS // N_DEV
LANE = 128
GATE_COL0 = 9 * BRANCH

ADAM_LR = 0.001
ADAM_B1 = 0.9
ADAM_B2 = 0.999
ADAM_EPS = 1e-08
ADAM_WD = 0.01
ADAM_STEP = 10

MIX_TILE = 256
VMEM_LIMIT = 60 * 1024 * 1024


def _cp(*sem):
    return pltpu.CompilerParams(dimension_semantics=sem or None, vmem_limit_bytes=VMEM_LIMIT)


def _dot(a, b):
    return jnp.dot(a, b, preferred_element_type=f32)


def _dot_nt(a, b):
    return lax.dot_general(a, b, (((1,), (1,)), ((), ())), preferred_element_type=f32)


def _dot_tn(a, b):
    return lax.dot_general(a, b, (((0,), (0,)), ((), ())), preferred_element_type=f32)


def _dot_exact(ones, b):
    hi = b.astype(bf16)
    rest = b - hi.astype(f32)
    mid = rest.astype(bf16)
    low = (rest - mid.astype(f32)).astype(bf16)
    ones = ones.astype(bf16)
    return _dot(ones, hi) + _dot(ones, mid) + _dot(ones, low)


def _sigmoid(x):
    return jax.nn.sigmoid(x)


_GELU_C = 0.7978845608028654
_GELU_A = 0.044715


def _gelu(x):
    return 0.5 * x * (1.0 + jnp.tanh(_GELU_C * (x + _GELU_A * x * x * x)))


def _gelu_grad(x):
    x2 = x * x
    t = jnp.tanh(_GELU_C * (x + _GELU_A * x * x2))
    return 0.5 * (1.0 + t) + 0.5 * x * (1.0 - t * t) * _GELU_C * (1.0 + 3.0 * _GELU_A * x2)


def _rms_stats(x):
    r = lax.rsqrt(jnp.mean(x * x, axis=-1, keepdims=True) + NORM_EPS)
    return r, x * r


def _rms_bwd(dh, xh, r, g):
    dg = jnp.sum(dh * xh, axis=0, keepdims=True)
    dxn = dh * g
    dx = r * (dxn - xh * jnp.mean(dxn * xh, axis=-1, keepdims=True))
    return dx, dg


def _tri(n, upper=False):
    r = lax.broadcasted_iota(jnp.int32, (n, n), 0)
    c = lax.broadcasted_iota(jnp.int32, (n, n), 1)
    return (c >= r) if upper else (c <= r)


def _acc_rows(ref, first, val):
    @pl.when(first)
    def _():
        ref[...] = val

    @pl.when(jnp.logical_not(first))
    def _():
        ref[...] += val


def _rms_mm(x, g, w_t, tm=1024, tn=1536):
    s, n = x.shape[0], w_t.shape[0]
    jm = GATE_COL0 // tn

    def body(x_ref, g_ref, w_ref, pm_ref, pg_ref, h_ref, hs):
        j = pl.program_id(1)

        @pl.when(j == 0)
        def _():
            _, xh = _rms_stats(x_ref[...])
            hv = (xh * g_ref[...]).astype(bf16)
            hs[...] = hv
            h_ref[...] = hv

        res = _dot_nt(hs[...], w_ref[...])

        @pl.when(j < jm)
        def _():
            pm_ref[...] = res

        @pl.when(j >= jm)
        def _():
            pg_ref[...] = res.astype(bf16)

    return pl.pallas_call(
        body, name="rms_mm", grid=(s // tm, n // tn),
        in_specs=[pl.BlockSpec((tm, D_MODEL), lambda i, j: (i, 0)), pl.BlockSpec((1, D_MODEL), lambda i, j: (0, 0)),
                  pl.BlockSpec((tn, D_MODEL), lambda i, j: (j, 0))],
        out_specs=[pl.BlockSpec((tm, tn), lambda i, j: (i, jnp.minimum(j, jm - 1))),
                   pl.BlockSpec((tm, tn), lambda i, j: (i, jnp.maximum(j - jm, 0))), pl.BlockSpec((tm, D_MODEL), lambda i, j: (i, 0))],
        out_shape=[SDS((s, GATE_COL0), f32), SDS((s, n - GATE_COL0), bf16), SDS((s, D_MODEL), bf16)],
        scratch_shapes=[pltpu.VMEM((tm, D_MODEL), bf16)], compiler_params=_cp("parallel", "arbitrary"),
    )(x, g, w_t)


def _hgrn_gates(fp, lb):
    logf = jnp.logaddexp(jnp.log(jnp.maximum(lb, LB_FLOOR)), jnp.log1p(-lb) + jax.nn.log_sigmoid(fp))
    snf = _sigmoid(-fp)
    return logf, snf, (1.0 - lb) * snf


def _p_specs(tile, cols, row_map):
    return [pl.BlockSpec((tile, BRANCH), functools.partial(lambda c, i: (row_map(i), c), c)) for c in cols]


def _mixer_fwd(p, lb, gout, wconv, lng, lnb, wsg, bsg_t):
    s = p.shape[0]
    tt = MIX_TILE
    nch = tt // HGRN_CHUNK

    def body(q_ref, fp_ref, iv_ref, go_ref, bg_ref, cg_ref, xc_ref, u_ref, v_ref, lb_ref, gout_ref, wconv_ref, lng_ref,
             lnb_ref, wsg_ref, bsg_ref, z_ref, opre_ref, st_ref, st_scr, zbuf):
        @pl.when(pl.program_id(0) == 0)
        def _():
            st_scr[...] = jnp.zeros_like(st_scr)
            zbuf[0:8, :] = jnp.zeros((8, BRANCH), f32)

        lbv = lb_ref[...]
        gout_v = gout_ref[...]
        causal = _tri(HGRN_CHUNK)
        tri = causal.astype(f32)
        last_row = lax.broadcasted_iota(jnp.int32, (HGRN_CHUNK, 1), 0) == HGRN_CHUNK - 1
        for c in range(nch):
            rows = slice(HGRN_CHUNK * c, HGRN_CHUNK * (c + 1))
            q_raw = q_ref[rows, :]
            qs = q_raw * _sigmoid(q_raw)
            logf, _, kk = _hgrn_gates(fp_ref[rows, :], lbv)
            b = _dot_exact(tri, logf)
            bl = jnp.sum(jnp.where(last_row, b, 0.0), axis=0, keepdims=True)
            qb = (qs * jnp.exp(b)).astype(bf16)
            kb = (kk * jnp.exp(-b)).astype(bf16)
            kd = (kk * jnp.exp(bl - b)).astype(bf16)
            ebl = jnp.exp(bl)
            vc = iv_ref[rows, :].astype(bf16)
            gate = _sigmoid(go_ref[rows, :])
            for h in range(HEADS):
                sl = slice(HEAD_DIM * h, HEAD_DIM * (h + 1))
                st = st_scr[h]
                st_ref[c, h] = st
                a = jnp.where(causal, _dot_nt(qb[:, sl], kb[:, sl]), 0.0)
                o = _dot(a.astype(bf16), vc[:, sl]) + _dot_nt(qb[:, sl], st.astype(bf16))
                opre_ref[rows, sl] = o
                st_scr[h] = st * ebl[:, sl] + _dot_tn(vc[:, sl], kd[:, sl])
                _, oh = _rms_stats(o)
                z_ref[rows, sl] = (oh * gout_v[:, sl] * gate[:, sl]).astype(bf16)

        zc = cg_ref[...] * xc_ref[...]
        zbuf[8:8 + tt, :] = zc
        y = wconv_ref[0:1, :] * zbuf[pl.ds(6, tt), :] + wconv_ref[1:2, :] * zbuf[pl.ds(7, tt), :] + wconv_ref[2:3, :] * zc
        z_ref[:, BRANCH:2 * BRANCH] = (bg_ref[...] * y).astype(bf16)
        zbuf[0:8, :] = zbuf[tt:tt + 8, :]

        lng_v, lnb_v = lng_ref[...], lnb_ref[...]
        low = _tri(SG_CHUNK)
        wms = [jnp.where(low, wsg_ref[g], 0.0).astype(bf16) for g in range(SG_GROUPS)]
        for cc in range(tt // SG_CHUNK):
            rows = slice(SG_CHUNK * cc, SG_CHUNK * (cc + 1))
            ug = _gelu(u_ref[rows, :])
            vg = _gelu(v_ref[rows, :])
            vcen = vg - jnp.mean(vg, axis=-1, keepdims=True)
            rstd = lax.rsqrt(jnp.mean(vcen * vcen, axis=-1, keepdims=True) + LN_EPS)
            vn = (vcen * rstd * lng_v + lnb_v).astype(bf16)
            for g in range(SG_GROUPS):
                sl = slice(LANE * g, LANE * (g + 1))
                sv = _dot(wms[g], vn[:, sl]) + bsg_ref[:, g:g + 1]
                z_ref[rows, 2 * BRANCH + LANE * g:2 * BRANCH + LANE * (g + 1)] = (ug[:, sl] * sv).astype(bf16)

    full = lambda shape: pl.BlockSpec(shape, lambda i: (0,) * len(shape))
    return pl.pallas_call(
        body, name="mixer_fwd", grid=(s // tt,),
        in_specs=_p_specs(tt, range(9), lambda i: i) + [full((1, BRANCH)), full((1, BRANCH)), full((3, BRANCH)), full((1, BRANCH)),
                                                        full((1, BRANCH)), full((SG_GROUPS, SG_CHUNK, SG_CHUNK)), full((SG_CHUNK, SG_GROUPS))],
        out_specs=[pl.BlockSpec((tt, 3 * BRANCH), lambda i: (i, 0)), pl.BlockSpec((tt, BRANCH), lambda i: (i, 0)),
                   pl.BlockSpec((nch, HEADS, HEAD_DIM, HEAD_DIM), lambda i: (i, 0, 0, 0))],
        out_shape=[SDS((s, 3 * BRANCH), bf16), SDS((s, BRANCH), f32), SDS((s // HGRN_CHUNK, HEADS, HEAD_DIM, HEAD_DIM), f32)],
        scratch_shapes=[pltpu.VMEM((HEADS, HEAD_DIM, HEAD_DIM), f32), pltpu.VMEM((tt + 8, BRANCH), f32)],
        compiler_params=_cp("arbitrary"),
    )(*([p] * 9), lb, gout, wconv, lng, lnb, wsg, bsg_t)


def _branch_gate(z, wb, pg, x, wo, tm=512):
    s = z.shape[0]

    def body(z_ref, wb_ref, g_ref, x_ref, wo_ref, y_ref, m_ref, x1_ref):
        acc = None
        for n in range(3):
            cols = slice(D_MODEL * n, D_MODEL * (n + 1))
            yn = _dot(z_ref[:, BRANCH * n:BRANCH * (n + 1)], wb_ref[n])
            y_ref[:, cols] = yn.astype(bf16)
            t = _sigmoid(g_ref[:, cols].astype(f32)) * yn
            acc = t if acc is None else acc + t
        merged = acc.astype(bf16)
        m_ref[...] = merged
        x1_ref[...] = x_ref[...] + _dot(merged, wo_ref[...])

    row = pl.BlockSpec((tm, D_MODEL), lambda i: (i, 0))
    wide = pl.BlockSpec((tm, 3 * D_MODEL), lambda i: (i, 0))
    return pl.pallas_call(
        body, name="branch_gate", grid=(s // tm,),
        in_specs=[pl.BlockSpec((tm, 3 * BRANCH), lambda i: (i, 0)), pl.BlockSpec((3, BRANCH, D_MODEL), lambda i: (0, 0, 0)), wide, row,
                  pl.BlockSpec((D_MODEL, D_MODEL), lambda i: (0, 0))],
        out_specs=[wide, row, row],
        out_shape=[SDS((s, 3 * D_MODEL), bf16), SDS((s, D_MODEL), bf16), SDS((s, D_MODEL), f32)], compiler_params=_cp("parallel"),
    )(z, wb, pg, x, wo)


def _ffn(x1, g, w1, w2, tm=1024, tf=1024):
    s = x1.shape[0]
    nf = D_FF // tf

    def body(x_ref, g_ref, w1_ref, w2_ref, o_ref, h_ref, ra_ref, hs, acc):
        f = pl.program_id(1)

        @pl.when(f == 0)
        def _():
            _, xh = _rms_stats(x_ref[...])
            hv = (xh * g_ref[...]).astype(bf16)
            hs[...] = hv
            h_ref[...] = hv
            acc[...] = jnp.zeros_like(acc)

        ra = jnp.maximum(_dot(hs[...], w1_ref[...]), 0.0)
        ra_ref[...] = ra.astype(bf16)
        acc[...] += _dot((ra * ra).astype(bf16), w2_ref[...])

        @pl.when(f == nf - 1)
        def _():
            o_ref[...] = x_ref[...] + acc[...]

    return pl.pallas_call(
        body, name="ffn", grid=(s // tm, nf),
        in_specs=[pl.BlockSpec((tm, D_MODEL), lambda i, f: (i, 0)), pl.BlockSpec((1, D_MODEL), lambda i, f: (0, 0)),
                  pl.BlockSpec((D_MODEL, tf), lambda i, f: (0, f)), pl.BlockSpec((tf, D_MODEL), lambda i, f: (f, 0))],
        out_specs=[pl.BlockSpec((tm, D_MODEL), lambda i, f: (i, 0)), pl.BlockSpec((tm, D_MODEL), lambda i, f: (i, 0)),
                   pl.BlockSpec((tm, tf), lambda i, f: (i, f))],
        out_shape=[SDS((s, D_MODEL), f32), SDS((s, D_MODEL), bf16), SDS((s, D_FF), bf16)],
        scratch_shapes=[pltpu.VMEM((tm, D_MODEL), bf16), pltpu.VMEM((tm, D_MODEL), f32)], compiler_params=_cp("parallel", "arbitrary"),
    )(x1, g, w1, w2)


def _final(x, target, g, tm=512):
    s = x.shape[0]

    def body(x_ref, t_ref, g_ref, loss_ref, dx_ref, dxb_ref, dg_ref):
        first = pl.program_id(0) == 0
        gv = g_ref[...]
        r, xh = _rms_stats(x_ref[...])
        e = xh * gv - t_ref[...]
        tile_loss = 0.5 * jnp.sum(jnp.mean(e * e, axis=-1, keepdims=True), axis=0, keepdims=True)
        dx, dg = _rms_bwd(e * (1.0 / D_MODEL), xh, r, gv)
        dx_ref[...] = dx
        dxb_ref[...] = dx.astype(bf16)
        _acc_rows(dg_ref, first, dg)
        _acc_rows(loss_ref, first, jnp.broadcast_to(tile_loss, (1, LANE)))

    row = pl.BlockSpec((tm, D_MODEL), lambda i: (i, 0))
    return pl.pallas_call(
        body, name="final_loss", grid=(s // tm,), in_specs=[row, row, pl.BlockSpec((1, D_MODEL), lambda i: (0, 0))],
        out_specs=[pl.BlockSpec((1, LANE), lambda i: (0, 0)), row, row, pl.BlockSpec((1, D_MODEL), lambda i: (0, 0))],
        out_shape=[SDS((1, LANE), f32), SDS((s, D_MODEL), f32), SDS((s, D_MODEL), bf16), SDS((1, D_MODEL), f32)],
        compiler_params=_cp("arbitrary"),
    )(x, target, g)


def _ffn_bwd(dx2, dx2b, x1, g, ra, w1, w2, merged, tm=512, tf=2048):
    s = x1.shape[0]
    nf = D_FF // tf
    ni = s // tm

    def body(dx_ref, dxb_ref, x_ref, g_ref, ra_ref, w1_ref, w2_ref, m_ref, da_ref, dx1_ref, dx1b_ref, dg_ref, dwo_ref, acc, acc_o):
        i, f = pl.program_id(0), pl.program_id(1)

        @pl.when(f == 0)
        def _():
            acc[...] = jnp.zeros_like(acc)

        da = (_dot_nt(dxb_ref[...], w2_ref[...]) * (2.0 * ra_ref[...].astype(f32))).astype(bf16)
        da_ref[...] = da
        acc[...] += _dot_nt(da, w1_ref[...])

        @pl.when(f == nf - 1)
        def _():
            r, xh = _rms_stats(x_ref[...])
            dx, dg = _rms_bwd(acc[...], xh, r, g_ref[...])
            dx = dx + dx_ref[...]
            dxb = dx.astype(bf16)
            dx1_ref[...] = dx
            dx1b_ref[...] = dxb
            _acc_rows(dg_ref, i == 0, dg)
            _acc_rows(acc_o, i == 0, _dot_tn(m_ref[...], dxb))

            @pl.when(i == ni - 1)
            def _():
                dwo_ref[...] = acc_o[...].astype(bf16)

    row = pl.BlockSpec((tm, D_MODEL), lambda i, f: (i, 0))
    col = pl.BlockSpec((tm, tf), lambda i, f: (i, f))
    whole = pl.BlockSpec((D_MODEL, D_MODEL), lambda i, f: (0, 0))
    return pl.pallas_call(
        body, name="ffn_bwd", grid=(ni, nf),
        in_specs=[row, row, row, pl.BlockSpec((1, D_MODEL), lambda i, f: (0, 0)), col,
                  pl.BlockSpec((D_MODEL, tf), lambda i, f: (0, f)), pl.BlockSpec((tf, D_MODEL), lambda i, f: (f, 0)), row],
        out_specs=[col, row, row, pl.BlockSpec((1, D_MODEL), lambda i, f: (0, 0)), whole],
        out_shape=[SDS((s, D_FF), bf16), SDS((s, D_MODEL), f32), SDS((s, D_MODEL), bf16), SDS((1, D_MODEL), f32),
                   SDS((D_MODEL, D_MODEL), bf16)],
        scratch_shapes=[pltpu.VMEM((tm, D_MODEL), f32), pltpu.VMEM((D_MODEL, D_MODEL), f32)],
        compiler_params=_cp("arbitrary", "arbitrary"),
    )(dx2, dx2b, x1, g, ra, w1, w2, merged)


def _mm_tn(a, b, nb, m, n, tm, tn, name="mm_tn", square_a=False):
    s = a.shape[0]
    mi, nj = m // tm, n // tn

    def body(a_ref, b_ref, o_ref):
        av = a_ref[...]
        if square_a:
            av = av.astype(f32)
            av = (av * av).astype(bf16)
        o_ref[...] = _dot_tn(av, b_ref[...]).astype(bf16)

    return pl.pallas_call(
        body, name=name, grid=(nb, mi, nj),
        in_specs=[pl.BlockSpec((s, tm), lambda k, i, j: (0, k * mi + i)), pl.BlockSpec((s, tn), lambda k, i, j: (0, k * nj + j))],
        out_specs=pl.BlockSpec((None, tm, tn), lambda k, i, j: (k, i, j)), out_shape=SDS((nb, m, n), bf16),
        compiler_params=_cp("parallel", "parallel", "parallel"),
    )(a, b)


def _dw_in(dpm, dpg, h, tm=768):
    s = h.shape[0]
    km, kg = dpm.shape[1] // tm, dpg.shape[1] // tm

    def body(am_ref, ag_ref, h_ref, o_ref):
        i = pl.program_id(0)

        @pl.when(i < km)
        def _():
            o_ref[...] = _dot_tn(am_ref[...], h_ref[...]).astype(bf16)

        @pl.when(i >= km)
        def _():
            o_ref[...] = _dot_tn(ag_ref[...], h_ref[...]).astype(bf16)

    return pl.pallas_call(
        body, name="dw_in", grid=(km + kg,),
        in_specs=[pl.BlockSpec((s, tm), lambda i: (0, jnp.minimum(i, km - 1))), pl.BlockSpec((s, tm), lambda i: (0, jnp.maximum(i - km, 0))),
                  pl.BlockSpec((s, D_MODEL), lambda i: (0, 0))],
        out_specs=pl.BlockSpec((tm, D_MODEL), lambda i: (i, 0)), out_shape=SDS((km * tm + kg * tm, D_MODEL), bf16),
        compiler_params=_cp("arbitrary"),
    )(dpm, dpg, h)


def _mm_tn_slabs(a, b, nb, m, nblk, rel, width, tm=512, name="mm_tn_slabs"):
    s = a.shape[0]
    n = b.shape[1] // nb
    ng, mi, nw = n // nblk, m // tm, len(rel)

    def body(a_ref, b_ref, o_ref):
        full = _dot_tn(a_ref[...], b_ref[...])
        for r, start in enumerate(rel):
            o_ref[r] = full[:, start:start + width].astype(bf16)

    return pl.pallas_call(
        body, name=name, grid=(nb, ng, mi),
        in_specs=[pl.BlockSpec((s, tm), lambda k, g, i: (0, k * mi + i)), pl.BlockSpec((s, nblk), lambda k, g, i: (0, k * ng + g))],
        out_specs=pl.BlockSpec((nw, None, tm, width), lambda k, g, i: (g, k, i, 0)), out_shape=SDS((ng * nw, nb, m, width), bf16),
        compiler_params=_cp("parallel", "parallel", "parallel"),
    )(a, b)


def _merge_bwd(dx1b, wo, y, pg, wb, after, tm=512):
    s = dx1b.shape[0]

    def body(dx_ref, wo_ref, y_ref, g_ref, wb_ref, after_ref, dy_ref, dg_ref, dz_ref):
        del after_ref
        dm = _dot_nt(dx_ref[...], wo_ref[...])
        for n in range(3):
            cols = slice(D_MODEL * n, D_MODEL * (n + 1))
            gate = _sigmoid(g_ref[:, cols].astype(f32))
            t = dm * gate
            dy = t.astype(bf16)
            dy_ref[:, cols] = dy
            dg_ref[:, cols] = (t * y_ref[:, cols].astype(f32) * (1.0 - gate)).astype(bf16)
            dz_ref[:, BRANCH * n:BRANCH * (n + 1)] = _dot_nt(dy, wb_ref[n]).astype(bf16)

    wide = pl.BlockSpec((tm, 3 * D_MODEL), lambda i: (i, 0))
    return pl.pallas_call(
        body, name="merge_bwd", grid=(s // tm,),
        in_specs=[pl.BlockSpec((tm, D_MODEL), lambda i: (i, 0)), pl.BlockSpec((D_MODEL, D_MODEL), lambda i: (0, 0)), wide, wide,
                  pl.BlockSpec((3, BRANCH, D_MODEL), lambda i: (0, 0, 0)), pl.BlockSpec(memory_space=pl.ANY)],
        out_specs=[wide, wide, pl.BlockSpec((tm, 3 * BRANCH), lambda i: (i, 0))],
        out_shape=[SDS((s, 3 * D_MODEL), bf16), SDS((s, 3 * D_MODEL), bf16), SDS((s, 3 * BRANCH), bf16)],
        compiler_params=_cp("parallel"),
    )(dx1b, wo, y, pg, wb, after)


def _mixer_bwd(p, dz, opre, states, lb, gout, wconv, lng, lnb, wsg, bsg_t):
    s = p.shape[0]
    tt = MIX_TILE
    nt = s // tt
    nch = tt // HGRN_CHUNK
    rev = lambda i: nt - 1 - i

    def body(q_ref, fp_ref, iv_ref, go_ref, bg_ref, cg_ref, xc_ref, u_ref, v_ref, cgp_ref, xcp_ref, dz_ref, opre_ref, st_ref,
             lb_ref, gout_ref, wconv_ref, lng_ref, lnb_ref, wsg_ref, bsg_ref,
             dp_ref, vec_ref, dwsg_ref, dbsg_ref, dst_scr, zbuf, dybuf, dbsg_acc):
        i = pl.program_id(0)

        @pl.when(i == 0)
        def _():
            dst_scr[...] = jnp.zeros_like(dst_scr)
            dybuf[tt:tt + 8, :] = jnp.zeros((8, BRANCH), f32)
            vec_ref[...] = jnp.zeros_like(vec_ref)
            dwsg_ref[...] = jnp.zeros_like(dwsg_ref)
            dbsg_acc[...] = jnp.zeros_like(dbsg_acc)

        lbv = lb_ref[...]
        gout_v = gout_ref[...]
        causal = _tri(HGRN_CHUNK)
        tri = causal.astype(f32)
        tri_up = _tri(HGRN_CHUNK, upper=True).astype(f32)
        last_row = lax.broadcasted_iota(jnp.int32, (HGRN_CHUNK, 1), 0) == HGRN_CHUNK - 1
        lb_live = (lbv > LB_FLOOR).astype(f32)
        dlb = jnp.zeros((1, BRANCH), f32)
        dgout = jnp.zeros((1, BRANCH), f32)
        for c in reversed(range(nch)):
            rows = slice(HGRN_CHUNK * c, HGRN_CHUNK * (c + 1))
            q_c, fp = q_ref[rows, :], fp_ref[rows, :]
            sq_c = _sigmoid(q_c)
            sfp_c = _sigmoid(fp)
            logf, snf_c, kk = _hgrn_gates(fp, lbv)
            invf_c = jnp.exp(-logf)
            doa = dz_ref[rows, 0:BRANCH].astype(f32)
            o = opre_ref[rows, :]
            sgo = _sigmoid(go_ref[rows, :])
            d_o, dgo, dg_c = [], [], []
            for h in range(HEADS):
                sl = slice(HEAD_DIM * h, HEAD_DIM * (h + 1))
                r, oh = _rms_stats(o[:, sl])
                dgo.append(doa[:, sl] * oh * gout_v[:, sl] * sgo[:, sl] * (1.0 - sgo[:, sl]))
                dx, dg = _rms_bwd(doa[:, sl] * sgo[:, sl], oh, r, gout_v[:, sl])
                d_o.append(dx)
                dg_c.append(dg)
            dp_ref[rows, 3 * BRANCH:4 * BRANCH] = jnp.concatenate(dgo, axis=1).astype(bf16)
            dgout = dgout + jnp.concatenate(dg_c, axis=1)
            dob = jnp.concatenate(d_o, axis=1).astype(bf16)
            b = _dot_exact(tri, logf)
            bl = jnp.sum(jnp.where(last_row, b, 0.0), axis=0, keepdims=True)
            eb, enb, edl, ebl = jnp.exp(b), jnp.exp(-b), jnp.exp(bl - b), jnp.exp(bl)
            qbf, kbf, kdf = q_c * sq_c * eb, kk * enb, kk * edl
            qb, kb, kd = qbf.astype(bf16), kbf.astype(bf16), kdf.astype(bf16)
            vc = iv_ref[rows, :].astype(bf16)
            dv, dqb, dkb, dkd, debl = [], [], [], [], []
            for h in range(HEADS):
                sl = slice(HEAD_DIM * h, HEAD_DIM * (h + 1))
                st = st_ref[c, h]
                dst = dst_scr[h]
                stb, dstb = st.astype(bf16), dst.astype(bf16)
                a = jnp.where(causal, _dot_nt(qb[:, sl], kb[:, sl]), 0.0).astype(bf16)
                da = jnp.where(causal, _dot_nt(dob[:, sl], vc[:, sl]), 0.0).astype(bf16)
                dv.append(_dot_tn(a, dob[:, sl]) + _dot_nt(kd[:, sl], dstb))
                dqb.append(_dot(dob[:, sl], stb) + _dot(da, kb[:, sl]))
                dkb.append(_dot_tn(da, qb[:, sl]))
                dkd.append(_dot(vc[:, sl], dstb))
                debl.append(jnp.sum(st * dst, axis=0, keepdims=True))
                dst_scr[h] = _dot_tn(dob[:, sl], qb[:, sl]) + dst * ebl[:, sl]
            dv, dqb, dkb, dkd = (jnp.concatenate(t, axis=1) for t in (dv, dqb, dkb, dkd))
            debl = jnp.concatenate(debl, axis=1)
            t_kd = dkd * kdf
            dbl = ebl * debl + jnp.sum(t_kd, axis=0, keepdims=True)
            db = dqb * qbf - dkb * kbf - t_kd + jnp.where(last_row, dbl, 0.0)
            dkk = dkb * enb + dkd * edl
            dlc = _dot_exact(tri_up, db)
            slope = (1.0 - lbv) * sfp_c * snf_c
            dp_ref[rows, 0:BRANCH] = (dqb * eb * sq_c * (1.0 + q_c * (1.0 - sq_c))).astype(bf16)
            dp_ref[rows, BRANCH:2 * BRANCH] = (slope * (dlc * invf_c - dkk)).astype(bf16)
            dp_ref[rows, 2 * BRANCH:3 * BRANCH] = dv.astype(bf16)
            dlb = dlb + jnp.sum(dlc * (lb_live - sfp_c) * invf_c - dkk * snf_c, axis=0, keepdims=True)
        vec_ref[0:1, :] += dlb
        vec_ref[1:2, :] += dgout

        dob_ = dz_ref[:, BRANCH:2 * BRANCH].astype(f32)
        bg, cg, xc = bg_ref[...], cg_ref[...], xc_ref[...]
        zc = cg * xc
        zbuf[0:8, :] = jnp.where(i < nt - 1, cgp_ref[...] * xcp_ref[...], 0.0)
        zbuf[8:8 + tt, :] = zc
        w0, w1, w2 = wconv_ref[0:1, :], wconv_ref[1:2, :], wconv_ref[2:3, :]
        y = w0 * zbuf[pl.ds(6, tt), :] + w1 * zbuf[pl.ds(7, tt), :] + w2 * zc
        dy = dob_ * bg
        dybuf[0:tt, :] = dy
        dy1, dy2 = dybuf[pl.ds(1, tt), :], dybuf[pl.ds(2, tt), :]
        dzc = w2 * dy + w1 * dy1 + w0 * dy2
        dp_ref[:, 4 * BRANCH:5 * BRANCH] = (dob_ * y).astype(bf16)
        dp_ref[:, 5 * BRANCH:6 * BRANCH] = (dzc * xc).astype(bf16)
        dp_ref[:, 6 * BRANCH:7 * BRANCH] = (dzc * cg).astype(bf16)
        vec_ref[4:5, :] += jnp.sum(zc * dy2, axis=0, keepdims=True)
        vec_ref[5:6, :] += jnp.sum(zc * dy1, axis=0, keepdims=True)
        vec_ref[6:7, :] += jnp.sum(zc * dy, axis=0, keepdims=True)
        dybuf[tt:tt + 8, :] = dybuf[0:8, :]

        lng_v, lnb_v = lng_ref[...], lnb_ref[...]
        low = _tri(SG_CHUNK)
        wms = [jnp.where(low, wsg_ref[g], 0.0).astype(bf16) for g in range(SG_GROUPS)]
        dlng = jnp.zeros((1, BRANCH), f32)
        dlnb = jnp.zeros((1, BRANCH), f32)
        for cc in range(tt // SG_CHUNK):
            rows = slice(SG_CHUNK * cc, SG_CHUNK * (cc + 1))
            doc = dz_ref[rows, 2 * BRANCH:3 * BRANCH].astype(f32)
            u_raw, v_raw = u_ref[rows, :], v_ref[rows, :]
            ug = _gelu(u_raw)
            vg = _gelu(v_raw)
            vcen = vg - jnp.mean(vg, axis=-1, keepdims=True)
            rstd = lax.rsqrt(jnp.mean(vcen * vcen, axis=-1, keepdims=True) + LN_EPS)
            vhat = vcen * rstd
            vn = (vhat * lng_v + lnb_v).astype(bf16)
            dvn = []
            for g in range(SG_GROUPS):
                sl = slice(LANE * g, LANE * (g + 1))
                sv = _dot(wms[g], vn[:, sl]) + bsg_ref[:, g:g + 1]
                dp_ref[rows, 7 * BRANCH + LANE * g:7 * BRANCH + LANE * (g + 1)] = (doc[:, sl] * sv * _gelu_grad(u_raw[:, sl])).astype(bf16)
                dsv = doc[:, sl] * ug[:, sl]
                dsvb = dsv.astype(bf16)
                dbsg_acc[:, sl] += dsv
                dwsg_ref[g] += jnp.where(low, _dot_nt(dsvb, vn[:, sl]), 0.0)
                dvn.append(_dot_tn(wms[g], dsvb))
            dvn = jnp.concatenate(dvn, axis=1)
            dlng = dlng + jnp.sum(dvn * vhat, axis=0, keepdims=True)
            dlnb = dlnb + jnp.sum(dvn, axis=0, keepdims=True)
            dvh = dvn * lng_v
            dvg = rstd * (dvh - jnp.mean(dvh, axis=-1, keepdims=True) - vhat * jnp.mean(dvh * vhat, axis=-1, keepdims=True))
            dp_ref[rows, 8 * BRANCH:9 * BRANCH] = (dvg * _gelu_grad(v_raw)).astype(bf16)
        vec_ref[2:3, :] += dlng
        vec_ref[3:4, :] += dlnb

        @pl.when(i == nt - 1)
        def _():
            for g in range(SG_GROUPS):
                dbsg_ref[:, g:g + 1] = jnp.sum(dbsg_acc[:, LANE * g:LANE * (g + 1)], axis=1, keepdims=True)

    full = lambda shape: pl.BlockSpec(shape, lambda i: (0,) * len(shape))
    tail = lambda c: pl.BlockSpec((8, BRANCH), lambda i: (jnp.maximum(rev(i) * (tt // 8) - 1, 0), c))
    return pl.pallas_call(
        body, name="mixer_bwd", grid=(nt,),
        in_specs=_p_specs(tt, range(9), rev) + [tail(5), tail(6), pl.BlockSpec((tt, 3 * BRANCH), lambda i: (rev(i), 0)),
                                                pl.BlockSpec((tt, BRANCH), lambda i: (rev(i), 0)),
                                                pl.BlockSpec((nch, HEADS, HEAD_DIM, HEAD_DIM), lambda i: (rev(i), 0, 0, 0)),
                                                full((1, BRANCH)), full((1, BRANCH)), full((3, BRANCH)), full((1, BRANCH)), full((1, BRANCH)),
                                                full((SG_GROUPS, SG_CHUNK, SG_CHUNK)), full((SG_CHUNK, SG_GROUPS))],
        out_specs=[pl.BlockSpec((tt, 9 * BRANCH), lambda i: (rev(i), 0)), full((8, BRANCH)), full((SG_GROUPS, SG_CHUNK, SG_CHUNK)),
                   full((SG_CHUNK, SG_GROUPS))],
        out_shape=[SDS((s, 9 * BRANCH), bf16), SDS((8, BRANCH), f32), SDS((SG_GROUPS, SG_CHUNK, SG_CHUNK), f32), SDS((SG_CHUNK, SG_GROUPS), f32)],
        scratch_shapes=[pltpu.VMEM((HEADS, HEAD_DIM, HEAD_DIM), f32), pltpu.VMEM((tt + 8, BRANCH), f32), pltpu.VMEM((tt + 8, BRANCH), f32),
                        pltpu.VMEM((SG_CHUNK, BRANCH), f32)],
        compiler_params=_cp("arbitrary"),
    )(*([p] * 11), dz, opre, states, lb, gout, wconv, lng, lnb, wsg, bsg_t)


def _dh_bwd(dpm, dpg, w_t, x, dx1, g, after, tm=1024, tk=1536):
    s = x.shape[0]
    km = dpm.shape[1] // tk
    nk = km + dpg.shape[1] // tk

    def body(dpm_ref, dpg_ref, w_ref, x_ref, dx1_ref, g_ref, after_ref, dx_ref, dxb_ref, dg_ref, acc):
        del after_ref
        i, k = pl.program_id(0), pl.program_id(1)

        @pl.when(k == 0)
        def _():
            acc[...] = jnp.zeros_like(acc)

        @pl.when(k < km)
        def _():
            acc[...] += _dot(dpm_ref[...], w_ref[...])

        @pl.when(k >= km)
        def _():
            acc[...] += _dot(dpg_ref[...], w_ref[...])

        @pl.when(k == nk - 1)
        def _():
            r, xh = _rms_stats(x_ref[...])
            dx, dg = _rms_bwd(acc[...], xh, r, g_ref[...])
            dx = dx + dx1_ref[...]
            dx_ref[...] = dx
            dxb_ref[...] = dx.astype(bf16)
            _acc_rows(dg_ref, i == 0, dg)

    row = pl.BlockSpec((tm, D_MODEL), lambda i, k: (i, 0))
    vec = pl.BlockSpec((1, D_MODEL), lambda i, k: (0, 0))
    return pl.pallas_call(
        body, name="dh_bwd", grid=(s // tm, nk),
        in_specs=[pl.BlockSpec((tm, tk), lambda i, k: (i, jnp.minimum(k, km - 1))),
                  pl.BlockSpec((tm, tk), lambda i, k: (i, jnp.maximum(k - km, 0))),
                  pl.BlockSpec((tk, D_MODEL), lambda i, k: (k, 0)), row, row, vec, pl.BlockSpec(memory_space=pl.ANY)],
        out_specs=[row, row, vec], out_shape=[SDS((s, D_MODEL), f32), SDS((s, D_MODEL), bf16), SDS((1, D_MODEL), f32)],
        scratch_shapes=[pltpu.VMEM((tm, D_MODEL), f32)], compiler_params=_cp("arbitrary", "arbitrary"),
    )(dpm, dpg, w_t, x, dx1, g, after)


def _layer_fwd(x, weight, sm):
    p, pg, h = _rms_mm(x, sm["g_mix"], weight("w_in", x))
    z, opre, states = _mixer_fwd(p, sm["lb"], sm["g_out"], sm["w_conv"], sm["ln_g"], sm["ln_b"], sm["w_sg"], sm["b_sg_t"])
    y, merged, x1 = _branch_gate(z, weight("w_branch", z), pg, x, weight("w_o", z))
    x2, h2, ra = _ffn(x1, sm["g_ffn"], weight("w_ff1", x1), weight("w_ff2", x1))
    saved = dict(x=x, p=p, pg=pg, h=h, z=z, opre=opre, states=states, y=y, merged=merged, x1=x1, h2=h2, ra=ra)
    return x2, saved


def _layer_bwd(dx2, dx2b, sv, w, sm, between, before_end):
    nchip = N_DEV // 2
    by_chip = lambda g: g.reshape((nchip, 2) + g.shape[1:])
    da, dx1, dx1b, dg_ffn, g_o = _ffn_bwd(dx2, dx2b, sv["x1"], sm["g_ffn"], sv["ra"], w["w_ff1"], w["w_ff2"], sv["merged"])
    g_o = by_chip(g_o.reshape(N_DEV, D_MODEL // N_DEV, D_MODEL))
    g_ff2 = by_chip(_mm_tn(sv["ra"], dx2b, 1, D_FF, D_MODEL, 512, 1024, name="dw_ff2", square_a=True)[0]
                    .reshape(N_DEV, D_FF // N_DEV, D_MODEL))
    g_ff1 = by_chip(_mm_tn_slabs(sv["h2"], da, 1, D_MODEL, D_FF // 2, [i * (D_FF // N_DEV) for i in range(nchip)], D_FF // N_DEV,
                                 name="dw_ff1")[:, 0])
    dy, dpg, dz = _merge_bwd(dx1b, w["w_o"], sv["y"], sv["pg"], w["w_branch"], between(dx1))
    g_branch = by_chip(_mm_tn_slabs(sv["z"], dy, 3, BRANCH, D_MODEL, [i * (D_MODEL // N_DEV) for i in range(N_DEV)], D_MODEL // N_DEV,
                                    name="dw_branch"))
    dpm, vecs, dwsg, dbsg_t = _mixer_bwd(sv["p"], dz, sv["opre"], sv["states"], sm["lb"], sm["g_out"], sm["w_conv"],
                                         sm["ln_g"], sm["ln_b"], sm["w_sg"], sm["b_sg_t"])
    g_in = by_chip(_dw_in(dpm, dpg, sv["h"]).reshape(N_DEV, SHARD_IN, D_MODEL))
    big = dict(w_in=g_in, w_branch=g_branch, w_o=g_o, w_ff1=g_ff1, w_ff2=g_ff2)
    dx, dxb, dg_mix = _dh_bwd(dpm, dpg, w["w_in"], sv["x"], dx1, sm["g_mix"], before_end(big))
    small = dict(g_mix=dg_mix, g_ffn=dg_ffn, vecs=vecs, w_sg=dwsg, b_sg_t=dbsg_t, dx1=dx1)
    return dx, dxb, big, small


BIG = ("w_in", "w_branch", "w_o", "w_ff1", "w_ff2")
ANY = pl.BlockSpec(memory_space=pl.ANY)


def _place():
    return lax.axis_index("x"), lax.axis_index("y"), lax.axis_index("c")


def _al(v, m):
    return pl.multiple_of(v * m, m)


def _shard_of(refs, dev, which=range(len(BIG))):
    out = []
    for ref, t in zip(refs, which):
        by_cols = BIG[t] in ("w_branch", "w_ff1")
        n = ref.shape[-1 if by_cols else 0] // N_DEV
        part = pl.ds(_al(dev, n), n)
        out.append(ref.at[(slice(None),) * (len(ref.shape) - 1) + (part,)] if by_cols else ref.at[part])
    return out


def _gather_out_shapes(shards):
    s_in, s_b, s_o, s_1, s_2 = (shards[n] for n in BIG)
    return [SDS((s_in.shape[1] * N_DEV, s_in.shape[2]), bf16), SDS(s_b.shape[1:3] + (s_b.shape[3] * N_DEV,), bf16),
            SDS((s_o.shape[1] * N_DEV, s_o.shape[2]), bf16), SDS((s_1.shape[1], s_1.shape[2] * N_DEV), bf16),
            SDS((s_2.shape[1] * N_DEV, s_2.shape[2]), bf16)]


def _seq_all_gather_layer(layer, which, n_early, shard_refs, out_shapes, tag=""):
    nt = len(which)
    outs = [jax.empty_ref(sh, memory_space=pltpu.MemorySpace.HBM) for sh in out_shapes]
    early, late = tuple(range(n_early)), tuple(range(n_early, nt))

    @pl.kernel(mesh=plsc.ScalarSubcoreMesh(axis_name="seq", num_cores=1), name=f"seq_all_gather_l{layer}{tag}",
               scratch_types=(pltpu.SemaphoreType.DMA((9,)), pltpu.SemaphoreType.DMA((9,))),
               compiler_params=pltpu.CompilerParams(collective_id=1))
    def launch(send_sems, recv_sems):
        x, y, c = _place()
        me, sibling = (x, y, c), (x, y, 1 - c)
        first, second, diag = _ici_route(x, y, c)
        _handshake([sibling, first, second])
        mine = [r.at[layer] for r in shard_refs]

        def copies(k, blk, to, src=None, part=range(nt)):
            dst = _shard_of(outs, 4 * blk[0] + 2 * blk[1] + blk[2], which)
            src = dst if src is None else src
            return [pltpu.make_async_remote_copy(src_ref=src[t], dst_ref=dst[t], send_sem=send_sems.at[k], recv_sem=recv_sems.at[k],
                                                 device_id=to, device_id_type=MESH) for t in part]

        def start(cps):
            for cp in cps:
                cp.start()
            return cps

        def landed(cps):
            for cp in cps:
                cp.wait_recv()

        sent = start(copies(0, me, sibling, src=mine) + copies(1, me, first, src=mine, part=early)
                     + copies(2, me, first, src=mine, part=late) + copies(3, me, second, src=mine))
        landed(copies(1, first, me, part=early))
        sent += start(copies(4, first, second, part=early) + copies(6, first, sibling, part=early))
        landed(copies(2, first, me, part=late))
        sent += start(copies(5, first, second, part=late) + copies(6, first, sibling, part=late))
        landed(copies(3, second, me))
        sent += start(copies(7, second, sibling))
        landed(copies(4, diag, me, part=early) + copies(5, diag, me, part=late))
        sent += start(copies(8, diag, sibling))
        other = lambda p: (p[0], p[1], 1 - c)
        landed(copies(0, sibling, me) + copies(6, other(second), me) + copies(7, other(first), me) + copies(8, other(diag), me))
        for cp in sent:
            cp.wait_send()

    launch()
    return [o[...] for o in outs]


def _ici_route(x, y, c):
    return (x ^ (1 - c), y ^ c, c), (x ^ c, y ^ (1 - c), c), (1 - x, 1 - y, c)


def _place_own(where, which, shards, gathered, after):
    nt = len(which)

    def body(where_ref, *refs):
        del where_ref
        for src, dst in zip(refs[:nt], refs[2 * nt + 1:]):
            dst[...] = src[...]

    in_specs, out_specs = [], []
    for t, sh in zip(which, shards):
        blk = sh.shape[1:]
        in_specs.append(pl.BlockSpec((None,) + blk, functools.partial(lambda nd, i, wh: (wh[0],) + (0,) * nd, len(blk))))
        by_cols = BIG[t] in ("w_branch", "w_ff1")
        out_specs.append(pl.BlockSpec(blk, functools.partial(
            lambda nd, cols, i, wh: (0,) * (nd - 1) + (wh[1],) if cols else (wh[1],) + (0,) * (nd - 1), len(blk), by_cols)))
    return pl.pallas_call(
        body, name="place_own", out_shape=[SDS(g.shape, g.dtype) for g in gathered],
        input_output_aliases={1 + nt + i: i for i in range(nt)}, compiler_params=_cp("arbitrary"),
        grid_spec=pltpu.PrefetchScalarGridSpec(num_scalar_prefetch=1, grid=(1,), in_specs=in_specs + [ANY] * (nt + 1), out_specs=out_specs),
    )(where, *shards, *gathered, after)


def _handshake(peers):
    barrier = pltpu.get_barrier_semaphore()
    for p in peers:
        pl.semaphore_signal(barrier, inc=1, device_id=p, device_id_type=MESH)
    pl.semaphore_wait(barrier, len(peers))


def _seq_exchange_on_chip(grads):
    nt, nchip = len(BIG), N_DEV // 2
    g_refs = [jax.new_ref(g, memory_space=pltpu.MemorySpace.HBM) for g in grads]
    outs = [jax.empty_ref(SDS((nchip,) + g.shape[2:], bf16), memory_space=pltpu.MemorySpace.HBM) for g in grads]

    @pl.kernel(mesh=plsc.ScalarSubcoreMesh(axis_name="seq", num_cores=1), name="seq_rs_on_chip",
               scratch_types=(pltpu.SemaphoreType.DMA((nchip,)), pltpu.SemaphoreType.DMA((nchip,))),
               compiler_params=pltpu.CompilerParams(collective_id=2))
    def launch(send_sems, recv_sems):
        x, y, c = _place()
        sibling = (x, y, 1 - c)
        _handshake([sibling])
        remote = [pltpu.make_async_remote_copy(src_ref=g_refs[t].at[j, 1 - c], dst_ref=outs[t].at[j], send_sem=send_sems.at[j],
                                               recv_sem=recv_sems.at[j], device_id=sibling, device_id_type=MESH)
                  for j in range(nchip) for t in range(nt)]
        for cp in remote:
            cp.start()
        for cp in remote:
            cp.wait_recv()
        for cp in remote:
            cp.wait_send()

    launch()
    return [o[...] for o in outs], [g[...] for g in g_refs]


def _seq_exchange_between_chips(sums):
    nt = len(BIG)
    s_refs = [jax.new_ref(a, memory_space=pltpu.MemorySpace.HBM) for a in sums]
    outs = [jax.empty_ref(SDS((3,) + a.shape[1:], bf16), memory_space=pltpu.MemorySpace.HBM) for a in sums]
    transit = [jax.empty_ref(SDS(a.shape[1:], bf16), memory_space=pltpu.MemorySpace.HBM) for a in sums]

    early, late = (0,), tuple(range(1, nt))

    @pl.kernel(mesh=plsc.ScalarSubcoreMesh(axis_name="seq", num_cores=1), name="seq_rs_between_chips",
               scratch_types=(pltpu.SemaphoreType.DMA((6,)), pltpu.SemaphoreType.DMA((6,))),
               compiler_params=pltpu.CompilerParams(collective_id=3))
    def launch(send_sems, recv_sems):
        x, y, c = _place()
        first, second, diag = _ici_route(x, y, c)
        _handshake([first, second])

        def copies(k, src, dst, to, part=range(nt)):
            return [pltpu.make_async_remote_copy(src_ref=src(t), dst_ref=dst(t), send_sem=send_sems.at[k], recv_sem=recv_sems.at[k],
                                                 device_id=to, device_id_type=MESH) for t in part]

        chip_of = lambda p: 2 * p[0] + p[1]
        for_diag = lambda t: s_refs[t].at[chip_of(diag)]
        through = lambda t: transit[t]
        last = lambda t: outs[t].at[2]
        direct = (copies(0, lambda t: s_refs[t].at[chip_of(first)], lambda t: outs[t].at[0], first)
                  + copies(1, lambda t: s_refs[t].at[chip_of(second)], lambda t: outs[t].at[1], second))
        via = [copies(2, for_diag, through, first, early), copies(3, for_diag, through, first, late)]
        passed = [copies(4, through, last, second, early), copies(5, through, last, second, late)]
        for cp in via[0] + direct + via[1]:
            cp.start()
        for arrived, onward in zip(via, passed):
            for cp in arrived:
                cp.wait_recv()
            for cp in onward:
                cp.start()
        sent = direct + via[0] + via[1] + passed[0] + passed[1]
        for cp in direct + passed[0] + passed[1]:
            cp.wait_recv()
        for cp in sent:
            cp.wait_send()

    launch()
    return [o[...] for o in outs], [a[...] for a in s_refs]


def _chip_sums(place, mine, other, after, steps=2):
    nt, nchip = len(mine), mine[0].shape[0]
    m4 = [a.reshape(nchip, 2, -1, a.shape[-1]) for a in mine]
    o3 = [a.reshape(nchip, -1, a.shape[-1]) for a in other]

    def body(p_ref, *refs):
        del p_ref
        for a_ref, b_ref, o_ref in zip(refs[:nt], refs[nt:2 * nt], refs[2 * nt + 1:]):
            o_ref[...] = (a_ref[...].astype(f32) + b_ref[...].astype(f32)).astype(bf16)

    tiles = [(a.shape[1] // steps, a.shape[2]) for a in o3]
    blks = [pl.BlockSpec((None,) + t, lambda j, i, p_ref: (p_ref[1 + j], i, 0)) for t in tiles]
    outs = pl.pallas_call(
        body, name="chip_sums", out_shape=[SDS(a.shape, bf16) for a in o3], compiler_params=_cp("parallel", "parallel"),
        grid_spec=pltpu.PrefetchScalarGridSpec(
            num_scalar_prefetch=1, grid=(nchip - 1, steps),
            in_specs=[pl.BlockSpec((None, None) + t, lambda j, i, p_ref: (p_ref[1 + j], p_ref[0], i, 0)) for t in tiles] + blks + [ANY],
            out_specs=blks),
    )(place, *m4, *o3, after)
    return [o.reshape(a.shape) for o, a in zip(outs, other)]


def _all_reduce_rows(pack):
    rows = pack.shape[0]
    blk = rows // N_DEV

    def body(in_ref, out_ref, land, send1, recv1, send2, recv2):
        x, y, c = _place()
        me = 4 * x + 2 * y + c
        others = [(px, py, pc) for px in range(2) for py in range(2) for pc in range(2)]

        def is_me(p):
            return jnp.logical_and(jnp.logical_and(p[0] == x, p[1] == y), p[2] == c)

        land[me] = in_ref[pl.ds(_al(me, blk), blk), :]
        for d, p in enumerate(others):
            @pl.when(jnp.logical_not(is_me(p)))
            def _():
                pltpu.make_async_remote_copy(src_ref=in_ref.at[pl.ds(d * blk, blk), :], dst_ref=land.at[me], send_sem=send1.at[d],
                                             recv_sem=recv1.at[me], device_id=p, device_id_type=MESH).start()
        for d, p in enumerate(others):
            @pl.when(jnp.logical_not(is_me(p)))
            def _():
                cp = pltpu.make_async_remote_copy(src_ref=in_ref.at[pl.ds(d * blk, blk), :], dst_ref=land.at[d], send_sem=send1.at[d],
                                                  recv_sem=recv1.at[d], device_id=p, device_id_type=MESH)
                cp.wait_recv()
                cp.wait_send()
        total = land[0]
        for d in range(1, N_DEV):
            total = total + land[d]
        out_ref[pl.ds(_al(me, blk), blk), :] = total
        for d, p in enumerate(others):
            @pl.when(jnp.logical_not(is_me(p)))
            def _():
                mine = out_ref.at[pl.ds(_al(me, blk), blk), :]
                pltpu.make_async_remote_copy(src_ref=mine, dst_ref=mine, send_sem=send2.at[d], recv_sem=recv2.at[me],
                                             device_id=p, device_id_type=MESH).start()
        for d, p in enumerate(others):
            @pl.when(jnp.logical_not(is_me(p)))
            def _():
                theirs = out_ref.at[pl.ds(d * blk, blk), :]
                cp = pltpu.make_async_remote_copy(src_ref=theirs, dst_ref=theirs, send_sem=send2.at[d], recv_sem=recv2.at[d],
                                                  device_id=p, device_id_type=MESH)
                cp.wait_recv()
                cp.wait_send()

    vm = pl.BlockSpec(memory_space=pltpu.VMEM)
    return pl.pallas_call(
        body, name="all_reduce_rows", in_specs=[vm], out_specs=vm, out_shape=SDS((rows, LANE), f32),
        scratch_shapes=[pltpu.VMEM((N_DEV, blk, LANE), f32)] + [pltpu.SemaphoreType.DMA((N_DEV,))] * 4,
        compiler_params=pltpu.CompilerParams(vmem_limit_bytes=VMEM_LIMIT),
    )(pack)


def _lower_bounds_fwd(lower, after):
    def body(l_ref, *rest):
        o_ref = rest[-1]
        sm = _layer_softmax(l_ref)
        run = jnp.zeros_like(sm[0])
        for l in range(DEPTH):
            o_ref[l:l + 1, :] = run
            if l + 1 < DEPTH:
                run = run + sm[l + 1]

    vm = pl.BlockSpec(memory_space=pltpu.VMEM)
    return pl.pallas_call(body, name="lower_bounds_fwd", in_specs=[vm] + [ANY] * len(after), out_specs=vm,
                          out_shape=SDS(lower.shape, f32))(lower, *after)


def _layer_softmax(l_ref):
    rows = [l_ref[l:l + 1, :] for l in range(DEPTH)]
    top = functools.reduce(jnp.maximum, rows)
    e = [jnp.exp(r - top) for r in rows]
    tot = functools.reduce(lambda a, b: a + b, e)
    return [v / tot for v in e]


def _lower_bounds_bwd(lower, dlbs):
    def body(l_ref, d_ref, o_ref):
        sm = _layer_softmax(l_ref)
        dsm = [None] * DEPTH
        run = jnp.zeros_like(sm[0])
        dsm[0] = run
        for l in reversed(range(1, DEPTH)):
            run = run + d_ref[l:l + 1, :]
            dsm[l] = run
        inner = functools.reduce(lambda a, b: a + b, [sm[l] * dsm[l] for l in range(DEPTH)])
        for l in range(DEPTH):
            o_ref[l:l + 1, :] = sm[l] * (dsm[l] - inner)

    return pl.pallas_call(body, name="lower_bounds_bwd", out_shape=SDS(lower.shape, f32))(lower, dlbs)


_ADAM_C1 = 1.0 - ADAM_B1 ** ADAM_STEP
_ADAM_C2 = 1.0 - ADAM_B2 ** ADAM_STEP


def _adamw(w, g, m, v):
    m = ADAM_B1 * m + (1.0 - ADAM_B1) * g
    v = ADAM_B2 * v + (1.0 - ADAM_B2) * (g * g)
    delta = -ADAM_LR * ((m / _ADAM_C1) / (jnp.sqrt(v / _ADAM_C2) + ADAM_EPS) + ADAM_WD * w)
    return delta, m, v


def _adam_big(where, names, w, m, v, mine, other, landed, outs, after, steps=4):
    nt = len(names)
    three = lambda a: a.reshape(a.shape[0], -1, a.shape[-1])
    w3, m3, v3 = ([three(d[n]) for n in names] for d in (w, m, v))
    outs3 = [three(a) for n in names for a in outs[n]]
    mine4 = [a.reshape(a.shape[0], 2, -1, a.shape[-1]) for a in mine]
    other3 = [three(a) for a in other]
    land3 = [three(a) for a in landed]

    def body(where_ref, *refs):
        del where_ref
        o_refs = refs[6 * nt + 4 * nt + 1:]
        for t in range(nt):
            w_ref, m_ref, v_ref, mine_ref, other_ref, land_ref = (refs[q * nt + t] for q in range(6))
            g = mine_ref[...].astype(f32) + other_ref[...].astype(f32)
            for k in range(3):
                g = g + land_ref[k].astype(f32)
            delta, nm, nv = _adamw(w_ref[...], g, m_ref[...], v_ref[...])
            for o_ref, val in zip(o_refs[4 * t:4 * t + 4], (g, delta, nm, nv)):
                o_ref[...] = val

    tiles = [(a.shape[1] // steps, a.shape[2]) for a in w3]
    own = [pl.BlockSpec((None,) + t, lambda i, wh: (wh[0], i, 0)) for t in tiles]
    res = pl.pallas_call(
        body, name="adam_big", out_shape=[SDS(a.shape, f32) for a in outs3],
        input_output_aliases={1 + 6 * nt + i: i for i in range(4 * nt)}, compiler_params=_cp("parallel"),
        grid_spec=pltpu.PrefetchScalarGridSpec(
            num_scalar_prefetch=1, grid=(steps,),
            in_specs=own * 3 + [pl.BlockSpec((None, None) + t, lambda i, wh: (wh[1], wh[2], i, 0)) for t in tiles]
            + [pl.BlockSpec((None,) + t, lambda i, wh: (wh[1], i, 0)) for t in tiles]
            + [pl.BlockSpec((3,) + t, lambda i, wh: (0, i, 0)) for t in tiles] + [ANY] * (4 * nt + 1),
            out_specs=[s for s in own for _ in range(4)]),
    )(where, *w3, *m3, *v3, *mine4, *other3, *land3, *outs3, after)
    return {n: [o.reshape(w[n].shape) for o in res[4 * t:4 * t + 4]] for t, n in enumerate(names)}


def _touch(a, after):
    a2 = a.reshape(-1, a.shape[-1])

    def body(a_ref, after_ref, o_ref):
        del after_ref
        o_ref[...] = a_ref[0:8, :].astype(f32)

    return pl.pallas_call(
        body, name="touch", grid=(1,), in_specs=[pl.BlockSpec((16, LANE), lambda i: (0, 0)), ANY],
        out_specs=pl.BlockSpec((8, LANE), lambda i: (0, 0)), out_shape=SDS((8, LANE), f32),
    )(a2, after)


def _adam_rows(w, g, m, v):
    def body(w_ref, g_ref, m_ref, v_ref, d_ref, nm_ref, nv_ref):
        delta, nm, nv = _adamw(w_ref[...], g_ref[...], m_ref[...], v_ref[...])
        d_ref[...] = delta
        nm_ref[...] = nm
        nv_ref[...] = nv

    return pl.pallas_call(body, name="adam_rows", out_shape=[SDS(w.shape, f32)] * 3)(w, g, m, v)


SMALL = ("g_mix", "lower_bounds", "g_hgrn_out", "w_conv", "sg_ln_g", "sg_ln_b", "w_sg", "b_sg", "g_ffn", "g_final")
WEIGHTS = ("w_in", "g_mix", "lower_bounds", "g_hgrn_out", "w_conv", "sg_ln_g", "sg_ln_b", "w_sg", "b_sg", "w_branch", "w_o", "g_ffn",
           "w_ff1", "w_ff2", "g_final")


def _pack_rows(arrays, multiple):
    flat = jnp.concatenate([a.reshape(-1) for a in arrays])
    rows = -(-flat.shape[0] // (LANE * multiple)) * multiple
    return jnp.pad(flat, (0, rows * LANE - flat.shape[0])).reshape(rows, LANE)


def _unpack_rows(pack, like):
    flat = pack.reshape(-1)
    out, at = [], 0
    for a in like:
        out.append(flat[at:at + a.size].reshape(a.shape))
        at += a.size
    return out


def kernel(x, w_in, g_mix, lower_bounds, g_hgrn_out, w_conv, sg_ln_g, sg_ln_b, w_sg, b_sg, w_branch, w_o, g_ffn, w_ff1, w_ff2, g_final, loss_target, m_w_in, m_g_mix, m_lower_bounds, m_g_hgrn_out, m_w_conv, m_sg_ln_g, m_sg_ln_b, m_w_sg, m_b_sg, m_w_branch, m_w_o, m_g_ffn, m_w_ff1, m_w_ff2, m_g_final, v_w_in, v_g_mix, v_lower_bounds, v_g_hgrn_out, v_w_conv, v_sg_ln_g, v_sg_ln_b, v_w_sg, v_b_sg, v_w_branch, v_w_o, v_g_ffn, v_w_ff1, v_w_ff2, v_g_final):
    weights = dict(w_in=w_in, g_mix=g_mix, lower_bounds=lower_bounds, g_hgrn_out=g_hgrn_out, w_conv=w_conv, sg_ln_g=sg_ln_g,
                   sg_ln_b=sg_ln_b, w_sg=w_sg, b_sg=b_sg, w_branch=w_branch, w_o=w_o, g_ffn=g_ffn, w_ff1=w_ff1, w_ff2=w_ff2, g_final=g_final)
    mom1 = dict(w_in=m_w_in, g_mix=m_g_mix, lower_bounds=m_lower_bounds, g_hgrn_out=m_g_hgrn_out, w_conv=m_w_conv, sg_ln_g=m_sg_ln_g,
                sg_ln_b=m_sg_ln_b, w_sg=m_w_sg, b_sg=m_b_sg, w_branch=m_w_branch, w_o=m_w_o, g_ffn=m_g_ffn, w_ff1=m_w_ff1, w_ff2=m_w_ff2,
                g_final=m_g_final)
    mom2 = dict(w_in=v_w_in, g_mix=v_g_mix, lower_bounds=v_lower_bounds, g_hgrn_out=v_g_hgrn_out, w_conv=v_w_conv, sg_ln_g=v_sg_ln_g,
                sg_ln_b=v_sg_ln_b, w_sg=v_w_sg, b_sg=v_b_sg, w_branch=v_w_branch, w_o=v_w_o, g_ffn=v_g_ffn, w_ff1=v_w_ff1, w_ff2=v_w_ff2,
                g_final=v_g_final)
    xi, yi, ci = _place()
    dev = 4 * xi + 2 * yi + ci
    conv_cols = w_conv.shape[-1]

    for d in (weights, mom1, mom2):
        d["w_in"] = jnp.swapaxes(d["w_in"], 1, 2)
    shards = {n: weights[n].astype(bf16) for n in BIG}

    conv_place = lax.dynamic_update_slice(jnp.zeros((DEPTH, 3, BRANCH), f32), w_conv, (0, 0, dev * conv_cols))
    (w_conv_full,) = _unpack_rows(_all_reduce_rows(_pack_rows([conv_place], 8 * N_DEV)), [conv_place])
    small_packs = [_pack_rows([d[n] for n in SMALL], 8) for d in (weights, mom1, mom2)]
    lbs = _lower_bounds_fwd(lower_bounds, small_packs)

    def small_of(l):
        return dict(g_mix=g_mix[l][None], lb=lbs[l][None], g_out=g_hgrn_out[l][None], w_conv=w_conv_full[l], ln_g=sg_ln_g[l][None],
                    ln_b=sg_ln_b[l][None], w_sg=w_sg[l], b_sg_t=b_sg[l].T, g_ffn=g_ffn[l][None])

    act = x[0]
    full, saved = [], []
    shard_refs = [jax.new_ref(shards[n], memory_space=pltpu.MemorySpace.HBM) for n in BIG]
    shapes = _gather_out_shapes(shards)
    groups = [g for l in range(DEPTH) for g in ((l, (0,), 1, "a"), (l, (1, 2, 3, 4), 2, "b"))]
    arrived = {}
    for l, which, n_early, tag in groups:
        got = _seq_all_gather_layer(l, which, n_early, [shard_refs[t] for t in which], [shapes[t] for t in which], tag)
        arrived.update({(l, BIG[t]): (which, got) for t in which})

    for l in range(DEPTH):
        full.append({})

        def weight(name, after, l=l):
            if name not in full[l]:
                which, got = arrived[(l, name)]
                where = jnp.stack([jnp.int32(l), dev.astype(jnp.int32)])
                full[l].update(zip([BIG[t] for t in which], _place_own(where, which, [shards[BIG[t]] for t in which], got, after)))
            return full[l][name]

        act, sv = _layer_fwd(act, weight, small_of(l))
        saved.append(sv)
    loss_row, dx, dxb, dg_final = _final(act, loss_target[0], g_final[None])

    chip, core = (2 * xi + yi).astype(jnp.int32), ci.astype(jnp.int32)
    place = jnp.stack([core] + [(chip + k) % (N_DEV // 2) for k in (1, 2, 3)])
    big_out = {n: [lax.empty(weights[n].shape, f32) for _ in range(4)] for n in BIG}
    small_grads = [None] * DEPTH

    def chip_sums(stage, after):
        l, received, mine = stage
        sums = _chip_sums(place, mine, received, after)
        placed.append(sums[BIG.index("w_o")])
        landed, _ = _seq_exchange_between_chips(sums)
        return l, mine, received, landed

    def adam_layer(stage, after):
        l, mine, received, landed = stage
        where = jnp.stack([jnp.int32(l), chip, core])
        big_out.update(_adam_big(where, BIG, weights, mom1, mom2, mine, received, landed, big_out, after))

    above = None
    placed = []
    for l in reversed(range(DEPTH)):
        summed = []

        def between(dx1):
            if above is None:
                return dx1
            summed.append(chip_sums(above, dx1))
            return placed[-1]

        def before_end(big):
            return _touch(summed[0][3][BIG.index("w_o")], big["w_in"]) if summed else big["w_in"]

        dx, dxb, big, small_grads[l] = _layer_bwd(dx, dxb, saved[l], full[l], small_of(l), between, before_end)
        if summed:
            adam_layer(summed[0], dx)
        above = (l, *_seq_exchange_on_chip([big[n] for n in BIG]))

    stack = lambda f: jnp.stack([f(small_grads[l]) for l in range(DEPTH)])
    d_lower = _lower_bounds_bwd(lower_bounds, stack(lambda s: s["vecs"][0]))
    local_small = dict(g_mix=stack(lambda s: s["g_mix"][0]), lower_bounds=d_lower, g_hgrn_out=stack(lambda s: s["vecs"][1]),
                       w_conv=stack(lambda s: s["vecs"][4:7]), sg_ln_g=stack(lambda s: s["vecs"][2]), sg_ln_b=stack(lambda s: s["vecs"][3]),
                       w_sg=stack(lambda s: s["w_sg"]), b_sg=stack(lambda s: s["b_sg_t"].T), g_ffn=stack(lambda s: s["g_ffn"][0]),
                       g_final=dg_final[0])
    order = [local_small[n] for n in SMALL] + [loss_row]
    *reduced, loss_sum = _unpack_rows(_all_reduce_rows(_pack_rows(order, 8 * N_DEV)), order)
    loss = loss_sum[0, 0]
    grads = dict(zip(SMALL, reduced))
    grads["w_conv"] = lax.dynamic_slice(grads["w_conv"], (0, 0, dev * conv_cols), (DEPTH, 3, conv_cols))

    deltas, new_m, new_v = {}, {}, {}
    like = [weights[n] for n in SMALL]
    small_out = _adam_rows(small_packs[0], _pack_rows([grads[n] for n in SMALL], 8), small_packs[1], small_packs[2])
    for out, pack in zip((deltas, new_m, new_v), small_out):
        out.update(zip(SMALL, _unpack_rows(pack, like)))
    adam_layer(chip_sums(above, dx), small_out[0])
    for n in BIG:
        grads[n], deltas[n], new_m[n], new_v[n] = (jnp.swapaxes(a, 1, 2) if n == "w_in" else a for a in big_out[n])

    return (loss, dx[None], *[grads[n] for n in WEIGHTS], *[deltas[n] for n in WEIGHTS], *[new_m[n] for n in WEIGHTS],
            *[new_v[n] for n in WEIGHTS])
```

```python
import functools

import jax
import jax.numpy as jnp
from jax import lax
from jax.experimental import pallas as pl
from jax.experimental.pallas import tpu as pltpu
from jax.experimental.pallas import tpu_sc as plsc

f32 = jnp.float32
bf16 = jnp.bfloat16
SDS = jax.ShapeDtypeStruct
MESH = pl.DeviceIdType.MESH

D_MODEL = 1024
BRANCH = 512
N_COLS = 7680
D_FF = 4096
DEPTH = 4
HEADS = 4
HEAD_DIM = 128
HGRN_CHUNK = 64
SG_CHUNK = 128
SG_GROUPS = 4
NORM_EPS = 1e-6
LN_EPS = 1e-5
LB_FLOOR = 1e-30
N_DEV = 8
SHARD_IN = N_COLS // N_DEV
LANE = 128
GATE_COL0 = 9 * BRANCH

ADAM_LR = 0.001
ADAM_B1 = 0.9
ADAM_B2 = 0.999
ADAM_EPS = 1e-08
ADAM_WD = 0.01
ADAM_STEP = 10

MIX_TILE = 256
VMEM_LIMIT = 56 * 1024 * 1024


def _cp(*sem):
    return pltpu.CompilerParams(dimension_semantics=sem or None, vmem_limit_bytes=VMEM_LIMIT)


def _dot(a, b):
    return jnp.dot(a, b, preferred_element_type=f32)


def _dot_nt(a, b):
    return lax.dot_general(a, b, (((1,), (1,)), ((), ())), preferred_element_type=f32)


def _dot_tn(a, b):
    return lax.dot_general(a, b, (((0,), (0,)), ((), ())), preferred_element_type=f32)


def _dot_exact(ones, b):
    hi = b.astype(bf16)
    rest = b - hi.astype(f32)
    mid = rest.astype(bf16)
    low = (rest - mid.astype(f32)).astype(bf16)
    ones = ones.astype(bf16)
    return _dot(ones, hi) + _dot(ones, mid) + _dot(ones, low)


def _sigmoid(x):
    return jax.nn.sigmoid(x)


_GELU_C = 0.7978845608028654
_GELU_A = 0.044715


def _gelu(x):
    return 0.5 * x * (1.0 + jnp.tanh(_GELU_C * (x + _GELU_A * x * x * x)))


def _gelu_grad(x):
    x2 = x * x
    t = jnp.tanh(_GELU_C * (x + _GELU_A * x * x2))
    return 0.5 * (1.0 + t) + 0.5 * x * (1.0 - t * t) * _GELU_C * (1.0 + 3.0 * _GELU_A * x2)


def _rms_stats(x):
    r = lax.rsqrt(jnp.mean(x * x, axis=-1, keepdims=True) + NORM_EPS)
    return r, x * r


def _rms_bwd(dh, xh, r, g):
    dg = jnp.sum(dh * xh, axis=0, keepdims=True)
    dxn = dh * g
    dx = r * (dxn - xh * jnp.mean(dxn * xh, axis=-1, keepdims=True))
    return dx, dg


def _tri(n, upper=False):
    r = lax.broadcasted_iota(jnp.int32, (n, n), 0)
    c = lax.broadcasted_iota(jnp.int32, (n, n), 1)
    return (c >= r) if upper else (c <= r)


def _acc_rows(ref, first, val):
    @pl.when(first)
    def _():
        ref[...] = val

    @pl.when(jnp.logical_not(first))
    def _():
        ref[...] += val


def _rms_mm(x, g, w_t, tm=1024, tn=1536):
    s, n = x.shape[0], w_t.shape[0]
    jm = GATE_COL0 // tn

    def body(x_ref, g_ref, w_ref, pm_ref, pg_ref, h_ref, hs):
        j = pl.program_id(1)

        @pl.when(j == 0)
        def _():
            _, xh = _rms_stats(x_ref[...])
            hv = (xh * g_ref[...]).astype(bf16)
            hs[...] = hv
            h_ref[...] = hv

        res = _dot_nt(hs[...], w_ref[...])

        @pl.when(j < jm)
        def _():
            pm_ref[...] = res

        @pl.when(j >= jm)
        def _():
            pg_ref[...] = res.astype(bf16)

    return pl.pallas_call(
        body, name="rms_mm", grid=(s // tm, n // tn),
        in_specs=[pl.BlockSpec((tm, D_MODEL), lambda i, j: (i, 0)), pl.BlockSpec((1, D_MODEL), lambda i, j: (0, 0)),
                  pl.BlockSpec((tn, D_MODEL), lambda i, j: (j, 0))],
        out_specs=[pl.BlockSpec((tm, tn), lambda i, j: (i, jnp.minimum(j, jm - 1))),
                   pl.BlockSpec((tm, tn), lambda i, j: (i, jnp.maximum(j - jm, 0))), pl.BlockSpec((tm, D_MODEL), lambda i, j: (i, 0))],
        out_shape=[SDS((s, GATE_COL0), f32), SDS((s, n - GATE_COL0), bf16), SDS((s, D_MODEL), bf16)],
        scratch_shapes=[pltpu.VMEM((tm, D_MODEL), bf16)], compiler_params=_cp("parallel", "arbitrary"),
    )(x, g, w_t)


def _hgrn_gates(fp, lb):
    logf = jnp.logaddexp(jnp.log(jnp.maximum(lb, LB_FLOOR)), jnp.log1p(-lb) + jax.nn.log_sigmoid(fp))
    snf = _sigmoid(-fp)
    return logf, snf, (1.0 - lb) * snf


def _p_specs(tile, cols, row_map):
    return [pl.BlockSpec((tile, BRANCH), functools.partial(lambda c, i: (row_map(i), c), c)) for c in cols]


def _mixer_fwd(p, lb, gout, wconv, lng, lnb, wsg, bsg_t):
    s = p.shape[0]
    tt = MIX_TILE
    nch = tt // HGRN_CHUNK

    def body(q_ref, fp_ref, iv_ref, go_ref, bg_ref, cg_ref, xc_ref, u_ref, v_ref, lb_ref, gout_ref, wconv_ref, lng_ref,
             lnb_ref, wsg_ref, bsg_ref, z_ref, opre_ref, st_ref, st_scr, zbuf):
        @pl.when(pl.program_id(0) == 0)
        def _():
            st_scr[...] = jnp.zeros_like(st_scr)
            zbuf[0:8, :] = jnp.zeros((8, BRANCH), f32)

        lbv = lb_ref[...]
        gout_v = gout_ref[...]
        causal = _tri(HGRN_CHUNK)
        tri = causal.astype(f32)
        last_row = lax.broadcasted_iota(jnp.int32, (HGRN_CHUNK, 1), 0) == HGRN_CHUNK - 1
        for c in range(nch):
            rows = slice(HGRN_CHUNK * c, HGRN_CHUNK * (c + 1))
            q_raw = q_ref[rows, :]
            qs = q_raw * _sigmoid(q_raw)
            logf, _, kk = _hgrn_gates(fp_ref[rows, :], lbv)
            b = _dot_exact(tri, logf)
            bl = jnp.sum(jnp.where(last_row, b, 0.0), axis=0, keepdims=True)
            qb = (qs * jnp.exp(b)).astype(bf16)
            kb = (kk * jnp.exp(-b)).astype(bf16)
            kd = (kk * jnp.exp(bl - b)).astype(bf16)
            ebl = jnp.exp(bl)
            vc = iv_ref[rows, :].astype(bf16)
            gate = _sigmoid(go_ref[rows, :])
            for h in range(HEADS):
                sl = slice(HEAD_DIM * h, HEAD_DIM * (h + 1))
                st = st_scr[h]
                st_ref[c, h] = st
                a = jnp.where(causal, _dot_nt(qb[:, sl], kb[:, sl]), 0.0)
                o = _dot(a.astype(bf16), vc[:, sl]) + _dot_nt(qb[:, sl], st.astype(bf16))
                opre_ref[rows, sl] = o
                st_scr[h] = st * ebl[:, sl] + _dot_tn(vc[:, sl], kd[:, sl])
                _, oh = _rms_stats(o)
                z_ref[rows, sl] = (oh * gout_v[:, sl] * gate[:, sl]).astype(bf16)

        zc = cg_ref[...] * xc_ref[...]
        zbuf[8:8 + tt, :] = zc
        y = wconv_ref[0:1, :] * zbuf[pl.ds(6, tt), :] + wconv_ref[1:2, :] * zbuf[pl.ds(7, tt), :] + wconv_ref[2:3, :] * zc
        z_ref[:, BRANCH:2 * BRANCH] = (bg_ref[...] * y).astype(bf16)
        zbuf[0:8, :] = zbuf[tt:tt + 8, :]

        lng_v, lnb_v = lng_ref[...], lnb_ref[...]
        low = _tri(SG_CHUNK)
        wms = [jnp.where(low, wsg_ref[g], 0.0).astype(bf16) for g in range(SG_GROUPS)]
        for cc in range(tt // SG_CHUNK):
            rows = slice(SG_CHUNK * cc, SG_CHUNK * (cc + 1))
            ug = _gelu(u_ref[rows, :])
            vg = _gelu(v_ref[rows, :])
            vcen = vg - jnp.mean(vg, axis=-1, keepdims=True)
            rstd = lax.rsqrt(jnp.mean(vcen * vcen, axis=-1, keepdims=True) + LN_EPS)
            vn = (vcen * rstd * lng_v + lnb_v).astype(bf16)
            for g in range(SG_GROUPS):
                sl = slice(LANE * g, LANE * (g + 1))
                sv = _dot(wms[g], vn[:, sl]) + bsg_ref[:, g:g + 1]
                z_ref[rows, 2 * BRANCH + LANE * g:2 * BRANCH + LANE * (g + 1)] = (ug[:, sl] * sv).astype(bf16)

    full = lambda shape: pl.BlockSpec(shape, lambda i: (0,) * len(shape))
    return pl.pallas_call(
        body, name="mixer_fwd", grid=(s // tt,),
        in_specs=_p_specs(tt, range(9), lambda i: i) + [full((1, BRANCH)), full((1, BRANCH)), full((3, BRANCH)), full((1, BRANCH)),
                                                        full((1, BRANCH)), full((SG_GROUPS, SG_CHUNK, SG_CHUNK)), full((SG_CHUNK, SG_GROUPS))],
        out_specs=[pl.BlockSpec((tt, 3 * BRANCH), lambda i: (i, 0)), pl.BlockSpec((tt, BRANCH), lambda i: (i, 0)),
                   pl.BlockSpec((nch, HEADS, HEAD_DIM, HEAD_DIM), lambda i: (i, 0, 0, 0))],
        out_shape=[SDS((s, 3 * BRANCH), bf16), SDS((s, BRANCH), f32), SDS((s // HGRN_CHUNK, HEADS, HEAD_DIM, HEAD_DIM), f32)],
        scratch_shapes=[pltpu.VMEM((HEADS, HEAD_DIM, HEAD_DIM), f32), pltpu.VMEM((tt + 8, BRANCH), f32)],
        compiler_params=_cp("arbitrary"),
    )(*([p] * 9), lb, gout, wconv, lng, lnb, wsg, bsg_t)


def _branch_gate(z, wb, pg, x, wo, tm=512):
    s = z.shape[0]

    def body(z_ref, wb_ref, g_ref, x_ref, wo_ref, y_ref, m_ref, x1_ref):
        acc = None
        for n in range(3):
            cols = slice(D_MODEL * n, D_MODEL * (n + 1))
            yn = _dot(z_ref[:, BRANCH * n:BRANCH * (n + 1)], wb_ref[n])
            y_ref[:, cols] = yn.astype(bf16)
            t = _sigmoid(g_ref[:, cols].astype(f32)) * yn
            acc = t if acc is None else acc + t
        merged = acc.astype(bf16)
        m_ref[...] = merged
        x1_ref[...] = x_ref[...] + _dot(merged, wo_ref[...])

    row = pl.BlockSpec((tm, D_MODEL), lambda i: (i, 0))
    wide = pl.BlockSpec((tm, 3 * D_MODEL), lambda i: (i, 0))
    return pl.pallas_call(
        body, name="branch_gate", grid=(s // tm,),
        in_specs=[pl.BlockSpec((tm, 3 * BRANCH), lambda i: (i, 0)), pl.BlockSpec((3, BRANCH, D_MODEL), lambda i: (0, 0, 0)), wide, row,
                  pl.BlockSpec((D_MODEL, D_MODEL), lambda i: (0, 0))],
        out_specs=[wide, row, row],
        out_shape=[SDS((s, 3 * D_MODEL), bf16), SDS((s, D_MODEL), bf16), SDS((s, D_MODEL), f32)], compiler_params=_cp("parallel"),
    )(z, wb, pg, x, wo)


def _ffn(x1, g, w1, w2, tm=1024, tf=1024):
    s = x1.shape[0]
    nf = D_FF // tf

    def body(x_ref, g_ref, w1_ref, w2_ref, o_ref, h_ref, ra_ref, hs, acc):
        f = pl.program_id(1)

        @pl.when(f == 0)
        def _():
            _, xh = _rms_stats(x_ref[...])
            hv = (xh * g_ref[...]).astype(bf16)
            hs[...] = hv
            h_ref[...] = hv
            acc[...] = jnp.zeros_like(acc)

        ra = jnp.maximum(_dot(hs[...], w1_ref[...]), 0.0)
        ra_ref[...] = ra.astype(bf16)
        acc[...] += _dot((ra * ra).astype(bf16), w2_ref[...])

        @pl.when(f == nf - 1)
        def _():
            o_ref[...] = x_ref[...] + acc[...]

    return pl.pallas_call(
        body, name="ffn", grid=(s // tm, nf),
        in_specs=[pl.BlockSpec((tm, D_MODEL), lambda i, f: (i, 0)), pl.BlockSpec((1, D_MODEL), lambda i, f: (0, 0)),
                  pl.BlockSpec((D_MODEL, tf), lambda i, f: (0, f)), pl.BlockSpec((tf, D_MODEL), lambda i, f: (f, 0))],
        out_specs=[pl.BlockSpec((tm, D_MODEL), lambda i, f: (i, 0)), pl.BlockSpec((tm, D_MODEL), lambda i, f: (i, 0)),
                   pl.BlockSpec((tm, tf), lambda i, f: (i, f))],
        out_shape=[SDS((s, D_MODEL), f32), SDS((s, D_MODEL), bf16), SDS((s, D_FF), bf16)],
        scratch_shapes=[pltpu.VMEM((tm, D_MODEL), bf16), pltpu.VMEM((tm, D_MODEL), f32)], compiler_params=_cp("parallel", "arbitrary"),
    )(x1, g, w1, w2)


def _final(x, target, g, tm=512):
    s = x.shape[0]

    def body(x_ref, t_ref, g_ref, loss_ref, dx_ref, dxb_ref, dg_ref):
        first = pl.program_id(0) == 0
        gv = g_ref[...]
        r, xh = _rms_stats(x_ref[...])
        e = xh * gv - t_ref[...]
        tile_loss = 0.5 * jnp.sum(jnp.mean(e * e, axis=-1, keepdims=True), axis=0, keepdims=True)
        dx, dg = _rms_bwd(e * (1.0 / D_MODEL), xh, r, gv)
        dx_ref[...] = dx
        dxb_ref[...] = dx.astype(bf16)
        _acc_rows(dg_ref, first, dg)
        _acc_rows(loss_ref, first, jnp.broadcast_to(tile_loss, (1, LANE)))

    row = pl.BlockSpec((tm, D_MODEL), lambda i: (i, 0))
    return pl.pallas_call(
        body, name="final_loss", grid=(s // tm,), in_specs=[row, row, pl.BlockSpec((1, D_MODEL), lambda i: (0, 0))],
        out_specs=[pl.BlockSpec((1, LANE), lambda i: (0, 0)), row, row, pl.BlockSpec((1, D_MODEL), lambda i: (0, 0))],
        out_shape=[SDS((1, LANE), f32), SDS((s, D_MODEL), f32), SDS((s, D_MODEL), bf16), SDS((1, D_MODEL), f32)],
        compiler_params=_cp("arbitrary"),
    )(x, target, g)


def _ffn_bwd(dx2, dx2b, x1, g, ra, w1, w2, tm=512, tf=2048):
    s = x1.shape[0]
    nf = D_FF // tf

    def body(dx_ref, dxb_ref, x_ref, g_ref, ra_ref, w1_ref, w2_ref, da_ref, dx1_ref, dx1b_ref, dg_ref, acc):
        i, f = pl.program_id(0), pl.program_id(1)

        @pl.when(f == 0)
        def _():
            acc[...] = jnp.zeros_like(acc)

        da = (_dot_nt(dxb_ref[...], w2_ref[...]) * (2.0 * ra_ref[...].astype(f32))).astype(bf16)
        da_ref[...] = da
        acc[...] += _dot_nt(da, w1_ref[...])

        @pl.when(f == nf - 1)
        def _():
            r, xh = _rms_stats(x_ref[...])
            dx, dg = _rms_bwd(acc[...], xh, r, g_ref[...])
            dx = dx + dx_ref[...]
            dx1_ref[...] = dx
            dx1b_ref[...] = dx.astype(bf16)
            _acc_rows(dg_ref, i == 0, dg)

    row = pl.BlockSpec((tm, D_MODEL), lambda i, f: (i, 0))
    col = pl.BlockSpec((tm, tf), lambda i, f: (i, f))
    return pl.pallas_call(
        body, name="ffn_bwd", grid=(s // tm, nf),
        in_specs=[row, row, row, pl.BlockSpec((1, D_MODEL), lambda i, f: (0, 0)), col,
                  pl.BlockSpec((D_MODEL, tf), lambda i, f: (0, f)), pl.BlockSpec((tf, D_MODEL), lambda i, f: (f, 0))],
        out_specs=[col, row, row, pl.BlockSpec((1, D_MODEL), lambda i, f: (0, 0))],
        out_shape=[SDS((s, D_FF), bf16), SDS((s, D_MODEL), f32), SDS((s, D_MODEL), bf16), SDS((1, D_MODEL), f32)],
        scratch_shapes=[pltpu.VMEM((tm, D_MODEL), f32)], compiler_params=_cp("arbitrary", "arbitrary"),
    )(dx2, dx2b, x1, g, ra, w1, w2)


def _dw_in(dpm, dpg, h, tm=768):
    s = h.shape[0]
    km, kg = dpm.shape[1] // tm, dpg.shape[1] // tm

    def body(am_ref, ag_ref, h_ref, o_ref):
        i = pl.program_id(0)

        @pl.when(i < km)
        def _():
            o_ref[...] = _dot_tn(am_ref[...], h_ref[...]).astype(bf16)

        @pl.when(i >= km)
        def _():
            o_ref[...] = _dot_tn(ag_ref[...], h_ref[...]).astype(bf16)

    return pl.pallas_call(
        body, name="dw_in", grid=(km + kg,),
        in_specs=[pl.BlockSpec((s, tm), lambda i: (0, jnp.minimum(i, km - 1))), pl.BlockSpec((s, tm), lambda i: (0, jnp.maximum(i - km, 0))),
                  pl.BlockSpec((s, D_MODEL), lambda i: (0, 0))],
        out_specs=pl.BlockSpec((tm, D_MODEL), lambda i: (i, 0)), out_shape=SDS((km * tm + kg * tm, D_MODEL), bf16),
        compiler_params=_cp("arbitrary"),
    )(dpm, dpg, h)


def _dw_ff2_o(ra, dx2b, merged, dx1b, tm=512):
    s = ra.shape[0]
    k2, ko = D_FF // tm, D_MODEL // tm

    def body(ra_ref, dx2_ref, m_ref, dx1_ref, o2_ref, oo_ref):
        i = pl.program_id(0)

        @pl.when(i < k2)
        def _():
            av = ra_ref[...].astype(f32)
            o2_ref[...] = _dot_tn((av * av).astype(bf16), dx2_ref[...]).astype(bf16)

        @pl.when(i >= k2)
        def _():
            oo_ref[...] = _dot_tn(m_ref[...], dx1_ref[...]).astype(bf16)

    first, then = (lambda i: jnp.minimum(i, k2 - 1)), (lambda i: jnp.maximum(i - k2, 0))
    whole = pl.BlockSpec((s, D_MODEL), lambda i: (0, 0))
    return pl.pallas_call(
        body, name="dw_ff2_o", grid=(k2 + ko,),
        in_specs=[pl.BlockSpec((s, tm), lambda i: (0, first(i))), whole, pl.BlockSpec((s, tm), lambda i: (0, then(i))), whole],
        out_specs=[pl.BlockSpec((tm, D_MODEL), lambda i: (first(i), 0)), pl.BlockSpec((tm, D_MODEL), lambda i: (then(i), 0))],
        out_shape=[SDS((D_FF, D_MODEL), bf16), SDS((D_MODEL, D_MODEL), bf16)], compiler_params=_cp("arbitrary"),
    )(ra, dx2b, merged, dx1b)


def _mm_tn_slabs(a, b, nb, m, nblk, rel, width, tm=512, name="mm_tn_slabs"):
    s = a.shape[0]
    n = b.shape[1] // nb
    ng, mi, nw = n // nblk, m // tm, len(rel)

    def body(a_ref, b_ref, o_ref):
        full = _dot_tn(a_ref[...], b_ref[...])
        for r, start in enumerate(rel):
            o_ref[r] = full[:, start:start + width].astype(bf16)

    return pl.pallas_call(
        body, name=name, grid=(nb, ng, mi),
        in_specs=[pl.BlockSpec((s, tm), lambda k, g, i: (0, k * mi + i)), pl.BlockSpec((s, nblk), lambda k, g, i: (0, k * ng + g))],
        out_specs=pl.BlockSpec((nw, None, tm, width), lambda k, g, i: (g, k, i, 0)), out_shape=SDS((ng * nw, nb, m, width), bf16),
        compiler_params=_cp("parallel", "parallel", "parallel"),
    )(a, b)


def _merge_bwd(dx1b, wo, y, pg, wb, after, tm=512):
    s = dx1b.shape[0]

    def body(dx_ref, wo_ref, y_ref, g_ref, wb_ref, after_ref, dy_ref, dg_ref, dz_ref):
        del after_ref
        dm = _dot_nt(dx_ref[...], wo_ref[...])
        for n in range(3):
            cols = slice(D_MODEL * n, D_MODEL * (n + 1))
            gate = _sigmoid(g_ref[:, cols].astype(f32))
            t = dm * gate
            dy = t.astype(bf16)
            dy_ref[:, cols] = dy
            dg_ref[:, cols] = (t * y_ref[:, cols].astype(f32) * (1.0 - gate)).astype(bf16)
            dz_ref[:, BRANCH * n:BRANCH * (n + 1)] = _dot_nt(dy, wb_ref[n]).astype(bf16)

    wide = pl.BlockSpec((tm, 3 * D_MODEL), lambda i: (i, 0))
    return pl.pallas_call(
        body, name="merge_bwd", grid=(s // tm,),
        in_specs=[pl.BlockSpec((tm, D_MODEL), lambda i: (i, 0)), pl.BlockSpec((D_MODEL, D_MODEL), lambda i: (0, 0)), wide, wide,
                  pl.BlockSpec((3, BRANCH, D_MODEL), lambda i: (0, 0, 0)), pl.BlockSpec(memory_space=pl.ANY)],
        out_specs=[wide, wide, pl.BlockSpec((tm, 3 * BRANCH), lambda i: (i, 0))],
        out_shape=[SDS((s, 3 * D_MODEL), bf16), SDS((s, 3 * D_MODEL), bf16), SDS((s, 3 * BRANCH), bf16)],
        compiler_params=_cp("parallel"),
    )(dx1b, wo, y, pg, wb, after)


def _mixer_bwd(p, dz, opre, states, lb, gout, wconv, lng, lnb, wsg, bsg_t):
    s = p.shape[0]
    tt = MIX_TILE
    nt = s // tt
    nch = tt // HGRN_CHUNK
    rev = lambda i: nt - 1 - i

    def body(q_ref, fp_ref, iv_ref, go_ref, bg_ref, cg_ref, xc_ref, u_ref, v_ref, cgp_ref, xcp_ref, dz_ref, opre_ref, st_ref,
             lb_ref, gout_ref, wconv_ref, lng_ref, lnb_ref, wsg_ref, bsg_ref,
             dp_ref, vec_ref, dwsg_ref, dbsg_ref, dst_scr, zbuf, dybuf, dbsg_acc):
        i = pl.program_id(0)

        @pl.when(i == 0)
        def _():
            dst_scr[...] = jnp.zeros_like(dst_scr)
            dybuf[tt:tt + 8, :] = jnp.zeros((8, BRANCH), f32)
            vec_ref[...] = jnp.zeros_like(vec_ref)
            dwsg_ref[...] = jnp.zeros_like(dwsg_ref)
            dbsg_acc[...] = jnp.zeros_like(dbsg_acc)

        lbv = lb_ref[...]
        gout_v = gout_ref[...]
        causal = _tri(HGRN_CHUNK)
        tri = causal.astype(f32)
        tri_up = _tri(HGRN_CHUNK, upper=True).astype(f32)
        last_row = lax.broadcasted_iota(jnp.int32, (HGRN_CHUNK, 1), 0) == HGRN_CHUNK - 1
        lb_live = (lbv > LB_FLOOR).astype(f32)
        dlb = jnp.zeros((1, BRANCH), f32)
        dgout = jnp.zeros((1, BRANCH), f32)
        for c in reversed(range(nch)):
            rows = slice(HGRN_CHUNK * c, HGRN_CHUNK * (c + 1))
            q_c, fp = q_ref[rows, :], fp_ref[rows, :]
            sq_c = _sigmoid(q_c)
            sfp_c = _sigmoid(fp)
            logf, snf_c, kk = _hgrn_gates(fp, lbv)
            invf_c = jnp.exp(-logf)
            doa = dz_ref[rows, 0:BRANCH].astype(f32)
            o = opre_ref[rows, :]
            sgo = _sigmoid(go_ref[rows, :])
            d_o, dgo, dg_c = [], [], []
            for h in range(HEADS):
                sl = slice(HEAD_DIM * h, HEAD_DIM * (h + 1))
                r, oh = _rms_stats(o[:, sl])
                dgo.append(doa[:, sl] * oh * gout_v[:, sl] * sgo[:, sl] * (1.0 - sgo[:, sl]))
                dx, dg = _rms_bwd(doa[:, sl] * sgo[:, sl], oh, r, gout_v[:, sl])
                d_o.append(dx)
                dg_c.append(dg)
            dp_ref[rows, 3 * BRANCH:4 * BRANCH] = jnp.concatenate(dgo, axis=1).astype(bf16)
            dgout = dgout + jnp.concatenate(dg_c, axis=1)
            dob = jnp.concatenate(d_o, axis=1).astype(bf16)
            b = _dot_exact(tri, logf)
            bl = jnp.sum(jnp.where(last_row, b, 0.0), axis=0, keepdims=True)
            eb, enb, edl, ebl = jnp.exp(b), jnp.exp(-b), jnp.exp(bl - b), jnp.exp(bl)
            qbf, kbf, kdf = q_c * sq_c * eb, kk * enb, kk * edl
            qb, kb, kd = qbf.astype(bf16), kbf.astype(bf16), kdf.astype(bf16)
            vc = iv_ref[rows, :].astype(bf16)
            dv, dqb, dkb, dkd, debl = [], [], [], [], []
            for h in range(HEADS):
                sl = slice(HEAD_DIM * h, HEAD_DIM * (h + 1))
                st = st_ref[c, h]
                dst = dst_scr[h]
                stb, dstb = st.astype(bf16), dst.astype(bf16)
                a = jnp.where(causal, _dot_nt(qb[:, sl], kb[:, sl]), 0.0).astype(bf16)
                da = jnp.where(causal, _dot_nt(dob[:, sl], vc[:, sl]), 0.0).astype(bf16)
                dv.append(_dot_tn(a, dob[:, sl]) + _dot_nt(kd[:, sl], dstb))
                dqb.append(_dot(dob[:, sl], stb) + _dot(da, kb[:, sl]))
                dkb.append(_dot_tn(da, qb[:, sl]))
                dkd.append(_dot(vc[:, sl], dstb))
                debl.append(jnp.sum(st * dst, axis=0, keepdims=True))
                dst_scr[h] = _dot_tn(dob[:, sl], qb[:, sl]) + dst * ebl[:, sl]
            dv, dqb, dkb, dkd = (jnp.concatenate(t, axis=1) for t in (dv, dqb, dkb, dkd))
            debl = jnp.concatenate(debl, axis=1)
            t_kd = dkd * kdf
            dbl = ebl * debl + jnp.sum(t_kd, axis=0, keepdims=True)
            db = dqb * qbf - dkb * kbf - t_kd + jnp.where(last_row, dbl, 0.0)
            dkk = dkb * enb + dkd * edl
            dlc = _dot_exact(tri_up, db)
            slope = (1.0 - lbv) * sfp_c * snf_c
            dp_ref[rows, 0:BRANCH] = (dqb * eb * sq_c * (1.0 + q_c * (1.0 - sq_c))).astype(bf16)
            dp_ref[rows, BRANCH:2 * BRANCH] = (slope * (dlc * invf_c - dkk)).astype(bf16)
            dp_ref[rows, 2 * BRANCH:3 * BRANCH] = dv.astype(bf16)
            dlb = dlb + jnp.sum(dlc * (lb_live - sfp_c) * invf_c - dkk * snf_c, axis=0, keepdims=True)
        vec_ref[0:1, :] += dlb
        vec_ref[1:2, :] += dgout

        dob_ = dz_ref[:, BRANCH:2 * BRANCH].astype(f32)
        bg, cg, xc = bg_ref[...], cg_ref[...], xc_ref[...]
        zc = cg * xc
        zbuf[0:8, :] = jnp.where(i < nt - 1, cgp_ref[...] * xcp_ref[...], 0.0)
        zbuf[8:8 + tt, :] = zc
        w0, w1, w2 = wconv_ref[0:1, :], wconv_ref[1:2, :], wconv_ref[2:3, :]
        y = w0 * zbuf[pl.ds(6, tt), :] + w1 * zbuf[pl.ds(7, tt), :] + w2 * zc
        dy = dob_ * bg
        dybuf[0:tt, :] = dy
        dy1, dy2 = dybuf[pl.ds(1, tt), :], dybuf[pl.ds(2, tt), :]
        dzc = w2 * dy + w1 * dy1 + w0 * dy2
        dp_ref[:, 4 * BRANCH:5 * BRANCH] = (dob_ * y).astype(bf16)
        dp_ref[:, 5 * BRANCH:6 * BRANCH] = (dzc * xc).astype(bf16)
        dp_ref[:, 6 * BRANCH:7 * BRANCH] = (dzc * cg).astype(bf16)
        vec_ref[4:5, :] += jnp.sum(zc * dy2, axis=0, keepdims=True)
        vec_ref[5:6, :] += jnp.sum(zc * dy1, axis=0, keepdims=True)
        vec_ref[6:7, :] += jnp.sum(zc * dy, axis=0, keepdims=True)
        dybuf[tt:tt + 8, :] = dybuf[0:8, :]

        lng_v, lnb_v = lng_ref[...], lnb_ref[...]
        low = _tri(SG_CHUNK)
        wms = [jnp.where(low, wsg_ref[g], 0.0).astype(bf16) for g in range(SG_GROUPS)]
        dlng = jnp.zeros((1, BRANCH), f32)
        dlnb = jnp.zeros((1, BRANCH), f32)
        for cc in range(tt // SG_CHUNK):
            rows = slice(SG_CHUNK * cc, SG_CHUNK * (cc + 1))
            doc = dz_ref[rows, 2 * BRANCH:3 * BRANCH].astype(f32)
            u_raw, v_raw = u_ref[rows, :], v_ref[rows, :]
            ug = _gelu(u_raw)
            vg = _gelu(v_raw)
            vcen = vg - jnp.mean(vg, axis=-1, keepdims=True)
            rstd = lax.rsqrt(jnp.mean(vcen * vcen, axis=-1, keepdims=True) + LN_EPS)
            vhat = vcen * rstd
            vn = (vhat * lng_v + lnb_v).astype(bf16)
            dvn = []
            for g in range(SG_GROUPS):
                sl = slice(LANE * g, LANE * (g + 1))
                sv = _dot(wms[g], vn[:, sl]) + bsg_ref[:, g:g + 1]
                dp_ref[rows, 7 * BRANCH + LANE * g:7 * BRANCH + LANE * (g + 1)] = (doc[:, sl] * sv * _gelu_grad(u_raw[:, sl])).astype(bf16)
                dsv = doc[:, sl] * ug[:, sl]
                dsvb = dsv.astype(bf16)
                dbsg_acc[:, sl] += dsv
                dwsg_ref[g] += jnp.where(low, _dot_nt(dsvb, vn[:, sl]), 0.0)
                dvn.append(_dot_tn(wms[g], dsvb))
            dvn = jnp.concatenate(dvn, axis=1)
            dlng = dlng + jnp.sum(dvn * vhat, axis=0, keepdims=True)
            dlnb = dlnb + jnp.sum(dvn, axis=0, keepdims=True)
            dvh = dvn * lng_v
            dvg = rstd * (dvh - jnp.mean(dvh, axis=-1, keepdims=True) - vhat * jnp.mean(dvh * vhat, axis=-1, keepdims=True))
            dp_ref[rows, 8 * BRANCH:9 * BRANCH] = (dvg * _gelu_grad(v_raw)).astype(bf16)
        vec_ref[2:3, :] += dlng
        vec_ref[3:4, :] += dlnb

        @pl.when(i == nt - 1)
        def _():
            for g in range(SG_GROUPS):
                dbsg_ref[:, g:g + 1] = jnp.sum(dbsg_acc[:, LANE * g:LANE * (g + 1)], axis=1, keepdims=True)

    full = lambda shape: pl.BlockSpec(shape, lambda i: (0,) * len(shape))
    tail = lambda c: pl.BlockSpec((8, BRANCH), lambda i: (jnp.maximum(rev(i) * (tt // 8) - 1, 0), c))
    return pl.pallas_call(
        body, name="mixer_bwd", grid=(nt,),
        in_specs=_p_specs(tt, range(9), rev) + [tail(5), tail(6), pl.BlockSpec((tt, 3 * BRANCH), lambda i: (rev(i), 0)),
                                                pl.BlockSpec((tt, BRANCH), lambda i: (rev(i), 0)),
                                                pl.BlockSpec((nch, HEADS, HEAD_DIM, HEAD_DIM), lambda i: (rev(i), 0, 0, 0)),
                                                full((1, BRANCH)), full((1, BRANCH)), full((3, BRANCH)), full((1, BRANCH)), full((1, BRANCH)),
                                                full((SG_GROUPS, SG_CHUNK, SG_CHUNK)), full((SG_CHUNK, SG_GROUPS))],
        out_specs=[pl.BlockSpec((tt, 9 * BRANCH), lambda i: (rev(i), 0)), full((8, BRANCH)), full((SG_GROUPS, SG_CHUNK, SG_CHUNK)),
                   full((SG_CHUNK, SG_GROUPS))],
        out_shape=[SDS((s, 9 * BRANCH), bf16), SDS((8, BRANCH), f32), SDS((SG_GROUPS, SG_CHUNK, SG_CHUNK), f32), SDS((SG_CHUNK, SG_GROUPS), f32)],
        scratch_shapes=[pltpu.VMEM((HEADS, HEAD_DIM, HEAD_DIM), f32), pltpu.VMEM((tt + 8, BRANCH), f32), pltpu.VMEM((tt + 8, BRANCH), f32),
                        pltpu.VMEM((SG_CHUNK, BRANCH), f32)],
        compiler_params=_cp("arbitrary"),
    )(*([p] * 11), dz, opre, states, lb, gout, wconv, lng, lnb, wsg, bsg_t)


def _dh_bwd(dpm, dpg, w_t, x, dx1, g, after, tm=1024, tk=1536):
    s = x.shape[0]
    km = dpm.shape[1] // tk
    nk = km + dpg.shape[1] // tk

    def body(dpm_ref, dpg_ref, w_ref, x_ref, dx1_ref, g_ref, after_ref, dx_ref, dxb_ref, dg_ref, acc):
        del after_ref
        i, k = pl.program_id(0), pl.program_id(1)

        @pl.when(k == 0)
        def _():
            acc[...] = jnp.zeros_like(acc)

        @pl.when(k < km)
        def _():
            acc[...] += _dot(dpm_ref[...], w_ref[...])

        @pl.when(k >= km)
        def _():
            acc[...] += _dot(dpg_ref[...], w_ref[...])

        @pl.when(k == nk - 1)
        def _():
            r, xh = _rms_stats(x_ref[...])
            dx, dg = _rms_bwd(acc[...], xh, r, g_ref[...])
            dx = dx + dx1_ref[...]
            dx_ref[...] = dx
            dxb_ref[...] = dx.astype(bf16)
            _acc_rows(dg_ref, i == 0, dg)

    row = pl.BlockSpec((tm, D_MODEL), lambda i, k: (i, 0))
    vec = pl.BlockSpec((1, D_MODEL), lambda i, k: (0, 0))
    return pl.pallas_call(
        body, name="dh_bwd", grid=(s // tm, nk),
        in_specs=[pl.BlockSpec((tm, tk), lambda i, k: (i, jnp.minimum(k, km - 1))),
                  pl.BlockSpec((tm, tk), lambda i, k: (i, jnp.maximum(k - km, 0))),
                  pl.BlockSpec((tk, D_MODEL), lambda i, k: (k, 0)), row, row, vec, pl.BlockSpec(memory_space=pl.ANY)],
        out_specs=[row, row, vec], out_shape=[SDS((s, D_MODEL), f32), SDS((s, D_MODEL), bf16), SDS((1, D_MODEL), f32)],
        scratch_shapes=[pltpu.VMEM((tm, D_MODEL), f32)], compiler_params=_cp("arbitrary", "arbitrary"),
    )(dpm, dpg, w_t, x, dx1, g, after)


def _layer_fwd(x, weight, sm):
    p, pg, h = _rms_mm(x, sm["g_mix"], weight("w_in", x))
    z, opre, states = _mixer_fwd(p, sm["lb"], sm["g_out"], sm["w_conv"], sm["ln_g"], sm["ln_b"], sm["w_sg"], sm["b_sg_t"])
    y, merged, x1 = _branch_gate(z, weight("w_branch", z), pg, x, weight("w_o", z))
    x2, h2, ra = _ffn(x1, sm["g_ffn"], weight("w_ff1", x1), weight("w_ff2", x1))
    saved = dict(x=x, p=p, pg=pg, h=h, z=z, opre=opre, states=states, y=y, merged=merged, x1=x1, h2=h2, ra=ra)
    return x2, saved


def _layer_bwd(dx2, dx2b, sv, w, sm, between, before_end):
    nchip = N_DEV // 2
    by_chip = lambda g: g.reshape((nchip, 2) + g.shape[1:])
    da, dx1, dx1b, dg_ffn = _ffn_bwd(dx2, dx2b, sv["x1"], sm["g_ffn"], sv["ra"], w["w_ff1"], w["w_ff2"])
    g_ff2, g_o = _dw_ff2_o(sv["ra"], dx2b, sv["merged"], dx1b)
    g_ff2 = by_chip(g_ff2.reshape(N_DEV, D_FF // N_DEV, D_MODEL))
    g_o = by_chip(g_o.reshape(N_DEV, D_MODEL // N_DEV, D_MODEL))
    g_ff1 = by_chip(_mm_tn_slabs(sv["h2"], da, 1, D_MODEL, D_FF // 2, [i * (D_FF // N_DEV) for i in range(nchip)], D_FF // N_DEV,
                                 name="dw_ff1")[:, 0])
    dy, dpg, dz = _merge_bwd(dx1b, w["w_o"], sv["y"], sv["pg"], w["w_branch"], between(dx1))
    g_branch = by_chip(_mm_tn_slabs(sv["z"], dy, 3, BRANCH, D_MODEL, [i * (D_MODEL // N_DEV) for i in range(N_DEV)], D_MODEL // N_DEV,
                                    name="dw_branch"))
    dpm, vecs, dwsg, dbsg_t = _mixer_bwd(sv["p"], dz, sv["opre"], sv["states"], sm["lb"], sm["g_out"], sm["w_conv"],
                                         sm["ln_g"], sm["ln_b"], sm["w_sg"], sm["b_sg_t"])
    g_in = by_chip(_dw_in(dpm, dpg, sv["h"]).reshape(N_DEV, SHARD_IN, D_MODEL))
    big = dict(w_in=g_in, w_branch=g_branch, w_o=g_o, w_ff1=g_ff1, w_ff2=g_ff2)
    dx, dxb, dg_mix = _dh_bwd(dpm, dpg, w["w_in"], sv["x"], dx1, sm["g_mix"], before_end(big))
    small = dict(g_mix=dg_mix, g_ffn=dg_ffn, vecs=vecs, w_sg=dwsg, b_sg_t=dbsg_t, dx1=dx1)
    return dx, dxb, big, small


BIG = ("w_in", "w_branch", "w_o", "w_ff1", "w_ff2")
ANY = pl.BlockSpec(memory_space=pl.ANY)


def _place():
    return lax.axis_index("x"), lax.axis_index("y"), lax.axis_index("c")


def _al(v, m):
    return pl.multiple_of(v * m, m)


def _shard_of(refs, dev, which=range(len(BIG))):
    out = []
    for ref, t in zip(refs, which):
        by_cols = BIG[t] in ("w_branch", "w_ff1")
        n = ref.shape[-1 if by_cols else 0] // N_DEV
        part = pl.ds(_al(dev, n), n)
        out.append(ref.at[(slice(None),) * (len(ref.shape) - 1) + (part,)] if by_cols else ref.at[part])
    return out


def _gather_out_shapes(shards):
    s_in, s_b, s_o, s_1, s_2 = (shards[n] for n in BIG)
    return [SDS((s_in.shape[1] * N_DEV, s_in.shape[2]), bf16), SDS(s_b.shape[1:3] + (s_b.shape[3] * N_DEV,), bf16),
            SDS((s_o.shape[1] * N_DEV, s_o.shape[2]), bf16), SDS((s_1.shape[1], s_1.shape[2] * N_DEV), bf16),
            SDS((s_2.shape[1] * N_DEV, s_2.shape[2]), bf16)]


def _seq_all_gather_layer(layer, which, n_early, shard_refs, out_shapes, tag=""):
    nt = len(which)
    outs = [jax.empty_ref(sh, memory_space=pltpu.MemorySpace.HBM) for sh in out_shapes]
    early, late = tuple(range(n_early)), tuple(range(n_early, nt))

    @pl.kernel(mesh=plsc.ScalarSubcoreMesh(axis_name="seq", num_cores=1), name=f"seq_all_gather_l{layer}{tag}",
               scratch_types=(pltpu.SemaphoreType.DMA((9,)), pltpu.SemaphoreType.DMA((9,))),
               compiler_params=pltpu.CompilerParams(collective_id=1))
    def launch(send_sems, recv_sems):
        x, y, c = _place()
        me, sibling = (x, y, c), (x, y, 1 - c)
        first, second, diag = _ici_route(x, y, c)
        _handshake([sibling, first, second])
        mine = [r.at[layer] for r in shard_refs]

        def copies(k, blk, to, src=None, part=range(nt)):
            dst = _shard_of(outs, 4 * blk[0] + 2 * blk[1] + blk[2], which)
            src = dst if src is None else src
            return [pltpu.make_async_remote_copy(src_ref=src[t], dst_ref=dst[t], send_sem=send_sems.at[k], recv_sem=recv_sems.at[k],
                                                 device_id=to, device_id_type=MESH) for t in part]

        def start(cps):
            for cp in cps:
                cp.start()
            return cps

        def landed(cps):
            for cp in cps:
                cp.wait_recv()

        sent = start(copies(0, me, sibling, src=mine) + copies(1, me, first, src=mine, part=early)
                     + copies(2, me, first, src=mine, part=late) + copies(3, me, second, src=mine))
        landed(copies(1, first, me, part=early))
        sent += start(copies(4, first, second, part=early) + copies(6, first, sibling, part=early))
        landed(copies(2, first, me, part=late))
        sent += start(copies(5, first, second, part=late) + copies(6, first, sibling, part=late))
        landed(copies(3, second, me))
        sent += start(copies(7, second, sibling))
        landed(copies(4, diag, me, part=early) + copies(5, diag, me, part=late))
        sent += start(copies(8, diag, sibling))
        other = lambda p: (p[0], p[1], 1 - c)
        landed(copies(0, sibling, me) + copies(6, other(second), me) + copies(7, other(first), me) + copies(8, other(diag), me))
        for cp in sent:
            cp.wait_send()

    launch()
    return [o[...] for o in outs]


def _ici_route(x, y, c):
    return (x ^ (1 - c), y ^ c, c), (x ^ c, y ^ (1 - c), c), (1 - x, 1 - y, c)


def _place_own(where, which, shards, gathered, after):
    nt = len(which)

    def body(where_ref, *refs):
        del where_ref
        for src, dst in zip(refs[:nt], refs[2 * nt + 1:]):
            dst[...] = src[...]

    in_specs, out_specs = [], []
    for t, sh in zip(which, shards):
        blk = sh.shape[1:]
        in_specs.append(pl.BlockSpec((None,) + blk, functools.partial(lambda nd, i, wh: (wh[0],) + (0,) * nd, len(blk))))
        by_cols = BIG[t] in ("w_branch", "w_ff1")
        out_specs.append(pl.BlockSpec(blk, functools.partial(
            lambda nd, cols, i, wh: (0,) * (nd - 1) + (wh[1],) if cols else (wh[1],) + (0,) * (nd - 1), len(blk), by_cols)))
    return pl.pallas_call(
        body, name="place_own", out_shape=[SDS(g.shape, g.dtype) for g in gathered],
        input_output_aliases={1 + nt + i: i for i in range(nt)}, compiler_params=_cp("arbitrary"),
        grid_spec=pltpu.PrefetchScalarGridSpec(num_scalar_prefetch=1, grid=(1,), in_specs=in_specs + [ANY] * (nt + 1), out_specs=out_specs),
    )(where, *shards, *gathered, after)


def _handshake(peers):
    barrier = pltpu.get_barrier_semaphore()
    for p in peers:
        pl.semaphore_signal(barrier, inc=1, device_id=p, device_id_type=MESH)
    pl.semaphore_wait(barrier, len(peers))


def _seq_exchange_on_chip(grads):
    nt, nchip = len(BIG), N_DEV // 2
    g_refs = [jax.new_ref(g, memory_space=pltpu.MemorySpace.HBM) for g in grads]
    outs = [jax.empty_ref(SDS((nchip,) + g.shape[2:], bf16), memory_space=pltpu.MemorySpace.HBM) for g in grads]

    @pl.kernel(mesh=plsc.ScalarSubcoreMesh(axis_name="seq", num_cores=1), name="seq_rs_on_chip",
               scratch_types=(pltpu.SemaphoreType.DMA((nchip,)), pltpu.SemaphoreType.DMA((nchip,))),
               compiler_params=pltpu.CompilerParams(collective_id=2))
    def launch(send_sems, recv_sems):
        x, y, c = _place()
        sibling = (x, y, 1 - c)
        _handshake([sibling])
        remote = [pltpu.make_async_remote_copy(src_ref=g_refs[t].at[j, 1 - c], dst_ref=outs[t].at[j], send_sem=send_sems.at[j],
                                               recv_sem=recv_sems.at[j], device_id=sibling, device_id_type=MESH)
                  for j in range(nchip) for t in range(nt)]
        for cp in remote:
            cp.start()
        for cp in remote:
            cp.wait_recv()
        for cp in remote:
            cp.wait_send()

    launch()
    return [o[...] for o in outs], [g[...] for g in g_refs]


def _seq_exchange_between_chips(sums):
    nt = len(BIG)
    s_refs = [jax.new_ref(a, memory_space=pltpu.MemorySpace.HBM) for a in sums]
    outs = [jax.empty_ref(SDS((3,) + a.shape[1:], bf16), memory_space=pltpu.MemorySpace.HBM) for a in sums]
    transit = [jax.empty_ref(SDS(a.shape[1:], bf16), memory_space=pltpu.MemorySpace.HBM) for a in sums]

    early, late = (0,), tuple(range(1, nt))

    @pl.kernel(mesh=plsc.ScalarSubcoreMesh(axis_name="seq", num_cores=1), name="seq_rs_between_chips",
               scratch_types=(pltpu.SemaphoreType.DMA((6,)), pltpu.SemaphoreType.DMA((6,))),
               compiler_params=pltpu.CompilerParams(collective_id=3))
    def launch(send_sems, recv_sems):
        x, y, c = _place()
        first, second, diag = _ici_route(x, y, c)
        _handshake([first, second])

        def copies(k, src, dst, to, part=range(nt)):
            return [pltpu.make_async_remote_copy(src_ref=src(t), dst_ref=dst(t), send_sem=send_sems.at[k], recv_sem=recv_sems.at[k],
                                                 device_id=to, device_id_type=MESH) for t in part]

        chip_of = lambda p: 2 * p[0] + p[1]
        for_diag = lambda t: s_refs[t].at[chip_of(diag)]
        through = lambda t: transit[t]
        last = lambda t: outs[t].at[2]
        direct = (copies(0, lambda t: s_refs[t].at[chip_of(first)], lambda t: outs[t].at[0], first)
                  + copies(1, lambda t: s_refs[t].at[chip_of(second)], lambda t: outs[t].at[1], second))
        via = [copies(2, for_diag, through, first, early), copies(3, for_diag, through, first, late)]
        passed = [copies(4, through, last, second, early), copies(5, through, last, second, late)]
        for cp in via[0] + direct + via[1]:
            cp.start()
        for arrived, onward in zip(via, passed):
            for cp in arrived:
                cp.wait_recv()
            for cp in onward:
                cp.start()
        sent = direct + via[0] + via[1] + passed[0] + passed[1]
        for cp in direct + passed[0] + passed[1]:
            cp.wait_recv()
        for cp in sent:
            cp.wait_send()

    launch()
    return [o[...] for o in outs], [a[...] for a in s_refs]


def _chip_sums(place, mine, other, after, steps=2):
    nt, nchip = len(mine), mine[0].shape[0]
    m4 = [a.reshape(nchip, 2, -1, a.shape[-1]) for a in mine]
    o3 = [a.reshape(nchip, -1, a.shape[-1]) for a in other]

    def body(p_ref, *refs):
        del p_ref
        for a_ref, b_ref, o_ref in zip(refs[:nt], refs[nt:2 * nt], refs[2 * nt + 1:]):
            o_ref[...] = (a_ref[...].astype(f32) + b_ref[...].astype(f32)).astype(bf16)

    tiles = [(a.shape[1] // steps, a.shape[2]) for a in o3]
    blks = [pl.BlockSpec((None,) + t, lambda j, i, p_ref: (p_ref[1 + j], i, 0)) for t in tiles]
    outs = pl.pallas_call(
        body, name="chip_sums", out_shape=[SDS(a.shape, bf16) for a in o3], compiler_params=_cp("parallel", "parallel"),
        grid_spec=pltpu.PrefetchScalarGridSpec(
            num_scalar_prefetch=1, grid=(nchip - 1, steps),
            in_specs=[pl.BlockSpec((None, None) + t, lambda j, i, p_ref: (p_ref[1 + j], p_ref[0], i, 0)) for t in tiles] + blks + [ANY],
            out_specs=blks),
    )(place, *m4, *o3, after)
    return [o.reshape(a.shape) for o, a in zip(outs, other)]


def _all_reduce_rows(pack):
    rows = pack.shape[0]
    blk = rows // N_DEV

    def body(in_ref, out_ref, land, send1, recv1, send2, recv2):
        x, y, c = _place()
        me = 4 * x + 2 * y + c
        others = [(px, py, pc) for px in range(2) for py in range(2) for pc in range(2)]

        def is_me(p):
            return jnp.logical_and(jnp.logical_and(p[0] == x, p[1] == y), p[2] == c)

        land[me] = in_ref[pl.ds(_al(me, blk), blk), :]
        for d, p in enumerate(others):
            @pl.when(jnp.logical_not(is_me(p)))
            def _():
                pltpu.make_async_remote_copy(src_ref=in_ref.at[pl.ds(d * blk, blk), :], dst_ref=land.at[me], send_sem=send1.at[d],
                                             recv_sem=recv1.at[me], device_id=p, device_id_type=MESH).start()
        for d, p in enumerate(others):
            @pl.when(jnp.logical_not(is_me(p)))
            def _():
                cp = pltpu.make_async_remote_copy(src_ref=in_ref.at[pl.ds(d * blk, blk), :], dst_ref=land.at[d], send_sem=send1.at[d],
                                                  recv_sem=recv1.at[d], device_id=p, device_id_type=MESH)
                cp.wait_recv()
                cp.wait_send()
        total = land[0]
        for d in range(1, N_DEV):
            total = total + land[d]
        out_ref[pl.ds(_al(me, blk), blk), :] = total
        for d, p in enumerate(others):
            @pl.when(jnp.logical_not(is_me(p)))
            def _():
                mine = out_ref.at[pl.ds(_al(me, blk), blk), :]
                pltpu.make_async_remote_copy(src_ref=mine, dst_ref=mine, send_sem=send2.at[d], recv_sem=recv2.at[me],
                                             device_id=p, device_id_type=MESH).start()
        for d, p in enumerate(others):
            @pl.when(jnp.logical_not(is_me(p)))
            def _():
                theirs = out_ref.at[pl.ds(d * blk, blk), :]
                cp = pltpu.make_async_remote_copy(src_ref=theirs, dst_ref=theirs, send_sem=send2.at[d], recv_sem=recv2.at[d],
                                                  device_id=p, device_id_type=MESH)
                cp.wait_recv()
                cp.wait_send()

    vm = pl.BlockSpec(memory_space=pltpu.VMEM)
    return pl.pallas_call(
        body, name="all_reduce_rows", in_specs=[vm], out_specs=vm, out_shape=SDS((rows, LANE), f32),
        scratch_shapes=[pltpu.VMEM((N_DEV, blk, LANE), f32)] + [pltpu.SemaphoreType.DMA((N_DEV,))] * 4,
        compiler_params=pltpu.CompilerParams(vmem_limit_bytes=VMEM_LIMIT),
    )(pack)


def _lower_bounds_fwd(lower, after):
    def body(l_ref, *rest):
        o_ref = rest[-1]
        sm = _layer_softmax(l_ref)
        run = jnp.zeros_like(sm[0])
        for l in range(DEPTH):
            o_ref[l:l + 1, :] = run
            if l + 1 < DEPTH:
                run = run + sm[l + 1]

    vm = pl.BlockSpec(memory_space=pltpu.VMEM)
    return pl.pallas_call(body, name="lower_bounds_fwd", in_specs=[vm] + [ANY] * len(after), out_specs=vm,
                          out_shape=SDS(lower.shape, f32))(lower, *after)


def _layer_softmax(l_ref):
    rows = [l_ref[l:l + 1, :] for l in range(DEPTH)]
    top = functools.reduce(jnp.maximum, rows)
    e = [jnp.exp(r - top) for r in rows]
    tot = functools.reduce(lambda a, b: a + b, e)
    return [v / tot for v in e]


def _lower_bounds_bwd(lower, dlbs):
    def body(l_ref, d_ref, o_ref):
        sm = _layer_softmax(l_ref)
        dsm = [None] * DEPTH
        run = jnp.zeros_like(sm[0])
        dsm[0] = run
        for l in reversed(range(1, DEPTH)):
            run = run + d_ref[l:l + 1, :]
            dsm[l] = run
        inner = functools.reduce(lambda a, b: a + b, [sm[l] * dsm[l] for l in range(DEPTH)])
        for l in range(DEPTH):
            o_ref[l:l + 1, :] = sm[l] * (dsm[l] - inner)

    return pl.pallas_call(body, name="lower_bounds_bwd", out_shape=SDS(lower.shape, f32))(lower, dlbs)


_ADAM_C1 = 1.0 - ADAM_B1 ** ADAM_STEP
_ADAM_C2 = 1.0 - ADAM_B2 ** ADAM_STEP


def _adamw(w, g, m, v):
    m = ADAM_B1 * m + (1.0 - ADAM_B1) * g
    v = ADAM_B2 * v + (1.0 - ADAM_B2) * (g * g)
    delta = -ADAM_LR * ((m / _ADAM_C1) / (jnp.sqrt(v / _ADAM_C2) + ADAM_EPS) + ADAM_WD * w)
    return delta, m, v


def _adam_big(where, names, w, m, v, mine, other, landed, outs, after, steps=4):
    nt = len(names)
    three = lambda a: a.reshape(a.shape[0], -1, a.shape[-1])
    w3, m3, v3 = ([three(d[n]) for n in names] for d in (w, m, v))
    outs3 = [three(a) for n in names for a in outs[n]]
    mine4 = [a.reshape(a.shape[0], 2, -1, a.shape[-1]) for a in mine]
    other3 = [three(a) for a in other]
    land3 = [three(a) for a in landed]

    def body(where_ref, *refs):
        del where_ref
        o_refs = refs[6 * nt + 4 * nt + 1:]
        for t in range(nt):
            w_ref, m_ref, v_ref, mine_ref, other_ref, land_ref = (refs[q * nt + t] for q in range(6))
            g = mine_ref[...].astype(f32) + other_ref[...].astype(f32)
            for k in range(3):
                g = g + land_ref[k].astype(f32)
            delta, nm, nv = _adamw(w_ref[...], g, m_ref[...], v_ref[...])
            for o_ref, val in zip(o_refs[4 * t:4 * t + 4], (g, delta, nm, nv)):
                o_ref[...] = val

    tiles = [(a.shape[1] // steps, a.shape[2]) for a in w3]
    own = [pl.BlockSpec((None,) + t, lambda i, wh: (wh[0], i, 0)) for t in tiles]
    res = pl.pallas_call(
        body, name="adam_big", out_shape=[SDS(a.shape, f32) for a in outs3],
        input_output_aliases={1 + 6 * nt + i: i for i in range(4 * nt)}, compiler_params=_cp("parallel"),
        grid_spec=pltpu.PrefetchScalarGridSpec(
            num_scalar_prefetch=1, grid=(steps,),
            in_specs=own * 3 + [pl.BlockSpec((None, None) + t, lambda i, wh: (wh[1], wh[2], i, 0)) for t in tiles]
            + [pl.BlockSpec((None,) + t, lambda i, wh: (wh[1], i, 0)) for t in tiles]
            + [pl.BlockSpec((3,) + t, lambda i, wh: (0, i, 0)) for t in tiles] + [ANY] * (4 * nt + 1),
            out_specs=[s for s in own for _ in range(4)]),
    )(where, *w3, *m3, *v3, *mine4, *other3, *land3, *outs3, after)
    return {n: [o.reshape(w[n].shape) for o in res[4 * t:4 * t + 4]] for t, n in enumerate(names)}


def _touch(a, after):
    a2 = a.reshape(-1, a.shape[-1])

    def body(a_ref, after_ref, o_ref):
        del after_ref
        o_ref[...] = a_ref[0:8, :].astype(f32)

    return pl.pallas_call(
        body, name="touch", grid=(1,), in_specs=[pl.BlockSpec((16, LANE), lambda i: (0, 0)), ANY],
        out_specs=pl.BlockSpec((8, LANE), lambda i: (0, 0)), out_shape=SDS((8, LANE), f32),
    )(a2, after)


def _adam_rows(w, g, m, v):
    def body(w_ref, g_ref, m_ref, v_ref, d_ref, nm_ref, nv_ref):
        delta, nm, nv = _adamw(w_ref[...], g_ref[...], m_ref[...], v_ref[...])
        d_ref[...] = delta
        nm_ref[...] = nm
        nv_ref[...] = nv

    return pl.pallas_call(body, name="adam_rows", out_shape=[SDS(w.shape, f32)] * 3)(w, g, m, v)


SMALL = ("g_mix", "lower_bounds", "g_hgrn_out", "w_conv", "sg_ln_g", "sg_ln_b", "w_sg", "b_sg", "g_ffn", "g_final")
WEIGHTS = ("w_in", "g_mix", "lower_bounds", "g_hgrn_out", "w_conv", "sg_ln_g", "sg_ln_b", "w_sg", "b_sg", "w_branch", "w_o", "g_ffn",
           "w_ff1", "w_ff2", "g_final")


def _pack_rows(arrays, multiple):
    flat = jnp.concatenate([a.reshape(-1) for a in arrays])
    rows = -(-flat.shape[0] // (LANE * multiple)) * multiple
    return jnp.pad(flat, (0, rows * LANE - flat.shape[0])).reshape(rows, LANE)


def _unpack_rows(pack, like):
    flat = pack.reshape(-1)
    out, at = [], 0
    for a in like:
        out.append(flat[at:at + a.size].reshape(a.shape))
        at += a.size
    return out


def kernel(x, w_in, g_mix, lower_bounds, g_hgrn_out, w_conv, sg_ln_g, sg_ln_b, w_sg, b_sg, w_branch, w_o, g_ffn, w_ff1, w_ff2, g_final, loss_target, m_w_in, m_g_mix, m_lower_bounds, m_g_hgrn_out, m_w_conv, m_sg_ln_g, m_sg_ln_b, m_w_sg, m_b_sg, m_w_branch, m_w_o, m_g_ffn, m_w_ff1, m_w_ff2, m_g_final, v_w_in, v_g_mix, v_lower_bounds, v_g_hgrn_out, v_w_conv, v_sg_ln_g, v_sg_ln_b, v_w_sg, v_b_sg, v_w_branch, v_w_o, v_g_ffn, v_w_ff1, v_w_ff2, v_g_final):
    weights = dict(w_in=w_in, g_mix=g_mix, lower_bounds=lower_bounds, g_hgrn_out=g_hgrn_out, w_conv=w_conv, sg_ln_g=sg_ln_g,
                   sg_ln_b=sg_ln_b, w_sg=w_sg, b_sg=b_sg, w_branch=w_branch, w_o=w_o, g_ffn=g_ffn, w_ff1=w_ff1, w_ff2=w_ff2, g_final=g_final)
    mom1 = dict(w_in=m_w_in, g_mix=m_g_mix, lower_bounds=m_lower_bounds, g_hgrn_out=m_g_hgrn_out, w_conv=m_w_conv, sg_ln_g=m_sg_ln_g,
                sg_ln_b=m_sg_ln_b, w_sg=m_w_sg, b_sg=m_b_sg, w_branch=m_w_branch, w_o=m_w_o, g_ffn=m_g_ffn, w_ff1=m_w_ff1, w_ff2=m_w_ff2,
                g_final=m_g_final)
    mom2 = dict(w_in=v_w_in, g_mix=v_g_mix, lower_bounds=v_lower_bounds, g_hgrn_out=v_g_hgrn_out, w_conv=v_w_conv, sg_ln_g=v_sg_ln_g,
                sg_ln_b=v_sg_ln_b, w_sg=v_w_sg, b_sg=v_b_sg, w_branch=v_w_branch, w_o=v_w_o, g_ffn=v_g_ffn, w_ff1=v_w_ff1, w_ff2=v_w_ff2,
                g_final=v_g_final)
    xi, yi, ci = _place()
    dev = 4 * xi + 2 * yi + ci
    conv_cols = w_conv.shape[-1]

    for d in (weights, mom1, mom2):
        d["w_in"] = jnp.swapaxes(d["w_in"], 1, 2)
    shards = {n: weights[n].astype(bf16) for n in BIG}

    conv_place = lax.dynamic_update_slice(jnp.zeros((DEPTH, 3, BRANCH), f32), w_conv, (0, 0, dev * conv_cols))
    (w_conv_full,) = _unpack_rows(_all_reduce_rows(_pack_rows([conv_place], 8 * N_DEV)), [conv_place])
    small_packs = [_pack_rows([d[n] for n in SMALL], 8) for d in (weights, mom1, mom2)]
    lbs = _lower_bounds_fwd(lower_bounds, small_packs)

    def small_of(l):
        return dict(g_mix=g_mix[l][None], lb=lbs[l][None], g_out=g_hgrn_out[l][None], w_conv=w_conv_full[l], ln_g=sg_ln_g[l][None],
                    ln_b=sg_ln_b[l][None], w_sg=w_sg[l], b_sg_t=b_sg[l].T, g_ffn=g_ffn[l][None])

    act = x[0]
    full, saved = [], []
    shard_refs = [jax.new_ref(shards[n], memory_space=pltpu.MemorySpace.HBM) for n in BIG]
    shapes = _gather_out_shapes(shards)
    groups = [g for l in range(DEPTH) for g in ((l, (0,), 1, "a"), (l, (1, 2, 3, 4), 2, "b"))]
    arrived = {}
    for l, which, n_early, tag in groups:
        got = _seq_all_gather_layer(l, which, n_early, [shard_refs[t] for t in which], [shapes[t] for t in which], tag)
        arrived.update({(l, BIG[t]): (which, got) for t in which})

    for l in range(DEPTH):
        full.append({})

        def weight(name, after, l=l):
            if name not in full[l]:
                which, got = arrived[(l, name)]
                where = jnp.stack([jnp.int32(l), dev.astype(jnp.int32)])
                full[l].update(zip([BIG[t] for t in which], _place_own(where, which, [shards[BIG[t]] for t in which], got, after)))
            return full[l][name]

        act, sv = _layer_fwd(act, weight, small_of(l))
        saved.append(sv)
    loss_row, dx, dxb, dg_final = _final(act, loss_target[0], g_final[None])

    chip, core = (2 * xi + yi).astype(jnp.int32), ci.astype(jnp.int32)
    place = jnp.stack([core] + [(chip + k) % (N_DEV // 2) for k in (1, 2, 3)])
    big_out = {n: [lax.empty(weights[n].shape, f32) for _ in range(4)] for n in BIG}
    small_grads = [None] * DEPTH

    def chip_sums(stage, after):
        l, received, mine = stage
        sums = _chip_sums(place, mine, received, after)
        placed.append(sums[BIG.index("w_o")])
        landed, _ = _seq_exchange_between_chips(sums)
        return l, mine, received, landed

    def adam_layer(stage, after):
        l, mine, received, landed = stage
        where = jnp.stack([jnp.int32(l), chip, core])
        big_out.update(_adam_big(where, BIG, weights, mom1, mom2, mine, received, landed, big_out, after))

    above = None
    placed = []
    for l in reversed(range(DEPTH)):
        summed = []

        def between(dx1):
            if above is None:
                return dx1
            summed.append(chip_sums(above, dx1))
            return placed[-1]

        def before_end(big):
            return _touch(summed[0][3][BIG.index("w_o")], big["w_in"]) if summed else big["w_in"]

        dx, dxb, big, small_grads[l] = _layer_bwd(dx, dxb, saved[l], full[l], small_of(l), between, before_end)
        if summed:
            adam_layer(summed[0], dx)
        above = (l, *_seq_exchange_on_chip([big[n] for n in BIG]))

    stack = lambda f: jnp.stack([f(small_grads[l]) for l in range(DEPTH)])
    d_lower = _lower_bounds_bwd(lower_bounds, stack(lambda s: s["vecs"][0]))
    local_small = dict(g_mix=stack(lambda s: s["g_mix"][0]), lower_bounds=d_lower, g_hgrn_out=stack(lambda s: s["vecs"][1]),
                       w_conv=stack(lambda s: s["vecs"][4:7]), sg_ln_g=stack(lambda s: s["vecs"][2]), sg_ln_b=stack(lambda s: s["vecs"][3]),
                       w_sg=stack(lambda s: s["w_sg"]), b_sg=stack(lambda s: s["b_sg_t"].T), g_ffn=stack(lambda s: s["g_ffn"][0]),
                       g_final=dg_final[0])
    order = [local_small[n] for n in SMALL] + [loss_row]
    *reduced, loss_sum = _unpack_rows(_all_reduce_rows(_pack_rows(order, 8 * N_DEV)), order)
    loss = loss_sum[0, 0]
    grads = dict(zip(SMALL, reduced))
    grads["w_conv"] = lax.dynamic_slice(grads["w_conv"], (0, 0, dev * conv_cols), (DEPTH, 3, conv_cols))

    deltas, new_m, new_v = {}, {}, {}
    like = [weights[n] for n in SMALL]
    small_out = _adam_rows(small_packs[0], _pack_rows([grads[n] for n in SMALL], 8), small_packs[1], small_packs[2])
    for out, pack in zip((deltas, new_m, new_v), small_out):
        out.update(zip(SMALL, _unpack_rows(pack, like)))
    adam_layer(chip_sums(above, dx), small_out[0])
    for n in BIG:
        grads[n], deltas[n], new_m[n], new_v[n] = (jnp.swapaxes(a, 1, 2) if n == "w_in" else a for a in big_out[n])

    return (loss, dx[None], *[grads[n] for n in WEIGHTS], *[deltas[n] for n in WEIGHTS], *[new_m[n] for n in WEIGHTS],
            *[new_v[n] for n in WEIGHTS])
```

```python
import functools

import jax
import jax.numpy as jnp
from jax import lax
from jax.experimental import pallas as pl
from jax.experimental.pallas import tpu as pltpu
from jax.experimental.pallas import tpu_sc as plsc

f32 = jnp.float32
bf16 = jnp.bfloat16
SDS = jax.ShapeDtypeStruct
MESH = pl.DeviceIdType.MESH

D_MODEL = 1024
BRANCH = 512
N_COLS = 7680
D_FF = 4096
DEPTH = 4
HEADS = 4
HEAD_DIM = 128
HGRN_CHUNK = 64
SG_CHUNK = 128
SG_GROUPS = 4
NORM_EPS = 1e-6
LN_EPS = 1e-5
LB_FLOOR = 1e-30
N_DEV = 8
SHARD_IN = N_COLS // N_DEV
LANE = 128
GATE_COL0 = 9 * BRANCH

ADAM_LR = 0.001
ADAM_B1 = 0.9
ADAM_B2 = 0.999
ADAM_EPS = 1e-08
ADAM_WD = 0.01
ADAM_STEP = 10

MIX_TILE = 256
VMEM_LIMIT = 56 * 1024 * 1024


def _cp(*sem):
    return pltpu.CompilerParams(dimension_semantics=sem or None, vmem_limit_bytes=VMEM_LIMIT)


def _dot(a, b):
    return jnp.dot(a, b, preferred_element_type=f32)


def _dot_nt(a, b):
    return lax.dot_general(a, b, (((1,), (1,)), ((), ())), preferred_element_type=f32)


def _dot_tn(a, b):
    return lax.dot_general(a, b, (((0,), (0,)), ((), ())), preferred_element_type=f32)


def _dot_exact(ones, b):
    hi = b.astype(bf16)
    rest = b - hi.astype(f32)
    mid = rest.astype(bf16)
    low = (rest - mid.astype(f32)).astype(bf16)
    ones = ones.astype(bf16)
    return _dot(ones, hi) + _dot(ones, mid) + _dot(ones, low)


def _sigmoid(x):
    return jax.nn.sigmoid(x)


_GELU_C = 0.7978845608028654
_GELU_A = 0.044715


def _gelu(x):
    return 0.5 * x * (1.0 + jnp.tanh(_GELU_C * (x + _GELU_A * x * x * x)))


def _gelu_grad(x):
    x2 = x * x
    t = jnp.tanh(_GELU_C * (x + _GELU_A * x * x2))
    return 0.5 * (1.0 + t) + 0.5 * x * (1.0 - t * t) * _GELU_C * (1.0 + 3.0 * _GELU_A * x2)


def _rms_stats(x):
    r = lax.rsqrt(jnp.mean(x * x, axis=-1, keepdims=True) + NORM_EPS)
    return r, x * r


def _rms_bwd(dh, xh, r, g):
    dg = jnp.sum(dh * xh, axis=0, keepdims=True)
    dxn = dh * g
    dx = r * (dxn - xh * jnp.mean(dxn * xh, axis=-1, keepdims=True))
    return dx, dg


def _tri(n, upper=False):
    r = lax.broadcasted_iota(jnp.int32, (n, n), 0)
    c = lax.broadcasted_iota(jnp.int32, (n, n), 1)
    return (c >= r) if upper else (c <= r)


def _acc_rows(ref, first, val):
    @pl.when(first)
    def _():
        ref[...] = val

    @pl.when(jnp.logical_not(first))
    def _():
        ref[...] += val


def _rms_mm(x, g, w_t, tm=1024, tn=1536):
    s, n = x.shape[0], w_t.shape[0]
    jm = GATE_COL0 // tn

    def body(x_ref, g_ref, w_ref, pm_ref, pg_ref, h_ref, hs):
        j = pl.program_id(1)

        @pl.when(j == 0)
        def _():
            _, xh = _rms_stats(x_ref[...])
            hv = (xh * g_ref[...]).astype(bf16)
            hs[...] = hv
            h_ref[...] = hv

        res = _dot_nt(hs[...], w_ref[...])

        @pl.when(j < jm)
        def _():
            pm_ref[...] = res

        @pl.when(j >= jm)
        def _():
            pg_ref[...] = res.astype(bf16)

    return pl.pallas_call(
        body, name="rms_mm", grid=(s // tm, n // tn),
        in_specs=[pl.BlockSpec((tm, D_MODEL), lambda i, j: (i, 0)), pl.BlockSpec((1, D_MODEL), lambda i, j: (0, 0)),
                  pl.BlockSpec((tn, D_MODEL), lambda i, j: (j, 0))],
        out_specs=[pl.BlockSpec((tm, tn), lambda i, j: (i, jnp.minimum(j, jm - 1))),
                   pl.BlockSpec((tm, tn), lambda i, j: (i, jnp.maximum(j - jm, 0))), pl.BlockSpec((tm, D_MODEL), lambda i, j: (i, 0))],
        out_shape=[SDS((s, GATE_COL0), f32), SDS((s, n - GATE_COL0), bf16), SDS((s, D_MODEL), bf16)],
        scratch_shapes=[pltpu.VMEM((tm, D_MODEL), bf16)], compiler_params=_cp("parallel", "arbitrary"),
    )(x, g, w_t)


def _hgrn_gates(fp, lb):
    logf = jnp.logaddexp(jnp.log(jnp.maximum(lb, LB_FLOOR)), jnp.log1p(-lb) + jax.nn.log_sigmoid(fp))
    snf = _sigmoid(-fp)
    return logf, snf, (1.0 - lb) * snf


def _p_specs(tile, cols, row_map):
    return [pl.BlockSpec((tile, BRANCH), functools.partial(lambda c, i: (row_map(i), c), c)) for c in cols]


def _mixer_fwd(p, lb, gout, wconv, lng, lnb, wsg, bsg_t):
    s = p.shape[0]
    tt = MIX_TILE
    nch = tt // HGRN_CHUNK

    def body(q_ref, fp_ref, iv_ref, go_ref, bg_ref, cg_ref, xc_ref, u_ref, v_ref, lb_ref, gout_ref, wconv_ref, lng_ref,
             lnb_ref, wsg_ref, bsg_ref, z_ref, opre_ref, st_ref, st_scr, zbuf):
        @pl.when(pl.program_id(0) == 0)
        def _():
            st_scr[...] = jnp.zeros_like(st_scr)
            zbuf[0:8, :] = jnp.zeros((8, BRANCH), f32)

        lbv = lb_ref[...]
        gout_v = gout_ref[...]
        causal = _tri(HGRN_CHUNK)
        tri = causal.astype(f32)
        last_row = lax.broadcasted_iota(jnp.int32, (HGRN_CHUNK, 1), 0) == HGRN_CHUNK - 1
        for c in range(nch):
            rows = slice(HGRN_CHUNK * c, HGRN_CHUNK * (c + 1))
            q_raw = q_ref[rows, :]
            qs = q_raw * _sigmoid(q_raw)
            logf, _, kk = _hgrn_gates(fp_ref[rows, :], lbv)
            b = _dot_exact(tri, logf)
            bl = jnp.sum(jnp.where(last_row, b, 0.0), axis=0, keepdims=True)
            qb = (qs * jnp.exp(b)).astype(bf16)
            kb = (kk * jnp.exp(-b)).astype(bf16)
            kd = (kk * jnp.exp(bl - b)).astype(bf16)
            ebl = jnp.exp(bl)
            vc = iv_ref[rows, :].astype(bf16)
            gate = _sigmoid(go_ref[rows, :])
            for h in range(HEADS):
                sl = slice(HEAD_DIM * h, HEAD_DIM * (h + 1))
                st = st_scr[h]
                st_ref[c, h] = st
                a = jnp.where(causal, _dot_nt(qb[:, sl], kb[:, sl]), 0.0)
                o = _dot(a.astype(bf16), vc[:, sl]) + _dot_nt(qb[:, sl], st.astype(bf16))
                opre_ref[rows, sl] = o
                st_scr[h] = st * ebl[:, sl] + _dot_tn(vc[:, sl], kd[:, sl])
                _, oh = _rms_stats(o)
                z_ref[rows, sl] = (oh * gout_v[:, sl] * gate[:, sl]).astype(bf16)

        zc = cg_ref[...] * xc_ref[...]
        zbuf[8:8 + tt, :] = zc
        y = wconv_ref[0:1, :] * zbuf[pl.ds(6, tt), :] + wconv_ref[1:2, :] * zbuf[pl.ds(7, tt), :] + wconv_ref[2:3, :] * zc
        z_ref[:, BRANCH:2 * BRANCH] = (bg_ref[...] * y).astype(bf16)
        zbuf[0:8, :] = zbuf[tt:tt + 8, :]

        lng_v, lnb_v = lng_ref[...], lnb_ref[...]
        low = _tri(SG_CHUNK)
        wms = [jnp.where(low, wsg_ref[g], 0.0).astype(bf16) for g in range(SG_GROUPS)]
        for cc in range(tt // SG_CHUNK):
            rows = slice(SG_CHUNK * cc, SG_CHUNK * (cc + 1))
            ug = _gelu(u_ref[rows, :])
            vg = _gelu(v_ref[rows, :])
            vcen = vg - jnp.mean(vg, axis=-1, keepdims=True)
            rstd = lax.rsqrt(jnp.mean(vcen * vcen, axis=-1, keepdims=True) + LN_EPS)
            vn = (vcen * rstd * lng_v + lnb_v).astype(bf16)
            for g in range(SG_GROUPS):
                sl = slice(LANE * g, LANE * (g + 1))
                sv = _dot(wms[g], vn[:, sl]) + bsg_ref[:, g:g + 1]
                z_ref[rows, 2 * BRANCH + LANE * g:2 * BRANCH + LANE * (g + 1)] = (ug[:, sl] * sv).astype(bf16)

    full = lambda shape: pl.BlockSpec(shape, lambda i: (0,) * len(shape))
    return pl.pallas_call(
        body, name="mixer_fwd", grid=(s // tt,),
        in_specs=_p_specs(tt, range(9), lambda i: i) + [full((1, BRANCH)), full((1, BRANCH)), full((3, BRANCH)), full((1, BRANCH)),
                                                        full((1, BRANCH)), full((SG_GROUPS, SG_CHUNK, SG_CHUNK)), full((SG_CHUNK, SG_GROUPS))],
        out_specs=[pl.BlockSpec((tt, 3 * BRANCH), lambda i: (i, 0)), pl.BlockSpec((tt, BRANCH), lambda i: (i, 0)),
                   pl.BlockSpec((nch, HEADS, HEAD_DIM, HEAD_DIM), lambda i: (i, 0, 0, 0))],
        out_shape=[SDS((s, 3 * BRANCH), bf16), SDS((s, BRANCH), f32), SDS((s // HGRN_CHUNK, HEADS, HEAD_DIM, HEAD_DIM), f32)],
        scratch_shapes=[pltpu.VMEM((HEADS, HEAD_DIM, HEAD_DIM), f32), pltpu.VMEM((tt + 8, BRANCH), f32)],
        compiler_params=_cp("arbitrary"),
    )(*([p] * 9), lb, gout, wconv, lng, lnb, wsg, bsg_t)


def _branch_gate(z, wb, pg, x, wo, tm=512):
    s = z.shape[0]

    def body(z_ref, wb_ref, g_ref, x_ref, wo_ref, y_ref, m_ref, x1_ref):
        acc = None
        for n in range(3):
            cols = slice(D_MODEL * n, D_MODEL * (n + 1))
            yn = _dot(z_ref[:, BRANCH * n:BRANCH * (n + 1)], wb_ref[n])
            y_ref[:, cols] = yn.astype(bf16)
            t = _sigmoid(g_ref[:, cols].astype(f32)) * yn
            acc = t if acc is None else acc + t
        merged = acc.astype(bf16)
        m_ref[...] = merged
        x1_ref[...] = x_ref[...] + _dot(merged, wo_ref[...])

    row = pl.BlockSpec((tm, D_MODEL), lambda i: (i, 0))
    wide = pl.BlockSpec((tm, 3 * D_MODEL), lambda i: (i, 0))
    return pl.pallas_call(
        body, name="branch_gate", grid=(s // tm,),
        in_specs=[pl.BlockSpec((tm, 3 * BRANCH), lambda i: (i, 0)), pl.BlockSpec((3, BRANCH, D_MODEL), lambda i: (0, 0, 0)), wide, row,
                  pl.BlockSpec((D_MODEL, D_MODEL), lambda i: (0, 0))],
        out_specs=[wide, row, row],
        out_shape=[SDS((s, 3 * D_MODEL), bf16), SDS((s, D_MODEL), bf16), SDS((s, D_MODEL), f32)], compiler_params=_cp("parallel"),
    )(z, wb, pg, x, wo)


def _ffn(x1, g, w1, w2, tm=1024, tf=1024):
    s = x1.shape[0]
    nf = D_FF // tf

    def body(x_ref, g_ref, w1_ref, w2_ref, o_ref, h_ref, ra_ref, hs, acc):
        f = pl.program_id(1)

        @pl.when(f == 0)
        def _():
            _, xh = _rms_stats(x_ref[...])
            hv = (xh * g_ref[...]).astype(bf16)
            hs[...] = hv
            h_ref[...] = hv
            acc[...] = jnp.zeros_like(acc)

        ra = jnp.maximum(_dot(hs[...], w1_ref[...]), 0.0)
        ra_ref[...] = ra.astype(bf16)
        acc[...] += _dot((ra * ra).astype(bf16), w2_ref[...])

        @pl.when(f == nf - 1)
        def _():
            o_ref[...] = x_ref[...] + acc[...]

    return pl.pallas_call(
        body, name="ffn", grid=(s // tm, nf),
        in_specs=[pl.BlockSpec((tm, D_MODEL), lambda i, f: (i, 0)), pl.BlockSpec((1, D_MODEL), lambda i, f: (0, 0)),
                  pl.BlockSpec((D_MODEL, tf), lambda i, f: (0, f)), pl.BlockSpec((tf, D_MODEL), lambda i, f: (f, 0))],
        out_specs=[pl.BlockSpec((tm, D_MODEL), lambda i, f: (i, 0)), pl.BlockSpec((tm, D_MODEL), lambda i, f: (i, 0)),
                   pl.BlockSpec((tm, tf), lambda i, f: (i, f))],
        out_shape=[SDS((s, D_MODEL), f32), SDS((s, D_MODEL), bf16), SDS((s, D_FF), bf16)],
        scratch_shapes=[pltpu.VMEM((tm, D_MODEL), bf16), pltpu.VMEM((tm, D_MODEL), f32)], compiler_params=_cp("parallel", "arbitrary"),
    )(x1, g, w1, w2)


def _final(x, target, g, tm=512):
    s = x.shape[0]

    def body(x_ref, t_ref, g_ref, loss_ref, dx_ref, dxb_ref, dg_ref):
        first = pl.program_id(0) == 0
        gv = g_ref[...]
        r, xh = _rms_stats(x_ref[...])
        e = xh * gv - t_ref[...]
        tile_loss = 0.5 * jnp.sum(jnp.mean(e * e, axis=-1, keepdims=True), axis=0, keepdims=True)
        dx, dg = _rms_bwd(e * (1.0 / D_MODEL), xh, r, gv)
        dx_ref[...] = dx
        dxb_ref[...] = dx.astype(bf16)
        _acc_rows(dg_ref, first, dg)
        _acc_rows(loss_ref, first, jnp.broadcast_to(tile_loss, (1, LANE)))

    row = pl.BlockSpec((tm, D_MODEL), lambda i: (i, 0))
    return pl.pallas_call(
        body, name="final_loss", grid=(s // tm,), in_specs=[row, row, pl.BlockSpec((1, D_MODEL), lambda i: (0, 0))],
        out_specs=[pl.BlockSpec((1, LANE), lambda i: (0, 0)), row, row, pl.BlockSpec((1, D_MODEL), lambda i: (0, 0))],
        out_shape=[SDS((1, LANE), f32), SDS((s, D_MODEL), f32), SDS((s, D_MODEL), bf16), SDS((1, D_MODEL), f32)],
        compiler_params=_cp("arbitrary"),
    )(x, target, g)


def _ffn_bwd(dx2, dx2b, x1, g, ra, w1, w2, tm=512, tf=2048):
    s = x1.shape[0]
    nf = D_FF // tf

    def body(dx_ref, dxb_ref, x_ref, g_ref, ra_ref, w1_ref, w2_ref, da_ref, dx1_ref, dx1b_ref, dg_ref, acc):
        i, f = pl.program_id(0), pl.program_id(1)

        @pl.when(f == 0)
        def _():
            acc[...] = jnp.zeros_like(acc)

        da = (_dot_nt(dxb_ref[...], w2_ref[...]) * (2.0 * ra_ref[...].astype(f32))).astype(bf16)
        da_ref[...] = da
        acc[...] += _dot_nt(da, w1_ref[...])

        @pl.when(f == nf - 1)
        def _():
            r, xh = _rms_stats(x_ref[...])
            dx, dg = _rms_bwd(acc[...], xh, r, g_ref[...])
            dx = dx + dx_ref[...]
            dx1_ref[...] = dx
            dx1b_ref[...] = dx.astype(bf16)
            _acc_rows(dg_ref, i == 0, dg)

    row = pl.BlockSpec((tm, D_MODEL), lambda i, f: (i, 0))
    col = pl.BlockSpec((tm, tf), lambda i, f: (i, f))
    return pl.pallas_call(
        body, name="ffn_bwd", grid=(s // tm, nf),
        in_specs=[row, row, row, pl.BlockSpec((1, D_MODEL), lambda i, f: (0, 0)), col,
                  pl.BlockSpec((D_MODEL, tf), lambda i, f: (0, f)), pl.BlockSpec((tf, D_MODEL), lambda i, f: (f, 0))],
        out_specs=[col, row, row, pl.BlockSpec((1, D_MODEL), lambda i, f: (0, 0))],
        out_shape=[SDS((s, D_FF), bf16), SDS((s, D_MODEL), f32), SDS((s, D_MODEL), bf16), SDS((1, D_MODEL), f32)],
        scratch_shapes=[pltpu.VMEM((tm, D_MODEL), f32)], compiler_params=_cp("arbitrary", "arbitrary"),
    )(dx2, dx2b, x1, g, ra, w1, w2)


def _dw_in(dpm, dpg, h, tm=768):
    s = h.shape[0]
    km, kg = dpm.shape[1] // tm, dpg.shape[1] // tm

    def body(am_ref, ag_ref, h_ref, o_ref):
        i = pl.program_id(0)

        @pl.when(i < km)
        def _():
            o_ref[...] = _dot_tn(am_ref[...], h_ref[...]).astype(bf16)

        @pl.when(i >= km)
        def _():
            o_ref[...] = _dot_tn(ag_ref[...], h_ref[...]).astype(bf16)

    return pl.pallas_call(
        body, name="dw_in", grid=(km + kg,),
        in_specs=[pl.BlockSpec((s, tm), lambda i: (0, jnp.minimum(i, km - 1))), pl.BlockSpec((s, tm), lambda i: (0, jnp.maximum(i - km, 0))),
                  pl.BlockSpec((s, D_MODEL), lambda i: (0, 0))],
        out_specs=pl.BlockSpec((tm, D_MODEL), lambda i: (i, 0)), out_shape=SDS((km * tm + kg * tm, D_MODEL), bf16),
        compiler_params=_cp("arbitrary"),
    )(dpm, dpg, h)


def _dw_ff2_o(ra, dx2b, merged, dx1b, tm=512):
    s = ra.shape[0]
    k2, ko = D_FF // tm, D_MODEL // tm

    def body(ra_ref, dx2_ref, m_ref, dx1_ref, o2_ref, oo_ref):
        i = pl.program_id(0)

        @pl.when(i < k2)
        def _():
            av = ra_ref[...].astype(f32)
            o2_ref[...] = _dot_tn((av * av).astype(bf16), dx2_ref[...]).astype(bf16)

        @pl.when(i >= k2)
        def _():
            oo_ref[...] = _dot_tn(m_ref[...], dx1_ref[...]).astype(bf16)

    first, then = (lambda i: jnp.minimum(i, k2 - 1)), (lambda i: jnp.maximum(i - k2, 0))
    whole = pl.BlockSpec((s, D_MODEL), lambda i: (0, 0))
    return pl.pallas_call(
        body, name="dw_ff2_o", grid=(k2 + ko,),
        in_specs=[pl.BlockSpec((s, tm), lambda i: (0, first(i))), whole, pl.BlockSpec((s, tm), lambda i: (0, then(i))), whole],
        out_specs=[pl.BlockSpec((tm, D_MODEL), lambda i: (first(i), 0)), pl.BlockSpec((tm, D_MODEL), lambda i: (then(i), 0))],
        out_shape=[SDS((D_FF, D_MODEL), bf16), SDS((D_MODEL, D_MODEL), bf16)], compiler_params=_cp("arbitrary"),
    )(ra, dx2b, merged, dx1b)


def _mm_tn_slabs(a, b, nb, m, nblk, rel, width, tm=512, name="mm_tn_slabs"):
    s = a.shape[0]
    n = b.shape[1] // nb
    ng, mi, nw = n // nblk, m // tm, len(rel)

    def body(a_ref, b_ref, o_ref):
        full = _dot_tn(a_ref[...], b_ref[...])
        for r, start in enumerate(rel):
            o_ref[r] = full[:, start:start + width].astype(bf16)

    return pl.pallas_call(
        body, name=name, grid=(nb, ng, mi),
        in_specs=[pl.BlockSpec((s, tm), lambda k, g, i: (0, k * mi + i)), pl.BlockSpec((s, nblk), lambda k, g, i: (0, k * ng + g))],
        out_specs=pl.BlockSpec((nw, None, tm, width), lambda k, g, i: (g, k, i, 0)), out_shape=SDS((ng * nw, nb, m, width), bf16),
        compiler_params=_cp("parallel", "parallel", "parallel"),
    )(a, b)


def _merge_bwd(dx1b, wo, y, pg, wb, after, tm=512):
    s = dx1b.shape[0]

    def body(dx_ref, wo_ref, y_ref, g_ref, wb_ref, after_ref, dy_ref, dg_ref, dz_ref):
        del after_ref
        dm = _dot_nt(dx_ref[...], wo_ref[...])
        for n in range(3):
            cols = slice(D_MODEL * n, D_MODEL * (n + 1))
            gate = _sigmoid(g_ref[:, cols].astype(f32))
            t = dm * gate
            dy = t.astype(bf16)
            dy_ref[:, cols] = dy
            dg_ref[:, cols] = (t * y_ref[:, cols].astype(f32) * (1.0 - gate)).astype(bf16)
            dz_ref[:, BRANCH * n:BRANCH * (n + 1)] = _dot_nt(dy, wb_ref[n]).astype(bf16)

    wide = pl.BlockSpec((tm, 3 * D_MODEL), lambda i: (i, 0))
    return pl.pallas_call(
        body, name="merge_bwd", grid=(s // tm,),
        in_specs=[pl.BlockSpec((tm, D_MODEL), lambda i: (i, 0)), pl.BlockSpec((D_MODEL, D_MODEL), lambda i: (0, 0)), wide, wide,
                  pl.BlockSpec((3, BRANCH, D_MODEL), lambda i: (0, 0, 0)), pl.BlockSpec(memory_space=pl.ANY)],
        out_specs=[wide, wide, pl.BlockSpec((tm, 3 * BRANCH), lambda i: (i, 0))],
        out_shape=[SDS((s, 3 * D_MODEL), bf16), SDS((s, 3 * D_MODEL), bf16), SDS((s, 3 * BRANCH), bf16)],
        compiler_params=_cp("parallel"),
    )(dx1b, wo, y, pg, wb, after)


def _mixer_bwd(p, dz, opre, states, lb, gout, wconv, lng, lnb, wsg, bsg_t):
    s = p.shape[0]
    tt = MIX_TILE
    nt = s // tt
    nch = tt // HGRN_CHUNK
    rev = lambda i: nt - 1 - i

    def body(q_ref, fp_ref, iv_ref, go_ref, bg_ref, cg_ref, xc_ref, u_ref, v_ref, cgp_ref, xcp_ref, dz_ref, opre_ref, st_ref,
             lb_ref, gout_ref, wconv_ref, lng_ref, lnb_ref, wsg_ref, bsg_ref,
             dp_ref, vec_ref, dwsg_ref, dbsg_ref, dst_scr, zbuf, dybuf, dbsg_acc):
        i = pl.program_id(0)

        @pl.when(i == 0)
        def _():
            dst_scr[...] = jnp.zeros_like(dst_scr)
            dybuf[tt:tt + 8, :] = jnp.zeros((8, BRANCH), f32)
            vec_ref[...] = jnp.zeros_like(vec_ref)
            dwsg_ref[...] = jnp.zeros_like(dwsg_ref)
            dbsg_acc[...] = jnp.zeros_like(dbsg_acc)

        lbv = lb_ref[...]
        gout_v = gout_ref[...]
        causal = _tri(HGRN_CHUNK)
        tri = causal.astype(f32)
        tri_up = _tri(HGRN_CHUNK, upper=True).astype(f32)
        last_row = lax.broadcasted_iota(jnp.int32, (HGRN_CHUNK, 1), 0) == HGRN_CHUNK - 1
        lb_live = (lbv > LB_FLOOR).astype(f32)
        dlb = jnp.zeros((1, BRANCH), f32)
        dgout = jnp.zeros((1, BRANCH), f32)
        for c in reversed(range(nch)):
            rows = slice(HGRN_CHUNK * c, HGRN_CHUNK * (c + 1))
            q_c, fp = q_ref[rows, :], fp_ref[rows, :]
            sq_c = _sigmoid(q_c)
            sfp_c = _sigmoid(fp)
            logf, snf_c, kk = _hgrn_gates(fp, lbv)
            invf_c = jnp.exp(-logf)
            doa = dz_ref[rows, 0:BRANCH].astype(f32)
            o = opre_ref[rows, :]
            sgo = _sigmoid(go_ref[rows, :])
            d_o, dgo, dg_c = [], [], []
            for h in range(HEADS):
                sl = slice(HEAD_DIM * h, HEAD_DIM * (h + 1))
                r, oh = _rms_stats(o[:, sl])
                dgo.append(doa[:, sl] * oh * gout_v[:, sl] * sgo[:, sl] * (1.0 - sgo[:, sl]))
                dx, dg = _rms_bwd(doa[:, sl] * sgo[:, sl], oh, r, gout_v[:, sl])
                d_o.append(dx)
                dg_c.append(dg)
            dp_ref[rows, 3 * BRANCH:4 * BRANCH] = jnp.concatenate(dgo, axis=1).astype(bf16)
            dgout = dgout + jnp.concatenate(dg_c, axis=1)
            dob = jnp.concatenate(d_o, axis=1).astype(bf16)
            b = _dot_exact(tri, logf)
            bl = jnp.sum(jnp.where(last_row, b, 0.0), axis=0, keepdims=True)
            eb, enb, edl, ebl = jnp.exp(b), jnp.exp(-b), jnp.exp(bl - b), jnp.exp(bl)
            qbf, kbf, kdf = q_c * sq_c * eb, kk * enb, kk * edl
            qb, kb, kd = qbf.astype(bf16), kbf.astype(bf16), kdf.astype(bf16)
            vc = iv_ref[rows, :].astype(bf16)
            dv, dqb, dkb, dkd, debl = [], [], [], [], []
            for h in range(HEADS):
                sl = slice(HEAD_DIM * h, HEAD_DIM * (h + 1))
                st = st_ref[c, h]
                dst = dst_scr[h]
                stb, dstb = st.astype(bf16), dst.astype(bf16)
                a = jnp.where(causal, _dot_nt(qb[:, sl], kb[:, sl]), 0.0).astype(bf16)
                da = jnp.where(causal, _dot_nt(dob[:, sl], vc[:, sl]), 0.0).astype(bf16)
                dv.append(_dot_tn(a, dob[:, sl]) + _dot_nt(kd[:, sl], dstb))
                dqb.append(_dot(dob[:, sl], stb) + _dot(da, kb[:, sl]))
                dkb.append(_dot_tn(da, qb[:, sl]))
                dkd.append(_dot(vc[:, sl], dstb))
                debl.append(jnp.sum(st * dst, axis=0, keepdims=True))
                dst_scr[h] = _dot_tn(dob[:, sl], qb[:, sl]) + dst * ebl[:, sl]
            dv, dqb, dkb, dkd = (jnp.concatenate(t, axis=1) for t in (dv, dqb, dkb, dkd))
            debl = jnp.concatenate(debl, axis=1)
            t_kd = dkd * kdf
            dbl = ebl * debl + jnp.sum(t_kd, axis=0, keepdims=True)
            db = dqb * qbf - dkb * kbf - t_kd + jnp.where(last_row, dbl, 0.0)
            dkk = dkb * enb + dkd * edl
            dlc = _dot_exact(tri_up, db)
            slope = (1.0 - lbv) * sfp_c * snf_c
            dp_ref[rows, 0:BRANCH] = (dqb * eb * sq_c * (1.0 + q_c * (1.0 - sq_c))).astype(bf16)
            dp_ref[rows, BRANCH:2 * BRANCH] = (slope * (dlc * invf_c - dkk)).astype(bf16)
            dp_ref[rows, 2 * BRANCH:3 * BRANCH] = dv.astype(bf16)
            dlb = dlb + jnp.sum(dlc * (lb_live - sfp_c) * invf_c - dkk * snf_c, axis=0, keepdims=True)
        vec_ref[0:1, :] += dlb
        vec_ref[1:2, :] += dgout

        dob_ = dz_ref[:, BRANCH:2 * BRANCH].astype(f32)
        bg, cg, xc = bg_ref[...], cg_ref[...], xc_ref[...]
        zc = cg * xc
        zbuf[0:8, :] = jnp.where(i < nt - 1, cgp_ref[...] * xcp_ref[...], 0.0)
        zbuf[8:8 + tt, :] = zc
        w0, w1, w2 = wconv_ref[0:1, :], wconv_ref[1:2, :], wconv_ref[2:3, :]
        y = w0 * zbuf[pl.ds(6, tt), :] + w1 * zbuf[pl.ds(7, tt), :] + w2 * zc
        dy = dob_ * bg
        dybuf[0:tt, :] = dy
        dy1, dy2 = dybuf[pl.ds(1, tt), :], dybuf[pl.ds(2, tt), :]
        dzc = w2 * dy + w1 * dy1 + w0 * dy2
        dp_ref[:, 4 * BRANCH:5 * BRANCH] = (dob_ * y).astype(bf16)
        dp_ref[:, 5 * BRANCH:6 * BRANCH] = (dzc * xc).astype(bf16)
        dp_ref[:, 6 * BRANCH:7 * BRANCH] = (dzc * cg).astype(bf16)
        vec_ref[4:5, :] += jnp.sum(zc * dy2, axis=0, keepdims=True)
        vec_ref[5:6, :] += jnp.sum(zc * dy1, axis=0, keepdims=True)
        vec_ref[6:7, :] += jnp.sum(zc * dy, axis=0, keepdims=True)
        dybuf[tt:tt + 8, :] = dybuf[0:8, :]

        lng_v, lnb_v = lng_ref[...], lnb_ref[...]
        low = _tri(SG_CHUNK)
        wms = [jnp.where(low, wsg_ref[g], 0.0).astype(bf16) for g in range(SG_GROUPS)]
        dlng = jnp.zeros((1, BRANCH), f32)
        dlnb = jnp.zeros((1, BRANCH), f32)
        for cc in range(tt // SG_CHUNK):
            rows = slice(SG_CHUNK * cc, SG_CHUNK * (cc + 1))
            doc = dz_ref[rows, 2 * BRANCH:3 * BRANCH].astype(f32)
            u_raw, v_raw = u_ref[rows, :], v_ref[rows, :]
            ug = _gelu(u_raw)
            vg = _gelu(v_raw)
            vcen = vg - jnp.mean(vg, axis=-1, keepdims=True)
            rstd = lax.rsqrt(jnp.mean(vcen * vcen, axis=-1, keepdims=True) + LN_EPS)
            vhat = vcen * rstd
            vn = (vhat * lng_v + lnb_v).astype(bf16)
            dvn = []
            for g in range(SG_GROUPS):
                sl = slice(LANE * g, LANE * (g + 1))
                sv = _dot(wms[g], vn[:, sl]) + bsg_ref[:, g:g + 1]
                dp_ref[rows, 7 * BRANCH + LANE * g:7 * BRANCH + LANE * (g + 1)] = (doc[:, sl] * sv * _gelu_grad(u_raw[:, sl])).astype(bf16)
                dsv = doc[:, sl] * ug[:, sl]
                dsvb = dsv.astype(bf16)
                dbsg_acc[:, sl] += dsv
                dwsg_ref[g] += jnp.where(low, _dot_nt(dsvb, vn[:, sl]), 0.0)
                dvn.append(_dot_tn(wms[g], dsvb))
            dvn = jnp.concatenate(dvn, axis=1)
            dlng = dlng + jnp.sum(dvn * vhat, axis=0, keepdims=True)
            dlnb = dlnb + jnp.sum(dvn, axis=0, keepdims=True)
            dvh = dvn * lng_v
            dvg = rstd * (dvh - jnp.mean(dvh, axis=-1, keepdims=True) - vhat * jnp.mean(dvh * vhat, axis=-1, keepdims=True))
            dp_ref[rows, 8 * BRANCH:9 * BRANCH] = (dvg * _gelu_grad(v_raw)).astype(bf16)
        vec_ref[2:3, :] += dlng
        vec_ref[3:4, :] += dlnb

        @pl.when(i == nt - 1)
        def _():
            for g in range(SG_GROUPS):
                dbsg_ref[:, g:g + 1] = jnp.sum(dbsg_acc[:, LANE * g:LANE * (g + 1)], axis=1, keepdims=True)

    full = lambda shape: pl.BlockSpec(shape, lambda i: (0,) * len(shape))
    tail = lambda c: pl.BlockSpec((8, BRANCH), lambda i: (jnp.maximum(rev(i) * (tt // 8) - 1, 0), c))
    return pl.pallas_call(
        body, name="mixer_bwd", grid=(nt,),
        in_specs=_p_specs(tt, range(9), rev) + [tail(5), tail(6), pl.BlockSpec((tt, 3 * BRANCH), lambda i: (rev(i), 0)),
                                                pl.BlockSpec((tt, BRANCH), lambda i: (rev(i), 0)),
                                                pl.BlockSpec((nch, HEADS, HEAD_DIM, HEAD_DIM), lambda i: (rev(i), 0, 0, 0)),
                                                full((1, BRANCH)), full((1, BRANCH)), full((3, BRANCH)), full((1, BRANCH)), full((1, BRANCH)),
                                                full((SG_GROUPS, SG_CHUNK, SG_CHUNK)), full((SG_CHUNK, SG_GROUPS))],
        out_specs=[pl.BlockSpec((tt, 9 * BRANCH), lambda i: (rev(i), 0)), full((8, BRANCH)), full((SG_GROUPS, SG_CHUNK, SG_CHUNK)),
                   full((SG_CHUNK, SG_GROUPS))],
        out_shape=[SDS((s, 9 * BRANCH), bf16), SDS((8, BRANCH), f32), SDS((SG_GROUPS, SG_CHUNK, SG_CHUNK), f32), SDS((SG_CHUNK, SG_GROUPS), f32)],
        scratch_shapes=[pltpu.VMEM((HEADS, HEAD_DIM, HEAD_DIM), f32), pltpu.VMEM((tt + 8, BRANCH), f32), pltpu.VMEM((tt + 8, BRANCH), f32),
                        pltpu.VMEM((SG_CHUNK, BRANCH), f32)],
        compiler_params=_cp("arbitrary"),
    )(*([p] * 11), dz, opre, states, lb, gout, wconv, lng, lnb, wsg, bsg_t)


def _dh_bwd(dpm, dpg, w_t, x, dx1, g, after, tm=1024, tk=1536):
    s = x.shape[0]
    km = dpm.shape[1] // tk
    nk = km + dpg.shape[1] // tk

    def body(dpm_ref, dpg_ref, w_ref, x_ref, dx1_ref, g_ref, after_ref, dx_ref, dxb_ref, dg_ref, acc):
        del after_ref
        i, k = pl.program_id(0), pl.program_id(1)

        @pl.when(k == 0)
        def _():
            acc[...] = jnp.zeros_like(acc)

        @pl.when(k < km)
        def _():
            acc[...] += _dot(dpm_ref[...], w_ref[...])

        @pl.when(k >= km)
        def _():
            acc[...] += _dot(dpg_ref[...], w_ref[...])

        @pl.when(k == nk - 1)
        def _():
            r, xh = _rms_stats(x_ref[...])
            dx, dg = _rms_bwd(acc[...], xh, r, g_ref[...])
            dx = dx + dx1_ref[...]
            dx_ref[...] = dx
            dxb_ref[...] = dx.astype(bf16)
            _acc_rows(dg_ref, i == 0, dg)

    row = pl.BlockSpec((tm, D_MODEL), lambda i, k: (i, 0))
    vec = pl.BlockSpec((1, D_MODEL), lambda i, k: (0, 0))
    return pl.pallas_call(
        body, name="dh_bwd", grid=(s // tm, nk),
        in_specs=[pl.BlockSpec((tm, tk), lambda i, k: (i, jnp.minimum(k, km - 1))),
                  pl.BlockSpec((tm, tk), lambda i, k: (i, jnp.maximum(k - km, 0))),
                  pl.BlockSpec((tk, D_MODEL), lambda i, k: (k, 0)), row, row, vec, pl.BlockSpec(memory_space=pl.ANY)],
        out_specs=[row, row, vec], out_shape=[SDS((s, D_MODEL), f32), SDS((s, D_MODEL), bf16), SDS((1, D_MODEL), f32)],
        scratch_shapes=[pltpu.VMEM((tm, D_MODEL), f32)], compiler_params=_cp("arbitrary", "arbitrary"),
    )(dpm, dpg, w_t, x, dx1, g, after)


def _layer_fwd(x, weight, sm):
    p, pg, h = _rms_mm(x, sm["g_mix"], weight("w_in", x))
    z, opre, states = _mixer_fwd(p, sm["lb"], sm["g_out"], sm["w_conv"], sm["ln_g"], sm["ln_b"], sm["w_sg"], sm["b_sg_t"])
    y, merged, x1 = _branch_gate(z, weight("w_branch", z), pg, x, weight("w_o", z))
    x2, h2, ra = _ffn(x1, sm["g_ffn"], weight("w_ff1", x1), weight("w_ff2", x1))
    saved = dict(x=x, p=p, pg=pg, h=h, z=z, opre=opre, states=states, y=y, merged=merged, x1=x1, h2=h2, ra=ra)
    return x2, saved


def _layer_bwd(dx2, dx2b, sv, w, sm, between, before_end):
    nchip = N_DEV // 2
    by_chip = lambda g: g.reshape((nchip, 2) + g.shape[1:])
    da, dx1, dx1b, dg_ffn = _ffn_bwd(dx2, dx2b, sv["x1"], sm["g_ffn"], sv["ra"], w["w_ff1"], w["w_ff2"])
    g_ff2, g_o = _dw_ff2_o(sv["ra"], dx2b, sv["merged"], dx1b)
    g_ff2 = by_chip(g_ff2.reshape(N_DEV, D_FF // N_DEV, D_MODEL))
    g_o = by_chip(g_o.reshape(N_DEV, D_MODEL // N_DEV, D_MODEL))
    g_ff1 = by_chip(_mm_tn_slabs(sv["h2"], da, 1, D_MODEL, D_FF // 2, [i * (D_FF // N_DEV) for i in range(nchip)], D_FF // N_DEV,
                                 name="dw_ff1")[:, 0])
    dy, dpg, dz = _merge_bwd(dx1b, w["w_o"], sv["y"], sv["pg"], w["w_branch"], between(dx1))
    g_branch = by_chip(_mm_tn_slabs(sv["z"], dy, 3, BRANCH, D_MODEL, [i * (D_MODEL // N_DEV) for i in range(N_DEV)], D_MODEL // N_DEV,
                                    name="dw_branch"))
    dpm, vecs, dwsg, dbsg_t = _mixer_bwd(sv["p"], dz, sv["opre"], sv["states"], sm["lb"], sm["g_out"], sm["w_conv"],
                                         sm["ln_g"], sm["ln_b"], sm["w_sg"], sm["b_sg_t"])
    g_in = by_chip(_dw_in(dpm, dpg, sv["h"]).reshape(N_DEV, SHARD_IN, D_MODEL))
    big = dict(w_in=g_in, w_branch=g_branch, w_o=g_o, w_ff1=g_ff1, w_ff2=g_ff2)
    dx, dxb, dg_mix = _dh_bwd(dpm, dpg, w["w_in"], sv["x"], dx1, sm["g_mix"], before_end(big))
    small = dict(g_mix=dg_mix, g_ffn=dg_ffn, vecs=vecs, w_sg=dwsg, b_sg_t=dbsg_t, dx1=dx1)
    return dx, dxb, big, small


BIG = ("w_in", "w_branch", "w_o", "w_ff1", "w_ff2")
ANY = pl.BlockSpec(memory_space=pl.ANY)


def _place():
    return lax.axis_index("x"), lax.axis_index("y"), lax.axis_index("c")


def _al(v, m):
    return pl.multiple_of(v * m, m)


def _shard_of(refs, dev, which=range(len(BIG))):
    out = []
    for ref, t in zip(refs, which):
        by_cols = BIG[t] in ("w_branch", "w_ff1")
        n = ref.shape[-1 if by_cols else 0] // N_DEV
        part = pl.ds(_al(dev, n), n)
        out.append(ref.at[(slice(None),) * (len(ref.shape) - 1) + (part,)] if by_cols else ref.at[part])
    return out


def _gather_out_shapes(shards):
    s_in, s_b, s_o, s_1, s_2 = (shards[n] for n in BIG)
    return [SDS((s_in.shape[1] * N_DEV, s_in.shape[2]), bf16), SDS(s_b.shape[1:3] + (s_b.shape[3] * N_DEV,), bf16),
            SDS((s_o.shape[1] * N_DEV, s_o.shape[2]), bf16), SDS((s_1.shape[1], s_1.shape[2] * N_DEV), bf16),
            SDS((s_2.shape[1] * N_DEV, s_2.shape[2]), bf16)]


def _seq_all_gather_layer(layer, which, n_early, shard_refs, out_shapes, tag="", place_mine=False):
    nt = len(which)
    outs = [jax.empty_ref(sh, memory_space=pltpu.MemorySpace.HBM) for sh in out_shapes]
    early, late = tuple(range(n_early)), tuple(range(n_early, nt))

    @pl.kernel(mesh=plsc.ScalarSubcoreMesh(axis_name="seq", num_cores=1), name=f"seq_all_gather_l{layer}{tag}",
               scratch_types=(pltpu.SemaphoreType.DMA((9,)), pltpu.SemaphoreType.DMA((9,)), pltpu.SemaphoreType.DMA((nt,))),
               compiler_params=pltpu.CompilerParams(collective_id=1))
    def launch(send_sems, recv_sems, local_sems):
        x, y, c = _place()
        me, sibling = (x, y, c), (x, y, 1 - c)
        first, second, diag = _ici_route(x, y, c)
        _handshake([sibling, first, second])
        mine = [r.at[layer] for r in shard_refs]
        own = [pltpu.make_async_copy(src, dst, local_sems.at[t])
               for t, (src, dst) in enumerate(zip(mine, _shard_of(outs, 4 * x + 2 * y + c, which)))] if place_mine else []

        def copies(k, blk, to, src=None, part=range(nt)):
            dst = _shard_of(outs, 4 * blk[0] + 2 * blk[1] + blk[2], which)
            src = dst if src is None else src
            return [pltpu.make_async_remote_copy(src_ref=src[t], dst_ref=dst[t], send_sem=send_sems.at[k], recv_sem=recv_sems.at[k],
                                                 device_id=to, device_id_type=MESH) for t in part]

        def start(cps):
            for cp in cps:
                cp.start()
            return cps

        def landed(cps):
            for cp in cps:
                cp.wait_recv()

        sent = start(copies(0, me, sibling, src=mine) + copies(1, me, first, src=mine, part=early)
                     + copies(2, me, first, src=mine, part=late) + copies(3, me, second, src=mine))
        start(own)
        landed(copies(1, first, me, part=early))
        sent += start(copies(4, first, second, part=early) + copies(6, first, sibling, part=early))
        landed(copies(2, first, me, part=late))
        sent += start(copies(5, first, second, part=late) + copies(6, first, sibling, part=late))
        landed(copies(3, second, me))
        sent += start(copies(7, second, sibling))
        landed(copies(4, diag, me, part=early) + copies(5, diag, me, part=late))
        sent += start(copies(8, diag, sibling))
        other = lambda p: (p[0], p[1], 1 - c)
        landed(copies(0, sibling, me) + copies(6, other(second), me) + copies(7, other(first), me) + copies(8, other(diag), me))
        for cp in sent:
            cp.wait_send()
        for cp in own:
            cp.wait()

    launch()
    return [o[...] for o in outs]


def _ici_route(x, y, c):
    return (x ^ (1 - c), y ^ c, c), (x ^ c, y ^ (1 - c), c), (1 - x, 1 - y, c)


def _place_own(where, which, shards, gathered, after):
    nt = len(which)

    def body(where_ref, *refs):
        del where_ref
        for src, dst in zip(refs[:nt], refs[2 * nt + 1:]):
            dst[...] = src[...]

    in_specs, out_specs = [], []
    for t, sh in zip(which, shards):
        blk = sh.shape[1:]
        in_specs.append(pl.BlockSpec((None,) + blk, functools.partial(lambda nd, i, wh: (wh[0],) + (0,) * nd, len(blk))))
        by_cols = BIG[t] in ("w_branch", "w_ff1")
        out_specs.append(pl.BlockSpec(blk, functools.partial(
            lambda nd, cols, i, wh: (0,) * (nd - 1) + (wh[1],) if cols else (wh[1],) + (0,) * (nd - 1), len(blk), by_cols)))
    return pl.pallas_call(
        body, name="place_own", out_shape=[SDS(g.shape, g.dtype) for g in gathered],
        input_output_aliases={1 + nt + i: i for i in range(nt)}, compiler_params=_cp("arbitrary"),
        grid_spec=pltpu.PrefetchScalarGridSpec(num_scalar_prefetch=1, grid=(1,), in_specs=in_specs + [ANY] * (nt + 1), out_specs=out_specs),
    )(where, *shards, *gathered, after)


def _handshake(peers):
    barrier = pltpu.get_barrier_semaphore()
    for p in peers:
        pl.semaphore_signal(barrier, inc=1, device_id=p, device_id_type=MESH)
    pl.semaphore_wait(barrier, len(peers))


def _seq_exchange_on_chip(grads):
    nt, nchip = len(BIG), N_DEV // 2
    g_refs = [jax.new_ref(g, memory_space=pltpu.MemorySpace.HBM) for g in grads]
    outs = [jax.empty_ref(SDS((nchip,) + g.shape[2:], bf16), memory_space=pltpu.MemorySpace.HBM) for g in grads]

    @pl.kernel(mesh=plsc.ScalarSubcoreMesh(axis_name="seq", num_cores=1), name="seq_rs_on_chip",
               scratch_types=(pltpu.SemaphoreType.DMA((nchip,)), pltpu.SemaphoreType.DMA((nchip,))),
               compiler_params=pltpu.CompilerParams(collective_id=2))
    def launch(send_sems, recv_sems):
        x, y, c = _place()
        sibling = (x, y, 1 - c)
        _handshake([sibling])
        remote = [pltpu.make_async_remote_copy(src_ref=g_refs[t].at[j, 1 - c], dst_ref=outs[t].at[j], send_sem=send_sems.at[j],
                                               recv_sem=recv_sems.at[j], device_id=sibling, device_id_type=MESH)
                  for j in range(nchip) for t in range(nt)]
        for cp in remote:
            cp.start()
        for cp in remote:
            cp.wait_recv()
        for cp in remote:
            cp.wait_send()

    launch()
    return [o[...] for o in outs], [g[...] for g in g_refs]


def _seq_exchange_between_chips(sums):
    nt = len(BIG)
    s_refs = [jax.new_ref(a, memory_space=pltpu.MemorySpace.HBM) for a in sums]
    outs = [jax.empty_ref(SDS((3,) + a.shape[1:], bf16), memory_space=pltpu.MemorySpace.HBM) for a in sums]
    transit = [jax.empty_ref(SDS(a.shape[1:], bf16), memory_space=pltpu.MemorySpace.HBM) for a in sums]

    early, late = (0,), tuple(range(1, nt))

    @pl.kernel(mesh=plsc.ScalarSubcoreMesh(axis_name="seq", num_cores=1), name="seq_rs_between_chips",
               scratch_types=(pltpu.SemaphoreType.DMA((6,)), pltpu.SemaphoreType.DMA((6,))),
               compiler_params=pltpu.CompilerParams(collective_id=3))
    def launch(send_sems, recv_sems):
        x, y, c = _place()
        first, second, diag = _ici_route(x, y, c)
        _handshake([first, second])

        def copies(k, src, dst, to, part=range(nt)):
            return [pltpu.make_async_remote_copy(src_ref=src(t), dst_ref=dst(t), send_sem=send_sems.at[k], recv_sem=recv_sems.at[k],
                                                 device_id=to, device_id_type=MESH) for t in part]

        chip_of = lambda p: 2 * p[0] + p[1]
        for_diag = lambda t: s_refs[t].at[chip_of(diag)]
        through = lambda t: transit[t]
        last = lambda t: outs[t].at[2]
        direct = (copies(0, lambda t: s_refs[t].at[chip_of(first)], lambda t: outs[t].at[0], first)
                  + copies(1, lambda t: s_refs[t].at[chip_of(second)], lambda t: outs[t].at[1], second))
        via = [copies(2, for_diag, through, first, early), copies(3, for_diag, through, first, late)]
        passed = [copies(4, through, last, second, early), copies(5, through, last, second, late)]
        for cp in via[0] + direct + via[1]:
            cp.start()
        for arrived, onward in zip(via, passed):
            for cp in arrived:
                cp.wait_recv()
            for cp in onward:
                cp.start()
        sent = direct + via[0] + via[1] + passed[0] + passed[1]
        for cp in direct + passed[0] + passed[1]:
            cp.wait_recv()
        for cp in sent:
            cp.wait_send()

    launch()
    return [o[...] for o in outs], [a[...] for a in s_refs]


def _chip_sums(place, mine, other, after, steps=2):
    nt, nchip = len(mine), mine[0].shape[0]
    m4 = [a.reshape(nchip, 2, -1, a.shape[-1]) for a in mine]
    o3 = [a.reshape(nchip, -1, a.shape[-1]) for a in other]

    def body(p_ref, *refs):
        del p_ref
        for a_ref, b_ref, o_ref in zip(refs[:nt], refs[nt:2 * nt], refs[2 * nt + 1:]):
            o_ref[...] = (a_ref[...].astype(f32) + b_ref[...].astype(f32)).astype(bf16)

    tiles = [(a.shape[1] // steps, a.shape[2]) for a in o3]
    blks = [pl.BlockSpec((None,) + t, lambda j, i, p_ref: (p_ref[1 + j], i, 0)) for t in tiles]
    outs = pl.pallas_call(
        body, name="chip_sums", out_shape=[SDS(a.shape, bf16) for a in o3], compiler_params=_cp("parallel", "parallel"),
        grid_spec=pltpu.PrefetchScalarGridSpec(
            num_scalar_prefetch=1, grid=(nchip - 1, steps),
            in_specs=[pl.BlockSpec((None, None) + t, lambda j, i, p_ref: (p_ref[1 + j], p_ref[0], i, 0)) for t in tiles] + blks + [ANY],
            out_specs=blks),
    )(place, *m4, *o3, after)
    return [o.reshape(a.shape) for o, a in zip(outs, other)]


def _all_reduce_rows(pack):
    rows = pack.shape[0]
    blk = rows // N_DEV

    def body(in_ref, out_ref, land, send1, recv1, send2, recv2):
        x, y, c = _place()
        me = 4 * x + 2 * y + c
        others = [(px, py, pc) for px in range(2) for py in range(2) for pc in range(2)]

        def is_me(p):
            return jnp.logical_and(jnp.logical_and(p[0] == x, p[1] == y), p[2] == c)

        land[me] = in_ref[pl.ds(_al(me, blk), blk), :]
        for d, p in enumerate(others):
            @pl.when(jnp.logical_not(is_me(p)))
            def _():
                pltpu.make_async_remote_copy(src_ref=in_ref.at[pl.ds(d * blk, blk), :], dst_ref=land.at[me], send_sem=send1.at[d],
                                             recv_sem=recv1.at[me], device_id=p, device_id_type=MESH).start()
        for d, p in enumerate(others):
            @pl.when(jnp.logical_not(is_me(p)))
            def _():
                cp = pltpu.make_async_remote_copy(src_ref=in_ref.at[pl.ds(d * blk, blk), :], dst_ref=land.at[d], send_sem=send1.at[d],
                                                  recv_sem=recv1.at[d], device_id=p, device_id_type=MESH)
                cp.wait_recv()
                cp.wait_send()
        total = land[0]
        for d in range(1, N_DEV):
            total = total + land[d]
        out_ref[pl.ds(_al(me, blk), blk), :] = total
        for d, p in enumerate(others):
            @pl.when(jnp.logical_not(is_me(p)))
            def _():
                mine = out_ref.at[pl.ds(_al(me, blk), blk), :]
                pltpu.make_async_remote_copy(src_ref=mine, dst_ref=mine, send_sem=send2.at[d], recv_sem=recv2.at[me],
                                             device_id=p, device_id_type=MESH).start()
        for d, p in enumerate(others):
            @pl.when(jnp.logical_not(is_me(p)))
            def _():
                theirs = out_ref.at[pl.ds(d * blk, blk), :]
                cp = pltpu.make_async_remote_copy(src_ref=theirs, dst_ref=theirs, send_sem=send2.at[d], recv_sem=recv2.at[d],
                                                  device_id=p, device_id_type=MESH)
                cp.wait_recv()
                cp.wait_send()

    vm = pl.BlockSpec(memory_space=pltpu.VMEM)
    return pl.pallas_call(
        body, name="all_reduce_rows", in_specs=[vm], out_specs=vm, out_shape=SDS((rows, LANE), f32),
        scratch_shapes=[pltpu.VMEM((N_DEV, blk, LANE), f32)] + [pltpu.SemaphoreType.DMA((N_DEV,))] * 4,
        compiler_params=pltpu.CompilerParams(vmem_limit_bytes=VMEM_LIMIT),
    )(pack)


def _lower_bounds_fwd(lower, after):
    def body(l_ref, *rest):
        o_ref = rest[-1]
        sm = _layer_softmax(l_ref)
        run = jnp.zeros_like(sm[0])
        for l in range(DEPTH):
            o_ref[l:l + 1, :] = run
            if l + 1 < DEPTH:
                run = run + sm[l + 1]

    vm = pl.BlockSpec(memory_space=pltpu.VMEM)
    return pl.pallas_call(body, name="lower_bounds_fwd", in_specs=[vm] + [ANY] * len(after), out_specs=vm,
                          out_shape=SDS(lower.shape, f32))(lower, *after)


def _layer_softmax(l_ref):
    rows = [l_ref[l:l + 1, :] for l in range(DEPTH)]
    top = functools.reduce(jnp.maximum, rows)
    e = [jnp.exp(r - top) for r in rows]
    tot = functools.reduce(lambda a, b: a + b, e)
    return [v / tot for v in e]


def _lower_bounds_bwd(lower, dlbs):
    def body(l_ref, d_ref, o_ref):
        sm = _layer_softmax(l_ref)
        dsm = [None] * DEPTH
        run = jnp.zeros_like(sm[0])
        dsm[0] = run
        for l in reversed(range(1, DEPTH)):
            run = run + d_ref[l:l + 1, :]
            dsm[l] = run
        inner = functools.reduce(lambda a, b: a + b, [sm[l] * dsm[l] for l in range(DEPTH)])
        for l in range(DEPTH):
            o_ref[l:l + 1, :] = sm[l] * (dsm[l] - inner)

    return pl.pallas_call(body, name="lower_bounds_bwd", out_shape=SDS(lower.shape, f32))(lower, dlbs)


_ADAM_C1 = 1.0 - ADAM_B1 ** ADAM_STEP
_ADAM_C2 = 1.0 - ADAM_B2 ** ADAM_STEP


def _adamw(w, g, m, v):
    m = ADAM_B1 * m + (1.0 - ADAM_B1) * g
    v = ADAM_B2 * v + (1.0 - ADAM_B2) * (g * g)
    delta = -ADAM_LR * ((m / _ADAM_C1) / (jnp.sqrt(v / _ADAM_C2) + ADAM_EPS) + ADAM_WD * w)
    return delta, m, v


def _adam_big(where, names, w, m, v, mine, other, landed, outs, after, steps=4):
    nt = len(names)
    three = lambda a: a.reshape(a.shape[0], -1, a.shape[-1])
    w3, m3, v3 = ([three(d[n]) for n in names] for d in (w, m, v))
    outs3 = [three(a) for n in names for a in outs[n]]
    mine4 = [a.reshape(a.shape[0], 2, -1, a.shape[-1]) for a in mine]
    other3 = [three(a) for a in other]
    land3 = [three(a) for a in landed]

    def body(where_ref, *refs):
        del where_ref
        o_refs = refs[6 * nt + 4 * nt + 1:]
        for t in range(nt):
            w_ref, m_ref, v_ref, mine_ref, other_ref, land_ref = (refs[q * nt + t] for q in range(6))
            g = mine_ref[...].astype(f32) + other_ref[...].astype(f32)
            for k in range(3):
                g = g + land_ref[k].astype(f32)
            delta, nm, nv = _adamw(w_ref[...], g, m_ref[...], v_ref[...])
            for o_ref, val in zip(o_refs[4 * t:4 * t + 4], (g, delta, nm, nv)):
                o_ref[...] = val

    tiles = [(a.shape[1] // steps, a.shape[2]) for a in w3]
    own = [pl.BlockSpec((None,) + t, lambda i, wh: (wh[0], i, 0)) for t in tiles]
    res = pl.pallas_call(
        body, name="adam_big", out_shape=[SDS(a.shape, f32) for a in outs3],
        input_output_aliases={1 + 6 * nt + i: i for i in range(4 * nt)}, compiler_params=_cp("parallel"),
        grid_spec=pltpu.PrefetchScalarGridSpec(
            num_scalar_prefetch=1, grid=(steps,),
            in_specs=own * 3 + [pl.BlockSpec((None, None) + t, lambda i, wh: (wh[1], wh[2], i, 0)) for t in tiles]
            + [pl.BlockSpec((None,) + t, lambda i, wh: (wh[1], i, 0)) for t in tiles]
            + [pl.BlockSpec((3,) + t, lambda i, wh: (0, i, 0)) for t in tiles] + [ANY] * (4 * nt + 1),
            out_specs=[s for s in own for _ in range(4)]),
    )(where, *w3, *m3, *v3, *mine4, *other3, *land3, *outs3, after)
    return {n: [o.reshape(w[n].shape) for o in res[4 * t:4 * t + 4]] for t, n in enumerate(names)}


def _touch(a, after):
    a2 = a.reshape(-1, a.shape[-1])

    def body(a_ref, after_ref, o_ref):
        del after_ref
        o_ref[...] = a_ref[0:8, :].astype(f32)

    return pl.pallas_call(
        body, name="touch", grid=(1,), in_specs=[pl.BlockSpec((16, LANE), lambda i: (0, 0)), ANY],
        out_specs=pl.BlockSpec((8, LANE), lambda i: (0, 0)), out_shape=SDS((8, LANE), f32),
    )(a2, after)


def _adam_rows(w, g, m, v):
    def body(w_ref, g_ref, m_ref, v_ref, d_ref, nm_ref, nv_ref):
        delta, nm, nv = _adamw(w_ref[...], g_ref[...], m_ref[...], v_ref[...])
        d_ref[...] = delta
        nm_ref[...] = nm
        nv_ref[...] = nv

    return pl.pallas_call(body, name="adam_rows", out_shape=[SDS(w.shape, f32)] * 3)(w, g, m, v)


SMALL = ("g_mix", "lower_bounds", "g_hgrn_out", "w_conv", "sg_ln_g", "sg_ln_b", "w_sg", "b_sg", "g_ffn", "g_final")
WEIGHTS = ("w_in", "g_mix", "lower_bounds", "g_hgrn_out", "w_conv", "sg_ln_g", "sg_ln_b", "w_sg", "b_sg", "w_branch", "w_o", "g_ffn",
           "w_ff1", "w_ff2", "g_final")


def _pack_rows(arrays, multiple):
    flat = jnp.concatenate([a.reshape(-1) for a in arrays])
    rows = -(-flat.shape[0] // (LANE * multiple)) * multiple
    return jnp.pad(flat, (0, rows * LANE - flat.shape[0])).reshape(rows, LANE)


def _unpack_rows(pack, like):
    flat = pack.reshape(-1)
    out, at = [], 0
    for a in like:
        out.append(flat[at:at + a.size].reshape(a.shape))
        at += a.size
    return out


def kernel(x, w_in, g_mix, lower_bounds, g_hgrn_out, w_conv, sg_ln_g, sg_ln_b, w_sg, b_sg, w_branch, w_o, g_ffn, w_ff1, w_ff2, g_final, loss_target, m_w_in, m_g_mix, m_lower_bounds, m_g_hgrn_out, m_w_conv, m_sg_ln_g, m_sg_ln_b, m_w_sg, m_b_sg, m_w_branch, m_w_o, m_g_ffn, m_w_ff1, m_w_ff2, m_g_final, v_w_in, v_g_mix, v_lower_bounds, v_g_hgrn_out, v_w_conv, v_sg_ln_g, v_sg_ln_b, v_w_sg, v_b_sg, v_w_branch, v_w_o, v_g_ffn, v_w_ff1, v_w_ff2, v_g_final):
    weights = dict(w_in=w_in, g_mix=g_mix, lower_bounds=lower_bounds, g_hgrn_out=g_hgrn_out, w_conv=w_conv, sg_ln_g=sg_ln_g,
                   sg_ln_b=sg_ln_b, w_sg=w_sg, b_sg=b_sg, w_branch=w_branch, w_o=w_o, g_ffn=g_ffn, w_ff1=w_ff1, w_ff2=w_ff2, g_final=g_final)
    mom1 = dict(w_in=m_w_in, g_mix=m_g_mix, lower_bounds=m_lower_bounds, g_hgrn_out=m_g_hgrn_out, w_conv=m_w_conv, sg_ln_g=m_sg_ln_g,
                sg_ln_b=m_sg_ln_b, w_sg=m_w_sg, b_sg=m_b_sg, w_branch=m_w_branch, w_o=m_w_o, g_ffn=m_g_ffn, w_ff1=m_w_ff1, w_ff2=m_w_ff2,
                g_final=m_g_final)
    mom2 = dict(w_in=v_w_in, g_mix=v_g_mix, lower_bounds=v_lower_bounds, g_hgrn_out=v_g_hgrn_out, w_conv=v_w_conv, sg_ln_g=v_sg_ln_g,
                sg_ln_b=v_sg_ln_b, w_sg=v_w_sg, b_sg=v_b_sg, w_branch=v_w_branch, w_o=v_w_o, g_ffn=v_g_ffn, w_ff1=v_w_ff1, w_ff2=v_w_ff2,
                g_final=v_g_final)
    xi, yi, ci = _place()
    dev = 4 * xi + 2 * yi + ci
    conv_cols = w_conv.shape[-1]

    for d in (weights, mom1, mom2):
        d["w_in"] = jnp.swapaxes(d["w_in"], 1, 2)
    shards = {n: weights[n].astype(bf16) for n in BIG}

    conv_place = lax.dynamic_update_slice(jnp.zeros((DEPTH, 3, BRANCH), f32), w_conv, (0, 0, dev * conv_cols))
    (w_conv_full,) = _unpack_rows(_all_reduce_rows(_pack_rows([conv_place], 8 * N_DEV)), [conv_place])
    small_packs = [_pack_rows([d[n] for n in SMALL], 8) for d in (weights, mom1, mom2)]
    lbs = _lower_bounds_fwd(lower_bounds, small_packs)

    def small_of(l):
        return dict(g_mix=g_mix[l][None], lb=lbs[l][None], g_out=g_hgrn_out[l][None], w_conv=w_conv_full[l], ln_g=sg_ln_g[l][None],
                    ln_b=sg_ln_b[l][None], w_sg=w_sg[l], b_sg_t=b_sg[l].T, g_ffn=g_ffn[l][None])

    act = x[0]
    full, saved = [], []
    shard_refs = [jax.new_ref(shards[n], memory_space=pltpu.MemorySpace.HBM) for n in BIG]
    shapes = _gather_out_shapes(shards)
    groups = [g for l in range(DEPTH) for g in ((l, (0,), 1, "a"), (l, (1, 2, 3, 4), 2, "b"))]
    arrived = {}
    for l, which, n_early, tag in groups:
        got = _seq_all_gather_layer(l, which, n_early, [shard_refs[t] for t in which], [shapes[t] for t in which], tag, place_mine=l > 0)
        arrived.update({(l, BIG[t]): (which, got) for t in which})

    for l in range(DEPTH):
        full.append({})

        def weight(name, after, l=l):
            if name not in full[l]:
                which, got = arrived[(l, name)]
                where = jnp.stack([jnp.int32(l), dev.astype(jnp.int32)])
                if l > 0:
                    full[l].update(zip([BIG[t] for t in which], got))
                else:
                    full[l].update(zip([BIG[t] for t in which], _place_own(where, which, [shards[BIG[t]] for t in which], got, after)))
            return full[l][name]

        act, sv = _layer_fwd(act, weight, small_of(l))
        saved.append(sv)
    loss_row, dx, dxb, dg_final = _final(act, loss_target[0], g_final[None])

    chip, core = (2 * xi + yi).astype(jnp.int32), ci.astype(jnp.int32)
    place = jnp.stack([core] + [(chip + k) % (N_DEV // 2) for k in (1, 2, 3)])
    big_out = {n: [lax.empty(weights[n].shape, f32) for _ in range(4)] for n in BIG}
    small_grads = [None] * DEPTH

    def chip_sums(stage, after):
        l, received, mine = stage
        sums = _chip_sums(place, mine, received, after)
        placed.append(sums[BIG.index("w_o")])
        landed, _ = _seq_exchange_between_chips(sums)
        return l, mine, received, landed

    def adam_layer(stage, after):
        l, mine, received, landed = stage
        where = jnp.stack([jnp.int32(l), chip, core])
        big_out.update(_adam_big(where, BIG, weights, mom1, mom2, mine, received, landed, big_out, after))

    above = None
    placed = []
    for l in reversed(range(DEPTH)):
        summed = []

        def between(dx1):
            if above is None:
                return dx1
            summed.append(chip_sums(above, dx1))
            return placed[-1]

        def before_end(big):
            return _touch(summed[0][3][BIG.index("w_o")], big["w_in"]) if summed else big["w_in"]

        dx, dxb, big, small_grads[l] = _layer_bwd(dx, dxb, saved[l], full[l], small_of(l), between, before_end)
        if summed:
            adam_layer(summed[0], dx)
        above = (l, *_seq_exchange_on_chip([big[n] for n in BIG]))

    stack = lambda f: jnp.stack([f(small_grads[l]) for l in range(DEPTH)])
    d_lower = _lower_bounds_bwd(lower_bounds, stack(lambda s: s["vecs"][0]))
    local_small = dict(g_mix=stack(lambda s: s["g_mix"][0]), lower_bounds=d_lower, g_hgrn_out=stack(lambda s: s["vecs"][1]),
                       w_conv=stack(lambda s: s["vecs"][4:7]), sg_ln_g=stack(lambda s: s["vecs"][2]), sg_ln_b=stack(lambda s: s["vecs"][3]),
                       w_sg=stack(lambda s: s["w_sg"]), b_sg=stack(lambda s: s["b_sg_t"].T), g_ffn=stack(lambda s: s["g_ffn"][0]),
                       g_final=dg_final[0])
    order = [local_small[n] for n in SMALL] + [loss_row]
    *reduced, loss_sum = _unpack_rows(_all_reduce_rows(_pack_rows(order, 8 * N_DEV)), order)
    loss = loss_sum[0, 0]
    grads = dict(zip(SMALL, reduced))
    grads["w_conv"] = lax.dynamic_slice(grads["w_conv"], (0, 0, dev * conv_cols), (DEPTH, 3, conv_cols))

    deltas, new_m, new_v = {}, {}, {}
    like = [weights[n] for n in SMALL]
    small_out = _adam_rows(small_packs[0], _pack_rows([grads[n] for n in SMALL], 8), small_packs[1], small_packs[2])
    for out, pack in zip((deltas, new_m, new_v), small_out):
        out.update(zip(SMALL, _unpack_rows(pack, like)))
    adam_layer(chip_sums(above, dx), small_out[0])
    for n in BIG:
        grads[n], deltas[n], new_m[n], new_v[n] = (jnp.swapaxes(a, 1, 2) if n == "w_in" else a for a in big_out[n])

    return (loss, dx[None], *[grads[n] for n in WEIGHTS], *[deltas[n] for n in WEIGHTS], *[new_m[n] for n in WEIGHTS],
            *[new_v[n] for n in WEIGHTS])
```

```python
import functools

import jax
import jax.numpy as jnp
from jax import lax
from jax.experimental import pallas as pl
from jax.experimental.pallas import tpu as pltpu
from jax.experimental.pallas import tpu_sc as plsc

f32 = jnp.float32
bf16 = jnp.bfloat16
SDS = jax.ShapeDtypeStruct
MESH = pl.DeviceIdType.MESH

D_MODEL = 1024
BRANCH = 512
N_COLS = 7680
D_FF = 4096
DEPTH = 4
HEADS = 4
HEAD_DIM = 128
HGRN_CHUNK = 64
SG_CHUNK = 128
SG_GROUPS = 4
NORM_EPS = 1e-6
LN_EPS = 1e-5
LB_FLOOR = 1e-30
N_DEV = 8
SHARD_IN = N_COLS // N_DEV
LANE = 128
GATE_COL0 = 9 * BRANCH

ADAM_LR = 0.001
ADAM_B1 = 0.9
ADAM_B2 = 0.999
ADAM_EPS = 1e-08
ADAM_WD = 0.01
ADAM_STEP = 10

MIX_TILE = 256
VMEM_LIMIT = 56 * 1024 * 1024


def _cp(*sem):
    return pltpu.CompilerParams(dimension_semantics=sem or None, vmem_limit_bytes=VMEM_LIMIT)


def _dot(a, b):
    return jnp.dot(a, b, preferred_element_type=f32)


def _dot_nt(a, b):
    return lax.dot_general(a, b, (((1,), (1,)), ((), ())), preferred_element_type=f32)


def _dot_tn(a, b):
    return lax.dot_general(a, b, (((0,), (0,)), ((), ())), preferred_element_type=f32)


def _dot_exact(ones, b):
    hi = b.astype(bf16)
    rest = b - hi.astype(f32)
    mid = rest.astype(bf16)
    low = (rest - mid.astype(f32)).astype(bf16)
    ones = ones.astype(bf16)
    return _dot(ones, hi) + _dot(ones, mid) + _dot(ones, low)


def _sigmoid(x):
    return jax.nn.sigmoid(x)


_GELU_C = 0.7978845608028654
_GELU_A = 0.044715


def _gelu(x):
    return 0.5 * x * (1.0 + jnp.tanh(_GELU_C * (x + _GELU_A * x * x * x)))


def _gelu_grad(x):
    x2 = x * x
    t = jnp.tanh(_GELU_C * (x + _GELU_A * x * x2))
    return 0.5 * (1.0 + t) + 0.5 * x * (1.0 - t * t) * _GELU_C * (1.0 + 3.0 * _GELU_A * x2)


def _rms_stats(x):
    r = lax.rsqrt(jnp.mean(x * x, axis=-1, keepdims=True) + NORM_EPS)
    return r, x * r


def _rms_bwd(dh, xh, r, g):
    dg = jnp.sum(dh * xh, axis=0, keepdims=True)
    dxn = dh * g
    dx = r * (dxn - xh * jnp.mean(dxn * xh, axis=-1, keepdims=True))
    return dx, dg


def _tri(n, upper=False):
    r = lax.broadcasted_iota(jnp.int32, (n, n), 0)
    c = lax.broadcasted_iota(jnp.int32, (n, n), 1)
    return (c >= r) if upper else (c <= r)


def _acc_rows(ref, first, val):
    @pl.when(first)
    def _():
        ref[...] = val

    @pl.when(jnp.logical_not(first))
    def _():
        ref[...] += val


def _rms_mm(x, g, w_t, tm=1024, tn=1536):
    s, n = x.shape[0], w_t.shape[0]
    jm = GATE_COL0 // tn

    def body(x_ref, g_ref, w_ref, pm_ref, pg_ref, h_ref, hs):
        j = pl.program_id(1)

        @pl.when(j == 0)
        def _():
            _, xh = _rms_stats(x_ref[...])
            hv = (xh * g_ref[...]).astype(bf16)
            hs[...] = hv
            h_ref[...] = hv

        res = _dot_nt(hs[...], w_ref[...])

        @pl.when(j < jm)
        def _():
            pm_ref[...] = res

        @pl.when(j >= jm)
        def _():
            pg_ref[...] = res.astype(bf16)

    return pl.pallas_call(
        body, name="rms_mm", grid=(s // tm, n // tn),
        in_specs=[pl.BlockSpec((tm, D_MODEL), lambda i, j: (i, 0)), pl.BlockSpec((1, D_MODEL), lambda i, j: (0, 0)),
                  pl.BlockSpec((tn, D_MODEL), lambda i, j: (j, 0))],
        out_specs=[pl.BlockSpec((tm, tn), lambda i, j: (i, jnp.minimum(j, jm - 1))),
                   pl.BlockSpec((tm, tn), lambda i, j: (i, jnp.maximum(j - jm, 0))), pl.BlockSpec((tm, D_MODEL), lambda i, j: (i, 0))],
        out_shape=[SDS((s, GATE_COL0), f32), SDS((s, n - GATE_COL0), bf16), SDS((s, D_MODEL), bf16)],
        scratch_shapes=[pltpu.VMEM((tm, D_MODEL), bf16)], compiler_params=_cp("parallel", "arbitrary"),
    )(x, g, w_t)


def _hgrn_gates(fp, lb):
    logf = jnp.logaddexp(jnp.log(jnp.maximum(lb, LB_FLOOR)), jnp.log1p(-lb) + jax.nn.log_sigmoid(fp))
    snf = _sigmoid(-fp)
    return logf, snf, (1.0 - lb) * snf


def _p_specs(tile, cols, row_map):
    return [pl.BlockSpec((tile, BRANCH), functools.partial(lambda c, i: (row_map(i), c), c)) for c in cols]


def _mixer_fwd(p, lb, gout, wconv, lng, lnb, wsg, bsg_t):
    s = p.shape[0]
    tt = MIX_TILE
    nch = tt // HGRN_CHUNK

    def body(q_ref, fp_ref, iv_ref, go_ref, bg_ref, cg_ref, xc_ref, u_ref, v_ref, lb_ref, gout_ref, wconv_ref, lng_ref,
             lnb_ref, wsg_ref, bsg_ref, z_ref, opre_ref, st_ref, st_scr, zbuf):
        @pl.when(pl.program_id(0) == 0)
        def _():
            st_scr[...] = jnp.zeros_like(st_scr)
            zbuf[0:8, :] = jnp.zeros((8, BRANCH), f32)

        lbv = lb_ref[...]
        gout_v = gout_ref[...]
        causal = _tri(HGRN_CHUNK)
        tri = causal.astype(f32)
        last_row = lax.broadcasted_iota(jnp.int32, (HGRN_CHUNK, 1), 0) == HGRN_CHUNK - 1
        for c in range(nch):
            rows = slice(HGRN_CHUNK * c, HGRN_CHUNK * (c + 1))
            q_raw = q_ref[rows, :]
            qs = q_raw * _sigmoid(q_raw)
            logf, _, kk = _hgrn_gates(fp_ref[rows, :], lbv)
            b = _dot_exact(tri, logf)
            bl = jnp.sum(jnp.where(last_row, b, 0.0), axis=0, keepdims=True)
            qb = (qs * jnp.exp(b)).astype(bf16)
            kb = (kk * jnp.exp(-b)).astype(bf16)
            kd = (kk * jnp.exp(bl - b)).astype(bf16)
            ebl = jnp.exp(bl)
            vc = iv_ref[rows, :].astype(bf16)
            gate = _sigmoid(go_ref[rows, :])
            for h in range(HEADS):
                sl = slice(HEAD_DIM * h, HEAD_DIM * (h + 1))
                st = st_scr[h]
                st_ref[c, h] = st
                a = jnp.where(causal, _dot_nt(qb[:, sl], kb[:, sl]), 0.0)
                o = _dot(a.astype(bf16), vc[:, sl]) + _dot_nt(qb[:, sl], st.astype(bf16))
                opre_ref[rows, sl] = o
                st_scr[h] = st * ebl[:, sl] + _dot_tn(vc[:, sl], kd[:, sl])
                _, oh = _rms_stats(o)
                z_ref[rows, sl] = (oh * gout_v[:, sl] * gate[:, sl]).astype(bf16)

        zc = cg_ref[...] * xc_ref[...]
        zbuf[8:8 + tt, :] = zc
        y = wconv_ref[0:1, :] * zbuf[pl.ds(6, tt), :] + wconv_ref[1:2, :] * zbuf[pl.ds(7, tt), :] + wconv_ref[2:3, :] * zc
        z_ref[:, BRANCH:2 * BRANCH] = (bg_ref[...] * y).astype(bf16)
        zbuf[0:8, :] = zbuf[tt:tt + 8, :]

        lng_v, lnb_v = lng_ref[...], lnb_ref[...]
        low = _tri(SG_CHUNK)
        wms = [jnp.where(low, wsg_ref[g], 0.0).astype(bf16) for g in range(SG_GROUPS)]
        for cc in range(tt // SG_CHUNK):
            rows = slice(SG_CHUNK * cc, SG_CHUNK * (cc + 1))
            ug = _gelu(u_ref[rows, :])
            vg = _gelu(v_ref[rows, :])
            vcen = vg - jnp.mean(vg, axis=-1, keepdims=True)
            rstd = lax.rsqrt(jnp.mean(vcen * vcen, axis=-1, keepdims=True) + LN_EPS)
            vn = (vcen * rstd * lng_v + lnb_v).astype(bf16)
            for g in range(SG_GROUPS):
                sl = slice(LANE * g, LANE * (g + 1))
                sv = _dot(wms[g], vn[:, sl]) + bsg_ref[:, g:g + 1]
                z_ref[rows, 2 * BRANCH + LANE * g:2 * BRANCH + LANE * (g + 1)] = (ug[:, sl] * sv).astype(bf16)

    full = lambda shape: pl.BlockSpec(shape, lambda i: (0,) * len(shape))
    return pl.pallas_call(
        body, name="mixer_fwd", grid=(s // tt,),
        in_specs=_p_specs(tt, range(9), lambda i: i) + [full((1, BRANCH)), full((1, BRANCH)), full((3, BRANCH)), full((1, BRANCH)),
                                                        full((1, BRANCH)), full((SG_GROUPS, SG_CHUNK, SG_CHUNK)), full((SG_CHUNK, SG_GROUPS))],
        out_specs=[pl.BlockSpec((tt, 3 * BRANCH), lambda i: (i, 0)), pl.BlockSpec((tt, BRANCH), lambda i: (i, 0)),
                   pl.BlockSpec((nch, HEADS, HEAD_DIM, HEAD_DIM), lambda i: (i, 0, 0, 0))],
        out_shape=[SDS((s, 3 * BRANCH), bf16), SDS((s, BRANCH), f32), SDS((s // HGRN_CHUNK, HEADS, HEAD_DIM, HEAD_DIM), f32)],
        scratch_shapes=[pltpu.VMEM((HEADS, HEAD_DIM, HEAD_DIM), f32), pltpu.VMEM((tt + 8, BRANCH), f32)],
        compiler_params=_cp("arbitrary"),
    )(*([p] * 9), lb, gout, wconv, lng, lnb, wsg, bsg_t)


def _branch_gate(z, wb, pg, x, wo, tm=512):
    s = z.shape[0]

    def body(z_ref, wb_ref, g_ref, x_ref, wo_ref, y_ref, m_ref, x1_ref):
        acc = None
        for n in range(3):
            cols = slice(D_MODEL * n, D_MODEL * (n + 1))
            yn = _dot(z_ref[:, BRANCH * n:BRANCH * (n + 1)], wb_ref[n])
            y_ref[:, cols] = yn.astype(bf16)
            t = _sigmoid(g_ref[:, cols].astype(f32)) * yn
            acc = t if acc is None else acc + t
        merged = acc.astype(bf16)
        m_ref[...] = merged
        x1_ref[...] = x_ref[...] + _dot(merged, wo_ref[...])

    row = pl.BlockSpec((tm, D_MODEL), lambda i: (i, 0))
    wide = pl.BlockSpec((tm, 3 * D_MODEL), lambda i: (i, 0))
    return pl.pallas_call(
        body, name="branch_gate", grid=(s // tm,),
        in_specs=[pl.BlockSpec((tm, 3 * BRANCH), lambda i: (i, 0)), pl.BlockSpec((3, BRANCH, D_MODEL), lambda i: (0, 0, 0)), wide, row,
                  pl.BlockSpec((D_MODEL, D_MODEL), lambda i: (0, 0))],
        out_specs=[wide, row, row],
        out_shape=[SDS((s, 3 * D_MODEL), bf16), SDS((s, D_MODEL), bf16), SDS((s, D_MODEL), f32)], compiler_params=_cp("parallel"),
    )(z, wb, pg, x, wo)


def _ffn(x1, g, w1, w2, tm=1024, tf=1024):
    s = x1.shape[0]
    nf = D_FF // tf

    def body(x_ref, g_ref, w1_ref, w2_ref, o_ref, h_ref, ra_ref, hs, acc):
        f = pl.program_id(1)

        @pl.when(f == 0)
        def _():
            _, xh = _rms_stats(x_ref[...])
            hv = (xh * g_ref[...]).astype(bf16)
            hs[...] = hv
            h_ref[...] = hv
            acc[...] = jnp.zeros_like(acc)

        ra = jnp.maximum(_dot(hs[...], w1_ref[...]), 0.0)
        ra_ref[...] = ra.astype(bf16)
        acc[...] += _dot((ra * ra).astype(bf16), w2_ref[...])

        @pl.when(f == nf - 1)
        def _():
            o_ref[...] = x_ref[...] + acc[...]

    return pl.pallas_call(
        body, name="ffn", grid=(s // tm, nf),
        in_specs=[pl.BlockSpec((tm, D_MODEL), lambda i, f: (i, 0)), pl.BlockSpec((1, D_MODEL), lambda i, f: (0, 0)),
                  pl.BlockSpec((D_MODEL, tf), lambda i, f: (0, f)), pl.BlockSpec((tf, D_MODEL), lambda i, f: (f, 0))],
        out_specs=[pl.BlockSpec((tm, D_MODEL), lambda i, f: (i, 0)), pl.BlockSpec((tm, D_MODEL), lambda i, f: (i, 0)),
                   pl.BlockSpec((tm, tf), lambda i, f: (i, f))],
        out_shape=[SDS((s, D_MODEL), f32), SDS((s, D_MODEL), bf16), SDS((s, D_FF), bf16)],
        scratch_shapes=[pltpu.VMEM((tm, D_MODEL), bf16), pltpu.VMEM((tm, D_MODEL), f32)], compiler_params=_cp("parallel", "arbitrary"),
    )(x1, g, w1, w2)


def _final(x, target, g, tm=512):
    s = x.shape[0]

    def body(x_ref, t_ref, g_ref, loss_ref, dx_ref, dxb_ref, dg_ref):
        first = pl.program_id(0) == 0
        gv = g_ref[...]
        r, xh = _rms_stats(x_ref[...])
        e = xh * gv - t_ref[...]
        tile_loss = 0.5 * jnp.sum(jnp.mean(e * e, axis=-1, keepdims=True), axis=0, keepdims=True)
        dx, dg = _rms_bwd(e * (1.0 / D_MODEL), xh, r, gv)
        dx_ref[...] = dx
        dxb_ref[...] = dx.astype(bf16)
        _acc_rows(dg_ref, first, dg)
        _acc_rows(loss_ref, first, jnp.broadcast_to(tile_loss, (1, LANE)))

    row = pl.BlockSpec((tm, D_MODEL), lambda i: (i, 0))
    return pl.pallas_call(
        body, name="final_loss", grid=(s // tm,), in_specs=[row, row, pl.BlockSpec((1, D_MODEL), lambda i: (0, 0))],
        out_specs=[pl.BlockSpec((1, LANE), lambda i: (0, 0)), row, row, pl.BlockSpec((1, D_MODEL), lambda i: (0, 0))],
        out_shape=[SDS((1, LANE), f32), SDS((s, D_MODEL), f32), SDS((s, D_MODEL), bf16), SDS((1, D_MODEL), f32)],
        compiler_params=_cp("arbitrary"),
    )(x, target, g)


def _ffn_bwd(dx2, dx2b, x1, g, ra, w1, w2, tm=512, tf=2048):
    s = x1.shape[0]
    nf = D_FF // tf

    def body(dx_ref, dxb_ref, x_ref, g_ref, ra_ref, w1_ref, w2_ref, da_ref, dx1_ref, dx1b_ref, dg_ref, acc):
        i, f = pl.program_id(0), pl.program_id(1)

        @pl.when(f == 0)
        def _():
            acc[...] = jnp.zeros_like(acc)

        da = (_dot_nt(dxb_ref[...], w2_ref[...]) * (2.0 * ra_ref[...].astype(f32))).astype(bf16)
        da_ref[...] = da
        acc[...] += _dot_nt(da, w1_ref[...])

        @pl.when(f == nf - 1)
        def _():
            r, xh = _rms_stats(x_ref[...])
            dx, dg = _rms_bwd(acc[...], xh, r, g_ref[...])
            dx = dx + dx_ref[...]
            dx1_ref[...] = dx
            dx1b_ref[...] = dx.astype(bf16)
            _acc_rows(dg_ref, i == 0, dg)

    row = pl.BlockSpec((tm, D_MODEL), lambda i, f: (i, 0))
    col = pl.BlockSpec((tm, tf), lambda i, f: (i, f))
    return pl.pallas_call(
        body, name="ffn_bwd", grid=(s // tm, nf),
        in_specs=[row, row, row, pl.BlockSpec((1, D_MODEL), lambda i, f: (0, 0)), col,
                  pl.BlockSpec((D_MODEL, tf), lambda i, f: (0, f)), pl.BlockSpec((tf, D_MODEL), lambda i, f: (f, 0))],
        out_specs=[col, row, row, pl.BlockSpec((1, D_MODEL), lambda i, f: (0, 0))],
        out_shape=[SDS((s, D_FF), bf16), SDS((s, D_MODEL), f32), SDS((s, D_MODEL), bf16), SDS((1, D_MODEL), f32)],
        scratch_shapes=[pltpu.VMEM((tm, D_MODEL), f32)], compiler_params=_cp("arbitrary", "arbitrary"),
    )(dx2, dx2b, x1, g, ra, w1, w2)


def _mm_tn(a, b, nb, m, n, tm, tn, name="mm_tn", square_a=False):
    s = a.shape[0]
    mi, nj = m // tm, n // tn

    def body(a_ref, b_ref, o_ref):
        av = a_ref[...]
        if square_a:
            av = av.astype(f32)
            av = (av * av).astype(bf16)
        o_ref[...] = _dot_tn(av, b_ref[...]).astype(bf16)

    return pl.pallas_call(
        body, name=name, grid=(nb, mi, nj),
        in_specs=[pl.BlockSpec((s, tm), lambda k, i, j: (0, k * mi + i)), pl.BlockSpec((s, tn), lambda k, i, j: (0, k * nj + j))],
        out_specs=pl.BlockSpec((None, tm, tn), lambda k, i, j: (k, i, j)), out_shape=SDS((nb, m, n), bf16),
        compiler_params=_cp("parallel", "parallel", "parallel"),
    )(a, b)


def _dw_in(dpm, dpg, h, tm=768):
    s = h.shape[0]
    km, kg = dpm.shape[1] // tm, dpg.shape[1] // tm

    def body(am_ref, ag_ref, h_ref, o_ref):
        i = pl.program_id(0)

        @pl.when(i < km)
        def _():
            o_ref[...] = _dot_tn(am_ref[...], h_ref[...]).astype(bf16)

        @pl.when(i >= km)
        def _():
            o_ref[...] = _dot_tn(ag_ref[...], h_ref[...]).astype(bf16)

    return pl.pallas_call(
        body, name="dw_in", grid=(km + kg,),
        in_specs=[pl.BlockSpec((s, tm), lambda i: (0, jnp.minimum(i, km - 1))), pl.BlockSpec((s, tm), lambda i: (0, jnp.maximum(i - km, 0))),
                  pl.BlockSpec((s, D_MODEL), lambda i: (0, 0))],
        out_specs=pl.BlockSpec((tm, D_MODEL), lambda i: (i, 0)), out_shape=SDS((km * tm + kg * tm, D_MODEL), bf16),
        compiler_params=_cp("arbitrary"),
    )(dpm, dpg, h)


def _mm_tn_slabs(a, b, nb, m, nblk, rel, width, tm=512, name="mm_tn_slabs"):
    s = a.shape[0]
    n = b.shape[1] // nb
    ng, mi, nw = n // nblk, m // tm, len(rel)

    def body(a_ref, b_ref, o_ref):
        full = _dot_tn(a_ref[...], b_ref[...])
        for r, start in enumerate(rel):
            o_ref[r] = full[:, start:start + width].astype(bf16)

    return pl.pallas_call(
        body, name=name, grid=(nb, ng, mi),
        in_specs=[pl.BlockSpec((s, tm), lambda k, g, i: (0, k * mi + i)), pl.BlockSpec((s, nblk), lambda k, g, i: (0, k * ng + g))],
        out_specs=pl.BlockSpec((nw, None, tm, width), lambda k, g, i: (g, k, i, 0)), out_shape=SDS((ng * nw, nb, m, width), bf16),
        compiler_params=_cp("parallel", "parallel", "parallel"),
    )(a, b)


def _merge_bwd(dx1b, wo, y, pg, wb, after, tm=512):
    s = dx1b.shape[0]

    def body(dx_ref, wo_ref, y_ref, g_ref, wb_ref, after_ref, dy_ref, dg_ref, dz_ref):
        del after_ref
        dm = _dot_nt(dx_ref[...], wo_ref[...])
        for n in range(3):
            cols = slice(D_MODEL * n, D_MODEL * (n + 1))
            gate = _sigmoid(g_ref[:, cols].astype(f32))
            t = dm * gate
            dy = t.astype(bf16)
            dy_ref[:, cols] = dy
            dg_ref[:, cols] = (t * y_ref[:, cols].astype(f32) * (1.0 - gate)).astype(bf16)
            dz_ref[:, BRANCH * n:BRANCH * (n + 1)] = _dot_nt(dy, wb_ref[n]).astype(bf16)

    wide = pl.BlockSpec((tm, 3 * D_MODEL), lambda i: (i, 0))
    return pl.pallas_call(
        body, name="merge_bwd", grid=(s // tm,),
        in_specs=[pl.BlockSpec((tm, D_MODEL), lambda i: (i, 0)), pl.BlockSpec((D_MODEL, D_MODEL), lambda i: (0, 0)), wide, wide,
                  pl.BlockSpec((3, BRANCH, D_MODEL), lambda i: (0, 0, 0)), pl.BlockSpec(memory_space=pl.ANY)],
        out_specs=[wide, wide, pl.BlockSpec((tm, 3 * BRANCH), lambda i: (i, 0))],
        out_shape=[SDS((s, 3 * D_MODEL), bf16), SDS((s, 3 * D_MODEL), bf16), SDS((s, 3 * BRANCH), bf16)],
        compiler_params=_cp("parallel"),
    )(dx1b, wo, y, pg, wb, after)


def _mixer_bwd(p, dz, opre, states, lb, gout, wconv, lng, lnb, wsg, bsg_t):
    s = p.shape[0]
    tt = MIX_TILE
    nt = s // tt
    nch = tt // HGRN_CHUNK
    rev = lambda i: nt - 1 - i

    def body(q_ref, fp_ref, iv_ref, go_ref, bg_ref, cg_ref, xc_ref, u_ref, v_ref, cgp_ref, xcp_ref, dz_ref, opre_ref, st_ref,
             lb_ref, gout_ref, wconv_ref, lng_ref, lnb_ref, wsg_ref, bsg_ref,
             dp_ref, vec_ref, dwsg_ref, dbsg_ref, dst_scr, zbuf, dybuf, dbsg_acc):
        i = pl.program_id(0)

        @pl.when(i == 0)
        def _():
            dst_scr[...] = jnp.zeros_like(dst_scr)
            dybuf[tt:tt + 8, :] = jnp.zeros((8, BRANCH), f32)
            vec_ref[...] = jnp.zeros_like(vec_ref)
            dwsg_ref[...] = jnp.zeros_like(dwsg_ref)
            dbsg_acc[...] = jnp.zeros_like(dbsg_acc)

        lbv = lb_ref[...]
        gout_v = gout_ref[...]
        causal = _tri(HGRN_CHUNK)
        tri = causal.astype(f32)
        tri_up = _tri(HGRN_CHUNK, upper=True).astype(f32)
        last_row = lax.broadcasted_iota(jnp.int32, (HGRN_CHUNK, 1), 0) == HGRN_CHUNK - 1
        lb_live = (lbv > LB_FLOOR).astype(f32)
        dlb = jnp.zeros((1, BRANCH), f32)
        dgout = jnp.zeros((1, BRANCH), f32)
        for c in reversed(range(nch)):
            rows = slice(HGRN_CHUNK * c, HGRN_CHUNK * (c + 1))
            q_c, fp = q_ref[rows, :], fp_ref[rows, :]
            sq_c = _sigmoid(q_c)
            sfp_c = _sigmoid(fp)
            logf, snf_c, kk = _hgrn_gates(fp, lbv)
            invf_c = jnp.exp(-logf)
            doa = dz_ref[rows, 0:BRANCH].astype(f32)
            o = opre_ref[rows, :]
            sgo = _sigmoid(go_ref[rows, :])
            d_o, dgo, dg_c = [], [], []
            for h in range(HEADS):
                sl = slice(HEAD_DIM * h, HEAD_DIM * (h + 1))
                r, oh = _rms_stats(o[:, sl])
                dgo.append(doa[:, sl] * oh * gout_v[:, sl] * sgo[:, sl] * (1.0 - sgo[:, sl]))
                dx, dg = _rms_bwd(doa[:, sl] * sgo[:, sl], oh, r, gout_v[:, sl])
                d_o.append(dx)
                dg_c.append(dg)
            dp_ref[rows, 3 * BRANCH:4 * BRANCH] = jnp.concatenate(dgo, axis=1).astype(bf16)
            dgout = dgout + jnp.concatenate(dg_c, axis=1)
            dob = jnp.concatenate(d_o, axis=1).astype(bf16)
            b = _dot_exact(tri, logf)
            bl = jnp.sum(jnp.where(last_row, b, 0.0), axis=0, keepdims=True)
            eb, enb, edl, ebl = jnp.exp(b), jnp.exp(-b), jnp.exp(bl - b), jnp.exp(bl)
            qbf, kbf, kdf = q_c * sq_c * eb, kk * enb, kk * edl
            qb, kb, kd = qbf.astype(bf16), kbf.astype(bf16), kdf.astype(bf16)
            vc = iv_ref[rows, :].astype(bf16)
            dv, dqb, dkb, dkd, debl = [], [], [], [], []
            for h in range(HEADS):
                sl = slice(HEAD_DIM * h, HEAD_DIM * (h + 1))
                st = st_ref[c, h]
                dst = dst_scr[h]
                stb, dstb = st.astype(bf16), dst.astype(bf16)
                a = jnp.where(causal, _dot_nt(qb[:, sl], kb[:, sl]), 0.0).astype(bf16)
                da = jnp.where(causal, _dot_nt(dob[:, sl], vc[:, sl]), 0.0).astype(bf16)
                dv.append(_dot_tn(a, dob[:, sl]) + _dot_nt(kd[:, sl], dstb))
                dqb.append(_dot(dob[:, sl], stb) + _dot(da, kb[:, sl]))
                dkb.append(_dot_tn(da, qb[:, sl]))
                dkd.append(_dot(vc[:, sl], dstb))
                debl.append(jnp.sum(st * dst, axis=0, keepdims=True))
                dst_scr[h] = _dot_tn(dob[:, sl], qb[:, sl]) + dst * ebl[:, sl]
            dv, dqb, dkb, dkd = (jnp.concatenate(t, axis=1) for t in (dv, dqb, dkb, dkd))
            debl = jnp.concatenate(debl, axis=1)
            t_kd = dkd * kdf
            dbl = ebl * debl + jnp.sum(t_kd, axis=0, keepdims=True)
            db = dqb * qbf - dkb * kbf - t_kd + jnp.where(last_row, dbl, 0.0)
            dkk = dkb * enb + dkd * edl
            dlc = _dot_exact(tri_up, db)
            slope = (1.0 - lbv) * sfp_c * snf_c
            dp_ref[rows, 0:BRANCH] = (dqb * eb * sq_c * (1.0 + q_c * (1.0 - sq_c))).astype(bf16)
            dp_ref[rows, BRANCH:2 * BRANCH] = (slope * (dlc * invf_c - dkk)).astype(bf16)
            dp_ref[rows, 2 * BRANCH:3 * BRANCH] = dv.astype(bf16)
            dlb = dlb + jnp.sum(dlc * (lb_live - sfp_c) * invf_c - dkk * snf_c, axis=0, keepdims=True)
        vec_ref[0:1, :] += dlb
        vec_ref[1:2, :] += dgout

        dob_ = dz_ref[:, BRANCH:2 * BRANCH].astype(f32)
        bg, cg, xc = bg_ref[...], cg_ref[...], xc_ref[...]
        zc = cg * xc
        zbuf[0:8, :] = jnp.where(i < nt - 1, cgp_ref[...] * xcp_ref[...], 0.0)
        zbuf[8:8 + tt, :] = zc
        w0, w1, w2 = wconv_ref[0:1, :], wconv_ref[1:2, :], wconv_ref[2:3, :]
        y = w0 * zbuf[pl.ds(6, tt), :] + w1 * zbuf[pl.ds(7, tt), :] + w2 * zc
        dy = dob_ * bg
        dybuf[0:tt, :] = dy
        dy1, dy2 = dybuf[pl.ds(1, tt), :], dybuf[pl.ds(2, tt), :]
        dzc = w2 * dy + w1 * dy1 + w0 * dy2
        dp_ref[:, 4 * BRANCH:5 * BRANCH] = (dob_ * y).astype(bf16)
        dp_ref[:, 5 * BRANCH:6 * BRANCH] = (dzc * xc).astype(bf16)
        dp_ref[:, 6 * BRANCH:7 * BRANCH] = (dzc * cg).astype(bf16)
        vec_ref[4:5, :] += jnp.sum(zc * dy2, axis=0, keepdims=True)
        vec_ref[5:6, :] += jnp.sum(zc * dy1, axis=0, keepdims=True)
        vec_ref[6:7, :] += jnp.sum(zc * dy, axis=0, keepdims=True)
        dybuf[tt:tt + 8, :] = dybuf[0:8, :]

        lng_v, lnb_v = lng_ref[...], lnb_ref[...]
        low = _tri(SG_CHUNK)
        wms = [jnp.where(low, wsg_ref[g], 0.0).astype(bf16) for g in range(SG_GROUPS)]
        dlng = jnp.zeros((1, BRANCH), f32)
        dlnb = jnp.zeros((1, BRANCH), f32)
        for cc in range(tt // SG_CHUNK):
            rows = slice(SG_CHUNK * cc, SG_CHUNK * (cc + 1))
            doc = dz_ref[rows, 2 * BRANCH:3 * BRANCH].astype(f32)
            u_raw, v_raw = u_ref[rows, :], v_ref[rows, :]
            ug = _gelu(u_raw)
            vg = _gelu(v_raw)
            vcen = vg - jnp.mean(vg, axis=-1, keepdims=True)
            rstd = lax.rsqrt(jnp.mean(vcen * vcen, axis=-1, keepdims=True) + LN_EPS)
            vhat = vcen * rstd
            vn = (vhat * lng_v + lnb_v).astype(bf16)
            dvn = []
            for g in range(SG_GROUPS):
                sl = slice(LANE * g, LANE * (g + 1))
                sv = _dot(wms[g], vn[:, sl]) + bsg_ref[:, g:g + 1]
                dp_ref[rows, 7 * BRANCH + LANE * g:7 * BRANCH + LANE * (g + 1)] = (doc[:, sl] * sv * _gelu_grad(u_raw[:, sl])).astype(bf16)
                dsv = doc[:, sl] * ug[:, sl]
                dsvb = dsv.astype(bf16)
                dbsg_acc[:, sl] += dsv
                dwsg_ref[g] += jnp.where(low, _dot_nt(dsvb, vn[:, sl]), 0.0)
                dvn.append(_dot_tn(wms[g], dsvb))
            dvn = jnp.concatenate(dvn, axis=1)
            dlng = dlng + jnp.sum(dvn * vhat, axis=0, keepdims=True)
            dlnb = dlnb + jnp.sum(dvn, axis=0, keepdims=True)
            dvh = dvn * lng_v
            dvg = rstd * (dvh - jnp.mean(dvh, axis=-1, keepdims=True) - vhat * jnp.mean(dvh * vhat, axis=-1, keepdims=True))
            dp_ref[rows, 8 * BRANCH:9 * BRANCH] = (dvg * _gelu_grad(v_raw)).astype(bf16)
        vec_ref[2:3, :] += dlng
        vec_ref[3:4, :] += dlnb

        @pl.when(i == nt - 1)
        def _():
            for g in range(SG_GROUPS):
                dbsg_ref[:, g:g + 1] = jnp.sum(dbsg_acc[:, LANE * g:LANE * (g + 1)], axis=1, keepdims=True)

    full = lambda shape: pl.BlockSpec(shape, lambda i: (0,) * len(shape))
    tail = lambda c: pl.BlockSpec((8, BRANCH), lambda i: (jnp.maximum(rev(i) * (tt // 8) - 1, 0), c))
    return pl.pallas_call(
        body, name="mixer_bwd", grid=(nt,),
        in_specs=_p_specs(tt, range(9), rev) + [tail(5), tail(6), pl.BlockSpec((tt, 3 * BRANCH), lambda i: (rev(i), 0)),
                                                pl.BlockSpec((tt, BRANCH), lambda i: (rev(i), 0)),
                                                pl.BlockSpec((nch, HEADS, HEAD_DIM, HEAD_DIM), lambda i: (rev(i), 0, 0, 0)),
                                                full((1, BRANCH)), full((1, BRANCH)), full((3, BRANCH)), full((1, BRANCH)), full((1, BRANCH)),
                                                full((SG_GROUPS, SG_CHUNK, SG_CHUNK)), full((SG_CHUNK, SG_GROUPS))],
        out_specs=[pl.BlockSpec((tt, 9 * BRANCH), lambda i: (rev(i), 0)), full((8, BRANCH)), full((SG_GROUPS, SG_CHUNK, SG_CHUNK)),
                   full((SG_CHUNK, SG_GROUPS))],
        out_shape=[SDS((s, 9 * BRANCH), bf16), SDS((8, BRANCH), f32), SDS((SG_GROUPS, SG_CHUNK, SG_CHUNK), f32), SDS((SG_CHUNK, SG_GROUPS), f32)],
        scratch_shapes=[pltpu.VMEM((HEADS, HEAD_DIM, HEAD_DIM), f32), pltpu.VMEM((tt + 8, BRANCH), f32), pltpu.VMEM((tt + 8, BRANCH), f32),
                        pltpu.VMEM((SG_CHUNK, BRANCH), f32)],
        compiler_params=_cp("arbitrary"),
    )(*([p] * 11), dz, opre, states, lb, gout, wconv, lng, lnb, wsg, bsg_t)


def _dh_bwd(dpm, dpg, w_t, x, dx1, g, after, tm=1024, tk=1536):
    s = x.shape[0]
    km = dpm.shape[1] // tk
    nk = km + dpg.shape[1] // tk

    def body(dpm_ref, dpg_ref, w_ref, x_ref, dx1_ref, g_ref, after_ref, dx_ref, dxb_ref, dg_ref, acc):
        del after_ref
        i, k = pl.program_id(0), pl.program_id(1)

        @pl.when(k == 0)
        def _():
            acc[...] = jnp.zeros_like(acc)

        @pl.when(k < km)
        def _():
            acc[...] += _dot(dpm_ref[...], w_ref[...])

        @pl.when(k >= km)
        def _():
            acc[...] += _dot(dpg_ref[...], w_ref[...])

        @pl.when(k == nk - 1)
        def _():
            r, xh = _rms_stats(x_ref[...])
            dx, dg = _rms_bwd(acc[...], xh, r, g_ref[...])
            dx = dx + dx1_ref[...]
            dx_ref[...] = dx
            dxb_ref[...] = dx.astype(bf16)
            _acc_rows(dg_ref, i == 0, dg)

    row = pl.BlockSpec((tm, D_MODEL), lambda i, k: (i, 0))
    vec = pl.BlockSpec((1, D_MODEL), lambda i, k: (0, 0))
    return pl.pallas_call(
        body, name="dh_bwd", grid=(s // tm, nk),
        in_specs=[pl.BlockSpec((tm, tk), lambda i, k: (i, jnp.minimum(k, km - 1))),
                  pl.BlockSpec((tm, tk), lambda i, k: (i, jnp.maximum(k - km, 0))),
                  pl.BlockSpec((tk, D_MODEL), lambda i, k: (k, 0)), row, row, vec, pl.BlockSpec(memory_space=pl.ANY)],
        out_specs=[row, row, vec], out_shape=[SDS((s, D_MODEL), f32), SDS((s, D_MODEL), bf16), SDS((1, D_MODEL), f32)],
        scratch_shapes=[pltpu.VMEM((tm, D_MODEL), f32)], compiler_params=_cp("arbitrary", "arbitrary"),
    )(dpm, dpg, w_t, x, dx1, g, after)


def _layer_fwd(x, weight, sm):
    p, pg, h = _rms_mm(x, sm["g_mix"], weight("w_in", x))
    z, opre, states = _mixer_fwd(p, sm["lb"], sm["g_out"], sm["w_conv"], sm["ln_g"], sm["ln_b"], sm["w_sg"], sm["b_sg_t"])
    y, merged, x1 = _branch_gate(z, weight("w_branch", z), pg, x, weight("w_o", z))
    x2, h2, ra = _ffn(x1, sm["g_ffn"], weight("w_ff1", x1), weight("w_ff2", x1))
    saved = dict(x=x, p=p, pg=pg, h=h, z=z, opre=opre, states=states, y=y, merged=merged, x1=x1, h2=h2, ra=ra)
    return x2, saved


def _layer_bwd(dx2, dx2b, sv, w, sm, between, before_end):
    nchip = N_DEV // 2
    by_chip = lambda g: g.reshape((nchip, 2) + g.shape[1:])
    da, dx1, dx1b, dg_ffn = _ffn_bwd(dx2, dx2b, sv["x1"], sm["g_ffn"], sv["ra"], w["w_ff1"], w["w_ff2"])
    g_ff2 = by_chip(_mm_tn(sv["ra"], dx2b, 1, D_FF, D_MODEL, 512, 1024, name="dw_ff2", square_a=True)[0]
                    .reshape(N_DEV, D_FF // N_DEV, D_MODEL))
    g_ff1 = by_chip(_mm_tn_slabs(sv["h2"], da, 1, D_MODEL, D_FF // 2, [i * (D_FF // N_DEV) for i in range(nchip)], D_FF // N_DEV,
                                 name="dw_ff1")[:, 0])
    g_o = by_chip(_mm_tn(sv["merged"], dx1b, 1, D_MODEL, D_MODEL, 512, 1024, name="dw_o")[0].reshape(N_DEV, D_MODEL // N_DEV, D_MODEL))
    dy, dpg, dz = _merge_bwd(dx1b, w["w_o"], sv["y"], sv["pg"], w["w_branch"], between(dx1))
    g_branch = by_chip(_mm_tn_slabs(sv["z"], dy, 3, BRANCH, D_MODEL, [i * (D_MODEL // N_DEV) for i in range(N_DEV)], D_MODEL // N_DEV,
                                    name="dw_branch"))
    dpm, vecs, dwsg, dbsg_t = _mixer_bwd(sv["p"], dz, sv["opre"], sv["states"], sm["lb"], sm["g_out"], sm["w_conv"],
                                         sm["ln_g"], sm["ln_b"], sm["w_sg"], sm["b_sg_t"])
    g_in = by_chip(_dw_in(dpm, dpg, sv["h"]).reshape(N_DEV, SHARD_IN, D_MODEL))
    big = dict(w_in=g_in, w_branch=g_branch, w_o=g_o, w_ff1=g_ff1, w_ff2=g_ff2)
    dx, dxb, dg_mix = _dh_bwd(dpm, dpg, w["w_in"], sv["x"], dx1, sm["g_mix"], before_end(big))
    small = dict(g_mix=dg_mix, g_ffn=dg_ffn, vecs=vecs, w_sg=dwsg, b_sg_t=dbsg_t, dx1=dx1)
    return dx, dxb, big, small


BIG = ("w_in", "w_branch", "w_o", "w_ff1", "w_ff2")
ANY = pl.BlockSpec(memory_space=pl.ANY)


def _place():
    return lax.axis_index("x"), lax.axis_index("y"), lax.axis_index("c")


def _al(v, m):
    return pl.multiple_of(v * m, m)


def _shard_of(refs, dev, which=range(len(BIG))):
    out = []
    for ref, t in zip(refs, which):
        by_cols = BIG[t] in ("w_branch", "w_ff1")
        n = ref.shape[-1 if by_cols else 0] // N_DEV
        part = pl.ds(_al(dev, n), n)
        out.append(ref.at[(slice(None),) * (len(ref.shape) - 1) + (part,)] if by_cols else ref.at[part])
    return out


def _gather_out_shapes(shards):
    s_in, s_b, s_o, s_1, s_2 = (shards[n] for n in BIG)
    return [SDS((s_in.shape[1] * N_DEV, s_in.shape[2]), bf16), SDS(s_b.shape[1:3] + (s_b.shape[3] * N_DEV,), bf16),
            SDS((s_o.shape[1] * N_DEV, s_o.shape[2]), bf16), SDS((s_1.shape[1], s_1.shape[2] * N_DEV), bf16),
            SDS((s_2.shape[1] * N_DEV, s_2.shape[2]), bf16)]


def _seq_all_gather_layer(layer, which, n_early, shard_refs, out_shapes, tag=""):
    nt = len(which)
    outs = [jax.empty_ref(sh, memory_space=pltpu.MemorySpace.HBM) for sh in out_shapes]
    early, late = tuple(range(n_early)), tuple(range(n_early, nt))

    @pl.kernel(mesh=plsc.ScalarSubcoreMesh(axis_name="seq", num_cores=1), name=f"seq_all_gather_l{layer}{tag}",
               scratch_types=(pltpu.SemaphoreType.DMA((9,)), pltpu.SemaphoreType.DMA((9,))),
               compiler_params=pltpu.CompilerParams(collective_id=1))
    def launch(send_sems, recv_sems):
        x, y, c = _place()
        me, sibling = (x, y, c), (x, y, 1 - c)
        first, second, diag = _ici_route(x, y, c)
        _handshake([sibling, first, second])
        mine = [r.at[layer] for r in shard_refs]

        def copies(k, blk, to, src=None, part=range(nt)):
            dst = _shard_of(outs, 4 * blk[0] + 2 * blk[1] + blk[2], which)
            src = dst if src is None else src
            return [pltpu.make_async_remote_copy(src_ref=src[t], dst_ref=dst[t], send_sem=send_sems.at[k], recv_sem=recv_sems.at[k],
                                                 device_id=to, device_id_type=MESH) for t in part]

        def start(cps):
            for cp in cps:
                cp.start()
            return cps

        def landed(cps):
            for cp in cps:
                cp.wait_recv()

        sent = start(copies(0, me, sibling, src=mine) + copies(1, me, first, src=mine, part=early)
                     + copies(2, me, first, src=mine, part=late) + copies(3, me, second, src=mine))
        landed(copies(1, first, me, part=early))
        sent += start(copies(4, first, second, part=early) + copies(6, first, sibling, part=early))
        landed(copies(2, first, me, part=late))
        sent += start(copies(5, first, second, part=late) + copies(6, first, sibling, part=late))
        landed(copies(3, second, me))
        sent += start(copies(7, second, sibling))
        landed(copies(4, diag, me, part=early) + copies(5, diag, me, part=late))
        sent += start(copies(8, diag, sibling))
        other = lambda p: (p[0], p[1], 1 - c)
        landed(copies(0, sibling, me) + copies(6, other(second), me) + copies(7, other(first), me) + copies(8, other(diag), me))
        for cp in sent:
            cp.wait_send()

    launch()
    return [o[...] for o in outs]


def _ici_route(x, y, c):
    return (x ^ (1 - c), y ^ c, c), (x ^ c, y ^ (1 - c), c), (1 - x, 1 - y, c)


def _place_own(where, which, shards, gathered, after):
    nt = len(which)

    def body(where_ref, *refs):
        del where_ref
        for src, dst in zip(refs[:nt], refs[2 * nt + 1:]):
            dst[...] = src[...]

    in_specs, out_specs = [], []
    for t, sh in zip(which, shards):
        blk = sh.shape[1:]
        in_specs.append(pl.BlockSpec((None,) + blk, functools.partial(lambda nd, i, wh: (wh[0],) + (0,) * nd, len(blk))))
        by_cols = BIG[t] in ("w_branch", "w_ff1")
        out_specs.append(pl.BlockSpec(blk, functools.partial(
            lambda nd, cols, i, wh: (0,) * (nd - 1) + (wh[1],) if cols else (wh[1],) + (0,) * (nd - 1), len(blk), by_cols)))
    return pl.pallas_call(
        body, name="place_own", out_shape=[SDS(g.shape, g.dtype) for g in gathered],
        input_output_aliases={1 + nt + i: i for i in range(nt)}, compiler_params=_cp("arbitrary"),
        grid_spec=pltpu.PrefetchScalarGridSpec(num_scalar_prefetch=1, grid=(1,), in_specs=in_specs + [ANY] * (nt + 1), out_specs=out_specs),
    )(where, *shards, *gathered, after)


def _handshake(peers):
    barrier = pltpu.get_barrier_semaphore()
    for p in peers:
        pl.semaphore_signal(barrier, inc=1, device_id=p, device_id_type=MESH)
    pl.semaphore_wait(barrier, len(peers))


def _seq_exchange_on_chip(grads):
    nt, nchip = len(BIG), N_DEV // 2
    g_refs = [jax.new_ref(g, memory_space=pltpu.MemorySpace.HBM) for g in grads]
    outs = [jax.empty_ref(SDS((nchip,) + g.shape[2:], bf16), memory_space=pltpu.MemorySpace.HBM) for g in grads]

    @pl.kernel(mesh=plsc.ScalarSubcoreMesh(axis_name="seq", num_cores=1), name="seq_rs_on_chip",
               scratch_types=(pltpu.SemaphoreType.DMA((nchip,)), pltpu.SemaphoreType.DMA((nchip,))),
               compiler_params=pltpu.CompilerParams(collective_id=2))
    def launch(send_sems, recv_sems):
        x, y, c = _place()
        sibling = (x, y, 1 - c)
        _handshake([sibling])
        remote = [pltpu.make_async_remote_copy(src_ref=g_refs[t].at[j, 1 - c], dst_ref=outs[t].at[j], send_sem=send_sems.at[j],
                                               recv_sem=recv_sems.at[j], device_id=sibling, device_id_type=MESH)
                  for j in range(nchip) for t in range(nt)]
        for cp in remote:
            cp.start()
        for cp in remote:
            cp.wait_recv()
        for cp in remote:
            cp.wait_send()

    launch()
    return [o[...] for o in outs], [g[...] for g in g_refs]


def _seq_exchange_between_chips(sums):
    nt = len(BIG)
    s_refs = [jax.new_ref(a, memory_space=pltpu.MemorySpace.HBM) for a in sums]
    outs = [jax.empty_ref(SDS((3,) + a.shape[1:], bf16), memory_space=pltpu.MemorySpace.HBM) for a in sums]
    transit = [jax.empty_ref(SDS(a.shape[1:], bf16), memory_space=pltpu.MemorySpace.HBM) for a in sums]

    early, late = (0,), tuple(range(1, nt))

    @pl.kernel(mesh=plsc.ScalarSubcoreMesh(axis_name="seq", num_cores=1), name="seq_rs_between_chips",
               scratch_types=(pltpu.SemaphoreType.DMA((6,)), pltpu.SemaphoreType.DMA((6,))),
               compiler_params=pltpu.CompilerParams(collective_id=3))
    def launch(send_sems, recv_sems):
        x, y, c = _place()
        first, second, diag = _ici_route(x, y, c)
        _handshake([first, second])

        def copies(k, src, dst, to, part=range(nt)):
            return [pltpu.make_async_remote_copy(src_ref=src(t), dst_ref=dst(t), send_sem=send_sems.at[k], recv_sem=recv_sems.at[k],
                                                 device_id=to, device_id_type=MESH) for t in part]

        chip_of = lambda p: 2 * p[0] + p[1]
        for_diag = lambda t: s_refs[t].at[chip_of(diag)]
        through = lambda t: transit[t]
        last = lambda t: outs[t].at[2]
        direct = (copies(0, lambda t: s_refs[t].at[chip_of(first)], lambda t: outs[t].at[0], first)
                  + copies(1, lambda t: s_refs[t].at[chip_of(second)], lambda t: outs[t].at[1], second))
        via = [copies(2, for_diag, through, first, early), copies(3, for_diag, through, first, late)]
        passed = [copies(4, through, last, second, early), copies(5, through, last, second, late)]
        for cp in via[0] + via[1] + direct:
            cp.start()
        for arrived, onward in zip(via, passed):
            for cp in arrived:
                cp.wait_recv()
            for cp in onward:
                cp.start()
        sent = direct + via[0] + via[1] + passed[0] + passed[1]
        for cp in direct + passed[0] + passed[1]:
            cp.wait_recv()
        for cp in sent:
            cp.wait_send()

    launch()
    return [o[...] for o in outs], [a[...] for a in s_refs]


def _chip_sums(place, mine, other, after, steps=2):
    nt, nchip = len(mine), mine[0].shape[0]
    m4 = [a.reshape(nchip, 2, -1, a.shape[-1]) for a in mine]
    o3 = [a.reshape(nchip, -1, a.shape[-1]) for a in other]

    def body(p_ref, *refs):
        del p_ref
        for a_ref, b_ref, o_ref in zip(refs[:nt], refs[nt:2 * nt], refs[2 * nt + 1:]):
            o_ref[...] = (a_ref[...].astype(f32) + b_ref[...].astype(f32)).astype(bf16)

    tiles = [(a.shape[1] // steps, a.shape[2]) for a in o3]
    blks = [pl.BlockSpec((None,) + t, lambda j, i, p_ref: (p_ref[1 + j], i, 0)) for t in tiles]
    outs = pl.pallas_call(
        body, name="chip_sums", out_shape=[SDS(a.shape, bf16) for a in o3], compiler_params=_cp("parallel", "parallel"),
        grid_spec=pltpu.PrefetchScalarGridSpec(
            num_scalar_prefetch=1, grid=(nchip - 1, steps),
            in_specs=[pl.BlockSpec((None, None) + t, lambda j, i, p_ref: (p_ref[1 + j], p_ref[0], i, 0)) for t in tiles] + blks + [ANY],
            out_specs=blks),
    )(place, *m4, *o3, after)
    return [o.reshape(a.shape) for o, a in zip(outs, other)]


def _all_reduce_rows(pack):
    rows = pack.shape[0]
    blk = rows // N_DEV

    def body(in_ref, out_ref, land, send1, recv1, send2, recv2):
        x, y, c = _place()
        me = 4 * x + 2 * y + c
        others = [(px, py, pc) for px in range(2) for py in range(2) for pc in range(2)]

        def is_me(p):
            return jnp.logical_and(jnp.logical_and(p[0] == x, p[1] == y), p[2] == c)

        land[me] = in_ref[pl.ds(_al(me, blk), blk), :]
        for d, p in enumerate(others):
            @pl.when(jnp.logical_not(is_me(p)))
            def _():
                pltpu.make_async_remote_copy(src_ref=in_ref.at[pl.ds(d * blk, blk), :], dst_ref=land.at[me], send_sem=send1.at[d],
                                             recv_sem=recv1.at[me], device_id=p, device_id_type=MESH).start()
        for d, p in enumerate(others):
            @pl.when(jnp.logical_not(is_me(p)))
            def _():
                cp = pltpu.make_async_remote_copy(src_ref=in_ref.at[pl.ds(d * blk, blk), :], dst_ref=land.at[d], send_sem=send1.at[d],
                                                  recv_sem=recv1.at[d], device_id=p, device_id_type=MESH)
                cp.wait_recv()
                cp.wait_send()
        total = land[0]
        for d in range(1, N_DEV):
            total = total + land[d]
        out_ref[pl.ds(_al(me, blk), blk), :] = total
        for d, p in enumerate(others):
            @pl.when(jnp.logical_not(is_me(p)))
            def _():
                mine = out_ref.at[pl.ds(_al(me, blk), blk), :]
                pltpu.make_async_remote_copy(src_ref=mine, dst_ref=mine, send_sem=send2.at[d], recv_sem=recv2.at[me],
                                             device_id=p, device_id_type=MESH).start()
        for d, p in enumerate(others):
            @pl.when(jnp.logical_not(is_me(p)))
            def _():
                theirs = out_ref.at[pl.ds(d * blk, blk), :]
                cp = pltpu.make_async_remote_copy(src_ref=theirs, dst_ref=theirs, send_sem=send2.at[d], recv_sem=recv2.at[d],
                                                  device_id=p, device_id_type=MESH)
                cp.wait_recv()
                cp.wait_send()

    vm = pl.BlockSpec(memory_space=pltpu.VMEM)
    return pl.pallas_call(
        body, name="all_reduce_rows", in_specs=[vm], out_specs=vm, out_shape=SDS((rows, LANE), f32),
        scratch_shapes=[pltpu.VMEM((N_DEV, blk, LANE), f32)] + [pltpu.SemaphoreType.DMA((N_DEV,))] * 4,
        compiler_params=pltpu.CompilerParams(vmem_limit_bytes=VMEM_LIMIT),
    )(pack)


def _lower_bounds_fwd(lower, after):
    def body(l_ref, *rest):
        o_ref = rest[-1]
        sm = _layer_softmax(l_ref)
        run = jnp.zeros_like(sm[0])
        for l in range(DEPTH):
            o_ref[l:l + 1, :] = run
            if l + 1 < DEPTH:
                run = run + sm[l + 1]

    vm = pl.BlockSpec(memory_space=pltpu.VMEM)
    return pl.pallas_call(body, name="lower_bounds_fwd", in_specs=[vm] + [ANY] * len(after), out_specs=vm,
                          out_shape=SDS(lower.shape, f32))(lower, *after)


def _layer_softmax(l_ref):
    rows = [l_ref[l:l + 1, :] for l in range(DEPTH)]
    top = functools.reduce(jnp.maximum, rows)
    e = [jnp.exp(r - top) for r in rows]
    tot = functools.reduce(lambda a, b: a + b, e)
    return [v / tot for v in e]


def _lower_bounds_bwd(lower, dlbs):
    def body(l_ref, d_ref, o_ref):
        sm = _layer_softmax(l_ref)
        dsm = [None] * DEPTH
        run = jnp.zeros_like(sm[0])
        dsm[0] = run
        for l in reversed(range(1, DEPTH)):
            run = run + d_ref[l:l + 1, :]
            dsm[l] = run
        inner = functools.reduce(lambda a, b: a + b, [sm[l] * dsm[l] for l in range(DEPTH)])
        for l in range(DEPTH):
            o_ref[l:l + 1, :] = sm[l] * (dsm[l] - inner)

    return pl.pallas_call(body, name="lower_bounds_bwd", out_shape=SDS(lower.shape, f32))(lower, dlbs)


_ADAM_C1 = 1.0 - ADAM_B1 ** ADAM_STEP
_ADAM_C2 = 1.0 - ADAM_B2 ** ADAM_STEP


def _adamw(w, g, m, v):
    m = ADAM_B1 * m + (1.0 - ADAM_B1) * g
    v = ADAM_B2 * v + (1.0 - ADAM_B2) * (g * g)
    delta = -ADAM_LR * ((m / _ADAM_C1) / (jnp.sqrt(v / _ADAM_C2) + ADAM_EPS) + ADAM_WD * w)
    return delta, m, v


def _adam_big(where, names, w, m, v, mine, other, landed, outs, after, steps=4):
    nt = len(names)
    three = lambda a: a.reshape(a.shape[0], -1, a.shape[-1])
    w3, m3, v3 = ([three(d[n]) for n in names] for d in (w, m, v))
    outs3 = [three(a) for n in names for a in outs[n]]
    mine4 = [a.reshape(a.shape[0], 2, -1, a.shape[-1]) for a in mine]
    other3 = [three(a) for a in other]
    land3 = [three(a) for a in landed]

    def body(where_ref, *refs):
        del where_ref
        o_refs = refs[6 * nt + 4 * nt + 1:]
        for t in range(nt):
            w_ref, m_ref, v_ref, mine_ref, other_ref, land_ref = (refs[q * nt + t] for q in range(6))
            g = mine_ref[...].astype(f32) + other_ref[...].astype(f32)
            for k in range(3):
                g = g + land_ref[k].astype(f32)
            delta, nm, nv = _adamw(w_ref[...], g, m_ref[...], v_ref[...])
            for o_ref, val in zip(o_refs[4 * t:4 * t + 4], (g, delta, nm, nv)):
                o_ref[...] = val

    tiles = [(a.shape[1] // steps, a.shape[2]) for a in w3]
    own = [pl.BlockSpec((None,) + t, lambda i, wh: (wh[0], i, 0)) for t in tiles]
    res = pl.pallas_call(
        body, name="adam_big", out_shape=[SDS(a.shape, f32) for a in outs3],
        input_output_aliases={1 + 6 * nt + i: i for i in range(4 * nt)}, compiler_params=_cp("parallel"),
        grid_spec=pltpu.PrefetchScalarGridSpec(
            num_scalar_prefetch=1, grid=(steps,),
            in_specs=own * 3 + [pl.BlockSpec((None, None) + t, lambda i, wh: (wh[1], wh[2], i, 0)) for t in tiles]
            + [pl.BlockSpec((None,) + t, lambda i, wh: (wh[1], i, 0)) for t in tiles]
            + [pl.BlockSpec((3,) + t, lambda i, wh: (0, i, 0)) for t in tiles] + [ANY] * (4 * nt + 1),
            out_specs=[s for s in own for _ in range(4)]),
    )(where, *w3, *m3, *v3, *mine4, *other3, *land3, *outs3, after)
    return {n: [o.reshape(w[n].shape) for o in res[4 * t:4 * t + 4]] for t, n in enumerate(names)}


def _touch(a, after):
    a2 = a.reshape(-1, a.shape[-1])

    def body(a_ref, after_ref, o_ref):
        del after_ref
        o_ref[...] = a_ref[0:8, :].astype(f32)

    return pl.pallas_call(
        body, name="touch", grid=(1,), in_specs=[pl.BlockSpec((16, LANE), lambda i: (0, 0)), ANY],
        out_specs=pl.BlockSpec((8, LANE), lambda i: (0, 0)), out_shape=SDS((8, LANE), f32),
    )(a2, after)


def _adam_rows(w, g, m, v):
    def body(w_ref, g_ref, m_ref, v_ref, d_ref, nm_ref, nv_ref):
        delta, nm, nv = _adamw(w_ref[...], g_ref[...], m_ref[...], v_ref[...])
        d_ref[...] = delta
        nm_ref[...] = nm
        nv_ref[...] = nv

    return pl.pallas_call(body, name="adam_rows", out_shape=[SDS(w.shape, f32)] * 3)(w, g, m, v)


SMALL = ("g_mix", "lower_bounds", "g_hgrn_out", "w_conv", "sg_ln_g", "sg_ln_b", "w_sg", "b_sg", "g_ffn", "g_final")
WEIGHTS = ("w_in", "g_mix", "lower_bounds", "g_hgrn_out", "w_conv", "sg_ln_g", "sg_ln_b", "w_sg", "b_sg", "w_branch", "w_o", "g_ffn",
           "w_ff1", "w_ff2", "g_final")


def _pack_rows(arrays, multiple):
    flat = jnp.concatenate([a.reshape(-1) for a in arrays])
    rows = -(-flat.shape[0] // (LANE * multiple)) * multiple
    return jnp.pad(flat, (0, rows * LANE - flat.shape[0])).reshape(rows, LANE)


def _unpack_rows(pack, like):
    flat = pack.reshape(-1)
    out, at = [], 0
    for a in like:
        out.append(flat[at:at + a.size].reshape(a.shape))
        at += a.size
    return out


def kernel(x, w_in, g_mix, lower_bounds, g_hgrn_out, w_conv, sg_ln_g, sg_ln_b, w_sg, b_sg, w_branch, w_o, g_ffn, w_ff1, w_ff2, g_final, loss_target, m_w_in, m_g_mix, m_lower_bounds, m_g_hgrn_out, m_w_conv, m_sg_ln_g, m_sg_ln_b, m_w_sg, m_b_sg, m_w_branch, m_w_o, m_g_ffn, m_w_ff1, m_w_ff2, m_g_final, v_w_in, v_g_mix, v_lower_bounds, v_g_hgrn_out, v_w_conv, v_sg_ln_g, v_sg_ln_b, v_w_sg, v_b_sg, v_w_branch, v_w_o, v_g_ffn, v_w_ff1, v_w_ff2, v_g_final):
    weights = dict(w_in=w_in, g_mix=g_mix, lower_bounds=lower_bounds, g_hgrn_out=g_hgrn_out, w_conv=w_conv, sg_ln_g=sg_ln_g,
                   sg_ln_b=sg_ln_b, w_sg=w_sg, b_sg=b_sg, w_branch=w_branch, w_o=w_o, g_ffn=g_ffn, w_ff1=w_ff1, w_ff2=w_ff2, g_final=g_final)
    mom1 = dict(w_in=m_w_in, g_mix=m_g_mix, lower_bounds=m_lower_bounds, g_hgrn_out=m_g_hgrn_out, w_conv=m_w_conv, sg_ln_g=m_sg_ln_g,
                sg_ln_b=m_sg_ln_b, w_sg=m_w_sg, b_sg=m_b_sg, w_branch=m_w_branch, w_o=m_w_o, g_ffn=m_g_ffn, w_ff1=m_w_ff1, w_ff2=m_w_ff2,
                g_final=m_g_final)
    mom2 = dict(w_in=v_w_in, g_mix=v_g_mix, lower_bounds=v_lower_bounds, g_hgrn_out=v_g_hgrn_out, w_conv=v_w_conv, sg_ln_g=v_sg_ln_g,
                sg_ln_b=v_sg_ln_b, w_sg=v_w_sg, b_sg=v_b_sg, w_branch=v_w_branch, w_o=v_w_o, g_ffn=v_g_ffn, w_ff1=v_w_ff1, w_ff2=v_w_ff2,
                g_final=v_g_final)
    xi, yi, ci = _place()
    dev = 4 * xi + 2 * yi + ci
    conv_cols = w_conv.shape[-1]

    for d in (weights, mom1, mom2):
        d["w_in"] = jnp.swapaxes(d["w_in"], 1, 2)
    shards = {n: weights[n].astype(bf16) for n in BIG}

    conv_place = lax.dynamic_update_slice(jnp.zeros((DEPTH, 3, BRANCH), f32), w_conv, (0, 0, dev * conv_cols))
    (w_conv_full,) = _unpack_rows(_all_reduce_rows(_pack_rows([conv_place], 8 * N_DEV)), [conv_place])
    small_packs = [_pack_rows([d[n] for n in SMALL], 8) for d in (weights, mom1, mom2)]
    lbs = _lower_bounds_fwd(lower_bounds, small_packs)

    def small_of(l):
        return dict(g_mix=g_mix[l][None], lb=lbs[l][None], g_out=g_hgrn_out[l][None], w_conv=w_conv_full[l], ln_g=sg_ln_g[l][None],
                    ln_b=sg_ln_b[l][None], w_sg=w_sg[l], b_sg_t=b_sg[l].T, g_ffn=g_ffn[l][None])

    act = x[0]
    full, saved = [], []
    shard_refs = [jax.new_ref(shards[n], memory_space=pltpu.MemorySpace.HBM) for n in BIG]
    shapes = _gather_out_shapes(shards)
    groups = [g for l in range(DEPTH) for g in ((l, (0,), 1, "a"), (l, (1, 2, 3, 4), 2, "b"))]
    arrived = {}
    for l, which, n_early, tag in groups:
        got = _seq_all_gather_layer(l, which, n_early, [shard_refs[t] for t in which], [shapes[t] for t in which], tag)
        arrived.update({(l, BIG[t]): (which, got) for t in which})

    for l in range(DEPTH):
        full.append({})

        def weight(name, after, l=l):
            if name not in full[l]:
                which, got = arrived[(l, name)]
                where = jnp.stack([jnp.int32(l), dev.astype(jnp.int32)])
                full[l].update(zip([BIG[t] for t in which], _place_own(where, which, [shards[BIG[t]] for t in which], got, after)))
            return full[l][name]

        act, sv = _layer_fwd(act, weight, small_of(l))
        saved.append(sv)
    loss_row, dx, dxb, dg_final = _final(act, loss_target[0], g_final[None])

    chip, core = (2 * xi + yi).astype(jnp.int32), ci.astype(jnp.int32)
    place = jnp.stack([core] + [(chip + k) % (N_DEV // 2) for k in (1, 2, 3)])
    big_out = {n: [lax.empty(weights[n].shape, f32) for _ in range(4)] for n in BIG}
    small_grads = [None] * DEPTH

    def chip_sums(stage, after):
        l, received, mine = stage
        sums = _chip_sums(place, mine, received, after)
        placed.append(sums[BIG.index("w_o")])
        landed, _ = _seq_exchange_between_chips(sums)
        return l, mine, received, landed

    def adam_layer(stage, after):
        l, mine, received, landed = stage
        where = jnp.stack([jnp.int32(l), chip, core])
        big_out.update(_adam_big(where, BIG, weights, mom1, mom2, mine, received, landed, big_out, after))

    above = None
    placed = []
    for l in reversed(range(DEPTH)):
        summed = []

        def between(dx1):
            if above is None:
                return dx1
            summed.append(chip_sums(above, dx1))
            return placed[-1]

        def before_end(big):
            return _touch(summed[0][3][BIG.index("w_o")], big["w_in"]) if summed else big["w_in"]

        dx, dxb, big, small_grads[l] = _layer_bwd(dx, dxb, saved[l], full[l], small_of(l), between, before_end)
        if summed:
            adam_layer(summed[0], dx)
        above = (l, *_seq_exchange_on_chip([big[n] for n in BIG]))

    stack = lambda f: jnp.stack([f(small_grads[l]) for l in range(DEPTH)])
    d_lower = _lower_bounds_bwd(lower_bounds, stack(lambda s: s["vecs"][0]))
    local_small = dict(g_mix=stack(lambda s: s["g_mix"][0]), lower_bounds=d_lower, g_hgrn_out=stack(lambda s: s["vecs"][1]),
                       w_conv=stack(lambda s: s["vecs"][4:7]), sg_ln_g=stack(lambda s: s["vecs"][2]), sg_ln_b=stack(lambda s: s["vecs"][3]),
                       w_sg=stack(lambda s: s["w_sg"]), b_sg=stack(lambda s: s["b_sg_t"].T), g_ffn=stack(lambda s: s["g_ffn"][0]),
                       g_final=dg_final[0])
    order = [local_small[n] for n in SMALL] + [loss_row]
    *reduced, loss_sum = _unpack_rows(_all_reduce_rows(_pack_rows(order, 8 * N_DEV)), order)
    loss = loss_sum[0, 0]
    grads = dict(zip(SMALL, reduced))
    grads["w_conv"] = lax.dynamic_slice(grads["w_conv"], (0, 0, dev * conv_cols), (DEPTH, 3, conv_cols))

    deltas, new_m, new_v = {}, {}, {}
    like = [weights[n] for n in SMALL]
    small_out = _adam_rows(small_packs[0], _pack_rows([grads[n] for n in SMALL], 8), small_packs[1], small_packs[2])
    for out, pack in zip((deltas, new_m, new_v), small_out):
        out.update(zip(SMALL, _unpack_rows(pack, like)))
    adam_layer(chip_sums(above, dx), small_out[0])
    for n in BIG:
        grads[n], deltas[n], new_m[n], new_v[n] = (jnp.swapaxes(a, 1, 2) if n == "w_in" else a for a in big_out[n])

    return (loss, dx[None], *[grads[n] for n in WEIGHTS], *[deltas[n] for n in WEIGHTS], *[new_m[n] for n in WEIGHTS],
            *[new_v[n] for n in WEIGHTS])
```
